```python
import jax, jax.numpy as jnp
from jax import lax
import numpy as np

D_MODEL = 1024
BATCH = 8
SEQ = 8192
DEPTH = 2

GROUP_WIDTH = D_MODEL // 2
D_MIX = 3 * GROUP_WIDTH
BLOCK = 128
RMS_EPS = 1e-6
NEG_INF = -1e30

MLA_HEADS = 8
MLA_NOPE_DIM = 64
MLA_ROPE_DIM = 32
MLA_QK_DIM = MLA_NOPE_DIM + MLA_ROPE_DIM
MLA_V_DIM = 64
MLA_Q_LORA = 256
MLA_KV_LORA = 128
ROPE_THETA = 10000.0

CONV_DIM = GROUP_WIDTH
CONV_WIDTH = 3

SWA_HEADS = 8
SWA_KV_HEADS = 2
SWA_GROUP = SWA_HEADS // SWA_KV_HEADS
SWA_HEAD_DIM = 64
SWA_WINDOW = 128

IN_SPLITS = (
    MLA_Q_LORA, MLA_KV_LORA, MLA_ROPE_DIM, GROUP_WIDTH,
    CONV_DIM, CONV_DIM, CONV_DIM, GROUP_WIDTH,
    SWA_HEADS * SWA_HEAD_DIM, SWA_KV_HEADS * SWA_HEAD_DIM,
    SWA_KV_HEADS * SWA_HEAD_DIM, GROUP_WIDTH,
)
IN_COLS = sum(IN_SPLITS)

kernel_name = "hymba_mla_shortconv_swa_hybrid"


def rms_norm(x, g):
    xf = x.astype(jnp.float32)
    y = xf * lax.rsqrt(jnp.mean(xf * xf, axis=-1, keepdims=True) + RMS_EPS)
    return (y * g.astype(jnp.float32)).astype(x.dtype)


def apply_rope(x, pos):
    half = x.shape[-1] // 2
    inv_freq = jnp.power(jnp.float32(ROPE_THETA), -jnp.arange(half, dtype=jnp.float32) / half)
    ang = pos[:, None] * inv_freq[None, :]
    cos = jnp.cos(ang)[:, None, :]
    sin = jnp.sin(ang)[:, None, :]
    xf = x.astype(jnp.float32)
    x1, x2 = xf[..., :half], xf[..., half:]
    out = jnp.concatenate([x1 * cos - x2 * sin, x2 * cos + x1 * sin], axis=-1)
    return out.astype(x.dtype)


def mla_mixer(q_lat, kv_lat, k_rope, q_a_norm, w_qb, kv_a_norm, w_kvb, q_norm, k_norm):
    b, s, _ = q_lat.shape
    q = (rms_norm(q_lat, q_a_norm) @ w_qb).reshape(b, s, MLA_HEADS, MLA_QK_DIM)
    kv = (rms_norm(kv_lat, kv_a_norm) @ w_kvb).reshape(b, s, MLA_HEADS, MLA_NOPE_DIM + MLA_V_DIM)
    k_nope, v = kv[..., :MLA_NOPE_DIM], kv[..., MLA_NOPE_DIM:]
    k_pe = jnp.broadcast_to(k_rope[:, :, None, :], (b, s, MLA_HEADS, MLA_ROPE_DIM))
    k = jnp.concatenate([k_nope, k_pe], axis=-1)
    q = rms_norm(q, q_norm)
    k = rms_norm(k, k_norm)
    pos = jnp.arange(s, dtype=jnp.float32)
    q = jnp.concatenate([q[..., :MLA_NOPE_DIM], apply_rope(q[..., MLA_NOPE_DIM:], pos)], axis=-1)
    k = jnp.concatenate([k[..., :MLA_NOPE_DIM], apply_rope(k[..., MLA_NOPE_DIM:], pos)], axis=-1)
    scale = MLA_QK_DIM ** -0.5
    nb = s // BLOCK
    q_blocks = q.reshape(b, nb, BLOCK, MLA_HEADS, MLA_QK_DIM).transpose(1, 0, 2, 3, 4)
    starts = jnp.arange(nb, dtype=jnp.int32) * BLOCK
    key_pos = jnp.arange(s, dtype=jnp.int32)

    def attend_block(args):
        qb, start = args
        sc = jnp.einsum('bqhd,bkhd->bhqk', qb, k, preferred_element_type=jnp.float32) * scale
        q_pos = start + jnp.arange(BLOCK, dtype=jnp.int32)
        causal = key_pos[None, :] <= q_pos[:, None]
        sc = jnp.where(causal[None, None], sc, NEG_INF)
        p = jax.nn.softmax(sc, axis=-1).astype(v.dtype)
        return jnp.einsum('bhqk,bkhd->bqhd', p, v)

    o = lax.map(attend_block, (q_blocks, starts))
    return o.transpose(1, 0, 2, 3, 4).reshape(b, s, MLA_HEADS * MLA_V_DIM)


def short_conv_mixer(h, b_gate, c_gate, conv_w):
    u = c_gate * h
    y = lax.conv_general_dilated(
        u, conv_w[:, None, :].astype(u.dtype), window_strides=(1,),
        padding=[(CONV_WIDTH - 1, 0)], dimension_numbers=('NWC', 'WIO', 'NWC'),
        feature_group_count=CONV_DIM)
    return b_gate * y


def _band(t):
    b, s, kv, d = t.shape
    nb = s // BLOCK
    tp = jnp.concatenate([jnp.zeros((b, BLOCK, kv, d), t.dtype), t], axis=1)
    tp = tp.reshape(b, nb + 1, BLOCK, kv, d)
    return jnp.concatenate([tp[:, :-1], tp[:, 1:]], axis=2)


def swa_mixer(q, k, v, q_norm, k_norm, sinks):
    b, s, _ = q.shape
    nb = s // BLOCK
    q = rms_norm(q.reshape(b, s, SWA_HEADS, SWA_HEAD_DIM), q_norm)
    k = rms_norm(k.reshape(b, s, SWA_KV_HEADS, SWA_HEAD_DIM), k_norm)
    v = v.reshape(b, s, SWA_KV_HEADS, SWA_HEAD_DIM)
    qb = q.reshape(b, nb, BLOCK, SWA_KV_HEADS, SWA_GROUP, SWA_HEAD_DIM)
    kb, vb = _band(k), _band(v)
    sc = jnp.einsum('bnqkgd,bnskd->bnkgqs', qb, kb,
                    preferred_element_type=jnp.float32) * (SWA_HEAD_DIM ** -0.5)
    q_idx = jnp.arange(BLOCK, dtype=jnp.int32)[:, None]
    k_idx = jnp.arange(2 * BLOCK, dtype=jnp.int32)[None, :]
    dist = BLOCK + q_idx - k_idx
    key_pos = (jnp.arange(nb, dtype=jnp.int32)[:, None] - 1) * BLOCK + k_idx
    valid = ((dist >= 0) & (dist < SWA_WINDOW))[None] & (key_pos >= 0)[:, None, :]
    slopes = jnp.exp2(-8.0 * jnp.arange(1, SWA_HEADS + 1, dtype=jnp.float32) / SWA_HEADS)
    slopes = slopes.reshape(SWA_KV_HEADS, SWA_GROUP)
    sc = sc - slopes[None, None, :, :, None, None] * dist.astype(jnp.float32)[None, None, None, None]
    sc = jnp.where(valid[None, :, None, None], sc, NEG_INF)
    sink = jnp.broadcast_to(
        sinks.astype(jnp.float32).reshape(SWA_KV_HEADS, SWA_GROUP)[None, None, :, :, None, None],
        sc.shape[:-1] + (1,))
    p = jax.nn.softmax(jnp.concatenate([sc, sink], axis=-1), axis=-1)[..., :-1].astype(v.dtype)
    o = jnp.einsum('bnkgqs,bnskd->bnqkgd', p, vb)
    return o.reshape(b, s, SWA_HEADS * SWA_HEAD_DIM)


def hybrid_layer(x, norm_g, w_in, mla_q_a_norm, mla_w_qb, mla_kv_a_norm, mla_w_kvb,
                 mla_q_norm, mla_k_norm, conv_w, swa_q_norm, swa_k_norm, swa_sinks, w_out):
    h = rms_norm(x, norm_g)
    proj = h @ w_in
    offsets = [int(o) for o in np.cumsum(IN_SPLITS)[:-1]]
    (q_lat, kv_lat, k_rope, g_mla,
     c_h, c_b, c_c, g_conv,
     s_q, s_k, s_v, g_swa) = jnp.split(proj, offsets, axis=-1)
    y_mla = mla_mixer(q_lat, kv_lat, k_rope, mla_q_a_norm, mla_w_qb, mla_kv_a_norm,
                      mla_w_kvb, mla_q_norm, mla_k_norm) * jax.nn.silu(g_mla)
    y_conv = short_conv_mixer(c_h, c_b, c_c, conv_w) * jax.nn.silu(g_conv)
    y_swa = swa_mixer(s_q, s_k, s_v, swa_q_norm, swa_k_norm, swa_sinks) * jax.nn.silu(g_swa)
    y = jnp.concatenate([y_mla, y_conv, y_swa], axis=-1) @ w_out
    return x + y


def _fwd_setup_inputs(seed: int = 0) -> dict:
    key = jax.random.key(seed)
    ks = jax.random.split(key, 14)
    f32 = jnp.float32

    def nrm(k, shape, scale):
        return jax.random.normal(k, shape, f32) * scale

    def gain(k, n):
        return 1.0 + 0.02 * jax.random.normal(k, (DEPTH, n), f32)

    return {
        "x": jax.random.normal(ks[0], (BATCH, SEQ, D_MODEL), f32),
        "norm_g": gain(ks[1], D_MODEL),
        "w_in": nrm(ks[2], (DEPTH, D_MODEL, IN_COLS), D_MODEL ** -0.5),
        "mla_q_a_norm": gain(ks[3], MLA_Q_LORA),
        "mla_w_qb": nrm(ks[4], (DEPTH, MLA_Q_LORA, MLA_HEADS * MLA_QK_DIM), MLA_Q_LORA ** -0.5),
        "mla_kv_a_norm": gain(ks[5], MLA_KV_LORA),
        "mla_w_kvb": nrm(ks[6], (DEPTH, MLA_KV_LORA, MLA_HEADS * (MLA_NOPE_DIM + MLA_V_DIM)), MLA_KV_LORA ** -0.5),
        "mla_q_norm": gain(ks[7], MLA_QK_DIM),
        "mla_k_norm": gain(ks[8], MLA_QK_DIM),
        "conv_w": nrm(ks[9], (DEPTH, CONV_WIDTH, CONV_DIM), CONV_WIDTH ** -0.5),
        "swa_q_norm": gain(ks[10], SWA_HEAD_DIM),
        "swa_k_norm": gain(ks[11], SWA_HEAD_DIM),
        "swa_sinks": nrm(ks[12], (DEPTH, SWA_HEADS), 0.5),
        "w_out": nrm(ks[13], (DEPTH, D_MIX, D_MODEL), D_MIX ** -0.5),
    }


def _fwd_reference(x, norm_g, w_in, mla_q_a_norm, mla_w_qb, mla_kv_a_norm, mla_w_kvb,
              mla_q_norm, mla_k_norm, conv_w, swa_q_norm, swa_k_norm, swa_sinks, w_out):
    for l in range(DEPTH):
        x = hybrid_layer(x, norm_g[l], w_in[l], mla_q_a_norm[l], mla_w_qb[l], mla_kv_a_norm[l],
                         mla_w_kvb[l], mla_q_norm[l], mla_k_norm[l], conv_w[l], swa_q_norm[l],
                         swa_k_norm[l], swa_sinks[l], w_out[l])
    return x


import jax as _jax
import jax.numpy as _jnp

TWIN_FORMAT = 'train_step'
FWD_PARAMS = ['x', 'norm_g', 'w_in', 'mla_q_a_norm', 'mla_w_qb', 'mla_kv_a_norm', 'mla_w_kvb', 'mla_q_norm', 'mla_k_norm', 'conv_w', 'swa_q_norm', 'swa_k_norm', 'swa_sinks', 'w_out']
TWIN_WEIGHTS = ['norm_g', 'w_in', 'mla_q_a_norm', 'mla_w_qb', 'mla_kv_a_norm', 'mla_w_kvb', 'mla_q_norm', 'mla_k_norm', 'conv_w', 'swa_q_norm', 'swa_k_norm', 'swa_sinks', 'w_out']
TWIN_DIFF_INPUT = 'x'
TWIN_INPUTS = ['x', 'norm_g', 'w_in', 'mla_q_a_norm', 'mla_w_qb', 'mla_kv_a_norm', 'mla_w_kvb', 'mla_q_norm', 'mla_k_norm', 'conv_w', 'swa_q_norm', 'swa_k_norm', 'swa_sinks', 'w_out', 'loss_target', 'm_norm_g', 'm_w_in', 'm_mla_q_a_norm', 'm_mla_w_qb', 'm_mla_kv_a_norm', 'm_mla_w_kvb', 'm_mla_q_norm', 'm_mla_k_norm', 'm_conv_w', 'm_swa_q_norm', 'm_swa_k_norm', 'm_swa_sinks', 'm_w_out', 'v_norm_g', 'v_w_in', 'v_mla_q_a_norm', 'v_mla_w_qb', 'v_mla_kv_a_norm', 'v_mla_w_kvb', 'v_mla_q_norm', 'v_mla_k_norm', 'v_conv_w', 'v_swa_q_norm', 'v_swa_k_norm', 'v_swa_sinks', 'v_w_out']
TWIN_OUTPUTS = ['loss', 'grad_x', 'grad_norm_g', 'grad_w_in', 'grad_mla_q_a_norm', 'grad_mla_w_qb', 'grad_mla_kv_a_norm', 'grad_mla_w_kvb', 'grad_mla_q_norm', 'grad_mla_k_norm', 'grad_conv_w', 'grad_swa_q_norm', 'grad_swa_k_norm', 'grad_swa_sinks', 'grad_w_out', 'delta_norm_g', 'delta_w_in', 'delta_mla_q_a_norm', 'delta_mla_w_qb', 'delta_mla_kv_a_norm', 'delta_mla_w_kvb', 'delta_mla_q_norm', 'delta_mla_k_norm', 'delta_conv_w', 'delta_swa_q_norm', 'delta_swa_k_norm', 'delta_swa_sinks', 'delta_w_out', 'new_m_norm_g', 'new_m_w_in', 'new_m_mla_q_a_norm', 'new_m_mla_w_qb', 'new_m_mla_kv_a_norm', 'new_m_mla_w_kvb', 'new_m_mla_q_norm', 'new_m_mla_k_norm', 'new_m_conv_w', 'new_m_swa_q_norm', 'new_m_swa_k_norm', 'new_m_swa_sinks', 'new_m_w_out', 'new_v_norm_g', 'new_v_w_in', 'new_v_mla_q_a_norm', 'new_v_mla_w_qb', 'new_v_mla_kv_a_norm', 'new_v_mla_w_kvb', 'new_v_mla_q_norm', 'new_v_mla_k_norm', 'new_v_conv_w', 'new_v_swa_q_norm', 'new_v_swa_k_norm', 'new_v_swa_sinks', 'new_v_w_out']
TWIN_LEAF_KINDS = {'loss': 'loss', 'grad_x': 'grad_x', 'grad_norm_g': 'grad_w', 'grad_w_in': 'grad_w', 'grad_mla_q_a_norm': 'grad_w', 'grad_mla_w_qb': 'grad_w', 'grad_mla_kv_a_norm': 'grad_w', 'grad_mla_w_kvb': 'grad_w', 'grad_mla_q_norm': 'grad_w', 'grad_mla_k_norm': 'grad_w', 'grad_conv_w': 'grad_w', 'grad_swa_q_norm': 'grad_w', 'grad_swa_k_norm': 'grad_w', 'grad_swa_sinks': 'grad_w', 'grad_w_out': 'grad_w', 'delta_norm_g': 'delta_w', 'delta_w_in': 'delta_w', 'delta_mla_q_a_norm': 'delta_w', 'delta_mla_w_qb': 'delta_w', 'delta_mla_kv_a_norm': 'delta_w', 'delta_mla_w_kvb': 'delta_w', 'delta_mla_q_norm': 'delta_w', 'delta_mla_k_norm': 'delta_w', 'delta_conv_w': 'delta_w', 'delta_swa_q_norm': 'delta_w', 'delta_swa_k_norm': 'delta_w', 'delta_swa_sinks': 'delta_w', 'delta_w_out': 'delta_w', 'new_m_norm_g': 'new_m', 'new_m_w_in': 'new_m', 'new_m_mla_q_a_norm': 'new_m', 'new_m_mla_w_qb': 'new_m', 'new_m_mla_kv_a_norm': 'new_m', 'new_m_mla_w_kvb': 'new_m', 'new_m_mla_q_norm': 'new_m', 'new_m_mla_k_norm': 'new_m', 'new_m_conv_w': 'new_m', 'new_m_swa_q_norm': 'new_m', 'new_m_swa_k_norm': 'new_m', 'new_m_swa_sinks': 'new_m', 'new_m_w_out': 'new_m', 'new_v_norm_g': 'new_v', 'new_v_w_in': 'new_v', 'new_v_mla_q_a_norm': 'new_v', 'new_v_mla_w_qb': 'new_v', 'new_v_mla_kv_a_norm': 'new_v', 'new_v_mla_w_kvb': 'new_v', 'new_v_mla_q_norm': 'new_v', 'new_v_mla_k_norm': 'new_v', 'new_v_conv_w': 'new_v', 'new_v_swa_q_norm': 'new_v', 'new_v_swa_k_norm': 'new_v', 'new_v_swa_sinks': 'new_v', 'new_v_w_out': 'new_v'}


def _forward(args):
    return _fwd_reference(*[args[k] for k in FWD_PARAMS])


def _output_shape():
    def fwd():
        inp = _fwd_setup_inputs(0)
        return _fwd_reference(*[inp[k] for k in FWD_PARAMS])
    out = _jax.eval_shape(fwd)
    return out.shape, out.dtype

N_MICROBATCH = 1
ADAM_LR = 0.001
ADAM_B1 = 0.9
ADAM_B2 = 0.999
ADAM_EPS = 1e-08
ADAM_WD = 0.01
ADAM_STEP = 10
PER_EXAMPLE_BATCH_AXIS = {'x': 0, 'loss_target': 0}
SHARED_INPUTS = []
_WEIGHT_DTYPES = {'norm_g': _jnp.float32, 'w_in': _jnp.float32, 'mla_q_a_norm': _jnp.float32, 'mla_w_qb': _jnp.float32, 'mla_kv_a_norm': _jnp.float32, 'mla_w_kvb': _jnp.float32, 'mla_q_norm': _jnp.float32, 'mla_k_norm': _jnp.float32, 'conv_w': _jnp.float32, 'swa_q_norm': _jnp.float32, 'swa_k_norm': _jnp.float32, 'swa_sinks': _jnp.float32, 'w_out': _jnp.float32}
MOMENT_SCALE = {'norm_g': 3.291545e+01, 'w_in': 5.284669e-01, 'mla_q_a_norm': 8.366563e-02, 'mla_w_qb': 4.839008e-02, 'mla_kv_a_norm': 5.940195e-01, 'mla_w_kvb': 6.379751e-02, 'mla_q_norm': 4.451997e-01, 'mla_k_norm': 4.467219e-01, 'conv_w': 8.428015e+00, 'swa_q_norm': 3.984643e+00, 'swa_k_norm': 3.970651e+00, 'swa_sinks': 1.268663e+01, 'w_out': 3.624079e-01}


def _to_microbatches(a, axis):
    t = _jnp.moveaxis(a, axis, 0)
    t = t.reshape((N_MICROBATCH, t.shape[0] // N_MICROBATCH) + t.shape[1:])
    return _jnp.moveaxis(t, 1, axis + 1)


def setup_inputs(seed: int = 0) -> dict:
    inp = _fwd_setup_inputs(seed)
    key = _jax.random.fold_in(_jax.random.key(seed), 7919)
    shape, _ = _output_shape()
    out = dict(inp)
    out["loss_target"] = _jax.random.normal(_jax.random.fold_in(key, 0), shape, _jnp.float32)
    for i, name in enumerate(TWIN_WEIGHTS):
        w = inp[name].astype(_jnp.float32)
        if MOMENT_SCALE is None:
            s = _jnp.sqrt(_jnp.mean(_jnp.square(w)) + 1e-30)
        else:
            s = MOMENT_SCALE[name]
        km, kv = _jax.random.split(_jax.random.fold_in(key, i + 1))
        out[name] = w
        out["m_" + name] = s * _jax.random.normal(km, w.shape, _jnp.float32)
        out["v_" + name] = (s * s) * _jax.random.uniform(kv, w.shape, _jnp.float32, 0.5, 1.5)
    if N_MICROBATCH > 1:
        for name, axis in PER_EXAMPLE_BATCH_AXIS.items():
            out[name] = _to_microbatches(out[name], axis)
    return {'x': out['x'], 'norm_g': out['norm_g'], 'w_in': out['w_in'], 'mla_q_a_norm': out['mla_q_a_norm'], 'mla_w_qb': out['mla_w_qb'], 'mla_kv_a_norm': out['mla_kv_a_norm'], 'mla_w_kvb': out['mla_w_kvb'], 'mla_q_norm': out['mla_q_norm'], 'mla_k_norm': out['mla_k_norm'], 'conv_w': out['conv_w'], 'swa_q_norm': out['swa_q_norm'], 'swa_k_norm': out['swa_k_norm'], 'swa_sinks': out['swa_sinks'], 'w_out': out['w_out'], 'loss_target': out['loss_target'], 'm_norm_g': out['m_norm_g'], 'm_w_in': out['m_w_in'], 'm_mla_q_a_norm': out['m_mla_q_a_norm'], 'm_mla_w_qb': out['m_mla_w_qb'], 'm_mla_kv_a_norm': out['m_mla_kv_a_norm'], 'm_mla_w_kvb': out['m_mla_w_kvb'], 'm_mla_q_norm': out['m_mla_q_norm'], 'm_mla_k_norm': out['m_mla_k_norm'], 'm_conv_w': out['m_conv_w'], 'm_swa_q_norm': out['m_swa_q_norm'], 'm_swa_k_norm': out['m_swa_k_norm'], 'm_swa_sinks': out['m_swa_sinks'], 'm_w_out': out['m_w_out'], 'v_norm_g': out['v_norm_g'], 'v_w_in': out['v_w_in'], 'v_mla_q_a_norm': out['v_mla_q_a_norm'], 'v_mla_w_qb': out['v_mla_w_qb'], 'v_mla_kv_a_norm': out['v_mla_kv_a_norm'], 'v_mla_w_kvb': out['v_mla_w_kvb'], 'v_mla_q_norm': out['v_mla_q_norm'], 'v_mla_k_norm': out['v_mla_k_norm'], 'v_conv_w': out['v_conv_w'], 'v_swa_q_norm': out['v_swa_q_norm'], 'v_swa_k_norm': out['v_swa_k_norm'], 'v_swa_sinks': out['v_swa_sinks'], 'v_w_out': out['v_w_out']}


def _loss(weights, diff, rest, loss_target):
    with _jax.named_scope("forward"):
        args = {**rest, TWIN_DIFF_INPUT: diff, **{k: w.astype(_WEIGHT_DTYPES[k]) for k, w in weights.items()}}
        y = _forward(args)
    with _jax.named_scope("loss_head"):
        err = _jnp.square(y.astype(_jnp.float32) - loss_target)
        return 0.5 * _jnp.sum(_jnp.mean(err, axis=-1)) if err.ndim else 0.5 * err


def _adamw(w, g, m, v):
    m = ADAM_B1 * m + (1.0 - ADAM_B1) * g
    v = ADAM_B2 * v + (1.0 - ADAM_B2) * _jnp.square(g)
    m_hat = m / (1.0 - ADAM_B1 ** ADAM_STEP)
    v_hat = v / (1.0 - ADAM_B2 ** ADAM_STEP)
    delta = -ADAM_LR * (m_hat / (_jnp.sqrt(v_hat) + ADAM_EPS) + ADAM_WD * w)
    return delta, m, v


def reference(x, norm_g, w_in, mla_q_a_norm, mla_w_qb, mla_kv_a_norm, mla_w_kvb, mla_q_norm, mla_k_norm, conv_w, swa_q_norm, swa_k_norm, swa_sinks, w_out, loss_target, m_norm_g, m_w_in, m_mla_q_a_norm, m_mla_w_qb, m_mla_kv_a_norm, m_mla_w_kvb, m_mla_q_norm, m_mla_k_norm, m_conv_w, m_swa_q_norm, m_swa_k_norm, m_swa_sinks, m_w_out, v_norm_g, v_w_in, v_mla_q_a_norm, v_mla_w_qb, v_mla_kv_a_norm, v_mla_w_kvb, v_mla_q_norm, v_mla_k_norm, v_conv_w, v_swa_q_norm, v_swa_k_norm, v_swa_sinks, v_w_out):
    given = dict(x=x, norm_g=norm_g, w_in=w_in, mla_q_a_norm=mla_q_a_norm, mla_w_qb=mla_w_qb, mla_kv_a_norm=mla_kv_a_norm, mla_w_kvb=mla_w_kvb, mla_q_norm=mla_q_norm, mla_k_norm=mla_k_norm, conv_w=conv_w, swa_q_norm=swa_q_norm, swa_k_norm=swa_k_norm, swa_sinks=swa_sinks, w_out=w_out, loss_target=loss_target, m_norm_g=m_norm_g, m_w_in=m_w_in, m_mla_q_a_norm=m_mla_q_a_norm, m_mla_w_qb=m_mla_w_qb, m_mla_kv_a_norm=m_mla_kv_a_norm, m_mla_w_kvb=m_mla_w_kvb, m_mla_q_norm=m_mla_q_norm, m_mla_k_norm=m_mla_k_norm, m_conv_w=m_conv_w, m_swa_q_norm=m_swa_q_norm, m_swa_k_norm=m_swa_k_norm, m_swa_sinks=m_swa_sinks, m_w_out=m_w_out, v_norm_g=v_norm_g, v_w_in=v_w_in, v_mla_q_a_norm=v_mla_q_a_norm, v_mla_w_qb=v_mla_w_qb, v_mla_kv_a_norm=v_mla_kv_a_norm, v_mla_w_kvb=v_mla_w_kvb, v_mla_q_norm=v_mla_q_norm, v_mla_k_norm=v_mla_k_norm, v_conv_w=v_conv_w, v_swa_q_norm=v_swa_q_norm, v_swa_k_norm=v_swa_k_norm, v_swa_sinks=v_swa_sinks, v_w_out=v_w_out)
    weights = {n: given[n] for n in TWIN_WEIGHTS}
    shared = {n: given[n] for n in SHARED_INPUTS}
    per_example = {n: given[n] for n in ['x']}
    grad_fn = _jax.value_and_grad(_loss, argnums=(0, 1))

    def one_microbatch(ex, loss_target):
        ex = dict(ex)
        diff = ex.pop(TWIN_DIFF_INPUT)
        return grad_fn(weights, diff, {**shared, **ex}, loss_target)

    if N_MICROBATCH == 1:
        loss, (grad_w, grad_x) = one_microbatch(per_example, given["loss_target"])
    else:
        def body(carry, xs):
            loss_sum, grad_sum = carry
            l_k, (gw_k, gx_k) = one_microbatch(xs[0], xs[1])
            with _jax.named_scope("update"):
                return (loss_sum + l_k, _jax.tree.map(_jnp.add, grad_sum, gw_k)), gx_k

        init = (_jnp.zeros((), _jnp.float32), _jax.tree.map(_jnp.zeros_like, weights))
        (loss, grad_w), grad_x = _jax.lax.scan(body, init, (per_example, given["loss_target"]))
    with _jax.named_scope("update"):
        delta_w, new_m, new_v = {}, {}, {}
        for n in TWIN_WEIGHTS:
            delta_w[n], new_m[n], new_v[n] = _adamw(weights[n], grad_w[n], given["m_" + n], given["v_" + n])
    return (loss, grad_x, *[grad_w[n] for n in TWIN_WEIGHTS], *[delta_w[n] for n in TWIN_WEIGHTS],
            *[new_m[n] for n in TWIN_WEIGHTS], *[new_v[n] for n in TWIN_WEIGHTS])
```

```python
import functools

import jax
import jax.numpy as jnp
import numpy as np
from jax import lax
from jax.experimental import pallas as pl
from jax.experimental.pallas import tpu as pltpu

F32 = jnp.float32
MXU_DTYPE = jnp.bfloat16
WIRE_DTYPE = jnp.bfloat16

N_DEV = 8
DEPTH = 2
D_MODEL = 1024
GROUP_WIDTH = 512
D_MIX = 3 * GROUP_WIDTH
BLOCK = 128
RMS_EPS = 1e-6
NEG_INF = -1e30
HEADS = 8
MLA_QK = 96
MLA_NOPE = 64
MLA_ROPE = 32
MLA_Q_LORA = 256
MLA_KV_LORA = 128
ROPE_THETA = 10000.0
SWA_HEAD_DIM = 64
LANES = 128
IN_COLS = 4256

ADAM_LR = 0.001
ADAM_B1 = 0.9
ADAM_B2 = 0.999
ADAM_EPS = 1e-08
ADAM_WD = 0.01
ADAM_STEP = 10

NP = 4352
CB_QLAT = 0
CB_KVLAT = 2
CB_KROPE = 3
CB_GMLA, CB_CH, CB_CB, CB_CC, CB_GCONV, CB_SQ, CB_GSWA = 1, 2, 3, 4, 5, 6, 7
CB_SK, CB_SV = 32, 33

TM_PROJ = 256
TM_ROW = 256
TQ = 256
TM_SWA = 256
VMEM_MB = 2 ** 20


def _cp(sem, vmem_mb):
    return pltpu.CompilerParams(dimension_semantics=sem, vmem_limit_bytes=vmem_mb * VMEM_MB)


def _sds(shape, dtype):
    return jax.ShapeDtypeStruct(shape, dtype)


def _dot(a, b):
    return jnp.dot(a, b, preferred_element_type=F32)


def _dot_nt(a, b):
    return lax.dot_general(a, b, (((1,), (1,)), ((), ())), preferred_element_type=F32)


def _dot_tn(a, b):
    return lax.dot_general(a, b, (((0,), (0,)), ((), ())), preferred_element_type=F32)


def _rms(x, n):
    r = lax.rsqrt(jnp.sum(x * x, axis=-1, keepdims=True) * (1.0 / n) + RMS_EPS)
    return x * r, r


def _rms_bwd(dy, xhat, r, w, n):
    g = dy * w
    return r * (g - xhat * (jnp.sum(g * xhat, axis=-1, keepdims=True) * (1.0 / n)))


def _rms_halves(x, half1):
    x2 = x * x
    s0 = jnp.sum(jnp.where(half1, 0.0, x2), axis=-1, keepdims=True)
    s1 = jnp.sum(jnp.where(half1, x2, 0.0), axis=-1, keepdims=True)
    r = jnp.where(half1, lax.rsqrt(s1 * (1.0 / 64) + RMS_EPS), lax.rsqrt(s0 * (1.0 / 64) + RMS_EPS))
    return x * r, r


def _rms_halves_bwd(dy, xhat, r, w, half1):
    g = dy * w
    t = g * xhat
    m0 = jnp.sum(jnp.where(half1, 0.0, t), axis=-1, keepdims=True) * (1.0 / 64)
    m1 = jnp.sum(jnp.where(half1, t, 0.0), axis=-1, keepdims=True) * (1.0 / 64)
    return r * (g - xhat * jnp.where(half1, m1, m0))


def _sigmoid(x):
    return 1.0 / (1.0 + jnp.exp(-x))


def _rope(x, c, s1, s2):
    return x * c + pltpu.roll(x, 112, 1) * s1 + pltpu.roll(x, 16, 1) * s2


def _rope_bwd(dy, c, s1, s2):
    return dy * c + pltpu.roll(dy * s1, 16, 1) + pltpu.roll(dy * s2, 112, 1)


def _fold_rows8(x):
    return jnp.sum(x.reshape(x.shape[0] // 8, 8, x.shape[1]), axis=0)


def _row0(v, rows=8):
    row = lax.broadcasted_iota(jnp.int32, (rows, v.shape[1]), 0)
    return jnp.where(row == 0, jnp.broadcast_to(v, (rows, v.shape[1])), 0.0)


def _mm_nn(a, b, name, out_dtype=F32, residual=None, tm=TM_PROJ):
    M, K = a.shape
    N = b.shape[1]
    tm = min(tm, M)

    def body(*refs):
        if residual is None:
            a_ref, b_ref, o_ref = refs
            acc = _dot(a_ref[...].astype(MXU_DTYPE), b_ref[...])
        else:
            a_ref, b_ref, r_ref, o_ref = refs
            acc = _dot(a_ref[...].astype(MXU_DTYPE), b_ref[...]) + r_ref[...]
        o_ref[...] = acc.astype(out_dtype)

    in_specs = [pl.BlockSpec((tm, K), lambda i: (i, 0)), pl.BlockSpec((K, N), lambda i: (0, 0))]
    args = [a, b]
    if residual is not None:
        in_specs.append(pl.BlockSpec((tm, N), lambda i: (i, 0)))
        args.append(residual)
    return pl.pallas_call(
        body, name=name, grid=(M // tm,), in_specs=in_specs,
        out_specs=pl.BlockSpec((tm, N), lambda i: (i, 0)), out_shape=_sds((M, N), out_dtype),
        compiler_params=_cp(("parallel",), 48))(*args)


def _mm_tn(a, b, name, out_dtype, tn, tk=512):
    T, M = a.shape
    N = b.shape[1]
    tk = min(tk, T)
    nk = T // tk

    def body(a_ref, b_ref, o_ref, acc_ref):
        k = pl.program_id(1)

        @pl.when(k == 0)
        def _():
            acc_ref[...] = jnp.zeros_like(acc_ref)

        acc_ref[...] += _dot_tn(a_ref[...].astype(MXU_DTYPE), b_ref[...].astype(MXU_DTYPE))

        @pl.when(k == nk - 1)
        def _():
            o_ref[...] = acc_ref[...].astype(out_dtype)

    return pl.pallas_call(
        body, name=name, grid=(N // tn, nk),
        in_specs=[pl.BlockSpec((tk, M), lambda n, k: (k, 0)), pl.BlockSpec((tk, tn), lambda n, k: (k, n))],
        out_specs=pl.BlockSpec((M, tn), lambda n, k: (0, n)), out_shape=_sds((M, N), out_dtype),
        scratch_shapes=[pltpu.VMEM((M, tn), F32)],
        compiler_params=_cp(("parallel", "arbitrary"), 48))(a, b)


def _inproj_fwd(x, ng, wp):
    T, D = x.shape
    tm = min(TM_PROJ, T)

    def body(x_ref, g_ref, w_ref, proj_ref, h_ref):
        xhat, _ = _rms(x_ref[...], D)
        h = (xhat * g_ref[...]).astype(MXU_DTYPE)
        h_ref[...] = h
        proj_ref[...] = _dot(h, w_ref[...])

    return pl.pallas_call(
        body, name="inproj_fwd", grid=(T // tm,),
        in_specs=[pl.BlockSpec((tm, D), lambda i: (i, 0)), pl.BlockSpec((1, D), lambda i: (0, 0)),
                  pl.BlockSpec((D, NP), lambda i: (0, 0))],
        out_specs=[pl.BlockSpec((tm, NP), lambda i: (i, 0)), pl.BlockSpec((tm, D), lambda i: (i, 0))],
        out_shape=[_sds((T, NP), F32), _sds((T, D), MXU_DTYPE)],
        compiler_params=_cp(("parallel",), 48))(x, ng, wp)


def _mla_prep_fwd(proj, lw, rope):
    T = proj.shape[0]
    tm = min(TM_ROW, T)

    def body(ql_ref, kvl_ref, kr_ref, qa_ref, kva_ref, wq_ref, wk_ref, wv_ref, qn_ref, kn_ref,
             c_ref, s1_ref, s2_ref, q_out, k_out, v_out):
        c, s1, s2 = c_ref[...], s1_ref[...], s2_ref[...]
        qhat, _ = _rms(ql_ref[...], MLA_Q_LORA)
        qn = (qhat * qa_ref[...]).astype(MXU_DTYPE)
        khat, _ = _rms(kvl_ref[...], MLA_KV_LORA)
        kvn = (khat * kva_ref[...]).astype(MXU_DTYPE)
        kr = kr_ref[...]
        half1 = lax.broadcasted_iota(jnp.int32, (tm, LANES), 1) >= 64
        for h in range(HEADS):
            xh, _ = _rms(_dot(qn, wq_ref[h]), MLA_QK)
            q_out[h] = _rope(xh * qn_ref[...], c, s1, s2).astype(MXU_DTYPE)
            xh, _ = _rms(_dot(kvn, wk_ref[h]) + kr, MLA_QK)
            k_out[h] = _rope(xh * kn_ref[...], c, s1, s2).astype(MXU_DTYPE)
        v = _dot(kvn, wv_ref[...])
        for h in range(HEADS):
            vp = v[:, LANES * (h // 2):LANES * (h // 2 + 1)]
            own = half1 if h % 2 else jnp.logical_not(half1)
            v_out[h] = jnp.where(own, vp, 0.0).astype(MXU_DTYPE)

    full = lambda shape: pl.BlockSpec(shape, lambda i: (0,) * len(shape))
    hd = pl.BlockSpec((HEADS, tm, LANES), lambda i: (0, i, 0))
    return pl.pallas_call(
        body, name="mla_prep_fwd", grid=(T // tm,),
        in_specs=[pl.BlockSpec((tm, 256), lambda i: (i, CB_QLAT)), pl.BlockSpec((tm, LANES), lambda i: (i, CB_KVLAT)),
                  pl.BlockSpec((tm, LANES), lambda i: (i, CB_KROPE)),
                  full((1, 256)), full((1, LANES)), full((HEADS, 256, LANES)), full((HEADS, LANES, LANES)),
                  full((LANES, 512)), full((1, LANES)), full((1, LANES)),
                  pl.BlockSpec((tm, LANES), lambda i: (i, 0)), pl.BlockSpec((tm, LANES), lambda i: (i, 0)),
                  pl.BlockSpec((tm, LANES), lambda i: (i, 0))],
        out_specs=[hd, hd, hd],
        out_shape=[_sds((HEADS, T, LANES), MXU_DTYPE)] * 3,
        compiler_params=_cp(("parallel",), 32))(
            proj, proj, proj, lw["qa"], lw["kva"], lw["wq"], lw["wk"], lw["wv"], lw["qn"], lw["kn"],
            rope[0], rope[1], rope[2])


def _mla_attn_fwd(q, k, vpad):
    T = q.shape[1]
    tq = min(TQ, T)
    scale = MLA_QK ** -0.5

    def body(q_ref, k_ref, v_ref, o_ref, lse_ref, m_s, l_s, acc_s):
        i = pl.program_id(1)
        row = lax.broadcasted_iota(jnp.int32, (tq, tq), 0)
        col = lax.broadcasted_iota(jnp.int32, (tq, tq), 1)
        o_tot = jnp.zeros((tq, LANES), F32)
        for r in range(2):
            qh = q_ref[r]
            m_s[...] = jnp.full((tq, 1), NEG_INF, F32)
            l_s[...] = jnp.zeros((tq, 1), F32)
            acc_s[...] = jnp.zeros((tq, LANES), F32)

            def step(kj, masked):
                rows = pl.ds(pl.multiple_of(kj * tq, tq), tq)
                s = _dot_nt(qh, k_ref[r, rows, :]) * scale
                if masked:
                    s = jnp.where(col <= row, s, NEG_INF)
                m_old = m_s[...]
                m_new = jnp.maximum(m_old, jnp.max(s, axis=-1, keepdims=True))
                alpha = jnp.exp(m_old - m_new)
                p = jnp.exp(s - m_new)
                l_s[...] = alpha * l_s[...] + jnp.sum(p, axis=-1, keepdims=True)
                acc_s[...] = alpha * acc_s[...] + _dot(p.astype(MXU_DTYPE), v_ref[r, rows, :])
                m_s[...] = m_new

            def loop_body(kj, carry):
                step(kj, False)
                return carry

            lax.fori_loop(0, i, loop_body, 0)
            step(i, True)
            o_tot = o_tot + acc_s[...] / l_s[...]
            lse_ref[r] = m_s[...] + jnp.log(l_s[...])
        o_ref[...] = o_tot

    return pl.pallas_call(
        body, name="mla_attn_fwd", grid=(HEADS // 2, T // tq),
        in_specs=[pl.BlockSpec((2, tq, LANES), lambda j, i: (j, i, 0)),
                  pl.BlockSpec((2, T, LANES), lambda j, i: (j, 0, 0)),
                  pl.BlockSpec((2, T, LANES), lambda j, i: (j, 0, 0))],
        out_specs=[pl.BlockSpec((tq, LANES), lambda j, i: (i, j)),
                   pl.BlockSpec((2, tq, 1), lambda j, i: (j, i, 0))],
        out_shape=[_sds((T, GROUP_WIDTH), F32), _sds((HEADS, T, 1), F32)],
        scratch_shapes=[pltpu.VMEM((tq, 1), F32), pltpu.VMEM((tq, 1), F32), pltpu.VMEM((tq, LANES), F32)],
        compiler_params=_cp(("parallel", "arbitrary"), 40))(q, k, vpad)


def _swa_masks(nb_first):
    qi = lax.broadcasted_iota(jnp.int32, (BLOCK, 2 * BLOCK), 0)
    ki = lax.broadcasted_iota(jnp.int32, (BLOCK, 2 * BLOCK), 1)
    dist = BLOCK + qi - ki
    valid = (dist >= 0) & (dist < BLOCK) & ((ki >= BLOCK) | jnp.logical_not(nb_first))
    return dist.astype(F32), valid


def _swa_kv_variants(x, half1):
    xs = pltpu.roll(x, 64, 1)
    out = {}
    for g in range(2):
        for r in range(2):
            own = half1 if r else jnp.logical_not(half1)
            out[(g, r)] = jnp.where(own, x if g == r else xs, 0.0).astype(MXU_DTYPE)
    return out


def _swa_fwd(proj, lw):
    T = proj.shape[0]
    tm = min(TM_SWA, T)
    nb = tm // BLOCK
    scale = SWA_HEAD_DIM ** -0.5

    def body(q_ref, k_ref, v_ref, pk_ref, pv_ref, qw_ref, kw_ref, sink_ref, o_ref):
        i = pl.program_id(0)
        half1 = lax.broadcasted_iota(jnp.int32, (1, LANES), 1) >= 64
        k_all = jnp.concatenate([pk_ref[...], k_ref[...]], axis=0)
        v_all = jnp.concatenate([pv_ref[...], v_ref[...]], axis=0)
        khat, _ = _rms_halves(k_all, half1)
        kp = _swa_kv_variants(khat * kw_ref[...], half1)
        vp = _swa_kv_variants(v_all, half1)
        qn = []
        for j in range(4):
            qhat, _ = _rms_halves(q_ref[:, LANES * j:LANES * (j + 1)], half1)
            qn.append((qhat * qw_ref[...]).astype(MXU_DTYPE))
        for b in range(nb):
            dist, valid = _swa_masks((i == 0) & (b == 0))
            ks = slice(b * BLOCK, b * BLOCK + 2 * BLOCK)
            for j in range(4):
                g = j // 2
                qb = qn[j][b * BLOCK:(b + 1) * BLOCK]
                o = jnp.zeros((BLOCK, LANES), F32)
                for r in range(2):
                    h = 2 * j + r
                    s = _dot_nt(qb, kp[(g, r)][ks]) * scale - (2.0 ** -(h + 1)) * dist
                    s = jnp.where(valid, s, NEG_INF)
                    sink = sink_ref[h]
                    m = jnp.maximum(jnp.max(s, axis=-1, keepdims=True), sink)
                    e = jnp.exp(s - m)
                    den = jnp.sum(e, axis=-1, keepdims=True) + jnp.exp(sink - m)
                    o = o + _dot((e / den).astype(MXU_DTYPE), vp[(g, r)][ks])
                o_ref[b * BLOCK:(b + 1) * BLOCK, LANES * j:LANES * (j + 1)] = o

    prev = lambda cb: pl.BlockSpec((BLOCK, LANES), lambda i: (jnp.maximum(i * nb - 1, 0), cb))
    return pl.pallas_call(
        body, name="swa_fwd", grid=(T // tm,),
        in_specs=[pl.BlockSpec((tm, 512), lambda i: (i, CB_SQ)), pl.BlockSpec((tm, LANES), lambda i: (i, CB_SK)),
                  pl.BlockSpec((tm, LANES), lambda i: (i, CB_SV)), prev(CB_SK), prev(CB_SV),
                  pl.BlockSpec((1, LANES), lambda i: (0, 0)), pl.BlockSpec((1, LANES), lambda i: (0, 0)),
                  pl.BlockSpec(memory_space=pltpu.SMEM)],
        out_specs=pl.BlockSpec((tm, 512), lambda i: (i, 0)),
        out_shape=_sds((T, GROUP_WIDTH), F32),
        compiler_params=_cp(("parallel",), 32))(proj, proj, proj, proj, proj, lw["sqn"], lw["skn"], lw["sinks"])


def _shift_down(u, prev, n, row):
    tm = u.shape[0]
    out = pltpu.roll(u, n, 0)
    row8 = lax.broadcasted_iota(jnp.int32, prev.shape, 0)
    for t in range(n):
        src = jnp.sum(jnp.where(row8 == 8 - n + t, prev, 0.0), axis=0, keepdims=True)
        out = jnp.where(row == t, src, out)
    return out


def _shift_up(u, nxt, n, row):
    tm = u.shape[0]
    out = pltpu.roll(u, tm - n, 0)
    row8 = lax.broadcasted_iota(jnp.int32, nxt.shape, 0)
    for t in range(n):
        src = jnp.sum(jnp.where(row8 == t, nxt, 0.0), axis=0, keepdims=True)
        out = jnp.where(row == tm - n + t, src, out)
    return out


def _mix_fwd(proj, o_mla, o_swa, conv_w):
    T = proj.shape[0]
    tm = min(TM_ROW, T)

    def body(gm_ref, ch_ref, cb_ref, cc_ref, gc_ref, gs_ref, pch_ref, pcc_ref, om_ref, os_ref, w_ref, y_ref):
        i = pl.program_id(0)
        row = lax.broadcasted_iota(jnp.int32, (tm, GROUP_WIDTH), 0)
        u = cc_ref[...] * ch_ref[...]
        u_prev = jnp.where(i > 0, pcc_ref[...] * pch_ref[...], 0.0)
        z = (w_ref[0:1, :] * _shift_down(u, u_prev, 2, row) + w_ref[1:2, :] * _shift_down(u, u_prev, 1, row)
             + w_ref[2:3, :] * u)
        gm, gc, gs = gm_ref[...], gc_ref[...], gs_ref[...]
        y_ref[:, 0:512] = (om_ref[...] * (gm * _sigmoid(gm))).astype(MXU_DTYPE)
        y_ref[:, 512:1024] = (cb_ref[...] * z * (gc * _sigmoid(gc))).astype(MXU_DTYPE)
        y_ref[:, 1024:1536] = (os_ref[...] * (gs * _sigmoid(gs))).astype(MXU_DTYPE)

    blk = lambda cb: pl.BlockSpec((tm, 512), lambda i: (i, cb))
    prev = lambda cb: pl.BlockSpec((8, 512), lambda i: (jnp.maximum(i * (tm // 8) - 1, 0), cb))
    tile = pl.BlockSpec((tm, 512), lambda i: (i, 0))
    return pl.pallas_call(
        body, name="mix_fwd", grid=(T // tm,),
        in_specs=[blk(CB_GMLA), blk(CB_CH), blk(CB_CB), blk(CB_CC), blk(CB_GCONV), blk(CB_GSWA),
                  prev(CB_CH), prev(CB_CC), tile, tile, pl.BlockSpec((8, 512), lambda i: (0, 0))],
        out_specs=pl.BlockSpec((tm, D_MIX), lambda i: (i, 0)),
        out_shape=_sds((T, D_MIX), MXU_DTYPE),
        compiler_params=_cp(("parallel",), 32))(
            proj, proj, proj, proj, proj, proj, proj, proj, o_mla, o_swa, conv_w)


def _loss_grad(y, target):
    T, D = y.shape
    tm = min(TM_ROW, T)
    nt = T // tm

    def body(y_ref, t_ref, g_ref, loss_ref, acc_ref):
        i = pl.program_id(0)

        @pl.when(i == 0)
        def _():
            acc_ref[...] = jnp.zeros_like(acc_ref)

        err = y_ref[...] - t_ref[...]
        g_ref[...] = err * (1.0 / D)
        acc_ref[...] += _fold_rows8(err * err)

        @pl.when(i == nt - 1)
        def _():
            tot = jnp.sum(jnp.sum(acc_ref[...], axis=1, keepdims=True), axis=0, keepdims=True)
            loss_ref[...] = jnp.broadcast_to(tot * (0.5 / D), (8, LANES))

    return pl.pallas_call(
        body, name="loss_grad", grid=(nt,),
        in_specs=[pl.BlockSpec((tm, D), lambda i: (i, 0)), pl.BlockSpec((tm, D), lambda i: (i, 0))],
        out_specs=[pl.BlockSpec((tm, D), lambda i: (i, 0)), pl.BlockSpec((8, LANES), lambda i: (0, 0))],
        out_shape=[_sds((T, D), F32), _sds((8, LANES), F32)],
        scratch_shapes=[pltpu.VMEM((8, D), F32)],
        compiler_params=_cp(("arbitrary",), 32))(y, target)


def _mix_bwd(dycat, proj, o_mla, o_swa, conv_w):
    T = proj.shape[0]
    tm = min(TM_ROW, T)
    nt = T // tm

    def body(dym_ref, dyc_ref, dys_ref, gm_ref, ch_ref, cb_ref, cc_ref, gc_ref, gs_ref, pch_ref, pcc_ref,
             ndy_ref, ncb_ref, ngc_ref, om_ref, os_ref, w_ref,
             d1_ref, dgs_ref, dom_ref, dos_ref, dw_ref):
        i = pl.program_id(0)

        @pl.when(i == 0)
        def _():
            dw_ref[...] = jnp.zeros_like(dw_ref)

        row = lax.broadcasted_iota(jnp.int32, (tm, GROUP_WIDTH), 0)

        def gate(g):
            sg = _sigmoid(g)
            return g * sg, sg * (1.0 + g * (1.0 - sg))

        gm = gm_ref[...]
        silu, dsilu = gate(gm)
        dym = dym_ref[...]
        dom_ref[...] = dym * silu
        d1_ref[:, 0:512] = (dym * om_ref[...] * dsilu).astype(MXU_DTYPE)

        gs = gs_ref[...]
        silu, dsilu = gate(gs)
        dys = dys_ref[...]
        dos_ref[...] = dys * silu
        dgs_ref[...] = (dys * os_ref[...] * dsilu).astype(MXU_DTYPE)

        ch, cb, cc, gc, dyc = ch_ref[...], cb_ref[...], cc_ref[...], gc_ref[...], dyc_ref[...]
        w0, w1, w2 = w_ref[0:1, :], w_ref[1:2, :], w_ref[2:3, :]
        u = cc * ch
        u_prev = jnp.where(i > 0, pcc_ref[...] * pch_ref[...], 0.0)
        u1 = _shift_down(u, u_prev, 1, row)
        u2 = _shift_down(u, u_prev, 2, row)
        z = w0 * u2 + w1 * u1 + w2 * u
        silu, dsilu = gate(gc)
        dz = dyc * cb * silu
        ngc = ngc_ref[...]
        dz_next = jnp.where(i < nt - 1, ndy_ref[...] * ncb_ref[...] * (ngc * _sigmoid(ngc)), 0.0)
        du = w2 * dz + w1 * _shift_up(dz, dz_next, 1, row) + w0 * _shift_up(dz, dz_next, 2, row)
        d1_ref[:, 512:1024] = (du * cc).astype(MXU_DTYPE)
        d1_ref[:, 1024:1536] = (dyc * z * silu).astype(MXU_DTYPE)
        d1_ref[:, 1536:2048] = (du * ch).astype(MXU_DTYPE)
        d1_ref[:, 2048:2560] = (dyc * cb * z * dsilu).astype(MXU_DTYPE)
        row8 = lax.broadcasted_iota(jnp.int32, (8, GROUP_WIDTH), 0)
        dw = jnp.zeros((8, GROUP_WIDTH), F32)
        for t, shifted in enumerate((u2, u1, u)):
            dw = dw + jnp.where(row8 == t, jnp.sum(dz * shifted, axis=0, keepdims=True), 0.0)
        dw_ref[...] += dw

    blk = lambda cb: pl.BlockSpec((tm, 512), lambda i: (i, cb))
    prev = lambda cb: pl.BlockSpec((8, 512), lambda i: (jnp.maximum(i * (tm // 8) - 1, 0), cb))
    nxt = lambda cb: pl.BlockSpec((8, 512), lambda i: (jnp.minimum((i + 1) * (tm // 8), T // 8 - 1), cb))
    tile = pl.BlockSpec((tm, 512), lambda i: (i, 0))
    return pl.pallas_call(
        body, name="mix_bwd", grid=(nt,),
        in_specs=[blk(0), blk(1), blk(2), blk(CB_GMLA), blk(CB_CH), blk(CB_CB), blk(CB_CC), blk(CB_GCONV),
                  blk(CB_GSWA), prev(CB_CH), prev(CB_CC), nxt(1), nxt(CB_CB), nxt(CB_GCONV), tile, tile,
                  pl.BlockSpec((8, 512), lambda i: (0, 0))],
        out_specs=[pl.BlockSpec((tm, 2560), lambda i: (i, 0)), tile, tile, tile,
                   pl.BlockSpec((8, 512), lambda i: (0, 0))],
        out_shape=[_sds((T, 2560), MXU_DTYPE), _sds((T, 512), MXU_DTYPE), _sds((T, 512), F32),
                   _sds((T, 512), F32), _sds((8, 512), F32)],
        compiler_params=_cp(("arbitrary",), 48))(
            dycat, dycat, dycat, proj, proj, proj, proj, proj, proj, proj, proj, dycat, proj, proj,
            o_mla, o_swa, conv_w)


def _swa_bwd(proj, o_swa, do_swa, lw):
    T = proj.shape[0]
    tm = min(TM_SWA, T)
    nb = tm // BLOCK
    scale = SWA_HEAD_DIM ** -0.5

    def body(q_ref, k_ref, v_ref, pk_ref, pv_ref, o_ref, do_ref, qw_ref, kw_ref, sink_ref,
             dq_ref, dk_ref, dv_ref, dqw_ref, dsink_ref):
        i = pl.program_id(0)

        @pl.when(i == 0)
        def _():
            dk_ref[...] = jnp.zeros_like(dk_ref)
            dv_ref[...] = jnp.zeros_like(dv_ref)
            dqw_ref[...] = jnp.zeros_like(dqw_ref)
            dsink_ref[...] = jnp.zeros_like(dsink_ref)

        half1 = lax.broadcasted_iota(jnp.int32, (1, LANES), 1) >= 64
        k_all = jnp.concatenate([pk_ref[...], k_ref[...]], axis=0)
        v_all = jnp.concatenate([pv_ref[...], v_ref[...]], axis=0)
        khat, _ = _rms_halves(k_all, half1)
        kp = _swa_kv_variants(khat * kw_ref[...], half1)
        vp = _swa_kv_variants(v_all, half1)
        qw = qw_ref[...]
        dqw = jnp.zeros((1, LANES), F32)
        dsink_rows = [jnp.zeros((1, 1), F32) for _ in range(HEADS)]
        for j in range(4):
            g = j // 2
            cols = slice(LANES * j, LANES * (j + 1))
            qhat, qr = _rms_halves(q_ref[:, cols], half1)
            qn = (qhat * qw).astype(MXU_DTYPE)
            do = do_ref[:, cols]
            dob = do.astype(MXU_DTYPE)
            prod = do * o_ref[:, cols]
            dqn_blocks = []
            for b in range(nb):
                dist, valid = _swa_masks((i == 0) & (b == 0))
                ks = slice(b * BLOCK, b * BLOCK + 2 * BLOCK)
                rs = slice(b * BLOCK, (b + 1) * BLOCK)
                qb = qn[rs]
                dqn = jnp.zeros((BLOCK, LANES), F32)
                for r in range(2):
                    h = 2 * j + r
                    own = half1 if r else jnp.logical_not(half1)
                    s = _dot_nt(qb, kp[(g, r)][ks]) * scale - (2.0 ** -(h + 1)) * dist
                    s = jnp.where(valid, s, NEG_INF)
                    sink = sink_ref[h]
                    m = jnp.maximum(jnp.max(s, axis=-1, keepdims=True), sink)
                    e = jnp.exp(s - m)
                    es = jnp.exp(sink - m)
                    inv = 1.0 / (jnp.sum(e, axis=-1, keepdims=True) + es)
                    p = e * inv
                    dd = jnp.sum(jnp.where(own, prod[rs], 0.0), axis=-1, keepdims=True)
                    dp = _dot_nt(dob[rs], vp[(g, r)][ks])
                    ds = (p * (dp - dd) * scale).astype(MXU_DTYPE)
                    dsink_rows[h] = dsink_rows[h] - jnp.sum(es * inv * dd, axis=0, keepdims=True)
                    dqn = dqn + _dot(ds, kp[(g, r)][ks])
                    dkp = jnp.where(own, _dot_tn(ds, qb), 0.0)
                    dvp = jnp.where(own, _dot_tn(p.astype(MXU_DTYPE), dob[rs]), 0.0)
                    if g != r:
                        dkp = pltpu.roll(dkp, 64, 1)
                        dvp = pltpu.roll(dvp, 64, 1)
                    dst = pl.ds(pl.multiple_of((i * nb + b) * BLOCK, BLOCK), 2 * BLOCK)
                    dk_ref[dst, :] += dkp
                    dv_ref[dst, :] += dvp
                dqn_blocks.append(dqn)
            dqn = jnp.concatenate(dqn_blocks, axis=0) if nb > 1 else dqn_blocks[0]
            dqw = dqw + jnp.sum(dqn * qhat, axis=0, keepdims=True)
            dq_ref[:, cols] = _rms_halves_bwd(dqn, qhat, qr, qw, half1).astype(MXU_DTYPE)
        dqw_ref[...] += _row0(dqw + pltpu.roll(dqw, 64, 1))
        row8 = lax.broadcasted_iota(jnp.int32, (8, LANES), 0)
        dsink = jnp.zeros((8, LANES), F32)
        for h in range(HEADS):
            dsink = dsink + jnp.where(row8 == h, jnp.broadcast_to(dsink_rows[h], (8, LANES)), 0.0)
        dsink_ref[...] += dsink

    prev = lambda cb: pl.BlockSpec((BLOCK, LANES), lambda i: (jnp.maximum(i * nb - 1, 0), cb))
    tile = pl.BlockSpec((tm, 512), lambda i: (i, 0))
    small = pl.BlockSpec((8, LANES), lambda i: (0, 0))
    acc = pl.BlockSpec((T + BLOCK, LANES), lambda i: (0, 0))
    return pl.pallas_call(
        body, name="swa_bwd", grid=(T // tm,),
        in_specs=[pl.BlockSpec((tm, 512), lambda i: (i, CB_SQ)), pl.BlockSpec((tm, LANES), lambda i: (i, CB_SK)),
                  pl.BlockSpec((tm, LANES), lambda i: (i, CB_SV)), prev(CB_SK), prev(CB_SV), tile, tile,
                  pl.BlockSpec((1, LANES), lambda i: (0, 0)), pl.BlockSpec((1, LANES), lambda i: (0, 0)),
                  pl.BlockSpec(memory_space=pltpu.SMEM)],
        out_specs=[tile, acc, acc, small, small],
        out_shape=[_sds((T, 512), MXU_DTYPE), _sds((T + BLOCK, LANES), F32), _sds((T + BLOCK, LANES), F32),
                   _sds((8, LANES), F32), _sds((8, LANES), F32)],
        compiler_params=_cp(("arbitrary",), 40))(
            proj, proj, proj, proj, proj, o_swa, do_swa, lw["sqn"], lw["skn"], lw["sinks"])


def _swa_kv_bwd(proj, dkn, dv, lw):
    T = proj.shape[0]
    tm = BLOCK

    def body(k_ref, dkn_ref, dv_ref, kw_ref, d_ref, dkw_ref):
        i = pl.program_id(0)

        @pl.when(i == 0)
        def _():
            dkw_ref[...] = jnp.zeros_like(dkw_ref)

        half1 = lax.broadcasted_iota(jnp.int32, (1, LANES), 1) >= 64
        khat, kr = _rms_halves(k_ref[...], half1)
        dkn_t = dkn_ref[...]
        dkw = jnp.sum(dkn_t * khat, axis=0, keepdims=True)
        dkw_ref[...] += _row0(dkw + pltpu.roll(dkw, 64, 1))
        d_ref[:, 0:LANES] = _rms_halves_bwd(dkn_t, khat, kr, kw_ref[...], half1).astype(MXU_DTYPE)
        d_ref[:, LANES:2 * LANES] = dv_ref[...].astype(MXU_DTYPE)

    return pl.pallas_call(
        body, name="swa_kv_bwd", grid=(T // tm,),
        in_specs=[pl.BlockSpec((tm, LANES), lambda i: (i, CB_SK)), pl.BlockSpec((tm, LANES), lambda i: (i + 1, 0)),
                  pl.BlockSpec((tm, LANES), lambda i: (i + 1, 0)), pl.BlockSpec((1, LANES), lambda i: (0, 0))],
        out_specs=[pl.BlockSpec((tm, 2 * LANES), lambda i: (i, 0)), pl.BlockSpec((8, LANES), lambda i: (0, 0))],
        out_shape=[_sds((T, 2 * LANES), MXU_DTYPE), _sds((8, LANES), F32)],
        compiler_params=_cp(("arbitrary",), 32))(proj, dkn, dv, lw["skn"])


def _mla_attn_bwd(q, k, vpad, o, do, lse):
    T = q.shape[1]
    tq = min(TQ, T)
    scale = MLA_QK ** -0.5

    def body(q_ref, k_ref, v_ref, o_ref, do_ref, lse_ref, dq_ref, dk_ref, dv_ref, dq_s):
        h = pl.program_id(0)
        i = pl.program_id(1)

        @pl.when(i == 0)
        def _():
            dk_ref[...] = jnp.zeros_like(dk_ref)
            dv_ref[...] = jnp.zeros_like(dv_ref)

        row = lax.broadcasted_iota(jnp.int32, (tq, tq), 0)
        col = lax.broadcasted_iota(jnp.int32, (tq, tq), 1)
        own = (lax.broadcasted_iota(jnp.int32, (1, LANES), 1) // 64) == (h % 2)
        do_t = do_ref[...]
        dob = do_t.astype(MXU_DTYPE)
        dd = jnp.sum(jnp.where(own, do_t * o_ref[...], 0.0), axis=-1, keepdims=True)
        qh = q_ref[0]
        lse_t = lse_ref[0]
        dq_s[...] = jnp.zeros((tq, LANES), F32)

        def step(kj, masked):
            rows = pl.ds(pl.multiple_of(kj * tq, tq), tq)
            kt = k_ref[0, rows, :]
            s = _dot_nt(qh, kt) * scale
            if masked:
                s = jnp.where(col <= row, s, NEG_INF)
            p = jnp.exp(s - lse_t)
            dp = _dot_nt(dob, v_ref[0, rows, :])
            ds = (p * (dp - dd) * scale).astype(MXU_DTYPE)
            dq_s[...] += _dot(ds, kt)
            dk_ref[0, rows, :] += _dot_tn(ds, qh)
            dv_ref[0, rows, :] += jnp.where(own, _dot_tn(p.astype(MXU_DTYPE), dob), 0.0)

        def loop_body(kj, carry):
            step(kj, False)
            return carry

        lax.fori_loop(0, i, loop_body, 0)
        step(i, True)
        dq_ref[0] = dq_s[...]

    res = pl.BlockSpec((1, T, LANES), lambda h, i: (h, 0, 0))
    return pl.pallas_call(
        body, name="mla_attn_bwd", grid=(HEADS, T // tq),
        in_specs=[pl.BlockSpec((1, tq, LANES), lambda h, i: (h, i, 0)), res, res,
                  pl.BlockSpec((tq, LANES), lambda h, i: (i, h // 2)),
                  pl.BlockSpec((tq, LANES), lambda h, i: (i, h // 2)),
                  pl.BlockSpec((1, tq, 1), lambda h, i: (h, i, 0))],
        out_specs=[pl.BlockSpec((1, tq, LANES), lambda h, i: (h, i, 0)), res, res],
        out_shape=[_sds((HEADS, T, LANES), F32)] * 3,
        scratch_shapes=[pltpu.VMEM((tq, LANES), F32)],
        compiler_params=_cp(("parallel", "arbitrary"), 48))(q, k, vpad, o, do, lse)


def _mla_prep_bwd(proj, dq, dk, dv, lw, rope):
    T = proj.shape[0]
    tm = min(TM_ROW, T)

    def body(ql_ref, kvl_ref, kr_ref, dq_ref, dk_ref, dv_ref, qa_ref, kva_ref, wq_ref, wk_ref, wv_ref,
             wqt_ref, wkt_ref, wvt_ref, qn_ref, kn_ref, c_ref, s1_ref, s2_ref,
             d_ref, dwq_ref, dwk_ref, dwv_ref, dqa_ref, dkva_ref, dqn_ref, dkn_ref):
        i = pl.program_id(0)

        @pl.when(i == 0)
        def _():
            for ref in (dwq_ref, dwk_ref, dwv_ref, dqa_ref, dkva_ref, dqn_ref, dkn_ref):
                ref[...] = jnp.zeros_like(ref)

        c, s1, s2 = c_ref[...], s1_ref[...], s2_ref[...]
        lane = lax.broadcasted_iota(jnp.int32, (1, LANES), 1)
        qlhat, qlr = _rms(ql_ref[...], MLA_Q_LORA)
        qn = (qlhat * qa_ref[...]).astype(MXU_DTYPE)
        kvhat, kvr = _rms(kvl_ref[...], MLA_KV_LORA)
        kvn = (kvhat * kva_ref[...]).astype(MXU_DTYPE)
        kr = kr_ref[...]
        dqnl = jnp.zeros((tm, MLA_Q_LORA), F32)
        dkvn = jnp.zeros((tm, MLA_KV_LORA), F32)
        dkr = jnp.zeros((tm, LANES), F32)
        dqw = jnp.zeros((1, LANES), F32)
        dkw = jnp.zeros((1, LANES), F32)
        for h in range(HEADS):
            xh, r = _rms(_dot(qn, wq_ref[h]), MLA_QK)
            dy = _rope_bwd(dq_ref[h], c, s1, s2)
            dqw = dqw + jnp.sum(dy * xh, axis=0, keepdims=True)
            dx = _rms_bwd(dy, xh, r, qn_ref[...], MLA_QK).astype(MXU_DTYPE)
            dwq_ref[h] += _dot_tn(qn, dx)
            dqnl = dqnl + _dot(dx, wqt_ref[h])

            xh, r = _rms(_dot(kvn, wk_ref[h]) + kr, MLA_QK)
            dy = _rope_bwd(dk_ref[h], c, s1, s2)
            dkw = dkw + jnp.sum(dy * xh, axis=0, keepdims=True)
            dxf = _rms_bwd(dy, xh, r, kn_ref[...], MLA_QK)
            dkr = dkr + dxf
            dx = dxf.astype(MXU_DTYPE)
            dwk_ref[h] += _dot_tn(kvn, dx)
            dkvn = dkvn + _dot(dx, wkt_ref[h])
        dvc = jnp.concatenate([dv_ref[2 * j] + dv_ref[2 * j + 1] for j in range(4)], axis=1).astype(MXU_DTYPE)
        dwv_ref[...] += _dot_tn(kvn, dvc)
        dkvn = dkvn + _dot(dvc, wvt_ref[...])
        dqa_ref[...] += _row0(jnp.sum(dqnl * qlhat, axis=0, keepdims=True))
        dkva_ref[...] += _row0(jnp.sum(dkvn * kvhat, axis=0, keepdims=True))
        dqn_ref[...] += _row0(dqw)
        dkn_ref[...] += _row0(dkw)
        d_ref[:, 0:256] = _rms_bwd(dqnl, qlhat, qlr, qa_ref[...], MLA_Q_LORA).astype(MXU_DTYPE)
        d_ref[:, 256:384] = _rms_bwd(dkvn, kvhat, kvr, kva_ref[...], MLA_KV_LORA).astype(MXU_DTYPE)
        d_ref[:, 384:512] = jnp.where((lane >= 64) & (lane < 96), dkr, 0.0).astype(MXU_DTYPE)

    full = lambda shape: pl.BlockSpec(shape, lambda i: (0,) * len(shape))
    hd = pl.BlockSpec((HEADS, tm, LANES), lambda i: (0, i, 0))
    tab = pl.BlockSpec((tm, LANES), lambda i: (i, 0))
    return pl.pallas_call(
        body, name="mla_prep_bwd", grid=(T // tm,),
        in_specs=[pl.BlockSpec((tm, 256), lambda i: (i, CB_QLAT)), pl.BlockSpec((tm, LANES), lambda i: (i, CB_KVLAT)),
                  pl.BlockSpec((tm, LANES), lambda i: (i, CB_KROPE)), hd, hd, hd,
                  full((1, 256)), full((1, LANES)), full((HEADS, 256, LANES)), full((HEADS, LANES, LANES)),
                  full((LANES, 512)), full((HEADS, LANES, 256)), full((HEADS, LANES, LANES)), full((512, LANES)),
                  full((1, LANES)), full((1, LANES)), tab, tab, tab],
        out_specs=[pl.BlockSpec((tm, 512), lambda i: (i, 0)), full((HEADS, 256, LANES)),
                   full((HEADS, LANES, LANES)), full((LANES, 512)), full((8, 256)), full((8, LANES)),
                   full((8, LANES)), full((8, LANES))],
        out_shape=[_sds((T, 512), MXU_DTYPE), _sds((HEADS, 256, LANES), F32), _sds((HEADS, LANES, LANES), F32),
                   _sds((LANES, 512), F32), _sds((8, 256), F32), _sds((8, LANES), F32), _sds((8, LANES), F32),
                   _sds((8, LANES), F32)],
        compiler_params=_cp(("arbitrary",), 48))(
            proj, proj, proj, dq, dk, dv, lw["qa"], lw["kva"], lw["wq"], lw["wk"], lw["wv"],
            lw["wqt"], lw["wkt"], lw["wvt"], lw["qn"], lw["kn"], rope[0], rope[1], rope[2])


def _norm_bwd(dh, x, g_in, ng):
    T, D = x.shape
    tm = min(TM_ROW, T)

    def body(dh_ref, x_ref, g_ref, w_ref, dx_ref, dw_ref):
        i = pl.program_id(0)

        @pl.when(i == 0)
        def _():
            dw_ref[...] = jnp.zeros_like(dw_ref)

        xhat, r = _rms(x_ref[...], D)
        dh_t = dh_ref[...]
        dw_ref[...] += _row0(jnp.sum(dh_t * xhat, axis=0, keepdims=True))
        dx_ref[...] = g_ref[...] + _rms_bwd(dh_t, xhat, r, w_ref[...], D)

    tile = pl.BlockSpec((tm, D), lambda i: (i, 0))
    return pl.pallas_call(
        body, name="norm_bwd", grid=(T // tm,),
        in_specs=[tile, tile, tile, pl.BlockSpec((1, D), lambda i: (0, 0))],
        out_specs=[tile, pl.BlockSpec((8, D), lambda i: (0, 0))],
        out_shape=[_sds((T, D), F32), _sds((8, D), F32)],
        compiler_params=_cp(("arbitrary",), 32))(dh, x, g_in, ng)


def _rope_tables(T):
    half = MLA_ROPE // 2
    inv_freq = jnp.power(jnp.float32(ROPE_THETA), -jnp.arange(half, dtype=F32) / half)
    ang = jnp.arange(T, dtype=F32)[:, None] * inv_freq[None, :]
    cos, sin = jnp.cos(ang), jnp.sin(ang)
    z = lambda n: jnp.zeros((T, n), F32)
    c = jnp.concatenate([jnp.ones((T, MLA_NOPE), F32), cos, cos, z(32)], axis=1)
    s1 = jnp.concatenate([z(64), -sin, z(48)], axis=1)
    s2 = jnp.concatenate([z(80), sin, z(32)], axis=1)
    return c, s1, s2


def _pad_lanes(v, n=LANES):
    v = v.reshape(1, -1)
    return jnp.pad(v, ((0, 0), (0, n - v.shape[1])))


def _pack_win(w):
    z = lambda n: jnp.zeros((w.shape[0], n), w.dtype)
    return jnp.concatenate([w[:, 0:384], z(64), w[:, 384:416], z(32), w[:, 416:2976], w[:, 2976:3488],
                            w[:, 3744:4256], w[:, 3488:3616], w[:, 3616:3744]], axis=1)


def _unpack_dwin(d):
    return jnp.concatenate([d[:, 0:384], d[:, 448:480], d[:, 512:3072], d[:, 3072:3584], d[:, 4096:4224],
                            d[:, 4224:4352], d[:, 3584:4096]], axis=1)


def _layer_weights(l, norm_g, w_in_full, qa, wqb_full, kva, wkvb_full, qn, kn, conv_full, sqn, skn, sinks,
                   w_out_full):
    wp = _pack_win(w_in_full)
    wq = jnp.pad(wqb_full, ((0, 0), (0, 0), (0, LANES - MLA_QK)))
    wk = jnp.pad(wkvb_full[:, :, :MLA_NOPE], ((0, 0), (0, 0), (0, LANES - MLA_NOPE)))
    wv = jnp.transpose(wkvb_full[:, :, MLA_NOPE:], (1, 0, 2)).reshape(MLA_KV_LORA, GROUP_WIDTH)
    return dict(
        ng=norm_g[l].reshape(1, -1), wp=wp, wpt=wp.T, qa=qa[l].reshape(1, -1), kva=kva[l].reshape(1, -1),
        wq=wq, wk=wk, wv=wv, wqt=jnp.transpose(wq, (0, 2, 1)), wkt=jnp.transpose(wk, (0, 2, 1)), wvt=wv.T,
        qn=_pad_lanes(qn[l]), kn=_pad_lanes(kn[l]),
        conv=jnp.pad(conv_full, ((0, 5), (0, 0))),
        sqn=jnp.tile(sqn[l].reshape(1, -1), (1, 2)), skn=jnp.tile(skn[l].reshape(1, -1), (1, 2)),
        sinks=sinks[l], wo=w_out_full, wot=w_out_full.T)


def _layer_fwd(x, lw, rope):
    proj, h = _inproj_fwd(x, lw["ng"], lw["wp"])
    q, k, vpad = _mla_prep_fwd(proj, lw, rope)
    o_mla, lse = _mla_attn_fwd(q, k, vpad)
    o_swa = _swa_fwd(proj, lw)
    ycat = _mix_fwd(proj, o_mla, o_swa, lw["conv"])
    x_next = _mm_nn(ycat, lw["wo"], "outproj_fwd", residual=x)
    return x_next, dict(x=x, proj=proj, h=h, q=q, k=k, vpad=vpad, o_mla=o_mla, lse=lse, o_swa=o_swa, ycat=ycat)


def _layer_bwd(g, sv, lw, rope):
    proj = sv["proj"]
    dycat = _mm_nn(g, lw["wot"], "outproj_bwd_dy")
    d_wo = _mm_tn(sv["ycat"], g, "outproj_bwd_dw", WIRE_DTYPE, tn=D_MODEL)
    d1, dgs, do_mla, do_swa, d_conv = _mix_bwd(dycat, proj, sv["o_mla"], sv["o_swa"], lw["conv"])
    dsq, dkn_acc, dv_acc, d_sqn, d_sinks = _swa_bwd(proj, sv["o_swa"], do_swa, lw)
    dskv, d_skn = _swa_kv_bwd(proj, dkn_acc, dv_acc, lw)
    dq, dk, dv = _mla_attn_bwd(sv["q"], sv["k"], sv["vpad"], sv["o_mla"], do_mla, sv["lse"])
    dmla, d_wq, d_wk, d_wv, d_qa, d_kva, d_qn, d_kn = _mla_prep_bwd(proj, dq, dk, dv, lw, rope)
    dproj = jnp.concatenate([dmla, d1, dsq, dgs, dskv], axis=1)
    dh = _mm_nn(dproj, lw["wpt"], "inproj_bwd_dh")
    dx, d_ng = _norm_bwd(dh, sv["x"], g, lw["ng"])
    d_wp = _mm_tn(sv["h"], dproj, "inproj_bwd_dw", WIRE_DTYPE, tn=NP // 2)
    grads = dict(
        w_in=_unpack_dwin(d_wp), w_out=d_wo,
        w_qb=d_wq[:, :, :MLA_QK],
        w_kvb=jnp.concatenate([d_wk[:, :, :MLA_NOPE],
                               jnp.transpose(d_wv.reshape(MLA_KV_LORA, HEADS, MLA_NOPE), (1, 0, 2))], axis=2),
        conv=d_conv[0:3], norm_g=d_ng[0], qa=d_qa[0], kva=d_kva[0], qn=d_qn[0, :MLA_QK], kn=d_kn[0, :MLA_QK],
        sqn=d_sqn[0, :SWA_HEAD_DIM], skn=d_skn[0, :SWA_HEAD_DIM], sinks=d_sinks[:, 0])
    return dx, grads


def _local_step(x, target, lws, rope):
    saved = []
    for lw in lws:
        x, sv = _layer_fwd(x, lw, rope)
        saved.append(sv)
    g, loss_tile = _loss_grad(x, target)
    grads = [None] * len(lws)
    for l in reversed(range(len(lws))):
        g, grads[l] = _layer_bwd(g, saved[l], lws[l], rope)
    return loss_tile, g, grads


def _my_coords():
    return lax.axis_index("x"), lax.axis_index("y"), lax.axis_index("c")


def _peer(me, k):
    x, y, c = me
    return (1 - x if k & 4 else x, 1 - y if k & 2 else y, 1 - c if k & 1 else c)


def _lin(d):
    return 4 * d[0] + 2 * d[1] + d[2]


def _all_gather(shards):
    n = len(shards)

    def body(*refs):
        ins, outs = refs[:n], refs[n:2 * n]
        send_sems, recv_sems, local_sems = refs[2 * n:]
        me = _my_coords()
        my = _lin(me)
        local = [pltpu.make_async_copy(ins[a], outs[a].at[my], local_sems.at[a]) for a in range(n)]
        for cp in local:
            cp.start()
        sends = []
        for a in range(n):
            for k in range(1, N_DEV):
                cp = pltpu.make_async_remote_copy(
                    src_ref=ins[a], dst_ref=outs[a].at[my], send_sem=send_sems.at[a * 7 + k - 1],
                    recv_sem=recv_sems.at[a * 7 + k - 1], device_id=_peer(me, k),
                    device_id_type=pl.DeviceIdType.MESH)
                cp.start()
                sends.append(cp)
        for a in range(n):
            for k in range(1, N_DEV):
                src = _lin(_peer(me, k))
                pltpu.make_async_remote_copy(
                    src_ref=ins[a], dst_ref=outs[a].at[src], send_sem=send_sems.at[a * 7 + k - 1],
                    recv_sem=recv_sems.at[a * 7 + k - 1], device_id=_peer(me, k),
                    device_id_type=pl.DeviceIdType.MESH).wait_recv()
        for cp in sends:
            cp.wait_send()
        for cp in local:
            cp.wait()

    any_spec = pl.BlockSpec(memory_space=pl.ANY)
    return pl.pallas_call(
        body, name="weight_all_gather",
        in_specs=[any_spec] * n, out_specs=[any_spec] * n,
        out_shape=[_sds((N_DEV,) + s.shape, s.dtype) for s in shards],
        scratch_shapes=[pltpu.SemaphoreType.DMA((7 * n,)), pltpu.SemaphoreType.DMA((7 * n,)),
                        pltpu.SemaphoreType.DMA((n,))],
    )(*shards)


def _grad_exchange(slots):
    n = len(slots)

    def body(*refs):
        ins, outs = refs[:n], refs[n:2 * n]
        send_sems, recv_sems, local_sems = refs[2 * n:]
        me = _my_coords()
        my = _lin(me)
        local = [pltpu.make_async_copy(ins[a].at[my], outs[a].at[my], local_sems.at[a]) for a in range(n)]
        for cp in local:
            cp.start()
        sends = []
        for a in range(n):
            for k in range(1, N_DEV):
                peer = _peer(me, k)
                cp = pltpu.make_async_remote_copy(
                    src_ref=ins[a].at[_lin(peer)], dst_ref=outs[a].at[my], send_sem=send_sems.at[a * 7 + k - 1],
                    recv_sem=recv_sems.at[a * 7 + k - 1], device_id=peer, device_id_type=pl.DeviceIdType.MESH)
                cp.start()
                sends.append(cp)
        for a in range(n):
            for k in range(1, N_DEV):
                peer = _peer(me, k)
                pltpu.make_async_remote_copy(
                    src_ref=ins[a].at[my], dst_ref=outs[a].at[_lin(peer)], send_sem=send_sems.at[a * 7 + k - 1],
                    recv_sem=recv_sems.at[a * 7 + k - 1], device_id=peer,
                    device_id_type=pl.DeviceIdType.MESH).wait_recv()
        for cp in sends:
            cp.wait_send()
        for cp in local:
            cp.wait()

    any_spec = pl.BlockSpec(memory_space=pl.ANY)
    return pl.pallas_call(
        body, name="grad_exchange",
        in_specs=[any_spec] * n, out_specs=[any_spec] * n,
        out_shape=[_sds(s.shape, s.dtype) for s in slots],
        scratch_shapes=[pltpu.SemaphoreType.DMA((7 * n,)), pltpu.SemaphoreType.DMA((7 * n,)),
                        pltpu.SemaphoreType.DMA((n,))],
    )(*slots)


def _small_all_reduce(v):
    R = v.shape[0]

    def body(v_ref, o_ref, buf, send_sems, recv_sems):
        me = _my_coords()
        my = _lin(me)
        sends = []
        for k in range(1, N_DEV):
            cp = pltpu.make_async_remote_copy(
                src_ref=v_ref, dst_ref=buf.at[my], send_sem=send_sems.at[k - 1], recv_sem=recv_sems.at[k - 1],
                device_id=_peer(me, k), device_id_type=pl.DeviceIdType.MESH)
            cp.start()
            sends.append(cp)
        buf[my] = v_ref[...]
        for k in range(1, N_DEV):
            pltpu.make_async_remote_copy(
                src_ref=v_ref, dst_ref=buf.at[_lin(_peer(me, k))], send_sem=send_sems.at[k - 1],
                recv_sem=recv_sems.at[k - 1], device_id=_peer(me, k),
                device_id_type=pl.DeviceIdType.MESH).wait_recv()
        for cp in sends:
            cp.wait_send()
        tot = buf[0]
        for d in range(1, N_DEV):
            tot = tot + buf[d]
        o_ref[...] = tot

    vm = pl.BlockSpec(memory_space=pltpu.VMEM)
    return pl.pallas_call(
        body, name="small_all_reduce", in_specs=[vm], out_specs=vm, out_shape=_sds(v.shape, F32),
        scratch_shapes=[pltpu.VMEM((N_DEV, R, LANES), F32), pltpu.SemaphoreType.DMA((7,)),
                        pltpu.SemaphoreType.DMA((7,))],
    )(v)


def _adamw_math(w, g, m, v):
    m = ADAM_B1 * m + (1.0 - ADAM_B1) * g
    v = ADAM_B2 * v + (1.0 - ADAM_B2) * (g * g)
    m_hat = m / (1.0 - ADAM_B1 ** ADAM_STEP)
    v_hat = v / (1.0 - ADAM_B2 ** ADAM_STEP)
    delta = -ADAM_LR * (m_hat / (jnp.sqrt(v_hat) + ADAM_EPS) + ADAM_WD * w)
    return delta, m, v


def _adamw(parts, w, m, v, name, tr):
    P, R, C = parts.shape
    tr = min(tr, R)

    def body(p_ref, w_ref, m_ref, v_ref, g_out, d_out, m_out, v_out):
        g = p_ref[0].astype(F32)
        for d in range(1, P):
            g = g + p_ref[d].astype(F32)
        delta, m_new, v_new = _adamw_math(w_ref[...], g, m_ref[...], v_ref[...])
        g_out[...] = g
        d_out[...] = delta
        m_out[...] = m_new
        v_out[...] = v_new

    tile = pl.BlockSpec((tr, C), lambda i: (i, 0))
    return pl.pallas_call(
        body, name=name, grid=(R // tr,),
        in_specs=[pl.BlockSpec((P, tr, C), lambda i: (0, i, 0)), tile, tile, tile],
        out_specs=[tile] * 4, out_shape=[_sds((R, C), F32)] * 4,
        compiler_params=_cp(("parallel",), 32))(parts, w, m, v)


SMALL = (("norm_g", D_MODEL), ("mla_q_a_norm", MLA_Q_LORA), ("mla_kv_a_norm", MLA_KV_LORA), ("mla_q_norm", MLA_QK),
         ("mla_k_norm", MLA_QK), ("swa_q_norm", SWA_HEAD_DIM), ("swa_k_norm", SWA_HEAD_DIM), ("swa_sinks", HEADS))
SMALL_GRAD_KEY = dict(norm_g="norm_g", mla_q_a_norm="qa", mla_kv_a_norm="kva", mla_q_norm="qn", mla_k_norm="kn",
                      swa_q_norm="sqn", swa_k_norm="skn", swa_sinks="sinks")
SMALL_ROWS = 32
CONV_ROWS = 24


def _pack_small(get):
    parts = []
    for l in range(DEPTH):
        for name, n in SMALL:
            v = get(name, l).reshape(-1)
            parts.append(jnp.pad(v, (0, (-n) % LANES)))
    return jnp.concatenate(parts).reshape(SMALL_ROWS, LANES)


def _unpack_small(packed):
    flat = packed.reshape(-1)
    out = {name: [] for name, _ in SMALL}
    off = 0
    for l in range(DEPTH):
        for name, n in SMALL:
            out[name].append(flat[off:off + n])
            off += n + (-n) % LANES
    return {name: jnp.stack(v) for name, v in out.items()}


def kernel(x, norm_g, w_in, mla_q_a_norm, mla_w_qb, mla_kv_a_norm, mla_w_kvb, mla_q_norm, mla_k_norm, conv_w, swa_q_norm, swa_k_norm, swa_sinks, w_out, loss_target, m_norm_g, m_w_in, m_mla_q_a_norm, m_mla_w_qb, m_mla_kv_a_norm, m_mla_w_kvb, m_mla_q_norm, m_mla_k_norm, m_conv_w, m_swa_q_norm, m_swa_k_norm, m_swa_sinks, m_w_out, v_norm_g, v_w_in, v_mla_q_a_norm, v_mla_w_qb, v_mla_kv_a_norm, v_mla_w_kvb, v_mla_q_norm, v_mla_k_norm, v_conv_w, v_swa_q_norm, v_swa_k_norm, v_swa_sinks, v_w_out):
    T = x.shape[1]
    weights = dict(norm_g=norm_g, w_in=w_in, mla_q_a_norm=mla_q_a_norm, mla_w_qb=mla_w_qb,
                   mla_kv_a_norm=mla_kv_a_norm, mla_w_kvb=mla_w_kvb, mla_q_norm=mla_q_norm, mla_k_norm=mla_k_norm,
                   conv_w=conv_w, swa_q_norm=swa_q_norm, swa_k_norm=swa_k_norm, swa_sinks=swa_sinks, w_out=w_out)
    mom_m = dict(norm_g=m_norm_g, w_in=m_w_in, mla_q_a_norm=m_mla_q_a_norm, mla_w_qb=m_mla_w_qb,
                 mla_kv_a_norm=m_mla_kv_a_norm, mla_w_kvb=m_mla_w_kvb, mla_q_norm=m_mla_q_norm,
                 mla_k_norm=m_mla_k_norm, conv_w=m_conv_w, swa_q_norm=m_swa_q_norm, swa_k_norm=m_swa_k_norm,
                 swa_sinks=m_swa_sinks, w_out=m_w_out)
    mom_v = dict(norm_g=v_norm_g, w_in=v_w_in, mla_q_a_norm=v_mla_q_a_norm, mla_w_qb=v_mla_w_qb,
                 mla_kv_a_norm=v_mla_kv_a_norm, mla_w_kvb=v_mla_w_kvb, mla_q_norm=v_mla_q_norm,
                 mla_k_norm=v_mla_k_norm, conv_w=v_conv_w, swa_q_norm=v_swa_q_norm, swa_k_norm=v_swa_k_norm,
                 swa_sinks=v_swa_sinks, w_out=v_w_out)

    g_win, g_wqb, g_wkvb, g_wout, g_conv = _all_gather([
        w_in.astype(MXU_DTYPE), mla_w_qb.astype(MXU_DTYPE), mla_w_kvb.astype(MXU_DTYPE), w_out.astype(MXU_DTYPE),
        conv_w])
    lws = []
    for l in range(DEPTH):
        w_in_full = jnp.transpose(g_win[:, l], (1, 0, 2)).reshape(D_MODEL, IN_COLS)
        conv_full = jnp.transpose(g_conv[:, l], (1, 0, 2)).reshape(3, GROUP_WIDTH)
        lws.append(_layer_weights(l, norm_g, w_in_full, mla_q_a_norm, g_wqb[:, l], mla_kv_a_norm, g_wkvb[:, l],
                                  mla_q_norm, mla_k_norm, conv_full, swa_q_norm, swa_k_norm, swa_sinks,
                                  g_wout[:, l].reshape(D_MIX, D_MODEL)))

    loss_tile, grad_x, grads = _local_step(x[0], loss_target[0], lws, _rope_tables(T))

    big_slots = [
        jnp.stack([jnp.transpose(g["w_in"].reshape(D_MODEL, N_DEV, IN_COLS // N_DEV), (1, 0, 2)) for g in grads], 1),
        jnp.stack([g["w_out"].reshape(N_DEV, D_MIX // N_DEV, D_MODEL) for g in grads], 1),
        jnp.stack([g["w_qb"] for g in grads], 1),
        jnp.stack([g["w_kvb"] for g in grads], 1),
    ]
    r_win, r_wout, r_wqb, r_wkvb = _grad_exchange(big_slots)

    small = jnp.concatenate([
        _pack_small(lambda name, l: grads[l][SMALL_GRAD_KEY[name]]),
        jnp.stack([g["conv"] for g in grads]).reshape(CONV_ROWS, LANES),
        loss_tile], axis=0)
    small = _small_all_reduce(small)
    loss = small[SMALL_ROWS + CONV_ROWS, 0]
    my = _lin(_my_coords())
    conv_g = lax.dynamic_slice_in_dim(small[SMALL_ROWS:SMALL_ROWS + CONV_ROWS].reshape(DEPTH, 3, GROUP_WIDTH),
                                      my * 64, 64, axis=2)

    out = {}

    def big(name, recv, rows, cols, tr):
        res = _adamw(recv.reshape(N_DEV, rows, cols), weights[name].reshape(rows, cols),
                     mom_m[name].reshape(rows, cols), mom_v[name].reshape(rows, cols), "adamw_" + name, tr)
        out[name] = [r.reshape(weights[name].shape) for r in res]

    big("w_in", r_win, DEPTH * D_MODEL, IN_COLS // N_DEV, 256)
    big("w_out", r_wout, DEPTH * D_MIX // N_DEV, D_MODEL, 192)
    big("mla_w_qb", r_wqb, DEPTH * MLA_Q_LORA, MLA_QK, 512)
    big("mla_w_kvb", r_wkvb, DEPTH * MLA_KV_LORA, 128, 256)

    pad_conv = lambda a: jnp.pad(a.reshape(-1), (0, 8 * LANES - 6 * 64)).reshape(8, LANES)
    cat = lambda src: jnp.concatenate([_pack_small(lambda name, l: src[name][l]), pad_conv(src["conv_w"])], axis=0)
    g_small = jnp.concatenate([small[:SMALL_ROWS], pad_conv(conv_g)], axis=0)
    res = _adamw(g_small[None], cat(weights), cat(mom_m), cat(mom_v), "adamw_small", SMALL_ROWS + 8)
    smalls = [_unpack_small(r[:SMALL_ROWS]) for r in res]
    for name, _ in SMALL:
        out[name] = [s[name] for s in smalls]
    out["conv_w"] = [r[SMALL_ROWS:].reshape(-1)[:6 * 64].reshape(DEPTH, 3, 64) for r in res]

    order = ["norm_g", "w_in", "mla_q_a_norm", "mla_w_qb", "mla_kv_a_norm", "mla_w_kvb", "mla_q_norm", "mla_k_norm",
             "conv_w", "swa_q_norm", "swa_k_norm", "swa_sinks", "w_out"]
    result = [loss, grad_x[None]]
    for idx in range(4):
        result += [out[name][idx] for name in order]
    return tuple(result)
```

```python
import functools

import jax
import jax.numpy as jnp
import numpy as np
from jax import lax
from jax.experimental import pallas as pl
from jax.experimental.pallas import tpu as pltpu

F32 = jnp.float32
MXU_DTYPE = jnp.bfloat16
WIRE_DTYPE = jnp.bfloat16

N_DEV = 8
DEPTH = 2
D_MODEL = 1024
GROUP_WIDTH = 512
D_MIX = 3 * GROUP_WIDTH
BLOCK = 128
RMS_EPS = 1e-6
NEG_INF = -1e30
HEADS = 8
MLA_QK = 96
MLA_NOPE = 64
MLA_ROPE = 32
MLA_Q_LORA = 256
MLA_KV_LORA = 128
ROPE_THETA = 10000.0
SWA_HEAD_DIM = 64
LANES = 128
IN_COLS = 4256

ADAM_LR = 0.001
ADAM_B1 = 0.9
ADAM_B2 = 0.999
ADAM_EPS = 1e-08
ADAM_WD = 0.01
ADAM_STEP = 10

NP = 4352
CB_QLAT = 0
CB_KVLAT = 2
CB_KROPE = 3
CB_GMLA, CB_CH, CB_CB, CB_CC, CB_GCONV, CB_SQ, CB_GSWA = 1, 2, 3, 4, 5, 6, 7
CB_SK, CB_SV = 32, 33

TM_PROJ = 256
TM_ROW = 256
TQ = 512
TM_SWA = 256
VMEM_MB = 2 ** 20


def _cp(sem, vmem_mb):
    return pltpu.CompilerParams(dimension_semantics=sem, vmem_limit_bytes=vmem_mb * VMEM_MB)


def _sds(shape, dtype):
    return jax.ShapeDtypeStruct(shape, dtype)


def _dot(a, b):
    return jnp.dot(a, b, preferred_element_type=F32)


def _dot_nt(a, b):
    return lax.dot_general(a, b, (((1,), (1,)), ((), ())), preferred_element_type=F32)


def _dot_tn(a, b):
    return lax.dot_general(a, b, (((0,), (0,)), ((), ())), preferred_element_type=F32)


def _rms(x, n):
    r = lax.rsqrt(jnp.sum(x * x, axis=-1, keepdims=True) * (1.0 / n) + RMS_EPS)
    return x * r, r


def _rms_bwd(dy, xhat, r, w, n):
    g = dy * w
    return r * (g - xhat * (jnp.sum(g * xhat, axis=-1, keepdims=True) * (1.0 / n)))


def _rms_halves(x, half1):
    x2 = x * x
    s0 = jnp.sum(jnp.where(half1, 0.0, x2), axis=-1, keepdims=True)
    s1 = jnp.sum(jnp.where(half1, x2, 0.0), axis=-1, keepdims=True)
    r = jnp.where(half1, lax.rsqrt(s1 * (1.0 / 64) + RMS_EPS), lax.rsqrt(s0 * (1.0 / 64) + RMS_EPS))
    return x * r, r


def _rms_halves_bwd(dy, xhat, r, w, half1):
    g = dy * w
    t = g * xhat
    m0 = jnp.sum(jnp.where(half1, 0.0, t), axis=-1, keepdims=True) * (1.0 / 64)
    m1 = jnp.sum(jnp.where(half1, t, 0.0), axis=-1, keepdims=True) * (1.0 / 64)
    return r * (g - xhat * jnp.where(half1, m1, m0))


def _sigmoid(x):
    return 1.0 / (1.0 + jnp.exp(-x))


def _rope(x, c, s1, s2):
    return x * c + pltpu.roll(x, 112, 1) * s1 + pltpu.roll(x, 16, 1) * s2


def _rope_bwd(dy, c, s1, s2):
    return dy * c + pltpu.roll(dy * s1, 16, 1) + pltpu.roll(dy * s2, 112, 1)


def _fold_rows8(x):
    return jnp.sum(x.reshape(x.shape[0] // 8, 8, x.shape[1]), axis=0)


def _row0(v, rows=8):
    row = lax.broadcasted_iota(jnp.int32, (rows, v.shape[1]), 0)
    return jnp.where(row == 0, jnp.broadcast_to(v, (rows, v.shape[1])), 0.0)


def _mm_nn(a, b, name, out_dtype=F32, residual=None, tm=TM_PROJ):
    M, K = a.shape
    N = b.shape[1]
    tm = min(tm, M)

    def body(*refs):
        if residual is None:
            a_ref, b_ref, o_ref = refs
            acc = _dot(a_ref[...].astype(MXU_DTYPE), b_ref[...])
        else:
            a_ref, b_ref, r_ref, o_ref = refs
            acc = _dot(a_ref[...].astype(MXU_DTYPE), b_ref[...]) + r_ref[...]
        o_ref[...] = acc.astype(out_dtype)

    in_specs = [pl.BlockSpec((tm, K), lambda i: (i, 0)), pl.BlockSpec((K, N), lambda i: (0, 0))]
    args = [a, b]
    if residual is not None:
        in_specs.append(pl.BlockSpec((tm, N), lambda i: (i, 0)))
        args.append(residual)
    return pl.pallas_call(
        body, name=name, grid=(M // tm,), in_specs=in_specs,
        out_specs=pl.BlockSpec((tm, N), lambda i: (i, 0)), out_shape=_sds((M, N), out_dtype),
        compiler_params=_cp(("parallel",), 48))(*args)


def _mm_tn(a, b, name, out_dtype, tn, tk=512):
    T, M = a.shape
    N = b.shape[1]
    tk = min(tk, T)
    nk = T // tk

    def body(a_ref, b_ref, o_ref, acc_ref):
        k = pl.program_id(1)

        @pl.when(k == 0)
        def _():
            acc_ref[...] = jnp.zeros_like(acc_ref)

        acc_ref[...] += _dot_tn(a_ref[...].astype(MXU_DTYPE), b_ref[...].astype(MXU_DTYPE))

        @pl.when(k == nk - 1)
        def _():
            o_ref[...] = acc_ref[...].astype(out_dtype)

    return pl.pallas_call(
        body, name=name, grid=(N // tn, nk),
        in_specs=[pl.BlockSpec((tk, M), lambda n, k: (k, 0)), pl.BlockSpec((tk, tn), lambda n, k: (k, n))],
        out_specs=pl.BlockSpec((M, tn), lambda n, k: (0, n)), out_shape=_sds((M, N), out_dtype),
        scratch_shapes=[pltpu.VMEM((M, tn), F32)],
        compiler_params=_cp(("parallel", "arbitrary"), 48))(a, b)


def _inproj_fwd(x, ng, wp):
    T, D = x.shape
    tm = min(TM_PROJ, T)

    def body(x_ref, g_ref, w_ref, proj_ref, h_ref):
        xhat, _ = _rms(x_ref[...], D)
        h = (xhat * g_ref[...]).astype(MXU_DTYPE)
        h_ref[...] = h
        proj_ref[...] = _dot(h, w_ref[...])

    return pl.pallas_call(
        body, name="inproj_fwd", grid=(T // tm,),
        in_specs=[pl.BlockSpec((tm, D), lambda i: (i, 0)), pl.BlockSpec((1, D), lambda i: (0, 0)),
                  pl.BlockSpec((D, NP), lambda i: (0, 0))],
        out_specs=[pl.BlockSpec((tm, NP), lambda i: (i, 0)), pl.BlockSpec((tm, D), lambda i: (i, 0))],
        out_shape=[_sds((T, NP), F32), _sds((T, D), MXU_DTYPE)],
        compiler_params=_cp(("parallel",), 48))(x, ng, wp)


def _mla_prep_fwd(proj, lw, rope):
    T = proj.shape[0]
    tm = min(TQ, T)

    def body(ql_ref, kvl_ref, kr_ref, qa_ref, kva_ref, wq_ref, wk_ref, wv_ref, qn_ref, kn_ref,
             c_ref, s1_ref, s2_ref, q_out, k_out, kt_out, v_out, vt_out):
        c, s1, s2 = c_ref[...], s1_ref[...], s2_ref[...]
        qhat, _ = _rms(ql_ref[...], MLA_Q_LORA)
        qn = (qhat * qa_ref[...]).astype(MXU_DTYPE)
        khat, _ = _rms(kvl_ref[...], MLA_KV_LORA)
        kvn = (khat * kva_ref[...]).astype(MXU_DTYPE)
        kr = kr_ref[...]
        half1 = lax.broadcasted_iota(jnp.int32, (tm, LANES), 1) >= 64
        for h in range(HEADS):
            xh, _ = _rms(_dot(qn, wq_ref[h]), MLA_QK)
            q_out[h] = _rope(xh * qn_ref[...], c, s1, s2).astype(MXU_DTYPE)
            xh, _ = _rms(_dot(kvn, wk_ref[h]) + kr, MLA_QK)
            kh = _rope(xh * kn_ref[...], c, s1, s2)
            k_out[h] = kh.astype(MXU_DTYPE)
            kt_out[h, 0] = kh.T.astype(MXU_DTYPE)
        v = _dot(kvn, wv_ref[...])
        for h in range(HEADS):
            vp = v[:, LANES * (h // 2):LANES * (h // 2 + 1)]
            own = half1 if h % 2 else jnp.logical_not(half1)
            vp = jnp.where(own, vp, 0.0)
            v_out[h] = vp.astype(MXU_DTYPE)
            vt_out[h, 0] = vp.T.astype(MXU_DTYPE)

    full = lambda shape: pl.BlockSpec(shape, lambda i: (0,) * len(shape))
    hd = pl.BlockSpec((HEADS, tm, LANES), lambda i: (0, i, 0))
    hdt = pl.BlockSpec((HEADS, 1, LANES, tm), lambda i: (0, i, 0, 0))
    nat = _sds((HEADS, T, LANES), MXU_DTYPE)
    tr = _sds((HEADS, T // tm, LANES, tm), MXU_DTYPE)
    return pl.pallas_call(
        body, name="mla_prep_fwd", grid=(T // tm,),
        in_specs=[pl.BlockSpec((tm, 256), lambda i: (i, CB_QLAT)), pl.BlockSpec((tm, LANES), lambda i: (i, CB_KVLAT)),
                  pl.BlockSpec((tm, LANES), lambda i: (i, CB_KROPE)),
                  full((1, 256)), full((1, LANES)), full((HEADS, 256, LANES)), full((HEADS, LANES, LANES)),
                  full((LANES, 512)), full((1, LANES)), full((1, LANES)),
                  pl.BlockSpec((tm, LANES), lambda i: (i, 0)), pl.BlockSpec((tm, LANES), lambda i: (i, 0)),
                  pl.BlockSpec((tm, LANES), lambda i: (i, 0))],
        out_specs=[hd, hd, hdt, hd, hdt],
        out_shape=[nat, nat, tr, nat, tr],
        compiler_params=_cp(("parallel",), 32))(
            proj, proj, proj, lw["qa"], lw["kva"], lw["wq"], lw["wk"], lw["wv"], lw["qn"], lw["kn"],
            rope[0], rope[1], rope[2])


def _mla_attn_fwd(q, k, vt):
    T = q.shape[1]
    tq = min(TQ, T)
    scale = MLA_QK ** -0.5

    def body(q_ref, k_ref, vt_ref, o_ref, lse_ref, acc_s):
        i = pl.program_id(1)
        key = lax.broadcasted_iota(jnp.int32, (tq, tq), 0)
        qry = lax.broadcasted_iota(jnp.int32, (tq, tq), 1)
        qs = [q_ref[0], q_ref[1]]
        acc_s[...] = jnp.zeros_like(acc_s)

        def step(kj, carry, masked):
            rows = pl.ds(pl.multiple_of(kj * tq, tq), tq)
            new = []
            for r in range(2):
                m_old, l_old = carry[r]
                s = _dot_nt(k_ref[r, rows, :], qs[r]) * scale
                if masked:
                    s = jnp.where(key <= qry, s, NEG_INF)
                m_new = jnp.maximum(m_old, jnp.max(s, axis=0, keepdims=True))
                alpha = jnp.exp(m_old - m_new)
                p = jnp.exp(s - m_new)
                l_new = alpha * l_old + jnp.sum(p, axis=0, keepdims=True)
                acc_s[r] = alpha * acc_s[r] + _dot(vt_ref[r, kj], p.astype(MXU_DTYPE))
                new.append((m_new, l_new))
            return tuple(new)

        init = tuple((jnp.full((1, tq), NEG_INF, F32), jnp.zeros((1, tq), F32)) for _ in range(2))
        carry = lax.fori_loop(0, i, lambda kj, c: step(kj, c, False), init)
        carry = step(i, carry, True)
        o_t = acc_s[0] / carry[0][1] + acc_s[1] / carry[1][1]
        o_ref[...] = o_t.T
        for r in range(2):
            lse_ref[r] = carry[r][0] + jnp.log(carry[r][1])

    nt = T // tq
    return pl.pallas_call(
        body, name="mla_attn_fwd", grid=(HEADS // 2, nt),
        in_specs=[pl.BlockSpec((2, tq, LANES), lambda j, i: (j, i, 0)),
                  pl.BlockSpec((2, T, LANES), lambda j, i: (j, 0, 0)),
                  pl.BlockSpec((2, nt, LANES, tq), lambda j, i: (j, 0, 0, 0))],
        out_specs=[pl.BlockSpec((tq, LANES), lambda j, i: (i, j)),
                   pl.BlockSpec((2, 1, tq), lambda j, i: (j, 0, i))],
        out_shape=[_sds((T, GROUP_WIDTH), F32), _sds((HEADS, 1, T), F32)],
        scratch_shapes=[pltpu.VMEM((2, LANES, tq), F32)],
        compiler_params=_cp(("parallel", "arbitrary"), 40))(q, k, vt)


def _swa_masks(nb_first):
    qi = lax.broadcasted_iota(jnp.int32, (BLOCK, 2 * BLOCK), 0)
    ki = lax.broadcasted_iota(jnp.int32, (BLOCK, 2 * BLOCK), 1)
    dist = BLOCK + qi - ki
    valid = (dist >= 0) & (dist < BLOCK) & ((ki >= BLOCK) | jnp.logical_not(nb_first))
    return dist.astype(F32), valid


def _swa_kv_variants(x, half1):
    xs = pltpu.roll(x, 64, 1)
    out = {}
    for g in range(2):
        for r in range(2):
            own = half1 if r else jnp.logical_not(half1)
            out[(g, r)] = jnp.where(own, x if g == r else xs, 0.0).astype(MXU_DTYPE)
    return out


def _swa_fwd(proj, lw):
    T = proj.shape[0]
    tm = min(TM_SWA, T)
    nb = tm // BLOCK
    scale = SWA_HEAD_DIM ** -0.5

    def body(q_ref, k_ref, v_ref, pk_ref, pv_ref, qw_ref, kw_ref, sink_ref, o_ref):
        i = pl.program_id(0)
        half1 = lax.broadcasted_iota(jnp.int32, (1, LANES), 1) >= 64
        k_all = jnp.concatenate([pk_ref[...], k_ref[...]], axis=0)
        v_all = jnp.concatenate([pv_ref[...], v_ref[...]], axis=0)
        khat, _ = _rms_halves(k_all, half1)
        kp = _swa_kv_variants(khat * kw_ref[...], half1)
        vp = _swa_kv_variants(v_all, half1)
        qn = []
        for j in range(4):
            qhat, _ = _rms_halves(q_ref[:, LANES * j:LANES * (j + 1)], half1)
            qn.append((qhat * qw_ref[...]).astype(MXU_DTYPE))
        for b in range(nb):
            dist, valid = _swa_masks((i == 0) & (b == 0))
            ks = slice(b * BLOCK, b * BLOCK + 2 * BLOCK)
            for j in range(4):
                g = j // 2
                qb = qn[j][b * BLOCK:(b + 1) * BLOCK]
                o = jnp.zeros((BLOCK, LANES), F32)
                for r in range(2):
                    h = 2 * j + r
                    s = _dot_nt(qb, kp[(g, r)][ks]) * scale - (2.0 ** -(h + 1)) * dist
                    s = jnp.where(valid, s, NEG_INF)
                    sink = sink_ref[h]
                    m = jnp.maximum(jnp.max(s, axis=-1, keepdims=True), sink)
                    e = jnp.exp(s - m)
                    den = jnp.sum(e, axis=-1, keepdims=True) + jnp.exp(sink - m)
                    o = o + _dot((e / den).astype(MXU_DTYPE), vp[(g, r)][ks])
                o_ref[b * BLOCK:(b + 1) * BLOCK, LANES * j:LANES * (j + 1)] = o

    prev = lambda cb: pl.BlockSpec((BLOCK, LANES), lambda i: (jnp.maximum(i * nb - 1, 0), cb))
    return pl.pallas_call(
        body, name="swa_fwd", grid=(T // tm,),
        in_specs=[pl.BlockSpec((tm, 512), lambda i: (i, CB_SQ)), pl.BlockSpec((tm, LANES), lambda i: (i, CB_SK)),
                  pl.BlockSpec((tm, LANES), lambda i: (i, CB_SV)), prev(CB_SK), prev(CB_SV),
                  pl.BlockSpec((1, LANES), lambda i: (0, 0)), pl.BlockSpec((1, LANES), lambda i: (0, 0)),
                  pl.BlockSpec(memory_space=pltpu.SMEM)],
        out_specs=pl.BlockSpec((tm, 512), lambda i: (i, 0)),
        out_shape=_sds((T, GROUP_WIDTH), F32),
        compiler_params=_cp(("parallel",), 32))(proj, proj, proj, proj, proj, lw["sqn"], lw["skn"], lw["sinks"])


def _shift_down(u, prev, n, row):
    tm = u.shape[0]
    out = pltpu.roll(u, n, 0)
    row8 = lax.broadcasted_iota(jnp.int32, prev.shape, 0)
    for t in range(n):
        src = jnp.sum(jnp.where(row8 == 8 - n + t, prev, 0.0), axis=0, keepdims=True)
        out = jnp.where(row == t, src, out)
    return out


def _shift_up(u, nxt, n, row):
    tm = u.shape[0]
    out = pltpu.roll(u, tm - n, 0)
    row8 = lax.broadcasted_iota(jnp.int32, nxt.shape, 0)
    for t in range(n):
        src = jnp.sum(jnp.where(row8 == t, nxt, 0.0), axis=0, keepdims=True)
        out = jnp.where(row == tm - n + t, src, out)
    return out


def _mix_fwd(proj, o_mla, o_swa, conv_w):
    T = proj.shape[0]
    tm = min(TM_ROW, T)

    def body(gm_ref, ch_ref, cb_ref, cc_ref, gc_ref, gs_ref, pch_ref, pcc_ref, om_ref, os_ref, w_ref, y_ref):
        i = pl.program_id(0)
        row = lax.broadcasted_iota(jnp.int32, (tm, GROUP_WIDTH), 0)
        u = cc_ref[...] * ch_ref[...]
        u_prev = jnp.where(i > 0, pcc_ref[...] * pch_ref[...], 0.0)
        z = (w_ref[0:1, :] * _shift_down(u, u_prev, 2, row) + w_ref[1:2, :] * _shift_down(u, u_prev, 1, row)
             + w_ref[2:3, :] * u)
        gm, gc, gs = gm_ref[...], gc_ref[...], gs_ref[...]
        y_ref[:, 0:512] = (om_ref[...] * (gm * _sigmoid(gm))).astype(MXU_DTYPE)
        y_ref[:, 512:1024] = (cb_ref[...] * z * (gc * _sigmoid(gc))).astype(MXU_DTYPE)
        y_ref[:, 1024:1536] = (os_ref[...] * (gs * _sigmoid(gs))).astype(MXU_DTYPE)

    blk = lambda cb: pl.BlockSpec((tm, 512), lambda i: (i, cb))
    prev = lambda cb: pl.BlockSpec((8, 512), lambda i: (jnp.maximum(i * (tm // 8) - 1, 0), cb))
    tile = pl.BlockSpec((tm, 512), lambda i: (i, 0))
    return pl.pallas_call(
        body, name="mix_fwd", grid=(T // tm,),
        in_specs=[blk(CB_GMLA), blk(CB_CH), blk(CB_CB), blk(CB_CC), blk(CB_GCONV), blk(CB_GSWA),
                  prev(CB_CH), prev(CB_CC), tile, tile, pl.BlockSpec((8, 512), lambda i: (0, 0))],
        out_specs=pl.BlockSpec((tm, D_MIX), lambda i: (i, 0)),
        out_shape=_sds((T, D_MIX), MXU_DTYPE),
        compiler_params=_cp(("parallel",), 32))(
            proj, proj, proj, proj, proj, proj, proj, proj, o_mla, o_swa, conv_w)


def _loss_grad(y, target):
    T, D = y.shape
    tm = min(TM_ROW, T)
    nt = T // tm

    def body(y_ref, t_ref, g_ref, loss_ref, acc_ref):
        i = pl.program_id(0)

        @pl.when(i == 0)
        def _():
            acc_ref[...] = jnp.zeros_like(acc_ref)

        err = y_ref[...] - t_ref[...]
        g_ref[...] = err * (1.0 / D)
        acc_ref[...] += _fold_rows8(err * err)

        @pl.when(i == nt - 1)
        def _():
            tot = jnp.sum(jnp.sum(acc_ref[...], axis=1, keepdims=True), axis=0, keepdims=True)
            loss_ref[...] = jnp.broadcast_to(tot * (0.5 / D), (8, LANES))

    return pl.pallas_call(
        body, name="loss_grad", grid=(nt,),
        in_specs=[pl.BlockSpec((tm, D), lambda i: (i, 0)), pl.BlockSpec((tm, D), lambda i: (i, 0))],
        out_specs=[pl.BlockSpec((tm, D), lambda i: (i, 0)), pl.BlockSpec((8, LANES), lambda i: (0, 0))],
        out_shape=[_sds((T, D), F32), _sds((8, LANES), F32)],
        scratch_shapes=[pltpu.VMEM((8, D), F32)],
        compiler_params=_cp(("arbitrary",), 32))(y, target)


def _mix_bwd(dycat, proj, o_mla, o_swa, conv_w):
    T = proj.shape[0]
    tm = min(TM_ROW, T)
    nt = T // tm

    def body(dym_ref, dyc_ref, dys_ref, gm_ref, ch_ref, cb_ref, cc_ref, gc_ref, gs_ref, pch_ref, pcc_ref,
             ndy_ref, ncb_ref, ngc_ref, om_ref, os_ref, w_ref,
             d1_ref, dgs_ref, dom_ref, dos_ref, dw_ref):
        i = pl.program_id(0)

        @pl.when(i == 0)
        def _():
            dw_ref[...] = jnp.zeros_like(dw_ref)

        row = lax.broadcasted_iota(jnp.int32, (tm, GROUP_WIDTH), 0)

        def gate(g):
            sg = _sigmoid(g)
            return g * sg, sg * (1.0 + g * (1.0 - sg))

        gm = gm_ref[...]
        silu, dsilu = gate(gm)
        dym = dym_ref[...]
        dom_ref[...] = dym * silu
        d1_ref[:, 0:512] = (dym * om_ref[...] * dsilu).astype(MXU_DTYPE)

        gs = gs_ref[...]
        silu, dsilu = gate(gs)
        dys = dys_ref[...]
        dos_ref[...] = dys * silu
        dgs_ref[...] = (dys * os_ref[...] * dsilu).astype(MXU_DTYPE)

        ch, cb, cc, gc, dyc = ch_ref[...], cb_ref[...], cc_ref[...], gc_ref[...], dyc_ref[...]
        w0, w1, w2 = w_ref[0:1, :], w_ref[1:2, :], w_ref[2:3, :]
        u = cc * ch
        u_prev = jnp.where(i > 0, pcc_ref[...] * pch_ref[...], 0.0)
        u1 = _shift_down(u, u_prev, 1, row)
        u2 = _shift_down(u, u_prev, 2, row)
        z = w0 * u2 + w1 * u1 + w2 * u
        silu, dsilu = gate(gc)
        dz = dyc * cb * silu
        ngc = ngc_ref[...]
        dz_next = jnp.where(i < nt - 1, ndy_ref[...] * ncb_ref[...] * (ngc * _sigmoid(ngc)), 0.0)
        du = w2 * dz + w1 * _shift_up(dz, dz_next, 1, row) + w0 * _shift_up(dz, dz_next, 2, row)
        d1_ref[:, 512:1024] = (du * cc).astype(MXU_DTYPE)
        d1_ref[:, 1024:1536] = (dyc * z * silu).astype(MXU_DTYPE)
        d1_ref[:, 1536:2048] = (du * ch).astype(MXU_DTYPE)
        d1_ref[:, 2048:2560] = (dyc * cb * z * dsilu).astype(MXU_DTYPE)
        row8 = lax.broadcasted_iota(jnp.int32, (8, GROUP_WIDTH), 0)
        dw = jnp.zeros((8, GROUP_WIDTH), F32)
        for t, shifted in enumerate((u2, u1, u)):
            dw = dw + jnp.where(row8 == t, jnp.sum(dz * shifted, axis=0, keepdims=True), 0.0)
        dw_ref[...] += dw

    blk = lambda cb: pl.BlockSpec((tm, 512), lambda i: (i, cb))
    prev = lambda cb: pl.BlockSpec((8, 512), lambda i: (jnp.maximum(i * (tm // 8) - 1, 0), cb))
    nxt = lambda cb: pl.BlockSpec((8, 512), lambda i: (jnp.minimum((i + 1) * (tm // 8), T // 8 - 1), cb))
    tile = pl.BlockSpec((tm, 512), lambda i: (i, 0))
    return pl.pallas_call(
        body, name="mix_bwd", grid=(nt,),
        in_specs=[blk(0), blk(1), blk(2), blk(CB_GMLA), blk(CB_CH), blk(CB_CB), blk(CB_CC), blk(CB_GCONV),
                  blk(CB_GSWA), prev(CB_CH), prev(CB_CC), nxt(1), nxt(CB_CB), nxt(CB_GCONV), tile, tile,
                  pl.BlockSpec((8, 512), lambda i: (0, 0))],
        out_specs=[pl.BlockSpec((tm, 2560), lambda i: (i, 0)), tile, tile, tile,
                   pl.BlockSpec((8, 512), lambda i: (0, 0))],
        out_shape=[_sds((T, 2560), MXU_DTYPE), _sds((T, 512), MXU_DTYPE), _sds((T, 512), F32),
                   _sds((T, 512), F32), _sds((8, 512), F32)],
        compiler_params=_cp(("arbitrary",), 48))(
            dycat, dycat, dycat, proj, proj, proj, proj, proj, proj, proj, proj, dycat, proj, proj,
            o_mla, o_swa, conv_w)


def _swa_bwd(proj, o_swa, do_swa, lw):
    T = proj.shape[0]
    tm = min(TM_SWA, T)
    nb = tm // BLOCK
    scale = SWA_HEAD_DIM ** -0.5

    def body(q_ref, k_ref, v_ref, pk_ref, pv_ref, o_ref, do_ref, qw_ref, kw_ref, sink_ref,
             dq_ref, dk_ref, dv_ref, dqw_ref, dsink_ref):
        i = pl.program_id(0)

        @pl.when(i == 0)
        def _():
            dk_ref[...] = jnp.zeros_like(dk_ref)
            dv_ref[...] = jnp.zeros_like(dv_ref)
            dqw_ref[...] = jnp.zeros_like(dqw_ref)
            dsink_ref[...] = jnp.zeros_like(dsink_ref)

        half1 = lax.broadcasted_iota(jnp.int32, (1, LANES), 1) >= 64
        k_all = jnp.concatenate([pk_ref[...], k_ref[...]], axis=0)
        v_all = jnp.concatenate([pv_ref[...], v_ref[...]], axis=0)
        khat, _ = _rms_halves(k_all, half1)
        kp = _swa_kv_variants(khat * kw_ref[...], half1)
        vp = _swa_kv_variants(v_all, half1)
        qw = qw_ref[...]
        dqw = jnp.zeros((1, LANES), F32)
        dsink_rows = [jnp.zeros((1, 1), F32) for _ in range(HEADS)]
        for j in range(4):
            g = j // 2
            cols = slice(LANES * j, LANES * (j + 1))
            qhat, qr = _rms_halves(q_ref[:, cols], half1)
            qn = (qhat * qw).astype(MXU_DTYPE)
            do = do_ref[:, cols]
            dob = do.astype(MXU_DTYPE)
            prod = do * o_ref[:, cols]
            dqn_blocks = []
            for b in range(nb):
                dist, valid = _swa_masks((i == 0) & (b == 0))
                ks = slice(b * BLOCK, b * BLOCK + 2 * BLOCK)
                rs = slice(b * BLOCK, (b + 1) * BLOCK)
                qb = qn[rs]
                dqn = jnp.zeros((BLOCK, LANES), F32)
                for r in range(2):
                    h = 2 * j + r
                    own = half1 if r else jnp.logical_not(half1)
                    s = _dot_nt(qb, kp[(g, r)][ks]) * scale - (2.0 ** -(h + 1)) * dist
                    s = jnp.where(valid, s, NEG_INF)
                    sink = sink_ref[h]
                    m = jnp.maximum(jnp.max(s, axis=-1, keepdims=True), sink)
                    e = jnp.exp(s - m)
                    es = jnp.exp(sink - m)
                    inv = 1.0 / (jnp.sum(e, axis=-1, keepdims=True) + es)
                    p = e * inv
                    dd = jnp.sum(jnp.where(own, prod[rs], 0.0), axis=-1, keepdims=True)
                    dp = _dot_nt(dob[rs], vp[(g, r)][ks])
                    ds = (p * (dp - dd) * scale).astype(MXU_DTYPE)
                    dsink_rows[h] = dsink_rows[h] - jnp.sum(es * inv * dd, axis=0, keepdims=True)
                    dqn = dqn + _dot(ds, kp[(g, r)][ks])
                    dkp = jnp.where(own, _dot_tn(ds, qb), 0.0)
                    dvp = jnp.where(own, _dot_tn(p.astype(MXU_DTYPE), dob[rs]), 0.0)
                    if g != r:
                        dkp = pltpu.roll(dkp, 64, 1)
                        dvp = pltpu.roll(dvp, 64, 1)
                    dst = pl.ds(pl.multiple_of((i * nb + b) * BLOCK, BLOCK), 2 * BLOCK)
                    dk_ref[dst, :] += dkp
                    dv_ref[dst, :] += dvp
                dqn_blocks.append(dqn)
            dqn = jnp.concatenate(dqn_blocks, axis=0) if nb > 1 else dqn_blocks[0]
            dqw = dqw + jnp.sum(dqn * qhat, axis=0, keepdims=True)
            dq_ref[:, cols] = _rms_halves_bwd(dqn, qhat, qr, qw, half1).astype(MXU_DTYPE)
        dqw_ref[...] += _row0(dqw + pltpu.roll(dqw, 64, 1))
        row8 = lax.broadcasted_iota(jnp.int32, (8, LANES), 0)
        dsink = jnp.zeros((8, LANES), F32)
        for h in range(HEADS):
            dsink = dsink + jnp.where(row8 == h, jnp.broadcast_to(dsink_rows[h], (8, LANES)), 0.0)
        dsink_ref[...] += dsink

    prev = lambda cb: pl.BlockSpec((BLOCK, LANES), lambda i: (jnp.maximum(i * nb - 1, 0), cb))
    tile = pl.BlockSpec((tm, 512), lambda i: (i, 0))
    small = pl.BlockSpec((8, LANES), lambda i: (0, 0))
    acc = pl.BlockSpec((T + BLOCK, LANES), lambda i: (0, 0))
    return pl.pallas_call(
        body, name="swa_bwd", grid=(T // tm,),
        in_specs=[pl.BlockSpec((tm, 512), lambda i: (i, CB_SQ)), pl.BlockSpec((tm, LANES), lambda i: (i, CB_SK)),
                  pl.BlockSpec((tm, LANES), lambda i: (i, CB_SV)), prev(CB_SK), prev(CB_SV), tile, tile,
                  pl.BlockSpec((1, LANES), lambda i: (0, 0)), pl.BlockSpec((1, LANES), lambda i: (0, 0)),
                  pl.BlockSpec(memory_space=pltpu.SMEM)],
        out_specs=[tile, acc, acc, small, small],
        out_shape=[_sds((T, 512), MXU_DTYPE), _sds((T + BLOCK, LANES), F32), _sds((T + BLOCK, LANES), F32),
                   _sds((8, LANES), F32), _sds((8, LANES), F32)],
        compiler_params=_cp(("arbitrary",), 40))(
            proj, proj, proj, proj, proj, o_swa, do_swa, lw["sqn"], lw["skn"], lw["sinks"])


def _swa_kv_bwd(proj, dkn, dv, lw):
    T = proj.shape[0]
    tm = BLOCK

    def body(k_ref, dkn_ref, dv_ref, kw_ref, d_ref, dkw_ref):
        i = pl.program_id(0)

        @pl.when(i == 0)
        def _():
            dkw_ref[...] = jnp.zeros_like(dkw_ref)

        half1 = lax.broadcasted_iota(jnp.int32, (1, LANES), 1) >= 64
        khat, kr = _rms_halves(k_ref[...], half1)
        dkn_t = dkn_ref[...]
        dkw = jnp.sum(dkn_t * khat, axis=0, keepdims=True)
        dkw_ref[...] += _row0(dkw + pltpu.roll(dkw, 64, 1))
        d_ref[:, 0:LANES] = _rms_halves_bwd(dkn_t, khat, kr, kw_ref[...], half1).astype(MXU_DTYPE)
        d_ref[:, LANES:2 * LANES] = dv_ref[...].astype(MXU_DTYPE)

    return pl.pallas_call(
        body, name="swa_kv_bwd", grid=(T // tm,),
        in_specs=[pl.BlockSpec((tm, LANES), lambda i: (i, CB_SK)), pl.BlockSpec((tm, LANES), lambda i: (i + 1, 0)),
                  pl.BlockSpec((tm, LANES), lambda i: (i + 1, 0)), pl.BlockSpec((1, LANES), lambda i: (0, 0))],
        out_specs=[pl.BlockSpec((tm, 2 * LANES), lambda i: (i, 0)), pl.BlockSpec((8, LANES), lambda i: (0, 0))],
        out_shape=[_sds((T, 2 * LANES), MXU_DTYPE), _sds((8, LANES), F32)],
        compiler_params=_cp(("arbitrary",), 32))(proj, dkn, dv, lw["skn"])


def _mla_attn_bwd(q, k, kt, vpad, o, do, lse):
    T = q.shape[1]
    tq = min(TQ, T)
    scale = MLA_QK ** -0.5

    def body(q_ref, k_ref, kt_ref, v_ref, o_ref, do_ref, lse_ref, dq_ref, dk_ref, dv_ref, dqt_s):
        h = pl.program_id(0)
        i = pl.program_id(1)

        @pl.when(i == 0)
        def _():
            dk_ref[...] = jnp.zeros_like(dk_ref)
            dv_ref[...] = jnp.zeros_like(dv_ref)

        key = lax.broadcasted_iota(jnp.int32, (tq, tq), 0)
        qry = lax.broadcasted_iota(jnp.int32, (tq, tq), 1)
        own = (lax.broadcasted_iota(jnp.int32, (1, LANES), 1) // 64) == (h % 2)
        own_rows = (lax.broadcasted_iota(jnp.int32, (LANES, 1), 0) // 64) == (h % 2)
        do_t = do_ref[...]
        dob = do_t.astype(MXU_DTYPE)
        prod_t = (do_t * o_ref[...]).T
        dd = jnp.sum(jnp.where(own_rows, prod_t, 0.0), axis=0, keepdims=True)
        qh = q_ref[0]
        lse_t = lse_ref[0]
        dqt_s[...] = jnp.zeros_like(dqt_s)

        def step(kj, masked):
            rows = pl.ds(pl.multiple_of(kj * tq, tq), tq)
            k_nat = k_ref[0, rows, :]
            s = _dot_nt(k_nat, qh) * scale
            if masked:
                s = jnp.where(key <= qry, s, NEG_INF)
            p = jnp.exp(s - lse_t)
            dp = _dot_nt(v_ref[0, rows, :], dob)
            ds = (p * (dp - dd) * scale).astype(MXU_DTYPE)
            dqt_s[...] += _dot(kt_ref[0, kj], ds)
            dk_ref[0, rows, :] += _dot(ds, qh)
            dv_ref[0, rows, :] += jnp.where(own, _dot(p.astype(MXU_DTYPE), dob), 0.0)

        def loop_body(kj, carry):
            step(kj, False)
            return carry

        lax.fori_loop(0, i, loop_body, 0)
        step(i, True)
        dq_ref[0] = dqt_s[...].T

    nt = T // tq
    res = pl.BlockSpec((1, T, LANES), lambda h, i: (h, 0, 0))
    return pl.pallas_call(
        body, name="mla_attn_bwd", grid=(HEADS, nt),
        in_specs=[pl.BlockSpec((1, tq, LANES), lambda h, i: (h, i, 0)), res,
                  pl.BlockSpec((1, nt, LANES, tq), lambda h, i: (h, 0, 0, 0)), res,
                  pl.BlockSpec((tq, LANES), lambda h, i: (i, h // 2)),
                  pl.BlockSpec((tq, LANES), lambda h, i: (i, h // 2)),
                  pl.BlockSpec((1, 1, tq), lambda h, i: (h, 0, i))],
        out_specs=[pl.BlockSpec((1, tq, LANES), lambda h, i: (h, i, 0)), res, res],
        out_shape=[_sds((HEADS, T, LANES), F32)] * 3,
        scratch_shapes=[pltpu.VMEM((LANES, tq), F32)],
        compiler_params=_cp(("parallel", "arbitrary"), 48))(q, k, kt, vpad, o, do, lse)


def _mla_prep_bwd(proj, dq, dk, dv, lw, rope):
    T = proj.shape[0]
    tm = min(TM_ROW, T)

    def body(ql_ref, kvl_ref, kr_ref, dq_ref, dk_ref, dv_ref, qa_ref, kva_ref, wq_ref, wk_ref, wv_ref,
             wqt_ref, wkt_ref, wvt_ref, qn_ref, kn_ref, c_ref, s1_ref, s2_ref,
             d_ref, dwq_ref, dwk_ref, dwv_ref, dqa_ref, dkva_ref, dqn_ref, dkn_ref):
        i = pl.program_id(0)

        @pl.when(i == 0)
        def _():
            for ref in (dwq_ref, dwk_ref, dwv_ref, dqa_ref, dkva_ref, dqn_ref, dkn_ref):
                ref[...] = jnp.zeros_like(ref)

        c, s1, s2 = c_ref[...], s1_ref[...], s2_ref[...]
        lane = lax.broadcasted_iota(jnp.int32, (1, LANES), 1)
        qlhat, qlr = _rms(ql_ref[...], MLA_Q_LORA)
        qn = (qlhat * qa_ref[...]).astype(MXU_DTYPE)
        kvhat, kvr = _rms(kvl_ref[...], MLA_KV_LORA)
        kvn = (kvhat * kva_ref[...]).astype(MXU_DTYPE)
        kr = kr_ref[...]
        dqnl = jnp.zeros((tm, MLA_Q_LORA), F32)
        dkvn = jnp.zeros((tm, MLA_KV_LORA), F32)
        dkr = jnp.zeros((tm, LANES), F32)
        dqw = jnp.zeros((1, LANES), F32)
        dkw = jnp.zeros((1, LANES), F32)
        for h in range(HEADS):
            xh, r = _rms(_dot(qn, wq_ref[h]), MLA_QK)
            dy = _rope_bwd(dq_ref[h], c, s1, s2)
            dqw = dqw + jnp.sum(dy * xh, axis=0, keepdims=True)
            dx = _rms_bwd(dy, xh, r, qn_ref[...], MLA_QK).astype(MXU_DTYPE)
            dwq_ref[h] += _dot_tn(qn, dx)
            dqnl = dqnl + _dot(dx, wqt_ref[h])

            xh, r = _rms(_dot(kvn, wk_ref[h]) + kr, MLA_QK)
            dy = _rope_bwd(dk_ref[h], c, s1, s2)
            dkw = dkw + jnp.sum(dy * xh, axis=0, keepdims=True)
            dxf = _rms_bwd(dy, xh, r, kn_ref[...], MLA_QK)
            dkr = dkr + dxf
            dx = dxf.astype(MXU_DTYPE)
            dwk_ref[h] += _dot_tn(kvn, dx)
            dkvn = dkvn + _dot(dx, wkt_ref[h])
        dvc = jnp.concatenate([dv_ref[2 * j] + dv_ref[2 * j + 1] for j in range(4)], axis=1).astype(MXU_DTYPE)
        dwv_ref[...] += _dot_tn(kvn, dvc)
        dkvn = dkvn + _dot(dvc, wvt_ref[...])
        dqa_ref[...] += _row0(jnp.sum(dqnl * qlhat, axis=0, keepdims=True))
        dkva_ref[...] += _row0(jnp.sum(dkvn * kvhat, axis=0, keepdims=True))
        dqn_ref[...] += _row0(dqw)
        dkn_ref[...] += _row0(dkw)
        d_ref[:, 0:256] = _rms_bwd(dqnl, qlhat, qlr, qa_ref[...], MLA_Q_LORA).astype(MXU_DTYPE)
        d_ref[:, 256:384] = _rms_bwd(dkvn, kvhat, kvr, kva_ref[...], MLA_KV_LORA).astype(MXU_DTYPE)
        d_ref[:, 384:512] = jnp.where((lane >= 64) & (lane < 96), dkr, 0.0).astype(MXU_DTYPE)

    full = lambda shape: pl.BlockSpec(shape, lambda i: (0,) * len(shape))
    hd = pl.BlockSpec((HEADS, tm, LANES), lambda i: (0, i, 0))
    tab = pl.BlockSpec((tm, LANES), lambda i: (i, 0))
    return pl.pallas_call(
        body, name="mla_prep_bwd", grid=(T // tm,),
        in_specs=[pl.BlockSpec((tm, 256), lambda i: (i, CB_QLAT)), pl.BlockSpec((tm, LANES), lambda i: (i, CB_KVLAT)),
                  pl.BlockSpec((tm, LANES), lambda i: (i, CB_KROPE)), hd, hd, hd,
                  full((1, 256)), full((1, LANES)), full((HEADS, 256, LANES)), full((HEADS, LANES, LANES)),
                  full((LANES, 512)), full((HEADS, LANES, 256)), full((HEADS, LANES, LANES)), full((512, LANES)),
                  full((1, LANES)), full((1, LANES)), tab, tab, tab],
        out_specs=[pl.BlockSpec((tm, 512), lambda i: (i, 0)), full((HEADS, 256, LANES)),
                   full((HEADS, LANES, LANES)), full((LANES, 512)), full((8, 256)), full((8, LANES)),
                   full((8, LANES)), full((8, LANES))],
        out_shape=[_sds((T, 512), MXU_DTYPE), _sds((HEADS, 256, LANES), F32), _sds((HEADS, LANES, LANES), F32),
                   _sds((LANES, 512), F32), _sds((8, 256), F32), _sds((8, LANES), F32), _sds((8, LANES), F32),
                   _sds((8, LANES), F32)],
        compiler_params=_cp(("arbitrary",), 48))(
            proj, proj, proj, dq, dk, dv, lw["qa"], lw["kva"], lw["wq"], lw["wk"], lw["wv"],
            lw["wqt"], lw["wkt"], lw["wvt"], lw["qn"], lw["kn"], rope[0], rope[1], rope[2])


def _norm_bwd(dh, x, g_in, ng):
    T, D = x.shape
    tm = min(TM_ROW, T)

    def body(dh_ref, x_ref, g_ref, w_ref, dx_ref, dw_ref):
        i = pl.program_id(0)

        @pl.when(i == 0)
        def _():
            dw_ref[...] = jnp.zeros_like(dw_ref)

        xhat, r = _rms(x_ref[...], D)
        dh_t = dh_ref[...]
        dw_ref[...] += _row0(jnp.sum(dh_t * xhat, axis=0, keepdims=True))
        dx_ref[...] = g_ref[...] + _rms_bwd(dh_t, xhat, r, w_ref[...], D)

    tile = pl.BlockSpec((tm, D), lambda i: (i, 0))
    return pl.pallas_call(
        body, name="norm_bwd", grid=(T // tm,),
        in_specs=[tile, tile, tile, pl.BlockSpec((1, D), lambda i: (0, 0))],
        out_specs=[tile, pl.BlockSpec((8, D), lambda i: (0, 0))],
        out_shape=[_sds((T, D), F32), _sds((8, D), F32)],
        compiler_params=_cp(("arbitrary",), 32))(dh, x, g_in, ng)


def _rope_tables(T):
    half = MLA_ROPE // 2
    inv_freq = jnp.power(jnp.float32(ROPE_THETA), -jnp.arange(half, dtype=F32) / half)
    ang = jnp.arange(T, dtype=F32)[:, None] * inv_freq[None, :]
    cos, sin = jnp.cos(ang), jnp.sin(ang)
    z = lambda n: jnp.zeros((T, n), F32)
    c = jnp.concatenate([jnp.ones((T, MLA_NOPE), F32), cos, cos, z(32)], axis=1)
    s1 = jnp.concatenate([z(64), -sin, z(48)], axis=1)
    s2 = jnp.concatenate([z(80), sin, z(32)], axis=1)
    return c, s1, s2


def _pad_lanes(v, n=LANES):
    v = v.reshape(1, -1)
    return jnp.pad(v, ((0, 0), (0, n - v.shape[1])))


def _pack_win(w):
    z = lambda n: jnp.zeros((w.shape[0], n), w.dtype)
    return jnp.concatenate([w[:, 0:384], z(64), w[:, 384:416], z(32), w[:, 416:2976], w[:, 2976:3488],
                            w[:, 3744:4256], w[:, 3488:3616], w[:, 3616:3744]], axis=1)


def _unpack_dwin(d):
    return jnp.concatenate([d[:, 0:384], d[:, 448:480], d[:, 512:3072], d[:, 3072:3584], d[:, 4096:4224],
                            d[:, 4224:4352], d[:, 3584:4096]], axis=1)


def _layer_weights(l, norm_g, w_in_full, qa, wqb_full, kva, wkvb_full, qn, kn, conv_full, sqn, skn, sinks,
                   w_out_full):
    wp = _pack_win(w_in_full)
    wq = jnp.pad(wqb_full, ((0, 0), (0, 0), (0, LANES - MLA_QK)))
    wk = jnp.pad(wkvb_full[:, :, :MLA_NOPE], ((0, 0), (0, 0), (0, LANES - MLA_NOPE)))
    wv = jnp.transpose(wkvb_full[:, :, MLA_NOPE:], (1, 0, 2)).reshape(MLA_KV_LORA, GROUP_WIDTH)
    return dict(
        ng=norm_g[l].reshape(1, -1), wp=wp, wpt=wp.T, qa=qa[l].reshape(1, -1), kva=kva[l].reshape(1, -1),
        wq=wq, wk=wk, wv=wv, wqt=jnp.transpose(wq, (0, 2, 1)), wkt=jnp.transpose(wk, (0, 2, 1)), wvt=wv.T,
        qn=_pad_lanes(qn[l]), kn=_pad_lanes(kn[l]),
        conv=jnp.pad(conv_full, ((0, 5), (0, 0))),
        sqn=jnp.tile(sqn[l].reshape(1, -1), (1, 2)), skn=jnp.tile(skn[l].reshape(1, -1), (1, 2)),
        sinks=sinks[l], wo=w_out_full, wot=w_out_full.T)


def _layer_fwd(x, lw, rope):
    proj, h = _inproj_fwd(x, lw["ng"], lw["wp"])
    q, k, kt, vpad, vt = _mla_prep_fwd(proj, lw, rope)
    o_mla, lse = _mla_attn_fwd(q, k, vt)
    o_swa = _swa_fwd(proj, lw)
    ycat = _mix_fwd(proj, o_mla, o_swa, lw["conv"])
    x_next = _mm_nn(ycat, lw["wo"], "outproj_fwd", residual=x)
    return x_next, dict(x=x, proj=proj, h=h, q=q, k=k, kt=kt, vpad=vpad, o_mla=o_mla, lse=lse, o_swa=o_swa, ycat=ycat)


def _layer_bwd(g, sv, lw, rope):
    proj = sv["proj"]
    dycat = _mm_nn(g, lw["wot"], "outproj_bwd_dy")
    d_wo = _mm_tn(sv["ycat"], g, "outproj_bwd_dw", WIRE_DTYPE, tn=D_MODEL)
    d1, dgs, do_mla, do_swa, d_conv = _mix_bwd(dycat, proj, sv["o_mla"], sv["o_swa"], lw["conv"])
    dsq, dkn_acc, dv_acc, d_sqn, d_sinks = _swa_bwd(proj, sv["o_swa"], do_swa, lw)
    dskv, d_skn = _swa_kv_bwd(proj, dkn_acc, dv_acc, lw)
    dq, dk, dv = _mla_attn_bwd(sv["q"], sv["k"], sv["kt"], sv["vpad"], sv["o_mla"], do_mla, sv["lse"])
    dmla, d_wq, d_wk, d_wv, d_qa, d_kva, d_qn, d_kn = _mla_prep_bwd(proj, dq, dk, dv, lw, rope)
    dproj = jnp.concatenate([dmla, d1, dsq, dgs, dskv], axis=1)
    dh = _mm_nn(dproj, lw["wpt"], "inproj_bwd_dh")
    dx, d_ng = _norm_bwd(dh, sv["x"], g, lw["ng"])
    d_wp = _mm_tn(sv["h"], dproj, "inproj_bwd_dw", WIRE_DTYPE, tn=NP // 2)
    grads = dict(
        w_in=_unpack_dwin(d_wp), w_out=d_wo,
        w_qb=d_wq[:, :, :MLA_QK],
        w_kvb=jnp.concatenate([d_wk[:, :, :MLA_NOPE],
                               jnp.transpose(d_wv.reshape(MLA_KV_LORA, HEADS, MLA_NOPE), (1, 0, 2))], axis=2),
        conv=d_conv[0:3], norm_g=d_ng[0], qa=d_qa[0], kva=d_kva[0], qn=d_qn[0, :MLA_QK], kn=d_kn[0, :MLA_QK],
        sqn=d_sqn[0, :SWA_HEAD_DIM], skn=d_skn[0, :SWA_HEAD_DIM], sinks=d_sinks[:, 0])
    return dx, grads


def _local_step(x, target, lws, rope):
    saved = []
    for lw in lws:
        x, sv = _layer_fwd(x, lw, rope)
        saved.append(sv)
    g, loss_tile = _loss_grad(x, target)
    grads = [None] * len(lws)
    for l in reversed(range(len(lws))):
        g, grads[l] = _layer_bwd(g, saved[l], lws[l], rope)
    return loss_tile, g, grads


def _my_coords():
    return lax.axis_index("x"), lax.axis_index("y"), lax.axis_index("c")


def _peer(me, k):
    x, y, c = me
    return (1 - x if k & 4 else x, 1 - y if k & 2 else y, 1 - c if k & 1 else c)


def _lin(d):
    return 4 * d[0] + 2 * d[1] + d[2]


def _all_gather(shards):
    n = len(shards)

    def body(*refs):
        ins, outs = refs[:n], refs[n:2 * n]
        send_sems, recv_sems, local_sems = refs[2 * n:]
        me = _my_coords()
        my = _lin(me)
        local = [pltpu.make_async_copy(ins[a], outs[a].at[my], local_sems.at[a]) for a in range(n)]
        for cp in local:
            cp.start()
        sends = []
        for a in range(n):
            for k in range(1, N_DEV):
                cp = pltpu.make_async_remote_copy(
                    src_ref=ins[a], dst_ref=outs[a].at[my], send_sem=send_sems.at[a * 7 + k - 1],
                    recv_sem=recv_sems.at[a * 7 + k - 1], device_id=_peer(me, k),
                    device_id_type=pl.DeviceIdType.MESH)
                cp.start()
                sends.append(cp)
        for a in range(n):
            for k in range(1, N_DEV):
                src = _lin(_peer(me, k))
                pltpu.make_async_remote_copy(
                    src_ref=ins[a], dst_ref=outs[a].at[src], send_sem=send_sems.at[a * 7 + k - 1],
                    recv_sem=recv_sems.at[a * 7 + k - 1], device_id=_peer(me, k),
                    device_id_type=pl.DeviceIdType.MESH).wait_recv()
        for cp in sends:
            cp.wait_send()
        for cp in local:
            cp.wait()

    any_spec = pl.BlockSpec(memory_space=pl.ANY)
    return pl.pallas_call(
        body, name="weight_all_gather",
        in_specs=[any_spec] * n, out_specs=[any_spec] * n,
        out_shape=[_sds((N_DEV,) + s.shape, s.dtype) for s in shards],
        scratch_shapes=[pltpu.SemaphoreType.DMA((7 * n,)), pltpu.SemaphoreType.DMA((7 * n,)),
                        pltpu.SemaphoreType.DMA((n,))],
    )(*shards)


def _grad_exchange(slots):
    n = len(slots)

    def body(*refs):
        ins, outs = refs[:n], refs[n:2 * n]
        send_sems, recv_sems, local_sems = refs[2 * n:]
        me = _my_coords()
        my = _lin(me)
        local = [pltpu.make_async_copy(ins[a].at[my], outs[a].at[my], local_sems.at[a]) for a in range(n)]
        for cp in local:
            cp.start()
        sends = []
        for a in range(n):
            for k in range(1, N_DEV):
                peer = _peer(me, k)
                cp = pltpu.make_async_remote_copy(
                    src_ref=ins[a].at[_lin(peer)], dst_ref=outs[a].at[my], send_sem=send_sems.at[a * 7 + k - 1],
                    recv_sem=recv_sems.at[a * 7 + k - 1], device_id=peer, device_id_type=pl.DeviceIdType.MESH)
                cp.start()
                sends.append(cp)
        for a in range(n):
            for k in range(1, N_DEV):
                peer = _peer(me, k)
                pltpu.make_async_remote_copy(
                    src_ref=ins[a].at[my], dst_ref=outs[a].at[_lin(peer)], send_sem=send_sems.at[a * 7 + k - 1],
                    recv_sem=recv_sems.at[a * 7 + k - 1], device_id=peer,
                    device_id_type=pl.DeviceIdType.MESH).wait_recv()
        for cp in sends:
            cp.wait_send()
        for cp in local:
            cp.wait()

    any_spec = pl.BlockSpec(memory_space=pl.ANY)
    return pl.pallas_call(
        body, name="grad_exchange",
        in_specs=[any_spec] * n, out_specs=[any_spec] * n,
        out_shape=[_sds(s.shape, s.dtype) for s in slots],
        scratch_shapes=[pltpu.SemaphoreType.DMA((7 * n,)), pltpu.SemaphoreType.DMA((7 * n,)),
                        pltpu.SemaphoreType.DMA((n,))],
    )(*slots)


def _small_all_reduce(v):
    R = v.shape[0]

    def body(v_ref, o_ref, buf, send_sems, recv_sems):
        me = _my_coords()
        my = _lin(me)
        sends = []
        for k in range(1, N_DEV):
            cp = pltpu.make_async_remote_copy(
                src_ref=v_ref, dst_ref=buf.at[my], send_sem=send_sems.at[k - 1], recv_sem=recv_sems.at[k - 1],
                device_id=_peer(me, k), device_id_type=pl.DeviceIdType.MESH)
            cp.start()
            sends.append(cp)
        buf[my] = v_ref[...]
        for k in range(1, N_DEV):
            pltpu.make_async_remote_copy(
                src_ref=v_ref, dst_ref=buf.at[_lin(_peer(me, k))], send_sem=send_sems.at[k - 1],
                recv_sem=recv_sems.at[k - 1], device_id=_peer(me, k),
                device_id_type=pl.DeviceIdType.MESH).wait_recv()
        for cp in sends:
            cp.wait_send()
        tot = buf[0]
        for d in range(1, N_DEV):
            tot = tot + buf[d]
        o_ref[...] = tot

    vm = pl.BlockSpec(memory_space=pltpu.VMEM)
    return pl.pallas_call(
        body, name="small_all_reduce", in_specs=[vm], out_specs=vm, out_shape=_sds(v.shape, F32),
        scratch_shapes=[pltpu.VMEM((N_DEV, R, LANES), F32), pltpu.SemaphoreType.DMA((7,)),
                        pltpu.SemaphoreType.DMA((7,))],
    )(v)


def _adamw_math(w, g, m, v):
    m = ADAM_B1 * m + (1.0 - ADAM_B1) * g
    v = ADAM_B2 * v + (1.0 - ADAM_B2) * (g * g)
    m_hat = m / (1.0 - ADAM_B1 ** ADAM_STEP)
    v_hat = v / (1.0 - ADAM_B2 ** ADAM_STEP)
    delta = -ADAM_LR * (m_hat / (jnp.sqrt(v_hat) + ADAM_EPS) + ADAM_WD * w)
    return delta, m, v


def _adamw(parts, w, m, v, name, tr):
    P, R, C = parts.shape
    tr = min(tr, R)

    def body(p_ref, w_ref, m_ref, v_ref, g_out, d_out, m_out, v_out):
        g = p_ref[0].astype(F32)
        for d in range(1, P):
            g = g + p_ref[d].astype(F32)
        delta, m_new, v_new = _adamw_math(w_ref[...], g, m_ref[...], v_ref[...])
        g_out[...] = g
        d_out[...] = delta
        m_out[...] = m_new
        v_out[...] = v_new

    tile = pl.BlockSpec((tr, C), lambda i: (i, 0))
    return pl.pallas_call(
        body, name=name, grid=(R // tr,),
        in_specs=[pl.BlockSpec((P, tr, C), lambda i: (0, i, 0)), tile, tile, tile],
        out_specs=[tile] * 4, out_shape=[_sds((R, C), F32)] * 4,
        compiler_params=_cp(("parallel",), 32))(parts, w, m, v)


SMALL = (("norm_g", D_MODEL), ("mla_q_a_norm", MLA_Q_LORA), ("mla_kv_a_norm", MLA_KV_LORA), ("mla_q_norm", MLA_QK),
         ("mla_k_norm", MLA_QK), ("swa_q_norm", SWA_HEAD_DIM), ("swa_k_norm", SWA_HEAD_DIM), ("swa_sinks", HEADS))
SMALL_GRAD_KEY = dict(norm_g="norm_g", mla_q_a_norm="qa", mla_kv_a_norm="kva", mla_q_norm="qn", mla_k_norm="kn",
                      swa_q_norm="sqn", swa_k_norm="skn", swa_sinks="sinks")
SMALL_ROWS = 32
CONV_ROWS = 24


def _pack_small(get):
    parts = []
    for l in range(DEPTH):
        for name, n in SMALL:
            v = get(name, l).reshape(-1)
            parts.append(jnp.pad(v, (0, (-n) % LANES)))
    return jnp.concatenate(parts).reshape(SMALL_ROWS, LANES)


def _unpack_small(packed):
    flat = packed.reshape(-1)
    out = {name: [] for name, _ in SMALL}
    off = 0
    for l in range(DEPTH):
        for name, n in SMALL:
            out[name].append(flat[off:off + n])
            off += n + (-n) % LANES
    return {name: jnp.stack(v) for name, v in out.items()}


def kernel(x, norm_g, w_in, mla_q_a_norm, mla_w_qb, mla_kv_a_norm, mla_w_kvb, mla_q_norm, mla_k_norm, conv_w, swa_q_norm, swa_k_norm, swa_sinks, w_out, loss_target, m_norm_g, m_w_in, m_mla_q_a_norm, m_mla_w_qb, m_mla_kv_a_norm, m_mla_w_kvb, m_mla_q_norm, m_mla_k_norm, m_conv_w, m_swa_q_norm, m_swa_k_norm, m_swa_sinks, m_w_out, v_norm_g, v_w_in, v_mla_q_a_norm, v_mla_w_qb, v_mla_kv_a_norm, v_mla_w_kvb, v_mla_q_norm, v_mla_k_norm, v_conv_w, v_swa_q_norm, v_swa_k_norm, v_swa_sinks, v_w_out):
    T = x.shape[1]
    weights = dict(norm_g=norm_g, w_in=w_in, mla_q_a_norm=mla_q_a_norm, mla_w_qb=mla_w_qb,
                   mla_kv_a_norm=mla_kv_a_norm, mla_w_kvb=mla_w_kvb, mla_q_norm=mla_q_norm, mla_k_norm=mla_k_norm,
                   conv_w=conv_w, swa_q_norm=swa_q_norm, swa_k_norm=swa_k_norm, swa_sinks=swa_sinks, w_out=w_out)
    mom_m = dict(norm_g=m_norm_g, w_in=m_w_in, mla_q_a_norm=m_mla_q_a_norm, mla_w_qb=m_mla_w_qb,
                 mla_kv_a_norm=m_mla_kv_a_norm, mla_w_kvb=m_mla_w_kvb, mla_q_norm=m_mla_q_norm,
                 mla_k_norm=m_mla_k_norm, conv_w=m_conv_w, swa_q_norm=m_swa_q_norm, swa_k_norm=m_swa_k_norm,
                 swa_sinks=m_swa_sinks, w_out=m_w_out)
    mom_v = dict(norm_g=v_norm_g, w_in=v_w_in, mla_q_a_norm=v_mla_q_a_norm, mla_w_qb=v_mla_w_qb,
                 mla_kv_a_norm=v_mla_kv_a_norm, mla_w_kvb=v_mla_w_kvb, mla_q_norm=v_mla_q_norm,
                 mla_k_norm=v_mla_k_norm, conv_w=v_conv_w, swa_q_norm=v_swa_q_norm, swa_k_norm=v_swa_k_norm,
                 swa_sinks=v_swa_sinks, w_out=v_w_out)

    g_win, g_wqb, g_wkvb, g_wout, g_conv = _all_gather([
        w_in.astype(MXU_DTYPE), mla_w_qb.astype(MXU_DTYPE), mla_w_kvb.astype(MXU_DTYPE), w_out.astype(MXU_DTYPE),
        conv_w])
    lws = []
    for l in range(DEPTH):
        w_in_full = jnp.transpose(g_win[:, l], (1, 0, 2)).reshape(D_MODEL, IN_COLS)
        conv_full = jnp.transpose(g_conv[:, l], (1, 0, 2)).reshape(3, GROUP_WIDTH)
        lws.append(_layer_weights(l, norm_g, w_in_full, mla_q_a_norm, g_wqb[:, l], mla_kv_a_norm, g_wkvb[:, l],
                                  mla_q_norm, mla_k_norm, conv_full, swa_q_norm, swa_k_norm, swa_sinks,
                                  g_wout[:, l].reshape(D_MIX, D_MODEL)))

    loss_tile, grad_x, grads = _local_step(x[0], loss_target[0], lws, _rope_tables(T))

    big_slots = [
        jnp.stack([jnp.transpose(g["w_in"].reshape(D_MODEL, N_DEV, IN_COLS // N_DEV), (1, 0, 2)) for g in grads], 1),
        jnp.stack([g["w_out"].reshape(N_DEV, D_MIX // N_DEV, D_MODEL) for g in grads], 1),
        jnp.stack([g["w_qb"] for g in grads], 1),
        jnp.stack([g["w_kvb"] for g in grads], 1),
    ]
    r_win, r_wout, r_wqb, r_wkvb = _grad_exchange(big_slots)

    small = jnp.concatenate([
        _pack_small(lambda name, l: grads[l][SMALL_GRAD_KEY[name]]),
        jnp.stack([g["conv"] for g in grads]).reshape(CONV_ROWS, LANES),
        loss_tile], axis=0)
    small = _small_all_reduce(small)
    loss = small[SMALL_ROWS + CONV_ROWS, 0]
    my = _lin(_my_coords())
    conv_g = lax.dynamic_slice_in_dim(small[SMALL_ROWS:SMALL_ROWS + CONV_ROWS].reshape(DEPTH, 3, GROUP_WIDTH),
                                      my * 64, 64, axis=2)

    out = {}

    def big(name, recv, rows, cols, tr):
        res = _adamw(recv.reshape(N_DEV, rows, cols), weights[name].reshape(rows, cols),
                     mom_m[name].reshape(rows, cols), mom_v[name].reshape(rows, cols), "adamw_" + name, tr)
        out[name] = [r.reshape(weights[name].shape) for r in res]

    big("w_in", r_win, DEPTH * D_MODEL, IN_COLS // N_DEV, 256)
    big("w_out", r_wout, DEPTH * D_MIX // N_DEV, D_MODEL, 192)
    big("mla_w_qb", r_wqb, DEPTH * MLA_Q_LORA, MLA_QK, 512)
    big("mla_w_kvb", r_wkvb, DEPTH * MLA_KV_LORA, 128, 256)

    pad_conv = lambda a: jnp.pad(a.reshape(-1), (0, 8 * LANES - 6 * 64)).reshape(8, LANES)
    cat = lambda src: jnp.concatenate([_pack_small(lambda name, l: src[name][l]), pad_conv(src["conv_w"])], axis=0)
    g_small = jnp.concatenate([small[:SMALL_ROWS], pad_conv(conv_g)], axis=0)
    res = _adamw(g_small[None], cat(weights), cat(mom_m), cat(mom_v), "adamw_small", SMALL_ROWS + 8)
    smalls = [_unpack_small(r[:SMALL_ROWS]) for r in res]
    for name, _ in SMALL:
        out[name] = [s[name] for s in smalls]
    out["conv_w"] = [r[SMALL_ROWS:].reshape(-1)[:6 * 64].reshape(DEPTH, 3, 64) for r in res]

    order = ["norm_g", "w_in", "mla_q_a_norm", "mla_w_qb", "mla_kv_a_norm", "mla_w_kvb", "mla_q_norm", "mla_k_norm",
             "conv_w", "swa_q_norm", "swa_k_norm", "swa_sinks", "w_out"]
    result = [loss, grad_x[None]]
    for idx in range(4):
        result += [out[name][idx] for name in order]
    return tuple(result)
```

```python
import functools

import jax
import jax.numpy as jnp
import numpy as np
from jax import lax
from jax.experimental import pallas as pl
from jax.experimental.pallas import tpu as pltpu

F32 = jnp.float32
MXU_DTYPE = jnp.bfloat16
WIRE_DTYPE = jnp.bfloat16

N_DEV = 8
DEPTH = 2
D_MODEL = 1024
GROUP_WIDTH = 512
D_MIX = 3 * GROUP_WIDTH
BLOCK = 128
RMS_EPS = 1e-6
NEG_INF = -1e30
HEADS = 8
MLA_QK = 96
MLA_NOPE = 64
MLA_ROPE = 32
MLA_Q_LORA = 256
MLA_KV_LORA = 128
ROPE_THETA = 10000.0
SWA_HEAD_DIM = 64
LANES = 128
IN_COLS = 4256

ADAM_LR = 0.001
ADAM_B1 = 0.9
ADAM_B2 = 0.999
ADAM_EPS = 1e-08
ADAM_WD = 0.01
ADAM_STEP = 10

NP = 4352
CB_QLAT = 0
CB_KVLAT = 2
CB_KROPE = 3
CB_GMLA, CB_CH, CB_CB, CB_CC, CB_GCONV, CB_SQ, CB_GSWA = 1, 2, 3, 4, 5, 6, 7
CB_SK, CB_SV = 32, 33

TM_PROJ = 256
TM_ROW = 256
TK = 256
TQ = 2 * TK
MLA_SCALE = MLA_QK ** -0.5
LOG2E = 1.4426950408889634
LN2 = 0.6931471805599453
TM_SWA = 256
VMEM_MB = 2 ** 20


def _cp(sem, vmem_mb):
    return pltpu.CompilerParams(dimension_semantics=sem, vmem_limit_bytes=vmem_mb * VMEM_MB)


def _sds(shape, dtype):
    return jax.ShapeDtypeStruct(shape, dtype)


def _dot(a, b):
    return jnp.dot(a, b, preferred_element_type=F32)


def _dot_nt(a, b):
    return lax.dot_general(a, b, (((1,), (1,)), ((), ())), preferred_element_type=F32)


def _dot_tn(a, b):
    return lax.dot_general(a, b, (((0,), (0,)), ((), ())), preferred_element_type=F32)


def _rms(x, n):
    r = lax.rsqrt(jnp.sum(x * x, axis=-1, keepdims=True) * (1.0 / n) + RMS_EPS)
    return x * r, r


def _rms_bwd(dy, xhat, r, w, n):
    g = dy * w
    return r * (g - xhat * (jnp.sum(g * xhat, axis=-1, keepdims=True) * (1.0 / n)))


def _rms_halves(x, half1):
    x2 = x * x
    s0 = jnp.sum(jnp.where(half1, 0.0, x2), axis=-1, keepdims=True)
    s1 = jnp.sum(jnp.where(half1, x2, 0.0), axis=-1, keepdims=True)
    r = jnp.where(half1, lax.rsqrt(s1 * (1.0 / 64) + RMS_EPS), lax.rsqrt(s0 * (1.0 / 64) + RMS_EPS))
    return x * r, r


def _rms_halves_bwd(dy, xhat, r, w, half1):
    g = dy * w
    t = g * xhat
    m0 = jnp.sum(jnp.where(half1, 0.0, t), axis=-1, keepdims=True) * (1.0 / 64)
    m1 = jnp.sum(jnp.where(half1, t, 0.0), axis=-1, keepdims=True) * (1.0 / 64)
    return r * (g - xhat * jnp.where(half1, m1, m0))


def _sigmoid(x):
    return 1.0 / (1.0 + jnp.exp(-x))


def _rope(x, c, s1, s2):
    return x * c + pltpu.roll(x, 112, 1) * s1 + pltpu.roll(x, 16, 1) * s2


def _rope_bwd(dy, c, s1, s2):
    return dy * c + pltpu.roll(dy * s1, 16, 1) + pltpu.roll(dy * s2, 112, 1)


def _fold_rows8(x):
    return jnp.sum(x.reshape(x.shape[0] // 8, 8, x.shape[1]), axis=0)


def _row0(v, rows=8):
    row = lax.broadcasted_iota(jnp.int32, (rows, v.shape[1]), 0)
    return jnp.where(row == 0, jnp.broadcast_to(v, (rows, v.shape[1])), 0.0)


def _mm_nn(a, b, name, out_dtype=F32, residual=None, tm=TM_PROJ):
    M, K = a.shape
    N = b.shape[1]
    tm = min(tm, M)

    def body(*refs):
        if residual is None:
            a_ref, b_ref, o_ref = refs
            acc = _dot(a_ref[...].astype(MXU_DTYPE), b_ref[...])
        else:
            a_ref, b_ref, r_ref, o_ref = refs
            acc = _dot(a_ref[...].astype(MXU_DTYPE), b_ref[...]) + r_ref[...]
        o_ref[...] = acc.astype(out_dtype)

    in_specs = [pl.BlockSpec((tm, K), lambda i: (i, 0)), pl.BlockSpec((K, N), lambda i: (0, 0))]
    args = [a, b]
    if residual is not None:
        in_specs.append(pl.BlockSpec((tm, N), lambda i: (i, 0)))
        args.append(residual)
    return pl.pallas_call(
        body, name=name, grid=(M // tm,), in_specs=in_specs,
        out_specs=pl.BlockSpec((tm, N), lambda i: (i, 0)), out_shape=_sds((M, N), out_dtype),
        compiler_params=_cp(("parallel",), 48))(*args)


def _mm_tn(a, b, name, out_dtype, tn, tk=512):
    T, M = a.shape
    N = b.shape[1]
    tk = min(tk, T)
    nk = T // tk

    def body(a_ref, b_ref, o_ref, acc_ref):
        k = pl.program_id(1)

        @pl.when(k == 0)
        def _():
            acc_ref[...] = jnp.zeros_like(acc_ref)

        acc_ref[...] += _dot_tn(a_ref[...].astype(MXU_DTYPE), b_ref[...].astype(MXU_DTYPE))

        @pl.when(k == nk - 1)
        def _():
            o_ref[...] = acc_ref[...].astype(out_dtype)

    return pl.pallas_call(
        body, name=name, grid=(N // tn, nk),
        in_specs=[pl.BlockSpec((tk, M), lambda n, k: (k, 0)), pl.BlockSpec((tk, tn), lambda n, k: (k, n))],
        out_specs=pl.BlockSpec((M, tn), lambda n, k: (0, n)), out_shape=_sds((M, N), out_dtype),
        scratch_shapes=[pltpu.VMEM((M, tn), F32)],
        compiler_params=_cp(("parallel", "arbitrary"), 48))(a, b)


def _inproj_fwd(x, ng, wp):
    T, D = x.shape
    tm = min(TM_PROJ, T)

    def body(x_ref, g_ref, w_ref, proj_ref, h_ref):
        xhat, _ = _rms(x_ref[...], D)
        h = (xhat * g_ref[...]).astype(MXU_DTYPE)
        h_ref[...] = h
        proj_ref[...] = _dot(h, w_ref[...])

    return pl.pallas_call(
        body, name="inproj_fwd", grid=(T // tm,),
        in_specs=[pl.BlockSpec((tm, D), lambda i: (i, 0)), pl.BlockSpec((1, D), lambda i: (0, 0)),
                  pl.BlockSpec((D, NP), lambda i: (0, 0))],
        out_specs=[pl.BlockSpec((tm, NP), lambda i: (i, 0)), pl.BlockSpec((tm, D), lambda i: (i, 0))],
        out_shape=[_sds((T, NP), F32), _sds((T, D), MXU_DTYPE)],
        compiler_params=_cp(("parallel",), 48))(x, ng, wp)


def _mla_prep_fwd(proj, lw, rope):
    T = proj.shape[0]
    tm = min(TK, T // 2)

    def body(ql_ref, kvl_ref, kr_ref, qa_ref, kva_ref, wq_ref, wk_ref, wv_ref, qn_ref, kn_ref,
             c_ref, s1_ref, s2_ref, q_out, k_out, kt_out, v_out, vt_out):
        c, s1, s2 = c_ref[...], s1_ref[...], s2_ref[...]
        qhat, _ = _rms(ql_ref[...], MLA_Q_LORA)
        qn = (qhat * qa_ref[...]).astype(MXU_DTYPE)
        khat, _ = _rms(kvl_ref[...], MLA_KV_LORA)
        kvn = (khat * kva_ref[...]).astype(MXU_DTYPE)
        kr = kr_ref[...]
        half1 = lax.broadcasted_iota(jnp.int32, (tm, LANES), 1) >= 64
        for h in range(HEADS):
            xh, _ = _rms(_dot(qn, wq_ref[h]), MLA_QK)
            q_out[h] = (_rope(xh * qn_ref[...], c, s1, s2) * (MLA_SCALE * LOG2E)).astype(MXU_DTYPE)
            xh, _ = _rms(_dot(kvn, wk_ref[h]) + kr, MLA_QK)
            kh = _rope(xh * kn_ref[...], c, s1, s2)
            k_out[h] = kh.astype(MXU_DTYPE)
            kt_out[h, 0] = kh.T.astype(MXU_DTYPE)
        v = _dot(kvn, wv_ref[...])
        for h in range(HEADS):
            vp = v[:, LANES * (h // 2):LANES * (h // 2 + 1)]
            own = half1 if h % 2 else jnp.logical_not(half1)
            vp = jnp.where(own, vp, 0.0)
            v_out[h] = vp.astype(MXU_DTYPE)
            vt_out[h, 0] = vp.T.astype(MXU_DTYPE)

    full = lambda shape: pl.BlockSpec(shape, lambda i: (0,) * len(shape))
    hd = pl.BlockSpec((HEADS, tm, LANES), lambda i: (0, i, 0))
    hdt = pl.BlockSpec((HEADS, 1, LANES, tm), lambda i: (0, i, 0, 0))
    nat = _sds((HEADS, T, LANES), MXU_DTYPE)
    tr = _sds((HEADS, T // tm, LANES, tm), MXU_DTYPE)
    return pl.pallas_call(
        body, name="mla_prep_fwd", grid=(T // tm,),
        in_specs=[pl.BlockSpec((tm, 256), lambda i: (i, CB_QLAT)), pl.BlockSpec((tm, LANES), lambda i: (i, CB_KVLAT)),
                  pl.BlockSpec((tm, LANES), lambda i: (i, CB_KROPE)),
                  full((1, 256)), full((1, LANES)), full((HEADS, 256, LANES)), full((HEADS, LANES, LANES)),
                  full((LANES, 512)), full((1, LANES)), full((1, LANES)),
                  pl.BlockSpec((tm, LANES), lambda i: (i, 0)), pl.BlockSpec((tm, LANES), lambda i: (i, 0)),
                  pl.BlockSpec((tm, LANES), lambda i: (i, 0))],
        out_specs=[hd, hd, hdt, hd, hdt],
        out_shape=[nat, nat, tr, nat, tr],
        compiler_params=_cp(("parallel",), 32))(
            proj, proj, proj, lw["qa"], lw["kva"], lw["wq"], lw["wk"], lw["wv"], lw["qn"], lw["kn"],
            rope[0], rope[1], rope[2])


def _mla_attn_fwd(q, k, vt):
    T = q.shape[1]
    tk = min(TK, T // 2)
    tq = 2 * tk

    def body(q_ref, k_ref, vt_ref, o_ref, lse_ref, acc_s, m_s, l_s, s_a, s_b):
        i = pl.program_id(1)
        key = lax.broadcasted_iota(jnp.int32, (tk, tq), 0)
        qry = lax.broadcasted_iota(jnp.int32, (tk, tq), 1)
        qs = [q_ref[0], q_ref[1]]
        acc_s[...] = jnp.zeros_like(acc_s)
        l_s[...] = jnp.zeros_like(l_s)
        m_s[...] = jnp.full(m_s.shape, NEG_INF, F32)

        def scores(kj, buf):
            rows = pl.ds(pl.multiple_of(kj * tk, tk), tk)
            for r in range(2):
                buf[r] = _dot_nt(k_ref[r, rows, :], qs[r])

        def consume(kj, buf, diag):
            for r in range(2):
                s = buf[r]
                if diag is not None:
                    s = jnp.where(key + diag * tk <= qry, s, NEG_INF)
                m_old = m_s[r]
                m_new = jnp.maximum(m_old, jnp.max(s, axis=0, keepdims=True))
                alpha = jnp.exp2(m_old - m_new)
                p = jnp.exp2(s - m_new)
                l_s[r] = alpha * l_s[r] + jnp.sum(p, axis=0, keepdims=True)
                m_s[r] = m_new
                acc_s[r] = alpha * acc_s[r] + _dot(vt_ref[r, kj], p.astype(MXU_DTYPE))

        scores(0, s_a)

        def pair(kp, carry):
            kj = 2 * kp
            scores(kj + 1, s_b)
            consume(kj, s_a, None)
            scores(kj + 2, s_a)
            consume(kj + 1, s_b, None)
            return carry

        lax.fori_loop(0, i, pair, 0)
        scores(2 * i + 1, s_b)
        consume(2 * i, s_a, 0)
        consume(2 * i + 1, s_b, 1)
        o_t = acc_s[0] / l_s[0] + acc_s[1] / l_s[1]
        o_ref[...] = o_t.T
        for r in range(2):
            lse_ref[r] = m_s[r] + jnp.log2(l_s[r])

    return pl.pallas_call(
        body, name="mla_attn_fwd", grid=(HEADS // 2, T // tq),
        in_specs=[pl.BlockSpec((2, tq, LANES), lambda j, i: (j, i, 0)),
                  pl.BlockSpec((2, T, LANES), lambda j, i: (j, 0, 0)),
                  pl.BlockSpec((2, T // tk, LANES, tk), lambda j, i: (j, 0, 0, 0))],
        out_specs=[pl.BlockSpec((tq, LANES), lambda j, i: (i, j)),
                   pl.BlockSpec((2, 1, tq), lambda j, i: (j, 0, i))],
        out_shape=[_sds((T, GROUP_WIDTH), F32), _sds((HEADS, 1, T), F32)],
        scratch_shapes=[pltpu.VMEM((2, LANES, tq), F32), pltpu.VMEM((2, 1, tq), F32), pltpu.VMEM((2, 1, tq), F32),
                        pltpu.VMEM((2, tk, tq), F32), pltpu.VMEM((2, tk, tq), F32)],
        compiler_params=_cp(("parallel", "arbitrary"), 40))(q, k, vt)


def _swa_masks(nb_first):
    qi = lax.broadcasted_iota(jnp.int32, (BLOCK, 2 * BLOCK), 0)
    ki = lax.broadcasted_iota(jnp.int32, (BLOCK, 2 * BLOCK), 1)
    dist = BLOCK + qi - ki
    valid = (dist >= 0) & (dist < BLOCK) & ((ki >= BLOCK) | jnp.logical_not(nb_first))
    return dist.astype(F32), valid


def _swa_kv_variants(x, half1):
    xs = pltpu.roll(x, 64, 1)
    out = {}
    for g in range(2):
        for r in range(2):
            own = half1 if r else jnp.logical_not(half1)
            out[(g, r)] = jnp.where(own, x if g == r else xs, 0.0).astype(MXU_DTYPE)
    return out


def _swa_fwd(proj, lw):
    T = proj.shape[0]
    tm = min(TM_SWA, T)
    nb = tm // BLOCK
    scale = SWA_HEAD_DIM ** -0.5

    def body(q_ref, k_ref, v_ref, pk_ref, pv_ref, qw_ref, kw_ref, sink_ref, o_ref):
        i = pl.program_id(0)
        half1 = lax.broadcasted_iota(jnp.int32, (1, LANES), 1) >= 64
        k_all = jnp.concatenate([pk_ref[...], k_ref[...]], axis=0)
        v_all = jnp.concatenate([pv_ref[...], v_ref[...]], axis=0)
        khat, _ = _rms_halves(k_all, half1)
        kp = _swa_kv_variants(khat * kw_ref[...], half1)
        vp = _swa_kv_variants(v_all, half1)
        qn = []
        for j in range(4):
            qhat, _ = _rms_halves(q_ref[:, LANES * j:LANES * (j + 1)], half1)
            qn.append((qhat * qw_ref[...]).astype(MXU_DTYPE))
        for b in range(nb):
            dist, valid = _swa_masks((i == 0) & (b == 0))
            ks = slice(b * BLOCK, b * BLOCK + 2 * BLOCK)
            for j in range(4):
                g = j // 2
                qb = qn[j][b * BLOCK:(b + 1) * BLOCK]
                o = jnp.zeros((BLOCK, LANES), F32)
                for r in range(2):
                    h = 2 * j + r
                    s = _dot_nt(qb, kp[(g, r)][ks]) * scale - (2.0 ** -(h + 1)) * dist
                    s = jnp.where(valid, s, NEG_INF)
                    sink = sink_ref[h]
                    m = jnp.maximum(jnp.max(s, axis=-1, keepdims=True), sink)
                    e = jnp.exp(s - m)
                    den = jnp.sum(e, axis=-1, keepdims=True) + jnp.exp(sink - m)
                    o = o + _dot((e / den).astype(MXU_DTYPE), vp[(g, r)][ks])
                o_ref[b * BLOCK:(b + 1) * BLOCK, LANES * j:LANES * (j + 1)] = o

    prev = lambda cb: pl.BlockSpec((BLOCK, LANES), lambda i: (jnp.maximum(i * nb - 1, 0), cb))
    return pl.pallas_call(
        body, name="swa_fwd", grid=(T // tm,),
        in_specs=[pl.BlockSpec((tm, 512), lambda i: (i, CB_SQ)), pl.BlockSpec((tm, LANES), lambda i: (i, CB_SK)),
                  pl.BlockSpec((tm, LANES), lambda i: (i, CB_SV)), prev(CB_SK), prev(CB_SV),
                  pl.BlockSpec((1, LANES), lambda i: (0, 0)), pl.BlockSpec((1, LANES), lambda i: (0, 0)),
                  pl.BlockSpec(memory_space=pltpu.SMEM)],
        out_specs=pl.BlockSpec((tm, 512), lambda i: (i, 0)),
        out_shape=_sds((T, GROUP_WIDTH), F32),
        compiler_params=_cp(("parallel",), 32))(proj, proj, proj, proj, proj, lw["sqn"], lw["skn"], lw["sinks"])


def _shift_down(u, prev, n, row):
    tm = u.shape[0]
    out = pltpu.roll(u, n, 0)
    row8 = lax.broadcasted_iota(jnp.int32, prev.shape, 0)
    for t in range(n):
        src = jnp.sum(jnp.where(row8 == 8 - n + t, prev, 0.0), axis=0, keepdims=True)
        out = jnp.where(row == t, src, out)
    return out


def _shift_up(u, nxt, n, row):
    tm = u.shape[0]
    out = pltpu.roll(u, tm - n, 0)
    row8 = lax.broadcasted_iota(jnp.int32, nxt.shape, 0)
    for t in range(n):
        src = jnp.sum(jnp.where(row8 == t, nxt, 0.0), axis=0, keepdims=True)
        out = jnp.where(row == tm - n + t, src, out)
    return out


def _mix_fwd(proj, o_mla, o_swa, conv_w):
    T = proj.shape[0]
    tm = min(TM_ROW, T)

    def body(gm_ref, ch_ref, cb_ref, cc_ref, gc_ref, gs_ref, pch_ref, pcc_ref, om_ref, os_ref, w_ref, y_ref):
        i = pl.program_id(0)
        row = lax.broadcasted_iota(jnp.int32, (tm, GROUP_WIDTH), 0)
        u = cc_ref[...] * ch_ref[...]
        u_prev = jnp.where(i > 0, pcc_ref[...] * pch_ref[...], 0.0)
        z = (w_ref[0:1, :] * _shift_down(u, u_prev, 2, row) + w_ref[1:2, :] * _shift_down(u, u_prev, 1, row)
             + w_ref[2:3, :] * u)
        gm, gc, gs = gm_ref[...], gc_ref[...], gs_ref[...]
        y_ref[:, 0:512] = (om_ref[...] * (gm * _sigmoid(gm))).astype(MXU_DTYPE)
        y_ref[:, 512:1024] = (cb_ref[...] * z * (gc * _sigmoid(gc))).astype(MXU_DTYPE)
        y_ref[:, 1024:1536] = (os_ref[...] * (gs * _sigmoid(gs))).astype(MXU_DTYPE)

    blk = lambda cb: pl.BlockSpec((tm, 512), lambda i: (i, cb))
    prev = lambda cb: pl.BlockSpec((8, 512), lambda i: (jnp.maximum(i * (tm // 8) - 1, 0), cb))
    tile = pl.BlockSpec((tm, 512), lambda i: (i, 0))
    return pl.pallas_call(
        body, name="mix_fwd", grid=(T // tm,),
        in_specs=[blk(CB_GMLA), blk(CB_CH), blk(CB_CB), blk(CB_CC), blk(CB_GCONV), blk(CB_GSWA),
                  prev(CB_CH), prev(CB_CC), tile, tile, pl.BlockSpec((8, 512), lambda i: (0, 0))],
        out_specs=pl.BlockSpec((tm, D_MIX), lambda i: (i, 0)),
        out_shape=_sds((T, D_MIX), MXU_DTYPE),
        compiler_params=_cp(("parallel",), 32))(
            proj, proj, proj, proj, proj, proj, proj, proj, o_mla, o_swa, conv_w)


def _loss_grad(y, target):
    T, D = y.shape
    tm = min(TM_ROW, T)
    nt = T // tm

    def body(y_ref, t_ref, g_ref, loss_ref, acc_ref):
        i = pl.program_id(0)

        @pl.when(i == 0)
        def _():
            acc_ref[...] = jnp.zeros_like(acc_ref)

        err = y_ref[...] - t_ref[...]
        g_ref[...] = err * (1.0 / D)
        acc_ref[...] += _fold_rows8(err * err)

        @pl.when(i == nt - 1)
        def _():
            tot = jnp.sum(jnp.sum(acc_ref[...], axis=1, keepdims=True), axis=0, keepdims=True)
            loss_ref[...] = jnp.broadcast_to(tot * (0.5 / D), (8, LANES))

    return pl.pallas_call(
        body, name="loss_grad", grid=(nt,),
        in_specs=[pl.BlockSpec((tm, D), lambda i: (i, 0)), pl.BlockSpec((tm, D), lambda i: (i, 0))],
        out_specs=[pl.BlockSpec((tm, D), lambda i: (i, 0)), pl.BlockSpec((8, LANES), lambda i: (0, 0))],
        out_shape=[_sds((T, D), F32), _sds((8, LANES), F32)],
        scratch_shapes=[pltpu.VMEM((8, D), F32)],
        compiler_params=_cp(("arbitrary",), 32))(y, target)


def _mix_bwd(dycat, proj, o_mla, o_swa, conv_w):
    T = proj.shape[0]
    tm = min(TM_ROW, T)
    nt = T // tm

    def body(dym_ref, dyc_ref, dys_ref, gm_ref, ch_ref, cb_ref, cc_ref, gc_ref, gs_ref, pch_ref, pcc_ref,
             ndy_ref, ncb_ref, ngc_ref, om_ref, os_ref, w_ref,
             d1_ref, dgs_ref, dom_ref, dos_ref, dw_ref):
        i = pl.program_id(0)

        @pl.when(i == 0)
        def _():
            dw_ref[...] = jnp.zeros_like(dw_ref)

        row = lax.broadcasted_iota(jnp.int32, (tm, GROUP_WIDTH), 0)

        def gate(g):
            sg = _sigmoid(g)
            return g * sg, sg * (1.0 + g * (1.0 - sg))

        gm = gm_ref[...]
        silu, dsilu = gate(gm)
        dym = dym_ref[...]
        dom_ref[...] = dym * silu
        d1_ref[:, 0:512] = (dym * om_ref[...] * dsilu).astype(MXU_DTYPE)

        gs = gs_ref[...]
        silu, dsilu = gate(gs)
        dys = dys_ref[...]
        dos_ref[...] = dys * silu
        dgs_ref[...] = (dys * os_ref[...] * dsilu).astype(MXU_DTYPE)

        ch, cb, cc, gc, dyc = ch_ref[...], cb_ref[...], cc_ref[...], gc_ref[...], dyc_ref[...]
        w0, w1, w2 = w_ref[0:1, :], w_ref[1:2, :], w_ref[2:3, :]
        u = cc * ch
        u_prev = jnp.where(i > 0, pcc_ref[...] * pch_ref[...], 0.0)
        u1 = _shift_down(u, u_prev, 1, row)
        u2 = _shift_down(u, u_prev, 2, row)
        z = w0 * u2 + w1 * u1 + w2 * u
        silu, dsilu = gate(gc)
        dz = dyc * cb * silu
        ngc = ngc_ref[...]
        dz_next = jnp.where(i < nt - 1, ndy_ref[...] * ncb_ref[...] * (ngc * _sigmoid(ngc)), 0.0)
        du = w2 * dz + w1 * _shift_up(dz, dz_next, 1, row) + w0 * _shift_up(dz, dz_next, 2, row)
        d1_ref[:, 512:1024] = (du * cc).astype(MXU_DTYPE)
        d1_ref[:, 1024:1536] = (dyc * z * silu).astype(MXU_DTYPE)
        d1_ref[:, 1536:2048] = (du * ch).astype(MXU_DTYPE)
        d1_ref[:, 2048:2560] = (dyc * cb * z * dsilu).astype(MXU_DTYPE)
        row8 = lax.broadcasted_iota(jnp.int32, (8, GROUP_WIDTH), 0)
        dw = jnp.zeros((8, GROUP_WIDTH), F32)
        for t, shifted in enumerate((u2, u1, u)):
            dw = dw + jnp.where(row8 == t, jnp.sum(dz * shifted, axis=0, keepdims=True), 0.0)
        dw_ref[...] += dw

    blk = lambda cb: pl.BlockSpec((tm, 512), lambda i: (i, cb))
    prev = lambda cb: pl.BlockSpec((8, 512), lambda i: (jnp.maximum(i * (tm // 8) - 1, 0), cb))
    nxt = lambda cb: pl.BlockSpec((8, 512), lambda i: (jnp.minimum((i + 1) * (tm // 8), T // 8 - 1), cb))
    tile = pl.BlockSpec((tm, 512), lambda i: (i, 0))
    return pl.pallas_call(
        body, name="mix_bwd", grid=(nt,),
        in_specs=[blk(0), blk(1), blk(2), blk(CB_GMLA), blk(CB_CH), blk(CB_CB), blk(CB_CC), blk(CB_GCONV),
                  blk(CB_GSWA), prev(CB_CH), prev(CB_CC), nxt(1), nxt(CB_CB), nxt(CB_GCONV), tile, tile,
                  pl.BlockSpec((8, 512), lambda i: (0, 0))],
        out_specs=[pl.BlockSpec((tm, 2560), lambda i: (i, 0)), tile, tile, tile,
                   pl.BlockSpec((8, 512), lambda i: (0, 0))],
        out_shape=[_sds((T, 2560), MXU_DTYPE), _sds((T, 512), MXU_DTYPE), _sds((T, 512), F32),
                   _sds((T, 512), F32), _sds((8, 512), F32)],
        compiler_params=_cp(("arbitrary",), 48))(
            dycat, dycat, dycat, proj, proj, proj, proj, proj, proj, proj, proj, dycat, proj, proj,
            o_mla, o_swa, conv_w)


def _swa_bwd(proj, o_swa, do_swa, lw):
    T = proj.shape[0]
    tm = min(TM_SWA, T)
    nb = tm // BLOCK
    scale = SWA_HEAD_DIM ** -0.5

    def body(q_ref, k_ref, v_ref, pk_ref, pv_ref, o_ref, do_ref, qw_ref, kw_ref, sink_ref,
             dq_ref, dk_ref, dv_ref, dqw_ref, dsink_ref):
        i = pl.program_id(0)

        @pl.when(i == 0)
        def _():
            dk_ref[...] = jnp.zeros_like(dk_ref)
            dv_ref[...] = jnp.zeros_like(dv_ref)
            dqw_ref[...] = jnp.zeros_like(dqw_ref)
            dsink_ref[...] = jnp.zeros_like(dsink_ref)

        half1 = lax.broadcasted_iota(jnp.int32, (1, LANES), 1) >= 64
        k_all = jnp.concatenate([pk_ref[...], k_ref[...]], axis=0)
        v_all = jnp.concatenate([pv_ref[...], v_ref[...]], axis=0)
        khat, _ = _rms_halves(k_all, half1)
        kp = _swa_kv_variants(khat * kw_ref[...], half1)
        vp = _swa_kv_variants(v_all, half1)
        qw = qw_ref[...]
        dqw = jnp.zeros((1, LANES), F32)
        dsink_rows = [jnp.zeros((1, 1), F32) for _ in range(HEADS)]
        for j in range(4):
            g = j // 2
            cols = slice(LANES * j, LANES * (j + 1))
            qhat, qr = _rms_halves(q_ref[:, cols], half1)
            qn = (qhat * qw).astype(MXU_DTYPE)
            do = do_ref[:, cols]
            dob = do.astype(MXU_DTYPE)
            prod = do * o_ref[:, cols]
            dqn_blocks = []
            for b in range(nb):
                dist, valid = _swa_masks((i == 0) & (b == 0))
                ks = slice(b * BLOCK, b * BLOCK + 2 * BLOCK)
                rs = slice(b * BLOCK, (b + 1) * BLOCK)
                qb = qn[rs]
                dqn = jnp.zeros((BLOCK, LANES), F32)
                for r in range(2):
                    h = 2 * j + r
                    own = half1 if r else jnp.logical_not(half1)
                    s = _dot_nt(qb, kp[(g, r)][ks]) * scale - (2.0 ** -(h + 1)) * dist
                    s = jnp.where(valid, s, NEG_INF)
                    sink = sink_ref[h]
                    m = jnp.maximum(jnp.max(s, axis=-1, keepdims=True), sink)
                    e = jnp.exp(s - m)
                    es = jnp.exp(sink - m)
                    inv = 1.0 / (jnp.sum(e, axis=-1, keepdims=True) + es)
                    p = e * inv
                    dd = jnp.sum(jnp.where(own, prod[rs], 0.0), axis=-1, keepdims=True)
                    dp = _dot_nt(dob[rs], vp[(g, r)][ks])
                    ds = (p * (dp - dd) * scale).astype(MXU_DTYPE)
                    dsink_rows[h] = dsink_rows[h] - jnp.sum(es * inv * dd, axis=0, keepdims=True)
                    dqn = dqn + _dot(ds, kp[(g, r)][ks])
                    dkp = jnp.where(own, _dot_tn(ds, qb), 0.0)
                    dvp = jnp.where(own, _dot_tn(p.astype(MXU_DTYPE), dob[rs]), 0.0)
                    if g != r:
                        dkp = pltpu.roll(dkp, 64, 1)
                        dvp = pltpu.roll(dvp, 64, 1)
                    dst = pl.ds(pl.multiple_of((i * nb + b) * BLOCK, BLOCK), 2 * BLOCK)
                    dk_ref[dst, :] += dkp
                    dv_ref[dst, :] += dvp
                dqn_blocks.append(dqn)
            dqn = jnp.concatenate(dqn_blocks, axis=0) if nb > 1 else dqn_blocks[0]
            dqw = dqw + jnp.sum(dqn * qhat, axis=0, keepdims=True)
            dq_ref[:, cols] = _rms_halves_bwd(dqn, qhat, qr, qw, half1).astype(MXU_DTYPE)
        dqw_ref[...] += _row0(dqw + pltpu.roll(dqw, 64, 1))
        row8 = lax.broadcasted_iota(jnp.int32, (8, LANES), 0)
        dsink = jnp.zeros((8, LANES), F32)
        for h in range(HEADS):
            dsink = dsink + jnp.where(row8 == h, jnp.broadcast_to(dsink_rows[h], (8, LANES)), 0.0)
        dsink_ref[...] += dsink

    prev = lambda cb: pl.BlockSpec((BLOCK, LANES), lambda i: (jnp.maximum(i * nb - 1, 0), cb))
    tile = pl.BlockSpec((tm, 512), lambda i: (i, 0))
    small = pl.BlockSpec((8, LANES), lambda i: (0, 0))
    acc = pl.BlockSpec((T + BLOCK, LANES), lambda i: (0, 0))
    return pl.pallas_call(
        body, name="swa_bwd", grid=(T // tm,),
        in_specs=[pl.BlockSpec((tm, 512), lambda i: (i, CB_SQ)), pl.BlockSpec((tm, LANES), lambda i: (i, CB_SK)),
                  pl.BlockSpec((tm, LANES), lambda i: (i, CB_SV)), prev(CB_SK), prev(CB_SV), tile, tile,
                  pl.BlockSpec((1, LANES), lambda i: (0, 0)), pl.BlockSpec((1, LANES), lambda i: (0, 0)),
                  pl.BlockSpec(memory_space=pltpu.SMEM)],
        out_specs=[tile, acc, acc, small, small],
        out_shape=[_sds((T, 512), MXU_DTYPE), _sds((T + BLOCK, LANES), F32), _sds((T + BLOCK, LANES), F32),
                   _sds((8, LANES), F32), _sds((8, LANES), F32)],
        compiler_params=_cp(("arbitrary",), 40))(
            proj, proj, proj, proj, proj, o_swa, do_swa, lw["sqn"], lw["skn"], lw["sinks"])


def _swa_kv_bwd(proj, dkn, dv, lw):
    T = proj.shape[0]
    tm = BLOCK

    def body(k_ref, dkn_ref, dv_ref, kw_ref, d_ref, dkw_ref):
        i = pl.program_id(0)

        @pl.when(i == 0)
        def _():
            dkw_ref[...] = jnp.zeros_like(dkw_ref)

        half1 = lax.broadcasted_iota(jnp.int32, (1, LANES), 1) >= 64
        khat, kr = _rms_halves(k_ref[...], half1)
        dkn_t = dkn_ref[...]
        dkw = jnp.sum(dkn_t * khat, axis=0, keepdims=True)
        dkw_ref[...] += _row0(dkw + pltpu.roll(dkw, 64, 1))
        d_ref[:, 0:LANES] = _rms_halves_bwd(dkn_t, khat, kr, kw_ref[...], half1).astype(MXU_DTYPE)
        d_ref[:, LANES:2 * LANES] = dv_ref[...].astype(MXU_DTYPE)

    return pl.pallas_call(
        body, name="swa_kv_bwd", grid=(T // tm,),
        in_specs=[pl.BlockSpec((tm, LANES), lambda i: (i, CB_SK)), pl.BlockSpec((tm, LANES), lambda i: (i + 1, 0)),
                  pl.BlockSpec((tm, LANES), lambda i: (i + 1, 0)), pl.BlockSpec((1, LANES), lambda i: (0, 0))],
        out_specs=[pl.BlockSpec((tm, 2 * LANES), lambda i: (i, 0)), pl.BlockSpec((8, LANES), lambda i: (0, 0))],
        out_shape=[_sds((T, 2 * LANES), MXU_DTYPE), _sds((8, LANES), F32)],
        compiler_params=_cp(("arbitrary",), 32))(proj, dkn, dv, lw["skn"])


def _mla_attn_bwd(q, k, kt, vpad, o, do, lse):
    T = q.shape[1]
    tk = min(TK, T // 2)
    tq = 2 * tk

    def body(q_ref, k_ref, kt_ref, v_ref, o_ref, do_ref, lse_ref, dq_ref, dk_ref, dv_ref, dqt_s,
             s_a, s_b, p_a, p_b):
        h = pl.program_id(0)
        i = pl.program_id(1)

        @pl.when(i == 0)
        def _():
            dk_ref[...] = jnp.zeros_like(dk_ref)
            dv_ref[...] = jnp.zeros_like(dv_ref)

        key = lax.broadcasted_iota(jnp.int32, (tk, tq), 0)
        qry = lax.broadcasted_iota(jnp.int32, (tk, tq), 1)
        own = (lax.broadcasted_iota(jnp.int32, (1, LANES), 1) // 64) == (h % 2)
        own_rows = (lax.broadcasted_iota(jnp.int32, (LANES, 1), 0) // 64) == (h % 2)
        do_t = do_ref[...]
        dob = do_t.astype(MXU_DTYPE)
        prod_t = (do_t * o_ref[...]).T
        dd = jnp.sum(jnp.where(own_rows, prod_t, 0.0), axis=0, keepdims=True)
        qh = q_ref[0]
        lse_t = lse_ref[0]
        dqt_s[...] = jnp.zeros_like(dqt_s)

        def scores(kj, s_buf, p_buf):
            rows = pl.ds(pl.multiple_of(kj * tk, tk), tk)
            s_buf[...] = _dot_nt(k_ref[0, rows, :], qh)
            p_buf[...] = _dot_nt(v_ref[0, rows, :], dob)

        def consume(kj, s_buf, p_buf, diag):
            rows = pl.ds(pl.multiple_of(kj * tk, tk), tk)
            s = s_buf[...]
            if diag is not None:
                s = jnp.where(key + diag * tk <= qry, s, NEG_INF)
            p = jnp.exp2(s - lse_t)
            ds = (p * (p_buf[...] - dd)).astype(MXU_DTYPE)
            dqt_s[...] += _dot(kt_ref[0, kj], ds)
            dk_ref[0, rows, :] += _dot(ds, qh)
            dv_ref[0, rows, :] += jnp.where(own, _dot(p.astype(MXU_DTYPE), dob), 0.0)

        scores(0, s_a, p_a)

        def pair(kp, carry):
            kj = 2 * kp
            scores(kj + 1, s_b, p_b)
            consume(kj, s_a, p_a, None)
            scores(kj + 2, s_a, p_a)
            consume(kj + 1, s_b, p_b, None)
            return carry

        lax.fori_loop(0, i, pair, 0)
        scores(2 * i + 1, s_b, p_b)
        consume(2 * i, s_a, p_a, 0)
        consume(2 * i + 1, s_b, p_b, 1)
        dq_ref[0] = dqt_s[...].T

    res = pl.BlockSpec((1, T, LANES), lambda h, i: (h, 0, 0))
    buf = pltpu.VMEM((tk, tq), F32)
    return pl.pallas_call(
        body, name="mla_attn_bwd", grid=(HEADS, T // tq),
        in_specs=[pl.BlockSpec((1, tq, LANES), lambda h, i: (h, i, 0)), res,
                  pl.BlockSpec((1, T // tk, LANES, tk), lambda h, i: (h, 0, 0, 0)), res,
                  pl.BlockSpec((tq, LANES), lambda h, i: (i, h // 2)),
                  pl.BlockSpec((tq, LANES), lambda h, i: (i, h // 2)),
                  pl.BlockSpec((1, 1, tq), lambda h, i: (h, 0, i))],
        out_specs=[pl.BlockSpec((1, tq, LANES), lambda h, i: (h, i, 0)), res, res],
        out_shape=[_sds((HEADS, T, LANES), F32)] * 3,
        scratch_shapes=[pltpu.VMEM((LANES, tq), F32), buf, buf, buf, buf],
        compiler_params=_cp(("parallel", "arbitrary"), 48))(q, k, kt, vpad, o, do, lse)


def _mla_prep_bwd(proj, dq, dk, dv, lw, rope):
    T = proj.shape[0]
    tm = min(TM_ROW, T)

    def body(ql_ref, kvl_ref, kr_ref, dq_ref, dk_ref, dv_ref, qa_ref, kva_ref, wq_ref, wk_ref, wv_ref,
             wqt_ref, wkt_ref, wvt_ref, qn_ref, kn_ref, c_ref, s1_ref, s2_ref,
             d_ref, dwq_ref, dwk_ref, dwv_ref, dqa_ref, dkva_ref, dqn_ref, dkn_ref):
        i = pl.program_id(0)

        @pl.when(i == 0)
        def _():
            for ref in (dwq_ref, dwk_ref, dwv_ref, dqa_ref, dkva_ref, dqn_ref, dkn_ref):
                ref[...] = jnp.zeros_like(ref)

        c, s1, s2 = c_ref[...], s1_ref[...], s2_ref[...]
        lane = lax.broadcasted_iota(jnp.int32, (1, LANES), 1)
        qlhat, qlr = _rms(ql_ref[...], MLA_Q_LORA)
        qn = (qlhat * qa_ref[...]).astype(MXU_DTYPE)
        kvhat, kvr = _rms(kvl_ref[...], MLA_KV_LORA)
        kvn = (kvhat * kva_ref[...]).astype(MXU_DTYPE)
        kr = kr_ref[...]
        dqnl = jnp.zeros((tm, MLA_Q_LORA), F32)
        dkvn = jnp.zeros((tm, MLA_KV_LORA), F32)
        dkr = jnp.zeros((tm, LANES), F32)
        dqw = jnp.zeros((1, LANES), F32)
        dkw = jnp.zeros((1, LANES), F32)
        for h in range(HEADS):
            xh, r = _rms(_dot(qn, wq_ref[h]), MLA_QK)
            dy = _rope_bwd(dq_ref[h] * MLA_SCALE, c, s1, s2)
            dqw = dqw + jnp.sum(dy * xh, axis=0, keepdims=True)
            dx = _rms_bwd(dy, xh, r, qn_ref[...], MLA_QK).astype(MXU_DTYPE)
            dwq_ref[h] += _dot_tn(qn, dx)
            dqnl = dqnl + _dot(dx, wqt_ref[h])

            xh, r = _rms(_dot(kvn, wk_ref[h]) + kr, MLA_QK)
            dy = _rope_bwd(dk_ref[h] * LN2, c, s1, s2)
            dkw = dkw + jnp.sum(dy * xh, axis=0, keepdims=True)
            dxf = _rms_bwd(dy, xh, r, kn_ref[...], MLA_QK)
            dkr = dkr + dxf
            dx = dxf.astype(MXU_DTYPE)
            dwk_ref[h] += _dot_tn(kvn, dx)
            dkvn = dkvn + _dot(dx, wkt_ref[h])
        dvc = jnp.concatenate([dv_ref[2 * j] + dv_ref[2 * j + 1] for j in range(4)], axis=1).astype(MXU_DTYPE)
        dwv_ref[...] += _dot_tn(kvn, dvc)
        dkvn = dkvn + _dot(dvc, wvt_ref[...])
        dqa_ref[...] += _row0(jnp.sum(dqnl * qlhat, axis=0, keepdims=True))
        dkva_ref[...] += _row0(jnp.sum(dkvn * kvhat, axis=0, keepdims=True))
        dqn_ref[...] += _row0(dqw)
        dkn_ref[...] += _row0(dkw)
        d_ref[:, 0:256] = _rms_bwd(dqnl, qlhat, qlr, qa_ref[...], MLA_Q_LORA).astype(MXU_DTYPE)
        d_ref[:, 256:384] = _rms_bwd(dkvn, kvhat, kvr, kva_ref[...], MLA_KV_LORA).astype(MXU_DTYPE)
        d_ref[:, 384:512] = jnp.where((lane >= 64) & (lane < 96), dkr, 0.0).astype(MXU_DTYPE)

    full = lambda shape: pl.BlockSpec(shape, lambda i: (0,) * len(shape))
    hd = pl.BlockSpec((HEADS, tm, LANES), lambda i: (0, i, 0))
    tab = pl.BlockSpec((tm, LANES), lambda i: (i, 0))
    return pl.pallas_call(
        body, name="mla_prep_bwd", grid=(T // tm,),
        in_specs=[pl.BlockSpec((tm, 256), lambda i: (i, CB_QLAT)), pl.BlockSpec((tm, LANES), lambda i: (i, CB_KVLAT)),
                  pl.BlockSpec((tm, LANES), lambda i: (i, CB_KROPE)), hd, hd, hd,
                  full((1, 256)), full((1, LANES)), full((HEADS, 256, LANES)), full((HEADS, LANES, LANES)),
                  full((LANES, 512)), full((HEADS, LANES, 256)), full((HEADS, LANES, LANES)), full((512, LANES)),
                  full((1, LANES)), full((1, LANES)), tab, tab, tab],
        out_specs=[pl.BlockSpec((tm, 512), lambda i: (i, 0)), full((HEADS, 256, LANES)),
                   full((HEADS, LANES, LANES)), full((LANES, 512)), full((8, 256)), full((8, LANES)),
                   full((8, LANES)), full((8, LANES))],
        out_shape=[_sds((T, 512), MXU_DTYPE), _sds((HEADS, 256, LANES), F32), _sds((HEADS, LANES, LANES), F32),
                   _sds((LANES, 512), F32), _sds((8, 256), F32), _sds((8, LANES), F32), _sds((8, LANES), F32),
                   _sds((8, LANES), F32)],
        compiler_params=_cp(("arbitrary",), 48))(
            proj, proj, proj, dq, dk, dv, lw["qa"], lw["kva"], lw["wq"], lw["wk"], lw["wv"],
            lw["wqt"], lw["wkt"], lw["wvt"], lw["qn"], lw["kn"], rope[0], rope[1], rope[2])


def _norm_bwd(dh, x, g_in, ng):
    T, D = x.shape
    tm = min(TM_ROW, T)

    def body(dh_ref, x_ref, g_ref, w_ref, dx_ref, dw_ref):
        i = pl.program_id(0)

        @pl.when(i == 0)
        def _():
            dw_ref[...] = jnp.zeros_like(dw_ref)

        xhat, r = _rms(x_ref[...], D)
        dh_t = dh_ref[...]
        dw_ref[...] += _row0(jnp.sum(dh_t * xhat, axis=0, keepdims=True))
        dx_ref[...] = g_ref[...] + _rms_bwd(dh_t, xhat, r, w_ref[...], D)

    tile = pl.BlockSpec((tm, D), lambda i: (i, 0))
    return pl.pallas_call(
        body, name="norm_bwd", grid=(T // tm,),
        in_specs=[tile, tile, tile, pl.BlockSpec((1, D), lambda i: (0, 0))],
        out_specs=[tile, pl.BlockSpec((8, D), lambda i: (0, 0))],
        out_shape=[_sds((T, D), F32), _sds((8, D), F32)],
        compiler_params=_cp(("arbitrary",), 32))(dh, x, g_in, ng)


def _rope_tables(T):
    half = MLA_ROPE // 2
    inv_freq = jnp.power(jnp.float32(ROPE_THETA), -jnp.arange(half, dtype=F32) / half)
    ang = jnp.arange(T, dtype=F32)[:, None] * inv_freq[None, :]
    cos, sin = jnp.cos(ang), jnp.sin(ang)
    z = lambda n: jnp.zeros((T, n), F32)
    c = jnp.concatenate([jnp.ones((T, MLA_NOPE), F32), cos, cos, z(32)], axis=1)
    s1 = jnp.concatenate([z(64), -sin, z(48)], axis=1)
    s2 = jnp.concatenate([z(80), sin, z(32)], axis=1)
    return c, s1, s2


def _pad_lanes(v, n=LANES):
    v = v.reshape(1, -1)
    return jnp.pad(v, ((0, 0), (0, n - v.shape[1])))


def _pack_win(w):
    z = lambda n: jnp.zeros((w.shape[0], n), w.dtype)
    return jnp.concatenate([w[:, 0:384], z(64), w[:, 384:416], z(32), w[:, 416:2976], w[:, 2976:3488],
                            w[:, 3744:4256], w[:, 3488:3616], w[:, 3616:3744]], axis=1)


def _unpack_dwin(d):
    return jnp.concatenate([d[:, 0:384], d[:, 448:480], d[:, 512:3072], d[:, 3072:3584], d[:, 4096:4224],
                            d[:, 4224:4352], d[:, 3584:4096]], axis=1)


def _layer_weights(l, norm_g, w_in_full, qa, wqb_full, kva, wkvb_full, qn, kn, conv_full, sqn, skn, sinks,
                   w_out_full):
    wp = _pack_win(w_in_full)
    wq = jnp.pad(wqb_full, ((0, 0), (0, 0), (0, LANES - MLA_QK)))
    wk = jnp.pad(wkvb_full[:, :, :MLA_NOPE], ((0, 0), (0, 0), (0, LANES - MLA_NOPE)))
    wv = jnp.transpose(wkvb_full[:, :, MLA_NOPE:], (1, 0, 2)).reshape(MLA_KV_LORA, GROUP_WIDTH)
    return dict(
        ng=norm_g[l].reshape(1, -1), wp=wp, wpt=wp.T, qa=qa[l].reshape(1, -1), kva=kva[l].reshape(1, -1),
        wq=wq, wk=wk, wv=wv, wqt=jnp.transpose(wq, (0, 2, 1)), wkt=jnp.transpose(wk, (0, 2, 1)), wvt=wv.T,
        qn=_pad_lanes(qn[l]), kn=_pad_lanes(kn[l]),
        conv=jnp.pad(conv_full, ((0, 5), (0, 0))),
        sqn=jnp.tile(sqn[l].reshape(1, -1), (1, 2)), skn=jnp.tile(skn[l].reshape(1, -1), (1, 2)),
        sinks=sinks[l], wo=w_out_full, wot=w_out_full.T)


def _layer_fwd(x, lw, rope):
    proj, h = _inproj_fwd(x, lw["ng"], lw["wp"])
    q, k, kt, vpad, vt = _mla_prep_fwd(proj, lw, rope)
    o_mla, lse = _mla_attn_fwd(q, k, vt)
    o_swa = _swa_fwd(proj, lw)
    ycat = _mix_fwd(proj, o_mla, o_swa, lw["conv"])
    x_next = _mm_nn(ycat, lw["wo"], "outproj_fwd", residual=x)
    return x_next, dict(x=x, proj=proj, h=h, q=q, k=k, kt=kt, vpad=vpad, o_mla=o_mla, lse=lse, o_swa=o_swa, ycat=ycat)


def _layer_bwd(g, sv, lw, rope):
    proj = sv["proj"]
    dycat = _mm_nn(g, lw["wot"], "outproj_bwd_dy")
    d_wo = _mm_tn(sv["ycat"], g, "outproj_bwd_dw", WIRE_DTYPE, tn=D_MODEL)
    d1, dgs, do_mla, do_swa, d_conv = _mix_bwd(dycat, proj, sv["o_mla"], sv["o_swa"], lw["conv"])
    dsq, dkn_acc, dv_acc, d_sqn, d_sinks = _swa_bwd(proj, sv["o_swa"], do_swa, lw)
    dskv, d_skn = _swa_kv_bwd(proj, dkn_acc, dv_acc, lw)
    dq, dk, dv = _mla_attn_bwd(sv["q"], sv["k"], sv["kt"], sv["vpad"], sv["o_mla"], do_mla, sv["lse"])
    dmla, d_wq, d_wk, d_wv, d_qa, d_kva, d_qn, d_kn = _mla_prep_bwd(proj, dq, dk, dv, lw, rope)
    dproj = jnp.concatenate([dmla, d1, dsq, dgs, dskv], axis=1)
    dh = _mm_nn(dproj, lw["wpt"], "inproj_bwd_dh")
    dx, d_ng = _norm_bwd(dh, sv["x"], g, lw["ng"])
    d_wp = _mm_tn(sv["h"], dproj, "inproj_bwd_dw", WIRE_DTYPE, tn=NP // 2)
    grads = dict(
        w_in=_unpack_dwin(d_wp), w_out=d_wo,
        w_qb=d_wq[:, :, :MLA_QK],
        w_kvb=jnp.concatenate([d_wk[:, :, :MLA_NOPE],
                               jnp.transpose(d_wv.reshape(MLA_KV_LORA, HEADS, MLA_NOPE), (1, 0, 2))], axis=2),
        conv=d_conv[0:3], norm_g=d_ng[0], qa=d_qa[0], kva=d_kva[0], qn=d_qn[0, :MLA_QK], kn=d_kn[0, :MLA_QK],
        sqn=d_sqn[0, :SWA_HEAD_DIM], skn=d_skn[0, :SWA_HEAD_DIM], sinks=d_sinks[:, 0])
    return dx, grads


def _local_step(x, target, lws, rope):
    saved = []
    for lw in lws:
        x, sv = _layer_fwd(x, lw, rope)
        saved.append(sv)
    g, loss_tile = _loss_grad(x, target)
    grads = [None] * len(lws)
    for l in reversed(range(len(lws))):
        g, grads[l] = _layer_bwd(g, saved[l], lws[l], rope)
    return loss_tile, g, grads


def _my_coords():
    return lax.axis_index("x"), lax.axis_index("y"), lax.axis_index("c")


def _peer(me, k):
    x, y, c = me
    return (1 - x if k & 4 else x, 1 - y if k & 2 else y, 1 - c if k & 1 else c)


def _lin(d):
    return 4 * d[0] + 2 * d[1] + d[2]


def _all_gather(shards):
    n = len(shards)

    def body(*refs):
        ins, outs = refs[:n], refs[n:2 * n]
        send_sems, recv_sems, local_sems = refs[2 * n:]
        me = _my_coords()
        my = _lin(me)
        local = [pltpu.make_async_copy(ins[a], outs[a].at[my], local_sems.at[a]) for a in range(n)]
        for cp in local:
            cp.start()
        sends = []
        for a in range(n):
            for k in range(1, N_DEV):
                cp = pltpu.make_async_remote_copy(
                    src_ref=ins[a], dst_ref=outs[a].at[my], send_sem=send_sems.at[a * 7 + k - 1],
                    recv_sem=recv_sems.at[a * 7 + k - 1], device_id=_peer(me, k),
                    device_id_type=pl.DeviceIdType.MESH)
                cp.start()
                sends.append(cp)
        for a in range(n):
            for k in range(1, N_DEV):
                src = _lin(_peer(me, k))
                pltpu.make_async_remote_copy(
                    src_ref=ins[a], dst_ref=outs[a].at[src], send_sem=send_sems.at[a * 7 + k - 1],
                    recv_sem=recv_sems.at[a * 7 + k - 1], device_id=_peer(me, k),
                    device_id_type=pl.DeviceIdType.MESH).wait_recv()
        for cp in sends:
            cp.wait_send()
        for cp in local:
            cp.wait()

    any_spec = pl.BlockSpec(memory_space=pl.ANY)
    return pl.pallas_call(
        body, name="weight_all_gather",
        in_specs=[any_spec] * n, out_specs=[any_spec] * n,
        out_shape=[_sds((N_DEV,) + s.shape, s.dtype) for s in shards],
        scratch_shapes=[pltpu.SemaphoreType.DMA((7 * n,)), pltpu.SemaphoreType.DMA((7 * n,)),
                        pltpu.SemaphoreType.DMA((n,))],
    )(*shards)


def _grad_exchange(slots):
    n = len(slots)

    def body(*refs):
        ins, outs = refs[:n], refs[n:2 * n]
        send_sems, recv_sems, local_sems = refs[2 * n:]
        me = _my_coords()
        my = _lin(me)
        local = [pltpu.make_async_copy(ins[a].at[my], outs[a].at[my], local_sems.at[a]) for a in range(n)]
        for cp in local:
            cp.start()
        sends = []
        for a in range(n):
            for k in range(1, N_DEV):
                peer = _peer(me, k)
                cp = pltpu.make_async_remote_copy(
                    src_ref=ins[a].at[_lin(peer)], dst_ref=outs[a].at[my], send_sem=send_sems.at[a * 7 + k - 1],
                    recv_sem=recv_sems.at[a * 7 + k - 1], device_id=peer, device_id_type=pl.DeviceIdType.MESH)
                cp.start()
                sends.append(cp)
        for a in range(n):
            for k in range(1, N_DEV):
                peer = _peer(me, k)
                pltpu.make_async_remote_copy(
                    src_ref=ins[a].at[my], dst_ref=outs[a].at[_lin(peer)], send_sem=send_sems.at[a * 7 + k - 1],
                    recv_sem=recv_sems.at[a * 7 + k - 1], device_id=peer,
                    device_id_type=pl.DeviceIdType.MESH).wait_recv()
        for cp in sends:
            cp.wait_send()
        for cp in local:
            cp.wait()

    any_spec = pl.BlockSpec(memory_space=pl.ANY)
    return pl.pallas_call(
        body, name="grad_exchange",
        in_specs=[any_spec] * n, out_specs=[any_spec] * n,
        out_shape=[_sds(s.shape, s.dtype) for s in slots],
        scratch_shapes=[pltpu.SemaphoreType.DMA((7 * n,)), pltpu.SemaphoreType.DMA((7 * n,)),
                        pltpu.SemaphoreType.DMA((n,))],
    )(*slots)


def _small_all_reduce(v):
    R = v.shape[0]

    def body(v_ref, o_ref, buf, send_sems, recv_sems):
        me = _my_coords()
        my = _lin(me)
        sends = []
        for k in range(1, N_DEV):
            cp = pltpu.make_async_remote_copy(
                src_ref=v_ref, dst_ref=buf.at[my], send_sem=send_sems.at[k - 1], recv_sem=recv_sems.at[k - 1],
                device_id=_peer(me, k), device_id_type=pl.DeviceIdType.MESH)
            cp.start()
            sends.append(cp)
        buf[my] = v_ref[...]
        for k in range(1, N_DEV):
            pltpu.make_async_remote_copy(
                src_ref=v_ref, dst_ref=buf.at[_lin(_peer(me, k))], send_sem=send_sems.at[k - 1],
                recv_sem=recv_sems.at[k - 1], device_id=_peer(me, k),
                device_id_type=pl.DeviceIdType.MESH).wait_recv()
        for cp in sends:
            cp.wait_send()
        tot = buf[0]
        for d in range(1, N_DEV):
            tot = tot + buf[d]
        o_ref[...] = tot

    vm = pl.BlockSpec(memory_space=pltpu.VMEM)
    return pl.pallas_call(
        body, name="small_all_reduce", in_specs=[vm], out_specs=vm, out_shape=_sds(v.shape, F32),
        scratch_shapes=[pltpu.VMEM((N_DEV, R, LANES), F32), pltpu.SemaphoreType.DMA((7,)),
                        pltpu.SemaphoreType.DMA((7,))],
    )(v)


def _adamw_math(w, g, m, v):
    m = ADAM_B1 * m + (1.0 - ADAM_B1) * g
    v = ADAM_B2 * v + (1.0 - ADAM_B2) * (g * g)
    m_hat = m / (1.0 - ADAM_B1 ** ADAM_STEP)
    v_hat = v / (1.0 - ADAM_B2 ** ADAM_STEP)
    delta = -ADAM_LR * (m_hat / (jnp.sqrt(v_hat) + ADAM_EPS) + ADAM_WD * w)
    return delta, m, v


def _adamw(parts, w, m, v, name, tr):
    P, R, C = parts.shape
    tr = min(tr, R)

    def body(p_ref, w_ref, m_ref, v_ref, g_out, d_out, m_out, v_out):
        g = p_ref[0].astype(F32)
        for d in range(1, P):
            g = g + p_ref[d].astype(F32)
        delta, m_new, v_new = _adamw_math(w_ref[...], g, m_ref[...], v_ref[...])
        g_out[...] = g
        d_out[...] = delta
        m_out[...] = m_new
        v_out[...] = v_new

    tile = pl.BlockSpec((tr, C), lambda i: (i, 0))
    return pl.pallas_call(
        body, name=name, grid=(R // tr,),
        in_specs=[pl.BlockSpec((P, tr, C), lambda i: (0, i, 0)), tile, tile, tile],
        out_specs=[tile] * 4, out_shape=[_sds((R, C), F32)] * 4,
        compiler_params=_cp(("parallel",), 32))(parts, w, m, v)


SMALL = (("norm_g", D_MODEL), ("mla_q_a_norm", MLA_Q_LORA), ("mla_kv_a_norm", MLA_KV_LORA), ("mla_q_norm", MLA_QK),
         ("mla_k_norm", MLA_QK), ("swa_q_norm", SWA_HEAD_DIM), ("swa_k_norm", SWA_HEAD_DIM), ("swa_sinks", HEADS))
SMALL_GRAD_KEY = dict(norm_g="norm_g", mla_q_a_norm="qa", mla_kv_a_norm="kva", mla_q_norm="qn", mla_k_norm="kn",
                      swa_q_norm="sqn", swa_k_norm="skn", swa_sinks="sinks")
SMALL_ROWS = 32
CONV_ROWS = 24


def _pack_small(get):
    parts = []
    for l in range(DEPTH):
        for name, n in SMALL:
            v = get(name, l).reshape(-1)
            parts.append(jnp.pad(v, (0, (-n) % LANES)))
    return jnp.concatenate(parts).reshape(SMALL_ROWS, LANES)


def _unpack_small(packed):
    flat = packed.reshape(-1)
    out = {name: [] for name, _ in SMALL}
    off = 0
    for l in range(DEPTH):
        for name, n in SMALL:
            out[name].append(flat[off:off + n])
            off += n + (-n) % LANES
    return {name: jnp.stack(v) for name, v in out.items()}


def kernel(x, norm_g, w_in, mla_q_a_norm, mla_w_qb, mla_kv_a_norm, mla_w_kvb, mla_q_norm, mla_k_norm, conv_w, swa_q_norm, swa_k_norm, swa_sinks, w_out, loss_target, m_norm_g, m_w_in, m_mla_q_a_norm, m_mla_w_qb, m_mla_kv_a_norm, m_mla_w_kvb, m_mla_q_norm, m_mla_k_norm, m_conv_w, m_swa_q_norm, m_swa_k_norm, m_swa_sinks, m_w_out, v_norm_g, v_w_in, v_mla_q_a_norm, v_mla_w_qb, v_mla_kv_a_norm, v_mla_w_kvb, v_mla_q_norm, v_mla_k_norm, v_conv_w, v_swa_q_norm, v_swa_k_norm, v_swa_sinks, v_w_out):
    T = x.shape[1]
    weights = dict(norm_g=norm_g, w_in=w_in, mla_q_a_norm=mla_q_a_norm, mla_w_qb=mla_w_qb,
                   mla_kv_a_norm=mla_kv_a_norm, mla_w_kvb=mla_w_kvb, mla_q_norm=mla_q_norm, mla_k_norm=mla_k_norm,
                   conv_w=conv_w, swa_q_norm=swa_q_norm, swa_k_norm=swa_k_norm, swa_sinks=swa_sinks, w_out=w_out)
    mom_m = dict(norm_g=m_norm_g, w_in=m_w_in, mla_q_a_norm=m_mla_q_a_norm, mla_w_qb=m_mla_w_qb,
                 mla_kv_a_norm=m_mla_kv_a_norm, mla_w_kvb=m_mla_w_kvb, mla_q_norm=m_mla_q_norm,
                 mla_k_norm=m_mla_k_norm, conv_w=m_conv_w, swa_q_norm=m_swa_q_norm, swa_k_norm=m_swa_k_norm,
                 swa_sinks=m_swa_sinks, w_out=m_w_out)
    mom_v = dict(norm_g=v_norm_g, w_in=v_w_in, mla_q_a_norm=v_mla_q_a_norm, mla_w_qb=v_mla_w_qb,
                 mla_kv_a_norm=v_mla_kv_a_norm, mla_w_kvb=v_mla_w_kvb, mla_q_norm=v_mla_q_norm,
                 mla_k_norm=v_mla_k_norm, conv_w=v_conv_w, swa_q_norm=v_swa_q_norm, swa_k_norm=v_swa_k_norm,
                 swa_sinks=v_swa_sinks, w_out=v_w_out)

    g_win, g_wqb, g_wkvb, g_wout, g_conv = _all_gather([
        w_in.astype(MXU_DTYPE), mla_w_qb.astype(MXU_DTYPE), mla_w_kvb.astype(MXU_DTYPE), w_out.astype(MXU_DTYPE),
        conv_w])
    lws = []
    for l in range(DEPTH):
        w_in_full = jnp.transpose(g_win[:, l], (1, 0, 2)).reshape(D_MODEL, IN_COLS)
        conv_full = jnp.transpose(g_conv[:, l], (1, 0, 2)).reshape(3, GROUP_WIDTH)
        lws.append(_layer_weights(l, norm_g, w_in_full, mla_q_a_norm, g_wqb[:, l], mla_kv_a_norm, g_wkvb[:, l],
                                  mla_q_norm, mla_k_norm, conv_full, swa_q_norm, swa_k_norm, swa_sinks,
                                  g_wout[:, l].reshape(D_MIX, D_MODEL)))

    loss_tile, grad_x, grads = _local_step(x[0], loss_target[0], lws, _rope_tables(T))

    big_slots = [
        jnp.stack([jnp.transpose(g["w_in"].reshape(D_MODEL, N_DEV, IN_COLS // N_DEV), (1, 0, 2)) for g in grads], 1),
        jnp.stack([g["w_out"].reshape(N_DEV, D_MIX // N_DEV, D_MODEL) for g in grads], 1),
        jnp.stack([g["w_qb"] for g in grads], 1),
        jnp.stack([g["w_kvb"] for g in grads], 1),
    ]
    r_win, r_wout, r_wqb, r_wkvb = _grad_exchange(big_slots)

    small = jnp.concatenate([
        _pack_small(lambda name, l: grads[l][SMALL_GRAD_KEY[name]]),
        jnp.stack([g["conv"] for g in grads]).reshape(CONV_ROWS, LANES),
        loss_tile], axis=0)
    small = _small_all_reduce(small)
    loss = small[SMALL_ROWS + CONV_ROWS, 0]
    my = _lin(_my_coords())
    conv_g = lax.dynamic_slice_in_dim(small[SMALL_ROWS:SMALL_ROWS + CONV_ROWS].reshape(DEPTH, 3, GROUP_WIDTH),
                                      my * 64, 64, axis=2)

    out = {}

    def big(name, recv, rows, cols, tr):
        res = _adamw(recv.reshape(N_DEV, rows, cols), weights[name].reshape(rows, cols),
                     mom_m[name].reshape(rows, cols), mom_v[name].reshape(rows, cols), "adamw_" + name, tr)
        out[name] = [r.reshape(weights[name].shape) for r in res]

    big("w_in", r_win, DEPTH * D_MODEL, IN_COLS // N_DEV, 256)
    big("w_out", r_wout, DEPTH * D_MIX // N_DEV, D_MODEL, 192)
    big("mla_w_qb", r_wqb, DEPTH * MLA_Q_LORA, MLA_QK, 512)
    big("mla_w_kvb", r_wkvb, DEPTH * MLA_KV_LORA, 128, 256)

    pad_conv = lambda a: jnp.pad(a.reshape(-1), (0, 8 * LANES - 6 * 64)).reshape(8, LANES)
    cat = lambda src: jnp.concatenate([_pack_small(lambda name, l: src[name][l]), pad_conv(src["conv_w"])], axis=0)
    g_small = jnp.concatenate([small[:SMALL_ROWS], pad_conv(conv_g)], axis=0)
    res = _adamw(g_small[None], cat(weights), cat(mom_m), cat(mom_v), "adamw_small", SMALL_ROWS + 8)
    smalls = [_unpack_small(r[:SMALL_ROWS]) for r in res]
    for name, _ in SMALL:
        out[name] = [s[name] for s in smalls]
    out["conv_w"] = [r[SMALL_ROWS:].reshape(-1)[:6 * 64].reshape(DEPTH, 3, 64) for r in res]

    order = ["norm_g", "w_in", "mla_q_a_norm", "mla_w_qb", "mla_kv_a_norm", "mla_w_kvb", "mla_q_norm", "mla_k_norm",
             "conv_w", "swa_q_norm", "swa_k_norm", "swa_sinks", "w_out"]
    result = [loss, grad_x[None]]
    for idx in range(4):
        result += [out[name][idx] for name in order]
    return tuple(result)
```

```python
import functools

import jax
import jax.numpy as jnp
import numpy as np
from jax import lax
from jax.experimental import pallas as pl
from jax.experimental.pallas import tpu as pltpu

F32 = jnp.float32
MXU_DTYPE = jnp.bfloat16
WIRE_DTYPE = jnp.bfloat16

N_DEV = 8
DEPTH = 2
D_MODEL = 1024
GROUP_WIDTH = 512
D_MIX = 3 * GROUP_WIDTH
BLOCK = 128
RMS_EPS = 1e-6
NEG_INF = -1e30
HEADS = 8
MLA_QK = 96
MLA_NOPE = 64
MLA_ROPE = 32
MLA_Q_LORA = 256
MLA_KV_LORA = 128
ROPE_THETA = 10000.0
SWA_HEAD_DIM = 64
LANES = 128
IN_COLS = 4256

ADAM_LR = 0.001
ADAM_B1 = 0.9
ADAM_B2 = 0.999
ADAM_EPS = 1e-08
ADAM_WD = 0.01
ADAM_STEP = 10

NP = 4352
CB_QLAT = 0
CB_KVLAT = 2
CB_KROPE = 3
CB_GMLA, CB_CH, CB_CB, CB_CC, CB_GCONV, CB_SQ, CB_GSWA = 1, 2, 3, 4, 5, 6, 7
CB_SK, CB_SV = 32, 33

TM_PROJ = 256
TM_ROW = 256
TK = 256
TQ = 2 * TK
MLA_SCALE = MLA_QK ** -0.5
LOG2E = 1.4426950408889634
LN2 = 0.6931471805599453
TM_SWA = 256
VMEM_MB = 2 ** 20


def _cp(sem, vmem_mb):
    return pltpu.CompilerParams(dimension_semantics=sem, vmem_limit_bytes=vmem_mb * VMEM_MB)


def _sds(shape, dtype):
    return jax.ShapeDtypeStruct(shape, dtype)


def _dot(a, b):
    return jnp.dot(a, b, preferred_element_type=F32)


def _dot_nt(a, b):
    return lax.dot_general(a, b, (((1,), (1,)), ((), ())), preferred_element_type=F32)


def _dot_tn(a, b):
    return lax.dot_general(a, b, (((0,), (0,)), ((), ())), preferred_element_type=F32)


def _rms(x, n):
    r = lax.rsqrt(jnp.sum(x * x, axis=-1, keepdims=True) * (1.0 / n) + RMS_EPS)
    return x * r, r


def _rms_bwd(dy, xhat, r, w, n):
    g = dy * w
    return r * (g - xhat * (jnp.sum(g * xhat, axis=-1, keepdims=True) * (1.0 / n)))


def _rms_halves(x, half1):
    x2 = x * x
    s0 = jnp.sum(jnp.where(half1, 0.0, x2), axis=-1, keepdims=True)
    s1 = jnp.sum(jnp.where(half1, x2, 0.0), axis=-1, keepdims=True)
    r = jnp.where(half1, lax.rsqrt(s1 * (1.0 / 64) + RMS_EPS), lax.rsqrt(s0 * (1.0 / 64) + RMS_EPS))
    return x * r, r


def _rms_halves_bwd(dy, xhat, r, w, half1):
    g = dy * w
    t = g * xhat
    m0 = jnp.sum(jnp.where(half1, 0.0, t), axis=-1, keepdims=True) * (1.0 / 64)
    m1 = jnp.sum(jnp.where(half1, t, 0.0), axis=-1, keepdims=True) * (1.0 / 64)
    return r * (g - xhat * jnp.where(half1, m1, m0))


def _sigmoid(x):
    return 1.0 / (1.0 + jnp.exp(-x))


def _rope(x, c, s1, s2):
    return x * c + pltpu.roll(x, 112, 1) * s1 + pltpu.roll(x, 16, 1) * s2


def _rope_bwd(dy, c, s1, s2):
    return dy * c + pltpu.roll(dy * s1, 16, 1) + pltpu.roll(dy * s2, 112, 1)


def _fold_rows8(x):
    return jnp.sum(x.reshape(x.shape[0] // 8, 8, x.shape[1]), axis=0)


def _row0(v, rows=8):
    row = lax.broadcasted_iota(jnp.int32, (rows, v.shape[1]), 0)
    return jnp.where(row == 0, jnp.broadcast_to(v, (rows, v.shape[1])), 0.0)


def _mm_nn(a, b, name, out_dtype=F32, residual=None, tm=TM_PROJ):
    M, K = a.shape
    N = b.shape[1]
    tm = min(tm, M)

    def body(*refs):
        if residual is None:
            a_ref, b_ref, o_ref = refs
            acc = _dot(a_ref[...].astype(MXU_DTYPE), b_ref[...])
        else:
            a_ref, b_ref, r_ref, o_ref = refs
            acc = _dot(a_ref[...].astype(MXU_DTYPE), b_ref[...]) + r_ref[...]
        o_ref[...] = acc.astype(out_dtype)

    in_specs = [pl.BlockSpec((tm, K), lambda i: (i, 0)), pl.BlockSpec((K, N), lambda i: (0, 0))]
    args = [a, b]
    if residual is not None:
        in_specs.append(pl.BlockSpec((tm, N), lambda i: (i, 0)))
        args.append(residual)
    return pl.pallas_call(
        body, name=name, grid=(M // tm,), in_specs=in_specs,
        out_specs=pl.BlockSpec((tm, N), lambda i: (i, 0)), out_shape=_sds((M, N), out_dtype),
        compiler_params=_cp(("parallel",), 48))(*args)


def _mm_tn(a, b, name, out_dtype, tn, tk=512):
    T, M = a.shape
    N = b.shape[1]
    tk = min(tk, T)
    nk = T // tk

    def body(a_ref, b_ref, o_ref, acc_ref):
        k = pl.program_id(1)

        @pl.when(k == 0)
        def _():
            acc_ref[...] = jnp.zeros_like(acc_ref)

        acc_ref[...] += _dot_tn(a_ref[...].astype(MXU_DTYPE), b_ref[...].astype(MXU_DTYPE))

        @pl.when(k == nk - 1)
        def _():
            o_ref[...] = acc_ref[...].astype(out_dtype)

    return pl.pallas_call(
        body, name=name, grid=(N // tn, nk),
        in_specs=[pl.BlockSpec((tk, M), lambda n, k: (k, 0)), pl.BlockSpec((tk, tn), lambda n, k: (k, n))],
        out_specs=pl.BlockSpec((M, tn), lambda n, k: (0, n)), out_shape=_sds((M, N), out_dtype),
        scratch_shapes=[pltpu.VMEM((M, tn), F32)],
        compiler_params=_cp(("parallel", "arbitrary"), 48))(a, b)


def _inproj_fwd(x, ng, wp):
    T, D = x.shape
    tm = min(TM_PROJ, T)

    def body(x_ref, g_ref, w_ref, proj_ref, h_ref):
        xhat, _ = _rms(x_ref[...], D)
        h = (xhat * g_ref[...]).astype(MXU_DTYPE)
        h_ref[...] = h
        proj_ref[...] = _dot(h, w_ref[...])

    return pl.pallas_call(
        body, name="inproj_fwd", grid=(T // tm,),
        in_specs=[pl.BlockSpec((tm, D), lambda i: (i, 0)), pl.BlockSpec((1, D), lambda i: (0, 0)),
                  pl.BlockSpec((D, NP), lambda i: (0, 0))],
        out_specs=[pl.BlockSpec((tm, NP), lambda i: (i, 0)), pl.BlockSpec((tm, D), lambda i: (i, 0))],
        out_shape=[_sds((T, NP), F32), _sds((T, D), MXU_DTYPE)],
        compiler_params=_cp(("parallel",), 48))(x, ng, wp)


def _mla_prep_fwd(proj, lw, rope):
    T = proj.shape[0]
    tm = min(TK, T // 2)

    def body(ql_ref, kvl_ref, kr_ref, qa_ref, kva_ref, wq_ref, wk_ref, wv_ref, qn_ref, kn_ref,
             c_ref, s1_ref, s2_ref, q_out, k_out, kt_out, v_out, vt_out):
        c, s1, s2 = c_ref[...], s1_ref[...], s2_ref[...]
        qhat, _ = _rms(ql_ref[...], MLA_Q_LORA)
        qn = (qhat * qa_ref[...]).astype(MXU_DTYPE)
        khat, _ = _rms(kvl_ref[...], MLA_KV_LORA)
        kvn = (khat * kva_ref[...]).astype(MXU_DTYPE)
        kr = kr_ref[...]
        half1 = lax.broadcasted_iota(jnp.int32, (tm, LANES), 1) >= 64
        for h in range(HEADS):
            xh, _ = _rms(_dot(qn, wq_ref[h]), MLA_QK)
            q_out[h] = (_rope(xh * qn_ref[...], c, s1, s2) * (MLA_SCALE * LOG2E)).astype(MXU_DTYPE)
            xh, _ = _rms(_dot(kvn, wk_ref[h]) + kr, MLA_QK)
            kh = _rope(xh * kn_ref[...], c, s1, s2)
            k_out[h] = kh.astype(MXU_DTYPE)
            kt_out[h, 0] = kh.T.astype(MXU_DTYPE)
        v = _dot(kvn, wv_ref[...])
        for h in range(HEADS):
            vp = v[:, LANES * (h // 2):LANES * (h // 2 + 1)]
            own = half1 if h % 2 else jnp.logical_not(half1)
            vp = jnp.where(own, vp, 0.0)
            v_out[h] = vp.astype(MXU_DTYPE)
            vt_out[h, 0] = vp.T.astype(MXU_DTYPE)

    full = lambda shape: pl.BlockSpec(shape, lambda i: (0,) * len(shape))
    hd = pl.BlockSpec((HEADS, tm, LANES), lambda i: (0, i, 0))
    hdt = pl.BlockSpec((HEADS, 1, LANES, tm), lambda i: (0, i, 0, 0))
    nat = _sds((HEADS, T, LANES), MXU_DTYPE)
    tr = _sds((HEADS, T // tm, LANES, tm), MXU_DTYPE)
    return pl.pallas_call(
        body, name="mla_prep_fwd", grid=(T // tm,),
        in_specs=[pl.BlockSpec((tm, 256), lambda i: (i, CB_QLAT)), pl.BlockSpec((tm, LANES), lambda i: (i, CB_KVLAT)),
                  pl.BlockSpec((tm, LANES), lambda i: (i, CB_KROPE)),
                  full((1, 256)), full((1, LANES)), full((HEADS, 256, LANES)), full((HEADS, LANES, LANES)),
                  full((LANES, 512)), full((1, LANES)), full((1, LANES)),
                  pl.BlockSpec((tm, LANES), lambda i: (i, 0)), pl.BlockSpec((tm, LANES), lambda i: (i, 0)),
                  pl.BlockSpec((tm, LANES), lambda i: (i, 0))],
        out_specs=[hd, hd, hdt, hd, hdt],
        out_shape=[nat, nat, tr, nat, tr],
        compiler_params=_cp(("parallel",), 32))(
            proj, proj, proj, lw["qa"], lw["kva"], lw["wq"], lw["wk"], lw["wv"], lw["qn"], lw["kn"],
            rope[0], rope[1], rope[2])


def _mla_attn_fwd(q, k, vt):
    T = q.shape[1]
    tk = min(TK, T // 2)
    tq = 2 * tk

    def body(q_ref, k_ref, vt_ref, o_ref, lse_ref, acc_s, m_s, l_s, s_a, s_b):
        i = pl.program_id(1)
        key = lax.broadcasted_iota(jnp.int32, (tk, tq), 0)
        qry = lax.broadcasted_iota(jnp.int32, (tk, tq), 1)
        qs = [q_ref[0], q_ref[1]]
        acc_s[...] = jnp.zeros_like(acc_s)
        l_s[...] = jnp.zeros_like(l_s)
        m_s[...] = jnp.full(m_s.shape, NEG_INF, F32)

        def scores(kj, buf):
            rows = pl.ds(pl.multiple_of(kj * tk, tk), tk)
            for r in range(2):
                buf[r] = _dot_nt(k_ref[r, rows, :], qs[r])

        def consume(kj, buf, diag):
            for r in range(2):
                s = buf[r]
                if diag is not None:
                    s = jnp.where(key + diag * tk <= qry, s, NEG_INF)
                m_old = m_s[r]
                m_new = jnp.maximum(m_old, jnp.max(s, axis=0, keepdims=True))
                alpha = jnp.exp2(m_old - m_new)
                p = jnp.exp2(s - m_new)
                l_s[r] = alpha * l_s[r] + jnp.sum(p, axis=0, keepdims=True)
                m_s[r] = m_new
                acc_s[r] = alpha * acc_s[r] + _dot(vt_ref[r, kj], p.astype(MXU_DTYPE))

        scores(0, s_a)

        def pair(kj):
            scores(kj + 1, s_b)
            consume(kj, s_a, None)
            scores(kj + 2, s_a)
            consume(kj + 1, s_b, None)

        def quad(kq, carry):
            pair(4 * kq)
            pair(4 * kq + 2)
            return carry

        lax.fori_loop(0, i // 2, quad, 0)

        @pl.when(i % 2 == 1)
        def _():
            pair(2 * i - 2)

        scores(2 * i + 1, s_b)
        consume(2 * i, s_a, 0)
        consume(2 * i + 1, s_b, 1)
        o_t = acc_s[0] / l_s[0] + acc_s[1] / l_s[1]
        o_ref[...] = o_t.T
        for r in range(2):
            lse_ref[r] = m_s[r] + jnp.log2(l_s[r])

    return pl.pallas_call(
        body, name="mla_attn_fwd", grid=(HEADS // 2, T // tq),
        in_specs=[pl.BlockSpec((2, tq, LANES), lambda j, i: (j, i, 0)),
                  pl.BlockSpec((2, T, LANES), lambda j, i: (j, 0, 0)),
                  pl.BlockSpec((2, T // tk, LANES, tk), lambda j, i: (j, 0, 0, 0))],
        out_specs=[pl.BlockSpec((tq, LANES), lambda j, i: (i, j)),
                   pl.BlockSpec((2, 1, tq), lambda j, i: (j, 0, i))],
        out_shape=[_sds((T, GROUP_WIDTH), F32), _sds((HEADS, 1, T), F32)],
        scratch_shapes=[pltpu.VMEM((2, LANES, tq), F32), pltpu.VMEM((2, 1, tq), F32), pltpu.VMEM((2, 1, tq), F32),
                        pltpu.VMEM((2, tk, tq), F32), pltpu.VMEM((2, tk, tq), F32)],
        compiler_params=_cp(("parallel", "arbitrary"), 40))(q, k, vt)


def _swa_masks(nb_first):
    qi = lax.broadcasted_iota(jnp.int32, (BLOCK, 2 * BLOCK), 0)
    ki = lax.broadcasted_iota(jnp.int32, (BLOCK, 2 * BLOCK), 1)
    dist = BLOCK + qi - ki
    valid = (dist >= 0) & (dist < BLOCK) & ((ki >= BLOCK) | jnp.logical_not(nb_first))
    return dist.astype(F32), valid


def _swa_kv_variants(x, half1):
    xs = pltpu.roll(x, 64, 1)
    out = {}
    for g in range(2):
        for r in range(2):
            own = half1 if r else jnp.logical_not(half1)
            out[(g, r)] = jnp.where(own, x if g == r else xs, 0.0).astype(MXU_DTYPE)
    return out


def _swa_fwd(proj, lw):
    T = proj.shape[0]
    tm = min(TM_SWA, T)
    nb = tm // BLOCK
    scale = SWA_HEAD_DIM ** -0.5

    def body(q_ref, k_ref, v_ref, pk_ref, pv_ref, qw_ref, kw_ref, sink_ref, o_ref):
        i = pl.program_id(0)
        half1 = lax.broadcasted_iota(jnp.int32, (1, LANES), 1) >= 64
        k_all = jnp.concatenate([pk_ref[...], k_ref[...]], axis=0)
        v_all = jnp.concatenate([pv_ref[...], v_ref[...]], axis=0)
        khat, _ = _rms_halves(k_all, half1)
        kp = _swa_kv_variants(khat * kw_ref[...], half1)
        vp = _swa_kv_variants(v_all, half1)
        qn = []
        for j in range(4):
            qhat, _ = _rms_halves(q_ref[:, LANES * j:LANES * (j + 1)], half1)
            qn.append((qhat * qw_ref[...]).astype(MXU_DTYPE))
        for b in range(nb):
            dist, valid = _swa_masks((i == 0) & (b == 0))
            ks = slice(b * BLOCK, b * BLOCK + 2 * BLOCK)
            for j in range(4):
                g = j // 2
                qb = qn[j][b * BLOCK:(b + 1) * BLOCK]
                o = jnp.zeros((BLOCK, LANES), F32)
                for r in range(2):
                    h = 2 * j + r
                    s = _dot_nt(qb, kp[(g, r)][ks]) * scale - (2.0 ** -(h + 1)) * dist
                    s = jnp.where(valid, s, NEG_INF)
                    sink = sink_ref[h]
                    m = jnp.maximum(jnp.max(s, axis=-1, keepdims=True), sink)
                    e = jnp.exp(s - m)
                    den = jnp.sum(e, axis=-1, keepdims=True) + jnp.exp(sink - m)
                    o = o + _dot((e / den).astype(MXU_DTYPE), vp[(g, r)][ks])
                o_ref[b * BLOCK:(b + 1) * BLOCK, LANES * j:LANES * (j + 1)] = o

    prev = lambda cb: pl.BlockSpec((BLOCK, LANES), lambda i: (jnp.maximum(i * nb - 1, 0), cb))
    return pl.pallas_call(
        body, name="swa_fwd", grid=(T // tm,),
        in_specs=[pl.BlockSpec((tm, 512), lambda i: (i, CB_SQ)), pl.BlockSpec((tm, LANES), lambda i: (i, CB_SK)),
                  pl.BlockSpec((tm, LANES), lambda i: (i, CB_SV)), prev(CB_SK), prev(CB_SV),
                  pl.BlockSpec((1, LANES), lambda i: (0, 0)), pl.BlockSpec((1, LANES), lambda i: (0, 0)),
                  pl.BlockSpec(memory_space=pltpu.SMEM)],
        out_specs=pl.BlockSpec((tm, 512), lambda i: (i, 0)),
        out_shape=_sds((T, GROUP_WIDTH), F32),
        compiler_params=_cp(("parallel",), 32))(proj, proj, proj, proj, proj, lw["sqn"], lw["skn"], lw["sinks"])


def _shift_down(u, prev, n, row):
    tm = u.shape[0]
    out = pltpu.roll(u, n, 0)
    row8 = lax.broadcasted_iota(jnp.int32, prev.shape, 0)
    for t in range(n):
        src = jnp.sum(jnp.where(row8 == 8 - n + t, prev, 0.0), axis=0, keepdims=True)
        out = jnp.where(row == t, src, out)
    return out


def _shift_up(u, nxt, n, row):
    tm = u.shape[0]
    out = pltpu.roll(u, tm - n, 0)
    row8 = lax.broadcasted_iota(jnp.int32, nxt.shape, 0)
    for t in range(n):
        src = jnp.sum(jnp.where(row8 == t, nxt, 0.0), axis=0, keepdims=True)
        out = jnp.where(row == tm - n + t, src, out)
    return out


def _mix_fwd(proj, o_mla, o_swa, conv_w):
    T = proj.shape[0]
    tm = min(TM_ROW, T)

    def body(gm_ref, ch_ref, cb_ref, cc_ref, gc_ref, gs_ref, pch_ref, pcc_ref, om_ref, os_ref, w_ref, y_ref):
        i = pl.program_id(0)
        row = lax.broadcasted_iota(jnp.int32, (tm, GROUP_WIDTH), 0)
        u = cc_ref[...] * ch_ref[...]
        u_prev = jnp.where(i > 0, pcc_ref[...] * pch_ref[...], 0.0)
        z = (w_ref[0:1, :] * _shift_down(u, u_prev, 2, row) + w_ref[1:2, :] * _shift_down(u, u_prev, 1, row)
             + w_ref[2:3, :] * u)
        gm, gc, gs = gm_ref[...], gc_ref[...], gs_ref[...]
        y_ref[:, 0:512] = (om_ref[...] * (gm * _sigmoid(gm))).astype(MXU_DTYPE)
        y_ref[:, 512:1024] = (cb_ref[...] * z * (gc * _sigmoid(gc))).astype(MXU_DTYPE)
        y_ref[:, 1024:1536] = (os_ref[...] * (gs * _sigmoid(gs))).astype(MXU_DTYPE)

    blk = lambda cb: pl.BlockSpec((tm, 512), lambda i: (i, cb))
    prev = lambda cb: pl.BlockSpec((8, 512), lambda i: (jnp.maximum(i * (tm // 8) - 1, 0), cb))
    tile = pl.BlockSpec((tm, 512), lambda i: (i, 0))
    return pl.pallas_call(
        body, name="mix_fwd", grid=(T // tm,),
        in_specs=[blk(CB_GMLA), blk(CB_CH), blk(CB_CB), blk(CB_CC), blk(CB_GCONV), blk(CB_GSWA),
                  prev(CB_CH), prev(CB_CC), tile, tile, pl.BlockSpec((8, 512), lambda i: (0, 0))],
        out_specs=pl.BlockSpec((tm, D_MIX), lambda i: (i, 0)),
        out_shape=_sds((T, D_MIX), MXU_DTYPE),
        compiler_params=_cp(("parallel",), 32))(
            proj, proj, proj, proj, proj, proj, proj, proj, o_mla, o_swa, conv_w)


def _loss_grad(y, target):
    T, D = y.shape
    tm = min(TM_ROW, T)
    nt = T // tm

    def body(y_ref, t_ref, g_ref, loss_ref, acc_ref):
        i = pl.program_id(0)

        @pl.when(i == 0)
        def _():
            acc_ref[...] = jnp.zeros_like(acc_ref)

        err = y_ref[...] - t_ref[...]
        g_ref[...] = err * (1.0 / D)
        acc_ref[...] += _fold_rows8(err * err)

        @pl.when(i == nt - 1)
        def _():
            tot = jnp.sum(jnp.sum(acc_ref[...], axis=1, keepdims=True), axis=0, keepdims=True)
            loss_ref[...] = jnp.broadcast_to(tot * (0.5 / D), (8, LANES))

    return pl.pallas_call(
        body, name="loss_grad", grid=(nt,),
        in_specs=[pl.BlockSpec((tm, D), lambda i: (i, 0)), pl.BlockSpec((tm, D), lambda i: (i, 0))],
        out_specs=[pl.BlockSpec((tm, D), lambda i: (i, 0)), pl.BlockSpec((8, LANES), lambda i: (0, 0))],
        out_shape=[_sds((T, D), F32), _sds((8, LANES), F32)],
        scratch_shapes=[pltpu.VMEM((8, D), F32)],
        compiler_params=_cp(("arbitrary",), 32))(y, target)


def _mix_bwd(dycat, proj, o_mla, o_swa, conv_w):
    T = proj.shape[0]
    tm = min(TM_ROW, T)
    nt = T // tm

    def body(dym_ref, dyc_ref, dys_ref, gm_ref, ch_ref, cb_ref, cc_ref, gc_ref, gs_ref, pch_ref, pcc_ref,
             ndy_ref, ncb_ref, ngc_ref, om_ref, os_ref, w_ref,
             d1_ref, dgs_ref, dom_ref, dos_ref, dw_ref):
        i = pl.program_id(0)

        @pl.when(i == 0)
        def _():
            dw_ref[...] = jnp.zeros_like(dw_ref)

        row = lax.broadcasted_iota(jnp.int32, (tm, GROUP_WIDTH), 0)

        def gate(g):
            sg = _sigmoid(g)
            return g * sg, sg * (1.0 + g * (1.0 - sg))

        gm = gm_ref[...]
        silu, dsilu = gate(gm)
        dym = dym_ref[...]
        dom_ref[...] = dym * silu
        d1_ref[:, 0:512] = (dym * om_ref[...] * dsilu).astype(MXU_DTYPE)

        gs = gs_ref[...]
        silu, dsilu = gate(gs)
        dys = dys_ref[...]
        dos_ref[...] = dys * silu
        dgs_ref[...] = (dys * os_ref[...] * dsilu).astype(MXU_DTYPE)

        ch, cb, cc, gc, dyc = ch_ref[...], cb_ref[...], cc_ref[...], gc_ref[...], dyc_ref[...]
        w0, w1, w2 = w_ref[0:1, :], w_ref[1:2, :], w_ref[2:3, :]
        u = cc * ch
        u_prev = jnp.where(i > 0, pcc_ref[...] * pch_ref[...], 0.0)
        u1 = _shift_down(u, u_prev, 1, row)
        u2 = _shift_down(u, u_prev, 2, row)
        z = w0 * u2 + w1 * u1 + w2 * u
        silu, dsilu = gate(gc)
        dz = dyc * cb * silu
        ngc = ngc_ref[...]
        dz_next = jnp.where(i < nt - 1, ndy_ref[...] * ncb_ref[...] * (ngc * _sigmoid(ngc)), 0.0)
        du = w2 * dz + w1 * _shift_up(dz, dz_next, 1, row) + w0 * _shift_up(dz, dz_next, 2, row)
        d1_ref[:, 512:1024] = (du * cc).astype(MXU_DTYPE)
        d1_ref[:, 1024:1536] = (dyc * z * silu).astype(MXU_DTYPE)
        d1_ref[:, 1536:2048] = (du * ch).astype(MXU_DTYPE)
        d1_ref[:, 2048:2560] = (dyc * cb * z * dsilu).astype(MXU_DTYPE)
        row8 = lax.broadcasted_iota(jnp.int32, (8, GROUP_WIDTH), 0)
        dw = jnp.zeros((8, GROUP_WIDTH), F32)
        for t, shifted in enumerate((u2, u1, u)):
            dw = dw + jnp.where(row8 == t, jnp.sum(dz * shifted, axis=0, keepdims=True), 0.0)
        dw_ref[...] += dw

    blk = lambda cb: pl.BlockSpec((tm, 512), lambda i: (i, cb))
    prev = lambda cb: pl.BlockSpec((8, 512), lambda i: (jnp.maximum(i * (tm // 8) - 1, 0), cb))
    nxt = lambda cb: pl.BlockSpec((8, 512), lambda i: (jnp.minimum((i + 1) * (tm // 8), T // 8 - 1), cb))
    tile = pl.BlockSpec((tm, 512), lambda i: (i, 0))
    return pl.pallas_call(
        body, name="mix_bwd", grid=(nt,),
        in_specs=[blk(0), blk(1), blk(2), blk(CB_GMLA), blk(CB_CH), blk(CB_CB), blk(CB_CC), blk(CB_GCONV),
                  blk(CB_GSWA), prev(CB_CH), prev(CB_CC), nxt(1), nxt(CB_CB), nxt(CB_GCONV), tile, tile,
                  pl.BlockSpec((8, 512), lambda i: (0, 0))],
        out_specs=[pl.BlockSpec((tm, 2560), lambda i: (i, 0)), tile, tile, tile,
                   pl.BlockSpec((8, 512), lambda i: (0, 0))],
        out_shape=[_sds((T, 2560), MXU_DTYPE), _sds((T, 512), MXU_DTYPE), _sds((T, 512), F32),
                   _sds((T, 512), F32), _sds((8, 512), F32)],
        compiler_params=_cp(("arbitrary",), 48))(
            dycat, dycat, dycat, proj, proj, proj, proj, proj, proj, proj, proj, dycat, proj, proj,
            o_mla, o_swa, conv_w)


def _swa_bwd(proj, o_swa, do_swa, lw):
    T = proj.shape[0]
    tm = min(TM_SWA, T)
    nb = tm // BLOCK
    scale = SWA_HEAD_DIM ** -0.5

    def body(q_ref, k_ref, v_ref, pk_ref, pv_ref, o_ref, do_ref, qw_ref, kw_ref, sink_ref,
             dq_ref, dk_ref, dv_ref, dqw_ref, dsink_ref):
        i = pl.program_id(0)

        @pl.when(i == 0)
        def _():
            dk_ref[...] = jnp.zeros_like(dk_ref)
            dv_ref[...] = jnp.zeros_like(dv_ref)
            dqw_ref[...] = jnp.zeros_like(dqw_ref)
            dsink_ref[...] = jnp.zeros_like(dsink_ref)

        half1 = lax.broadcasted_iota(jnp.int32, (1, LANES), 1) >= 64
        k_all = jnp.concatenate([pk_ref[...], k_ref[...]], axis=0)
        v_all = jnp.concatenate([pv_ref[...], v_ref[...]], axis=0)
        khat, _ = _rms_halves(k_all, half1)
        kp = _swa_kv_variants(khat * kw_ref[...], half1)
        vp = _swa_kv_variants(v_all, half1)
        qw = qw_ref[...]
        dqw = jnp.zeros((1, LANES), F32)
        dsink_rows = [jnp.zeros((1, 1), F32) for _ in range(HEADS)]
        for j in range(4):
            g = j // 2
            cols = slice(LANES * j, LANES * (j + 1))
            qhat, qr = _rms_halves(q_ref[:, cols], half1)
            qn = (qhat * qw).astype(MXU_DTYPE)
            do = do_ref[:, cols]
            dob = do.astype(MXU_DTYPE)
            prod = do * o_ref[:, cols]
            dqn_blocks = []
            for b in range(nb):
                dist, valid = _swa_masks((i == 0) & (b == 0))
                ks = slice(b * BLOCK, b * BLOCK + 2 * BLOCK)
                rs = slice(b * BLOCK, (b + 1) * BLOCK)
                qb = qn[rs]
                dqn = jnp.zeros((BLOCK, LANES), F32)
                for r in range(2):
                    h = 2 * j + r
                    own = half1 if r else jnp.logical_not(half1)
                    s = _dot_nt(qb, kp[(g, r)][ks]) * scale - (2.0 ** -(h + 1)) * dist
                    s = jnp.where(valid, s, NEG_INF)
                    sink = sink_ref[h]
                    m = jnp.maximum(jnp.max(s, axis=-1, keepdims=True), sink)
                    e = jnp.exp(s - m)
                    es = jnp.exp(sink - m)
                    inv = 1.0 / (jnp.sum(e, axis=-1, keepdims=True) + es)
                    p = e * inv
                    dd = jnp.sum(jnp.where(own, prod[rs], 0.0), axis=-1, keepdims=True)
                    dp = _dot_nt(dob[rs], vp[(g, r)][ks])
                    ds = (p * (dp - dd) * scale).astype(MXU_DTYPE)
                    dsink_rows[h] = dsink_rows[h] - jnp.sum(es * inv * dd, axis=0, keepdims=True)
                    dqn = dqn + _dot(ds, kp[(g, r)][ks])
                    dkp = jnp.where(own, _dot_tn(ds, qb), 0.0)
                    dvp = jnp.where(own, _dot_tn(p.astype(MXU_DTYPE), dob[rs]), 0.0)
                    if g != r:
                        dkp = pltpu.roll(dkp, 64, 1)
                        dvp = pltpu.roll(dvp, 64, 1)
                    dst = pl.ds(pl.multiple_of((i * nb + b) * BLOCK, BLOCK), 2 * BLOCK)
                    dk_ref[dst, :] += dkp
                    dv_ref[dst, :] += dvp
                dqn_blocks.append(dqn)
            dqn = jnp.concatenate(dqn_blocks, axis=0) if nb > 1 else dqn_blocks[0]
            dqw = dqw + jnp.sum(dqn * qhat, axis=0, keepdims=True)
            dq_ref[:, cols] = _rms_halves_bwd(dqn, qhat, qr, qw, half1).astype(MXU_DTYPE)
        dqw_ref[...] += _row0(dqw + pltpu.roll(dqw, 64, 1))
        row8 = lax.broadcasted_iota(jnp.int32, (8, LANES), 0)
        dsink = jnp.zeros((8, LANES), F32)
        for h in range(HEADS):
            dsink = dsink + jnp.where(row8 == h, jnp.broadcast_to(dsink_rows[h], (8, LANES)), 0.0)
        dsink_ref[...] += dsink

    prev = lambda cb: pl.BlockSpec((BLOCK, LANES), lambda i: (jnp.maximum(i * nb - 1, 0), cb))
    tile = pl.BlockSpec((tm, 512), lambda i: (i, 0))
    small = pl.BlockSpec((8, LANES), lambda i: (0, 0))
    acc = pl.BlockSpec((T + BLOCK, LANES), lambda i: (0, 0))
    return pl.pallas_call(
        body, name="swa_bwd", grid=(T // tm,),
        in_specs=[pl.BlockSpec((tm, 512), lambda i: (i, CB_SQ)), pl.BlockSpec((tm, LANES), lambda i: (i, CB_SK)),
                  pl.BlockSpec((tm, LANES), lambda i: (i, CB_SV)), prev(CB_SK), prev(CB_SV), tile, tile,
                  pl.BlockSpec((1, LANES), lambda i: (0, 0)), pl.BlockSpec((1, LANES), lambda i: (0, 0)),
                  pl.BlockSpec(memory_space=pltpu.SMEM)],
        out_specs=[tile, acc, acc, small, small],
        out_shape=[_sds((T, 512), MXU_DTYPE), _sds((T + BLOCK, LANES), F32), _sds((T + BLOCK, LANES), F32),
                   _sds((8, LANES), F32), _sds((8, LANES), F32)],
        compiler_params=_cp(("arbitrary",), 40))(
            proj, proj, proj, proj, proj, o_swa, do_swa, lw["sqn"], lw["skn"], lw["sinks"])


def _swa_kv_bwd(proj, dkn, dv, lw):
    T = proj.shape[0]
    tm = BLOCK

    def body(k_ref, dkn_ref, dv_ref, kw_ref, d_ref, dkw_ref):
        i = pl.program_id(0)

        @pl.when(i == 0)
        def _():
            dkw_ref[...] = jnp.zeros_like(dkw_ref)

        half1 = lax.broadcasted_iota(jnp.int32, (1, LANES), 1) >= 64
        khat, kr = _rms_halves(k_ref[...], half1)
        dkn_t = dkn_ref[...]
        dkw = jnp.sum(dkn_t * khat, axis=0, keepdims=True)
        dkw_ref[...] += _row0(dkw + pltpu.roll(dkw, 64, 1))
        d_ref[:, 0:LANES] = _rms_halves_bwd(dkn_t, khat, kr, kw_ref[...], half1).astype(MXU_DTYPE)
        d_ref[:, LANES:2 * LANES] = dv_ref[...].astype(MXU_DTYPE)

    return pl.pallas_call(
        body, name="swa_kv_bwd", grid=(T // tm,),
        in_specs=[pl.BlockSpec((tm, LANES), lambda i: (i, CB_SK)), pl.BlockSpec((tm, LANES), lambda i: (i + 1, 0)),
                  pl.BlockSpec((tm, LANES), lambda i: (i + 1, 0)), pl.BlockSpec((1, LANES), lambda i: (0, 0))],
        out_specs=[pl.BlockSpec((tm, 2 * LANES), lambda i: (i, 0)), pl.BlockSpec((8, LANES), lambda i: (0, 0))],
        out_shape=[_sds((T, 2 * LANES), MXU_DTYPE), _sds((8, LANES), F32)],
        compiler_params=_cp(("arbitrary",), 32))(proj, dkn, dv, lw["skn"])


def _mla_attn_bwd(q, k, kt, vpad, o, do, lse):
    T = q.shape[1]
    tk = min(TK, T // 2)
    tq = 2 * tk

    def body(q_ref, k_ref, kt_ref, v_ref, o_ref, do_ref, lse_ref, dq_ref, dk_ref, dv_ref, dqt_s,
             s_a, s_b, p_a, p_b):
        h = pl.program_id(0)
        i = pl.program_id(1)

        @pl.when(i == 0)
        def _():
            dk_ref[...] = jnp.zeros_like(dk_ref)
            dv_ref[...] = jnp.zeros_like(dv_ref)

        key = lax.broadcasted_iota(jnp.int32, (tk, tq), 0)
        qry = lax.broadcasted_iota(jnp.int32, (tk, tq), 1)
        own_rows = (lax.broadcasted_iota(jnp.int32, (LANES, 1), 0) // 64) == (h % 2)
        do_t = do_ref[...]
        dob = do_t.astype(MXU_DTYPE)
        dob_t = do_t.T.astype(MXU_DTYPE)
        prod_t = (do_t * o_ref[...]).T
        dd = jnp.sum(jnp.where(own_rows, prod_t, 0.0), axis=0, keepdims=True)
        qh = q_ref[0]
        qh_t = qh.astype(F32).T.astype(MXU_DTYPE)
        lse_t = lse_ref[0]
        dqt_s[...] = jnp.zeros_like(dqt_s)

        def scores(kj, s_buf, p_buf):
            rows = pl.ds(pl.multiple_of(kj * tk, tk), tk)
            s_buf[...] = _dot_nt(k_ref[0, rows, :], qh)
            p_buf[...] = _dot_nt(v_ref[0, rows, :], dob)

        def consume(kj, s_buf, p_buf, diag):
            s = s_buf[...]
            if diag is not None:
                s = jnp.where(key + diag * tk <= qry, s, NEG_INF)
            p = jnp.exp2(s - lse_t)
            ds = (p * (p_buf[...] - dd)).astype(MXU_DTYPE)
            dqt_s[...] += _dot(kt_ref[0, kj], ds)
            dk_ref[0, kj] += _dot_nt(qh_t, ds)
            dv_ref[0, kj] += jnp.where(own_rows, _dot_nt(dob_t, p.astype(MXU_DTYPE)), 0.0)

        scores(0, s_a, p_a)

        def pair(kj):
            scores(kj + 1, s_b, p_b)
            consume(kj, s_a, p_a, None)
            scores(kj + 2, s_a, p_a)
            consume(kj + 1, s_b, p_b, None)

        def quad(kq, carry):
            pair(4 * kq)
            pair(4 * kq + 2)
            return carry

        lax.fori_loop(0, i // 2, quad, 0)

        @pl.when(i % 2 == 1)
        def _():
            pair(2 * i - 2)

        scores(2 * i + 1, s_b, p_b)
        consume(2 * i, s_a, p_a, 0)
        consume(2 * i + 1, s_b, p_b, 1)
        dq_ref[0] = dqt_s[...].T

    res = pl.BlockSpec((1, T, LANES), lambda h, i: (h, 0, 0))
    res_t = pl.BlockSpec((1, T // tk, LANES, tk), lambda h, i: (h, 0, 0, 0))
    buf = pltpu.VMEM((tk, tq), F32)
    return pl.pallas_call(
        body, name="mla_attn_bwd", grid=(HEADS, T // tq),
        in_specs=[pl.BlockSpec((1, tq, LANES), lambda h, i: (h, i, 0)), res, res_t, res,
                  pl.BlockSpec((tq, LANES), lambda h, i: (i, h // 2)),
                  pl.BlockSpec((tq, LANES), lambda h, i: (i, h // 2)),
                  pl.BlockSpec((1, 1, tq), lambda h, i: (h, 0, i))],
        out_specs=[pl.BlockSpec((1, tq, LANES), lambda h, i: (h, i, 0)), res_t, res_t],
        out_shape=[_sds((HEADS, T, LANES), F32), _sds((HEADS, T // tk, LANES, tk), F32),
                   _sds((HEADS, T // tk, LANES, tk), F32)],
        scratch_shapes=[pltpu.VMEM((LANES, tq), F32), buf, buf, buf, buf],
        compiler_params=_cp(("parallel", "arbitrary"), 48))(q, k, kt, vpad, o, do, lse)


def _mla_prep_bwd(proj, dq, dk, dv, lw, rope):
    T = proj.shape[0]
    tm = min(TK, T // 2)

    def body(ql_ref, kvl_ref, kr_ref, dq_ref, dk_ref, dv_ref, qa_ref, kva_ref, wq_ref, wk_ref, wv_ref,
             wqt_ref, wkt_ref, wvt_ref, qn_ref, kn_ref, c_ref, s1_ref, s2_ref,
             d_ref, dwq_ref, dwk_ref, dwv_ref, dqa_ref, dkva_ref, dqn_ref, dkn_ref):
        i = pl.program_id(0)

        @pl.when(i == 0)
        def _():
            for ref in (dwq_ref, dwk_ref, dwv_ref, dqa_ref, dkva_ref, dqn_ref, dkn_ref):
                ref[...] = jnp.zeros_like(ref)

        c, s1, s2 = c_ref[...], s1_ref[...], s2_ref[...]
        lane = lax.broadcasted_iota(jnp.int32, (1, LANES), 1)
        qlhat, qlr = _rms(ql_ref[...], MLA_Q_LORA)
        qn = (qlhat * qa_ref[...]).astype(MXU_DTYPE)
        kvhat, kvr = _rms(kvl_ref[...], MLA_KV_LORA)
        kvn = (kvhat * kva_ref[...]).astype(MXU_DTYPE)
        kr = kr_ref[...]
        dqnl = jnp.zeros((tm, MLA_Q_LORA), F32)
        dkvn = jnp.zeros((tm, MLA_KV_LORA), F32)
        dkr = jnp.zeros((tm, LANES), F32)
        dqw = jnp.zeros((1, LANES), F32)
        dkw = jnp.zeros((1, LANES), F32)
        for h in range(HEADS):
            xh, r = _rms(_dot(qn, wq_ref[h]), MLA_QK)
            dy = _rope_bwd(dq_ref[h] * MLA_SCALE, c, s1, s2)
            dqw = dqw + jnp.sum(dy * xh, axis=0, keepdims=True)
            dx = _rms_bwd(dy, xh, r, qn_ref[...], MLA_QK).astype(MXU_DTYPE)
            dwq_ref[h] += _dot_tn(qn, dx)
            dqnl = dqnl + _dot(dx, wqt_ref[h])

            xh, r = _rms(_dot(kvn, wk_ref[h]) + kr, MLA_QK)
            dy = _rope_bwd(dk_ref[h, 0].T * LN2, c, s1, s2)
            dkw = dkw + jnp.sum(dy * xh, axis=0, keepdims=True)
            dxf = _rms_bwd(dy, xh, r, kn_ref[...], MLA_QK)
            dkr = dkr + dxf
            dx = dxf.astype(MXU_DTYPE)
            dwk_ref[h] += _dot_tn(kvn, dx)
            dkvn = dkvn + _dot(dx, wkt_ref[h])
        dvc = jnp.concatenate([(dv_ref[2 * j, 0] + dv_ref[2 * j + 1, 0]).T for j in range(4)],
                              axis=1).astype(MXU_DTYPE)
        dwv_ref[...] += _dot_tn(kvn, dvc)
        dkvn = dkvn + _dot(dvc, wvt_ref[...])
        dqa_ref[...] += _row0(jnp.sum(dqnl * qlhat, axis=0, keepdims=True))
        dkva_ref[...] += _row0(jnp.sum(dkvn * kvhat, axis=0, keepdims=True))
        dqn_ref[...] += _row0(dqw)
        dkn_ref[...] += _row0(dkw)
        d_ref[:, 0:256] = _rms_bwd(dqnl, qlhat, qlr, qa_ref[...], MLA_Q_LORA).astype(MXU_DTYPE)
        d_ref[:, 256:384] = _rms_bwd(dkvn, kvhat, kvr, kva_ref[...], MLA_KV_LORA).astype(MXU_DTYPE)
        d_ref[:, 384:512] = jnp.where((lane >= 64) & (lane < 96), dkr, 0.0).astype(MXU_DTYPE)

    full = lambda shape: pl.BlockSpec(shape, lambda i: (0,) * len(shape))
    hd = pl.BlockSpec((HEADS, tm, LANES), lambda i: (0, i, 0))
    hdt = pl.BlockSpec((HEADS, 1, LANES, tm), lambda i: (0, i, 0, 0))
    tab = pl.BlockSpec((tm, LANES), lambda i: (i, 0))
    return pl.pallas_call(
        body, name="mla_prep_bwd", grid=(T // tm,),
        in_specs=[pl.BlockSpec((tm, 256), lambda i: (i, CB_QLAT)), pl.BlockSpec((tm, LANES), lambda i: (i, CB_KVLAT)),
                  pl.BlockSpec((tm, LANES), lambda i: (i, CB_KROPE)), hd, hdt, hdt,
                  full((1, 256)), full((1, LANES)), full((HEADS, 256, LANES)), full((HEADS, LANES, LANES)),
                  full((LANES, 512)), full((HEADS, LANES, 256)), full((HEADS, LANES, LANES)), full((512, LANES)),
                  full((1, LANES)), full((1, LANES)), tab, tab, tab],
        out_specs=[pl.BlockSpec((tm, 512), lambda i: (i, 0)), full((HEADS, 256, LANES)),
                   full((HEADS, LANES, LANES)), full((LANES, 512)), full((8, 256)), full((8, LANES)),
                   full((8, LANES)), full((8, LANES))],
        out_shape=[_sds((T, 512), MXU_DTYPE), _sds((HEADS, 256, LANES), F32), _sds((HEADS, LANES, LANES), F32),
                   _sds((LANES, 512), F32), _sds((8, 256), F32), _sds((8, LANES), F32), _sds((8, LANES), F32),
                   _sds((8, LANES), F32)],
        compiler_params=_cp(("arbitrary",), 48))(
            proj, proj, proj, dq, dk, dv, lw["qa"], lw["kva"], lw["wq"], lw["wk"], lw["wv"],
            lw["wqt"], lw["wkt"], lw["wvt"], lw["qn"], lw["kn"], rope[0], rope[1], rope[2])


def _norm_bwd(dh, x, g_in, ng):
    T, D = x.shape
    tm = min(TM_ROW, T)

    def body(dh_ref, x_ref, g_ref, w_ref, dx_ref, dw_ref):
        i = pl.program_id(0)

        @pl.when(i == 0)
        def _():
            dw_ref[...] = jnp.zeros_like(dw_ref)

        xhat, r = _rms(x_ref[...], D)
        dh_t = dh_ref[...]
        dw_ref[...] += _row0(jnp.sum(dh_t * xhat, axis=0, keepdims=True))
        dx_ref[...] = g_ref[...] + _rms_bwd(dh_t, xhat, r, w_ref[...], D)

    tile = pl.BlockSpec((tm, D), lambda i: (i, 0))
    return pl.pallas_call(
        body, name="norm_bwd", grid=(T // tm,),
        in_specs=[tile, tile, tile, pl.BlockSpec((1, D), lambda i: (0, 0))],
        out_specs=[tile, pl.BlockSpec((8, D), lambda i: (0, 0))],
        out_shape=[_sds((T, D), F32), _sds((8, D), F32)],
        compiler_params=_cp(("arbitrary",), 32))(dh, x, g_in, ng)


def _rope_tables(T):
    half = MLA_ROPE // 2
    inv_freq = jnp.power(jnp.float32(ROPE_THETA), -jnp.arange(half, dtype=F32) / half)
    ang = jnp.arange(T, dtype=F32)[:, None] * inv_freq[None, :]
    cos, sin = jnp.cos(ang), jnp.sin(ang)
    z = lambda n: jnp.zeros((T, n), F32)
    c = jnp.concatenate([jnp.ones((T, MLA_NOPE), F32), cos, cos, z(32)], axis=1)
    s1 = jnp.concatenate([z(64), -sin, z(48)], axis=1)
    s2 = jnp.concatenate([z(80), sin, z(32)], axis=1)
    return c, s1, s2


def _pad_lanes(v, n=LANES):
    v = v.reshape(1, -1)
    return jnp.pad(v, ((0, 0), (0, n - v.shape[1])))


def _pack_win(w):
    z = lambda n: jnp.zeros((w.shape[0], n), w.dtype)
    return jnp.concatenate([w[:, 0:384], z(64), w[:, 384:416], z(32), w[:, 416:2976], w[:, 2976:3488],
                            w[:, 3744:4256], w[:, 3488:3616], w[:, 3616:3744]], axis=1)


def _unpack_dwin(d):
    return jnp.concatenate([d[:, 0:384], d[:, 448:480], d[:, 512:3072], d[:, 3072:3584], d[:, 4096:4224],
                            d[:, 4224:4352], d[:, 3584:4096]], axis=1)


def _layer_weights(l, norm_g, w_in_full, qa, wqb_full, kva, wkvb_full, qn, kn, conv_full, sqn, skn, sinks,
                   w_out_full):
    wp = _pack_win(w_in_full)
    wq = jnp.pad(wqb_full, ((0, 0), (0, 0), (0, LANES - MLA_QK)))
    wk = jnp.pad(wkvb_full[:, :, :MLA_NOPE], ((0, 0), (0, 0), (0, LANES - MLA_NOPE)))
    wv = jnp.transpose(wkvb_full[:, :, MLA_NOPE:], (1, 0, 2)).reshape(MLA_KV_LORA, GROUP_WIDTH)
    return dict(
        ng=norm_g[l].reshape(1, -1), wp=wp, wpt=wp.T, qa=qa[l].reshape(1, -1), kva=kva[l].reshape(1, -1),
        wq=wq, wk=wk, wv=wv, wqt=jnp.transpose(wq, (0, 2, 1)), wkt=jnp.transpose(wk, (0, 2, 1)), wvt=wv.T,
        qn=_pad_lanes(qn[l]), kn=_pad_lanes(kn[l]),
        conv=jnp.pad(conv_full, ((0, 5), (0, 0))),
        sqn=jnp.tile(sqn[l].reshape(1, -1), (1, 2)), skn=jnp.tile(skn[l].reshape(1, -1), (1, 2)),
        sinks=sinks[l], wo=w_out_full, wot=w_out_full.T)


def _layer_fwd(x, lw, rope):
    proj, h = _inproj_fwd(x, lw["ng"], lw["wp"])
    q, k, kt, vpad, vt = _mla_prep_fwd(proj, lw, rope)
    o_mla, lse = _mla_attn_fwd(q, k, vt)
    o_swa = _swa_fwd(proj, lw)
    ycat = _mix_fwd(proj, o_mla, o_swa, lw["conv"])
    x_next = _mm_nn(ycat, lw["wo"], "outproj_fwd", residual=x)
    return x_next, dict(x=x, proj=proj, h=h, q=q, k=k, kt=kt, vpad=vpad, o_mla=o_mla, lse=lse, o_swa=o_swa, ycat=ycat)


def _layer_bwd(g, sv, lw, rope):
    proj = sv["proj"]
    dycat = _mm_nn(g, lw["wot"], "outproj_bwd_dy")
    d_wo = _mm_tn(sv["ycat"], g, "outproj_bwd_dw", WIRE_DTYPE, tn=D_MODEL)
    d1, dgs, do_mla, do_swa, d_conv = _mix_bwd(dycat, proj, sv["o_mla"], sv["o_swa"], lw["conv"])
    dsq, dkn_acc, dv_acc, d_sqn, d_sinks = _swa_bwd(proj, sv["o_swa"], do_swa, lw)
    dskv, d_skn = _swa_kv_bwd(proj, dkn_acc, dv_acc, lw)
    dq, dk, dv = _mla_attn_bwd(sv["q"], sv["k"], sv["kt"], sv["vpad"], sv["o_mla"], do_mla, sv["lse"])
    dmla, d_wq, d_wk, d_wv, d_qa, d_kva, d_qn, d_kn = _mla_prep_bwd(proj, dq, dk, dv, lw, rope)
    dproj = jnp.concatenate([dmla, d1, dsq, dgs, dskv], axis=1)
    dh = _mm_nn(dproj, lw["wpt"], "inproj_bwd_dh")
    dx, d_ng = _norm_bwd(dh, sv["x"], g, lw["ng"])
    d_wp = _mm_tn(sv["h"], dproj, "inproj_bwd_dw", WIRE_DTYPE, tn=NP // 2)
    grads = dict(
        w_in=_unpack_dwin(d_wp), w_out=d_wo,
        w_qb=d_wq[:, :, :MLA_QK],
        w_kvb=jnp.concatenate([d_wk[:, :, :MLA_NOPE],
                               jnp.transpose(d_wv.reshape(MLA_KV_LORA, HEADS, MLA_NOPE), (1, 0, 2))], axis=2),
        conv=d_conv[0:3], norm_g=d_ng[0], qa=d_qa[0], kva=d_kva[0], qn=d_qn[0, :MLA_QK], kn=d_kn[0, :MLA_QK],
        sqn=d_sqn[0, :SWA_HEAD_DIM], skn=d_skn[0, :SWA_HEAD_DIM], sinks=d_sinks[:, 0])
    return dx, grads


def _local_step(x, target, lws, rope):
    saved = []
    for lw in lws:
        x, sv = _layer_fwd(x, lw, rope)
        saved.append(sv)
    g, loss_tile = _loss_grad(x, target)
    grads = [None] * len(lws)
    for l in reversed(range(len(lws))):
        g, grads[l] = _layer_bwd(g, saved[l], lws[l], rope)
    return loss_tile, g, grads


def _my_coords():
    return lax.axis_index("x"), lax.axis_index("y"), lax.axis_index("c")


def _peer(me, k):
    x, y, c = me
    return (1 - x if k & 4 else x, 1 - y if k & 2 else y, 1 - c if k & 1 else c)


def _lin(d):
    return 4 * d[0] + 2 * d[1] + d[2]


def _all_gather(shards):
    n = len(shards)

    def body(*refs):
        ins, outs = refs[:n], refs[n:2 * n]
        send_sems, recv_sems, local_sems = refs[2 * n:]
        me = _my_coords()
        my = _lin(me)
        local = [pltpu.make_async_copy(ins[a], outs[a].at[my], local_sems.at[a]) for a in range(n)]
        for cp in local:
            cp.start()
        sends = []
        for a in range(n):
            for k in range(1, N_DEV):
                cp = pltpu.make_async_remote_copy(
                    src_ref=ins[a], dst_ref=outs[a].at[my], send_sem=send_sems.at[a * 7 + k - 1],
                    recv_sem=recv_sems.at[a * 7 + k - 1], device_id=_peer(me, k),
                    device_id_type=pl.DeviceIdType.MESH)
                cp.start()
                sends.append(cp)
        for a in range(n):
            for k in range(1, N_DEV):
                src = _lin(_peer(me, k))
                pltpu.make_async_remote_copy(
                    src_ref=ins[a], dst_ref=outs[a].at[src], send_sem=send_sems.at[a * 7 + k - 1],
                    recv_sem=recv_sems.at[a * 7 + k - 1], device_id=_peer(me, k),
                    device_id_type=pl.DeviceIdType.MESH).wait_recv()
        for cp in sends:
            cp.wait_send()
        for cp in local:
            cp.wait()

    any_spec = pl.BlockSpec(memory_space=pl.ANY)
    return pl.pallas_call(
        body, name="weight_all_gather",
        in_specs=[any_spec] * n, out_specs=[any_spec] * n,
        out_shape=[_sds((N_DEV,) + s.shape, s.dtype) for s in shards],
        scratch_shapes=[pltpu.SemaphoreType.DMA((7 * n,)), pltpu.SemaphoreType.DMA((7 * n,)),
                        pltpu.SemaphoreType.DMA((n,))],
    )(*shards)


def _grad_exchange(slots):
    n = len(slots)

    def body(*refs):
        ins, outs = refs[:n], refs[n:2 * n]
        send_sems, recv_sems, local_sems = refs[2 * n:]
        me = _my_coords()
        my = _lin(me)
        local = [pltpu.make_async_copy(ins[a].at[my], outs[a].at[my], local_sems.at[a]) for a in range(n)]
        for cp in local:
            cp.start()
        sends = []
        for a in range(n):
            for k in range(1, N_DEV):
                peer = _peer(me, k)
                cp = pltpu.make_async_remote_copy(
                    src_ref=ins[a].at[_lin(peer)], dst_ref=outs[a].at[my], send_sem=send_sems.at[a * 7 + k - 1],
                    recv_sem=recv_sems.at[a * 7 + k - 1], device_id=peer, device_id_type=pl.DeviceIdType.MESH)
                cp.start()
                sends.append(cp)
        for a in range(n):
            for k in range(1, N_DEV):
                peer = _peer(me, k)
                pltpu.make_async_remote_copy(
                    src_ref=ins[a].at[my], dst_ref=outs[a].at[_lin(peer)], send_sem=send_sems.at[a * 7 + k - 1],
                    recv_sem=recv_sems.at[a * 7 + k - 1], device_id=peer,
                    device_id_type=pl.DeviceIdType.MESH).wait_recv()
        for cp in sends:
            cp.wait_send()
        for cp in local:
            cp.wait()

    any_spec = pl.BlockSpec(memory_space=pl.ANY)
    return pl.pallas_call(
        body, name="grad_exchange",
        in_specs=[any_spec] * n, out_specs=[any_spec] * n,
        out_shape=[_sds(s.shape, s.dtype) for s in slots],
        scratch_shapes=[pltpu.SemaphoreType.DMA((7 * n,)), pltpu.SemaphoreType.DMA((7 * n,)),
                        pltpu.SemaphoreType.DMA((n,))],
    )(*slots)


def _small_all_reduce(v):
    R = v.shape[0]

    def body(v_ref, o_ref, buf, send_sems, recv_sems):
        me = _my_coords()
        my = _lin(me)
        sends = []
        for k in range(1, N_DEV):
            cp = pltpu.make_async_remote_copy(
                src_ref=v_ref, dst_ref=buf.at[my], send_sem=send_sems.at[k - 1], recv_sem=recv_sems.at[k - 1],
                device_id=_peer(me, k), device_id_type=pl.DeviceIdType.MESH)
            cp.start()
            sends.append(cp)
        buf[my] = v_ref[...]
        for k in range(1, N_DEV):
            pltpu.make_async_remote_copy(
                src_ref=v_ref, dst_ref=buf.at[_lin(_peer(me, k))], send_sem=send_sems.at[k - 1],
                recv_sem=recv_sems.at[k - 1], device_id=_peer(me, k),
                device_id_type=pl.DeviceIdType.MESH).wait_recv()
        for cp in sends:
            cp.wait_send()
        tot = buf[0]
        for d in range(1, N_DEV):
            tot = tot + buf[d]
        o_ref[...] = tot

    vm = pl.BlockSpec(memory_space=pltpu.VMEM)
    return pl.pallas_call(
        body, name="small_all_reduce", in_specs=[vm], out_specs=vm, out_shape=_sds(v.shape, F32),
        scratch_shapes=[pltpu.VMEM((N_DEV, R, LANES), F32), pltpu.SemaphoreType.DMA((7,)),
                        pltpu.SemaphoreType.DMA((7,))],
    )(v)


def _adamw_math(w, g, m, v):
    m = ADAM_B1 * m + (1.0 - ADAM_B1) * g
    v = ADAM_B2 * v + (1.0 - ADAM_B2) * (g * g)
    m_hat = m / (1.0 - ADAM_B1 ** ADAM_STEP)
    v_hat = v / (1.0 - ADAM_B2 ** ADAM_STEP)
    delta = -ADAM_LR * (m_hat / (jnp.sqrt(v_hat) + ADAM_EPS) + ADAM_WD * w)
    return delta, m, v


def _adamw(parts, w, m, v, name, tr):
    P, R, C = parts.shape
    tr = min(tr, R)

    def body(p_ref, w_ref, m_ref, v_ref, g_out, d_out, m_out, v_out):
        g = p_ref[0].astype(F32)
        for d in range(1, P):
            g = g + p_ref[d].astype(F32)
        delta, m_new, v_new = _adamw_math(w_ref[...], g, m_ref[...], v_ref[...])
        g_out[...] = g
        d_out[...] = delta
        m_out[...] = m_new
        v_out[...] = v_new

    tile = pl.BlockSpec((tr, C), lambda i: (i, 0))
    return pl.pallas_call(
        body, name=name, grid=(R // tr,),
        in_specs=[pl.BlockSpec((P, tr, C), lambda i: (0, i, 0)), tile, tile, tile],
        out_specs=[tile] * 4, out_shape=[_sds((R, C), F32)] * 4,
        compiler_params=_cp(("parallel",), 32))(parts, w, m, v)


SMALL = (("norm_g", D_MODEL), ("mla_q_a_norm", MLA_Q_LORA), ("mla_kv_a_norm", MLA_KV_LORA), ("mla_q_norm", MLA_QK),
         ("mla_k_norm", MLA_QK), ("swa_q_norm", SWA_HEAD_DIM), ("swa_k_norm", SWA_HEAD_DIM), ("swa_sinks", HEADS))
SMALL_GRAD_KEY = dict(norm_g="norm_g", mla_q_a_norm="qa", mla_kv_a_norm="kva", mla_q_norm="qn", mla_k_norm="kn",
                      swa_q_norm="sqn", swa_k_norm="skn", swa_sinks="sinks")
SMALL_ROWS = 32
CONV_ROWS = 24


def _pack_small(get):
    parts = []
    for l in range(DEPTH):
        for name, n in SMALL:
            v = get(name, l).reshape(-1)
            parts.append(jnp.pad(v, (0, (-n) % LANES)))
    return jnp.concatenate(parts).reshape(SMALL_ROWS, LANES)


def _unpack_small(packed):
    flat = packed.reshape(-1)
    out = {name: [] for name, _ in SMALL}
    off = 0
    for l in range(DEPTH):
        for name, n in SMALL:
            out[name].append(flat[off:off + n])
            off += n + (-n) % LANES
    return {name: jnp.stack(v) for name, v in out.items()}


def kernel(x, norm_g, w_in, mla_q_a_norm, mla_w_qb, mla_kv_a_norm, mla_w_kvb, mla_q_norm, mla_k_norm, conv_w, swa_q_norm, swa_k_norm, swa_sinks, w_out, loss_target, m_norm_g, m_w_in, m_mla_q_a_norm, m_mla_w_qb, m_mla_kv_a_norm, m_mla_w_kvb, m_mla_q_norm, m_mla_k_norm, m_conv_w, m_swa_q_norm, m_swa_k_norm, m_swa_sinks, m_w_out, v_norm_g, v_w_in, v_mla_q_a_norm, v_mla_w_qb, v_mla_kv_a_norm, v_mla_w_kvb, v_mla_q_norm, v_mla_k_norm, v_conv_w, v_swa_q_norm, v_swa_k_norm, v_swa_sinks, v_w_out):
    T = x.shape[1]
    weights = dict(norm_g=norm_g, w_in=w_in, mla_q_a_norm=mla_q_a_norm, mla_w_qb=mla_w_qb,
                   mla_kv_a_norm=mla_kv_a_norm, mla_w_kvb=mla_w_kvb, mla_q_norm=mla_q_norm, mla_k_norm=mla_k_norm,
                   conv_w=conv_w, swa_q_norm=swa_q_norm, swa_k_norm=swa_k_norm, swa_sinks=swa_sinks, w_out=w_out)
    mom_m = dict(norm_g=m_norm_g, w_in=m_w_in, mla_q_a_norm=m_mla_q_a_norm, mla_w_qb=m_mla_w_qb,
                 mla_kv_a_norm=m_mla_kv_a_norm, mla_w_kvb=m_mla_w_kvb, mla_q_norm=m_mla_q_norm,
                 mla_k_norm=m_mla_k_norm, conv_w=m_conv_w, swa_q_norm=m_swa_q_norm, swa_k_norm=m_swa_k_norm,
                 swa_sinks=m_swa_sinks, w_out=m_w_out)
    mom_v = dict(norm_g=v_norm_g, w_in=v_w_in, mla_q_a_norm=v_mla_q_a_norm, mla_w_qb=v_mla_w_qb,
                 mla_kv_a_norm=v_mla_kv_a_norm, mla_w_kvb=v_mla_w_kvb, mla_q_norm=v_mla_q_norm,
                 mla_k_norm=v_mla_k_norm, conv_w=v_conv_w, swa_q_norm=v_swa_q_norm, swa_k_norm=v_swa_k_norm,
                 swa_sinks=v_swa_sinks, w_out=v_w_out)

    g_win, g_wqb, g_wkvb, g_wout, g_conv = _all_gather([
        w_in.astype(MXU_DTYPE), mla_w_qb.astype(MXU_DTYPE), mla_w_kvb.astype(MXU_DTYPE), w_out.astype(MXU_DTYPE),
        conv_w])
    lws = []
    for l in range(DEPTH):
        w_in_full = jnp.transpose(g_win[:, l], (1, 0, 2)).reshape(D_MODEL, IN_COLS)
        conv_full = jnp.transpose(g_conv[:, l], (1, 0, 2)).reshape(3, GROUP_WIDTH)
        lws.append(_layer_weights(l, norm_g, w_in_full, mla_q_a_norm, g_wqb[:, l], mla_kv_a_norm, g_wkvb[:, l],
                                  mla_q_norm, mla_k_norm, conv_full, swa_q_norm, swa_k_norm, swa_sinks,
                                  g_wout[:, l].reshape(D_MIX, D_MODEL)))

    loss_tile, grad_x, grads = _local_step(x[0], loss_target[0], lws, _rope_tables(T))

    big_slots = [
        jnp.stack([jnp.transpose(g["w_in"].reshape(D_MODEL, N_DEV, IN_COLS // N_DEV), (1, 0, 2)) for g in grads], 1),
        jnp.stack([g["w_out"].reshape(N_DEV, D_MIX // N_DEV, D_MODEL) for g in grads], 1),
        jnp.stack([g["w_qb"] for g in grads], 1),
        jnp.stack([g["w_kvb"] for g in grads], 1),
    ]
    r_win, r_wout, r_wqb, r_wkvb = _grad_exchange(big_slots)

    small = jnp.concatenate([
        _pack_small(lambda name, l: grads[l][SMALL_GRAD_KEY[name]]),
        jnp.stack([g["conv"] for g in grads]).reshape(CONV_ROWS, LANES),
        loss_tile], axis=0)
    small = _small_all_reduce(small)
    loss = small[SMALL_ROWS + CONV_ROWS, 0]
    my = _lin(_my_coords())
    conv_g = lax.dynamic_slice_in_dim(small[SMALL_ROWS:SMALL_ROWS + CONV_ROWS].reshape(DEPTH, 3, GROUP_WIDTH),
                                      my * 64, 64, axis=2)

    out = {}

    def big(name, recv, rows, cols, tr):
        res = _adamw(recv.reshape(N_DEV, rows, cols), weights[name].reshape(rows, cols),
                     mom_m[name].reshape(rows, cols), mom_v[name].reshape(rows, cols), "adamw_" + name, tr)
        out[name] = [r.reshape(weights[name].shape) for r in res]

    big("w_in", r_win, DEPTH * D_MODEL, IN_COLS // N_DEV, 256)
    big("w_out", r_wout, DEPTH * D_MIX // N_DEV, D_MODEL, 192)
    big("mla_w_qb", r_wqb, DEPTH * MLA_Q_LORA, MLA_QK, 512)
    big("mla_w_kvb", r_wkvb, DEPTH * MLA_KV_LORA, 128, 256)

    pad_conv = lambda a: jnp.pad(a.reshape(-1), (0, 8 * LANES - 6 * 64)).reshape(8, LANES)
    cat = lambda src: jnp.concatenate([_pack_small(lambda name, l: src[name][l]), pad_conv(src["conv_w"])], axis=0)
    g_small = jnp.concatenate([small[:SMALL_ROWS], pad_conv(conv_g)], axis=0)
    res = _adamw(g_small[None], cat(weights), cat(mom_m), cat(mom_v), "adamw_small", SMALL_ROWS + 8)
    smalls = [_unpack_small(r[:SMALL_ROWS]) for r in res]
    for name, _ in SMALL:
        out[name] = [s[name] for s in smalls]
    out["conv_w"] = [r[SMALL_ROWS:].reshape(-1)[:6 * 64].reshape(DEPTH, 3, 64) for r in res]

    order = ["norm_g", "w_in", "mla_q_a_norm", "mla_w_qb", "mla_kv_a_norm", "mla_w_kvb", "mla_q_norm", "mla_k_norm",
             "conv_w", "swa_q_norm", "swa_k_norm", "swa_sinks", "w_out"]
    result = [loss, grad_x[None]]
    for idx in range(4):
        result += [out[name][idx] for name in order]
    return tuple(result)
```

```python
import functools

import jax
import jax.numpy as jnp
import numpy as np
from jax import lax
from jax.experimental import pallas as pl
from jax.experimental.pallas import tpu as pltpu

F32 = jnp.float32
MXU_DTYPE = jnp.bfloat16
WIRE_DTYPE = jnp.bfloat16

N_DEV = 8
DEPTH = 2
D_MODEL = 1024
GROUP_WIDTH = 512
D_MIX = 3 * GROUP_WIDTH
BLOCK = 128
RMS_EPS = 1e-6
NEG_INF = -1e30
HEADS = 8
MLA_QK = 96
MLA_NOPE = 64
MLA_ROPE = 32
MLA_Q_LORA = 256
MLA_KV_LORA = 128
ROPE_THETA = 10000.0
SWA_HEAD_DIM = 64
LANES = 128
IN_COLS = 4256

ADAM_LR = 0.001
ADAM_B1 = 0.9
ADAM_B2 = 0.999
ADAM_EPS = 1e-08
ADAM_WD = 0.01
ADAM_STEP = 10

NP = 4352
CB_QLAT = 0
CB_KVLAT = 2
CB_KROPE = 3
CB_GMLA, CB_CH, CB_CB, CB_CC, CB_GCONV, CB_SQ, CB_GSWA = 1, 2, 3, 4, 5, 6, 7
CB_SK, CB_SV = 32, 33

TM_PROJ = 256
TM_ROW = 256
TK = 256
TQ = 2 * TK
MLA_SCALE = MLA_QK ** -0.5
LOG2E = 1.4426950408889634
LN2 = 0.6931471805599453
TM_SWA = 256
VMEM_MB = 2 ** 20


def _cp(sem, vmem_mb):
    return pltpu.CompilerParams(dimension_semantics=sem, vmem_limit_bytes=vmem_mb * VMEM_MB)


def _sds(shape, dtype):
    return jax.ShapeDtypeStruct(shape, dtype)


def _dot(a, b):
    return jnp.dot(a, b, preferred_element_type=F32)


def _dot_nt(a, b):
    return lax.dot_general(a, b, (((1,), (1,)), ((), ())), preferred_element_type=F32)


def _dot_tn(a, b):
    return lax.dot_general(a, b, (((0,), (0,)), ((), ())), preferred_element_type=F32)


def _rms(x, n):
    r = lax.rsqrt(jnp.sum(x * x, axis=-1, keepdims=True) * (1.0 / n) + RMS_EPS)
    return x * r, r


def _rms_bwd(dy, xhat, r, w, n):
    g = dy * w
    return r * (g - xhat * (jnp.sum(g * xhat, axis=-1, keepdims=True) * (1.0 / n)))


def _rms_halves(x, half1):
    x2 = x * x
    s0 = jnp.sum(jnp.where(half1, 0.0, x2), axis=-1, keepdims=True)
    s1 = jnp.sum(jnp.where(half1, x2, 0.0), axis=-1, keepdims=True)
    r = jnp.where(half1, lax.rsqrt(s1 * (1.0 / 64) + RMS_EPS), lax.rsqrt(s0 * (1.0 / 64) + RMS_EPS))
    return x * r, r


def _rms_halves_bwd(dy, xhat, r, w, half1):
    g = dy * w
    t = g * xhat
    m0 = jnp.sum(jnp.where(half1, 0.0, t), axis=-1, keepdims=True) * (1.0 / 64)
    m1 = jnp.sum(jnp.where(half1, t, 0.0), axis=-1, keepdims=True) * (1.0 / 64)
    return r * (g - xhat * jnp.where(half1, m1, m0))


def _sigmoid(x):
    return 1.0 / (1.0 + jnp.exp(-x))


def _rope(x, c, s1, s2):
    return x * c + pltpu.roll(x, 112, 1) * s1 + pltpu.roll(x, 16, 1) * s2


def _rope_bwd(dy, c, s1, s2):
    return dy * c + pltpu.roll(dy * s1, 16, 1) + pltpu.roll(dy * s2, 112, 1)


def _fold_rows8(x):
    return jnp.sum(x.reshape(x.shape[0] // 8, 8, x.shape[1]), axis=0)


def _row0(v, rows=8):
    row = lax.broadcasted_iota(jnp.int32, (rows, v.shape[1]), 0)
    return jnp.where(row == 0, jnp.broadcast_to(v, (rows, v.shape[1])), 0.0)


def _mm_nn(a, b, name, out_dtype=F32, residual=None, tm=TM_PROJ):
    M, K = a.shape
    N = b.shape[1]
    tm = min(tm, M)

    def body(*refs):
        if residual is None:
            a_ref, b_ref, o_ref = refs
            acc = _dot(a_ref[...].astype(MXU_DTYPE), b_ref[...])
        else:
            a_ref, b_ref, r_ref, o_ref = refs
            acc = _dot(a_ref[...].astype(MXU_DTYPE), b_ref[...]) + r_ref[...]
        o_ref[...] = acc.astype(out_dtype)

    in_specs = [pl.BlockSpec((tm, K), lambda i: (i, 0)), pl.BlockSpec((K, N), lambda i: (0, 0))]
    args = [a, b]
    if residual is not None:
        in_specs.append(pl.BlockSpec((tm, N), lambda i: (i, 0)))
        args.append(residual)
    return pl.pallas_call(
        body, name=name, grid=(M // tm,), in_specs=in_specs,
        out_specs=pl.BlockSpec((tm, N), lambda i: (i, 0)), out_shape=_sds((M, N), out_dtype),
        compiler_params=_cp(("parallel",), 48))(*args)


def _mm_tn(a, b, name, out_dtype, tn, tk=512):
    T, M = a.shape
    N = b.shape[1]
    tk = min(tk, T)
    nk = T // tk

    def body(a_ref, b_ref, o_ref, acc_ref):
        k = pl.program_id(1)

        @pl.when(k == 0)
        def _():
            acc_ref[...] = jnp.zeros_like(acc_ref)

        acc_ref[...] += _dot_tn(a_ref[...].astype(MXU_DTYPE), b_ref[...].astype(MXU_DTYPE))

        @pl.when(k == nk - 1)
        def _():
            o_ref[...] = acc_ref[...].astype(out_dtype)

    return pl.pallas_call(
        body, name=name, grid=(N // tn, nk),
        in_specs=[pl.BlockSpec((tk, M), lambda n, k: (k, 0)), pl.BlockSpec((tk, tn), lambda n, k: (k, n))],
        out_specs=pl.BlockSpec((M, tn), lambda n, k: (0, n)), out_shape=_sds((M, N), out_dtype),
        scratch_shapes=[pltpu.VMEM((M, tn), F32)],
        compiler_params=_cp(("parallel", "arbitrary"), 48))(a, b)


def _inproj_fwd(x, ng, wp):
    T, D = x.shape
    tm = min(TM_PROJ, T)

    def body(x_ref, g_ref, w_ref, proj_ref, h_ref):
        xhat, _ = _rms(x_ref[...], D)
        h = (xhat * g_ref[...]).astype(MXU_DTYPE)
        h_ref[...] = h
        proj_ref[...] = _dot(h, w_ref[...])

    return pl.pallas_call(
        body, name="inproj_fwd", grid=(T // tm,),
        in_specs=[pl.BlockSpec((tm, D), lambda i: (i, 0)), pl.BlockSpec((1, D), lambda i: (0, 0)),
                  pl.BlockSpec((D, NP), lambda i: (0, 0))],
        out_specs=[pl.BlockSpec((tm, NP), lambda i: (i, 0)), pl.BlockSpec((tm, D), lambda i: (i, 0))],
        out_shape=[_sds((T, NP), F32), _sds((T, D), MXU_DTYPE)],
        compiler_params=_cp(("parallel",), 48))(x, ng, wp)


def _mla_prep_fwd(proj, lw, rope):
    T = proj.shape[0]
    tm = min(TK, T // 2)

    def body(ql_ref, kvl_ref, kr_ref, qa_ref, kva_ref, wq_ref, wk_ref, wv_ref, qn_ref, kn_ref,
             c_ref, s1_ref, s2_ref, q_out, k_out, kt_out, v_out, vt_out):
        c, s1, s2 = c_ref[...], s1_ref[...], s2_ref[...]
        qhat, _ = _rms(ql_ref[...], MLA_Q_LORA)
        qn = (qhat * qa_ref[...]).astype(MXU_DTYPE)
        khat, _ = _rms(kvl_ref[...], MLA_KV_LORA)
        kvn = (khat * kva_ref[...]).astype(MXU_DTYPE)
        kr = kr_ref[...]
        half1 = lax.broadcasted_iota(jnp.int32, (tm, LANES), 1) >= 64
        for h in range(HEADS):
            xh, _ = _rms(_dot(qn, wq_ref[h]), MLA_QK)
            q_out[h] = (_rope(xh * qn_ref[...], c, s1, s2) * (MLA_SCALE * LOG2E)).astype(MXU_DTYPE)
            xh, _ = _rms(_dot(kvn, wk_ref[h]) + kr, MLA_QK)
            kh = _rope(xh * kn_ref[...], c, s1, s2)
            k_out[h] = kh.astype(MXU_DTYPE)
            kt_out[h, 0] = kh.T.astype(MXU_DTYPE)
        v = _dot(kvn, wv_ref[...])
        for h in range(HEADS):
            vp = v[:, LANES * (h // 2):LANES * (h // 2 + 1)]
            own = half1 if h % 2 else jnp.logical_not(half1)
            vp = jnp.where(own, vp, 0.0)
            v_out[h] = vp.astype(MXU_DTYPE)
            vt_out[h, 0] = vp.T.astype(MXU_DTYPE)

    full = lambda shape: pl.BlockSpec(shape, lambda i: (0,) * len(shape))
    hd = pl.BlockSpec((HEADS, tm, LANES), lambda i: (0, i, 0))
    hdt = pl.BlockSpec((HEADS, 1, LANES, tm), lambda i: (0, i, 0, 0))
    nat = _sds((HEADS, T, LANES), MXU_DTYPE)
    tr = _sds((HEADS, T // tm, LANES, tm), MXU_DTYPE)
    return pl.pallas_call(
        body, name="mla_prep_fwd", grid=(T // tm,),
        in_specs=[pl.BlockSpec((tm, 256), lambda i: (i, CB_QLAT)), pl.BlockSpec((tm, LANES), lambda i: (i, CB_KVLAT)),
                  pl.BlockSpec((tm, LANES), lambda i: (i, CB_KROPE)),
                  full((1, 256)), full((1, LANES)), full((HEADS, 256, LANES)), full((HEADS, LANES, LANES)),
                  full((LANES, 512)), full((1, LANES)), full((1, LANES)),
                  pl.BlockSpec((tm, LANES), lambda i: (i, 0)), pl.BlockSpec((tm, LANES), lambda i: (i, 0)),
                  pl.BlockSpec((tm, LANES), lambda i: (i, 0))],
        out_specs=[hd, hd, hdt, hd, hdt],
        out_shape=[nat, nat, tr, nat, tr],
        compiler_params=_cp(("parallel",), 32))(
            proj, proj, proj, lw["qa"], lw["kva"], lw["wq"], lw["wk"], lw["wv"], lw["qn"], lw["kn"],
            rope[0], rope[1], rope[2])


def _mla_attn_fwd(q, k, vt):
    T = q.shape[1]
    tk = min(TK, T // 2)
    tq = 2 * tk

    def body(q_ref, k_ref, vt_ref, o_ref, lse_ref, acc_s, m_s, l_s, s_a, s_b):
        i = pl.program_id(1)
        key = lax.broadcasted_iota(jnp.int32, (tk, tq), 0)
        qry = lax.broadcasted_iota(jnp.int32, (tk, tq), 1)
        qs = [q_ref[0], q_ref[1]]
        acc_s[...] = jnp.zeros_like(acc_s)
        l_s[...] = jnp.zeros_like(l_s)
        m_s[...] = jnp.full(m_s.shape, NEG_INF, F32)

        def scores(kj, buf):
            rows = pl.ds(pl.multiple_of(kj * tk, tk), tk)
            for r in range(2):
                buf[r] = _dot_nt(k_ref[r, rows, :], qs[r])

        def consume(kj, buf, diag):
            for r in range(2):
                s = buf[r]
                if diag is not None:
                    s = jnp.where(key + diag * tk <= qry, s, NEG_INF)
                m_old = m_s[r]
                m_new = jnp.maximum(m_old, jnp.max(s, axis=0, keepdims=True))
                alpha = jnp.exp2(m_old - m_new)
                p = jnp.exp2(s - m_new)
                l_s[r] = alpha * l_s[r] + jnp.sum(p, axis=0, keepdims=True)
                m_s[r] = m_new
                acc_s[r] = alpha * acc_s[r] + _dot(vt_ref[r, kj], p.astype(MXU_DTYPE))

        scores(0, s_a)

        def pair(kj):
            scores(kj + 1, s_b)
            consume(kj, s_a, None)
            scores(kj + 2, s_a)
            consume(kj + 1, s_b, None)

        def quad(kq, carry):
            pair(4 * kq)
            pair(4 * kq + 2)
            return carry

        lax.fori_loop(0, i // 2, quad, 0)

        @pl.when(i % 2 == 1)
        def _():
            pair(2 * i - 2)

        scores(2 * i + 1, s_b)
        consume(2 * i, s_a, 0)
        consume(2 * i + 1, s_b, 1)
        o_t = acc_s[0] / l_s[0] + acc_s[1] / l_s[1]
        o_ref[...] = o_t.T
        for r in range(2):
            lse_ref[r] = m_s[r] + jnp.log2(l_s[r])

    return pl.pallas_call(
        body, name="mla_attn_fwd", grid=(HEADS // 2, T // tq),
        in_specs=[pl.BlockSpec((2, tq, LANES), lambda j, i: (j, i, 0)),
                  pl.BlockSpec((2, T, LANES), lambda j, i: (j, 0, 0)),
                  pl.BlockSpec((2, T // tk, LANES, tk), lambda j, i: (j, 0, 0, 0))],
        out_specs=[pl.BlockSpec((tq, LANES), lambda j, i: (i, j)),
                   pl.BlockSpec((2, 1, tq), lambda j, i: (j, 0, i))],
        out_shape=[_sds((T, GROUP_WIDTH), F32), _sds((HEADS, 1, T), F32)],
        scratch_shapes=[pltpu.VMEM((2, LANES, tq), F32), pltpu.VMEM((2, 1, tq), F32), pltpu.VMEM((2, 1, tq), F32),
                        pltpu.VMEM((2, tk, tq), F32), pltpu.VMEM((2, tk, tq), F32)],
        compiler_params=_cp(("parallel", "arbitrary"), 40))(q, k, vt)


def _swa_masks(nb_first):
    qi = lax.broadcasted_iota(jnp.int32, (BLOCK, 2 * BLOCK), 0)
    ki = lax.broadcasted_iota(jnp.int32, (BLOCK, 2 * BLOCK), 1)
    dist = BLOCK + qi - ki
    valid = (dist >= 0) & (dist < BLOCK) & ((ki >= BLOCK) | jnp.logical_not(nb_first))
    return dist.astype(F32), valid


def _swa_kv_variants(x, half1):
    xs = pltpu.roll(x, 64, 1)
    out = {}
    for g in range(2):
        for r in range(2):
            own = half1 if r else jnp.logical_not(half1)
            out[(g, r)] = jnp.where(own, x if g == r else xs, 0.0).astype(MXU_DTYPE)
    return out


def _swa_fwd(proj, lw):
    T = proj.shape[0]
    tm = min(TM_SWA, T)
    nb = tm // BLOCK
    scale = SWA_HEAD_DIM ** -0.5

    def body(q_ref, k_ref, v_ref, pk_ref, pv_ref, qw_ref, kw_ref, sink_ref, o_ref):
        i = pl.program_id(0)
        half1 = lax.broadcasted_iota(jnp.int32, (1, LANES), 1) >= 64
        k_all = jnp.concatenate([pk_ref[...], k_ref[...]], axis=0)
        v_all = jnp.concatenate([pv_ref[...], v_ref[...]], axis=0)
        khat, _ = _rms_halves(k_all, half1)
        kp = _swa_kv_variants(khat * kw_ref[...], half1)
        vp = _swa_kv_variants(v_all, half1)
        qn = []
        for j in range(4):
            qhat, _ = _rms_halves(q_ref[:, LANES * j:LANES * (j + 1)], half1)
            qn.append((qhat * qw_ref[...]).astype(MXU_DTYPE))
        for b in range(nb):
            dist, valid = _swa_masks((i == 0) & (b == 0))
            ks = slice(b * BLOCK, b * BLOCK + 2 * BLOCK)
            for j in range(4):
                g = j // 2
                qb = qn[j][b * BLOCK:(b + 1) * BLOCK]
                o = jnp.zeros((BLOCK, LANES), F32)
                for r in range(2):
                    h = 2 * j + r
                    s = _dot_nt(qb, kp[(g, r)][ks]) * scale - (2.0 ** -(h + 1)) * dist
                    s = jnp.where(valid, s, NEG_INF)
                    sink = sink_ref[h]
                    m = jnp.maximum(jnp.max(s, axis=-1, keepdims=True), sink)
                    e = jnp.exp(s - m)
                    den = jnp.sum(e, axis=-1, keepdims=True) + jnp.exp(sink - m)
                    o = o + _dot((e / den).astype(MXU_DTYPE), vp[(g, r)][ks])
                o_ref[b * BLOCK:(b + 1) * BLOCK, LANES * j:LANES * (j + 1)] = o

    prev = lambda cb: pl.BlockSpec((BLOCK, LANES), lambda i: (jnp.maximum(i * nb - 1, 0), cb))
    return pl.pallas_call(
        body, name="swa_fwd", grid=(T // tm,),
        in_specs=[pl.BlockSpec((tm, 512), lambda i: (i, CB_SQ)), pl.BlockSpec((tm, LANES), lambda i: (i, CB_SK)),
                  pl.BlockSpec((tm, LANES), lambda i: (i, CB_SV)), prev(CB_SK), prev(CB_SV),
                  pl.BlockSpec((1, LANES), lambda i: (0, 0)), pl.BlockSpec((1, LANES), lambda i: (0, 0)),
                  pl.BlockSpec(memory_space=pltpu.SMEM)],
        out_specs=pl.BlockSpec((tm, 512), lambda i: (i, 0)),
        out_shape=_sds((T, GROUP_WIDTH), F32),
        compiler_params=_cp(("parallel",), 32))(proj, proj, proj, proj, proj, lw["sqn"], lw["skn"], lw["sinks"])


def _shift_down(u, prev, n, row):
    tm = u.shape[0]
    out = pltpu.roll(u, n, 0)
    row8 = lax.broadcasted_iota(jnp.int32, prev.shape, 0)
    for t in range(n):
        src = jnp.sum(jnp.where(row8 == 8 - n + t, prev, 0.0), axis=0, keepdims=True)
        out = jnp.where(row == t, src, out)
    return out


def _shift_up(u, nxt, n, row):
    tm = u.shape[0]
    out = pltpu.roll(u, tm - n, 0)
    row8 = lax.broadcasted_iota(jnp.int32, nxt.shape, 0)
    for t in range(n):
        src = jnp.sum(jnp.where(row8 == t, nxt, 0.0), axis=0, keepdims=True)
        out = jnp.where(row == tm - n + t, src, out)
    return out


def _mix_fwd(proj, o_mla, o_swa, conv_w):
    T = proj.shape[0]
    tm = min(TM_ROW, T)

    def body(gm_ref, ch_ref, cb_ref, cc_ref, gc_ref, gs_ref, pch_ref, pcc_ref, om_ref, os_ref, w_ref, y_ref):
        i = pl.program_id(0)
        row = lax.broadcasted_iota(jnp.int32, (tm, GROUP_WIDTH), 0)
        u = cc_ref[...] * ch_ref[...]
        u_prev = jnp.where(i > 0, pcc_ref[...] * pch_ref[...], 0.0)
        z = (w_ref[0:1, :] * _shift_down(u, u_prev, 2, row) + w_ref[1:2, :] * _shift_down(u, u_prev, 1, row)
             + w_ref[2:3, :] * u)
        gm, gc, gs = gm_ref[...], gc_ref[...], gs_ref[...]
        y_ref[:, 0:512] = (om_ref[...] * (gm * _sigmoid(gm))).astype(MXU_DTYPE)
        y_ref[:, 512:1024] = (cb_ref[...] * z * (gc * _sigmoid(gc))).astype(MXU_DTYPE)
        y_ref[:, 1024:1536] = (os_ref[...] * (gs * _sigmoid(gs))).astype(MXU_DTYPE)

    blk = lambda cb: pl.BlockSpec((tm, 512), lambda i: (i, cb))
    prev = lambda cb: pl.BlockSpec((8, 512), lambda i: (jnp.maximum(i * (tm // 8) - 1, 0), cb))
    tile = pl.BlockSpec((tm, 512), lambda i: (i, 0))
    return pl.pallas_call(
        body, name="mix_fwd", grid=(T // tm,),
        in_specs=[blk(CB_GMLA), blk(CB_CH), blk(CB_CB), blk(CB_CC), blk(CB_GCONV), blk(CB_GSWA),
                  prev(CB_CH), prev(CB_CC), tile, tile, pl.BlockSpec((8, 512), lambda i: (0, 0))],
        out_specs=pl.BlockSpec((tm, D_MIX), lambda i: (i, 0)),
        out_shape=_sds((T, D_MIX), MXU_DTYPE),
        compiler_params=_cp(("parallel",), 32))(
            proj, proj, proj, proj, proj, proj, proj, proj, o_mla, o_swa, conv_w)


def _loss_grad(y, target):
    T, D = y.shape
    tm = min(TM_ROW, T)
    nt = T // tm

    def body(y_ref, t_ref, g_ref, loss_ref, acc_ref):
        i = pl.program_id(0)

        @pl.when(i == 0)
        def _():
            acc_ref[...] = jnp.zeros_like(acc_ref)

        err = y_ref[...] - t_ref[...]
        g_ref[...] = err * (1.0 / D)
        acc_ref[...] += _fold_rows8(err * err)

        @pl.when(i == nt - 1)
        def _():
            tot = jnp.sum(jnp.sum(acc_ref[...], axis=1, keepdims=True), axis=0, keepdims=True)
            loss_ref[...] = jnp.broadcast_to(tot * (0.5 / D), (8, LANES))

    return pl.pallas_call(
        body, name="loss_grad", grid=(nt,),
        in_specs=[pl.BlockSpec((tm, D), lambda i: (i, 0)), pl.BlockSpec((tm, D), lambda i: (i, 0))],
        out_specs=[pl.BlockSpec((tm, D), lambda i: (i, 0)), pl.BlockSpec((8, LANES), lambda i: (0, 0))],
        out_shape=[_sds((T, D), F32), _sds((8, LANES), F32)],
        scratch_shapes=[pltpu.VMEM((8, D), F32)],
        compiler_params=_cp(("arbitrary",), 32))(y, target)


def _mix_bwd(dycat, proj, o_mla, o_swa, conv_w):
    T = proj.shape[0]
    tm = min(TM_ROW, T)
    nt = T // tm

    def body(dym_ref, dyc_ref, dys_ref, gm_ref, ch_ref, cb_ref, cc_ref, gc_ref, gs_ref, pch_ref, pcc_ref,
             ndy_ref, ncb_ref, ngc_ref, om_ref, os_ref, w_ref,
             d1_ref, dgs_ref, dom_ref, dos_ref, dw_ref):
        i = pl.program_id(0)

        @pl.when(i == 0)
        def _():
            dw_ref[...] = jnp.zeros_like(dw_ref)

        row = lax.broadcasted_iota(jnp.int32, (tm, GROUP_WIDTH), 0)

        def gate(g):
            sg = _sigmoid(g)
            return g * sg, sg * (1.0 + g * (1.0 - sg))

        gm = gm_ref[...]
        silu, dsilu = gate(gm)
        dym = dym_ref[...]
        dom_ref[...] = dym * silu
        d1_ref[:, 0:512] = (dym * om_ref[...] * dsilu).astype(MXU_DTYPE)

        gs = gs_ref[...]
        silu, dsilu = gate(gs)
        dys = dys_ref[...]
        dos_ref[...] = dys * silu
        dgs_ref[...] = (dys * os_ref[...] * dsilu).astype(MXU_DTYPE)

        ch, cb, cc, gc, dyc = ch_ref[...], cb_ref[...], cc_ref[...], gc_ref[...], dyc_ref[...]
        w0, w1, w2 = w_ref[0:1, :], w_ref[1:2, :], w_ref[2:3, :]
        u = cc * ch
        u_prev = jnp.where(i > 0, pcc_ref[...] * pch_ref[...], 0.0)
        u1 = _shift_down(u, u_prev, 1, row)
        u2 = _shift_down(u, u_prev, 2, row)
        z = w0 * u2 + w1 * u1 + w2 * u
        silu, dsilu = gate(gc)
        dz = dyc * cb * silu
        ngc = ngc_ref[...]
        dz_next = jnp.where(i < nt - 1, ndy_ref[...] * ncb_ref[...] * (ngc * _sigmoid(ngc)), 0.0)
        du = w2 * dz + w1 * _shift_up(dz, dz_next, 1, row) + w0 * _shift_up(dz, dz_next, 2, row)
        d1_ref[:, 512:1024] = (du * cc).astype(MXU_DTYPE)
        d1_ref[:, 1024:1536] = (dyc * z * silu).astype(MXU_DTYPE)
        d1_ref[:, 1536:2048] = (du * ch).astype(MXU_DTYPE)
        d1_ref[:, 2048:2560] = (dyc * cb * z * dsilu).astype(MXU_DTYPE)
        row8 = lax.broadcasted_iota(jnp.int32, (8, GROUP_WIDTH), 0)
        dw = jnp.zeros((8, GROUP_WIDTH), F32)
        for t, shifted in enumerate((u2, u1, u)):
            dw = dw + jnp.where(row8 == t, jnp.sum(dz * shifted, axis=0, keepdims=True), 0.0)
        dw_ref[...] += dw

    blk = lambda cb: pl.BlockSpec((tm, 512), lambda i: (i, cb))
    prev = lambda cb: pl.BlockSpec((8, 512), lambda i: (jnp.maximum(i * (tm // 8) - 1, 0), cb))
    nxt = lambda cb: pl.BlockSpec((8, 512), lambda i: (jnp.minimum((i + 1) * (tm // 8), T // 8 - 1), cb))
    tile = pl.BlockSpec((tm, 512), lambda i: (i, 0))
    return pl.pallas_call(
        body, name="mix_bwd", grid=(nt,),
        in_specs=[blk(0), blk(1), blk(2), blk(CB_GMLA), blk(CB_CH), blk(CB_CB), blk(CB_CC), blk(CB_GCONV),
                  blk(CB_GSWA), prev(CB_CH), prev(CB_CC), nxt(1), nxt(CB_CB), nxt(CB_GCONV), tile, tile,
                  pl.BlockSpec((8, 512), lambda i: (0, 0))],
        out_specs=[pl.BlockSpec((tm, 2560), lambda i: (i, 0)), tile, tile, tile,
                   pl.BlockSpec((8, 512), lambda i: (0, 0))],
        out_shape=[_sds((T, 2560), MXU_DTYPE), _sds((T, 512), MXU_DTYPE), _sds((T, 512), F32),
                   _sds((T, 512), F32), _sds((8, 512), F32)],
        compiler_params=_cp(("arbitrary",), 48))(
            dycat, dycat, dycat, proj, proj, proj, proj, proj, proj, proj, proj, dycat, proj, proj,
            o_mla, o_swa, conv_w)


def _swa_bwd(proj, o_swa, do_swa, lw):
    T = proj.shape[0]
    tm = min(TM_SWA, T)
    nb = tm // BLOCK
    scale = SWA_HEAD_DIM ** -0.5

    def body(q_ref, k_ref, v_ref, pk_ref, pv_ref, o_ref, do_ref, qw_ref, kw_ref, sink_ref,
             dq_ref, dk_ref, dv_ref, dqw_ref, dsink_ref):
        i = pl.program_id(0)

        @pl.when(i == 0)
        def _():
            dk_ref[...] = jnp.zeros_like(dk_ref)
            dv_ref[...] = jnp.zeros_like(dv_ref)
            dqw_ref[...] = jnp.zeros_like(dqw_ref)
            dsink_ref[...] = jnp.zeros_like(dsink_ref)

        half1 = lax.broadcasted_iota(jnp.int32, (1, LANES), 1) >= 64
        k_all = jnp.concatenate([pk_ref[...], k_ref[...]], axis=0)
        v_all = jnp.concatenate([pv_ref[...], v_ref[...]], axis=0)
        khat, _ = _rms_halves(k_all, half1)
        kp = _swa_kv_variants(khat * kw_ref[...], half1)
        vp = _swa_kv_variants(v_all, half1)
        qw = qw_ref[...]
        dqw = jnp.zeros((1, LANES), F32)
        dsink_rows = [jnp.zeros((1, 1), F32) for _ in range(HEADS)]
        for j in range(4):
            g = j // 2
            cols = slice(LANES * j, LANES * (j + 1))
            qhat, qr = _rms_halves(q_ref[:, cols], half1)
            qn = (qhat * qw).astype(MXU_DTYPE)
            do = do_ref[:, cols]
            dob = do.astype(MXU_DTYPE)
            prod = do * o_ref[:, cols]
            dqn_blocks = []
            for b in range(nb):
                dist, valid = _swa_masks((i == 0) & (b == 0))
                ks = slice(b * BLOCK, b * BLOCK + 2 * BLOCK)
                rs = slice(b * BLOCK, (b + 1) * BLOCK)
                qb = qn[rs]
                dqn = jnp.zeros((BLOCK, LANES), F32)
                for r in range(2):
                    h = 2 * j + r
                    own = half1 if r else jnp.logical_not(half1)
                    s = _dot_nt(qb, kp[(g, r)][ks]) * scale - (2.0 ** -(h + 1)) * dist
                    s = jnp.where(valid, s, NEG_INF)
                    sink = sink_ref[h]
                    m = jnp.maximum(jnp.max(s, axis=-1, keepdims=True), sink)
                    e = jnp.exp(s - m)
                    es = jnp.exp(sink - m)
                    inv = 1.0 / (jnp.sum(e, axis=-1, keepdims=True) + es)
                    p = e * inv
                    dd = jnp.sum(jnp.where(own, prod[rs], 0.0), axis=-1, keepdims=True)
                    dp = _dot_nt(dob[rs], vp[(g, r)][ks])
                    ds = (p * (dp - dd) * scale).astype(MXU_DTYPE)
                    dsink_rows[h] = dsink_rows[h] - jnp.sum(es * inv * dd, axis=0, keepdims=True)
                    dqn = dqn + _dot(ds, kp[(g, r)][ks])
                    dkp = jnp.where(own, _dot_tn(ds, qb), 0.0)
                    dvp = jnp.where(own, _dot_tn(p.astype(MXU_DTYPE), dob[rs]), 0.0)
                    if g != r:
                        dkp = pltpu.roll(dkp, 64, 1)
                        dvp = pltpu.roll(dvp, 64, 1)
                    dst = pl.ds(pl.multiple_of((i * nb + b) * BLOCK, BLOCK), 2 * BLOCK)
                    dk_ref[dst, :] += dkp
                    dv_ref[dst, :] += dvp
                dqn_blocks.append(dqn)
            dqn = jnp.concatenate(dqn_blocks, axis=0) if nb > 1 else dqn_blocks[0]
            dqw = dqw + jnp.sum(dqn * qhat, axis=0, keepdims=True)
            dq_ref[:, cols] = _rms_halves_bwd(dqn, qhat, qr, qw, half1).astype(MXU_DTYPE)
        dqw_ref[...] += _row0(dqw + pltpu.roll(dqw, 64, 1))
        row8 = lax.broadcasted_iota(jnp.int32, (8, LANES), 0)
        dsink = jnp.zeros((8, LANES), F32)
        for h in range(HEADS):
            dsink = dsink + jnp.where(row8 == h, jnp.broadcast_to(dsink_rows[h], (8, LANES)), 0.0)
        dsink_ref[...] += dsink

    prev = lambda cb: pl.BlockSpec((BLOCK, LANES), lambda i: (jnp.maximum(i * nb - 1, 0), cb))
    tile = pl.BlockSpec((tm, 512), lambda i: (i, 0))
    small = pl.BlockSpec((8, LANES), lambda i: (0, 0))
    acc = pl.BlockSpec((T + BLOCK, LANES), lambda i: (0, 0))
    return pl.pallas_call(
        body, name="swa_bwd", grid=(T // tm,),
        in_specs=[pl.BlockSpec((tm, 512), lambda i: (i, CB_SQ)), pl.BlockSpec((tm, LANES), lambda i: (i, CB_SK)),
                  pl.BlockSpec((tm, LANES), lambda i: (i, CB_SV)), prev(CB_SK), prev(CB_SV), tile, tile,
                  pl.BlockSpec((1, LANES), lambda i: (0, 0)), pl.BlockSpec((1, LANES), lambda i: (0, 0)),
                  pl.BlockSpec(memory_space=pltpu.SMEM)],
        out_specs=[tile, acc, acc, small, small],
        out_shape=[_sds((T, 512), MXU_DTYPE), _sds((T + BLOCK, LANES), F32), _sds((T + BLOCK, LANES), F32),
                   _sds((8, LANES), F32), _sds((8, LANES), F32)],
        compiler_params=_cp(("arbitrary",), 40))(
            proj, proj, proj, proj, proj, o_swa, do_swa, lw["sqn"], lw["skn"], lw["sinks"])


def _swa_kv_bwd(proj, dkn, dv, lw):
    T = proj.shape[0]
    tm = BLOCK

    def body(k_ref, dkn_ref, dv_ref, kw_ref, d_ref, dkw_ref):
        i = pl.program_id(0)

        @pl.when(i == 0)
        def _():
            dkw_ref[...] = jnp.zeros_like(dkw_ref)

        half1 = lax.broadcasted_iota(jnp.int32, (1, LANES), 1) >= 64
        khat, kr = _rms_halves(k_ref[...], half1)
        dkn_t = dkn_ref[...]
        dkw = jnp.sum(dkn_t * khat, axis=0, keepdims=True)
        dkw_ref[...] += _row0(dkw + pltpu.roll(dkw, 64, 1))
        d_ref[:, 0:LANES] = _rms_halves_bwd(dkn_t, khat, kr, kw_ref[...], half1).astype(MXU_DTYPE)
        d_ref[:, LANES:2 * LANES] = dv_ref[...].astype(MXU_DTYPE)

    return pl.pallas_call(
        body, name="swa_kv_bwd", grid=(T // tm,),
        in_specs=[pl.BlockSpec((tm, LANES), lambda i: (i, CB_SK)), pl.BlockSpec((tm, LANES), lambda i: (i + 1, 0)),
                  pl.BlockSpec((tm, LANES), lambda i: (i + 1, 0)), pl.BlockSpec((1, LANES), lambda i: (0, 0))],
        out_specs=[pl.BlockSpec((tm, 2 * LANES), lambda i: (i, 0)), pl.BlockSpec((8, LANES), lambda i: (0, 0))],
        out_shape=[_sds((T, 2 * LANES), MXU_DTYPE), _sds((8, LANES), F32)],
        compiler_params=_cp(("arbitrary",), 32))(proj, dkn, dv, lw["skn"])


def _mla_attn_bwd(q, k, kt, vpad, o, do, lse):
    T = q.shape[1]
    tk = min(TK, T // 2)
    tq = 2 * tk

    def body(q_ref, k_ref, kt_ref, v_ref, o_ref, do_ref, lse_ref, dq_ref, dk_ref, dv_ref, dqt_s,
             s_a, s_b, p_a, p_b):
        h = pl.program_id(0)
        i = pl.program_id(1)

        @pl.when(i == 0)
        def _():
            dk_ref[...] = jnp.zeros_like(dk_ref)
            dv_ref[...] = jnp.zeros_like(dv_ref)

        key = lax.broadcasted_iota(jnp.int32, (tk, tq), 0)
        qry = lax.broadcasted_iota(jnp.int32, (tk, tq), 1)
        own = (lax.broadcasted_iota(jnp.int32, (1, LANES), 1) // 64) == (h % 2)
        own_rows = (lax.broadcasted_iota(jnp.int32, (LANES, 1), 0) // 64) == (h % 2)
        do_t = do_ref[...]
        dob = do_t.astype(MXU_DTYPE)
        prod_t = (do_t * o_ref[...]).T
        dd = jnp.sum(jnp.where(own_rows, prod_t, 0.0), axis=0, keepdims=True)
        qh = q_ref[0]
        lse_t = lse_ref[0]
        dqt_s[...] = jnp.zeros_like(dqt_s)

        def scores(kj, s_buf, p_buf):
            rows = pl.ds(pl.multiple_of(kj * tk, tk), tk)
            s_buf[...] = _dot_nt(k_ref[0, rows, :], qh)
            p_buf[...] = _dot_nt(v_ref[0, rows, :], dob)

        def consume(kj, s_buf, p_buf, diag):
            rows = pl.ds(pl.multiple_of(kj * tk, tk), tk)
            s = s_buf[...]
            if diag is not None:
                s = jnp.where(key + diag * tk <= qry, s, NEG_INF)
            p = jnp.exp2(s - lse_t)
            ds = (p * (p_buf[...] - dd)).astype(MXU_DTYPE)
            dqt_s[...] += _dot(kt_ref[0, kj], ds)
            dk_ref[0, rows, :] += _dot(ds, qh)
            dv_ref[0, rows, :] += jnp.where(own, _dot(p.astype(MXU_DTYPE), dob), 0.0)

        scores(0, s_a, p_a)

        def pair(kj):
            scores(kj + 1, s_b, p_b)
            consume(kj, s_a, p_a, None)
            scores(kj + 2, s_a, p_a)
            consume(kj + 1, s_b, p_b, None)

        def quad(kq, carry):
            pair(4 * kq)
            pair(4 * kq + 2)
            return carry

        lax.fori_loop(0, i // 2, quad, 0)

        @pl.when(i % 2 == 1)
        def _():
            pair(2 * i - 2)

        scores(2 * i + 1, s_b, p_b)
        consume(2 * i, s_a, p_a, 0)
        consume(2 * i + 1, s_b, p_b, 1)
        dq_ref[0] = dqt_s[...].T

    res = pl.BlockSpec((1, T, LANES), lambda h, i: (h, 0, 0))
    res_t = pl.BlockSpec((1, T // tk, LANES, tk), lambda h, i: (h, 0, 0, 0))
    buf = pltpu.VMEM((tk, tq), F32)
    return pl.pallas_call(
        body, name="mla_attn_bwd", grid=(HEADS, T // tq),
        in_specs=[pl.BlockSpec((1, tq, LANES), lambda h, i: (h, i, 0)), res, res_t, res,
                  pl.BlockSpec((tq, LANES), lambda h, i: (i, h // 2)),
                  pl.BlockSpec((tq, LANES), lambda h, i: (i, h // 2)),
                  pl.BlockSpec((1, 1, tq), lambda h, i: (h, 0, i))],
        out_specs=[pl.BlockSpec((1, tq, LANES), lambda h, i: (h, i, 0)), res, res],
        out_shape=[_sds((HEADS, T, LANES), F32)] * 3,
        scratch_shapes=[pltpu.VMEM((LANES, tq), F32), buf, buf, buf, buf],
        compiler_params=_cp(("parallel", "arbitrary"), 48))(q, k, kt, vpad, o, do, lse)


def _mla_prep_bwd(proj, dq, dk, dv, lw, rope):
    T = proj.shape[0]
    tm = min(TK, T // 2)

    def body(ql_ref, kvl_ref, kr_ref, dq_ref, dk_ref, dv_ref, qa_ref, kva_ref, wq_ref, wk_ref, wv_ref,
             wqt_ref, wkt_ref, wvt_ref, qn_ref, kn_ref, c_ref, s1_ref, s2_ref,
             d_ref, dwq_ref, dwk_ref, dwv_ref, dqa_ref, dkva_ref, dqn_ref, dkn_ref):
        i = pl.program_id(0)

        @pl.when(i == 0)
        def _():
            for ref in (dwq_ref, dwk_ref, dwv_ref, dqa_ref, dkva_ref, dqn_ref, dkn_ref):
                ref[...] = jnp.zeros_like(ref)

        c, s1, s2 = c_ref[...], s1_ref[...], s2_ref[...]
        lane = lax.broadcasted_iota(jnp.int32, (1, LANES), 1)
        qlhat, qlr = _rms(ql_ref[...], MLA_Q_LORA)
        qn = (qlhat * qa_ref[...]).astype(MXU_DTYPE)
        kvhat, kvr = _rms(kvl_ref[...], MLA_KV_LORA)
        kvn = (kvhat * kva_ref[...]).astype(MXU_DTYPE)
        kr = kr_ref[...]
        dqnl = jnp.zeros((tm, MLA_Q_LORA), F32)
        dkvn = jnp.zeros((tm, MLA_KV_LORA), F32)
        dkr = jnp.zeros((tm, LANES), F32)
        dqw = jnp.zeros((1, LANES), F32)
        dkw = jnp.zeros((1, LANES), F32)
        for h in range(HEADS):
            xh, r = _rms(_dot(qn, wq_ref[h]), MLA_QK)
            dy = _rope_bwd(dq_ref[h] * MLA_SCALE, c, s1, s2)
            dqw = dqw + jnp.sum(dy * xh, axis=0, keepdims=True)
            dx = _rms_bwd(dy, xh, r, qn_ref[...], MLA_QK).astype(MXU_DTYPE)
            dwq_ref[h] += _dot_tn(qn, dx)
            dqnl = dqnl + _dot(dx, wqt_ref[h])

            xh, r = _rms(_dot(kvn, wk_ref[h]) + kr, MLA_QK)
            dy = _rope_bwd(dk_ref[h] * LN2, c, s1, s2)
            dkw = dkw + jnp.sum(dy * xh, axis=0, keepdims=True)
            dxf = _rms_bwd(dy, xh, r, kn_ref[...], MLA_QK)
            dkr = dkr + dxf
            dx = dxf.astype(MXU_DTYPE)
            dwk_ref[h] += _dot_tn(kvn, dx)
            dkvn = dkvn + _dot(dx, wkt_ref[h])
        dvc = jnp.concatenate([dv_ref[2 * j] + dv_ref[2 * j + 1] for j in range(4)], axis=1).astype(MXU_DTYPE)
        dwv_ref[...] += _dot_tn(kvn, dvc)
        dkvn = dkvn + _dot(dvc, wvt_ref[...])
        dqa_ref[...] += _row0(jnp.sum(dqnl * qlhat, axis=0, keepdims=True))
        dkva_ref[...] += _row0(jnp.sum(dkvn * kvhat, axis=0, keepdims=True))
        dqn_ref[...] += _row0(dqw)
        dkn_ref[...] += _row0(dkw)
        d_ref[:, 0:256] = _rms_bwd(dqnl, qlhat, qlr, qa_ref[...], MLA_Q_LORA).astype(MXU_DTYPE)
        d_ref[:, 256:384] = _rms_bwd(dkvn, kvhat, kvr, kva_ref[...], MLA_KV_LORA).astype(MXU_DTYPE)
        d_ref[:, 384:512] = jnp.where((lane >= 64) & (lane < 96), dkr, 0.0).astype(MXU_DTYPE)

    full = lambda shape: pl.BlockSpec(shape, lambda i: (0,) * len(shape))
    hd = pl.BlockSpec((HEADS, tm, LANES), lambda i: (0, i, 0))
    tab = pl.BlockSpec((tm, LANES), lambda i: (i, 0))
    return pl.pallas_call(
        body, name="mla_prep_bwd", grid=(T // tm,),
        in_specs=[pl.BlockSpec((tm, 256), lambda i: (i, CB_QLAT)), pl.BlockSpec((tm, LANES), lambda i: (i, CB_KVLAT)),
                  pl.BlockSpec((tm, LANES), lambda i: (i, CB_KROPE)), hd, hd, hd,
                  full((1, 256)), full((1, LANES)), full((HEADS, 256, LANES)), full((HEADS, LANES, LANES)),
                  full((LANES, 512)), full((HEADS, LANES, 256)), full((HEADS, LANES, LANES)), full((512, LANES)),
                  full((1, LANES)), full((1, LANES)), tab, tab, tab],
        out_specs=[pl.BlockSpec((tm, 512), lambda i: (i, 0)), full((HEADS, 256, LANES)),
                   full((HEADS, LANES, LANES)), full((LANES, 512)), full((8, 256)), full((8, LANES)),
                   full((8, LANES)), full((8, LANES))],
        out_shape=[_sds((T, 512), MXU_DTYPE), _sds((HEADS, 256, LANES), F32), _sds((HEADS, LANES, LANES), F32),
                   _sds((LANES, 512), F32), _sds((8, 256), F32), _sds((8, LANES), F32), _sds((8, LANES), F32),
                   _sds((8, LANES), F32)],
        compiler_params=_cp(("arbitrary",), 48))(
            proj, proj, proj, dq, dk, dv, lw["qa"], lw["kva"], lw["wq"], lw["wk"], lw["wv"],
            lw["wqt"], lw["wkt"], lw["wvt"], lw["qn"], lw["kn"], rope[0], rope[1], rope[2])


def _norm_bwd(dh, x, g_in, ng):
    T, D = x.shape
    tm = min(TM_ROW, T)

    def body(dh_ref, x_ref, g_ref, w_ref, dx_ref, dw_ref):
        i = pl.program_id(0)

        @pl.when(i == 0)
        def _():
            dw_ref[...] = jnp.zeros_like(dw_ref)

        xhat, r = _rms(x_ref[...], D)
        dh_t = dh_ref[...]
        dw_ref[...] += _row0(jnp.sum(dh_t * xhat, axis=0, keepdims=True))
        dx_ref[...] = g_ref[...] + _rms_bwd(dh_t, xhat, r, w_ref[...], D)

    tile = pl.BlockSpec((tm, D), lambda i: (i, 0))
    return pl.pallas_call(
        body, name="norm_bwd", grid=(T // tm,),
        in_specs=[tile, tile, tile, pl.BlockSpec((1, D), lambda i: (0, 0))],
        out_specs=[tile, pl.BlockSpec((8, D), lambda i: (0, 0))],
        out_shape=[_sds((T, D), F32), _sds((8, D), F32)],
        compiler_params=_cp(("arbitrary",), 32))(dh, x, g_in, ng)


def _rope_tables(T):
    half = MLA_ROPE // 2
    inv_freq = jnp.power(jnp.float32(ROPE_THETA), -jnp.arange(half, dtype=F32) / half)
    ang = jnp.arange(T, dtype=F32)[:, None] * inv_freq[None, :]
    cos, sin = jnp.cos(ang), jnp.sin(ang)
    z = lambda n: jnp.zeros((T, n), F32)
    c = jnp.concatenate([jnp.ones((T, MLA_NOPE), F32), cos, cos, z(32)], axis=1)
    s1 = jnp.concatenate([z(64), -sin, z(48)], axis=1)
    s2 = jnp.concatenate([z(80), sin, z(32)], axis=1)
    return c, s1, s2


def _pad_lanes(v, n=LANES):
    v = v.reshape(1, -1)
    return jnp.pad(v, ((0, 0), (0, n - v.shape[1])))


def _pack_win(w):
    z = lambda n: jnp.zeros((w.shape[0], n), w.dtype)
    return jnp.concatenate([w[:, 0:384], z(64), w[:, 384:416], z(32), w[:, 416:2976], w[:, 2976:3488],
                            w[:, 3744:4256], w[:, 3488:3616], w[:, 3616:3744]], axis=1)


def _unpack_dwin(d):
    return jnp.concatenate([d[:, 0:384], d[:, 448:480], d[:, 512:3072], d[:, 3072:3584], d[:, 4096:4224],
                            d[:, 4224:4352], d[:, 3584:4096]], axis=1)


def _layer_weights(l, norm_g, w_in_full, qa, wqb_full, kva, wkvb_full, qn, kn, conv_full, sqn, skn, sinks,
                   w_out_full):
    wp = _pack_win(w_in_full)
    wq = jnp.pad(wqb_full, ((0, 0), (0, 0), (0, LANES - MLA_QK)))
    wk = jnp.pad(wkvb_full[:, :, :MLA_NOPE], ((0, 0), (0, 0), (0, LANES - MLA_NOPE)))
    wv = jnp.transpose(wkvb_full[:, :, MLA_NOPE:], (1, 0, 2)).reshape(MLA_KV_LORA, GROUP_WIDTH)
    return dict(
        ng=norm_g[l].reshape(1, -1), wp=wp, wpt=wp.T, qa=qa[l].reshape(1, -1), kva=kva[l].reshape(1, -1),
        wq=wq, wk=wk, wv=wv, wqt=jnp.transpose(wq, (0, 2, 1)), wkt=jnp.transpose(wk, (0, 2, 1)), wvt=wv.T,
        qn=_pad_lanes(qn[l]), kn=_pad_lanes(kn[l]),
        conv=jnp.pad(conv_full, ((0, 5), (0, 0))),
        sqn=jnp.tile(sqn[l].reshape(1, -1), (1, 2)), skn=jnp.tile(skn[l].reshape(1, -1), (1, 2)),
        sinks=sinks[l], wo=w_out_full, wot=w_out_full.T)


def _layer_fwd(x, lw, rope):
    proj, h = _inproj_fwd(x, lw["ng"], lw["wp"])
    q, k, kt, vpad, vt = _mla_prep_fwd(proj, lw, rope)
    o_mla, lse = _mla_attn_fwd(q, k, vt)
    o_swa = _swa_fwd(proj, lw)
    ycat = _mix_fwd(proj, o_mla, o_swa, lw["conv"])
    x_next = _mm_nn(ycat, lw["wo"], "outproj_fwd", residual=x)
    return x_next, dict(x=x, proj=proj, h=h, q=q, k=k, kt=kt, vpad=vpad, o_mla=o_mla, lse=lse, o_swa=o_swa, ycat=ycat)


def _layer_bwd(g, sv, lw, rope):
    proj = sv["proj"]
    dycat = _mm_nn(g, lw["wot"], "outproj_bwd_dy")
    d_wo = _mm_tn(sv["ycat"], g, "outproj_bwd_dw", WIRE_DTYPE, tn=D_MODEL)
    d1, dgs, do_mla, do_swa, d_conv = _mix_bwd(dycat, proj, sv["o_mla"], sv["o_swa"], lw["conv"])
    dsq, dkn_acc, dv_acc, d_sqn, d_sinks = _swa_bwd(proj, sv["o_swa"], do_swa, lw)
    dskv, d_skn = _swa_kv_bwd(proj, dkn_acc, dv_acc, lw)
    dq, dk, dv = _mla_attn_bwd(sv["q"], sv["k"], sv["kt"], sv["vpad"], sv["o_mla"], do_mla, sv["lse"])
    dmla, d_wq, d_wk, d_wv, d_qa, d_kva, d_qn, d_kn = _mla_prep_bwd(proj, dq, dk, dv, lw, rope)
    dproj = jnp.concatenate([dmla, d1, dsq, dgs, dskv], axis=1)
    dh = _mm_nn(dproj, lw["wpt"], "inproj_bwd_dh")
    dx, d_ng = _norm_bwd(dh, sv["x"], g, lw["ng"])
    d_wp = _mm_tn(sv["h"], dproj, "inproj_bwd_dw", WIRE_DTYPE, tn=NP // 2)
    grads = dict(
        w_in=_unpack_dwin(d_wp), w_out=d_wo,
        w_qb=d_wq[:, :, :MLA_QK],
        w_kvb=jnp.concatenate([d_wk[:, :, :MLA_NOPE],
                               jnp.transpose(d_wv.reshape(MLA_KV_LORA, HEADS, MLA_NOPE), (1, 0, 2))], axis=2),
        conv=d_conv[0:3], norm_g=d_ng[0], qa=d_qa[0], kva=d_kva[0], qn=d_qn[0, :MLA_QK], kn=d_kn[0, :MLA_QK],
        sqn=d_sqn[0, :SWA_HEAD_DIM], skn=d_skn[0, :SWA_HEAD_DIM], sinks=d_sinks[:, 0])
    return dx, grads


def _local_step(x, target, lws, rope):
    saved = []
    for lw in lws:
        x, sv = _layer_fwd(x, lw, rope)
        saved.append(sv)
    g, loss_tile = _loss_grad(x, target)
    grads = [None] * len(lws)
    for l in reversed(range(len(lws))):
        g, grads[l] = _layer_bwd(g, saved[l], lws[l], rope)
    return loss_tile, g, grads


def _my_coords():
    return lax.axis_index("x"), lax.axis_index("y"), lax.axis_index("c")


def _peer(me, k):
    x, y, c = me
    return (1 - x if k & 4 else x, 1 - y if k & 2 else y, 1 - c if k & 1 else c)


def _lin(d):
    return 4 * d[0] + 2 * d[1] + d[2]


def _all_gather(shards):
    n = len(shards)

    def body(*refs):
        ins, outs = refs[:n], refs[n:2 * n]
        send_sems, recv_sems, local_sems = refs[2 * n:]
        me = _my_coords()
        my = _lin(me)
        local = [pltpu.make_async_copy(ins[a], outs[a].at[my], local_sems.at[a]) for a in range(n)]
        for cp in local:
            cp.start()
        sends = []
        for a in range(n):
            for k in range(1, N_DEV):
                cp = pltpu.make_async_remote_copy(
                    src_ref=ins[a], dst_ref=outs[a].at[my], send_sem=send_sems.at[a * 7 + k - 1],
                    recv_sem=recv_sems.at[a * 7 + k - 1], device_id=_peer(me, k),
                    device_id_type=pl.DeviceIdType.MESH)
                cp.start()
                sends.append(cp)
        for a in range(n):
            for k in range(1, N_DEV):
                src = _lin(_peer(me, k))
                pltpu.make_async_remote_copy(
                    src_ref=ins[a], dst_ref=outs[a].at[src], send_sem=send_sems.at[a * 7 + k - 1],
                    recv_sem=recv_sems.at[a * 7 + k - 1], device_id=_peer(me, k),
                    device_id_type=pl.DeviceIdType.MESH).wait_recv()
        for cp in sends:
            cp.wait_send()
        for cp in local:
            cp.wait()

    any_spec = pl.BlockSpec(memory_space=pl.ANY)
    return pl.pallas_call(
        body, name="weight_all_gather",
        in_specs=[any_spec] * n, out_specs=[any_spec] * n,
        out_shape=[_sds((N_DEV,) + s.shape, s.dtype) for s in shards],
        scratch_shapes=[pltpu.SemaphoreType.DMA((7 * n,)), pltpu.SemaphoreType.DMA((7 * n,)),
                        pltpu.SemaphoreType.DMA((n,))],
    )(*shards)


def _grad_exchange(slots):
    n = len(slots)

    def body(*refs):
        ins, outs = refs[:n], refs[n:2 * n]
        send_sems, recv_sems, local_sems = refs[2 * n:]
        me = _my_coords()
        my = _lin(me)
        local = [pltpu.make_async_copy(ins[a].at[my], outs[a].at[my], local_sems.at[a]) for a in range(n)]
        for cp in local:
            cp.start()
        sends = []
        for a in range(n):
            for k in range(1, N_DEV):
                peer = _peer(me, k)
                cp = pltpu.make_async_remote_copy(
                    src_ref=ins[a].at[_lin(peer)], dst_ref=outs[a].at[my], send_sem=send_sems.at[a * 7 + k - 1],
                    recv_sem=recv_sems.at[a * 7 + k - 1], device_id=peer, device_id_type=pl.DeviceIdType.MESH)
                cp.start()
                sends.append(cp)
        for a in range(n):
            for k in range(1, N_DEV):
                peer = _peer(me, k)
                pltpu.make_async_remote_copy(
                    src_ref=ins[a].at[my], dst_ref=outs[a].at[_lin(peer)], send_sem=send_sems.at[a * 7 + k - 1],
                    recv_sem=recv_sems.at[a * 7 + k - 1], device_id=peer,
                    device_id_type=pl.DeviceIdType.MESH).wait_recv()
        for cp in sends:
            cp.wait_send()
        for cp in local:
            cp.wait()

    any_spec = pl.BlockSpec(memory_space=pl.ANY)
    return pl.pallas_call(
        body, name="grad_exchange",
        in_specs=[any_spec] * n, out_specs=[any_spec] * n,
        out_shape=[_sds(s.shape, s.dtype) for s in slots],
        scratch_shapes=[pltpu.SemaphoreType.DMA((7 * n,)), pltpu.SemaphoreType.DMA((7 * n,)),
                        pltpu.SemaphoreType.DMA((n,))],
    )(*slots)


def _push_copies(ins, lands, send_sems, recv_sems, gather):
    me = _my_coords()
    my = _lin(me)
    out, inc = [], []
    for a in range(len(ins)):
        for k in range(1, N_DEV):
            peer = _peer(me, k)
            sems = dict(send_sem=send_sems.at[a * 7 + k - 1], recv_sem=recv_sems.at[a * 7 + k - 1],
                        device_id=peer, device_id_type=pl.DeviceIdType.MESH)
            src = ins[a] if gather else ins[a].at[_lin(peer)]
            out.append(pltpu.make_async_remote_copy(src_ref=src, dst_ref=lands[a].at[my], **sems))
            inc.append(pltpu.make_async_remote_copy(src_ref=src, dst_ref=lands[a].at[_lin(peer)], **sems))
    return out, inc


def _push_start(arrays, name, gather):
    n = len(arrays)
    land_shapes = [((N_DEV,) + a.shape) if gather else a.shape for a in arrays]

    def body(*refs):
        ins, lands = refs[:n], refs[n:2 * n]
        send_sems, recv_sems = refs[2 * n], refs[2 * n + 1]
        token = refs[-1]
        out, _ = _push_copies(ins, lands, send_sems, recv_sems, gather)
        for cp in out:
            cp.start()
        token[...] = jnp.zeros_like(token)

    hbm = pl.BlockSpec(memory_space=pltpu.HBM)
    sem = pl.BlockSpec(memory_space=pltpu.SEMAPHORE)
    res = pl.pallas_call(
        body, name=name,
        out_shape=(pltpu.SemaphoreType.DMA((7 * n,)), pltpu.SemaphoreType.DMA((7 * n,)),
                   *[pltpu.HBM(a.shape, a.dtype) for a in arrays],
                   *[pltpu.HBM(s, a.dtype) for s, a in zip(land_shapes, arrays)],
                   _sds((8, LANES), F32)),
        in_specs=(hbm,) * (2 * n),
        out_specs=(sem, sem) + (hbm,) * (2 * n) + (pl.BlockSpec(memory_space=pltpu.VMEM),),
        input_output_aliases={i: 2 + i for i in range(2 * n)},
        compiler_params=pltpu.CompilerParams(has_side_effects=pltpu.SideEffectType.DATAFLOW_SIDE_EFFECTING),
    )(*[pltpu.with_memory_space_constraint(a, pltpu.HBM) for a in arrays],
      *[pltpu.with_memory_space_constraint(lax.empty(s, a.dtype), pltpu.HBM) for s, a in zip(land_shapes, arrays)])
    return dict(send=res[0], recv=res[1], src=res[2:2 + n], land=res[2 + n:2 + 2 * n], token=res[-1][0, 0],
                gather=gather)


def _push_wait(handle, after, name):
    n = len(handle["src"])
    gather = handle["gather"]

    def body(*refs):
        ins, lands = refs[:n], refs[n:2 * n]
        send_sems, recv_sems = refs[2 * n], refs[2 * n + 1]
        out, inc = _push_copies(ins, lands, send_sems, recv_sems, gather)
        for cp in out:
            cp.wait_send()
        for cp in inc:
            cp.wait_recv()

    hbm = pl.BlockSpec(memory_space=pltpu.HBM)
    sem = pl.BlockSpec(memory_space=pltpu.SEMAPHORE)
    res = pl.pallas_call(
        body, name=name,
        out_shape=tuple(pltpu.HBM(a.shape, a.dtype) for a in (*handle["src"], *handle["land"])),
        in_specs=(hbm,) * (2 * n) + (sem, sem, pl.BlockSpec(memory_space=pl.ANY)),
        out_specs=(hbm,) * (2 * n),
        input_output_aliases={i: i for i in range(2 * n)},
        compiler_params=pltpu.CompilerParams(has_side_effects=pltpu.SideEffectType.DATAFLOW_SIDE_EFFECTING),
    )(*handle["src"], *handle["land"], handle["send"], handle["recv"], after)
    return res[n:]


def _small_all_reduce(v):
    R = v.shape[0]

    def body(v_ref, o_ref, buf, send_sems, recv_sems):
        me = _my_coords()
        my = _lin(me)
        sends = []
        for k in range(1, N_DEV):
            cp = pltpu.make_async_remote_copy(
                src_ref=v_ref, dst_ref=buf.at[my], send_sem=send_sems.at[k - 1], recv_sem=recv_sems.at[k - 1],
                device_id=_peer(me, k), device_id_type=pl.DeviceIdType.MESH)
            cp.start()
            sends.append(cp)
        buf[my] = v_ref[...]
        for k in range(1, N_DEV):
            pltpu.make_async_remote_copy(
                src_ref=v_ref, dst_ref=buf.at[_lin(_peer(me, k))], send_sem=send_sems.at[k - 1],
                recv_sem=recv_sems.at[k - 1], device_id=_peer(me, k),
                device_id_type=pl.DeviceIdType.MESH).wait_recv()
        for cp in sends:
            cp.wait_send()
        tot = buf[0]
        for d in range(1, N_DEV):
            tot = tot + buf[d]
        o_ref[...] = tot

    vm = pl.BlockSpec(memory_space=pltpu.VMEM)
    return pl.pallas_call(
        body, name="small_all_reduce", in_specs=[vm], out_specs=vm, out_shape=_sds(v.shape, F32),
        scratch_shapes=[pltpu.VMEM((N_DEV, R, LANES), F32), pltpu.SemaphoreType.DMA((7,)),
                        pltpu.SemaphoreType.DMA((7,))],
    )(v)


def _adamw_math(w, g, m, v):
    m = ADAM_B1 * m + (1.0 - ADAM_B1) * g
    v = ADAM_B2 * v + (1.0 - ADAM_B2) * (g * g)
    m_hat = m / (1.0 - ADAM_B1 ** ADAM_STEP)
    v_hat = v / (1.0 - ADAM_B2 ** ADAM_STEP)
    delta = -ADAM_LR * (m_hat / (jnp.sqrt(v_hat) + ADAM_EPS) + ADAM_WD * w)
    return delta, m, v


def _adamw(parts, w, m, v, name, tr):
    P, R, C = parts.shape
    tr = min(tr, R)

    def body(p_ref, w_ref, m_ref, v_ref, g_out, d_out, m_out, v_out):
        g = p_ref[0].astype(F32)
        for d in range(1, P):
            g = g + p_ref[d].astype(F32)
        delta, m_new, v_new = _adamw_math(w_ref[...], g, m_ref[...], v_ref[...])
        g_out[...] = g
        d_out[...] = delta
        m_out[...] = m_new
        v_out[...] = v_new

    tile = pl.BlockSpec((tr, C), lambda i: (i, 0))
    return pl.pallas_call(
        body, name=name, grid=(R // tr,),
        in_specs=[pl.BlockSpec((P, tr, C), lambda i: (0, i, 0)), tile, tile, tile],
        out_specs=[tile] * 4, out_shape=[_sds((R, C), F32)] * 4,
        compiler_params=_cp(("parallel",), 32))(parts, w, m, v)


SMALL = (("norm_g", D_MODEL), ("mla_q_a_norm", MLA_Q_LORA), ("mla_kv_a_norm", MLA_KV_LORA), ("mla_q_norm", MLA_QK),
         ("mla_k_norm", MLA_QK), ("swa_q_norm", SWA_HEAD_DIM), ("swa_k_norm", SWA_HEAD_DIM), ("swa_sinks", HEADS))
SMALL_GRAD_KEY = dict(norm_g="norm_g", mla_q_a_norm="qa", mla_kv_a_norm="kva", mla_q_norm="qn", mla_k_norm="kn",
                      swa_q_norm="sqn", swa_k_norm="skn", swa_sinks="sinks")
SMALL_ROWS = 32
CONV_ROWS = 24


def _pack_small(get):
    parts = []
    for l in range(DEPTH):
        for name, n in SMALL:
            v = get(name, l).reshape(-1)
            parts.append(jnp.pad(v, (0, (-n) % LANES)))
    return jnp.concatenate(parts).reshape(SMALL_ROWS, LANES)


def _unpack_small(packed):
    flat = packed.reshape(-1)
    out = {name: [] for name, _ in SMALL}
    off = 0
    for l in range(DEPTH):
        for name, n in SMALL:
            out[name].append(flat[off:off + n])
            off += n + (-n) % LANES
    return {name: jnp.stack(v) for name, v in out.items()}


def kernel(x, norm_g, w_in, mla_q_a_norm, mla_w_qb, mla_kv_a_norm, mla_w_kvb, mla_q_norm, mla_k_norm, conv_w, swa_q_norm, swa_k_norm, swa_sinks, w_out, loss_target, m_norm_g, m_w_in, m_mla_q_a_norm, m_mla_w_qb, m_mla_kv_a_norm, m_mla_w_kvb, m_mla_q_norm, m_mla_k_norm, m_conv_w, m_swa_q_norm, m_swa_k_norm, m_swa_sinks, m_w_out, v_norm_g, v_w_in, v_mla_q_a_norm, v_mla_w_qb, v_mla_kv_a_norm, v_mla_w_kvb, v_mla_q_norm, v_mla_k_norm, v_conv_w, v_swa_q_norm, v_swa_k_norm, v_swa_sinks, v_w_out):
    T = x.shape[1]
    weights = dict(norm_g=norm_g, w_in=w_in, mla_q_a_norm=mla_q_a_norm, mla_w_qb=mla_w_qb,
                   mla_kv_a_norm=mla_kv_a_norm, mla_w_kvb=mla_w_kvb, mla_q_norm=mla_q_norm, mla_k_norm=mla_k_norm,
                   conv_w=conv_w, swa_q_norm=swa_q_norm, swa_k_norm=swa_k_norm, swa_sinks=swa_sinks, w_out=w_out)
    mom_m = dict(norm_g=m_norm_g, w_in=m_w_in, mla_q_a_norm=m_mla_q_a_norm, mla_w_qb=m_mla_w_qb,
                 mla_kv_a_norm=m_mla_kv_a_norm, mla_w_kvb=m_mla_w_kvb, mla_q_norm=m_mla_q_norm,
                 mla_k_norm=m_mla_k_norm, conv_w=m_conv_w, swa_q_norm=m_swa_q_norm, swa_k_norm=m_swa_k_norm,
                 swa_sinks=m_swa_sinks, w_out=m_w_out)
    mom_v = dict(norm_g=v_norm_g, w_in=v_w_in, mla_q_a_norm=v_mla_q_a_norm, mla_w_qb=v_mla_w_qb,
                 mla_kv_a_norm=v_mla_kv_a_norm, mla_w_kvb=v_mla_w_kvb, mla_q_norm=v_mla_q_norm,
                 mla_k_norm=v_mla_k_norm, conv_w=v_conv_w, swa_q_norm=v_swa_q_norm, swa_k_norm=v_swa_k_norm,
                 swa_sinks=v_swa_sinks, w_out=v_w_out)

    my = _lin(_my_coords())
    rope = _rope_tables(T)

    def shards(l):
        return [w_in[l].astype(MXU_DTYPE), mla_w_qb[l].astype(MXU_DTYPE), mla_w_kvb[l].astype(MXU_DTYPE),
                w_out[l].astype(MXU_DTYPE), conv_w[l]]

    def layer_weights(l, gathered):
        g_win, g_wqb, g_wkvb, g_wout, g_conv = gathered
        return _layer_weights(
            l, norm_g, jnp.transpose(g_win, (1, 0, 2)).reshape(D_MODEL, IN_COLS), mla_q_a_norm, g_wqb,
            mla_kv_a_norm, g_wkvb, mla_q_norm, mla_k_norm, jnp.transpose(g_conv, (1, 0, 2)).reshape(3, GROUP_WIDTH),
            swa_q_norm, swa_k_norm, swa_sinks, g_wout.reshape(D_MIX, D_MODEL))

    def slots(g):
        return [jnp.transpose(g["w_in"].reshape(D_MODEL, N_DEV, IN_COLS // N_DEV), (1, 0, 2)),
                g["w_out"].reshape(N_DEV, D_MIX // N_DEV, D_MODEL), g["w_qb"], g["w_kvb"]]

    def own_slot(landed, mine):
        return [lax.dynamic_update_index_in_dim(a, m, my, 0) for a, m in zip(landed, mine)]

    lw0 = layer_weights(0, _all_gather(shards(0)))
    gather1 = _push_start(shards(1), "weight_gather_start", gather=True)
    x1, sv0 = _layer_fwd(x[0], dict(lw0, ng=lw0["ng"] + gather1["token"]), rope)
    lw1 = layer_weights(1, own_slot(_push_wait(gather1, x1, "weight_gather_wait"), shards(1)))
    x2, sv1 = _layer_fwd(x1, lw1, rope)
    g2, loss_tile = _loss_grad(x2, loss_target[0])

    g1, grads1 = _layer_bwd(g2, sv1, lw1, rope)
    slots1 = slots(grads1)
    exchange1 = _push_start(slots1, "grad_exchange_start", gather=False)
    grad_x, grads0 = _layer_bwd(g1, sv0, dict(lw0, conv=lw0["conv"] + exchange1["token"]), rope)
    recv1 = own_slot(_push_wait(exchange1, grad_x, "grad_exchange_wait"), [s[my] for s in slots1])
    recv0 = _grad_exchange(slots(grads0))
    grads = [grads0, grads1]
    r_win, r_wout, r_wqb, r_wkvb = [jnp.stack([a, b], axis=1) for a, b in zip(recv0, recv1)]

    small = jnp.concatenate([
        _pack_small(lambda name, l: grads[l][SMALL_GRAD_KEY[name]]),
        jnp.stack([g["conv"] for g in grads]).reshape(CONV_ROWS, LANES),
        loss_tile], axis=0)
    small = _small_all_reduce(small)
    loss = small[SMALL_ROWS + CONV_ROWS, 0]
    my = _lin(_my_coords())
    conv_g = lax.dynamic_slice_in_dim(small[SMALL_ROWS:SMALL_ROWS + CONV_ROWS].reshape(DEPTH, 3, GROUP_WIDTH),
                                      my * 64, 64, axis=2)

    out = {}

    def big(name, recv, rows, cols, tr):
        res = _adamw(recv.reshape(N_DEV, rows, cols), weights[name].reshape(rows, cols),
                     mom_m[name].reshape(rows, cols), mom_v[name].reshape(rows, cols), "adamw_" + name, tr)
        out[name] = [r.reshape(weights[name].shape) for r in res]

    big("w_in", r_win, DEPTH * D_MODEL, IN_COLS // N_DEV, 256)
    big("w_out", r_wout, DEPTH * D_MIX // N_DEV, D_MODEL, 192)
    big("mla_w_qb", r_wqb, DEPTH * MLA_Q_LORA, MLA_QK, 512)
    big("mla_w_kvb", r_wkvb, DEPTH * MLA_KV_LORA, 128, 256)

    pad_conv = lambda a: jnp.pad(a.reshape(-1), (0, 8 * LANES - 6 * 64)).reshape(8, LANES)
    cat = lambda src: jnp.concatenate([_pack_small(lambda name, l: src[name][l]), pad_conv(src["conv_w"])], axis=0)
    g_small = jnp.concatenate([small[:SMALL_ROWS], pad_conv(conv_g)], axis=0)
    res = _adamw(g_small[None], cat(weights), cat(mom_m), cat(mom_v), "adamw_small", SMALL_ROWS + 8)
    smalls = [_unpack_small(r[:SMALL_ROWS]) for r in res]
    for name, _ in SMALL:
        out[name] = [s[name] for s in smalls]
    out["conv_w"] = [r[SMALL_ROWS:].reshape(-1)[:6 * 64].reshape(DEPTH, 3, 64) for r in res]

    order = ["norm_g", "w_in", "mla_q_a_norm", "mla_w_qb", "mla_kv_a_norm", "mla_w_kvb", "mla_q_norm", "mla_k_norm",
             "conv_w", "swa_q_norm", "swa_k_norm", "swa_sinks", "w_out"]
    result = [loss, grad_x[None]]
    for idx in range(4):
        result += [out[name][idx] for name in order]
    return tuple(result)
```

```python
import functools

import jax
import jax.numpy as jnp
import numpy as np
from jax import lax
from jax.experimental import pallas as pl
from jax.experimental.pallas import tpu as pltpu

F32 = jnp.float32
MXU_DTYPE = jnp.bfloat16
WIRE_DTYPE = jnp.bfloat16

N_DEV = 8
DEPTH = 2
D_MODEL = 1024
GROUP_WIDTH = 512
D_MIX = 3 * GROUP_WIDTH
BLOCK = 128
RMS_EPS = 1e-6
NEG_INF = -1e30
HEADS = 8
MLA_QK = 96
MLA_NOPE = 64
MLA_ROPE = 32
MLA_Q_LORA = 256
MLA_KV_LORA = 128
ROPE_THETA = 10000.0
SWA_HEAD_DIM = 64
LANES = 128
IN_COLS = 4256

ADAM_LR = 0.001
ADAM_B1 = 0.9
ADAM_B2 = 0.999
ADAM_EPS = 1e-08
ADAM_WD = 0.01
ADAM_STEP = 10

NP = 4352
CB_QLAT = 0
CB_KVLAT = 2
CB_KROPE = 3
CB_GMLA, CB_CH, CB_CB, CB_CC, CB_GCONV, CB_SQ, CB_GSWA = 1, 2, 3, 4, 5, 6, 7
CB_SK, CB_SV = 32, 33

TM_PROJ = 256
TM_ROW = 256
TK = 256
TQ = 2 * TK
MLA_SCALE = MLA_QK ** -0.5
LOG2E = 1.4426950408889634
LN2 = 0.6931471805599453
TM_SWA = 256
VMEM_MB = 2 ** 20


def _cp(sem, vmem_mb):
    return pltpu.CompilerParams(dimension_semantics=sem, vmem_limit_bytes=vmem_mb * VMEM_MB)


def _sds(shape, dtype):
    return jax.ShapeDtypeStruct(shape, dtype)


def _dot(a, b):
    return jnp.dot(a, b, preferred_element_type=F32)


def _dot_nt(a, b):
    return lax.dot_general(a, b, (((1,), (1,)), ((), ())), preferred_element_type=F32)


def _dot_tn(a, b):
    return lax.dot_general(a, b, (((0,), (0,)), ((), ())), preferred_element_type=F32)


def _rms(x, n):
    r = lax.rsqrt(jnp.sum(x * x, axis=-1, keepdims=True) * (1.0 / n) + RMS_EPS)
    return x * r, r


def _rms_bwd(dy, xhat, r, w, n):
    g = dy * w
    return r * (g - xhat * (jnp.sum(g * xhat, axis=-1, keepdims=True) * (1.0 / n)))


def _rms_halves(x, half1):
    x2 = x * x
    s0 = jnp.sum(jnp.where(half1, 0.0, x2), axis=-1, keepdims=True)
    s1 = jnp.sum(jnp.where(half1, x2, 0.0), axis=-1, keepdims=True)
    r = jnp.where(half1, lax.rsqrt(s1 * (1.0 / 64) + RMS_EPS), lax.rsqrt(s0 * (1.0 / 64) + RMS_EPS))
    return x * r, r


def _rms_halves_bwd(dy, xhat, r, w, half1):
    g = dy * w
    t = g * xhat
    m0 = jnp.sum(jnp.where(half1, 0.0, t), axis=-1, keepdims=True) * (1.0 / 64)
    m1 = jnp.sum(jnp.where(half1, t, 0.0), axis=-1, keepdims=True) * (1.0 / 64)
    return r * (g - xhat * jnp.where(half1, m1, m0))


def _sigmoid(x):
    return 1.0 / (1.0 + jnp.exp(-x))


def _rope(x, c, s1, s2):
    return x * c + pltpu.roll(x, 112, 1) * s1 + pltpu.roll(x, 16, 1) * s2


def _rope_bwd(dy, c, s1, s2):
    return dy * c + pltpu.roll(dy * s1, 16, 1) + pltpu.roll(dy * s2, 112, 1)


def _fold_rows8(x):
    return jnp.sum(x.reshape(x.shape[0] // 8, 8, x.shape[1]), axis=0)


def _row0(v, rows=8):
    row = lax.broadcasted_iota(jnp.int32, (rows, v.shape[1]), 0)
    return jnp.where(row == 0, jnp.broadcast_to(v, (rows, v.shape[1])), 0.0)


def _mm_nn(a, b, name, out_dtype=F32, residual=None, tm=TM_PROJ):
    M, K = a.shape
    N = b.shape[1]
    tm = min(tm, M)

    def body(*refs):
        if residual is None:
            a_ref, b_ref, o_ref = refs
            acc = _dot(a_ref[...].astype(MXU_DTYPE), b_ref[...])
        else:
            a_ref, b_ref, r_ref, o_ref = refs
            acc = _dot(a_ref[...].astype(MXU_DTYPE), b_ref[...]) + r_ref[...]
        o_ref[...] = acc.astype(out_dtype)

    in_specs = [pl.BlockSpec((tm, K), lambda i: (i, 0)), pl.BlockSpec((K, N), lambda i: (0, 0))]
    args = [a, b]
    if residual is not None:
        in_specs.append(pl.BlockSpec((tm, N), lambda i: (i, 0)))
        args.append(residual)
    return pl.pallas_call(
        body, name=name, grid=(M // tm,), in_specs=in_specs,
        out_specs=pl.BlockSpec((tm, N), lambda i: (i, 0)), out_shape=_sds((M, N), out_dtype),
        compiler_params=_cp(("parallel",), 48))(*args)


def _mm_tn(a, b, name, out_dtype, tn, tk=512):
    T, M = a.shape
    N = b.shape[1]
    tk = min(tk, T)
    nk = T // tk

    def body(a_ref, b_ref, o_ref, acc_ref):
        k = pl.program_id(1)

        @pl.when(k == 0)
        def _():
            acc_ref[...] = jnp.zeros_like(acc_ref)

        acc_ref[...] += _dot_tn(a_ref[...].astype(MXU_DTYPE), b_ref[...].astype(MXU_DTYPE))

        @pl.when(k == nk - 1)
        def _():
            o_ref[...] = acc_ref[...].astype(out_dtype)

    return pl.pallas_call(
        body, name=name, grid=(N // tn, nk),
        in_specs=[pl.BlockSpec((tk, M), lambda n, k: (k, 0)), pl.BlockSpec((tk, tn), lambda n, k: (k, n))],
        out_specs=pl.BlockSpec((M, tn), lambda n, k: (0, n)), out_shape=_sds((M, N), out_dtype),
        scratch_shapes=[pltpu.VMEM((M, tn), F32)],
        compiler_params=_cp(("parallel", "arbitrary"), 48))(a, b)


def _inproj_fwd(x, ng, wp):
    T, D = x.shape
    tm = min(TM_PROJ, T)

    def body(x_ref, g_ref, w_ref, proj_ref, h_ref):
        xhat, _ = _rms(x_ref[...], D)
        h = (xhat * g_ref[...]).astype(MXU_DTYPE)
        h_ref[...] = h
        proj_ref[...] = _dot(h, w_ref[...])

    return pl.pallas_call(
        body, name="inproj_fwd", grid=(T // tm,),
        in_specs=[pl.BlockSpec((tm, D), lambda i: (i, 0)), pl.BlockSpec((1, D), lambda i: (0, 0)),
                  pl.BlockSpec((D, NP), lambda i: (0, 0))],
        out_specs=[pl.BlockSpec((tm, NP), lambda i: (i, 0)), pl.BlockSpec((tm, D), lambda i: (i, 0))],
        out_shape=[_sds((T, NP), F32), _sds((T, D), MXU_DTYPE)],
        compiler_params=_cp(("parallel",), 48))(x, ng, wp)


def _mla_prep_fwd(proj, lw, rope):
    T = proj.shape[0]
    tm = min(TK, T // 2)

    def body(ql_ref, kvl_ref, kr_ref, qa_ref, kva_ref, wq_ref, wk_ref, wv_ref, qn_ref, kn_ref,
             c_ref, s1_ref, s2_ref, q_out, k_out, kt_out, v_out, vt_out):
        c, s1, s2 = c_ref[...], s1_ref[...], s2_ref[...]
        qhat, _ = _rms(ql_ref[...], MLA_Q_LORA)
        qn = (qhat * qa_ref[...]).astype(MXU_DTYPE)
        khat, _ = _rms(kvl_ref[...], MLA_KV_LORA)
        kvn = (khat * kva_ref[...]).astype(MXU_DTYPE)
        kr = kr_ref[...]
        half1 = lax.broadcasted_iota(jnp.int32, (tm, LANES), 1) >= 64
        for h in range(HEADS):
            xh, _ = _rms(_dot(qn, wq_ref[h]), MLA_QK)
            q_out[h] = (_rope(xh * qn_ref[...], c, s1, s2) * (MLA_SCALE * LOG2E)).astype(MXU_DTYPE)
            xh, _ = _rms(_dot(kvn, wk_ref[h]) + kr, MLA_QK)
            kh = _rope(xh * kn_ref[...], c, s1, s2)
            k_out[h] = kh.astype(MXU_DTYPE)
            kt_out[h, 0] = kh.T.astype(MXU_DTYPE)
        v = _dot(kvn, wv_ref[...])
        for h in range(HEADS):
            vp = v[:, LANES * (h // 2):LANES * (h // 2 + 1)]
            own = half1 if h % 2 else jnp.logical_not(half1)
            vp = jnp.where(own, vp, 0.0)
            v_out[h] = vp.astype(MXU_DTYPE)
            vt_out[h, 0] = vp.T.astype(MXU_DTYPE)

    full = lambda shape: pl.BlockSpec(shape, lambda i: (0,) * len(shape))
    hd = pl.BlockSpec((HEADS, tm, LANES), lambda i: (0, i, 0))
    hdt = pl.BlockSpec((HEADS, 1, LANES, tm), lambda i: (0, i, 0, 0))
    nat = _sds((HEADS, T, LANES), MXU_DTYPE)
    tr = _sds((HEADS, T // tm, LANES, tm), MXU_DTYPE)
    return pl.pallas_call(
        body, name="mla_prep_fwd", grid=(T // tm,),
        in_specs=[pl.BlockSpec((tm, 256), lambda i: (i, CB_QLAT)), pl.BlockSpec((tm, LANES), lambda i: (i, CB_KVLAT)),
                  pl.BlockSpec((tm, LANES), lambda i: (i, CB_KROPE)),
                  full((1, 256)), full((1, LANES)), full((HEADS, 256, LANES)), full((HEADS, LANES, LANES)),
                  full((LANES, 512)), full((1, LANES)), full((1, LANES)),
                  pl.BlockSpec((tm, LANES), lambda i: (i, 0)), pl.BlockSpec((tm, LANES), lambda i: (i, 0)),
                  pl.BlockSpec((tm, LANES), lambda i: (i, 0))],
        out_specs=[hd, hd, hdt, hd, hdt],
        out_shape=[nat, nat, tr, nat, tr],
        compiler_params=_cp(("parallel",), 32))(
            proj, proj, proj, lw["qa"], lw["kva"], lw["wq"], lw["wk"], lw["wv"], lw["qn"], lw["kn"],
            rope[0], rope[1], rope[2])


def _mla_attn_fwd(q, k, vt):
    T = q.shape[1]
    tk = min(TK, T // 2)
    tq = 2 * tk

    def body(q_ref, k_ref, vt_ref, o_ref, lse_ref, acc_s, m_s, l_s, s_a, s_b):
        i = pl.program_id(1)
        key = lax.broadcasted_iota(jnp.int32, (tk, tq), 0)
        qry = lax.broadcasted_iota(jnp.int32, (tk, tq), 1)
        qs = [q_ref[0], q_ref[1]]
        acc_s[...] = jnp.zeros_like(acc_s)
        l_s[...] = jnp.zeros_like(l_s)
        m_s[...] = jnp.full(m_s.shape, NEG_INF, F32)

        def scores(kj, buf):
            rows = pl.ds(pl.multiple_of(kj * tk, tk), tk)
            for r in range(2):
                buf[r] = _dot_nt(k_ref[r, rows, :], qs[r])

        def consume(kj, buf, diag):
            for r in range(2):
                s = buf[r]
                if diag is not None:
                    s = jnp.where(key + diag * tk <= qry, s, NEG_INF)
                m_old = m_s[r]
                m_new = jnp.maximum(m_old, jnp.max(s, axis=0, keepdims=True))
                alpha = jnp.exp2(m_old - m_new)
                p = jnp.exp2(s - m_new)
                l_s[r] = alpha * l_s[r] + jnp.sum(p, axis=0, keepdims=True)
                m_s[r] = m_new
                acc_s[r] = alpha * acc_s[r] + _dot(vt_ref[r, kj], p.astype(MXU_DTYPE))

        scores(0, s_a)

        def pair(kj):
            scores(kj + 1, s_b)
            consume(kj, s_a, None)
            scores(kj + 2, s_a)
            consume(kj + 1, s_b, None)

        def quad(kq, carry):
            pair(4 * kq)
            pair(4 * kq + 2)
            return carry

        lax.fori_loop(0, i // 2, quad, 0)

        @pl.when(i % 2 == 1)
        def _():
            pair(2 * i - 2)

        scores(2 * i + 1, s_b)
        consume(2 * i, s_a, 0)
        consume(2 * i + 1, s_b, 1)
        o_t = acc_s[0] / l_s[0] + acc_s[1] / l_s[1]
        o_ref[...] = o_t.T
        for r in range(2):
            lse_ref[r] = m_s[r] + jnp.log2(l_s[r])

    return pl.pallas_call(
        body, name="mla_attn_fwd", grid=(HEADS // 2, T // tq),
        in_specs=[pl.BlockSpec((2, tq, LANES), lambda j, i: (j, i, 0)),
                  pl.BlockSpec((2, T, LANES), lambda j, i: (j, 0, 0)),
                  pl.BlockSpec((2, T // tk, LANES, tk), lambda j, i: (j, 0, 0, 0))],
        out_specs=[pl.BlockSpec((tq, LANES), lambda j, i: (i, j)),
                   pl.BlockSpec((2, 1, tq), lambda j, i: (j, 0, i))],
        out_shape=[_sds((T, GROUP_WIDTH), F32), _sds((HEADS, 1, T), F32)],
        scratch_shapes=[pltpu.VMEM((2, LANES, tq), F32), pltpu.VMEM((2, 1, tq), F32), pltpu.VMEM((2, 1, tq), F32),
                        pltpu.VMEM((2, tk, tq), F32), pltpu.VMEM((2, tk, tq), F32)],
        compiler_params=_cp(("parallel", "arbitrary"), 40))(q, k, vt)


def _swa_masks(nb_first):
    qi = lax.broadcasted_iota(jnp.int32, (BLOCK, 2 * BLOCK), 0)
    ki = lax.broadcasted_iota(jnp.int32, (BLOCK, 2 * BLOCK), 1)
    dist = BLOCK + qi - ki
    valid = (dist >= 0) & (dist < BLOCK) & ((ki >= BLOCK) | jnp.logical_not(nb_first))
    return dist.astype(F32), valid


def _swa_kv_variants(x, half1):
    xs = pltpu.roll(x, 64, 1)
    out = {}
    for g in range(2):
        for r in range(2):
            own = half1 if r else jnp.logical_not(half1)
            out[(g, r)] = jnp.where(own, x if g == r else xs, 0.0).astype(MXU_DTYPE)
    return out


def _swa_fwd(proj, lw):
    T = proj.shape[0]
    tm = min(TM_SWA, T)
    nb = tm // BLOCK
    scale = SWA_HEAD_DIM ** -0.5

    def body(q_ref, k_ref, v_ref, pk_ref, pv_ref, qw_ref, kw_ref, sink_ref, o_ref):
        i = pl.program_id(0)
        half1 = lax.broadcasted_iota(jnp.int32, (1, LANES), 1) >= 64
        k_all = jnp.concatenate([pk_ref[...], k_ref[...]], axis=0)
        v_all = jnp.concatenate([pv_ref[...], v_ref[...]], axis=0)
        khat, _ = _rms_halves(k_all, half1)
        kp = _swa_kv_variants(khat * kw_ref[...], half1)
        vp = _swa_kv_variants(v_all, half1)
        qn = []
        for j in range(4):
            qhat, _ = _rms_halves(q_ref[:, LANES * j:LANES * (j + 1)], half1)
            qn.append((qhat * qw_ref[...]).astype(MXU_DTYPE))
        for b in range(nb):
            dist, valid = _swa_masks((i == 0) & (b == 0))
            ks = slice(b * BLOCK, b * BLOCK + 2 * BLOCK)
            for j in range(4):
                g = j // 2
                qb = qn[j][b * BLOCK:(b + 1) * BLOCK]
                o = jnp.zeros((BLOCK, LANES), F32)
                for r in range(2):
                    h = 2 * j + r
                    s = _dot_nt(qb, kp[(g, r)][ks]) * scale - (2.0 ** -(h + 1)) * dist
                    s = jnp.where(valid, s, NEG_INF)
                    sink = sink_ref[h]
                    m = jnp.maximum(jnp.max(s, axis=-1, keepdims=True), sink)
                    e = jnp.exp(s - m)
                    den = jnp.sum(e, axis=-1, keepdims=True) + jnp.exp(sink - m)
                    o = o + _dot((e / den).astype(MXU_DTYPE), vp[(g, r)][ks])
                o_ref[b * BLOCK:(b + 1) * BLOCK, LANES * j:LANES * (j + 1)] = o

    prev = lambda cb: pl.BlockSpec((BLOCK, LANES), lambda i: (jnp.maximum(i * nb - 1, 0), cb))
    return pl.pallas_call(
        body, name="swa_fwd", grid=(T // tm,),
        in_specs=[pl.BlockSpec((tm, 512), lambda i: (i, CB_SQ)), pl.BlockSpec((tm, LANES), lambda i: (i, CB_SK)),
                  pl.BlockSpec((tm, LANES), lambda i: (i, CB_SV)), prev(CB_SK), prev(CB_SV),
                  pl.BlockSpec((1, LANES), lambda i: (0, 0)), pl.BlockSpec((1, LANES), lambda i: (0, 0)),
                  pl.BlockSpec(memory_space=pltpu.SMEM)],
        out_specs=pl.BlockSpec((tm, 512), lambda i: (i, 0)),
        out_shape=_sds((T, GROUP_WIDTH), F32),
        compiler_params=_cp(("parallel",), 32))(proj, proj, proj, proj, proj, lw["sqn"], lw["skn"], lw["sinks"])


def _shift_down(u, prev, n, row):
    tm = u.shape[0]
    out = pltpu.roll(u, n, 0)
    row8 = lax.broadcasted_iota(jnp.int32, prev.shape, 0)
    for t in range(n):
        src = jnp.sum(jnp.where(row8 == 8 - n + t, prev, 0.0), axis=0, keepdims=True)
        out = jnp.where(row == t, src, out)
    return out


def _shift_up(u, nxt, n, row):
    tm = u.shape[0]
    out = pltpu.roll(u, tm - n, 0)
    row8 = lax.broadcasted_iota(jnp.int32, nxt.shape, 0)
    for t in range(n):
        src = jnp.sum(jnp.where(row8 == t, nxt, 0.0), axis=0, keepdims=True)
        out = jnp.where(row == tm - n + t, src, out)
    return out


def _mix_fwd(proj, o_mla, o_swa, conv_w):
    T = proj.shape[0]
    tm = min(TM_ROW, T)

    def body(gm_ref, ch_ref, cb_ref, cc_ref, gc_ref, gs_ref, pch_ref, pcc_ref, om_ref, os_ref, w_ref, y_ref):
        i = pl.program_id(0)
        row = lax.broadcasted_iota(jnp.int32, (tm, GROUP_WIDTH), 0)
        u = cc_ref[...] * ch_ref[...]
        u_prev = jnp.where(i > 0, pcc_ref[...] * pch_ref[...], 0.0)
        z = (w_ref[0:1, :] * _shift_down(u, u_prev, 2, row) + w_ref[1:2, :] * _shift_down(u, u_prev, 1, row)
             + w_ref[2:3, :] * u)
        gm, gc, gs = gm_ref[...], gc_ref[...], gs_ref[...]
        y_ref[:, 0:512] = (om_ref[...] * (gm * _sigmoid(gm))).astype(MXU_DTYPE)
        y_ref[:, 512:1024] = (cb_ref[...] * z * (gc * _sigmoid(gc))).astype(MXU_DTYPE)
        y_ref[:, 1024:1536] = (os_ref[...] * (gs * _sigmoid(gs))).astype(MXU_DTYPE)

    blk = lambda cb: pl.BlockSpec((tm, 512), lambda i: (i, cb))
    prev = lambda cb: pl.BlockSpec((8, 512), lambda i: (jnp.maximum(i * (tm // 8) - 1, 0), cb))
    tile = pl.BlockSpec((tm, 512), lambda i: (i, 0))
    return pl.pallas_call(
        body, name="mix_fwd", grid=(T // tm,),
        in_specs=[blk(CB_GMLA), blk(CB_CH), blk(CB_CB), blk(CB_CC), blk(CB_GCONV), blk(CB_GSWA),
                  prev(CB_CH), prev(CB_CC), tile, tile, pl.BlockSpec((8, 512), lambda i: (0, 0))],
        out_specs=pl.BlockSpec((tm, D_MIX), lambda i: (i, 0)),
        out_shape=_sds((T, D_MIX), MXU_DTYPE),
        compiler_params=_cp(("parallel",), 32))(
            proj, proj, proj, proj, proj, proj, proj, proj, o_mla, o_swa, conv_w)


def _loss_grad(y, target):
    T, D = y.shape
    tm = min(TM_ROW, T)
    nt = T // tm

    def body(y_ref, t_ref, g_ref, loss_ref, acc_ref):
        i = pl.program_id(0)

        @pl.when(i == 0)
        def _():
            acc_ref[...] = jnp.zeros_like(acc_ref)

        err = y_ref[...] - t_ref[...]
        g_ref[...] = err * (1.0 / D)
        acc_ref[...] += _fold_rows8(err * err)

        @pl.when(i == nt - 1)
        def _():
            tot = jnp.sum(jnp.sum(acc_ref[...], axis=1, keepdims=True), axis=0, keepdims=True)
            loss_ref[...] = jnp.broadcast_to(tot * (0.5 / D), (8, LANES))

    return pl.pallas_call(
        body, name="loss_grad", grid=(nt,),
        in_specs=[pl.BlockSpec((tm, D), lambda i: (i, 0)), pl.BlockSpec((tm, D), lambda i: (i, 0))],
        out_specs=[pl.BlockSpec((tm, D), lambda i: (i, 0)), pl.BlockSpec((8, LANES), lambda i: (0, 0))],
        out_shape=[_sds((T, D), F32), _sds((8, LANES), F32)],
        scratch_shapes=[pltpu.VMEM((8, D), F32)],
        compiler_params=_cp(("arbitrary",), 32))(y, target)


def _mix_bwd(dycat, proj, o_mla, o_swa, conv_w):
    T = proj.shape[0]
    tm = min(TM_ROW, T)
    nt = T // tm

    def body(dym_ref, dyc_ref, dys_ref, gm_ref, ch_ref, cb_ref, cc_ref, gc_ref, gs_ref, pch_ref, pcc_ref,
             ndy_ref, ncb_ref, ngc_ref, om_ref, os_ref, w_ref,
             d1_ref, dgs_ref, dom_ref, dos_ref, dw_ref):
        i = pl.program_id(0)

        @pl.when(i == 0)
        def _():
            dw_ref[...] = jnp.zeros_like(dw_ref)

        row = lax.broadcasted_iota(jnp.int32, (tm, GROUP_WIDTH), 0)

        def gate(g):
            sg = _sigmoid(g)
            return g * sg, sg * (1.0 + g * (1.0 - sg))

        gm = gm_ref[...]
        silu, dsilu = gate(gm)
        dym = dym_ref[...]
        dom_ref[...] = dym * silu
        d1_ref[:, 0:512] = (dym * om_ref[...] * dsilu).astype(MXU_DTYPE)

        gs = gs_ref[...]
        silu, dsilu = gate(gs)
        dys = dys_ref[...]
        dos_ref[...] = dys * silu
        dgs_ref[...] = (dys * os_ref[...] * dsilu).astype(MXU_DTYPE)

        ch, cb, cc, gc, dyc = ch_ref[...], cb_ref[...], cc_ref[...], gc_ref[...], dyc_ref[...]
        w0, w1, w2 = w_ref[0:1, :], w_ref[1:2, :], w_ref[2:3, :]
        u = cc * ch
        u_prev = jnp.where(i > 0, pcc_ref[...] * pch_ref[...], 0.0)
        u1 = _shift_down(u, u_prev, 1, row)
        u2 = _shift_down(u, u_prev, 2, row)
        z = w0 * u2 + w1 * u1 + w2 * u
        silu, dsilu = gate(gc)
        dz = dyc * cb * silu
        ngc = ngc_ref[...]
        dz_next = jnp.where(i < nt - 1, ndy_ref[...] * ncb_ref[...] * (ngc * _sigmoid(ngc)), 0.0)
        du = w2 * dz + w1 * _shift_up(dz, dz_next, 1, row) + w0 * _shift_up(dz, dz_next, 2, row)
        d1_ref[:, 512:1024] = (du * cc).astype(MXU_DTYPE)
        d1_ref[:, 1024:1536] = (dyc * z * silu).astype(MXU_DTYPE)
        d1_ref[:, 1536:2048] = (du * ch).astype(MXU_DTYPE)
        d1_ref[:, 2048:2560] = (dyc * cb * z * dsilu).astype(MXU_DTYPE)
        row8 = lax.broadcasted_iota(jnp.int32, (8, GROUP_WIDTH), 0)
        dw = jnp.zeros((8, GROUP_WIDTH), F32)
        for t, shifted in enumerate((u2, u1, u)):
            dw = dw + jnp.where(row8 == t, jnp.sum(dz * shifted, axis=0, keepdims=True), 0.0)
        dw_ref[...] += dw

    blk = lambda cb: pl.BlockSpec((tm, 512), lambda i: (i, cb))
    prev = lambda cb: pl.BlockSpec((8, 512), lambda i: (jnp.maximum(i * (tm // 8) - 1, 0), cb))
    nxt = lambda cb: pl.BlockSpec((8, 512), lambda i: (jnp.minimum((i + 1) * (tm // 8), T // 8 - 1), cb))
    tile = pl.BlockSpec((tm, 512), lambda i: (i, 0))
    return pl.pallas_call(
        body, name="mix_bwd", grid=(nt,),
        in_specs=[blk(0), blk(1), blk(2), blk(CB_GMLA), blk(CB_CH), blk(CB_CB), blk(CB_CC), blk(CB_GCONV),
                  blk(CB_GSWA), prev(CB_CH), prev(CB_CC), nxt(1), nxt(CB_CB), nxt(CB_GCONV), tile, tile,
                  pl.BlockSpec((8, 512), lambda i: (0, 0))],
        out_specs=[pl.BlockSpec((tm, 2560), lambda i: (i, 0)), tile, tile, tile,
                   pl.BlockSpec((8, 512), lambda i: (0, 0))],
        out_shape=[_sds((T, 2560), MXU_DTYPE), _sds((T, 512), MXU_DTYPE), _sds((T, 512), F32),
                   _sds((T, 512), F32), _sds((8, 512), F32)],
        compiler_params=_cp(("arbitrary",), 48))(
            dycat, dycat, dycat, proj, proj, proj, proj, proj, proj, proj, proj, dycat, proj, proj,
            o_mla, o_swa, conv_w)


def _swa_bwd(proj, o_swa, do_swa, lw):
    T = proj.shape[0]
    tm = min(TM_SWA, T)
    nb = tm // BLOCK
    scale = SWA_HEAD_DIM ** -0.5

    def body(q_ref, k_ref, v_ref, pk_ref, pv_ref, o_ref, do_ref, qw_ref, kw_ref, sink_ref,
             dq_ref, dk_ref, dv_ref, dqw_ref, dsink_ref):
        i = pl.program_id(0)

        @pl.when(i == 0)
        def _():
            dk_ref[...] = jnp.zeros_like(dk_ref)
            dv_ref[...] = jnp.zeros_like(dv_ref)
            dqw_ref[...] = jnp.zeros_like(dqw_ref)
            dsink_ref[...] = jnp.zeros_like(dsink_ref)

        half1 = lax.broadcasted_iota(jnp.int32, (1, LANES), 1) >= 64
        k_all = jnp.concatenate([pk_ref[...], k_ref[...]], axis=0)
        v_all = jnp.concatenate([pv_ref[...], v_ref[...]], axis=0)
        khat, _ = _rms_halves(k_all, half1)
        kp = _swa_kv_variants(khat * kw_ref[...], half1)
        vp = _swa_kv_variants(v_all, half1)
        qw = qw_ref[...]
        dqw = jnp.zeros((1, LANES), F32)
        dsink_rows = [jnp.zeros((1, 1), F32) for _ in range(HEADS)]
        for j in range(4):
            g = j // 2
            cols = slice(LANES * j, LANES * (j + 1))
            qhat, qr = _rms_halves(q_ref[:, cols], half1)
            qn = (qhat * qw).astype(MXU_DTYPE)
            do = do_ref[:, cols]
            dob = do.astype(MXU_DTYPE)
            prod = do * o_ref[:, cols]
            dqn_blocks = []
            for b in range(nb):
                dist, valid = _swa_masks((i == 0) & (b == 0))
                ks = slice(b * BLOCK, b * BLOCK + 2 * BLOCK)
                rs = slice(b * BLOCK, (b + 1) * BLOCK)
                qb = qn[rs]
                dqn = jnp.zeros((BLOCK, LANES), F32)
                for r in range(2):
                    h = 2 * j + r
                    own = half1 if r else jnp.logical_not(half1)
                    s = _dot_nt(qb, kp[(g, r)][ks]) * scale - (2.0 ** -(h + 1)) * dist
                    s = jnp.where(valid, s, NEG_INF)
                    sink = sink_ref[h]
                    m = jnp.maximum(jnp.max(s, axis=-1, keepdims=True), sink)
                    e = jnp.exp(s - m)
                    es = jnp.exp(sink - m)
                    inv = 1.0 / (jnp.sum(e, axis=-1, keepdims=True) + es)
                    p = e * inv
                    dd = jnp.sum(jnp.where(own, prod[rs], 0.0), axis=-1, keepdims=True)
                    dp = _dot_nt(dob[rs], vp[(g, r)][ks])
                    ds = (p * (dp - dd) * scale).astype(MXU_DTYPE)
                    dsink_rows[h] = dsink_rows[h] - jnp.sum(es * inv * dd, axis=0, keepdims=True)
                    dqn = dqn + _dot(ds, kp[(g, r)][ks])
                    dkp = jnp.where(own, _dot_tn(ds, qb), 0.0)
                    dvp = jnp.where(own, _dot_tn(p.astype(MXU_DTYPE), dob[rs]), 0.0)
                    if g != r:
                        dkp = pltpu.roll(dkp, 64, 1)
                        dvp = pltpu.roll(dvp, 64, 1)
                    dst = pl.ds(pl.multiple_of((i * nb + b) * BLOCK, BLOCK), 2 * BLOCK)
                    dk_ref[dst, :] += dkp
                    dv_ref[dst, :] += dvp
                dqn_blocks.append(dqn)
            dqn = jnp.concatenate(dqn_blocks, axis=0) if nb > 1 else dqn_blocks[0]
            dqw = dqw + jnp.sum(dqn * qhat, axis=0, keepdims=True)
            dq_ref[:, cols] = _rms_halves_bwd(dqn, qhat, qr, qw, half1).astype(MXU_DTYPE)
        dqw_ref[...] += _row0(dqw + pltpu.roll(dqw, 64, 1))
        row8 = lax.broadcasted_iota(jnp.int32, (8, LANES), 0)
        dsink = jnp.zeros((8, LANES), F32)
        for h in range(HEADS):
            dsink = dsink + jnp.where(row8 == h, jnp.broadcast_to(dsink_rows[h], (8, LANES)), 0.0)
        dsink_ref[...] += dsink

    prev = lambda cb: pl.BlockSpec((BLOCK, LANES), lambda i: (jnp.maximum(i * nb - 1, 0), cb))
    tile = pl.BlockSpec((tm, 512), lambda i: (i, 0))
    small = pl.BlockSpec((8, LANES), lambda i: (0, 0))
    acc = pl.BlockSpec((T + BLOCK, LANES), lambda i: (0, 0))
    return pl.pallas_call(
        body, name="swa_bwd", grid=(T // tm,),
        in_specs=[pl.BlockSpec((tm, 512), lambda i: (i, CB_SQ)), pl.BlockSpec((tm, LANES), lambda i: (i, CB_SK)),
                  pl.BlockSpec((tm, LANES), lambda i: (i, CB_SV)), prev(CB_SK), prev(CB_SV), tile, tile,
                  pl.BlockSpec((1, LANES), lambda i: (0, 0)), pl.BlockSpec((1, LANES), lambda i: (0, 0)),
                  pl.BlockSpec(memory_space=pltpu.SMEM)],
        out_specs=[tile, acc, acc, small, small],
        out_shape=[_sds((T, 512), MXU_DTYPE), _sds((T + BLOCK, LANES), F32), _sds((T + BLOCK, LANES), F32),
                   _sds((8, LANES), F32), _sds((8, LANES), F32)],
        compiler_params=_cp(("arbitrary",), 40))(
            proj, proj, proj, proj, proj, o_swa, do_swa, lw["sqn"], lw["skn"], lw["sinks"])


def _swa_kv_bwd(proj, dkn, dv, lw):
    T = proj.shape[0]
    tm = BLOCK

    def body(k_ref, dkn_ref, dv_ref, kw_ref, d_ref, dkw_ref):
        i = pl.program_id(0)

        @pl.when(i == 0)
        def _():
            dkw_ref[...] = jnp.zeros_like(dkw_ref)

        half1 = lax.broadcasted_iota(jnp.int32, (1, LANES), 1) >= 64
        khat, kr = _rms_halves(k_ref[...], half1)
        dkn_t = dkn_ref[...]
        dkw = jnp.sum(dkn_t * khat, axis=0, keepdims=True)
        dkw_ref[...] += _row0(dkw + pltpu.roll(dkw, 64, 1))
        d_ref[:, 0:LANES] = _rms_halves_bwd(dkn_t, khat, kr, kw_ref[...], half1).astype(MXU_DTYPE)
        d_ref[:, LANES:2 * LANES] = dv_ref[...].astype(MXU_DTYPE)

    return pl.pallas_call(
        body, name="swa_kv_bwd", grid=(T // tm,),
        in_specs=[pl.BlockSpec((tm, LANES), lambda i: (i, CB_SK)), pl.BlockSpec((tm, LANES), lambda i: (i + 1, 0)),
                  pl.BlockSpec((tm, LANES), lambda i: (i + 1, 0)), pl.BlockSpec((1, LANES), lambda i: (0, 0))],
        out_specs=[pl.BlockSpec((tm, 2 * LANES), lambda i: (i, 0)), pl.BlockSpec((8, LANES), lambda i: (0, 0))],
        out_shape=[_sds((T, 2 * LANES), MXU_DTYPE), _sds((8, LANES), F32)],
        compiler_params=_cp(("arbitrary",), 32))(proj, dkn, dv, lw["skn"])


def _mla_attn_bwd(q, k, kt, vpad, o, do, lse):
    T = q.shape[1]
    tk = min(TK, T // 2)
    tq = 2 * tk

    def body(q_ref, k_ref, kt_ref, v_ref, o_ref, do_ref, lse_ref, dq_ref, dk_ref, dv_ref, dqt_s,
             s_a, s_b, p_a, p_b):
        h = pl.program_id(0)
        i = pl.program_id(1)

        @pl.when(i == 0)
        def _():
            dk_ref[...] = jnp.zeros_like(dk_ref)
            dv_ref[...] = jnp.zeros_like(dv_ref)

        key = lax.broadcasted_iota(jnp.int32, (tk, tq), 0)
        qry = lax.broadcasted_iota(jnp.int32, (tk, tq), 1)
        own = (lax.broadcasted_iota(jnp.int32, (1, LANES), 1) // 64) == (h % 2)
        own_rows = (lax.broadcasted_iota(jnp.int32, (LANES, 1), 0) // 64) == (h % 2)
        do_t = do_ref[...]
        dob = do_t.astype(MXU_DTYPE)
        prod_t = (do_t * o_ref[...]).T
        dd = jnp.sum(jnp.where(own_rows, prod_t, 0.0), axis=0, keepdims=True)
        qh = q_ref[0]
        lse_t = lse_ref[0]
        dqt_s[...] = jnp.zeros_like(dqt_s)

        def scores(kj, s_buf, p_buf):
            rows = pl.ds(pl.multiple_of(kj * tk, tk), tk)
            s_buf[...] = _dot_nt(k_ref[0, rows, :], qh)
            p_buf[...] = _dot_nt(v_ref[0, rows, :], dob)

        def consume(kj, s_buf, p_buf, diag):
            rows = pl.ds(pl.multiple_of(kj * tk, tk), tk)
            s = s_buf[...]
            if diag is not None:
                s = jnp.where(key + diag * tk <= qry, s, NEG_INF)
            p = jnp.exp2(s - lse_t)
            ds = (p * (p_buf[...] - dd)).astype(MXU_DTYPE)
            dqt_s[...] += _dot(kt_ref[0, kj], ds)
            dk_ref[0, rows, :] += _dot(ds, qh)
            dv_ref[0, rows, :] += jnp.where(own, _dot(p.astype(MXU_DTYPE), dob), 0.0)

        scores(0, s_a, p_a)

        def pair(kj):
            scores(kj + 1, s_b, p_b)
            consume(kj, s_a, p_a, None)
            scores(kj + 2, s_a, p_a)
            consume(kj + 1, s_b, p_b, None)

        def quad(kq, carry):
            pair(4 * kq)
            pair(4 * kq + 2)
            return carry

        lax.fori_loop(0, i // 2, quad, 0)

        @pl.when(i % 2 == 1)
        def _():
            pair(2 * i - 2)

        scores(2 * i + 1, s_b, p_b)
        consume(2 * i, s_a, p_a, 0)
        consume(2 * i + 1, s_b, p_b, 1)
        dq_ref[0] = dqt_s[...].T

    res = pl.BlockSpec((1, T, LANES), lambda h, i: (h, 0, 0))
    res_t = pl.BlockSpec((1, T // tk, LANES, tk), lambda h, i: (h, 0, 0, 0))
    buf = pltpu.VMEM((tk, tq), F32)
    return pl.pallas_call(
        body, name="mla_attn_bwd", grid=(HEADS, T // tq),
        in_specs=[pl.BlockSpec((1, tq, LANES), lambda h, i: (h, i, 0)), res, res_t, res,
                  pl.BlockSpec((tq, LANES), lambda h, i: (i, h // 2)),
                  pl.BlockSpec((tq, LANES), lambda h, i: (i, h // 2)),
                  pl.BlockSpec((1, 1, tq), lambda h, i: (h, 0, i))],
        out_specs=[pl.BlockSpec((1, tq, LANES), lambda h, i: (h, i, 0)), res, res],
        out_shape=[_sds((HEADS, T, LANES), F32)] * 3,
        scratch_shapes=[pltpu.VMEM((LANES, tq), F32), buf, buf, buf, buf],
        compiler_params=_cp(("parallel", "arbitrary"), 48))(q, k, kt, vpad, o, do, lse)


def _mla_prep_bwd(proj, dq, dk, dv, lw, rope):
    T = proj.shape[0]
    tm = min(TK, T // 2)

    def body(ql_ref, kvl_ref, kr_ref, dq_ref, dk_ref, dv_ref, qa_ref, kva_ref, wq_ref, wk_ref, wv_ref,
             wqt_ref, wkt_ref, wvt_ref, qn_ref, kn_ref, c_ref, s1_ref, s2_ref,
             d_ref, dwq_ref, dwk_ref, dwv_ref, dqa_ref, dkva_ref, dqn_ref, dkn_ref):
        i = pl.program_id(0)

        @pl.when(i == 0)
        def _():
            for ref in (dwq_ref, dwk_ref, dwv_ref, dqa_ref, dkva_ref, dqn_ref, dkn_ref):
                ref[...] = jnp.zeros_like(ref)

        c, s1, s2 = c_ref[...], s1_ref[...], s2_ref[...]
        lane = lax.broadcasted_iota(jnp.int32, (1, LANES), 1)
        qlhat, qlr = _rms(ql_ref[...], MLA_Q_LORA)
        qn = (qlhat * qa_ref[...]).astype(MXU_DTYPE)
        kvhat, kvr = _rms(kvl_ref[...], MLA_KV_LORA)
        kvn = (kvhat * kva_ref[...]).astype(MXU_DTYPE)
        kr = kr_ref[...]
        dqnl = jnp.zeros((tm, MLA_Q_LORA), F32)
        dkvn = jnp.zeros((tm, MLA_KV_LORA), F32)
        dkr = jnp.zeros((tm, LANES), F32)
        dqw = jnp.zeros((1, LANES), F32)
        dkw = jnp.zeros((1, LANES), F32)
        for h in range(HEADS):
            xh, r = _rms(_dot(qn, wq_ref[h]), MLA_QK)
            dy = _rope_bwd(dq_ref[h] * MLA_SCALE, c, s1, s2)
            dqw = dqw + jnp.sum(dy * xh, axis=0, keepdims=True)
            dx = _rms_bwd(dy, xh, r, qn_ref[...], MLA_QK).astype(MXU_DTYPE)
            dwq_ref[h] += _dot_tn(qn, dx)
            dqnl = dqnl + _dot(dx, wqt_ref[h])

            xh, r = _rms(_dot(kvn, wk_ref[h]) + kr, MLA_QK)
            dy = _rope_bwd(dk_ref[h] * LN2, c, s1, s2)
            dkw = dkw + jnp.sum(dy * xh, axis=0, keepdims=True)
            dxf = _rms_bwd(dy, xh, r, kn_ref[...], MLA_QK)
            dkr = dkr + dxf
            dx = dxf.astype(MXU_DTYPE)
            dwk_ref[h] += _dot_tn(kvn, dx)
            dkvn = dkvn + _dot(dx, wkt_ref[h])
        dvc = jnp.concatenate([dv_ref[2 * j] + dv_ref[2 * j + 1] for j in range(4)], axis=1).astype(MXU_DTYPE)
        dwv_ref[...] += _dot_tn(kvn, dvc)
        dkvn = dkvn + _dot(dvc, wvt_ref[...])
        dqa_ref[...] += _row0(jnp.sum(dqnl * qlhat, axis=0, keepdims=True))
        dkva_ref[...] += _row0(jnp.sum(dkvn * kvhat, axis=0, keepdims=True))
        dqn_ref[...] += _row0(dqw)
        dkn_ref[...] += _row0(dkw)
        d_ref[:, 0:256] = _rms_bwd(dqnl, qlhat, qlr, qa_ref[...], MLA_Q_LORA).astype(MXU_DTYPE)
        d_ref[:, 256:384] = _rms_bwd(dkvn, kvhat, kvr, kva_ref[...], MLA_KV_LORA).astype(MXU_DTYPE)
        d_ref[:, 384:512] = jnp.where((lane >= 64) & (lane < 96), dkr, 0.0).astype(MXU_DTYPE)

    full = lambda shape: pl.BlockSpec(shape, lambda i: (0,) * len(shape))
    hd = pl.BlockSpec((HEADS, tm, LANES), lambda i: (0, i, 0))
    tab = pl.BlockSpec((tm, LANES), lambda i: (i, 0))
    return pl.pallas_call(
        body, name="mla_prep_bwd", grid=(T // tm,),
        in_specs=[pl.BlockSpec((tm, 256), lambda i: (i, CB_QLAT)), pl.BlockSpec((tm, LANES), lambda i: (i, CB_KVLAT)),
                  pl.BlockSpec((tm, LANES), lambda i: (i, CB_KROPE)), hd, hd, hd,
                  full((1, 256)), full((1, LANES)), full((HEADS, 256, LANES)), full((HEADS, LANES, LANES)),
                  full((LANES, 512)), full((HEADS, LANES, 256)), full((HEADS, LANES, LANES)), full((512, LANES)),
                  full((1, LANES)), full((1, LANES)), tab, tab, tab],
        out_specs=[pl.BlockSpec((tm, 512), lambda i: (i, 0)), full((HEADS, 256, LANES)),
                   full((HEADS, LANES, LANES)), full((LANES, 512)), full((8, 256)), full((8, LANES)),
                   full((8, LANES)), full((8, LANES))],
        out_shape=[_sds((T, 512), MXU_DTYPE), _sds((HEADS, 256, LANES), F32), _sds((HEADS, LANES, LANES), F32),
                   _sds((LANES, 512), F32), _sds((8, 256), F32), _sds((8, LANES), F32), _sds((8, LANES), F32),
                   _sds((8, LANES), F32)],
        compiler_params=_cp(("arbitrary",), 48))(
            proj, proj, proj, dq, dk, dv, lw["qa"], lw["kva"], lw["wq"], lw["wk"], lw["wv"],
            lw["wqt"], lw["wkt"], lw["wvt"], lw["qn"], lw["kn"], rope[0], rope[1], rope[2])


def _norm_bwd(dh, x, g_in, ng):
    T, D = x.shape
    tm = min(TM_ROW, T)

    def body(dh_ref, x_ref, g_ref, w_ref, dx_ref, dw_ref):
        i = pl.program_id(0)

        @pl.when(i == 0)
        def _():
            dw_ref[...] = jnp.zeros_like(dw_ref)

        xhat, r = _rms(x_ref[...], D)
        dh_t = dh_ref[...]
        dw_ref[...] += _row0(jnp.sum(dh_t * xhat, axis=0, keepdims=True))
        dx_ref[...] = g_ref[...] + _rms_bwd(dh_t, xhat, r, w_ref[...], D)

    tile = pl.BlockSpec((tm, D), lambda i: (i, 0))
    return pl.pallas_call(
        body, name="norm_bwd", grid=(T // tm,),
        in_specs=[tile, tile, tile, pl.BlockSpec((1, D), lambda i: (0, 0))],
        out_specs=[tile, pl.BlockSpec((8, D), lambda i: (0, 0))],
        out_shape=[_sds((T, D), F32), _sds((8, D), F32)],
        compiler_params=_cp(("arbitrary",), 32))(dh, x, g_in, ng)


def _rope_tables(T):
    half = MLA_ROPE // 2
    inv_freq = jnp.power(jnp.float32(ROPE_THETA), -jnp.arange(half, dtype=F32) / half)
    ang = jnp.arange(T, dtype=F32)[:, None] * inv_freq[None, :]
    cos, sin = jnp.cos(ang), jnp.sin(ang)
    z = lambda n: jnp.zeros((T, n), F32)
    c = jnp.concatenate([jnp.ones((T, MLA_NOPE), F32), cos, cos, z(32)], axis=1)
    s1 = jnp.concatenate([z(64), -sin, z(48)], axis=1)
    s2 = jnp.concatenate([z(80), sin, z(32)], axis=1)
    return c, s1, s2


def _pad_lanes(v, n=LANES):
    v = v.reshape(1, -1)
    return jnp.pad(v, ((0, 0), (0, n - v.shape[1])))


def _pack_win(w):
    z = lambda n: jnp.zeros((w.shape[0], n), w.dtype)
    return jnp.concatenate([w[:, 0:384], z(64), w[:, 384:416], z(32), w[:, 416:2976], w[:, 2976:3488],
                            w[:, 3744:4256], w[:, 3488:3616], w[:, 3616:3744]], axis=1)


def _unpack_dwin(d):
    return jnp.concatenate([d[:, 0:384], d[:, 448:480], d[:, 512:3072], d[:, 3072:3584], d[:, 4096:4224],
                            d[:, 4224:4352], d[:, 3584:4096]], axis=1)


def _inproj_weights(l, norm_g, w_in_full):
    wp = _pack_win(w_in_full)
    return dict(ng=norm_g[l].reshape(1, -1), wp=wp, wpt=wp.T)


def _mixer_weights(l, qa, wqb_full, kva, wkvb_full, qn, kn, conv_full, sqn, skn, sinks, w_out_full):
    wq = jnp.pad(wqb_full, ((0, 0), (0, 0), (0, LANES - MLA_QK)))
    wk = jnp.pad(wkvb_full[:, :, :MLA_NOPE], ((0, 0), (0, 0), (0, LANES - MLA_NOPE)))
    wv = jnp.transpose(wkvb_full[:, :, MLA_NOPE:], (1, 0, 2)).reshape(MLA_KV_LORA, GROUP_WIDTH)
    return dict(
        qa=qa[l].reshape(1, -1), kva=kva[l].reshape(1, -1),
        wq=wq, wk=wk, wv=wv, wqt=jnp.transpose(wq, (0, 2, 1)), wkt=jnp.transpose(wk, (0, 2, 1)), wvt=wv.T,
        qn=_pad_lanes(qn[l]), kn=_pad_lanes(kn[l]),
        conv=jnp.pad(conv_full, ((0, 5), (0, 0))),
        sqn=jnp.tile(sqn[l].reshape(1, -1), (1, 2)), skn=jnp.tile(skn[l].reshape(1, -1), (1, 2)),
        sinks=sinks[l], wo=w_out_full, wot=w_out_full.T)


def _layer_weights(l, norm_g, w_in_full, qa, wqb_full, kva, wkvb_full, qn, kn, conv_full, sqn, skn, sinks,
                   w_out_full):
    return dict(_inproj_weights(l, norm_g, w_in_full),
                **_mixer_weights(l, qa, wqb_full, kva, wkvb_full, qn, kn, conv_full, sqn, skn, sinks, w_out_full))


def _layer_fwd(x, lw, rope, late_weights=None):
    proj, h = _inproj_fwd(x, lw["ng"], lw["wp"])
    if late_weights is not None:
        lw = dict(lw, **late_weights(proj))
    q, k, kt, vpad, vt = _mla_prep_fwd(proj, lw, rope)
    o_mla, lse = _mla_attn_fwd(q, k, vt)
    o_swa = _swa_fwd(proj, lw)
    ycat = _mix_fwd(proj, o_mla, o_swa, lw["conv"])
    x_next = _mm_nn(ycat, lw["wo"], "outproj_fwd", residual=x)
    return x_next, dict(x=x, proj=proj, h=h, q=q, k=k, kt=kt, vpad=vpad, o_mla=o_mla, lse=lse, o_swa=o_swa, ycat=ycat,
                        lw=lw)


def _layer_bwd(g, sv, lw, rope, on_big_grads=None):
    proj = sv["proj"]
    dycat = _mm_nn(g, lw["wot"], "outproj_bwd_dy")
    d_wo = _mm_tn(sv["ycat"], g, "outproj_bwd_dw", WIRE_DTYPE, tn=D_MODEL)
    d1, dgs, do_mla, do_swa, d_conv = _mix_bwd(dycat, proj, sv["o_mla"], sv["o_swa"], lw["conv"])
    dsq, dkn_acc, dv_acc, d_sqn, d_sinks = _swa_bwd(proj, sv["o_swa"], do_swa, lw)
    dskv, d_skn = _swa_kv_bwd(proj, dkn_acc, dv_acc, lw)
    dq, dk, dv = _mla_attn_bwd(sv["q"], sv["k"], sv["kt"], sv["vpad"], sv["o_mla"], do_mla, sv["lse"])
    dmla, d_wq, d_wk, d_wv, d_qa, d_kva, d_qn, d_kn = _mla_prep_bwd(proj, dq, dk, dv, lw, rope)
    dproj = jnp.concatenate([dmla, d1, dsq, dgs, dskv], axis=1)
    d_wp = _mm_tn(sv["h"], dproj, "inproj_bwd_dw", WIRE_DTYPE, tn=NP // 2)
    grads = dict(
        w_in=_unpack_dwin(d_wp), w_out=d_wo,
        w_qb=d_wq[:, :, :MLA_QK],
        w_kvb=jnp.concatenate([d_wk[:, :, :MLA_NOPE],
                               jnp.transpose(d_wv.reshape(MLA_KV_LORA, HEADS, MLA_NOPE), (1, 0, 2))], axis=2))
    ng = lw["ng"] if on_big_grads is None else lw["ng"] + on_big_grads(grads)
    dh = _mm_nn(dproj, lw["wpt"], "inproj_bwd_dh")
    dx, d_ng = _norm_bwd(dh, sv["x"], g, ng)
    grads.update(
        conv=d_conv[0:3], norm_g=d_ng[0], qa=d_qa[0], kva=d_kva[0], qn=d_qn[0, :MLA_QK], kn=d_kn[0, :MLA_QK],
        sqn=d_sqn[0, :SWA_HEAD_DIM], skn=d_skn[0, :SWA_HEAD_DIM], sinks=d_sinks[:, 0])
    return dx, grads


def _local_step(x, target, lws, rope):
    saved = []
    for lw in lws:
        x, sv = _layer_fwd(x, lw, rope)
        saved.append(sv)
    g, loss_tile = _loss_grad(x, target)
    grads = [None] * len(lws)
    for l in reversed(range(len(lws))):
        g, grads[l] = _layer_bwd(g, saved[l], lws[l], rope)
    return loss_tile, g, grads


def _my_coords():
    return lax.axis_index("x"), lax.axis_index("y"), lax.axis_index("c")


def _peer(me, k):
    x, y, c = me
    return (1 - x if k & 4 else x, 1 - y if k & 2 else y, 1 - c if k & 1 else c)


def _lin(d):
    return 4 * d[0] + 2 * d[1] + d[2]


def _all_gather(shards):
    n = len(shards)

    def body(*refs):
        ins, outs = refs[:n], refs[n:2 * n]
        send_sems, recv_sems, local_sems = refs[2 * n:]
        me = _my_coords()
        my = _lin(me)
        local = [pltpu.make_async_copy(ins[a], outs[a].at[my], local_sems.at[a]) for a in range(n)]
        for cp in local:
            cp.start()
        sends = []
        for a in range(n):
            for k in range(1, N_DEV):
                cp = pltpu.make_async_remote_copy(
                    src_ref=ins[a], dst_ref=outs[a].at[my], send_sem=send_sems.at[a * 7 + k - 1],
                    recv_sem=recv_sems.at[a * 7 + k - 1], device_id=_peer(me, k),
                    device_id_type=pl.DeviceIdType.MESH)
                cp.start()
                sends.append(cp)
        for a in range(n):
            for k in range(1, N_DEV):
                src = _lin(_peer(me, k))
                pltpu.make_async_remote_copy(
                    src_ref=ins[a], dst_ref=outs[a].at[src], send_sem=send_sems.at[a * 7 + k - 1],
                    recv_sem=recv_sems.at[a * 7 + k - 1], device_id=_peer(me, k),
                    device_id_type=pl.DeviceIdType.MESH).wait_recv()
        for cp in sends:
            cp.wait_send()
        for cp in local:
            cp.wait()

    any_spec = pl.BlockSpec(memory_space=pl.ANY)
    return pl.pallas_call(
        body, name="weight_all_gather",
        in_specs=[any_spec] * n, out_specs=[any_spec] * n,
        out_shape=[_sds((N_DEV,) + s.shape, s.dtype) for s in shards],
        scratch_shapes=[pltpu.SemaphoreType.DMA((7 * n,)), pltpu.SemaphoreType.DMA((7 * n,)),
                        pltpu.SemaphoreType.DMA((n,))],
    )(*shards)


def _push_copies(ins, lands, send_sems, recv_sems, gather):
    me = _my_coords()
    my = _lin(me)
    out, inc = [], []
    for a in range(len(ins)):
        for k in range(1, N_DEV):
            peer = _peer(me, k)
            sems = dict(send_sem=send_sems.at[a * 7 + k - 1], recv_sem=recv_sems.at[a * 7 + k - 1],
                        device_id=peer, device_id_type=pl.DeviceIdType.MESH)
            src = ins[a] if gather else ins[a].at[_lin(peer)]
            out.append(pltpu.make_async_remote_copy(src_ref=src, dst_ref=lands[a].at[my], **sems))
            inc.append(pltpu.make_async_remote_copy(src_ref=src, dst_ref=lands[a].at[_lin(peer)], **sems))
    return out, inc


def _push_start(arrays, name, gather):
    n = len(arrays)
    land_shapes = [((N_DEV,) + a.shape) if gather else a.shape for a in arrays]

    def body(*refs):
        ins, lands = refs[:n], refs[n:2 * n]
        send_sems, recv_sems = refs[2 * n], refs[2 * n + 1]
        token = refs[-1]
        out, _ = _push_copies(ins, lands, send_sems, recv_sems, gather)
        for cp in out:
            cp.start()
        token[...] = jnp.zeros_like(token)

    hbm = pl.BlockSpec(memory_space=pltpu.HBM)
    sem = pl.BlockSpec(memory_space=pltpu.SEMAPHORE)
    res = pl.pallas_call(
        body, name=name,
        out_shape=(pltpu.SemaphoreType.DMA((7 * n,)), pltpu.SemaphoreType.DMA((7 * n,)),
                   *[pltpu.HBM(a.shape, a.dtype) for a in arrays],
                   *[pltpu.HBM(s, a.dtype) for s, a in zip(land_shapes, arrays)],
                   _sds((8, LANES), F32)),
        in_specs=(hbm,) * (2 * n),
        out_specs=(sem, sem) + (hbm,) * (2 * n) + (pl.BlockSpec(memory_space=pltpu.VMEM),),
        input_output_aliases={i: 2 + i for i in range(2 * n)},
        compiler_params=pltpu.CompilerParams(has_side_effects=pltpu.SideEffectType.DATAFLOW_SIDE_EFFECTING),
    )(*[pltpu.with_memory_space_constraint(a, pltpu.HBM) for a in arrays],
      *[pltpu.with_memory_space_constraint(lax.empty(s, a.dtype), pltpu.HBM) for s, a in zip(land_shapes, arrays)])
    return dict(send=res[0], recv=res[1], src=res[2:2 + n], land=res[2 + n:2 + 2 * n], token=res[-1][0, 0],
                gather=gather)


def _push_wait(handle, after, name):
    n = len(handle["src"])
    gather = handle["gather"]

    def body(*refs):
        ins, lands = refs[:n], refs[n:2 * n]
        send_sems, recv_sems = refs[2 * n], refs[2 * n + 1]
        out, inc = _push_copies(ins, lands, send_sems, recv_sems, gather)
        for cp in out:
            cp.wait_send()
        for cp in inc:
            cp.wait_recv()

    hbm = pl.BlockSpec(memory_space=pltpu.HBM)
    sem = pl.BlockSpec(memory_space=pltpu.SEMAPHORE)
    res = pl.pallas_call(
        body, name=name,
        out_shape=tuple(pltpu.HBM(a.shape, a.dtype) for a in (*handle["src"], *handle["land"])),
        in_specs=(hbm,) * (2 * n) + (sem, sem, pl.BlockSpec(memory_space=pl.ANY)),
        out_specs=(hbm,) * (2 * n),
        input_output_aliases={i: i for i in range(2 * n)},
        compiler_params=pltpu.CompilerParams(has_side_effects=pltpu.SideEffectType.DATAFLOW_SIDE_EFFECTING),
    )(*handle["src"], *handle["land"], handle["send"], handle["recv"], after)
    return res[n:]


def _small_all_reduce(v):
    R = v.shape[0]

    def body(v_ref, o_ref, buf, send_sems, recv_sems):
        me = _my_coords()
        my = _lin(me)
        sends = []
        for k in range(1, N_DEV):
            cp = pltpu.make_async_remote_copy(
                src_ref=v_ref, dst_ref=buf.at[my], send_sem=send_sems.at[k - 1], recv_sem=recv_sems.at[k - 1],
                device_id=_peer(me, k), device_id_type=pl.DeviceIdType.MESH)
            cp.start()
            sends.append(cp)
        buf[my] = v_ref[...]
        for k in range(1, N_DEV):
            pltpu.make_async_remote_copy(
                src_ref=v_ref, dst_ref=buf.at[_lin(_peer(me, k))], send_sem=send_sems.at[k - 1],
                recv_sem=recv_sems.at[k - 1], device_id=_peer(me, k),
                device_id_type=pl.DeviceIdType.MESH).wait_recv()
        for cp in sends:
            cp.wait_send()
        tot = buf[0]
        for d in range(1, N_DEV):
            tot = tot + buf[d]
        o_ref[...] = tot

    vm = pl.BlockSpec(memory_space=pltpu.VMEM)
    return pl.pallas_call(
        body, name="small_all_reduce", in_specs=[vm], out_specs=vm, out_shape=_sds(v.shape, F32),
        scratch_shapes=[pltpu.VMEM((N_DEV, R, LANES), F32), pltpu.SemaphoreType.DMA((7,)),
                        pltpu.SemaphoreType.DMA((7,))],
    )(v)


def _adamw_math(w, g, m, v):
    m = ADAM_B1 * m + (1.0 - ADAM_B1) * g
    v = ADAM_B2 * v + (1.0 - ADAM_B2) * (g * g)
    m_hat = m / (1.0 - ADAM_B1 ** ADAM_STEP)
    v_hat = v / (1.0 - ADAM_B2 ** ADAM_STEP)
    delta = -ADAM_LR * (m_hat / (jnp.sqrt(v_hat) + ADAM_EPS) + ADAM_WD * w)
    return delta, m, v


def _adamw(parts, w, m, v, name, tr):
    P, R, C = parts.shape
    tr = min(tr, R)

    def body(p_ref, w_ref, m_ref, v_ref, g_out, d_out, m_out, v_out):
        g = p_ref[0].astype(F32)
        for d in range(1, P):
            g = g + p_ref[d].astype(F32)
        delta, m_new, v_new = _adamw_math(w_ref[...], g, m_ref[...], v_ref[...])
        g_out[...] = g
        d_out[...] = delta
        m_out[...] = m_new
        v_out[...] = v_new

    tile = pl.BlockSpec((tr, C), lambda i: (i, 0))
    return pl.pallas_call(
        body, name=name, grid=(R // tr,),
        in_specs=[pl.BlockSpec((P, tr, C), lambda i: (0, i, 0)), tile, tile, tile],
        out_specs=[tile] * 4, out_shape=[_sds((R, C), F32)] * 4,
        compiler_params=_cp(("parallel",), 32))(parts, w, m, v)


SMALL = (("norm_g", D_MODEL), ("mla_q_a_norm", MLA_Q_LORA), ("mla_kv_a_norm", MLA_KV_LORA), ("mla_q_norm", MLA_QK),
         ("mla_k_norm", MLA_QK), ("swa_q_norm", SWA_HEAD_DIM), ("swa_k_norm", SWA_HEAD_DIM), ("swa_sinks", HEADS))
SMALL_GRAD_KEY = dict(norm_g="norm_g", mla_q_a_norm="qa", mla_kv_a_norm="kva", mla_q_norm="qn", mla_k_norm="kn",
                      swa_q_norm="sqn", swa_k_norm="skn", swa_sinks="sinks")
SMALL_ROWS = 32
CONV_ROWS = 24


def _pack_small(get):
    parts = []
    for l in range(DEPTH):
        for name, n in SMALL:
            v = get(name, l).reshape(-1)
            parts.append(jnp.pad(v, (0, (-n) % LANES)))
    return jnp.concatenate(parts).reshape(SMALL_ROWS, LANES)


def _unpack_small(packed):
    flat = packed.reshape(-1)
    out = {name: [] for name, _ in SMALL}
    off = 0
    for l in range(DEPTH):
        for name, n in SMALL:
            out[name].append(flat[off:off + n])
            off += n + (-n) % LANES
    return {name: jnp.stack(v) for name, v in out.items()}


def kernel(x, norm_g, w_in, mla_q_a_norm, mla_w_qb, mla_kv_a_norm, mla_w_kvb, mla_q_norm, mla_k_norm, conv_w, swa_q_norm, swa_k_norm, swa_sinks, w_out, loss_target, m_norm_g, m_w_in, m_mla_q_a_norm, m_mla_w_qb, m_mla_kv_a_norm, m_mla_w_kvb, m_mla_q_norm, m_mla_k_norm, m_conv_w, m_swa_q_norm, m_swa_k_norm, m_swa_sinks, m_w_out, v_norm_g, v_w_in, v_mla_q_a_norm, v_mla_w_qb, v_mla_kv_a_norm, v_mla_w_kvb, v_mla_q_norm, v_mla_k_norm, v_conv_w, v_swa_q_norm, v_swa_k_norm, v_swa_sinks, v_w_out):
    T = x.shape[1]
    weights = dict(norm_g=norm_g, w_in=w_in, mla_q_a_norm=mla_q_a_norm, mla_w_qb=mla_w_qb,
                   mla_kv_a_norm=mla_kv_a_norm, mla_w_kvb=mla_w_kvb, mla_q_norm=mla_q_norm, mla_k_norm=mla_k_norm,
                   conv_w=conv_w, swa_q_norm=swa_q_norm, swa_k_norm=swa_k_norm, swa_sinks=swa_sinks, w_out=w_out)
    mom_m = dict(norm_g=m_norm_g, w_in=m_w_in, mla_q_a_norm=m_mla_q_a_norm, mla_w_qb=m_mla_w_qb,
                 mla_kv_a_norm=m_mla_kv_a_norm, mla_w_kvb=m_mla_w_kvb, mla_q_norm=m_mla_q_norm,
                 mla_k_norm=m_mla_k_norm, conv_w=m_conv_w, swa_q_norm=m_swa_q_norm, swa_k_norm=m_swa_k_norm,
                 swa_sinks=m_swa_sinks, w_out=m_w_out)
    mom_v = dict(norm_g=v_norm_g, w_in=v_w_in, mla_q_a_norm=v_mla_q_a_norm, mla_w_qb=v_mla_w_qb,
                 mla_kv_a_norm=v_mla_kv_a_norm, mla_w_kvb=v_mla_w_kvb, mla_q_norm=v_mla_q_norm,
                 mla_k_norm=v_mla_k_norm, conv_w=v_conv_w, swa_q_norm=v_swa_q_norm, swa_k_norm=v_swa_k_norm,
                 swa_sinks=v_swa_sinks, w_out=v_w_out)

    my = _lin(_my_coords())
    rope = _rope_tables(T)

    def shards(l):
        return [w_in[l].astype(MXU_DTYPE), mla_w_qb[l].astype(MXU_DTYPE), mla_w_kvb[l].astype(MXU_DTYPE),
                w_out[l].astype(MXU_DTYPE), conv_w[l]]

    def inproj_weights(l, g_win):
        return _inproj_weights(l, norm_g, jnp.transpose(g_win, (1, 0, 2)).reshape(D_MODEL, IN_COLS))

    def mixer_weights(l, gathered):
        g_wqb, g_wkvb, g_wout, g_conv = gathered
        return _mixer_weights(
            l, mla_q_a_norm, g_wqb, mla_kv_a_norm, g_wkvb, mla_q_norm, mla_k_norm,
            jnp.transpose(g_conv, (1, 0, 2)).reshape(3, GROUP_WIDTH), swa_q_norm, swa_k_norm, swa_sinks,
            g_wout.reshape(D_MIX, D_MODEL))

    def slots(g):
        return [jnp.transpose(g["w_in"].reshape(D_MODEL, N_DEV, IN_COLS // N_DEV), (1, 0, 2)),
                g["w_out"].reshape(N_DEV, D_MIX // N_DEV, D_MODEL), g["w_qb"], g["w_kvb"]]

    def own_slot(landed, mine):
        return [lax.dynamic_update_index_in_dim(a, m, my, 0) for a, m in zip(landed, mine)]

    def landed(handle, after, name, mine):
        return own_slot(_push_wait(handle, after, name), mine)

    lw0 = inproj_weights(0, _all_gather(shards(0)[:1])[0])
    gather0 = _push_start(shards(0)[1:], "weight_gather0_start", gather=True)
    gather1 = _push_start(shards(1), "weight_gather1_start", gather=True)
    x1, sv0 = _layer_fwd(
        x[0], dict(lw0, ng=lw0["ng"] + (gather0["token"] + gather1["token"])), rope,
        late_weights=lambda proj: mixer_weights(0, landed(gather0, proj, "weight_gather0_wait", shards(0)[1:])))
    g1_all = landed(gather1, x1, "weight_gather1_wait", shards(1))
    x2, sv1 = _layer_fwd(x1, dict(inproj_weights(1, g1_all[0]), **mixer_weights(1, g1_all[1:])), rope)
    g2, loss_tile = _loss_grad(x2, loss_target[0])

    started = {}

    def start_exchange(l, big):
        sl = slots(big)
        started[l] = (sl, _push_start(sl, "grad_exchange%d_start" % l, gather=False))
        return started[l][1]["token"]

    def received(l, after):
        sl, handle = started[l]
        return landed(handle, after, "grad_exchange%d_wait" % l, [s[my] for s in sl])

    g1, grads1 = _layer_bwd(g2, sv1, sv1["lw"], rope, on_big_grads=lambda big: start_exchange(1, big))
    lw0b = dict(sv0["lw"], conv=sv0["lw"]["conv"] + started[1][1]["token"])
    grad_x, grads0 = _layer_bwd(g1, sv0, lw0b, rope, on_big_grads=lambda big: start_exchange(0, big))
    recv1 = received(1, grad_x)
    recv0 = received(0, grad_x)
    grads = [grads0, grads1]
    r_win, r_wout, r_wqb, r_wkvb = [jnp.stack([a, b], axis=1) for a, b in zip(recv0, recv1)]

    small = jnp.concatenate([
        _pack_small(lambda name, l: grads[l][SMALL_GRAD_KEY[name]]),
        jnp.stack([g["conv"] for g in grads]).reshape(CONV_ROWS, LANES),
        loss_tile], axis=0)
    small = _small_all_reduce(small)
    loss = small[SMALL_ROWS + CONV_ROWS, 0]
    my = _lin(_my_coords())
    conv_g = lax.dynamic_slice_in_dim(small[SMALL_ROWS:SMALL_ROWS + CONV_ROWS].reshape(DEPTH, 3, GROUP_WIDTH),
                                      my * 64, 64, axis=2)

    out = {}

    def big(name, recv, rows, cols, tr):
        res = _adamw(recv.reshape(N_DEV, rows, cols), weights[name].reshape(rows, cols),
                     mom_m[name].reshape(rows, cols), mom_v[name].reshape(rows, cols), "adamw_" + name, tr)
        out[name] = [r.reshape(weights[name].shape) for r in res]

    big("w_in", r_win, DEPTH * D_MODEL, IN_COLS // N_DEV, 256)
    big("w_out", r_wout, DEPTH * D_MIX // N_DEV, D_MODEL, 192)
    big("mla_w_qb", r_wqb, DEPTH * MLA_Q_LORA, MLA_QK, 512)
    big("mla_w_kvb", r_wkvb, DEPTH * MLA_KV_LORA, 128, 256)

    pad_conv = lambda a: jnp.pad(a.reshape(-1), (0, 8 * LANES - 6 * 64)).reshape(8, LANES)
    cat = lambda src: jnp.concatenate([_pack_small(lambda name, l: src[name][l]), pad_conv(src["conv_w"])], axis=0)
    g_small = jnp.concatenate([small[:SMALL_ROWS], pad_conv(conv_g)], axis=0)
    res = _adamw(g_small[None], cat(weights), cat(mom_m), cat(mom_v), "adamw_small", SMALL_ROWS + 8)
    smalls = [_unpack_small(r[:SMALL_ROWS]) for r in res]
    for name, _ in SMALL:
        out[name] = [s[name] for s in smalls]
    out["conv_w"] = [r[SMALL_ROWS:].reshape(-1)[:6 * 64].reshape(DEPTH, 3, 64) for r in res]

    order = ["norm_g", "w_in", "mla_q_a_norm", "mla_w_qb", "mla_kv_a_norm", "mla_w_kvb", "mla_q_norm", "mla_k_norm",
             "conv_w", "swa_q_norm", "swa_k_norm", "swa_sinks", "w_out"]
    result = [loss, grad_x[None]]
    for idx in range(4):
        result += [out[name][idx] for name in order]
    return tuple(result)
```

```python
import functools

import jax
import jax.numpy as jnp
import numpy as np
from jax import lax
from jax.experimental import pallas as pl
from jax.experimental.pallas import tpu as pltpu

F32 = jnp.float32
MXU_DTYPE = jnp.bfloat16
WIRE_DTYPE = jnp.bfloat16

N_DEV = 8
DEPTH = 2
D_MODEL = 1024
GROUP_WIDTH = 512
D_MIX = 3 * GROUP_WIDTH
BLOCK = 128
RMS_EPS = 1e-6
NEG_INF = -1e30
HEADS = 8
MLA_QK = 96
MLA_NOPE = 64
MLA_ROPE = 32
MLA_Q_LORA = 256
MLA_KV_LORA = 128
ROPE_THETA = 10000.0
SWA_HEAD_DIM = 64
LANES = 128
IN_COLS = 4256

ADAM_LR = 0.001
ADAM_B1 = 0.9
ADAM_B2 = 0.999
ADAM_EPS = 1e-08
ADAM_WD = 0.01
ADAM_STEP = 10

NP = 4352
CB_QLAT = 0
CB_KVLAT = 2
CB_KROPE = 3
CB_GMLA, CB_CH, CB_CB, CB_CC, CB_GCONV, CB_SQ, CB_GSWA = 1, 2, 3, 4, 5, 6, 7
CB_SK, CB_SV = 32, 33

TM_PROJ = 256
TM_ROW = 256
TK = 256
TQ = 2 * TK
MLA_SCALE = MLA_QK ** -0.5
LOG2E = 1.4426950408889634
LN2 = 0.6931471805599453
TM_SWA = 256
VMEM_MB = 2 ** 20


def _cp(sem, vmem_mb):
    return pltpu.CompilerParams(dimension_semantics=sem, vmem_limit_bytes=vmem_mb * VMEM_MB)


def _sds(shape, dtype):
    return jax.ShapeDtypeStruct(shape, dtype)


def _dot(a, b):
    return jnp.dot(a, b, preferred_element_type=F32)


def _dot_nt(a, b):
    return lax.dot_general(a, b, (((1,), (1,)), ((), ())), preferred_element_type=F32)


def _dot_tn(a, b):
    return lax.dot_general(a, b, (((0,), (0,)), ((), ())), preferred_element_type=F32)


def _rms(x, n):
    r = lax.rsqrt(jnp.sum(x * x, axis=-1, keepdims=True) * (1.0 / n) + RMS_EPS)
    return x * r, r


def _rms_bwd(dy, xhat, r, w, n):
    g = dy * w
    return r * (g - xhat * (jnp.sum(g * xhat, axis=-1, keepdims=True) * (1.0 / n)))


def _rms_halves(x, half1):
    x2 = x * x
    s0 = jnp.sum(jnp.where(half1, 0.0, x2), axis=-1, keepdims=True)
    s1 = jnp.sum(jnp.where(half1, x2, 0.0), axis=-1, keepdims=True)
    r = jnp.where(half1, lax.rsqrt(s1 * (1.0 / 64) + RMS_EPS), lax.rsqrt(s0 * (1.0 / 64) + RMS_EPS))
    return x * r, r


def _rms_halves_bwd(dy, xhat, r, w, half1):
    g = dy * w
    t = g * xhat
    m0 = jnp.sum(jnp.where(half1, 0.0, t), axis=-1, keepdims=True) * (1.0 / 64)
    m1 = jnp.sum(jnp.where(half1, t, 0.0), axis=-1, keepdims=True) * (1.0 / 64)
    return r * (g - xhat * jnp.where(half1, m1, m0))


def _sigmoid(x):
    return 1.0 / (1.0 + jnp.exp(-x))


def _rope(x, c, s1, s2):
    ax = x.ndim - 1
    return x * c + pltpu.roll(x, 112, ax) * s1 + pltpu.roll(x, 16, ax) * s2


def _rope_bwd(dy, c, s1, s2):
    ax = dy.ndim - 1
    return dy * c + pltpu.roll(dy * s1, 16, ax) + pltpu.roll(dy * s2, 112, ax)


def _fold_rows8(x):
    return jnp.sum(x.reshape(x.shape[0] // 8, 8, x.shape[1]), axis=0)


def _row0(v, rows=8):
    row = lax.broadcasted_iota(jnp.int32, (rows, v.shape[1]), 0)
    return jnp.where(row == 0, jnp.broadcast_to(v, (rows, v.shape[1])), 0.0)


def _mm_nn(a, b, name, out_dtype=F32, residual=None, tm=TM_PROJ):
    M, K = a.shape
    N = b.shape[1]
    tm = min(tm, M)

    def body(*refs):
        if residual is None:
            a_ref, b_ref, o_ref = refs
            acc = _dot(a_ref[...].astype(MXU_DTYPE), b_ref[...])
        else:
            a_ref, b_ref, r_ref, o_ref = refs
            acc = _dot(a_ref[...].astype(MXU_DTYPE), b_ref[...]) + r_ref[...]
        o_ref[...] = acc.astype(out_dtype)

    in_specs = [pl.BlockSpec((tm, K), lambda i: (i, 0)), pl.BlockSpec((K, N), lambda i: (0, 0))]
    args = [a, b]
    if residual is not None:
        in_specs.append(pl.BlockSpec((tm, N), lambda i: (i, 0)))
        args.append(residual)
    return pl.pallas_call(
        body, name=name, grid=(M // tm,), in_specs=in_specs,
        out_specs=pl.BlockSpec((tm, N), lambda i: (i, 0)), out_shape=_sds((M, N), out_dtype),
        compiler_params=_cp(("parallel",), 48))(*args)


def _mm_tn(a, b, name, out_dtype, tn, tk=512):
    T, M = a.shape
    N = b.shape[1]
    tk = min(tk, T)
    nk = T // tk

    def body(a_ref, b_ref, o_ref, acc_ref):
        k = pl.program_id(1)

        @pl.when(k == 0)
        def _():
            acc_ref[...] = jnp.zeros_like(acc_ref)

        acc_ref[...] += _dot_tn(a_ref[...].astype(MXU_DTYPE), b_ref[...].astype(MXU_DTYPE))

        @pl.when(k == nk - 1)
        def _():
            o_ref[...] = acc_ref[...].astype(out_dtype)

    return pl.pallas_call(
        body, name=name, grid=(N // tn, nk),
        in_specs=[pl.BlockSpec((tk, M), lambda n, k: (k, 0)), pl.BlockSpec((tk, tn), lambda n, k: (k, n))],
        out_specs=pl.BlockSpec((M, tn), lambda n, k: (0, n)), out_shape=_sds((M, N), out_dtype),
        scratch_shapes=[pltpu.VMEM((M, tn), F32)],
        compiler_params=_cp(("parallel", "arbitrary"), 48))(a, b)


def _inproj_fwd(x, ng, wp):
    T, D = x.shape
    tm = min(TM_PROJ, T)

    def body(x_ref, g_ref, w_ref, proj_ref, h_ref):
        xhat, _ = _rms(x_ref[...], D)
        h = (xhat * g_ref[...]).astype(MXU_DTYPE)
        h_ref[...] = h
        proj_ref[...] = _dot(h, w_ref[...])

    return pl.pallas_call(
        body, name="inproj_fwd", grid=(T // tm,),
        in_specs=[pl.BlockSpec((tm, D), lambda i: (i, 0)), pl.BlockSpec((1, D), lambda i: (0, 0)),
                  pl.BlockSpec((D, NP), lambda i: (0, 0))],
        out_specs=[pl.BlockSpec((tm, NP), lambda i: (i, 0)), pl.BlockSpec((tm, D), lambda i: (i, 0))],
        out_shape=[_sds((T, NP), F32), _sds((T, D), MXU_DTYPE)],
        compiler_params=_cp(("parallel",), 48))(x, ng, wp)


def _mla_prep_fwd(proj, lw, rope):
    T = proj.shape[0]
    tm = min(TK, T // 2)

    def body(ql_ref, kvl_ref, kr_ref, qa_ref, kva_ref, wq_ref, wk_ref, wv_ref, qn_ref, kn_ref,
             c_ref, s1_ref, s2_ref, q_out, k_out, kt_out, v_out, vt_out):
        c, s1, s2 = c_ref[...], s1_ref[...], s2_ref[...]
        qhat, _ = _rms(ql_ref[...], MLA_Q_LORA)
        qn = (qhat * qa_ref[...]).astype(MXU_DTYPE)
        khat, _ = _rms(kvl_ref[...], MLA_KV_LORA)
        kvn = (khat * kva_ref[...]).astype(MXU_DTYPE)
        kr = kr_ref[...]
        half1 = lax.broadcasted_iota(jnp.int32, (tm, LANES), 1) >= 64
        q3, _ = _rms(jnp.stack([_dot(qn, wq_ref[h]) for h in range(HEADS)]), MLA_QK)
        q_out[...] = (_rope(q3 * qn_ref[...], c, s1, s2) * (MLA_SCALE * LOG2E)).astype(MXU_DTYPE)
        k3, _ = _rms(jnp.stack([_dot(kvn, wk_ref[h]) for h in range(HEADS)]) + kr, MLA_QK)
        k3 = _rope(k3 * kn_ref[...], c, s1, s2)
        k_out[...] = k3.astype(MXU_DTYPE)
        for h in range(HEADS):
            kt_out[h, 0] = k3[h].T.astype(MXU_DTYPE)
        v = _dot(kvn, wv_ref[...])
        for h in range(HEADS):
            vp = v[:, LANES * (h // 2):LANES * (h // 2 + 1)]
            own = half1 if h % 2 else jnp.logical_not(half1)
            vp = jnp.where(own, vp, 0.0)
            v_out[h] = vp.astype(MXU_DTYPE)
            vt_out[h, 0] = vp.T.astype(MXU_DTYPE)

    full = lambda shape: pl.BlockSpec(shape, lambda i: (0,) * len(shape))
    hd = pl.BlockSpec((HEADS, tm, LANES), lambda i: (0, i, 0))
    hdt = pl.BlockSpec((HEADS, 1, LANES, tm), lambda i: (0, i, 0, 0))
    nat = _sds((HEADS, T, LANES), MXU_DTYPE)
    tr = _sds((HEADS, T // tm, LANES, tm), MXU_DTYPE)
    return pl.pallas_call(
        body, name="mla_prep_fwd", grid=(T // tm,),
        in_specs=[pl.BlockSpec((tm, 256), lambda i: (i, CB_QLAT)), pl.BlockSpec((tm, LANES), lambda i: (i, CB_KVLAT)),
                  pl.BlockSpec((tm, LANES), lambda i: (i, CB_KROPE)),
                  full((1, 256)), full((1, LANES)), full((HEADS, 256, LANES)), full((HEADS, LANES, LANES)),
                  full((LANES, 512)), full((1, LANES)), full((1, LANES)),
                  pl.BlockSpec((tm, LANES), lambda i: (i, 0)), pl.BlockSpec((tm, LANES), lambda i: (i, 0)),
                  pl.BlockSpec((tm, LANES), lambda i: (i, 0))],
        out_specs=[hd, hd, hdt, hd, hdt],
        out_shape=[nat, nat, tr, nat, tr],
        compiler_params=_cp(("parallel",), 32))(
            proj, proj, proj, lw["qa"], lw["kva"], lw["wq"], lw["wk"], lw["wv"], lw["qn"], lw["kn"],
            rope[0], rope[1], rope[2])


def _mla_attn_fwd(q, k, vt):
    T = q.shape[1]
    tk = min(TK, T // 2)
    tq = 2 * tk

    def body(q_ref, k_ref, vt_ref, o_ref, lse_ref, acc_s, m_s, l_s, s_a, s_b):
        i = pl.program_id(1)
        key = lax.broadcasted_iota(jnp.int32, (tk, tq), 0)
        qry = lax.broadcasted_iota(jnp.int32, (tk, tq), 1)
        qs = [q_ref[0], q_ref[1]]
        acc_s[...] = jnp.zeros_like(acc_s)
        l_s[...] = jnp.zeros_like(l_s)
        m_s[...] = jnp.full(m_s.shape, NEG_INF, F32)

        def scores(kj, buf):
            rows = pl.ds(pl.multiple_of(kj * tk, tk), tk)
            for r in range(2):
                buf[r] = _dot_nt(k_ref[r, rows, :], qs[r])

        def consume(kj, buf, diag):
            for r in range(2):
                s = buf[r]
                if diag is not None:
                    s = jnp.where(key + diag * tk <= qry, s, NEG_INF)
                m_old = m_s[r]
                m_new = jnp.maximum(m_old, jnp.max(s, axis=0, keepdims=True))
                alpha = jnp.exp2(m_old - m_new)
                p = jnp.exp2(s - m_new)
                l_s[r] = alpha * l_s[r] + jnp.sum(p, axis=0, keepdims=True)
                m_s[r] = m_new
                acc_s[r] = alpha * acc_s[r] + _dot(vt_ref[r, kj], p.astype(MXU_DTYPE))

        scores(0, s_a)

        def pair(kj):
            scores(kj + 1, s_b)
            consume(kj, s_a, None)
            scores(kj + 2, s_a)
            consume(kj + 1, s_b, None)

        def quad(kq, carry):
            pair(4 * kq)
            pair(4 * kq + 2)
            return carry

        lax.fori_loop(0, i // 2, quad, 0)

        @pl.when(i % 2 == 1)
        def _():
            pair(2 * i - 2)

        scores(2 * i + 1, s_b)
        consume(2 * i, s_a, 0)
        consume(2 * i + 1, s_b, 1)
        o_t = acc_s[0] / l_s[0] + acc_s[1] / l_s[1]
        o_ref[...] = o_t.T
        for r in range(2):
            lse_ref[r] = m_s[r] + jnp.log2(l_s[r])

    return pl.pallas_call(
        body, name="mla_attn_fwd", grid=(HEADS // 2, T // tq),
        in_specs=[pl.BlockSpec((2, tq, LANES), lambda j, i: (j, i, 0)),
                  pl.BlockSpec((2, T, LANES), lambda j, i: (j, 0, 0)),
                  pl.BlockSpec((2, T // tk, LANES, tk), lambda j, i: (j, 0, 0, 0))],
        out_specs=[pl.BlockSpec((tq, LANES), lambda j, i: (i, j)),
                   pl.BlockSpec((2, 1, tq), lambda j, i: (j, 0, i))],
        out_shape=[_sds((T, GROUP_WIDTH), F32), _sds((HEADS, 1, T), F32)],
        scratch_shapes=[pltpu.VMEM((2, LANES, tq), F32), pltpu.VMEM((2, 1, tq), F32), pltpu.VMEM((2, 1, tq), F32),
                        pltpu.VMEM((2, tk, tq), F32), pltpu.VMEM((2, tk, tq), F32)],
        compiler_params=_cp(("parallel", "arbitrary"), 40))(q, k, vt)


def _swa_kv_variants(x, half1):
    xs = pltpu.roll(x, 64, 1)
    out = {}
    for g in range(2):
        for r in range(2):
            own = half1 if r else jnp.logical_not(half1)
            out[(g, r)] = jnp.where(own, x if g == r else xs, 0.0).astype(MXU_DTYPE)
    return out


def _swa_alibi():
    qi = np.arange(BLOCK)[:, None]
    ki = np.arange(2 * BLOCK)[None, :]
    dist = BLOCK + qi - ki
    slopes = 2.0 ** -(np.arange(HEADS) + 1.0)
    tab = np.where(((dist >= 0) & (dist < BLOCK))[None], slopes[:, None, None] * dist[None], 1e30)
    return jnp.asarray(tab, F32)


def _swa_probs(i, q_ref, k_ref, v_ref, pk_ref, pv_ref, qw_ref, kw_ref, alibi_ref, sink_ref):
    scale = SWA_HEAD_DIM ** -0.5
    half1 = lax.broadcasted_iota(jnp.int32, (1, LANES), 1) >= 64
    k_all = jnp.concatenate([pk_ref[...], k_ref[...]], axis=0)
    v_all = jnp.concatenate([pv_ref[...], v_ref[...]], axis=0)
    khat, _ = _rms_halves(k_all, half1)
    kp = _swa_kv_variants(khat * kw_ref[...], half1)
    vp = _swa_kv_variants(v_all, half1)
    qhat, qr, qn = [], [], []
    for j in range(4):
        xh, r = _rms_halves(q_ref[:, LANES * j:LANES * (j + 1)], half1)
        qhat.append(xh)
        qr.append(r)
        qn.append((xh * qw_ref[...]).astype(MXU_DTYPE))
    ki = lax.broadcasted_iota(jnp.int32, (1, 2 * BLOCK), 1)
    first = jnp.where((i == 0) & (ki < BLOCK), NEG_INF, 0.0)
    s = jnp.stack([_dot_nt(qn[h // 2], kp[(h // 4, h % 2)]) for h in range(HEADS)]) * scale
    s = s - alibi_ref[...] + first
    sink = jnp.stack([jnp.full((1, 1), sink_ref[h], F32) for h in range(HEADS)])
    m = jnp.maximum(jnp.max(s, axis=-1, keepdims=True), sink)
    e = jnp.exp(s - m)
    es = jnp.exp(sink - m)
    inv = 1.0 / (jnp.sum(e, axis=-1, keepdims=True) + es)
    return e * inv, es * inv, dict(half1=half1, kp=kp, vp=vp, qhat=qhat, qr=qr, qn=qn)


def _swa_fwd(proj, lw):
    T = proj.shape[0]

    def body(q_ref, k_ref, v_ref, pk_ref, pv_ref, qw_ref, kw_ref, alibi_ref, sink_ref, o_ref):
        p, _, c = _swa_probs(pl.program_id(0), q_ref, k_ref, v_ref, pk_ref, pv_ref, qw_ref, kw_ref, alibi_ref,
                             sink_ref)
        p = p.astype(MXU_DTYPE)
        for j in range(4):
            o_ref[:, LANES * j:LANES * (j + 1)] = (_dot(p[2 * j], c["vp"][(j // 2, 0)])
                                                   + _dot(p[2 * j + 1], c["vp"][(j // 2, 1)]))

    prev = lambda cb: pl.BlockSpec((BLOCK, LANES), lambda i: (jnp.maximum(i - 1, 0), cb))
    return pl.pallas_call(
        body, name="swa_fwd", grid=(T // BLOCK,),
        in_specs=[pl.BlockSpec((BLOCK, 512), lambda i: (i, CB_SQ)), pl.BlockSpec((BLOCK, LANES), lambda i: (i, CB_SK)),
                  pl.BlockSpec((BLOCK, LANES), lambda i: (i, CB_SV)), prev(CB_SK), prev(CB_SV),
                  pl.BlockSpec((1, LANES), lambda i: (0, 0)), pl.BlockSpec((1, LANES), lambda i: (0, 0)),
                  pl.BlockSpec((HEADS, BLOCK, 2 * BLOCK), lambda i: (0, 0, 0)),
                  pl.BlockSpec(memory_space=pltpu.SMEM)],
        out_specs=pl.BlockSpec((BLOCK, 512), lambda i: (i, 0)),
        out_shape=_sds((T, GROUP_WIDTH), F32),
        compiler_params=_cp(("parallel",), 32))(
            proj, proj, proj, proj, proj, lw["sqn"], lw["skn"], _swa_alibi(), lw["sinks"])


def _shift_down(u, prev, n, row):
    tm = u.shape[0]
    out = pltpu.roll(u, n, 0)
    row8 = lax.broadcasted_iota(jnp.int32, prev.shape, 0)
    for t in range(n):
        src = jnp.sum(jnp.where(row8 == 8 - n + t, prev, 0.0), axis=0, keepdims=True)
        out = jnp.where(row == t, src, out)
    return out


def _shift_up(u, nxt, n, row):
    tm = u.shape[0]
    out = pltpu.roll(u, tm - n, 0)
    row8 = lax.broadcasted_iota(jnp.int32, nxt.shape, 0)
    for t in range(n):
        src = jnp.sum(jnp.where(row8 == t, nxt, 0.0), axis=0, keepdims=True)
        out = jnp.where(row == tm - n + t, src, out)
    return out


def _mix_fwd(proj, o_mla, o_swa, conv_w):
    T = proj.shape[0]
    tm = min(TM_ROW, T)

    def body(gm_ref, ch_ref, cb_ref, cc_ref, gc_ref, gs_ref, pch_ref, pcc_ref, om_ref, os_ref, w_ref, y_ref):
        i = pl.program_id(0)
        row = lax.broadcasted_iota(jnp.int32, (tm, GROUP_WIDTH), 0)
        u = cc_ref[...] * ch_ref[...]
        u_prev = jnp.where(i > 0, pcc_ref[...] * pch_ref[...], 0.0)
        z = (w_ref[0:1, :] * _shift_down(u, u_prev, 2, row) + w_ref[1:2, :] * _shift_down(u, u_prev, 1, row)
             + w_ref[2:3, :] * u)
        gm, gc, gs = gm_ref[...], gc_ref[...], gs_ref[...]
        y_ref[:, 0:512] = (om_ref[...] * (gm * _sigmoid(gm))).astype(MXU_DTYPE)
        y_ref[:, 512:1024] = (cb_ref[...] * z * (gc * _sigmoid(gc))).astype(MXU_DTYPE)
        y_ref[:, 1024:1536] = (os_ref[...] * (gs * _sigmoid(gs))).astype(MXU_DTYPE)

    blk = lambda cb: pl.BlockSpec((tm, 512), lambda i: (i, cb))
    prev = lambda cb: pl.BlockSpec((8, 512), lambda i: (jnp.maximum(i * (tm // 8) - 1, 0), cb))
    tile = pl.BlockSpec((tm, 512), lambda i: (i, 0))
    return pl.pallas_call(
        body, name="mix_fwd", grid=(T // tm,),
        in_specs=[blk(CB_GMLA), blk(CB_CH), blk(CB_CB), blk(CB_CC), blk(CB_GCONV), blk(CB_GSWA),
                  prev(CB_CH), prev(CB_CC), tile, tile, pl.BlockSpec((8, 512), lambda i: (0, 0))],
        out_specs=pl.BlockSpec((tm, D_MIX), lambda i: (i, 0)),
        out_shape=_sds((T, D_MIX), MXU_DTYPE),
        compiler_params=_cp(("parallel",), 32))(
            proj, proj, proj, proj, proj, proj, proj, proj, o_mla, o_swa, conv_w)


def _loss_grad(y, target):
    T, D = y.shape
    tm = min(TM_ROW, T)
    nt = T // tm

    def body(y_ref, t_ref, g_ref, loss_ref, acc_ref):
        i = pl.program_id(0)

        @pl.when(i == 0)
        def _():
            acc_ref[...] = jnp.zeros_like(acc_ref)

        err = y_ref[...] - t_ref[...]
        g_ref[...] = err * (1.0 / D)
        acc_ref[...] += _fold_rows8(err * err)

        @pl.when(i == nt - 1)
        def _():
            tot = jnp.sum(jnp.sum(acc_ref[...], axis=1, keepdims=True), axis=0, keepdims=True)
            loss_ref[...] = jnp.broadcast_to(tot * (0.5 / D), (8, LANES))

    return pl.pallas_call(
        body, name="loss_grad", grid=(nt,),
        in_specs=[pl.BlockSpec((tm, D), lambda i: (i, 0)), pl.BlockSpec((tm, D), lambda i: (i, 0))],
        out_specs=[pl.BlockSpec((tm, D), lambda i: (i, 0)), pl.BlockSpec((8, LANES), lambda i: (0, 0))],
        out_shape=[_sds((T, D), F32), _sds((8, LANES), F32)],
        scratch_shapes=[pltpu.VMEM((8, D), F32)],
        compiler_params=_cp(("arbitrary",), 32))(y, target)


def _mix_bwd(dycat, proj, o_mla, o_swa, conv_w):
    T = proj.shape[0]
    tm = min(TM_ROW, T)
    nt = T // tm

    def body(dym_ref, dyc_ref, dys_ref, gm_ref, ch_ref, cb_ref, cc_ref, gc_ref, gs_ref, pch_ref, pcc_ref,
             ndy_ref, ncb_ref, ngc_ref, om_ref, os_ref, w_ref,
             d1_ref, dgs_ref, dom_ref, dos_ref, dw_ref):
        i = pl.program_id(0)

        @pl.when(i == 0)
        def _():
            dw_ref[...] = jnp.zeros_like(dw_ref)

        row = lax.broadcasted_iota(jnp.int32, (tm, GROUP_WIDTH), 0)

        def gate(g):
            sg = _sigmoid(g)
            return g * sg, sg * (1.0 + g * (1.0 - sg))

        gm = gm_ref[...]
        silu, dsilu = gate(gm)
        dym = dym_ref[...]
        dom_ref[...] = dym * silu
        d1_ref[:, 0:512] = (dym * om_ref[...] * dsilu).astype(MXU_DTYPE)

        gs = gs_ref[...]
        silu, dsilu = gate(gs)
        dys = dys_ref[...]
        dos_ref[...] = dys * silu
        dgs_ref[...] = (dys * os_ref[...] * dsilu).astype(MXU_DTYPE)

        ch, cb, cc, gc, dyc = ch_ref[...], cb_ref[...], cc_ref[...], gc_ref[...], dyc_ref[...]
        w0, w1, w2 = w_ref[0:1, :], w_ref[1:2, :], w_ref[2:3, :]
        u = cc * ch
        u_prev = jnp.where(i > 0, pcc_ref[...] * pch_ref[...], 0.0)
        u1 = _shift_down(u, u_prev, 1, row)
        u2 = _shift_down(u, u_prev, 2, row)
        z = w0 * u2 + w1 * u1 + w2 * u
        silu, dsilu = gate(gc)
        dz = dyc * cb * silu
        ngc = ngc_ref[...]
        dz_next = jnp.where(i < nt - 1, ndy_ref[...] * ncb_ref[...] * (ngc * _sigmoid(ngc)), 0.0)
        du = w2 * dz + w1 * _shift_up(dz, dz_next, 1, row) + w0 * _shift_up(dz, dz_next, 2, row)
        d1_ref[:, 512:1024] = (du * cc).astype(MXU_DTYPE)
        d1_ref[:, 1024:1536] = (dyc * z * silu).astype(MXU_DTYPE)
        d1_ref[:, 1536:2048] = (du * ch).astype(MXU_DTYPE)
        d1_ref[:, 2048:2560] = (dyc * cb * z * dsilu).astype(MXU_DTYPE)
        row8 = lax.broadcasted_iota(jnp.int32, (8, GROUP_WIDTH), 0)
        dw = jnp.zeros((8, GROUP_WIDTH), F32)
        for t, shifted in enumerate((u2, u1, u)):
            dw = dw + jnp.where(row8 == t, jnp.sum(dz * shifted, axis=0, keepdims=True), 0.0)
        dw_ref[...] += dw

    blk = lambda cb: pl.BlockSpec((tm, 512), lambda i: (i, cb))
    prev = lambda cb: pl.BlockSpec((8, 512), lambda i: (jnp.maximum(i * (tm // 8) - 1, 0), cb))
    nxt = lambda cb: pl.BlockSpec((8, 512), lambda i: (jnp.minimum((i + 1) * (tm // 8), T // 8 - 1), cb))
    tile = pl.BlockSpec((tm, 512), lambda i: (i, 0))
    return pl.pallas_call(
        body, name="mix_bwd", grid=(nt,),
        in_specs=[blk(0), blk(1), blk(2), blk(CB_GMLA), blk(CB_CH), blk(CB_CB), blk(CB_CC), blk(CB_GCONV),
                  blk(CB_GSWA), prev(CB_CH), prev(CB_CC), nxt(1), nxt(CB_CB), nxt(CB_GCONV), tile, tile,
                  pl.BlockSpec((8, 512), lambda i: (0, 0))],
        out_specs=[pl.BlockSpec((tm, 2560), lambda i: (i, 0)), tile, tile, tile,
                   pl.BlockSpec((8, 512), lambda i: (0, 0))],
        out_shape=[_sds((T, 2560), MXU_DTYPE), _sds((T, 512), MXU_DTYPE), _sds((T, 512), F32),
                   _sds((T, 512), F32), _sds((8, 512), F32)],
        compiler_params=_cp(("arbitrary",), 48))(
            dycat, dycat, dycat, proj, proj, proj, proj, proj, proj, proj, proj, dycat, proj, proj,
            o_mla, o_swa, conv_w)


def _swa_bwd(proj, o_swa, do_swa, lw):
    T = proj.shape[0]
    scale = SWA_HEAD_DIM ** -0.5

    def body(q_ref, k_ref, v_ref, pk_ref, pv_ref, o_ref, do_ref, qw_ref, kw_ref, alibi_ref, sink_ref,
             dq_ref, dk_ref, dv_ref, dqw_ref, dsink_ref):
        i = pl.program_id(0)

        @pl.when(i == 0)
        def _():
            dk_ref[...] = jnp.zeros_like(dk_ref)
            dv_ref[...] = jnp.zeros_like(dv_ref)
            dqw_ref[...] = jnp.zeros_like(dqw_ref)
            dsink_ref[...] = jnp.zeros_like(dsink_ref)

        p, p_sink, c = _swa_probs(i, q_ref, k_ref, v_ref, pk_ref, pv_ref, qw_ref, kw_ref, alibi_ref, sink_ref)
        half1, kp, vp, qn, qhat, qr = c["half1"], c["kp"], c["vp"], c["qn"], c["qhat"], c["qr"]
        qw = qw_ref[...]
        dob, dd_rows = [], []
        for j in range(4):
            cols = slice(LANES * j, LANES * (j + 1))
            do = do_ref[:, cols]
            dob.append(do.astype(MXU_DTYPE))
            prod = do * o_ref[:, cols]
            dd_rows.append(jnp.sum(jnp.where(half1, 0.0, prod), axis=-1, keepdims=True))
            dd_rows.append(jnp.sum(jnp.where(half1, prod, 0.0), axis=-1, keepdims=True))
        dd = jnp.stack(dd_rows)
        dp = jnp.stack([_dot_nt(dob[h // 2], vp[(h // 4, h % 2)]) for h in range(HEADS)])
        ds = (p * (dp - dd) * scale).astype(MXU_DTYPE)
        dsink = -jnp.sum(p_sink * dd, axis=1, keepdims=True)
        pb = p.astype(MXU_DTYPE)

        dqw = jnp.zeros((1, LANES), F32)
        for j in range(4):
            g = j // 2
            dqn = _dot(ds[2 * j], kp[(g, 0)]) + _dot(ds[2 * j + 1], kp[(g, 1)])
            dqw = dqw + jnp.sum(dqn * qhat[j], axis=0, keepdims=True)
            dq_ref[:, LANES * j:LANES * (j + 1)] = _rms_halves_bwd(dqn, qhat[j], qr[j], qw, half1).astype(MXU_DTYPE)
        dqw_ref[...] += _row0(dqw + pltpu.roll(dqw, 64, 1))

        dk_tot = jnp.zeros((2 * BLOCK, LANES), F32)
        dv_tot = jnp.zeros((2 * BLOCK, LANES), F32)
        for g in range(2):
            for r in range(2):
                own = half1 if r else jnp.logical_not(half1)
                ha, hb = 4 * g + r, 4 * g + 2 + r
                dkp = jnp.where(own, _dot_tn(ds[ha], qn[2 * g]) + _dot_tn(ds[hb], qn[2 * g + 1]), 0.0)
                dvp = jnp.where(own, _dot_tn(pb[ha], dob[2 * g]) + _dot_tn(pb[hb], dob[2 * g + 1]), 0.0)
                if g != r:
                    dkp = pltpu.roll(dkp, 64, 1)
                    dvp = pltpu.roll(dvp, 64, 1)
                dk_tot = dk_tot + dkp
                dv_tot = dv_tot + dvp
        dst = pl.ds(pl.multiple_of(i * BLOCK, BLOCK), 2 * BLOCK)
        dk_ref[dst, :] += dk_tot
        dv_ref[dst, :] += dv_tot

        row8 = lax.broadcasted_iota(jnp.int32, (8, LANES), 0)
        dsink_tile = jnp.zeros((8, LANES), F32)
        for h in range(HEADS):
            dsink_tile = dsink_tile + jnp.where(row8 == h, jnp.broadcast_to(dsink[h], (8, LANES)), 0.0)
        dsink_ref[...] += dsink_tile

    prev = lambda cb: pl.BlockSpec((BLOCK, LANES), lambda i: (jnp.maximum(i - 1, 0), cb))
    tile = pl.BlockSpec((BLOCK, 512), lambda i: (i, 0))
    small = pl.BlockSpec((8, LANES), lambda i: (0, 0))
    acc = pl.BlockSpec((T + BLOCK, LANES), lambda i: (0, 0))
    return pl.pallas_call(
        body, name="swa_bwd", grid=(T // BLOCK,),
        in_specs=[pl.BlockSpec((BLOCK, 512), lambda i: (i, CB_SQ)), pl.BlockSpec((BLOCK, LANES), lambda i: (i, CB_SK)),
                  pl.BlockSpec((BLOCK, LANES), lambda i: (i, CB_SV)), prev(CB_SK), prev(CB_SV), tile, tile,
                  pl.BlockSpec((1, LANES), lambda i: (0, 0)), pl.BlockSpec((1, LANES), lambda i: (0, 0)),
                  pl.BlockSpec((HEADS, BLOCK, 2 * BLOCK), lambda i: (0, 0, 0)),
                  pl.BlockSpec(memory_space=pltpu.SMEM)],
        out_specs=[tile, acc, acc, small, small],
        out_shape=[_sds((T, 512), MXU_DTYPE), _sds((T + BLOCK, LANES), F32), _sds((T + BLOCK, LANES), F32),
                   _sds((8, LANES), F32), _sds((8, LANES), F32)],
        compiler_params=_cp(("arbitrary",), 40))(
            proj, proj, proj, proj, proj, o_swa, do_swa, lw["sqn"], lw["skn"], _swa_alibi(), lw["sinks"])


def _swa_kv_bwd(proj, dkn, dv, lw):
    T = proj.shape[0]
    tm = BLOCK

    def body(k_ref, dkn_ref, dv_ref, kw_ref, d_ref, dkw_ref):
        i = pl.program_id(0)

        @pl.when(i == 0)
        def _():
            dkw_ref[...] = jnp.zeros_like(dkw_ref)

        half1 = lax.broadcasted_iota(jnp.int32, (1, LANES), 1) >= 64
        khat, kr = _rms_halves(k_ref[...], half1)
        dkn_t = dkn_ref[...]
        dkw = jnp.sum(dkn_t * khat, axis=0, keepdims=True)
        dkw_ref[...] += _row0(dkw + pltpu.roll(dkw, 64, 1))
        d_ref[:, 0:LANES] = _rms_halves_bwd(dkn_t, khat, kr, kw_ref[...], half1).astype(MXU_DTYPE)
        d_ref[:, LANES:2 * LANES] = dv_ref[...].astype(MXU_DTYPE)

    return pl.pallas_call(
        body, name="swa_kv_bwd", grid=(T // tm,),
        in_specs=[pl.BlockSpec((tm, LANES), lambda i: (i, CB_SK)), pl.BlockSpec((tm, LANES), lambda i: (i + 1, 0)),
                  pl.BlockSpec((tm, LANES), lambda i: (i + 1, 0)), pl.BlockSpec((1, LANES), lambda i: (0, 0))],
        out_specs=[pl.BlockSpec((tm, 2 * LANES), lambda i: (i, 0)), pl.BlockSpec((8, LANES), lambda i: (0, 0))],
        out_shape=[_sds((T, 2 * LANES), MXU_DTYPE), _sds((8, LANES), F32)],
        compiler_params=_cp(("arbitrary",), 32))(proj, dkn, dv, lw["skn"])


def _mla_attn_bwd(q, k, kt, vpad, o, do, lse):
    T = q.shape[1]
    tk = min(TK, T // 2)
    tq = 2 * tk

    def body(q_ref, k_ref, kt_ref, v_ref, o_ref, do_ref, lse_ref, dq_ref, dk_ref, dv_ref, dqt_s,
             s_a, s_b, p_a, p_b):
        h = pl.program_id(0)
        i = pl.program_id(1)

        @pl.when(i == 0)
        def _():
            dk_ref[...] = jnp.zeros_like(dk_ref)
            dv_ref[...] = jnp.zeros_like(dv_ref)

        key = lax.broadcasted_iota(jnp.int32, (tk, tq), 0)
        qry = lax.broadcasted_iota(jnp.int32, (tk, tq), 1)
        own = (lax.broadcasted_iota(jnp.int32, (1, LANES), 1) // 64) == (h % 2)
        own_rows = (lax.broadcasted_iota(jnp.int32, (LANES, 1), 0) // 64) == (h % 2)
        do_t = do_ref[...]
        dob = do_t.astype(MXU_DTYPE)
        prod_t = (do_t * o_ref[...]).T
        dd = jnp.sum(jnp.where(own_rows, prod_t, 0.0), axis=0, keepdims=True)
        qh = q_ref[0]
        lse_t = lse_ref[0]
        dqt_s[...] = jnp.zeros_like(dqt_s)

        def scores(kj, s_buf, p_buf):
            rows = pl.ds(pl.multiple_of(kj * tk, tk), tk)
            s_buf[...] = _dot_nt(k_ref[0, rows, :], qh)
            p_buf[...] = _dot_nt(v_ref[0, rows, :], dob)

        def consume(kj, s_buf, p_buf, diag):
            rows = pl.ds(pl.multiple_of(kj * tk, tk), tk)
            s = s_buf[...]
            if diag is not None:
                s = jnp.where(key + diag * tk <= qry, s, NEG_INF)
            p = jnp.exp2(s - lse_t)
            ds = (p * (p_buf[...] - dd)).astype(MXU_DTYPE)
            dqt_s[...] += _dot(kt_ref[0, kj], ds)
            dk_ref[0, rows, :] += _dot(ds, qh)
            dv_ref[0, rows, :] += jnp.where(own, _dot(p.astype(MXU_DTYPE), dob), 0.0)

        scores(0, s_a, p_a)

        def pair(kj):
            scores(kj + 1, s_b, p_b)
            consume(kj, s_a, p_a, None)
            scores(kj + 2, s_a, p_a)
            consume(kj + 1, s_b, p_b, None)

        def quad(kq, carry):
            pair(4 * kq)
            pair(4 * kq + 2)
            return carry

        lax.fori_loop(0, i // 2, quad, 0)

        @pl.when(i % 2 == 1)
        def _():
            pair(2 * i - 2)

        scores(2 * i + 1, s_b, p_b)
        consume(2 * i, s_a, p_a, 0)
        consume(2 * i + 1, s_b, p_b, 1)
        dq_ref[0] = dqt_s[...].T

    res = pl.BlockSpec((1, T, LANES), lambda h, i: (h, 0, 0))
    res_t = pl.BlockSpec((1, T // tk, LANES, tk), lambda h, i: (h, 0, 0, 0))
    buf = pltpu.VMEM((tk, tq), F32)
    return pl.pallas_call(
        body, name="mla_attn_bwd", grid=(HEADS, T // tq),
        in_specs=[pl.BlockSpec((1, tq, LANES), lambda h, i: (h, i, 0)), res, res_t, res,
                  pl.BlockSpec((tq, LANES), lambda h, i: (i, h // 2)),
                  pl.BlockSpec((tq, LANES), lambda h, i: (i, h // 2)),
                  pl.BlockSpec((1, 1, tq), lambda h, i: (h, 0, i))],
        out_specs=[pl.BlockSpec((1, tq, LANES), lambda h, i: (h, i, 0)), res, res],
        out_shape=[_sds((HEADS, T, LANES), F32)] * 3,
        scratch_shapes=[pltpu.VMEM((LANES, tq), F32), buf, buf, buf, buf],
        compiler_params=_cp(("parallel", "arbitrary"), 48))(q, k, kt, vpad, o, do, lse)


def _mla_prep_bwd(proj, dq, dk, dv, lw, rope):
    T = proj.shape[0]
    tm = min(TK, T // 2)

    def body(ql_ref, kvl_ref, kr_ref, dq_ref, dk_ref, dv_ref, qa_ref, kva_ref, wq_ref, wk_ref, wv_ref,
             wqt_ref, wkt_ref, wvt_ref, qn_ref, kn_ref, c_ref, s1_ref, s2_ref,
             d_ref, dwq_ref, dwk_ref, dwv_ref, dqa_ref, dkva_ref, dqn_ref, dkn_ref):
        i = pl.program_id(0)

        @pl.when(i == 0)
        def _():
            for ref in (dwq_ref, dwk_ref, dwv_ref, dqa_ref, dkva_ref, dqn_ref, dkn_ref):
                ref[...] = jnp.zeros_like(ref)

        c, s1, s2 = c_ref[...], s1_ref[...], s2_ref[...]
        lane = lax.broadcasted_iota(jnp.int32, (1, LANES), 1)
        qlhat, qlr = _rms(ql_ref[...], MLA_Q_LORA)
        qn = (qlhat * qa_ref[...]).astype(MXU_DTYPE)
        kvhat, kvr = _rms(kvl_ref[...], MLA_KV_LORA)
        kvn = (kvhat * kva_ref[...]).astype(MXU_DTYPE)
        kr = kr_ref[...]
        x3, r3 = _rms(jnp.stack([_dot(qn, wq_ref[h]) for h in range(HEADS)]), MLA_QK)
        dy3 = _rope_bwd(dq_ref[...] * MLA_SCALE, c, s1, s2)
        dqw = jnp.sum(jnp.sum(dy3 * x3, axis=0), axis=0, keepdims=True)
        dx3 = _rms_bwd(dy3, x3, r3, qn_ref[...], MLA_QK).astype(MXU_DTYPE)
        dqnl = jnp.zeros((tm, MLA_Q_LORA), F32)
        for h in range(HEADS):
            dwq_ref[h] += _dot_tn(qn, dx3[h])
            dqnl = dqnl + _dot(dx3[h], wqt_ref[h])

        x3, r3 = _rms(jnp.stack([_dot(kvn, wk_ref[h]) for h in range(HEADS)]) + kr, MLA_QK)
        dy3 = _rope_bwd(dk_ref[...] * LN2, c, s1, s2)
        dkw = jnp.sum(jnp.sum(dy3 * x3, axis=0), axis=0, keepdims=True)
        dxf3 = _rms_bwd(dy3, x3, r3, kn_ref[...], MLA_QK)
        dkr = jnp.sum(dxf3, axis=0)
        dx3 = dxf3.astype(MXU_DTYPE)
        dkvn = jnp.zeros((tm, MLA_KV_LORA), F32)
        for h in range(HEADS):
            dwk_ref[h] += _dot_tn(kvn, dx3[h])
            dkvn = dkvn + _dot(dx3[h], wkt_ref[h])
        dvc = jnp.concatenate([dv_ref[2 * j] + dv_ref[2 * j + 1] for j in range(4)], axis=1).astype(MXU_DTYPE)
        dwv_ref[...] += _dot_tn(kvn, dvc)
        dkvn = dkvn + _dot(dvc, wvt_ref[...])
        dqa_ref[...] += _row0(jnp.sum(dqnl * qlhat, axis=0, keepdims=True))
        dkva_ref[...] += _row0(jnp.sum(dkvn * kvhat, axis=0, keepdims=True))
        dqn_ref[...] += _row0(dqw)
        dkn_ref[...] += _row0(dkw)
        d_ref[:, 0:256] = _rms_bwd(dqnl, qlhat, qlr, qa_ref[...], MLA_Q_LORA).astype(MXU_DTYPE)
        d_ref[:, 256:384] = _rms_bwd(dkvn, kvhat, kvr, kva_ref[...], MLA_KV_LORA).astype(MXU_DTYPE)
        d_ref[:, 384:512] = jnp.where((lane >= 64) & (lane < 96), dkr, 0.0).astype(MXU_DTYPE)

    full = lambda shape: pl.BlockSpec(shape, lambda i: (0,) * len(shape))
    hd = pl.BlockSpec((HEADS, tm, LANES), lambda i: (0, i, 0))
    tab = pl.BlockSpec((tm, LANES), lambda i: (i, 0))
    return pl.pallas_call(
        body, name="mla_prep_bwd", grid=(T // tm,),
        in_specs=[pl.BlockSpec((tm, 256), lambda i: (i, CB_QLAT)), pl.BlockSpec((tm, LANES), lambda i: (i, CB_KVLAT)),
                  pl.BlockSpec((tm, LANES), lambda i: (i, CB_KROPE)), hd, hd, hd,
                  full((1, 256)), full((1, LANES)), full((HEADS, 256, LANES)), full((HEADS, LANES, LANES)),
                  full((LANES, 512)), full((HEADS, LANES, 256)), full((HEADS, LANES, LANES)), full((512, LANES)),
                  full((1, LANES)), full((1, LANES)), tab, tab, tab],
        out_specs=[pl.BlockSpec((tm, 512), lambda i: (i, 0)), full((HEADS, 256, LANES)),
                   full((HEADS, LANES, LANES)), full((LANES, 512)), full((8, 256)), full((8, LANES)),
                   full((8, LANES)), full((8, LANES))],
        out_shape=[_sds((T, 512), MXU_DTYPE), _sds((HEADS, 256, LANES), F32), _sds((HEADS, LANES, LANES), F32),
                   _sds((LANES, 512), F32), _sds((8, 256), F32), _sds((8, LANES), F32), _sds((8, LANES), F32),
                   _sds((8, LANES), F32)],
        compiler_params=_cp(("arbitrary",), 48))(
            proj, proj, proj, dq, dk, dv, lw["qa"], lw["kva"], lw["wq"], lw["wk"], lw["wv"],
            lw["wqt"], lw["wkt"], lw["wvt"], lw["qn"], lw["kn"], rope[0], rope[1], rope[2])


def _norm_bwd(dh, x, g_in, ng):
    T, D = x.shape
    tm = min(TM_ROW, T)

    def body(dh_ref, x_ref, g_ref, w_ref, dx_ref, dw_ref):
        i = pl.program_id(0)

        @pl.when(i == 0)
        def _():
            dw_ref[...] = jnp.zeros_like(dw_ref)

        xhat, r = _rms(x_ref[...], D)
        dh_t = dh_ref[...]
        dw_ref[...] += _row0(jnp.sum(dh_t * xhat, axis=0, keepdims=True))
        dx_ref[...] = g_ref[...] + _rms_bwd(dh_t, xhat, r, w_ref[...], D)

    tile = pl.BlockSpec((tm, D), lambda i: (i, 0))
    return pl.pallas_call(
        body, name="norm_bwd", grid=(T // tm,),
        in_specs=[tile, tile, tile, pl.BlockSpec((1, D), lambda i: (0, 0))],
        out_specs=[tile, pl.BlockSpec((8, D), lambda i: (0, 0))],
        out_shape=[_sds((T, D), F32), _sds((8, D), F32)],
        compiler_params=_cp(("arbitrary",), 32))(dh, x, g_in, ng)


def _rope_tables(T):
    half = MLA_ROPE // 2
    inv_freq = jnp.power(jnp.float32(ROPE_THETA), -jnp.arange(half, dtype=F32) / half)
    ang = jnp.arange(T, dtype=F32)[:, None] * inv_freq[None, :]
    cos, sin = jnp.cos(ang), jnp.sin(ang)
    z = lambda n: jnp.zeros((T, n), F32)
    c = jnp.concatenate([jnp.ones((T, MLA_NOPE), F32), cos, cos, z(32)], axis=1)
    s1 = jnp.concatenate([z(64), -sin, z(48)], axis=1)
    s2 = jnp.concatenate([z(80), sin, z(32)], axis=1)
    return c, s1, s2


def _pad_lanes(v, n=LANES):
    v = v.reshape(1, -1)
    return jnp.pad(v, ((0, 0), (0, n - v.shape[1])))


def _pack_win(w):
    z = lambda n: jnp.zeros((w.shape[0], n), w.dtype)
    return jnp.concatenate([w[:, 0:384], z(64), w[:, 384:416], z(32), w[:, 416:2976], w[:, 2976:3488],
                            w[:, 3744:4256], w[:, 3488:3616], w[:, 3616:3744]], axis=1)


def _unpack_dwin(d):
    return jnp.concatenate([d[:, 0:384], d[:, 448:480], d[:, 512:3072], d[:, 3072:3584], d[:, 4096:4224],
                            d[:, 4224:4352], d[:, 3584:4096]], axis=1)


def _inproj_weights(l, norm_g, w_in_full):
    wp = _pack_win(w_in_full)
    return dict(ng=norm_g[l].reshape(1, -1), wp=wp, wpt=wp.T)


def _mixer_weights(l, qa, wqb_full, kva, wkvb_full, qn, kn, conv_full, sqn, skn, sinks, w_out_full):
    wq = jnp.pad(wqb_full, ((0, 0), (0, 0), (0, LANES - MLA_QK)))
    wk = jnp.pad(wkvb_full[:, :, :MLA_NOPE], ((0, 0), (0, 0), (0, LANES - MLA_NOPE)))
    wv = jnp.transpose(wkvb_full[:, :, MLA_NOPE:], (1, 0, 2)).reshape(MLA_KV_LORA, GROUP_WIDTH)
    return dict(
        qa=qa[l].reshape(1, -1), kva=kva[l].reshape(1, -1),
        wq=wq, wk=wk, wv=wv, wqt=jnp.transpose(wq, (0, 2, 1)), wkt=jnp.transpose(wk, (0, 2, 1)), wvt=wv.T,
        qn=_pad_lanes(qn[l]), kn=_pad_lanes(kn[l]),
        conv=jnp.pad(conv_full, ((0, 5), (0, 0))),
        sqn=jnp.tile(sqn[l].reshape(1, -1), (1, 2)), skn=jnp.tile(skn[l].reshape(1, -1), (1, 2)),
        sinks=sinks[l], wo=w_out_full, wot=w_out_full.T)


def _layer_weights(l, norm_g, w_in_full, qa, wqb_full, kva, wkvb_full, qn, kn, conv_full, sqn, skn, sinks,
                   w_out_full):
    return dict(_inproj_weights(l, norm_g, w_in_full),
                **_mixer_weights(l, qa, wqb_full, kva, wkvb_full, qn, kn, conv_full, sqn, skn, sinks, w_out_full))


def _layer_fwd(x, lw, rope, late_weights=None):
    proj, h = _inproj_fwd(x, lw["ng"], lw["wp"])
    if late_weights is not None:
        lw = dict(lw, **late_weights(proj))
    q, k, kt, vpad, vt = _mla_prep_fwd(proj, lw, rope)
    o_mla, lse = _mla_attn_fwd(q, k, vt)
    o_swa = _swa_fwd(proj, lw)
    ycat = _mix_fwd(proj, o_mla, o_swa, lw["conv"])
    x_next = _mm_nn(ycat, lw["wo"], "outproj_fwd", residual=x)
    return x_next, dict(x=x, proj=proj, h=h, q=q, k=k, kt=kt, vpad=vpad, o_mla=o_mla, lse=lse, o_swa=o_swa, ycat=ycat,
                        lw=lw)


def _layer_bwd(g, sv, lw, rope, on_big_grads=None):
    proj = sv["proj"]
    dycat = _mm_nn(g, lw["wot"], "outproj_bwd_dy")
    d_wo = _mm_tn(sv["ycat"], g, "outproj_bwd_dw", WIRE_DTYPE, tn=D_MODEL)
    d1, dgs, do_mla, do_swa, d_conv = _mix_bwd(dycat, proj, sv["o_mla"], sv["o_swa"], lw["conv"])
    dsq, dkn_acc, dv_acc, d_sqn, d_sinks = _swa_bwd(proj, sv["o_swa"], do_swa, lw)
    dskv, d_skn = _swa_kv_bwd(proj, dkn_acc, dv_acc, lw)
    dq, dk, dv = _mla_attn_bwd(sv["q"], sv["k"], sv["kt"], sv["vpad"], sv["o_mla"], do_mla, sv["lse"])
    dmla, d_wq, d_wk, d_wv, d_qa, d_kva, d_qn, d_kn = _mla_prep_bwd(proj, dq, dk, dv, lw, rope)
    dproj = jnp.concatenate([dmla, d1, dsq, dgs, dskv], axis=1)
    d_wp = _mm_tn(sv["h"], dproj, "inproj_bwd_dw", WIRE_DTYPE, tn=NP // 2)
    grads = dict(
        w_in=_unpack_dwin(d_wp), w_out=d_wo,
        w_qb=d_wq[:, :, :MLA_QK],
        w_kvb=jnp.concatenate([d_wk[:, :, :MLA_NOPE],
                               jnp.transpose(d_wv.reshape(MLA_KV_LORA, HEADS, MLA_NOPE), (1, 0, 2))], axis=2))
    ng = lw["ng"] if on_big_grads is None else lw["ng"] + on_big_grads(grads)
    dh = _mm_nn(dproj, lw["wpt"], "inproj_bwd_dh")
    dx, d_ng = _norm_bwd(dh, sv["x"], g, ng)
    grads.update(
        conv=d_conv[0:3], norm_g=d_ng[0], qa=d_qa[0], kva=d_kva[0], qn=d_qn[0, :MLA_QK], kn=d_kn[0, :MLA_QK],
        sqn=d_sqn[0, :SWA_HEAD_DIM], skn=d_skn[0, :SWA_HEAD_DIM], sinks=d_sinks[:, 0])
    return dx, grads


def _local_step(x, target, lws, rope):
    saved = []
    for lw in lws:
        x, sv = _layer_fwd(x, lw, rope)
        saved.append(sv)
    g, loss_tile = _loss_grad(x, target)
    grads = [None] * len(lws)
    for l in reversed(range(len(lws))):
        g, grads[l] = _layer_bwd(g, saved[l], lws[l], rope)
    return loss_tile, g, grads


def _my_coords():
    return lax.axis_index("x"), lax.axis_index("y"), lax.axis_index("c")


def _peer(me, k):
    x, y, c = me
    return (1 - x if k & 4 else x, 1 - y if k & 2 else y, 1 - c if k & 1 else c)


def _lin(d):
    return 4 * d[0] + 2 * d[1] + d[2]


def _all_gather(shards):
    n = len(shards)

    def body(*refs):
        ins, outs = refs[:n], refs[n:2 * n]
        send_sems, recv_sems, local_sems = refs[2 * n:]
        me = _my_coords()
        my = _lin(me)
        local = [pltpu.make_async_copy(ins[a], outs[a].at[my], local_sems.at[a]) for a in range(n)]
        for cp in local:
            cp.start()
        sends = []
        for a in range(n):
            for k in range(1, N_DEV):
                cp = pltpu.make_async_remote_copy(
                    src_ref=ins[a], dst_ref=outs[a].at[my], send_sem=send_sems.at[a * 7 + k - 1],
                    recv_sem=recv_sems.at[a * 7 + k - 1], device_id=_peer(me, k),
                    device_id_type=pl.DeviceIdType.MESH)
                cp.start()
                sends.append(cp)
        for a in range(n):
            for k in range(1, N_DEV):
                src = _lin(_peer(me, k))
                pltpu.make_async_remote_copy(
                    src_ref=ins[a], dst_ref=outs[a].at[src], send_sem=send_sems.at[a * 7 + k - 1],
                    recv_sem=recv_sems.at[a * 7 + k - 1], device_id=_peer(me, k),
                    device_id_type=pl.DeviceIdType.MESH).wait_recv()
        for cp in sends:
            cp.wait_send()
        for cp in local:
            cp.wait()

    any_spec = pl.BlockSpec(memory_space=pl.ANY)
    return pl.pallas_call(
        body, name="weight_all_gather",
        in_specs=[any_spec] * n, out_specs=[any_spec] * n,
        out_shape=[_sds((N_DEV,) + s.shape, s.dtype) for s in shards],
        scratch_shapes=[pltpu.SemaphoreType.DMA((7 * n,)), pltpu.SemaphoreType.DMA((7 * n,)),
                        pltpu.SemaphoreType.DMA((n,))],
    )(*shards)


def _push_copies(ins, lands, send_sems, recv_sems, gather):
    me = _my_coords()
    my = _lin(me)
    out, inc = [], []
    for a in range(len(ins)):
        for k in range(1, N_DEV):
            peer = _peer(me, k)
            sems = dict(send_sem=send_sems.at[a * 7 + k - 1], recv_sem=recv_sems.at[a * 7 + k - 1],
                        device_id=peer, device_id_type=pl.DeviceIdType.MESH)
            src = ins[a] if gather else ins[a].at[_lin(peer)]
            out.append(pltpu.make_async_remote_copy(src_ref=src, dst_ref=lands[a].at[my], **sems))
            inc.append(pltpu.make_async_remote_copy(src_ref=src, dst_ref=lands[a].at[_lin(peer)], **sems))
    return out, inc


def _push_start(arrays, name, gather):
    n = len(arrays)
    land_shapes = [((N_DEV,) + a.shape) if gather else a.shape for a in arrays]

    def body(*refs):
        ins, lands = refs[:n], refs[n:2 * n]
        send_sems, recv_sems = refs[2 * n], refs[2 * n + 1]
        token = refs[-1]
        out, _ = _push_copies(ins, lands, send_sems, recv_sems, gather)
        for cp in out:
            cp.start()
        token[...] = jnp.zeros_like(token)

    hbm = pl.BlockSpec(memory_space=pltpu.HBM)
    sem = pl.BlockSpec(memory_space=pltpu.SEMAPHORE)
    res = pl.pallas_call(
        body, name=name,
        out_shape=(pltpu.SemaphoreType.DMA((7 * n,)), pltpu.SemaphoreType.DMA((7 * n,)),
                   *[pltpu.HBM(a.shape, a.dtype) for a in arrays],
                   *[pltpu.HBM(s, a.dtype) for s, a in zip(land_shapes, arrays)],
                   _sds((8, LANES), F32)),
        in_specs=(hbm,) * (2 * n),
        out_specs=(sem, sem) + (hbm,) * (2 * n) + (pl.BlockSpec(memory_space=pltpu.VMEM),),
        input_output_aliases={i: 2 + i for i in range(2 * n)},
        compiler_params=pltpu.CompilerParams(has_side_effects=pltpu.SideEffectType.DATAFLOW_SIDE_EFFECTING),
    )(*[pltpu.with_memory_space_constraint(a, pltpu.HBM) for a in arrays],
      *[pltpu.with_memory_space_constraint(lax.empty(s, a.dtype), pltpu.HBM) for s, a in zip(land_shapes, arrays)])
    return dict(send=res[0], recv=res[1], src=res[2:2 + n], land=res[2 + n:2 + 2 * n], token=res[-1][0, 0],
                gather=gather)


def _push_wait(handle, after, name):
    n = len(handle["src"])
    gather = handle["gather"]

    def body(*refs):
        ins, lands = refs[:n], refs[n:2 * n]
        send_sems, recv_sems = refs[2 * n], refs[2 * n + 1]
        out, inc = _push_copies(ins, lands, send_sems, recv_sems, gather)
        for cp in out:
            cp.wait_send()
        for cp in inc:
            cp.wait_recv()

    hbm = pl.BlockSpec(memory_space=pltpu.HBM)
    sem = pl.BlockSpec(memory_space=pltpu.SEMAPHORE)
    res = pl.pallas_call(
        body, name=name,
        out_shape=tuple(pltpu.HBM(a.shape, a.dtype) for a in (*handle["src"], *handle["land"])),
        in_specs=(hbm,) * (2 * n) + (sem, sem, pl.BlockSpec(memory_space=pl.ANY)),
        out_specs=(hbm,) * (2 * n),
        input_output_aliases={i: i for i in range(2 * n)},
        compiler_params=pltpu.CompilerParams(has_side_effects=pltpu.SideEffectType.DATAFLOW_SIDE_EFFECTING),
    )(*handle["src"], *handle["land"], handle["send"], handle["recv"], after)
    return res[n:]


def _small_all_reduce(v):
    R = v.shape[0]

    def body(v_ref, o_ref, buf, send_sems, recv_sems):
        me = _my_coords()
        my = _lin(me)
        sends = []
        for k in range(1, N_DEV):
            cp = pltpu.make_async_remote_copy(
                src_ref=v_ref, dst_ref=buf.at[my], send_sem=send_sems.at[k - 1], recv_sem=recv_sems.at[k - 1],
                device_id=_peer(me, k), device_id_type=pl.DeviceIdType.MESH)
            cp.start()
            sends.append(cp)
        buf[my] = v_ref[...]
        for k in range(1, N_DEV):
            pltpu.make_async_remote_copy(
                src_ref=v_ref, dst_ref=buf.at[_lin(_peer(me, k))], send_sem=send_sems.at[k - 1],
                recv_sem=recv_sems.at[k - 1], device_id=_peer(me, k),
                device_id_type=pl.DeviceIdType.MESH).wait_recv()
        for cp in sends:
            cp.wait_send()
        tot = buf[0]
        for d in range(1, N_DEV):
            tot = tot + buf[d]
        o_ref[...] = tot

    vm = pl.BlockSpec(memory_space=pltpu.VMEM)
    return pl.pallas_call(
        body, name="small_all_reduce", in_specs=[vm], out_specs=vm, out_shape=_sds(v.shape, F32),
        scratch_shapes=[pltpu.VMEM((N_DEV, R, LANES), F32), pltpu.SemaphoreType.DMA((7,)),
                        pltpu.SemaphoreType.DMA((7,))],
    )(v)


def _adamw_math(w, g, m, v):
    m = ADAM_B1 * m + (1.0 - ADAM_B1) * g
    v = ADAM_B2 * v + (1.0 - ADAM_B2) * (g * g)
    m_hat = m / (1.0 - ADAM_B1 ** ADAM_STEP)
    v_hat = v / (1.0 - ADAM_B2 ** ADAM_STEP)
    delta = -ADAM_LR * (m_hat / (jnp.sqrt(v_hat) + ADAM_EPS) + ADAM_WD * w)
    return delta, m, v


def _adamw(parts, w, m, v, name, tr):
    P, R, C = parts.shape
    tr = min(tr, R)

    def body(p_ref, w_ref, m_ref, v_ref, g_out, d_out, m_out, v_out):
        g = p_ref[0].astype(F32)
        for d in range(1, P):
            g = g + p_ref[d].astype(F32)
        delta, m_new, v_new = _adamw_math(w_ref[...], g, m_ref[...], v_ref[...])
        g_out[...] = g
        d_out[...] = delta
        m_out[...] = m_new
        v_out[...] = v_new

    tile = pl.BlockSpec((tr, C), lambda i: (i, 0))
    return pl.pallas_call(
        body, name=name, grid=(R // tr,),
        in_specs=[pl.BlockSpec((P, tr, C), lambda i: (0, i, 0)), tile, tile, tile],
        out_specs=[tile] * 4, out_shape=[_sds((R, C), F32)] * 4,
        compiler_params=_cp(("parallel",), 32))(parts, w, m, v)


SMALL = (("norm_g", D_MODEL), ("mla_q_a_norm", MLA_Q_LORA), ("mla_kv_a_norm", MLA_KV_LORA), ("mla_q_norm", MLA_QK),
         ("mla_k_norm", MLA_QK), ("swa_q_norm", SWA_HEAD_DIM), ("swa_k_norm", SWA_HEAD_DIM), ("swa_sinks", HEADS))
SMALL_GRAD_KEY = dict(norm_g="norm_g", mla_q_a_norm="qa", mla_kv_a_norm="kva", mla_q_norm="qn", mla_k_norm="kn",
                      swa_q_norm="sqn", swa_k_norm="skn", swa_sinks="sinks")
SMALL_ROWS = 32
CONV_ROWS = 24


def _pack_small(get):
    parts = []
    for l in range(DEPTH):
        for name, n in SMALL:
            v = get(name, l).reshape(-1)
            parts.append(jnp.pad(v, (0, (-n) % LANES)))
    return jnp.concatenate(parts).reshape(SMALL_ROWS, LANES)


def _unpack_small(packed):
    flat = packed.reshape(-1)
    out = {name: [] for name, _ in SMALL}
    off = 0
    for l in range(DEPTH):
        for name, n in SMALL:
            out[name].append(flat[off:off + n])
            off += n + (-n) % LANES
    return {name: jnp.stack(v) for name, v in out.items()}


def kernel(x, norm_g, w_in, mla_q_a_norm, mla_w_qb, mla_kv_a_norm, mla_w_kvb, mla_q_norm, mla_k_norm, conv_w, swa_q_norm, swa_k_norm, swa_sinks, w_out, loss_target, m_norm_g, m_w_in, m_mla_q_a_norm, m_mla_w_qb, m_mla_kv_a_norm, m_mla_w_kvb, m_mla_q_norm, m_mla_k_norm, m_conv_w, m_swa_q_norm, m_swa_k_norm, m_swa_sinks, m_w_out, v_norm_g, v_w_in, v_mla_q_a_norm, v_mla_w_qb, v_mla_kv_a_norm, v_mla_w_kvb, v_mla_q_norm, v_mla_k_norm, v_conv_w, v_swa_q_norm, v_swa_k_norm, v_swa_sinks, v_w_out):
    T = x.shape[1]
    weights = dict(norm_g=norm_g, w_in=w_in, mla_q_a_norm=mla_q_a_norm, mla_w_qb=mla_w_qb,
                   mla_kv_a_norm=mla_kv_a_norm, mla_w_kvb=mla_w_kvb, mla_q_norm=mla_q_norm, mla_k_norm=mla_k_norm,
                   conv_w=conv_w, swa_q_norm=swa_q_norm, swa_k_norm=swa_k_norm, swa_sinks=swa_sinks, w_out=w_out)
    mom_m = dict(norm_g=m_norm_g, w_in=m_w_in, mla_q_a_norm=m_mla_q_a_norm, mla_w_qb=m_mla_w_qb,
                 mla_kv_a_norm=m_mla_kv_a_norm, mla_w_kvb=m_mla_w_kvb, mla_q_norm=m_mla_q_norm,
                 mla_k_norm=m_mla_k_norm, conv_w=m_conv_w, swa_q_norm=m_swa_q_norm, swa_k_norm=m_swa_k_norm,
                 swa_sinks=m_swa_sinks, w_out=m_w_out)
    mom_v = dict(norm_g=v_norm_g, w_in=v_w_in, mla_q_a_norm=v_mla_q_a_norm, mla_w_qb=v_mla_w_qb,
                 mla_kv_a_norm=v_mla_kv_a_norm, mla_w_kvb=v_mla_w_kvb, mla_q_norm=v_mla_q_norm,
                 mla_k_norm=v_mla_k_norm, conv_w=v_conv_w, swa_q_norm=v_swa_q_norm, swa_k_norm=v_swa_k_norm,
                 swa_sinks=v_swa_sinks, w_out=v_w_out)

    my = _lin(_my_coords())
    rope = _rope_tables(T)

    def shards(l):
        return [w_in[l].astype(MXU_DTYPE), mla_w_qb[l].astype(MXU_DTYPE), mla_w_kvb[l].astype(MXU_DTYPE),
                w_out[l].astype(MXU_DTYPE), conv_w[l]]

    def inproj_weights(l, g_win):
        return _inproj_weights(l, norm_g, jnp.transpose(g_win, (1, 0, 2)).reshape(D_MODEL, IN_COLS))

    def mixer_weights(l, gathered):
        g_wqb, g_wkvb, g_wout, g_conv = gathered
        return _mixer_weights(
            l, mla_q_a_norm, g_wqb, mla_kv_a_norm, g_wkvb, mla_q_norm, mla_k_norm,
            jnp.transpose(g_conv, (1, 0, 2)).reshape(3, GROUP_WIDTH), swa_q_norm, swa_k_norm, swa_sinks,
            g_wout.reshape(D_MIX, D_MODEL))

    def slots(g):
        return [jnp.transpose(g["w_in"].reshape(D_MODEL, N_DEV, IN_COLS // N_DEV), (1, 0, 2)),
                g["w_out"].reshape(N_DEV, D_MIX // N_DEV, D_MODEL), g["w_qb"], g["w_kvb"]]

    def own_slot(landed, mine):
        return [lax.dynamic_update_index_in_dim(a, m, my, 0) for a, m in zip(landed, mine)]

    def landed(handle, after, name, mine):
        return own_slot(_push_wait(handle, after, name), mine)

    lw0 = inproj_weights(0, _all_gather(shards(0)[:1])[0])
    gather0 = _push_start(shards(0)[1:], "weight_gather0_start", gather=True)
    gather1 = _push_start(shards(1), "weight_gather1_start", gather=True)
    x1, sv0 = _layer_fwd(
        x[0], dict(lw0, ng=lw0["ng"] + (gather0["token"] + gather1["token"])), rope,
        late_weights=lambda proj: mixer_weights(0, landed(gather0, proj, "weight_gather0_wait", shards(0)[1:])))
    g1_all = landed(gather1, x1, "weight_gather1_wait", shards(1))
    x2, sv1 = _layer_fwd(x1, dict(inproj_weights(1, g1_all[0]), **mixer_weights(1, g1_all[1:])), rope)
    g2, loss_tile = _loss_grad(x2, loss_target[0])

    started = {}

    def start_exchange(l, big):
        sl = slots(big)
        started[l] = (sl, _push_start(sl, "grad_exchange%d_start" % l, gather=False))
        return started[l][1]["token"]

    def received(l, after):
        sl, handle = started[l]
        return landed(handle, after, "grad_exchange%d_wait" % l, [s[my] for s in sl])

    g1, grads1 = _layer_bwd(g2, sv1, sv1["lw"], rope, on_big_grads=lambda big: start_exchange(1, big))
    lw0b = dict(sv0["lw"], conv=sv0["lw"]["conv"] + started[1][1]["token"])
    grad_x, grads0 = _layer_bwd(g1, sv0, lw0b, rope, on_big_grads=lambda big: start_exchange(0, big))
    recv1 = received(1, grad_x)
    recv0 = received(0, grad_x)
    grads = [grads0, grads1]
    r_win, r_wout, r_wqb, r_wkvb = [jnp.stack([a, b], axis=1) for a, b in zip(recv0, recv1)]

    small = jnp.concatenate([
        _pack_small(lambda name, l: grads[l][SMALL_GRAD_KEY[name]]),
        jnp.stack([g["conv"] for g in grads]).reshape(CONV_ROWS, LANES),
        loss_tile], axis=0)
    small = _small_all_reduce(small)
    loss = small[SMALL_ROWS + CONV_ROWS, 0]
    my = _lin(_my_coords())
    conv_g = lax.dynamic_slice_in_dim(small[SMALL_ROWS:SMALL_ROWS + CONV_ROWS].reshape(DEPTH, 3, GROUP_WIDTH),
                                      my * 64, 64, axis=2)

    out = {}

    def big(name, recv, rows, cols, tr):
        res = _adamw(recv.reshape(N_DEV, rows, cols), weights[name].reshape(rows, cols),
                     mom_m[name].reshape(rows, cols), mom_v[name].reshape(rows, cols), "adamw_" + name, tr)
        out[name] = [r.reshape(weights[name].shape) for r in res]

    big("w_in", r_win, DEPTH * D_MODEL, IN_COLS // N_DEV, 256)
    big("w_out", r_wout, DEPTH * D_MIX // N_DEV, D_MODEL, 192)
    big("mla_w_qb", r_wqb, DEPTH * MLA_Q_LORA, MLA_QK, 512)
    big("mla_w_kvb", r_wkvb, DEPTH * MLA_KV_LORA, 128, 256)

    pad_conv = lambda a: jnp.pad(a.reshape(-1), (0, 8 * LANES - 6 * 64)).reshape(8, LANES)
    cat = lambda src: jnp.concatenate([_pack_small(lambda name, l: src[name][l]), pad_conv(src["conv_w"])], axis=0)
    g_small = jnp.concatenate([small[:SMALL_ROWS], pad_conv(conv_g)], axis=0)
    res = _adamw(g_small[None], cat(weights), cat(mom_m), cat(mom_v), "adamw_small", SMALL_ROWS + 8)
    smalls = [_unpack_small(r[:SMALL_ROWS]) for r in res]
    for name, _ in SMALL:
        out[name] = [s[name] for s in smalls]
    out["conv_w"] = [r[SMALL_ROWS:].reshape(-1)[:6 * 64].reshape(DEPTH, 3, 64) for r in res]

    order = ["norm_g", "w_in", "mla_q_a_norm", "mla_w_qb", "mla_kv_a_norm", "mla_w_kvb", "mla_q_norm", "mla_k_norm",
             "conv_w", "swa_q_norm", "swa_k_norm", "swa_sinks", "w_out"]
    result = [loss, grad_x[None]]
    for idx in range(4):
        result += [out[name][idx] for name in order]
    return tuple(result)
```

```python
import functools

import jax
import jax.numpy as jnp
import numpy as np
from jax import lax
from jax.experimental import pallas as pl
from jax.experimental.pallas import tpu as pltpu

F32 = jnp.float32
MXU_DTYPE = jnp.bfloat16
WIRE_DTYPE = jnp.bfloat16

N_DEV = 8
DEPTH = 2
D_MODEL = 1024
GROUP_WIDTH = 512
D_MIX = 3 * GROUP_WIDTH
BLOCK = 128
RMS_EPS = 1e-6
NEG_INF = -1e30
HEADS = 8
MLA_QK = 96
MLA_NOPE = 64
MLA_ROPE = 32
MLA_Q_LORA = 256
MLA_KV_LORA = 128
ROPE_THETA = 10000.0
SWA_HEAD_DIM = 64
LANES = 128
IN_COLS = 4256

ADAM_LR = 0.001
ADAM_B1 = 0.9
ADAM_B2 = 0.999
ADAM_EPS = 1e-08
ADAM_WD = 0.01
ADAM_STEP = 10

NP = 4352
CB_QLAT = 0
CB_KVLAT = 2
CB_KROPE = 3
CB_GMLA, CB_CH, CB_CB, CB_CC, CB_GCONV, CB_SQ, CB_GSWA = 1, 2, 3, 4, 5, 6, 7
CB_SK, CB_SV = 32, 33

TM_PROJ = 256
TM_ROW = 256
TK = 256
TQ = 2 * TK
MLA_SCALE = MLA_QK ** -0.5
LOG2E = 1.4426950408889634
LN2 = 0.6931471805599453
TM_SWA = 512
VMEM_MB = 2 ** 20


def _cp(sem, vmem_mb):
    return pltpu.CompilerParams(dimension_semantics=sem, vmem_limit_bytes=vmem_mb * VMEM_MB)


def _sds(shape, dtype):
    return jax.ShapeDtypeStruct(shape, dtype)


def _dot(a, b):
    return jnp.dot(a, b, preferred_element_type=F32)


def _dot_nt(a, b):
    return lax.dot_general(a, b, (((1,), (1,)), ((), ())), preferred_element_type=F32)


def _dot_tn(a, b):
    return lax.dot_general(a, b, (((0,), (0,)), ((), ())), preferred_element_type=F32)


def _rms(x, n):
    r = lax.rsqrt(jnp.sum(x * x, axis=-1, keepdims=True) * (1.0 / n) + RMS_EPS)
    return x * r, r


def _rms_bwd(dy, xhat, r, w, n):
    g = dy * w
    return r * (g - xhat * (jnp.sum(g * xhat, axis=-1, keepdims=True) * (1.0 / n)))


def _rms_halves(x, half1):
    x2 = x * x
    s0 = jnp.sum(jnp.where(half1, 0.0, x2), axis=-1, keepdims=True)
    s1 = jnp.sum(jnp.where(half1, x2, 0.0), axis=-1, keepdims=True)
    r = jnp.where(half1, lax.rsqrt(s1 * (1.0 / 64) + RMS_EPS), lax.rsqrt(s0 * (1.0 / 64) + RMS_EPS))
    return x * r, r


def _rms_halves_bwd(dy, xhat, r, w, half1):
    g = dy * w
    t = g * xhat
    m0 = jnp.sum(jnp.where(half1, 0.0, t), axis=-1, keepdims=True) * (1.0 / 64)
    m1 = jnp.sum(jnp.where(half1, t, 0.0), axis=-1, keepdims=True) * (1.0 / 64)
    return r * (g - xhat * jnp.where(half1, m1, m0))


def _sigmoid(x):
    return 1.0 / (1.0 + jnp.exp(-x))


def _rope(x, c, s1, s2):
    ax = x.ndim - 1
    return x * c + pltpu.roll(x, 112, ax) * s1 + pltpu.roll(x, 16, ax) * s2


def _rope_bwd(dy, c, s1, s2):
    ax = dy.ndim - 1
    return dy * c + pltpu.roll(dy * s1, 16, ax) + pltpu.roll(dy * s2, 112, ax)


def _fold_rows8(x):
    return jnp.sum(x.reshape(x.shape[0] // 8, 8, x.shape[1]), axis=0)


def _row0(v, rows=8):
    row = lax.broadcasted_iota(jnp.int32, (rows, v.shape[1]), 0)
    return jnp.where(row == 0, jnp.broadcast_to(v, (rows, v.shape[1])), 0.0)


def _mm_nn(a, b, name, out_dtype=F32, residual=None, tm=TM_PROJ):
    M, K = a.shape
    N = b.shape[1]
    tm = min(tm, M)

    def body(*refs):
        if residual is None:
            a_ref, b_ref, o_ref = refs
            acc = _dot(a_ref[...].astype(MXU_DTYPE), b_ref[...])
        else:
            a_ref, b_ref, r_ref, o_ref = refs
            acc = _dot(a_ref[...].astype(MXU_DTYPE), b_ref[...]) + r_ref[...]
        o_ref[...] = acc.astype(out_dtype)

    in_specs = [pl.BlockSpec((tm, K), lambda i: (i, 0)), pl.BlockSpec((K, N), lambda i: (0, 0))]
    args = [a, b]
    if residual is not None:
        in_specs.append(pl.BlockSpec((tm, N), lambda i: (i, 0)))
        args.append(residual)
    return pl.pallas_call(
        body, name=name, grid=(M // tm,), in_specs=in_specs,
        out_specs=pl.BlockSpec((tm, N), lambda i: (i, 0)), out_shape=_sds((M, N), out_dtype),
        compiler_params=_cp(("parallel",), 48))(*args)


def _mm_tn(a, b, name, out_dtype, tn, tk=512):
    T, M = a.shape
    N = b.shape[1]
    tk = min(tk, T)
    nk = T // tk

    def body(a_ref, b_ref, o_ref, acc_ref):
        k = pl.program_id(1)

        @pl.when(k == 0)
        def _():
            acc_ref[...] = jnp.zeros_like(acc_ref)

        acc_ref[...] += _dot_tn(a_ref[...].astype(MXU_DTYPE), b_ref[...].astype(MXU_DTYPE))

        @pl.when(k == nk - 1)
        def _():
            o_ref[...] = acc_ref[...].astype(out_dtype)

    return pl.pallas_call(
        body, name=name, grid=(N // tn, nk),
        in_specs=[pl.BlockSpec((tk, M), lambda n, k: (k, 0)), pl.BlockSpec((tk, tn), lambda n, k: (k, n))],
        out_specs=pl.BlockSpec((M, tn), lambda n, k: (0, n)), out_shape=_sds((M, N), out_dtype),
        scratch_shapes=[pltpu.VMEM((M, tn), F32)],
        compiler_params=_cp(("parallel", "arbitrary"), 48))(a, b)


def _inproj_fwd(x, ng, wp):
    T, D = x.shape
    tm = min(TM_PROJ, T)

    def body(x_ref, g_ref, w_ref, proj_ref, h_ref):
        xhat, _ = _rms(x_ref[...], D)
        h = (xhat * g_ref[...]).astype(MXU_DTYPE)
        h_ref[...] = h
        proj_ref[...] = _dot(h, w_ref[...])

    return pl.pallas_call(
        body, name="inproj_fwd", grid=(T // tm,),
        in_specs=[pl.BlockSpec((tm, D), lambda i: (i, 0)), pl.BlockSpec((1, D), lambda i: (0, 0)),
                  pl.BlockSpec((D, NP), lambda i: (0, 0))],
        out_specs=[pl.BlockSpec((tm, NP), lambda i: (i, 0)), pl.BlockSpec((tm, D), lambda i: (i, 0))],
        out_shape=[_sds((T, NP), F32), _sds((T, D), MXU_DTYPE)],
        compiler_params=_cp(("parallel",), 48))(x, ng, wp)


def _mla_prep_fwd(proj, lw, rope):
    T = proj.shape[0]
    tm = min(TK, T // 2)

    def body(ql_ref, kvl_ref, kr_ref, qa_ref, kva_ref, wq_ref, wk_ref, wv_ref, qn_ref, kn_ref,
             c_ref, s1_ref, s2_ref, q_out, k_out, kt_out, v_out, vt_out):
        c, s1, s2 = c_ref[...], s1_ref[...], s2_ref[...]
        qhat, _ = _rms(ql_ref[...], MLA_Q_LORA)
        qn = (qhat * qa_ref[...]).astype(MXU_DTYPE)
        khat, _ = _rms(kvl_ref[...], MLA_KV_LORA)
        kvn = (khat * kva_ref[...]).astype(MXU_DTYPE)
        kr = kr_ref[...]
        half1 = lax.broadcasted_iota(jnp.int32, (tm, LANES), 1) >= 64
        q3, _ = _rms(jnp.stack([_dot(qn, wq_ref[h]) for h in range(HEADS)]), MLA_QK)
        q_out[...] = (_rope(q3 * qn_ref[...], c, s1, s2) * (MLA_SCALE * LOG2E)).astype(MXU_DTYPE)
        k3, _ = _rms(jnp.stack([_dot(kvn, wk_ref[h]) for h in range(HEADS)]) + kr, MLA_QK)
        k3 = _rope(k3 * kn_ref[...], c, s1, s2)
        k_out[...] = k3.astype(MXU_DTYPE)
        for h in range(HEADS):
            kt_out[h, 0] = k3[h].T.astype(MXU_DTYPE)
        v = _dot(kvn, wv_ref[...])
        for h in range(HEADS):
            vp = v[:, LANES * (h // 2):LANES * (h // 2 + 1)]
            own = half1 if h % 2 else jnp.logical_not(half1)
            vp = jnp.where(own, vp, 0.0)
            v_out[h] = vp.astype(MXU_DTYPE)
            vt_out[h, 0] = vp.T.astype(MXU_DTYPE)

    full = lambda shape: pl.BlockSpec(shape, lambda i: (0,) * len(shape))
    hd = pl.BlockSpec((HEADS, tm, LANES), lambda i: (0, i, 0))
    hdt = pl.BlockSpec((HEADS, 1, LANES, tm), lambda i: (0, i, 0, 0))
    nat = _sds((HEADS, T, LANES), MXU_DTYPE)
    tr = _sds((HEADS, T // tm, LANES, tm), MXU_DTYPE)
    return pl.pallas_call(
        body, name="mla_prep_fwd", grid=(T // tm,),
        in_specs=[pl.BlockSpec((tm, 256), lambda i: (i, CB_QLAT)), pl.BlockSpec((tm, LANES), lambda i: (i, CB_KVLAT)),
                  pl.BlockSpec((tm, LANES), lambda i: (i, CB_KROPE)),
                  full((1, 256)), full((1, LANES)), full((HEADS, 256, LANES)), full((HEADS, LANES, LANES)),
                  full((LANES, 512)), full((1, LANES)), full((1, LANES)),
                  pl.BlockSpec((tm, LANES), lambda i: (i, 0)), pl.BlockSpec((tm, LANES), lambda i: (i, 0)),
                  pl.BlockSpec((tm, LANES), lambda i: (i, 0))],
        out_specs=[hd, hd, hdt, hd, hdt],
        out_shape=[nat, nat, tr, nat, tr],
        compiler_params=_cp(("parallel",), 32))(
            proj, proj, proj, lw["qa"], lw["kva"], lw["wq"], lw["wk"], lw["wv"], lw["qn"], lw["kn"],
            rope[0], rope[1], rope[2])


def _mla_attn_fwd(q, k, vt):
    T = q.shape[1]
    tk = min(TK, T // 2)
    tq = 2 * tk

    def body(q_ref, k_ref, vt_ref, o_ref, lse_ref, acc_s, m_s, l_s, s_a, s_b):
        i = pl.program_id(1)
        key = lax.broadcasted_iota(jnp.int32, (tk, tq), 0)
        qry = lax.broadcasted_iota(jnp.int32, (tk, tq), 1)
        qs = [q_ref[0], q_ref[1]]
        acc_s[...] = jnp.zeros_like(acc_s)
        l_s[...] = jnp.zeros_like(l_s)
        m_s[...] = jnp.full(m_s.shape, NEG_INF, F32)

        def scores(kj, buf):
            rows = pl.ds(pl.multiple_of(kj * tk, tk), tk)
            for r in range(2):
                buf[r] = _dot_nt(k_ref[r, rows, :], qs[r])

        def consume(kj, buf, diag):
            for r in range(2):
                s = buf[r]
                if diag is not None:
                    s = jnp.where(key + diag * tk <= qry, s, NEG_INF)
                m_old = m_s[r]
                m_new = jnp.maximum(m_old, jnp.max(s, axis=0, keepdims=True))
                alpha = jnp.exp2(m_old - m_new)
                p = jnp.exp2(s - m_new)
                l_s[r] = alpha * l_s[r] + jnp.sum(p, axis=0, keepdims=True)
                m_s[r] = m_new
                acc_s[r] = alpha * acc_s[r] + _dot(vt_ref[r, kj], p.astype(MXU_DTYPE))

        scores(0, s_a)

        def pair(kj):
            scores(kj + 1, s_b)
            consume(kj, s_a, None)
            scores(kj + 2, s_a)
            consume(kj + 1, s_b, None)

        def quad(kq, carry):
            pair(4 * kq)
            pair(4 * kq + 2)
            return carry

        lax.fori_loop(0, i // 2, quad, 0)

        @pl.when(i % 2 == 1)
        def _():
            pair(2 * i - 2)

        scores(2 * i + 1, s_b)
        consume(2 * i, s_a, 0)
        consume(2 * i + 1, s_b, 1)
        o_t = acc_s[0] / l_s[0] + acc_s[1] / l_s[1]
        o_ref[...] = o_t.T
        for r in range(2):
            lse_ref[r] = m_s[r] + jnp.log2(l_s[r])

    return pl.pallas_call(
        body, name="mla_attn_fwd", grid=(HEADS // 2, T // tq),
        in_specs=[pl.BlockSpec((2, tq, LANES), lambda j, i: (j, i, 0)),
                  pl.BlockSpec((2, T, LANES), lambda j, i: (j, 0, 0)),
                  pl.BlockSpec((2, T // tk, LANES, tk), lambda j, i: (j, 0, 0, 0))],
        out_specs=[pl.BlockSpec((tq, LANES), lambda j, i: (i, j)),
                   pl.BlockSpec((2, 1, tq), lambda j, i: (j, 0, i))],
        out_shape=[_sds((T, GROUP_WIDTH), F32), _sds((HEADS, 1, T), F32)],
        scratch_shapes=[pltpu.VMEM((2, LANES, tq), F32), pltpu.VMEM((2, 1, tq), F32), pltpu.VMEM((2, 1, tq), F32),
                        pltpu.VMEM((2, tk, tq), F32), pltpu.VMEM((2, tk, tq), F32)],
        compiler_params=_cp(("parallel", "arbitrary"), 40))(q, k, vt)


def _swa_kv_variants(x, half1):
    xs = pltpu.roll(x, 64, 1)
    out = {}
    for g in range(2):
        for r in range(2):
            own = half1 if r else jnp.logical_not(half1)
            out[(g, r)] = jnp.where(own, x if g == r else xs, 0.0).astype(MXU_DTYPE)
    return out


def _swa_alibi():
    qi = np.arange(BLOCK)[:, None]
    ki = np.arange(2 * BLOCK)[None, :]
    dist = BLOCK + qi - ki
    slopes = 2.0 ** -(np.arange(HEADS) + 1.0)
    tab = np.where(((dist >= 0) & (dist < BLOCK))[None], slopes[:, None, None] * dist[None], 1e30)
    return jnp.asarray(tab, F32)


def _swa_probs(i, nb, q_ref, k_ref, v_ref, pk_ref, pv_ref, qw_ref, kw_ref, alibi_ref, sink_ref):
    scale = SWA_HEAD_DIM ** -0.5
    half1 = lax.broadcasted_iota(jnp.int32, (1, LANES), 1) >= 64
    k_all = jnp.concatenate([pk_ref[...], k_ref[...]], axis=0)
    v_all = jnp.concatenate([pv_ref[...], v_ref[...]], axis=0)
    khat, _ = _rms_halves(k_all, half1)
    kp = _swa_kv_variants(khat * kw_ref[...], half1)
    vp = _swa_kv_variants(v_all, half1)
    qhat, qr, qn = [], [], []
    for j in range(4):
        xh, r = _rms_halves(q_ref[:, LANES * j:LANES * (j + 1)], half1)
        qhat.append(xh)
        qr.append(r)
        qn.append((xh * qw_ref[...]).astype(MXU_DTYPE))
    ki = lax.broadcasted_iota(jnp.int32, (1, 2 * BLOCK), 1)
    first = jnp.where((i == 0) & (ki < BLOCK), NEG_INF, 0.0)
    s = jnp.stack([_dot_nt(qn[h // 2][BLOCK * b:BLOCK * (b + 1)], kp[(h // 4, h % 2)][BLOCK * b:BLOCK * (b + 2)])
                   for b in range(nb) for h in range(HEADS)]) * scale - alibi_ref[...]
    s = jnp.concatenate([s[:HEADS] + first, s[HEADS:]], axis=0) if nb > 1 else s + first
    sink = jnp.stack([jnp.full((1, 1), sink_ref[h], F32) for _ in range(nb) for h in range(HEADS)])
    m = jnp.maximum(jnp.max(s, axis=-1, keepdims=True), sink)
    e = jnp.exp(s - m)
    es = jnp.exp(sink - m)
    inv = 1.0 / (jnp.sum(e, axis=-1, keepdims=True) + es)
    return e * inv, es * inv, dict(half1=half1, kp=kp, vp=vp, qhat=qhat, qr=qr, qn=qn)


def _swa_fwd(proj, lw):
    T = proj.shape[0]
    tm = min(TM_SWA, T)
    nb = tm // BLOCK

    def body(q_ref, k_ref, v_ref, pk_ref, pv_ref, qw_ref, kw_ref, alibi_ref, sink_ref, o_ref):
        p, _, c = _swa_probs(pl.program_id(0), nb, q_ref, k_ref, v_ref, pk_ref, pv_ref, qw_ref, kw_ref, alibi_ref,
                             sink_ref)
        p = p.astype(MXU_DTYPE)
        for b in range(nb):
            ks = slice(BLOCK * b, BLOCK * (b + 2))
            for j in range(4):
                o_ref[BLOCK * b:BLOCK * (b + 1), LANES * j:LANES * (j + 1)] = (
                    _dot(p[HEADS * b + 2 * j], c["vp"][(j // 2, 0)][ks])
                    + _dot(p[HEADS * b + 2 * j + 1], c["vp"][(j // 2, 1)][ks]))

    prev = lambda cb: pl.BlockSpec((BLOCK, LANES), lambda i: (jnp.maximum(i * nb - 1, 0), cb))
    return pl.pallas_call(
        body, name="swa_fwd", grid=(T // tm,),
        in_specs=[pl.BlockSpec((tm, 512), lambda i: (i, CB_SQ)), pl.BlockSpec((tm, LANES), lambda i: (i, CB_SK)),
                  pl.BlockSpec((tm, LANES), lambda i: (i, CB_SV)), prev(CB_SK), prev(CB_SV),
                  pl.BlockSpec((1, LANES), lambda i: (0, 0)), pl.BlockSpec((1, LANES), lambda i: (0, 0)),
                  pl.BlockSpec((nb * HEADS, BLOCK, 2 * BLOCK), lambda i: (0, 0, 0)),
                  pl.BlockSpec(memory_space=pltpu.SMEM)],
        out_specs=pl.BlockSpec((tm, 512), lambda i: (i, 0)),
        out_shape=_sds((T, GROUP_WIDTH), F32),
        compiler_params=_cp(("parallel",), 40))(
            proj, proj, proj, proj, proj, lw["sqn"], lw["skn"], jnp.tile(_swa_alibi(), (nb, 1, 1)), lw["sinks"])


def _shift_down(u, prev, n, row):
    tm = u.shape[0]
    out = pltpu.roll(u, n, 0)
    row8 = lax.broadcasted_iota(jnp.int32, prev.shape, 0)
    for t in range(n):
        src = jnp.sum(jnp.where(row8 == 8 - n + t, prev, 0.0), axis=0, keepdims=True)
        out = jnp.where(row == t, src, out)
    return out


def _shift_up(u, nxt, n, row):
    tm = u.shape[0]
    out = pltpu.roll(u, tm - n, 0)
    row8 = lax.broadcasted_iota(jnp.int32, nxt.shape, 0)
    for t in range(n):
        src = jnp.sum(jnp.where(row8 == t, nxt, 0.0), axis=0, keepdims=True)
        out = jnp.where(row == tm - n + t, src, out)
    return out


def _mix_fwd(proj, o_mla, o_swa, conv_w):
    T = proj.shape[0]
    tm = min(TM_ROW, T)

    def body(gm_ref, ch_ref, cb_ref, cc_ref, gc_ref, gs_ref, pch_ref, pcc_ref, om_ref, os_ref, w_ref, y_ref):
        i = pl.program_id(0)
        row = lax.broadcasted_iota(jnp.int32, (tm, GROUP_WIDTH), 0)
        u = cc_ref[...] * ch_ref[...]
        u_prev = jnp.where(i > 0, pcc_ref[...] * pch_ref[...], 0.0)
        z = (w_ref[0:1, :] * _shift_down(u, u_prev, 2, row) + w_ref[1:2, :] * _shift_down(u, u_prev, 1, row)
             + w_ref[2:3, :] * u)
        gm, gc, gs = gm_ref[...], gc_ref[...], gs_ref[...]
        y_ref[:, 0:512] = (om_ref[...] * (gm * _sigmoid(gm))).astype(MXU_DTYPE)
        y_ref[:, 512:1024] = (cb_ref[...] * z * (gc * _sigmoid(gc))).astype(MXU_DTYPE)
        y_ref[:, 1024:1536] = (os_ref[...] * (gs * _sigmoid(gs))).astype(MXU_DTYPE)

    blk = lambda cb: pl.BlockSpec((tm, 512), lambda i: (i, cb))
    prev = lambda cb: pl.BlockSpec((8, 512), lambda i: (jnp.maximum(i * (tm // 8) - 1, 0), cb))
    tile = pl.BlockSpec((tm, 512), lambda i: (i, 0))
    return pl.pallas_call(
        body, name="mix_fwd", grid=(T // tm,),
        in_specs=[blk(CB_GMLA), blk(CB_CH), blk(CB_CB), blk(CB_CC), blk(CB_GCONV), blk(CB_GSWA),
                  prev(CB_CH), prev(CB_CC), tile, tile, pl.BlockSpec((8, 512), lambda i: (0, 0))],
        out_specs=pl.BlockSpec((tm, D_MIX), lambda i: (i, 0)),
        out_shape=_sds((T, D_MIX), MXU_DTYPE),
        compiler_params=_cp(("parallel",), 32))(
            proj, proj, proj, proj, proj, proj, proj, proj, o_mla, o_swa, conv_w)


def _loss_grad(y, target):
    T, D = y.shape
    tm = min(TM_ROW, T)
    nt = T // tm

    def body(y_ref, t_ref, g_ref, loss_ref, acc_ref):
        i = pl.program_id(0)

        @pl.when(i == 0)
        def _():
            acc_ref[...] = jnp.zeros_like(acc_ref)

        err = y_ref[...] - t_ref[...]
        g_ref[...] = err * (1.0 / D)
        acc_ref[...] += _fold_rows8(err * err)

        @pl.when(i == nt - 1)
        def _():
            tot = jnp.sum(jnp.sum(acc_ref[...], axis=1, keepdims=True), axis=0, keepdims=True)
            loss_ref[...] = jnp.broadcast_to(tot * (0.5 / D), (8, LANES))

    return pl.pallas_call(
        body, name="loss_grad", grid=(nt,),
        in_specs=[pl.BlockSpec((tm, D), lambda i: (i, 0)), pl.BlockSpec((tm, D), lambda i: (i, 0))],
        out_specs=[pl.BlockSpec((tm, D), lambda i: (i, 0)), pl.BlockSpec((8, LANES), lambda i: (0, 0))],
        out_shape=[_sds((T, D), F32), _sds((8, LANES), F32)],
        scratch_shapes=[pltpu.VMEM((8, D), F32)],
        compiler_params=_cp(("arbitrary",), 32))(y, target)


def _mix_bwd(dycat, proj, o_mla, o_swa, conv_w):
    T = proj.shape[0]
    tm = min(TM_ROW, T)
    nt = T // tm

    def body(dym_ref, dyc_ref, dys_ref, gm_ref, ch_ref, cb_ref, cc_ref, gc_ref, gs_ref, pch_ref, pcc_ref,
             ndy_ref, ncb_ref, ngc_ref, om_ref, os_ref, w_ref,
             d1_ref, dgs_ref, dom_ref, dos_ref, dw_ref):
        i = pl.program_id(0)

        @pl.when(i == 0)
        def _():
            dw_ref[...] = jnp.zeros_like(dw_ref)

        row = lax.broadcasted_iota(jnp.int32, (tm, GROUP_WIDTH), 0)

        def gate(g):
            sg = _sigmoid(g)
            return g * sg, sg * (1.0 + g * (1.0 - sg))

        gm = gm_ref[...]
        silu, dsilu = gate(gm)
        dym = dym_ref[...]
        dom_ref[...] = dym * silu
        d1_ref[:, 0:512] = (dym * om_ref[...] * dsilu).astype(MXU_DTYPE)

        gs = gs_ref[...]
        silu, dsilu = gate(gs)
        dys = dys_ref[...]
        dos_ref[...] = dys * silu
        dgs_ref[...] = (dys * os_ref[...] * dsilu).astype(MXU_DTYPE)

        ch, cb, cc, gc, dyc = ch_ref[...], cb_ref[...], cc_ref[...], gc_ref[...], dyc_ref[...]
        w0, w1, w2 = w_ref[0:1, :], w_ref[1:2, :], w_ref[2:3, :]
        u = cc * ch
        u_prev = jnp.where(i > 0, pcc_ref[...] * pch_ref[...], 0.0)
        u1 = _shift_down(u, u_prev, 1, row)
        u2 = _shift_down(u, u_prev, 2, row)
        z = w0 * u2 + w1 * u1 + w2 * u
        silu, dsilu = gate(gc)
        dz = dyc * cb * silu
        ngc = ngc_ref[...]
        dz_next = jnp.where(i < nt - 1, ndy_ref[...] * ncb_ref[...] * (ngc * _sigmoid(ngc)), 0.0)
        du = w2 * dz + w1 * _shift_up(dz, dz_next, 1, row) + w0 * _shift_up(dz, dz_next, 2, row)
        d1_ref[:, 512:1024] = (du * cc).astype(MXU_DTYPE)
        d1_ref[:, 1024:1536] = (dyc * z * silu).astype(MXU_DTYPE)
        d1_ref[:, 1536:2048] = (du * ch).astype(MXU_DTYPE)
        d1_ref[:, 2048:2560] = (dyc * cb * z * dsilu).astype(MXU_DTYPE)
        row8 = lax.broadcasted_iota(jnp.int32, (8, GROUP_WIDTH), 0)
        dw = jnp.zeros((8, GROUP_WIDTH), F32)
        for t, shifted in enumerate((u2, u1, u)):
            dw = dw + jnp.where(row8 == t, jnp.sum(dz * shifted, axis=0, keepdims=True), 0.0)
        dw_ref[...] += dw

    blk = lambda cb: pl.BlockSpec((tm, 512), lambda i: (i, cb))
    prev = lambda cb: pl.BlockSpec((8, 512), lambda i: (jnp.maximum(i * (tm // 8) - 1, 0), cb))
    nxt = lambda cb: pl.BlockSpec((8, 512), lambda i: (jnp.minimum((i + 1) * (tm // 8), T // 8 - 1), cb))
    tile = pl.BlockSpec((tm, 512), lambda i: (i, 0))
    return pl.pallas_call(
        body, name="mix_bwd", grid=(nt,),
        in_specs=[blk(0), blk(1), blk(2), blk(CB_GMLA), blk(CB_CH), blk(CB_CB), blk(CB_CC), blk(CB_GCONV),
                  blk(CB_GSWA), prev(CB_CH), prev(CB_CC), nxt(1), nxt(CB_CB), nxt(CB_GCONV), tile, tile,
                  pl.BlockSpec((8, 512), lambda i: (0, 0))],
        out_specs=[pl.BlockSpec((tm, 2560), lambda i: (i, 0)), tile, tile, tile,
                   pl.BlockSpec((8, 512), lambda i: (0, 0))],
        out_shape=[_sds((T, 2560), MXU_DTYPE), _sds((T, 512), MXU_DTYPE), _sds((T, 512), F32),
                   _sds((T, 512), F32), _sds((8, 512), F32)],
        compiler_params=_cp(("arbitrary",), 48))(
            dycat, dycat, dycat, proj, proj, proj, proj, proj, proj, proj, proj, dycat, proj, proj,
            o_mla, o_swa, conv_w)


def _swa_bwd(proj, o_swa, do_swa, lw):
    T = proj.shape[0]
    tm = min(TM_SWA, T)
    nb = tm // BLOCK
    scale = SWA_HEAD_DIM ** -0.5

    def body(q_ref, k_ref, v_ref, pk_ref, pv_ref, o_ref, do_ref, qw_ref, kw_ref, alibi_ref, sink_ref,
             dq_ref, dk_ref, dv_ref, dqw_ref, dsink_ref):
        i = pl.program_id(0)

        @pl.when(i == 0)
        def _():
            dk_ref[...] = jnp.zeros_like(dk_ref)
            dv_ref[...] = jnp.zeros_like(dv_ref)
            dqw_ref[...] = jnp.zeros_like(dqw_ref)
            dsink_ref[...] = jnp.zeros_like(dsink_ref)

        p, p_sink, c = _swa_probs(i, nb, q_ref, k_ref, v_ref, pk_ref, pv_ref, qw_ref, kw_ref, alibi_ref, sink_ref)
        half1, kp, vp, qn, qhat, qr = c["half1"], c["kp"], c["vp"], c["qn"], c["qhat"], c["qr"]
        qw = qw_ref[...]
        rows = [slice(BLOCK * b, BLOCK * (b + 1)) for b in range(nb)]
        keys = [slice(BLOCK * b, BLOCK * (b + 2)) for b in range(nb)]
        dob, dd0, dd1 = [], [], []
        for j in range(4):
            cols = slice(LANES * j, LANES * (j + 1))
            do = do_ref[:, cols]
            dob.append(do.astype(MXU_DTYPE))
            prod = do * o_ref[:, cols]
            dd0.append(jnp.sum(jnp.where(half1, 0.0, prod), axis=-1, keepdims=True))
            dd1.append(jnp.sum(jnp.where(half1, prod, 0.0), axis=-1, keepdims=True))
        dd = jnp.stack([(dd1 if h % 2 else dd0)[h // 2][rows[b]] for b in range(nb) for h in range(HEADS)])
        dp = jnp.stack([_dot_nt(dob[h // 2][rows[b]], vp[(h // 4, h % 2)][keys[b]])
                        for b in range(nb) for h in range(HEADS)])
        ds = (p * (dp - dd) * scale).astype(MXU_DTYPE)
        dsink = -jnp.sum(p_sink * dd, axis=1, keepdims=True)
        pb = p.astype(MXU_DTYPE)

        dqw = jnp.zeros((1, LANES), F32)
        for j in range(4):
            g = j // 2
            dqn = [_dot(ds[HEADS * b + 2 * j], kp[(g, 0)][keys[b]]) + _dot(ds[HEADS * b + 2 * j + 1], kp[(g, 1)][keys[b]])
                   for b in range(nb)]
            dqn = jnp.concatenate(dqn, axis=0) if nb > 1 else dqn[0]
            dqw = dqw + jnp.sum(dqn * qhat[j], axis=0, keepdims=True)
            dq_ref[:, LANES * j:LANES * (j + 1)] = _rms_halves_bwd(dqn, qhat[j], qr[j], qw, half1).astype(MXU_DTYPE)
        dqw_ref[...] += _row0(dqw + pltpu.roll(dqw, 64, 1))

        dk_tot = jnp.zeros((tm + BLOCK, LANES), F32)
        dv_tot = jnp.zeros((tm + BLOCK, LANES), F32)
        for b in range(nb):
            dk_b = jnp.zeros((2 * BLOCK, LANES), F32)
            dv_b = jnp.zeros((2 * BLOCK, LANES), F32)
            for g in range(2):
                for r in range(2):
                    own = half1 if r else jnp.logical_not(half1)
                    ha, hb = HEADS * b + 4 * g + r, HEADS * b + 4 * g + 2 + r
                    qa, qb = qn[2 * g][rows[b]], qn[2 * g + 1][rows[b]]
                    da, db = dob[2 * g][rows[b]], dob[2 * g + 1][rows[b]]
                    dkp = jnp.where(own, _dot_tn(ds[ha], qa) + _dot_tn(ds[hb], qb), 0.0)
                    dvp = jnp.where(own, _dot_tn(pb[ha], da) + _dot_tn(pb[hb], db), 0.0)
                    if g != r:
                        dkp = pltpu.roll(dkp, 64, 1)
                        dvp = pltpu.roll(dvp, 64, 1)
                    dk_b = dk_b + dkp
                    dv_b = dv_b + dvp
            pad = lambda x: jnp.concatenate(
                [z for z in (jnp.zeros((BLOCK * b, LANES), F32), x, jnp.zeros((BLOCK * (nb - 1 - b), LANES), F32))
                 if z.shape[0]], axis=0)
            dk_tot = dk_tot + pad(dk_b)
            dv_tot = dv_tot + pad(dv_b)
        dst = pl.ds(pl.multiple_of(i * tm, BLOCK), tm + BLOCK)
        dk_ref[dst, :] += dk_tot
        dv_ref[dst, :] += dv_tot

        row8 = lax.broadcasted_iota(jnp.int32, (8, LANES), 0)
        dsink_tile = jnp.zeros((8, LANES), F32)
        for b in range(nb):
            for h in range(HEADS):
                dsink_tile = dsink_tile + jnp.where(row8 == h, jnp.broadcast_to(dsink[HEADS * b + h], (8, LANES)), 0.0)
        dsink_ref[...] += dsink_tile

    prev = lambda cb: pl.BlockSpec((BLOCK, LANES), lambda i: (jnp.maximum(i * nb - 1, 0), cb))
    tile = pl.BlockSpec((tm, 512), lambda i: (i, 0))
    small = pl.BlockSpec((8, LANES), lambda i: (0, 0))
    acc = pl.BlockSpec((T + BLOCK, LANES), lambda i: (0, 0))
    return pl.pallas_call(
        body, name="swa_bwd", grid=(T // tm,),
        in_specs=[pl.BlockSpec((tm, 512), lambda i: (i, CB_SQ)), pl.BlockSpec((tm, LANES), lambda i: (i, CB_SK)),
                  pl.BlockSpec((tm, LANES), lambda i: (i, CB_SV)), prev(CB_SK), prev(CB_SV), tile, tile,
                  pl.BlockSpec((1, LANES), lambda i: (0, 0)), pl.BlockSpec((1, LANES), lambda i: (0, 0)),
                  pl.BlockSpec((nb * HEADS, BLOCK, 2 * BLOCK), lambda i: (0, 0, 0)),
                  pl.BlockSpec(memory_space=pltpu.SMEM)],
        out_specs=[tile, acc, acc, small, small],
        out_shape=[_sds((T, 512), MXU_DTYPE), _sds((T + BLOCK, LANES), F32), _sds((T + BLOCK, LANES), F32),
                   _sds((8, LANES), F32), _sds((8, LANES), F32)],
        compiler_params=_cp(("arbitrary",), 48))(
            proj, proj, proj, proj, proj, o_swa, do_swa, lw["sqn"], lw["skn"], jnp.tile(_swa_alibi(), (nb, 1, 1)),
            lw["sinks"])


def _swa_kv_bwd(proj, dkn, dv, lw):
    T = proj.shape[0]
    tm = BLOCK

    def body(k_ref, dkn_ref, dv_ref, kw_ref, d_ref, dkw_ref):
        i = pl.program_id(0)

        @pl.when(i == 0)
        def _():
            dkw_ref[...] = jnp.zeros_like(dkw_ref)

        half1 = lax.broadcasted_iota(jnp.int32, (1, LANES), 1) >= 64
        khat, kr = _rms_halves(k_ref[...], half1)
        dkn_t = dkn_ref[...]
        dkw = jnp.sum(dkn_t * khat, axis=0, keepdims=True)
        dkw_ref[...] += _row0(dkw + pltpu.roll(dkw, 64, 1))
        d_ref[:, 0:LANES] = _rms_halves_bwd(dkn_t, khat, kr, kw_ref[...], half1).astype(MXU_DTYPE)
        d_ref[:, LANES:2 * LANES] = dv_ref[...].astype(MXU_DTYPE)

    return pl.pallas_call(
        body, name="swa_kv_bwd", grid=(T // tm,),
        in_specs=[pl.BlockSpec((tm, LANES), lambda i: (i, CB_SK)), pl.BlockSpec((tm, LANES), lambda i: (i + 1, 0)),
                  pl.BlockSpec((tm, LANES), lambda i: (i + 1, 0)), pl.BlockSpec((1, LANES), lambda i: (0, 0))],
        out_specs=[pl.BlockSpec((tm, 2 * LANES), lambda i: (i, 0)), pl.BlockSpec((8, LANES), lambda i: (0, 0))],
        out_shape=[_sds((T, 2 * LANES), MXU_DTYPE), _sds((8, LANES), F32)],
        compiler_params=_cp(("arbitrary",), 32))(proj, dkn, dv, lw["skn"])


def _mla_attn_bwd(q, k, kt, vpad, o, do, lse):
    T = q.shape[1]
    tk = min(TK, T // 2)
    tq = 2 * tk

    def body(q_ref, k_ref, kt_ref, v_ref, o_ref, do_ref, lse_ref, dq_ref, dk_ref, dv_ref, dqt_s,
             s_a, s_b, p_a, p_b):
        h = pl.program_id(0)
        i = pl.program_id(1)

        @pl.when(i == 0)
        def _():
            dk_ref[...] = jnp.zeros_like(dk_ref)
            dv_ref[...] = jnp.zeros_like(dv_ref)

        key = lax.broadcasted_iota(jnp.int32, (tk, tq), 0)
        qry = lax.broadcasted_iota(jnp.int32, (tk, tq), 1)
        own = (lax.broadcasted_iota(jnp.int32, (1, LANES), 1) // 64) == (h % 2)
        own_rows = (lax.broadcasted_iota(jnp.int32, (LANES, 1), 0) // 64) == (h % 2)
        do_t = do_ref[...]
        dob = do_t.astype(MXU_DTYPE)
        prod_t = (do_t * o_ref[...]).T
        dd = jnp.sum(jnp.where(own_rows, prod_t, 0.0), axis=0, keepdims=True)
        qh = q_ref[0]
        lse_t = lse_ref[0]
        dqt_s[...] = jnp.zeros_like(dqt_s)

        def scores(kj, s_buf, p_buf):
            rows = pl.ds(pl.multiple_of(kj * tk, tk), tk)
            s_buf[...] = _dot_nt(k_ref[0, rows, :], qh)
            p_buf[...] = _dot_nt(v_ref[0, rows, :], dob)

        def consume(kj, s_buf, p_buf, diag):
            rows = pl.ds(pl.multiple_of(kj * tk, tk), tk)
            s = s_buf[...]
            if diag is not None:
                s = jnp.where(key + diag * tk <= qry, s, NEG_INF)
            p = jnp.exp2(s - lse_t)
            ds = (p * (p_buf[...] - dd)).astype(MXU_DTYPE)
            dqt_s[...] += _dot(kt_ref[0, kj], ds)
            dk_ref[0, rows, :] += _dot(ds, qh)
            dv_ref[0, rows, :] += jnp.where(own, _dot(p.astype(MXU_DTYPE), dob), 0.0)

        scores(0, s_a, p_a)

        def pair(kj):
            scores(kj + 1, s_b, p_b)
            consume(kj, s_a, p_a, None)
            scores(kj + 2, s_a, p_a)
            consume(kj + 1, s_b, p_b, None)

        def quad(kq, carry):
            pair(4 * kq)
            pair(4 * kq + 2)
            return carry

        lax.fori_loop(0, i // 2, quad, 0)

        @pl.when(i % 2 == 1)
        def _():
            pair(2 * i - 2)

        scores(2 * i + 1, s_b, p_b)
        consume(2 * i, s_a, p_a, 0)
        consume(2 * i + 1, s_b, p_b, 1)
        dq_ref[0] = dqt_s[...].T

    res = pl.BlockSpec((1, T, LANES), lambda h, i: (h, 0, 0))
    res_t = pl.BlockSpec((1, T // tk, LANES, tk), lambda h, i: (h, 0, 0, 0))
    buf = pltpu.VMEM((tk, tq), F32)
    return pl.pallas_call(
        body, name="mla_attn_bwd", grid=(HEADS, T // tq),
        in_specs=[pl.BlockSpec((1, tq, LANES), lambda h, i: (h, i, 0)), res, res_t, res,
                  pl.BlockSpec((tq, LANES), lambda h, i: (i, h // 2)),
                  pl.BlockSpec((tq, LANES), lambda h, i: (i, h // 2)),
                  pl.BlockSpec((1, 1, tq), lambda h, i: (h, 0, i))],
        out_specs=[pl.BlockSpec((1, tq, LANES), lambda h, i: (h, i, 0)), res, res],
        out_shape=[_sds((HEADS, T, LANES), F32)] * 3,
        scratch_shapes=[pltpu.VMEM((LANES, tq), F32), buf, buf, buf, buf],
        compiler_params=_cp(("parallel", "arbitrary"), 48))(q, k, kt, vpad, o, do, lse)


def _mla_prep_bwd(proj, dq, dk, dv, lw, rope):
    T = proj.shape[0]
    tm = min(TK, T // 2)

    def body(ql_ref, kvl_ref, kr_ref, dq_ref, dk_ref, dv_ref, qa_ref, kva_ref, wq_ref, wk_ref, wv_ref,
             wqt_ref, wkt_ref, wvt_ref, qn_ref, kn_ref, c_ref, s1_ref, s2_ref,
             d_ref, dwq_ref, dwk_ref, dwv_ref, dqa_ref, dkva_ref, dqn_ref, dkn_ref):
        i = pl.program_id(0)

        @pl.when(i == 0)
        def _():
            for ref in (dwq_ref, dwk_ref, dwv_ref, dqa_ref, dkva_ref, dqn_ref, dkn_ref):
                ref[...] = jnp.zeros_like(ref)

        c, s1, s2 = c_ref[...], s1_ref[...], s2_ref[...]
        lane = lax.broadcasted_iota(jnp.int32, (1, LANES), 1)
        qlhat, qlr = _rms(ql_ref[...], MLA_Q_LORA)
        qn = (qlhat * qa_ref[...]).astype(MXU_DTYPE)
        kvhat, kvr = _rms(kvl_ref[...], MLA_KV_LORA)
        kvn = (kvhat * kva_ref[...]).astype(MXU_DTYPE)
        kr = kr_ref[...]
        x3, r3 = _rms(jnp.stack([_dot(qn, wq_ref[h]) for h in range(HEADS)]), MLA_QK)
        dy3 = _rope_bwd(dq_ref[...] * MLA_SCALE, c, s1, s2)
        dqw = jnp.sum(jnp.sum(dy3 * x3, axis=0), axis=0, keepdims=True)
        dx3 = _rms_bwd(dy3, x3, r3, qn_ref[...], MLA_QK).astype(MXU_DTYPE)
        dqnl = jnp.zeros((tm, MLA_Q_LORA), F32)
        for h in range(HEADS):
            dwq_ref[h] += _dot_tn(qn, dx3[h])
            dqnl = dqnl + _dot(dx3[h], wqt_ref[h])

        x3, r3 = _rms(jnp.stack([_dot(kvn, wk_ref[h]) for h in range(HEADS)]) + kr, MLA_QK)
        dy3 = _rope_bwd(dk_ref[...] * LN2, c, s1, s2)
        dkw = jnp.sum(jnp.sum(dy3 * x3, axis=0), axis=0, keepdims=True)
        dxf3 = _rms_bwd(dy3, x3, r3, kn_ref[...], MLA_QK)
        dkr = jnp.sum(dxf3, axis=0)
        dx3 = dxf3.astype(MXU_DTYPE)
        dkvn = jnp.zeros((tm, MLA_KV_LORA), F32)
        for h in range(HEADS):
            dwk_ref[h] += _dot_tn(kvn, dx3[h])
            dkvn = dkvn + _dot(dx3[h], wkt_ref[h])
        dvc = jnp.concatenate([dv_ref[2 * j] + dv_ref[2 * j + 1] for j in range(4)], axis=1).astype(MXU_DTYPE)
        dwv_ref[...] += _dot_tn(kvn, dvc)
        dkvn = dkvn + _dot(dvc, wvt_ref[...])
        dqa_ref[...] += _row0(jnp.sum(dqnl * qlhat, axis=0, keepdims=True))
        dkva_ref[...] += _row0(jnp.sum(dkvn * kvhat, axis=0, keepdims=True))
        dqn_ref[...] += _row0(dqw)
        dkn_ref[...] += _row0(dkw)
        d_ref[:, 0:256] = _rms_bwd(dqnl, qlhat, qlr, qa_ref[...], MLA_Q_LORA).astype(MXU_DTYPE)
        d_ref[:, 256:384] = _rms_bwd(dkvn, kvhat, kvr, kva_ref[...], MLA_KV_LORA).astype(MXU_DTYPE)
        d_ref[:, 384:512] = jnp.where((lane >= 64) & (lane < 96), dkr, 0.0).astype(MXU_DTYPE)

    full = lambda shape: pl.BlockSpec(shape, lambda i: (0,) * len(shape))
    hd = pl.BlockSpec((HEADS, tm, LANES), lambda i: (0, i, 0))
    tab = pl.BlockSpec((tm, LANES), lambda i: (i, 0))
    return pl.pallas_call(
        body, name="mla_prep_bwd", grid=(T // tm,),
        in_specs=[pl.BlockSpec((tm, 256), lambda i: (i, CB_QLAT)), pl.BlockSpec((tm, LANES), lambda i: (i, CB_KVLAT)),
                  pl.BlockSpec((tm, LANES), lambda i: (i, CB_KROPE)), hd, hd, hd,
                  full((1, 256)), full((1, LANES)), full((HEADS, 256, LANES)), full((HEADS, LANES, LANES)),
                  full((LANES, 512)), full((HEADS, LANES, 256)), full((HEADS, LANES, LANES)), full((512, LANES)),
                  full((1, LANES)), full((1, LANES)), tab, tab, tab],
        out_specs=[pl.BlockSpec((tm, 512), lambda i: (i, 0)), full((HEADS, 256, LANES)),
                   full((HEADS, LANES, LANES)), full((LANES, 512)), full((8, 256)), full((8, LANES)),
                   full((8, LANES)), full((8, LANES))],
        out_shape=[_sds((T, 512), MXU_DTYPE), _sds((HEADS, 256, LANES), F32), _sds((HEADS, LANES, LANES), F32),
                   _sds((LANES, 512), F32), _sds((8, 256), F32), _sds((8, LANES), F32), _sds((8, LANES), F32),
                   _sds((8, LANES), F32)],
        compiler_params=_cp(("arbitrary",), 48))(
            proj, proj, proj, dq, dk, dv, lw["qa"], lw["kva"], lw["wq"], lw["wk"], lw["wv"],
            lw["wqt"], lw["wkt"], lw["wvt"], lw["qn"], lw["kn"], rope[0], rope[1], rope[2])


def _norm_bwd(dh, x, g_in, ng):
    T, D = x.shape
    tm = min(TM_ROW, T)

    def body(dh_ref, x_ref, g_ref, w_ref, dx_ref, dw_ref):
        i = pl.program_id(0)

        @pl.when(i == 0)
        def _():
            dw_ref[...] = jnp.zeros_like(dw_ref)

        xhat, r = _rms(x_ref[...], D)
        dh_t = dh_ref[...]
        dw_ref[...] += _row0(jnp.sum(dh_t * xhat, axis=0, keepdims=True))
        dx_ref[...] = g_ref[...] + _rms_bwd(dh_t, xhat, r, w_ref[...], D)

    tile = pl.BlockSpec((tm, D), lambda i: (i, 0))
    return pl.pallas_call(
        body, name="norm_bwd", grid=(T // tm,),
        in_specs=[tile, tile, tile, pl.BlockSpec((1, D), lambda i: (0, 0))],
        out_specs=[tile, pl.BlockSpec((8, D), lambda i: (0, 0))],
        out_shape=[_sds((T, D), F32), _sds((8, D), F32)],
        compiler_params=_cp(("arbitrary",), 32))(dh, x, g_in, ng)


def _rope_tables(T):
    half = MLA_ROPE // 2
    inv_freq = jnp.power(jnp.float32(ROPE_THETA), -jnp.arange(half, dtype=F32) / half)
    ang = jnp.arange(T, dtype=F32)[:, None] * inv_freq[None, :]
    cos, sin = jnp.cos(ang), jnp.sin(ang)
    z = lambda n: jnp.zeros((T, n), F32)
    c = jnp.concatenate([jnp.ones((T, MLA_NOPE), F32), cos, cos, z(32)], axis=1)
    s1 = jnp.concatenate([z(64), -sin, z(48)], axis=1)
    s2 = jnp.concatenate([z(80), sin, z(32)], axis=1)
    return c, s1, s2


def _pad_lanes(v, n=LANES):
    v = v.reshape(1, -1)
    return jnp.pad(v, ((0, 0), (0, n - v.shape[1])))


def _pack_win(w):
    z = lambda n: jnp.zeros((w.shape[0], n), w.dtype)
    return jnp.concatenate([w[:, 0:384], z(64), w[:, 384:416], z(32), w[:, 416:2976], w[:, 2976:3488],
                            w[:, 3744:4256], w[:, 3488:3616], w[:, 3616:3744]], axis=1)


def _unpack_dwin(d):
    return jnp.concatenate([d[:, 0:384], d[:, 448:480], d[:, 512:3072], d[:, 3072:3584], d[:, 4096:4224],
                            d[:, 4224:4352], d[:, 3584:4096]], axis=1)


def _inproj_weights(l, norm_g, w_in_full):
    wp = _pack_win(w_in_full)
    return dict(ng=norm_g[l].reshape(1, -1), wp=wp, wpt=wp.T)


def _mixer_weights(l, qa, wqb_full, kva, wkvb_full, qn, kn, conv_full, sqn, skn, sinks, w_out_full):
    wq = jnp.pad(wqb_full, ((0, 0), (0, 0), (0, LANES - MLA_QK)))
    wk = jnp.pad(wkvb_full[:, :, :MLA_NOPE], ((0, 0), (0, 0), (0, LANES - MLA_NOPE)))
    wv = jnp.transpose(wkvb_full[:, :, MLA_NOPE:], (1, 0, 2)).reshape(MLA_KV_LORA, GROUP_WIDTH)
    return dict(
        qa=qa[l].reshape(1, -1), kva=kva[l].reshape(1, -1),
        wq=wq, wk=wk, wv=wv, wqt=jnp.transpose(wq, (0, 2, 1)), wkt=jnp.transpose(wk, (0, 2, 1)), wvt=wv.T,
        qn=_pad_lanes(qn[l]), kn=_pad_lanes(kn[l]),
        conv=jnp.pad(conv_full, ((0, 5), (0, 0))),
        sqn=jnp.tile(sqn[l].reshape(1, -1), (1, 2)), skn=jnp.tile(skn[l].reshape(1, -1), (1, 2)),
        sinks=sinks[l], wo=w_out_full, wot=w_out_full.T)


def _layer_weights(l, norm_g, w_in_full, qa, wqb_full, kva, wkvb_full, qn, kn, conv_full, sqn, skn, sinks,
                   w_out_full):
    return dict(_inproj_weights(l, norm_g, w_in_full),
                **_mixer_weights(l, qa, wqb_full, kva, wkvb_full, qn, kn, conv_full, sqn, skn, sinks, w_out_full))


def _layer_fwd(x, lw, rope, late_weights=None):
    proj, h = _inproj_fwd(x, lw["ng"], lw["wp"])
    if late_weights is not None:
        lw = dict(lw, **late_weights(proj))
    q, k, kt, vpad, vt = _mla_prep_fwd(proj, lw, rope)
    o_mla, lse = _mla_attn_fwd(q, k, vt)
    o_swa = _swa_fwd(proj, lw)
    ycat = _mix_fwd(proj, o_mla, o_swa, lw["conv"])
    x_next = _mm_nn(ycat, lw["wo"], "outproj_fwd", residual=x)
    return x_next, dict(x=x, proj=proj, h=h, q=q, k=k, kt=kt, vpad=vpad, o_mla=o_mla, lse=lse, o_swa=o_swa, ycat=ycat,
                        lw=lw)


def _layer_bwd(g, sv, lw, rope, on_big_grads=None):
    proj = sv["proj"]
    dycat = _mm_nn(g, lw["wot"], "outproj_bwd_dy")
    d_wo = _mm_tn(sv["ycat"], g, "outproj_bwd_dw", WIRE_DTYPE, tn=D_MODEL)
    d1, dgs, do_mla, do_swa, d_conv = _mix_bwd(dycat, proj, sv["o_mla"], sv["o_swa"], lw["conv"])
    dsq, dkn_acc, dv_acc, d_sqn, d_sinks = _swa_bwd(proj, sv["o_swa"], do_swa, lw)
    dskv, d_skn = _swa_kv_bwd(proj, dkn_acc, dv_acc, lw)
    dq, dk, dv = _mla_attn_bwd(sv["q"], sv["k"], sv["kt"], sv["vpad"], sv["o_mla"], do_mla, sv["lse"])
    dmla, d_wq, d_wk, d_wv, d_qa, d_kva, d_qn, d_kn = _mla_prep_bwd(proj, dq, dk, dv, lw, rope)
    dproj = jnp.concatenate([dmla, d1, dsq, dgs, dskv], axis=1)
    d_wp = _mm_tn(sv["h"], dproj, "inproj_bwd_dw", WIRE_DTYPE, tn=NP // 2)
    grads = dict(
        w_in=_unpack_dwin(d_wp), w_out=d_wo,
        w_qb=d_wq[:, :, :MLA_QK],
        w_kvb=jnp.concatenate([d_wk[:, :, :MLA_NOPE],
                               jnp.transpose(d_wv.reshape(MLA_KV_LORA, HEADS, MLA_NOPE), (1, 0, 2))], axis=2))
    ng = lw["ng"] if on_big_grads is None else lw["ng"] + on_big_grads(grads)
    dh = _mm_nn(dproj, lw["wpt"], "inproj_bwd_dh")
    dx, d_ng = _norm_bwd(dh, sv["x"], g, ng)
    grads.update(
        conv=d_conv[0:3], norm_g=d_ng[0], qa=d_qa[0], kva=d_kva[0], qn=d_qn[0, :MLA_QK], kn=d_kn[0, :MLA_QK],
        sqn=d_sqn[0, :SWA_HEAD_DIM], skn=d_skn[0, :SWA_HEAD_DIM], sinks=d_sinks[:, 0])
    return dx, grads


def _local_step(x, target, lws, rope):
    saved = []
    for lw in lws:
        x, sv = _layer_fwd(x, lw, rope)
        saved.append(sv)
    g, loss_tile = _loss_grad(x, target)
    grads = [None] * len(lws)
    for l in reversed(range(len(lws))):
        g, grads[l] = _layer_bwd(g, saved[l], lws[l], rope)
    return loss_tile, g, grads


def _my_coords():
    return lax.axis_index("x"), lax.axis_index("y"), lax.axis_index("c")


def _peer(me, k):
    x, y, c = me
    return (1 - x if k & 4 else x, 1 - y if k & 2 else y, 1 - c if k & 1 else c)


def _lin(d):
    return 4 * d[0] + 2 * d[1] + d[2]


def _all_gather(shards):
    n = len(shards)

    def body(*refs):
        ins, outs = refs[:n], refs[n:2 * n]
        send_sems, recv_sems, local_sems = refs[2 * n:]
        me = _my_coords()
        my = _lin(me)
        local = [pltpu.make_async_copy(ins[a], outs[a].at[my], local_sems.at[a]) for a in range(n)]
        for cp in local:
            cp.start()
        sends = []
        for a in range(n):
            for k in range(1, N_DEV):
                cp = pltpu.make_async_remote_copy(
                    src_ref=ins[a], dst_ref=outs[a].at[my], send_sem=send_sems.at[a * 7 + k - 1],
                    recv_sem=recv_sems.at[a * 7 + k - 1], device_id=_peer(me, k),
                    device_id_type=pl.DeviceIdType.MESH)
                cp.start()
                sends.append(cp)
        for a in range(n):
            for k in range(1, N_DEV):
                src = _lin(_peer(me, k))
                pltpu.make_async_remote_copy(
                    src_ref=ins[a], dst_ref=outs[a].at[src], send_sem=send_sems.at[a * 7 + k - 1],
                    recv_sem=recv_sems.at[a * 7 + k - 1], device_id=_peer(me, k),
                    device_id_type=pl.DeviceIdType.MESH).wait_recv()
        for cp in sends:
            cp.wait_send()
        for cp in local:
            cp.wait()

    any_spec = pl.BlockSpec(memory_space=pl.ANY)
    return pl.pallas_call(
        body, name="weight_all_gather",
        in_specs=[any_spec] * n, out_specs=[any_spec] * n,
        out_shape=[_sds((N_DEV,) + s.shape, s.dtype) for s in shards],
        scratch_shapes=[pltpu.SemaphoreType.DMA((7 * n,)), pltpu.SemaphoreType.DMA((7 * n,)),
                        pltpu.SemaphoreType.DMA((n,))],
    )(*shards)


def _push_copies(ins, lands, send_sems, recv_sems, gather):
    me = _my_coords()
    my = _lin(me)
    out, inc = [], []
    for a in range(len(ins)):
        for k in range(1, N_DEV):
            peer = _peer(me, k)
            sems = dict(send_sem=send_sems.at[a * 7 + k - 1], recv_sem=recv_sems.at[a * 7 + k - 1],
                        device_id=peer, device_id_type=pl.DeviceIdType.MESH)
            src = ins[a] if gather else ins[a].at[_lin(peer)]
            out.append(pltpu.make_async_remote_copy(src_ref=src, dst_ref=lands[a].at[my], **sems))
            inc.append(pltpu.make_async_remote_copy(src_ref=src, dst_ref=lands[a].at[_lin(peer)], **sems))
    return out, inc


def _push_start(arrays, name, gather):
    n = len(arrays)
    land_shapes = [((N_DEV,) + a.shape) if gather else a.shape for a in arrays]

    def body(*refs):
        ins, lands = refs[:n], refs[n:2 * n]
        send_sems, recv_sems = refs[2 * n], refs[2 * n + 1]
        token = refs[-1]
        out, _ = _push_copies(ins, lands, send_sems, recv_sems, gather)
        for cp in out:
            cp.start()
        token[...] = jnp.zeros_like(token)

    hbm = pl.BlockSpec(memory_space=pltpu.HBM)
    sem = pl.BlockSpec(memory_space=pltpu.SEMAPHORE)
    res = pl.pallas_call(
        body, name=name,
        out_shape=(pltpu.SemaphoreType.DMA((7 * n,)), pltpu.SemaphoreType.DMA((7 * n,)),
                   *[pltpu.HBM(a.shape, a.dtype) for a in arrays],
                   *[pltpu.HBM(s, a.dtype) for s, a in zip(land_shapes, arrays)],
                   _sds((8, LANES), F32)),
        in_specs=(hbm,) * (2 * n),
        out_specs=(sem, sem) + (hbm,) * (2 * n) + (pl.BlockSpec(memory_space=pltpu.VMEM),),
        input_output_aliases={i: 2 + i for i in range(2 * n)},
        compiler_params=pltpu.CompilerParams(has_side_effects=pltpu.SideEffectType.DATAFLOW_SIDE_EFFECTING),
    )(*[pltpu.with_memory_space_constraint(a, pltpu.HBM) for a in arrays],
      *[pltpu.with_memory_space_constraint(lax.empty(s, a.dtype), pltpu.HBM) for s, a in zip(land_shapes, arrays)])
    return dict(send=res[0], recv=res[1], src=res[2:2 + n], land=res[2 + n:2 + 2 * n], token=res[-1][0, 0],
                gather=gather)


def _push_wait(handle, after, name):
    n = len(handle["src"])
    gather = handle["gather"]

    def body(*refs):
        ins, lands = refs[:n], refs[n:2 * n]
        send_sems, recv_sems = refs[2 * n], refs[2 * n + 1]
        out, inc = _push_copies(ins, lands, send_sems, recv_sems, gather)
        for cp in out:
            cp.wait_send()
        for cp in inc:
            cp.wait_recv()

    hbm = pl.BlockSpec(memory_space=pltpu.HBM)
    sem = pl.BlockSpec(memory_space=pltpu.SEMAPHORE)
    res = pl.pallas_call(
        body, name=name,
        out_shape=tuple(pltpu.HBM(a.shape, a.dtype) for a in (*handle["src"], *handle["land"])),
        in_specs=(hbm,) * (2 * n) + (sem, sem, pl.BlockSpec(memory_space=pl.ANY)),
        out_specs=(hbm,) * (2 * n),
        input_output_aliases={i: i for i in range(2 * n)},
        compiler_params=pltpu.CompilerParams(has_side_effects=pltpu.SideEffectType.DATAFLOW_SIDE_EFFECTING),
    )(*handle["src"], *handle["land"], handle["send"], handle["recv"], after)
    return res[n:]


def _small_all_reduce(v):
    R = v.shape[0]

    def body(v_ref, o_ref, buf, send_sems, recv_sems):
        me = _my_coords()
        my = _lin(me)
        sends = []
        for k in range(1, N_DEV):
            cp = pltpu.make_async_remote_copy(
                src_ref=v_ref, dst_ref=buf.at[my], send_sem=send_sems.at[k - 1], recv_sem=recv_sems.at[k - 1],
                device_id=_peer(me, k), device_id_type=pl.DeviceIdType.MESH)
            cp.start()
            sends.append(cp)
        buf[my] = v_ref[...]
        for k in range(1, N_DEV):
            pltpu.make_async_remote_copy(
                src_ref=v_ref, dst_ref=buf.at[_lin(_peer(me, k))], send_sem=send_sems.at[k - 1],
                recv_sem=recv_sems.at[k - 1], device_id=_peer(me, k),
                device_id_type=pl.DeviceIdType.MESH).wait_recv()
        for cp in sends:
            cp.wait_send()
        tot = buf[0]
        for d in range(1, N_DEV):
            tot = tot + buf[d]
        o_ref[...] = tot

    vm = pl.BlockSpec(memory_space=pltpu.VMEM)
    return pl.pallas_call(
        body, name="small_all_reduce", in_specs=[vm], out_specs=vm, out_shape=_sds(v.shape, F32),
        scratch_shapes=[pltpu.VMEM((N_DEV, R, LANES), F32), pltpu.SemaphoreType.DMA((7,)),
                        pltpu.SemaphoreType.DMA((7,))],
    )(v)


def _adamw_math(w, g, m, v):
    m = ADAM_B1 * m + (1.0 - ADAM_B1) * g
    v = ADAM_B2 * v + (1.0 - ADAM_B2) * (g * g)
    m_hat = m / (1.0 - ADAM_B1 ** ADAM_STEP)
    v_hat = v / (1.0 - ADAM_B2 ** ADAM_STEP)
    delta = -ADAM_LR * (m_hat / (jnp.sqrt(v_hat) + ADAM_EPS) + ADAM_WD * w)
    return delta, m, v


def _adamw(parts, w, m, v, name, tr):
    P, R, C = parts.shape
    tr = min(tr, R)

    def body(p_ref, w_ref, m_ref, v_ref, g_out, d_out, m_out, v_out):
        g = p_ref[0].astype(F32)
        for d in range(1, P):
            g = g + p_ref[d].astype(F32)
        delta, m_new, v_new = _adamw_math(w_ref[...], g, m_ref[...], v_ref[...])
        g_out[...] = g
        d_out[...] = delta
        m_out[...] = m_new
        v_out[...] = v_new

    tile = pl.BlockSpec((tr, C), lambda i: (i, 0))
    return pl.pallas_call(
        body, name=name, grid=(R // tr,),
        in_specs=[pl.BlockSpec((P, tr, C), lambda i: (0, i, 0)), tile, tile, tile],
        out_specs=[tile] * 4, out_shape=[_sds((R, C), F32)] * 4,
        compiler_params=_cp(("parallel",), 32))(parts, w, m, v)


SMALL = (("norm_g", D_MODEL), ("mla_q_a_norm", MLA_Q_LORA), ("mla_kv_a_norm", MLA_KV_LORA), ("mla_q_norm", MLA_QK),
         ("mla_k_norm", MLA_QK), ("swa_q_norm", SWA_HEAD_DIM), ("swa_k_norm", SWA_HEAD_DIM), ("swa_sinks", HEADS))
SMALL_GRAD_KEY = dict(norm_g="norm_g", mla_q_a_norm="qa", mla_kv_a_norm="kva", mla_q_norm="qn", mla_k_norm="kn",
                      swa_q_norm="sqn", swa_k_norm="skn", swa_sinks="sinks")
SMALL_ROWS = 32
CONV_ROWS = 24


def _pack_small(get):
    parts = []
    for l in range(DEPTH):
        for name, n in SMALL:
            v = get(name, l).reshape(-1)
            parts.append(jnp.pad(v, (0, (-n) % LANES)))
    return jnp.concatenate(parts).reshape(SMALL_ROWS, LANES)


def _unpack_small(packed):
    flat = packed.reshape(-1)
    out = {name: [] for name, _ in SMALL}
    off = 0
    for l in range(DEPTH):
        for name, n in SMALL:
            out[name].append(flat[off:off + n])
            off += n + (-n) % LANES
    return {name: jnp.stack(v) for name, v in out.items()}


def kernel(x, norm_g, w_in, mla_q_a_norm, mla_w_qb, mla_kv_a_norm, mla_w_kvb, mla_q_norm, mla_k_norm, conv_w, swa_q_norm, swa_k_norm, swa_sinks, w_out, loss_target, m_norm_g, m_w_in, m_mla_q_a_norm, m_mla_w_qb, m_mla_kv_a_norm, m_mla_w_kvb, m_mla_q_norm, m_mla_k_norm, m_conv_w, m_swa_q_norm, m_swa_k_norm, m_swa_sinks, m_w_out, v_norm_g, v_w_in, v_mla_q_a_norm, v_mla_w_qb, v_mla_kv_a_norm, v_mla_w_kvb, v_mla_q_norm, v_mla_k_norm, v_conv_w, v_swa_q_norm, v_swa_k_norm, v_swa_sinks, v_w_out):
    T = x.shape[1]
    weights = dict(norm_g=norm_g, w_in=w_in, mla_q_a_norm=mla_q_a_norm, mla_w_qb=mla_w_qb,
                   mla_kv_a_norm=mla_kv_a_norm, mla_w_kvb=mla_w_kvb, mla_q_norm=mla_q_norm, mla_k_norm=mla_k_norm,
                   conv_w=conv_w, swa_q_norm=swa_q_norm, swa_k_norm=swa_k_norm, swa_sinks=swa_sinks, w_out=w_out)
    mom_m = dict(norm_g=m_norm_g, w_in=m_w_in, mla_q_a_norm=m_mla_q_a_norm, mla_w_qb=m_mla_w_qb,
                 mla_kv_a_norm=m_mla_kv_a_norm, mla_w_kvb=m_mla_w_kvb, mla_q_norm=m_mla_q_norm,
                 mla_k_norm=m_mla_k_norm, conv_w=m_conv_w, swa_q_norm=m_swa_q_norm, swa_k_norm=m_swa_k_norm,
                 swa_sinks=m_swa_sinks, w_out=m_w_out)
    mom_v = dict(norm_g=v_norm_g, w_in=v_w_in, mla_q_a_norm=v_mla_q_a_norm, mla_w_qb=v_mla_w_qb,
                 mla_kv_a_norm=v_mla_kv_a_norm, mla_w_kvb=v_mla_w_kvb, mla_q_norm=v_mla_q_norm,
                 mla_k_norm=v_mla_k_norm, conv_w=v_conv_w, swa_q_norm=v_swa_q_norm, swa_k_norm=v_swa_k_norm,
                 swa_sinks=v_swa_sinks, w_out=v_w_out)

    my = _lin(_my_coords())
    rope = _rope_tables(T)

    def shards(l):
        return [w_in[l].astype(MXU_DTYPE), mla_w_qb[l].astype(MXU_DTYPE), mla_w_kvb[l].astype(MXU_DTYPE),
                w_out[l].astype(MXU_DTYPE), conv_w[l]]

    def inproj_weights(l, g_win):
        return _inproj_weights(l, norm_g, jnp.transpose(g_win, (1, 0, 2)).reshape(D_MODEL, IN_COLS))

    def mixer_weights(l, gathered):
        g_wqb, g_wkvb, g_wout, g_conv = gathered
        return _mixer_weights(
            l, mla_q_a_norm, g_wqb, mla_kv_a_norm, g_wkvb, mla_q_norm, mla_k_norm,
            jnp.transpose(g_conv, (1, 0, 2)).reshape(3, GROUP_WIDTH), swa_q_norm, swa_k_norm, swa_sinks,
            g_wout.reshape(D_MIX, D_MODEL))

    def slots(g):
        return [jnp.transpose(g["w_in"].reshape(D_MODEL, N_DEV, IN_COLS // N_DEV), (1, 0, 2)),
                g["w_out"].reshape(N_DEV, D_MIX // N_DEV, D_MODEL), g["w_qb"], g["w_kvb"]]

    def own_slot(landed, mine):
        return [lax.dynamic_update_index_in_dim(a, m, my, 0) for a, m in zip(landed, mine)]

    def landed(handle, after, name, mine):
        return own_slot(_push_wait(handle, after, name), mine)

    lw0 = inproj_weights(0, _all_gather(shards(0)[:1])[0])
    gather0 = _push_start(shards(0)[1:], "weight_gather0_start", gather=True)
    gather1 = _push_start(shards(1), "weight_gather1_start", gather=True)
    x1, sv0 = _layer_fwd(
        x[0], dict(lw0, ng=lw0["ng"] + (gather0["token"] + gather1["token"])), rope,
        late_weights=lambda proj: mixer_weights(0, landed(gather0, proj, "weight_gather0_wait", shards(0)[1:])))
    g1_all = landed(gather1, x1, "weight_gather1_wait", shards(1))
    x2, sv1 = _layer_fwd(x1, dict(inproj_weights(1, g1_all[0]), **mixer_weights(1, g1_all[1:])), rope)
    g2, loss_tile = _loss_grad(x2, loss_target[0])

    started = {}

    def start_exchange(l, big):
        sl = slots(big)
        started[l] = (sl, _push_start(sl, "grad_exchange%d_start" % l, gather=False))
        return started[l][1]["token"]

    def received(l, after):
        sl, handle = started[l]
        return landed(handle, after, "grad_exchange%d_wait" % l, [s[my] for s in sl])

    g1, grads1 = _layer_bwd(g2, sv1, sv1["lw"], rope, on_big_grads=lambda big: start_exchange(1, big))
    lw0b = dict(sv0["lw"], conv=sv0["lw"]["conv"] + started[1][1]["token"])
    grad_x, grads0 = _layer_bwd(g1, sv0, lw0b, rope, on_big_grads=lambda big: start_exchange(0, big))
    recv1 = received(1, grad_x)
    recv0 = received(0, grad_x)
    grads = [grads0, grads1]
    r_win, r_wout, r_wqb, r_wkvb = [jnp.stack([a, b], axis=1) for a, b in zip(recv0, recv1)]

    small = jnp.concatenate([
        _pack_small(lambda name, l: grads[l][SMALL_GRAD_KEY[name]]),
        jnp.stack([g["conv"] for g in grads]).reshape(CONV_ROWS, LANES),
        loss_tile], axis=0)
    small = _small_all_reduce(small)
    loss = small[SMALL_ROWS + CONV_ROWS, 0]
    my = _lin(_my_coords())
    conv_g = lax.dynamic_slice_in_dim(small[SMALL_ROWS:SMALL_ROWS + CONV_ROWS].reshape(DEPTH, 3, GROUP_WIDTH),
                                      my * 64, 64, axis=2)

    out = {}

    def big(name, recv, rows, cols, tr):
        res = _adamw(recv.reshape(N_DEV, rows, cols), weights[name].reshape(rows, cols),
                     mom_m[name].reshape(rows, cols), mom_v[name].reshape(rows, cols), "adamw_" + name, tr)
        out[name] = [r.reshape(weights[name].shape) for r in res]

    big("w_in", r_win, DEPTH * D_MODEL, IN_COLS // N_DEV, 256)
    big("w_out", r_wout, DEPTH * D_MIX // N_DEV, D_MODEL, 192)
    big("mla_w_qb", r_wqb, DEPTH * MLA_Q_LORA, MLA_QK, 512)
    big("mla_w_kvb", r_wkvb, DEPTH * MLA_KV_LORA, 128, 256)

    pad_conv = lambda a: jnp.pad(a.reshape(-1), (0, 8 * LANES - 6 * 64)).reshape(8, LANES)
    cat = lambda src: jnp.concatenate([_pack_small(lambda name, l: src[name][l]), pad_conv(src["conv_w"])], axis=0)
    g_small = jnp.concatenate([small[:SMALL_ROWS], pad_conv(conv_g)], axis=0)
    res = _adamw(g_small[None], cat(weights), cat(mom_m), cat(mom_v), "adamw_small", SMALL_ROWS + 8)
    smalls = [_unpack_small(r[:SMALL_ROWS]) for r in res]
    for name, _ in SMALL:
        out[name] = [s[name] for s in smalls]
    out["conv_w"] = [r[SMALL_ROWS:].reshape(-1)[:6 * 64].reshape(DEPTH, 3, 64) for r in res]

    order = ["norm_g", "w_in", "mla_q_a_norm", "mla_w_qb", "mla_kv_a_norm", "mla_w_kvb", "mla_q_norm", "mla_k_norm",
             "conv_w", "swa_q_norm", "swa_k_norm", "swa_sinks", "w_out"]
    result = [loss, grad_x[None]]
    for idx in range(4):
        result += [out[name][idx] for name in order]
    return tuple(result)
```

```python
import functools

import jax
import jax.numpy as jnp
import numpy as np
from jax import lax
from jax.experimental import pallas as pl
from jax.experimental.pallas import tpu as pltpu

F32 = jnp.float32
MXU_DTYPE = jnp.bfloat16
WIRE_DTYPE = jnp.bfloat16

N_DEV = 8
DEPTH = 2
D_MODEL = 1024
GROUP_WIDTH = 512
D_MIX = 3 * GROUP_WIDTH
BLOCK = 128
RMS_EPS = 1e-6
NEG_INF = -1e30
HEADS = 8
MLA_QK = 96
MLA_NOPE = 64
MLA_ROPE = 32
MLA_Q_LORA = 256
MLA_KV_LORA = 128
ROPE_THETA = 10000.0
SWA_HEAD_DIM = 64
LANES = 128
IN_COLS = 4256

ADAM_LR = 0.001
ADAM_B1 = 0.9
ADAM_B2 = 0.999
ADAM_EPS = 1e-08
ADAM_WD = 0.01
ADAM_STEP = 10

NP = 4352
CB_QLAT = 0
CB_KVLAT = 2
CB_KROPE = 3
CB_GMLA, CB_CH, CB_CB, CB_CC, CB_GCONV, CB_SQ, CB_GSWA = 1, 2, 3, 4, 5, 6, 7
CB_SK, CB_SV = 32, 33

TM_PROJ = 256
TM_ROW = 256
TK = 256
TQ = 2 * TK
MLA_SCALE = MLA_QK ** -0.5
LOG2E = 1.4426950408889634
LN2 = 0.6931471805599453
TM_SWA = 512
VMEM_MB = 2 ** 20


def _cp(sem, vmem_mb):
    return pltpu.CompilerParams(dimension_semantics=sem, vmem_limit_bytes=vmem_mb * VMEM_MB)


def _sds(shape, dtype):
    return jax.ShapeDtypeStruct(shape, dtype)


def _dot(a, b):
    return jnp.dot(a, b, preferred_element_type=F32)


def _dot_nt(a, b):
    return lax.dot_general(a, b, (((1,), (1,)), ((), ())), preferred_element_type=F32)


def _dot_tn(a, b):
    return lax.dot_general(a, b, (((0,), (0,)), ((), ())), preferred_element_type=F32)


def _rms(x, n):
    r = lax.rsqrt(jnp.sum(x * x, axis=-1, keepdims=True) * (1.0 / n) + RMS_EPS)
    return x * r, r


def _rms_bwd(dy, xhat, r, w, n):
    g = dy * w
    return r * (g - xhat * (jnp.sum(g * xhat, axis=-1, keepdims=True) * (1.0 / n)))


def _rms_halves(x, half1):
    x2 = x * x
    s0 = jnp.sum(jnp.where(half1, 0.0, x2), axis=-1, keepdims=True)
    s1 = jnp.sum(jnp.where(half1, x2, 0.0), axis=-1, keepdims=True)
    r = jnp.where(half1, lax.rsqrt(s1 * (1.0 / 64) + RMS_EPS), lax.rsqrt(s0 * (1.0 / 64) + RMS_EPS))
    return x * r, r


def _rms_halves_bwd(dy, xhat, r, w, half1):
    g = dy * w
    t = g * xhat
    m0 = jnp.sum(jnp.where(half1, 0.0, t), axis=-1, keepdims=True) * (1.0 / 64)
    m1 = jnp.sum(jnp.where(half1, t, 0.0), axis=-1, keepdims=True) * (1.0 / 64)
    return r * (g - xhat * jnp.where(half1, m1, m0))


def _sigmoid(x):
    return 1.0 / (1.0 + jnp.exp(-x))


def _rope(x, c, s1, s2):
    ax = x.ndim - 1
    return x * c + pltpu.roll(x, 112, ax) * s1 + pltpu.roll(x, 16, ax) * s2


def _rope_bwd(dy, c, s1, s2):
    ax = dy.ndim - 1
    return dy * c + pltpu.roll(dy * s1, 16, ax) + pltpu.roll(dy * s2, 112, ax)


def _fold_rows8(x):
    return jnp.sum(x.reshape(x.shape[0] // 8, 8, x.shape[1]), axis=0)


def _row0(v, rows=8):
    row = lax.broadcasted_iota(jnp.int32, (rows, v.shape[1]), 0)
    return jnp.where(row == 0, jnp.broadcast_to(v, (rows, v.shape[1])), 0.0)


def _mm_nn(a, b, name, out_dtype=F32, residual=None, tm=TM_PROJ):
    M, K = a.shape
    N = b.shape[1]
    tm = min(tm, M)

    def body(*refs):
        if residual is None:
            a_ref, b_ref, o_ref = refs
            acc = _dot(a_ref[...].astype(MXU_DTYPE), b_ref[...])
        else:
            a_ref, b_ref, r_ref, o_ref = refs
            acc = _dot(a_ref[...].astype(MXU_DTYPE), b_ref[...]) + r_ref[...]
        o_ref[...] = acc.astype(out_dtype)

    in_specs = [pl.BlockSpec((tm, K), lambda i: (i, 0)), pl.BlockSpec((K, N), lambda i: (0, 0))]
    args = [a, b]
    if residual is not None:
        in_specs.append(pl.BlockSpec((tm, N), lambda i: (i, 0)))
        args.append(residual)
    return pl.pallas_call(
        body, name=name, grid=(M // tm,), in_specs=in_specs,
        out_specs=pl.BlockSpec((tm, N), lambda i: (i, 0)), out_shape=_sds((M, N), out_dtype),
        compiler_params=_cp(("parallel",), 48))(*args)


def _mm_tn(a, b, name, out_dtype, tn, tk=512):
    T, M = a.shape
    N = b.shape[1]
    tk = min(tk, T)
    nk = T // tk

    def body(a_ref, b_ref, o_ref, acc_ref):
        k = pl.program_id(1)

        @pl.when(k == 0)
        def _():
            acc_ref[...] = jnp.zeros_like(acc_ref)

        acc_ref[...] += _dot_tn(a_ref[...].astype(MXU_DTYPE), b_ref[...].astype(MXU_DTYPE))

        @pl.when(k == nk - 1)
        def _():
            o_ref[...] = acc_ref[...].astype(out_dtype)

    return pl.pallas_call(
        body, name=name, grid=(N // tn, nk),
        in_specs=[pl.BlockSpec((tk, M), lambda n, k: (k, 0)), pl.BlockSpec((tk, tn), lambda n, k: (k, n))],
        out_specs=pl.BlockSpec((M, tn), lambda n, k: (0, n)), out_shape=_sds((M, N), out_dtype),
        scratch_shapes=[pltpu.VMEM((M, tn), F32)],
        compiler_params=_cp(("parallel", "arbitrary"), 48))(a, b)


def _inproj_fwd(x, ng, wp):
    T, D = x.shape
    tm = min(TM_PROJ, T)

    def body(x_ref, g_ref, w_ref, proj_ref, h_ref):
        xhat, _ = _rms(x_ref[...], D)
        h = (xhat * g_ref[...]).astype(MXU_DTYPE)
        h_ref[...] = h
        proj_ref[...] = _dot(h, w_ref[...])

    return pl.pallas_call(
        body, name="inproj_fwd", grid=(T // tm,),
        in_specs=[pl.BlockSpec((tm, D), lambda i: (i, 0)), pl.BlockSpec((1, D), lambda i: (0, 0)),
                  pl.BlockSpec((D, NP), lambda i: (0, 0))],
        out_specs=[pl.BlockSpec((tm, NP), lambda i: (i, 0)), pl.BlockSpec((tm, D), lambda i: (i, 0))],
        out_shape=[_sds((T, NP), F32), _sds((T, D), MXU_DTYPE)],
        compiler_params=_cp(("parallel",), 48))(x, ng, wp)


def _mla_prep_fwd(proj, lw, rope):
    T = proj.shape[0]
    tk = min(TK, T // 2)
    nsub = 2
    tm = nsub * tk

    def body(ql_ref, kvl_ref, kr_ref, qa_ref, kva_ref, wq_ref, wk_ref, wv_ref, qn_ref, kn_ref,
             c_ref, s1_ref, s2_ref, q_out, k_out, kt_out, v_out, vt_out):
        c, s1, s2 = c_ref[...], s1_ref[...], s2_ref[...]
        qhat, _ = _rms(ql_ref[...], MLA_Q_LORA)
        qn = (qhat * qa_ref[...]).astype(MXU_DTYPE)
        khat, _ = _rms(kvl_ref[...], MLA_KV_LORA)
        kvn = (khat * kva_ref[...]).astype(MXU_DTYPE)
        kr = kr_ref[...]
        half1 = lax.broadcasted_iota(jnp.int32, (tm, LANES), 1) >= 64
        q3, _ = _rms(jnp.stack([_dot(qn, wq_ref[h]) for h in range(HEADS)]), MLA_QK)
        q_out[...] = (_rope(q3 * qn_ref[...], c, s1, s2) * (MLA_SCALE * LOG2E)).astype(MXU_DTYPE)
        k3, _ = _rms(jnp.stack([_dot(kvn, wk_ref[h]) for h in range(HEADS)]) + kr, MLA_QK)
        k3 = _rope(k3 * kn_ref[...], c, s1, s2)
        k_out[...] = k3.astype(MXU_DTYPE)
        for h in range(HEADS):
            for t in range(nsub):
                kt_out[h, t] = k3[h, tk * t:tk * (t + 1)].T.astype(MXU_DTYPE)
        v = _dot(kvn, wv_ref[...])
        for h in range(HEADS):
            vp = v[:, LANES * (h // 2):LANES * (h // 2 + 1)]
            own = half1 if h % 2 else jnp.logical_not(half1)
            vp = jnp.where(own, vp, 0.0)
            v_out[h] = vp.astype(MXU_DTYPE)
            for t in range(nsub):
                vt_out[h, t] = vp[tk * t:tk * (t + 1)].T.astype(MXU_DTYPE)

    full = lambda shape: pl.BlockSpec(shape, lambda i: (0,) * len(shape))
    hd = pl.BlockSpec((HEADS, tm, LANES), lambda i: (0, i, 0))
    hdt = pl.BlockSpec((HEADS, nsub, LANES, tk), lambda i: (0, i, 0, 0))
    nat = _sds((HEADS, T, LANES), MXU_DTYPE)
    tr = _sds((HEADS, T // tk, LANES, tk), MXU_DTYPE)
    return pl.pallas_call(
        body, name="mla_prep_fwd", grid=(T // tm,),
        in_specs=[pl.BlockSpec((tm, 256), lambda i: (i, CB_QLAT)), pl.BlockSpec((tm, LANES), lambda i: (i, CB_KVLAT)),
                  pl.BlockSpec((tm, LANES), lambda i: (i, CB_KROPE)),
                  full((1, 256)), full((1, LANES)), full((HEADS, 256, LANES)), full((HEADS, LANES, LANES)),
                  full((LANES, 512)), full((1, LANES)), full((1, LANES)),
                  pl.BlockSpec((tm, LANES), lambda i: (i, 0)), pl.BlockSpec((tm, LANES), lambda i: (i, 0)),
                  pl.BlockSpec((tm, LANES), lambda i: (i, 0))],
        out_specs=[hd, hd, hdt, hd, hdt],
        out_shape=[nat, nat, tr, nat, tr],
        compiler_params=_cp(("parallel",), 32))(
            proj, proj, proj, lw["qa"], lw["kva"], lw["wq"], lw["wk"], lw["wv"], lw["qn"], lw["kn"],
            rope[0], rope[1], rope[2])


def _mla_attn_fwd(q, k, vt):
    T = q.shape[1]
    tk = min(TK, T // 2)
    tq = 2 * tk

    def body(q_ref, k_ref, vt_ref, o_ref, lse_ref, acc_s, m_s, l_s, s_a, s_b):
        i = pl.program_id(1)
        key = lax.broadcasted_iota(jnp.int32, (tk, tq), 0)
        qry = lax.broadcasted_iota(jnp.int32, (tk, tq), 1)
        qs = [q_ref[0], q_ref[1]]
        acc_s[...] = jnp.zeros_like(acc_s)
        l_s[...] = jnp.zeros_like(l_s)
        m_s[...] = jnp.full(m_s.shape, NEG_INF, F32)

        def scores(kj, buf):
            rows = pl.ds(pl.multiple_of(kj * tk, tk), tk)
            for r in range(2):
                buf[r] = _dot_nt(k_ref[r, rows, :], qs[r])

        def consume(kj, buf, diag):
            for r in range(2):
                s = buf[r]
                if diag is not None:
                    s = jnp.where(key + diag * tk <= qry, s, NEG_INF)
                m_old = m_s[r]
                m_new = jnp.maximum(m_old, jnp.max(s, axis=0, keepdims=True))
                alpha = jnp.exp2(m_old - m_new)
                p = jnp.exp2(s - m_new)
                l_s[r] = alpha * l_s[r] + jnp.sum(p, axis=0, keepdims=True)
                m_s[r] = m_new
                acc_s[r] = alpha * acc_s[r] + _dot(vt_ref[r, kj], p.astype(MXU_DTYPE))

        scores(0, s_a)

        def pair(kj):
            scores(kj + 1, s_b)
            consume(kj, s_a, None)
            scores(kj + 2, s_a)
            consume(kj + 1, s_b, None)

        def quad(kq, carry):
            pair(4 * kq)
            pair(4 * kq + 2)
            return carry

        lax.fori_loop(0, i // 2, quad, 0)

        @pl.when(i % 2 == 1)
        def _():
            pair(2 * i - 2)

        scores(2 * i + 1, s_b)
        consume(2 * i, s_a, 0)
        consume(2 * i + 1, s_b, 1)
        o_t = acc_s[0] / l_s[0] + acc_s[1] / l_s[1]
        o_ref[...] = o_t.T
        for r in range(2):
            lse_ref[r] = m_s[r] + jnp.log2(l_s[r])

    return pl.pallas_call(
        body, name="mla_attn_fwd", grid=(HEADS // 2, T // tq),
        in_specs=[pl.BlockSpec((2, tq, LANES), lambda j, i: (j, i, 0)),
                  pl.BlockSpec((2, T, LANES), lambda j, i: (j, 0, 0)),
                  pl.BlockSpec((2, T // tk, LANES, tk), lambda j, i: (j, 0, 0, 0))],
        out_specs=[pl.BlockSpec((tq, LANES), lambda j, i: (i, j)),
                   pl.BlockSpec((2, 1, tq), lambda j, i: (j, 0, i))],
        out_shape=[_sds((T, GROUP_WIDTH), F32), _sds((HEADS, 1, T), F32)],
        scratch_shapes=[pltpu.VMEM((2, LANES, tq), F32), pltpu.VMEM((2, 1, tq), F32), pltpu.VMEM((2, 1, tq), F32),
                        pltpu.VMEM((2, tk, tq), F32), pltpu.VMEM((2, tk, tq), F32)],
        compiler_params=_cp(("parallel", "arbitrary"), 40))(q, k, vt)


def _swa_kv_variants(x, half1):
    xs = pltpu.roll(x, 64, 1)
    out = {}
    for g in range(2):
        for r in range(2):
            own = half1 if r else jnp.logical_not(half1)
            out[(g, r)] = jnp.where(own, x if g == r else xs, 0.0).astype(MXU_DTYPE)
    return out


def _swa_alibi():
    qi = np.arange(BLOCK)[:, None]
    ki = np.arange(2 * BLOCK)[None, :]
    dist = BLOCK + qi - ki
    slopes = 2.0 ** -(np.arange(HEADS) + 1.0)
    tab = np.where(((dist >= 0) & (dist < BLOCK))[None], slopes[:, None, None] * dist[None], 1e30)
    return jnp.asarray(tab, F32)


def _swa_probs(i, nb, q_ref, k_ref, v_ref, pk_ref, pv_ref, qw_ref, kw_ref, alibi_ref, sink_ref):
    scale = SWA_HEAD_DIM ** -0.5
    half1 = lax.broadcasted_iota(jnp.int32, (1, LANES), 1) >= 64
    k_all = jnp.concatenate([pk_ref[...], k_ref[...]], axis=0)
    v_all = jnp.concatenate([pv_ref[...], v_ref[...]], axis=0)
    khat, _ = _rms_halves(k_all, half1)
    kp = _swa_kv_variants(khat * kw_ref[...], half1)
    vp = _swa_kv_variants(v_all, half1)
    qhat, qr, qn = [], [], []
    for j in range(4):
        xh, r = _rms_halves(q_ref[:, LANES * j:LANES * (j + 1)], half1)
        qhat.append(xh)
        qr.append(r)
        qn.append((xh * qw_ref[...]).astype(MXU_DTYPE))
    ki = lax.broadcasted_iota(jnp.int32, (1, 2 * BLOCK), 1)
    first = jnp.where((i == 0) & (ki < BLOCK), NEG_INF, 0.0)
    s = jnp.stack([_dot_nt(qn[h // 2][BLOCK * b:BLOCK * (b + 1)], kp[(h // 4, h % 2)][BLOCK * b:BLOCK * (b + 2)])
                   for b in range(nb) for h in range(HEADS)]) * scale - alibi_ref[...]
    s = jnp.concatenate([s[:HEADS] + first, s[HEADS:]], axis=0) if nb > 1 else s + first
    sink = jnp.stack([jnp.full((1, 1), sink_ref[h], F32) for _ in range(nb) for h in range(HEADS)])
    m = jnp.maximum(jnp.max(s, axis=-1, keepdims=True), sink)
    e = jnp.exp(s - m)
    es = jnp.exp(sink - m)
    inv = 1.0 / (jnp.sum(e, axis=-1, keepdims=True) + es)
    return e * inv, es * inv, dict(half1=half1, kp=kp, vp=vp, qhat=qhat, qr=qr, qn=qn)


def _swa_fwd(proj, lw):
    T = proj.shape[0]
    tm = min(TM_SWA, T)
    nb = tm // BLOCK

    def body(q_ref, k_ref, v_ref, pk_ref, pv_ref, qw_ref, kw_ref, alibi_ref, sink_ref, o_ref):
        p, _, c = _swa_probs(pl.program_id(0), nb, q_ref, k_ref, v_ref, pk_ref, pv_ref, qw_ref, kw_ref, alibi_ref,
                             sink_ref)
        p = p.astype(MXU_DTYPE)
        for b in range(nb):
            ks = slice(BLOCK * b, BLOCK * (b + 2))
            for j in range(4):
                o_ref[BLOCK * b:BLOCK * (b + 1), LANES * j:LANES * (j + 1)] = (
                    _dot(p[HEADS * b + 2 * j], c["vp"][(j // 2, 0)][ks])
                    + _dot(p[HEADS * b + 2 * j + 1], c["vp"][(j // 2, 1)][ks]))

    prev = lambda cb: pl.BlockSpec((BLOCK, LANES), lambda i: (jnp.maximum(i * nb - 1, 0), cb))
    return pl.pallas_call(
        body, name="swa_fwd", grid=(T // tm,),
        in_specs=[pl.BlockSpec((tm, 512), lambda i: (i, CB_SQ)), pl.BlockSpec((tm, LANES), lambda i: (i, CB_SK)),
                  pl.BlockSpec((tm, LANES), lambda i: (i, CB_SV)), prev(CB_SK), prev(CB_SV),
                  pl.BlockSpec((1, LANES), lambda i: (0, 0)), pl.BlockSpec((1, LANES), lambda i: (0, 0)),
                  pl.BlockSpec((nb * HEADS, BLOCK, 2 * BLOCK), lambda i: (0, 0, 0)),
                  pl.BlockSpec(memory_space=pltpu.SMEM)],
        out_specs=pl.BlockSpec((tm, 512), lambda i: (i, 0)),
        out_shape=_sds((T, GROUP_WIDTH), F32),
        compiler_params=_cp(("parallel",), 40))(
            proj, proj, proj, proj, proj, lw["sqn"], lw["skn"], jnp.tile(_swa_alibi(), (nb, 1, 1)), lw["sinks"])


def _shift_down(u, prev, n, row):
    tm = u.shape[0]
    out = pltpu.roll(u, n, 0)
    row8 = lax.broadcasted_iota(jnp.int32, prev.shape, 0)
    for t in range(n):
        src = jnp.sum(jnp.where(row8 == 8 - n + t, prev, 0.0), axis=0, keepdims=True)
        out = jnp.where(row == t, src, out)
    return out


def _shift_up(u, nxt, n, row):
    tm = u.shape[0]
    out = pltpu.roll(u, tm - n, 0)
    row8 = lax.broadcasted_iota(jnp.int32, nxt.shape, 0)
    for t in range(n):
        src = jnp.sum(jnp.where(row8 == t, nxt, 0.0), axis=0, keepdims=True)
        out = jnp.where(row == tm - n + t, src, out)
    return out


def _mix_fwd(proj, o_mla, o_swa, conv_w):
    T = proj.shape[0]
    tm = min(TM_ROW, T)

    def body(gm_ref, ch_ref, cb_ref, cc_ref, gc_ref, gs_ref, pch_ref, pcc_ref, om_ref, os_ref, w_ref, y_ref):
        i = pl.program_id(0)
        row = lax.broadcasted_iota(jnp.int32, (tm, GROUP_WIDTH), 0)
        u = cc_ref[...] * ch_ref[...]
        u_prev = jnp.where(i > 0, pcc_ref[...] * pch_ref[...], 0.0)
        z = (w_ref[0:1, :] * _shift_down(u, u_prev, 2, row) + w_ref[1:2, :] * _shift_down(u, u_prev, 1, row)
             + w_ref[2:3, :] * u)
        gm, gc, gs = gm_ref[...], gc_ref[...], gs_ref[...]
        y_ref[:, 0:512] = (om_ref[...] * (gm * _sigmoid(gm))).astype(MXU_DTYPE)
        y_ref[:, 512:1024] = (cb_ref[...] * z * (gc * _sigmoid(gc))).astype(MXU_DTYPE)
        y_ref[:, 1024:1536] = (os_ref[...] * (gs * _sigmoid(gs))).astype(MXU_DTYPE)

    blk = lambda cb: pl.BlockSpec((tm, 512), lambda i: (i, cb))
    prev = lambda cb: pl.BlockSpec((8, 512), lambda i: (jnp.maximum(i * (tm // 8) - 1, 0), cb))
    tile = pl.BlockSpec((tm, 512), lambda i: (i, 0))
    return pl.pallas_call(
        body, name="mix_fwd", grid=(T // tm,),
        in_specs=[blk(CB_GMLA), blk(CB_CH), blk(CB_CB), blk(CB_CC), blk(CB_GCONV), blk(CB_GSWA),
                  prev(CB_CH), prev(CB_CC), tile, tile, pl.BlockSpec((8, 512), lambda i: (0, 0))],
        out_specs=pl.BlockSpec((tm, D_MIX), lambda i: (i, 0)),
        out_shape=_sds((T, D_MIX), MXU_DTYPE),
        compiler_params=_cp(("parallel",), 32))(
            proj, proj, proj, proj, proj, proj, proj, proj, o_mla, o_swa, conv_w)


def _loss_grad(y, target):
    T, D = y.shape
    tm = min(TM_ROW, T)
    nt = T // tm

    def body(y_ref, t_ref, g_ref, loss_ref, acc_ref):
        i = pl.program_id(0)

        @pl.when(i == 0)
        def _():
            acc_ref[...] = jnp.zeros_like(acc_ref)

        err = y_ref[...] - t_ref[...]
        g_ref[...] = err * (1.0 / D)
        acc_ref[...] += _fold_rows8(err * err)

        @pl.when(i == nt - 1)
        def _():
            tot = jnp.sum(jnp.sum(acc_ref[...], axis=1, keepdims=True), axis=0, keepdims=True)
            loss_ref[...] = jnp.broadcast_to(tot * (0.5 / D), (8, LANES))

    return pl.pallas_call(
        body, name="loss_grad", grid=(nt,),
        in_specs=[pl.BlockSpec((tm, D), lambda i: (i, 0)), pl.BlockSpec((tm, D), lambda i: (i, 0))],
        out_specs=[pl.BlockSpec((tm, D), lambda i: (i, 0)), pl.BlockSpec((8, LANES), lambda i: (0, 0))],
        out_shape=[_sds((T, D), F32), _sds((8, LANES), F32)],
        scratch_shapes=[pltpu.VMEM((8, D), F32)],
        compiler_params=_cp(("arbitrary",), 32))(y, target)


def _mix_bwd(dycat, proj, o_mla, o_swa, conv_w):
    T = proj.shape[0]
    tm = min(TM_ROW, T)
    nt = T // tm

    def body(dym_ref, dyc_ref, dys_ref, gm_ref, ch_ref, cb_ref, cc_ref, gc_ref, gs_ref, pch_ref, pcc_ref,
             ndy_ref, ncb_ref, ngc_ref, om_ref, os_ref, w_ref,
             d1_ref, dgs_ref, dom_ref, dos_ref, dw_ref):
        i = pl.program_id(0)

        @pl.when(i == 0)
        def _():
            dw_ref[...] = jnp.zeros_like(dw_ref)

        row = lax.broadcasted_iota(jnp.int32, (tm, GROUP_WIDTH), 0)

        def gate(g):
            sg = _sigmoid(g)
            return g * sg, sg * (1.0 + g * (1.0 - sg))

        gm = gm_ref[...]
        silu, dsilu = gate(gm)
        dym = dym_ref[...]
        dom_ref[...] = dym * silu
        d1_ref[:, 0:512] = (dym * om_ref[...] * dsilu).astype(MXU_DTYPE)

        gs = gs_ref[...]
        silu, dsilu = gate(gs)
        dys = dys_ref[...]
        dos_ref[...] = dys * silu
        dgs_ref[...] = (dys * os_ref[...] * dsilu).astype(MXU_DTYPE)

        ch, cb, cc, gc, dyc = ch_ref[...], cb_ref[...], cc_ref[...], gc_ref[...], dyc_ref[...]
        w0, w1, w2 = w_ref[0:1, :], w_ref[1:2, :], w_ref[2:3, :]
        u = cc * ch
        u_prev = jnp.where(i > 0, pcc_ref[...] * pch_ref[...], 0.0)
        u1 = _shift_down(u, u_prev, 1, row)
        u2 = _shift_down(u, u_prev, 2, row)
        z = w0 * u2 + w1 * u1 + w2 * u
        silu, dsilu = gate(gc)
        dz = dyc * cb * silu
        ngc = ngc_ref[...]
        dz_next = jnp.where(i < nt - 1, ndy_ref[...] * ncb_ref[...] * (ngc * _sigmoid(ngc)), 0.0)
        du = w2 * dz + w1 * _shift_up(dz, dz_next, 1, row) + w0 * _shift_up(dz, dz_next, 2, row)
        d1_ref[:, 512:1024] = (du * cc).astype(MXU_DTYPE)
        d1_ref[:, 1024:1536] = (dyc * z * silu).astype(MXU_DTYPE)
        d1_ref[:, 1536:2048] = (du * ch).astype(MXU_DTYPE)
        d1_ref[:, 2048:2560] = (dyc * cb * z * dsilu).astype(MXU_DTYPE)
        row8 = lax.broadcasted_iota(jnp.int32, (8, GROUP_WIDTH), 0)
        dw = jnp.zeros((8, GROUP_WIDTH), F32)
        for t, shifted in enumerate((u2, u1, u)):
            dw = dw + jnp.where(row8 == t, jnp.sum(dz * shifted, axis=0, keepdims=True), 0.0)
        dw_ref[...] += dw

    blk = lambda cb: pl.BlockSpec((tm, 512), lambda i: (i, cb))
    prev = lambda cb: pl.BlockSpec((8, 512), lambda i: (jnp.maximum(i * (tm // 8) - 1, 0), cb))
    nxt = lambda cb: pl.BlockSpec((8, 512), lambda i: (jnp.minimum((i + 1) * (tm // 8), T // 8 - 1), cb))
    tile = pl.BlockSpec((tm, 512), lambda i: (i, 0))
    return pl.pallas_call(
        body, name="mix_bwd", grid=(nt,),
        in_specs=[blk(0), blk(1), blk(2), blk(CB_GMLA), blk(CB_CH), blk(CB_CB), blk(CB_CC), blk(CB_GCONV),
                  blk(CB_GSWA), prev(CB_CH), prev(CB_CC), nxt(1), nxt(CB_CB), nxt(CB_GCONV), tile, tile,
                  pl.BlockSpec((8, 512), lambda i: (0, 0))],
        out_specs=[pl.BlockSpec((tm, 2560), lambda i: (i, 0)), tile, tile, tile,
                   pl.BlockSpec((8, 512), lambda i: (0, 0))],
        out_shape=[_sds((T, 2560), MXU_DTYPE), _sds((T, 512), MXU_DTYPE), _sds((T, 512), F32),
                   _sds((T, 512), F32), _sds((8, 512), F32)],
        compiler_params=_cp(("arbitrary",), 48))(
            dycat, dycat, dycat, proj, proj, proj, proj, proj, proj, proj, proj, dycat, proj, proj,
            o_mla, o_swa, conv_w)


def _swa_bwd(proj, o_swa, do_swa, lw):
    T = proj.shape[0]
    tm = min(TM_SWA, T)
    nb = tm // BLOCK
    scale = SWA_HEAD_DIM ** -0.5

    def body(q_ref, k_ref, v_ref, pk_ref, pv_ref, o_ref, do_ref, qw_ref, kw_ref, alibi_ref, sink_ref,
             dq_ref, dk_ref, dv_ref, dqw_ref, dsink_ref):
        i = pl.program_id(0)

        @pl.when(i == 0)
        def _():
            dk_ref[...] = jnp.zeros_like(dk_ref)
            dv_ref[...] = jnp.zeros_like(dv_ref)
            dqw_ref[...] = jnp.zeros_like(dqw_ref)
            dsink_ref[...] = jnp.zeros_like(dsink_ref)

        p, p_sink, c = _swa_probs(i, nb, q_ref, k_ref, v_ref, pk_ref, pv_ref, qw_ref, kw_ref, alibi_ref, sink_ref)
        half1, kp, vp, qn, qhat, qr = c["half1"], c["kp"], c["vp"], c["qn"], c["qhat"], c["qr"]
        qw = qw_ref[...]
        rows = [slice(BLOCK * b, BLOCK * (b + 1)) for b in range(nb)]
        keys = [slice(BLOCK * b, BLOCK * (b + 2)) for b in range(nb)]
        dob, dd0, dd1 = [], [], []
        for j in range(4):
            cols = slice(LANES * j, LANES * (j + 1))
            do = do_ref[:, cols]
            dob.append(do.astype(MXU_DTYPE))
            prod = do * o_ref[:, cols]
            dd0.append(jnp.sum(jnp.where(half1, 0.0, prod), axis=-1, keepdims=True))
            dd1.append(jnp.sum(jnp.where(half1, prod, 0.0), axis=-1, keepdims=True))
        dd = jnp.stack([(dd1 if h % 2 else dd0)[h // 2][rows[b]] for b in range(nb) for h in range(HEADS)])
        dp = jnp.stack([_dot_nt(dob[h // 2][rows[b]], vp[(h // 4, h % 2)][keys[b]])
                        for b in range(nb) for h in range(HEADS)])
        ds = (p * (dp - dd) * scale).astype(MXU_DTYPE)
        dsink = -jnp.sum(p_sink * dd, axis=1, keepdims=True)
        pb = p.astype(MXU_DTYPE)

        dqw = jnp.zeros((1, LANES), F32)
        for j in range(4):
            g = j // 2
            dqn = [_dot(ds[HEADS * b + 2 * j], kp[(g, 0)][keys[b]]) + _dot(ds[HEADS * b + 2 * j + 1], kp[(g, 1)][keys[b]])
                   for b in range(nb)]
            dqn = jnp.concatenate(dqn, axis=0) if nb > 1 else dqn[0]
            dqw = dqw + jnp.sum(dqn * qhat[j], axis=0, keepdims=True)
            dq_ref[:, LANES * j:LANES * (j + 1)] = _rms_halves_bwd(dqn, qhat[j], qr[j], qw, half1).astype(MXU_DTYPE)
        dqw_ref[...] += _row0(dqw + pltpu.roll(dqw, 64, 1))

        dk_tot = jnp.zeros((tm + BLOCK, LANES), F32)
        dv_tot = jnp.zeros((tm + BLOCK, LANES), F32)
        for b in range(nb):
            dk_b = jnp.zeros((2 * BLOCK, LANES), F32)
            dv_b = jnp.zeros((2 * BLOCK, LANES), F32)
            for g in range(2):
                for r in range(2):
                    own = half1 if r else jnp.logical_not(half1)
                    ha, hb = HEADS * b + 4 * g + r, HEADS * b + 4 * g + 2 + r
                    qa, qb = qn[2 * g][rows[b]], qn[2 * g + 1][rows[b]]
                    da, db = dob[2 * g][rows[b]], dob[2 * g + 1][rows[b]]
                    dkp = jnp.where(own, _dot_tn(ds[ha], qa) + _dot_tn(ds[hb], qb), 0.0)
                    dvp = jnp.where(own, _dot_tn(pb[ha], da) + _dot_tn(pb[hb], db), 0.0)
                    if g != r:
                        dkp = pltpu.roll(dkp, 64, 1)
                        dvp = pltpu.roll(dvp, 64, 1)
                    dk_b = dk_b + dkp
                    dv_b = dv_b + dvp
            pad = lambda x: jnp.concatenate(
                [z for z in (jnp.zeros((BLOCK * b, LANES), F32), x, jnp.zeros((BLOCK * (nb - 1 - b), LANES), F32))
                 if z.shape[0]], axis=0)
            dk_tot = dk_tot + pad(dk_b)
            dv_tot = dv_tot + pad(dv_b)
        dst = pl.ds(pl.multiple_of(i * tm, BLOCK), tm + BLOCK)
        dk_ref[dst, :] += dk_tot
        dv_ref[dst, :] += dv_tot

        row8 = lax.broadcasted_iota(jnp.int32, (8, LANES), 0)
        dsink_tile = jnp.zeros((8, LANES), F32)
        for b in range(nb):
            for h in range(HEADS):
                dsink_tile = dsink_tile + jnp.where(row8 == h, jnp.broadcast_to(dsink[HEADS * b + h], (8, LANES)), 0.0)
        dsink_ref[...] += dsink_tile

    prev = lambda cb: pl.BlockSpec((BLOCK, LANES), lambda i: (jnp.maximum(i * nb - 1, 0), cb))
    tile = pl.BlockSpec((tm, 512), lambda i: (i, 0))
    small = pl.BlockSpec((8, LANES), lambda i: (0, 0))
    acc = pl.BlockSpec((T + BLOCK, LANES), lambda i: (0, 0))
    return pl.pallas_call(
        body, name="swa_bwd", grid=(T // tm,),
        in_specs=[pl.BlockSpec((tm, 512), lambda i: (i, CB_SQ)), pl.BlockSpec((tm, LANES), lambda i: (i, CB_SK)),
                  pl.BlockSpec((tm, LANES), lambda i: (i, CB_SV)), prev(CB_SK), prev(CB_SV), tile, tile,
                  pl.BlockSpec((1, LANES), lambda i: (0, 0)), pl.BlockSpec((1, LANES), lambda i: (0, 0)),
                  pl.BlockSpec((nb * HEADS, BLOCK, 2 * BLOCK), lambda i: (0, 0, 0)),
                  pl.BlockSpec(memory_space=pltpu.SMEM)],
        out_specs=[tile, acc, acc, small, small],
        out_shape=[_sds((T, 512), MXU_DTYPE), _sds((T + BLOCK, LANES), F32), _sds((T + BLOCK, LANES), F32),
                   _sds((8, LANES), F32), _sds((8, LANES), F32)],
        compiler_params=_cp(("arbitrary",), 48))(
            proj, proj, proj, proj, proj, o_swa, do_swa, lw["sqn"], lw["skn"], jnp.tile(_swa_alibi(), (nb, 1, 1)),
            lw["sinks"])


def _swa_kv_bwd(proj, dkn, dv, lw):
    T = proj.shape[0]
    tm = BLOCK

    def body(k_ref, dkn_ref, dv_ref, kw_ref, d_ref, dkw_ref):
        i = pl.program_id(0)

        @pl.when(i == 0)
        def _():
            dkw_ref[...] = jnp.zeros_like(dkw_ref)

        half1 = lax.broadcasted_iota(jnp.int32, (1, LANES), 1) >= 64
        khat, kr = _rms_halves(k_ref[...], half1)
        dkn_t = dkn_ref[...]
        dkw = jnp.sum(dkn_t * khat, axis=0, keepdims=True)
        dkw_ref[...] += _row0(dkw + pltpu.roll(dkw, 64, 1))
        d_ref[:, 0:LANES] = _rms_halves_bwd(dkn_t, khat, kr, kw_ref[...], half1).astype(MXU_DTYPE)
        d_ref[:, LANES:2 * LANES] = dv_ref[...].astype(MXU_DTYPE)

    return pl.pallas_call(
        body, name="swa_kv_bwd", grid=(T // tm,),
        in_specs=[pl.BlockSpec((tm, LANES), lambda i: (i, CB_SK)), pl.BlockSpec((tm, LANES), lambda i: (i + 1, 0)),
                  pl.BlockSpec((tm, LANES), lambda i: (i + 1, 0)), pl.BlockSpec((1, LANES), lambda i: (0, 0))],
        out_specs=[pl.BlockSpec((tm, 2 * LANES), lambda i: (i, 0)), pl.BlockSpec((8, LANES), lambda i: (0, 0))],
        out_shape=[_sds((T, 2 * LANES), MXU_DTYPE), _sds((8, LANES), F32)],
        compiler_params=_cp(("arbitrary",), 32))(proj, dkn, dv, lw["skn"])


def _mla_attn_bwd(q, k, kt, vpad, o, do, lse):
    T = q.shape[1]
    tk = min(TK, T // 2)
    tq = 2 * tk

    def body(q_ref, k_ref, kt_ref, v_ref, o_ref, do_ref, lse_ref, dq_ref, dk_ref, dv_ref, dqt_s,
             s_a, s_b, p_a, p_b):
        h = pl.program_id(0)
        i = pl.program_id(1)

        @pl.when(i == 0)
        def _():
            dk_ref[...] = jnp.zeros_like(dk_ref)
            dv_ref[...] = jnp.zeros_like(dv_ref)

        key = lax.broadcasted_iota(jnp.int32, (tk, tq), 0)
        qry = lax.broadcasted_iota(jnp.int32, (tk, tq), 1)
        own = (lax.broadcasted_iota(jnp.int32, (1, LANES), 1) // 64) == (h % 2)
        own_rows = (lax.broadcasted_iota(jnp.int32, (LANES, 1), 0) // 64) == (h % 2)
        do_t = do_ref[...]
        dob = do_t.astype(MXU_DTYPE)
        prod_t = (do_t * o_ref[...]).T
        dd = jnp.sum(jnp.where(own_rows, prod_t, 0.0), axis=0, keepdims=True)
        qh = q_ref[0]
        lse_t = lse_ref[0]
        dqt_s[...] = jnp.zeros_like(dqt_s)

        def scores(kj, s_buf, p_buf):
            rows = pl.ds(pl.multiple_of(kj * tk, tk), tk)
            s_buf[...] = _dot_nt(k_ref[0, rows, :], qh)
            p_buf[...] = _dot_nt(v_ref[0, rows, :], dob)

        def consume(kj, s_buf, p_buf, diag):
            rows = pl.ds(pl.multiple_of(kj * tk, tk), tk)
            s = s_buf[...]
            if diag is not None:
                s = jnp.where(key + diag * tk <= qry, s, NEG_INF)
            p = jnp.exp2(s - lse_t)
            ds = (p * (p_buf[...] - dd)).astype(MXU_DTYPE)
            dqt_s[...] += _dot(kt_ref[0, kj], ds)
            dk_ref[0, rows, :] += _dot(ds, qh)
            dv_ref[0, rows, :] += jnp.where(own, _dot(p.astype(MXU_DTYPE), dob), 0.0)

        scores(0, s_a, p_a)

        def pair(kj):
            scores(kj + 1, s_b, p_b)
            consume(kj, s_a, p_a, None)
            scores(kj + 2, s_a, p_a)
            consume(kj + 1, s_b, p_b, None)

        def quad(kq, carry):
            pair(4 * kq)
            pair(4 * kq + 2)
            return carry

        lax.fori_loop(0, i // 2, quad, 0)

        @pl.when(i % 2 == 1)
        def _():
            pair(2 * i - 2)

        scores(2 * i + 1, s_b, p_b)
        consume(2 * i, s_a, p_a, 0)
        consume(2 * i + 1, s_b, p_b, 1)
        dq_ref[0] = dqt_s[...].T

    res = pl.BlockSpec((1, T, LANES), lambda h, i: (h, 0, 0))
    res_t = pl.BlockSpec((1, T // tk, LANES, tk), lambda h, i: (h, 0, 0, 0))
    buf = pltpu.VMEM((tk, tq), F32)
    return pl.pallas_call(
        body, name="mla_attn_bwd", grid=(HEADS, T // tq),
        in_specs=[pl.BlockSpec((1, tq, LANES), lambda h, i: (h, i, 0)), res, res_t, res,
                  pl.BlockSpec((tq, LANES), lambda h, i: (i, h // 2)),
                  pl.BlockSpec((tq, LANES), lambda h, i: (i, h // 2)),
                  pl.BlockSpec((1, 1, tq), lambda h, i: (h, 0, i))],
        out_specs=[pl.BlockSpec((1, tq, LANES), lambda h, i: (h, i, 0)), res, res],
        out_shape=[_sds((HEADS, T, LANES), F32)] * 3,
        scratch_shapes=[pltpu.VMEM((LANES, tq), F32), buf, buf, buf, buf],
        compiler_params=_cp(("parallel", "arbitrary"), 48))(q, k, kt, vpad, o, do, lse)


def _mla_prep_bwd(proj, dq, dk, dv, lw, rope):
    T = proj.shape[0]
    tm = min(TK, T // 2)

    def body(ql_ref, kvl_ref, kr_ref, dq_ref, dk_ref, dv_ref, qa_ref, kva_ref, wq_ref, wk_ref, wv_ref,
             wqt_ref, wkt_ref, wvt_ref, qn_ref, kn_ref, c_ref, s1_ref, s2_ref,
             d_ref, dwq_ref, dwk_ref, dwv_ref, dqa_ref, dkva_ref, dqn_ref, dkn_ref):
        i = pl.program_id(0)

        @pl.when(i == 0)
        def _():
            for ref in (dwq_ref, dwk_ref, dwv_ref, dqa_ref, dkva_ref, dqn_ref, dkn_ref):
                ref[...] = jnp.zeros_like(ref)

        c, s1, s2 = c_ref[...], s1_ref[...], s2_ref[...]
        lane = lax.broadcasted_iota(jnp.int32, (1, LANES), 1)
        qlhat, qlr = _rms(ql_ref[...], MLA_Q_LORA)
        qn = (qlhat * qa_ref[...]).astype(MXU_DTYPE)
        kvhat, kvr = _rms(kvl_ref[...], MLA_KV_LORA)
        kvn = (kvhat * kva_ref[...]).astype(MXU_DTYPE)
        kr = kr_ref[...]
        x3, r3 = _rms(jnp.stack([_dot(qn, wq_ref[h]) for h in range(HEADS)]), MLA_QK)
        dy3 = _rope_bwd(dq_ref[...] * MLA_SCALE, c, s1, s2)
        dqw = jnp.sum(jnp.sum(dy3 * x3, axis=0), axis=0, keepdims=True)
        dx3 = _rms_bwd(dy3, x3, r3, qn_ref[...], MLA_QK).astype(MXU_DTYPE)
        dqnl = jnp.zeros((tm, MLA_Q_LORA), F32)
        for h in range(HEADS):
            dwq_ref[h] += _dot_tn(qn, dx3[h])
            dqnl = dqnl + _dot(dx3[h], wqt_ref[h])

        x3, r3 = _rms(jnp.stack([_dot(kvn, wk_ref[h]) for h in range(HEADS)]) + kr, MLA_QK)
        dy3 = _rope_bwd(dk_ref[...] * LN2, c, s1, s2)
        dkw = jnp.sum(jnp.sum(dy3 * x3, axis=0), axis=0, keepdims=True)
        dxf3 = _rms_bwd(dy3, x3, r3, kn_ref[...], MLA_QK)
        dkr = jnp.sum(dxf3, axis=0)
        dx3 = dxf3.astype(MXU_DTYPE)
        dkvn = jnp.zeros((tm, MLA_KV_LORA), F32)
        for h in range(HEADS):
            dwk_ref[h] += _dot_tn(kvn, dx3[h])
            dkvn = dkvn + _dot(dx3[h], wkt_ref[h])
        dvc = jnp.concatenate([dv_ref[2 * j] + dv_ref[2 * j + 1] for j in range(4)], axis=1).astype(MXU_DTYPE)
        dwv_ref[...] += _dot_tn(kvn, dvc)
        dkvn = dkvn + _dot(dvc, wvt_ref[...])
        dqa_ref[...] += _row0(jnp.sum(dqnl * qlhat, axis=0, keepdims=True))
        dkva_ref[...] += _row0(jnp.sum(dkvn * kvhat, axis=0, keepdims=True))
        dqn_ref[...] += _row0(dqw)
        dkn_ref[...] += _row0(dkw)
        d_ref[:, 0:256] = _rms_bwd(dqnl, qlhat, qlr, qa_ref[...], MLA_Q_LORA).astype(MXU_DTYPE)
        d_ref[:, 256:384] = _rms_bwd(dkvn, kvhat, kvr, kva_ref[...], MLA_KV_LORA).astype(MXU_DTYPE)
        d_ref[:, 384:512] = jnp.where((lane >= 64) & (lane < 96), dkr, 0.0).astype(MXU_DTYPE)

    full = lambda shape: pl.BlockSpec(shape, lambda i: (0,) * len(shape))
    hd = pl.BlockSpec((HEADS, tm, LANES), lambda i: (0, i, 0))
    tab = pl.BlockSpec((tm, LANES), lambda i: (i, 0))
    return pl.pallas_call(
        body, name="mla_prep_bwd", grid=(T // tm,),
        in_specs=[pl.BlockSpec((tm, 256), lambda i: (i, CB_QLAT)), pl.BlockSpec((tm, LANES), lambda i: (i, CB_KVLAT)),
                  pl.BlockSpec((tm, LANES), lambda i: (i, CB_KROPE)), hd, hd, hd,
                  full((1, 256)), full((1, LANES)), full((HEADS, 256, LANES)), full((HEADS, LANES, LANES)),
                  full((LANES, 512)), full((HEADS, LANES, 256)), full((HEADS, LANES, LANES)), full((512, LANES)),
                  full((1, LANES)), full((1, LANES)), tab, tab, tab],
        out_specs=[pl.BlockSpec((tm, 512), lambda i: (i, 0)), full((HEADS, 256, LANES)),
                   full((HEADS, LANES, LANES)), full((LANES, 512)), full((8, 256)), full((8, LANES)),
                   full((8, LANES)), full((8, LANES))],
        out_shape=[_sds((T, 512), MXU_DTYPE), _sds((HEADS, 256, LANES), F32), _sds((HEADS, LANES, LANES), F32),
                   _sds((LANES, 512), F32), _sds((8, 256), F32), _sds((8, LANES), F32), _sds((8, LANES), F32),
                   _sds((8, LANES), F32)],
        compiler_params=_cp(("arbitrary",), 48))(
            proj, proj, proj, dq, dk, dv, lw["qa"], lw["kva"], lw["wq"], lw["wk"], lw["wv"],
            lw["wqt"], lw["wkt"], lw["wvt"], lw["qn"], lw["kn"], rope[0], rope[1], rope[2])


def _norm_bwd(dh, x, g_in, ng):
    T, D = x.shape
    tm = min(TM_ROW, T)

    def body(dh_ref, x_ref, g_ref, w_ref, dx_ref, dw_ref):
        i = pl.program_id(0)

        @pl.when(i == 0)
        def _():
            dw_ref[...] = jnp.zeros_like(dw_ref)

        xhat, r = _rms(x_ref[...], D)
        dh_t = dh_ref[...]
        dw_ref[...] += _row0(jnp.sum(dh_t * xhat, axis=0, keepdims=True))
        dx_ref[...] = g_ref[...] + _rms_bwd(dh_t, xhat, r, w_ref[...], D)

    tile = pl.BlockSpec((tm, D), lambda i: (i, 0))
    return pl.pallas_call(
        body, name="norm_bwd", grid=(T // tm,),
        in_specs=[tile, tile, tile, pl.BlockSpec((1, D), lambda i: (0, 0))],
        out_specs=[tile, pl.BlockSpec((8, D), lambda i: (0, 0))],
        out_shape=[_sds((T, D), F32), _sds((8, D), F32)],
        compiler_params=_cp(("arbitrary",), 32))(dh, x, g_in, ng)


def _rope_tables(T, token=0.0):
    half = MLA_ROPE // 2
    inv_freq = jnp.power(jnp.float32(ROPE_THETA), -jnp.arange(half, dtype=F32) / half)
    ang = (jnp.arange(T, dtype=F32) + token)[:, None] * inv_freq[None, :]
    cos, sin = jnp.cos(ang), jnp.sin(ang)
    z = lambda n: jnp.zeros((T, n), F32)
    c = jnp.concatenate([jnp.ones((T, MLA_NOPE), F32), cos, cos, z(32)], axis=1)
    s1 = jnp.concatenate([z(64), -sin, z(48)], axis=1)
    s2 = jnp.concatenate([z(80), sin, z(32)], axis=1)
    return c, s1, s2


def _pad_lanes(v, n=LANES):
    v = v.reshape(1, -1)
    return jnp.pad(v, ((0, 0), (0, n - v.shape[1])))


def _pack_win(w):
    z = lambda n: jnp.zeros((w.shape[0], n), w.dtype)
    return jnp.concatenate([w[:, 0:384], z(64), w[:, 384:416], z(32), w[:, 416:2976], w[:, 2976:3488],
                            w[:, 3744:4256], w[:, 3488:3616], w[:, 3616:3744]], axis=1)


def _unpack_dwin(d):
    return jnp.concatenate([d[:, 0:384], d[:, 448:480], d[:, 512:3072], d[:, 3072:3584], d[:, 4096:4224],
                            d[:, 4224:4352], d[:, 3584:4096]], axis=1)


def _inproj_weights(l, norm_g, w_in_full):
    wp = _pack_win(w_in_full)
    return dict(ng=norm_g[l].reshape(1, -1), wp=wp, wpt=wp.T)


def _mixer_weights(l, qa, wqb_full, kva, wkvb_full, qn, kn, conv_full, sqn, skn, sinks, w_out_full):
    wq = jnp.pad(wqb_full, ((0, 0), (0, 0), (0, LANES - MLA_QK)))
    wk = jnp.pad(wkvb_full[:, :, :MLA_NOPE], ((0, 0), (0, 0), (0, LANES - MLA_NOPE)))
    wv = jnp.transpose(wkvb_full[:, :, MLA_NOPE:], (1, 0, 2)).reshape(MLA_KV_LORA, GROUP_WIDTH)
    return dict(
        qa=qa[l].reshape(1, -1), kva=kva[l].reshape(1, -1),
        wq=wq, wk=wk, wv=wv, wqt=jnp.transpose(wq, (0, 2, 1)), wkt=jnp.transpose(wk, (0, 2, 1)), wvt=wv.T,
        qn=_pad_lanes(qn[l]), kn=_pad_lanes(kn[l]),
        conv=jnp.pad(conv_full, ((0, 5), (0, 0))),
        sqn=jnp.tile(sqn[l].reshape(1, -1), (1, 2)), skn=jnp.tile(skn[l].reshape(1, -1), (1, 2)),
        sinks=sinks[l], wo=w_out_full, wot=w_out_full.T)


def _layer_weights(l, norm_g, w_in_full, qa, wqb_full, kva, wkvb_full, qn, kn, conv_full, sqn, skn, sinks,
                   w_out_full):
    return dict(_inproj_weights(l, norm_g, w_in_full),
                **_mixer_weights(l, qa, wqb_full, kva, wkvb_full, qn, kn, conv_full, sqn, skn, sinks, w_out_full))


def _layer_fwd(x, lw, rope, late_weights=None):
    proj, h = _inproj_fwd(x, lw["ng"], lw["wp"])
    if late_weights is not None:
        lw = dict(lw, **late_weights(proj))
    q, k, kt, vpad, vt = _mla_prep_fwd(proj, lw, rope)
    o_mla, lse = _mla_attn_fwd(q, k, vt)
    o_swa = _swa_fwd(proj, lw)
    ycat = _mix_fwd(proj, o_mla, o_swa, lw["conv"])
    x_next = _mm_nn(ycat, lw["wo"], "outproj_fwd", residual=x)
    return x_next, dict(x=x, proj=proj, h=h, q=q, k=k, kt=kt, vpad=vpad, o_mla=o_mla, lse=lse, o_swa=o_swa, ycat=ycat,
                        lw=lw)


def _layer_bwd(g, sv, lw, rope, on_big_grads=None):
    proj = sv["proj"]
    dycat = _mm_nn(g, lw["wot"], "outproj_bwd_dy")
    d_wo = _mm_tn(sv["ycat"], g, "outproj_bwd_dw", WIRE_DTYPE, tn=D_MODEL)
    d1, dgs, do_mla, do_swa, d_conv = _mix_bwd(dycat, proj, sv["o_mla"], sv["o_swa"], lw["conv"])
    dsq, dkn_acc, dv_acc, d_sqn, d_sinks = _swa_bwd(proj, sv["o_swa"], do_swa, lw)
    dskv, d_skn = _swa_kv_bwd(proj, dkn_acc, dv_acc, lw)
    dq, dk, dv = _mla_attn_bwd(sv["q"], sv["k"], sv["kt"], sv["vpad"], sv["o_mla"], do_mla, sv["lse"])
    dmla, d_wq, d_wk, d_wv, d_qa, d_kva, d_qn, d_kn = _mla_prep_bwd(proj, dq, dk, dv, lw, rope)
    grads = dict(
        w_out=d_wo, w_qb=d_wq[:, :, :MLA_QK],
        w_kvb=jnp.concatenate([d_wk[:, :, :MLA_NOPE],
                               jnp.transpose(d_wv.reshape(MLA_KV_LORA, HEADS, MLA_NOPE), (1, 0, 2))], axis=2))
    token = 0.0 if on_big_grads is None else on_big_grads("mixer", grads)
    dproj = jnp.concatenate([dmla, d1, dsq, dgs, dskv], axis=1)
    d_wp = _mm_tn(sv["h"], dproj, "inproj_bwd_dw", WIRE_DTYPE, tn=NP // 2)
    grads["w_in"] = _unpack_dwin(d_wp)
    token = token if on_big_grads is None else token + on_big_grads("w_in", grads)
    ng = lw["ng"] + token
    dh = _mm_nn(dproj, lw["wpt"], "inproj_bwd_dh")
    dx, d_ng = _norm_bwd(dh, sv["x"], g, ng)
    grads.update(
        conv=d_conv[0:3], norm_g=d_ng[0], qa=d_qa[0], kva=d_kva[0], qn=d_qn[0, :MLA_QK], kn=d_kn[0, :MLA_QK],
        sqn=d_sqn[0, :SWA_HEAD_DIM], skn=d_skn[0, :SWA_HEAD_DIM], sinks=d_sinks[:, 0])
    return dx, grads


def _local_step(x, target, lws, rope):
    saved = []
    for lw in lws:
        x, sv = _layer_fwd(x, lw, rope)
        saved.append(sv)
    g, loss_tile = _loss_grad(x, target)
    grads = [None] * len(lws)
    for l in reversed(range(len(lws))):
        g, grads[l] = _layer_bwd(g, saved[l], lws[l], rope)
    return loss_tile, g, grads


def _my_coords():
    return lax.axis_index("x"), lax.axis_index("y"), lax.axis_index("c")


def _peer(me, k):
    x, y, c = me
    return (1 - x if k & 4 else x, 1 - y if k & 2 else y, 1 - c if k & 1 else c)


def _lin(d):
    return 4 * d[0] + 2 * d[1] + d[2]


def _push_copies(ins, lands, send_sems, recv_sems, gather):
    me = _my_coords()
    my = _lin(me)
    out, inc = [], []
    for a in range(len(ins)):
        for k in range(1, N_DEV):
            peer = _peer(me, k)
            sems = dict(send_sem=send_sems.at[a * 7 + k - 1], recv_sem=recv_sems.at[a * 7 + k - 1],
                        device_id=peer, device_id_type=pl.DeviceIdType.MESH)
            src = ins[a] if gather else ins[a].at[_lin(peer)]
            out.append(pltpu.make_async_remote_copy(src_ref=src, dst_ref=lands[a].at[my], **sems))
            inc.append(pltpu.make_async_remote_copy(src_ref=src, dst_ref=lands[a].at[_lin(peer)], **sems))
    return out, inc


def _push_start(arrays, name, gather):
    n = len(arrays)
    land_shapes = [((N_DEV,) + a.shape) if gather else a.shape for a in arrays]

    def body(*refs):
        ins, lands = refs[:n], refs[n:2 * n]
        send_sems, recv_sems = refs[2 * n], refs[2 * n + 1]
        token = refs[-1]
        out, _ = _push_copies(ins, lands, send_sems, recv_sems, gather)
        for cp in out:
            cp.start()
        token[...] = jnp.zeros_like(token)

    hbm = pl.BlockSpec(memory_space=pltpu.HBM)
    sem = pl.BlockSpec(memory_space=pltpu.SEMAPHORE)
    res = pl.pallas_call(
        body, name=name,
        out_shape=(pltpu.SemaphoreType.DMA((7 * n,)), pltpu.SemaphoreType.DMA((7 * n,)),
                   *[pltpu.HBM(a.shape, a.dtype) for a in arrays],
                   *[pltpu.HBM(s, a.dtype) for s, a in zip(land_shapes, arrays)],
                   _sds((8, LANES), F32)),
        in_specs=(hbm,) * (2 * n),
        out_specs=(sem, sem) + (hbm,) * (2 * n) + (pl.BlockSpec(memory_space=pltpu.VMEM),),
        input_output_aliases={i: 2 + i for i in range(2 * n)},
        compiler_params=pltpu.CompilerParams(has_side_effects=pltpu.SideEffectType.DATAFLOW_SIDE_EFFECTING),
    )(*[pltpu.with_memory_space_constraint(a, pltpu.HBM) for a in arrays],
      *[pltpu.with_memory_space_constraint(lax.empty(s, a.dtype), pltpu.HBM) for s, a in zip(land_shapes, arrays)])
    return dict(send=res[0], recv=res[1], src=res[2:2 + n], land=res[2 + n:2 + 2 * n], token=res[-1][0, 0],
                gather=gather)


def _push_wait(handle, after, name):
    n = len(handle["src"])
    gather = handle["gather"]

    def body(*refs):
        ins, lands = refs[:n], refs[n:2 * n]
        send_sems, recv_sems = refs[2 * n], refs[2 * n + 1]
        out, inc = _push_copies(ins, lands, send_sems, recv_sems, gather)
        for cp in out:
            cp.wait_send()
        for cp in inc:
            cp.wait_recv()

    hbm = pl.BlockSpec(memory_space=pltpu.HBM)
    sem = pl.BlockSpec(memory_space=pltpu.SEMAPHORE)
    res = pl.pallas_call(
        body, name=name,
        out_shape=tuple(pltpu.HBM(a.shape, a.dtype) for a in (*handle["src"], *handle["land"])),
        in_specs=(hbm,) * (2 * n) + (sem, sem, pl.BlockSpec(memory_space=pl.ANY)),
        out_specs=(hbm,) * (2 * n),
        input_output_aliases={i: i for i in range(2 * n)},
        compiler_params=pltpu.CompilerParams(has_side_effects=pltpu.SideEffectType.DATAFLOW_SIDE_EFFECTING),
    )(*handle["src"], *handle["land"], handle["send"], handle["recv"], after)
    return res[n:]


def _small_all_reduce(v):
    R = v.shape[0]

    def body(v_ref, o_ref, buf, send_sems, recv_sems):
        me = _my_coords()
        my = _lin(me)
        sends = []
        for k in range(1, N_DEV):
            cp = pltpu.make_async_remote_copy(
                src_ref=v_ref, dst_ref=buf.at[my], send_sem=send_sems.at[k - 1], recv_sem=recv_sems.at[k - 1],
                device_id=_peer(me, k), device_id_type=pl.DeviceIdType.MESH)
            cp.start()
            sends.append(cp)
        buf[my] = v_ref[...]
        for k in range(1, N_DEV):
            pltpu.make_async_remote_copy(
                src_ref=v_ref, dst_ref=buf.at[_lin(_peer(me, k))], send_sem=send_sems.at[k - 1],
                recv_sem=recv_sems.at[k - 1], device_id=_peer(me, k),
                device_id_type=pl.DeviceIdType.MESH).wait_recv()
        for cp in sends:
            cp.wait_send()
        tot = buf[0]
        for d in range(1, N_DEV):
            tot = tot + buf[d]
        o_ref[...] = tot

    vm = pl.BlockSpec(memory_space=pltpu.VMEM)
    return pl.pallas_call(
        body, name="small_all_reduce", in_specs=[vm], out_specs=vm, out_shape=_sds(v.shape, F32),
        scratch_shapes=[pltpu.VMEM((N_DEV, R, LANES), F32), pltpu.SemaphoreType.DMA((7,)),
                        pltpu.SemaphoreType.DMA((7,))],
    )(v)


def _adamw_math(w, g, m, v):
    m = ADAM_B1 * m + (1.0 - ADAM_B1) * g
    v = ADAM_B2 * v + (1.0 - ADAM_B2) * (g * g)
    m_hat = m / (1.0 - ADAM_B1 ** ADAM_STEP)
    v_hat = v / (1.0 - ADAM_B2 ** ADAM_STEP)
    delta = -ADAM_LR * (m_hat / (jnp.sqrt(v_hat) + ADAM_EPS) + ADAM_WD * w)
    return delta, m, v


def _adamw(parts, w, m, v, name, tr):
    P, R, C = parts.shape
    tr = min(tr, R)

    def body(p_ref, w_ref, m_ref, v_ref, g_out, d_out, m_out, v_out):
        g = p_ref[0].astype(F32)
        for d in range(1, P):
            g = g + p_ref[d].astype(F32)
        delta, m_new, v_new = _adamw_math(w_ref[...], g, m_ref[...], v_ref[...])
        g_out[...] = g
        d_out[...] = delta
        m_out[...] = m_new
        v_out[...] = v_new

    tile = pl.BlockSpec((tr, C), lambda i: (i, 0))
    return pl.pallas_call(
        body, name=name, grid=(R // tr,),
        in_specs=[pl.BlockSpec((P, tr, C), lambda i: (0, i, 0)), tile, tile, tile],
        out_specs=[tile] * 4, out_shape=[_sds((R, C), F32)] * 4,
        compiler_params=_cp(("parallel",), 32))(parts, w, m, v)


SMALL = (("norm_g", D_MODEL), ("mla_q_a_norm", MLA_Q_LORA), ("mla_kv_a_norm", MLA_KV_LORA), ("mla_q_norm", MLA_QK),
         ("mla_k_norm", MLA_QK), ("swa_q_norm", SWA_HEAD_DIM), ("swa_k_norm", SWA_HEAD_DIM), ("swa_sinks", HEADS))
SMALL_GRAD_KEY = dict(norm_g="norm_g", mla_q_a_norm="qa", mla_kv_a_norm="kva", mla_q_norm="qn", mla_k_norm="kn",
                      swa_q_norm="sqn", swa_k_norm="skn", swa_sinks="sinks")
SMALL_ROWS = 32
CONV_ROWS = 24


def _pack_small(get):
    parts = []
    for l in range(DEPTH):
        for name, n in SMALL:
            v = get(name, l).reshape(-1)
            parts.append(jnp.pad(v, (0, (-n) % LANES)))
    return jnp.concatenate(parts).reshape(SMALL_ROWS, LANES)


def _unpack_small(packed):
    flat = packed.reshape(-1)
    out = {name: [] for name, _ in SMALL}
    off = 0
    for l in range(DEPTH):
        for name, n in SMALL:
            out[name].append(flat[off:off + n])
            off += n + (-n) % LANES
    return {name: jnp.stack(v) for name, v in out.items()}


def kernel(x, norm_g, w_in, mla_q_a_norm, mla_w_qb, mla_kv_a_norm, mla_w_kvb, mla_q_norm, mla_k_norm, conv_w, swa_q_norm, swa_k_norm, swa_sinks, w_out, loss_target, m_norm_g, m_w_in, m_mla_q_a_norm, m_mla_w_qb, m_mla_kv_a_norm, m_mla_w_kvb, m_mla_q_norm, m_mla_k_norm, m_conv_w, m_swa_q_norm, m_swa_k_norm, m_swa_sinks, m_w_out, v_norm_g, v_w_in, v_mla_q_a_norm, v_mla_w_qb, v_mla_kv_a_norm, v_mla_w_kvb, v_mla_q_norm, v_mla_k_norm, v_conv_w, v_swa_q_norm, v_swa_k_norm, v_swa_sinks, v_w_out):
    T = x.shape[1]
    weights = dict(norm_g=norm_g, w_in=w_in, mla_q_a_norm=mla_q_a_norm, mla_w_qb=mla_w_qb,
                   mla_kv_a_norm=mla_kv_a_norm, mla_w_kvb=mla_w_kvb, mla_q_norm=mla_q_norm, mla_k_norm=mla_k_norm,
                   conv_w=conv_w, swa_q_norm=swa_q_norm, swa_k_norm=swa_k_norm, swa_sinks=swa_sinks, w_out=w_out)
    mom_m = dict(norm_g=m_norm_g, w_in=m_w_in, mla_q_a_norm=m_mla_q_a_norm, mla_w_qb=m_mla_w_qb,
                 mla_kv_a_norm=m_mla_kv_a_norm, mla_w_kvb=m_mla_w_kvb, mla_q_norm=m_mla_q_norm,
                 mla_k_norm=m_mla_k_norm, conv_w=m_conv_w, swa_q_norm=m_swa_q_norm, swa_k_norm=m_swa_k_norm,
                 swa_sinks=m_swa_sinks, w_out=m_w_out)
    mom_v = dict(norm_g=v_norm_g, w_in=v_w_in, mla_q_a_norm=v_mla_q_a_norm, mla_w_qb=v_mla_w_qb,
                 mla_kv_a_norm=v_mla_kv_a_norm, mla_w_kvb=v_mla_w_kvb, mla_q_norm=v_mla_q_norm,
                 mla_k_norm=v_mla_k_norm, conv_w=v_conv_w, swa_q_norm=v_swa_q_norm, swa_k_norm=v_swa_k_norm,
                 swa_sinks=v_swa_sinks, w_out=v_w_out)

    my = _lin(_my_coords())

    def shards(l):
        return [w_in[l].astype(MXU_DTYPE), mla_w_qb[l].astype(MXU_DTYPE), mla_w_kvb[l].astype(MXU_DTYPE),
                w_out[l].astype(MXU_DTYPE), conv_w[l]]

    def inproj_weights(l, g_win):
        return _inproj_weights(l, norm_g, jnp.transpose(g_win, (1, 0, 2)).reshape(D_MODEL, IN_COLS))

    def mixer_weights(l, gathered):
        g_wqb, g_wkvb, g_wout, g_conv = gathered
        return _mixer_weights(
            l, mla_q_a_norm, g_wqb, mla_kv_a_norm, g_wkvb, mla_q_norm, mla_k_norm,
            jnp.transpose(g_conv, (1, 0, 2)).reshape(3, GROUP_WIDTH), swa_q_norm, swa_k_norm, swa_sinks,
            g_wout.reshape(D_MIX, D_MODEL))

    slot_of = dict(
        w_in=lambda g: jnp.transpose(g["w_in"].reshape(D_MODEL, N_DEV, IN_COLS // N_DEV), (1, 0, 2)),
        w_out=lambda g: g["w_out"].reshape(N_DEV, D_MIX // N_DEV, D_MODEL),
        w_qb=lambda g: g["w_qb"], w_kvb=lambda g: g["w_kvb"])

    def own_slot(landed, mine):
        return [lax.dynamic_update_index_in_dim(a, m, my, 0) for a, m in zip(landed, mine)]

    def landed(handle, after, name, mine):
        return own_slot(_push_wait(handle, after, name), mine)

    gather_in0 = _push_start(shards(0)[:1], "weight_gather_in0_start", gather=True)
    gather0 = _push_start(shards(0)[1:], "weight_gather0_start", gather=True)
    gather1 = _push_start(shards(1), "weight_gather1_start", gather=True)
    rope = _rope_tables(T, gather_in0["token"] + gather0["token"] + gather1["token"])
    lw0 = inproj_weights(0, landed(gather_in0, rope[0], "weight_gather_in0_wait", shards(0)[:1])[0])
    x1, sv0 = _layer_fwd(
        x[0], lw0, rope,
        late_weights=lambda proj: mixer_weights(0, landed(gather0, proj, "weight_gather0_wait", shards(0)[1:])))
    g1_all = landed(gather1, x1, "weight_gather1_wait", shards(1))
    x2, sv1 = _layer_fwd(x1, dict(inproj_weights(1, g1_all[0]), **mixer_weights(1, g1_all[1:])), rope)
    g2, loss_tile = _loss_grad(x2, loss_target[0])

    parts = {(1, "w_in"): ("w_in", "w_out", "w_qb", "w_kvb"), (0, "mixer"): ("w_out", "w_qb", "w_kvb"),
             (0, "w_in"): ("w_in",)}
    started = []

    def start_exchange(l, part, g):
        if (l, part) not in parts:
            return 0.0
        sl = [slot_of[n](g) for n in parts[(l, part)]]
        handle = _push_start(sl, "grad_exchange%d_%s_start" % (l, part), gather=False)
        started.append((l, part, sl, handle))
        return handle["token"]

    g1, grads1 = _layer_bwd(g2, sv1, sv1["lw"], rope, on_big_grads=functools.partial(start_exchange, 1))
    lw0b = dict(sv0["lw"], conv=sv0["lw"]["conv"] + started[0][3]["token"])
    grad_x, grads0 = _layer_bwd(g1, sv0, lw0b, rope, on_big_grads=functools.partial(start_exchange, 0))
    recv = {}
    for l, part, sl, handle in started:
        got = landed(handle, grad_x, "grad_exchange%d_%s_wait" % (l, part), [s[my] for s in sl])
        recv.update({(l, n): a for n, a in zip(parts[(l, part)], got)})
    grads = [grads0, grads1]
    r_win, r_wout, r_wqb, r_wkvb = [jnp.stack([recv[(0, n)], recv[(1, n)]], axis=1)
                                    for n in ("w_in", "w_out", "w_qb", "w_kvb")]

    small = jnp.concatenate([
        _pack_small(lambda name, l: grads[l][SMALL_GRAD_KEY[name]]),
        jnp.stack([g["conv"] for g in grads]).reshape(CONV_ROWS, LANES),
        loss_tile], axis=0)
    small = _small_all_reduce(small)
    loss = small[SMALL_ROWS + CONV_ROWS, 0]
    my = _lin(_my_coords())
    conv_g = lax.dynamic_slice_in_dim(small[SMALL_ROWS:SMALL_ROWS + CONV_ROWS].reshape(DEPTH, 3, GROUP_WIDTH),
                                      my * 64, 64, axis=2)

    out = {}

    def big(name, recv, rows, cols, tr):
        res = _adamw(recv.reshape(N_DEV, rows, cols), weights[name].reshape(rows, cols),
                     mom_m[name].reshape(rows, cols), mom_v[name].reshape(rows, cols), "adamw_" + name, tr)
        out[name] = [r.reshape(weights[name].shape) for r in res]

    big("w_in", r_win, DEPTH * D_MODEL, IN_COLS // N_DEV, 256)
    big("w_out", r_wout, DEPTH * D_MIX // N_DEV, D_MODEL, 192)
    big("mla_w_qb", r_wqb, DEPTH * MLA_Q_LORA, MLA_QK, 512)
    big("mla_w_kvb", r_wkvb, DEPTH * MLA_KV_LORA, 128, 256)

    pad_conv = lambda a: jnp.pad(a.reshape(-1), (0, 8 * LANES - 6 * 64)).reshape(8, LANES)
    cat = lambda src: jnp.concatenate([_pack_small(lambda name, l: src[name][l]), pad_conv(src["conv_w"])], axis=0)
    g_small = jnp.concatenate([small[:SMALL_ROWS], pad_conv(conv_g)], axis=0)
    res = _adamw(g_small[None], cat(weights), cat(mom_m), cat(mom_v), "adamw_small", SMALL_ROWS + 8)
    smalls = [_unpack_small(r[:SMALL_ROWS]) for r in res]
    for name, _ in SMALL:
        out[name] = [s[name] for s in smalls]
    out["conv_w"] = [r[SMALL_ROWS:].reshape(-1)[:6 * 64].reshape(DEPTH, 3, 64) for r in res]

    order = ["norm_g", "w_in", "mla_q_a_norm", "mla_w_qb", "mla_kv_a_norm", "mla_w_kvb", "mla_q_norm", "mla_k_norm",
             "conv_w", "swa_q_norm", "swa_k_norm", "swa_sinks", "w_out"]
    result = [loss, grad_x[None]]
    for idx in range(4):
        result += [out[name][idx] for name in order]
    return tuple(result)
```

```python
import functools

import jax
import jax.numpy as jnp
import numpy as np
from jax import lax
from jax.experimental import pallas as pl
from jax.experimental.pallas import tpu as pltpu

F32 = jnp.float32
MXU_DTYPE = jnp.bfloat16
WIRE_DTYPE = jnp.bfloat16

N_DEV = 8
DEPTH = 2
D_MODEL = 1024
GROUP_WIDTH = 512
D_MIX = 3 * GROUP_WIDTH
BLOCK = 128
RMS_EPS = 1e-6
NEG_INF = -1e30
HEADS = 8
MLA_QK = 96
MLA_NOPE = 64
MLA_ROPE = 32
MLA_Q_LORA = 256
MLA_KV_LORA = 128
ROPE_THETA = 10000.0
SWA_HEAD_DIM = 64
LANES = 128
IN_COLS = 4256

ADAM_LR = 0.001
ADAM_B1 = 0.9
ADAM_B2 = 0.999
ADAM_EPS = 1e-08
ADAM_WD = 0.01
ADAM_STEP = 10

NP = 4352
CB_QLAT = 0
CB_KVLAT = 2
CB_KROPE = 3
CB_GMLA, CB_CH, CB_CB, CB_CC, CB_GCONV, CB_SQ, CB_GSWA = 1, 2, 3, 4, 5, 6, 7
CB_SK, CB_SV = 32, 33

TM_PROJ = 256
TM_ROW = 256
TK = 256
TQ = 2 * TK
MLA_SCALE = MLA_QK ** -0.5
MLA_ONES_ROW = (64, 0)
LOG2E = 1.4426950408889634
LN2 = 0.6931471805599453
TM_SWA = 512
VMEM_MB = 2 ** 20


def _cp(sem, vmem_mb):
    return pltpu.CompilerParams(dimension_semantics=sem, vmem_limit_bytes=vmem_mb * VMEM_MB)


def _sds(shape, dtype):
    return jax.ShapeDtypeStruct(shape, dtype)


def _dot(a, b):
    return jnp.dot(a, b, preferred_element_type=F32)


def _dot_nt(a, b):
    return lax.dot_general(a, b, (((1,), (1,)), ((), ())), preferred_element_type=F32)


def _dot_tn(a, b):
    return lax.dot_general(a, b, (((0,), (0,)), ((), ())), preferred_element_type=F32)


def _rms(x, n):
    r = lax.rsqrt(jnp.sum(x * x, axis=-1, keepdims=True) * (1.0 / n) + RMS_EPS)
    return x * r, r


def _rms_bwd(dy, xhat, r, w, n):
    g = dy * w
    return r * (g - xhat * (jnp.sum(g * xhat, axis=-1, keepdims=True) * (1.0 / n)))


def _rms_halves(x, half1):
    x2 = x * x
    s0 = jnp.sum(jnp.where(half1, 0.0, x2), axis=-1, keepdims=True)
    s1 = jnp.sum(jnp.where(half1, x2, 0.0), axis=-1, keepdims=True)
    r = jnp.where(half1, lax.rsqrt(s1 * (1.0 / 64) + RMS_EPS), lax.rsqrt(s0 * (1.0 / 64) + RMS_EPS))
    return x * r, r


def _rms_halves_bwd(dy, xhat, r, w, half1):
    g = dy * w
    t = g * xhat
    m0 = jnp.sum(jnp.where(half1, 0.0, t), axis=-1, keepdims=True) * (1.0 / 64)
    m1 = jnp.sum(jnp.where(half1, t, 0.0), axis=-1, keepdims=True) * (1.0 / 64)
    return r * (g - xhat * jnp.where(half1, m1, m0))


def _sigmoid(x):
    return 1.0 / (1.0 + jnp.exp(-x))


def _rope(x, c, s1, s2):
    ax = x.ndim - 1
    return x * c + pltpu.roll(x, 112, ax) * s1 + pltpu.roll(x, 16, ax) * s2


def _rope_bwd(dy, c, s1, s2):
    ax = dy.ndim - 1
    return dy * c + pltpu.roll(dy * s1, 16, ax) + pltpu.roll(dy * s2, 112, ax)


def _fold_rows8(x):
    return jnp.sum(x.reshape(x.shape[0] // 8, 8, x.shape[1]), axis=0)


def _row0(v, rows=8):
    row = lax.broadcasted_iota(jnp.int32, (rows, v.shape[1]), 0)
    return jnp.where(row == 0, jnp.broadcast_to(v, (rows, v.shape[1])), 0.0)


def _mm_nn(a, b, name, out_dtype=F32, residual=None, tm=TM_PROJ):
    M, K = a.shape
    N = b.shape[1]
    tm = min(tm, M)

    def body(*refs):
        if residual is None:
            a_ref, b_ref, o_ref = refs
            acc = _dot(a_ref[...].astype(MXU_DTYPE), b_ref[...])
        else:
            a_ref, b_ref, r_ref, o_ref = refs
            acc = _dot(a_ref[...].astype(MXU_DTYPE), b_ref[...]) + r_ref[...]
        o_ref[...] = acc.astype(out_dtype)

    in_specs = [pl.BlockSpec((tm, K), lambda i: (i, 0)), pl.BlockSpec((K, N), lambda i: (0, 0))]
    args = [a, b]
    if residual is not None:
        in_specs.append(pl.BlockSpec((tm, N), lambda i: (i, 0)))
        args.append(residual)
    return pl.pallas_call(
        body, name=name, grid=(M // tm,), in_specs=in_specs,
        out_specs=pl.BlockSpec((tm, N), lambda i: (i, 0)), out_shape=_sds((M, N), out_dtype),
        compiler_params=_cp(("parallel",), 48))(*args)


def _mm_tn(a, b, name, out_dtype, tn, tk=512):
    T, M = a.shape
    N = b.shape[1]
    tk = min(tk, T)
    nk = T // tk

    def body(a_ref, b_ref, o_ref, acc_ref):
        k = pl.program_id(1)

        @pl.when(k == 0)
        def _():
            acc_ref[...] = jnp.zeros_like(acc_ref)

        acc_ref[...] += _dot_tn(a_ref[...].astype(MXU_DTYPE), b_ref[...].astype(MXU_DTYPE))

        @pl.when(k == nk - 1)
        def _():
            o_ref[...] = acc_ref[...].astype(out_dtype)

    return pl.pallas_call(
        body, name=name, grid=(N // tn, nk),
        in_specs=[pl.BlockSpec((tk, M), lambda n, k: (k, 0)), pl.BlockSpec((tk, tn), lambda n, k: (k, n))],
        out_specs=pl.BlockSpec((M, tn), lambda n, k: (0, n)), out_shape=_sds((M, N), out_dtype),
        scratch_shapes=[pltpu.VMEM((M, tn), F32)],
        compiler_params=_cp(("parallel", "arbitrary"), 48))(a, b)


def _inproj_fwd(x, ng, wp):
    T, D = x.shape
    tm = min(TM_PROJ, T)

    def body(x_ref, g_ref, w_ref, proj_ref, h_ref):
        xhat, _ = _rms(x_ref[...], D)
        h = (xhat * g_ref[...]).astype(MXU_DTYPE)
        h_ref[...] = h
        proj_ref[...] = _dot(h, w_ref[...])

    return pl.pallas_call(
        body, name="inproj_fwd", grid=(T // tm,),
        in_specs=[pl.BlockSpec((tm, D), lambda i: (i, 0)), pl.BlockSpec((1, D), lambda i: (0, 0)),
                  pl.BlockSpec((D, NP), lambda i: (0, 0))],
        out_specs=[pl.BlockSpec((tm, NP), lambda i: (i, 0)), pl.BlockSpec((tm, D), lambda i: (i, 0))],
        out_shape=[_sds((T, NP), F32), _sds((T, D), MXU_DTYPE)],
        compiler_params=_cp(("parallel",), 48))(x, ng, wp)


def _mla_prep_fwd(proj, lw, rope):
    T = proj.shape[0]
    tk = min(TK, T // 2)
    nsub = 2
    tm = nsub * tk

    def body(ql_ref, kvl_ref, kr_ref, qa_ref, kva_ref, wq_ref, wk_ref, wv_ref, qn_ref, kn_ref,
             c_ref, s1_ref, s2_ref, q_out, k_out, kt_out, v_out, vt_out):
        c, s1, s2 = c_ref[...], s1_ref[...], s2_ref[...]
        qhat, _ = _rms(ql_ref[...], MLA_Q_LORA)
        qn = (qhat * qa_ref[...]).astype(MXU_DTYPE)
        khat, _ = _rms(kvl_ref[...], MLA_KV_LORA)
        kvn = (khat * kva_ref[...]).astype(MXU_DTYPE)
        kr = kr_ref[...]
        half1 = lax.broadcasted_iota(jnp.int32, (tm, LANES), 1) >= 64
        ones_row = lax.broadcasted_iota(jnp.int32, (LANES, 1), 0)
        q3, _ = _rms(jnp.stack([_dot(qn, wq_ref[h]) for h in range(HEADS)]), MLA_QK)
        q_out[...] = (_rope(q3 * qn_ref[...], c, s1, s2) * (MLA_SCALE * LOG2E)).astype(MXU_DTYPE)
        k3, _ = _rms(jnp.stack([_dot(kvn, wk_ref[h]) for h in range(HEADS)]) + kr, MLA_QK)
        k3 = _rope(k3 * kn_ref[...], c, s1, s2)
        k_out[...] = k3.astype(MXU_DTYPE)
        for h in range(HEADS):
            for t in range(nsub):
                kt_out[h, t] = k3[h, tk * t:tk * (t + 1)].T.astype(MXU_DTYPE)
        v = _dot(kvn, wv_ref[...])
        for h in range(HEADS):
            vp = v[:, LANES * (h // 2):LANES * (h // 2 + 1)]
            own = half1 if h % 2 else jnp.logical_not(half1)
            vp = jnp.where(own, vp, 0.0)
            v_out[h] = vp.astype(MXU_DTYPE)
            for t in range(nsub):
                vpt = vp[tk * t:tk * (t + 1)].T
                vt_out[h, t] = jnp.where(ones_row == MLA_ONES_ROW[h % 2], 1.0, vpt).astype(MXU_DTYPE)

    full = lambda shape: pl.BlockSpec(shape, lambda i: (0,) * len(shape))
    hd = pl.BlockSpec((HEADS, tm, LANES), lambda i: (0, i, 0))
    hdt = pl.BlockSpec((HEADS, nsub, LANES, tk), lambda i: (0, i, 0, 0))
    nat = _sds((HEADS, T, LANES), MXU_DTYPE)
    tr = _sds((HEADS, T // tk, LANES, tk), MXU_DTYPE)
    return pl.pallas_call(
        body, name="mla_prep_fwd", grid=(T // tm,),
        in_specs=[pl.BlockSpec((tm, 256), lambda i: (i, CB_QLAT)), pl.BlockSpec((tm, LANES), lambda i: (i, CB_KVLAT)),
                  pl.BlockSpec((tm, LANES), lambda i: (i, CB_KROPE)),
                  full((1, 256)), full((1, LANES)), full((HEADS, 256, LANES)), full((HEADS, LANES, LANES)),
                  full((LANES, 512)), full((1, LANES)), full((1, LANES)),
                  pl.BlockSpec((tm, LANES), lambda i: (i, 0)), pl.BlockSpec((tm, LANES), lambda i: (i, 0)),
                  pl.BlockSpec((tm, LANES), lambda i: (i, 0))],
        out_specs=[hd, hd, hdt, hd, hdt],
        out_shape=[nat, nat, tr, nat, tr],
        compiler_params=_cp(("parallel",), 32))(
            proj, proj, proj, lw["qa"], lw["kva"], lw["wq"], lw["wk"], lw["wv"], lw["qn"], lw["kn"],
            rope[0], rope[1], rope[2])


def _mla_attn_fwd(q, k, vt):
    T = q.shape[1]
    tk = min(TK, T // 2)
    tq = 2 * tk

    def body(q_ref, k_ref, vt_ref, o_ref, lse_ref, acc_s, m_s, s_a, s_b):
        i = pl.program_id(1)
        key = lax.broadcasted_iota(jnp.int32, (tk, tq), 0)
        qry = lax.broadcasted_iota(jnp.int32, (tk, tq), 1)
        qs = [q_ref[0], q_ref[1]]
        acc_s[...] = jnp.zeros_like(acc_s)
        m_s[...] = jnp.full(m_s.shape, NEG_INF, F32)

        def scores(kj, buf):
            rows = pl.ds(pl.multiple_of(kj * tk, tk), tk)
            for r in range(2):
                buf[r] = _dot_nt(k_ref[r, rows, :], qs[r])

        def consume(kj, buf, diag):
            for r in range(2):
                s = buf[r]
                if diag is not None:
                    s = jnp.where(key + diag * tk <= qry, s, NEG_INF)
                m_old = m_s[r]
                m_new = jnp.maximum(m_old, jnp.max(s, axis=0, keepdims=True))
                alpha = jnp.exp2(m_old - m_new)
                p = jnp.exp2(s - m_new)
                m_s[r] = m_new
                acc_s[r] = alpha * acc_s[r] + _dot(vt_ref[r, kj], p.astype(MXU_DTYPE))

        scores(0, s_a)

        def pair(kj):
            scores(kj + 1, s_b)
            consume(kj, s_a, None)
            scores(kj + 2, s_a)
            consume(kj + 1, s_b, None)

        def quad(kq, carry):
            pair(4 * kq)
            pair(4 * kq + 2)
            return carry

        lax.fori_loop(0, i // 2, quad, 0)

        @pl.when(i % 2 == 1)
        def _():
            pair(2 * i - 2)

        scores(2 * i + 1, s_b)
        consume(2 * i, s_a, 0)
        consume(2 * i + 1, s_b, 1)
        l = [acc_s[r, pl.ds(MLA_ONES_ROW[r], 1), :] for r in range(2)]
        head0 = lax.broadcasted_iota(jnp.int32, (LANES, 1), 0) < 64
        o_ref[...] = jnp.where(head0, acc_s[0] / l[0], acc_s[1] / l[1]).T
        for r in range(2):
            lse_ref[r] = m_s[r] + jnp.log2(l[r])

    return pl.pallas_call(
        body, name="mla_attn_fwd", grid=(HEADS // 2, T // tq),
        in_specs=[pl.BlockSpec((2, tq, LANES), lambda j, i: (j, i, 0)),
                  pl.BlockSpec((2, T, LANES), lambda j, i: (j, 0, 0)),
                  pl.BlockSpec((2, T // tk, LANES, tk), lambda j, i: (j, 0, 0, 0))],
        out_specs=[pl.BlockSpec((tq, LANES), lambda j, i: (i, j)),
                   pl.BlockSpec((2, 1, tq), lambda j, i: (j, 0, i))],
        out_shape=[_sds((T, GROUP_WIDTH), F32), _sds((HEADS, 1, T), F32)],
        scratch_shapes=[pltpu.VMEM((2, LANES, tq), F32), pltpu.VMEM((2, 1, tq), F32),
                        pltpu.VMEM((2, tk, tq), F32), pltpu.VMEM((2, tk, tq), F32)],
        compiler_params=_cp(("parallel", "arbitrary"), 40))(q, k, vt)


def _swa_kv_variants(x, half1):
    xs = pltpu.roll(x, 64, 1)
    out = {}
    for g in range(2):
        for r in range(2):
            own = half1 if r else jnp.logical_not(half1)
            out[(g, r)] = jnp.where(own, x if g == r else xs, 0.0).astype(MXU_DTYPE)
    return out


def _swa_alibi():
    qi = np.arange(BLOCK)[:, None]
    ki = np.arange(2 * BLOCK)[None, :]
    dist = BLOCK + qi - ki
    slopes = 2.0 ** -(np.arange(HEADS) + 1.0)
    tab = np.where(((dist >= 0) & (dist < BLOCK))[None], slopes[:, None, None] * dist[None], 1e30)
    return jnp.asarray(tab, F32)


def _swa_probs(i, nb, q_ref, k_ref, v_ref, pk_ref, pv_ref, qw_ref, kw_ref, alibi_ref, sink_ref):
    scale = SWA_HEAD_DIM ** -0.5
    half1 = lax.broadcasted_iota(jnp.int32, (1, LANES), 1) >= 64
    k_all = jnp.concatenate([pk_ref[...], k_ref[...]], axis=0)
    v_all = jnp.concatenate([pv_ref[...], v_ref[...]], axis=0)
    khat, _ = _rms_halves(k_all, half1)
    kp = _swa_kv_variants(khat * kw_ref[...], half1)
    vp = _swa_kv_variants(v_all, half1)
    qhat, qr, qn = [], [], []
    for j in range(4):
        xh, r = _rms_halves(q_ref[:, LANES * j:LANES * (j + 1)], half1)
        qhat.append(xh)
        qr.append(r)
        qn.append((xh * qw_ref[...]).astype(MXU_DTYPE))
    ki = lax.broadcasted_iota(jnp.int32, (1, 2 * BLOCK), 1)
    first = jnp.where((i == 0) & (ki < BLOCK), NEG_INF, 0.0)
    s = jnp.stack([_dot_nt(qn[h // 2][BLOCK * b:BLOCK * (b + 1)], kp[(h // 4, h % 2)][BLOCK * b:BLOCK * (b + 2)])
                   for b in range(nb) for h in range(HEADS)]) * scale - alibi_ref[...]
    s = jnp.concatenate([s[:HEADS] + first, s[HEADS:]], axis=0) if nb > 1 else s + first
    sink = jnp.stack([jnp.full((1, 1), sink_ref[h], F32) for _ in range(nb) for h in range(HEADS)])
    m = jnp.maximum(jnp.max(s, axis=-1, keepdims=True), sink)
    e = jnp.exp(s - m)
    es = jnp.exp(sink - m)
    inv = 1.0 / (jnp.sum(e, axis=-1, keepdims=True) + es)
    return e * inv, es * inv, dict(half1=half1, kp=kp, vp=vp, qhat=qhat, qr=qr, qn=qn)


def _swa_fwd(proj, lw):
    T = proj.shape[0]
    tm = min(TM_SWA, T)
    nb = tm // BLOCK

    def body(q_ref, k_ref, v_ref, pk_ref, pv_ref, qw_ref, kw_ref, alibi_ref, sink_ref, o_ref):
        p, _, c = _swa_probs(pl.program_id(0), nb, q_ref, k_ref, v_ref, pk_ref, pv_ref, qw_ref, kw_ref, alibi_ref,
                             sink_ref)
        p = p.astype(MXU_DTYPE)
        for b in range(nb):
            ks = slice(BLOCK * b, BLOCK * (b + 2))
            for j in range(4):
                o_ref[BLOCK * b:BLOCK * (b + 1), LANES * j:LANES * (j + 1)] = (
                    _dot(p[HEADS * b + 2 * j], c["vp"][(j // 2, 0)][ks])
                    + _dot(p[HEADS * b + 2 * j + 1], c["vp"][(j // 2, 1)][ks]))

    prev = lambda cb: pl.BlockSpec((BLOCK, LANES), lambda i: (jnp.maximum(i * nb - 1, 0), cb))
    return pl.pallas_call(
        body, name="swa_fwd", grid=(T // tm,),
        in_specs=[pl.BlockSpec((tm, 512), lambda i: (i, CB_SQ)), pl.BlockSpec((tm, LANES), lambda i: (i, CB_SK)),
                  pl.BlockSpec((tm, LANES), lambda i: (i, CB_SV)), prev(CB_SK), prev(CB_SV),
                  pl.BlockSpec((1, LANES), lambda i: (0, 0)), pl.BlockSpec((1, LANES), lambda i: (0, 0)),
                  pl.BlockSpec((nb * HEADS, BLOCK, 2 * BLOCK), lambda i: (0, 0, 0)),
                  pl.BlockSpec(memory_space=pltpu.SMEM)],
        out_specs=pl.BlockSpec((tm, 512), lambda i: (i, 0)),
        out_shape=_sds((T, GROUP_WIDTH), F32),
        compiler_params=_cp(("parallel",), 40))(
            proj, proj, proj, proj, proj, lw["sqn"], lw["skn"], jnp.tile(_swa_alibi(), (nb, 1, 1)), lw["sinks"])


def _shift_down(u, prev, n, row):
    tm = u.shape[0]
    out = pltpu.roll(u, n, 0)
    row8 = lax.broadcasted_iota(jnp.int32, prev.shape, 0)
    for t in range(n):
        src = jnp.sum(jnp.where(row8 == 8 - n + t, prev, 0.0), axis=0, keepdims=True)
        out = jnp.where(row == t, src, out)
    return out


def _shift_up(u, nxt, n, row):
    tm = u.shape[0]
    out = pltpu.roll(u, tm - n, 0)
    row8 = lax.broadcasted_iota(jnp.int32, nxt.shape, 0)
    for t in range(n):
        src = jnp.sum(jnp.where(row8 == t, nxt, 0.0), axis=0, keepdims=True)
        out = jnp.where(row == tm - n + t, src, out)
    return out


def _mix_fwd(proj, o_mla, o_swa, conv_w):
    T = proj.shape[0]
    tm = min(TM_ROW, T)

    def body(gm_ref, ch_ref, cb_ref, cc_ref, gc_ref, gs_ref, pch_ref, pcc_ref, om_ref, os_ref, w_ref, y_ref):
        i = pl.program_id(0)
        row = lax.broadcasted_iota(jnp.int32, (tm, GROUP_WIDTH), 0)
        u = cc_ref[...] * ch_ref[...]
        u_prev = jnp.where(i > 0, pcc_ref[...] * pch_ref[...], 0.0)
        z = (w_ref[0:1, :] * _shift_down(u, u_prev, 2, row) + w_ref[1:2, :] * _shift_down(u, u_prev, 1, row)
             + w_ref[2:3, :] * u)
        gm, gc, gs = gm_ref[...], gc_ref[...], gs_ref[...]
        y_ref[:, 0:512] = (om_ref[...] * (gm * _sigmoid(gm))).astype(MXU_DTYPE)
        y_ref[:, 512:1024] = (cb_ref[...] * z * (gc * _sigmoid(gc))).astype(MXU_DTYPE)
        y_ref[:, 1024:1536] = (os_ref[...] * (gs * _sigmoid(gs))).astype(MXU_DTYPE)

    blk = lambda cb: pl.BlockSpec((tm, 512), lambda i: (i, cb))
    prev = lambda cb: pl.BlockSpec((8, 512), lambda i: (jnp.maximum(i * (tm // 8) - 1, 0), cb))
    tile = pl.BlockSpec((tm, 512), lambda i: (i, 0))
    return pl.pallas_call(
        body, name="mix_fwd", grid=(T // tm,),
        in_specs=[blk(CB_GMLA), blk(CB_CH), blk(CB_CB), blk(CB_CC), blk(CB_GCONV), blk(CB_GSWA),
                  prev(CB_CH), prev(CB_CC), tile, tile, pl.BlockSpec((8, 512), lambda i: (0, 0))],
        out_specs=pl.BlockSpec((tm, D_MIX), lambda i: (i, 0)),
        out_shape=_sds((T, D_MIX), MXU_DTYPE),
        compiler_params=_cp(("parallel",), 32))(
            proj, proj, proj, proj, proj, proj, proj, proj, o_mla, o_swa, conv_w)


def _outproj_loss(ycat, wo, x, target):
    T, D = x.shape
    K = ycat.shape[1]
    tm = min(TM_PROJ, T)
    nt = T // tm

    def body(y_ref, w_ref, x_ref, t_ref, g_ref, loss_ref, acc_ref):
        i = pl.program_id(0)

        @pl.when(i == 0)
        def _():
            acc_ref[...] = jnp.zeros_like(acc_ref)

        err = _dot(y_ref[...], w_ref[...]) + x_ref[...] - t_ref[...]
        g_ref[...] = err * (1.0 / D)
        acc_ref[...] += _fold_rows8(err * err)

        @pl.when(i == nt - 1)
        def _():
            tot = jnp.sum(jnp.sum(acc_ref[...], axis=1, keepdims=True), axis=0, keepdims=True)
            loss_ref[...] = jnp.broadcast_to(tot * (0.5 / D), (8, LANES))

    tile = pl.BlockSpec((tm, D), lambda i: (i, 0))
    return pl.pallas_call(
        body, name="outproj_loss", grid=(nt,),
        in_specs=[pl.BlockSpec((tm, K), lambda i: (i, 0)), pl.BlockSpec((K, D), lambda i: (0, 0)), tile, tile],
        out_specs=[tile, pl.BlockSpec((8, LANES), lambda i: (0, 0))],
        out_shape=[_sds((T, D), F32), _sds((8, LANES), F32)],
        scratch_shapes=[pltpu.VMEM((8, D), F32)],
        compiler_params=_cp(("arbitrary",), 48))(ycat, wo, x, target)


def _outproj_bwd(g, ycat, wot):
    T, D = g.shape
    K = ycat.shape[1]
    tm = min(512, T)
    nt = T // tm

    def body(g_ref, y_ref, wt_ref, dy_ref, dw_ref, acc_ref):
        i = pl.program_id(0)

        @pl.when(i == 0)
        def _():
            acc_ref[...] = jnp.zeros_like(acc_ref)

        gb = g_ref[...].astype(MXU_DTYPE)
        dy_ref[...] = _dot(gb, wt_ref[...])
        acc_ref[...] += _dot_tn(y_ref[...], gb)

        @pl.when(i == nt - 1)
        def _():
            dw_ref[...] = acc_ref[...].astype(WIRE_DTYPE)

    return pl.pallas_call(
        body, name="outproj_bwd", grid=(nt,),
        in_specs=[pl.BlockSpec((tm, D), lambda i: (i, 0)), pl.BlockSpec((tm, K), lambda i: (i, 0)),
                  pl.BlockSpec((D, K), lambda i: (0, 0))],
        out_specs=[pl.BlockSpec((tm, K), lambda i: (i, 0)), pl.BlockSpec((K, D), lambda i: (0, 0))],
        out_shape=[_sds((T, K), F32), _sds((K, D), WIRE_DTYPE)],
        scratch_shapes=[pltpu.VMEM((K, D), F32)],
        compiler_params=_cp(("arbitrary",), 48))(g, ycat, wot)


def _mix_bwd(dycat, proj, o_mla, o_swa, conv_w):
    T = proj.shape[0]
    tm = min(TM_ROW, T)
    nt = T // tm

    def body(dym_ref, dyc_ref, dys_ref, gm_ref, ch_ref, cb_ref, cc_ref, gc_ref, gs_ref, pch_ref, pcc_ref,
             ndy_ref, ncb_ref, ngc_ref, om_ref, os_ref, w_ref,
             d1_ref, dgs_ref, dom_ref, dos_ref, dw_ref):
        i = pl.program_id(0)

        @pl.when(i == 0)
        def _():
            dw_ref[...] = jnp.zeros_like(dw_ref)

        row = lax.broadcasted_iota(jnp.int32, (tm, GROUP_WIDTH), 0)

        def gate(g):
            sg = _sigmoid(g)
            return g * sg, sg * (1.0 + g * (1.0 - sg))

        gm = gm_ref[...]
        silu, dsilu = gate(gm)
        dym = dym_ref[...]
        dom_ref[...] = dym * silu
        d1_ref[:, 0:512] = (dym * om_ref[...] * dsilu).astype(MXU_DTYPE)

        gs = gs_ref[...]
        silu, dsilu = gate(gs)
        dys = dys_ref[...]
        dos_ref[...] = dys * silu
        dgs_ref[...] = (dys * os_ref[...] * dsilu).astype(MXU_DTYPE)

        ch, cb, cc, gc, dyc = ch_ref[...], cb_ref[...], cc_ref[...], gc_ref[...], dyc_ref[...]
        w0, w1, w2 = w_ref[0:1, :], w_ref[1:2, :], w_ref[2:3, :]
        u = cc * ch
        u_prev = jnp.where(i > 0, pcc_ref[...] * pch_ref[...], 0.0)
        u1 = _shift_down(u, u_prev, 1, row)
        u2 = _shift_down(u, u_prev, 2, row)
        z = w0 * u2 + w1 * u1 + w2 * u
        silu, dsilu = gate(gc)
        dz = dyc * cb * silu
        ngc = ngc_ref[...]
        dz_next = jnp.where(i < nt - 1, ndy_ref[...] * ncb_ref[...] * (ngc * _sigmoid(ngc)), 0.0)
        du = w2 * dz + w1 * _shift_up(dz, dz_next, 1, row) + w0 * _shift_up(dz, dz_next, 2, row)
        d1_ref[:, 512:1024] = (du * cc).astype(MXU_DTYPE)
        d1_ref[:, 1024:1536] = (dyc * z * silu).astype(MXU_DTYPE)
        d1_ref[:, 1536:2048] = (du * ch).astype(MXU_DTYPE)
        d1_ref[:, 2048:2560] = (dyc * cb * z * dsilu).astype(MXU_DTYPE)
        row8 = lax.broadcasted_iota(jnp.int32, (8, GROUP_WIDTH), 0)
        dw = jnp.zeros((8, GROUP_WIDTH), F32)
        for t, shifted in enumerate((u2, u1, u)):
            dw = dw + jnp.where(row8 == t, jnp.sum(dz * shifted, axis=0, keepdims=True), 0.0)
        dw_ref[...] += dw

    blk = lambda cb: pl.BlockSpec((tm, 512), lambda i: (i, cb))
    prev = lambda cb: pl.BlockSpec((8, 512), lambda i: (jnp.maximum(i * (tm // 8) - 1, 0), cb))
    nxt = lambda cb: pl.BlockSpec((8, 512), lambda i: (jnp.minimum((i + 1) * (tm // 8), T // 8 - 1), cb))
    tile = pl.BlockSpec((tm, 512), lambda i: (i, 0))
    return pl.pallas_call(
        body, name="mix_bwd", grid=(nt,),
        in_specs=[blk(0), blk(1), blk(2), blk(CB_GMLA), blk(CB_CH), blk(CB_CB), blk(CB_CC), blk(CB_GCONV),
                  blk(CB_GSWA), prev(CB_CH), prev(CB_CC), nxt(1), nxt(CB_CB), nxt(CB_GCONV), tile, tile,
                  pl.BlockSpec((8, 512), lambda i: (0, 0))],
        out_specs=[pl.BlockSpec((tm, 2560), lambda i: (i, 0)), tile, tile, tile,
                   pl.BlockSpec((8, 512), lambda i: (0, 0))],
        out_shape=[_sds((T, 2560), MXU_DTYPE), _sds((T, 512), MXU_DTYPE), _sds((T, 512), F32),
                   _sds((T, 512), F32), _sds((8, 512), F32)],
        compiler_params=_cp(("arbitrary",), 48))(
            dycat, dycat, dycat, proj, proj, proj, proj, proj, proj, proj, proj, dycat, proj, proj,
            o_mla, o_swa, conv_w)


def _swa_bwd(proj, o_swa, do_swa, lw):
    T = proj.shape[0]
    tm = min(TM_SWA, T)
    nb = tm // BLOCK
    scale = SWA_HEAD_DIM ** -0.5

    def body(q_ref, k_ref, v_ref, pk_ref, pv_ref, o_ref, do_ref, qw_ref, kw_ref, alibi_ref, sink_ref,
             dq_ref, dk_ref, dv_ref, dqw_ref, dsink_ref):
        i = pl.program_id(0)

        @pl.when(i == 0)
        def _():
            dk_ref[...] = jnp.zeros_like(dk_ref)
            dv_ref[...] = jnp.zeros_like(dv_ref)
            dqw_ref[...] = jnp.zeros_like(dqw_ref)
            dsink_ref[...] = jnp.zeros_like(dsink_ref)

        p, p_sink, c = _swa_probs(i, nb, q_ref, k_ref, v_ref, pk_ref, pv_ref, qw_ref, kw_ref, alibi_ref, sink_ref)
        half1, kp, vp, qn, qhat, qr = c["half1"], c["kp"], c["vp"], c["qn"], c["qhat"], c["qr"]
        qw = qw_ref[...]
        rows = [slice(BLOCK * b, BLOCK * (b + 1)) for b in range(nb)]
        keys = [slice(BLOCK * b, BLOCK * (b + 2)) for b in range(nb)]
        dob, dd0, dd1 = [], [], []
        for j in range(4):
            cols = slice(LANES * j, LANES * (j + 1))
            do = do_ref[:, cols]
            dob.append(do.astype(MXU_DTYPE))
            prod = do * o_ref[:, cols]
            dd0.append(jnp.sum(jnp.where(half1, 0.0, prod), axis=-1, keepdims=True))
            dd1.append(jnp.sum(jnp.where(half1, prod, 0.0), axis=-1, keepdims=True))
        dd = jnp.stack([(dd1 if h % 2 else dd0)[h // 2][rows[b]] for b in range(nb) for h in range(HEADS)])
        dp = jnp.stack([_dot_nt(dob[h // 2][rows[b]], vp[(h // 4, h % 2)][keys[b]])
                        for b in range(nb) for h in range(HEADS)])
        ds = (p * (dp - dd) * scale).astype(MXU_DTYPE)
        dsink = -jnp.sum(p_sink * dd, axis=1, keepdims=True)
        pb = p.astype(MXU_DTYPE)

        dqw = jnp.zeros((1, LANES), F32)
        for j in range(4):
            g = j // 2
            dqn = [_dot(ds[HEADS * b + 2 * j], kp[(g, 0)][keys[b]]) + _dot(ds[HEADS * b + 2 * j + 1], kp[(g, 1)][keys[b]])
                   for b in range(nb)]
            dqn = jnp.concatenate(dqn, axis=0) if nb > 1 else dqn[0]
            dqw = dqw + jnp.sum(dqn * qhat[j], axis=0, keepdims=True)
            dq_ref[:, LANES * j:LANES * (j + 1)] = _rms_halves_bwd(dqn, qhat[j], qr[j], qw, half1).astype(MXU_DTYPE)
        dqw_ref[...] += _row0(dqw + pltpu.roll(dqw, 64, 1))

        dk_tot = jnp.zeros((tm + BLOCK, LANES), F32)
        dv_tot = jnp.zeros((tm + BLOCK, LANES), F32)
        for b in range(nb):
            dk_b = jnp.zeros((2 * BLOCK, LANES), F32)
            dv_b = jnp.zeros((2 * BLOCK, LANES), F32)
            for g in range(2):
                for r in range(2):
                    own = half1 if r else jnp.logical_not(half1)
                    ha, hb = HEADS * b + 4 * g + r, HEADS * b + 4 * g + 2 + r
                    qa, qb = qn[2 * g][rows[b]], qn[2 * g + 1][rows[b]]
                    da, db = dob[2 * g][rows[b]], dob[2 * g + 1][rows[b]]
                    dkp = jnp.where(own, _dot_tn(ds[ha], qa) + _dot_tn(ds[hb], qb), 0.0)
                    dvp = jnp.where(own, _dot_tn(pb[ha], da) + _dot_tn(pb[hb], db), 0.0)
                    if g != r:
                        dkp = pltpu.roll(dkp, 64, 1)
                        dvp = pltpu.roll(dvp, 64, 1)
                    dk_b = dk_b + dkp
                    dv_b = dv_b + dvp
            pad = lambda x: jnp.concatenate(
                [z for z in (jnp.zeros((BLOCK * b, LANES), F32), x, jnp.zeros((BLOCK * (nb - 1 - b), LANES), F32))
                 if z.shape[0]], axis=0)
            dk_tot = dk_tot + pad(dk_b)
            dv_tot = dv_tot + pad(dv_b)
        dst = pl.ds(pl.multiple_of(i * tm, BLOCK), tm + BLOCK)
        dk_ref[dst, :] += dk_tot
        dv_ref[dst, :] += dv_tot

        row8 = lax.broadcasted_iota(jnp.int32, (8, LANES), 0)
        dsink_tile = jnp.zeros((8, LANES), F32)
        for b in range(nb):
            for h in range(HEADS):
                dsink_tile = dsink_tile + jnp.where(row8 == h, jnp.broadcast_to(dsink[HEADS * b + h], (8, LANES)), 0.0)
        dsink_ref[...] += dsink_tile

    prev = lambda cb: pl.BlockSpec((BLOCK, LANES), lambda i: (jnp.maximum(i * nb - 1, 0), cb))
    tile = pl.BlockSpec((tm, 512), lambda i: (i, 0))
    small = pl.BlockSpec((8, LANES), lambda i: (0, 0))
    acc = pl.BlockSpec((T + BLOCK, LANES), lambda i: (0, 0))
    return pl.pallas_call(
        body, name="swa_bwd", grid=(T // tm,),
        in_specs=[pl.BlockSpec((tm, 512), lambda i: (i, CB_SQ)), pl.BlockSpec((tm, LANES), lambda i: (i, CB_SK)),
                  pl.BlockSpec((tm, LANES), lambda i: (i, CB_SV)), prev(CB_SK), prev(CB_SV), tile, tile,
                  pl.BlockSpec((1, LANES), lambda i: (0, 0)), pl.BlockSpec((1, LANES), lambda i: (0, 0)),
                  pl.BlockSpec((nb * HEADS, BLOCK, 2 * BLOCK), lambda i: (0, 0, 0)),
                  pl.BlockSpec(memory_space=pltpu.SMEM)],
        out_specs=[tile, acc, acc, small, small],
        out_shape=[_sds((T, 512), MXU_DTYPE), _sds((T + BLOCK, LANES), F32), _sds((T + BLOCK, LANES), F32),
                   _sds((8, LANES), F32), _sds((8, LANES), F32)],
        compiler_params=_cp(("arbitrary",), 48))(
            proj, proj, proj, proj, proj, o_swa, do_swa, lw["sqn"], lw["skn"], jnp.tile(_swa_alibi(), (nb, 1, 1)),
            lw["sinks"])


def _swa_kv_bwd(proj, dkn, dv, lw):
    T = proj.shape[0]
    tm = BLOCK

    def body(k_ref, dkn_ref, dv_ref, kw_ref, d_ref, dkw_ref):
        i = pl.program_id(0)

        @pl.when(i == 0)
        def _():
            dkw_ref[...] = jnp.zeros_like(dkw_ref)

        half1 = lax.broadcasted_iota(jnp.int32, (1, LANES), 1) >= 64
        khat, kr = _rms_halves(k_ref[...], half1)
        dkn_t = dkn_ref[...]
        dkw = jnp.sum(dkn_t * khat, axis=0, keepdims=True)
        dkw_ref[...] += _row0(dkw + pltpu.roll(dkw, 64, 1))
        d_ref[:, 0:LANES] = _rms_halves_bwd(dkn_t, khat, kr, kw_ref[...], half1).astype(MXU_DTYPE)
        d_ref[:, LANES:2 * LANES] = dv_ref[...].astype(MXU_DTYPE)

    return pl.pallas_call(
        body, name="swa_kv_bwd", grid=(T // tm,),
        in_specs=[pl.BlockSpec((tm, LANES), lambda i: (i, CB_SK)), pl.BlockSpec((tm, LANES), lambda i: (i + 1, 0)),
                  pl.BlockSpec((tm, LANES), lambda i: (i + 1, 0)), pl.BlockSpec((1, LANES), lambda i: (0, 0))],
        out_specs=[pl.BlockSpec((tm, 2 * LANES), lambda i: (i, 0)), pl.BlockSpec((8, LANES), lambda i: (0, 0))],
        out_shape=[_sds((T, 2 * LANES), MXU_DTYPE), _sds((8, LANES), F32)],
        compiler_params=_cp(("arbitrary",), 32))(proj, dkn, dv, lw["skn"])


def _mla_attn_bwd(q, k, kt, vpad, o, do, lse):
    T = q.shape[1]
    tk = min(TK, T // 2)
    tq = 2 * tk

    def body(q_ref, k_ref, kt_ref, v_ref, o_ref, do_ref, lse_ref, dq_ref, dk_ref, dv_ref, dqt_s,
             s_a, s_b, p_a, p_b):
        h = pl.program_id(0)
        i = pl.program_id(1)

        @pl.when(i == 0)
        def _():
            dk_ref[...] = jnp.zeros_like(dk_ref)
            dv_ref[...] = jnp.zeros_like(dv_ref)

        key = lax.broadcasted_iota(jnp.int32, (tk, tq), 0)
        qry = lax.broadcasted_iota(jnp.int32, (tk, tq), 1)
        own = (lax.broadcasted_iota(jnp.int32, (1, LANES), 1) // 64) == (h % 2)
        own_rows = (lax.broadcasted_iota(jnp.int32, (LANES, 1), 0) // 64) == (h % 2)
        do_t = do_ref[...]
        dob = do_t.astype(MXU_DTYPE)
        prod_t = (do_t * o_ref[...]).T
        dd = jnp.sum(jnp.where(own_rows, prod_t, 0.0), axis=0, keepdims=True)
        qh = q_ref[0]
        lse_t = lse_ref[0]
        dqt_s[...] = jnp.zeros_like(dqt_s)

        def scores(kj, s_buf, p_buf):
            rows = pl.ds(pl.multiple_of(kj * tk, tk), tk)
            s_buf[...] = _dot_nt(k_ref[0, rows, :], qh)
            p_buf[...] = _dot_nt(v_ref[0, rows, :], dob)

        def consume(kj, s_buf, p_buf, diag):
            rows = pl.ds(pl.multiple_of(kj * tk, tk), tk)
            s = s_buf[...]
            if diag is not None:
                s = jnp.where(key + diag * tk <= qry, s, NEG_INF)
            p = jnp.exp2(s - lse_t)
            ds = (p * (p_buf[...] - dd)).astype(MXU_DTYPE)
            dqt_s[...] += _dot(kt_ref[0, kj], ds)
            dk_ref[0, rows, :] += _dot(ds, qh)
            dv_ref[0, rows, :] += jnp.where(own, _dot(p.astype(MXU_DTYPE), dob), 0.0)

        scores(0, s_a, p_a)

        def pair(kj):
            scores(kj + 1, s_b, p_b)
            consume(kj, s_a, p_a, None)
            scores(kj + 2, s_a, p_a)
            consume(kj + 1, s_b, p_b, None)

        def quad(kq, carry):
            pair(4 * kq)
            pair(4 * kq + 2)
            return carry

        lax.fori_loop(0, i // 2, quad, 0)

        @pl.when(i % 2 == 1)
        def _():
            pair(2 * i - 2)

        scores(2 * i + 1, s_b, p_b)
        consume(2 * i, s_a, p_a, 0)
        consume(2 * i + 1, s_b, p_b, 1)
        dq_ref[0] = dqt_s[...].T

    res = pl.BlockSpec((1, T, LANES), lambda h, i: (h, 0, 0))
    res_t = pl.BlockSpec((1, T // tk, LANES, tk), lambda h, i: (h, 0, 0, 0))
    buf = pltpu.VMEM((tk, tq), F32)
    return pl.pallas_call(
        body, name="mla_attn_bwd", grid=(HEADS, T // tq),
        in_specs=[pl.BlockSpec((1, tq, LANES), lambda h, i: (h, i, 0)), res, res_t, res,
                  pl.BlockSpec((tq, LANES), lambda h, i: (i, h // 2)),
                  pl.BlockSpec((tq, LANES), lambda h, i: (i, h // 2)),
                  pl.BlockSpec((1, 1, tq), lambda h, i: (h, 0, i))],
        out_specs=[pl.BlockSpec((1, tq, LANES), lambda h, i: (h, i, 0)), res, res],
        out_shape=[_sds((HEADS, T, LANES), F32)] * 3,
        scratch_shapes=[pltpu.VMEM((LANES, tq), F32), buf, buf, buf, buf],
        compiler_params=_cp(("parallel", "arbitrary"), 48))(q, k, kt, vpad, o, do, lse)


def _mla_prep_bwd(proj, dq, dk, dv, lw, rope):
    T = proj.shape[0]
    tm = min(TK, T // 2)

    def body(ql_ref, kvl_ref, kr_ref, dq_ref, dk_ref, dv_ref, qa_ref, kva_ref, wq_ref, wk_ref, wv_ref,
             wqt_ref, wkt_ref, wvt_ref, qn_ref, kn_ref, c_ref, s1_ref, s2_ref,
             d_ref, dwq_ref, dwk_ref, dwv_ref, dqa_ref, dkva_ref, dqn_ref, dkn_ref):
        i = pl.program_id(0)

        @pl.when(i == 0)
        def _():
            for ref in (dwq_ref, dwk_ref, dwv_ref, dqa_ref, dkva_ref, dqn_ref, dkn_ref):
                ref[...] = jnp.zeros_like(ref)

        c, s1, s2 = c_ref[...], s1_ref[...], s2_ref[...]
        lane = lax.broadcasted_iota(jnp.int32, (1, LANES), 1)
        qlhat, qlr = _rms(ql_ref[...], MLA_Q_LORA)
        qn = (qlhat * qa_ref[...]).astype(MXU_DTYPE)
        kvhat, kvr = _rms(kvl_ref[...], MLA_KV_LORA)
        kvn = (kvhat * kva_ref[...]).astype(MXU_DTYPE)
        kr = kr_ref[...]
        x3, r3 = _rms(jnp.stack([_dot(qn, wq_ref[h]) for h in range(HEADS)]), MLA_QK)
        dy3 = _rope_bwd(dq_ref[...] * MLA_SCALE, c, s1, s2)
        dqw = jnp.sum(jnp.sum(dy3 * x3, axis=0), axis=0, keepdims=True)
        dx3 = _rms_bwd(dy3, x3, r3, qn_ref[...], MLA_QK).astype(MXU_DTYPE)
        dqnl = jnp.zeros((tm, MLA_Q_LORA), F32)
        for h in range(HEADS):
            dwq_ref[h] += _dot_tn(qn, dx3[h])
            dqnl = dqnl + _dot(dx3[h], wqt_ref[h])

        x3, r3 = _rms(jnp.stack([_dot(kvn, wk_ref[h]) for h in range(HEADS)]) + kr, MLA_QK)
        dy3 = _rope_bwd(dk_ref[...] * LN2, c, s1, s2)
        dkw = jnp.sum(jnp.sum(dy3 * x3, axis=0), axis=0, keepdims=True)
        dxf3 = _rms_bwd(dy3, x3, r3, kn_ref[...], MLA_QK)
        dkr = jnp.sum(dxf3, axis=0)
        dx3 = dxf3.astype(MXU_DTYPE)
        dkvn = jnp.zeros((tm, MLA_KV_LORA), F32)
        for h in range(HEADS):
            dwk_ref[h] += _dot_tn(kvn, dx3[h])
            dkvn = dkvn + _dot(dx3[h], wkt_ref[h])
        dvc = jnp.concatenate([dv_ref[2 * j] + dv_ref[2 * j + 1] for j in range(4)], axis=1).astype(MXU_DTYPE)
        dwv_ref[...] += _dot_tn(kvn, dvc)
        dkvn = dkvn + _dot(dvc, wvt_ref[...])
        dqa_ref[...] += _row0(jnp.sum(dqnl * qlhat, axis=0, keepdims=True))
        dkva_ref[...] += _row0(jnp.sum(dkvn * kvhat, axis=0, keepdims=True))
        dqn_ref[...] += _row0(dqw)
        dkn_ref[...] += _row0(dkw)
        d_ref[:, 0:256] = _rms_bwd(dqnl, qlhat, qlr, qa_ref[...], MLA_Q_LORA).astype(MXU_DTYPE)
        d_ref[:, 256:384] = _rms_bwd(dkvn, kvhat, kvr, kva_ref[...], MLA_KV_LORA).astype(MXU_DTYPE)
        d_ref[:, 384:512] = jnp.where((lane >= 64) & (lane < 96), dkr, 0.0).astype(MXU_DTYPE)

    full = lambda shape: pl.BlockSpec(shape, lambda i: (0,) * len(shape))
    hd = pl.BlockSpec((HEADS, tm, LANES), lambda i: (0, i, 0))
    tab = pl.BlockSpec((tm, LANES), lambda i: (i, 0))
    return pl.pallas_call(
        body, name="mla_prep_bwd", grid=(T // tm,),
        in_specs=[pl.BlockSpec((tm, 256), lambda i: (i, CB_QLAT)), pl.BlockSpec((tm, LANES), lambda i: (i, CB_KVLAT)),
                  pl.BlockSpec((tm, LANES), lambda i: (i, CB_KROPE)), hd, hd, hd,
                  full((1, 256)), full((1, LANES)), full((HEADS, 256, LANES)), full((HEADS, LANES, LANES)),
                  full((LANES, 512)), full((HEADS, LANES, 256)), full((HEADS, LANES, LANES)), full((512, LANES)),
                  full((1, LANES)), full((1, LANES)), tab, tab, tab],
        out_specs=[pl.BlockSpec((tm, 512), lambda i: (i, 0)), full((HEADS, 256, LANES)),
                   full((HEADS, LANES, LANES)), full((LANES, 512)), full((8, 256)), full((8, LANES)),
                   full((8, LANES)), full((8, LANES))],
        out_shape=[_sds((T, 512), MXU_DTYPE), _sds((HEADS, 256, LANES), F32), _sds((HEADS, LANES, LANES), F32),
                   _sds((LANES, 512), F32), _sds((8, 256), F32), _sds((8, LANES), F32), _sds((8, LANES), F32),
                   _sds((8, LANES), F32)],
        compiler_params=_cp(("arbitrary",), 48))(
            proj, proj, proj, dq, dk, dv, lw["qa"], lw["kva"], lw["wq"], lw["wk"], lw["wv"],
            lw["wqt"], lw["wkt"], lw["wvt"], lw["qn"], lw["kn"], rope[0], rope[1], rope[2])


def _inproj_bwd_dx(dproj, wpt, x, g_in, ng):
    T, D = x.shape
    tm = min(TM_PROJ, T)

    def body(dp_ref, wt_ref, x_ref, g_ref, w_ref, dx_ref, dw_ref):
        i = pl.program_id(0)

        @pl.when(i == 0)
        def _():
            dw_ref[...] = jnp.zeros_like(dw_ref)

        dh = _dot(dp_ref[...], wt_ref[...])
        xhat, r = _rms(x_ref[...], D)
        dw_ref[...] += _row0(jnp.sum(dh * xhat, axis=0, keepdims=True))
        dx_ref[...] = g_ref[...] + _rms_bwd(dh, xhat, r, w_ref[...], D)

    tile = pl.BlockSpec((tm, D), lambda i: (i, 0))
    return pl.pallas_call(
        body, name="inproj_bwd_dx", grid=(T // tm,),
        in_specs=[pl.BlockSpec((tm, NP), lambda i: (i, 0)), pl.BlockSpec((NP, D), lambda i: (0, 0)), tile, tile,
                  pl.BlockSpec((1, D), lambda i: (0, 0))],
        out_specs=[tile, pl.BlockSpec((8, D), lambda i: (0, 0))],
        out_shape=[_sds((T, D), F32), _sds((8, D), F32)],
        compiler_params=_cp(("arbitrary",), 48))(dproj, wpt, x, g_in, ng)


def _rope_tables(T, token=0.0):
    half = MLA_ROPE // 2
    inv_freq = jnp.power(jnp.float32(ROPE_THETA), -jnp.arange(half, dtype=F32) / half)
    ang = (jnp.arange(T, dtype=F32) + token)[:, None] * inv_freq[None, :]
    cos, sin = jnp.cos(ang), jnp.sin(ang)
    z = lambda n: jnp.zeros((T, n), F32)
    c = jnp.concatenate([jnp.ones((T, MLA_NOPE), F32), cos, cos, z(32)], axis=1)
    s1 = jnp.concatenate([z(64), -sin, z(48)], axis=1)
    s2 = jnp.concatenate([z(80), sin, z(32)], axis=1)
    return c, s1, s2


def _pad_lanes(v, n=LANES):
    v = v.reshape(1, -1)
    return jnp.pad(v, ((0, 0), (0, n - v.shape[1])))


def _pack_win(w):
    z = lambda n: jnp.zeros((w.shape[0], n), w.dtype)
    return jnp.concatenate([w[:, 0:384], z(64), w[:, 384:416], z(32), w[:, 416:2976], w[:, 2976:3488],
                            w[:, 3744:4256], w[:, 3488:3616], w[:, 3616:3744]], axis=1)


def _unpack_dwin(d):
    return jnp.concatenate([d[:, 0:384], d[:, 448:480], d[:, 512:3072], d[:, 3072:3584], d[:, 4096:4224],
                            d[:, 4224:4352], d[:, 3584:4096]], axis=1)


def _inproj_weights(l, norm_g, w_in_full):
    wp = _pack_win(w_in_full)
    return dict(ng=norm_g[l].reshape(1, -1), wp=wp, wpt=wp.T)


def _mixer_weights(l, qa, wqb_full, kva, wkvb_full, qn, kn, conv_full, sqn, skn, sinks, w_out_full):
    wq = jnp.pad(wqb_full, ((0, 0), (0, 0), (0, LANES - MLA_QK)))
    wk = jnp.pad(wkvb_full[:, :, :MLA_NOPE], ((0, 0), (0, 0), (0, LANES - MLA_NOPE)))
    wv = jnp.transpose(wkvb_full[:, :, MLA_NOPE:], (1, 0, 2)).reshape(MLA_KV_LORA, GROUP_WIDTH)
    return dict(
        qa=qa[l].reshape(1, -1), kva=kva[l].reshape(1, -1),
        wq=wq, wk=wk, wv=wv, wqt=jnp.transpose(wq, (0, 2, 1)), wkt=jnp.transpose(wk, (0, 2, 1)), wvt=wv.T,
        qn=_pad_lanes(qn[l]), kn=_pad_lanes(kn[l]),
        conv=jnp.pad(conv_full, ((0, 5), (0, 0))),
        sqn=jnp.tile(sqn[l].reshape(1, -1), (1, 2)), skn=jnp.tile(skn[l].reshape(1, -1), (1, 2)),
        sinks=sinks[l], wo=w_out_full, wot=w_out_full.T)


def _layer_weights(l, norm_g, w_in_full, qa, wqb_full, kva, wkvb_full, qn, kn, conv_full, sqn, skn, sinks,
                   w_out_full):
    return dict(_inproj_weights(l, norm_g, w_in_full),
                **_mixer_weights(l, qa, wqb_full, kva, wkvb_full, qn, kn, conv_full, sqn, skn, sinks, w_out_full))


def _layer_fwd(x, lw, rope, late_weights=None, target=None):
    proj, h = _inproj_fwd(x, lw["ng"], lw["wp"])
    if late_weights is not None:
        lw = dict(lw, **late_weights(proj))
    q, k, kt, vpad, vt = _mla_prep_fwd(proj, lw, rope)
    o_mla, lse = _mla_attn_fwd(q, k, vt)
    o_swa = _swa_fwd(proj, lw)
    ycat = _mix_fwd(proj, o_mla, o_swa, lw["conv"])
    if target is None:
        out = _mm_nn(ycat, lw["wo"], "outproj_fwd", residual=x)
    else:
        out = _outproj_loss(ycat, lw["wo"], x, target)
    return out, dict(x=x, proj=proj, h=h, q=q, k=k, kt=kt, vpad=vpad, o_mla=o_mla, lse=lse, o_swa=o_swa, ycat=ycat,
                     lw=lw)


def _layer_bwd(g, sv, lw, rope, on_big_grads=None):
    proj = sv["proj"]
    dycat, d_wo = _outproj_bwd(g, sv["ycat"], lw["wot"])
    d1, dgs, do_mla, do_swa, d_conv = _mix_bwd(dycat, proj, sv["o_mla"], sv["o_swa"], lw["conv"])
    dsq, dkn_acc, dv_acc, d_sqn, d_sinks = _swa_bwd(proj, sv["o_swa"], do_swa, lw)
    dskv, d_skn = _swa_kv_bwd(proj, dkn_acc, dv_acc, lw)
    dq, dk, dv = _mla_attn_bwd(sv["q"], sv["k"], sv["kt"], sv["vpad"], sv["o_mla"], do_mla, sv["lse"])
    dmla, d_wq, d_wk, d_wv, d_qa, d_kva, d_qn, d_kn = _mla_prep_bwd(proj, dq, dk, dv, lw, rope)
    grads = dict(
        w_out=d_wo, w_qb=d_wq[:, :, :MLA_QK],
        w_kvb=jnp.concatenate([d_wk[:, :, :MLA_NOPE],
                               jnp.transpose(d_wv.reshape(MLA_KV_LORA, HEADS, MLA_NOPE), (1, 0, 2))], axis=2))
    token = 0.0 if on_big_grads is None else on_big_grads("mixer", grads)
    dproj = jnp.concatenate([dmla, d1, dsq, dgs, dskv], axis=1)
    d_wp = _mm_tn(sv["h"], dproj, "inproj_bwd_dw", WIRE_DTYPE, tn=NP // 2)
    grads["w_in"] = _unpack_dwin(d_wp)
    token = token if on_big_grads is None else token + on_big_grads("w_in", grads)
    dx, d_ng = _inproj_bwd_dx(dproj, lw["wpt"], sv["x"], g, lw["ng"] + token)
    grads.update(
        conv=d_conv[0:3], norm_g=d_ng[0], qa=d_qa[0], kva=d_kva[0], qn=d_qn[0, :MLA_QK], kn=d_kn[0, :MLA_QK],
        sqn=d_sqn[0, :SWA_HEAD_DIM], skn=d_skn[0, :SWA_HEAD_DIM], sinks=d_sinks[:, 0])
    return dx, grads


def _local_step(x, target, lws, rope):
    saved = []
    for l, lw in enumerate(lws):
        x, sv = _layer_fwd(x, lw, rope, target=target if l == len(lws) - 1 else None)
        saved.append(sv)
    g, loss_tile = x
    grads = [None] * len(lws)
    for l in reversed(range(len(lws))):
        g, grads[l] = _layer_bwd(g, saved[l], lws[l], rope)
    return loss_tile, g, grads


def _my_coords():
    return lax.axis_index("x"), lax.axis_index("y"), lax.axis_index("c")


def _peer(me, k):
    x, y, c = me
    return (1 - x if k & 4 else x, 1 - y if k & 2 else y, 1 - c if k & 1 else c)


def _lin(d):
    return 4 * d[0] + 2 * d[1] + d[2]


def _push_copies(ins, lands, send_sems, recv_sems, gather):
    me = _my_coords()
    my = _lin(me)
    out, inc = [], []
    for a in range(len(ins)):
        for k in range(1, N_DEV):
            peer = _peer(me, k)
            sems = dict(send_sem=send_sems.at[a * 7 + k - 1], recv_sem=recv_sems.at[a * 7 + k - 1],
                        device_id=peer, device_id_type=pl.DeviceIdType.MESH)
            src = ins[a] if gather else ins[a].at[_lin(peer)]
            out.append(pltpu.make_async_remote_copy(src_ref=src, dst_ref=lands[a].at[my], **sems))
            inc.append(pltpu.make_async_remote_copy(src_ref=src, dst_ref=lands[a].at[_lin(peer)], **sems))
    return out, inc


def _push_start(arrays, name, gather):
    n = len(arrays)
    land_shapes = [((N_DEV,) + a.shape) if gather else a.shape for a in arrays]

    def body(*refs):
        ins, lands = refs[:n], refs[n:2 * n]
        send_sems, recv_sems = refs[2 * n], refs[2 * n + 1]
        token = refs[-1]
        out, _ = _push_copies(ins, lands, send_sems, recv_sems, gather)
        for cp in out:
            cp.start()
        token[...] = jnp.zeros_like(token)

    hbm = pl.BlockSpec(memory_space=pltpu.HBM)
    sem = pl.BlockSpec(memory_space=pltpu.SEMAPHORE)
    res = pl.pallas_call(
        body, name=name,
        out_shape=(pltpu.SemaphoreType.DMA((7 * n,)), pltpu.SemaphoreType.DMA((7 * n,)),
                   *[pltpu.HBM(a.shape, a.dtype) for a in arrays],
                   *[pltpu.HBM(s, a.dtype) for s, a in zip(land_shapes, arrays)],
                   _sds((8, LANES), F32)),
        in_specs=(hbm,) * (2 * n),
        out_specs=(sem, sem) + (hbm,) * (2 * n) + (pl.BlockSpec(memory_space=pltpu.VMEM),),
        input_output_aliases={i: 2 + i for i in range(2 * n)},
        compiler_params=pltpu.CompilerParams(has_side_effects=pltpu.SideEffectType.DATAFLOW_SIDE_EFFECTING),
    )(*[pltpu.with_memory_space_constraint(a, pltpu.HBM) for a in arrays],
      *[pltpu.with_memory_space_constraint(lax.empty(s, a.dtype), pltpu.HBM) for s, a in zip(land_shapes, arrays)])
    return dict(send=res[0], recv=res[1], src=res[2:2 + n], land=res[2 + n:2 + 2 * n], token=res[-1][0, 0],
                gather=gather)


def _push_wait(handle, after, name):
    n = len(handle["src"])
    gather = handle["gather"]

    def body(*refs):
        ins, lands = refs[:n], refs[n:2 * n]
        send_sems, recv_sems = refs[2 * n], refs[2 * n + 1]
        out, inc = _push_copies(ins, lands, send_sems, recv_sems, gather)
        for cp in out:
            cp.wait_send()
        for cp in inc:
            cp.wait_recv()

    hbm = pl.BlockSpec(memory_space=pltpu.HBM)
    sem = pl.BlockSpec(memory_space=pltpu.SEMAPHORE)
    res = pl.pallas_call(
        body, name=name,
        out_shape=tuple(pltpu.HBM(a.shape, a.dtype) for a in (*handle["src"], *handle["land"])),
        in_specs=(hbm,) * (2 * n) + (sem, sem, pl.BlockSpec(memory_space=pl.ANY)),
        out_specs=(hbm,) * (2 * n),
        input_output_aliases={i: i for i in range(2 * n)},
        compiler_params=pltpu.CompilerParams(has_side_effects=pltpu.SideEffectType.DATAFLOW_SIDE_EFFECTING),
    )(*handle["src"], *handle["land"], handle["send"], handle["recv"], after)
    return res[n:]


def _small_all_reduce(v):
    R = v.shape[0]

    def body(v_ref, o_ref, buf, send_sems, recv_sems):
        me = _my_coords()
        my = _lin(me)
        sends = []
        for k in range(1, N_DEV):
            cp = pltpu.make_async_remote_copy(
                src_ref=v_ref, dst_ref=buf.at[my], send_sem=send_sems.at[k - 1], recv_sem=recv_sems.at[k - 1],
                device_id=_peer(me, k), device_id_type=pl.DeviceIdType.MESH)
            cp.start()
            sends.append(cp)
        buf[my] = v_ref[...]
        for k in range(1, N_DEV):
            pltpu.make_async_remote_copy(
                src_ref=v_ref, dst_ref=buf.at[_lin(_peer(me, k))], send_sem=send_sems.at[k - 1],
                recv_sem=recv_sems.at[k - 1], device_id=_peer(me, k),
                device_id_type=pl.DeviceIdType.MESH).wait_recv()
        for cp in sends:
            cp.wait_send()
        tot = buf[0]
        for d in range(1, N_DEV):
            tot = tot + buf[d]
        o_ref[...] = tot

    vm = pl.BlockSpec(memory_space=pltpu.VMEM)
    return pl.pallas_call(
        body, name="small_all_reduce", in_specs=[vm], out_specs=vm, out_shape=_sds(v.shape, F32),
        scratch_shapes=[pltpu.VMEM((N_DEV, R, LANES), F32), pltpu.SemaphoreType.DMA((7,)),
                        pltpu.SemaphoreType.DMA((7,))],
    )(v)


def _adamw_math(w, g, m, v):
    m = ADAM_B1 * m + (1.0 - ADAM_B1) * g
    v = ADAM_B2 * v + (1.0 - ADAM_B2) * (g * g)
    m_hat = m / (1.0 - ADAM_B1 ** ADAM_STEP)
    v_hat = v / (1.0 - ADAM_B2 ** ADAM_STEP)
    delta = -ADAM_LR * (m_hat / (jnp.sqrt(v_hat) + ADAM_EPS) + ADAM_WD * w)
    return delta, m, v


def _adamw(parts, w, m, v, name, tr):
    P, R, C = parts.shape
    tr = min(tr, R)

    def body(p_ref, w_ref, m_ref, v_ref, g_out, d_out, m_out, v_out):
        g = p_ref[0].astype(F32)
        for d in range(1, P):
            g = g + p_ref[d].astype(F32)
        delta, m_new, v_new = _adamw_math(w_ref[...], g, m_ref[...], v_ref[...])
        g_out[...] = g
        d_out[...] = delta
        m_out[...] = m_new
        v_out[...] = v_new

    tile = pl.BlockSpec((tr, C), lambda i: (i, 0))
    return pl.pallas_call(
        body, name=name, grid=(R // tr,),
        in_specs=[pl.BlockSpec((P, tr, C), lambda i: (0, i, 0)), tile, tile, tile],
        out_specs=[tile] * 4, out_shape=[_sds((R, C), F32)] * 4,
        compiler_params=_cp(("parallel",), 32))(parts, w, m, v)


SMALL = (("norm_g", D_MODEL), ("mla_q_a_norm", MLA_Q_LORA), ("mla_kv_a_norm", MLA_KV_LORA), ("mla_q_norm", MLA_QK),
         ("mla_k_norm", MLA_QK), ("swa_q_norm", SWA_HEAD_DIM), ("swa_k_norm", SWA_HEAD_DIM), ("swa_sinks", HEADS))
SMALL_GRAD_KEY = dict(norm_g="norm_g", mla_q_a_norm="qa", mla_kv_a_norm="kva", mla_q_norm="qn", mla_k_norm="kn",
                      swa_q_norm="sqn", swa_k_norm="skn", swa_sinks="sinks")
SMALL_ROWS = 32
CONV_ROWS = 24


def _pack_small(get):
    parts = []
    for l in range(DEPTH):
        for name, n in SMALL:
            v = get(name, l).reshape(-1)
            parts.append(jnp.pad(v, (0, (-n) % LANES)))
    return jnp.concatenate(parts).reshape(SMALL_ROWS, LANES)


def _unpack_small(packed):
    flat = packed.reshape(-1)
    out = {name: [] for name, _ in SMALL}
    off = 0
    for l in range(DEPTH):
        for name, n in SMALL:
            out[name].append(flat[off:off + n])
            off += n + (-n) % LANES
    return {name: jnp.stack(v) for name, v in out.items()}


def kernel(x, norm_g, w_in, mla_q_a_norm, mla_w_qb, mla_kv_a_norm, mla_w_kvb, mla_q_norm, mla_k_norm, conv_w, swa_q_norm, swa_k_norm, swa_sinks, w_out, loss_target, m_norm_g, m_w_in, m_mla_q_a_norm, m_mla_w_qb, m_mla_kv_a_norm, m_mla_w_kvb, m_mla_q_norm, m_mla_k_norm, m_conv_w, m_swa_q_norm, m_swa_k_norm, m_swa_sinks, m_w_out, v_norm_g, v_w_in, v_mla_q_a_norm, v_mla_w_qb, v_mla_kv_a_norm, v_mla_w_kvb, v_mla_q_norm, v_mla_k_norm, v_conv_w, v_swa_q_norm, v_swa_k_norm, v_swa_sinks, v_w_out):
    T = x.shape[1]
    weights = dict(norm_g=norm_g, w_in=w_in, mla_q_a_norm=mla_q_a_norm, mla_w_qb=mla_w_qb,
                   mla_kv_a_norm=mla_kv_a_norm, mla_w_kvb=mla_w_kvb, mla_q_norm=mla_q_norm, mla_k_norm=mla_k_norm,
                   conv_w=conv_w, swa_q_norm=swa_q_norm, swa_k_norm=swa_k_norm, swa_sinks=swa_sinks, w_out=w_out)
    mom_m = dict(norm_g=m_norm_g, w_in=m_w_in, mla_q_a_norm=m_mla_q_a_norm, mla_w_qb=m_mla_w_qb,
                 mla_kv_a_norm=m_mla_kv_a_norm, mla_w_kvb=m_mla_w_kvb, mla_q_norm=m_mla_q_norm,
                 mla_k_norm=m_mla_k_norm, conv_w=m_conv_w, swa_q_norm=m_swa_q_norm, swa_k_norm=m_swa_k_norm,
                 swa_sinks=m_swa_sinks, w_out=m_w_out)
    mom_v = dict(norm_g=v_norm_g, w_in=v_w_in, mla_q_a_norm=v_mla_q_a_norm, mla_w_qb=v_mla_w_qb,
                 mla_kv_a_norm=v_mla_kv_a_norm, mla_w_kvb=v_mla_w_kvb, mla_q_norm=v_mla_q_norm,
                 mla_k_norm=v_mla_k_norm, conv_w=v_conv_w, swa_q_norm=v_swa_q_norm, swa_k_norm=v_swa_k_norm,
                 swa_sinks=v_swa_sinks, w_out=v_w_out)

    my = _lin(_my_coords())

    def shards(l):
        return [w_in[l].astype(MXU_DTYPE), mla_w_qb[l].astype(MXU_DTYPE), mla_w_kvb[l].astype(MXU_DTYPE),
                w_out[l].astype(MXU_DTYPE), conv_w[l]]

    def inproj_weights(l, g_win):
        return _inproj_weights(l, norm_g, jnp.transpose(g_win, (1, 0, 2)).reshape(D_MODEL, IN_COLS))

    def mixer_weights(l, gathered):
        g_wqb, g_wkvb, g_wout, g_conv = gathered
        return _mixer_weights(
            l, mla_q_a_norm, g_wqb, mla_kv_a_norm, g_wkvb, mla_q_norm, mla_k_norm,
            jnp.transpose(g_conv, (1, 0, 2)).reshape(3, GROUP_WIDTH), swa_q_norm, swa_k_norm, swa_sinks,
            g_wout.reshape(D_MIX, D_MODEL))

    slot_of = dict(
        w_in=lambda g: jnp.transpose(g["w_in"].reshape(D_MODEL, N_DEV, IN_COLS // N_DEV), (1, 0, 2)),
        w_out=lambda g: g["w_out"].reshape(N_DEV, D_MIX // N_DEV, D_MODEL),
        w_qb=lambda g: g["w_qb"], w_kvb=lambda g: g["w_kvb"])

    def own_slot(landed, mine):
        return [lax.dynamic_update_index_in_dim(a, m, my, 0) for a, m in zip(landed, mine)]

    def landed(handle, after, name, mine):
        return own_slot(_push_wait(handle, after, name), mine)

    gather_in0 = _push_start(shards(0)[:1], "weight_gather_in0_start", gather=True)
    gather0 = _push_start(shards(0)[1:], "weight_gather0_start", gather=True)
    gather1 = _push_start(shards(1), "weight_gather1_start", gather=True)
    rope = _rope_tables(T, gather_in0["token"] + gather0["token"] + gather1["token"])
    lw0 = inproj_weights(0, landed(gather_in0, rope[0], "weight_gather_in0_wait", shards(0)[:1])[0])
    x1, sv0 = _layer_fwd(
        x[0], lw0, rope,
        late_weights=lambda proj: mixer_weights(0, landed(gather0, proj, "weight_gather0_wait", shards(0)[1:])))
    g1_all = landed(gather1, x1, "weight_gather1_wait", shards(1))
    (g2, loss_tile), sv1 = _layer_fwd(x1, dict(inproj_weights(1, g1_all[0]), **mixer_weights(1, g1_all[1:])), rope,
                                      target=loss_target[0])

    parts = {(1, "w_in"): ("w_in", "w_out", "w_qb", "w_kvb"), (0, "mixer"): ("w_out", "w_qb", "w_kvb"),
             (0, "w_in"): ("w_in",)}
    started = []

    def start_exchange(l, part, g):
        if (l, part) not in parts:
            return 0.0
        sl = [slot_of[n](g) for n in parts[(l, part)]]
        handle = _push_start(sl, "grad_exchange%d_%s_start" % (l, part), gather=False)
        started.append((l, part, sl, handle))
        return handle["token"]

    g1, grads1 = _layer_bwd(g2, sv1, sv1["lw"], rope, on_big_grads=functools.partial(start_exchange, 1))
    lw0b = dict(sv0["lw"], conv=sv0["lw"]["conv"] + started[0][3]["token"])
    grad_x, grads0 = _layer_bwd(g1, sv0, lw0b, rope, on_big_grads=functools.partial(start_exchange, 0))
    recv = {}
    for l, part, sl, handle in started:
        got = landed(handle, grad_x, "grad_exchange%d_%s_wait" % (l, part), [s[my] for s in sl])
        recv.update({(l, n): a for n, a in zip(parts[(l, part)], got)})
    grads = [grads0, grads1]
    r_win, r_wout, r_wqb, r_wkvb = [jnp.stack([recv[(0, n)], recv[(1, n)]], axis=1)
                                    for n in ("w_in", "w_out", "w_qb", "w_kvb")]

    small = jnp.concatenate([
        _pack_small(lambda name, l: grads[l][SMALL_GRAD_KEY[name]]),
        jnp.stack([g["conv"] for g in grads]).reshape(CONV_ROWS, LANES),
        loss_tile], axis=0)
    small = _small_all_reduce(small)
    loss = small[SMALL_ROWS + CONV_ROWS, 0]
    my = _lin(_my_coords())
    conv_g = lax.dynamic_slice_in_dim(small[SMALL_ROWS:SMALL_ROWS + CONV_ROWS].reshape(DEPTH, 3, GROUP_WIDTH),
                                      my * 64, 64, axis=2)

    out = {}

    def big(name, recv, rows, cols, tr):
        res = _adamw(recv.reshape(N_DEV, rows, cols), weights[name].reshape(rows, cols),
                     mom_m[name].reshape(rows, cols), mom_v[name].reshape(rows, cols), "adamw_" + name, tr)
        out[name] = [r.reshape(weights[name].shape) for r in res]

    big("w_in", r_win, DEPTH * D_MODEL, IN_COLS // N_DEV, 256)
    big("w_out", r_wout, DEPTH * D_MIX // N_DEV, D_MODEL, 192)
    big("mla_w_qb", r_wqb, DEPTH * MLA_Q_LORA, MLA_QK, 512)
    big("mla_w_kvb", r_wkvb, DEPTH * MLA_KV_LORA, 128, 256)

    pad_conv = lambda a: jnp.pad(a.reshape(-1), (0, 8 * LANES - 6 * 64)).reshape(8, LANES)
    cat = lambda src: jnp.concatenate([_pack_small(lambda name, l: src[name][l]), pad_conv(src["conv_w"])], axis=0)
    g_small = jnp.concatenate([small[:SMALL_ROWS], pad_conv(conv_g)], axis=0)
    res = _adamw(g_small[None], cat(weights), cat(mom_m), cat(mom_v), "adamw_small", SMALL_ROWS + 8)
    smalls = [_unpack_small(r[:SMALL_ROWS]) for r in res]
    for name, _ in SMALL:
        out[name] = [s[name] for s in smalls]
    out["conv_w"] = [r[SMALL_ROWS:].reshape(-1)[:6 * 64].reshape(DEPTH, 3, 64) for r in res]

    order = ["norm_g", "w_in", "mla_q_a_norm", "mla_w_qb", "mla_kv_a_norm", "mla_w_kvb", "mla_q_norm", "mla_k_norm",
             "conv_w", "swa_q_norm", "swa_k_norm", "swa_sinks", "w_out"]
    result = [loss, grad_x[None]]
    for idx in range(4):
        result += [out[name][idx] for name in order]
    return tuple(result)
```

```python
import functools

import jax
import jax.numpy as jnp
import numpy as np
from jax import lax
from jax.experimental import pallas as pl
from jax.experimental.pallas import tpu as pltpu

F32 = jnp.float32
MXU_DTYPE = jnp.bfloat16
WIRE_DTYPE = jnp.bfloat16

N_DEV = 8
DEPTH = 2
D_MODEL = 1024
GROUP_WIDTH = 512
D_MIX = 3 * GROUP_WIDTH
BLOCK = 128
RMS_EPS = 1e-6
NEG_INF = -1e30
HEADS = 8
MLA_QK = 96
MLA_NOPE = 64
MLA_ROPE = 32
MLA_Q_LORA = 256
MLA_KV_LORA = 128
ROPE_THETA = 10000.0
SWA_HEAD_DIM = 64
LANES = 128
IN_COLS = 4256

ADAM_LR = 0.001
ADAM_B1 = 0.9
ADAM_B2 = 0.999
ADAM_EPS = 1e-08
ADAM_WD = 0.01
ADAM_STEP = 10

NP = 4352
CB_QLAT = 0
CB_KVLAT = 2
CB_KROPE = 3
CB_GMLA, CB_CH, CB_CB, CB_CC, CB_GCONV, CB_SQ, CB_GSWA = 1, 2, 3, 4, 5, 6, 7
CB_SK, CB_SV = 32, 33

TM_PROJ = 256
TM_ROW = 256
TK = 256
TQ = 2 * TK
MLA_SCALE = MLA_QK ** -0.5
MLA_ONES_ROW = (64, 0)
LOG2E = 1.4426950408889634
LN2 = 0.6931471805599453
TM_SWA = 512
VMEM_MB = 2 ** 20


def _cp(sem, vmem_mb):
    return pltpu.CompilerParams(dimension_semantics=sem, vmem_limit_bytes=vmem_mb * VMEM_MB)


def _sds(shape, dtype):
    return jax.ShapeDtypeStruct(shape, dtype)


def _dot(a, b):
    return jnp.dot(a, b, preferred_element_type=F32)


def _dot_nt(a, b):
    return lax.dot_general(a, b, (((1,), (1,)), ((), ())), preferred_element_type=F32)


def _dot_tn(a, b):
    return lax.dot_general(a, b, (((0,), (0,)), ((), ())), preferred_element_type=F32)


def _rms(x, n):
    r = lax.rsqrt(jnp.sum(x * x, axis=-1, keepdims=True) * (1.0 / n) + RMS_EPS)
    return x * r, r


def _rms_bwd(dy, xhat, r, w, n):
    g = dy * w
    return r * (g - xhat * (jnp.sum(g * xhat, axis=-1, keepdims=True) * (1.0 / n)))


def _rms_halves(x, half1):
    x2 = x * x
    s0 = jnp.sum(jnp.where(half1, 0.0, x2), axis=-1, keepdims=True)
    s1 = jnp.sum(jnp.where(half1, x2, 0.0), axis=-1, keepdims=True)
    r = jnp.where(half1, lax.rsqrt(s1 * (1.0 / 64) + RMS_EPS), lax.rsqrt(s0 * (1.0 / 64) + RMS_EPS))
    return x * r, r


def _rms_halves_bwd(dy, xhat, r, w, half1):
    g = dy * w
    t = g * xhat
    m0 = jnp.sum(jnp.where(half1, 0.0, t), axis=-1, keepdims=True) * (1.0 / 64)
    m1 = jnp.sum(jnp.where(half1, t, 0.0), axis=-1, keepdims=True) * (1.0 / 64)
    return r * (g - xhat * jnp.where(half1, m1, m0))


def _sigmoid(x):
    return 1.0 / (1.0 + jnp.exp(-x))


def _rope(x, c, s1, s2):
    ax = x.ndim - 1
    return x * c + pltpu.roll(x, 112, ax) * s1 + pltpu.roll(x, 16, ax) * s2


def _rope_bwd(dy, c, s1, s2):
    ax = dy.ndim - 1
    return dy * c + pltpu.roll(dy * s1, 16, ax) + pltpu.roll(dy * s2, 112, ax)


def _fold_rows8(x):
    return jnp.sum(x.reshape(x.shape[0] // 8, 8, x.shape[1]), axis=0)


def _row0(v, rows=8):
    row = lax.broadcasted_iota(jnp.int32, (rows, v.shape[1]), 0)
    return jnp.where(row == 0, jnp.broadcast_to(v, (rows, v.shape[1])), 0.0)


def _mm_nn(a, b, name, out_dtype=F32, residual=None, tm=TM_PROJ):
    M, K = a.shape
    N = b.shape[1]
    tm = min(tm, M)

    def body(*refs):
        if residual is None:
            a_ref, b_ref, o_ref = refs
            acc = _dot(a_ref[...].astype(MXU_DTYPE), b_ref[...])
        else:
            a_ref, b_ref, r_ref, o_ref = refs
            acc = _dot(a_ref[...].astype(MXU_DTYPE), b_ref[...]) + r_ref[...]
        o_ref[...] = acc.astype(out_dtype)

    in_specs = [pl.BlockSpec((tm, K), lambda i: (i, 0)), pl.BlockSpec((K, N), lambda i: (0, 0))]
    args = [a, b]
    if residual is not None:
        in_specs.append(pl.BlockSpec((tm, N), lambda i: (i, 0)))
        args.append(residual)
    return pl.pallas_call(
        body, name=name, grid=(M // tm,), in_specs=in_specs,
        out_specs=pl.BlockSpec((tm, N), lambda i: (i, 0)), out_shape=_sds((M, N), out_dtype),
        compiler_params=_cp(("parallel",), 48))(*args)


def _mm_tn(a, b, name, out_dtype, tn, tk=512):
    T, M = a.shape
    N = b.shape[1]
    tk = min(tk, T)
    nk = T // tk

    def body(a_ref, b_ref, o_ref, acc_ref):
        k = pl.program_id(1)

        @pl.when(k == 0)
        def _():
            acc_ref[...] = jnp.zeros_like(acc_ref)

        acc_ref[...] += _dot_tn(a_ref[...].astype(MXU_DTYPE), b_ref[...].astype(MXU_DTYPE))

        @pl.when(k == nk - 1)
        def _():
            o_ref[...] = acc_ref[...].astype(out_dtype)

    return pl.pallas_call(
        body, name=name, grid=(N // tn, nk),
        in_specs=[pl.BlockSpec((tk, M), lambda n, k: (k, 0)), pl.BlockSpec((tk, tn), lambda n, k: (k, n))],
        out_specs=pl.BlockSpec((M, tn), lambda n, k: (0, n)), out_shape=_sds((M, N), out_dtype),
        scratch_shapes=[pltpu.VMEM((M, tn), F32)],
        compiler_params=_cp(("parallel", "arbitrary"), 48))(a, b)


def _inproj_fwd(x, ng, wp):
    T, D = x.shape
    tm = min(TM_PROJ, T)

    def body(x_ref, g_ref, w_ref, proj_ref, h_ref):
        xhat, _ = _rms(x_ref[...], D)
        h = (xhat * g_ref[...]).astype(MXU_DTYPE)
        h_ref[...] = h
        proj_ref[...] = _dot(h, w_ref[...])

    return pl.pallas_call(
        body, name="inproj_fwd", grid=(T // tm,),
        in_specs=[pl.BlockSpec((tm, D), lambda i: (i, 0)), pl.BlockSpec((1, D), lambda i: (0, 0)),
                  pl.BlockSpec((D, NP), lambda i: (0, 0))],
        out_specs=[pl.BlockSpec((tm, NP), lambda i: (i, 0)), pl.BlockSpec((tm, D), lambda i: (i, 0))],
        out_shape=[_sds((T, NP), F32), _sds((T, D), MXU_DTYPE)],
        compiler_params=_cp(("parallel",), 48))(x, ng, wp)


def _mla_prep_fwd(proj, lw, rope):
    T = proj.shape[0]
    tk = min(TK, T // 2)
    nsub = 2
    tm = nsub * tk

    def body(ql_ref, kvl_ref, kr_ref, qa_ref, kva_ref, wq_ref, wk_ref, wv_ref, qn_ref, kn_ref,
             c_ref, s1_ref, s2_ref, q_out, k_out, kt_out, vt_out):
        c, s1, s2 = c_ref[...], s1_ref[...], s2_ref[...]
        qhat, _ = _rms(ql_ref[...], MLA_Q_LORA)
        qn = (qhat * qa_ref[...]).astype(MXU_DTYPE)
        khat, _ = _rms(kvl_ref[...], MLA_KV_LORA)
        kvn = (khat * kva_ref[...]).astype(MXU_DTYPE)
        kr = kr_ref[...]
        half1 = lax.broadcasted_iota(jnp.int32, (tm, LANES), 1) >= 64
        ones_row = lax.broadcasted_iota(jnp.int32, (LANES, 1), 0)
        q3, _ = _rms(jnp.stack([_dot(qn, wq_ref[h]) for h in range(HEADS)]), MLA_QK)
        q_out[...] = (_rope(q3 * qn_ref[...], c, s1, s2) * (MLA_SCALE * LOG2E)).astype(MXU_DTYPE)
        k3, _ = _rms(jnp.stack([_dot(kvn, wk_ref[h]) for h in range(HEADS)]) + kr, MLA_QK)
        k3 = _rope(k3 * kn_ref[...], c, s1, s2)
        k_out[...] = k3.astype(MXU_DTYPE)
        for h in range(HEADS):
            for t in range(nsub):
                kt_out[h, t] = k3[h, tk * t:tk * (t + 1)].T.astype(MXU_DTYPE)
        v = _dot(kvn, wv_ref[...])
        for h in range(HEADS):
            vp = v[:, LANES * (h // 2):LANES * (h // 2 + 1)]
            own = half1 if h % 2 else jnp.logical_not(half1)
            vp = jnp.where(own, vp, 0.0)
            for t in range(nsub):
                vpt = vp[tk * t:tk * (t + 1)].T
                vt_out[h, t] = jnp.where(ones_row == MLA_ONES_ROW[h % 2], 1.0, vpt).astype(MXU_DTYPE)

    full = lambda shape: pl.BlockSpec(shape, lambda i: (0,) * len(shape))
    hd = pl.BlockSpec((HEADS, tm, LANES), lambda i: (0, i, 0))
    hdt = pl.BlockSpec((HEADS, nsub, LANES, tk), lambda i: (0, i, 0, 0))
    nat = _sds((HEADS, T, LANES), MXU_DTYPE)
    tr = _sds((HEADS, T // tk, LANES, tk), MXU_DTYPE)
    return pl.pallas_call(
        body, name="mla_prep_fwd", grid=(T // tm,),
        in_specs=[pl.BlockSpec((tm, 256), lambda i: (i, CB_QLAT)), pl.BlockSpec((tm, LANES), lambda i: (i, CB_KVLAT)),
                  pl.BlockSpec((tm, LANES), lambda i: (i, CB_KROPE)),
                  full((1, 256)), full((1, LANES)), full((HEADS, 256, LANES)), full((HEADS, LANES, LANES)),
                  full((LANES, 512)), full((1, LANES)), full((1, LANES)),
                  pl.BlockSpec((tm, LANES), lambda i: (i, 0)), pl.BlockSpec((tm, LANES), lambda i: (i, 0)),
                  pl.BlockSpec((tm, LANES), lambda i: (i, 0))],
        out_specs=[hd, hd, hdt, hdt],
        out_shape=[nat, nat, tr, tr],
        compiler_params=_cp(("parallel",), 32))(
            proj, proj, proj, lw["qa"], lw["kva"], lw["wq"], lw["wk"], lw["wv"], lw["qn"], lw["kn"],
            rope[0], rope[1], rope[2])


def _mla_attn_fwd(q, k, vt):
    T = q.shape[1]
    tk = min(TK, T // 2)
    tq = 2 * tk

    def body(q_ref, k_ref, vt_ref, o_ref, lse_ref, acc_s, m_s, s_a, s_b):
        i = pl.program_id(1)
        key = lax.broadcasted_iota(jnp.int32, (tk, tq), 0)
        qry = lax.broadcasted_iota(jnp.int32, (tk, tq), 1)
        qs = [q_ref[0], q_ref[1]]
        acc_s[...] = jnp.zeros_like(acc_s)
        m_s[...] = jnp.full(m_s.shape, NEG_INF, F32)

        def scores(kj, buf):
            rows = pl.ds(pl.multiple_of(kj * tk, tk), tk)
            for r in range(2):
                buf[r] = _dot_nt(k_ref[r, rows, :], qs[r])

        def consume(kj, buf, diag):
            for r in range(2):
                s = buf[r]
                if diag is not None:
                    s = jnp.where(key + diag * tk <= qry, s, NEG_INF)
                m_old = m_s[r]
                m_new = jnp.maximum(m_old, jnp.max(s, axis=0, keepdims=True))
                alpha = jnp.exp2(m_old - m_new)
                p = jnp.exp2(s - m_new)
                m_s[r] = m_new
                acc_s[r] = alpha * acc_s[r] + _dot(vt_ref[r, kj], p.astype(MXU_DTYPE))

        scores(0, s_a)

        def pair(kj):
            scores(kj + 1, s_b)
            consume(kj, s_a, None)
            scores(kj + 2, s_a)
            consume(kj + 1, s_b, None)

        def quad(kq, carry):
            pair(4 * kq)
            pair(4 * kq + 2)
            return carry

        lax.fori_loop(0, i // 2, quad, 0)

        @pl.when(i % 2 == 1)
        def _():
            pair(2 * i - 2)

        scores(2 * i + 1, s_b)
        consume(2 * i, s_a, 0)
        consume(2 * i + 1, s_b, 1)
        l = [acc_s[r, pl.ds(MLA_ONES_ROW[r], 1), :] for r in range(2)]
        head0 = lax.broadcasted_iota(jnp.int32, (LANES, 1), 0) < 64
        o_ref[...] = jnp.where(head0, acc_s[0] / l[0], acc_s[1] / l[1]).T
        for r in range(2):
            lse_ref[r] = m_s[r] + jnp.log2(l[r])

    return pl.pallas_call(
        body, name="mla_attn_fwd", grid=(HEADS // 2, T // tq),
        in_specs=[pl.BlockSpec((2, tq, LANES), lambda j, i: (j, i, 0)),
                  pl.BlockSpec((2, T, LANES), lambda j, i: (j, 0, 0)),
                  pl.BlockSpec((2, T // tk, LANES, tk), lambda j, i: (j, 0, 0, 0))],
        out_specs=[pl.BlockSpec((tq, LANES), lambda j, i: (i, j)),
                   pl.BlockSpec((2, 1, tq), lambda j, i: (j, 0, i))],
        out_shape=[_sds((T, GROUP_WIDTH), F32), _sds((HEADS, 1, T), F32)],
        scratch_shapes=[pltpu.VMEM((2, LANES, tq), F32), pltpu.VMEM((2, 1, tq), F32),
                        pltpu.VMEM((2, tk, tq), F32), pltpu.VMEM((2, tk, tq), F32)],
        compiler_params=_cp(("parallel", "arbitrary"), 40))(q, k, vt)


def _swa_kv_variants(x, half1):
    xs = pltpu.roll(x, 64, 1)
    out = {}
    for g in range(2):
        for r in range(2):
            own = half1 if r else jnp.logical_not(half1)
            out[(g, r)] = jnp.where(own, x if g == r else xs, 0.0).astype(MXU_DTYPE)
    return out


def _swa_alibi():
    qi = np.arange(BLOCK)[:, None]
    ki = np.arange(2 * BLOCK)[None, :]
    dist = BLOCK + qi - ki
    slopes = 2.0 ** -(np.arange(HEADS) + 1.0)
    tab = np.where(((dist >= 0) & (dist < BLOCK))[None], slopes[:, None, None] * dist[None], 1e30)
    return jnp.asarray(tab, F32)


def _swa_probs(i, nb, q_ref, k_ref, v_ref, pk_ref, pv_ref, qw_ref, kw_ref, alibi_ref, sink_ref):
    scale = SWA_HEAD_DIM ** -0.5
    half1 = lax.broadcasted_iota(jnp.int32, (1, LANES), 1) >= 64
    k_all = jnp.concatenate([pk_ref[...], k_ref[...]], axis=0)
    v_all = jnp.concatenate([pv_ref[...], v_ref[...]], axis=0)
    khat, _ = _rms_halves(k_all, half1)
    kp = _swa_kv_variants(khat * kw_ref[...], half1)
    vp = _swa_kv_variants(v_all, half1)
    qhat, qr, qn = [], [], []
    for j in range(4):
        xh, r = _rms_halves(q_ref[:, LANES * j:LANES * (j + 1)], half1)
        qhat.append(xh)
        qr.append(r)
        qn.append((xh * qw_ref[...]).astype(MXU_DTYPE))
    ki = lax.broadcasted_iota(jnp.int32, (1, 2 * BLOCK), 1)
    first = jnp.where((i == 0) & (ki < BLOCK), NEG_INF, 0.0)
    s = jnp.stack([_dot_nt(qn[h // 2][BLOCK * b:BLOCK * (b + 1)], kp[(h // 4, h % 2)][BLOCK * b:BLOCK * (b + 2)])
                   for b in range(nb) for h in range(HEADS)]) * scale - alibi_ref[...]
    s = jnp.concatenate([s[:HEADS] + first, s[HEADS:]], axis=0) if nb > 1 else s + first
    sink = jnp.stack([jnp.full((1, 1), sink_ref[h], F32) for _ in range(nb) for h in range(HEADS)])
    m = jnp.maximum(jnp.max(s, axis=-1, keepdims=True), sink)
    e = jnp.exp(s - m)
    es = jnp.exp(sink - m)
    inv = 1.0 / (jnp.sum(e, axis=-1, keepdims=True) + es)
    return e * inv, es * inv, dict(half1=half1, kp=kp, vp=vp, qhat=qhat, qr=qr, qn=qn)


def _swa_fwd(proj, lw):
    T = proj.shape[0]
    tm = min(TM_SWA, T)
    nb = tm // BLOCK

    def body(q_ref, k_ref, v_ref, pk_ref, pv_ref, qw_ref, kw_ref, alibi_ref, sink_ref, o_ref):
        p, _, c = _swa_probs(pl.program_id(0), nb, q_ref, k_ref, v_ref, pk_ref, pv_ref, qw_ref, kw_ref, alibi_ref,
                             sink_ref)
        p = p.astype(MXU_DTYPE)
        for b in range(nb):
            ks = slice(BLOCK * b, BLOCK * (b + 2))
            for j in range(4):
                o_ref[BLOCK * b:BLOCK * (b + 1), LANES * j:LANES * (j + 1)] = (
                    _dot(p[HEADS * b + 2 * j], c["vp"][(j // 2, 0)][ks])
                    + _dot(p[HEADS * b + 2 * j + 1], c["vp"][(j // 2, 1)][ks]))

    prev = lambda cb: pl.BlockSpec((BLOCK, LANES), lambda i: (jnp.maximum(i * nb - 1, 0), cb))
    return pl.pallas_call(
        body, name="swa_fwd", grid=(T // tm,),
        in_specs=[pl.BlockSpec((tm, 512), lambda i: (i, CB_SQ)), pl.BlockSpec((tm, LANES), lambda i: (i, CB_SK)),
                  pl.BlockSpec((tm, LANES), lambda i: (i, CB_SV)), prev(CB_SK), prev(CB_SV),
                  pl.BlockSpec((1, LANES), lambda i: (0, 0)), pl.BlockSpec((1, LANES), lambda i: (0, 0)),
                  pl.BlockSpec((nb * HEADS, BLOCK, 2 * BLOCK), lambda i: (0, 0, 0)),
                  pl.BlockSpec(memory_space=pltpu.SMEM)],
        out_specs=pl.BlockSpec((tm, 512), lambda i: (i, 0)),
        out_shape=_sds((T, GROUP_WIDTH), F32),
        compiler_params=_cp(("parallel",), 40))(
            proj, proj, proj, proj, proj, lw["sqn"], lw["skn"], jnp.tile(_swa_alibi(), (nb, 1, 1)), lw["sinks"])


def _shift_down(u, prev, n, row):
    tm = u.shape[0]
    out = pltpu.roll(u, n, 0)
    row8 = lax.broadcasted_iota(jnp.int32, prev.shape, 0)
    for t in range(n):
        src = jnp.sum(jnp.where(row8 == 8 - n + t, prev, 0.0), axis=0, keepdims=True)
        out = jnp.where(row == t, src, out)
    return out


def _shift_up(u, nxt, n, row):
    tm = u.shape[0]
    out = pltpu.roll(u, tm - n, 0)
    row8 = lax.broadcasted_iota(jnp.int32, nxt.shape, 0)
    for t in range(n):
        src = jnp.sum(jnp.where(row8 == t, nxt, 0.0), axis=0, keepdims=True)
        out = jnp.where(row == tm - n + t, src, out)
    return out


def _mix_fwd(proj, o_mla, o_swa, conv_w):
    T = proj.shape[0]
    tm = min(TM_ROW, T)

    def body(gm_ref, ch_ref, cb_ref, cc_ref, gc_ref, gs_ref, pch_ref, pcc_ref, om_ref, os_ref, w_ref, y_ref):
        i = pl.program_id(0)
        row = lax.broadcasted_iota(jnp.int32, (tm, GROUP_WIDTH), 0)
        u = cc_ref[...] * ch_ref[...]
        u_prev = jnp.where(i > 0, pcc_ref[...] * pch_ref[...], 0.0)
        z = (w_ref[0:1, :] * _shift_down(u, u_prev, 2, row) + w_ref[1:2, :] * _shift_down(u, u_prev, 1, row)
             + w_ref[2:3, :] * u)
        gm, gc, gs = gm_ref[...], gc_ref[...], gs_ref[...]
        y_ref[:, 0:512] = (om_ref[...] * (gm * _sigmoid(gm))).astype(MXU_DTYPE)
        y_ref[:, 512:1024] = (cb_ref[...] * z * (gc * _sigmoid(gc))).astype(MXU_DTYPE)
        y_ref[:, 1024:1536] = (os_ref[...] * (gs * _sigmoid(gs))).astype(MXU_DTYPE)

    blk = lambda cb: pl.BlockSpec((tm, 512), lambda i: (i, cb))
    prev = lambda cb: pl.BlockSpec((8, 512), lambda i: (jnp.maximum(i * (tm // 8) - 1, 0), cb))
    tile = pl.BlockSpec((tm, 512), lambda i: (i, 0))
    return pl.pallas_call(
        body, name="mix_fwd", grid=(T // tm,),
        in_specs=[blk(CB_GMLA), blk(CB_CH), blk(CB_CB), blk(CB_CC), blk(CB_GCONV), blk(CB_GSWA),
                  prev(CB_CH), prev(CB_CC), tile, tile, pl.BlockSpec((8, 512), lambda i: (0, 0))],
        out_specs=pl.BlockSpec((tm, D_MIX), lambda i: (i, 0)),
        out_shape=_sds((T, D_MIX), MXU_DTYPE),
        compiler_params=_cp(("parallel",), 32))(
            proj, proj, proj, proj, proj, proj, proj, proj, o_mla, o_swa, conv_w)


def _outproj_loss(ycat, wo, x, target):
    T, D = x.shape
    K = ycat.shape[1]
    tm = min(TM_PROJ, T)
    nt = T // tm

    def body(y_ref, w_ref, x_ref, t_ref, g_ref, loss_ref, acc_ref):
        i = pl.program_id(0)

        @pl.when(i == 0)
        def _():
            acc_ref[...] = jnp.zeros_like(acc_ref)

        err = _dot(y_ref[...], w_ref[...]) + x_ref[...] - t_ref[...]
        g_ref[...] = err * (1.0 / D)
        acc_ref[...] += _fold_rows8(err * err)

        @pl.when(i == nt - 1)
        def _():
            tot = jnp.sum(jnp.sum(acc_ref[...], axis=1, keepdims=True), axis=0, keepdims=True)
            loss_ref[...] = jnp.broadcast_to(tot * (0.5 / D), (8, LANES))

    tile = pl.BlockSpec((tm, D), lambda i: (i, 0))
    return pl.pallas_call(
        body, name="outproj_loss", grid=(nt,),
        in_specs=[pl.BlockSpec((tm, K), lambda i: (i, 0)), pl.BlockSpec((K, D), lambda i: (0, 0)), tile, tile],
        out_specs=[tile, pl.BlockSpec((8, LANES), lambda i: (0, 0))],
        out_shape=[_sds((T, D), F32), _sds((8, LANES), F32)],
        scratch_shapes=[pltpu.VMEM((8, D), F32)],
        compiler_params=_cp(("arbitrary",), 48))(ycat, wo, x, target)


def _outproj_bwd(g, ycat, wot):
    T, D = g.shape
    K = ycat.shape[1]
    tm = min(512, T)
    nt = T // tm

    def body(g_ref, y_ref, wt_ref, dy_ref, dw_ref, acc_ref):
        i = pl.program_id(0)

        @pl.when(i == 0)
        def _():
            acc_ref[...] = jnp.zeros_like(acc_ref)

        gb = g_ref[...].astype(MXU_DTYPE)
        dy_ref[...] = _dot(gb, wt_ref[...])
        acc_ref[...] += _dot_tn(y_ref[...], gb)

        @pl.when(i == nt - 1)
        def _():
            dw_ref[...] = acc_ref[...].astype(WIRE_DTYPE)

    return pl.pallas_call(
        body, name="outproj_bwd", grid=(nt,),
        in_specs=[pl.BlockSpec((tm, D), lambda i: (i, 0)), pl.BlockSpec((tm, K), lambda i: (i, 0)),
                  pl.BlockSpec((D, K), lambda i: (0, 0))],
        out_specs=[pl.BlockSpec((tm, K), lambda i: (i, 0)), pl.BlockSpec((K, D), lambda i: (0, 0))],
        out_shape=[_sds((T, K), F32), _sds((K, D), WIRE_DTYPE)],
        scratch_shapes=[pltpu.VMEM((K, D), F32)],
        compiler_params=_cp(("arbitrary",), 48))(g, ycat, wot)


def _mix_bwd(dycat, proj, o_mla, o_swa, conv_w):
    T = proj.shape[0]
    tm = min(TM_ROW, T)
    nt = T // tm

    def body(dym_ref, dyc_ref, dys_ref, gm_ref, ch_ref, cb_ref, cc_ref, gc_ref, gs_ref, pch_ref, pcc_ref,
             ndy_ref, ncb_ref, ngc_ref, om_ref, os_ref, w_ref,
             d1_ref, dgs_ref, dom_ref, dos_ref, dw_ref):
        i = pl.program_id(0)

        @pl.when(i == 0)
        def _():
            dw_ref[...] = jnp.zeros_like(dw_ref)

        row = lax.broadcasted_iota(jnp.int32, (tm, GROUP_WIDTH), 0)

        def gate(g):
            sg = _sigmoid(g)
            return g * sg, sg * (1.0 + g * (1.0 - sg))

        gm = gm_ref[...]
        silu, dsilu = gate(gm)
        dym = dym_ref[...]
        dom_ref[...] = dym * silu
        d1_ref[:, 0:512] = (dym * om_ref[...] * dsilu).astype(MXU_DTYPE)

        gs = gs_ref[...]
        silu, dsilu = gate(gs)
        dys = dys_ref[...]
        dos_ref[...] = dys * silu
        dgs_ref[...] = (dys * os_ref[...] * dsilu).astype(MXU_DTYPE)

        ch, cb, cc, gc, dyc = ch_ref[...], cb_ref[...], cc_ref[...], gc_ref[...], dyc_ref[...]
        w0, w1, w2 = w_ref[0:1, :], w_ref[1:2, :], w_ref[2:3, :]
        u = cc * ch
        u_prev = jnp.where(i > 0, pcc_ref[...] * pch_ref[...], 0.0)
        u1 = _shift_down(u, u_prev, 1, row)
        u2 = _shift_down(u, u_prev, 2, row)
        z = w0 * u2 + w1 * u1 + w2 * u
        silu, dsilu = gate(gc)
        dz = dyc * cb * silu
        ngc = ngc_ref[...]
        dz_next = jnp.where(i < nt - 1, ndy_ref[...] * ncb_ref[...] * (ngc * _sigmoid(ngc)), 0.0)
        du = w2 * dz + w1 * _shift_up(dz, dz_next, 1, row) + w0 * _shift_up(dz, dz_next, 2, row)
        d1_ref[:, 512:1024] = (du * cc).astype(MXU_DTYPE)
        d1_ref[:, 1024:1536] = (dyc * z * silu).astype(MXU_DTYPE)
        d1_ref[:, 1536:2048] = (du * ch).astype(MXU_DTYPE)
        d1_ref[:, 2048:2560] = (dyc * cb * z * dsilu).astype(MXU_DTYPE)
        row8 = lax.broadcasted_iota(jnp.int32, (8, GROUP_WIDTH), 0)
        dw = jnp.zeros((8, GROUP_WIDTH), F32)
        for t, shifted in enumerate((u2, u1, u)):
            dw = dw + jnp.where(row8 == t, jnp.sum(dz * shifted, axis=0, keepdims=True), 0.0)
        dw_ref[...] += dw

    blk = lambda cb: pl.BlockSpec((tm, 512), lambda i: (i, cb))
    prev = lambda cb: pl.BlockSpec((8, 512), lambda i: (jnp.maximum(i * (tm // 8) - 1, 0), cb))
    nxt = lambda cb: pl.BlockSpec((8, 512), lambda i: (jnp.minimum((i + 1) * (tm // 8), T // 8 - 1), cb))
    tile = pl.BlockSpec((tm, 512), lambda i: (i, 0))
    return pl.pallas_call(
        body, name="mix_bwd", grid=(nt,),
        in_specs=[blk(0), blk(1), blk(2), blk(CB_GMLA), blk(CB_CH), blk(CB_CB), blk(CB_CC), blk(CB_GCONV),
                  blk(CB_GSWA), prev(CB_CH), prev(CB_CC), nxt(1), nxt(CB_CB), nxt(CB_GCONV), tile, tile,
                  pl.BlockSpec((8, 512), lambda i: (0, 0))],
        out_specs=[pl.BlockSpec((tm, 2560), lambda i: (i, 0)), tile, tile, tile,
                   pl.BlockSpec((8, 512), lambda i: (0, 0))],
        out_shape=[_sds((T, 2560), MXU_DTYPE), _sds((T, 512), MXU_DTYPE), _sds((T, 512), F32),
                   _sds((T, 512), F32), _sds((8, 512), F32)],
        compiler_params=_cp(("arbitrary",), 48))(
            dycat, dycat, dycat, proj, proj, proj, proj, proj, proj, proj, proj, dycat, proj, proj,
            o_mla, o_swa, conv_w)


def _swa_bwd(proj, o_swa, do_swa, lw):
    T = proj.shape[0]
    tm = min(TM_SWA, T)
    nb = tm // BLOCK
    scale = SWA_HEAD_DIM ** -0.5

    def body(q_ref, k_ref, v_ref, pk_ref, pv_ref, o_ref, do_ref, qw_ref, kw_ref, alibi_ref, sink_ref,
             dq_ref, dk_ref, dv_ref, dqw_ref, dsink_ref):
        i = pl.program_id(0)

        @pl.when(i == 0)
        def _():
            dk_ref[...] = jnp.zeros_like(dk_ref)
            dv_ref[...] = jnp.zeros_like(dv_ref)
            dqw_ref[...] = jnp.zeros_like(dqw_ref)
            dsink_ref[...] = jnp.zeros_like(dsink_ref)

        p, p_sink, c = _swa_probs(i, nb, q_ref, k_ref, v_ref, pk_ref, pv_ref, qw_ref, kw_ref, alibi_ref, sink_ref)
        half1, kp, vp, qn, qhat, qr = c["half1"], c["kp"], c["vp"], c["qn"], c["qhat"], c["qr"]
        qw = qw_ref[...]
        rows = [slice(BLOCK * b, BLOCK * (b + 1)) for b in range(nb)]
        keys = [slice(BLOCK * b, BLOCK * (b + 2)) for b in range(nb)]
        dob, dd0, dd1 = [], [], []
        for j in range(4):
            cols = slice(LANES * j, LANES * (j + 1))
            do = do_ref[:, cols]
            dob.append(do.astype(MXU_DTYPE))
            prod = do * o_ref[:, cols]
            dd0.append(jnp.sum(jnp.where(half1, 0.0, prod), axis=-1, keepdims=True))
            dd1.append(jnp.sum(jnp.where(half1, prod, 0.0), axis=-1, keepdims=True))
        dd = jnp.stack([(dd1 if h % 2 else dd0)[h // 2][rows[b]] for b in range(nb) for h in range(HEADS)])
        dp = jnp.stack([_dot_nt(dob[h // 2][rows[b]], vp[(h // 4, h % 2)][keys[b]])
                        for b in range(nb) for h in range(HEADS)])
        ds = (p * (dp - dd) * scale).astype(MXU_DTYPE)
        dsink = -jnp.sum(p_sink * dd, axis=1, keepdims=True)
        pb = p.astype(MXU_DTYPE)

        dqw = jnp.zeros((1, LANES), F32)
        for j in range(4):
            g = j // 2
            dqn = [_dot(ds[HEADS * b + 2 * j], kp[(g, 0)][keys[b]]) + _dot(ds[HEADS * b + 2 * j + 1], kp[(g, 1)][keys[b]])
                   for b in range(nb)]
            dqn = jnp.concatenate(dqn, axis=0) if nb > 1 else dqn[0]
            dqw = dqw + jnp.sum(dqn * qhat[j], axis=0, keepdims=True)
            dq_ref[:, LANES * j:LANES * (j + 1)] = _rms_halves_bwd(dqn, qhat[j], qr[j], qw, half1).astype(MXU_DTYPE)
        dqw_ref[...] += _row0(dqw + pltpu.roll(dqw, 64, 1))

        dk_tot = jnp.zeros((tm + BLOCK, LANES), F32)
        dv_tot = jnp.zeros((tm + BLOCK, LANES), F32)
        for b in range(nb):
            dk_b = jnp.zeros((2 * BLOCK, LANES), F32)
            dv_b = jnp.zeros((2 * BLOCK, LANES), F32)
            for g in range(2):
                for r in range(2):
                    own = half1 if r else jnp.logical_not(half1)
                    ha, hb = HEADS * b + 4 * g + r, HEADS * b + 4 * g + 2 + r
                    qa, qb = qn[2 * g][rows[b]], qn[2 * g + 1][rows[b]]
                    da, db = dob[2 * g][rows[b]], dob[2 * g + 1][rows[b]]
                    dkp = jnp.where(own, _dot_tn(ds[ha], qa) + _dot_tn(ds[hb], qb), 0.0)
                    dvp = jnp.where(own, _dot_tn(pb[ha], da) + _dot_tn(pb[hb], db), 0.0)
                    if g != r:
                        dkp = pltpu.roll(dkp, 64, 1)
                        dvp = pltpu.roll(dvp, 64, 1)
                    dk_b = dk_b + dkp
                    dv_b = dv_b + dvp
            pad = lambda x: jnp.concatenate(
                [z for z in (jnp.zeros((BLOCK * b, LANES), F32), x, jnp.zeros((BLOCK * (nb - 1 - b), LANES), F32))
                 if z.shape[0]], axis=0)
            dk_tot = dk_tot + pad(dk_b)
            dv_tot = dv_tot + pad(dv_b)
        dst = pl.ds(pl.multiple_of(i * tm, BLOCK), tm + BLOCK)
        dk_ref[dst, :] += dk_tot
        dv_ref[dst, :] += dv_tot

        row8 = lax.broadcasted_iota(jnp.int32, (8, LANES), 0)
        dsink_tile = jnp.zeros((8, LANES), F32)
        for b in range(nb):
            for h in range(HEADS):
                dsink_tile = dsink_tile + jnp.where(row8 == h, jnp.broadcast_to(dsink[HEADS * b + h], (8, LANES)), 0.0)
        dsink_ref[...] += dsink_tile

    prev = lambda cb: pl.BlockSpec((BLOCK, LANES), lambda i: (jnp.maximum(i * nb - 1, 0), cb))
    tile = pl.BlockSpec((tm, 512), lambda i: (i, 0))
    small = pl.BlockSpec((8, LANES), lambda i: (0, 0))
    acc = pl.BlockSpec((T + BLOCK, LANES), lambda i: (0, 0))
    return pl.pallas_call(
        body, name="swa_bwd", grid=(T // tm,),
        in_specs=[pl.BlockSpec((tm, 512), lambda i: (i, CB_SQ)), pl.BlockSpec((tm, LANES), lambda i: (i, CB_SK)),
                  pl.BlockSpec((tm, LANES), lambda i: (i, CB_SV)), prev(CB_SK), prev(CB_SV), tile, tile,
                  pl.BlockSpec((1, LANES), lambda i: (0, 0)), pl.BlockSpec((1, LANES), lambda i: (0, 0)),
                  pl.BlockSpec((nb * HEADS, BLOCK, 2 * BLOCK), lambda i: (0, 0, 0)),
                  pl.BlockSpec(memory_space=pltpu.SMEM)],
        out_specs=[tile, acc, acc, small, small],
        out_shape=[_sds((T, 512), MXU_DTYPE), _sds((T + BLOCK, LANES), F32), _sds((T + BLOCK, LANES), F32),
                   _sds((8, LANES), F32), _sds((8, LANES), F32)],
        compiler_params=_cp(("arbitrary",), 48))(
            proj, proj, proj, proj, proj, o_swa, do_swa, lw["sqn"], lw["skn"], jnp.tile(_swa_alibi(), (nb, 1, 1)),
            lw["sinks"])


def _swa_kv_bwd(proj, dkn, dv, lw):
    T = proj.shape[0]
    tm = BLOCK

    def body(k_ref, dkn_ref, dv_ref, kw_ref, d_ref, dkw_ref):
        i = pl.program_id(0)

        @pl.when(i == 0)
        def _():
            dkw_ref[...] = jnp.zeros_like(dkw_ref)

        half1 = lax.broadcasted_iota(jnp.int32, (1, LANES), 1) >= 64
        khat, kr = _rms_halves(k_ref[...], half1)
        dkn_t = dkn_ref[...]
        dkw = jnp.sum(dkn_t * khat, axis=0, keepdims=True)
        dkw_ref[...] += _row0(dkw + pltpu.roll(dkw, 64, 1))
        d_ref[:, 0:LANES] = _rms_halves_bwd(dkn_t, khat, kr, kw_ref[...], half1).astype(MXU_DTYPE)
        d_ref[:, LANES:2 * LANES] = dv_ref[...].astype(MXU_DTYPE)

    return pl.pallas_call(
        body, name="swa_kv_bwd", grid=(T // tm,),
        in_specs=[pl.BlockSpec((tm, LANES), lambda i: (i, CB_SK)), pl.BlockSpec((tm, LANES), lambda i: (i + 1, 0)),
                  pl.BlockSpec((tm, LANES), lambda i: (i + 1, 0)), pl.BlockSpec((1, LANES), lambda i: (0, 0))],
        out_specs=[pl.BlockSpec((tm, 2 * LANES), lambda i: (i, 0)), pl.BlockSpec((8, LANES), lambda i: (0, 0))],
        out_shape=[_sds((T, 2 * LANES), MXU_DTYPE), _sds((8, LANES), F32)],
        compiler_params=_cp(("arbitrary",), 32))(proj, dkn, dv, lw["skn"])


def _mla_attn_bwd(q, k, kt, vt, o, do, lse):
    T = q.shape[1]
    tk = min(TK, T // 2)
    tq = 2 * tk

    def body(q_ref, k_ref, kt_ref, vt_ref, o_ref, do_ref, lse_ref, dq_ref, dk_ref, dv_ref, dq_s, lse_s, dd_s,
             s_a, s_b, p_a, p_b):
        h = pl.program_id(0)
        i = pl.program_id(1)

        @pl.when(i == 0)
        def _():
            dk_ref[...] = jnp.zeros_like(dk_ref)
            dv_ref[...] = jnp.zeros_like(dv_ref)

        qry = lax.broadcasted_iota(jnp.int32, (tq, tk), 0)
        key = lax.broadcasted_iota(jnp.int32, (tq, tk), 1)
        own = (lax.broadcasted_iota(jnp.int32, (1, LANES), 1) // 64) == (h % 2)
        do_own = jnp.where(own, do_ref[...], 0.0)
        dob = do_own.astype(MXU_DTYPE)
        dob_t = do_own.T.astype(MXU_DTYPE)
        qh = q_ref[0]
        qh_t = qh.astype(F32).T.astype(MXU_DTYPE)
        dd_col = jnp.sum(do_own * o_ref[...], axis=-1, keepdims=True)
        lse_col = jnp.broadcast_to(lse_ref[0], (LANES, tq)).T
        for c in range(tk // LANES):
            lse_s[:, LANES * c:LANES * (c + 1)] = lse_col
            dd_s[:, LANES * c:LANES * (c + 1)] = jnp.broadcast_to(dd_col, (tq, LANES))
        dq_s[...] = jnp.zeros_like(dq_s)

        def scores(kj, s_buf, p_buf):
            s_buf[...] = _dot(qh, kt_ref[0, kj])
            p_buf[...] = _dot(dob, vt_ref[0, kj])

        def consume(kj, s_buf, p_buf, diag):
            rows = pl.ds(pl.multiple_of(kj * tk, tk), tk)
            s = s_buf[...]
            if diag is not None:
                s = jnp.where(key + diag * tk <= qry, s, NEG_INF)
            p = jnp.exp2(s - lse_s[...])
            ds = (p * (p_buf[...] - dd_s[...])).astype(MXU_DTYPE)
            dq_s[...] += _dot(ds, k_ref[0, rows, :])
            dk_ref[0, kj] += _dot(qh_t, ds)
            dv_ref[0, kj] += _dot(dob_t, p.astype(MXU_DTYPE))

        scores(0, s_a, p_a)

        def pair(kj):
            scores(kj + 1, s_b, p_b)
            consume(kj, s_a, p_a, None)
            scores(kj + 2, s_a, p_a)
            consume(kj + 1, s_b, p_b, None)

        def quad(kq, carry):
            pair(4 * kq)
            pair(4 * kq + 2)
            return carry

        lax.fori_loop(0, i // 2, quad, 0)

        @pl.when(i % 2 == 1)
        def _():
            pair(2 * i - 2)

        scores(2 * i + 1, s_b, p_b)
        consume(2 * i, s_a, p_a, 0)
        consume(2 * i + 1, s_b, p_b, 1)
        dq_ref[0] = dq_s[...]

    res = pl.BlockSpec((1, T, LANES), lambda h, i: (h, 0, 0))
    res_t = pl.BlockSpec((1, T // tk, LANES, tk), lambda h, i: (h, 0, 0, 0))
    buf = pltpu.VMEM((tq, tk), F32)
    acc_t = _sds((HEADS, T // tk, LANES, tk), F32)
    return pl.pallas_call(
        body, name="mla_attn_bwd", grid=(HEADS, T // tq),
        in_specs=[pl.BlockSpec((1, tq, LANES), lambda h, i: (h, i, 0)), res, res_t, res_t,
                  pl.BlockSpec((tq, LANES), lambda h, i: (i, h // 2)),
                  pl.BlockSpec((tq, LANES), lambda h, i: (i, h // 2)),
                  pl.BlockSpec((1, 1, tq), lambda h, i: (h, 0, i))],
        out_specs=[pl.BlockSpec((1, tq, LANES), lambda h, i: (h, i, 0)), res_t, res_t],
        out_shape=[_sds((HEADS, T, LANES), F32), acc_t, acc_t],
        scratch_shapes=[pltpu.VMEM((tq, LANES), F32), buf, buf, buf, buf, buf, buf],
        compiler_params=_cp(("parallel", "arbitrary"), 48))(q, k, kt, vt, o, do, lse)


def _mla_prep_bwd(proj, dq, dk, dv, lw, rope):
    T = proj.shape[0]
    tm = min(TK, T // 2)

    def body(ql_ref, kvl_ref, kr_ref, dq_ref, dk_ref, dv_ref, qa_ref, kva_ref, wq_ref, wk_ref, wv_ref,
             wqt_ref, wkt_ref, wvt_ref, qn_ref, kn_ref, c_ref, s1_ref, s2_ref,
             d_ref, dwq_ref, dwk_ref, dwv_ref, dqa_ref, dkva_ref, dqn_ref, dkn_ref):
        i = pl.program_id(0)

        @pl.when(i == 0)
        def _():
            for ref in (dwq_ref, dwk_ref, dwv_ref, dqa_ref, dkva_ref, dqn_ref, dkn_ref):
                ref[...] = jnp.zeros_like(ref)

        c, s1, s2 = c_ref[...], s1_ref[...], s2_ref[...]
        lane = lax.broadcasted_iota(jnp.int32, (1, LANES), 1)
        qlhat, qlr = _rms(ql_ref[...], MLA_Q_LORA)
        qn = (qlhat * qa_ref[...]).astype(MXU_DTYPE)
        kvhat, kvr = _rms(kvl_ref[...], MLA_KV_LORA)
        kvn = (kvhat * kva_ref[...]).astype(MXU_DTYPE)
        kr = kr_ref[...]
        x3, r3 = _rms(jnp.stack([_dot(qn, wq_ref[h]) for h in range(HEADS)]), MLA_QK)
        dy3 = _rope_bwd(dq_ref[...] * MLA_SCALE, c, s1, s2)
        dqw = jnp.sum(jnp.sum(dy3 * x3, axis=0), axis=0, keepdims=True)
        dx3 = _rms_bwd(dy3, x3, r3, qn_ref[...], MLA_QK).astype(MXU_DTYPE)
        dqnl = jnp.zeros((tm, MLA_Q_LORA), F32)
        for h in range(HEADS):
            dwq_ref[h] += _dot_tn(qn, dx3[h])
            dqnl = dqnl + _dot(dx3[h], wqt_ref[h])

        x3, r3 = _rms(jnp.stack([_dot(kvn, wk_ref[h]) for h in range(HEADS)]) + kr, MLA_QK)
        dy3 = _rope_bwd(jnp.stack([dk_ref[h, 0].T for h in range(HEADS)]) * LN2, c, s1, s2)
        dkw = jnp.sum(jnp.sum(dy3 * x3, axis=0), axis=0, keepdims=True)
        dxf3 = _rms_bwd(dy3, x3, r3, kn_ref[...], MLA_QK)
        dkr = jnp.sum(dxf3, axis=0)
        dx3 = dxf3.astype(MXU_DTYPE)
        dkvn = jnp.zeros((tm, MLA_KV_LORA), F32)
        for h in range(HEADS):
            dwk_ref[h] += _dot_tn(kvn, dx3[h])
            dkvn = dkvn + _dot(dx3[h], wkt_ref[h])
        dvc = jnp.concatenate([(dv_ref[2 * j, 0] + dv_ref[2 * j + 1, 0]).T for j in range(4)],
                              axis=1).astype(MXU_DTYPE)
        dwv_ref[...] += _dot_tn(kvn, dvc)
        dkvn = dkvn + _dot(dvc, wvt_ref[...])
        dqa_ref[...] += _row0(jnp.sum(dqnl * qlhat, axis=0, keepdims=True))
        dkva_ref[...] += _row0(jnp.sum(dkvn * kvhat, axis=0, keepdims=True))
        dqn_ref[...] += _row0(dqw)
        dkn_ref[...] += _row0(dkw)
        d_ref[:, 0:256] = _rms_bwd(dqnl, qlhat, qlr, qa_ref[...], MLA_Q_LORA).astype(MXU_DTYPE)
        d_ref[:, 256:384] = _rms_bwd(dkvn, kvhat, kvr, kva_ref[...], MLA_KV_LORA).astype(MXU_DTYPE)
        d_ref[:, 384:512] = jnp.where((lane >= 64) & (lane < 96), dkr, 0.0).astype(MXU_DTYPE)

    full = lambda shape: pl.BlockSpec(shape, lambda i: (0,) * len(shape))
    hd = pl.BlockSpec((HEADS, tm, LANES), lambda i: (0, i, 0))
    hdt = pl.BlockSpec((HEADS, 1, LANES, tm), lambda i: (0, i, 0, 0))
    tab = pl.BlockSpec((tm, LANES), lambda i: (i, 0))
    return pl.pallas_call(
        body, name="mla_prep_bwd", grid=(T // tm,),
        in_specs=[pl.BlockSpec((tm, 256), lambda i: (i, CB_QLAT)), pl.BlockSpec((tm, LANES), lambda i: (i, CB_KVLAT)),
                  pl.BlockSpec((tm, LANES), lambda i: (i, CB_KROPE)), hd, hdt, hdt,
                  full((1, 256)), full((1, LANES)), full((HEADS, 256, LANES)), full((HEADS, LANES, LANES)),
                  full((LANES, 512)), full((HEADS, LANES, 256)), full((HEADS, LANES, LANES)), full((512, LANES)),
                  full((1, LANES)), full((1, LANES)), tab, tab, tab],
        out_specs=[pl.BlockSpec((tm, 512), lambda i: (i, 0)), full((HEADS, 256, LANES)),
                   full((HEADS, LANES, LANES)), full((LANES, 512)), full((8, 256)), full((8, LANES)),
                   full((8, LANES)), full((8, LANES))],
        out_shape=[_sds((T, 512), MXU_DTYPE), _sds((HEADS, 256, LANES), F32), _sds((HEADS, LANES, LANES), F32),
                   _sds((LANES, 512), F32), _sds((8, 256), F32), _sds((8, LANES), F32), _sds((8, LANES), F32),
                   _sds((8, LANES), F32)],
        compiler_params=_cp(("arbitrary",), 48))(
            proj, proj, proj, dq, dk, dv, lw["qa"], lw["kva"], lw["wq"], lw["wk"], lw["wv"],
            lw["wqt"], lw["wkt"], lw["wvt"], lw["qn"], lw["kn"], rope[0], rope[1], rope[2])


def _inproj_bwd_dx(dproj, wpt, x, g_in, ng):
    T, D = x.shape
    tm = min(TM_PROJ, T)

    def body(dp_ref, wt_ref, x_ref, g_ref, w_ref, dx_ref, dw_ref):
        i = pl.program_id(0)

        @pl.when(i == 0)
        def _():
            dw_ref[...] = jnp.zeros_like(dw_ref)

        dh = _dot(dp_ref[...], wt_ref[...])
        xhat, r = _rms(x_ref[...], D)
        dw_ref[...] += _row0(jnp.sum(dh * xhat, axis=0, keepdims=True))
        dx_ref[...] = g_ref[...] + _rms_bwd(dh, xhat, r, w_ref[...], D)

    tile = pl.BlockSpec((tm, D), lambda i: (i, 0))
    return pl.pallas_call(
        body, name="inproj_bwd_dx", grid=(T // tm,),
        in_specs=[pl.BlockSpec((tm, NP), lambda i: (i, 0)), pl.BlockSpec((NP, D), lambda i: (0, 0)), tile, tile,
                  pl.BlockSpec((1, D), lambda i: (0, 0))],
        out_specs=[tile, pl.BlockSpec((8, D), lambda i: (0, 0))],
        out_shape=[_sds((T, D), F32), _sds((8, D), F32)],
        compiler_params=_cp(("arbitrary",), 48))(dproj, wpt, x, g_in, ng)


def _rope_tables(T, token=0.0):
    half = MLA_ROPE // 2
    inv_freq = jnp.power(jnp.float32(ROPE_THETA), -jnp.arange(half, dtype=F32) / half)
    ang = (jnp.arange(T, dtype=F32) + token)[:, None] * inv_freq[None, :]
    cos, sin = jnp.cos(ang), jnp.sin(ang)
    z = lambda n: jnp.zeros((T, n), F32)
    c = jnp.concatenate([jnp.ones((T, MLA_NOPE), F32), cos, cos, z(32)], axis=1)
    s1 = jnp.concatenate([z(64), -sin, z(48)], axis=1)
    s2 = jnp.concatenate([z(80), sin, z(32)], axis=1)
    return c, s1, s2


def _pad_lanes(v, n=LANES):
    v = v.reshape(1, -1)
    return jnp.pad(v, ((0, 0), (0, n - v.shape[1])))


def _pack_win(w):
    z = lambda n: jnp.zeros((w.shape[0], n), w.dtype)
    return jnp.concatenate([w[:, 0:384], z(64), w[:, 384:416], z(32), w[:, 416:2976], w[:, 2976:3488],
                            w[:, 3744:4256], w[:, 3488:3616], w[:, 3616:3744]], axis=1)


def _unpack_dwin(d):
    return jnp.concatenate([d[:, 0:384], d[:, 448:480], d[:, 512:3072], d[:, 3072:3584], d[:, 4096:4224],
                            d[:, 4224:4352], d[:, 3584:4096]], axis=1)


def _inproj_weights(l, norm_g, w_in_full):
    wp = _pack_win(w_in_full)
    return dict(ng=norm_g[l].reshape(1, -1), wp=wp, wpt=wp.T)


def _mixer_weights(l, qa, wqb_full, kva, wkvb_full, qn, kn, conv_full, sqn, skn, sinks, w_out_full):
    wq = jnp.pad(wqb_full, ((0, 0), (0, 0), (0, LANES - MLA_QK)))
    wk = jnp.pad(wkvb_full[:, :, :MLA_NOPE], ((0, 0), (0, 0), (0, LANES - MLA_NOPE)))
    wv = jnp.transpose(wkvb_full[:, :, MLA_NOPE:], (1, 0, 2)).reshape(MLA_KV_LORA, GROUP_WIDTH)
    return dict(
        qa=qa[l].reshape(1, -1), kva=kva[l].reshape(1, -1),
        wq=wq, wk=wk, wv=wv, wqt=jnp.transpose(wq, (0, 2, 1)), wkt=jnp.transpose(wk, (0, 2, 1)), wvt=wv.T,
        qn=_pad_lanes(qn[l]), kn=_pad_lanes(kn[l]),
        conv=jnp.pad(conv_full, ((0, 5), (0, 0))),
        sqn=jnp.tile(sqn[l].reshape(1, -1), (1, 2)), skn=jnp.tile(skn[l].reshape(1, -1), (1, 2)),
        sinks=sinks[l], wo=w_out_full, wot=w_out_full.T)


def _layer_weights(l, norm_g, w_in_full, qa, wqb_full, kva, wkvb_full, qn, kn, conv_full, sqn, skn, sinks,
                   w_out_full):
    return dict(_inproj_weights(l, norm_g, w_in_full),
                **_mixer_weights(l, qa, wqb_full, kva, wkvb_full, qn, kn, conv_full, sqn, skn, sinks, w_out_full))


def _layer_fwd(x, lw, rope, late_weights=None, target=None):
    proj, h = _inproj_fwd(x, lw["ng"], lw["wp"])
    if late_weights is not None:
        lw = dict(lw, **late_weights(proj))
    q, k, kt, vt = _mla_prep_fwd(proj, lw, rope)
    o_mla, lse = _mla_attn_fwd(q, k, vt)
    o_swa = _swa_fwd(proj, lw)
    ycat = _mix_fwd(proj, o_mla, o_swa, lw["conv"])
    if target is None:
        out = _mm_nn(ycat, lw["wo"], "outproj_fwd", residual=x)
    else:
        out = _outproj_loss(ycat, lw["wo"], x, target)
    return out, dict(x=x, proj=proj, h=h, q=q, k=k, kt=kt, vt=vt, o_mla=o_mla, lse=lse, o_swa=o_swa, ycat=ycat,
                     lw=lw)


def _layer_bwd(g, sv, lw, rope, on_big_grads=None):
    proj = sv["proj"]
    dycat, d_wo = _outproj_bwd(g, sv["ycat"], lw["wot"])
    d1, dgs, do_mla, do_swa, d_conv = _mix_bwd(dycat, proj, sv["o_mla"], sv["o_swa"], lw["conv"])
    dsq, dkn_acc, dv_acc, d_sqn, d_sinks = _swa_bwd(proj, sv["o_swa"], do_swa, lw)
    dskv, d_skn = _swa_kv_bwd(proj, dkn_acc, dv_acc, lw)
    dq, dk, dv = _mla_attn_bwd(sv["q"], sv["k"], sv["kt"], sv["vt"], sv["o_mla"], do_mla, sv["lse"])
    dmla, d_wq, d_wk, d_wv, d_qa, d_kva, d_qn, d_kn = _mla_prep_bwd(proj, dq, dk, dv, lw, rope)
    grads = dict(
        w_out=d_wo, w_qb=d_wq[:, :, :MLA_QK],
        w_kvb=jnp.concatenate([d_wk[:, :, :MLA_NOPE],
                               jnp.transpose(d_wv.reshape(MLA_KV_LORA, HEADS, MLA_NOPE), (1, 0, 2))], axis=2))
    token = 0.0 if on_big_grads is None else on_big_grads("mixer", grads)
    dproj = jnp.concatenate([dmla, d1, dsq, dgs, dskv], axis=1)
    d_wp = _mm_tn(sv["h"], dproj, "inproj_bwd_dw", WIRE_DTYPE, tn=NP // 2)
    grads["w_in"] = _unpack_dwin(d_wp)
    token = token if on_big_grads is None else token + on_big_grads("w_in", grads)
    dx, d_ng = _inproj_bwd_dx(dproj, lw["wpt"], sv["x"], g, lw["ng"] + token)
    grads.update(
        conv=d_conv[0:3], norm_g=d_ng[0], qa=d_qa[0], kva=d_kva[0], qn=d_qn[0, :MLA_QK], kn=d_kn[0, :MLA_QK],
        sqn=d_sqn[0, :SWA_HEAD_DIM], skn=d_skn[0, :SWA_HEAD_DIM], sinks=d_sinks[:, 0])
    return dx, grads


def _local_step(x, target, lws, rope):
    saved = []
    for l, lw in enumerate(lws):
        x, sv = _layer_fwd(x, lw, rope, target=target if l == len(lws) - 1 else None)
        saved.append(sv)
    g, loss_tile = x
    grads = [None] * len(lws)
    for l in reversed(range(len(lws))):
        g, grads[l] = _layer_bwd(g, saved[l], lws[l], rope)
    return loss_tile, g, grads


def _my_coords():
    return lax.axis_index("x"), lax.axis_index("y"), lax.axis_index("c")


def _peer(me, k):
    x, y, c = me
    return (1 - x if k & 4 else x, 1 - y if k & 2 else y, 1 - c if k & 1 else c)


def _lin(d):
    return 4 * d[0] + 2 * d[1] + d[2]


def _push_copies(ins, lands, send_sems, recv_sems, gather):
    me = _my_coords()
    my = _lin(me)
    out, inc = [], []
    for a in range(len(ins)):
        for k in range(1, N_DEV):
            peer = _peer(me, k)
            sems = dict(send_sem=send_sems.at[a * 7 + k - 1], recv_sem=recv_sems.at[a * 7 + k - 1],
                        device_id=peer, device_id_type=pl.DeviceIdType.MESH)
            src = ins[a] if gather else ins[a].at[_lin(peer)]
            out.append(pltpu.make_async_remote_copy(src_ref=src, dst_ref=lands[a].at[my], **sems))
            inc.append(pltpu.make_async_remote_copy(src_ref=src, dst_ref=lands[a].at[_lin(peer)], **sems))
    return out, inc


def _push_start(arrays, name, gather):
    n = len(arrays)
    land_shapes = [((N_DEV,) + a.shape) if gather else a.shape for a in arrays]

    def body(*refs):
        ins, lands = refs[:n], refs[n:2 * n]
        send_sems, recv_sems = refs[2 * n], refs[2 * n + 1]
        token = refs[-1]
        out, _ = _push_copies(ins, lands, send_sems, recv_sems, gather)
        for cp in out:
            cp.start()
        token[...] = jnp.zeros_like(token)

    hbm = pl.BlockSpec(memory_space=pltpu.HBM)
    sem = pl.BlockSpec(memory_space=pltpu.SEMAPHORE)
    res = pl.pallas_call(
        body, name=name,
        out_shape=(pltpu.SemaphoreType.DMA((7 * n,)), pltpu.SemaphoreType.DMA((7 * n,)),
                   *[pltpu.HBM(a.shape, a.dtype) for a in arrays],
                   *[pltpu.HBM(s, a.dtype) for s, a in zip(land_shapes, arrays)],
                   _sds((8, LANES), F32)),
        in_specs=(hbm,) * (2 * n),
        out_specs=(sem, sem) + (hbm,) * (2 * n) + (pl.BlockSpec(memory_space=pltpu.VMEM),),
        input_output_aliases={i: 2 + i for i in range(2 * n)},
        compiler_params=pltpu.CompilerParams(has_side_effects=pltpu.SideEffectType.DATAFLOW_SIDE_EFFECTING),
    )(*[pltpu.with_memory_space_constraint(a, pltpu.HBM) for a in arrays],
      *[pltpu.with_memory_space_constraint(lax.empty(s, a.dtype), pltpu.HBM) for s, a in zip(land_shapes, arrays)])
    return dict(send=res[0], recv=res[1], src=res[2:2 + n], land=res[2 + n:2 + 2 * n], token=res[-1][0, 0],
                gather=gather)


def _push_wait(handle, after, name):
    n = len(handle["src"])
    gather = handle["gather"]

    def body(*refs):
        ins, lands = refs[:n], refs[n:2 * n]
        send_sems, recv_sems = refs[2 * n], refs[2 * n + 1]
        out, inc = _push_copies(ins, lands, send_sems, recv_sems, gather)
        for cp in out:
            cp.wait_send()
        for cp in inc:
            cp.wait_recv()

    hbm = pl.BlockSpec(memory_space=pltpu.HBM)
    sem = pl.BlockSpec(memory_space=pltpu.SEMAPHORE)
    res = pl.pallas_call(
        body, name=name,
        out_shape=tuple(pltpu.HBM(a.shape, a.dtype) for a in (*handle["src"], *handle["land"])),
        in_specs=(hbm,) * (2 * n) + (sem, sem, pl.BlockSpec(memory_space=pl.ANY)),
        out_specs=(hbm,) * (2 * n),
        input_output_aliases={i: i for i in range(2 * n)},
        compiler_params=pltpu.CompilerParams(has_side_effects=pltpu.SideEffectType.DATAFLOW_SIDE_EFFECTING),
    )(*handle["src"], *handle["land"], handle["send"], handle["recv"], after)
    return res[n:]


def _small_all_reduce(v):
    R = v.shape[0]

    def body(v_ref, o_ref, buf, send_sems, recv_sems):
        me = _my_coords()
        my = _lin(me)
        sends = []
        for k in range(1, N_DEV):
            cp = pltpu.make_async_remote_copy(
                src_ref=v_ref, dst_ref=buf.at[my], send_sem=send_sems.at[k - 1], recv_sem=recv_sems.at[k - 1],
                device_id=_peer(me, k), device_id_type=pl.DeviceIdType.MESH)
            cp.start()
            sends.append(cp)
        buf[my] = v_ref[...]
        for k in range(1, N_DEV):
            pltpu.make_async_remote_copy(
                src_ref=v_ref, dst_ref=buf.at[_lin(_peer(me, k))], send_sem=send_sems.at[k - 1],
                recv_sem=recv_sems.at[k - 1], device_id=_peer(me, k),
                device_id_type=pl.DeviceIdType.MESH).wait_recv()
        for cp in sends:
            cp.wait_send()
        tot = buf[0]
        for d in range(1, N_DEV):
            tot = tot + buf[d]
        o_ref[...] = tot

    vm = pl.BlockSpec(memory_space=pltpu.VMEM)
    return pl.pallas_call(
        body, name="small_all_reduce", in_specs=[vm], out_specs=vm, out_shape=_sds(v.shape, F32),
        scratch_shapes=[pltpu.VMEM((N_DEV, R, LANES), F32), pltpu.SemaphoreType.DMA((7,)),
                        pltpu.SemaphoreType.DMA((7,))],
    )(v)


def _adamw_math(w, g, m, v):
    m = ADAM_B1 * m + (1.0 - ADAM_B1) * g
    v = ADAM_B2 * v + (1.0 - ADAM_B2) * (g * g)
    m_hat = m / (1.0 - ADAM_B1 ** ADAM_STEP)
    v_hat = v / (1.0 - ADAM_B2 ** ADAM_STEP)
    delta = -ADAM_LR * (m_hat / (jnp.sqrt(v_hat) + ADAM_EPS) + ADAM_WD * w)
    return delta, m, v


def _adamw(parts, w, m, v, name, tr):
    P, R, C = parts.shape
    tr = min(tr, R)

    def body(p_ref, w_ref, m_ref, v_ref, g_out, d_out, m_out, v_out):
        g = p_ref[0].astype(F32)
        for d in range(1, P):
            g = g + p_ref[d].astype(F32)
        delta, m_new, v_new = _adamw_math(w_ref[...], g, m_ref[...], v_ref[...])
        g_out[...] = g
        d_out[...] = delta
        m_out[...] = m_new
        v_out[...] = v_new

    tile = pl.BlockSpec((tr, C), lambda i: (i, 0))
    return pl.pallas_call(
        body, name=name, grid=(R // tr,),
        in_specs=[pl.BlockSpec((P, tr, C), lambda i: (0, i, 0)), tile, tile, tile],
        out_specs=[tile] * 4, out_shape=[_sds((R, C), F32)] * 4,
        compiler_params=_cp(("parallel",), 32))(parts, w, m, v)


SMALL = (("norm_g", D_MODEL), ("mla_q_a_norm", MLA_Q_LORA), ("mla_kv_a_norm", MLA_KV_LORA), ("mla_q_norm", MLA_QK),
         ("mla_k_norm", MLA_QK), ("swa_q_norm", SWA_HEAD_DIM), ("swa_k_norm", SWA_HEAD_DIM), ("swa_sinks", HEADS))
SMALL_GRAD_KEY = dict(norm_g="norm_g", mla_q_a_norm="qa", mla_kv_a_norm="kva", mla_q_norm="qn", mla_k_norm="kn",
                      swa_q_norm="sqn", swa_k_norm="skn", swa_sinks="sinks")
SMALL_ROWS = 32
CONV_ROWS = 24


def _pack_small(get):
    parts = []
    for l in range(DEPTH):
        for name, n in SMALL:
            v = get(name, l).reshape(-1)
            parts.append(jnp.pad(v, (0, (-n) % LANES)))
    return jnp.concatenate(parts).reshape(SMALL_ROWS, LANES)


def _unpack_small(packed):
    flat = packed.reshape(-1)
    out = {name: [] for name, _ in SMALL}
    off = 0
    for l in range(DEPTH):
        for name, n in SMALL:
            out[name].append(flat[off:off + n])
            off += n + (-n) % LANES
    return {name: jnp.stack(v) for name, v in out.items()}


def kernel(x, norm_g, w_in, mla_q_a_norm, mla_w_qb, mla_kv_a_norm, mla_w_kvb, mla_q_norm, mla_k_norm, conv_w, swa_q_norm, swa_k_norm, swa_sinks, w_out, loss_target, m_norm_g, m_w_in, m_mla_q_a_norm, m_mla_w_qb, m_mla_kv_a_norm, m_mla_w_kvb, m_mla_q_norm, m_mla_k_norm, m_conv_w, m_swa_q_norm, m_swa_k_norm, m_swa_sinks, m_w_out, v_norm_g, v_w_in, v_mla_q_a_norm, v_mla_w_qb, v_mla_kv_a_norm, v_mla_w_kvb, v_mla_q_norm, v_mla_k_norm, v_conv_w, v_swa_q_norm, v_swa_k_norm, v_swa_sinks, v_w_out):
    T = x.shape[1]
    weights = dict(norm_g=norm_g, w_in=w_in, mla_q_a_norm=mla_q_a_norm, mla_w_qb=mla_w_qb,
                   mla_kv_a_norm=mla_kv_a_norm, mla_w_kvb=mla_w_kvb, mla_q_norm=mla_q_norm, mla_k_norm=mla_k_norm,
                   conv_w=conv_w, swa_q_norm=swa_q_norm, swa_k_norm=swa_k_norm, swa_sinks=swa_sinks, w_out=w_out)
    mom_m = dict(norm_g=m_norm_g, w_in=m_w_in, mla_q_a_norm=m_mla_q_a_norm, mla_w_qb=m_mla_w_qb,
                 mla_kv_a_norm=m_mla_kv_a_norm, mla_w_kvb=m_mla_w_kvb, mla_q_norm=m_mla_q_norm,
                 mla_k_norm=m_mla_k_norm, conv_w=m_conv_w, swa_q_norm=m_swa_q_norm, swa_k_norm=m_swa_k_norm,
                 swa_sinks=m_swa_sinks, w_out=m_w_out)
    mom_v = dict(norm_g=v_norm_g, w_in=v_w_in, mla_q_a_norm=v_mla_q_a_norm, mla_w_qb=v_mla_w_qb,
                 mla_kv_a_norm=v_mla_kv_a_norm, mla_w_kvb=v_mla_w_kvb, mla_q_norm=v_mla_q_norm,
                 mla_k_norm=v_mla_k_norm, conv_w=v_conv_w, swa_q_norm=v_swa_q_norm, swa_k_norm=v_swa_k_norm,
                 swa_sinks=v_swa_sinks, w_out=v_w_out)

    my = _lin(_my_coords())

    def shards(l):
        return [w_in[l].astype(MXU_DTYPE), mla_w_qb[l].astype(MXU_DTYPE), mla_w_kvb[l].astype(MXU_DTYPE),
                w_out[l].astype(MXU_DTYPE), conv_w[l]]

    def inproj_weights(l, g_win):
        return _inproj_weights(l, norm_g, jnp.transpose(g_win, (1, 0, 2)).reshape(D_MODEL, IN_COLS))

    def mixer_weights(l, gathered):
        g_wqb, g_wkvb, g_wout, g_conv = gathered
        return _mixer_weights(
            l, mla_q_a_norm, g_wqb, mla_kv_a_norm, g_wkvb, mla_q_norm, mla_k_norm,
            jnp.transpose(g_conv, (1, 0, 2)).reshape(3, GROUP_WIDTH), swa_q_norm, swa_k_norm, swa_sinks,
            g_wout.reshape(D_MIX, D_MODEL))

    slot_of = dict(
        w_in=lambda g: jnp.transpose(g["w_in"].reshape(D_MODEL, N_DEV, IN_COLS // N_DEV), (1, 0, 2)),
        w_out=lambda g: g["w_out"].reshape(N_DEV, D_MIX // N_DEV, D_MODEL),
        w_qb=lambda g: g["w_qb"], w_kvb=lambda g: g["w_kvb"])

    def own_slot(landed, mine):
        return [lax.dynamic_update_index_in_dim(a, m, my, 0) for a, m in zip(landed, mine)]

    def landed(handle, after, name, mine):
        return own_slot(_push_wait(handle, after, name), mine)

    gather_in0 = _push_start(shards(0)[:1], "weight_gather_in0_start", gather=True)
    gather0 = _push_start(shards(0)[1:], "weight_gather0_start", gather=True)
    gather1 = _push_start(shards(1), "weight_gather1_start", gather=True)
    rope = _rope_tables(T, gather_in0["token"] + gather0["token"] + gather1["token"])
    lw0 = inproj_weights(0, landed(gather_in0, rope[0], "weight_gather_in0_wait", shards(0)[:1])[0])
    x1, sv0 = _layer_fwd(
        x[0], lw0, rope,
        late_weights=lambda proj: mixer_weights(0, landed(gather0, proj, "weight_gather0_wait", shards(0)[1:])))
    g1_all = landed(gather1, x1, "weight_gather1_wait", shards(1))
    (g2, loss_tile), sv1 = _layer_fwd(x1, dict(inproj_weights(1, g1_all[0]), **mixer_weights(1, g1_all[1:])), rope,
                                      target=loss_target[0])

    parts = {(1, "w_in"): ("w_in", "w_out", "w_qb", "w_kvb"), (0, "mixer"): ("w_out", "w_qb", "w_kvb"),
             (0, "w_in"): ("w_in",)}
    started = []

    def start_exchange(l, part, g):
        if (l, part) not in parts:
            return 0.0
        sl = [slot_of[n](g) for n in parts[(l, part)]]
        handle = _push_start(sl, "grad_exchange%d_%s_start" % (l, part), gather=False)
        started.append((l, part, sl, handle))
        return handle["token"]

    g1, grads1 = _layer_bwd(g2, sv1, sv1["lw"], rope, on_big_grads=functools.partial(start_exchange, 1))
    lw0b = dict(sv0["lw"], conv=sv0["lw"]["conv"] + started[0][3]["token"])
    grad_x, grads0 = _layer_bwd(g1, sv0, lw0b, rope, on_big_grads=functools.partial(start_exchange, 0))
    recv = {}
    for l, part, sl, handle in started:
        got = landed(handle, grad_x, "grad_exchange%d_%s_wait" % (l, part), [s[my] for s in sl])
        recv.update({(l, n): a for n, a in zip(parts[(l, part)], got)})
    grads = [grads0, grads1]
    r_win, r_wout, r_wqb, r_wkvb = [jnp.stack([recv[(0, n)], recv[(1, n)]], axis=1)
                                    for n in ("w_in", "w_out", "w_qb", "w_kvb")]

    small = jnp.concatenate([
        _pack_small(lambda name, l: grads[l][SMALL_GRAD_KEY[name]]),
        jnp.stack([g["conv"] for g in grads]).reshape(CONV_ROWS, LANES),
        loss_tile], axis=0)
    small = _small_all_reduce(small)
    loss = small[SMALL_ROWS + CONV_ROWS, 0]
    my = _lin(_my_coords())
    conv_g = lax.dynamic_slice_in_dim(small[SMALL_ROWS:SMALL_ROWS + CONV_ROWS].reshape(DEPTH, 3, GROUP_WIDTH),
                                      my * 64, 64, axis=2)

    out = {}

    def big(name, recv, rows, cols, tr):
        res = _adamw(recv.reshape(N_DEV, rows, cols), weights[name].reshape(rows, cols),
                     mom_m[name].reshape(rows, cols), mom_v[name].reshape(rows, cols), "adamw_" + name, tr)
        out[name] = [r.reshape(weights[name].shape) for r in res]

    big("w_in", r_win, DEPTH * D_MODEL, IN_COLS // N_DEV, 256)
    big("w_out", r_wout, DEPTH * D_MIX // N_DEV, D_MODEL, 192)
    big("mla_w_qb", r_wqb, DEPTH * MLA_Q_LORA, MLA_QK, 512)
    big("mla_w_kvb", r_wkvb, DEPTH * MLA_KV_LORA, 128, 256)

    pad_conv = lambda a: jnp.pad(a.reshape(-1), (0, 8 * LANES - 6 * 64)).reshape(8, LANES)
    cat = lambda src: jnp.concatenate([_pack_small(lambda name, l: src[name][l]), pad_conv(src["conv_w"])], axis=0)
    g_small = jnp.concatenate([small[:SMALL_ROWS], pad_conv(conv_g)], axis=0)
    res = _adamw(g_small[None], cat(weights), cat(mom_m), cat(mom_v), "adamw_small", SMALL_ROWS + 8)
    smalls = [_unpack_small(r[:SMALL_ROWS]) for r in res]
    for name, _ in SMALL:
        out[name] = [s[name] for s in smalls]
    out["conv_w"] = [r[SMALL_ROWS:].reshape(-1)[:6 * 64].reshape(DEPTH, 3, 64) for r in res]

    order = ["norm_g", "w_in", "mla_q_a_norm", "mla_w_qb", "mla_kv_a_norm", "mla_w_kvb", "mla_q_norm", "mla_k_norm",
             "conv_w", "swa_q_norm", "swa_k_norm", "swa_sinks", "w_out"]
    result = [loss, grad_x[None]]
    for idx in range(4):
        result += [out[name][idx] for name in order]
    return tuple(result)
```

```python
import functools

import jax
import jax.numpy as jnp
import numpy as np
from jax import lax
from jax.experimental import pallas as pl
from jax.experimental.pallas import tpu as pltpu

F32 = jnp.float32
MXU_DTYPE = jnp.bfloat16
WIRE_DTYPE = jnp.bfloat16

N_DEV = 8
DEPTH = 2
D_MODEL = 1024
GROUP_WIDTH = 512
D_MIX = 3 * GROUP_WIDTH
BLOCK = 128
RMS_EPS = 1e-6
NEG_INF = -1e30
HEADS = 8
MLA_QK = 96
MLA_NOPE = 64
MLA_ROPE = 32
MLA_Q_LORA = 256
MLA_KV_LORA = 128
ROPE_THETA = 10000.0
SWA_HEAD_DIM = 64
LANES = 128
IN_COLS = 4256

ADAM_LR = 0.001
ADAM_B1 = 0.9
ADAM_B2 = 0.999
ADAM_EPS = 1e-08
ADAM_WD = 0.01
ADAM_STEP = 10

NP = 4352
CB_QLAT = 0
CB_KVLAT = 2
CB_KROPE = 3
CB_GMLA, CB_CH, CB_CB, CB_CC, CB_GCONV, CB_SQ, CB_GSWA = 1, 2, 3, 4, 5, 6, 7
CB_SK, CB_SV = 32, 33

TM_PROJ = 256
TM_ROW = 256
TK = 256
TQ = 2 * TK
MLA_SCALE = MLA_QK ** -0.5
MLA_ONES_ROW = (64, 0)
LOG2E = 1.4426950408889634
LN2 = 0.6931471805599453
TM_SWA = 512
VMEM_MB = 2 ** 20


def _cp(sem, vmem_mb):
    return pltpu.CompilerParams(dimension_semantics=sem, vmem_limit_bytes=vmem_mb * VMEM_MB)


def _sds(shape, dtype):
    return jax.ShapeDtypeStruct(shape, dtype)


def _dot(a, b):
    return jnp.dot(a, b, preferred_element_type=F32)


def _dot_nt(a, b):
    return lax.dot_general(a, b, (((1,), (1,)), ((), ())), preferred_element_type=F32)


def _dot_tn(a, b):
    return lax.dot_general(a, b, (((0,), (0,)), ((), ())), preferred_element_type=F32)


def _rms(x, n):
    r = lax.rsqrt(jnp.sum(x * x, axis=-1, keepdims=True) * (1.0 / n) + RMS_EPS)
    return x * r, r


def _rms_bwd(dy, xhat, r, w, n):
    g = dy * w
    return r * (g - xhat * (jnp.sum(g * xhat, axis=-1, keepdims=True) * (1.0 / n)))


def _rms_halves(x, half1):
    x2 = x * x
    s0 = jnp.sum(jnp.where(half1, 0.0, x2), axis=-1, keepdims=True)
    s1 = jnp.sum(jnp.where(half1, x2, 0.0), axis=-1, keepdims=True)
    r = jnp.where(half1, lax.rsqrt(s1 * (1.0 / 64) + RMS_EPS), lax.rsqrt(s0 * (1.0 / 64) + RMS_EPS))
    return x * r, r


def _rms_halves_bwd(dy, xhat, r, w, half1):
    g = dy * w
    t = g * xhat
    m0 = jnp.sum(jnp.where(half1, 0.0, t), axis=-1, keepdims=True) * (1.0 / 64)
    m1 = jnp.sum(jnp.where(half1, t, 0.0), axis=-1, keepdims=True) * (1.0 / 64)
    return r * (g - xhat * jnp.where(half1, m1, m0))


def _sigmoid(x):
    return 1.0 / (1.0 + jnp.exp(-x))


def _rope(x, c, s1, s2):
    ax = x.ndim - 1
    return x * c + pltpu.roll(x, 112, ax) * s1 + pltpu.roll(x, 16, ax) * s2


def _rope_bwd(dy, c, s1, s2):
    ax = dy.ndim - 1
    return dy * c + pltpu.roll(dy * s1, 16, ax) + pltpu.roll(dy * s2, 112, ax)


def _fold_rows8(x):
    return jnp.sum(x.reshape(x.shape[0] // 8, 8, x.shape[1]), axis=0)


def _row0(v, rows=8):
    row = lax.broadcasted_iota(jnp.int32, (rows, v.shape[1]), 0)
    return jnp.where(row == 0, jnp.broadcast_to(v, (rows, v.shape[1])), 0.0)


def _mm_nn(a, b, name, out_dtype=F32, residual=None, tm=TM_PROJ):
    M, K = a.shape
    N = b.shape[1]
    tm = min(tm, M)

    def body(*refs):
        if residual is None:
            a_ref, b_ref, o_ref = refs
            acc = _dot(a_ref[...].astype(MXU_DTYPE), b_ref[...])
        else:
            a_ref, b_ref, r_ref, o_ref = refs
            acc = _dot(a_ref[...].astype(MXU_DTYPE), b_ref[...]) + r_ref[...]
        o_ref[...] = acc.astype(out_dtype)

    in_specs = [pl.BlockSpec((tm, K), lambda i: (i, 0)), pl.BlockSpec((K, N), lambda i: (0, 0))]
    args = [a, b]
    if residual is not None:
        in_specs.append(pl.BlockSpec((tm, N), lambda i: (i, 0)))
        args.append(residual)
    return pl.pallas_call(
        body, name=name, grid=(M // tm,), in_specs=in_specs,
        out_specs=pl.BlockSpec((tm, N), lambda i: (i, 0)), out_shape=_sds((M, N), out_dtype),
        compiler_params=_cp(("parallel",), 48))(*args)


def _mm_tn(a, b, name, out_dtype, tn, tk=512):
    T, M = a.shape
    N = b.shape[1]
    tk = min(tk, T)
    nk = T // tk

    def body(a_ref, b_ref, o_ref, acc_ref):
        k = pl.program_id(1)

        @pl.when(k == 0)
        def _():
            acc_ref[...] = jnp.zeros_like(acc_ref)

        acc_ref[...] += _dot_tn(a_ref[...].astype(MXU_DTYPE), b_ref[...].astype(MXU_DTYPE))

        @pl.when(k == nk - 1)
        def _():
            o_ref[...] = acc_ref[...].astype(out_dtype)

    return pl.pallas_call(
        body, name=name, grid=(N // tn, nk),
        in_specs=[pl.BlockSpec((tk, M), lambda n, k: (k, 0)), pl.BlockSpec((tk, tn), lambda n, k: (k, n))],
        out_specs=pl.BlockSpec((M, tn), lambda n, k: (0, n)), out_shape=_sds((M, N), out_dtype),
        scratch_shapes=[pltpu.VMEM((M, tn), F32)],
        compiler_params=_cp(("parallel", "arbitrary"), 48))(a, b)


def _inproj_fwd(x, ng, wp):
    T, D = x.shape
    tm = min(TM_PROJ, T)

    def body(x_ref, g_ref, w_ref, proj_ref, h_ref):
        xhat, _ = _rms(x_ref[...], D)
        h = (xhat * g_ref[...]).astype(MXU_DTYPE)
        h_ref[...] = h
        proj_ref[...] = _dot(h, w_ref[...])

    return pl.pallas_call(
        body, name="inproj_fwd", grid=(T // tm,),
        in_specs=[pl.BlockSpec((tm, D), lambda i: (i, 0)), pl.BlockSpec((1, D), lambda i: (0, 0)),
                  pl.BlockSpec((D, NP), lambda i: (0, 0))],
        out_specs=[pl.BlockSpec((tm, NP), lambda i: (i, 0)), pl.BlockSpec((tm, D), lambda i: (i, 0))],
        out_shape=[_sds((T, NP), F32), _sds((T, D), MXU_DTYPE)],
        compiler_params=_cp(("parallel",), 48))(x, ng, wp)


def _mla_prep_fwd(proj, lw, rope):
    T = proj.shape[0]
    tk = min(TK, T // 2)
    nsub = 2
    tm = nsub * tk

    def body(ql_ref, kvl_ref, kr_ref, qa_ref, kva_ref, wq_ref, wk_ref, wv_ref, qn_ref, kn_ref,
             c_ref, s1_ref, s2_ref, q_out, k_out, kt_out, vt_out):
        c, s1, s2 = c_ref[...], s1_ref[...], s2_ref[...]
        qhat, _ = _rms(ql_ref[...], MLA_Q_LORA)
        qn = (qhat * qa_ref[...]).astype(MXU_DTYPE)
        khat, _ = _rms(kvl_ref[...], MLA_KV_LORA)
        kvn = (khat * kva_ref[...]).astype(MXU_DTYPE)
        kr = kr_ref[...]
        half1 = lax.broadcasted_iota(jnp.int32, (tm, LANES), 1) >= 64
        ones_row = lax.broadcasted_iota(jnp.int32, (LANES, 1), 0)
        q3, _ = _rms(jnp.stack([_dot(qn, wq_ref[h]) for h in range(HEADS)]), MLA_QK)
        q_out[...] = (_rope(q3 * qn_ref[...], c, s1, s2) * (MLA_SCALE * LOG2E)).astype(MXU_DTYPE)
        k3, _ = _rms(jnp.stack([_dot(kvn, wk_ref[h]) for h in range(HEADS)]) + kr, MLA_QK)
        k3 = _rope(k3 * kn_ref[...], c, s1, s2)
        k_out[...] = k3.astype(MXU_DTYPE)
        for h in range(HEADS):
            for t in range(nsub):
                kt_out[h, t] = k3[h, tk * t:tk * (t + 1)].T.astype(MXU_DTYPE)
        v = _dot(kvn, wv_ref[...])
        for h in range(HEADS):
            vp = v[:, LANES * (h // 2):LANES * (h // 2 + 1)]
            own = half1 if h % 2 else jnp.logical_not(half1)
            vp = jnp.where(own, vp, 0.0)
            for t in range(nsub):
                vpt = vp[tk * t:tk * (t + 1)].T
                vt_out[h, t] = jnp.where(ones_row == MLA_ONES_ROW[h % 2], 1.0, vpt).astype(MXU_DTYPE)

    full = lambda shape: pl.BlockSpec(shape, lambda i: (0,) * len(shape))
    hd = pl.BlockSpec((HEADS, tm, LANES), lambda i: (0, i, 0))
    hdt = pl.BlockSpec((HEADS, nsub, LANES, tk), lambda i: (0, i, 0, 0))
    nat = _sds((HEADS, T, LANES), MXU_DTYPE)
    tr = _sds((HEADS, T // tk, LANES, tk), MXU_DTYPE)
    return pl.pallas_call(
        body, name="mla_prep_fwd", grid=(T // tm,),
        in_specs=[pl.BlockSpec((tm, 256), lambda i: (i, CB_QLAT)), pl.BlockSpec((tm, LANES), lambda i: (i, CB_KVLAT)),
                  pl.BlockSpec((tm, LANES), lambda i: (i, CB_KROPE)),
                  full((1, 256)), full((1, LANES)), full((HEADS, 256, LANES)), full((HEADS, LANES, LANES)),
                  full((LANES, 512)), full((1, LANES)), full((1, LANES)),
                  pl.BlockSpec((tm, LANES), lambda i: (i, 0)), pl.BlockSpec((tm, LANES), lambda i: (i, 0)),
                  pl.BlockSpec((tm, LANES), lambda i: (i, 0))],
        out_specs=[hd, hd, hdt, hdt],
        out_shape=[nat, nat, tr, tr],
        compiler_params=_cp(("parallel",), 32))(
            proj, proj, proj, lw["qa"], lw["kva"], lw["wq"], lw["wk"], lw["wv"], lw["qn"], lw["kn"],
            rope[0], rope[1], rope[2])


def _mla_attn_fwd(q, k, vt):
    T = q.shape[1]
    tk = min(TK, T // 2)
    tq = 2 * tk

    def body(q_ref, k_ref, vt_ref, o_ref, lse_ref, acc_s, m_s, s_a, s_b):
        i = pl.program_id(1)
        key = lax.broadcasted_iota(jnp.int32, (tk, tq), 0)
        qry = lax.broadcasted_iota(jnp.int32, (tk, tq), 1)
        qs = [q_ref[0], q_ref[1]]
        acc_s[...] = jnp.zeros_like(acc_s)
        m_s[...] = jnp.full(m_s.shape, NEG_INF, F32)

        def scores(kj, buf):
            rows = pl.ds(pl.multiple_of(kj * tk, tk), tk)
            for r in range(2):
                buf[r] = _dot_nt(k_ref[r, rows, :], qs[r])

        def consume(kj, buf, diag):
            for r in range(2):
                s = buf[r]
                if diag is not None:
                    s = jnp.where(key + diag * tk <= qry, s, NEG_INF)
                m_old = m_s[r]
                m_new = jnp.maximum(m_old, jnp.max(s, axis=0, keepdims=True))
                alpha = jnp.exp2(m_old - m_new)
                p = jnp.exp2(s - m_new)
                m_s[r] = m_new
                acc_s[r] = alpha * acc_s[r] + _dot(vt_ref[r, kj], p.astype(MXU_DTYPE))

        scores(0, s_a)

        def pair(kj):
            scores(kj + 1, s_b)
            consume(kj, s_a, None)
            scores(kj + 2, s_a)
            consume(kj + 1, s_b, None)

        def quad(kq, carry):
            pair(4 * kq)
            pair(4 * kq + 2)
            return carry

        lax.fori_loop(0, i // 2, quad, 0)

        @pl.when(i % 2 == 1)
        def _():
            pair(2 * i - 2)

        scores(2 * i + 1, s_b)
        consume(2 * i, s_a, 0)
        consume(2 * i + 1, s_b, 1)
        l = [acc_s[r, pl.ds(MLA_ONES_ROW[r], 1), :] for r in range(2)]
        head0 = lax.broadcasted_iota(jnp.int32, (LANES, 1), 0) < 64
        o_ref[...] = jnp.where(head0, acc_s[0] / l[0], acc_s[1] / l[1]).T
        for r in range(2):
            lse_ref[r] = m_s[r] + jnp.log2(l[r])

    return pl.pallas_call(
        body, name="mla_attn_fwd", grid=(HEADS // 2, T // tq),
        in_specs=[pl.BlockSpec((2, tq, LANES), lambda j, i: (j, i, 0)),
                  pl.BlockSpec((2, T, LANES), lambda j, i: (j, 0, 0)),
                  pl.BlockSpec((2, T // tk, LANES, tk), lambda j, i: (j, 0, 0, 0))],
        out_specs=[pl.BlockSpec((tq, LANES), lambda j, i: (i, j)),
                   pl.BlockSpec((2, 1, tq), lambda j, i: (j, 0, i))],
        out_shape=[_sds((T, GROUP_WIDTH), F32), _sds((HEADS, 1, T), F32)],
        scratch_shapes=[pltpu.VMEM((2, LANES, tq), F32), pltpu.VMEM((2, 1, tq), F32),
                        pltpu.VMEM((2, tk, tq), F32), pltpu.VMEM((2, tk, tq), F32)],
        compiler_params=_cp(("parallel", "arbitrary"), 40))(q, k, vt)


def _swa_kv_variants(x, half1):
    xs = pltpu.roll(x, 64, 1)
    out = {}
    for g in range(2):
        for r in range(2):
            own = half1 if r else jnp.logical_not(half1)
            out[(g, r)] = jnp.where(own, x if g == r else xs, 0.0).astype(MXU_DTYPE)
    return out


def _swa_alibi():
    qi = np.arange(BLOCK)[:, None]
    ki = np.arange(2 * BLOCK)[None, :]
    dist = BLOCK + qi - ki
    slopes = 2.0 ** -(np.arange(HEADS) + 1.0)
    tab = np.where(((dist >= 0) & (dist < BLOCK))[None], slopes[:, None, None] * dist[None], 1e30)
    return jnp.asarray(tab, F32)


def _swa_probs(i, nb, q_ref, k_ref, v_ref, pk_ref, pv_ref, qw_ref, kw_ref, alibi_ref, sink_ref):
    scale = SWA_HEAD_DIM ** -0.5
    half1 = lax.broadcasted_iota(jnp.int32, (1, LANES), 1) >= 64
    k_all = jnp.concatenate([pk_ref[...], k_ref[...]], axis=0)
    v_all = jnp.concatenate([pv_ref[...], v_ref[...]], axis=0)
    khat, _ = _rms_halves(k_all, half1)
    kp = _swa_kv_variants(khat * kw_ref[...], half1)
    vp = _swa_kv_variants(v_all, half1)
    qhat, qr, qn = [], [], []
    for j in range(4):
        xh, r = _rms_halves(q_ref[:, LANES * j:LANES * (j + 1)], half1)
        qhat.append(xh)
        qr.append(r)
        qn.append((xh * qw_ref[...]).astype(MXU_DTYPE))
    ki = lax.broadcasted_iota(jnp.int32, (1, 2 * BLOCK), 1)
    first = jnp.where((i == 0) & (ki < BLOCK), NEG_INF, 0.0)
    s = jnp.stack([_dot_nt(qn[h // 2][BLOCK * b:BLOCK * (b + 1)], kp[(h // 4, h % 2)][BLOCK * b:BLOCK * (b + 2)])
                   for b in range(nb) for h in range(HEADS)]) * scale - alibi_ref[...]
    s = jnp.concatenate([s[:HEADS] + first, s[HEADS:]], axis=0) if nb > 1 else s + first
    sink = jnp.stack([jnp.full((1, 1), sink_ref[h], F32) for _ in range(nb) for h in range(HEADS)])
    m = jnp.maximum(jnp.max(s, axis=-1, keepdims=True), sink)
    e = jnp.exp(s - m)
    es = jnp.exp(sink - m)
    inv = 1.0 / (jnp.sum(e, axis=-1, keepdims=True) + es)
    return e * inv, es * inv, dict(half1=half1, kp=kp, vp=vp, qhat=qhat, qr=qr, qn=qn)


def _swa_fwd(proj, lw):
    T = proj.shape[0]
    tm = min(TM_SWA, T)
    nb = tm // BLOCK

    def body(q_ref, k_ref, v_ref, pk_ref, pv_ref, qw_ref, kw_ref, alibi_ref, sink_ref, o_ref):
        p, _, c = _swa_probs(pl.program_id(0), nb, q_ref, k_ref, v_ref, pk_ref, pv_ref, qw_ref, kw_ref, alibi_ref,
                             sink_ref)
        p = p.astype(MXU_DTYPE)
        for b in range(nb):
            ks = slice(BLOCK * b, BLOCK * (b + 2))
            for j in range(4):
                o_ref[BLOCK * b:BLOCK * (b + 1), LANES * j:LANES * (j + 1)] = (
                    _dot(p[HEADS * b + 2 * j], c["vp"][(j // 2, 0)][ks])
                    + _dot(p[HEADS * b + 2 * j + 1], c["vp"][(j // 2, 1)][ks]))

    prev = lambda cb: pl.BlockSpec((BLOCK, LANES), lambda i: (jnp.maximum(i * nb - 1, 0), cb))
    return pl.pallas_call(
        body, name="swa_fwd", grid=(T // tm,),
        in_specs=[pl.BlockSpec((tm, 512), lambda i: (i, CB_SQ)), pl.BlockSpec((tm, LANES), lambda i: (i, CB_SK)),
                  pl.BlockSpec((tm, LANES), lambda i: (i, CB_SV)), prev(CB_SK), prev(CB_SV),
                  pl.BlockSpec((1, LANES), lambda i: (0, 0)), pl.BlockSpec((1, LANES), lambda i: (0, 0)),
                  pl.BlockSpec((nb * HEADS, BLOCK, 2 * BLOCK), lambda i: (0, 0, 0)),
                  pl.BlockSpec(memory_space=pltpu.SMEM)],
        out_specs=pl.BlockSpec((tm, 512), lambda i: (i, 0)),
        out_shape=_sds((T, GROUP_WIDTH), F32),
        compiler_params=_cp(("parallel",), 40))(
            proj, proj, proj, proj, proj, lw["sqn"], lw["skn"], jnp.tile(_swa_alibi(), (nb, 1, 1)), lw["sinks"])


def _shift_down(u, prev, n, row):
    tm = u.shape[0]
    out = pltpu.roll(u, n, 0)
    row8 = lax.broadcasted_iota(jnp.int32, prev.shape, 0)
    for t in range(n):
        src = jnp.sum(jnp.where(row8 == 8 - n + t, prev, 0.0), axis=0, keepdims=True)
        out = jnp.where(row == t, src, out)
    return out


def _shift_up(u, nxt, n, row):
    tm = u.shape[0]
    out = pltpu.roll(u, tm - n, 0)
    row8 = lax.broadcasted_iota(jnp.int32, nxt.shape, 0)
    for t in range(n):
        src = jnp.sum(jnp.where(row8 == t, nxt, 0.0), axis=0, keepdims=True)
        out = jnp.where(row == tm - n + t, src, out)
    return out


def _mix_fwd(proj, o_mla, o_swa, conv_w):
    T = proj.shape[0]
    tm = min(TM_ROW, T)

    def body(gm_ref, ch_ref, cb_ref, cc_ref, gc_ref, gs_ref, pch_ref, pcc_ref, om_ref, os_ref, w_ref, y_ref):
        i = pl.program_id(0)
        row = lax.broadcasted_iota(jnp.int32, (tm, GROUP_WIDTH), 0)
        u = cc_ref[...] * ch_ref[...]
        u_prev = jnp.where(i > 0, pcc_ref[...] * pch_ref[...], 0.0)
        z = (w_ref[0:1, :] * _shift_down(u, u_prev, 2, row) + w_ref[1:2, :] * _shift_down(u, u_prev, 1, row)
             + w_ref[2:3, :] * u)
        gm, gc, gs = gm_ref[...], gc_ref[...], gs_ref[...]
        y_ref[:, 0:512] = (om_ref[...] * (gm * _sigmoid(gm))).astype(MXU_DTYPE)
        y_ref[:, 512:1024] = (cb_ref[...] * z * (gc * _sigmoid(gc))).astype(MXU_DTYPE)
        y_ref[:, 1024:1536] = (os_ref[...] * (gs * _sigmoid(gs))).astype(MXU_DTYPE)

    blk = lambda cb: pl.BlockSpec((tm, 512), lambda i: (i, cb))
    prev = lambda cb: pl.BlockSpec((8, 512), lambda i: (jnp.maximum(i * (tm // 8) - 1, 0), cb))
    tile = pl.BlockSpec((tm, 512), lambda i: (i, 0))
    return pl.pallas_call(
        body, name="mix_fwd", grid=(T // tm,),
        in_specs=[blk(CB_GMLA), blk(CB_CH), blk(CB_CB), blk(CB_CC), blk(CB_GCONV), blk(CB_GSWA),
                  prev(CB_CH), prev(CB_CC), tile, tile, pl.BlockSpec((8, 512), lambda i: (0, 0))],
        out_specs=pl.BlockSpec((tm, D_MIX), lambda i: (i, 0)),
        out_shape=_sds((T, D_MIX), MXU_DTYPE),
        compiler_params=_cp(("parallel",), 32))(
            proj, proj, proj, proj, proj, proj, proj, proj, o_mla, o_swa, conv_w)


def _outproj_loss(ycat, wo, x, target):
    T, D = x.shape
    K = ycat.shape[1]
    tm = min(TM_PROJ, T)
    nt = T // tm

    def body(y_ref, w_ref, x_ref, t_ref, g_ref, loss_ref, acc_ref):
        i = pl.program_id(0)

        @pl.when(i == 0)
        def _():
            acc_ref[...] = jnp.zeros_like(acc_ref)

        err = _dot(y_ref[...], w_ref[...]) + x_ref[...] - t_ref[...]
        g_ref[...] = err * (1.0 / D)
        acc_ref[...] += _fold_rows8(err * err)

        @pl.when(i == nt - 1)
        def _():
            tot = jnp.sum(jnp.sum(acc_ref[...], axis=1, keepdims=True), axis=0, keepdims=True)
            loss_ref[...] = jnp.broadcast_to(tot * (0.5 / D), (8, LANES))

    tile = pl.BlockSpec((tm, D), lambda i: (i, 0))
    return pl.pallas_call(
        body, name="outproj_loss", grid=(nt,),
        in_specs=[pl.BlockSpec((tm, K), lambda i: (i, 0)), pl.BlockSpec((K, D), lambda i: (0, 0)), tile, tile],
        out_specs=[tile, pl.BlockSpec((8, LANES), lambda i: (0, 0))],
        out_shape=[_sds((T, D), F32), _sds((8, LANES), F32)],
        scratch_shapes=[pltpu.VMEM((8, D), F32)],
        compiler_params=_cp(("arbitrary",), 48))(ycat, wo, x, target)


def _outproj_bwd(g, ycat, wot):
    T, D = g.shape
    K = ycat.shape[1]
    tm = min(512, T)
    nt = T // tm

    def body(g_ref, y_ref, wt_ref, dy_ref, dw_ref, acc_ref):
        i = pl.program_id(0)

        @pl.when(i == 0)
        def _():
            acc_ref[...] = jnp.zeros_like(acc_ref)

        gb = g_ref[...].astype(MXU_DTYPE)
        dy_ref[...] = _dot(gb, wt_ref[...])
        acc_ref[...] += _dot_tn(y_ref[...], gb)

        @pl.when(i == nt - 1)
        def _():
            dw_ref[...] = acc_ref[...].astype(WIRE_DTYPE)

    return pl.pallas_call(
        body, name="outproj_bwd", grid=(nt,),
        in_specs=[pl.BlockSpec((tm, D), lambda i: (i, 0)), pl.BlockSpec((tm, K), lambda i: (i, 0)),
                  pl.BlockSpec((D, K), lambda i: (0, 0))],
        out_specs=[pl.BlockSpec((tm, K), lambda i: (i, 0)), pl.BlockSpec((K, D), lambda i: (0, 0))],
        out_shape=[_sds((T, K), F32), _sds((K, D), WIRE_DTYPE)],
        scratch_shapes=[pltpu.VMEM((K, D), F32)],
        compiler_params=_cp(("arbitrary",), 48))(g, ycat, wot)


def _mix_bwd(dycat, proj, o_mla, o_swa, conv_w):
    T = proj.shape[0]
    tm = min(TM_ROW, T)
    nt = T // tm

    def body(dym_ref, dyc_ref, dys_ref, gm_ref, ch_ref, cb_ref, cc_ref, gc_ref, gs_ref, pch_ref, pcc_ref,
             ndy_ref, ncb_ref, ngc_ref, om_ref, os_ref, w_ref,
             d1_ref, dgs_ref, dom_ref, dos_ref, dw_ref):
        i = pl.program_id(0)

        @pl.when(i == 0)
        def _():
            dw_ref[...] = jnp.zeros_like(dw_ref)

        row = lax.broadcasted_iota(jnp.int32, (tm, GROUP_WIDTH), 0)

        def gate(g):
            sg = _sigmoid(g)
            return g * sg, sg * (1.0 + g * (1.0 - sg))

        gm = gm_ref[...]
        silu, dsilu = gate(gm)
        dym = dym_ref[...]
        dom_ref[...] = dym * silu
        d1_ref[:, 0:512] = (dym * om_ref[...] * dsilu).astype(MXU_DTYPE)

        gs = gs_ref[...]
        silu, dsilu = gate(gs)
        dys = dys_ref[...]
        dos_ref[...] = dys * silu
        dgs_ref[...] = (dys * os_ref[...] * dsilu).astype(MXU_DTYPE)

        ch, cb, cc, gc, dyc = ch_ref[...], cb_ref[...], cc_ref[...], gc_ref[...], dyc_ref[...]
        w0, w1, w2 = w_ref[0:1, :], w_ref[1:2, :], w_ref[2:3, :]
        u = cc * ch
        u_prev = jnp.where(i > 0, pcc_ref[...] * pch_ref[...], 0.0)
        u1 = _shift_down(u, u_prev, 1, row)
        u2 = _shift_down(u, u_prev, 2, row)
        z = w0 * u2 + w1 * u1 + w2 * u
        silu, dsilu = gate(gc)
        dz = dyc * cb * silu
        ngc = ngc_ref[...]
        dz_next = jnp.where(i < nt - 1, ndy_ref[...] * ncb_ref[...] * (ngc * _sigmoid(ngc)), 0.0)
        du = w2 * dz + w1 * _shift_up(dz, dz_next, 1, row) + w0 * _shift_up(dz, dz_next, 2, row)
        d1_ref[:, 512:1024] = (du * cc).astype(MXU_DTYPE)
        d1_ref[:, 1024:1536] = (dyc * z * silu).astype(MXU_DTYPE)
        d1_ref[:, 1536:2048] = (du * ch).astype(MXU_DTYPE)
        d1_ref[:, 2048:2560] = (dyc * cb * z * dsilu).astype(MXU_DTYPE)
        row8 = lax.broadcasted_iota(jnp.int32, (8, GROUP_WIDTH), 0)
        dw = jnp.zeros((8, GROUP_WIDTH), F32)
        for t, shifted in enumerate((u2, u1, u)):
            dw = dw + jnp.where(row8 == t, jnp.sum(dz * shifted, axis=0, keepdims=True), 0.0)
        dw_ref[...] += dw

    blk = lambda cb: pl.BlockSpec((tm, 512), lambda i: (i, cb))
    prev = lambda cb: pl.BlockSpec((8, 512), lambda i: (jnp.maximum(i * (tm // 8) - 1, 0), cb))
    nxt = lambda cb: pl.BlockSpec((8, 512), lambda i: (jnp.minimum((i + 1) * (tm // 8), T // 8 - 1), cb))
    tile = pl.BlockSpec((tm, 512), lambda i: (i, 0))
    return pl.pallas_call(
        body, name="mix_bwd", grid=(nt,),
        in_specs=[blk(0), blk(1), blk(2), blk(CB_GMLA), blk(CB_CH), blk(CB_CB), blk(CB_CC), blk(CB_GCONV),
                  blk(CB_GSWA), prev(CB_CH), prev(CB_CC), nxt(1), nxt(CB_CB), nxt(CB_GCONV), tile, tile,
                  pl.BlockSpec((8, 512), lambda i: (0, 0))],
        out_specs=[pl.BlockSpec((tm, 2560), lambda i: (i, 0)), tile, tile, tile,
                   pl.BlockSpec((8, 512), lambda i: (0, 0))],
        out_shape=[_sds((T, 2560), MXU_DTYPE), _sds((T, 512), MXU_DTYPE), _sds((T, 512), F32),
                   _sds((T, 512), F32), _sds((8, 512), F32)],
        compiler_params=_cp(("arbitrary",), 48))(
            dycat, dycat, dycat, proj, proj, proj, proj, proj, proj, proj, proj, dycat, proj, proj,
            o_mla, o_swa, conv_w)


def _swa_bwd(proj, o_swa, do_swa, lw):
    T = proj.shape[0]
    tm = min(TM_SWA, T)
    nb = tm // BLOCK
    scale = SWA_HEAD_DIM ** -0.5

    def body(q_ref, k_ref, v_ref, pk_ref, pv_ref, o_ref, do_ref, qw_ref, kw_ref, alibi_ref, sink_ref,
             dq_ref, dk_ref, dv_ref, dqw_ref, dsink_ref):
        i = pl.program_id(0)

        @pl.when(i == 0)
        def _():
            dk_ref[...] = jnp.zeros_like(dk_ref)
            dv_ref[...] = jnp.zeros_like(dv_ref)
            dqw_ref[...] = jnp.zeros_like(dqw_ref)
            dsink_ref[...] = jnp.zeros_like(dsink_ref)

        p, p_sink, c = _swa_probs(i, nb, q_ref, k_ref, v_ref, pk_ref, pv_ref, qw_ref, kw_ref, alibi_ref, sink_ref)
        half1, kp, vp, qn, qhat, qr = c["half1"], c["kp"], c["vp"], c["qn"], c["qhat"], c["qr"]
        qw = qw_ref[...]
        rows = [slice(BLOCK * b, BLOCK * (b + 1)) for b in range(nb)]
        keys = [slice(BLOCK * b, BLOCK * (b + 2)) for b in range(nb)]
        dob, dd0, dd1 = [], [], []
        for j in range(4):
            cols = slice(LANES * j, LANES * (j + 1))
            do = do_ref[:, cols]
            dob.append(do.astype(MXU_DTYPE))
            prod = do * o_ref[:, cols]
            dd0.append(jnp.sum(jnp.where(half1, 0.0, prod), axis=-1, keepdims=True))
            dd1.append(jnp.sum(jnp.where(half1, prod, 0.0), axis=-1, keepdims=True))
        dd = jnp.stack([(dd1 if h % 2 else dd0)[h // 2][rows[b]] for b in range(nb) for h in range(HEADS)])
        dp = jnp.stack([_dot_nt(dob[h // 2][rows[b]], vp[(h // 4, h % 2)][keys[b]])
                        for b in range(nb) for h in range(HEADS)])
        ds = (p * (dp - dd) * scale).astype(MXU_DTYPE)
        dsink = -jnp.sum(p_sink * dd, axis=1, keepdims=True)
        pb = p.astype(MXU_DTYPE)

        dqw = jnp.zeros((1, LANES), F32)
        for j in range(4):
            g = j // 2
            dqn = [_dot(ds[HEADS * b + 2 * j], kp[(g, 0)][keys[b]]) + _dot(ds[HEADS * b + 2 * j + 1], kp[(g, 1)][keys[b]])
                   for b in range(nb)]
            dqn = jnp.concatenate(dqn, axis=0) if nb > 1 else dqn[0]
            dqw = dqw + jnp.sum(dqn * qhat[j], axis=0, keepdims=True)
            dq_ref[:, LANES * j:LANES * (j + 1)] = _rms_halves_bwd(dqn, qhat[j], qr[j], qw, half1).astype(MXU_DTYPE)
        dqw_ref[...] += _row0(dqw + pltpu.roll(dqw, 64, 1))

        dk_tot = jnp.zeros((tm + BLOCK, LANES), F32)
        dv_tot = jnp.zeros((tm + BLOCK, LANES), F32)
        for b in range(nb):
            dk_b = jnp.zeros((2 * BLOCK, LANES), F32)
            dv_b = jnp.zeros((2 * BLOCK, LANES), F32)
            for g in range(2):
                for r in range(2):
                    own = half1 if r else jnp.logical_not(half1)
                    ha, hb = HEADS * b + 4 * g + r, HEADS * b + 4 * g + 2 + r
                    qa, qb = qn[2 * g][rows[b]], qn[2 * g + 1][rows[b]]
                    da, db = dob[2 * g][rows[b]], dob[2 * g + 1][rows[b]]
                    dkp = jnp.where(own, _dot_tn(ds[ha], qa) + _dot_tn(ds[hb], qb), 0.0)
                    dvp = jnp.where(own, _dot_tn(pb[ha], da) + _dot_tn(pb[hb], db), 0.0)
                    if g != r:
                        dkp = pltpu.roll(dkp, 64, 1)
                        dvp = pltpu.roll(dvp, 64, 1)
                    dk_b = dk_b + dkp
                    dv_b = dv_b + dvp
            pad = lambda x: jnp.concatenate(
                [z for z in (jnp.zeros((BLOCK * b, LANES), F32), x, jnp.zeros((BLOCK * (nb - 1 - b), LANES), F32))
                 if z.shape[0]], axis=0)
            dk_tot = dk_tot + pad(dk_b)
            dv_tot = dv_tot + pad(dv_b)
        dst = pl.ds(pl.multiple_of(i * tm, BLOCK), tm + BLOCK)
        dk_ref[dst, :] += dk_tot
        dv_ref[dst, :] += dv_tot

        row8 = lax.broadcasted_iota(jnp.int32, (8, LANES), 0)
        dsink_tile = jnp.zeros((8, LANES), F32)
        for b in range(nb):
            for h in range(HEADS):
                dsink_tile = dsink_tile + jnp.where(row8 == h, jnp.broadcast_to(dsink[HEADS * b + h], (8, LANES)), 0.0)
        dsink_ref[...] += dsink_tile

    prev = lambda cb: pl.BlockSpec((BLOCK, LANES), lambda i: (jnp.maximum(i * nb - 1, 0), cb))
    tile = pl.BlockSpec((tm, 512), lambda i: (i, 0))
    small = pl.BlockSpec((8, LANES), lambda i: (0, 0))
    acc = pl.BlockSpec((T + BLOCK, LANES), lambda i: (0, 0))
    return pl.pallas_call(
        body, name="swa_bwd", grid=(T // tm,),
        in_specs=[pl.BlockSpec((tm, 512), lambda i: (i, CB_SQ)), pl.BlockSpec((tm, LANES), lambda i: (i, CB_SK)),
                  pl.BlockSpec((tm, LANES), lambda i: (i, CB_SV)), prev(CB_SK), prev(CB_SV), tile, tile,
                  pl.BlockSpec((1, LANES), lambda i: (0, 0)), pl.BlockSpec((1, LANES), lambda i: (0, 0)),
                  pl.BlockSpec((nb * HEADS, BLOCK, 2 * BLOCK), lambda i: (0, 0, 0)),
                  pl.BlockSpec(memory_space=pltpu.SMEM)],
        out_specs=[tile, acc, acc, small, small],
        out_shape=[_sds((T, 512), MXU_DTYPE), _sds((T + BLOCK, LANES), F32), _sds((T + BLOCK, LANES), F32),
                   _sds((8, LANES), F32), _sds((8, LANES), F32)],
        compiler_params=_cp(("arbitrary",), 48))(
            proj, proj, proj, proj, proj, o_swa, do_swa, lw["sqn"], lw["skn"], jnp.tile(_swa_alibi(), (nb, 1, 1)),
            lw["sinks"])


def _swa_kv_bwd(proj, dkn, dv, lw):
    T = proj.shape[0]
    tm = min(TM_SWA, T)
    dkn, dv = dkn[BLOCK:], dv[BLOCK:]

    def body(k_ref, dkn_ref, dv_ref, kw_ref, d_ref, dkw_ref):
        i = pl.program_id(0)

        @pl.when(i == 0)
        def _():
            dkw_ref[...] = jnp.zeros_like(dkw_ref)

        half1 = lax.broadcasted_iota(jnp.int32, (1, LANES), 1) >= 64
        khat, kr = _rms_halves(k_ref[...], half1)
        dkn_t = dkn_ref[...]
        dkw = jnp.sum(dkn_t * khat, axis=0, keepdims=True)
        dkw_ref[...] += _row0(dkw + pltpu.roll(dkw, 64, 1))
        d_ref[:, 0:LANES] = _rms_halves_bwd(dkn_t, khat, kr, kw_ref[...], half1).astype(MXU_DTYPE)
        d_ref[:, LANES:2 * LANES] = dv_ref[...].astype(MXU_DTYPE)

    return pl.pallas_call(
        body, name="swa_kv_bwd", grid=(T // tm,),
        in_specs=[pl.BlockSpec((tm, LANES), lambda i: (i, CB_SK)), pl.BlockSpec((tm, LANES), lambda i: (i, 0)),
                  pl.BlockSpec((tm, LANES), lambda i: (i, 0)), pl.BlockSpec((1, LANES), lambda i: (0, 0))],
        out_specs=[pl.BlockSpec((tm, 2 * LANES), lambda i: (i, 0)), pl.BlockSpec((8, LANES), lambda i: (0, 0))],
        out_shape=[_sds((T, 2 * LANES), MXU_DTYPE), _sds((8, LANES), F32)],
        compiler_params=_cp(("arbitrary",), 32))(proj, dkn, dv, lw["skn"])


def _mla_attn_bwd(q, k, kt, vt, o, do, lse):
    T = q.shape[1]
    tk = min(TK, T // 2)
    tq = 2 * tk

    def body(q_ref, k_ref, kt_ref, vt_ref, o_ref, do_ref, lse_ref, dq_ref, dk_ref, dv_ref, dq_s, lse_s, dd_s,
             s_a, s_b, p_a, p_b):
        h = pl.program_id(0)
        i = pl.program_id(1)

        @pl.when(i == 0)
        def _():
            dk_ref[...] = jnp.zeros_like(dk_ref)
            dv_ref[...] = jnp.zeros_like(dv_ref)

        qry = lax.broadcasted_iota(jnp.int32, (tq, tk), 0)
        key = lax.broadcasted_iota(jnp.int32, (tq, tk), 1)
        own = (lax.broadcasted_iota(jnp.int32, (1, LANES), 1) // 64) == (h % 2)
        do_own = jnp.where(own, do_ref[...], 0.0)
        dob = do_own.astype(MXU_DTYPE)
        dob_t = do_own.T.astype(MXU_DTYPE)
        qh = q_ref[0]
        qh_t = qh.astype(F32).T.astype(MXU_DTYPE)
        dd_col = jnp.sum(do_own * o_ref[...], axis=-1, keepdims=True)
        lse_col = jnp.broadcast_to(lse_ref[0], (LANES, tq)).T
        for c in range(tk // LANES):
            lse_s[:, LANES * c:LANES * (c + 1)] = lse_col
            dd_s[:, LANES * c:LANES * (c + 1)] = jnp.broadcast_to(dd_col, (tq, LANES))
        dq_s[...] = jnp.zeros_like(dq_s)

        def scores(kj, s_buf, p_buf):
            s_buf[...] = _dot(qh, kt_ref[0, kj])
            p_buf[...] = _dot(dob, vt_ref[0, kj])

        def consume(kj, s_buf, p_buf, diag):
            rows = pl.ds(pl.multiple_of(kj * tk, tk), tk)
            s = s_buf[...]
            if diag is not None:
                s = jnp.where(key + diag * tk <= qry, s, NEG_INF)
            p = jnp.exp2(s - lse_s[...])
            ds = (p * (p_buf[...] - dd_s[...])).astype(MXU_DTYPE)
            dq_s[...] += _dot(ds, k_ref[0, rows, :])
            dk_ref[0, kj] += _dot(qh_t, ds)
            dv_ref[0, kj] += _dot(dob_t, p.astype(MXU_DTYPE))

        scores(0, s_a, p_a)

        def pair(kj):
            scores(kj + 1, s_b, p_b)
            consume(kj, s_a, p_a, None)
            scores(kj + 2, s_a, p_a)
            consume(kj + 1, s_b, p_b, None)

        def quad(kq, carry):
            pair(4 * kq)
            pair(4 * kq + 2)
            return carry

        lax.fori_loop(0, i // 2, quad, 0)

        @pl.when(i % 2 == 1)
        def _():
            pair(2 * i - 2)

        scores(2 * i + 1, s_b, p_b)
        consume(2 * i, s_a, p_a, 0)
        consume(2 * i + 1, s_b, p_b, 1)
        dq_ref[0] = dq_s[...]

    res = pl.BlockSpec((1, T, LANES), lambda h, i: (h, 0, 0))
    res_t = pl.BlockSpec((1, T // tk, LANES, tk), lambda h, i: (h, 0, 0, 0))
    buf = pltpu.VMEM((tq, tk), F32)
    acc_t = _sds((HEADS, T // tk, LANES, tk), F32)
    return pl.pallas_call(
        body, name="mla_attn_bwd", grid=(HEADS, T // tq),
        in_specs=[pl.BlockSpec((1, tq, LANES), lambda h, i: (h, i, 0)), res, res_t, res_t,
                  pl.BlockSpec((tq, LANES), lambda h, i: (i, h // 2)),
                  pl.BlockSpec((tq, LANES), lambda h, i: (i, h // 2)),
                  pl.BlockSpec((1, 1, tq), lambda h, i: (h, 0, i))],
        out_specs=[pl.BlockSpec((1, tq, LANES), lambda h, i: (h, i, 0)), res_t, res_t],
        out_shape=[_sds((HEADS, T, LANES), F32), acc_t, acc_t],
        scratch_shapes=[pltpu.VMEM((tq, LANES), F32), buf, buf, buf, buf, buf, buf],
        compiler_params=_cp(("parallel", "arbitrary"), 48))(q, k, kt, vt, o, do, lse)


def _mla_prep_bwd(proj, dq, dk, dv, lw, rope):
    T = proj.shape[0]
    tm = min(TK, T // 2)

    def body(ql_ref, kvl_ref, kr_ref, dq_ref, dk_ref, dv_ref, qa_ref, kva_ref, wq_ref, wk_ref, wv_ref,
             wqt_ref, wkt_ref, wvt_ref, qn_ref, kn_ref, c_ref, s1_ref, s2_ref,
             d_ref, dwq_ref, dwk_ref, dwv_ref, dqa_ref, dkva_ref, dqn_ref, dkn_ref):
        i = pl.program_id(0)

        @pl.when(i == 0)
        def _():
            for ref in (dwq_ref, dwk_ref, dwv_ref, dqa_ref, dkva_ref, dqn_ref, dkn_ref):
                ref[...] = jnp.zeros_like(ref)

        c, s1, s2 = c_ref[...], s1_ref[...], s2_ref[...]
        lane = lax.broadcasted_iota(jnp.int32, (1, LANES), 1)
        qlhat, qlr = _rms(ql_ref[...], MLA_Q_LORA)
        qn = (qlhat * qa_ref[...]).astype(MXU_DTYPE)
        kvhat, kvr = _rms(kvl_ref[...], MLA_KV_LORA)
        kvn = (kvhat * kva_ref[...]).astype(MXU_DTYPE)
        kr = kr_ref[...]
        x3, r3 = _rms(jnp.stack([_dot(qn, wq_ref[h]) for h in range(HEADS)]), MLA_QK)
        dy3 = _rope_bwd(dq_ref[...] * MLA_SCALE, c, s1, s2)
        dqw = jnp.sum(jnp.sum(dy3 * x3, axis=0), axis=0, keepdims=True)
        dx3 = _rms_bwd(dy3, x3, r3, qn_ref[...], MLA_QK).astype(MXU_DTYPE)
        dqnl = jnp.zeros((tm, MLA_Q_LORA), F32)
        for h in range(HEADS):
            dwq_ref[h] += _dot_tn(qn, dx3[h])
            dqnl = dqnl + _dot(dx3[h], wqt_ref[h])

        x3, r3 = _rms(jnp.stack([_dot(kvn, wk_ref[h]) for h in range(HEADS)]) + kr, MLA_QK)
        dy3 = _rope_bwd(jnp.stack([dk_ref[h, 0].T for h in range(HEADS)]) * LN2, c, s1, s2)
        dkw = jnp.sum(jnp.sum(dy3 * x3, axis=0), axis=0, keepdims=True)
        dxf3 = _rms_bwd(dy3, x3, r3, kn_ref[...], MLA_QK)
        dkr = jnp.sum(dxf3, axis=0)
        dx3 = dxf3.astype(MXU_DTYPE)
        dkvn = jnp.zeros((tm, MLA_KV_LORA), F32)
        for h in range(HEADS):
            dwk_ref[h] += _dot_tn(kvn, dx3[h])
            dkvn = dkvn + _dot(dx3[h], wkt_ref[h])
        dvc = jnp.concatenate([(dv_ref[2 * j, 0] + dv_ref[2 * j + 1, 0]).T for j in range(4)],
                              axis=1).astype(MXU_DTYPE)
        dwv_ref[...] += _dot_tn(kvn, dvc)
        dkvn = dkvn + _dot(dvc, wvt_ref[...])
        dqa_ref[...] += _row0(jnp.sum(dqnl * qlhat, axis=0, keepdims=True))
        dkva_ref[...] += _row0(jnp.sum(dkvn * kvhat, axis=0, keepdims=True))
        dqn_ref[...] += _row0(dqw)
        dkn_ref[...] += _row0(dkw)
        d_ref[:, 0:256] = _rms_bwd(dqnl, qlhat, qlr, qa_ref[...], MLA_Q_LORA).astype(MXU_DTYPE)
        d_ref[:, 256:384] = _rms_bwd(dkvn, kvhat, kvr, kva_ref[...], MLA_KV_LORA).astype(MXU_DTYPE)
        d_ref[:, 384:512] = jnp.where((lane >= 64) & (lane < 96), dkr, 0.0).astype(MXU_DTYPE)

    full = lambda shape: pl.BlockSpec(shape, lambda i: (0,) * len(shape))
    hd = pl.BlockSpec((HEADS, tm, LANES), lambda i: (0, i, 0))
    hdt = pl.BlockSpec((HEADS, 1, LANES, tm), lambda i: (0, i, 0, 0))
    tab = pl.BlockSpec((tm, LANES), lambda i: (i, 0))
    return pl.pallas_call(
        body, name="mla_prep_bwd", grid=(T // tm,),
        in_specs=[pl.BlockSpec((tm, 256), lambda i: (i, CB_QLAT)), pl.BlockSpec((tm, LANES), lambda i: (i, CB_KVLAT)),
                  pl.BlockSpec((tm, LANES), lambda i: (i, CB_KROPE)), hd, hdt, hdt,
                  full((1, 256)), full((1, LANES)), full((HEADS, 256, LANES)), full((HEADS, LANES, LANES)),
                  full((LANES, 512)), full((HEADS, LANES, 256)), full((HEADS, LANES, LANES)), full((512, LANES)),
                  full((1, LANES)), full((1, LANES)), tab, tab, tab],
        out_specs=[pl.BlockSpec((tm, 512), lambda i: (i, 0)), full((HEADS, 256, LANES)),
                   full((HEADS, LANES, LANES)), full((LANES, 512)), full((8, 256)), full((8, LANES)),
                   full((8, LANES)), full((8, LANES))],
        out_shape=[_sds((T, 512), MXU_DTYPE), _sds((HEADS, 256, LANES), F32), _sds((HEADS, LANES, LANES), F32),
                   _sds((LANES, 512), F32), _sds((8, 256), F32), _sds((8, LANES), F32), _sds((8, LANES), F32),
                   _sds((8, LANES), F32)],
        compiler_params=_cp(("arbitrary",), 48))(
            proj, proj, proj, dq, dk, dv, lw["qa"], lw["kva"], lw["wq"], lw["wk"], lw["wv"],
            lw["wqt"], lw["wkt"], lw["wvt"], lw["qn"], lw["kn"], rope[0], rope[1], rope[2])


def _inproj_bwd_dx(dproj, wpt, x, g_in, ng):
    T, D = x.shape
    tm = min(TM_PROJ, T)

    def body(dp_ref, wt_ref, x_ref, g_ref, w_ref, dx_ref, dw_ref):
        i = pl.program_id(0)

        @pl.when(i == 0)
        def _():
            dw_ref[...] = jnp.zeros_like(dw_ref)

        dh = _dot(dp_ref[...], wt_ref[...])
        xhat, r = _rms(x_ref[...], D)
        dw_ref[...] += _row0(jnp.sum(dh * xhat, axis=0, keepdims=True))
        dx_ref[...] = g_ref[...] + _rms_bwd(dh, xhat, r, w_ref[...], D)

    tile = pl.BlockSpec((tm, D), lambda i: (i, 0))
    return pl.pallas_call(
        body, name="inproj_bwd_dx", grid=(T // tm,),
        in_specs=[pl.BlockSpec((tm, NP), lambda i: (i, 0)), pl.BlockSpec((NP, D), lambda i: (0, 0)), tile, tile,
                  pl.BlockSpec((1, D), lambda i: (0, 0))],
        out_specs=[tile, pl.BlockSpec((8, D), lambda i: (0, 0))],
        out_shape=[_sds((T, D), F32), _sds((8, D), F32)],
        compiler_params=_cp(("arbitrary",), 48))(dproj, wpt, x, g_in, ng)


def _rope_tables(T, token=0.0):
    half = MLA_ROPE // 2
    inv_freq = jnp.power(jnp.float32(ROPE_THETA), -jnp.arange(half, dtype=F32) / half)
    z = lambda n: jnp.zeros((n,), F32)
    freq = jnp.concatenate([z(MLA_NOPE), inv_freq, inv_freq, z(32)])
    first = jnp.concatenate([z(64), jnp.ones((16,), F32), z(48)])
    second = jnp.concatenate([z(80), jnp.ones((16,), F32), z(32)])
    ang = (jnp.arange(T, dtype=F32) + token)[:, None] * freq[None, :]
    sin = jnp.sin(ang)
    return jnp.cos(ang), -sin * first[None, :], sin * second[None, :]


def _pad_lanes(v, n=LANES):
    v = v.reshape(1, -1)
    return jnp.pad(v, ((0, 0), (0, n - v.shape[1])))


def _pack_win_t(wt):
    z = lambda n: jnp.zeros((n, wt.shape[1]), wt.dtype)
    return jnp.concatenate([wt[0:384], z(64), wt[384:416], z(32), wt[416:2976], wt[2976:3488],
                            wt[3744:4256], wt[3488:3616], wt[3616:3744]], axis=0)


def _unpack_dwin(d):
    return jnp.concatenate([d[:, 0:384], d[:, 448:480], d[:, 512:3072], d[:, 3072:3584], d[:, 4096:4224],
                            d[:, 4224:4352], d[:, 3584:4096]], axis=1)


def _inproj_weights(l, norm_g, w_in_t):
    wpt = _pack_win_t(w_in_t)
    return dict(ng=norm_g[l].reshape(1, -1), wp=wpt.T, wpt=wpt)


def _mixer_weights(l, qa, wqb_full, kva, wkvb_full, qn, kn, conv_full, sqn, skn, sinks, w_out_full):
    wq = jnp.pad(wqb_full, ((0, 0), (0, 0), (0, LANES - MLA_QK)))
    wk = jnp.pad(wkvb_full[:, :, :MLA_NOPE], ((0, 0), (0, 0), (0, LANES - MLA_NOPE)))
    wv = jnp.transpose(wkvb_full[:, :, MLA_NOPE:], (1, 0, 2)).reshape(MLA_KV_LORA, GROUP_WIDTH)
    return dict(
        qa=qa[l].reshape(1, -1), kva=kva[l].reshape(1, -1),
        wq=wq, wk=wk, wv=wv, wqt=jnp.transpose(wq, (0, 2, 1)), wkt=jnp.transpose(wk, (0, 2, 1)), wvt=wv.T,
        qn=_pad_lanes(qn[l]), kn=_pad_lanes(kn[l]),
        conv=jnp.pad(conv_full, ((0, 5), (0, 0))),
        sqn=jnp.tile(sqn[l].reshape(1, -1), (1, 2)), skn=jnp.tile(skn[l].reshape(1, -1), (1, 2)),
        sinks=sinks[l], wo=w_out_full, wot=w_out_full.T)


def _layer_weights(l, norm_g, w_in_full, qa, wqb_full, kva, wkvb_full, qn, kn, conv_full, sqn, skn, sinks,
                   w_out_full):
    return dict(_inproj_weights(l, norm_g, w_in_full.T),
                **_mixer_weights(l, qa, wqb_full, kva, wkvb_full, qn, kn, conv_full, sqn, skn, sinks, w_out_full))


def _layer_fwd(x, lw, rope, late_weights=None, target=None):
    proj, h = _inproj_fwd(x, lw["ng"], lw["wp"])
    if late_weights is not None:
        lw = dict(lw, **late_weights(proj))
    q, k, kt, vt = _mla_prep_fwd(proj, lw, rope)
    o_mla, lse = _mla_attn_fwd(q, k, vt)
    o_swa = _swa_fwd(proj, lw)
    ycat = _mix_fwd(proj, o_mla, o_swa, lw["conv"])
    if target is None:
        out = _mm_nn(ycat, lw["wo"], "outproj_fwd", residual=x)
    else:
        out = _outproj_loss(ycat, lw["wo"], x, target)
    return out, dict(x=x, proj=proj, h=h, q=q, k=k, kt=kt, vt=vt, o_mla=o_mla, lse=lse, o_swa=o_swa, ycat=ycat,
                     lw=lw)


def _layer_bwd(g, sv, lw, rope, on_big_grads=None):
    proj = sv["proj"]
    dycat, d_wo = _outproj_bwd(g, sv["ycat"], lw["wot"])
    d1, dgs, do_mla, do_swa, d_conv = _mix_bwd(dycat, proj, sv["o_mla"], sv["o_swa"], lw["conv"])
    dsq, dkn_acc, dv_acc, d_sqn, d_sinks = _swa_bwd(proj, sv["o_swa"], do_swa, lw)
    dskv, d_skn = _swa_kv_bwd(proj, dkn_acc, dv_acc, lw)
    dq, dk, dv = _mla_attn_bwd(sv["q"], sv["k"], sv["kt"], sv["vt"], sv["o_mla"], do_mla, sv["lse"])
    dmla, d_wq, d_wk, d_wv, d_qa, d_kva, d_qn, d_kn = _mla_prep_bwd(proj, dq, dk, dv, lw, rope)
    grads = dict(
        w_out=d_wo, w_qb=d_wq[:, :, :MLA_QK],
        w_kvb=jnp.concatenate([d_wk[:, :, :MLA_NOPE],
                               jnp.transpose(d_wv.reshape(MLA_KV_LORA, HEADS, MLA_NOPE), (1, 0, 2))], axis=2))
    token = 0.0 if on_big_grads is None else on_big_grads("mixer", grads)
    dproj = jnp.concatenate([dmla, d1, dsq, dgs, dskv], axis=1)
    d_wp = _mm_tn(sv["h"], dproj, "inproj_bwd_dw", WIRE_DTYPE, tn=NP // 2)
    grads["w_in"] = _unpack_dwin(d_wp)
    token = token if on_big_grads is None else token + on_big_grads("w_in", grads)
    dx, d_ng = _inproj_bwd_dx(dproj, lw["wpt"], sv["x"], g, lw["ng"] + token)
    grads.update(
        conv=d_conv[0:3], norm_g=d_ng[0], qa=d_qa[0], kva=d_kva[0], qn=d_qn[0, :MLA_QK], kn=d_kn[0, :MLA_QK],
        sqn=d_sqn[0, :SWA_HEAD_DIM], skn=d_skn[0, :SWA_HEAD_DIM], sinks=d_sinks[:, 0])
    return dx, grads


def _local_step(x, target, lws, rope):
    saved = []
    for l, lw in enumerate(lws):
        x, sv = _layer_fwd(x, lw, rope, target=target if l == len(lws) - 1 else None)
        saved.append(sv)
    g, loss_tile = x
    grads = [None] * len(lws)
    for l in reversed(range(len(lws))):
        g, grads[l] = _layer_bwd(g, saved[l], lws[l], rope)
    return loss_tile, g, grads


def _my_coords():
    return lax.axis_index("x"), lax.axis_index("y"), lax.axis_index("c")


def _peer(me, k):
    x, y, c = me
    return (1 - x if k & 4 else x, 1 - y if k & 2 else y, 1 - c if k & 1 else c)


def _lin(d):
    return 4 * d[0] + 2 * d[1] + d[2]


def _push_copies(ins, lands, send_sems, recv_sems, gather):
    me = _my_coords()
    my = _lin(me)
    out, inc = [], []
    for a in range(len(ins)):
        for k in range(1, N_DEV):
            peer = _peer(me, k)
            sems = dict(send_sem=send_sems.at[a * 7 + k - 1], recv_sem=recv_sems.at[a * 7 + k - 1],
                        device_id=peer, device_id_type=pl.DeviceIdType.MESH)
            src = ins[a] if gather else ins[a].at[_lin(peer)]
            out.append(pltpu.make_async_remote_copy(src_ref=src, dst_ref=lands[a].at[my], **sems))
            inc.append(pltpu.make_async_remote_copy(src_ref=src, dst_ref=lands[a].at[_lin(peer)], **sems))
    return out, inc


def _push_start(arrays, name, gather):
    n = len(arrays)
    land_shapes = [((N_DEV,) + a.shape) if gather else a.shape for a in arrays]

    def body(*refs):
        ins, lands = refs[:n], refs[n:2 * n]
        send_sems, recv_sems = refs[2 * n], refs[2 * n + 1]
        token = refs[-1]
        out, _ = _push_copies(ins, lands, send_sems, recv_sems, gather)
        for cp in out:
            cp.start()
        token[...] = jnp.zeros_like(token)

    hbm = pl.BlockSpec(memory_space=pltpu.HBM)
    sem = pl.BlockSpec(memory_space=pltpu.SEMAPHORE)
    res = pl.pallas_call(
        body, name=name,
        out_shape=(pltpu.SemaphoreType.DMA((7 * n,)), pltpu.SemaphoreType.DMA((7 * n,)),
                   *[pltpu.HBM(a.shape, a.dtype) for a in arrays],
                   *[pltpu.HBM(s, a.dtype) for s, a in zip(land_shapes, arrays)],
                   _sds((8, LANES), F32)),
        in_specs=(hbm,) * (2 * n),
        out_specs=(sem, sem) + (hbm,) * (2 * n) + (pl.BlockSpec(memory_space=pltpu.VMEM),),
        input_output_aliases={i: 2 + i for i in range(2 * n)},
        compiler_params=pltpu.CompilerParams(has_side_effects=pltpu.SideEffectType.DATAFLOW_SIDE_EFFECTING),
    )(*[pltpu.with_memory_space_constraint(a, pltpu.HBM) for a in arrays],
      *[pltpu.with_memory_space_constraint(lax.empty(s, a.dtype), pltpu.HBM) for s, a in zip(land_shapes, arrays)])
    return dict(send=res[0], recv=res[1], src=res[2:2 + n], land=res[2 + n:2 + 2 * n], token=res[-1][0, 0],
                gather=gather)


def _push_wait(handle, after, name):
    n = len(handle["src"])
    gather = handle["gather"]

    def body(*refs):
        ins, lands = refs[:n], refs[n:2 * n]
        send_sems, recv_sems = refs[2 * n], refs[2 * n + 1]
        out, inc = _push_copies(ins, lands, send_sems, recv_sems, gather)
        for cp in out:
            cp.wait_send()
        for cp in inc:
            cp.wait_recv()

    hbm = pl.BlockSpec(memory_space=pltpu.HBM)
    sem = pl.BlockSpec(memory_space=pltpu.SEMAPHORE)
    res = pl.pallas_call(
        body, name=name,
        out_shape=tuple(pltpu.HBM(a.shape, a.dtype) for a in (*handle["src"], *handle["land"])),
        in_specs=(hbm,) * (2 * n) + (sem, sem, pl.BlockSpec(memory_space=pl.ANY)),
        out_specs=(hbm,) * (2 * n),
        input_output_aliases={i: i for i in range(2 * n)},
        compiler_params=pltpu.CompilerParams(has_side_effects=pltpu.SideEffectType.DATAFLOW_SIDE_EFFECTING),
    )(*handle["src"], *handle["land"], handle["send"], handle["recv"], after)
    return res[n:]


def _small_all_reduce(v):
    R = v.shape[0]

    def body(v_ref, o_ref, buf, send_sems, recv_sems):
        me = _my_coords()
        my = _lin(me)
        sends = []
        for k in range(1, N_DEV):
            cp = pltpu.make_async_remote_copy(
                src_ref=v_ref, dst_ref=buf.at[my], send_sem=send_sems.at[k - 1], recv_sem=recv_sems.at[k - 1],
                device_id=_peer(me, k), device_id_type=pl.DeviceIdType.MESH)
            cp.start()
            sends.append(cp)
        buf[my] = v_ref[...]
        for k in range(1, N_DEV):
            pltpu.make_async_remote_copy(
                src_ref=v_ref, dst_ref=buf.at[_lin(_peer(me, k))], send_sem=send_sems.at[k - 1],
                recv_sem=recv_sems.at[k - 1], device_id=_peer(me, k),
                device_id_type=pl.DeviceIdType.MESH).wait_recv()
        for cp in sends:
            cp.wait_send()
        tot = buf[0]
        for d in range(1, N_DEV):
            tot = tot + buf[d]
        o_ref[...] = tot

    vm = pl.BlockSpec(memory_space=pltpu.VMEM)
    return pl.pallas_call(
        body, name="small_all_reduce", in_specs=[vm], out_specs=vm, out_shape=_sds(v.shape, F32),
        scratch_shapes=[pltpu.VMEM((N_DEV, R, LANES), F32), pltpu.SemaphoreType.DMA((7,)),
                        pltpu.SemaphoreType.DMA((7,))],
    )(v)


def _adamw_math(w, g, m, v):
    m = ADAM_B1 * m + (1.0 - ADAM_B1) * g
    v = ADAM_B2 * v + (1.0 - ADAM_B2) * (g * g)
    m_hat = m / (1.0 - ADAM_B1 ** ADAM_STEP)
    v_hat = v / (1.0 - ADAM_B2 ** ADAM_STEP)
    delta = -ADAM_LR * (m_hat / (jnp.sqrt(v_hat) + ADAM_EPS) + ADAM_WD * w)
    return delta, m, v


def _adamw(parts, w, m, v, name, tr):
    P, R, C = parts.shape
    tr = min(tr, R)

    def body(p_ref, w_ref, m_ref, v_ref, g_out, d_out, m_out, v_out):
        g = p_ref[0].astype(F32)
        for d in range(1, P):
            g = g + p_ref[d].astype(F32)
        delta, m_new, v_new = _adamw_math(w_ref[...], g, m_ref[...], v_ref[...])
        g_out[...] = g
        d_out[...] = delta
        m_out[...] = m_new
        v_out[...] = v_new

    tile = pl.BlockSpec((tr, C), lambda i: (i, 0))
    return pl.pallas_call(
        body, name=name, grid=(R // tr,),
        in_specs=[pl.BlockSpec((P, tr, C), lambda i: (0, i, 0)), tile, tile, tile],
        out_specs=[tile] * 4, out_shape=[_sds((R, C), F32)] * 4,
        compiler_params=_cp(("parallel",), 32))(parts, w, m, v)


SMALL = (("norm_g", D_MODEL), ("mla_q_a_norm", MLA_Q_LORA), ("mla_kv_a_norm", MLA_KV_LORA), ("mla_q_norm", MLA_QK),
         ("mla_k_norm", MLA_QK), ("swa_q_norm", SWA_HEAD_DIM), ("swa_k_norm", SWA_HEAD_DIM), ("swa_sinks", HEADS))
SMALL_GRAD_KEY = dict(norm_g="norm_g", mla_q_a_norm="qa", mla_kv_a_norm="kva", mla_q_norm="qn", mla_k_norm="kn",
                      swa_q_norm="sqn", swa_k_norm="skn", swa_sinks="sinks")
SMALL_ROWS = 32
CONV_ROWS = 24


def _pack_small(get):
    parts = []
    for l in range(DEPTH):
        for name, n in SMALL:
            v = get(name, l).reshape(-1)
            parts.append(jnp.pad(v, (0, (-n) % LANES)))
    return jnp.concatenate(parts).reshape(SMALL_ROWS, LANES)


def _unpack_small(packed):
    flat = packed.reshape(-1)
    out = {name: [] for name, _ in SMALL}
    off = 0
    for l in range(DEPTH):
        for name, n in SMALL:
            out[name].append(flat[off:off + n])
            off += n + (-n) % LANES
    return {name: jnp.stack(v) for name, v in out.items()}


def kernel(x, norm_g, w_in, mla_q_a_norm, mla_w_qb, mla_kv_a_norm, mla_w_kvb, mla_q_norm, mla_k_norm, conv_w, swa_q_norm, swa_k_norm, swa_sinks, w_out, loss_target, m_norm_g, m_w_in, m_mla_q_a_norm, m_mla_w_qb, m_mla_kv_a_norm, m_mla_w_kvb, m_mla_q_norm, m_mla_k_norm, m_conv_w, m_swa_q_norm, m_swa_k_norm, m_swa_sinks, m_w_out, v_norm_g, v_w_in, v_mla_q_a_norm, v_mla_w_qb, v_mla_kv_a_norm, v_mla_w_kvb, v_mla_q_norm, v_mla_k_norm, v_conv_w, v_swa_q_norm, v_swa_k_norm, v_swa_sinks, v_w_out):
    T = x.shape[1]
    weights = dict(norm_g=norm_g, w_in=w_in, mla_q_a_norm=mla_q_a_norm, mla_w_qb=mla_w_qb,
                   mla_kv_a_norm=mla_kv_a_norm, mla_w_kvb=mla_w_kvb, mla_q_norm=mla_q_norm, mla_k_norm=mla_k_norm,
                   conv_w=conv_w, swa_q_norm=swa_q_norm, swa_k_norm=swa_k_norm, swa_sinks=swa_sinks, w_out=w_out)
    mom_m = dict(norm_g=m_norm_g, w_in=m_w_in, mla_q_a_norm=m_mla_q_a_norm, mla_w_qb=m_mla_w_qb,
                 mla_kv_a_norm=m_mla_kv_a_norm, mla_w_kvb=m_mla_w_kvb, mla_q_norm=m_mla_q_norm,
                 mla_k_norm=m_mla_k_norm, conv_w=m_conv_w, swa_q_norm=m_swa_q_norm, swa_k_norm=m_swa_k_norm,
                 swa_sinks=m_swa_sinks, w_out=m_w_out)
    mom_v = dict(norm_g=v_norm_g, w_in=v_w_in, mla_q_a_norm=v_mla_q_a_norm, mla_w_qb=v_mla_w_qb,
                 mla_kv_a_norm=v_mla_kv_a_norm, mla_w_kvb=v_mla_w_kvb, mla_q_norm=v_mla_q_norm,
                 mla_k_norm=v_mla_k_norm, conv_w=v_conv_w, swa_q_norm=v_swa_q_norm, swa_k_norm=v_swa_k_norm,
                 swa_sinks=v_swa_sinks, w_out=v_w_out)

    my = _lin(_my_coords())

    def shards(l):
        return [w_in[l].astype(MXU_DTYPE).T, mla_w_qb[l].astype(MXU_DTYPE), mla_w_kvb[l].astype(MXU_DTYPE),
                w_out[l].astype(MXU_DTYPE), conv_w[l]]

    def inproj_weights(l, g_win_t):
        return _inproj_weights(l, norm_g, g_win_t.reshape(IN_COLS, D_MODEL))

    def mixer_weights(l, gathered):
        g_wqb, g_wkvb, g_wout, g_conv = gathered
        return _mixer_weights(
            l, mla_q_a_norm, g_wqb, mla_kv_a_norm, g_wkvb, mla_q_norm, mla_k_norm,
            jnp.transpose(g_conv, (1, 0, 2)).reshape(3, GROUP_WIDTH), swa_q_norm, swa_k_norm, swa_sinks,
            g_wout.reshape(D_MIX, D_MODEL))

    slot_of = dict(
        w_in=lambda g: jnp.transpose(g["w_in"].reshape(D_MODEL, N_DEV, IN_COLS // N_DEV), (1, 0, 2)),
        w_out=lambda g: g["w_out"].reshape(N_DEV, D_MIX // N_DEV, D_MODEL),
        w_qb=lambda g: g["w_qb"], w_kvb=lambda g: g["w_kvb"])

    def own_slot(landed, mine):
        return [lax.dynamic_update_index_in_dim(a, m, my, 0) for a, m in zip(landed, mine)]

    def landed(handle, after, name, mine):
        return own_slot(_push_wait(handle, after, name), mine)

    gather_in0 = _push_start(shards(0)[:1], "weight_gather_in0_start", gather=True)
    gather0 = _push_start(shards(0)[1:], "weight_gather0_start", gather=True)
    gather1 = _push_start(shards(1), "weight_gather1_start", gather=True)
    rope = _rope_tables(T, gather_in0["token"] + gather0["token"] + gather1["token"])
    lw0 = inproj_weights(0, landed(gather_in0, rope[0], "weight_gather_in0_wait", shards(0)[:1])[0])
    x1, sv0 = _layer_fwd(
        x[0], lw0, rope,
        late_weights=lambda proj: mixer_weights(0, landed(gather0, proj, "weight_gather0_wait", shards(0)[1:])))
    g1_all = landed(gather1, x1, "weight_gather1_wait", shards(1))
    (g2, loss_tile), sv1 = _layer_fwd(x1, dict(inproj_weights(1, g1_all[0]), **mixer_weights(1, g1_all[1:])), rope,
                                      target=loss_target[0])

    parts = {(1, "w_in"): ("w_in", "w_out", "w_qb", "w_kvb"), (0, "mixer"): ("w_out", "w_qb", "w_kvb"),
             (0, "w_in"): ("w_in",)}
    started = []

    def start_exchange(l, part, g):
        if (l, part) not in parts:
            return 0.0
        sl = [slot_of[n](g) for n in parts[(l, part)]]
        handle = _push_start(sl, "grad_exchange%d_%s_start" % (l, part), gather=False)
        started.append((l, part, sl, handle))
        return handle["token"]

    g1, grads1 = _layer_bwd(g2, sv1, sv1["lw"], rope, on_big_grads=functools.partial(start_exchange, 1))
    lw0b = dict(sv0["lw"], conv=sv0["lw"]["conv"] + started[0][3]["token"])
    grad_x, grads0 = _layer_bwd(g1, sv0, lw0b, rope, on_big_grads=functools.partial(start_exchange, 0))
    recv = {}
    for l, part, sl, handle in started:
        got = landed(handle, grad_x, "grad_exchange%d_%s_wait" % (l, part), [s[my] for s in sl])
        recv.update({(l, n): a for n, a in zip(parts[(l, part)], got)})
    grads = [grads0, grads1]
    r_win, r_wout, r_wqb, r_wkvb = [jnp.stack([recv[(0, n)], recv[(1, n)]], axis=1)
                                    for n in ("w_in", "w_out", "w_qb", "w_kvb")]

    small = jnp.concatenate([
        _pack_small(lambda name, l: grads[l][SMALL_GRAD_KEY[name]]),
        jnp.stack([g["conv"] for g in grads]).reshape(CONV_ROWS, LANES),
        loss_tile], axis=0)
    small = _small_all_reduce(small)
    loss = small[SMALL_ROWS + CONV_ROWS, 0]
    my = _lin(_my_coords())
    conv_g = lax.dynamic_slice_in_dim(small[SMALL_ROWS:SMALL_ROWS + CONV_ROWS].reshape(DEPTH, 3, GROUP_WIDTH),
                                      my * 64, 64, axis=2)

    out = {}

    def big(name, recv, rows, cols, tr):
        res = _adamw(recv.reshape(N_DEV, rows, cols), weights[name].reshape(rows, cols),
                     mom_m[name].reshape(rows, cols), mom_v[name].reshape(rows, cols), "adamw_" + name, tr)
        out[name] = [r.reshape(weights[name].shape) for r in res]

    big("w_in", r_win, DEPTH * D_MODEL, IN_COLS // N_DEV, 256)
    big("w_out", r_wout, DEPTH * D_MIX // N_DEV, D_MODEL, 192)
    big("mla_w_qb", r_wqb, DEPTH * MLA_Q_LORA, MLA_QK, 512)
    big("mla_w_kvb", r_wkvb, DEPTH * MLA_KV_LORA, 128, 256)

    pad_conv = lambda a: jnp.pad(a.reshape(-1), (0, 8 * LANES - 6 * 64)).reshape(8, LANES)
    cat = lambda src: jnp.concatenate([_pack_small(lambda name, l: src[name][l]), pad_conv(src["conv_w"])], axis=0)
    g_small = jnp.concatenate([small[:SMALL_ROWS], pad_conv(conv_g)], axis=0)
    res = _adamw(g_small[None], cat(weights), cat(mom_m), cat(mom_v), "adamw_small", SMALL_ROWS + 8)
    smalls = [_unpack_small(r[:SMALL_ROWS]) for r in res]
    for name, _ in SMALL:
        out[name] = [s[name] for s in smalls]
    out["conv_w"] = [r[SMALL_ROWS:].reshape(-1)[:6 * 64].reshape(DEPTH, 3, 64) for r in res]

    order = ["norm_g", "w_in", "mla_q_a_norm", "mla_w_qb", "mla_kv_a_norm", "mla_w_kvb", "mla_q_norm", "mla_k_norm",
             "conv_w", "swa_q_norm", "swa_k_norm", "swa_sinks", "w_out"]
    result = [loss, grad_x[None]]
    for idx in range(4):
        result += [out[name][idx] for name in order]
    return tuple(result)
```

```python
import functools

import jax
import jax.numpy as jnp
import numpy as np
from jax import lax
from jax.experimental import pallas as pl
from jax.experimental.pallas import tpu as pltpu

F32 = jnp.float32
MXU_DTYPE = jnp.bfloat16
WIRE_DTYPE = jnp.bfloat16

N_DEV = 8
DEPTH = 2
D_MODEL = 1024
GROUP_WIDTH = 512
D_MIX = 3 * GROUP_WIDTH
BLOCK = 128
RMS_EPS = 1e-6
NEG_INF = -1e30
HEADS = 8
MLA_QK = 96
MLA_NOPE = 64
MLA_ROPE = 32
MLA_Q_LORA = 256
MLA_KV_LORA = 128
ROPE_THETA = 10000.0
SWA_HEAD_DIM = 64
LANES = 128
IN_COLS = 4256

ADAM_LR = 0.001
ADAM_B1 = 0.9
ADAM_B2 = 0.999
ADAM_EPS = 1e-08
ADAM_WD = 0.01
ADAM_STEP = 10

NP = 4352
CB_GMLA, CB_CH, CB_CB, CB_CC, CB_GCONV, CB_GSWA, CB_SQ = 0, 1, 2, 3, 4, 5, 7
CB_QLAT = 12
CB_KVLAT, CB_KROPE = 26, 27
CB_SK, CB_SV = 32, 33
DPB_MIX, DPB_MLA, DPB_SQ, DPB_SKV = 0, 6, 7, 16

TM_PROJ = 256
TM_ROW = 256
TK = 256
TQ = 2 * TK
MLA_SCALE = MLA_QK ** -0.5
MLA_ONES_ROW = (64, 0)
LOG2E = 1.4426950408889634
LN2 = 0.6931471805599453
TM_SWA = 512
VMEM_MB = 2 ** 20


def _cp(sem, vmem_mb):
    return pltpu.CompilerParams(dimension_semantics=sem, vmem_limit_bytes=vmem_mb * VMEM_MB)


def _sds(shape, dtype):
    return jax.ShapeDtypeStruct(shape, dtype)


def _dot(a, b):
    return jnp.dot(a, b, preferred_element_type=F32)


def _dot_nt(a, b):
    return lax.dot_general(a, b, (((1,), (1,)), ((), ())), preferred_element_type=F32)


def _dot_tn(a, b):
    return lax.dot_general(a, b, (((0,), (0,)), ((), ())), preferred_element_type=F32)


def _rms(x, n):
    r = lax.rsqrt(jnp.sum(x * x, axis=-1, keepdims=True) * (1.0 / n) + RMS_EPS)
    return x * r, r


def _rms_bwd(dy, xhat, r, w, n):
    g = dy * w
    return r * (g - xhat * (jnp.sum(g * xhat, axis=-1, keepdims=True) * (1.0 / n)))


def _rms_halves(x, half1):
    x2 = x * x
    s0 = jnp.sum(jnp.where(half1, 0.0, x2), axis=-1, keepdims=True)
    s1 = jnp.sum(jnp.where(half1, x2, 0.0), axis=-1, keepdims=True)
    r = jnp.where(half1, lax.rsqrt(s1 * (1.0 / 64) + RMS_EPS), lax.rsqrt(s0 * (1.0 / 64) + RMS_EPS))
    return x * r, r


def _rms_halves_bwd(dy, xhat, r, w, half1):
    g = dy * w
    t = g * xhat
    m0 = jnp.sum(jnp.where(half1, 0.0, t), axis=-1, keepdims=True) * (1.0 / 64)
    m1 = jnp.sum(jnp.where(half1, t, 0.0), axis=-1, keepdims=True) * (1.0 / 64)
    return r * (g - xhat * jnp.where(half1, m1, m0))


def _sigmoid(x):
    return 1.0 / (1.0 + jnp.exp(-x))


def _rope(x, c, s1, s2):
    ax = x.ndim - 1
    return x * c + pltpu.roll(x, 112, ax) * s1 + pltpu.roll(x, 16, ax) * s2


def _rope_bwd(dy, c, s1, s2):
    ax = dy.ndim - 1
    return dy * c + pltpu.roll(dy * s1, 16, ax) + pltpu.roll(dy * s2, 112, ax)


def _fold_rows8(x):
    return jnp.sum(x.reshape(x.shape[0] // 8, 8, x.shape[1]), axis=0)


def _row0(v, rows=8):
    row = lax.broadcasted_iota(jnp.int32, (rows, v.shape[1]), 0)
    return jnp.where(row == 0, jnp.broadcast_to(v, (rows, v.shape[1])), 0.0)


def _mm_nn(a, b, name, out_dtype=F32, residual=None, tm=TM_PROJ):
    M, K = a.shape
    N = b.shape[1]
    tm = min(tm, M)

    def body(*refs):
        if residual is None:
            a_ref, b_ref, o_ref = refs
            acc = _dot(a_ref[...].astype(MXU_DTYPE), b_ref[...])
        else:
            a_ref, b_ref, r_ref, o_ref = refs
            acc = _dot(a_ref[...].astype(MXU_DTYPE), b_ref[...]) + r_ref[...]
        o_ref[...] = acc.astype(out_dtype)

    in_specs = [pl.BlockSpec((tm, K), lambda i: (i, 0)), pl.BlockSpec((K, N), lambda i: (0, 0))]
    args = [a, b]
    if residual is not None:
        in_specs.append(pl.BlockSpec((tm, N), lambda i: (i, 0)))
        args.append(residual)
    return pl.pallas_call(
        body, name=name, grid=(M // tm,), in_specs=in_specs,
        out_specs=pl.BlockSpec((tm, N), lambda i: (i, 0)), out_shape=_sds((M, N), out_dtype),
        compiler_params=_cp(("parallel",), 48))(*args)


def _mm_tn(a, b, name, out_dtype, tn, tk=512):
    T, M = a.shape
    N = b.shape[1]
    tk = min(tk, T)
    nk = T // tk

    def body(a_ref, b_ref, o_ref, acc_ref):
        k = pl.program_id(1)

        @pl.when(k == 0)
        def _():
            acc_ref[...] = jnp.zeros_like(acc_ref)

        acc_ref[...] += _dot_tn(a_ref[...].astype(MXU_DTYPE), b_ref[...].astype(MXU_DTYPE))

        @pl.when(k == nk - 1)
        def _():
            o_ref[...] = acc_ref[...].astype(out_dtype)

    return pl.pallas_call(
        body, name=name, grid=(N // tn, nk),
        in_specs=[pl.BlockSpec((tk, M), lambda n, k: (k, 0)), pl.BlockSpec((tk, tn), lambda n, k: (k, n))],
        out_specs=pl.BlockSpec((M, tn), lambda n, k: (0, n)), out_shape=_sds((M, N), out_dtype),
        scratch_shapes=[pltpu.VMEM((M, tn), F32)],
        compiler_params=_cp(("parallel", "arbitrary"), 48))(a, b)


def _inproj_fwd(x, ng, wp):
    T, D = x.shape
    tm = min(TM_PROJ, T)

    def body(x_ref, g_ref, w_ref, proj_ref, h_ref):
        xhat, _ = _rms(x_ref[...], D)
        h = (xhat * g_ref[...]).astype(MXU_DTYPE)
        h_ref[...] = h
        proj_ref[...] = _dot(h, w_ref[...])

    return pl.pallas_call(
        body, name="inproj_fwd", grid=(T // tm,),
        in_specs=[pl.BlockSpec((tm, D), lambda i: (i, 0)), pl.BlockSpec((1, D), lambda i: (0, 0)),
                  pl.BlockSpec((D, NP), lambda i: (0, 0))],
        out_specs=[pl.BlockSpec((tm, NP), lambda i: (i, 0)), pl.BlockSpec((tm, D), lambda i: (i, 0))],
        out_shape=[_sds((T, NP), F32), _sds((T, D), MXU_DTYPE)],
        compiler_params=_cp(("parallel",), 48))(x, ng, wp)


def _mla_prep_fwd(proj, lw, rope):
    T = proj.shape[0]
    tk = min(TK, T // 2)
    nsub = 2
    tm = nsub * tk

    def body(ql_ref, kvl_ref, kr_ref, qa_ref, kva_ref, wq_ref, wk_ref, wv_ref, qn_ref, kn_ref,
             c_ref, s1_ref, s2_ref, q_out, k_out, kt_out, vt_out):
        c, s1, s2 = c_ref[...], s1_ref[...], s2_ref[...]
        qhat, _ = _rms(ql_ref[...], MLA_Q_LORA)
        qn = (qhat * qa_ref[...]).astype(MXU_DTYPE)
        khat, _ = _rms(kvl_ref[...], MLA_KV_LORA)
        kvn = (khat * kva_ref[...]).astype(MXU_DTYPE)
        kr = kr_ref[...]
        half1 = lax.broadcasted_iota(jnp.int32, (tm, LANES), 1) >= 64
        ones_row = lax.broadcasted_iota(jnp.int32, (LANES, 1), 0)
        q3, _ = _rms(jnp.stack([_dot(qn, wq_ref[h]) for h in range(HEADS)]), MLA_QK)
        q_out[...] = (_rope(q3 * qn_ref[...], c, s1, s2) * (MLA_SCALE * LOG2E)).astype(MXU_DTYPE)
        k3, _ = _rms(jnp.stack([_dot(kvn, wk_ref[h]) for h in range(HEADS)]) + kr, MLA_QK)
        k3 = _rope(k3 * kn_ref[...], c, s1, s2)
        k_out[...] = k3.astype(MXU_DTYPE)
        for h in range(HEADS):
            for t in range(nsub):
                kt_out[h, t] = k3[h, tk * t:tk * (t + 1)].T.astype(MXU_DTYPE)
        v = _dot(kvn, wv_ref[...])
        for h in range(HEADS):
            vp = v[:, LANES * (h // 2):LANES * (h // 2 + 1)]
            own = half1 if h % 2 else jnp.logical_not(half1)
            vp = jnp.where(own, vp, 0.0)
            for t in range(nsub):
                vpt = vp[tk * t:tk * (t + 1)].T
                vt_out[h, t] = jnp.where(ones_row == MLA_ONES_ROW[h % 2], 1.0, vpt).astype(MXU_DTYPE)

    full = lambda shape: pl.BlockSpec(shape, lambda i: (0,) * len(shape))
    hd = pl.BlockSpec((HEADS, tm, LANES), lambda i: (0, i, 0))
    hdt = pl.BlockSpec((HEADS, nsub, LANES, tk), lambda i: (0, i, 0, 0))
    nat = _sds((HEADS, T, LANES), MXU_DTYPE)
    tr = _sds((HEADS, T // tk, LANES, tk), MXU_DTYPE)
    return pl.pallas_call(
        body, name="mla_prep_fwd", grid=(T // tm,),
        in_specs=[pl.BlockSpec((tm, 256), lambda i: (i, CB_QLAT)), pl.BlockSpec((tm, LANES), lambda i: (i, CB_KVLAT)),
                  pl.BlockSpec((tm, LANES), lambda i: (i, CB_KROPE)),
                  full((1, 256)), full((1, LANES)), full((HEADS, 256, LANES)), full((HEADS, LANES, LANES)),
                  full((LANES, 512)), full((1, LANES)), full((1, LANES)),
                  pl.BlockSpec((tm, LANES), lambda i: (i, 0)), pl.BlockSpec((tm, LANES), lambda i: (i, 0)),
                  pl.BlockSpec((tm, LANES), lambda i: (i, 0))],
        out_specs=[hd, hd, hdt, hdt],
        out_shape=[nat, nat, tr, tr],
        compiler_params=_cp(("parallel",), 32))(
            proj, proj, proj, lw["qa"], lw["kva"], lw["wq"], lw["wk"], lw["wv"], lw["qn"], lw["kn"],
            rope[0], rope[1], rope[2])


def _mla_attn_fwd(q, k, vt):
    T = q.shape[1]
    tk = min(TK, T // 2)
    tq = 2 * tk

    def body(q_ref, k_ref, vt_ref, o_ref, lse_ref, acc_s, m_s, s_a, s_b):
        i = pl.program_id(1)
        key = lax.broadcasted_iota(jnp.int32, (tk, tq), 0)
        qry = lax.broadcasted_iota(jnp.int32, (tk, tq), 1)
        qs = [q_ref[0], q_ref[1]]
        acc_s[...] = jnp.zeros_like(acc_s)
        m_s[...] = jnp.full(m_s.shape, NEG_INF, F32)

        def scores(kj, buf):
            rows = pl.ds(pl.multiple_of(kj * tk, tk), tk)
            for r in range(2):
                buf[r] = _dot_nt(k_ref[r, rows, :], qs[r])

        def consume(kj, buf, diag):
            for r in range(2):
                s = buf[r]
                if diag is not None:
                    s = jnp.where(key + diag * tk <= qry, s, NEG_INF)
                m_old = m_s[r]
                m_new = jnp.maximum(m_old, jnp.max(s, axis=0, keepdims=True))
                alpha = jnp.exp2(m_old - m_new)
                p = jnp.exp2(s - m_new)
                m_s[r] = m_new
                acc_s[r] = alpha * acc_s[r] + _dot(vt_ref[r, kj], p.astype(MXU_DTYPE))

        scores(0, s_a)

        def pair(kj):
            scores(kj + 1, s_b)
            consume(kj, s_a, None)
            scores(kj + 2, s_a)
            consume(kj + 1, s_b, None)

        def quad(kq, carry):
            pair(4 * kq)
            pair(4 * kq + 2)
            return carry

        lax.fori_loop(0, i // 2, quad, 0)

        @pl.when(i % 2 == 1)
        def _():
            pair(2 * i - 2)

        scores(2 * i + 1, s_b)
        consume(2 * i, s_a, 0)
        consume(2 * i + 1, s_b, 1)
        l = [acc_s[r, pl.ds(MLA_ONES_ROW[r], 1), :] for r in range(2)]
        head0 = lax.broadcasted_iota(jnp.int32, (LANES, 1), 0) < 64
        o_ref[...] = jnp.where(head0, acc_s[0] / l[0], acc_s[1] / l[1]).T
        for r in range(2):
            lse_ref[r] = m_s[r] + jnp.log2(l[r])

    return pl.pallas_call(
        body, name="mla_attn_fwd", grid=(HEADS // 2, T // tq),
        in_specs=[pl.BlockSpec((2, tq, LANES), lambda j, i: (j, i, 0)),
                  pl.BlockSpec((2, T, LANES), lambda j, i: (j, 0, 0)),
                  pl.BlockSpec((2, T // tk, LANES, tk), lambda j, i: (j, 0, 0, 0))],
        out_specs=[pl.BlockSpec((tq, LANES), lambda j, i: (i, j)),
                   pl.BlockSpec((2, 1, tq), lambda j, i: (j, 0, i))],
        out_shape=[_sds((T, GROUP_WIDTH), F32), _sds((HEADS, 1, T), F32)],
        scratch_shapes=[pltpu.VMEM((2, LANES, tq), F32), pltpu.VMEM((2, 1, tq), F32),
                        pltpu.VMEM((2, tk, tq), F32), pltpu.VMEM((2, tk, tq), F32)],
        compiler_params=_cp(("parallel", "arbitrary"), 40))(q, k, vt)


def _swa_kv_variants(x, half1):
    xs = pltpu.roll(x, 64, 1)
    out = {}
    for g in range(2):
        for r in range(2):
            own = half1 if r else jnp.logical_not(half1)
            out[(g, r)] = jnp.where(own, x if g == r else xs, 0.0).astype(MXU_DTYPE)
    return out


def _swa_alibi():
    qi = np.arange(BLOCK)[:, None]
    ki = np.arange(2 * BLOCK)[None, :]
    dist = BLOCK + qi - ki
    slopes = 2.0 ** -(np.arange(HEADS) + 1.0)
    tab = np.where(((dist >= 0) & (dist < BLOCK))[None], slopes[:, None, None] * dist[None], 1e30)
    return jnp.asarray(tab, F32)


def _swa_probs(i, nb, q_ref, k_ref, v_ref, pk_ref, pv_ref, qw_ref, kw_ref, alibi_ref, sink_ref):
    scale = SWA_HEAD_DIM ** -0.5
    half1 = lax.broadcasted_iota(jnp.int32, (1, LANES), 1) >= 64
    k_all = jnp.concatenate([pk_ref[...], k_ref[...]], axis=0)
    v_all = jnp.concatenate([pv_ref[...], v_ref[...]], axis=0)
    khat, _ = _rms_halves(k_all, half1)
    kp = _swa_kv_variants(khat * kw_ref[...], half1)
    vp = _swa_kv_variants(v_all, half1)
    qhat, qr, qn = [], [], []
    for j in range(4):
        xh, r = _rms_halves(q_ref[:, LANES * j:LANES * (j + 1)], half1)
        qhat.append(xh)
        qr.append(r)
        qn.append((xh * qw_ref[...]).astype(MXU_DTYPE))
    ki = lax.broadcasted_iota(jnp.int32, (1, 2 * BLOCK), 1)
    first = jnp.where((i == 0) & (ki < BLOCK), NEG_INF, 0.0)
    s = jnp.stack([_dot_nt(qn[h // 2][BLOCK * b:BLOCK * (b + 1)], kp[(h // 4, h % 2)][BLOCK * b:BLOCK * (b + 2)])
                   for b in range(nb) for h in range(HEADS)]) * scale - alibi_ref[...]
    s = jnp.concatenate([s[:HEADS] + first, s[HEADS:]], axis=0) if nb > 1 else s + first
    sink = jnp.stack([jnp.full((1, 1), sink_ref[h], F32) for _ in range(nb) for h in range(HEADS)])
    m = jnp.maximum(jnp.max(s, axis=-1, keepdims=True), sink)
    e = jnp.exp(s - m)
    es = jnp.exp(sink - m)
    inv = 1.0 / (jnp.sum(e, axis=-1, keepdims=True) + es)
    return e * inv, es * inv, dict(half1=half1, kp=kp, vp=vp, qhat=qhat, qr=qr, qn=qn)


def _swa_fwd(proj, lw):
    T = proj.shape[0]
    tm = min(TM_SWA, T)
    nb = tm // BLOCK

    def body(q_ref, k_ref, v_ref, pk_ref, pv_ref, qw_ref, kw_ref, alibi_ref, sink_ref, o_ref):
        p, _, c = _swa_probs(pl.program_id(0), nb, q_ref, k_ref, v_ref, pk_ref, pv_ref, qw_ref, kw_ref, alibi_ref,
                             sink_ref)
        p = p.astype(MXU_DTYPE)
        for b in range(nb):
            ks = slice(BLOCK * b, BLOCK * (b + 2))
            for j in range(4):
                o_ref[BLOCK * b:BLOCK * (b + 1), LANES * j:LANES * (j + 1)] = (
                    _dot(p[HEADS * b + 2 * j], c["vp"][(j // 2, 0)][ks])
                    + _dot(p[HEADS * b + 2 * j + 1], c["vp"][(j // 2, 1)][ks]))

    prev = lambda cb: pl.BlockSpec((BLOCK, LANES), lambda i: (jnp.maximum(i * nb - 1, 0), cb))
    return pl.pallas_call(
        body, name="swa_fwd", grid=(T // tm,),
        in_specs=[pl.BlockSpec((tm, 512), lambda i: (i, CB_SQ)), pl.BlockSpec((tm, LANES), lambda i: (i, CB_SK)),
                  pl.BlockSpec((tm, LANES), lambda i: (i, CB_SV)), prev(CB_SK), prev(CB_SV),
                  pl.BlockSpec((1, LANES), lambda i: (0, 0)), pl.BlockSpec((1, LANES), lambda i: (0, 0)),
                  pl.BlockSpec((nb * HEADS, BLOCK, 2 * BLOCK), lambda i: (0, 0, 0)),
                  pl.BlockSpec(memory_space=pltpu.SMEM)],
        out_specs=pl.BlockSpec((tm, 512), lambda i: (i, 0)),
        out_shape=_sds((T, GROUP_WIDTH), F32),
        compiler_params=_cp(("parallel",), 40))(
            proj, proj, proj, proj, proj, lw["sqn"], lw["skn"], jnp.tile(_swa_alibi(), (nb, 1, 1)), lw["sinks"])


def _shift_down(u, prev, n, row):
    tm = u.shape[0]
    out = pltpu.roll(u, n, 0)
    row8 = lax.broadcasted_iota(jnp.int32, prev.shape, 0)
    for t in range(n):
        src = jnp.sum(jnp.where(row8 == 8 - n + t, prev, 0.0), axis=0, keepdims=True)
        out = jnp.where(row == t, src, out)
    return out


def _shift_up(u, nxt, n, row):
    tm = u.shape[0]
    out = pltpu.roll(u, tm - n, 0)
    row8 = lax.broadcasted_iota(jnp.int32, nxt.shape, 0)
    for t in range(n):
        src = jnp.sum(jnp.where(row8 == t, nxt, 0.0), axis=0, keepdims=True)
        out = jnp.where(row == tm - n + t, src, out)
    return out


def _mix_fwd(proj, o_mla, o_swa, conv_w):
    T = proj.shape[0]
    tm = min(TM_ROW, T)

    def body(gm_ref, ch_ref, cb_ref, cc_ref, gc_ref, gs_ref, pch_ref, pcc_ref, om_ref, os_ref, w_ref, y_ref):
        i = pl.program_id(0)
        row = lax.broadcasted_iota(jnp.int32, (tm, GROUP_WIDTH), 0)
        u = cc_ref[...] * ch_ref[...]
        u_prev = jnp.where(i > 0, pcc_ref[...] * pch_ref[...], 0.0)
        z = (w_ref[0:1, :] * _shift_down(u, u_prev, 2, row) + w_ref[1:2, :] * _shift_down(u, u_prev, 1, row)
             + w_ref[2:3, :] * u)
        gm, gc, gs = gm_ref[...], gc_ref[...], gs_ref[...]
        y_ref[:, 0:512] = (om_ref[...] * (gm * _sigmoid(gm))).astype(MXU_DTYPE)
        y_ref[:, 512:1024] = (cb_ref[...] * z * (gc * _sigmoid(gc))).astype(MXU_DTYPE)
        y_ref[:, 1024:1536] = (os_ref[...] * (gs * _sigmoid(gs))).astype(MXU_DTYPE)

    blk = lambda cb: pl.BlockSpec((tm, 512), lambda i: (i, cb))
    prev = lambda cb: pl.BlockSpec((8, 512), lambda i: (jnp.maximum(i * (tm // 8) - 1, 0), cb))
    tile = pl.BlockSpec((tm, 512), lambda i: (i, 0))
    return pl.pallas_call(
        body, name="mix_fwd", grid=(T // tm,),
        in_specs=[blk(CB_GMLA), blk(CB_CH), blk(CB_CB), blk(CB_CC), blk(CB_GCONV), blk(CB_GSWA),
                  prev(CB_CH), prev(CB_CC), tile, tile, pl.BlockSpec((8, 512), lambda i: (0, 0))],
        out_specs=pl.BlockSpec((tm, D_MIX), lambda i: (i, 0)),
        out_shape=_sds((T, D_MIX), MXU_DTYPE),
        compiler_params=_cp(("parallel",), 32))(
            proj, proj, proj, proj, proj, proj, proj, proj, o_mla, o_swa, conv_w)


def _outproj_loss(ycat, wo, x, target):
    T, D = x.shape
    K = ycat.shape[1]
    tm = min(TM_PROJ, T)
    nt = T // tm

    def body(y_ref, w_ref, x_ref, t_ref, g_ref, loss_ref, acc_ref):
        i = pl.program_id(0)

        @pl.when(i == 0)
        def _():
            acc_ref[...] = jnp.zeros_like(acc_ref)

        err = _dot(y_ref[...], w_ref[...]) + x_ref[...] - t_ref[...]
        g_ref[...] = err * (1.0 / D)
        acc_ref[...] += _fold_rows8(err * err)

        @pl.when(i == nt - 1)
        def _():
            tot = jnp.sum(jnp.sum(acc_ref[...], axis=1, keepdims=True), axis=0, keepdims=True)
            loss_ref[...] = jnp.broadcast_to(tot * (0.5 / D), (8, LANES))

    tile = pl.BlockSpec((tm, D), lambda i: (i, 0))
    return pl.pallas_call(
        body, name="outproj_loss", grid=(nt,),
        in_specs=[pl.BlockSpec((tm, K), lambda i: (i, 0)), pl.BlockSpec((K, D), lambda i: (0, 0)), tile, tile],
        out_specs=[tile, pl.BlockSpec((8, LANES), lambda i: (0, 0))],
        out_shape=[_sds((T, D), F32), _sds((8, LANES), F32)],
        scratch_shapes=[pltpu.VMEM((8, D), F32)],
        compiler_params=_cp(("arbitrary",), 48))(ycat, wo, x, target)


def _outproj_bwd(g, ycat, wot):
    T, D = g.shape
    K = ycat.shape[1]
    tm = min(512, T)
    nt = T // tm

    def body(g_ref, y_ref, wt_ref, dy_ref, dw_ref, acc_ref):
        i = pl.program_id(0)

        @pl.when(i == 0)
        def _():
            acc_ref[...] = jnp.zeros_like(acc_ref)

        gb = g_ref[...].astype(MXU_DTYPE)
        dy_ref[...] = _dot(gb, wt_ref[...])
        acc_ref[...] += _dot_tn(y_ref[...], gb)

        @pl.when(i == nt - 1)
        def _():
            dw_ref[...] = acc_ref[...].astype(WIRE_DTYPE)

    return pl.pallas_call(
        body, name="outproj_bwd", grid=(nt,),
        in_specs=[pl.BlockSpec((tm, D), lambda i: (i, 0)), pl.BlockSpec((tm, K), lambda i: (i, 0)),
                  pl.BlockSpec((D, K), lambda i: (0, 0))],
        out_specs=[pl.BlockSpec((tm, K), lambda i: (i, 0)), pl.BlockSpec((K, D), lambda i: (0, 0))],
        out_shape=[_sds((T, K), F32), _sds((K, D), WIRE_DTYPE)],
        scratch_shapes=[pltpu.VMEM((K, D), F32)],
        compiler_params=_cp(("arbitrary",), 48))(g, ycat, wot)


def _mix_bwd(dycat, proj, o_mla, o_swa, conv_w):
    T = proj.shape[0]
    tm = min(TM_ROW, T)
    nt = T // tm

    def body(dym_ref, dyc_ref, dys_ref, gm_ref, ch_ref, cb_ref, cc_ref, gc_ref, gs_ref, pch_ref, pcc_ref,
             ndy_ref, ncb_ref, ngc_ref, om_ref, os_ref, w_ref,
             d1_ref, dom_ref, dos_ref, dw_ref):
        i = pl.program_id(0)

        @pl.when(i == 0)
        def _():
            dw_ref[...] = jnp.zeros_like(dw_ref)

        row = lax.broadcasted_iota(jnp.int32, (tm, GROUP_WIDTH), 0)

        def gate(g):
            sg = _sigmoid(g)
            return g * sg, sg * (1.0 + g * (1.0 - sg))

        gm = gm_ref[...]
        silu, dsilu = gate(gm)
        dym = dym_ref[...]
        dom_ref[...] = dym * silu
        d1_ref[:, 0:512] = (dym * om_ref[...] * dsilu).astype(MXU_DTYPE)

        gs = gs_ref[...]
        silu, dsilu = gate(gs)
        dys = dys_ref[...]
        dos_ref[...] = dys * silu
        d1_ref[:, 2560:3072] = (dys * os_ref[...] * dsilu).astype(MXU_DTYPE)

        ch, cb, cc, gc, dyc = ch_ref[...], cb_ref[...], cc_ref[...], gc_ref[...], dyc_ref[...]
        w0, w1, w2 = w_ref[0:1, :], w_ref[1:2, :], w_ref[2:3, :]
        u = cc * ch
        u_prev = jnp.where(i > 0, pcc_ref[...] * pch_ref[...], 0.0)
        u1 = _shift_down(u, u_prev, 1, row)
        u2 = _shift_down(u, u_prev, 2, row)
        z = w0 * u2 + w1 * u1 + w2 * u
        silu, dsilu = gate(gc)
        dz = dyc * cb * silu
        ngc = ngc_ref[...]
        dz_next = jnp.where(i < nt - 1, ndy_ref[...] * ncb_ref[...] * (ngc * _sigmoid(ngc)), 0.0)
        du = w2 * dz + w1 * _shift_up(dz, dz_next, 1, row) + w0 * _shift_up(dz, dz_next, 2, row)
        d1_ref[:, 512:1024] = (du * cc).astype(MXU_DTYPE)
        d1_ref[:, 1024:1536] = (dyc * z * silu).astype(MXU_DTYPE)
        d1_ref[:, 1536:2048] = (du * ch).astype(MXU_DTYPE)
        d1_ref[:, 2048:2560] = (dyc * cb * z * dsilu).astype(MXU_DTYPE)
        row8 = lax.broadcasted_iota(jnp.int32, (8, GROUP_WIDTH), 0)
        dw = jnp.zeros((8, GROUP_WIDTH), F32)
        for t, shifted in enumerate((u2, u1, u)):
            dw = dw + jnp.where(row8 == t, jnp.sum(dz * shifted, axis=0, keepdims=True), 0.0)
        dw_ref[...] += dw

    blk = lambda cb: pl.BlockSpec((tm, 512), lambda i: (i, cb))
    prev = lambda cb: pl.BlockSpec((8, 512), lambda i: (jnp.maximum(i * (tm // 8) - 1, 0), cb))
    nxt = lambda cb: pl.BlockSpec((8, 512), lambda i: (jnp.minimum((i + 1) * (tm // 8), T // 8 - 1), cb))
    tile = pl.BlockSpec((tm, 512), lambda i: (i, 0))
    return pl.pallas_call(
        body, name="mix_bwd", grid=(nt,),
        in_specs=[blk(0), blk(1), blk(2), blk(CB_GMLA), blk(CB_CH), blk(CB_CB), blk(CB_CC), blk(CB_GCONV),
                  blk(CB_GSWA), prev(CB_CH), prev(CB_CC), nxt(1), nxt(CB_CB), nxt(CB_GCONV), tile, tile,
                  pl.BlockSpec((8, 512), lambda i: (0, 0))],
        out_specs=[pl.BlockSpec((tm, 3072), lambda i: (i, DPB_MIX)), tile, tile,
                   pl.BlockSpec((8, 512), lambda i: (0, 0))],
        out_shape=[_sds((T, NP), MXU_DTYPE), _sds((T, 512), F32), _sds((T, 512), F32), _sds((8, 512), F32)],
        compiler_params=_cp(("arbitrary",), 48))(
            dycat, dycat, dycat, proj, proj, proj, proj, proj, proj, proj, proj, dycat, proj, proj,
            o_mla, o_swa, conv_w)


def _swa_bwd(proj, o_swa, do_swa, lw, dproj):
    T = proj.shape[0]
    tm = min(TM_SWA, T)
    nb = tm // BLOCK
    scale = SWA_HEAD_DIM ** -0.5

    def body(q_ref, k_ref, v_ref, pk_ref, pv_ref, o_ref, do_ref, qw_ref, kw_ref, alibi_ref, sink_ref, dproj_in,
             dq_ref, dk_ref, dv_ref, dqw_ref, dsink_ref):
        i = pl.program_id(0)

        @pl.when(i == 0)
        def _():
            dk_ref[...] = jnp.zeros_like(dk_ref)
            dv_ref[...] = jnp.zeros_like(dv_ref)
            dqw_ref[...] = jnp.zeros_like(dqw_ref)
            dsink_ref[...] = jnp.zeros_like(dsink_ref)

        p, p_sink, c = _swa_probs(i, nb, q_ref, k_ref, v_ref, pk_ref, pv_ref, qw_ref, kw_ref, alibi_ref, sink_ref)
        half1, kp, vp, qn, qhat, qr = c["half1"], c["kp"], c["vp"], c["qn"], c["qhat"], c["qr"]
        qw = qw_ref[...]
        rows = [slice(BLOCK * b, BLOCK * (b + 1)) for b in range(nb)]
        keys = [slice(BLOCK * b, BLOCK * (b + 2)) for b in range(nb)]
        dob, dd0, dd1 = [], [], []
        for j in range(4):
            cols = slice(LANES * j, LANES * (j + 1))
            do = do_ref[:, cols]
            dob.append(do.astype(MXU_DTYPE))
            prod = do * o_ref[:, cols]
            dd0.append(jnp.sum(jnp.where(half1, 0.0, prod), axis=-1, keepdims=True))
            dd1.append(jnp.sum(jnp.where(half1, prod, 0.0), axis=-1, keepdims=True))
        dd = jnp.stack([(dd1 if h % 2 else dd0)[h // 2][rows[b]] for b in range(nb) for h in range(HEADS)])
        dp = jnp.stack([_dot_nt(dob[h // 2][rows[b]], vp[(h // 4, h % 2)][keys[b]])
                        for b in range(nb) for h in range(HEADS)])
        ds = (p * (dp - dd) * scale).astype(MXU_DTYPE)
        dsink = -jnp.sum(p_sink * dd, axis=1, keepdims=True)
        pb = p.astype(MXU_DTYPE)

        dqw = jnp.zeros((1, LANES), F32)
        for j in range(4):
            g = j // 2
            dqn = [_dot(ds[HEADS * b + 2 * j], kp[(g, 0)][keys[b]]) + _dot(ds[HEADS * b + 2 * j + 1], kp[(g, 1)][keys[b]])
                   for b in range(nb)]
            dqn = jnp.concatenate(dqn, axis=0) if nb > 1 else dqn[0]
            dqw = dqw + jnp.sum(dqn * qhat[j], axis=0, keepdims=True)
            dq_ref[:, LANES * j:LANES * (j + 1)] = _rms_halves_bwd(dqn, qhat[j], qr[j], qw, half1).astype(MXU_DTYPE)
        dqw_ref[...] += _row0(dqw + pltpu.roll(dqw, 64, 1))

        dk_tot = jnp.zeros((tm + BLOCK, LANES), F32)
        dv_tot = jnp.zeros((tm + BLOCK, LANES), F32)
        for b in range(nb):
            dk_b = jnp.zeros((2 * BLOCK, LANES), F32)
            dv_b = jnp.zeros((2 * BLOCK, LANES), F32)
            for g in range(2):
                for r in range(2):
                    own = half1 if r else jnp.logical_not(half1)
                    ha, hb = HEADS * b + 4 * g + r, HEADS * b + 4 * g + 2 + r
                    qa, qb = qn[2 * g][rows[b]], qn[2 * g + 1][rows[b]]
                    da, db = dob[2 * g][rows[b]], dob[2 * g + 1][rows[b]]
                    dkp = jnp.where(own, _dot_tn(ds[ha], qa) + _dot_tn(ds[hb], qb), 0.0)
                    dvp = jnp.where(own, _dot_tn(pb[ha], da) + _dot_tn(pb[hb], db), 0.0)
                    if g != r:
                        dkp = pltpu.roll(dkp, 64, 1)
                        dvp = pltpu.roll(dvp, 64, 1)
                    dk_b = dk_b + dkp
                    dv_b = dv_b + dvp
            pad = lambda x: jnp.concatenate(
                [z for z in (jnp.zeros((BLOCK * b, LANES), F32), x, jnp.zeros((BLOCK * (nb - 1 - b), LANES), F32))
                 if z.shape[0]], axis=0)
            dk_tot = dk_tot + pad(dk_b)
            dv_tot = dv_tot + pad(dv_b)
        dst = pl.ds(pl.multiple_of(i * tm, BLOCK), tm + BLOCK)
        dk_ref[dst, :] += dk_tot
        dv_ref[dst, :] += dv_tot

        row8 = lax.broadcasted_iota(jnp.int32, (8, LANES), 0)
        dsink_tile = jnp.zeros((8, LANES), F32)
        for b in range(nb):
            for h in range(HEADS):
                dsink_tile = dsink_tile + jnp.where(row8 == h, jnp.broadcast_to(dsink[HEADS * b + h], (8, LANES)), 0.0)
        dsink_ref[...] += dsink_tile

    prev = lambda cb: pl.BlockSpec((BLOCK, LANES), lambda i: (jnp.maximum(i * nb - 1, 0), cb))
    tile = pl.BlockSpec((tm, 512), lambda i: (i, 0))
    small = pl.BlockSpec((8, LANES), lambda i: (0, 0))
    acc = pl.BlockSpec((T + BLOCK, LANES), lambda i: (0, 0))
    return pl.pallas_call(
        body, name="swa_bwd", grid=(T // tm,),
        in_specs=[pl.BlockSpec((tm, 512), lambda i: (i, CB_SQ)), pl.BlockSpec((tm, LANES), lambda i: (i, CB_SK)),
                  pl.BlockSpec((tm, LANES), lambda i: (i, CB_SV)), prev(CB_SK), prev(CB_SV), tile, tile,
                  pl.BlockSpec((1, LANES), lambda i: (0, 0)), pl.BlockSpec((1, LANES), lambda i: (0, 0)),
                  pl.BlockSpec((nb * HEADS, BLOCK, 2 * BLOCK), lambda i: (0, 0, 0)),
                  pl.BlockSpec(memory_space=pltpu.SMEM), pl.BlockSpec(memory_space=pl.ANY)],
        out_specs=[pl.BlockSpec((tm, 512), lambda i: (i, DPB_SQ)), acc, acc, small, small],
        out_shape=[_sds((T, NP), MXU_DTYPE), _sds((T + BLOCK, LANES), F32), _sds((T + BLOCK, LANES), F32),
                   _sds((8, LANES), F32), _sds((8, LANES), F32)],
        input_output_aliases={11: 0},
        compiler_params=_cp(("arbitrary",), 48))(
            proj, proj, proj, proj, proj, o_swa, do_swa, lw["sqn"], lw["skn"], jnp.tile(_swa_alibi(), (nb, 1, 1)),
            lw["sinks"], dproj)


def _swa_kv_bwd(proj, dkn, dv, lw, dproj):
    T = proj.shape[0]
    tm = min(TM_SWA, T)
    dkn, dv = dkn[BLOCK:], dv[BLOCK:]

    def body(k_ref, dkn_ref, dv_ref, kw_ref, dproj_in, d_ref, dkw_ref):
        i = pl.program_id(0)

        @pl.when(i == 0)
        def _():
            dkw_ref[...] = jnp.zeros_like(dkw_ref)

        half1 = lax.broadcasted_iota(jnp.int32, (1, LANES), 1) >= 64
        khat, kr = _rms_halves(k_ref[...], half1)
        dkn_t = dkn_ref[...]
        dkw = jnp.sum(dkn_t * khat, axis=0, keepdims=True)
        dkw_ref[...] += _row0(dkw + pltpu.roll(dkw, 64, 1))
        d_ref[:, 0:LANES] = _rms_halves_bwd(dkn_t, khat, kr, kw_ref[...], half1).astype(MXU_DTYPE)
        d_ref[:, LANES:2 * LANES] = dv_ref[...].astype(MXU_DTYPE)

    return pl.pallas_call(
        body, name="swa_kv_bwd", grid=(T // tm,),
        in_specs=[pl.BlockSpec((tm, LANES), lambda i: (i, CB_SK)), pl.BlockSpec((tm, LANES), lambda i: (i, 0)),
                  pl.BlockSpec((tm, LANES), lambda i: (i, 0)), pl.BlockSpec((1, LANES), lambda i: (0, 0)),
                  pl.BlockSpec(memory_space=pl.ANY)],
        out_specs=[pl.BlockSpec((tm, 2 * LANES), lambda i: (i, DPB_SKV)), pl.BlockSpec((8, LANES), lambda i: (0, 0))],
        out_shape=[_sds((T, NP), MXU_DTYPE), _sds((8, LANES), F32)],
        input_output_aliases={4: 0},
        compiler_params=_cp(("arbitrary",), 32))(proj, dkn, dv, lw["skn"], dproj)


def _mla_attn_bwd(q, k, kt, vt, o, do, lse):
    T = q.shape[1]
    tk = min(TK, T // 2)
    tq = 2 * tk

    def body(q_ref, k_ref, kt_ref, vt_ref, o_ref, do_ref, lse_ref, dq_ref, dk_ref, dv_ref, dq_s, lse_s, dd_s,
             s_a, s_b, p_a, p_b):
        h = pl.program_id(0)
        i = pl.program_id(1)

        @pl.when(i == 0)
        def _():
            dk_ref[...] = jnp.zeros_like(dk_ref)
            dv_ref[...] = jnp.zeros_like(dv_ref)

        qry = lax.broadcasted_iota(jnp.int32, (tq, tk), 0)
        key = lax.broadcasted_iota(jnp.int32, (tq, tk), 1)
        own = (lax.broadcasted_iota(jnp.int32, (1, LANES), 1) // 64) == (h % 2)
        do_own = jnp.where(own, do_ref[...], 0.0)
        dob = do_own.astype(MXU_DTYPE)
        dob_t = do_own.T.astype(MXU_DTYPE)
        qh = q_ref[0]
        qh_t = qh.astype(F32).T.astype(MXU_DTYPE)
        dd_col = jnp.sum(do_own * o_ref[...], axis=-1, keepdims=True)
        lse_col = jnp.broadcast_to(lse_ref[0], (LANES, tq)).T
        for c in range(tk // LANES):
            lse_s[:, LANES * c:LANES * (c + 1)] = lse_col
            dd_s[:, LANES * c:LANES * (c + 1)] = jnp.broadcast_to(dd_col, (tq, LANES))
        dq_s[...] = jnp.zeros_like(dq_s)

        def scores(kj, s_buf, p_buf):
            s_buf[...] = _dot(qh, kt_ref[0, kj])
            p_buf[...] = _dot(dob, vt_ref[0, kj])

        def consume(kj, s_buf, p_buf, diag):
            rows = pl.ds(pl.multiple_of(kj * tk, tk), tk)
            s = s_buf[...]
            if diag is not None:
                s = jnp.where(key + diag * tk <= qry, s, NEG_INF)
            p = jnp.exp2(s - lse_s[...])
            ds = (p * (p_buf[...] - dd_s[...])).astype(MXU_DTYPE)
            dq_s[...] += _dot(ds, k_ref[0, rows, :])
            dk_ref[0, kj] += _dot(qh_t, ds)
            dv_ref[0, kj] += _dot(dob_t, p.astype(MXU_DTYPE))

        scores(0, s_a, p_a)

        def pair(kj):
            scores(kj + 1, s_b, p_b)
            consume(kj, s_a, p_a, None)
            scores(kj + 2, s_a, p_a)
            consume(kj + 1, s_b, p_b, None)

        def quad(kq, carry):
            pair(4 * kq)
            pair(4 * kq + 2)
            return carry

        lax.fori_loop(0, i // 2, quad, 0)

        @pl.when(i % 2 == 1)
        def _():
            pair(2 * i - 2)

        scores(2 * i + 1, s_b, p_b)
        consume(2 * i, s_a, p_a, 0)
        consume(2 * i + 1, s_b, p_b, 1)
        dq_ref[0] = dq_s[...]

    res = pl.BlockSpec((1, T, LANES), lambda h, i: (h, 0, 0))
    res_t = pl.BlockSpec((1, T // tk, LANES, tk), lambda h, i: (h, 0, 0, 0))
    buf = pltpu.VMEM((tq, tk), F32)
    acc_t = _sds((HEADS, T // tk, LANES, tk), F32)
    return pl.pallas_call(
        body, name="mla_attn_bwd", grid=(HEADS, T // tq),
        in_specs=[pl.BlockSpec((1, tq, LANES), lambda h, i: (h, i, 0)), res, res_t, res_t,
                  pl.BlockSpec((tq, LANES), lambda h, i: (i, h // 2)),
                  pl.BlockSpec((tq, LANES), lambda h, i: (i, h // 2)),
                  pl.BlockSpec((1, 1, tq), lambda h, i: (h, 0, i))],
        out_specs=[pl.BlockSpec((1, tq, LANES), lambda h, i: (h, i, 0)), res_t, res_t],
        out_shape=[_sds((HEADS, T, LANES), F32), acc_t, acc_t],
        scratch_shapes=[pltpu.VMEM((tq, LANES), F32), buf, buf, buf, buf, buf, buf],
        compiler_params=_cp(("parallel", "arbitrary"), 48))(q, k, kt, vt, o, do, lse)


def _mla_prep_bwd(proj, dq, dk, dv, lw, rope, dproj):
    T = proj.shape[0]
    tm = min(TK, T // 2)

    def body(ql_ref, kvl_ref, kr_ref, dq_ref, dk_ref, dv_ref, qa_ref, kva_ref, wq_ref, wk_ref, wv_ref,
             wqt_ref, wkt_ref, wvt_ref, qn_ref, kn_ref, c_ref, s1_ref, s2_ref, dproj_in,
             d_ref, dwq_ref, dwk_ref, dwv_ref, dqa_ref, dkva_ref, dqn_ref, dkn_ref):
        i = pl.program_id(0)

        @pl.when(i == 0)
        def _():
            for ref in (dwq_ref, dwk_ref, dwv_ref, dqa_ref, dkva_ref, dqn_ref, dkn_ref):
                ref[...] = jnp.zeros_like(ref)

        c, s1, s2 = c_ref[...], s1_ref[...], s2_ref[...]
        lane = lax.broadcasted_iota(jnp.int32, (1, LANES), 1)
        qlhat, qlr = _rms(ql_ref[...], MLA_Q_LORA)
        qn = (qlhat * qa_ref[...]).astype(MXU_DTYPE)
        kvhat, kvr = _rms(kvl_ref[...], MLA_KV_LORA)
        kvn = (kvhat * kva_ref[...]).astype(MXU_DTYPE)
        kr = kr_ref[...]
        x3, r3 = _rms(jnp.stack([_dot(qn, wq_ref[h]) for h in range(HEADS)]), MLA_QK)
        dy3 = _rope_bwd(dq_ref[...] * MLA_SCALE, c, s1, s2)
        dqw = jnp.sum(jnp.sum(dy3 * x3, axis=0), axis=0, keepdims=True)
        dx3 = _rms_bwd(dy3, x3, r3, qn_ref[...], MLA_QK).astype(MXU_DTYPE)
        dqnl = jnp.zeros((tm, MLA_Q_LORA), F32)
        for h in range(HEADS):
            dwq_ref[h] += _dot_tn(qn, dx3[h])
            dqnl = dqnl + _dot(dx3[h], wqt_ref[h])

        x3, r3 = _rms(jnp.stack([_dot(kvn, wk_ref[h]) for h in range(HEADS)]) + kr, MLA_QK)
        dy3 = _rope_bwd(jnp.stack([dk_ref[h, 0].T for h in range(HEADS)]) * LN2, c, s1, s2)
        dkw = jnp.sum(jnp.sum(dy3 * x3, axis=0), axis=0, keepdims=True)
        dxf3 = _rms_bwd(dy3, x3, r3, kn_ref[...], MLA_QK)
        dkr = jnp.sum(dxf3, axis=0)
        dx3 = dxf3.astype(MXU_DTYPE)
        dkvn = jnp.zeros((tm, MLA_KV_LORA), F32)
        for h in range(HEADS):
            dwk_ref[h] += _dot_tn(kvn, dx3[h])
            dkvn = dkvn + _dot(dx3[h], wkt_ref[h])
        dvc = jnp.concatenate([(dv_ref[2 * j, 0] + dv_ref[2 * j + 1, 0]).T for j in range(4)],
                              axis=1).astype(MXU_DTYPE)
        dwv_ref[...] += _dot_tn(kvn, dvc)
        dkvn = dkvn + _dot(dvc, wvt_ref[...])
        dqa_ref[...] += _row0(jnp.sum(dqnl * qlhat, axis=0, keepdims=True))
        dkva_ref[...] += _row0(jnp.sum(dkvn * kvhat, axis=0, keepdims=True))
        dqn_ref[...] += _row0(dqw)
        dkn_ref[...] += _row0(dkw)
        d_ref[:, 0:256] = _rms_bwd(dqnl, qlhat, qlr, qa_ref[...], MLA_Q_LORA).astype(MXU_DTYPE)
        d_ref[:, 256:384] = _rms_bwd(dkvn, kvhat, kvr, kva_ref[...], MLA_KV_LORA).astype(MXU_DTYPE)
        d_ref[:, 384:512] = jnp.where((lane >= 64) & (lane < 96), dkr, 0.0).astype(MXU_DTYPE)

    full = lambda shape: pl.BlockSpec(shape, lambda i: (0,) * len(shape))
    hd = pl.BlockSpec((HEADS, tm, LANES), lambda i: (0, i, 0))
    hdt = pl.BlockSpec((HEADS, 1, LANES, tm), lambda i: (0, i, 0, 0))
    tab = pl.BlockSpec((tm, LANES), lambda i: (i, 0))
    return pl.pallas_call(
        body, name="mla_prep_bwd", grid=(T // tm,),
        in_specs=[pl.BlockSpec((tm, 256), lambda i: (i, CB_QLAT)), pl.BlockSpec((tm, LANES), lambda i: (i, CB_KVLAT)),
                  pl.BlockSpec((tm, LANES), lambda i: (i, CB_KROPE)), hd, hdt, hdt,
                  full((1, 256)), full((1, LANES)), full((HEADS, 256, LANES)), full((HEADS, LANES, LANES)),
                  full((LANES, 512)), full((HEADS, LANES, 256)), full((HEADS, LANES, LANES)), full((512, LANES)),
                  full((1, LANES)), full((1, LANES)), tab, tab, tab, pl.BlockSpec(memory_space=pl.ANY)],
        out_specs=[pl.BlockSpec((tm, 512), lambda i: (i, DPB_MLA)), full((HEADS, 256, LANES)),
                   full((HEADS, LANES, LANES)), full((LANES, 512)), full((8, 256)), full((8, LANES)),
                   full((8, LANES)), full((8, LANES))],
        out_shape=[_sds((T, NP), MXU_DTYPE), _sds((HEADS, 256, LANES), F32), _sds((HEADS, LANES, LANES), F32),
                   _sds((LANES, 512), F32), _sds((8, 256), F32), _sds((8, LANES), F32), _sds((8, LANES), F32),
                   _sds((8, LANES), F32)],
        input_output_aliases={19: 0},
        compiler_params=_cp(("arbitrary",), 48))(
            proj, proj, proj, dq, dk, dv, lw["qa"], lw["kva"], lw["wq"], lw["wk"], lw["wv"],
            lw["wqt"], lw["wkt"], lw["wvt"], lw["qn"], lw["kn"], rope[0], rope[1], rope[2], dproj)


def _inproj_bwd_dx(dproj, wpt, x, g_in, ng):
    T, D = x.shape
    tm = min(TM_PROJ, T)

    def body(dp_ref, wt_ref, x_ref, g_ref, w_ref, dx_ref, dw_ref):
        i = pl.program_id(0)

        @pl.when(i == 0)
        def _():
            dw_ref[...] = jnp.zeros_like(dw_ref)

        dh = _dot(dp_ref[...], wt_ref[...])
        xhat, r = _rms(x_ref[...], D)
        dw_ref[...] += _row0(jnp.sum(dh * xhat, axis=0, keepdims=True))
        dx_ref[...] = g_ref[...] + _rms_bwd(dh, xhat, r, w_ref[...], D)

    tile = pl.BlockSpec((tm, D), lambda i: (i, 0))
    return pl.pallas_call(
        body, name="inproj_bwd_dx", grid=(T // tm,),
        in_specs=[pl.BlockSpec((tm, NP), lambda i: (i, 0)), pl.BlockSpec((NP, D), lambda i: (0, 0)), tile, tile,
                  pl.BlockSpec((1, D), lambda i: (0, 0))],
        out_specs=[tile, pl.BlockSpec((8, D), lambda i: (0, 0))],
        out_shape=[_sds((T, D), F32), _sds((8, D), F32)],
        compiler_params=_cp(("arbitrary",), 48))(dproj, wpt, x, g_in, ng)


def _rope_tables(T, token=0.0):
    half = MLA_ROPE // 2
    inv_freq = jnp.power(jnp.float32(ROPE_THETA), -jnp.arange(half, dtype=F32) / half)
    z = lambda n: jnp.zeros((n,), F32)
    freq = jnp.concatenate([z(MLA_NOPE), inv_freq, inv_freq, z(32)])
    first = jnp.concatenate([z(64), jnp.ones((16,), F32), z(48)])
    second = jnp.concatenate([z(80), jnp.ones((16,), F32), z(32)])
    ang = (jnp.arange(T, dtype=F32) + token)[:, None] * freq[None, :]
    sin = jnp.sin(ang)
    return jnp.cos(ang), -sin * first[None, :], sin * second[None, :]


def _pad_lanes(v, n=LANES):
    v = v.reshape(1, -1)
    return jnp.pad(v, ((0, 0), (0, n - v.shape[1])))


def _pack_win_t(wt):
    z = lambda n: jnp.zeros((n, wt.shape[1]), wt.dtype)
    return jnp.concatenate([wt[416:2976], wt[3744:4256], wt[0:384], z(64), wt[384:416], z(32), wt[2976:3488],
                            wt[3488:3616], wt[3616:3744]], axis=0)


def _unpack_dwin(d):
    return jnp.concatenate([d[:, 3072:3456], d[:, 3520:3552], d[:, 0:2560], d[:, 3584:4096], d[:, 4096:4224],
                            d[:, 4224:4352], d[:, 2560:3072]], axis=1)


def _inproj_weights(l, norm_g, w_in_t):
    wpt = _pack_win_t(w_in_t)
    return dict(ng=norm_g[l].reshape(1, -1), wp=wpt.T, wpt=wpt)


def _mixer_weights(l, qa, wqb_full, kva, wkvb_full, qn, kn, conv_full, sqn, skn, sinks, w_out_full):
    wq = jnp.pad(wqb_full, ((0, 0), (0, 0), (0, LANES - MLA_QK)))
    wk = jnp.pad(wkvb_full[:, :, :MLA_NOPE], ((0, 0), (0, 0), (0, LANES - MLA_NOPE)))
    wv = jnp.transpose(wkvb_full[:, :, MLA_NOPE:], (1, 0, 2)).reshape(MLA_KV_LORA, GROUP_WIDTH)
    return dict(
        qa=qa[l].reshape(1, -1), kva=kva[l].reshape(1, -1),
        wq=wq, wk=wk, wv=wv, wqt=jnp.transpose(wq, (0, 2, 1)), wkt=jnp.transpose(wk, (0, 2, 1)), wvt=wv.T,
        qn=_pad_lanes(qn[l]), kn=_pad_lanes(kn[l]),
        conv=jnp.pad(conv_full, ((0, 5), (0, 0))),
        sqn=jnp.tile(sqn[l].reshape(1, -1), (1, 2)), skn=jnp.tile(skn[l].reshape(1, -1), (1, 2)),
        sinks=sinks[l], wo=w_out_full, wot=w_out_full.T)


def _layer_weights(l, norm_g, w_in_full, qa, wqb_full, kva, wkvb_full, qn, kn, conv_full, sqn, skn, sinks,
                   w_out_full):
    return dict(_inproj_weights(l, norm_g, w_in_full.T),
                **_mixer_weights(l, qa, wqb_full, kva, wkvb_full, qn, kn, conv_full, sqn, skn, sinks, w_out_full))


def _layer_fwd(x, lw, rope, late_weights=None, target=None):
    proj, h = _inproj_fwd(x, lw["ng"], lw["wp"])
    if late_weights is not None:
        lw = dict(lw, **late_weights(proj))
    q, k, kt, vt = _mla_prep_fwd(proj, lw, rope)
    o_mla, lse = _mla_attn_fwd(q, k, vt)
    o_swa = _swa_fwd(proj, lw)
    ycat = _mix_fwd(proj, o_mla, o_swa, lw["conv"])
    if target is None:
        out = _mm_nn(ycat, lw["wo"], "outproj_fwd", residual=x)
    else:
        out = _outproj_loss(ycat, lw["wo"], x, target)
    return out, dict(x=x, proj=proj, h=h, q=q, k=k, kt=kt, vt=vt, o_mla=o_mla, lse=lse, o_swa=o_swa, ycat=ycat,
                     lw=lw)


def _layer_bwd(g, sv, lw, rope, on_big_grads=None):
    proj = sv["proj"]
    dycat, d_wo = _outproj_bwd(g, sv["ycat"], lw["wot"])
    dproj, do_mla, do_swa, d_conv = _mix_bwd(dycat, proj, sv["o_mla"], sv["o_swa"], lw["conv"])
    dproj, dkn_acc, dv_acc, d_sqn, d_sinks = _swa_bwd(proj, sv["o_swa"], do_swa, lw, dproj)
    dproj, d_skn = _swa_kv_bwd(proj, dkn_acc, dv_acc, lw, dproj)
    dq, dk, dv = _mla_attn_bwd(sv["q"], sv["k"], sv["kt"], sv["vt"], sv["o_mla"], do_mla, sv["lse"])
    dproj, d_wq, d_wk, d_wv, d_qa, d_kva, d_qn, d_kn = _mla_prep_bwd(proj, dq, dk, dv, lw, rope, dproj)
    grads = dict(
        w_out=d_wo, w_qb=d_wq[:, :, :MLA_QK],
        w_kvb=jnp.concatenate([d_wk[:, :, :MLA_NOPE],
                               jnp.transpose(d_wv.reshape(MLA_KV_LORA, HEADS, MLA_NOPE), (1, 0, 2))], axis=2))
    token = 0.0 if on_big_grads is None else on_big_grads("mixer", grads)
    d_wp = _mm_tn(sv["h"], dproj, "inproj_bwd_dw", WIRE_DTYPE, tn=NP // 2)
    grads["w_in"] = _unpack_dwin(d_wp)
    token = token if on_big_grads is None else token + on_big_grads("w_in", grads)
    dx, d_ng = _inproj_bwd_dx(dproj, lw["wpt"], sv["x"], g, lw["ng"] + token)
    grads.update(
        conv=d_conv[0:3], norm_g=d_ng[0], qa=d_qa[0], kva=d_kva[0], qn=d_qn[0, :MLA_QK], kn=d_kn[0, :MLA_QK],
        sqn=d_sqn[0, :SWA_HEAD_DIM], skn=d_skn[0, :SWA_HEAD_DIM], sinks=d_sinks[:, 0])
    return dx, grads


def _local_step(x, target, lws, rope):
    saved = []
    for l, lw in enumerate(lws):
        x, sv = _layer_fwd(x, lw, rope, target=target if l == len(lws) - 1 else None)
        saved.append(sv)
    g, loss_tile = x
    grads = [None] * len(lws)
    for l in reversed(range(len(lws))):
        g, grads[l] = _layer_bwd(g, saved[l], lws[l], rope)
    return loss_tile, g, grads


def _my_coords():
    return lax.axis_index("x"), lax.axis_index("y"), lax.axis_index("c")


def _peer(me, k):
    x, y, c = me
    return (1 - x if k & 4 else x, 1 - y if k & 2 else y, 1 - c if k & 1 else c)


def _lin(d):
    return 4 * d[0] + 2 * d[1] + d[2]


def _push_copies(ins, lands, send_sems, recv_sems, gather):
    me = _my_coords()
    my = _lin(me)
    out, inc = [], []
    for a in range(len(ins)):
        for k in range(1, N_DEV):
            peer = _peer(me, k)
            sems = dict(send_sem=send_sems.at[a * 7 + k - 1], recv_sem=recv_sems.at[a * 7 + k - 1],
                        device_id=peer, device_id_type=pl.DeviceIdType.MESH)
            src = ins[a] if gather else ins[a].at[_lin(peer)]
            out.append(pltpu.make_async_remote_copy(src_ref=src, dst_ref=lands[a].at[my], **sems))
            inc.append(pltpu.make_async_remote_copy(src_ref=src, dst_ref=lands[a].at[_lin(peer)], **sems))
    return out, inc


def _push_start(arrays, name, gather):
    n = len(arrays)
    land_shapes = [((N_DEV,) + a.shape) if gather else a.shape for a in arrays]

    def body(*refs):
        ins, lands = refs[:n], refs[n:2 * n]
        send_sems, recv_sems = refs[2 * n], refs[2 * n + 1]
        token = refs[-1]
        out, _ = _push_copies(ins, lands, send_sems, recv_sems, gather)
        for cp in out:
            cp.start()
        token[...] = jnp.zeros_like(token)

    hbm = pl.BlockSpec(memory_space=pltpu.HBM)
    sem = pl.BlockSpec(memory_space=pltpu.SEMAPHORE)
    res = pl.pallas_call(
        body, name=name,
        out_shape=(pltpu.SemaphoreType.DMA((7 * n,)), pltpu.SemaphoreType.DMA((7 * n,)),
                   *[pltpu.HBM(a.shape, a.dtype) for a in arrays],
                   *[pltpu.HBM(s, a.dtype) for s, a in zip(land_shapes, arrays)],
                   _sds((8, LANES), F32)),
        in_specs=(hbm,) * (2 * n),
        out_specs=(sem, sem) + (hbm,) * (2 * n) + (pl.BlockSpec(memory_space=pltpu.VMEM),),
        input_output_aliases={i: 2 + i for i in range(2 * n)},
        compiler_params=pltpu.CompilerParams(has_side_effects=pltpu.SideEffectType.DATAFLOW_SIDE_EFFECTING),
    )(*[pltpu.with_memory_space_constraint(a, pltpu.HBM) for a in arrays],
      *[pltpu.with_memory_space_constraint(lax.empty(s, a.dtype), pltpu.HBM) for s, a in zip(land_shapes, arrays)])
    return dict(send=res[0], recv=res[1], src=res[2:2 + n], land=res[2 + n:2 + 2 * n], token=res[-1][0, 0],
                gather=gather)


def _push_wait(handle, after, name):
    n = len(handle["src"])
    gather = handle["gather"]

    def body(*refs):
        ins, lands = refs[:n], refs[n:2 * n]
        send_sems, recv_sems = refs[2 * n], refs[2 * n + 1]
        out, inc = _push_copies(ins, lands, send_sems, recv_sems, gather)
        for cp in out:
            cp.wait_send()
        for cp in inc:
            cp.wait_recv()

    hbm = pl.BlockSpec(memory_space=pltpu.HBM)
    sem = pl.BlockSpec(memory_space=pltpu.SEMAPHORE)
    res = pl.pallas_call(
        body, name=name,
        out_shape=tuple(pltpu.HBM(a.shape, a.dtype) for a in (*handle["src"], *handle["land"])),
        in_specs=(hbm,) * (2 * n) + (sem, sem, pl.BlockSpec(memory_space=pl.ANY)),
        out_specs=(hbm,) * (2 * n),
        input_output_aliases={i: i for i in range(2 * n)},
        compiler_params=pltpu.CompilerParams(has_side_effects=pltpu.SideEffectType.DATAFLOW_SIDE_EFFECTING),
    )(*handle["src"], *handle["land"], handle["send"], handle["recv"], after)
    return res[n:]


def _small_all_reduce(v):
    R = v.shape[0]

    def body(v_ref, o_ref, buf, send_sems, recv_sems):
        me = _my_coords()
        my = _lin(me)
        sends = []
        for k in range(1, N_DEV):
            cp = pltpu.make_async_remote_copy(
                src_ref=v_ref, dst_ref=buf.at[my], send_sem=send_sems.at[k - 1], recv_sem=recv_sems.at[k - 1],
                device_id=_peer(me, k), device_id_type=pl.DeviceIdType.MESH)
            cp.start()
            sends.append(cp)
        buf[my] = v_ref[...]
        for k in range(1, N_DEV):
            pltpu.make_async_remote_copy(
                src_ref=v_ref, dst_ref=buf.at[_lin(_peer(me, k))], send_sem=send_sems.at[k - 1],
                recv_sem=recv_sems.at[k - 1], device_id=_peer(me, k),
                device_id_type=pl.DeviceIdType.MESH).wait_recv()
        for cp in sends:
            cp.wait_send()
        tot = buf[0]
        for d in range(1, N_DEV):
            tot = tot + buf[d]
        o_ref[...] = tot

    vm = pl.BlockSpec(memory_space=pltpu.VMEM)
    return pl.pallas_call(
        body, name="small_all_reduce", in_specs=[vm], out_specs=vm, out_shape=_sds(v.shape, F32),
        scratch_shapes=[pltpu.VMEM((N_DEV, R, LANES), F32), pltpu.SemaphoreType.DMA((7,)),
                        pltpu.SemaphoreType.DMA((7,))],
    )(v)


def _adamw_math(w, g, m, v):
    m = ADAM_B1 * m + (1.0 - ADAM_B1) * g
    v = ADAM_B2 * v + (1.0 - ADAM_B2) * (g * g)
    m_hat = m / (1.0 - ADAM_B1 ** ADAM_STEP)
    v_hat = v / (1.0 - ADAM_B2 ** ADAM_STEP)
    delta = -ADAM_LR * (m_hat / (jnp.sqrt(v_hat) + ADAM_EPS) + ADAM_WD * w)
    return delta, m, v


def _adamw(parts, w, m, v, name, tr):
    P, R, C = parts.shape
    tr = min(tr, R)

    def body(p_ref, w_ref, m_ref, v_ref, g_out, d_out, m_out, v_out):
        g = p_ref[0].astype(F32)
        for d in range(1, P):
            g = g + p_ref[d].astype(F32)
        delta, m_new, v_new = _adamw_math(w_ref[...], g, m_ref[...], v_ref[...])
        g_out[...] = g
        d_out[...] = delta
        m_out[...] = m_new
        v_out[...] = v_new

    tile = pl.BlockSpec((tr, C), lambda i: (i, 0))
    return pl.pallas_call(
        body, name=name, grid=(R // tr,),
        in_specs=[pl.BlockSpec((P, tr, C), lambda i: (0, i, 0)), tile, tile, tile],
        out_specs=[tile] * 4, out_shape=[_sds((R, C), F32)] * 4,
        compiler_params=_cp(("parallel",), 32))(parts, w, m, v)


SMALL = (("norm_g", D_MODEL), ("mla_q_a_norm", MLA_Q_LORA), ("mla_kv_a_norm", MLA_KV_LORA), ("mla_q_norm", MLA_QK),
         ("mla_k_norm", MLA_QK), ("swa_q_norm", SWA_HEAD_DIM), ("swa_k_norm", SWA_HEAD_DIM), ("swa_sinks", HEADS))
SMALL_GRAD_KEY = dict(norm_g="norm_g", mla_q_a_norm="qa", mla_kv_a_norm="kva", mla_q_norm="qn", mla_k_norm="kn",
                      swa_q_norm="sqn", swa_k_norm="skn", swa_sinks="sinks")
SMALL_ROWS = 32
CONV_ROWS = 24


def _pack_small(get):
    parts = []
    for l in range(DEPTH):
        for name, n in SMALL:
            v = get(name, l).reshape(-1)
            parts.append(jnp.pad(v, (0, (-n) % LANES)))
    return jnp.concatenate(parts).reshape(SMALL_ROWS, LANES)


def _unpack_small(packed):
    flat = packed.reshape(-1)
    out = {name: [] for name, _ in SMALL}
    off = 0
    for l in range(DEPTH):
        for name, n in SMALL:
            out[name].append(flat[off:off + n])
            off += n + (-n) % LANES
    return {name: jnp.stack(v) for name, v in out.items()}


def kernel(x, norm_g, w_in, mla_q_a_norm, mla_w_qb, mla_kv_a_norm, mla_w_kvb, mla_q_norm, mla_k_norm, conv_w, swa_q_norm, swa_k_norm, swa_sinks, w_out, loss_target, m_norm_g, m_w_in, m_mla_q_a_norm, m_mla_w_qb, m_mla_kv_a_norm, m_mla_w_kvb, m_mla_q_norm, m_mla_k_norm, m_conv_w, m_swa_q_norm, m_swa_k_norm, m_swa_sinks, m_w_out, v_norm_g, v_w_in, v_mla_q_a_norm, v_mla_w_qb, v_mla_kv_a_norm, v_mla_w_kvb, v_mla_q_norm, v_mla_k_norm, v_conv_w, v_swa_q_norm, v_swa_k_norm, v_swa_sinks, v_w_out):
    T = x.shape[1]
    weights = dict(norm_g=norm_g, w_in=w_in, mla_q_a_norm=mla_q_a_norm, mla_w_qb=mla_w_qb,
                   mla_kv_a_norm=mla_kv_a_norm, mla_w_kvb=mla_w_kvb, mla_q_norm=mla_q_norm, mla_k_norm=mla_k_norm,
                   conv_w=conv_w, swa_q_norm=swa_q_norm, swa_k_norm=swa_k_norm, swa_sinks=swa_sinks, w_out=w_out)
    mom_m = dict(norm_g=m_norm_g, w_in=m_w_in, mla_q_a_norm=m_mla_q_a_norm, mla_w_qb=m_mla_w_qb,
                 mla_kv_a_norm=m_mla_kv_a_norm, mla_w_kvb=m_mla_w_kvb, mla_q_norm=m_mla_q_norm,
                 mla_k_norm=m_mla_k_norm, conv_w=m_conv_w, swa_q_norm=m_swa_q_norm, swa_k_norm=m_swa_k_norm,
                 swa_sinks=m_swa_sinks, w_out=m_w_out)
    mom_v = dict(norm_g=v_norm_g, w_in=v_w_in, mla_q_a_norm=v_mla_q_a_norm, mla_w_qb=v_mla_w_qb,
                 mla_kv_a_norm=v_mla_kv_a_norm, mla_w_kvb=v_mla_w_kvb, mla_q_norm=v_mla_q_norm,
                 mla_k_norm=v_mla_k_norm, conv_w=v_conv_w, swa_q_norm=v_swa_q_norm, swa_k_norm=v_swa_k_norm,
                 swa_sinks=v_swa_sinks, w_out=v_w_out)

    my = _lin(_my_coords())

    def shards(l):
        return [w_in[l].astype(MXU_DTYPE).T, mla_w_qb[l].astype(MXU_DTYPE), mla_w_kvb[l].astype(MXU_DTYPE),
                w_out[l].astype(MXU_DTYPE), conv_w[l]]

    def inproj_weights(l, g_win_t):
        return _inproj_weights(l, norm_g, g_win_t.reshape(IN_COLS, D_MODEL))

    def mixer_weights(l, gathered):
        g_wqb, g_wkvb, g_wout, g_conv = gathered
        return _mixer_weights(
            l, mla_q_a_norm, g_wqb, mla_kv_a_norm, g_wkvb, mla_q_norm, mla_k_norm,
            jnp.transpose(g_conv, (1, 0, 2)).reshape(3, GROUP_WIDTH), swa_q_norm, swa_k_norm, swa_sinks,
            g_wout.reshape(D_MIX, D_MODEL))

    slot_of = dict(
        w_in=lambda g: jnp.transpose(g["w_in"].reshape(D_MODEL, N_DEV, IN_COLS // N_DEV), (1, 0, 2)),
        w_out=lambda g: g["w_out"].reshape(N_DEV, D_MIX // N_DEV, D_MODEL),
        w_qb=lambda g: g["w_qb"], w_kvb=lambda g: g["w_kvb"])

    def own_slot(landed, mine):
        return [lax.dynamic_update_index_in_dim(a, m, my, 0) for a, m in zip(landed, mine)]

    def landed(handle, after, name, mine):
        return own_slot(_push_wait(handle, after, name), mine)

    gather_in0 = _push_start(shards(0)[:1], "weight_gather_in0_start", gather=True)
    gather0 = _push_start(shards(0)[1:], "weight_gather0_start", gather=True)
    gather1 = _push_start(shards(1), "weight_gather1_start", gather=True)
    rope = _rope_tables(T, gather_in0["token"] + gather0["token"] + gather1["token"])
    lw0 = inproj_weights(0, landed(gather_in0, rope[0], "weight_gather_in0_wait", shards(0)[:1])[0])
    x1, sv0 = _layer_fwd(
        x[0], lw0, rope,
        late_weights=lambda proj: mixer_weights(0, landed(gather0, proj, "weight_gather0_wait", shards(0)[1:])))
    g1_all = landed(gather1, x1, "weight_gather1_wait", shards(1))
    (g2, loss_tile), sv1 = _layer_fwd(x1, dict(inproj_weights(1, g1_all[0]), **mixer_weights(1, g1_all[1:])), rope,
                                      target=loss_target[0])

    parts = {(1, "w_in"): ("w_in", "w_out", "w_qb", "w_kvb"), (0, "mixer"): ("w_out", "w_qb", "w_kvb"),
             (0, "w_in"): ("w_in",)}
    started = []

    def start_exchange(l, part, g):
        if (l, part) not in parts:
            return 0.0
        sl = [slot_of[n](g) for n in parts[(l, part)]]
        handle = _push_start(sl, "grad_exchange%d_%s_start" % (l, part), gather=False)
        started.append((l, part, sl, handle))
        return handle["token"]

    g1, grads1 = _layer_bwd(g2, sv1, sv1["lw"], rope, on_big_grads=functools.partial(start_exchange, 1))
    lw0b = dict(sv0["lw"], conv=sv0["lw"]["conv"] + started[0][3]["token"])
    grad_x, grads0 = _layer_bwd(g1, sv0, lw0b, rope, on_big_grads=functools.partial(start_exchange, 0))
    recv = {}
    for l, part, sl, handle in started:
        got = landed(handle, grad_x, "grad_exchange%d_%s_wait" % (l, part), [s[my] for s in sl])
        recv.update({(l, n): a for n, a in zip(parts[(l, part)], got)})
    grads = [grads0, grads1]
    r_win, r_wout, r_wqb, r_wkvb = [jnp.stack([recv[(0, n)], recv[(1, n)]], axis=1)
                                    for n in ("w_in", "w_out", "w_qb", "w_kvb")]

    small = jnp.concatenate([
        _pack_small(lambda name, l: grads[l][SMALL_GRAD_KEY[name]]),
        jnp.stack([g["conv"] for g in grads]).reshape(CONV_ROWS, LANES),
        loss_tile], axis=0)
    small = _small_all_reduce(small)
    loss = small[SMALL_ROWS + CONV_ROWS, 0]
    my = _lin(_my_coords())
    conv_g = lax.dynamic_slice_in_dim(small[SMALL_ROWS:SMALL_ROWS + CONV_ROWS].reshape(DEPTH, 3, GROUP_WIDTH),
                                      my * 64, 64, axis=2)

    out = {}

    def big(name, recv, rows, cols, tr):
        res = _adamw(recv.reshape(N_DEV, rows, cols), weights[name].reshape(rows, cols),
                     mom_m[name].reshape(rows, cols), mom_v[name].reshape(rows, cols), "adamw_" + name, tr)
        out[name] = [r.reshape(weights[name].shape) for r in res]

    big("w_in", r_win, DEPTH * D_MODEL, IN_COLS // N_DEV, 256)
    big("w_out", r_wout, DEPTH * D_MIX // N_DEV, D_MODEL, 192)
    big("mla_w_qb", r_wqb, DEPTH * MLA_Q_LORA, MLA_QK, 512)
    big("mla_w_kvb", r_wkvb, DEPTH * MLA_KV_LORA, 128, 256)

    pad_conv = lambda a: jnp.pad(a.reshape(-1), (0, 8 * LANES - 6 * 64)).reshape(8, LANES)
    cat = lambda src: jnp.concatenate([_pack_small(lambda name, l: src[name][l]), pad_conv(src["conv_w"])], axis=0)
    g_small = jnp.concatenate([small[:SMALL_ROWS], pad_conv(conv_g)], axis=0)
    res = _adamw(g_small[None], cat(weights), cat(mom_m), cat(mom_v), "adamw_small", SMALL_ROWS + 8)
    smalls = [_unpack_small(r[:SMALL_ROWS]) for r in res]
    for name, _ in SMALL:
        out[name] = [s[name] for s in smalls]
    out["conv_w"] = [r[SMALL_ROWS:].reshape(-1)[:6 * 64].reshape(DEPTH, 3, 64) for r in res]

    order = ["norm_g", "w_in", "mla_q_a_norm", "mla_w_qb", "mla_kv_a_norm", "mla_w_kvb", "mla_q_norm", "mla_k_norm",
             "conv_w", "swa_q_norm", "swa_k_norm", "swa_sinks", "w_out"]
    result = [loss, grad_x[None]]
    for idx in range(4):
        result += [out[name][idx] for name in order]
    return tuple(result)
```

```python
import functools

import jax
import jax.numpy as jnp
import numpy as np
from jax import lax
from jax.experimental import pallas as pl
from jax.experimental.pallas import tpu as pltpu

F32 = jnp.float32
MXU_DTYPE = jnp.bfloat16
WIRE_DTYPE = jnp.bfloat16

N_DEV = 8
DEPTH = 2
D_MODEL = 1024
GROUP_WIDTH = 512
D_MIX = 3 * GROUP_WIDTH
BLOCK = 128
RMS_EPS = 1e-6
NEG_INF = -1e30
HEADS = 8
MLA_QK = 96
MLA_NOPE = 64
MLA_ROPE = 32
MLA_Q_LORA = 256
MLA_KV_LORA = 128
ROPE_THETA = 10000.0
SWA_HEAD_DIM = 64
LANES = 128
IN_COLS = 4256

ADAM_LR = 0.001
ADAM_B1 = 0.9
ADAM_B2 = 0.999
ADAM_EPS = 1e-08
ADAM_WD = 0.01
ADAM_STEP = 10

NP = 4352
CB_GMLA, CB_CH, CB_CB, CB_CC, CB_GCONV, CB_GSWA, CB_SQ = 0, 1, 2, 3, 4, 5, 7
CB_QLAT = 12
CB_KVLAT, CB_KROPE = 26, 27
CB_SK, CB_SV = 32, 33
DPB_MIX, DPB_MLA, DPB_SQ, DPB_SKV = 0, 6, 7, 16

TM_PROJ = 512
TM_ROW = 256
TK = 256
TQ = 2 * TK
MLA_SCALE = MLA_QK ** -0.5
MLA_ONES_ROW = (64, 0)
LOG2E = 1.4426950408889634
LN2 = 0.6931471805599453
TM_SWA = 512
VMEM_MB = 2 ** 20


def _cp(sem, vmem_mb):
    return pltpu.CompilerParams(dimension_semantics=sem, vmem_limit_bytes=vmem_mb * VMEM_MB)


def _sds(shape, dtype):
    return jax.ShapeDtypeStruct(shape, dtype)


def _dot(a, b):
    return jnp.dot(a, b, preferred_element_type=F32)


def _dot_nt(a, b):
    return lax.dot_general(a, b, (((1,), (1,)), ((), ())), preferred_element_type=F32)


def _dot_tn(a, b):
    return lax.dot_general(a, b, (((0,), (0,)), ((), ())), preferred_element_type=F32)


def _rms(x, n):
    r = lax.rsqrt(jnp.sum(x * x, axis=-1, keepdims=True) * (1.0 / n) + RMS_EPS)
    return x * r, r


def _rms_bwd(dy, xhat, r, w, n):
    g = dy * w
    return r * (g - xhat * (jnp.sum(g * xhat, axis=-1, keepdims=True) * (1.0 / n)))


def _rms_halves(x, half1):
    x2 = x * x
    s0 = jnp.sum(jnp.where(half1, 0.0, x2), axis=-1, keepdims=True)
    s1 = jnp.sum(jnp.where(half1, x2, 0.0), axis=-1, keepdims=True)
    r = jnp.where(half1, lax.rsqrt(s1 * (1.0 / 64) + RMS_EPS), lax.rsqrt(s0 * (1.0 / 64) + RMS_EPS))
    return x * r, r


def _rms_halves_bwd(dy, xhat, r, w, half1):
    g = dy * w
    t = g * xhat
    m0 = jnp.sum(jnp.where(half1, 0.0, t), axis=-1, keepdims=True) * (1.0 / 64)
    m1 = jnp.sum(jnp.where(half1, t, 0.0), axis=-1, keepdims=True) * (1.0 / 64)
    return r * (g - xhat * jnp.where(half1, m1, m0))


def _sigmoid(x):
    return 1.0 / (1.0 + jnp.exp(-x))


def _rope(x, c, s1, s2):
    ax = x.ndim - 1
    return x * c + pltpu.roll(x, 112, ax) * s1 + pltpu.roll(x, 16, ax) * s2


def _rope_bwd(dy, c, s1, s2):
    ax = dy.ndim - 1
    return dy * c + pltpu.roll(dy * s1, 16, ax) + pltpu.roll(dy * s2, 112, ax)


def _fold_rows8(x):
    return jnp.sum(x.reshape(x.shape[0] // 8, 8, x.shape[1]), axis=0)


def _row0(v, rows=8):
    row = lax.broadcasted_iota(jnp.int32, (rows, v.shape[1]), 0)
    return jnp.where(row == 0, jnp.broadcast_to(v, (rows, v.shape[1])), 0.0)


def _mm_nn(a, b, name, out_dtype=F32, residual=None, tm=TM_PROJ):
    M, K = a.shape
    N = b.shape[1]
    tm = min(tm, M)

    def body(*refs):
        if residual is None:
            a_ref, b_ref, o_ref = refs
            acc = _dot(a_ref[...].astype(MXU_DTYPE), b_ref[...])
        else:
            a_ref, b_ref, r_ref, o_ref = refs
            acc = _dot(a_ref[...].astype(MXU_DTYPE), b_ref[...]) + r_ref[...]
        o_ref[...] = acc.astype(out_dtype)

    in_specs = [pl.BlockSpec((tm, K), lambda i: (i, 0)), pl.BlockSpec((K, N), lambda i: (0, 0))]
    args = [a, b]
    if residual is not None:
        in_specs.append(pl.BlockSpec((tm, N), lambda i: (i, 0)))
        args.append(residual)
    return pl.pallas_call(
        body, name=name, grid=(M // tm,), in_specs=in_specs,
        out_specs=pl.BlockSpec((tm, N), lambda i: (i, 0)), out_shape=_sds((M, N), out_dtype),
        compiler_params=_cp(("parallel",), 48))(*args)


def _mm_tn(a, b, name, out_dtype, tn, tk=512):
    T, M = a.shape
    N = b.shape[1]
    tk = min(tk, T)
    nk = T // tk

    def body(a_ref, b_ref, o_ref, acc_ref):
        k = pl.program_id(1)

        @pl.when(k == 0)
        def _():
            acc_ref[...] = jnp.zeros_like(acc_ref)

        acc_ref[...] += _dot_tn(a_ref[...].astype(MXU_DTYPE), b_ref[...].astype(MXU_DTYPE))

        @pl.when(k == nk - 1)
        def _():
            o_ref[...] = acc_ref[...].astype(out_dtype)

    return pl.pallas_call(
        body, name=name, grid=(N // tn, nk),
        in_specs=[pl.BlockSpec((tk, M), lambda n, k: (k, 0)), pl.BlockSpec((tk, tn), lambda n, k: (k, n))],
        out_specs=pl.BlockSpec((M, tn), lambda n, k: (0, n)), out_shape=_sds((M, N), out_dtype),
        scratch_shapes=[pltpu.VMEM((M, tn), F32)],
        compiler_params=_cp(("parallel", "arbitrary"), 48))(a, b)


def _inproj_fwd(x, ng, wp):
    T, D = x.shape
    tm = min(TM_PROJ, T)

    def body(x_ref, g_ref, w_ref, proj_ref, h_ref):
        xhat, _ = _rms(x_ref[...], D)
        h = (xhat * g_ref[...]).astype(MXU_DTYPE)
        h_ref[...] = h
        proj_ref[...] = _dot(h, w_ref[...])

    return pl.pallas_call(
        body, name="inproj_fwd", grid=(T // tm,),
        in_specs=[pl.BlockSpec((tm, D), lambda i: (i, 0)), pl.BlockSpec((1, D), lambda i: (0, 0)),
                  pl.BlockSpec((D, NP), lambda i: (0, 0))],
        out_specs=[pl.BlockSpec((tm, NP), lambda i: (i, 0)), pl.BlockSpec((tm, D), lambda i: (i, 0))],
        out_shape=[_sds((T, NP), F32), _sds((T, D), MXU_DTYPE)],
        compiler_params=_cp(("parallel",), 48))(x, ng, wp)


def _mla_prep_fwd(proj, lw, rope):
    T = proj.shape[0]
    tk = min(TK, T // 2)
    nsub = 2
    tm = nsub * tk

    def body(ql_ref, kvl_ref, kr_ref, qa_ref, kva_ref, wq_ref, wk_ref, wv_ref, qn_ref, kn_ref,
             c_ref, s1_ref, s2_ref, q_out, k_out, kt_out, vt_out):
        c, s1, s2 = c_ref[...], s1_ref[...], s2_ref[...]
        qhat, _ = _rms(ql_ref[...], MLA_Q_LORA)
        qn = (qhat * qa_ref[...]).astype(MXU_DTYPE)
        khat, _ = _rms(kvl_ref[...], MLA_KV_LORA)
        kvn = (khat * kva_ref[...]).astype(MXU_DTYPE)
        kr = kr_ref[...]
        half1 = lax.broadcasted_iota(jnp.int32, (tm, LANES), 1) >= 64
        ones_row = lax.broadcasted_iota(jnp.int32, (LANES, 1), 0)
        q3, _ = _rms(jnp.stack([_dot(qn, wq_ref[h]) for h in range(HEADS)]), MLA_QK)
        q_out[...] = (_rope(q3 * qn_ref[...], c, s1, s2) * (MLA_SCALE * LOG2E)).astype(MXU_DTYPE)
        k3, _ = _rms(jnp.stack([_dot(kvn, wk_ref[h]) for h in range(HEADS)]) + kr, MLA_QK)
        k3 = _rope(k3 * kn_ref[...], c, s1, s2)
        k_out[...] = k3.astype(MXU_DTYPE)
        for h in range(HEADS):
            for t in range(nsub):
                kt_out[h, t] = k3[h, tk * t:tk * (t + 1)].T.astype(MXU_DTYPE)
        v = _dot(kvn, wv_ref[...])
        for h in range(HEADS):
            vp = v[:, LANES * (h // 2):LANES * (h // 2 + 1)]
            own = half1 if h % 2 else jnp.logical_not(half1)
            vp = jnp.where(own, vp, 0.0)
            for t in range(nsub):
                vpt = vp[tk * t:tk * (t + 1)].T
                vt_out[h, t] = jnp.where(ones_row == MLA_ONES_ROW[h % 2], 1.0, vpt).astype(MXU_DTYPE)

    full = lambda shape: pl.BlockSpec(shape, lambda i: (0,) * len(shape))
    hd = pl.BlockSpec((HEADS, tm, LANES), lambda i: (0, i, 0))
    hdt = pl.BlockSpec((HEADS, nsub, LANES, tk), lambda i: (0, i, 0, 0))
    nat = _sds((HEADS, T, LANES), MXU_DTYPE)
    tr = _sds((HEADS, T // tk, LANES, tk), MXU_DTYPE)
    return pl.pallas_call(
        body, name="mla_prep_fwd", grid=(T // tm,),
        in_specs=[pl.BlockSpec((tm, 256), lambda i: (i, CB_QLAT)), pl.BlockSpec((tm, LANES), lambda i: (i, CB_KVLAT)),
                  pl.BlockSpec((tm, LANES), lambda i: (i, CB_KROPE)),
                  full((1, 256)), full((1, LANES)), full((HEADS, 256, LANES)), full((HEADS, LANES, LANES)),
                  full((LANES, 512)), full((1, LANES)), full((1, LANES)),
                  pl.BlockSpec((tm, LANES), lambda i: (i, 0)), pl.BlockSpec((tm, LANES), lambda i: (i, 0)),
                  pl.BlockSpec((tm, LANES), lambda i: (i, 0))],
        out_specs=[hd, hd, hdt, hdt],
        out_shape=[nat, nat, tr, tr],
        compiler_params=_cp(("parallel",), 32))(
            proj, proj, proj, lw["qa"], lw["kva"], lw["wq"], lw["wk"], lw["wv"], lw["qn"], lw["kn"],
            rope[0], rope[1], rope[2])


def _mla_attn_fwd(q, k, vt):
    T = q.shape[1]
    tk = min(TK, T // 2)
    tq = 2 * tk

    def body(q_ref, k_ref, vt_ref, o_ref, lse_ref, acc_s, m_s, s_a, s_b):
        i = pl.program_id(1)
        key = lax.broadcasted_iota(jnp.int32, (tk, tq), 0)
        qry = lax.broadcasted_iota(jnp.int32, (tk, tq), 1)
        qs = [q_ref[0], q_ref[1]]
        acc_s[...] = jnp.zeros_like(acc_s)
        m_s[...] = jnp.full(m_s.shape, NEG_INF, F32)

        def scores(kj, buf):
            rows = pl.ds(pl.multiple_of(kj * tk, tk), tk)
            for r in range(2):
                buf[r] = _dot_nt(k_ref[r, rows, :], qs[r])

        def consume(kj, buf, diag):
            for r in range(2):
                s = buf[r]
                if diag is not None:
                    s = jnp.where(key + diag * tk <= qry, s, NEG_INF)
                m_old = m_s[r]
                m_new = jnp.maximum(m_old, jnp.max(s, axis=0, keepdims=True))
                alpha = jnp.exp2(m_old - m_new)
                p = jnp.exp2(s - m_new)
                m_s[r] = m_new
                acc_s[r] = alpha * acc_s[r] + _dot(vt_ref[r, kj], p.astype(MXU_DTYPE))

        scores(0, s_a)

        def pair(kj):
            scores(kj + 1, s_b)
            consume(kj, s_a, None)
            scores(kj + 2, s_a)
            consume(kj + 1, s_b, None)

        def quad(kq, carry):
            pair(4 * kq)
            pair(4 * kq + 2)
            return carry

        lax.fori_loop(0, i // 2, quad, 0)

        @pl.when(i % 2 == 1)
        def _():
            pair(2 * i - 2)

        scores(2 * i + 1, s_b)
        consume(2 * i, s_a, 0)
        consume(2 * i + 1, s_b, 1)
        l = [acc_s[r, pl.ds(MLA_ONES_ROW[r], 1), :] for r in range(2)]
        head0 = lax.broadcasted_iota(jnp.int32, (LANES, 1), 0) < 64
        o_ref[...] = jnp.where(head0, acc_s[0] / l[0], acc_s[1] / l[1]).T
        for r in range(2):
            lse_ref[r] = m_s[r] + jnp.log2(l[r])

    return pl.pallas_call(
        body, name="mla_attn_fwd", grid=(HEADS // 2, T // tq),
        in_specs=[pl.BlockSpec((2, tq, LANES), lambda j, i: (j, i, 0)),
                  pl.BlockSpec((2, T, LANES), lambda j, i: (j, 0, 0)),
                  pl.BlockSpec((2, T // tk, LANES, tk), lambda j, i: (j, 0, 0, 0))],
        out_specs=[pl.BlockSpec((tq, LANES), lambda j, i: (i, j)),
                   pl.BlockSpec((2, 1, tq), lambda j, i: (j, 0, i))],
        out_shape=[_sds((T, GROUP_WIDTH), F32), _sds((HEADS, 1, T), F32)],
        scratch_shapes=[pltpu.VMEM((2, LANES, tq), F32), pltpu.VMEM((2, 1, tq), F32),
                        pltpu.VMEM((2, tk, tq), F32), pltpu.VMEM((2, tk, tq), F32)],
        compiler_params=_cp(("parallel", "arbitrary"), 40))(q, k, vt)


def _swa_kv_variants(x, half1):
    xs = pltpu.roll(x, 64, 1)
    out = {}
    for g in range(2):
        for r in range(2):
            own = half1 if r else jnp.logical_not(half1)
            out[(g, r)] = jnp.where(own, x if g == r else xs, 0.0).astype(MXU_DTYPE)
    return out


def _swa_alibi():
    ki = np.arange(2 * BLOCK)[:, None]
    qi = np.arange(BLOCK)[None, :]
    dist = BLOCK + qi - ki
    slopes = 2.0 ** -(np.arange(HEADS) + 1.0)
    tab = np.where(((dist >= 0) & (dist < BLOCK))[None], slopes[:, None, None] * dist[None], 1e30)
    return jnp.asarray(tab, F32)


def _swa_kv_variants_t(xt, rows1):
    xs = pltpu.roll(xt, 64, 0)
    out = {}
    for g in range(2):
        for r in range(2):
            own = rows1 if r else jnp.logical_not(rows1)
            out[(g, r)] = jnp.where(own, xt if g == r else xs, 0.0).astype(MXU_DTYPE)
    return out


def _swa_probs(i, nb, q_ref, k_ref, v_ref, pk_ref, pv_ref, qw_ref, kw_ref, alibi_ref, sink_ref):
    scale = SWA_HEAD_DIM ** -0.5
    half1 = lax.broadcasted_iota(jnp.int32, (1, LANES), 1) >= 64
    k_all = jnp.concatenate([pk_ref[...], k_ref[...]], axis=0)
    v_all = jnp.concatenate([pv_ref[...], v_ref[...]], axis=0)
    khat, _ = _rms_halves(k_all, half1)
    kn = khat * kw_ref[...]
    kp = _swa_kv_variants(kn, half1)
    qhat, qr, qn, qt = [], [], [], []
    for j in range(4):
        xh, r = _rms_halves(q_ref[:, LANES * j:LANES * (j + 1)], half1)
        qf = xh * qw_ref[...]
        qhat.append(xh)
        qr.append(r)
        qn.append(qf.astype(MXU_DTYPE))
        qt.append(qf.T.astype(MXU_DTYPE))
    key = lax.broadcasted_iota(jnp.int32, (2 * BLOCK, BLOCK), 0)
    first = jnp.where((i == 0) & (key < BLOCK), NEG_INF, 0.0)
    s = jnp.stack([_dot(kp[(h // 4, h % 2)][BLOCK * b:BLOCK * (b + 2)], qt[h // 2][:, BLOCK * b:BLOCK * (b + 1)])
                   for b in range(nb) for h in range(HEADS)]) * scale - alibi_ref[...]
    s = jnp.concatenate([s[:HEADS] + first, s[HEADS:]], axis=0) if nb > 1 else s + first
    sink = jnp.stack([jnp.full((1, 1), sink_ref[h], F32) for _ in range(nb) for h in range(HEADS)])
    m = jnp.maximum(jnp.max(s, axis=1, keepdims=True), sink)
    e = jnp.exp(s - m)
    es = jnp.exp(sink - m)
    inv = 1.0 / (jnp.sum(e, axis=1, keepdims=True) + es)
    return e * inv, es * inv, dict(half1=half1, kn=kn, kp=kp, v_all=v_all, qhat=qhat, qr=qr, qn=qn)


def _swa_fwd(proj, lw):
    T = proj.shape[0]
    tm = min(TM_SWA, T)
    nb = tm // BLOCK

    def body(q_ref, k_ref, v_ref, pk_ref, pv_ref, qw_ref, kw_ref, alibi_ref, sink_ref, o_ref):
        p, _, c = _swa_probs(pl.program_id(0), nb, q_ref, k_ref, v_ref, pk_ref, pv_ref, qw_ref, kw_ref, alibi_ref,
                             sink_ref)
        p = p.astype(MXU_DTYPE)
        rows1 = lax.broadcasted_iota(jnp.int32, (LANES, 1), 0) >= 64
        vpt = _swa_kv_variants_t(c["v_all"].T, rows1)
        for j in range(4):
            g = j // 2
            o_t = [_dot(vpt[(g, 0)][:, BLOCK * b:BLOCK * (b + 2)], p[HEADS * b + 2 * j])
                   + _dot(vpt[(g, 1)][:, BLOCK * b:BLOCK * (b + 2)], p[HEADS * b + 2 * j + 1]) for b in range(nb)]
            o_t = jnp.concatenate(o_t, axis=1) if nb > 1 else o_t[0]
            o_ref[:, LANES * j:LANES * (j + 1)] = o_t.T

    prev = lambda cb: pl.BlockSpec((BLOCK, LANES), lambda i: (jnp.maximum(i * nb - 1, 0), cb))
    return pl.pallas_call(
        body, name="swa_fwd", grid=(T // tm,),
        in_specs=[pl.BlockSpec((tm, 512), lambda i: (i, CB_SQ)), pl.BlockSpec((tm, LANES), lambda i: (i, CB_SK)),
                  pl.BlockSpec((tm, LANES), lambda i: (i, CB_SV)), prev(CB_SK), prev(CB_SV),
                  pl.BlockSpec((1, LANES), lambda i: (0, 0)), pl.BlockSpec((1, LANES), lambda i: (0, 0)),
                  pl.BlockSpec((nb * HEADS, 2 * BLOCK, BLOCK), lambda i: (0, 0, 0)),
                  pl.BlockSpec(memory_space=pltpu.SMEM)],
        out_specs=pl.BlockSpec((tm, 512), lambda i: (i, 0)),
        out_shape=_sds((T, GROUP_WIDTH), F32),
        compiler_params=_cp(("parallel",), 40))(
            proj, proj, proj, proj, proj, lw["sqn"], lw["skn"], jnp.tile(_swa_alibi(), (nb, 1, 1)), lw["sinks"])


def _shift_down(u, prev, n, row):
    tm = u.shape[0]
    out = pltpu.roll(u, n, 0)
    row8 = lax.broadcasted_iota(jnp.int32, prev.shape, 0)
    for t in range(n):
        src = jnp.sum(jnp.where(row8 == 8 - n + t, prev, 0.0), axis=0, keepdims=True)
        out = jnp.where(row == t, src, out)
    return out


def _shift_up(u, nxt, n, row):
    tm = u.shape[0]
    out = pltpu.roll(u, tm - n, 0)
    row8 = lax.broadcasted_iota(jnp.int32, nxt.shape, 0)
    for t in range(n):
        src = jnp.sum(jnp.where(row8 == t, nxt, 0.0), axis=0, keepdims=True)
        out = jnp.where(row == tm - n + t, src, out)
    return out


def _mix_fwd(proj, o_mla, o_swa, conv_w):
    T = proj.shape[0]
    tm = min(TM_ROW, T)

    def body(gm_ref, ch_ref, cb_ref, cc_ref, gc_ref, gs_ref, pch_ref, pcc_ref, om_ref, os_ref, w_ref, y_ref):
        i = pl.program_id(0)
        row = lax.broadcasted_iota(jnp.int32, (tm, GROUP_WIDTH), 0)
        u = cc_ref[...] * ch_ref[...]
        u_prev = jnp.where(i > 0, pcc_ref[...] * pch_ref[...], 0.0)
        z = (w_ref[0:1, :] * _shift_down(u, u_prev, 2, row) + w_ref[1:2, :] * _shift_down(u, u_prev, 1, row)
             + w_ref[2:3, :] * u)
        gm, gc, gs = gm_ref[...], gc_ref[...], gs_ref[...]
        y_ref[:, 0:512] = (om_ref[...] * (gm * _sigmoid(gm))).astype(MXU_DTYPE)
        y_ref[:, 512:1024] = (cb_ref[...] * z * (gc * _sigmoid(gc))).astype(MXU_DTYPE)
        y_ref[:, 1024:1536] = (os_ref[...] * (gs * _sigmoid(gs))).astype(MXU_DTYPE)

    blk = lambda cb: pl.BlockSpec((tm, 512), lambda i: (i, cb))
    prev = lambda cb: pl.BlockSpec((8, 512), lambda i: (jnp.maximum(i * (tm // 8) - 1, 0), cb))
    tile = pl.BlockSpec((tm, 512), lambda i: (i, 0))
    return pl.pallas_call(
        body, name="mix_fwd", grid=(T // tm,),
        in_specs=[blk(CB_GMLA), blk(CB_CH), blk(CB_CB), blk(CB_CC), blk(CB_GCONV), blk(CB_GSWA),
                  prev(CB_CH), prev(CB_CC), tile, tile, pl.BlockSpec((8, 512), lambda i: (0, 0))],
        out_specs=pl.BlockSpec((tm, D_MIX), lambda i: (i, 0)),
        out_shape=_sds((T, D_MIX), MXU_DTYPE),
        compiler_params=_cp(("parallel",), 32))(
            proj, proj, proj, proj, proj, proj, proj, proj, o_mla, o_swa, conv_w)


def _outproj_loss(ycat, wo, x, target):
    T, D = x.shape
    K = ycat.shape[1]
    tm = min(TM_PROJ, T)
    nt = T // tm

    def body(y_ref, w_ref, x_ref, t_ref, g_ref, loss_ref, acc_ref):
        i = pl.program_id(0)

        @pl.when(i == 0)
        def _():
            acc_ref[...] = jnp.zeros_like(acc_ref)

        err = _dot(y_ref[...], w_ref[...]) + x_ref[...] - t_ref[...]
        g_ref[...] = err * (1.0 / D)
        acc_ref[...] += _fold_rows8(err * err)

        @pl.when(i == nt - 1)
        def _():
            tot = jnp.sum(jnp.sum(acc_ref[...], axis=1, keepdims=True), axis=0, keepdims=True)
            loss_ref[...] = jnp.broadcast_to(tot * (0.5 / D), (8, LANES))

    tile = pl.BlockSpec((tm, D), lambda i: (i, 0))
    return pl.pallas_call(
        body, name="outproj_loss", grid=(nt,),
        in_specs=[pl.BlockSpec((tm, K), lambda i: (i, 0)), pl.BlockSpec((K, D), lambda i: (0, 0)), tile, tile],
        out_specs=[tile, pl.BlockSpec((8, LANES), lambda i: (0, 0))],
        out_shape=[_sds((T, D), F32), _sds((8, LANES), F32)],
        scratch_shapes=[pltpu.VMEM((8, D), F32)],
        compiler_params=_cp(("arbitrary",), 48))(ycat, wo, x, target)


def _outproj_bwd(g, ycat, wot):
    T, D = g.shape
    K = ycat.shape[1]
    tm = min(512, T)
    nt = T // tm

    def body(g_ref, y_ref, wt_ref, dy_ref, dw_ref, acc_ref):
        i = pl.program_id(0)

        @pl.when(i == 0)
        def _():
            acc_ref[...] = jnp.zeros_like(acc_ref)

        gb = g_ref[...].astype(MXU_DTYPE)
        dy_ref[...] = _dot(gb, wt_ref[...])
        acc_ref[...] += _dot_tn(y_ref[...], gb)

        @pl.when(i == nt - 1)
        def _():
            dw_ref[...] = acc_ref[...].astype(WIRE_DTYPE)

    return pl.pallas_call(
        body, name="outproj_bwd", grid=(nt,),
        in_specs=[pl.BlockSpec((tm, D), lambda i: (i, 0)), pl.BlockSpec((tm, K), lambda i: (i, 0)),
                  pl.BlockSpec((D, K), lambda i: (0, 0))],
        out_specs=[pl.BlockSpec((tm, K), lambda i: (i, 0)), pl.BlockSpec((K, D), lambda i: (0, 0))],
        out_shape=[_sds((T, K), F32), _sds((K, D), WIRE_DTYPE)],
        scratch_shapes=[pltpu.VMEM((K, D), F32)],
        compiler_params=_cp(("arbitrary",), 48))(g, ycat, wot)


def _mix_bwd(dycat, proj, o_mla, o_swa, conv_w):
    T = proj.shape[0]
    tm = min(TM_ROW, T)
    nt = T // tm

    def body(dym_ref, dyc_ref, dys_ref, gm_ref, ch_ref, cb_ref, cc_ref, gc_ref, gs_ref, pch_ref, pcc_ref,
             ndy_ref, ncb_ref, ngc_ref, om_ref, os_ref, w_ref,
             d1_ref, dom_ref, dos_ref, dw_ref):
        i = pl.program_id(0)

        @pl.when(i == 0)
        def _():
            dw_ref[...] = jnp.zeros_like(dw_ref)

        row = lax.broadcasted_iota(jnp.int32, (tm, GROUP_WIDTH), 0)

        def gate(g):
            sg = _sigmoid(g)
            return g * sg, sg * (1.0 + g * (1.0 - sg))

        gm = gm_ref[...]
        silu, dsilu = gate(gm)
        dym = dym_ref[...]
        dom_ref[...] = dym * silu
        d1_ref[:, 0:512] = (dym * om_ref[...] * dsilu).astype(MXU_DTYPE)

        gs = gs_ref[...]
        silu, dsilu = gate(gs)
        dys = dys_ref[...]
        dos_ref[...] = dys * silu
        d1_ref[:, 2560:3072] = (dys * os_ref[...] * dsilu).astype(MXU_DTYPE)

        ch, cb, cc, gc, dyc = ch_ref[...], cb_ref[...], cc_ref[...], gc_ref[...], dyc_ref[...]
        w0, w1, w2 = w_ref[0:1, :], w_ref[1:2, :], w_ref[2:3, :]
        u = cc * ch
        u_prev = jnp.where(i > 0, pcc_ref[...] * pch_ref[...], 0.0)
        u1 = _shift_down(u, u_prev, 1, row)
        u2 = _shift_down(u, u_prev, 2, row)
        z = w0 * u2 + w1 * u1 + w2 * u
        silu, dsilu = gate(gc)
        dz = dyc * cb * silu
        ngc = ngc_ref[...]
        dz_next = jnp.where(i < nt - 1, ndy_ref[...] * ncb_ref[...] * (ngc * _sigmoid(ngc)), 0.0)
        du = w2 * dz + w1 * _shift_up(dz, dz_next, 1, row) + w0 * _shift_up(dz, dz_next, 2, row)
        d1_ref[:, 512:1024] = (du * cc).astype(MXU_DTYPE)
        d1_ref[:, 1024:1536] = (dyc * z * silu).astype(MXU_DTYPE)
        d1_ref[:, 1536:2048] = (du * ch).astype(MXU_DTYPE)
        d1_ref[:, 2048:2560] = (dyc * cb * z * dsilu).astype(MXU_DTYPE)
        row8 = lax.broadcasted_iota(jnp.int32, (8, GROUP_WIDTH), 0)
        dw = jnp.zeros((8, GROUP_WIDTH), F32)
        for t, shifted in enumerate((u2, u1, u)):
            dw = dw + jnp.where(row8 == t, jnp.sum(dz * shifted, axis=0, keepdims=True), 0.0)
        dw_ref[...] += dw

    blk = lambda cb: pl.BlockSpec((tm, 512), lambda i: (i, cb))
    prev = lambda cb: pl.BlockSpec((8, 512), lambda i: (jnp.maximum(i * (tm // 8) - 1, 0), cb))
    nxt = lambda cb: pl.BlockSpec((8, 512), lambda i: (jnp.minimum((i + 1) * (tm // 8), T // 8 - 1), cb))
    tile = pl.BlockSpec((tm, 512), lambda i: (i, 0))
    return pl.pallas_call(
        body, name="mix_bwd", grid=(nt,),
        in_specs=[blk(0), blk(1), blk(2), blk(CB_GMLA), blk(CB_CH), blk(CB_CB), blk(CB_CC), blk(CB_GCONV),
                  blk(CB_GSWA), prev(CB_CH), prev(CB_CC), nxt(1), nxt(CB_CB), nxt(CB_GCONV), tile, tile,
                  pl.BlockSpec((8, 512), lambda i: (0, 0))],
        out_specs=[pl.BlockSpec((tm, 3072), lambda i: (i, DPB_MIX)), tile, tile,
                   pl.BlockSpec((8, 512), lambda i: (0, 0))],
        out_shape=[_sds((T, NP), MXU_DTYPE), _sds((T, 512), F32), _sds((T, 512), F32), _sds((8, 512), F32)],
        compiler_params=_cp(("arbitrary",), 48))(
            dycat, dycat, dycat, proj, proj, proj, proj, proj, proj, proj, proj, dycat, proj, proj,
            o_mla, o_swa, conv_w)


def _swa_bwd(proj, o_swa, do_swa, lw, dproj):
    T = proj.shape[0]
    tm = min(TM_SWA, T)
    nb = tm // BLOCK
    scale = SWA_HEAD_DIM ** -0.5

    def body(q_ref, k_ref, v_ref, pk_ref, pv_ref, o_ref, do_ref, qw_ref, kw_ref, alibi_ref, sink_ref, dproj_in,
             dq_ref, dk_ref, dv_ref, dqw_ref, dsink_ref):
        i = pl.program_id(0)

        @pl.when(i == 0)
        def _():
            dk_ref[...] = jnp.zeros_like(dk_ref)
            dv_ref[...] = jnp.zeros_like(dv_ref)
            dqw_ref[...] = jnp.zeros_like(dqw_ref)
            dsink_ref[...] = jnp.zeros_like(dsink_ref)

        p, p_sink, c = _swa_probs(i, nb, q_ref, k_ref, v_ref, pk_ref, pv_ref, qw_ref, kw_ref, alibi_ref, sink_ref)
        half1, kp, qn, qhat, qr = c["half1"], c["kp"], c["qn"], c["qhat"], c["qr"]
        rows1 = lax.broadcasted_iota(jnp.int32, (LANES, 1), 0) >= 64
        kpt = _swa_kv_variants_t(c["kn"].T, rows1)
        vp = _swa_kv_variants(c["v_all"], half1)
        qw = qw_ref[...]
        rows = [slice(BLOCK * b, BLOCK * (b + 1)) for b in range(nb)]
        keys = [slice(BLOCK * b, BLOCK * (b + 2)) for b in range(nb)]
        dob, dot_b, dd0, dd1 = [], [], [], []
        for j in range(4):
            cols = slice(LANES * j, LANES * (j + 1))
            do = do_ref[:, cols]
            do_t = do.T
            prod_t = do_t * o_ref[:, cols].T
            dob.append(do.astype(MXU_DTYPE))
            dot_b.append(do_t.astype(MXU_DTYPE))
            dd0.append(jnp.sum(jnp.where(rows1, 0.0, prod_t), axis=0, keepdims=True))
            dd1.append(jnp.sum(jnp.where(rows1, prod_t, 0.0), axis=0, keepdims=True))
        dd = jnp.stack([(dd1 if h % 2 else dd0)[h // 2][:, rows[b]] for b in range(nb) for h in range(HEADS)])
        dp = jnp.stack([_dot(vp[(h // 4, h % 2)][keys[b]], dot_b[h // 2][:, rows[b]])
                        for b in range(nb) for h in range(HEADS)])
        ds = (p * (dp - dd) * scale).astype(MXU_DTYPE)
        dsink = -jnp.sum(p_sink * dd, axis=2, keepdims=True)
        pb = p.astype(MXU_DTYPE)

        dqw = jnp.zeros((1, LANES), F32)
        for j in range(4):
            g = j // 2
            dqn_t = [_dot(kpt[(g, 0)][:, keys[b]], ds[HEADS * b + 2 * j])
                     + _dot(kpt[(g, 1)][:, keys[b]], ds[HEADS * b + 2 * j + 1]) for b in range(nb)]
            dqn = (jnp.concatenate(dqn_t, axis=1) if nb > 1 else dqn_t[0]).T
            dqw = dqw + jnp.sum(dqn * qhat[j], axis=0, keepdims=True)
            dq_ref[:, LANES * j:LANES * (j + 1)] = _rms_halves_bwd(dqn, qhat[j], qr[j], qw, half1).astype(MXU_DTYPE)
        dqw_ref[...] += _row0(dqw + pltpu.roll(dqw, 64, 1))

        dk_tot = jnp.zeros((tm + BLOCK, LANES), F32)
        dv_tot = jnp.zeros((tm + BLOCK, LANES), F32)
        for b in range(nb):
            dk_b = jnp.zeros((2 * BLOCK, LANES), F32)
            dv_b = jnp.zeros((2 * BLOCK, LANES), F32)
            for g in range(2):
                for r in range(2):
                    own = half1 if r else jnp.logical_not(half1)
                    ha, hb = HEADS * b + 4 * g + r, HEADS * b + 4 * g + 2 + r
                    qa, qb = qn[2 * g][rows[b]], qn[2 * g + 1][rows[b]]
                    da, db = dob[2 * g][rows[b]], dob[2 * g + 1][rows[b]]
                    dkp = jnp.where(own, _dot(ds[ha], qa) + _dot(ds[hb], qb), 0.0)
                    dvp = jnp.where(own, _dot(pb[ha], da) + _dot(pb[hb], db), 0.0)
                    if g != r:
                        dkp = pltpu.roll(dkp, 64, 1)
                        dvp = pltpu.roll(dvp, 64, 1)
                    dk_b = dk_b + dkp
                    dv_b = dv_b + dvp
            pad = lambda x: jnp.concatenate(
                [z for z in (jnp.zeros((BLOCK * b, LANES), F32), x, jnp.zeros((BLOCK * (nb - 1 - b), LANES), F32))
                 if z.shape[0]], axis=0)
            dk_tot = dk_tot + pad(dk_b)
            dv_tot = dv_tot + pad(dv_b)
        dst = pl.ds(pl.multiple_of(i * tm, BLOCK), tm + BLOCK)
        dk_ref[dst, :] += dk_tot
        dv_ref[dst, :] += dv_tot

        row8 = lax.broadcasted_iota(jnp.int32, (8, LANES), 0)
        dsink_tile = jnp.zeros((8, LANES), F32)
        for b in range(nb):
            for h in range(HEADS):
                dsink_tile = dsink_tile + jnp.where(row8 == h, jnp.broadcast_to(dsink[HEADS * b + h], (8, LANES)), 0.0)
        dsink_ref[...] += dsink_tile

    prev = lambda cb: pl.BlockSpec((BLOCK, LANES), lambda i: (jnp.maximum(i * nb - 1, 0), cb))
    tile = pl.BlockSpec((tm, 512), lambda i: (i, 0))
    small = pl.BlockSpec((8, LANES), lambda i: (0, 0))
    acc = pl.BlockSpec((T + BLOCK, LANES), lambda i: (0, 0))
    return pl.pallas_call(
        body, name="swa_bwd", grid=(T // tm,),
        in_specs=[pl.BlockSpec((tm, 512), lambda i: (i, CB_SQ)), pl.BlockSpec((tm, LANES), lambda i: (i, CB_SK)),
                  pl.BlockSpec((tm, LANES), lambda i: (i, CB_SV)), prev(CB_SK), prev(CB_SV), tile, tile,
                  pl.BlockSpec((1, LANES), lambda i: (0, 0)), pl.BlockSpec((1, LANES), lambda i: (0, 0)),
                  pl.BlockSpec((nb * HEADS, 2 * BLOCK, BLOCK), lambda i: (0, 0, 0)),
                  pl.BlockSpec(memory_space=pltpu.SMEM), pl.BlockSpec(memory_space=pl.ANY)],
        out_specs=[pl.BlockSpec((tm, 512), lambda i: (i, DPB_SQ)), acc, acc, small, small],
        out_shape=[_sds((T, NP), MXU_DTYPE), _sds((T + BLOCK, LANES), F32), _sds((T + BLOCK, LANES), F32),
                   _sds((8, LANES), F32), _sds((8, LANES), F32)],
        input_output_aliases={11: 0},
        compiler_params=_cp(("arbitrary",), 48))(
            proj, proj, proj, proj, proj, o_swa, do_swa, lw["sqn"], lw["skn"], jnp.tile(_swa_alibi(), (nb, 1, 1)),
            lw["sinks"], dproj)


def _swa_kv_bwd(proj, dkn, dv, lw, dproj):
    T = proj.shape[0]
    tm = min(TM_SWA, T)
    dkn, dv = dkn[BLOCK:], dv[BLOCK:]

    def body(k_ref, dkn_ref, dv_ref, kw_ref, dproj_in, d_ref, dkw_ref):
        i = pl.program_id(0)

        @pl.when(i == 0)
        def _():
            dkw_ref[...] = jnp.zeros_like(dkw_ref)

        half1 = lax.broadcasted_iota(jnp.int32, (1, LANES), 1) >= 64
        khat, kr = _rms_halves(k_ref[...], half1)
        dkn_t = dkn_ref[...]
        dkw = jnp.sum(dkn_t * khat, axis=0, keepdims=True)
        dkw_ref[...] += _row0(dkw + pltpu.roll(dkw, 64, 1))
        d_ref[:, 0:LANES] = _rms_halves_bwd(dkn_t, khat, kr, kw_ref[...], half1).astype(MXU_DTYPE)
        d_ref[:, LANES:2 * LANES] = dv_ref[...].astype(MXU_DTYPE)

    return pl.pallas_call(
        body, name="swa_kv_bwd", grid=(T // tm,),
        in_specs=[pl.BlockSpec((tm, LANES), lambda i: (i, CB_SK)), pl.BlockSpec((tm, LANES), lambda i: (i, 0)),
                  pl.BlockSpec((tm, LANES), lambda i: (i, 0)), pl.BlockSpec((1, LANES), lambda i: (0, 0)),
                  pl.BlockSpec(memory_space=pl.ANY)],
        out_specs=[pl.BlockSpec((tm, 2 * LANES), lambda i: (i, DPB_SKV)), pl.BlockSpec((8, LANES), lambda i: (0, 0))],
        out_shape=[_sds((T, NP), MXU_DTYPE), _sds((8, LANES), F32)],
        input_output_aliases={4: 0},
        compiler_params=_cp(("arbitrary",), 32))(proj, dkn, dv, lw["skn"], dproj)


def _mla_attn_bwd(q, k, kt, vt, o, do, lse):
    T = q.shape[1]
    tk = min(TK, T // 2)
    tq = 2 * tk

    def body(q_ref, k_ref, kt_ref, vt_ref, o_ref, do_ref, lse_ref, dq_ref, dk_ref, dv_ref, dq_s, lse_s, dd_s,
             s_a, s_b, p_a, p_b):
        h = pl.program_id(0)
        i = pl.program_id(1)

        @pl.when(i == 0)
        def _():
            dk_ref[...] = jnp.zeros_like(dk_ref)
            dv_ref[...] = jnp.zeros_like(dv_ref)

        qry = lax.broadcasted_iota(jnp.int32, (tq, tk), 0)
        key = lax.broadcasted_iota(jnp.int32, (tq, tk), 1)
        own = (lax.broadcasted_iota(jnp.int32, (1, LANES), 1) // 64) == (h % 2)
        do_own = jnp.where(own, do_ref[...], 0.0)
        dob = do_own.astype(MXU_DTYPE)
        dob_t = do_own.T.astype(MXU_DTYPE)
        qh = q_ref[0]
        qh_t = qh.astype(F32).T.astype(MXU_DTYPE)
        dd_col = jnp.sum(do_own * o_ref[...], axis=-1, keepdims=True)
        lse_col = jnp.broadcast_to(lse_ref[0], (LANES, tq)).T
        for c in range(tk // LANES):
            lse_s[:, LANES * c:LANES * (c + 1)] = lse_col
            dd_s[:, LANES * c:LANES * (c + 1)] = jnp.broadcast_to(dd_col, (tq, LANES))
        dq_s[...] = jnp.zeros_like(dq_s)

        def scores(kj, s_buf, p_buf):
            s_buf[...] = _dot(qh, kt_ref[0, kj])
            p_buf[...] = _dot(dob, vt_ref[0, kj])

        def consume(kj, s_buf, p_buf, diag):
            rows = pl.ds(pl.multiple_of(kj * tk, tk), tk)
            s = s_buf[...]
            if diag is not None:
                s = jnp.where(key + diag * tk <= qry, s, NEG_INF)
            p = jnp.exp2(s - lse_s[...])
            ds = (p * (p_buf[...] - dd_s[...])).astype(MXU_DTYPE)
            dq_s[...] += _dot(ds, k_ref[0, rows, :])
            dk_ref[0, kj] += _dot(qh_t, ds)
            dv_ref[0, kj] += _dot(dob_t, p.astype(MXU_DTYPE))

        scores(0, s_a, p_a)

        def pair(kj):
            scores(kj + 1, s_b, p_b)
            consume(kj, s_a, p_a, None)
            scores(kj + 2, s_a, p_a)
            consume(kj + 1, s_b, p_b, None)

        def quad(kq, carry):
            pair(4 * kq)
            pair(4 * kq + 2)
            return carry

        lax.fori_loop(0, i // 2, quad, 0)

        @pl.when(i % 2 == 1)
        def _():
            pair(2 * i - 2)

        scores(2 * i + 1, s_b, p_b)
        consume(2 * i, s_a, p_a, 0)
        consume(2 * i + 1, s_b, p_b, 1)
        dq_ref[0] = dq_s[...]

    res = pl.BlockSpec((1, T, LANES), lambda h, i: (h, 0, 0))
    res_t = pl.BlockSpec((1, T // tk, LANES, tk), lambda h, i: (h, 0, 0, 0))
    buf = pltpu.VMEM((tq, tk), F32)
    acc_t = _sds((HEADS, T // tk, LANES, tk), F32)
    return pl.pallas_call(
        body, name="mla_attn_bwd", grid=(HEADS, T // tq),
        in_specs=[pl.BlockSpec((1, tq, LANES), lambda h, i: (h, i, 0)), res, res_t, res_t,
                  pl.BlockSpec((tq, LANES), lambda h, i: (i, h // 2)),
                  pl.BlockSpec((tq, LANES), lambda h, i: (i, h // 2)),
                  pl.BlockSpec((1, 1, tq), lambda h, i: (h, 0, i))],
        out_specs=[pl.BlockSpec((1, tq, LANES), lambda h, i: (h, i, 0)), res_t, res_t],
        out_shape=[_sds((HEADS, T, LANES), F32), acc_t, acc_t],
        scratch_shapes=[pltpu.VMEM((tq, LANES), F32), buf, buf, buf, buf, buf, buf],
        compiler_params=_cp(("parallel", "arbitrary"), 48))(q, k, kt, vt, o, do, lse)


def _mla_prep_bwd(proj, dq, dk, dv, lw, rope, dproj):
    T = proj.shape[0]
    tm = min(TK, T // 2)

    def body(ql_ref, kvl_ref, kr_ref, dq_ref, dk_ref, dv_ref, qa_ref, kva_ref, wq_ref, wk_ref, wv_ref,
             wqt_ref, wkt_ref, wvt_ref, qn_ref, kn_ref, c_ref, s1_ref, s2_ref, dproj_in,
             d_ref, dwq_ref, dwk_ref, dwv_ref, dqa_ref, dkva_ref, dqn_ref, dkn_ref):
        i = pl.program_id(0)

        @pl.when(i == 0)
        def _():
            for ref in (dwq_ref, dwk_ref, dwv_ref, dqa_ref, dkva_ref, dqn_ref, dkn_ref):
                ref[...] = jnp.zeros_like(ref)

        c, s1, s2 = c_ref[...], s1_ref[...], s2_ref[...]
        lane = lax.broadcasted_iota(jnp.int32, (1, LANES), 1)
        qlhat, qlr = _rms(ql_ref[...], MLA_Q_LORA)
        qn = (qlhat * qa_ref[...]).astype(MXU_DTYPE)
        kvhat, kvr = _rms(kvl_ref[...], MLA_KV_LORA)
        kvn = (kvhat * kva_ref[...]).astype(MXU_DTYPE)
        kr = kr_ref[...]
        x3, r3 = _rms(jnp.stack([_dot(qn, wq_ref[h]) for h in range(HEADS)]), MLA_QK)
        dy3 = _rope_bwd(dq_ref[...] * MLA_SCALE, c, s1, s2)
        dqw = jnp.sum(jnp.sum(dy3 * x3, axis=0), axis=0, keepdims=True)
        dx3 = _rms_bwd(dy3, x3, r3, qn_ref[...], MLA_QK).astype(MXU_DTYPE)
        dqnl = jnp.zeros((tm, MLA_Q_LORA), F32)
        for h in range(HEADS):
            dwq_ref[h] += _dot_tn(qn, dx3[h])
            dqnl = dqnl + _dot(dx3[h], wqt_ref[h])

        x3, r3 = _rms(jnp.stack([_dot(kvn, wk_ref[h]) for h in range(HEADS)]) + kr, MLA_QK)
        dy3 = _rope_bwd(jnp.stack([dk_ref[h, 0].T for h in range(HEADS)]) * LN2, c, s1, s2)
        dkw = jnp.sum(jnp.sum(dy3 * x3, axis=0), axis=0, keepdims=True)
        dxf3 = _rms_bwd(dy3, x3, r3, kn_ref[...], MLA_QK)
        dkr = jnp.sum(dxf3, axis=0)
        dx3 = dxf3.astype(MXU_DTYPE)
        dkvn = jnp.zeros((tm, MLA_KV_LORA), F32)
        for h in range(HEADS):
            dwk_ref[h] += _dot_tn(kvn, dx3[h])
            dkvn = dkvn + _dot(dx3[h], wkt_ref[h])
        dvc = jnp.concatenate([(dv_ref[2 * j, 0] + dv_ref[2 * j + 1, 0]).T for j in range(4)],
                              axis=1).astype(MXU_DTYPE)
        dwv_ref[...] += _dot_tn(kvn, dvc)
        dkvn = dkvn + _dot(dvc, wvt_ref[...])
        dqa_ref[...] += _row0(jnp.sum(dqnl * qlhat, axis=0, keepdims=True))
        dkva_ref[...] += _row0(jnp.sum(dkvn * kvhat, axis=0, keepdims=True))
        dqn_ref[...] += _row0(dqw)
        dkn_ref[...] += _row0(dkw)
        d_ref[:, 0:256] = _rms_bwd(dqnl, qlhat, qlr, qa_ref[...], MLA_Q_LORA).astype(MXU_DTYPE)
        d_ref[:, 256:384] = _rms_bwd(dkvn, kvhat, kvr, kva_ref[...], MLA_KV_LORA).astype(MXU_DTYPE)
        d_ref[:, 384:512] = jnp.where((lane >= 64) & (lane < 96), dkr, 0.0).astype(MXU_DTYPE)

    full = lambda shape: pl.BlockSpec(shape, lambda i: (0,) * len(shape))
    hd = pl.BlockSpec((HEADS, tm, LANES), lambda i: (0, i, 0))
    hdt = pl.BlockSpec((HEADS, 1, LANES, tm), lambda i: (0, i, 0, 0))
    tab = pl.BlockSpec((tm, LANES), lambda i: (i, 0))
    return pl.pallas_call(
        body, name="mla_prep_bwd", grid=(T // tm,),
        in_specs=[pl.BlockSpec((tm, 256), lambda i: (i, CB_QLAT)), pl.BlockSpec((tm, LANES), lambda i: (i, CB_KVLAT)),
                  pl.BlockSpec((tm, LANES), lambda i: (i, CB_KROPE)), hd, hdt, hdt,
                  full((1, 256)), full((1, LANES)), full((HEADS, 256, LANES)), full((HEADS, LANES, LANES)),
                  full((LANES, 512)), full((HEADS, LANES, 256)), full((HEADS, LANES, LANES)), full((512, LANES)),
                  full((1, LANES)), full((1, LANES)), tab, tab, tab, pl.BlockSpec(memory_space=pl.ANY)],
        out_specs=[pl.BlockSpec((tm, 512), lambda i: (i, DPB_MLA)), full((HEADS, 256, LANES)),
                   full((HEADS, LANES, LANES)), full((LANES, 512)), full((8, 256)), full((8, LANES)),
                   full((8, LANES)), full((8, LANES))],
        out_shape=[_sds((T, NP), MXU_DTYPE), _sds((HEADS, 256, LANES), F32), _sds((HEADS, LANES, LANES), F32),
                   _sds((LANES, 512), F32), _sds((8, 256), F32), _sds((8, LANES), F32), _sds((8, LANES), F32),
                   _sds((8, LANES), F32)],
        input_output_aliases={19: 0},
        compiler_params=_cp(("arbitrary",), 48))(
            proj, proj, proj, dq, dk, dv, lw["qa"], lw["kva"], lw["wq"], lw["wk"], lw["wv"],
            lw["wqt"], lw["wkt"], lw["wvt"], lw["qn"], lw["kn"], rope[0], rope[1], rope[2], dproj)


def _inproj_bwd_dx(dproj, wpt, x, g_in, ng):
    T, D = x.shape
    tm = min(TM_PROJ, T)

    def body(dp_ref, wt_ref, x_ref, g_ref, w_ref, dx_ref, dw_ref):
        i = pl.program_id(0)

        @pl.when(i == 0)
        def _():
            dw_ref[...] = jnp.zeros_like(dw_ref)

        dh = _dot(dp_ref[...], wt_ref[...])
        xhat, r = _rms(x_ref[...], D)
        dw_ref[...] += _row0(jnp.sum(dh * xhat, axis=0, keepdims=True))
        dx_ref[...] = g_ref[...] + _rms_bwd(dh, xhat, r, w_ref[...], D)

    tile = pl.BlockSpec((tm, D), lambda i: (i, 0))
    return pl.pallas_call(
        body, name="inproj_bwd_dx", grid=(T // tm,),
        in_specs=[pl.BlockSpec((tm, NP), lambda i: (i, 0)), pl.BlockSpec((NP, D), lambda i: (0, 0)), tile, tile,
                  pl.BlockSpec((1, D), lambda i: (0, 0))],
        out_specs=[tile, pl.BlockSpec((8, D), lambda i: (0, 0))],
        out_shape=[_sds((T, D), F32), _sds((8, D), F32)],
        compiler_params=_cp(("arbitrary",), 48))(dproj, wpt, x, g_in, ng)


def _rope_tables(T, token=0.0):
    half = MLA_ROPE // 2
    inv_freq = jnp.power(jnp.float32(ROPE_THETA), -jnp.arange(half, dtype=F32) / half)
    z = lambda n: jnp.zeros((n,), F32)
    freq = jnp.concatenate([z(MLA_NOPE), inv_freq, inv_freq, z(32)])
    first = jnp.concatenate([z(64), jnp.ones((16,), F32), z(48)])
    second = jnp.concatenate([z(80), jnp.ones((16,), F32), z(32)])
    ang = (jnp.arange(T, dtype=F32) + token)[:, None] * freq[None, :]
    sin = jnp.sin(ang)
    return jnp.cos(ang), -sin * first[None, :], sin * second[None, :]


def _pad_lanes(v, n=LANES):
    v = v.reshape(1, -1)
    return jnp.pad(v, ((0, 0), (0, n - v.shape[1])))


def _pack_win_t(wt):
    z = lambda n: jnp.zeros((n, wt.shape[1]), wt.dtype)
    return jnp.concatenate([wt[416:2976], wt[3744:4256], wt[0:384], z(64), wt[384:416], z(32), wt[2976:3488],
                            wt[3488:3616], wt[3616:3744]], axis=0)


def _unpack_dwin(d):
    return jnp.concatenate([d[:, 3072:3456], d[:, 3520:3552], d[:, 0:2560], d[:, 3584:4096], d[:, 4096:4224],
                            d[:, 4224:4352], d[:, 2560:3072]], axis=1)


def _inproj_weights(l, norm_g, w_in_t):
    wpt = _pack_win_t(w_in_t)
    return dict(ng=norm_g[l].reshape(1, -1), wp=wpt.T, wpt=wpt)


def _mixer_weights(l, qa, wqb_full, kva, wkvb_full, qn, kn, conv_full, sqn, skn, sinks, w_out_full):
    wq = jnp.pad(wqb_full, ((0, 0), (0, 0), (0, LANES - MLA_QK)))
    wk = jnp.pad(wkvb_full[:, :, :MLA_NOPE], ((0, 0), (0, 0), (0, LANES - MLA_NOPE)))
    wv = jnp.transpose(wkvb_full[:, :, MLA_NOPE:], (1, 0, 2)).reshape(MLA_KV_LORA, GROUP_WIDTH)
    return dict(
        qa=qa[l].reshape(1, -1), kva=kva[l].reshape(1, -1),
        wq=wq, wk=wk, wv=wv, wqt=jnp.transpose(wq, (0, 2, 1)), wkt=jnp.transpose(wk, (0, 2, 1)), wvt=wv.T,
        qn=_pad_lanes(qn[l]), kn=_pad_lanes(kn[l]),
        conv=jnp.pad(conv_full, ((0, 5), (0, 0))),
        sqn=jnp.tile(sqn[l].reshape(1, -1), (1, 2)), skn=jnp.tile(skn[l].reshape(1, -1), (1, 2)),
        sinks=sinks[l], wo=w_out_full, wot=w_out_full.T)


def _layer_weights(l, norm_g, w_in_full, qa, wqb_full, kva, wkvb_full, qn, kn, conv_full, sqn, skn, sinks,
                   w_out_full):
    return dict(_inproj_weights(l, norm_g, w_in_full.T),
                **_mixer_weights(l, qa, wqb_full, kva, wkvb_full, qn, kn, conv_full, sqn, skn, sinks, w_out_full))


def _layer_fwd(x, lw, rope, late_weights=None, target=None):
    proj, h = _inproj_fwd(x, lw["ng"], lw["wp"])
    if late_weights is not None:
        lw = dict(lw, **late_weights(proj))
    q, k, kt, vt = _mla_prep_fwd(proj, lw, rope)
    o_mla, lse = _mla_attn_fwd(q, k, vt)
    o_swa = _swa_fwd(proj, lw)
    ycat = _mix_fwd(proj, o_mla, o_swa, lw["conv"])
    if target is None:
        out = _mm_nn(ycat, lw["wo"], "outproj_fwd", residual=x)
    else:
        out = _outproj_loss(ycat, lw["wo"], x, target)
    return out, dict(x=x, proj=proj, h=h, q=q, k=k, kt=kt, vt=vt, o_mla=o_mla, lse=lse, o_swa=o_swa, ycat=ycat,
                     lw=lw)


def _layer_bwd(g, sv, lw, rope, on_big_grads=None):
    proj = sv["proj"]
    dycat, d_wo = _outproj_bwd(g, sv["ycat"], lw["wot"])
    dproj, do_mla, do_swa, d_conv = _mix_bwd(dycat, proj, sv["o_mla"], sv["o_swa"], lw["conv"])
    dproj, dkn_acc, dv_acc, d_sqn, d_sinks = _swa_bwd(proj, sv["o_swa"], do_swa, lw, dproj)
    dproj, d_skn = _swa_kv_bwd(proj, dkn_acc, dv_acc, lw, dproj)
    dq, dk, dv = _mla_attn_bwd(sv["q"], sv["k"], sv["kt"], sv["vt"], sv["o_mla"], do_mla, sv["lse"])
    dproj, d_wq, d_wk, d_wv, d_qa, d_kva, d_qn, d_kn = _mla_prep_bwd(proj, dq, dk, dv, lw, rope, dproj)
    grads = dict(
        w_out=d_wo, w_qb=d_wq[:, :, :MLA_QK],
        w_kvb=jnp.concatenate([d_wk[:, :, :MLA_NOPE],
                               jnp.transpose(d_wv.reshape(MLA_KV_LORA, HEADS, MLA_NOPE), (1, 0, 2))], axis=2))
    token = 0.0 if on_big_grads is None else on_big_grads("mixer", grads)
    d_wp = _mm_tn(sv["h"], dproj, "inproj_bwd_dw", WIRE_DTYPE, tn=NP // 2)
    grads["w_in"] = _unpack_dwin(d_wp)
    token = token if on_big_grads is None else token + on_big_grads("w_in", grads)
    dx, d_ng = _inproj_bwd_dx(dproj, lw["wpt"], sv["x"], g, lw["ng"] + token)
    grads.update(
        conv=d_conv[0:3], norm_g=d_ng[0], qa=d_qa[0], kva=d_kva[0], qn=d_qn[0, :MLA_QK], kn=d_kn[0, :MLA_QK],
        sqn=d_sqn[0, :SWA_HEAD_DIM], skn=d_skn[0, :SWA_HEAD_DIM], sinks=d_sinks[:, 0])
    return dx, grads


def _local_step(x, target, lws, rope):
    saved = []
    for l, lw in enumerate(lws):
        x, sv = _layer_fwd(x, lw, rope, target=target if l == len(lws) - 1 else None)
        saved.append(sv)
    g, loss_tile = x
    grads = [None] * len(lws)
    for l in reversed(range(len(lws))):
        g, grads[l] = _layer_bwd(g, saved[l], lws[l], rope)
    return loss_tile, g, grads


def _my_coords():
    return lax.axis_index("x"), lax.axis_index("y"), lax.axis_index("c")


def _peer(me, k):
    x, y, c = me
    return (1 - x if k & 4 else x, 1 - y if k & 2 else y, 1 - c if k & 1 else c)


def _lin(d):
    return 4 * d[0] + 2 * d[1] + d[2]


def _push_copies(ins, lands, send_sems, recv_sems, gather):
    me = _my_coords()
    my = _lin(me)
    out, inc = [], []
    for a in range(len(ins)):
        for k in range(1, N_DEV):
            peer = _peer(me, k)
            sems = dict(send_sem=send_sems.at[a * 7 + k - 1], recv_sem=recv_sems.at[a * 7 + k - 1],
                        device_id=peer, device_id_type=pl.DeviceIdType.MESH)
            src = ins[a] if gather else ins[a].at[_lin(peer)]
            out.append(pltpu.make_async_remote_copy(src_ref=src, dst_ref=lands[a].at[my], **sems))
            inc.append(pltpu.make_async_remote_copy(src_ref=src, dst_ref=lands[a].at[_lin(peer)], **sems))
    return out, inc


def _push_start(arrays, name, gather):
    n = len(arrays)
    land_shapes = [((N_DEV,) + a.shape) if gather else a.shape for a in arrays]

    def body(*refs):
        ins, lands = refs[:n], refs[n:2 * n]
        send_sems, recv_sems = refs[2 * n], refs[2 * n + 1]
        token = refs[-1]
        out, _ = _push_copies(ins, lands, send_sems, recv_sems, gather)
        for cp in out:
            cp.start()
        token[...] = jnp.zeros_like(token)

    hbm = pl.BlockSpec(memory_space=pltpu.HBM)
    sem = pl.BlockSpec(memory_space=pltpu.SEMAPHORE)
    res = pl.pallas_call(
        body, name=name,
        out_shape=(pltpu.SemaphoreType.DMA((7 * n,)), pltpu.SemaphoreType.DMA((7 * n,)),
                   *[pltpu.HBM(a.shape, a.dtype) for a in arrays],
                   *[pltpu.HBM(s, a.dtype) for s, a in zip(land_shapes, arrays)],
                   _sds((8, LANES), F32)),
        in_specs=(hbm,) * (2 * n),
        out_specs=(sem, sem) + (hbm,) * (2 * n) + (pl.BlockSpec(memory_space=pltpu.VMEM),),
        input_output_aliases={i: 2 + i for i in range(2 * n)},
        compiler_params=pltpu.CompilerParams(has_side_effects=pltpu.SideEffectType.DATAFLOW_SIDE_EFFECTING),
    )(*[pltpu.with_memory_space_constraint(a, pltpu.HBM) for a in arrays],
      *[pltpu.with_memory_space_constraint(lax.empty(s, a.dtype), pltpu.HBM) for s, a in zip(land_shapes, arrays)])
    return dict(send=res[0], recv=res[1], src=res[2:2 + n], land=res[2 + n:2 + 2 * n], token=res[-1][0, 0],
                gather=gather)


def _push_wait(handle, after, name):
    n = len(handle["src"])
    gather = handle["gather"]

    def body(*refs):
        ins, lands = refs[:n], refs[n:2 * n]
        send_sems, recv_sems = refs[2 * n], refs[2 * n + 1]
        out, inc = _push_copies(ins, lands, send_sems, recv_sems, gather)
        for cp in out:
            cp.wait_send()
        for cp in inc:
            cp.wait_recv()

    hbm = pl.BlockSpec(memory_space=pltpu.HBM)
    sem = pl.BlockSpec(memory_space=pltpu.SEMAPHORE)
    res = pl.pallas_call(
        body, name=name,
        out_shape=tuple(pltpu.HBM(a.shape, a.dtype) for a in (*handle["src"], *handle["land"])),
        in_specs=(hbm,) * (2 * n) + (sem, sem, pl.BlockSpec(memory_space=pl.ANY)),
        out_specs=(hbm,) * (2 * n),
        input_output_aliases={i: i for i in range(2 * n)},
        compiler_params=pltpu.CompilerParams(has_side_effects=pltpu.SideEffectType.DATAFLOW_SIDE_EFFECTING),
    )(*handle["src"], *handle["land"], handle["send"], handle["recv"], after)
    return res[n:]


def _small_all_reduce(v):
    R = v.shape[0]

    def body(v_ref, o_ref, buf, send_sems, recv_sems):
        me = _my_coords()
        my = _lin(me)
        sends = []
        for k in range(1, N_DEV):
            cp = pltpu.make_async_remote_copy(
                src_ref=v_ref, dst_ref=buf.at[my], send_sem=send_sems.at[k - 1], recv_sem=recv_sems.at[k - 1],
                device_id=_peer(me, k), device_id_type=pl.DeviceIdType.MESH)
            cp.start()
            sends.append(cp)
        buf[my] = v_ref[...]
        for k in range(1, N_DEV):
            pltpu.make_async_remote_copy(
                src_ref=v_ref, dst_ref=buf.at[_lin(_peer(me, k))], send_sem=send_sems.at[k - 1],
                recv_sem=recv_sems.at[k - 1], device_id=_peer(me, k),
                device_id_type=pl.DeviceIdType.MESH).wait_recv()
        for cp in sends:
            cp.wait_send()
        tot = buf[0]
        for d in range(1, N_DEV):
            tot = tot + buf[d]
        o_ref[...] = tot

    vm = pl.BlockSpec(memory_space=pltpu.VMEM)
    return pl.pallas_call(
        body, name="small_all_reduce", in_specs=[vm], out_specs=vm, out_shape=_sds(v.shape, F32),
        scratch_shapes=[pltpu.VMEM((N_DEV, R, LANES), F32), pltpu.SemaphoreType.DMA((7,)),
                        pltpu.SemaphoreType.DMA((7,))],
    )(v)


def _adamw_math(w, g, m, v):
    m = ADAM_B1 * m + (1.0 - ADAM_B1) * g
    v = ADAM_B2 * v + (1.0 - ADAM_B2) * (g * g)
    m_hat = m / (1.0 - ADAM_B1 ** ADAM_STEP)
    v_hat = v / (1.0 - ADAM_B2 ** ADAM_STEP)
    delta = -ADAM_LR * (m_hat / (jnp.sqrt(v_hat) + ADAM_EPS) + ADAM_WD * w)
    return delta, m, v


def _adamw(parts, w, m, v, name, tr):
    P, R, C = parts.shape
    tr = min(tr, R)

    def body(p_ref, w_ref, m_ref, v_ref, g_out, d_out, m_out, v_out):
        g = p_ref[0].astype(F32)
        for d in range(1, P):
            g = g + p_ref[d].astype(F32)
        delta, m_new, v_new = _adamw_math(w_ref[...], g, m_ref[...], v_ref[...])
        g_out[...] = g
        d_out[...] = delta
        m_out[...] = m_new
        v_out[...] = v_new

    tile = pl.BlockSpec((tr, C), lambda i: (i, 0))
    return pl.pallas_call(
        body, name=name, grid=(R // tr,),
        in_specs=[pl.BlockSpec((P, tr, C), lambda i: (0, i, 0)), tile, tile, tile],
        out_specs=[tile] * 4, out_shape=[_sds((R, C), F32)] * 4,
        compiler_params=_cp(("parallel",), 32))(parts, w, m, v)


SMALL = (("norm_g", D_MODEL), ("mla_q_a_norm", MLA_Q_LORA), ("mla_kv_a_norm", MLA_KV_LORA), ("mla_q_norm", MLA_QK),
         ("mla_k_norm", MLA_QK), ("swa_q_norm", SWA_HEAD_DIM), ("swa_k_norm", SWA_HEAD_DIM), ("swa_sinks", HEADS))
SMALL_GRAD_KEY = dict(norm_g="norm_g", mla_q_a_norm="qa", mla_kv_a_norm="kva", mla_q_norm="qn", mla_k_norm="kn",
                      swa_q_norm="sqn", swa_k_norm="skn", swa_sinks="sinks")
SMALL_ROWS = 32
CONV_ROWS = 24


def _pack_small(get):
    parts = []
    for l in range(DEPTH):
        for name, n in SMALL:
            v = get(name, l).reshape(-1)
            parts.append(jnp.pad(v, (0, (-n) % LANES)))
    return jnp.concatenate(parts).reshape(SMALL_ROWS, LANES)


def _unpack_small(packed):
    flat = packed.reshape(-1)
    out = {name: [] for name, _ in SMALL}
    off = 0
    for l in range(DEPTH):
        for name, n in SMALL:
            out[name].append(flat[off:off + n])
            off += n + (-n) % LANES
    return {name: jnp.stack(v) for name, v in out.items()}


def kernel(x, norm_g, w_in, mla_q_a_norm, mla_w_qb, mla_kv_a_norm, mla_w_kvb, mla_q_norm, mla_k_norm, conv_w, swa_q_norm, swa_k_norm, swa_sinks, w_out, loss_target, m_norm_g, m_w_in, m_mla_q_a_norm, m_mla_w_qb, m_mla_kv_a_norm, m_mla_w_kvb, m_mla_q_norm, m_mla_k_norm, m_conv_w, m_swa_q_norm, m_swa_k_norm, m_swa_sinks, m_w_out, v_norm_g, v_w_in, v_mla_q_a_norm, v_mla_w_qb, v_mla_kv_a_norm, v_mla_w_kvb, v_mla_q_norm, v_mla_k_norm, v_conv_w, v_swa_q_norm, v_swa_k_norm, v_swa_sinks, v_w_out):
    T = x.shape[1]
    weights = dict(norm_g=norm_g, w_in=w_in, mla_q_a_norm=mla_q_a_norm, mla_w_qb=mla_w_qb,
                   mla_kv_a_norm=mla_kv_a_norm, mla_w_kvb=mla_w_kvb, mla_q_norm=mla_q_norm, mla_k_norm=mla_k_norm,
                   conv_w=conv_w, swa_q_norm=swa_q_norm, swa_k_norm=swa_k_norm, swa_sinks=swa_sinks, w_out=w_out)
    mom_m = dict(norm_g=m_norm_g, w_in=m_w_in, mla_q_a_norm=m_mla_q_a_norm, mla_w_qb=m_mla_w_qb,
                 mla_kv_a_norm=m_mla_kv_a_norm, mla_w_kvb=m_mla_w_kvb, mla_q_norm=m_mla_q_norm,
                 mla_k_norm=m_mla_k_norm, conv_w=m_conv_w, swa_q_norm=m_swa_q_norm, swa_k_norm=m_swa_k_norm,
                 swa_sinks=m_swa_sinks, w_out=m_w_out)
    mom_v = dict(norm_g=v_norm_g, w_in=v_w_in, mla_q_a_norm=v_mla_q_a_norm, mla_w_qb=v_mla_w_qb,
                 mla_kv_a_norm=v_mla_kv_a_norm, mla_w_kvb=v_mla_w_kvb, mla_q_norm=v_mla_q_norm,
                 mla_k_norm=v_mla_k_norm, conv_w=v_conv_w, swa_q_norm=v_swa_q_norm, swa_k_norm=v_swa_k_norm,
                 swa_sinks=v_swa_sinks, w_out=v_w_out)

    my = _lin(_my_coords())

    def shards(l):
        return [w_in[l].astype(MXU_DTYPE).T, mla_w_qb[l].astype(MXU_DTYPE), mla_w_kvb[l].astype(MXU_DTYPE),
                w_out[l].astype(MXU_DTYPE), conv_w[l]]

    def inproj_weights(l, g_win_t):
        return _inproj_weights(l, norm_g, g_win_t.reshape(IN_COLS, D_MODEL))

    def mixer_weights(l, gathered):
        g_wqb, g_wkvb, g_wout, g_conv = gathered
        return _mixer_weights(
            l, mla_q_a_norm, g_wqb, mla_kv_a_norm, g_wkvb, mla_q_norm, mla_k_norm,
            jnp.transpose(g_conv, (1, 0, 2)).reshape(3, GROUP_WIDTH), swa_q_norm, swa_k_norm, swa_sinks,
            g_wout.reshape(D_MIX, D_MODEL))

    slot_of = dict(
        w_in=lambda g: jnp.transpose(g["w_in"].reshape(D_MODEL, N_DEV, IN_COLS // N_DEV), (1, 0, 2)),
        w_out=lambda g: g["w_out"].reshape(N_DEV, D_MIX // N_DEV, D_MODEL),
        w_qb=lambda g: g["w_qb"], w_kvb=lambda g: g["w_kvb"])

    def own_slot(landed, mine):
        return [lax.dynamic_update_index_in_dim(a, m, my, 0) for a, m in zip(landed, mine)]

    def landed(handle, after, name, mine):
        return own_slot(_push_wait(handle, after, name), mine)

    gather_in0 = _push_start(shards(0)[:1], "weight_gather_in0_start", gather=True)
    gather0 = _push_start(shards(0)[1:], "weight_gather0_start", gather=True)
    gather1 = _push_start(shards(1), "weight_gather1_start", gather=True)
    rope = _rope_tables(T, gather_in0["token"] + gather0["token"] + gather1["token"])
    lw0 = inproj_weights(0, landed(gather_in0, rope[0], "weight_gather_in0_wait", shards(0)[:1])[0])
    x1, sv0 = _layer_fwd(
        x[0], lw0, rope,
        late_weights=lambda proj: mixer_weights(0, landed(gather0, proj, "weight_gather0_wait", shards(0)[1:])))
    g1_all = landed(gather1, x1, "weight_gather1_wait", shards(1))
    (g2, loss_tile), sv1 = _layer_fwd(x1, dict(inproj_weights(1, g1_all[0]), **mixer_weights(1, g1_all[1:])), rope,
                                      target=loss_target[0])

    parts = {(1, "w_in"): ("w_in", "w_out", "w_qb", "w_kvb"), (0, "mixer"): ("w_out", "w_qb", "w_kvb"),
             (0, "w_in"): ("w_in",)}
    started = []

    def start_exchange(l, part, g):
        if (l, part) not in parts:
            return 0.0
        sl = [slot_of[n](g) for n in parts[(l, part)]]
        handle = _push_start(sl, "grad_exchange%d_%s_start" % (l, part), gather=False)
        started.append((l, part, sl, handle))
        return handle["token"]

    g1, grads1 = _layer_bwd(g2, sv1, sv1["lw"], rope, on_big_grads=functools.partial(start_exchange, 1))
    lw0b = dict(sv0["lw"], conv=sv0["lw"]["conv"] + started[0][3]["token"])
    grad_x, grads0 = _layer_bwd(g1, sv0, lw0b, rope, on_big_grads=functools.partial(start_exchange, 0))
    recv = {}
    for l, part, sl, handle in started:
        got = landed(handle, grad_x, "grad_exchange%d_%s_wait" % (l, part), [s[my] for s in sl])
        recv.update({(l, n): a for n, a in zip(parts[(l, part)], got)})
    grads = [grads0, grads1]
    r_win, r_wout, r_wqb, r_wkvb = [jnp.stack([recv[(0, n)], recv[(1, n)]], axis=1)
                                    for n in ("w_in", "w_out", "w_qb", "w_kvb")]

    small = jnp.concatenate([
        _pack_small(lambda name, l: grads[l][SMALL_GRAD_KEY[name]]),
        jnp.stack([g["conv"] for g in grads]).reshape(CONV_ROWS, LANES),
        loss_tile], axis=0)
    small = _small_all_reduce(small)
    loss = small[SMALL_ROWS + CONV_ROWS, 0]
    my = _lin(_my_coords())
    conv_g = lax.dynamic_slice_in_dim(small[SMALL_ROWS:SMALL_ROWS + CONV_ROWS].reshape(DEPTH, 3, GROUP_WIDTH),
                                      my * 64, 64, axis=2)

    out = {}

    def big(name, recv, rows, cols, tr):
        res = _adamw(recv.reshape(N_DEV, rows, cols), weights[name].reshape(rows, cols),
                     mom_m[name].reshape(rows, cols), mom_v[name].reshape(rows, cols), "adamw_" + name, tr)
        out[name] = [r.reshape(weights[name].shape) for r in res]

    big("w_in", r_win, DEPTH * D_MODEL, IN_COLS // N_DEV, 256)
    big("w_out", r_wout, DEPTH * D_MIX // N_DEV, D_MODEL, 192)
    big("mla_w_qb", r_wqb, DEPTH * MLA_Q_LORA, MLA_QK, 512)
    big("mla_w_kvb", r_wkvb, DEPTH * MLA_KV_LORA, 128, 256)

    pad_conv = lambda a: jnp.pad(a.reshape(-1), (0, 8 * LANES - 6 * 64)).reshape(8, LANES)
    cat = lambda src: jnp.concatenate([_pack_small(lambda name, l: src[name][l]), pad_conv(src["conv_w"])], axis=0)
    g_small = jnp.concatenate([small[:SMALL_ROWS], pad_conv(conv_g)], axis=0)
    res = _adamw(g_small[None], cat(weights), cat(mom_m), cat(mom_v), "adamw_small", SMALL_ROWS + 8)
    smalls = [_unpack_small(r[:SMALL_ROWS]) for r in res]
    for name, _ in SMALL:
        out[name] = [s[name] for s in smalls]
    out["conv_w"] = [r[SMALL_ROWS:].reshape(-1)[:6 * 64].reshape(DEPTH, 3, 64) for r in res]

    order = ["norm_g", "w_in", "mla_q_a_norm", "mla_w_qb", "mla_kv_a_norm", "mla_w_kvb", "mla_q_norm", "mla_k_norm",
             "conv_w", "swa_q_norm", "swa_k_norm", "swa_sinks", "w_out"]
    result = [loss, grad_x[None]]
    for idx in range(4):
        result += [out[name][idx] for name in order]
    return tuple(result)
```

```python
import functools

import jax
import jax.numpy as jnp
import numpy as np
from jax import lax
from jax.experimental import pallas as pl
from jax.experimental.pallas import tpu as pltpu

F32 = jnp.float32
MXU_DTYPE = jnp.bfloat16
WIRE_DTYPE = jnp.bfloat16

N_DEV = 8
DEPTH = 2
D_MODEL = 1024
GROUP_WIDTH = 512
D_MIX = 3 * GROUP_WIDTH
BLOCK = 128
RMS_EPS = 1e-6
NEG_INF = -1e30
HEADS = 8
MLA_QK = 96
MLA_NOPE = 64
MLA_ROPE = 32
MLA_Q_LORA = 256
MLA_KV_LORA = 128
ROPE_THETA = 10000.0
SWA_HEAD_DIM = 64
LANES = 128
IN_COLS = 4256

ADAM_LR = 0.001
ADAM_B1 = 0.9
ADAM_B2 = 0.999
ADAM_EPS = 1e-08
ADAM_WD = 0.01
ADAM_STEP = 10

NP = 4352
CB_GMLA, CB_CH, CB_CB, CB_CC, CB_GCONV, CB_GSWA, CB_SQ = 0, 1, 2, 3, 4, 5, 7
CB_QLAT = 12
CB_KVLAT, CB_KROPE = 26, 27
CB_SK, CB_SV = 32, 33
DPB_MIX, DPB_MLA, DPB_SQ, DPB_SKV = 0, 6, 7, 16

TM_PROJ = 512
TM_ROW = 256
TK = 256
TQ = 2 * TK
MLA_SCALE = MLA_QK ** -0.5
MLA_ONES_ROW = (64, 0)
LOG2E = 1.4426950408889634
LN2 = 0.6931471805599453
TM_SWA = 512
VMEM_MB = 2 ** 20


def _cp(sem, vmem_mb):
    return pltpu.CompilerParams(dimension_semantics=sem, vmem_limit_bytes=vmem_mb * VMEM_MB)


def _sds(shape, dtype):
    return jax.ShapeDtypeStruct(shape, dtype)


def _dot(a, b):
    return jnp.dot(a, b, preferred_element_type=F32)


def _dot_nt(a, b):
    return lax.dot_general(a, b, (((1,), (1,)), ((), ())), preferred_element_type=F32)


def _dot_tn(a, b):
    return lax.dot_general(a, b, (((0,), (0,)), ((), ())), preferred_element_type=F32)


def _rms(x, n):
    r = lax.rsqrt(jnp.sum(x * x, axis=-1, keepdims=True) * (1.0 / n) + RMS_EPS)
    return x * r, r


def _rms_bwd(dy, xhat, r, w, n):
    g = dy * w
    return r * (g - xhat * (jnp.sum(g * xhat, axis=-1, keepdims=True) * (1.0 / n)))


def _rms_halves(x, half1):
    x2 = x * x
    s0 = jnp.sum(jnp.where(half1, 0.0, x2), axis=-1, keepdims=True)
    s1 = jnp.sum(jnp.where(half1, x2, 0.0), axis=-1, keepdims=True)
    r = jnp.where(half1, lax.rsqrt(s1 * (1.0 / 64) + RMS_EPS), lax.rsqrt(s0 * (1.0 / 64) + RMS_EPS))
    return x * r, r


def _rms_halves_bwd(dy, xhat, r, w, half1):
    g = dy * w
    t = g * xhat
    m0 = jnp.sum(jnp.where(half1, 0.0, t), axis=-1, keepdims=True) * (1.0 / 64)
    m1 = jnp.sum(jnp.where(half1, t, 0.0), axis=-1, keepdims=True) * (1.0 / 64)
    return r * (g - xhat * jnp.where(half1, m1, m0))


def _sigmoid(x):
    return 1.0 / (1.0 + jnp.exp(-x))


def _rope(x, c, s1, s2):
    ax = x.ndim - 1
    return x * c + pltpu.roll(x, 112, ax) * s1 + pltpu.roll(x, 16, ax) * s2


def _rope_bwd(dy, c, s1, s2):
    ax = dy.ndim - 1
    return dy * c + pltpu.roll(dy * s1, 16, ax) + pltpu.roll(dy * s2, 112, ax)


def _fold_rows8(x):
    return jnp.sum(x.reshape(x.shape[0] // 8, 8, x.shape[1]), axis=0)


def _row0(v, rows=8):
    row = lax.broadcasted_iota(jnp.int32, (rows, v.shape[1]), 0)
    return jnp.where(row == 0, jnp.broadcast_to(v, (rows, v.shape[1])), 0.0)


def _mm_nn(a, b, name, out_dtype=F32, residual=None, tm=TM_PROJ):
    M, K = a.shape
    N = b.shape[1]
    tm = min(tm, M)

    def body(*refs):
        if residual is None:
            a_ref, b_ref, o_ref = refs
            acc = _dot(a_ref[...].astype(MXU_DTYPE), b_ref[...])
        else:
            a_ref, b_ref, r_ref, o_ref = refs
            acc = _dot(a_ref[...].astype(MXU_DTYPE), b_ref[...]) + r_ref[...]
        o_ref[...] = acc.astype(out_dtype)

    in_specs = [pl.BlockSpec((tm, K), lambda i: (i, 0)), pl.BlockSpec((K, N), lambda i: (0, 0))]
    args = [a, b]
    if residual is not None:
        in_specs.append(pl.BlockSpec((tm, N), lambda i: (i, 0)))
        args.append(residual)
    return pl.pallas_call(
        body, name=name, grid=(M // tm,), in_specs=in_specs,
        out_specs=pl.BlockSpec((tm, N), lambda i: (i, 0)), out_shape=_sds((M, N), out_dtype),
        compiler_params=_cp(("parallel",), 48))(*args)


def _mm_tn(a, b, name, out_dtype, tn, tk=512):
    T, M = a.shape
    N = b.shape[1]
    tk = min(tk, T)
    nk = T // tk

    def body(a_ref, b_ref, o_ref, acc_ref):
        k = pl.program_id(1)

        @pl.when(k == 0)
        def _():
            acc_ref[...] = jnp.zeros_like(acc_ref)

        acc_ref[...] += _dot_tn(a_ref[...].astype(MXU_DTYPE), b_ref[...].astype(MXU_DTYPE))

        @pl.when(k == nk - 1)
        def _():
            o_ref[...] = acc_ref[...].astype(out_dtype)

    return pl.pallas_call(
        body, name=name, grid=(N // tn, nk),
        in_specs=[pl.BlockSpec((tk, M), lambda n, k: (k, 0)), pl.BlockSpec((tk, tn), lambda n, k: (k, n))],
        out_specs=pl.BlockSpec((M, tn), lambda n, k: (0, n)), out_shape=_sds((M, N), out_dtype),
        scratch_shapes=[pltpu.VMEM((M, tn), F32)],
        compiler_params=_cp(("parallel", "arbitrary"), 48))(a, b)


def _inproj_fwd(x, ng, wp):
    T, D = x.shape
    tm = min(TM_PROJ, T)

    def body(x_ref, g_ref, w_ref, proj_ref, h_ref):
        xhat, _ = _rms(x_ref[...], D)
        h = (xhat * g_ref[...]).astype(MXU_DTYPE)
        h_ref[...] = h
        proj_ref[...] = _dot(h, w_ref[...])

    return pl.pallas_call(
        body, name="inproj_fwd", grid=(T // tm,),
        in_specs=[pl.BlockSpec((tm, D), lambda i: (i, 0)), pl.BlockSpec((1, D), lambda i: (0, 0)),
                  pl.BlockSpec((D, NP), lambda i: (0, 0))],
        out_specs=[pl.BlockSpec((tm, NP), lambda i: (i, 0)), pl.BlockSpec((tm, D), lambda i: (i, 0))],
        out_shape=[_sds((T, NP), F32), _sds((T, D), MXU_DTYPE)],
        compiler_params=_cp(("parallel",), 48))(x, ng, wp)


def _mla_prep_fwd(proj, lw, rope):
    T = proj.shape[0]
    tk = min(TK, T // 2)
    nsub = 2
    tm = nsub * tk

    def body(ql_ref, kvl_ref, kr_ref, qa_ref, kva_ref, wq_ref, wk_ref, wv_ref, qn_ref, kn_ref,
             c_ref, s1_ref, s2_ref, q_out, k_out, kt_out, vt_out):
        c, s1, s2 = c_ref[...], s1_ref[...], s2_ref[...]
        qhat, _ = _rms(ql_ref[...], MLA_Q_LORA)
        qn = (qhat * qa_ref[...]).astype(MXU_DTYPE)
        khat, _ = _rms(kvl_ref[...], MLA_KV_LORA)
        kvn = (khat * kva_ref[...]).astype(MXU_DTYPE)
        kr = kr_ref[...]
        half1 = lax.broadcasted_iota(jnp.int32, (tm, LANES), 1) >= 64
        ones_row = lax.broadcasted_iota(jnp.int32, (LANES, 1), 0)
        q3, _ = _rms(jnp.stack([_dot(qn, wq_ref[h]) for h in range(HEADS)]), MLA_QK)
        q_out[...] = (_rope(q3 * qn_ref[...], c, s1, s2) * (MLA_SCALE * LOG2E)).astype(MXU_DTYPE)
        k3, _ = _rms(jnp.stack([_dot(kvn, wk_ref[h]) for h in range(HEADS)]) + kr, MLA_QK)
        k3 = _rope(k3 * kn_ref[...], c, s1, s2)
        k_out[...] = k3.astype(MXU_DTYPE)
        for h in range(HEADS):
            for t in range(nsub):
                kt_out[h, t] = k3[h, tk * t:tk * (t + 1)].T.astype(MXU_DTYPE)
        v = _dot(kvn, wv_ref[...])
        for h in range(HEADS):
            vp = v[:, LANES * (h // 2):LANES * (h // 2 + 1)]
            own = half1 if h % 2 else jnp.logical_not(half1)
            vp = jnp.where(own, vp, 0.0)
            for t in range(nsub):
                vpt = vp[tk * t:tk * (t + 1)].T
                vt_out[h, t] = jnp.where(ones_row == MLA_ONES_ROW[h % 2], 1.0, vpt).astype(MXU_DTYPE)

    full = lambda shape: pl.BlockSpec(shape, lambda i: (0,) * len(shape))
    hd = pl.BlockSpec((HEADS, tm, LANES), lambda i: (0, i, 0))
    hdt = pl.BlockSpec((HEADS, nsub, LANES, tk), lambda i: (0, i, 0, 0))
    nat = _sds((HEADS, T, LANES), MXU_DTYPE)
    tr = _sds((HEADS, T // tk, LANES, tk), MXU_DTYPE)
    return pl.pallas_call(
        body, name="mla_prep_fwd", grid=(T // tm,),
        in_specs=[pl.BlockSpec((tm, 256), lambda i: (i, CB_QLAT)), pl.BlockSpec((tm, LANES), lambda i: (i, CB_KVLAT)),
                  pl.BlockSpec((tm, LANES), lambda i: (i, CB_KROPE)),
                  full((1, 256)), full((1, LANES)), full((HEADS, 256, LANES)), full((HEADS, LANES, LANES)),
                  full((LANES, 512)), full((1, LANES)), full((1, LANES)),
                  pl.BlockSpec((tm, LANES), lambda i: (i, 0)), pl.BlockSpec((tm, LANES), lambda i: (i, 0)),
                  pl.BlockSpec((tm, LANES), lambda i: (i, 0))],
        out_specs=[hd, hd, hdt, hdt],
        out_shape=[nat, nat, tr, tr],
        compiler_params=_cp(("parallel",), 32))(
            proj, proj, proj, lw["qa"], lw["kva"], lw["wq"], lw["wk"], lw["wv"], lw["qn"], lw["kn"],
            rope[0], rope[1], rope[2])


def _mla_attn_fwd(q, k, vt):
    T = q.shape[1]
    tk = min(TK, T // 2)
    tq = 2 * tk

    def body(q_ref, k_ref, vt_ref, o_ref, lse_ref, acc_s, m_s, s_a, s_b):
        i = pl.program_id(1)
        key = lax.broadcasted_iota(jnp.int32, (tk, tq), 0)
        qry = lax.broadcasted_iota(jnp.int32, (tk, tq), 1)
        qs = [q_ref[0], q_ref[1]]
        acc_s[...] = jnp.zeros_like(acc_s)
        m_s[...] = jnp.full(m_s.shape, NEG_INF, F32)

        def scores(kj, buf):
            rows = pl.ds(pl.multiple_of(kj * tk, tk), tk)
            for r in range(2):
                buf[r] = _dot_nt(k_ref[r, rows, :], qs[r])

        def consume(kj, buf, diag):
            for r in range(2):
                s = buf[r]
                if diag is not None:
                    s = jnp.where(key + diag * tk <= qry, s, NEG_INF)
                m_old = m_s[r]
                m_new = jnp.maximum(m_old, jnp.max(s, axis=0, keepdims=True))
                alpha = jnp.exp2(m_old - m_new)
                p = jnp.exp2(s - m_new)
                m_s[r] = m_new
                acc_s[r] = alpha * acc_s[r] + _dot(vt_ref[r, kj], p.astype(MXU_DTYPE))

        scores(0, s_a)

        def pair(kj):
            scores(kj + 1, s_b)
            consume(kj, s_a, None)
            scores(kj + 2, s_a)
            consume(kj + 1, s_b, None)

        def octet(ko, carry):
            for t in range(4):
                pair(8 * ko + 2 * t)
            return carry

        lax.fori_loop(0, i // 4, octet, 0)

        @pl.when(i % 4 >= 2)
        def _():
            pair(8 * (i // 4))
            pair(8 * (i // 4) + 2)

        @pl.when(i % 2 == 1)
        def _():
            pair(2 * i - 2)

        scores(2 * i + 1, s_b)
        consume(2 * i, s_a, 0)
        consume(2 * i + 1, s_b, 1)
        l = [acc_s[r, pl.ds(MLA_ONES_ROW[r], 1), :] for r in range(2)]
        head0 = lax.broadcasted_iota(jnp.int32, (LANES, 1), 0) < 64
        o_ref[...] = jnp.where(head0, acc_s[0] / l[0], acc_s[1] / l[1]).T
        for r in range(2):
            lse_ref[r] = m_s[r] + jnp.log2(l[r])

    return pl.pallas_call(
        body, name="mla_attn_fwd", grid=(HEADS // 2, T // tq),
        in_specs=[pl.BlockSpec((2, tq, LANES), lambda j, i: (j, i, 0)),
                  pl.BlockSpec((2, T, LANES), lambda j, i: (j, 0, 0)),
                  pl.BlockSpec((2, T // tk, LANES, tk), lambda j, i: (j, 0, 0, 0))],
        out_specs=[pl.BlockSpec((tq, LANES), lambda j, i: (i, j)),
                   pl.BlockSpec((2, 1, tq), lambda j, i: (j, 0, i))],
        out_shape=[_sds((T, GROUP_WIDTH), F32), _sds((HEADS, 1, T), F32)],
        scratch_shapes=[pltpu.VMEM((2, LANES, tq), F32), pltpu.VMEM((2, 1, tq), F32),
                        pltpu.VMEM((2, tk, tq), F32), pltpu.VMEM((2, tk, tq), F32)],
        compiler_params=_cp(("parallel", "arbitrary"), 40))(q, k, vt)


def _swa_kv_variants(x, half1):
    xs = pltpu.roll(x, 64, 1)
    out = {}
    for g in range(2):
        for r in range(2):
            own = half1 if r else jnp.logical_not(half1)
            out[(g, r)] = jnp.where(own, x if g == r else xs, 0.0).astype(MXU_DTYPE)
    return out


def _swa_alibi():
    ki = np.arange(2 * BLOCK)[:, None]
    qi = np.arange(BLOCK)[None, :]
    dist = BLOCK + qi - ki
    slopes = 2.0 ** -(np.arange(HEADS) + 1.0)
    tab = np.where(((dist >= 0) & (dist < BLOCK))[None], slopes[:, None, None] * dist[None], 1e30)
    return jnp.asarray(tab, F32)


def _swa_kv_variants_t(xt, rows1):
    xs = pltpu.roll(xt, 64, 0)
    out = {}
    for g in range(2):
        for r in range(2):
            own = rows1 if r else jnp.logical_not(rows1)
            out[(g, r)] = jnp.where(own, xt if g == r else xs, 0.0).astype(MXU_DTYPE)
    return out


def _swa_probs(i, nb, q_ref, k_ref, v_ref, pk_ref, pv_ref, qw_ref, kw_ref, alibi_ref, sink_ref):
    scale = SWA_HEAD_DIM ** -0.5
    half1 = lax.broadcasted_iota(jnp.int32, (1, LANES), 1) >= 64
    k_all = jnp.concatenate([pk_ref[...], k_ref[...]], axis=0)
    v_all = jnp.concatenate([pv_ref[...], v_ref[...]], axis=0)
    khat, _ = _rms_halves(k_all, half1)
    kn = khat * kw_ref[...]
    kp = _swa_kv_variants(kn, half1)
    qhat, qr, qn, qt = [], [], [], []
    for j in range(4):
        xh, r = _rms_halves(q_ref[:, LANES * j:LANES * (j + 1)], half1)
        qf = xh * qw_ref[...]
        qhat.append(xh)
        qr.append(r)
        qn.append(qf.astype(MXU_DTYPE))
        qt.append(qf.T.astype(MXU_DTYPE))
    key = lax.broadcasted_iota(jnp.int32, (2 * BLOCK, BLOCK), 0)
    first = jnp.where((i == 0) & (key < BLOCK), NEG_INF, 0.0)
    s = jnp.stack([_dot(kp[(h // 4, h % 2)][BLOCK * b:BLOCK * (b + 2)], qt[h // 2][:, BLOCK * b:BLOCK * (b + 1)])
                   for b in range(nb) for h in range(HEADS)]) * scale - alibi_ref[...]
    s = jnp.concatenate([s[:HEADS] + first, s[HEADS:]], axis=0) if nb > 1 else s + first
    sink = jnp.stack([jnp.full((1, 1), sink_ref[h], F32) for _ in range(nb) for h in range(HEADS)])
    m = jnp.maximum(jnp.max(s, axis=1, keepdims=True), sink)
    e = jnp.exp(s - m)
    es = jnp.exp(sink - m)
    inv = 1.0 / (jnp.sum(e, axis=1, keepdims=True) + es)
    return e * inv, es * inv, dict(half1=half1, kn=kn, kp=kp, v_all=v_all, qhat=qhat, qr=qr, qn=qn)


def _swa_fwd(proj, lw):
    T = proj.shape[0]
    tm = min(TM_SWA, T)
    nb = tm // BLOCK

    def body(q_ref, k_ref, v_ref, pk_ref, pv_ref, qw_ref, kw_ref, alibi_ref, sink_ref, o_ref):
        p, _, c = _swa_probs(pl.program_id(0), nb, q_ref, k_ref, v_ref, pk_ref, pv_ref, qw_ref, kw_ref, alibi_ref,
                             sink_ref)
        p = p.astype(MXU_DTYPE)
        rows1 = lax.broadcasted_iota(jnp.int32, (LANES, 1), 0) >= 64
        vpt = _swa_kv_variants_t(c["v_all"].T, rows1)
        for j in range(4):
            g = j // 2
            o_t = [_dot(vpt[(g, 0)][:, BLOCK * b:BLOCK * (b + 2)], p[HEADS * b + 2 * j])
                   + _dot(vpt[(g, 1)][:, BLOCK * b:BLOCK * (b + 2)], p[HEADS * b + 2 * j + 1]) for b in range(nb)]
            o_t = jnp.concatenate(o_t, axis=1) if nb > 1 else o_t[0]
            o_ref[:, LANES * j:LANES * (j + 1)] = o_t.T

    prev = lambda cb: pl.BlockSpec((BLOCK, LANES), lambda i: (jnp.maximum(i * nb - 1, 0), cb))
    return pl.pallas_call(
        body, name="swa_fwd", grid=(T // tm,),
        in_specs=[pl.BlockSpec((tm, 512), lambda i: (i, CB_SQ)), pl.BlockSpec((tm, LANES), lambda i: (i, CB_SK)),
                  pl.BlockSpec((tm, LANES), lambda i: (i, CB_SV)), prev(CB_SK), prev(CB_SV),
                  pl.BlockSpec((1, LANES), lambda i: (0, 0)), pl.BlockSpec((1, LANES), lambda i: (0, 0)),
                  pl.BlockSpec((nb * HEADS, 2 * BLOCK, BLOCK), lambda i: (0, 0, 0)),
                  pl.BlockSpec(memory_space=pltpu.SMEM)],
        out_specs=pl.BlockSpec((tm, 512), lambda i: (i, 0)),
        out_shape=_sds((T, GROUP_WIDTH), F32),
        compiler_params=_cp(("parallel",), 40))(
            proj, proj, proj, proj, proj, lw["sqn"], lw["skn"], jnp.tile(_swa_alibi(), (nb, 1, 1)), lw["sinks"])


def _shift_down(u, prev, n, row):
    tm = u.shape[0]
    out = pltpu.roll(u, n, 0)
    row8 = lax.broadcasted_iota(jnp.int32, prev.shape, 0)
    for t in range(n):
        src = jnp.sum(jnp.where(row8 == 8 - n + t, prev, 0.0), axis=0, keepdims=True)
        out = jnp.where(row == t, src, out)
    return out


def _shift_up(u, nxt, n, row):
    tm = u.shape[0]
    out = pltpu.roll(u, tm - n, 0)
    row8 = lax.broadcasted_iota(jnp.int32, nxt.shape, 0)
    for t in range(n):
        src = jnp.sum(jnp.where(row8 == t, nxt, 0.0), axis=0, keepdims=True)
        out = jnp.where(row == tm - n + t, src, out)
    return out


def _mix_fwd(proj, o_mla, o_swa, conv_w):
    T = proj.shape[0]
    tm = min(TM_ROW, T)

    def body(gm_ref, ch_ref, cb_ref, cc_ref, gc_ref, gs_ref, pch_ref, pcc_ref, om_ref, os_ref, w_ref, y_ref):
        i = pl.program_id(0)
        row = lax.broadcasted_iota(jnp.int32, (tm, GROUP_WIDTH), 0)
        u = cc_ref[...] * ch_ref[...]
        u_prev = jnp.where(i > 0, pcc_ref[...] * pch_ref[...], 0.0)
        z = (w_ref[0:1, :] * _shift_down(u, u_prev, 2, row) + w_ref[1:2, :] * _shift_down(u, u_prev, 1, row)
             + w_ref[2:3, :] * u)
        gm, gc, gs = gm_ref[...], gc_ref[...], gs_ref[...]
        y_ref[:, 0:512] = (om_ref[...] * (gm * _sigmoid(gm))).astype(MXU_DTYPE)
        y_ref[:, 512:1024] = (cb_ref[...] * z * (gc * _sigmoid(gc))).astype(MXU_DTYPE)
        y_ref[:, 1024:1536] = (os_ref[...] * (gs * _sigmoid(gs))).astype(MXU_DTYPE)

    blk = lambda cb: pl.BlockSpec((tm, 512), lambda i: (i, cb))
    prev = lambda cb: pl.BlockSpec((8, 512), lambda i: (jnp.maximum(i * (tm // 8) - 1, 0), cb))
    tile = pl.BlockSpec((tm, 512), lambda i: (i, 0))
    return pl.pallas_call(
        body, name="mix_fwd", grid=(T // tm,),
        in_specs=[blk(CB_GMLA), blk(CB_CH), blk(CB_CB), blk(CB_CC), blk(CB_GCONV), blk(CB_GSWA),
                  prev(CB_CH), prev(CB_CC), tile, tile, pl.BlockSpec((8, 512), lambda i: (0, 0))],
        out_specs=pl.BlockSpec((tm, D_MIX), lambda i: (i, 0)),
        out_shape=_sds((T, D_MIX), MXU_DTYPE),
        compiler_params=_cp(("parallel",), 32))(
            proj, proj, proj, proj, proj, proj, proj, proj, o_mla, o_swa, conv_w)


def _outproj_loss(ycat, wo, x, target):
    T, D = x.shape
    K = ycat.shape[1]
    tm = min(TM_PROJ, T)
    nt = T // tm

    def body(y_ref, w_ref, x_ref, t_ref, g_ref, loss_ref, acc_ref):
        i = pl.program_id(0)

        @pl.when(i == 0)
        def _():
            acc_ref[...] = jnp.zeros_like(acc_ref)

        err = _dot(y_ref[...], w_ref[...]) + x_ref[...] - t_ref[...]
        g_ref[...] = err * (1.0 / D)
        acc_ref[...] += _fold_rows8(err * err)

        @pl.when(i == nt - 1)
        def _():
            tot = jnp.sum(jnp.sum(acc_ref[...], axis=1, keepdims=True), axis=0, keepdims=True)
            loss_ref[...] = jnp.broadcast_to(tot * (0.5 / D), (8, LANES))

    tile = pl.BlockSpec((tm, D), lambda i: (i, 0))
    return pl.pallas_call(
        body, name="outproj_loss", grid=(nt,),
        in_specs=[pl.BlockSpec((tm, K), lambda i: (i, 0)), pl.BlockSpec((K, D), lambda i: (0, 0)), tile, tile],
        out_specs=[tile, pl.BlockSpec((8, LANES), lambda i: (0, 0))],
        out_shape=[_sds((T, D), F32), _sds((8, LANES), F32)],
        scratch_shapes=[pltpu.VMEM((8, D), F32)],
        compiler_params=_cp(("arbitrary",), 48))(ycat, wo, x, target)


def _outproj_bwd(g, ycat, wot):
    T, D = g.shape
    K = ycat.shape[1]
    tm = min(512, T)
    nt = T // tm

    def body(g_ref, y_ref, wt_ref, dy_ref, dw_ref, acc_ref):
        i = pl.program_id(0)

        @pl.when(i == 0)
        def _():
            acc_ref[...] = jnp.zeros_like(acc_ref)

        gb = g_ref[...].astype(MXU_DTYPE)
        dy_ref[...] = _dot(gb, wt_ref[...])
        acc_ref[...] += _dot_tn(y_ref[...], gb)

        @pl.when(i == nt - 1)
        def _():
            dw_ref[...] = acc_ref[...].astype(WIRE_DTYPE)

    return pl.pallas_call(
        body, name="outproj_bwd", grid=(nt,),
        in_specs=[pl.BlockSpec((tm, D), lambda i: (i, 0)), pl.BlockSpec((tm, K), lambda i: (i, 0)),
                  pl.BlockSpec((D, K), lambda i: (0, 0))],
        out_specs=[pl.BlockSpec((tm, K), lambda i: (i, 0)), pl.BlockSpec((K, D), lambda i: (0, 0))],
        out_shape=[_sds((T, K), F32), _sds((K, D), WIRE_DTYPE)],
        scratch_shapes=[pltpu.VMEM((K, D), F32)],
        compiler_params=_cp(("arbitrary",), 48))(g, ycat, wot)


def _mix_bwd(dycat, proj, o_mla, o_swa, conv_w):
    T = proj.shape[0]
    tm = min(TM_ROW, T)
    nt = T // tm

    def body(dym_ref, dyc_ref, dys_ref, gm_ref, ch_ref, cb_ref, cc_ref, gc_ref, gs_ref, pch_ref, pcc_ref,
             ndy_ref, ncb_ref, ngc_ref, om_ref, os_ref, w_ref,
             d1_ref, dom_ref, dos_ref, dw_ref):
        i = pl.program_id(0)

        @pl.when(i == 0)
        def _():
            dw_ref[...] = jnp.zeros_like(dw_ref)

        row = lax.broadcasted_iota(jnp.int32, (tm, GROUP_WIDTH), 0)

        def gate(g):
            sg = _sigmoid(g)
            return g * sg, sg * (1.0 + g * (1.0 - sg))

        gm = gm_ref[...]
        silu, dsilu = gate(gm)
        dym = dym_ref[...]
        dom_ref[...] = dym * silu
        d1_ref[:, 0:512] = (dym * om_ref[...] * dsilu).astype(MXU_DTYPE)

        gs = gs_ref[...]
        silu, dsilu = gate(gs)
        dys = dys_ref[...]
        dos_ref[...] = dys * silu
        d1_ref[:, 2560:3072] = (dys * os_ref[...] * dsilu).astype(MXU_DTYPE)

        ch, cb, cc, gc, dyc = ch_ref[...], cb_ref[...], cc_ref[...], gc_ref[...], dyc_ref[...]
        w0, w1, w2 = w_ref[0:1, :], w_ref[1:2, :], w_ref[2:3, :]
        u = cc * ch
        u_prev = jnp.where(i > 0, pcc_ref[...] * pch_ref[...], 0.0)
        u1 = _shift_down(u, u_prev, 1, row)
        u2 = _shift_down(u, u_prev, 2, row)
        z = w0 * u2 + w1 * u1 + w2 * u
        silu, dsilu = gate(gc)
        dz = dyc * cb * silu
        ngc = ngc_ref[...]
        dz_next = jnp.where(i < nt - 1, ndy_ref[...] * ncb_ref[...] * (ngc * _sigmoid(ngc)), 0.0)
        du = w2 * dz + w1 * _shift_up(dz, dz_next, 1, row) + w0 * _shift_up(dz, dz_next, 2, row)
        d1_ref[:, 512:1024] = (du * cc).astype(MXU_DTYPE)
        d1_ref[:, 1024:1536] = (dyc * z * silu).astype(MXU_DTYPE)
        d1_ref[:, 1536:2048] = (du * ch).astype(MXU_DTYPE)
        d1_ref[:, 2048:2560] = (dyc * cb * z * dsilu).astype(MXU_DTYPE)
        row8 = lax.broadcasted_iota(jnp.int32, (8, GROUP_WIDTH), 0)
        dw = jnp.zeros((8, GROUP_WIDTH), F32)
        for t, shifted in enumerate((u2, u1, u)):
            dw = dw + jnp.where(row8 == t, jnp.sum(dz * shifted, axis=0, keepdims=True), 0.0)
        dw_ref[...] += dw

    blk = lambda cb: pl.BlockSpec((tm, 512), lambda i: (i, cb))
    prev = lambda cb: pl.BlockSpec((8, 512), lambda i: (jnp.maximum(i * (tm // 8) - 1, 0), cb))
    nxt = lambda cb: pl.BlockSpec((8, 512), lambda i: (jnp.minimum((i + 1) * (tm // 8), T // 8 - 1), cb))
    tile = pl.BlockSpec((tm, 512), lambda i: (i, 0))
    return pl.pallas_call(
        body, name="mix_bwd", grid=(nt,),
        in_specs=[blk(0), blk(1), blk(2), blk(CB_GMLA), blk(CB_CH), blk(CB_CB), blk(CB_CC), blk(CB_GCONV),
                  blk(CB_GSWA), prev(CB_CH), prev(CB_CC), nxt(1), nxt(CB_CB), nxt(CB_GCONV), tile, tile,
                  pl.BlockSpec((8, 512), lambda i: (0, 0))],
        out_specs=[pl.BlockSpec((tm, 3072), lambda i: (i, DPB_MIX)), tile, tile,
                   pl.BlockSpec((8, 512), lambda i: (0, 0))],
        out_shape=[_sds((T, NP), MXU_DTYPE), _sds((T, 512), F32), _sds((T, 512), F32), _sds((8, 512), F32)],
        compiler_params=_cp(("arbitrary",), 48))(
            dycat, dycat, dycat, proj, proj, proj, proj, proj, proj, proj, proj, dycat, proj, proj,
            o_mla, o_swa, conv_w)


def _swa_bwd(proj, o_swa, do_swa, lw, dproj):
    T = proj.shape[0]
    tm = min(TM_SWA, T)
    nb = tm // BLOCK
    scale = SWA_HEAD_DIM ** -0.5

    def body(q_ref, k_ref, v_ref, pk_ref, pv_ref, o_ref, do_ref, qw_ref, kw_ref, alibi_ref, sink_ref, dproj_in,
             dq_ref, dk_ref, dv_ref, dqw_ref, dsink_ref):
        i = pl.program_id(0)

        @pl.when(i == 0)
        def _():
            dk_ref[...] = jnp.zeros_like(dk_ref)
            dv_ref[...] = jnp.zeros_like(dv_ref)
            dqw_ref[...] = jnp.zeros_like(dqw_ref)
            dsink_ref[...] = jnp.zeros_like(dsink_ref)

        p, p_sink, c = _swa_probs(i, nb, q_ref, k_ref, v_ref, pk_ref, pv_ref, qw_ref, kw_ref, alibi_ref, sink_ref)
        half1, kp, qn, qhat, qr = c["half1"], c["kp"], c["qn"], c["qhat"], c["qr"]
        rows1 = lax.broadcasted_iota(jnp.int32, (LANES, 1), 0) >= 64
        kpt = _swa_kv_variants_t(c["kn"].T, rows1)
        vp = _swa_kv_variants(c["v_all"], half1)
        qw = qw_ref[...]
        rows = [slice(BLOCK * b, BLOCK * (b + 1)) for b in range(nb)]
        keys = [slice(BLOCK * b, BLOCK * (b + 2)) for b in range(nb)]
        dob, dot_b, dd0, dd1 = [], [], [], []
        for j in range(4):
            cols = slice(LANES * j, LANES * (j + 1))
            do = do_ref[:, cols]
            do_t = do.T
            prod_t = do_t * o_ref[:, cols].T
            dob.append(do.astype(MXU_DTYPE))
            dot_b.append(do_t.astype(MXU_DTYPE))
            dd0.append(jnp.sum(jnp.where(rows1, 0.0, prod_t), axis=0, keepdims=True))
            dd1.append(jnp.sum(jnp.where(rows1, prod_t, 0.0), axis=0, keepdims=True))
        dd = jnp.stack([(dd1 if h % 2 else dd0)[h // 2][:, rows[b]] for b in range(nb) for h in range(HEADS)])
        dp = jnp.stack([_dot(vp[(h // 4, h % 2)][keys[b]], dot_b[h // 2][:, rows[b]])
                        for b in range(nb) for h in range(HEADS)])
        ds = (p * (dp - dd) * scale).astype(MXU_DTYPE)
        dsink = -jnp.sum(p_sink * dd, axis=2, keepdims=True)
        pb = p.astype(MXU_DTYPE)

        dqw = jnp.zeros((1, LANES), F32)
        for j in range(4):
            g = j // 2
            dqn_t = [_dot(kpt[(g, 0)][:, keys[b]], ds[HEADS * b + 2 * j])
                     + _dot(kpt[(g, 1)][:, keys[b]], ds[HEADS * b + 2 * j + 1]) for b in range(nb)]
            dqn = (jnp.concatenate(dqn_t, axis=1) if nb > 1 else dqn_t[0]).T
            dqw = dqw + jnp.sum(dqn * qhat[j], axis=0, keepdims=True)
            dq_ref[:, LANES * j:LANES * (j + 1)] = _rms_halves_bwd(dqn, qhat[j], qr[j], qw, half1).astype(MXU_DTYPE)
        dqw_ref[...] += _row0(dqw + pltpu.roll(dqw, 64, 1))

        dk_tot = jnp.zeros((tm + BLOCK, LANES), F32)
        dv_tot = jnp.zeros((tm + BLOCK, LANES), F32)
        for b in range(nb):
            dk_b = jnp.zeros((2 * BLOCK, LANES), F32)
            dv_b = jnp.zeros((2 * BLOCK, LANES), F32)
            for g in range(2):
                for r in range(2):
                    own = half1 if r else jnp.logical_not(half1)
                    ha, hb = HEADS * b + 4 * g + r, HEADS * b + 4 * g + 2 + r
                    qa, qb = qn[2 * g][rows[b]], qn[2 * g + 1][rows[b]]
                    da, db = dob[2 * g][rows[b]], dob[2 * g + 1][rows[b]]
                    dkp = jnp.where(own, _dot(ds[ha], qa) + _dot(ds[hb], qb), 0.0)
                    dvp = jnp.where(own, _dot(pb[ha], da) + _dot(pb[hb], db), 0.0)
                    if g != r:
                        dkp = pltpu.roll(dkp, 64, 1)
                        dvp = pltpu.roll(dvp, 64, 1)
                    dk_b = dk_b + dkp
                    dv_b = dv_b + dvp
            pad = lambda x: jnp.concatenate(
                [z for z in (jnp.zeros((BLOCK * b, LANES), F32), x, jnp.zeros((BLOCK * (nb - 1 - b), LANES), F32))
                 if z.shape[0]], axis=0)
            dk_tot = dk_tot + pad(dk_b)
            dv_tot = dv_tot + pad(dv_b)
        dst = pl.ds(pl.multiple_of(i * tm, BLOCK), tm + BLOCK)
        dk_ref[dst, :] += dk_tot
        dv_ref[dst, :] += dv_tot

        row8 = lax.broadcasted_iota(jnp.int32, (8, LANES), 0)
        dsink_tile = jnp.zeros((8, LANES), F32)
        for b in range(nb):
            for h in range(HEADS):
                dsink_tile = dsink_tile + jnp.where(row8 == h, jnp.broadcast_to(dsink[HEADS * b + h], (8, LANES)), 0.0)
        dsink_ref[...] += dsink_tile

    prev = lambda cb: pl.BlockSpec((BLOCK, LANES), lambda i: (jnp.maximum(i * nb - 1, 0), cb))
    tile = pl.BlockSpec((tm, 512), lambda i: (i, 0))
    small = pl.BlockSpec((8, LANES), lambda i: (0, 0))
    acc = pl.BlockSpec((T + BLOCK, LANES), lambda i: (0, 0))
    return pl.pallas_call(
        body, name="swa_bwd", grid=(T // tm,),
        in_specs=[pl.BlockSpec((tm, 512), lambda i: (i, CB_SQ)), pl.BlockSpec((tm, LANES), lambda i: (i, CB_SK)),
                  pl.BlockSpec((tm, LANES), lambda i: (i, CB_SV)), prev(CB_SK), prev(CB_SV), tile, tile,
                  pl.BlockSpec((1, LANES), lambda i: (0, 0)), pl.BlockSpec((1, LANES), lambda i: (0, 0)),
                  pl.BlockSpec((nb * HEADS, 2 * BLOCK, BLOCK), lambda i: (0, 0, 0)),
                  pl.BlockSpec(memory_space=pltpu.SMEM), pl.BlockSpec(memory_space=pl.ANY)],
        out_specs=[pl.BlockSpec((tm, 512), lambda i: (i, DPB_SQ)), acc, acc, small, small],
        out_shape=[_sds((T, NP), MXU_DTYPE), _sds((T + BLOCK, LANES), F32), _sds((T + BLOCK, LANES), F32),
                   _sds((8, LANES), F32), _sds((8, LANES), F32)],
        input_output_aliases={11: 0},
        compiler_params=_cp(("arbitrary",), 48))(
            proj, proj, proj, proj, proj, o_swa, do_swa, lw["sqn"], lw["skn"], jnp.tile(_swa_alibi(), (nb, 1, 1)),
            lw["sinks"], dproj)


def _swa_kv_bwd(proj, dkn, dv, lw, dproj):
    T = proj.shape[0]
    tm = min(TM_SWA, T)
    dkn, dv = dkn[BLOCK:], dv[BLOCK:]

    def body(k_ref, dkn_ref, dv_ref, kw_ref, dproj_in, d_ref, dkw_ref):
        i = pl.program_id(0)

        @pl.when(i == 0)
        def _():
            dkw_ref[...] = jnp.zeros_like(dkw_ref)

        half1 = lax.broadcasted_iota(jnp.int32, (1, LANES), 1) >= 64
        khat, kr = _rms_halves(k_ref[...], half1)
        dkn_t = dkn_ref[...]
        dkw = jnp.sum(dkn_t * khat, axis=0, keepdims=True)
        dkw_ref[...] += _row0(dkw + pltpu.roll(dkw, 64, 1))
        d_ref[:, 0:LANES] = _rms_halves_bwd(dkn_t, khat, kr, kw_ref[...], half1).astype(MXU_DTYPE)
        d_ref[:, LANES:2 * LANES] = dv_ref[...].astype(MXU_DTYPE)

    return pl.pallas_call(
        body, name="swa_kv_bwd", grid=(T // tm,),
        in_specs=[pl.BlockSpec((tm, LANES), lambda i: (i, CB_SK)), pl.BlockSpec((tm, LANES), lambda i: (i, 0)),
                  pl.BlockSpec((tm, LANES), lambda i: (i, 0)), pl.BlockSpec((1, LANES), lambda i: (0, 0)),
                  pl.BlockSpec(memory_space=pl.ANY)],
        out_specs=[pl.BlockSpec((tm, 2 * LANES), lambda i: (i, DPB_SKV)), pl.BlockSpec((8, LANES), lambda i: (0, 0))],
        out_shape=[_sds((T, NP), MXU_DTYPE), _sds((8, LANES), F32)],
        input_output_aliases={4: 0},
        compiler_params=_cp(("arbitrary",), 32))(proj, dkn, dv, lw["skn"], dproj)


def _mla_attn_bwd(q, k, kt, vt, o, do, lse):
    T = q.shape[1]
    tk = min(TK, T // 2)
    tq = 2 * tk

    def body(q_ref, k_ref, kt_ref, vt_ref, o_ref, do_ref, lse_ref, dq_ref, dk_ref, dv_ref, dq_s, lse_s, dd_s,
             s_a, s_b, p_a, p_b):
        h = pl.program_id(0)
        i = pl.program_id(1)

        @pl.when(i == 0)
        def _():
            dk_ref[...] = jnp.zeros_like(dk_ref)
            dv_ref[...] = jnp.zeros_like(dv_ref)

        qry = lax.broadcasted_iota(jnp.int32, (tq, tk), 0)
        key = lax.broadcasted_iota(jnp.int32, (tq, tk), 1)
        own = (lax.broadcasted_iota(jnp.int32, (1, LANES), 1) // 64) == (h % 2)
        do_own = jnp.where(own, do_ref[...], 0.0)
        dob = do_own.astype(MXU_DTYPE)
        dob_t = do_own.T.astype(MXU_DTYPE)
        qh = q_ref[0]
        qh_t = qh.astype(F32).T.astype(MXU_DTYPE)
        dd_col = jnp.sum(do_own * o_ref[...], axis=-1, keepdims=True)
        lse_col = jnp.broadcast_to(lse_ref[0], (LANES, tq)).T
        for c in range(tk // LANES):
            lse_s[:, LANES * c:LANES * (c + 1)] = lse_col
            dd_s[:, LANES * c:LANES * (c + 1)] = jnp.broadcast_to(dd_col, (tq, LANES))
        dq_s[...] = jnp.zeros_like(dq_s)

        def scores(kj, s_buf, p_buf):
            s_buf[...] = _dot(qh, kt_ref[0, kj])
            p_buf[...] = _dot(dob, vt_ref[0, kj])

        def consume(kj, s_buf, p_buf, diag):
            rows = pl.ds(pl.multiple_of(kj * tk, tk), tk)
            s = s_buf[...]
            if diag is not None:
                s = jnp.where(key + diag * tk <= qry, s, NEG_INF)
            p = jnp.exp2(s - lse_s[...])
            ds = (p * (p_buf[...] - dd_s[...])).astype(MXU_DTYPE)
            dq_s[...] += _dot(ds, k_ref[0, rows, :])
            dk_ref[0, kj] += _dot(qh_t, ds)
            dv_ref[0, kj] += _dot(dob_t, p.astype(MXU_DTYPE))

        scores(0, s_a, p_a)

        def pair(kj):
            scores(kj + 1, s_b, p_b)
            consume(kj, s_a, p_a, None)
            scores(kj + 2, s_a, p_a)
            consume(kj + 1, s_b, p_b, None)

        def octet(ko, carry):
            for t in range(4):
                pair(8 * ko + 2 * t)
            return carry

        lax.fori_loop(0, i // 4, octet, 0)

        @pl.when(i % 4 >= 2)
        def _():
            pair(8 * (i // 4))
            pair(8 * (i // 4) + 2)

        @pl.when(i % 2 == 1)
        def _():
            pair(2 * i - 2)

        scores(2 * i + 1, s_b, p_b)
        consume(2 * i, s_a, p_a, 0)
        consume(2 * i + 1, s_b, p_b, 1)
        dq_ref[0] = dq_s[...]

    res = pl.BlockSpec((1, T, LANES), lambda h, i: (h, 0, 0))
    res_t = pl.BlockSpec((1, T // tk, LANES, tk), lambda h, i: (h, 0, 0, 0))
    buf = pltpu.VMEM((tq, tk), F32)
    acc_t = _sds((HEADS, T // tk, LANES, tk), F32)
    return pl.pallas_call(
        body, name="mla_attn_bwd", grid=(HEADS, T // tq),
        in_specs=[pl.BlockSpec((1, tq, LANES), lambda h, i: (h, i, 0)), res, res_t, res_t,
                  pl.BlockSpec((tq, LANES), lambda h, i: (i, h // 2)),
                  pl.BlockSpec((tq, LANES), lambda h, i: (i, h // 2)),
                  pl.BlockSpec((1, 1, tq), lambda h, i: (h, 0, i))],
        out_specs=[pl.BlockSpec((1, tq, LANES), lambda h, i: (h, i, 0)), res_t, res_t],
        out_shape=[_sds((HEADS, T, LANES), F32), acc_t, acc_t],
        scratch_shapes=[pltpu.VMEM((tq, LANES), F32), buf, buf, buf, buf, buf, buf],
        compiler_params=_cp(("parallel", "arbitrary"), 48))(q, k, kt, vt, o, do, lse)


def _mla_prep_bwd(proj, dq, dk, dv, lw, rope, dproj):
    T = proj.shape[0]
    tm = min(TK, T // 2)

    def body(ql_ref, kvl_ref, kr_ref, dq_ref, dk_ref, dv_ref, qa_ref, kva_ref, wq_ref, wk_ref, wv_ref,
             wqt_ref, wkt_ref, wvt_ref, qn_ref, kn_ref, c_ref, s1_ref, s2_ref, dproj_in,
             d_ref, dwq_ref, dwk_ref, dwv_ref, dqa_ref, dkva_ref, dqn_ref, dkn_ref):
        i = pl.program_id(0)

        @pl.when(i == 0)
        def _():
            for ref in (dwq_ref, dwk_ref, dwv_ref, dqa_ref, dkva_ref, dqn_ref, dkn_ref):
                ref[...] = jnp.zeros_like(ref)

        c, s1, s2 = c_ref[...], s1_ref[...], s2_ref[...]
        lane = lax.broadcasted_iota(jnp.int32, (1, LANES), 1)
        qlhat, qlr = _rms(ql_ref[...], MLA_Q_LORA)
        qn = (qlhat * qa_ref[...]).astype(MXU_DTYPE)
        kvhat, kvr = _rms(kvl_ref[...], MLA_KV_LORA)
        kvn = (kvhat * kva_ref[...]).astype(MXU_DTYPE)
        kr = kr_ref[...]
        x3, r3 = _rms(jnp.stack([_dot(qn, wq_ref[h]) for h in range(HEADS)]), MLA_QK)
        dy3 = _rope_bwd(dq_ref[...] * MLA_SCALE, c, s1, s2)
        dqw = jnp.sum(jnp.sum(dy3 * x3, axis=0), axis=0, keepdims=True)
        dx3 = _rms_bwd(dy3, x3, r3, qn_ref[...], MLA_QK).astype(MXU_DTYPE)
        dqnl = jnp.zeros((tm, MLA_Q_LORA), F32)
        for h in range(HEADS):
            dwq_ref[h] += _dot_tn(qn, dx3[h])
            dqnl = dqnl + _dot(dx3[h], wqt_ref[h])

        x3, r3 = _rms(jnp.stack([_dot(kvn, wk_ref[h]) for h in range(HEADS)]) + kr, MLA_QK)
        dy3 = _rope_bwd(jnp.stack([dk_ref[h, 0].T for h in range(HEADS)]) * LN2, c, s1, s2)
        dkw = jnp.sum(jnp.sum(dy3 * x3, axis=0), axis=0, keepdims=True)
        dxf3 = _rms_bwd(dy3, x3, r3, kn_ref[...], MLA_QK)
        dkr = jnp.sum(dxf3, axis=0)
        dx3 = dxf3.astype(MXU_DTYPE)
        dkvn = jnp.zeros((tm, MLA_KV_LORA), F32)
        for h in range(HEADS):
            dwk_ref[h] += _dot_tn(kvn, dx3[h])
            dkvn = dkvn + _dot(dx3[h], wkt_ref[h])
        dvc = jnp.concatenate([(dv_ref[2 * j, 0] + dv_ref[2 * j + 1, 0]).T for j in range(4)],
                              axis=1).astype(MXU_DTYPE)
        dwv_ref[...] += _dot_tn(kvn, dvc)
        dkvn = dkvn + _dot(dvc, wvt_ref[...])
        dqa_ref[...] += _row0(jnp.sum(dqnl * qlhat, axis=0, keepdims=True))
        dkva_ref[...] += _row0(jnp.sum(dkvn * kvhat, axis=0, keepdims=True))
        dqn_ref[...] += _row0(dqw)
        dkn_ref[...] += _row0(dkw)
        d_ref[:, 0:256] = _rms_bwd(dqnl, qlhat, qlr, qa_ref[...], MLA_Q_LORA).astype(MXU_DTYPE)
        d_ref[:, 256:384] = _rms_bwd(dkvn, kvhat, kvr, kva_ref[...], MLA_KV_LORA).astype(MXU_DTYPE)
        d_ref[:, 384:512] = jnp.where((lane >= 64) & (lane < 96), dkr, 0.0).astype(MXU_DTYPE)

    full = lambda shape: pl.BlockSpec(shape, lambda i: (0,) * len(shape))
    hd = pl.BlockSpec((HEADS, tm, LANES), lambda i: (0, i, 0))
    hdt = pl.BlockSpec((HEADS, 1, LANES, tm), lambda i: (0, i, 0, 0))
    tab = pl.BlockSpec((tm, LANES), lambda i: (i, 0))
    return pl.pallas_call(
        body, name="mla_prep_bwd", grid=(T // tm,),
        in_specs=[pl.BlockSpec((tm, 256), lambda i: (i, CB_QLAT)), pl.BlockSpec((tm, LANES), lambda i: (i, CB_KVLAT)),
                  pl.BlockSpec((tm, LANES), lambda i: (i, CB_KROPE)), hd, hdt, hdt,
                  full((1, 256)), full((1, LANES)), full((HEADS, 256, LANES)), full((HEADS, LANES, LANES)),
                  full((LANES, 512)), full((HEADS, LANES, 256)), full((HEADS, LANES, LANES)), full((512, LANES)),
                  full((1, LANES)), full((1, LANES)), tab, tab, tab, pl.BlockSpec(memory_space=pl.ANY)],
        out_specs=[pl.BlockSpec((tm, 512), lambda i: (i, DPB_MLA)), full((HEADS, 256, LANES)),
                   full((HEADS, LANES, LANES)), full((LANES, 512)), full((8, 256)), full((8, LANES)),
                   full((8, LANES)), full((8, LANES))],
        out_shape=[_sds((T, NP), MXU_DTYPE), _sds((HEADS, 256, LANES), F32), _sds((HEADS, LANES, LANES), F32),
                   _sds((LANES, 512), F32), _sds((8, 256), F32), _sds((8, LANES), F32), _sds((8, LANES), F32),
                   _sds((8, LANES), F32)],
        input_output_aliases={19: 0},
        compiler_params=_cp(("arbitrary",), 48))(
            proj, proj, proj, dq, dk, dv, lw["qa"], lw["kva"], lw["wq"], lw["wk"], lw["wv"],
            lw["wqt"], lw["wkt"], lw["wvt"], lw["qn"], lw["kn"], rope[0], rope[1], rope[2], dproj)


def _inproj_bwd_dx(dproj, wpt, x, g_in, ng):
    T, D = x.shape
    tm = min(TM_PROJ, T)

    def body(dp_ref, wt_ref, x_ref, g_ref, w_ref, dx_ref, dw_ref):
        i = pl.program_id(0)

        @pl.when(i == 0)
        def _():
            dw_ref[...] = jnp.zeros_like(dw_ref)

        dh = _dot(dp_ref[...], wt_ref[...])
        xhat, r = _rms(x_ref[...], D)
        dw_ref[...] += _row0(jnp.sum(dh * xhat, axis=0, keepdims=True))
        dx_ref[...] = g_ref[...] + _rms_bwd(dh, xhat, r, w_ref[...], D)

    tile = pl.BlockSpec((tm, D), lambda i: (i, 0))
    return pl.pallas_call(
        body, name="inproj_bwd_dx", grid=(T // tm,),
        in_specs=[pl.BlockSpec((tm, NP), lambda i: (i, 0)), pl.BlockSpec((NP, D), lambda i: (0, 0)), tile, tile,
                  pl.BlockSpec((1, D), lambda i: (0, 0))],
        out_specs=[tile, pl.BlockSpec((8, D), lambda i: (0, 0))],
        out_shape=[_sds((T, D), F32), _sds((8, D), F32)],
        compiler_params=_cp(("arbitrary",), 48))(dproj, wpt, x, g_in, ng)


def _rope_tables(T, token=0.0):
    half = MLA_ROPE // 2
    inv_freq = jnp.power(jnp.float32(ROPE_THETA), -jnp.arange(half, dtype=F32) / half)
    z = lambda n: jnp.zeros((n,), F32)
    freq = jnp.concatenate([z(MLA_NOPE), inv_freq, inv_freq, z(32)])
    first = jnp.concatenate([z(64), jnp.ones((16,), F32), z(48)])
    second = jnp.concatenate([z(80), jnp.ones((16,), F32), z(32)])
    ang = (jnp.arange(T, dtype=F32) + token)[:, None] * freq[None, :]
    sin = jnp.sin(ang)
    return jnp.cos(ang), -sin * first[None, :], sin * second[None, :]


def _pad_lanes(v, n=LANES):
    v = v.reshape(1, -1)
    return jnp.pad(v, ((0, 0), (0, n - v.shape[1])))


def _pack_win_t(wt):
    z = lambda n: jnp.zeros((n, wt.shape[1]), wt.dtype)
    return jnp.concatenate([wt[416:2976], wt[3744:4256], wt[0:384], z(64), wt[384:416], z(32), wt[2976:3488],
                            wt[3488:3616], wt[3616:3744]], axis=0)


def _unpack_dwin(d):
    return jnp.concatenate([d[:, 3072:3456], d[:, 3520:3552], d[:, 0:2560], d[:, 3584:4096], d[:, 4096:4224],
                            d[:, 4224:4352], d[:, 2560:3072]], axis=1)


def _inproj_weights(l, norm_g, w_in_t):
    wpt = _pack_win_t(w_in_t)
    return dict(ng=norm_g[l].reshape(1, -1), wp=wpt.T, wpt=wpt)


def _mixer_weights(l, qa, wqb_full, kva, wkvb_full, qn, kn, conv_full, sqn, skn, sinks, w_out_full):
    wq = jnp.pad(wqb_full, ((0, 0), (0, 0), (0, LANES - MLA_QK)))
    wk = jnp.pad(wkvb_full[:, :, :MLA_NOPE], ((0, 0), (0, 0), (0, LANES - MLA_NOPE)))
    wv = jnp.transpose(wkvb_full[:, :, MLA_NOPE:], (1, 0, 2)).reshape(MLA_KV_LORA, GROUP_WIDTH)
    return dict(
        qa=qa[l].reshape(1, -1), kva=kva[l].reshape(1, -1),
        wq=wq, wk=wk, wv=wv, wqt=jnp.transpose(wq, (0, 2, 1)), wkt=jnp.transpose(wk, (0, 2, 1)), wvt=wv.T,
        qn=_pad_lanes(qn[l]), kn=_pad_lanes(kn[l]),
        conv=jnp.pad(conv_full, ((0, 5), (0, 0))),
        sqn=jnp.tile(sqn[l].reshape(1, -1), (1, 2)), skn=jnp.tile(skn[l].reshape(1, -1), (1, 2)),
        sinks=sinks[l], wo=w_out_full, wot=w_out_full.T)


def _layer_weights(l, norm_g, w_in_full, qa, wqb_full, kva, wkvb_full, qn, kn, conv_full, sqn, skn, sinks,
                   w_out_full):
    return dict(_inproj_weights(l, norm_g, w_in_full.T),
                **_mixer_weights(l, qa, wqb_full, kva, wkvb_full, qn, kn, conv_full, sqn, skn, sinks, w_out_full))


def _layer_fwd(x, lw, rope, late_weights=None, target=None):
    proj, h = _inproj_fwd(x, lw["ng"], lw["wp"])
    if late_weights is not None:
        lw = dict(lw, **late_weights(proj))
    q, k, kt, vt = _mla_prep_fwd(proj, lw, rope)
    o_mla, lse = _mla_attn_fwd(q, k, vt)
    o_swa = _swa_fwd(proj, lw)
    ycat = _mix_fwd(proj, o_mla, o_swa, lw["conv"])
    if target is None:
        out = _mm_nn(ycat, lw["wo"], "outproj_fwd", residual=x)
    else:
        out = _outproj_loss(ycat, lw["wo"], x, target)
    return out, dict(x=x, proj=proj, h=h, q=q, k=k, kt=kt, vt=vt, o_mla=o_mla, lse=lse, o_swa=o_swa, ycat=ycat,
                     lw=lw)


def _layer_bwd(g, sv, lw, rope, on_big_grads=None):
    proj = sv["proj"]
    dycat, d_wo = _outproj_bwd(g, sv["ycat"], lw["wot"])
    dproj, do_mla, do_swa, d_conv = _mix_bwd(dycat, proj, sv["o_mla"], sv["o_swa"], lw["conv"])
    dproj, dkn_acc, dv_acc, d_sqn, d_sinks = _swa_bwd(proj, sv["o_swa"], do_swa, lw, dproj)
    dproj, d_skn = _swa_kv_bwd(proj, dkn_acc, dv_acc, lw, dproj)
    dq, dk, dv = _mla_attn_bwd(sv["q"], sv["k"], sv["kt"], sv["vt"], sv["o_mla"], do_mla, sv["lse"])
    dproj, d_wq, d_wk, d_wv, d_qa, d_kva, d_qn, d_kn = _mla_prep_bwd(proj, dq, dk, dv, lw, rope, dproj)
    grads = dict(
        w_out=d_wo, w_qb=d_wq[:, :, :MLA_QK],
        w_kvb=jnp.concatenate([d_wk[:, :, :MLA_NOPE],
                               jnp.transpose(d_wv.reshape(MLA_KV_LORA, HEADS, MLA_NOPE), (1, 0, 2))], axis=2))
    token = 0.0 if on_big_grads is None else on_big_grads("mixer", grads)
    d_wp = _mm_tn(sv["h"], dproj, "inproj_bwd_dw", WIRE_DTYPE, tn=NP // 2)
    grads["w_in"] = _unpack_dwin(d_wp)
    token = token if on_big_grads is None else token + on_big_grads("w_in", grads)
    dx, d_ng = _inproj_bwd_dx(dproj, lw["wpt"], sv["x"], g, lw["ng"] + token)
    grads.update(
        conv=d_conv[0:3], norm_g=d_ng[0], qa=d_qa[0], kva=d_kva[0], qn=d_qn[0, :MLA_QK], kn=d_kn[0, :MLA_QK],
        sqn=d_sqn[0, :SWA_HEAD_DIM], skn=d_skn[0, :SWA_HEAD_DIM], sinks=d_sinks[:, 0])
    return dx, grads


def _local_step(x, target, lws, rope):
    saved = []
    for l, lw in enumerate(lws):
        x, sv = _layer_fwd(x, lw, rope, target=target if l == len(lws) - 1 else None)
        saved.append(sv)
    g, loss_tile = x
    grads = [None] * len(lws)
    for l in reversed(range(len(lws))):
        g, grads[l] = _layer_bwd(g, saved[l], lws[l], rope)
    return loss_tile, g, grads


def _my_coords():
    return lax.axis_index("x"), lax.axis_index("y"), lax.axis_index("c")


def _peer(me, k):
    x, y, c = me
    return (1 - x if k & 4 else x, 1 - y if k & 2 else y, 1 - c if k & 1 else c)


def _lin(d):
    return 4 * d[0] + 2 * d[1] + d[2]


def _push_copies(ins, lands, send_sems, recv_sems, gather):
    me = _my_coords()
    my = _lin(me)
    out, inc = [], []
    for a in range(len(ins)):
        for k in range(1, N_DEV):
            peer = _peer(me, k)
            sems = dict(send_sem=send_sems.at[a * 7 + k - 1], recv_sem=recv_sems.at[a * 7 + k - 1],
                        device_id=peer, device_id_type=pl.DeviceIdType.MESH)
            src = ins[a] if gather else ins[a].at[_lin(peer)]
            out.append(pltpu.make_async_remote_copy(src_ref=src, dst_ref=lands[a].at[my], **sems))
            inc.append(pltpu.make_async_remote_copy(src_ref=src, dst_ref=lands[a].at[_lin(peer)], **sems))
    return out, inc


def _push_start(arrays, name, gather):
    n = len(arrays)
    land_shapes = [((N_DEV,) + a.shape) if gather else a.shape for a in arrays]

    def body(*refs):
        ins, lands = refs[:n], refs[n:2 * n]
        send_sems, recv_sems = refs[2 * n], refs[2 * n + 1]
        token = refs[-1]
        out, _ = _push_copies(ins, lands, send_sems, recv_sems, gather)
        for cp in out:
            cp.start()
        token[...] = jnp.zeros_like(token)

    hbm = pl.BlockSpec(memory_space=pltpu.HBM)
    sem = pl.BlockSpec(memory_space=pltpu.SEMAPHORE)
    res = pl.pallas_call(
        body, name=name,
        out_shape=(pltpu.SemaphoreType.DMA((7 * n,)), pltpu.SemaphoreType.DMA((7 * n,)),
                   *[pltpu.HBM(a.shape, a.dtype) for a in arrays],
                   *[pltpu.HBM(s, a.dtype) for s, a in zip(land_shapes, arrays)],
                   _sds((8, LANES), F32)),
        in_specs=(hbm,) * (2 * n),
        out_specs=(sem, sem) + (hbm,) * (2 * n) + (pl.BlockSpec(memory_space=pltpu.VMEM),),
        input_output_aliases={i: 2 + i for i in range(2 * n)},
        compiler_params=pltpu.CompilerParams(has_side_effects=pltpu.SideEffectType.DATAFLOW_SIDE_EFFECTING),
    )(*[pltpu.with_memory_space_constraint(a, pltpu.HBM) for a in arrays],
      *[pltpu.with_memory_space_constraint(lax.empty(s, a.dtype), pltpu.HBM) for s, a in zip(land_shapes, arrays)])
    return dict(send=res[0], recv=res[1], src=res[2:2 + n], land=res[2 + n:2 + 2 * n], token=res[-1][0, 0],
                gather=gather)


def _push_wait(handle, after, name):
    n = len(handle["src"])
    gather = handle["gather"]

    def body(*refs):
        ins, lands = refs[:n], refs[n:2 * n]
        send_sems, recv_sems = refs[2 * n], refs[2 * n + 1]
        out, inc = _push_copies(ins, lands, send_sems, recv_sems, gather)
        for cp in out:
            cp.wait_send()
        for cp in inc:
            cp.wait_recv()

    hbm = pl.BlockSpec(memory_space=pltpu.HBM)
    sem = pl.BlockSpec(memory_space=pltpu.SEMAPHORE)
    res = pl.pallas_call(
        body, name=name,
        out_shape=tuple(pltpu.HBM(a.shape, a.dtype) for a in (*handle["src"], *handle["land"])),
        in_specs=(hbm,) * (2 * n) + (sem, sem, pl.BlockSpec(memory_space=pl.ANY)),
        out_specs=(hbm,) * (2 * n),
        input_output_aliases={i: i for i in range(2 * n)},
        compiler_params=pltpu.CompilerParams(has_side_effects=pltpu.SideEffectType.DATAFLOW_SIDE_EFFECTING),
    )(*handle["src"], *handle["land"], handle["send"], handle["recv"], after)
    return res[n:]


def _small_all_reduce(v):
    R = v.shape[0]

    def body(v_ref, o_ref, buf, send_sems, recv_sems):
        me = _my_coords()
        my = _lin(me)
        sends = []
        for k in range(1, N_DEV):
            cp = pltpu.make_async_remote_copy(
                src_ref=v_ref, dst_ref=buf.at[my], send_sem=send_sems.at[k - 1], recv_sem=recv_sems.at[k - 1],
                device_id=_peer(me, k), device_id_type=pl.DeviceIdType.MESH)
            cp.start()
            sends.append(cp)
        buf[my] = v_ref[...]
        for k in range(1, N_DEV):
            pltpu.make_async_remote_copy(
                src_ref=v_ref, dst_ref=buf.at[_lin(_peer(me, k))], send_sem=send_sems.at[k - 1],
                recv_sem=recv_sems.at[k - 1], device_id=_peer(me, k),
                device_id_type=pl.DeviceIdType.MESH).wait_recv()
        for cp in sends:
            cp.wait_send()
        tot = buf[0]
        for d in range(1, N_DEV):
            tot = tot + buf[d]
        o_ref[...] = tot

    vm = pl.BlockSpec(memory_space=pltpu.VMEM)
    return pl.pallas_call(
        body, name="small_all_reduce", in_specs=[vm], out_specs=vm, out_shape=_sds(v.shape, F32),
        scratch_shapes=[pltpu.VMEM((N_DEV, R, LANES), F32), pltpu.SemaphoreType.DMA((7,)),
                        pltpu.SemaphoreType.DMA((7,))],
    )(v)


def _adamw_math(w, g, m, v):
    m = ADAM_B1 * m + (1.0 - ADAM_B1) * g
    v = ADAM_B2 * v + (1.0 - ADAM_B2) * (g * g)
    m_hat = m / (1.0 - ADAM_B1 ** ADAM_STEP)
    v_hat = v / (1.0 - ADAM_B2 ** ADAM_STEP)
    delta = -ADAM_LR * (m_hat / (jnp.sqrt(v_hat) + ADAM_EPS) + ADAM_WD * w)
    return delta, m, v


def _adamw(parts, w, m, v, name, tr):
    P, R, C = parts.shape
    tr = min(tr, R)

    def body(p_ref, w_ref, m_ref, v_ref, g_out, d_out, m_out, v_out):
        g = p_ref[0].astype(F32)
        for d in range(1, P):
            g = g + p_ref[d].astype(F32)
        delta, m_new, v_new = _adamw_math(w_ref[...], g, m_ref[...], v_ref[...])
        g_out[...] = g
        d_out[...] = delta
        m_out[...] = m_new
        v_out[...] = v_new

    tile = pl.BlockSpec((tr, C), lambda i: (i, 0))
    return pl.pallas_call(
        body, name=name, grid=(R // tr,),
        in_specs=[pl.BlockSpec((P, tr, C), lambda i: (0, i, 0)), tile, tile, tile],
        out_specs=[tile] * 4, out_shape=[_sds((R, C), F32)] * 4,
        compiler_params=_cp(("parallel",), 32))(parts, w, m, v)


SMALL = (("norm_g", D_MODEL), ("mla_q_a_norm", MLA_Q_LORA), ("mla_kv_a_norm", MLA_KV_LORA), ("mla_q_norm", MLA_QK),
         ("mla_k_norm", MLA_QK), ("swa_q_norm", SWA_HEAD_DIM), ("swa_k_norm", SWA_HEAD_DIM), ("swa_sinks", HEADS))
SMALL_GRAD_KEY = dict(norm_g="norm_g", mla_q_a_norm="qa", mla_kv_a_norm="kva", mla_q_norm="qn", mla_k_norm="kn",
                      swa_q_norm="sqn", swa_k_norm="skn", swa_sinks="sinks")
SMALL_ROWS = 32
CONV_ROWS = 24


def _pack_small(get):
    parts = []
    for l in range(DEPTH):
        for name, n in SMALL:
            v = get(name, l).reshape(-1)
            parts.append(jnp.pad(v, (0, (-n) % LANES)))
    return jnp.concatenate(parts).reshape(SMALL_ROWS, LANES)


def _unpack_small(packed):
    flat = packed.reshape(-1)
    out = {name: [] for name, _ in SMALL}
    off = 0
    for l in range(DEPTH):
        for name, n in SMALL:
            out[name].append(flat[off:off + n])
            off += n + (-n) % LANES
    return {name: jnp.stack(v) for name, v in out.items()}


def kernel(x, norm_g, w_in, mla_q_a_norm, mla_w_qb, mla_kv_a_norm, mla_w_kvb, mla_q_norm, mla_k_norm, conv_w, swa_q_norm, swa_k_norm, swa_sinks, w_out, loss_target, m_norm_g, m_w_in, m_mla_q_a_norm, m_mla_w_qb, m_mla_kv_a_norm, m_mla_w_kvb, m_mla_q_norm, m_mla_k_norm, m_conv_w, m_swa_q_norm, m_swa_k_norm, m_swa_sinks, m_w_out, v_norm_g, v_w_in, v_mla_q_a_norm, v_mla_w_qb, v_mla_kv_a_norm, v_mla_w_kvb, v_mla_q_norm, v_mla_k_norm, v_conv_w, v_swa_q_norm, v_swa_k_norm, v_swa_sinks, v_w_out):
    T = x.shape[1]
    weights = dict(norm_g=norm_g, w_in=w_in, mla_q_a_norm=mla_q_a_norm, mla_w_qb=mla_w_qb,
                   mla_kv_a_norm=mla_kv_a_norm, mla_w_kvb=mla_w_kvb, mla_q_norm=mla_q_norm, mla_k_norm=mla_k_norm,
                   conv_w=conv_w, swa_q_norm=swa_q_norm, swa_k_norm=swa_k_norm, swa_sinks=swa_sinks, w_out=w_out)
    mom_m = dict(norm_g=m_norm_g, w_in=m_w_in, mla_q_a_norm=m_mla_q_a_norm, mla_w_qb=m_mla_w_qb,
                 mla_kv_a_norm=m_mla_kv_a_norm, mla_w_kvb=m_mla_w_kvb, mla_q_norm=m_mla_q_norm,
                 mla_k_norm=m_mla_k_norm, conv_w=m_conv_w, swa_q_norm=m_swa_q_norm, swa_k_norm=m_swa_k_norm,
                 swa_sinks=m_swa_sinks, w_out=m_w_out)
    mom_v = dict(norm_g=v_norm_g, w_in=v_w_in, mla_q_a_norm=v_mla_q_a_norm, mla_w_qb=v_mla_w_qb,
                 mla_kv_a_norm=v_mla_kv_a_norm, mla_w_kvb=v_mla_w_kvb, mla_q_norm=v_mla_q_norm,
                 mla_k_norm=v_mla_k_norm, conv_w=v_conv_w, swa_q_norm=v_swa_q_norm, swa_k_norm=v_swa_k_norm,
                 swa_sinks=v_swa_sinks, w_out=v_w_out)

    my = _lin(_my_coords())

    def shards(l):
        return [w_in[l].astype(MXU_DTYPE).T, mla_w_qb[l].astype(MXU_DTYPE), mla_w_kvb[l].astype(MXU_DTYPE),
                w_out[l].astype(MXU_DTYPE), conv_w[l]]

    def inproj_weights(l, g_win_t):
        return _inproj_weights(l, norm_g, g_win_t.reshape(IN_COLS, D_MODEL))

    def mixer_weights(l, gathered):
        g_wqb, g_wkvb, g_wout, g_conv = gathered
        return _mixer_weights(
            l, mla_q_a_norm, g_wqb, mla_kv_a_norm, g_wkvb, mla_q_norm, mla_k_norm,
            jnp.transpose(g_conv, (1, 0, 2)).reshape(3, GROUP_WIDTH), swa_q_norm, swa_k_norm, swa_sinks,
            g_wout.reshape(D_MIX, D_MODEL))

    slot_of = dict(
        w_in=lambda g: jnp.transpose(g["w_in"].reshape(D_MODEL, N_DEV, IN_COLS // N_DEV), (1, 0, 2)),
        w_out=lambda g: g["w_out"].reshape(N_DEV, D_MIX // N_DEV, D_MODEL),
        w_qb=lambda g: g["w_qb"], w_kvb=lambda g: g["w_kvb"])

    def own_slot(landed, mine):
        return [lax.dynamic_update_index_in_dim(a, m, my, 0) for a, m in zip(landed, mine)]

    def landed(handle, after, name, mine):
        return own_slot(_push_wait(handle, after, name), mine)

    gather_in0 = _push_start(shards(0)[:1], "weight_gather_in0_start", gather=True)
    gather0 = _push_start(shards(0)[1:], "weight_gather0_start", gather=True)
    gather1 = _push_start(shards(1), "weight_gather1_start", gather=True)
    rope = _rope_tables(T, gather_in0["token"] + gather0["token"] + gather1["token"])
    lw0 = inproj_weights(0, landed(gather_in0, rope[0], "weight_gather_in0_wait", shards(0)[:1])[0])
    x1, sv0 = _layer_fwd(
        x[0], lw0, rope,
        late_weights=lambda proj: mixer_weights(0, landed(gather0, proj, "weight_gather0_wait", shards(0)[1:])))
    g1_all = landed(gather1, x1, "weight_gather1_wait", shards(1))
    (g2, loss_tile), sv1 = _layer_fwd(x1, dict(inproj_weights(1, g1_all[0]), **mixer_weights(1, g1_all[1:])), rope,
                                      target=loss_target[0])

    parts = {(1, "w_in"): ("w_in", "w_out", "w_qb", "w_kvb"), (0, "mixer"): ("w_out", "w_qb", "w_kvb"),
             (0, "w_in"): ("w_in",)}
    started = []

    def start_exchange(l, part, g):
        if (l, part) not in parts:
            return 0.0
        sl = [slot_of[n](g) for n in parts[(l, part)]]
        handle = _push_start(sl, "grad_exchange%d_%s_start" % (l, part), gather=False)
        started.append((l, part, sl, handle))
        return handle["token"]

    g1, grads1 = _layer_bwd(g2, sv1, sv1["lw"], rope, on_big_grads=functools.partial(start_exchange, 1))
    lw0b = dict(sv0["lw"], conv=sv0["lw"]["conv"] + started[0][3]["token"])
    grad_x, grads0 = _layer_bwd(g1, sv0, lw0b, rope, on_big_grads=functools.partial(start_exchange, 0))
    recv = {}
    for l, part, sl, handle in started:
        got = landed(handle, grad_x, "grad_exchange%d_%s_wait" % (l, part), [s[my] for s in sl])
        recv.update({(l, n): a for n, a in zip(parts[(l, part)], got)})
    grads = [grads0, grads1]
    r_win, r_wout, r_wqb, r_wkvb = [jnp.stack([recv[(0, n)], recv[(1, n)]], axis=1)
                                    for n in ("w_in", "w_out", "w_qb", "w_kvb")]

    small = jnp.concatenate([
        _pack_small(lambda name, l: grads[l][SMALL_GRAD_KEY[name]]),
        jnp.stack([g["conv"] for g in grads]).reshape(CONV_ROWS, LANES),
        loss_tile], axis=0)
    small = _small_all_reduce(small)
    loss = small[SMALL_ROWS + CONV_ROWS, 0]
    my = _lin(_my_coords())
    conv_g = lax.dynamic_slice_in_dim(small[SMALL_ROWS:SMALL_ROWS + CONV_ROWS].reshape(DEPTH, 3, GROUP_WIDTH),
                                      my * 64, 64, axis=2)

    out = {}

    def big(name, recv, rows, cols, tr):
        res = _adamw(recv.reshape(N_DEV, rows, cols), weights[name].reshape(rows, cols),
                     mom_m[name].reshape(rows, cols), mom_v[name].reshape(rows, cols), "adamw_" + name, tr)
        out[name] = [r.reshape(weights[name].shape) for r in res]

    big("w_in", r_win, DEPTH * D_MODEL, IN_COLS // N_DEV, 256)
    big("w_out", r_wout, DEPTH * D_MIX // N_DEV, D_MODEL, 192)
    big("mla_w_qb", r_wqb, DEPTH * MLA_Q_LORA, MLA_QK, 512)
    big("mla_w_kvb", r_wkvb, DEPTH * MLA_KV_LORA, 128, 256)

    pad_conv = lambda a: jnp.pad(a.reshape(-1), (0, 8 * LANES - 6 * 64)).reshape(8, LANES)
    cat = lambda src: jnp.concatenate([_pack_small(lambda name, l: src[name][l]), pad_conv(src["conv_w"])], axis=0)
    g_small = jnp.concatenate([small[:SMALL_ROWS], pad_conv(conv_g)], axis=0)
    res = _adamw(g_small[None], cat(weights), cat(mom_m), cat(mom_v), "adamw_small", SMALL_ROWS + 8)
    smalls = [_unpack_small(r[:SMALL_ROWS]) for r in res]
    for name, _ in SMALL:
        out[name] = [s[name] for s in smalls]
    out["conv_w"] = [r[SMALL_ROWS:].reshape(-1)[:6 * 64].reshape(DEPTH, 3, 64) for r in res]

    order = ["norm_g", "w_in", "mla_q_a_norm", "mla_w_qb", "mla_kv_a_norm", "mla_w_kvb", "mla_q_norm", "mla_k_norm",
             "conv_w", "swa_q_norm", "swa_k_norm", "swa_sinks", "w_out"]
    result = [loss, grad_x[None]]
    for idx in range(4):
        result += [out[name][idx] for name in order]
    return tuple(result)
```

```python
import functools

import jax
import jax.numpy as jnp
import numpy as np
from jax import lax
from jax.experimental import pallas as pl
from jax.experimental.pallas import tpu as pltpu

F32 = jnp.float32
MXU_DTYPE = jnp.bfloat16
WIRE_DTYPE = jnp.bfloat16

N_DEV = 8
DEPTH = 2
D_MODEL = 1024
GROUP_WIDTH = 512
D_MIX = 3 * GROUP_WIDTH
BLOCK = 128
RMS_EPS = 1e-6
NEG_INF = -1e30
HEADS = 8
MLA_QK = 96
MLA_NOPE = 64
MLA_ROPE = 32
MLA_Q_LORA = 256
MLA_KV_LORA = 128
ROPE_THETA = 10000.0
SWA_HEAD_DIM = 64
LANES = 128
IN_COLS = 4256

ADAM_LR = 0.001
ADAM_B1 = 0.9
ADAM_B2 = 0.999
ADAM_EPS = 1e-08
ADAM_WD = 0.01
ADAM_STEP = 10

NP = 4352
CB_GMLA, CB_CH, CB_CB, CB_CC, CB_GCONV, CB_GSWA, CB_SQ = 0, 1, 2, 3, 4, 5, 7
CB_QLAT = 12
CB_KVLAT, CB_KROPE = 26, 27
CB_SK, CB_SV = 32, 33
DPB_MIX, DPB_MLA, DPB_SQ, DPB_SKV = 0, 6, 7, 16

TM_PROJ = 512
TM_ROW = 256
TK = 256
TQ = 2 * TK
MLA_SCALE = MLA_QK ** -0.5
MLA_ONES_ROW = (64, 0)
LOG2E = 1.4426950408889634
LN2 = 0.6931471805599453
TM_SWA = 512
VMEM_MB = 2 ** 20


def _cp(sem, vmem_mb):
    return pltpu.CompilerParams(dimension_semantics=sem, vmem_limit_bytes=vmem_mb * VMEM_MB)


def _sds(shape, dtype):
    return jax.ShapeDtypeStruct(shape, dtype)


def _dot(a, b):
    return jnp.dot(a, b, preferred_element_type=F32)


def _dot_nt(a, b):
    return lax.dot_general(a, b, (((1,), (1,)), ((), ())), preferred_element_type=F32)


def _dot_tn(a, b):
    return lax.dot_general(a, b, (((0,), (0,)), ((), ())), preferred_element_type=F32)


def _rms(x, n):
    r = lax.rsqrt(jnp.sum(x * x, axis=-1, keepdims=True) * (1.0 / n) + RMS_EPS)
    return x * r, r


def _rms_bwd(dy, xhat, r, w, n):
    g = dy * w
    return r * (g - xhat * (jnp.sum(g * xhat, axis=-1, keepdims=True) * (1.0 / n)))


def _rms_halves(x, half1):
    x2 = x * x
    s0 = jnp.sum(jnp.where(half1, 0.0, x2), axis=-1, keepdims=True)
    s1 = jnp.sum(jnp.where(half1, x2, 0.0), axis=-1, keepdims=True)
    r = jnp.where(half1, lax.rsqrt(s1 * (1.0 / 64) + RMS_EPS), lax.rsqrt(s0 * (1.0 / 64) + RMS_EPS))
    return x * r, r


def _rms_halves_bwd(dy, xhat, r, w, half1):
    g = dy * w
    t = g * xhat
    m0 = jnp.sum(jnp.where(half1, 0.0, t), axis=-1, keepdims=True) * (1.0 / 64)
    m1 = jnp.sum(jnp.where(half1, t, 0.0), axis=-1, keepdims=True) * (1.0 / 64)
    return r * (g - xhat * jnp.where(half1, m1, m0))


def _sigmoid(x):
    return 1.0 / (1.0 + jnp.exp(-x))


def _rope(x, c, s1, s2):
    ax = x.ndim - 1
    return x * c + pltpu.roll(x, 112, ax) * s1 + pltpu.roll(x, 16, ax) * s2


def _rope_bwd(dy, c, s1, s2):
    ax = dy.ndim - 1
    return dy * c + pltpu.roll(dy * s1, 16, ax) + pltpu.roll(dy * s2, 112, ax)


def _fold_rows8(x):
    return jnp.sum(x.reshape(x.shape[0] // 8, 8, x.shape[1]), axis=0)


def _row0(v, rows=8):
    row = lax.broadcasted_iota(jnp.int32, (rows, v.shape[1]), 0)
    return jnp.where(row == 0, jnp.broadcast_to(v, (rows, v.shape[1])), 0.0)


def _mm_nn(a, b, name, out_dtype=F32, residual=None, tm=TM_PROJ):
    M, K = a.shape
    N = b.shape[1]
    tm = min(tm, M)

    def body(*refs):
        if residual is None:
            a_ref, b_ref, o_ref = refs
            acc = _dot(a_ref[...].astype(MXU_DTYPE), b_ref[...])
        else:
            a_ref, b_ref, r_ref, o_ref = refs
            acc = _dot(a_ref[...].astype(MXU_DTYPE), b_ref[...]) + r_ref[...]
        o_ref[...] = acc.astype(out_dtype)

    in_specs = [pl.BlockSpec((tm, K), lambda i: (i, 0)), pl.BlockSpec((K, N), lambda i: (0, 0))]
    args = [a, b]
    if residual is not None:
        in_specs.append(pl.BlockSpec((tm, N), lambda i: (i, 0)))
        args.append(residual)
    return pl.pallas_call(
        body, name=name, grid=(M // tm,), in_specs=in_specs,
        out_specs=pl.BlockSpec((tm, N), lambda i: (i, 0)), out_shape=_sds((M, N), out_dtype),
        compiler_params=_cp(("parallel",), 48))(*args)


def _mm_tn(a, b, name, out_dtype, tn, tk=512):
    T, M = a.shape
    N = b.shape[1]
    tk = min(tk, T)
    nk = T // tk

    def body(a_ref, b_ref, o_ref, acc_ref):
        k = pl.program_id(1)

        @pl.when(k == 0)
        def _():
            acc_ref[...] = jnp.zeros_like(acc_ref)

        acc_ref[...] += _dot_tn(a_ref[...].astype(MXU_DTYPE), b_ref[...].astype(MXU_DTYPE))

        @pl.when(k == nk - 1)
        def _():
            o_ref[...] = acc_ref[...].astype(out_dtype)

    return pl.pallas_call(
        body, name=name, grid=(N // tn, nk),
        in_specs=[pl.BlockSpec((tk, M), lambda n, k: (k, 0)), pl.BlockSpec((tk, tn), lambda n, k: (k, n))],
        out_specs=pl.BlockSpec((M, tn), lambda n, k: (0, n)), out_shape=_sds((M, N), out_dtype),
        scratch_shapes=[pltpu.VMEM((M, tn), F32)],
        compiler_params=_cp(("parallel", "arbitrary"), 48))(a, b)


def _inproj_fwd(x, ng, wp):
    T, D = x.shape
    tm = min(TM_PROJ, T)

    def body(x_ref, g_ref, w_ref, proj_ref, h_ref):
        xhat, _ = _rms(x_ref[...], D)
        h = (xhat * g_ref[...]).astype(MXU_DTYPE)
        h_ref[...] = h
        proj_ref[...] = _dot(h, w_ref[...])

    return pl.pallas_call(
        body, name="inproj_fwd", grid=(T // tm,),
        in_specs=[pl.BlockSpec((tm, D), lambda i: (i, 0)), pl.BlockSpec((1, D), lambda i: (0, 0)),
                  pl.BlockSpec((D, NP), lambda i: (0, 0))],
        out_specs=[pl.BlockSpec((tm, NP), lambda i: (i, 0)), pl.BlockSpec((tm, D), lambda i: (i, 0))],
        out_shape=[_sds((T, NP), F32), _sds((T, D), MXU_DTYPE)],
        compiler_params=_cp(("parallel",), 48))(x, ng, wp)


def _mla_prep_fwd(proj, lw, rope):
    T = proj.shape[0]
    tk = min(TK, T // 2)
    nsub = 2
    tm = nsub * tk

    def body(ql_ref, kvl_ref, kr_ref, qa_ref, kva_ref, wq_ref, wk_ref, wv_ref, qn_ref, kn_ref,
             c_ref, s1_ref, s2_ref, q_out, k_out, kt_out, vt_out):
        c, s1, s2 = c_ref[...], s1_ref[...], s2_ref[...]
        qhat, _ = _rms(ql_ref[...], MLA_Q_LORA)
        qn = (qhat * qa_ref[...]).astype(MXU_DTYPE)
        khat, _ = _rms(kvl_ref[...], MLA_KV_LORA)
        kvn = (khat * kva_ref[...]).astype(MXU_DTYPE)
        kr = kr_ref[...]
        half1 = lax.broadcasted_iota(jnp.int32, (tm, LANES), 1) >= 64
        ones_row = lax.broadcasted_iota(jnp.int32, (LANES, 1), 0)
        q3, _ = _rms(jnp.stack([_dot(qn, wq_ref[h]) for h in range(HEADS)]), MLA_QK)
        q_out[...] = (_rope(q3 * qn_ref[...], c, s1, s2) * (MLA_SCALE * LOG2E)).astype(MXU_DTYPE)
        k3, _ = _rms(jnp.stack([_dot(kvn, wk_ref[h]) for h in range(HEADS)]) + kr, MLA_QK)
        k3 = _rope(k3 * kn_ref[...], c, s1, s2)
        k_out[...] = k3.astype(MXU_DTYPE)
        for h in range(HEADS):
            for t in range(nsub):
                kt_out[h, t] = k3[h, tk * t:tk * (t + 1)].T.astype(MXU_DTYPE)
        v = _dot(kvn, wv_ref[...])
        for h in range(HEADS):
            vp = v[:, LANES * (h // 2):LANES * (h // 2 + 1)]
            own = half1 if h % 2 else jnp.logical_not(half1)
            vp = jnp.where(own, vp, 0.0)
            for t in range(nsub):
                vpt = vp[tk * t:tk * (t + 1)].T
                vt_out[h, t] = jnp.where(ones_row == MLA_ONES_ROW[h % 2], 1.0, vpt).astype(MXU_DTYPE)

    full = lambda shape: pl.BlockSpec(shape, lambda i: (0,) * len(shape))
    hd = pl.BlockSpec((HEADS, tm, LANES), lambda i: (0, i, 0))
    hdt = pl.BlockSpec((HEADS, nsub, LANES, tk), lambda i: (0, i, 0, 0))
    nat = _sds((HEADS, T, LANES), MXU_DTYPE)
    tr = _sds((HEADS, T // tk, LANES, tk), MXU_DTYPE)
    return pl.pallas_call(
        body, name="mla_prep_fwd", grid=(T // tm,),
        in_specs=[pl.BlockSpec((tm, 256), lambda i: (i, CB_QLAT)), pl.BlockSpec((tm, LANES), lambda i: (i, CB_KVLAT)),
                  pl.BlockSpec((tm, LANES), lambda i: (i, CB_KROPE)),
                  full((1, 256)), full((1, LANES)), full((HEADS, 256, LANES)), full((HEADS, LANES, LANES)),
                  full((LANES, 512)), full((1, LANES)), full((1, LANES)),
                  pl.BlockSpec((tm, LANES), lambda i: (i, 0)), pl.BlockSpec((tm, LANES), lambda i: (i, 0)),
                  pl.BlockSpec((tm, LANES), lambda i: (i, 0))],
        out_specs=[hd, hd, hdt, hdt],
        out_shape=[nat, nat, tr, tr],
        compiler_params=_cp(("parallel",), 32))(
            proj, proj, proj, lw["qa"], lw["kva"], lw["wq"], lw["wk"], lw["wv"], lw["qn"], lw["kn"],
            rope[0], rope[1], rope[2])


def _mla_attn_fwd(q, k, vt):
    T = q.shape[1]
    tk = min(TK, T // 2)
    tq = 2 * tk

    def body(q_ref, k_ref, vt_ref, o_ref, lse_ref, acc_s, m_s, s_a, s_b):
        i = pl.program_id(1)
        key = lax.broadcasted_iota(jnp.int32, (tk, tq), 0)
        qry = lax.broadcasted_iota(jnp.int32, (tk, tq), 1)
        qs = [q_ref[0], q_ref[1]]
        acc_s[...] = jnp.zeros_like(acc_s)
        m_s[...] = jnp.full(m_s.shape, NEG_INF, F32)

        def scores(kj, buf):
            rows = pl.ds(pl.multiple_of(kj * tk, tk), tk)
            for r in range(2):
                buf[r] = _dot_nt(k_ref[r, rows, :], qs[r])

        def consume(kj, buf, diag):
            for r in range(2):
                s = buf[r]
                if diag is not None:
                    s = jnp.where(key + diag * tk <= qry, s, NEG_INF)
                m_old = m_s[r]
                m_new = jnp.maximum(m_old, jnp.max(s, axis=0, keepdims=True))
                alpha = jnp.exp2(m_old - m_new)
                p = jnp.exp2(s - m_new)
                m_s[r] = m_new
                acc_s[r] = alpha * acc_s[r] + _dot(vt_ref[r, kj], p.astype(MXU_DTYPE))

        scores(0, s_a)

        def pair(kj):
            scores(kj + 1, s_b)
            consume(kj, s_a, None)
            scores(kj + 2, s_a)
            consume(kj + 1, s_b, None)

        def octet(ko, carry):
            for t in range(4):
                pair(8 * ko + 2 * t)
            return carry

        lax.fori_loop(0, i // 4, octet, 0)

        @pl.when(i % 4 >= 2)
        def _():
            pair(8 * (i // 4))
            pair(8 * (i // 4) + 2)

        @pl.when(i % 2 == 1)
        def _():
            pair(2 * i - 2)

        scores(2 * i + 1, s_b)
        consume(2 * i, s_a, 0)
        consume(2 * i + 1, s_b, 1)
        l = [acc_s[r, pl.ds(MLA_ONES_ROW[r], 1), :] for r in range(2)]
        head0 = lax.broadcasted_iota(jnp.int32, (LANES, 1), 0) < 64
        o_ref[...] = jnp.where(head0, acc_s[0] / l[0], acc_s[1] / l[1]).T
        for r in range(2):
            lse_ref[r] = m_s[r] + jnp.log2(l[r])

    return pl.pallas_call(
        body, name="mla_attn_fwd", grid=(HEADS // 2, T // tq),
        in_specs=[pl.BlockSpec((2, tq, LANES), lambda j, i: (j, i, 0)),
                  pl.BlockSpec((2, T, LANES), lambda j, i: (j, 0, 0)),
                  pl.BlockSpec((2, T // tk, LANES, tk), lambda j, i: (j, 0, 0, 0))],
        out_specs=[pl.BlockSpec((tq, LANES), lambda j, i: (i, j)),
                   pl.BlockSpec((2, 1, tq), lambda j, i: (j, 0, i))],
        out_shape=[_sds((T, GROUP_WIDTH), F32), _sds((HEADS, 1, T), F32)],
        scratch_shapes=[pltpu.VMEM((2, LANES, tq), F32), pltpu.VMEM((2, 1, tq), F32),
                        pltpu.VMEM((2, tk, tq), F32), pltpu.VMEM((2, tk, tq), F32)],
        compiler_params=_cp(("parallel", "arbitrary"), 40))(q, k, vt)


def _swa_kv_variants(x, half1):
    xs = pltpu.roll(x, 64, 1)
    out = {}
    for g in range(2):
        for r in range(2):
            own = half1 if r else jnp.logical_not(half1)
            out[(g, r)] = jnp.where(own, x if g == r else xs, 0.0).astype(MXU_DTYPE)
    return out


def _swa_alibi():
    ki = np.arange(2 * BLOCK)[:, None]
    qi = np.arange(BLOCK)[None, :]
    dist = BLOCK + qi - ki
    slopes = 2.0 ** -(np.arange(HEADS) + 1.0)
    tab = np.where(((dist >= 0) & (dist < BLOCK))[None], slopes[:, None, None] * dist[None], 1e30)
    return jnp.asarray(tab, F32)


def _swa_kv_variants_t(xt, rows1):
    xs = pltpu.roll(xt, 64, 0)
    out = {}
    for g in range(2):
        for r in range(2):
            own = rows1 if r else jnp.logical_not(rows1)
            out[(g, r)] = jnp.where(own, xt if g == r else xs, 0.0).astype(MXU_DTYPE)
    return out


def _swa_probs(i, nb, q_ref, k_ref, v_ref, pk_ref, pv_ref, qw_ref, kw_ref, alibi_ref, sink_ref):
    scale = SWA_HEAD_DIM ** -0.5
    half1 = lax.broadcasted_iota(jnp.int32, (1, LANES), 1) >= 64
    k_all = jnp.concatenate([pk_ref[...], k_ref[...]], axis=0)
    v_all = jnp.concatenate([pv_ref[...], v_ref[...]], axis=0)
    khat, _ = _rms_halves(k_all, half1)
    kn = khat * kw_ref[...]
    kp = _swa_kv_variants(kn, half1)
    qhat, qr, qn, qt = [], [], [], []
    for j in range(4):
        xh, r = _rms_halves(q_ref[:, LANES * j:LANES * (j + 1)], half1)
        qf = xh * qw_ref[...]
        qhat.append(xh)
        qr.append(r)
        qn.append(qf.astype(MXU_DTYPE))
        qt.append(qf.T.astype(MXU_DTYPE))
    key = lax.broadcasted_iota(jnp.int32, (2 * BLOCK, BLOCK), 0)
    first = jnp.where((i == 0) & (key < BLOCK), NEG_INF, 0.0)
    s = jnp.stack([_dot(kp[(h // 4, h % 2)][BLOCK * b:BLOCK * (b + 2)], qt[h // 2][:, BLOCK * b:BLOCK * (b + 1)])
                   for b in range(nb) for h in range(HEADS)]) * scale - alibi_ref[...]
    s = jnp.concatenate([s[:HEADS] + first, s[HEADS:]], axis=0) if nb > 1 else s + first
    sink = jnp.stack([jnp.full((1, 1), sink_ref[h], F32) for _ in range(nb) for h in range(HEADS)])
    m = jnp.maximum(jnp.max(s, axis=1, keepdims=True), sink)
    e = jnp.exp(s - m)
    es = jnp.exp(sink - m)
    inv = 1.0 / (jnp.sum(e, axis=1, keepdims=True) + es)
    return e * inv, es * inv, dict(half1=half1, kn=kn, kp=kp, v_all=v_all, qhat=qhat, qr=qr, qn=qn)


def _swa_fwd(proj, lw):
    T = proj.shape[0]
    tm = min(TM_SWA, T)
    nb = tm // BLOCK

    def body(q_ref, k_ref, v_ref, pk_ref, pv_ref, qw_ref, kw_ref, alibi_ref, sink_ref, o_ref):
        p, _, c = _swa_probs(pl.program_id(0), nb, q_ref, k_ref, v_ref, pk_ref, pv_ref, qw_ref, kw_ref, alibi_ref,
                             sink_ref)
        p = p.astype(MXU_DTYPE)
        rows1 = lax.broadcasted_iota(jnp.int32, (LANES, 1), 0) >= 64
        vpt = _swa_kv_variants_t(c["v_all"].T, rows1)
        for j in range(4):
            g = j // 2
            o_t = [_dot(vpt[(g, 0)][:, BLOCK * b:BLOCK * (b + 2)], p[HEADS * b + 2 * j])
                   + _dot(vpt[(g, 1)][:, BLOCK * b:BLOCK * (b + 2)], p[HEADS * b + 2 * j + 1]) for b in range(nb)]
            o_t = jnp.concatenate(o_t, axis=1) if nb > 1 else o_t[0]
            o_ref[:, LANES * j:LANES * (j + 1)] = o_t.T

    prev = lambda cb: pl.BlockSpec((BLOCK, LANES), lambda i: (jnp.maximum(i * nb - 1, 0), cb))
    return pl.pallas_call(
        body, name="swa_fwd", grid=(T // tm,),
        in_specs=[pl.BlockSpec((tm, 512), lambda i: (i, CB_SQ)), pl.BlockSpec((tm, LANES), lambda i: (i, CB_SK)),
                  pl.BlockSpec((tm, LANES), lambda i: (i, CB_SV)), prev(CB_SK), prev(CB_SV),
                  pl.BlockSpec((1, LANES), lambda i: (0, 0)), pl.BlockSpec((1, LANES), lambda i: (0, 0)),
                  pl.BlockSpec((nb * HEADS, 2 * BLOCK, BLOCK), lambda i: (0, 0, 0)),
                  pl.BlockSpec(memory_space=pltpu.SMEM)],
        out_specs=pl.BlockSpec((tm, 512), lambda i: (i, 0)),
        out_shape=_sds((T, GROUP_WIDTH), F32),
        compiler_params=_cp(("parallel",), 40))(
            proj, proj, proj, proj, proj, lw["sqn"], lw["skn"], jnp.tile(_swa_alibi(), (nb, 1, 1)), lw["sinks"])


def _shift_down(u, prev, n, row):
    tm = u.shape[0]
    out = pltpu.roll(u, n, 0)
    row8 = lax.broadcasted_iota(jnp.int32, prev.shape, 0)
    for t in range(n):
        src = jnp.sum(jnp.where(row8 == 8 - n + t, prev, 0.0), axis=0, keepdims=True)
        out = jnp.where(row == t, src, out)
    return out


def _shift_up(u, nxt, n, row):
    tm = u.shape[0]
    out = pltpu.roll(u, tm - n, 0)
    row8 = lax.broadcasted_iota(jnp.int32, nxt.shape, 0)
    for t in range(n):
        src = jnp.sum(jnp.where(row8 == t, nxt, 0.0), axis=0, keepdims=True)
        out = jnp.where(row == tm - n + t, src, out)
    return out


def _mix_fwd(proj, o_mla, o_swa, conv_w):
    T = proj.shape[0]
    tm = min(TM_ROW, T)

    def body(gm_ref, ch_ref, cb_ref, cc_ref, gc_ref, gs_ref, pch_ref, pcc_ref, om_ref, os_ref, w_ref, y_ref):
        i = pl.program_id(0)
        row = lax.broadcasted_iota(jnp.int32, (tm, GROUP_WIDTH), 0)
        u = cc_ref[...] * ch_ref[...]
        u_prev = jnp.where(i > 0, pcc_ref[...] * pch_ref[...], 0.0)
        z = (w_ref[0:1, :] * _shift_down(u, u_prev, 2, row) + w_ref[1:2, :] * _shift_down(u, u_prev, 1, row)
             + w_ref[2:3, :] * u)
        gm, gc, gs = gm_ref[...], gc_ref[...], gs_ref[...]
        y_ref[:, 0:512] = (om_ref[...] * (gm * _sigmoid(gm))).astype(MXU_DTYPE)
        y_ref[:, 512:1024] = (cb_ref[...] * z * (gc * _sigmoid(gc))).astype(MXU_DTYPE)
        y_ref[:, 1024:1536] = (os_ref[...] * (gs * _sigmoid(gs))).astype(MXU_DTYPE)

    blk = lambda cb: pl.BlockSpec((tm, 512), lambda i: (i, cb))
    prev = lambda cb: pl.BlockSpec((8, 512), lambda i: (jnp.maximum(i * (tm // 8) - 1, 0), cb))
    tile = pl.BlockSpec((tm, 512), lambda i: (i, 0))
    return pl.pallas_call(
        body, name="mix_fwd", grid=(T // tm,),
        in_specs=[blk(CB_GMLA), blk(CB_CH), blk(CB_CB), blk(CB_CC), blk(CB_GCONV), blk(CB_GSWA),
                  prev(CB_CH), prev(CB_CC), tile, tile, pl.BlockSpec((8, 512), lambda i: (0, 0))],
        out_specs=pl.BlockSpec((tm, D_MIX), lambda i: (i, 0)),
        out_shape=_sds((T, D_MIX), MXU_DTYPE),
        compiler_params=_cp(("parallel",), 32))(
            proj, proj, proj, proj, proj, proj, proj, proj, o_mla, o_swa, conv_w)


def _outproj_loss(ycat, wo, x, target):
    T, D = x.shape
    K = ycat.shape[1]
    tm = min(TM_PROJ, T)
    nt = T // tm

    def body(y_ref, w_ref, x_ref, t_ref, g_ref, loss_ref, acc_ref):
        i = pl.program_id(0)

        @pl.when(i == 0)
        def _():
            acc_ref[...] = jnp.zeros_like(acc_ref)

        err = _dot(y_ref[...], w_ref[...]) + x_ref[...] - t_ref[...]
        g_ref[...] = err * (1.0 / D)
        acc_ref[...] += _fold_rows8(err * err)

        @pl.when(i == nt - 1)
        def _():
            tot = jnp.sum(jnp.sum(acc_ref[...], axis=1, keepdims=True), axis=0, keepdims=True)
            loss_ref[...] = jnp.broadcast_to(tot * (0.5 / D), (8, LANES))

    tile = pl.BlockSpec((tm, D), lambda i: (i, 0))
    return pl.pallas_call(
        body, name="outproj_loss", grid=(nt,),
        in_specs=[pl.BlockSpec((tm, K), lambda i: (i, 0)), pl.BlockSpec((K, D), lambda i: (0, 0)), tile, tile],
        out_specs=[tile, pl.BlockSpec((8, LANES), lambda i: (0, 0))],
        out_shape=[_sds((T, D), F32), _sds((8, LANES), F32)],
        scratch_shapes=[pltpu.VMEM((8, D), F32)],
        compiler_params=_cp(("arbitrary",), 48))(ycat, wo, x, target)


def _outproj_bwd(g, ycat, wot):
    T, D = g.shape
    K = ycat.shape[1]
    tm = min(512, T)
    nt = T // tm

    def body(g_ref, y_ref, wt_ref, dy_ref, dw_ref, acc_ref):
        i = pl.program_id(0)

        @pl.when(i == 0)
        def _():
            acc_ref[...] = jnp.zeros_like(acc_ref)

        gb = g_ref[...].astype(MXU_DTYPE)
        dy_ref[...] = _dot(gb, wt_ref[...])
        acc_ref[...] += _dot_tn(y_ref[...], gb)

        @pl.when(i == nt - 1)
        def _():
            dw_ref[...] = acc_ref[...].astype(WIRE_DTYPE)

    return pl.pallas_call(
        body, name="outproj_bwd", grid=(nt,),
        in_specs=[pl.BlockSpec((tm, D), lambda i: (i, 0)), pl.BlockSpec((tm, K), lambda i: (i, 0)),
                  pl.BlockSpec((D, K), lambda i: (0, 0))],
        out_specs=[pl.BlockSpec((tm, K), lambda i: (i, 0)), pl.BlockSpec((K, D), lambda i: (0, 0))],
        out_shape=[_sds((T, K), F32), _sds((K, D), WIRE_DTYPE)],
        scratch_shapes=[pltpu.VMEM((K, D), F32)],
        compiler_params=_cp(("arbitrary",), 48))(g, ycat, wot)


def _mix_bwd(dycat, proj, o_mla, o_swa, conv_w):
    T = proj.shape[0]
    tm = min(TM_ROW, T)
    nt = T // tm

    def body(dym_ref, dyc_ref, dys_ref, gm_ref, ch_ref, cb_ref, cc_ref, gc_ref, gs_ref, pch_ref, pcc_ref,
             ndy_ref, ncb_ref, ngc_ref, om_ref, os_ref, w_ref,
             d1_ref, dom_ref, dos_ref, dw_ref):
        i = pl.program_id(0)

        @pl.when(i == 0)
        def _():
            dw_ref[...] = jnp.zeros_like(dw_ref)

        row = lax.broadcasted_iota(jnp.int32, (tm, GROUP_WIDTH), 0)

        def gate(g):
            sg = _sigmoid(g)
            return g * sg, sg * (1.0 + g * (1.0 - sg))

        gm = gm_ref[...]
        silu, dsilu = gate(gm)
        dym = dym_ref[...]
        dom_ref[...] = dym * silu
        d1_ref[:, 0:512] = (dym * om_ref[...] * dsilu).astype(MXU_DTYPE)

        gs = gs_ref[...]
        silu, dsilu = gate(gs)
        dys = dys_ref[...]
        dos_ref[...] = dys * silu
        d1_ref[:, 2560:3072] = (dys * os_ref[...] * dsilu).astype(MXU_DTYPE)

        ch, cb, cc, gc, dyc = ch_ref[...], cb_ref[...], cc_ref[...], gc_ref[...], dyc_ref[...]
        w0, w1, w2 = w_ref[0:1, :], w_ref[1:2, :], w_ref[2:3, :]
        u = cc * ch
        u_prev = jnp.where(i > 0, pcc_ref[...] * pch_ref[...], 0.0)
        u1 = _shift_down(u, u_prev, 1, row)
        u2 = _shift_down(u, u_prev, 2, row)
        z = w0 * u2 + w1 * u1 + w2 * u
        silu, dsilu = gate(gc)
        dz = dyc * cb * silu
        ngc = ngc_ref[...]
        dz_next = jnp.where(i < nt - 1, ndy_ref[...] * ncb_ref[...] * (ngc * _sigmoid(ngc)), 0.0)
        du = w2 * dz + w1 * _shift_up(dz, dz_next, 1, row) + w0 * _shift_up(dz, dz_next, 2, row)
        d1_ref[:, 512:1024] = (du * cc).astype(MXU_DTYPE)
        d1_ref[:, 1024:1536] = (dyc * z * silu).astype(MXU_DTYPE)
        d1_ref[:, 1536:2048] = (du * ch).astype(MXU_DTYPE)
        d1_ref[:, 2048:2560] = (dyc * cb * z * dsilu).astype(MXU_DTYPE)
        row8 = lax.broadcasted_iota(jnp.int32, (8, GROUP_WIDTH), 0)
        dw = jnp.zeros((8, GROUP_WIDTH), F32)
        for t, shifted in enumerate((u2, u1, u)):
            dw = dw + jnp.where(row8 == t, jnp.sum(dz * shifted, axis=0, keepdims=True), 0.0)
        dw_ref[...] += dw

    blk = lambda cb: pl.BlockSpec((tm, 512), lambda i: (i, cb))
    prev = lambda cb: pl.BlockSpec((8, 512), lambda i: (jnp.maximum(i * (tm // 8) - 1, 0), cb))
    nxt = lambda cb: pl.BlockSpec((8, 512), lambda i: (jnp.minimum((i + 1) * (tm // 8), T // 8 - 1), cb))
    tile = pl.BlockSpec((tm, 512), lambda i: (i, 0))
    return pl.pallas_call(
        body, name="mix_bwd", grid=(nt,),
        in_specs=[blk(0), blk(1), blk(2), blk(CB_GMLA), blk(CB_CH), blk(CB_CB), blk(CB_CC), blk(CB_GCONV),
                  blk(CB_GSWA), prev(CB_CH), prev(CB_CC), nxt(1), nxt(CB_CB), nxt(CB_GCONV), tile, tile,
                  pl.BlockSpec((8, 512), lambda i: (0, 0))],
        out_specs=[pl.BlockSpec((tm, 3072), lambda i: (i, DPB_MIX)), tile, tile,
                   pl.BlockSpec((8, 512), lambda i: (0, 0))],
        out_shape=[_sds((T, NP), MXU_DTYPE), _sds((T, 512), F32), _sds((T, 512), F32), _sds((8, 512), F32)],
        compiler_params=_cp(("arbitrary",), 48))(
            dycat, dycat, dycat, proj, proj, proj, proj, proj, proj, proj, proj, dycat, proj, proj,
            o_mla, o_swa, conv_w)


def _swa_bwd(proj, o_swa, do_swa, lw, dproj):
    T = proj.shape[0]
    tm = min(TM_SWA, T)
    nb = tm // BLOCK
    scale = SWA_HEAD_DIM ** -0.5

    def body(q_ref, k_ref, v_ref, pk_ref, pv_ref, o_ref, do_ref, qw_ref, kw_ref, alibi_ref, sink_ref, dproj_in,
             dq_ref, dk_ref, dv_ref, dqw_ref, dsink_ref):
        i = pl.program_id(0)

        @pl.when(i == 0)
        def _():
            dk_ref[...] = jnp.zeros_like(dk_ref)
            dv_ref[...] = jnp.zeros_like(dv_ref)
            dqw_ref[...] = jnp.zeros_like(dqw_ref)
            dsink_ref[...] = jnp.zeros_like(dsink_ref)

        p, p_sink, c = _swa_probs(i, nb, q_ref, k_ref, v_ref, pk_ref, pv_ref, qw_ref, kw_ref, alibi_ref, sink_ref)
        half1, kp, qn, qhat, qr = c["half1"], c["kp"], c["qn"], c["qhat"], c["qr"]
        rows1 = lax.broadcasted_iota(jnp.int32, (LANES, 1), 0) >= 64
        kpt = _swa_kv_variants_t(c["kn"].T, rows1)
        vp = _swa_kv_variants(c["v_all"], half1)
        qw = qw_ref[...]
        rows = [slice(BLOCK * b, BLOCK * (b + 1)) for b in range(nb)]
        keys = [slice(BLOCK * b, BLOCK * (b + 2)) for b in range(nb)]
        dob, dot_b, dd0, dd1 = [], [], [], []
        for j in range(4):
            cols = slice(LANES * j, LANES * (j + 1))
            do = do_ref[:, cols]
            do_t = do.T
            prod_t = do_t * o_ref[:, cols].T
            dob.append(do.astype(MXU_DTYPE))
            dot_b.append(do_t.astype(MXU_DTYPE))
            dd0.append(jnp.sum(jnp.where(rows1, 0.0, prod_t), axis=0, keepdims=True))
            dd1.append(jnp.sum(jnp.where(rows1, prod_t, 0.0), axis=0, keepdims=True))
        dd = jnp.stack([(dd1 if h % 2 else dd0)[h // 2][:, rows[b]] for b in range(nb) for h in range(HEADS)])
        dp = jnp.stack([_dot(vp[(h // 4, h % 2)][keys[b]], dot_b[h // 2][:, rows[b]])
                        for b in range(nb) for h in range(HEADS)])
        ds = (p * (dp - dd) * scale).astype(MXU_DTYPE)
        dsink = -jnp.sum(p_sink * dd, axis=2, keepdims=True)
        pb = p.astype(MXU_DTYPE)

        dqw = jnp.zeros((1, LANES), F32)
        for j in range(4):
            g = j // 2
            dqn_t = [_dot(kpt[(g, 0)][:, keys[b]], ds[HEADS * b + 2 * j])
                     + _dot(kpt[(g, 1)][:, keys[b]], ds[HEADS * b + 2 * j + 1]) for b in range(nb)]
            dqn = (jnp.concatenate(dqn_t, axis=1) if nb > 1 else dqn_t[0]).T
            dqw = dqw + jnp.sum(dqn * qhat[j], axis=0, keepdims=True)
            dq_ref[:, LANES * j:LANES * (j + 1)] = _rms_halves_bwd(dqn, qhat[j], qr[j], qw, half1).astype(MXU_DTYPE)
        dqw_ref[...] += _row0(dqw + pltpu.roll(dqw, 64, 1))

        dk_tot = jnp.zeros((tm + BLOCK, LANES), F32)
        dv_tot = jnp.zeros((tm + BLOCK, LANES), F32)
        for b in range(nb):
            dk_b = jnp.zeros((2 * BLOCK, LANES), F32)
            dv_b = jnp.zeros((2 * BLOCK, LANES), F32)
            for g in range(2):
                for r in range(2):
                    own = half1 if r else jnp.logical_not(half1)
                    ha, hb = HEADS * b + 4 * g + r, HEADS * b + 4 * g + 2 + r
                    qa, qb = qn[2 * g][rows[b]], qn[2 * g + 1][rows[b]]
                    da, db = dob[2 * g][rows[b]], dob[2 * g + 1][rows[b]]
                    dkp = jnp.where(own, _dot(ds[ha], qa) + _dot(ds[hb], qb), 0.0)
                    dvp = jnp.where(own, _dot(pb[ha], da) + _dot(pb[hb], db), 0.0)
                    if g != r:
                        dkp = pltpu.roll(dkp, 64, 1)
                        dvp = pltpu.roll(dvp, 64, 1)
                    dk_b = dk_b + dkp
                    dv_b = dv_b + dvp
            pad = lambda x: jnp.concatenate(
                [z for z in (jnp.zeros((BLOCK * b, LANES), F32), x, jnp.zeros((BLOCK * (nb - 1 - b), LANES), F32))
                 if z.shape[0]], axis=0)
            dk_tot = dk_tot + pad(dk_b)
            dv_tot = dv_tot + pad(dv_b)
        dst = pl.ds(pl.multiple_of(i * tm, BLOCK), tm + BLOCK)
        dk_ref[dst, :] += dk_tot
        dv_ref[dst, :] += dv_tot

        row8 = lax.broadcasted_iota(jnp.int32, (8, LANES), 0)
        dsink_tile = jnp.zeros((8, LANES), F32)
        for b in range(nb):
            for h in range(HEADS):
                dsink_tile = dsink_tile + jnp.where(row8 == h, jnp.broadcast_to(dsink[HEADS * b + h], (8, LANES)), 0.0)
        dsink_ref[...] += dsink_tile

    prev = lambda cb: pl.BlockSpec((BLOCK, LANES), lambda i: (jnp.maximum(i * nb - 1, 0), cb))
    tile = pl.BlockSpec((tm, 512), lambda i: (i, 0))
    small = pl.BlockSpec((8, LANES), lambda i: (0, 0))
    acc = pl.BlockSpec((T + BLOCK, LANES), lambda i: (0, 0))
    return pl.pallas_call(
        body, name="swa_bwd", grid=(T // tm,),
        in_specs=[pl.BlockSpec((tm, 512), lambda i: (i, CB_SQ)), pl.BlockSpec((tm, LANES), lambda i: (i, CB_SK)),
                  pl.BlockSpec((tm, LANES), lambda i: (i, CB_SV)), prev(CB_SK), prev(CB_SV), tile, tile,
                  pl.BlockSpec((1, LANES), lambda i: (0, 0)), pl.BlockSpec((1, LANES), lambda i: (0, 0)),
                  pl.BlockSpec((nb * HEADS, 2 * BLOCK, BLOCK), lambda i: (0, 0, 0)),
                  pl.BlockSpec(memory_space=pltpu.SMEM), pl.BlockSpec(memory_space=pl.ANY)],
        out_specs=[pl.BlockSpec((tm, 512), lambda i: (i, DPB_SQ)), acc, acc, small, small],
        out_shape=[_sds((T, NP), MXU_DTYPE), _sds((T + BLOCK, LANES), F32), _sds((T + BLOCK, LANES), F32),
                   _sds((8, LANES), F32), _sds((8, LANES), F32)],
        input_output_aliases={11: 0},
        compiler_params=_cp(("arbitrary",), 48))(
            proj, proj, proj, proj, proj, o_swa, do_swa, lw["sqn"], lw["skn"], jnp.tile(_swa_alibi(), (nb, 1, 1)),
            lw["sinks"], dproj)


def _swa_kv_bwd(proj, dkn, dv, lw, dproj):
    T = proj.shape[0]
    tm = min(TM_SWA, T)
    dkn, dv = dkn[BLOCK:], dv[BLOCK:]

    def body(k_ref, dkn_ref, dv_ref, kw_ref, dproj_in, d_ref, dkw_ref):
        i = pl.program_id(0)

        @pl.when(i == 0)
        def _():
            dkw_ref[...] = jnp.zeros_like(dkw_ref)

        half1 = lax.broadcasted_iota(jnp.int32, (1, LANES), 1) >= 64
        khat, kr = _rms_halves(k_ref[...], half1)
        dkn_t = dkn_ref[...]
        dkw = jnp.sum(dkn_t * khat, axis=0, keepdims=True)
        dkw_ref[...] += _row0(dkw + pltpu.roll(dkw, 64, 1))
        d_ref[:, 0:LANES] = _rms_halves_bwd(dkn_t, khat, kr, kw_ref[...], half1).astype(MXU_DTYPE)
        d_ref[:, LANES:2 * LANES] = dv_ref[...].astype(MXU_DTYPE)

    return pl.pallas_call(
        body, name="swa_kv_bwd", grid=(T // tm,),
        in_specs=[pl.BlockSpec((tm, LANES), lambda i: (i, CB_SK)), pl.BlockSpec((tm, LANES), lambda i: (i, 0)),
                  pl.BlockSpec((tm, LANES), lambda i: (i, 0)), pl.BlockSpec((1, LANES), lambda i: (0, 0)),
                  pl.BlockSpec(memory_space=pl.ANY)],
        out_specs=[pl.BlockSpec((tm, 2 * LANES), lambda i: (i, DPB_SKV)), pl.BlockSpec((8, LANES), lambda i: (0, 0))],
        out_shape=[_sds((T, NP), MXU_DTYPE), _sds((8, LANES), F32)],
        input_output_aliases={4: 0},
        compiler_params=_cp(("arbitrary",), 32))(proj, dkn, dv, lw["skn"], dproj)


def _mla_attn_bwd(q, k, kt, vt, o, do, lse):
    T = q.shape[1]
    tk = min(TK, T // 2)
    tq = 2 * tk

    def body(q_ref, k_ref, kt_ref, vt_ref, o_ref, do_ref, lse_ref, dq_ref, dk_ref, dv_ref, dq_s, lse_s, dd_s,
             s_a, s_b, p_a, p_b):
        h = pl.program_id(0)
        i = pl.program_id(1)

        @pl.when(i == 0)
        def _():
            dk_ref[...] = jnp.zeros_like(dk_ref)
            dv_ref[...] = jnp.zeros_like(dv_ref)

        qry = lax.broadcasted_iota(jnp.int32, (tq, tk), 0)
        key = lax.broadcasted_iota(jnp.int32, (tq, tk), 1)
        own = (lax.broadcasted_iota(jnp.int32, (1, LANES), 1) // 64) == (h % 2)
        do_own = jnp.where(own, do_ref[...], 0.0)
        dob = do_own.astype(MXU_DTYPE)
        dob_t = do_own.T.astype(MXU_DTYPE)
        qh = q_ref[0]
        qh_t = qh.astype(F32).T.astype(MXU_DTYPE)
        dd_col = jnp.sum(do_own * o_ref[...], axis=-1, keepdims=True)
        lse_col = jnp.broadcast_to(lse_ref[0], (LANES, tq)).T
        for c in range(tk // LANES):
            lse_s[:, LANES * c:LANES * (c + 1)] = lse_col
            dd_s[:, LANES * c:LANES * (c + 1)] = jnp.broadcast_to(dd_col, (tq, LANES))
        dq_s[...] = jnp.zeros_like(dq_s)

        def scores(kj, s_buf, p_buf):
            s_buf[...] = _dot(qh, kt_ref[0, kj])
            p_buf[...] = _dot(dob, vt_ref[0, kj])

        def consume(kj, s_buf, p_buf, diag):
            rows = pl.ds(pl.multiple_of(kj * tk, tk), tk)
            s = s_buf[...]
            if diag is not None:
                s = jnp.where(key + diag * tk <= qry, s, NEG_INF)
            p = jnp.exp2(s - lse_s[...])
            ds = (p * (p_buf[...] - dd_s[...])).astype(MXU_DTYPE)
            dq_s[...] += _dot(ds, k_ref[0, rows, :])
            dk_ref[0, kj] += _dot(qh_t, ds)
            dv_ref[0, kj] += _dot(dob_t, p.astype(MXU_DTYPE))

        scores(0, s_a, p_a)

        def pair(kj):
            scores(kj + 1, s_b, p_b)
            consume(kj, s_a, p_a, None)
            scores(kj + 2, s_a, p_a)
            consume(kj + 1, s_b, p_b, None)

        def octet(ko, carry):
            for t in range(4):
                pair(8 * ko + 2 * t)
            return carry

        lax.fori_loop(0, i // 4, octet, 0)

        @pl.when(i % 4 >= 2)
        def _():
            pair(8 * (i // 4))
            pair(8 * (i // 4) + 2)

        @pl.when(i % 2 == 1)
        def _():
            pair(2 * i - 2)

        scores(2 * i + 1, s_b, p_b)
        consume(2 * i, s_a, p_a, 0)
        consume(2 * i + 1, s_b, p_b, 1)
        dq_ref[0] = dq_s[...]

    res = pl.BlockSpec((1, T, LANES), lambda h, i: (h, 0, 0))
    res_t = pl.BlockSpec((1, T // tk, LANES, tk), lambda h, i: (h, 0, 0, 0))
    buf = pltpu.VMEM((tq, tk), F32)
    acc_t = _sds((HEADS, T // tk, LANES, tk), F32)
    return pl.pallas_call(
        body, name="mla_attn_bwd", grid=(HEADS, T // tq),
        in_specs=[pl.BlockSpec((1, tq, LANES), lambda h, i: (h, i, 0)), res, res_t, res_t,
                  pl.BlockSpec((tq, LANES), lambda h, i: (i, h // 2)),
                  pl.BlockSpec((tq, LANES), lambda h, i: (i, h // 2)),
                  pl.BlockSpec((1, 1, tq), lambda h, i: (h, 0, i))],
        out_specs=[pl.BlockSpec((1, tq, LANES), lambda h, i: (h, i, 0)), res_t, res_t],
        out_shape=[_sds((HEADS, T, LANES), F32), acc_t, acc_t],
        scratch_shapes=[pltpu.VMEM((tq, LANES), F32), buf, buf, buf, buf, buf, buf],
        compiler_params=_cp(("parallel", "arbitrary"), 48))(q, k, kt, vt, o, do, lse)


def _mla_prep_bwd(proj, dq, dk, dv, lw, rope, dproj):
    T = proj.shape[0]
    tm = min(TK, T // 2)

    def body(ql_ref, kvl_ref, kr_ref, dq_ref, dk_ref, dv_ref, qa_ref, kva_ref, wq_ref, wk_ref, wv_ref,
             wqt_ref, wkt_ref, wvt_ref, qn_ref, kn_ref, c_ref, s1_ref, s2_ref, dproj_in,
             d_ref, dwq_ref, dwk_ref, dwv_ref, dqa_ref, dkva_ref, dqn_ref, dkn_ref):
        i = pl.program_id(0)

        @pl.when(i == 0)
        def _():
            for ref in (dwq_ref, dwk_ref, dwv_ref, dqa_ref, dkva_ref, dqn_ref, dkn_ref):
                ref[...] = jnp.zeros_like(ref)

        c, s1, s2 = c_ref[...], s1_ref[...], s2_ref[...]
        lane = lax.broadcasted_iota(jnp.int32, (1, LANES), 1)
        qlhat, qlr = _rms(ql_ref[...], MLA_Q_LORA)
        qn = (qlhat * qa_ref[...]).astype(MXU_DTYPE)
        kvhat, kvr = _rms(kvl_ref[...], MLA_KV_LORA)
        kvn = (kvhat * kva_ref[...]).astype(MXU_DTYPE)
        kr = kr_ref[...]
        x3, r3 = _rms(jnp.stack([_dot(qn, wq_ref[h]) for h in range(HEADS)]), MLA_QK)
        dy3 = _rope_bwd(dq_ref[...] * MLA_SCALE, c, s1, s2)
        dqw = jnp.sum(jnp.sum(dy3 * x3, axis=0), axis=0, keepdims=True)
        dx3 = _rms_bwd(dy3, x3, r3, qn_ref[...], MLA_QK).astype(MXU_DTYPE)
        dqnl = jnp.zeros((tm, MLA_Q_LORA), F32)
        for h in range(HEADS):
            dwq_ref[h] += _dot_tn(qn, dx3[h])
            dqnl = dqnl + _dot(dx3[h], wqt_ref[h])

        x3, r3 = _rms(jnp.stack([_dot(kvn, wk_ref[h]) for h in range(HEADS)]) + kr, MLA_QK)
        dy3 = _rope_bwd(jnp.stack([dk_ref[h, 0].T for h in range(HEADS)]) * LN2, c, s1, s2)
        dkw = jnp.sum(jnp.sum(dy3 * x3, axis=0), axis=0, keepdims=True)
        dxf3 = _rms_bwd(dy3, x3, r3, kn_ref[...], MLA_QK)
        dkr = jnp.sum(dxf3, axis=0)
        dx3 = dxf3.astype(MXU_DTYPE)
        dkvn = jnp.zeros((tm, MLA_KV_LORA), F32)
        for h in range(HEADS):
            dwk_ref[h] += _dot_tn(kvn, dx3[h])
            dkvn = dkvn + _dot(dx3[h], wkt_ref[h])
        dvc = jnp.concatenate([(dv_ref[2 * j, 0] + dv_ref[2 * j + 1, 0]).T for j in range(4)],
                              axis=1).astype(MXU_DTYPE)
        dwv_ref[...] += _dot_tn(kvn, dvc)
        dkvn = dkvn + _dot(dvc, wvt_ref[...])
        dqa_ref[...] += _row0(jnp.sum(dqnl * qlhat, axis=0, keepdims=True))
        dkva_ref[...] += _row0(jnp.sum(dkvn * kvhat, axis=0, keepdims=True))
        dqn_ref[...] += _row0(dqw)
        dkn_ref[...] += _row0(dkw)
        d_ref[:, 0:256] = _rms_bwd(dqnl, qlhat, qlr, qa_ref[...], MLA_Q_LORA).astype(MXU_DTYPE)
        d_ref[:, 256:384] = _rms_bwd(dkvn, kvhat, kvr, kva_ref[...], MLA_KV_LORA).astype(MXU_DTYPE)
        d_ref[:, 384:512] = jnp.where((lane >= 64) & (lane < 96), dkr, 0.0).astype(MXU_DTYPE)

    full = lambda shape: pl.BlockSpec(shape, lambda i: (0,) * len(shape))
    hd = pl.BlockSpec((HEADS, tm, LANES), lambda i: (0, i, 0))
    hdt = pl.BlockSpec((HEADS, 1, LANES, tm), lambda i: (0, i, 0, 0))
    tab = pl.BlockSpec((tm, LANES), lambda i: (i, 0))
    return pl.pallas_call(
        body, name="mla_prep_bwd", grid=(T // tm,),
        in_specs=[pl.BlockSpec((tm, 256), lambda i: (i, CB_QLAT)), pl.BlockSpec((tm, LANES), lambda i: (i, CB_KVLAT)),
                  pl.BlockSpec((tm, LANES), lambda i: (i, CB_KROPE)), hd, hdt, hdt,
                  full((1, 256)), full((1, LANES)), full((HEADS, 256, LANES)), full((HEADS, LANES, LANES)),
                  full((LANES, 512)), full((HEADS, LANES, 256)), full((HEADS, LANES, LANES)), full((512, LANES)),
                  full((1, LANES)), full((1, LANES)), tab, tab, tab, pl.BlockSpec(memory_space=pl.ANY)],
        out_specs=[pl.BlockSpec((tm, 512), lambda i: (i, DPB_MLA)), full((HEADS, 256, LANES)),
                   full((HEADS, LANES, LANES)), full((LANES, 512)), full((8, 256)), full((8, LANES)),
                   full((8, LANES)), full((8, LANES))],
        out_shape=[_sds((T, NP), MXU_DTYPE), _sds((HEADS, 256, LANES), F32), _sds((HEADS, LANES, LANES), F32),
                   _sds((LANES, 512), F32), _sds((8, 256), F32), _sds((8, LANES), F32), _sds((8, LANES), F32),
                   _sds((8, LANES), F32)],
        input_output_aliases={19: 0},
        compiler_params=_cp(("arbitrary",), 48))(
            proj, proj, proj, dq, dk, dv, lw["qa"], lw["kva"], lw["wq"], lw["wk"], lw["wv"],
            lw["wqt"], lw["wkt"], lw["wvt"], lw["qn"], lw["kn"], rope[0], rope[1], rope[2], dproj)


def _inproj_bwd_dx(dproj, wpt, x, g_in, ng):
    T, D = x.shape
    tm = min(TM_PROJ, T)

    def body(dp_ref, wt_ref, x_ref, g_ref, w_ref, dx_ref, dw_ref):
        i = pl.program_id(0)

        @pl.when(i == 0)
        def _():
            dw_ref[...] = jnp.zeros_like(dw_ref)

        dh = _dot(dp_ref[...], wt_ref[...])
        xhat, r = _rms(x_ref[...], D)
        dw_ref[...] += _row0(jnp.sum(dh * xhat, axis=0, keepdims=True))
        dx_ref[...] = g_ref[...] + _rms_bwd(dh, xhat, r, w_ref[...], D)

    tile = pl.BlockSpec((tm, D), lambda i: (i, 0))
    return pl.pallas_call(
        body, name="inproj_bwd_dx", grid=(T // tm,),
        in_specs=[pl.BlockSpec((tm, NP), lambda i: (i, 0)), pl.BlockSpec((NP, D), lambda i: (0, 0)), tile, tile,
                  pl.BlockSpec((1, D), lambda i: (0, 0))],
        out_specs=[tile, pl.BlockSpec((8, D), lambda i: (0, 0))],
        out_shape=[_sds((T, D), F32), _sds((8, D), F32)],
        compiler_params=_cp(("arbitrary",), 48))(dproj, wpt, x, g_in, ng)


def _rope_tables(T, token=0.0):
    half = MLA_ROPE // 2
    inv_freq = jnp.power(jnp.float32(ROPE_THETA), -jnp.arange(half, dtype=F32) / half)
    z = lambda n: jnp.zeros((n,), F32)
    freq = jnp.concatenate([z(MLA_NOPE), inv_freq, inv_freq, z(32)])
    first = jnp.concatenate([z(64), jnp.ones((16,), F32), z(48)])
    second = jnp.concatenate([z(80), jnp.ones((16,), F32), z(32)])
    ang = (jnp.arange(T, dtype=F32) + token)[:, None] * freq[None, :]
    sin = jnp.sin(ang)
    return jnp.cos(ang), -sin * first[None, :], sin * second[None, :]


def _pad_lanes(v, n=LANES):
    v = v.reshape(1, -1)
    return jnp.pad(v, ((0, 0), (0, n - v.shape[1])))


def _pack_win_t(wt):
    z = lambda n: jnp.zeros((n, wt.shape[1]), wt.dtype)
    return jnp.concatenate([wt[416:2976], wt[3744:4256], wt[0:384], z(64), wt[384:416], z(32), wt[2976:3488],
                            wt[3488:3616], wt[3616:3744]], axis=0)


def _unpack_dwin(d):
    return jnp.concatenate([d[:, 3072:3456], d[:, 3520:3552], d[:, 0:2560], d[:, 3584:4096], d[:, 4096:4224],
                            d[:, 4224:4352], d[:, 2560:3072]], axis=1)


def _inproj_weights(l, norm_g, w_in_t):
    wpt = _pack_win_t(w_in_t)
    return dict(ng=norm_g[l].reshape(1, -1), wp=wpt.T, wpt=wpt)


def _mixer_weights(l, qa, wqb_full, kva, wkvb_full, qn, kn, conv_full, sqn, skn, sinks, w_out_full):
    wq = jnp.pad(wqb_full, ((0, 0), (0, 0), (0, LANES - MLA_QK)))
    wk = jnp.pad(wkvb_full[:, :, :MLA_NOPE], ((0, 0), (0, 0), (0, LANES - MLA_NOPE)))
    wv = jnp.transpose(wkvb_full[:, :, MLA_NOPE:], (1, 0, 2)).reshape(MLA_KV_LORA, GROUP_WIDTH)
    return dict(
        qa=qa[l].reshape(1, -1), kva=kva[l].reshape(1, -1),
        wq=wq, wk=wk, wv=wv, wqt=jnp.transpose(wq, (0, 2, 1)), wkt=jnp.transpose(wk, (0, 2, 1)), wvt=wv.T,
        qn=_pad_lanes(qn[l]), kn=_pad_lanes(kn[l]),
        conv=jnp.pad(conv_full, ((0, 5), (0, 0))),
        sqn=jnp.tile(sqn[l].reshape(1, -1), (1, 2)), skn=jnp.tile(skn[l].reshape(1, -1), (1, 2)),
        sinks=sinks[l], wo=w_out_full, wot=w_out_full.T)


def _layer_weights(l, norm_g, w_in_full, qa, wqb_full, kva, wkvb_full, qn, kn, conv_full, sqn, skn, sinks,
                   w_out_full):
    return dict(_inproj_weights(l, norm_g, w_in_full.T),
                **_mixer_weights(l, qa, wqb_full, kva, wkvb_full, qn, kn, conv_full, sqn, skn, sinks, w_out_full))


def _layer_fwd(x, lw, rope, late_weights=None, target=None):
    proj, h = _inproj_fwd(x, lw["ng"], lw["wp"])
    if late_weights is not None:
        lw = dict(lw, **late_weights(proj))
    q, k, kt, vt = _mla_prep_fwd(proj, lw, rope)
    o_mla, lse = _mla_attn_fwd(q, k, vt)
    o_swa = _swa_fwd(proj, lw)
    ycat = _mix_fwd(proj, o_mla, o_swa, lw["conv"])
    if target is None:
        out = _mm_nn(ycat, lw["wo"], "outproj_fwd", residual=x)
    else:
        out = _outproj_loss(ycat, lw["wo"], x, target)
    return out, dict(x=x, proj=proj, h=h, q=q, k=k, kt=kt, vt=vt, o_mla=o_mla, lse=lse, o_swa=o_swa, ycat=ycat,
                     lw=lw)


def _layer_bwd(g, sv, lw, rope, on_big_grads=None):
    proj = sv["proj"]
    dycat, d_wo = _outproj_bwd(g, sv["ycat"], lw["wot"])
    dproj, do_mla, do_swa, d_conv = _mix_bwd(dycat, proj, sv["o_mla"], sv["o_swa"], lw["conv"])
    dproj, dkn_acc, dv_acc, d_sqn, d_sinks = _swa_bwd(proj, sv["o_swa"], do_swa, lw, dproj)
    dproj, d_skn = _swa_kv_bwd(proj, dkn_acc, dv_acc, lw, dproj)
    dq, dk, dv = _mla_attn_bwd(sv["q"], sv["k"], sv["kt"], sv["vt"], sv["o_mla"], do_mla, sv["lse"])
    dproj, d_wq, d_wk, d_wv, d_qa, d_kva, d_qn, d_kn = _mla_prep_bwd(proj, dq, dk, dv, lw, rope, dproj)
    grads = dict(
        w_out=d_wo, w_qb=d_wq[:, :, :MLA_QK],
        w_kvb=jnp.concatenate([d_wk[:, :, :MLA_NOPE],
                               jnp.transpose(d_wv.reshape(MLA_KV_LORA, HEADS, MLA_NOPE), (1, 0, 2))], axis=2))
    token = 0.0 if on_big_grads is None else on_big_grads("mixer", grads)
    d_wp = _mm_tn(sv["h"], dproj, "inproj_bwd_dw", WIRE_DTYPE, tn=NP // 2)
    grads["w_in"] = _unpack_dwin(d_wp)
    token = token if on_big_grads is None else token + on_big_grads("w_in", grads)
    dx, d_ng = _inproj_bwd_dx(dproj, lw["wpt"], sv["x"], g, lw["ng"] + token)
    grads.update(
        conv=d_conv[0:3], norm_g=d_ng[0], qa=d_qa[0], kva=d_kva[0], qn=d_qn[0, :MLA_QK], kn=d_kn[0, :MLA_QK],
        sqn=d_sqn[0, :SWA_HEAD_DIM], skn=d_skn[0, :SWA_HEAD_DIM], sinks=d_sinks[:, 0])
    return dx, grads


def _local_step(x, target, lws, rope):
    saved = []
    for l, lw in enumerate(lws):
        x, sv = _layer_fwd(x, lw, rope, target=target if l == len(lws) - 1 else None)
        saved.append(sv)
    g, loss_tile = x
    grads = [None] * len(lws)
    for l in reversed(range(len(lws))):
        g, grads[l] = _layer_bwd(g, saved[l], lws[l], rope)
    return loss_tile, g, grads


def _my_coords():
    return lax.axis_index("x"), lax.axis_index("y"), lax.axis_index("c")


def _peer(me, k):
    x, y, c = me
    return (1 - x if k & 4 else x, 1 - y if k & 2 else y, 1 - c if k & 1 else c)


def _lin(d):
    return 4 * d[0] + 2 * d[1] + d[2]


def _push_copies(ins, lands, send_sems, recv_sems, gather):
    me = _my_coords()
    my = _lin(me)
    out, inc = [], []
    for a in range(len(ins)):
        for k in range(1, N_DEV):
            peer = _peer(me, k)
            sems = dict(send_sem=send_sems.at[a * 7 + k - 1], recv_sem=recv_sems.at[a * 7 + k - 1],
                        device_id=peer, device_id_type=pl.DeviceIdType.MESH)
            src = ins[a] if gather else ins[a].at[_lin(peer)]
            out.append(pltpu.make_async_remote_copy(src_ref=src, dst_ref=lands[a].at[my], **sems))
            inc.append(pltpu.make_async_remote_copy(src_ref=src, dst_ref=lands[a].at[_lin(peer)], **sems))
    return out, inc


def _push_start(arrays, name, gather):
    n = len(arrays)
    land_shapes = [((N_DEV,) + a.shape) if gather else a.shape for a in arrays]

    def body(*refs):
        ins, lands = refs[:n], refs[n:2 * n]
        send_sems, recv_sems = refs[2 * n], refs[2 * n + 1]
        token = refs[-1]
        out, _ = _push_copies(ins, lands, send_sems, recv_sems, gather)
        for cp in out:
            cp.start()
        token[...] = jnp.zeros_like(token)

    hbm = pl.BlockSpec(memory_space=pltpu.HBM)
    sem = pl.BlockSpec(memory_space=pltpu.SEMAPHORE)
    res = pl.pallas_call(
        body, name=name,
        out_shape=(pltpu.SemaphoreType.DMA((7 * n,)), pltpu.SemaphoreType.DMA((7 * n,)),
                   *[pltpu.HBM(a.shape, a.dtype) for a in arrays],
                   *[pltpu.HBM(s, a.dtype) for s, a in zip(land_shapes, arrays)],
                   _sds((8, LANES), F32)),
        in_specs=(hbm,) * (2 * n),
        out_specs=(sem, sem) + (hbm,) * (2 * n) + (pl.BlockSpec(memory_space=pltpu.VMEM),),
        input_output_aliases={i: 2 + i for i in range(2 * n)},
        compiler_params=pltpu.CompilerParams(has_side_effects=pltpu.SideEffectType.DATAFLOW_SIDE_EFFECTING),
    )(*[pltpu.with_memory_space_constraint(a, pltpu.HBM) for a in arrays],
      *[pltpu.with_memory_space_constraint(lax.empty(s, a.dtype), pltpu.HBM) for s, a in zip(land_shapes, arrays)])
    return dict(send=res[0], recv=res[1], src=res[2:2 + n], land=res[2 + n:2 + 2 * n], token=res[-1][0, 0],
                gather=gather)


def _push_wait(handle, after, name):
    n = len(handle["src"])
    gather = handle["gather"]

    def body(*refs):
        ins, lands = refs[:n], refs[n:2 * n]
        send_sems, recv_sems = refs[2 * n], refs[2 * n + 1]
        out, inc = _push_copies(ins, lands, send_sems, recv_sems, gather)
        for cp in out:
            cp.wait_send()
        for cp in inc:
            cp.wait_recv()

    hbm = pl.BlockSpec(memory_space=pltpu.HBM)
    sem = pl.BlockSpec(memory_space=pltpu.SEMAPHORE)
    res = pl.pallas_call(
        body, name=name,
        out_shape=tuple(pltpu.HBM(a.shape, a.dtype) for a in (*handle["src"], *handle["land"])),
        in_specs=(hbm,) * (2 * n) + (sem, sem, pl.BlockSpec(memory_space=pl.ANY)),
        out_specs=(hbm,) * (2 * n),
        input_output_aliases={i: i for i in range(2 * n)},
        compiler_params=pltpu.CompilerParams(has_side_effects=pltpu.SideEffectType.DATAFLOW_SIDE_EFFECTING),
    )(*handle["src"], *handle["land"], handle["send"], handle["recv"], after)
    return res[n:]


def _small_all_reduce(v):
    R = v.shape[0]

    def body(v_ref, o_ref, buf, send_sems, recv_sems):
        me = _my_coords()
        my = _lin(me)
        sends = []
        for k in range(1, N_DEV):
            cp = pltpu.make_async_remote_copy(
                src_ref=v_ref, dst_ref=buf.at[my], send_sem=send_sems.at[k - 1], recv_sem=recv_sems.at[k - 1],
                device_id=_peer(me, k), device_id_type=pl.DeviceIdType.MESH)
            cp.start()
            sends.append(cp)
        buf[my] = v_ref[...]
        for k in range(1, N_DEV):
            pltpu.make_async_remote_copy(
                src_ref=v_ref, dst_ref=buf.at[_lin(_peer(me, k))], send_sem=send_sems.at[k - 1],
                recv_sem=recv_sems.at[k - 1], device_id=_peer(me, k),
                device_id_type=pl.DeviceIdType.MESH).wait_recv()
        for cp in sends:
            cp.wait_send()
        tot = buf[0]
        for d in range(1, N_DEV):
            tot = tot + buf[d]
        o_ref[...] = tot

    vm = pl.BlockSpec(memory_space=pltpu.VMEM)
    return pl.pallas_call(
        body, name="small_all_reduce", in_specs=[vm], out_specs=vm, out_shape=_sds(v.shape, F32),
        scratch_shapes=[pltpu.VMEM((N_DEV, R, LANES), F32), pltpu.SemaphoreType.DMA((7,)),
                        pltpu.SemaphoreType.DMA((7,))],
    )(v)


def _adamw_math(w, g, m, v):
    m = ADAM_B1 * m + (1.0 - ADAM_B1) * g
    v = ADAM_B2 * v + (1.0 - ADAM_B2) * (g * g)
    m_hat = m / (1.0 - ADAM_B1 ** ADAM_STEP)
    v_hat = v / (1.0 - ADAM_B2 ** ADAM_STEP)
    delta = -ADAM_LR * (m_hat / (jnp.sqrt(v_hat) + ADAM_EPS) + ADAM_WD * w)
    return delta, m, v


def _adamw(parts, w, m, v, name, tr):
    P, R, C = parts.shape
    tr = min(tr, R)

    def body(p_ref, w_ref, m_ref, v_ref, g_out, d_out, m_out, v_out):
        g = p_ref[0].astype(F32)
        for d in range(1, P):
            g = g + p_ref[d].astype(F32)
        delta, m_new, v_new = _adamw_math(w_ref[...], g, m_ref[...], v_ref[...])
        g_out[...] = g
        d_out[...] = delta
        m_out[...] = m_new
        v_out[...] = v_new

    tile = pl.BlockSpec((tr, C), lambda i: (i, 0))
    return pl.pallas_call(
        body, name=name, grid=(R // tr,),
        in_specs=[pl.BlockSpec((P, tr, C), lambda i: (0, i, 0)), tile, tile, tile],
        out_specs=[tile] * 4, out_shape=[_sds((R, C), F32)] * 4,
        compiler_params=_cp(("parallel",), 32))(parts, w, m, v)


SMALL = (("norm_g", D_MODEL), ("mla_q_a_norm", MLA_Q_LORA), ("mla_kv_a_norm", MLA_KV_LORA), ("mla_q_norm", MLA_QK),
         ("mla_k_norm", MLA_QK), ("swa_q_norm", SWA_HEAD_DIM), ("swa_k_norm", SWA_HEAD_DIM), ("swa_sinks", HEADS))
SMALL_GRAD_KEY = dict(norm_g="norm_g", mla_q_a_norm="qa", mla_kv_a_norm="kva", mla_q_norm="qn", mla_k_norm="kn",
                      swa_q_norm="sqn", swa_k_norm="skn", swa_sinks="sinks")
SMALL_ROWS = 32
CONV_ROWS = 24


def _pack_small(get):
    parts = []
    for l in range(DEPTH):
        for name, n in SMALL:
            v = get(name, l).reshape(-1)
            parts.append(jnp.pad(v, (0, (-n) % LANES)))
    return jnp.concatenate(parts).reshape(SMALL_ROWS, LANES)


def _unpack_small(packed):
    flat = packed.reshape(-1)
    out = {name: [] for name, _ in SMALL}
    off = 0
    for l in range(DEPTH):
        for name, n in SMALL:
            out[name].append(flat[off:off + n])
            off += n + (-n) % LANES
    return {name: jnp.stack(v) for name, v in out.items()}


def kernel(x, norm_g, w_in, mla_q_a_norm, mla_w_qb, mla_kv_a_norm, mla_w_kvb, mla_q_norm, mla_k_norm, conv_w, swa_q_norm, swa_k_norm, swa_sinks, w_out, loss_target, m_norm_g, m_w_in, m_mla_q_a_norm, m_mla_w_qb, m_mla_kv_a_norm, m_mla_w_kvb, m_mla_q_norm, m_mla_k_norm, m_conv_w, m_swa_q_norm, m_swa_k_norm, m_swa_sinks, m_w_out, v_norm_g, v_w_in, v_mla_q_a_norm, v_mla_w_qb, v_mla_kv_a_norm, v_mla_w_kvb, v_mla_q_norm, v_mla_k_norm, v_conv_w, v_swa_q_norm, v_swa_k_norm, v_swa_sinks, v_w_out):
    T = x.shape[1]
    weights = dict(norm_g=norm_g, w_in=w_in, mla_q_a_norm=mla_q_a_norm, mla_w_qb=mla_w_qb,
                   mla_kv_a_norm=mla_kv_a_norm, mla_w_kvb=mla_w_kvb, mla_q_norm=mla_q_norm, mla_k_norm=mla_k_norm,
                   conv_w=conv_w, swa_q_norm=swa_q_norm, swa_k_norm=swa_k_norm, swa_sinks=swa_sinks, w_out=w_out)
    mom_m = dict(norm_g=m_norm_g, w_in=m_w_in, mla_q_a_norm=m_mla_q_a_norm, mla_w_qb=m_mla_w_qb,
                 mla_kv_a_norm=m_mla_kv_a_norm, mla_w_kvb=m_mla_w_kvb, mla_q_norm=m_mla_q_norm,
                 mla_k_norm=m_mla_k_norm, conv_w=m_conv_w, swa_q_norm=m_swa_q_norm, swa_k_norm=m_swa_k_norm,
                 swa_sinks=m_swa_sinks, w_out=m_w_out)
    mom_v = dict(norm_g=v_norm_g, w_in=v_w_in, mla_q_a_norm=v_mla_q_a_norm, mla_w_qb=v_mla_w_qb,
                 mla_kv_a_norm=v_mla_kv_a_norm, mla_w_kvb=v_mla_w_kvb, mla_q_norm=v_mla_q_norm,
                 mla_k_norm=v_mla_k_norm, conv_w=v_conv_w, swa_q_norm=v_swa_q_norm, swa_k_norm=v_swa_k_norm,
                 swa_sinks=v_swa_sinks, w_out=v_w_out)

    my = _lin(_my_coords())

    def shards(l):
        return [w_in[l].astype(MXU_DTYPE).T, mla_w_qb[l].astype(MXU_DTYPE), mla_w_kvb[l].astype(MXU_DTYPE),
                w_out[l].astype(MXU_DTYPE), conv_w[l]]

    def inproj_weights(l, g_win_t):
        return _inproj_weights(l, norm_g, g_win_t.reshape(IN_COLS, D_MODEL))

    def mixer_weights(l, gathered):
        g_wqb, g_wkvb, g_wout, g_conv = gathered
        return _mixer_weights(
            l, mla_q_a_norm, g_wqb, mla_kv_a_norm, g_wkvb, mla_q_norm, mla_k_norm,
            jnp.transpose(g_conv, (1, 0, 2)).reshape(3, GROUP_WIDTH), swa_q_norm, swa_k_norm, swa_sinks,
            g_wout.reshape(D_MIX, D_MODEL))

    slot_of = dict(
        w_in=lambda g: jnp.transpose(g["w_in"].reshape(D_MODEL, N_DEV, IN_COLS // N_DEV), (1, 0, 2)),
        w_out=lambda g: g["w_out"].reshape(N_DEV, D_MIX // N_DEV, D_MODEL),
        w_qb=lambda g: g["w_qb"], w_kvb=lambda g: g["w_kvb"])

    def own_slot(landed, mine):
        return [lax.dynamic_update_index_in_dim(a, m, my, 0) for a, m in zip(landed, mine)]

    def landed(handle, after, name, mine):
        return own_slot(_push_wait(handle, after, name), mine)

    gather_in0 = _push_start(shards(0)[:1], "weight_gather_in0_start", gather=True)
    rope = _rope_tables(T, gather_in0["token"])
    w_in0_t = landed(gather_in0, rope[0], "weight_gather_in0_wait", shards(0)[:1])[0]
    w_in0_t, conv_late = lax.optimization_barrier((w_in0_t, conv_w))
    late = lambda l: shards(l)[:4] + [conv_late[l]]
    gather0 = _push_start(late(0)[1:], "weight_gather0_start", gather=True)
    gather1 = _push_start(late(1), "weight_gather1_start", gather=True)
    lw0 = inproj_weights(0, w_in0_t)
    lw0 = dict(lw0, ng=lw0["ng"] + (gather0["token"] + gather1["token"]))
    x1, sv0 = _layer_fwd(
        x[0], lw0, rope,
        late_weights=lambda proj: mixer_weights(0, landed(gather0, proj, "weight_gather0_wait", shards(0)[1:])))
    g1_all = landed(gather1, x1, "weight_gather1_wait", shards(1))
    (g2, loss_tile), sv1 = _layer_fwd(x1, dict(inproj_weights(1, g1_all[0]), **mixer_weights(1, g1_all[1:])), rope,
                                      target=loss_target[0])

    parts = {(1, "w_in"): ("w_in", "w_out", "w_qb", "w_kvb"), (0, "mixer"): ("w_out", "w_qb", "w_kvb"),
             (0, "w_in"): ("w_in",)}
    started = []

    def start_exchange(l, part, g):
        if (l, part) not in parts:
            return 0.0
        sl = [slot_of[n](g) for n in parts[(l, part)]]
        handle = _push_start(sl, "grad_exchange%d_%s_start" % (l, part), gather=False)
        started.append((l, part, sl, handle))
        return handle["token"]

    g1, grads1 = _layer_bwd(g2, sv1, sv1["lw"], rope, on_big_grads=functools.partial(start_exchange, 1))
    lw0b = dict(sv0["lw"], conv=sv0["lw"]["conv"] + started[0][3]["token"])
    grad_x, grads0 = _layer_bwd(g1, sv0, lw0b, rope, on_big_grads=functools.partial(start_exchange, 0))
    recv = {}

    def receive(l, part, sl, handle, after):
        got = landed(handle, after, "grad_exchange%d_%s_wait" % (l, part), [s[my] for s in sl])
        recv.update({(l, n): a for n, a in zip(parts[(l, part)], got)})

    for entry in started[:-1]:
        receive(*entry, after=grad_x)
    grads = [grads0, grads1]
    stacked = lambda n: jnp.stack([recv[(0, n)], recv[(1, n)]], axis=1)

    small = jnp.concatenate([
        _pack_small(lambda name, l: grads[l][SMALL_GRAD_KEY[name]]),
        jnp.stack([g["conv"] for g in grads]).reshape(CONV_ROWS, LANES),
        loss_tile], axis=0)
    small = _small_all_reduce(small)
    loss = small[SMALL_ROWS + CONV_ROWS, 0]
    my = _lin(_my_coords())
    conv_g = lax.dynamic_slice_in_dim(small[SMALL_ROWS:SMALL_ROWS + CONV_ROWS].reshape(DEPTH, 3, GROUP_WIDTH),
                                      my * 64, 64, axis=2)

    out = {}

    def big(name, recv, rows, cols, tr):
        res = _adamw(recv.reshape(N_DEV, rows, cols), weights[name].reshape(rows, cols),
                     mom_m[name].reshape(rows, cols), mom_v[name].reshape(rows, cols), "adamw_" + name, tr)
        out[name] = [r.reshape(weights[name].shape) for r in res]

    big("w_out", stacked("w_out"), DEPTH * D_MIX // N_DEV, D_MODEL, 192)
    big("mla_w_qb", stacked("w_qb"), DEPTH * MLA_Q_LORA, MLA_QK, 512)
    big("mla_w_kvb", stacked("w_kvb"), DEPTH * MLA_KV_LORA, 128, 256)
    receive(*started[-1], after=out["w_out"][1])
    big("w_in", stacked("w_in"), DEPTH * D_MODEL, IN_COLS // N_DEV, 256)

    pad_conv = lambda a: jnp.pad(a.reshape(-1), (0, 8 * LANES - 6 * 64)).reshape(8, LANES)
    cat = lambda src: jnp.concatenate([_pack_small(lambda name, l: src[name][l]), pad_conv(src["conv_w"])], axis=0)
    g_small = jnp.concatenate([small[:SMALL_ROWS], pad_conv(conv_g)], axis=0)
    res = _adamw(g_small[None], cat(weights), cat(mom_m), cat(mom_v), "adamw_small", SMALL_ROWS + 8)
    smalls = [_unpack_small(r[:SMALL_ROWS]) for r in res]
    for name, _ in SMALL:
        out[name] = [s[name] for s in smalls]
    out["conv_w"] = [r[SMALL_ROWS:].reshape(-1)[:6 * 64].reshape(DEPTH, 3, 64) for r in res]

    order = ["norm_g", "w_in", "mla_q_a_norm", "mla_w_qb", "mla_kv_a_norm", "mla_w_kvb", "mla_q_norm", "mla_k_norm",
             "conv_w", "swa_q_norm", "swa_k_norm", "swa_sinks", "w_out"]
    result = [loss, grad_x[None]]
    for idx in range(4):
        result += [out[name][idx] for name in order]
    return tuple(result)
```

```python
import functools

import jax
import jax.numpy as jnp
import numpy as np
from jax import lax
from jax.experimental import pallas as pl
from jax.experimental.pallas import tpu as pltpu

F32 = jnp.float32
MXU_DTYPE = jnp.bfloat16
WIRE_DTYPE = jnp.bfloat16

N_DEV = 8
DEPTH = 2
D_MODEL = 1024
GROUP_WIDTH = 512
D_MIX = 3 * GROUP_WIDTH
BLOCK = 128
RMS_EPS = 1e-6
NEG_INF = -1e30
HEADS = 8
MLA_QK = 96
MLA_NOPE = 64
MLA_ROPE = 32
MLA_Q_LORA = 256
MLA_KV_LORA = 128
ROPE_THETA = 10000.0
SWA_HEAD_DIM = 64
LANES = 128
IN_COLS = 4256

ADAM_LR = 0.001
ADAM_B1 = 0.9
ADAM_B2 = 0.999
ADAM_EPS = 1e-08
ADAM_WD = 0.01
ADAM_STEP = 10

NP = 4352
CB_GMLA, CB_CH, CB_CB, CB_CC, CB_GCONV, CB_GSWA, CB_SQ = 0, 1, 2, 3, 4, 5, 7
CB_QLAT = 12
CB_KVLAT, CB_KROPE = 26, 27
CB_SK, CB_SV = 32, 33
DPB_MIX, DPB_MLA, DPB_SQ, DPB_SKV = 0, 6, 7, 16

TM_PROJ = 512
TM_ROW = 256
TK = 256
TQ = 2 * TK
MLA_SCALE = MLA_QK ** -0.5
MLA_ONES_ROW = (64, 0)
LOG2E = 1.4426950408889634
LN2 = 0.6931471805599453
TM_SWA = 512
VMEM_MB = 2 ** 20


def _cp(sem, vmem_mb):
    return pltpu.CompilerParams(dimension_semantics=sem, vmem_limit_bytes=vmem_mb * VMEM_MB)


def _sds(shape, dtype):
    return jax.ShapeDtypeStruct(shape, dtype)


def _dot(a, b):
    return jnp.dot(a, b, preferred_element_type=F32)


def _dot_nt(a, b):
    return lax.dot_general(a, b, (((1,), (1,)), ((), ())), preferred_element_type=F32)


def _dot_tn(a, b):
    return lax.dot_general(a, b, (((0,), (0,)), ((), ())), preferred_element_type=F32)


def _rms(x, n):
    r = lax.rsqrt(jnp.sum(x * x, axis=-1, keepdims=True) * (1.0 / n) + RMS_EPS)
    return x * r, r


def _rms_bwd(dy, xhat, r, w, n):
    g = dy * w
    return r * (g - xhat * (jnp.sum(g * xhat, axis=-1, keepdims=True) * (1.0 / n)))


def _rms_halves(x, half1):
    x2 = x * x
    s0 = jnp.sum(jnp.where(half1, 0.0, x2), axis=-1, keepdims=True)
    s1 = jnp.sum(jnp.where(half1, x2, 0.0), axis=-1, keepdims=True)
    r = jnp.where(half1, lax.rsqrt(s1 * (1.0 / 64) + RMS_EPS), lax.rsqrt(s0 * (1.0 / 64) + RMS_EPS))
    return x * r, r


def _rms_halves_bwd(dy, xhat, r, w, half1):
    g = dy * w
    t = g * xhat
    m0 = jnp.sum(jnp.where(half1, 0.0, t), axis=-1, keepdims=True) * (1.0 / 64)
    m1 = jnp.sum(jnp.where(half1, t, 0.0), axis=-1, keepdims=True) * (1.0 / 64)
    return r * (g - xhat * jnp.where(half1, m1, m0))


def _sigmoid(x):
    return 1.0 / (1.0 + jnp.exp(-x))


def _rope(x, c, s1, s2):
    ax = x.ndim - 1
    return x * c + pltpu.roll(x, 112, ax) * s1 + pltpu.roll(x, 16, ax) * s2


def _rope_bwd(dy, c, s1, s2):
    ax = dy.ndim - 1
    return dy * c + pltpu.roll(dy * s1, 16, ax) + pltpu.roll(dy * s2, 112, ax)


def _fold_rows8(x):
    return jnp.sum(x.reshape(x.shape[0] // 8, 8, x.shape[1]), axis=0)


def _row0(v, rows=8):
    row = lax.broadcasted_iota(jnp.int32, (rows, v.shape[1]), 0)
    return jnp.where(row == 0, jnp.broadcast_to(v, (rows, v.shape[1])), 0.0)


def _mm_nn(a, b, name, out_dtype=F32, residual=None, tm=TM_PROJ):
    M, K = a.shape
    N = b.shape[1]
    tm = min(tm, M)

    def body(*refs):
        if residual is None:
            a_ref, b_ref, o_ref = refs
            acc = _dot(a_ref[...].astype(MXU_DTYPE), b_ref[...])
        else:
            a_ref, b_ref, r_ref, o_ref = refs
            acc = _dot(a_ref[...].astype(MXU_DTYPE), b_ref[...]) + r_ref[...]
        o_ref[...] = acc.astype(out_dtype)

    in_specs = [pl.BlockSpec((tm, K), lambda i: (i, 0)), pl.BlockSpec((K, N), lambda i: (0, 0))]
    args = [a, b]
    if residual is not None:
        in_specs.append(pl.BlockSpec((tm, N), lambda i: (i, 0)))
        args.append(residual)
    return pl.pallas_call(
        body, name=name, grid=(M // tm,), in_specs=in_specs,
        out_specs=pl.BlockSpec((tm, N), lambda i: (i, 0)), out_shape=_sds((M, N), out_dtype),
        compiler_params=_cp(("parallel",), 48))(*args)


def _mm_tn(a, b, name, out_dtype, tn, tk=512):
    T, M = a.shape
    N = b.shape[1]
    tk = min(tk, T)
    nk = T // tk

    def body(a_ref, b_ref, o_ref, acc_ref):
        k = pl.program_id(1)

        @pl.when(k == 0)
        def _():
            acc_ref[...] = jnp.zeros_like(acc_ref)

        acc_ref[...] += _dot_tn(a_ref[...].astype(MXU_DTYPE), b_ref[...].astype(MXU_DTYPE))

        @pl.when(k == nk - 1)
        def _():
            o_ref[...] = acc_ref[...].astype(out_dtype)

    return pl.pallas_call(
        body, name=name, grid=(N // tn, nk),
        in_specs=[pl.BlockSpec((tk, M), lambda n, k: (k, 0)), pl.BlockSpec((tk, tn), lambda n, k: (k, n))],
        out_specs=pl.BlockSpec((M, tn), lambda n, k: (0, n)), out_shape=_sds((M, N), out_dtype),
        scratch_shapes=[pltpu.VMEM((M, tn), F32)],
        compiler_params=_cp(("parallel", "arbitrary"), 48))(a, b)


def _inproj_fwd(x, ng, wp):
    T, D = x.shape
    tm = min(TM_PROJ, T)

    def body(x_ref, g_ref, w_ref, proj_ref, h_ref):
        xhat, _ = _rms(x_ref[...], D)
        h = (xhat * g_ref[...]).astype(MXU_DTYPE)
        h_ref[...] = h
        proj_ref[...] = _dot(h, w_ref[...])

    return pl.pallas_call(
        body, name="inproj_fwd", grid=(T // tm,),
        in_specs=[pl.BlockSpec((tm, D), lambda i: (i, 0)), pl.BlockSpec((1, D), lambda i: (0, 0)),
                  pl.BlockSpec((D, NP), lambda i: (0, 0))],
        out_specs=[pl.BlockSpec((tm, NP), lambda i: (i, 0)), pl.BlockSpec((tm, D), lambda i: (i, 0))],
        out_shape=[_sds((T, NP), F32), _sds((T, D), MXU_DTYPE)],
        compiler_params=_cp(("parallel",), 48))(x, ng, wp)


def _mla_prep_fwd(proj, lw, rope):
    T = proj.shape[0]
    tk = min(TK, T // 2)
    nsub = 2
    tm = nsub * tk

    def body(ql_ref, kvl_ref, kr_ref, qa_ref, kva_ref, wq_ref, wk_ref, wv_ref, qn_ref, kn_ref,
             c_ref, s1_ref, s2_ref, q_out, k_out, kt_out, vt_out):
        c, s1, s2 = c_ref[...], s1_ref[...], s2_ref[...]
        qhat, _ = _rms(ql_ref[...], MLA_Q_LORA)
        qn = (qhat * qa_ref[...]).astype(MXU_DTYPE)
        khat, _ = _rms(kvl_ref[...], MLA_KV_LORA)
        kvn = (khat * kva_ref[...]).astype(MXU_DTYPE)
        kr = kr_ref[...]
        half1 = lax.broadcasted_iota(jnp.int32, (tm, LANES), 1) >= 64
        ones_row = lax.broadcasted_iota(jnp.int32, (LANES, 1), 0)
        q3, _ = _rms(jnp.stack([_dot(qn, wq_ref[h]) for h in range(HEADS)]), MLA_QK)
        q_out[...] = (_rope(q3 * qn_ref[...], c, s1, s2) * (MLA_SCALE * LOG2E)).astype(MXU_DTYPE)
        k3, _ = _rms(jnp.stack([_dot(kvn, wk_ref[h]) for h in range(HEADS)]) + kr, MLA_QK)
        k3 = _rope(k3 * kn_ref[...], c, s1, s2)
        k_out[...] = k3.astype(MXU_DTYPE)
        for h in range(HEADS):
            for t in range(nsub):
                kt_out[h, t] = k3[h, tk * t:tk * (t + 1)].T.astype(MXU_DTYPE)
        v = _dot(kvn, wv_ref[...])
        for h in range(HEADS):
            vp = v[:, LANES * (h // 2):LANES * (h // 2 + 1)]
            own = half1 if h % 2 else jnp.logical_not(half1)
            vp = jnp.where(own, vp, 0.0)
            for t in range(nsub):
                vpt = vp[tk * t:tk * (t + 1)].T
                vt_out[h, t] = jnp.where(ones_row == MLA_ONES_ROW[h % 2], 1.0, vpt).astype(MXU_DTYPE)

    full = lambda shape: pl.BlockSpec(shape, lambda i: (0,) * len(shape))
    hd = pl.BlockSpec((HEADS, tm, LANES), lambda i: (0, i, 0))
    hdt = pl.BlockSpec((HEADS, nsub, LANES, tk), lambda i: (0, i, 0, 0))
    nat = _sds((HEADS, T, LANES), MXU_DTYPE)
    tr = _sds((HEADS, T // tk, LANES, tk), MXU_DTYPE)
    return pl.pallas_call(
        body, name="mla_prep_fwd", grid=(T // tm,),
        in_specs=[pl.BlockSpec((tm, 256), lambda i: (i, CB_QLAT)), pl.BlockSpec((tm, LANES), lambda i: (i, CB_KVLAT)),
                  pl.BlockSpec((tm, LANES), lambda i: (i, CB_KROPE)),
                  full((1, 256)), full((1, LANES)), full((HEADS, 256, LANES)), full((HEADS, LANES, LANES)),
                  full((LANES, 512)), full((1, LANES)), full((1, LANES)),
                  pl.BlockSpec((tm, LANES), lambda i: (i, 0)), pl.BlockSpec((tm, LANES), lambda i: (i, 0)),
                  pl.BlockSpec((tm, LANES), lambda i: (i, 0))],
        out_specs=[hd, hd, hdt, hdt],
        out_shape=[nat, nat, tr, tr],
        compiler_params=_cp(("parallel",), 32))(
            proj, proj, proj, lw["qa"], lw["kva"], lw["wq"], lw["wk"], lw["wv"], lw["qn"], lw["kn"],
            rope[0], rope[1], rope[2])


def _mla_attn_fwd(q, k, vt):
    T = q.shape[1]
    tk = min(TK, T // 2)
    tq = 2 * tk

    def body(q_ref, k_ref, vt_ref, o_ref, lse_ref, acc_s, m_s, s_a, s_b):
        i = pl.program_id(1)
        key = lax.broadcasted_iota(jnp.int32, (tk, tq), 0)
        qry = lax.broadcasted_iota(jnp.int32, (tk, tq), 1)
        qs = [q_ref[0], q_ref[1]]
        acc_s[...] = jnp.zeros_like(acc_s)
        m_s[...] = jnp.full(m_s.shape, NEG_INF, F32)

        def scores(kj, buf):
            rows = pl.ds(pl.multiple_of(kj * tk, tk), tk)
            for r in range(2):
                buf[r] = _dot_nt(k_ref[r, rows, :], qs[r])

        def consume(kj, buf, diag):
            for r in range(2):
                s = buf[r]
                if diag is not None:
                    s = jnp.where(key + diag * tk <= qry, s, NEG_INF)
                m_old = m_s[r]
                m_new = jnp.maximum(m_old, jnp.max(s, axis=0, keepdims=True))
                alpha = jnp.exp2(m_old - m_new)
                p = jnp.exp2(s - m_new)
                m_s[r] = m_new
                acc_s[r] = alpha * acc_s[r] + _dot(vt_ref[r, kj], p.astype(MXU_DTYPE))

        scores(0, s_a)

        def pair(kj):
            scores(kj + 1, s_b)
            consume(kj, s_a, None)
            scores(kj + 2, s_a)
            consume(kj + 1, s_b, None)

        def octet(ko, carry):
            for t in range(4):
                pair(8 * ko + 2 * t)
            return carry

        lax.fori_loop(0, i // 4, octet, 0)

        @pl.when(i % 4 >= 2)
        def _():
            pair(8 * (i // 4))
            pair(8 * (i // 4) + 2)

        @pl.when(i % 2 == 1)
        def _():
            pair(2 * i - 2)

        scores(2 * i + 1, s_b)
        consume(2 * i, s_a, 0)
        consume(2 * i + 1, s_b, 1)
        l = [acc_s[r, pl.ds(MLA_ONES_ROW[r], 1), :] for r in range(2)]
        head0 = lax.broadcasted_iota(jnp.int32, (LANES, 1), 0) < 64
        o_ref[...] = jnp.where(head0, acc_s[0] / l[0], acc_s[1] / l[1]).T
        for r in range(2):
            lse_ref[r] = m_s[r] + jnp.log2(l[r])

    return pl.pallas_call(
        body, name="mla_attn_fwd", grid=(HEADS // 2, T // tq),
        in_specs=[pl.BlockSpec((2, tq, LANES), lambda j, i: (j, i, 0)),
                  pl.BlockSpec((2, T, LANES), lambda j, i: (j, 0, 0)),
                  pl.BlockSpec((2, T // tk, LANES, tk), lambda j, i: (j, 0, 0, 0))],
        out_specs=[pl.BlockSpec((tq, LANES), lambda j, i: (i, j)),
                   pl.BlockSpec((2, 1, tq), lambda j, i: (j, 0, i))],
        out_shape=[_sds((T, GROUP_WIDTH), F32), _sds((HEADS, 1, T), F32)],
        scratch_shapes=[pltpu.VMEM((2, LANES, tq), F32), pltpu.VMEM((2, 1, tq), F32),
                        pltpu.VMEM((2, tk, tq), F32), pltpu.VMEM((2, tk, tq), F32)],
        compiler_params=_cp(("parallel", "arbitrary"), 40))(q, k, vt)


def _swa_kv_variants(x, half1):
    xs = pltpu.roll(x, 64, 1)
    out = {}
    for g in range(2):
        for r in range(2):
            own = half1 if r else jnp.logical_not(half1)
            out[(g, r)] = jnp.where(own, x if g == r else xs, 0.0).astype(MXU_DTYPE)
    return out


def _swa_alibi():
    ki = np.arange(2 * BLOCK)[:, None]
    qi = np.arange(BLOCK)[None, :]
    dist = BLOCK + qi - ki
    slopes = 2.0 ** -(np.arange(HEADS) + 1.0)
    tab = np.where(((dist >= 0) & (dist < BLOCK))[None], slopes[:, None, None] * dist[None], 1e30)
    return jnp.asarray(tab, F32)


def _swa_kv_variants_t(xt, rows1):
    xs = pltpu.roll(xt, 64, 0)
    out = {}
    for g in range(2):
        for r in range(2):
            own = rows1 if r else jnp.logical_not(rows1)
            out[(g, r)] = jnp.where(own, xt if g == r else xs, 0.0).astype(MXU_DTYPE)
    return out


def _swa_probs(i, nb, q_ref, k_ref, v_ref, pk_ref, pv_ref, qw_ref, kw_ref, alibi_ref, sink_ref):
    scale = SWA_HEAD_DIM ** -0.5
    half1 = lax.broadcasted_iota(jnp.int32, (1, LANES), 1) >= 64
    k_all = jnp.concatenate([pk_ref[...], k_ref[...]], axis=0)
    v_all = jnp.concatenate([pv_ref[...], v_ref[...]], axis=0)
    khat, _ = _rms_halves(k_all, half1)
    kn = khat * kw_ref[...]
    kp = _swa_kv_variants(kn, half1)
    qhat, qr, qn, qt = [], [], [], []
    for j in range(4):
        xh, r = _rms_halves(q_ref[:, LANES * j:LANES * (j + 1)], half1)
        qf = xh * qw_ref[...]
        qhat.append(xh)
        qr.append(r)
        qn.append(qf.astype(MXU_DTYPE))
        qt.append(qf.T.astype(MXU_DTYPE))
    key = lax.broadcasted_iota(jnp.int32, (2 * BLOCK, BLOCK), 0)
    first = jnp.where((i == 0) & (key < BLOCK), NEG_INF, 0.0)
    s = jnp.stack([_dot(kp[(h // 4, h % 2)][BLOCK * b:BLOCK * (b + 2)], qt[h // 2][:, BLOCK * b:BLOCK * (b + 1)])
                   for b in range(nb) for h in range(HEADS)]) * scale - alibi_ref[...]
    s = jnp.concatenate([s[:HEADS] + first, s[HEADS:]], axis=0) if nb > 1 else s + first
    sink = jnp.stack([jnp.full((1, 1), sink_ref[h], F32) for _ in range(nb) for h in range(HEADS)])
    m = jnp.maximum(jnp.max(s, axis=1, keepdims=True), sink)
    e = jnp.exp(s - m)
    es = jnp.exp(sink - m)
    inv = 1.0 / (jnp.sum(e, axis=1, keepdims=True) + es)
    return e * inv, es * inv, dict(half1=half1, kn=kn, kp=kp, v_all=v_all, qhat=qhat, qr=qr, qn=qn)


def _swa_fwd(proj, lw):
    T = proj.shape[0]
    tm = min(TM_SWA, T)
    nb = tm // BLOCK

    def body(q_ref, k_ref, v_ref, pk_ref, pv_ref, qw_ref, kw_ref, alibi_ref, sink_ref, o_ref):
        p, _, c = _swa_probs(pl.program_id(0), nb, q_ref, k_ref, v_ref, pk_ref, pv_ref, qw_ref, kw_ref, alibi_ref,
                             sink_ref)
        p = p.astype(MXU_DTYPE)
        rows1 = lax.broadcasted_iota(jnp.int32, (LANES, 1), 0) >= 64
        vpt = _swa_kv_variants_t(c["v_all"].T, rows1)
        for j in range(4):
            g = j // 2
            o_t = [_dot(vpt[(g, 0)][:, BLOCK * b:BLOCK * (b + 2)], p[HEADS * b + 2 * j])
                   + _dot(vpt[(g, 1)][:, BLOCK * b:BLOCK * (b + 2)], p[HEADS * b + 2 * j + 1]) for b in range(nb)]
            o_t = jnp.concatenate(o_t, axis=1) if nb > 1 else o_t[0]
            o_ref[:, LANES * j:LANES * (j + 1)] = o_t.T

    prev = lambda cb: pl.BlockSpec((BLOCK, LANES), lambda i: (jnp.maximum(i * nb - 1, 0), cb))
    return pl.pallas_call(
        body, name="swa_fwd", grid=(T // tm,),
        in_specs=[pl.BlockSpec((tm, 512), lambda i: (i, CB_SQ)), pl.BlockSpec((tm, LANES), lambda i: (i, CB_SK)),
                  pl.BlockSpec((tm, LANES), lambda i: (i, CB_SV)), prev(CB_SK), prev(CB_SV),
                  pl.BlockSpec((1, LANES), lambda i: (0, 0)), pl.BlockSpec((1, LANES), lambda i: (0, 0)),
                  pl.BlockSpec((nb * HEADS, 2 * BLOCK, BLOCK), lambda i: (0, 0, 0)),
                  pl.BlockSpec(memory_space=pltpu.SMEM)],
        out_specs=pl.BlockSpec((tm, 512), lambda i: (i, 0)),
        out_shape=_sds((T, GROUP_WIDTH), F32),
        compiler_params=_cp(("parallel",), 40))(
            proj, proj, proj, proj, proj, lw["sqn"], lw["skn"], jnp.tile(_swa_alibi(), (nb, 1, 1)), lw["sinks"])


def _shift_down(u, prev, n, row):
    tm = u.shape[0]
    out = pltpu.roll(u, n, 0)
    row8 = lax.broadcasted_iota(jnp.int32, prev.shape, 0)
    for t in range(n):
        src = jnp.sum(jnp.where(row8 == 8 - n + t, prev, 0.0), axis=0, keepdims=True)
        out = jnp.where(row == t, src, out)
    return out


def _shift_up(u, nxt, n, row):
    tm = u.shape[0]
    out = pltpu.roll(u, tm - n, 0)
    row8 = lax.broadcasted_iota(jnp.int32, nxt.shape, 0)
    for t in range(n):
        src = jnp.sum(jnp.where(row8 == t, nxt, 0.0), axis=0, keepdims=True)
        out = jnp.where(row == tm - n + t, src, out)
    return out


def _mix_fwd(proj, o_mla, o_swa, conv_w):
    T = proj.shape[0]
    tm = min(TM_ROW, T)

    def body(gm_ref, ch_ref, cb_ref, cc_ref, gc_ref, gs_ref, pch_ref, pcc_ref, om_ref, os_ref, w_ref, y_ref):
        i = pl.program_id(0)
        row = lax.broadcasted_iota(jnp.int32, (tm, GROUP_WIDTH), 0)
        u = cc_ref[...] * ch_ref[...]
        u_prev = jnp.where(i > 0, pcc_ref[...] * pch_ref[...], 0.0)
        z = (w_ref[0:1, :] * _shift_down(u, u_prev, 2, row) + w_ref[1:2, :] * _shift_down(u, u_prev, 1, row)
             + w_ref[2:3, :] * u)
        gm, gc, gs = gm_ref[...], gc_ref[...], gs_ref[...]
        y_ref[:, 0:512] = (om_ref[...] * (gm * _sigmoid(gm))).astype(MXU_DTYPE)
        y_ref[:, 512:1024] = (cb_ref[...] * z * (gc * _sigmoid(gc))).astype(MXU_DTYPE)
        y_ref[:, 1024:1536] = (os_ref[...] * (gs * _sigmoid(gs))).astype(MXU_DTYPE)

    blk = lambda cb: pl.BlockSpec((tm, 512), lambda i: (i, cb))
    prev = lambda cb: pl.BlockSpec((8, 512), lambda i: (jnp.maximum(i * (tm // 8) - 1, 0), cb))
    tile = pl.BlockSpec((tm, 512), lambda i: (i, 0))
    return pl.pallas_call(
        body, name="mix_fwd", grid=(T // tm,),
        in_specs=[blk(CB_GMLA), blk(CB_CH), blk(CB_CB), blk(CB_CC), blk(CB_GCONV), blk(CB_GSWA),
                  prev(CB_CH), prev(CB_CC), tile, tile, pl.BlockSpec((8, 512), lambda i: (0, 0))],
        out_specs=pl.BlockSpec((tm, D_MIX), lambda i: (i, 0)),
        out_shape=_sds((T, D_MIX), MXU_DTYPE),
        compiler_params=_cp(("parallel",), 32))(
            proj, proj, proj, proj, proj, proj, proj, proj, o_mla, o_swa, conv_w)


def _outproj_loss(ycat, wo, x, target):
    T, D = x.shape
    K = ycat.shape[1]
    tm = min(TM_PROJ, T)
    nt = T // tm

    def body(y_ref, w_ref, x_ref, t_ref, g_ref, loss_ref, acc_ref):
        i = pl.program_id(0)

        @pl.when(i == 0)
        def _():
            acc_ref[...] = jnp.zeros_like(acc_ref)

        err = _dot(y_ref[...], w_ref[...]) + x_ref[...] - t_ref[...]
        g_ref[...] = err * (1.0 / D)
        acc_ref[...] += _fold_rows8(err * err)

        @pl.when(i == nt - 1)
        def _():
            tot = jnp.sum(jnp.sum(acc_ref[...], axis=1, keepdims=True), axis=0, keepdims=True)
            loss_ref[...] = jnp.broadcast_to(tot * (0.5 / D), (8, LANES))

    tile = pl.BlockSpec((tm, D), lambda i: (i, 0))
    return pl.pallas_call(
        body, name="outproj_loss", grid=(nt,),
        in_specs=[pl.BlockSpec((tm, K), lambda i: (i, 0)), pl.BlockSpec((K, D), lambda i: (0, 0)), tile, tile],
        out_specs=[tile, pl.BlockSpec((8, LANES), lambda i: (0, 0))],
        out_shape=[_sds((T, D), F32), _sds((8, LANES), F32)],
        scratch_shapes=[pltpu.VMEM((8, D), F32)],
        compiler_params=_cp(("arbitrary",), 48))(ycat, wo, x, target)


def _outproj_bwd(g, ycat, wot):
    T, D = g.shape
    K = ycat.shape[1]
    tm = min(512, T)
    nt = T // tm

    def body(g_ref, y_ref, wt_ref, dy_ref, dw_ref, acc_ref):
        i = pl.program_id(0)

        @pl.when(i == 0)
        def _():
            acc_ref[...] = jnp.zeros_like(acc_ref)

        gb = g_ref[...].astype(MXU_DTYPE)
        dy_ref[...] = _dot(gb, wt_ref[...])
        acc_ref[...] += _dot_tn(y_ref[...], gb)

        @pl.when(i == nt - 1)
        def _():
            dw_ref[...] = acc_ref[...].astype(WIRE_DTYPE)

    return pl.pallas_call(
        body, name="outproj_bwd", grid=(nt,),
        in_specs=[pl.BlockSpec((tm, D), lambda i: (i, 0)), pl.BlockSpec((tm, K), lambda i: (i, 0)),
                  pl.BlockSpec((D, K), lambda i: (0, 0))],
        out_specs=[pl.BlockSpec((tm, K), lambda i: (i, 0)), pl.BlockSpec((K, D), lambda i: (0, 0))],
        out_shape=[_sds((T, K), F32), _sds((K, D), WIRE_DTYPE)],
        scratch_shapes=[pltpu.VMEM((K, D), F32)],
        compiler_params=_cp(("arbitrary",), 48))(g, ycat, wot)


def _mix_bwd(dycat, proj, o_mla, o_swa, conv_w):
    T = proj.shape[0]
    tm = min(TM_ROW, T)
    nt = T // tm

    def body(dym_ref, dyc_ref, dys_ref, gm_ref, ch_ref, cb_ref, cc_ref, gc_ref, gs_ref, pch_ref, pcc_ref,
             ndy_ref, ncb_ref, ngc_ref, om_ref, os_ref, w_ref,
             d1_ref, dom_ref, dos_ref, dw_ref):
        i = pl.program_id(0)

        @pl.when(i == 0)
        def _():
            dw_ref[...] = jnp.zeros_like(dw_ref)

        row = lax.broadcasted_iota(jnp.int32, (tm, GROUP_WIDTH), 0)

        def gate(g):
            sg = _sigmoid(g)
            return g * sg, sg * (1.0 + g * (1.0 - sg))

        gm = gm_ref[...]
        silu, dsilu = gate(gm)
        dym = dym_ref[...]
        dom_ref[...] = dym * silu
        d1_ref[:, 0:512] = (dym * om_ref[...] * dsilu).astype(MXU_DTYPE)

        gs = gs_ref[...]
        silu, dsilu = gate(gs)
        dys = dys_ref[...]
        dos_ref[...] = dys * silu
        d1_ref[:, 2560:3072] = (dys * os_ref[...] * dsilu).astype(MXU_DTYPE)

        ch, cb, cc, gc, dyc = ch_ref[...], cb_ref[...], cc_ref[...], gc_ref[...], dyc_ref[...]
        w0, w1, w2 = w_ref[0:1, :], w_ref[1:2, :], w_ref[2:3, :]
        u = cc * ch
        u_prev = jnp.where(i > 0, pcc_ref[...] * pch_ref[...], 0.0)
        u1 = _shift_down(u, u_prev, 1, row)
        u2 = _shift_down(u, u_prev, 2, row)
        z = w0 * u2 + w1 * u1 + w2 * u
        silu, dsilu = gate(gc)
        dz = dyc * cb * silu
        ngc = ngc_ref[...]
        dz_next = jnp.where(i < nt - 1, ndy_ref[...] * ncb_ref[...] * (ngc * _sigmoid(ngc)), 0.0)
        du = w2 * dz + w1 * _shift_up(dz, dz_next, 1, row) + w0 * _shift_up(dz, dz_next, 2, row)
        d1_ref[:, 512:1024] = (du * cc).astype(MXU_DTYPE)
        d1_ref[:, 1024:1536] = (dyc * z * silu).astype(MXU_DTYPE)
        d1_ref[:, 1536:2048] = (du * ch).astype(MXU_DTYPE)
        d1_ref[:, 2048:2560] = (dyc * cb * z * dsilu).astype(MXU_DTYPE)
        row8 = lax.broadcasted_iota(jnp.int32, (8, GROUP_WIDTH), 0)
        dw = jnp.zeros((8, GROUP_WIDTH), F32)
        for t, shifted in enumerate((u2, u1, u)):
            dw = dw + jnp.where(row8 == t, jnp.sum(dz * shifted, axis=0, keepdims=True), 0.0)
        dw_ref[...] += dw

    blk = lambda cb: pl.BlockSpec((tm, 512), lambda i: (i, cb))
    prev = lambda cb: pl.BlockSpec((8, 512), lambda i: (jnp.maximum(i * (tm // 8) - 1, 0), cb))
    nxt = lambda cb: pl.BlockSpec((8, 512), lambda i: (jnp.minimum((i + 1) * (tm // 8), T // 8 - 1), cb))
    tile = pl.BlockSpec((tm, 512), lambda i: (i, 0))
    return pl.pallas_call(
        body, name="mix_bwd", grid=(nt,),
        in_specs=[blk(0), blk(1), blk(2), blk(CB_GMLA), blk(CB_CH), blk(CB_CB), blk(CB_CC), blk(CB_GCONV),
                  blk(CB_GSWA), prev(CB_CH), prev(CB_CC), nxt(1), nxt(CB_CB), nxt(CB_GCONV), tile, tile,
                  pl.BlockSpec((8, 512), lambda i: (0, 0))],
        out_specs=[pl.BlockSpec((tm, 3072), lambda i: (i, DPB_MIX)), tile, tile,
                   pl.BlockSpec((8, 512), lambda i: (0, 0))],
        out_shape=[_sds((T, NP), MXU_DTYPE), _sds((T, 512), F32), _sds((T, 512), F32), _sds((8, 512), F32)],
        compiler_params=_cp(("arbitrary",), 48))(
            dycat, dycat, dycat, proj, proj, proj, proj, proj, proj, proj, proj, dycat, proj, proj,
            o_mla, o_swa, conv_w)


def _swa_bwd(proj, o_swa, do_swa, lw, dproj):
    T = proj.shape[0]
    tm = min(TM_SWA, T)
    nb = tm // BLOCK
    scale = SWA_HEAD_DIM ** -0.5

    def body(q_ref, k_ref, v_ref, pk_ref, pv_ref, o_ref, do_ref, qw_ref, kw_ref, alibi_ref, sink_ref, dproj_in,
             dq_ref, dk_ref, dv_ref, dqw_ref, dsink_ref):
        i = pl.program_id(0)

        @pl.when(i == 0)
        def _():
            dk_ref[...] = jnp.zeros_like(dk_ref)
            dv_ref[...] = jnp.zeros_like(dv_ref)
            dqw_ref[...] = jnp.zeros_like(dqw_ref)
            dsink_ref[...] = jnp.zeros_like(dsink_ref)

        p, p_sink, c = _swa_probs(i, nb, q_ref, k_ref, v_ref, pk_ref, pv_ref, qw_ref, kw_ref, alibi_ref, sink_ref)
        half1, kp, qn, qhat, qr = c["half1"], c["kp"], c["qn"], c["qhat"], c["qr"]
        rows1 = lax.broadcasted_iota(jnp.int32, (LANES, 1), 0) >= 64
        kpt = _swa_kv_variants_t(c["kn"].T, rows1)
        vp = _swa_kv_variants(c["v_all"], half1)
        qw = qw_ref[...]
        rows = [slice(BLOCK * b, BLOCK * (b + 1)) for b in range(nb)]
        keys = [slice(BLOCK * b, BLOCK * (b + 2)) for b in range(nb)]
        dob, dot_b, dd0, dd1 = [], [], [], []
        for j in range(4):
            cols = slice(LANES * j, LANES * (j + 1))
            do = do_ref[:, cols]
            do_t = do.T
            prod_t = do_t * o_ref[:, cols].T
            dob.append(do.astype(MXU_DTYPE))
            dot_b.append(do_t.astype(MXU_DTYPE))
            dd0.append(jnp.sum(jnp.where(rows1, 0.0, prod_t), axis=0, keepdims=True))
            dd1.append(jnp.sum(jnp.where(rows1, prod_t, 0.0), axis=0, keepdims=True))
        dd = jnp.stack([(dd1 if h % 2 else dd0)[h // 2][:, rows[b]] for b in range(nb) for h in range(HEADS)])
        dp = jnp.stack([_dot(vp[(h // 4, h % 2)][keys[b]], dot_b[h // 2][:, rows[b]])
                        for b in range(nb) for h in range(HEADS)])
        ds = (p * (dp - dd) * scale).astype(MXU_DTYPE)
        dsink = -jnp.sum(p_sink * dd, axis=2, keepdims=True)
        pb = p.astype(MXU_DTYPE)

        dqw = jnp.zeros((1, LANES), F32)
        for j in range(4):
            g = j // 2
            dqn_t = [_dot(kpt[(g, 0)][:, keys[b]], ds[HEADS * b + 2 * j])
                     + _dot(kpt[(g, 1)][:, keys[b]], ds[HEADS * b + 2 * j + 1]) for b in range(nb)]
            dqn = (jnp.concatenate(dqn_t, axis=1) if nb > 1 else dqn_t[0]).T
            dqw = dqw + jnp.sum(dqn * qhat[j], axis=0, keepdims=True)
            dq_ref[:, LANES * j:LANES * (j + 1)] = _rms_halves_bwd(dqn, qhat[j], qr[j], qw, half1).astype(MXU_DTYPE)
        dqw_ref[...] += _row0(dqw + pltpu.roll(dqw, 64, 1))

        dk_tot = jnp.zeros((tm + BLOCK, LANES), F32)
        dv_tot = jnp.zeros((tm + BLOCK, LANES), F32)
        for b in range(nb):
            dk_b = jnp.zeros((2 * BLOCK, LANES), F32)
            dv_b = jnp.zeros((2 * BLOCK, LANES), F32)
            for g in range(2):
                for r in range(2):
                    own = half1 if r else jnp.logical_not(half1)
                    ha, hb = HEADS * b + 4 * g + r, HEADS * b + 4 * g + 2 + r
                    qa, qb = qn[2 * g][rows[b]], qn[2 * g + 1][rows[b]]
                    da, db = dob[2 * g][rows[b]], dob[2 * g + 1][rows[b]]
                    dkp = jnp.where(own, _dot(ds[ha], qa) + _dot(ds[hb], qb), 0.0)
                    dvp = jnp.where(own, _dot(pb[ha], da) + _dot(pb[hb], db), 0.0)
                    if g != r:
                        dkp = pltpu.roll(dkp, 64, 1)
                        dvp = pltpu.roll(dvp, 64, 1)
                    dk_b = dk_b + dkp
                    dv_b = dv_b + dvp
            pad = lambda x: jnp.concatenate(
                [z for z in (jnp.zeros((BLOCK * b, LANES), F32), x, jnp.zeros((BLOCK * (nb - 1 - b), LANES), F32))
                 if z.shape[0]], axis=0)
            dk_tot = dk_tot + pad(dk_b)
            dv_tot = dv_tot + pad(dv_b)
        dst = pl.ds(pl.multiple_of(i * tm, BLOCK), tm + BLOCK)
        dk_ref[dst, :] += dk_tot
        dv_ref[dst, :] += dv_tot

        row8 = lax.broadcasted_iota(jnp.int32, (8, LANES), 0)
        dsink_tile = jnp.zeros((8, LANES), F32)
        for b in range(nb):
            for h in range(HEADS):
                dsink_tile = dsink_tile + jnp.where(row8 == h, jnp.broadcast_to(dsink[HEADS * b + h], (8, LANES)), 0.0)
        dsink_ref[...] += dsink_tile

    prev = lambda cb: pl.BlockSpec((BLOCK, LANES), lambda i: (jnp.maximum(i * nb - 1, 0), cb))
    tile = pl.BlockSpec((tm, 512), lambda i: (i, 0))
    small = pl.BlockSpec((8, LANES), lambda i: (0, 0))
    acc = pl.BlockSpec((T + BLOCK, LANES), lambda i: (0, 0))
    return pl.pallas_call(
        body, name="swa_bwd", grid=(T // tm,),
        in_specs=[pl.BlockSpec((tm, 512), lambda i: (i, CB_SQ)), pl.BlockSpec((tm, LANES), lambda i: (i, CB_SK)),
                  pl.BlockSpec((tm, LANES), lambda i: (i, CB_SV)), prev(CB_SK), prev(CB_SV), tile, tile,
                  pl.BlockSpec((1, LANES), lambda i: (0, 0)), pl.BlockSpec((1, LANES), lambda i: (0, 0)),
                  pl.BlockSpec((nb * HEADS, 2 * BLOCK, BLOCK), lambda i: (0, 0, 0)),
                  pl.BlockSpec(memory_space=pltpu.SMEM), pl.BlockSpec(memory_space=pl.ANY)],
        out_specs=[pl.BlockSpec((tm, 512), lambda i: (i, DPB_SQ)), acc, acc, small, small],
        out_shape=[_sds((T, NP), MXU_DTYPE), _sds((T + BLOCK, LANES), F32), _sds((T + BLOCK, LANES), F32),
                   _sds((8, LANES), F32), _sds((8, LANES), F32)],
        input_output_aliases={11: 0},
        compiler_params=_cp(("arbitrary",), 48))(
            proj, proj, proj, proj, proj, o_swa, do_swa, lw["sqn"], lw["skn"], jnp.tile(_swa_alibi(), (nb, 1, 1)),
            lw["sinks"], dproj)


def _swa_kv_bwd(proj, dkn, dv, lw, dproj):
    T = proj.shape[0]
    tm = min(TM_SWA, T)
    dkn, dv = dkn[BLOCK:], dv[BLOCK:]

    def body(k_ref, dkn_ref, dv_ref, kw_ref, dproj_in, d_ref, dkw_ref):
        i = pl.program_id(0)

        @pl.when(i == 0)
        def _():
            dkw_ref[...] = jnp.zeros_like(dkw_ref)

        half1 = lax.broadcasted_iota(jnp.int32, (1, LANES), 1) >= 64
        khat, kr = _rms_halves(k_ref[...], half1)
        dkn_t = dkn_ref[...]
        dkw = jnp.sum(dkn_t * khat, axis=0, keepdims=True)
        dkw_ref[...] += _row0(dkw + pltpu.roll(dkw, 64, 1))
        d_ref[:, 0:LANES] = _rms_halves_bwd(dkn_t, khat, kr, kw_ref[...], half1).astype(MXU_DTYPE)
        d_ref[:, LANES:2 * LANES] = dv_ref[...].astype(MXU_DTYPE)

    return pl.pallas_call(
        body, name="swa_kv_bwd", grid=(T // tm,),
        in_specs=[pl.BlockSpec((tm, LANES), lambda i: (i, CB_SK)), pl.BlockSpec((tm, LANES), lambda i: (i, 0)),
                  pl.BlockSpec((tm, LANES), lambda i: (i, 0)), pl.BlockSpec((1, LANES), lambda i: (0, 0)),
                  pl.BlockSpec(memory_space=pl.ANY)],
        out_specs=[pl.BlockSpec((tm, 2 * LANES), lambda i: (i, DPB_SKV)), pl.BlockSpec((8, LANES), lambda i: (0, 0))],
        out_shape=[_sds((T, NP), MXU_DTYPE), _sds((8, LANES), F32)],
        input_output_aliases={4: 0},
        compiler_params=_cp(("arbitrary",), 32))(proj, dkn, dv, lw["skn"], dproj)


def _mla_attn_bwd(q, k, kt, vt, o, do, lse):
    T = q.shape[1]
    tk = min(TK, T // 2)
    tq = 2 * tk

    def body(q_ref, k_ref, kt_ref, vt_ref, o_ref, do_ref, lse_ref, dq_ref, dk_ref, dv_ref, dq_s, lse_s, dd_s,
             s_a, s_b, p_a, p_b):
        h = pl.program_id(0)
        i = pl.program_id(1)

        @pl.when(i == 0)
        def _():
            dk_ref[...] = jnp.zeros_like(dk_ref)
            dv_ref[...] = jnp.zeros_like(dv_ref)

        qry = lax.broadcasted_iota(jnp.int32, (tq, tk), 0)
        key = lax.broadcasted_iota(jnp.int32, (tq, tk), 1)
        own = (lax.broadcasted_iota(jnp.int32, (1, LANES), 1) // 64) == (h % 2)
        do_own = jnp.where(own, do_ref[...], 0.0)
        dob = do_own.astype(MXU_DTYPE)
        dob_t = do_own.T.astype(MXU_DTYPE)
        qh = q_ref[0]
        qh_t = qh.astype(F32).T.astype(MXU_DTYPE)
        dd_col = jnp.sum(do_own * o_ref[...], axis=-1, keepdims=True)
        lse_col = jnp.broadcast_to(lse_ref[0], (LANES, tq)).T
        for c in range(tk // LANES):
            lse_s[:, LANES * c:LANES * (c + 1)] = lse_col
            dd_s[:, LANES * c:LANES * (c + 1)] = jnp.broadcast_to(dd_col, (tq, LANES))
        dq_s[...] = jnp.zeros_like(dq_s)

        def scores(kj, s_buf, p_buf):
            s_buf[...] = _dot(qh, kt_ref[0, kj])
            p_buf[...] = _dot(dob, vt_ref[0, kj])

        def consume(kj, s_buf, p_buf, diag):
            rows = pl.ds(pl.multiple_of(kj * tk, tk), tk)
            s = s_buf[...]
            if diag is not None:
                s = jnp.where(key + diag * tk <= qry, s, NEG_INF)
            p = jnp.exp2(s - lse_s[...])
            ds = (p * (p_buf[...] - dd_s[...])).astype(MXU_DTYPE)
            dq_s[...] += _dot(ds, k_ref[0, rows, :])
            dk_ref[0, kj] += _dot(qh_t, ds)
            dv_ref[0, kj] += _dot(dob_t, p.astype(MXU_DTYPE))

        scores(0, s_a, p_a)

        def pair(kj):
            scores(kj + 1, s_b, p_b)
            consume(kj, s_a, p_a, None)
            scores(kj + 2, s_a, p_a)
            consume(kj + 1, s_b, p_b, None)

        def octet(ko, carry):
            for t in range(4):
                pair(8 * ko + 2 * t)
            return carry

        lax.fori_loop(0, i // 4, octet, 0)

        @pl.when(i % 4 >= 2)
        def _():
            pair(8 * (i // 4))
            pair(8 * (i // 4) + 2)

        @pl.when(i % 2 == 1)
        def _():
            pair(2 * i - 2)

        scores(2 * i + 1, s_b, p_b)
        consume(2 * i, s_a, p_a, 0)
        consume(2 * i + 1, s_b, p_b, 1)
        dq_ref[0] = dq_s[...]

    res = pl.BlockSpec((1, T, LANES), lambda h, i: (h, 0, 0))
    res_t = pl.BlockSpec((1, T // tk, LANES, tk), lambda h, i: (h, 0, 0, 0))
    buf = pltpu.VMEM((tq, tk), F32)
    acc_t = _sds((HEADS, T // tk, LANES, tk), F32)
    return pl.pallas_call(
        body, name="mla_attn_bwd", grid=(HEADS, T // tq),
        in_specs=[pl.BlockSpec((1, tq, LANES), lambda h, i: (h, i, 0)), res, res_t, res_t,
                  pl.BlockSpec((tq, LANES), lambda h, i: (i, h // 2)),
                  pl.BlockSpec((tq, LANES), lambda h, i: (i, h // 2)),
                  pl.BlockSpec((1, 1, tq), lambda h, i: (h, 0, i))],
        out_specs=[pl.BlockSpec((1, tq, LANES), lambda h, i: (h, i, 0)), res_t, res_t],
        out_shape=[_sds((HEADS, T, LANES), F32), acc_t, acc_t],
        scratch_shapes=[pltpu.VMEM((tq, LANES), F32), buf, buf, buf, buf, buf, buf],
        compiler_params=_cp(("parallel", "arbitrary"), 48))(q, k, kt, vt, o, do, lse)


def _mla_prep_bwd(proj, dq, dk, dv, lw, rope, dproj):
    T = proj.shape[0]
    tm = min(TK, T // 2)

    def body(ql_ref, kvl_ref, kr_ref, dq_ref, dk_ref, dv_ref, qa_ref, kva_ref, wq_ref, wk_ref, wv_ref,
             wqt_ref, wkt_ref, wvt_ref, qn_ref, kn_ref, c_ref, s1_ref, s2_ref, dproj_in,
             d_ref, dwq_ref, dwk_ref, dwv_ref, dqa_ref, dkva_ref, dqn_ref, dkn_ref):
        i = pl.program_id(0)

        @pl.when(i == 0)
        def _():
            for ref in (dwq_ref, dwk_ref, dwv_ref, dqa_ref, dkva_ref, dqn_ref, dkn_ref):
                ref[...] = jnp.zeros_like(ref)

        c, s1, s2 = c_ref[...], s1_ref[...], s2_ref[...]
        lane = lax.broadcasted_iota(jnp.int32, (1, LANES), 1)
        qlhat, qlr = _rms(ql_ref[...], MLA_Q_LORA)
        qn = (qlhat * qa_ref[...]).astype(MXU_DTYPE)
        kvhat, kvr = _rms(kvl_ref[...], MLA_KV_LORA)
        kvn = (kvhat * kva_ref[...]).astype(MXU_DTYPE)
        kr = kr_ref[...]
        x3, r3 = _rms(jnp.stack([_dot(qn, wq_ref[h]) for h in range(HEADS)]), MLA_QK)
        dy3 = _rope_bwd(dq_ref[...] * MLA_SCALE, c, s1, s2)
        dqw = jnp.sum(jnp.sum(dy3 * x3, axis=0), axis=0, keepdims=True)
        dx3 = _rms_bwd(dy3, x3, r3, qn_ref[...], MLA_QK).astype(MXU_DTYPE)
        dqnl = jnp.zeros((tm, MLA_Q_LORA), F32)
        for h in range(HEADS):
            dwq_ref[h] += _dot_tn(qn, dx3[h])
            dqnl = dqnl + _dot(dx3[h], wqt_ref[h])

        x3, r3 = _rms(jnp.stack([_dot(kvn, wk_ref[h]) for h in range(HEADS)]) + kr, MLA_QK)
        dy3 = _rope_bwd(jnp.stack([dk_ref[h, 0].T for h in range(HEADS)]) * LN2, c, s1, s2)
        dkw = jnp.sum(jnp.sum(dy3 * x3, axis=0), axis=0, keepdims=True)
        dxf3 = _rms_bwd(dy3, x3, r3, kn_ref[...], MLA_QK)
        dkr = jnp.sum(dxf3, axis=0)
        dx3 = dxf3.astype(MXU_DTYPE)
        dkvn = jnp.zeros((tm, MLA_KV_LORA), F32)
        for h in range(HEADS):
            dwk_ref[h] += _dot_tn(kvn, dx3[h])
            dkvn = dkvn + _dot(dx3[h], wkt_ref[h])
        dvc = jnp.concatenate([(dv_ref[2 * j, 0] + dv_ref[2 * j + 1, 0]).T for j in range(4)],
                              axis=1).astype(MXU_DTYPE)
        dwv_ref[...] += _dot_tn(kvn, dvc)
        dkvn = dkvn + _dot(dvc, wvt_ref[...])
        dqa_ref[...] += _row0(jnp.sum(dqnl * qlhat, axis=0, keepdims=True))
        dkva_ref[...] += _row0(jnp.sum(dkvn * kvhat, axis=0, keepdims=True))
        dqn_ref[...] += _row0(dqw)
        dkn_ref[...] += _row0(dkw)
        d_ref[:, 0:256] = _rms_bwd(dqnl, qlhat, qlr, qa_ref[...], MLA_Q_LORA).astype(MXU_DTYPE)
        d_ref[:, 256:384] = _rms_bwd(dkvn, kvhat, kvr, kva_ref[...], MLA_KV_LORA).astype(MXU_DTYPE)
        d_ref[:, 384:512] = jnp.where((lane >= 64) & (lane < 96), dkr, 0.0).astype(MXU_DTYPE)

    full = lambda shape: pl.BlockSpec(shape, lambda i: (0,) * len(shape))
    hd = pl.BlockSpec((HEADS, tm, LANES), lambda i: (0, i, 0))
    hdt = pl.BlockSpec((HEADS, 1, LANES, tm), lambda i: (0, i, 0, 0))
    tab = pl.BlockSpec((tm, LANES), lambda i: (i, 0))
    return pl.pallas_call(
        body, name="mla_prep_bwd", grid=(T // tm,),
        in_specs=[pl.BlockSpec((tm, 256), lambda i: (i, CB_QLAT)), pl.BlockSpec((tm, LANES), lambda i: (i, CB_KVLAT)),
                  pl.BlockSpec((tm, LANES), lambda i: (i, CB_KROPE)), hd, hdt, hdt,
                  full((1, 256)), full((1, LANES)), full((HEADS, 256, LANES)), full((HEADS, LANES, LANES)),
                  full((LANES, 512)), full((HEADS, LANES, 256)), full((HEADS, LANES, LANES)), full((512, LANES)),
                  full((1, LANES)), full((1, LANES)), tab, tab, tab, pl.BlockSpec(memory_space=pl.ANY)],
        out_specs=[pl.BlockSpec((tm, 512), lambda i: (i, DPB_MLA)), full((HEADS, 256, LANES)),
                   full((HEADS, LANES, LANES)), full((LANES, 512)), full((8, 256)), full((8, LANES)),
                   full((8, LANES)), full((8, LANES))],
        out_shape=[_sds((T, NP), MXU_DTYPE), _sds((HEADS, 256, LANES), F32), _sds((HEADS, LANES, LANES), F32),
                   _sds((LANES, 512), F32), _sds((8, 256), F32), _sds((8, LANES), F32), _sds((8, LANES), F32),
                   _sds((8, LANES), F32)],
        input_output_aliases={19: 0},
        compiler_params=_cp(("arbitrary",), 48))(
            proj, proj, proj, dq, dk, dv, lw["qa"], lw["kva"], lw["wq"], lw["wk"], lw["wv"],
            lw["wqt"], lw["wkt"], lw["wvt"], lw["qn"], lw["kn"], rope[0], rope[1], rope[2], dproj)


def _inproj_bwd_dx(dproj, wpt, x, g_in, ng):
    T, D = x.shape
    tm = min(TM_PROJ, T)

    def body(dp_ref, wt_ref, x_ref, g_ref, w_ref, dx_ref, dw_ref):
        i = pl.program_id(0)

        @pl.when(i == 0)
        def _():
            dw_ref[...] = jnp.zeros_like(dw_ref)

        dh = _dot(dp_ref[...], wt_ref[...])
        xhat, r = _rms(x_ref[...], D)
        dw_ref[...] += _row0(jnp.sum(dh * xhat, axis=0, keepdims=True))
        dx_ref[...] = g_ref[...] + _rms_bwd(dh, xhat, r, w_ref[...], D)

    tile = pl.BlockSpec((tm, D), lambda i: (i, 0))
    return pl.pallas_call(
        body, name="inproj_bwd_dx", grid=(T // tm,),
        in_specs=[pl.BlockSpec((tm, NP), lambda i: (i, 0)), pl.BlockSpec((NP, D), lambda i: (0, 0)), tile, tile,
                  pl.BlockSpec((1, D), lambda i: (0, 0))],
        out_specs=[tile, pl.BlockSpec((8, D), lambda i: (0, 0))],
        out_shape=[_sds((T, D), F32), _sds((8, D), F32)],
        compiler_params=_cp(("arbitrary",), 48))(dproj, wpt, x, g_in, ng)


def _rope_tables(T, token=0.0):
    half = MLA_ROPE // 2
    inv_freq = jnp.power(jnp.float32(ROPE_THETA), -jnp.arange(half, dtype=F32) / half)
    z = lambda n: jnp.zeros((n,), F32)
    freq = jnp.concatenate([z(MLA_NOPE), inv_freq, inv_freq, z(32)])
    first = jnp.concatenate([z(64), jnp.ones((16,), F32), z(48)])
    second = jnp.concatenate([z(80), jnp.ones((16,), F32), z(32)])
    ang = (jnp.arange(T, dtype=F32) + token)[:, None] * freq[None, :]
    sin = jnp.sin(ang)
    return jnp.cos(ang), -sin * first[None, :], sin * second[None, :]


def _pad_lanes(v, n=LANES):
    v = v.reshape(1, -1)
    return jnp.pad(v, ((0, 0), (0, n - v.shape[1])))


def _pack_win_t(wt):
    z = lambda n: jnp.zeros((n, wt.shape[1]), wt.dtype)
    return jnp.concatenate([wt[416:2976], wt[3744:4256], wt[0:384], z(64), wt[384:416], z(32), wt[2976:3488],
                            wt[3488:3616], wt[3616:3744]], axis=0)


def _unpack_dwin(d):
    return jnp.concatenate([d[:, 3072:3456], d[:, 3520:3552], d[:, 0:2560], d[:, 3584:4096], d[:, 4096:4224],
                            d[:, 4224:4352], d[:, 2560:3072]], axis=1)


def _inproj_weights(l, norm_g, w_in_t):
    wpt = _pack_win_t(w_in_t)
    return dict(ng=norm_g[l].reshape(1, -1), wp=wpt.T, wpt=wpt)


def _mixer_weights(l, qa, wqb_full, kva, wkvb_full, qn, kn, conv_full, sqn, skn, sinks, w_out_full):
    wq = jnp.pad(wqb_full, ((0, 0), (0, 0), (0, LANES - MLA_QK)))
    wk = jnp.pad(wkvb_full[:, :, :MLA_NOPE], ((0, 0), (0, 0), (0, LANES - MLA_NOPE)))
    wv = jnp.transpose(wkvb_full[:, :, MLA_NOPE:], (1, 0, 2)).reshape(MLA_KV_LORA, GROUP_WIDTH)
    return dict(
        qa=qa[l].reshape(1, -1), kva=kva[l].reshape(1, -1),
        wq=wq, wk=wk, wv=wv, wqt=jnp.transpose(wq, (0, 2, 1)), wkt=jnp.transpose(wk, (0, 2, 1)), wvt=wv.T,
        qn=_pad_lanes(qn[l]), kn=_pad_lanes(kn[l]),
        conv=jnp.pad(conv_full, ((0, 5), (0, 0))),
        sqn=jnp.tile(sqn[l].reshape(1, -1), (1, 2)), skn=jnp.tile(skn[l].reshape(1, -1), (1, 2)),
        sinks=sinks[l], wo=w_out_full, wot=w_out_full.T)


def _layer_weights(l, norm_g, w_in_full, qa, wqb_full, kva, wkvb_full, qn, kn, conv_full, sqn, skn, sinks,
                   w_out_full):
    return dict(_inproj_weights(l, norm_g, w_in_full.T),
                **_mixer_weights(l, qa, wqb_full, kva, wkvb_full, qn, kn, conv_full, sqn, skn, sinks, w_out_full))


def _layer_fwd(x, lw, rope, late_weights=None, target=None):
    proj, h = _inproj_fwd(x, lw["ng"], lw["wp"])
    if late_weights is not None:
        lw = dict(lw, **late_weights(proj))
    q, k, kt, vt = _mla_prep_fwd(proj, lw, rope)
    o_mla, lse = _mla_attn_fwd(q, k, vt)
    o_swa = _swa_fwd(proj, lw)
    ycat = _mix_fwd(proj, o_mla, o_swa, lw["conv"])
    if target is None:
        out = _mm_nn(ycat, lw["wo"], "outproj_fwd", residual=x)
    else:
        out = _outproj_loss(ycat, lw["wo"], x, target)
    return out, dict(x=x, proj=proj, h=h, q=q, k=k, kt=kt, vt=vt, o_mla=o_mla, lse=lse, o_swa=o_swa, ycat=ycat,
                     lw=lw)


def _layer_bwd(g, sv, lw, rope, on_big_grads=None):
    proj = sv["proj"]
    dycat, d_wo = _outproj_bwd(g, sv["ycat"], lw["wot"])
    dproj, do_mla, do_swa, d_conv = _mix_bwd(dycat, proj, sv["o_mla"], sv["o_swa"], lw["conv"])
    dproj, dkn_acc, dv_acc, d_sqn, d_sinks = _swa_bwd(proj, sv["o_swa"], do_swa, lw, dproj)
    dproj, d_skn = _swa_kv_bwd(proj, dkn_acc, dv_acc, lw, dproj)
    dq, dk, dv = _mla_attn_bwd(sv["q"], sv["k"], sv["kt"], sv["vt"], sv["o_mla"], do_mla, sv["lse"])
    dproj, d_wq, d_wk, d_wv, d_qa, d_kva, d_qn, d_kn = _mla_prep_bwd(proj, dq, dk, dv, lw, rope, dproj)
    grads = dict(
        w_out=d_wo, w_qb=d_wq[:, :, :MLA_QK],
        w_kvb=jnp.concatenate([d_wk[:, :, :MLA_NOPE],
                               jnp.transpose(d_wv.reshape(MLA_KV_LORA, HEADS, MLA_NOPE), (1, 0, 2))], axis=2))
    token = 0.0 if on_big_grads is None else on_big_grads("mixer", grads)
    d_wp = _mm_tn(sv["h"], dproj, "inproj_bwd_dw", WIRE_DTYPE, tn=NP // 2)
    grads["w_in"] = _unpack_dwin(d_wp)
    token = token if on_big_grads is None else token + on_big_grads("w_in", grads)
    dx, d_ng = _inproj_bwd_dx(dproj, lw["wpt"], sv["x"], g, lw["ng"] + token)
    grads.update(
        conv=d_conv[0:3], norm_g=d_ng[0], qa=d_qa[0], kva=d_kva[0], qn=d_qn[0, :MLA_QK], kn=d_kn[0, :MLA_QK],
        sqn=d_sqn[0, :SWA_HEAD_DIM], skn=d_skn[0, :SWA_HEAD_DIM], sinks=d_sinks[:, 0])
    return dx, grads


def _local_step(x, target, lws, rope):
    saved = []
    for l, lw in enumerate(lws):
        x, sv = _layer_fwd(x, lw, rope, target=target if l == len(lws) - 1 else None)
        saved.append(sv)
    g, loss_tile = x
    grads = [None] * len(lws)
    for l in reversed(range(len(lws))):
        g, grads[l] = _layer_bwd(g, saved[l], lws[l], rope)
    return loss_tile, g, grads


def _my_coords():
    return lax.axis_index("x"), lax.axis_index("y"), lax.axis_index("c")


def _peer(me, k):
    x, y, c = me
    return (1 - x if k & 4 else x, 1 - y if k & 2 else y, 1 - c if k & 1 else c)


def _lin(d):
    return 4 * d[0] + 2 * d[1] + d[2]


def _push_copies(ins, lands, send_sems, recv_sems, gather):
    me = _my_coords()
    my = _lin(me)
    out, inc = [], []
    for a in range(len(ins)):
        for k in range(1, N_DEV):
            peer = _peer(me, k)
            sems = dict(send_sem=send_sems.at[a * 7 + k - 1], recv_sem=recv_sems.at[a * 7 + k - 1],
                        device_id=peer, device_id_type=pl.DeviceIdType.MESH)
            src = ins[a] if gather else ins[a].at[_lin(peer)]
            out.append(pltpu.make_async_remote_copy(src_ref=src, dst_ref=lands[a].at[my], **sems))
            inc.append(pltpu.make_async_remote_copy(src_ref=src, dst_ref=lands[a].at[_lin(peer)], **sems))
    return out, inc


def _push_start(arrays, name, gather):
    n = len(arrays)
    land_shapes = [((N_DEV,) + a.shape) if gather else a.shape for a in arrays]

    def body(*refs):
        ins, lands = refs[:n], refs[n:2 * n]
        send_sems, recv_sems = refs[2 * n], refs[2 * n + 1]
        token = refs[-1]
        out, _ = _push_copies(ins, lands, send_sems, recv_sems, gather)
        for cp in out:
            cp.start()
        token[...] = jnp.zeros_like(token)

    hbm = pl.BlockSpec(memory_space=pltpu.HBM)
    sem = pl.BlockSpec(memory_space=pltpu.SEMAPHORE)
    res = pl.pallas_call(
        body, name=name,
        out_shape=(pltpu.SemaphoreType.DMA((7 * n,)), pltpu.SemaphoreType.DMA((7 * n,)),
                   *[pltpu.HBM(a.shape, a.dtype) for a in arrays],
                   *[pltpu.HBM(s, a.dtype) for s, a in zip(land_shapes, arrays)],
                   _sds((8, LANES), F32)),
        in_specs=(hbm,) * (2 * n),
        out_specs=(sem, sem) + (hbm,) * (2 * n) + (pl.BlockSpec(memory_space=pltpu.VMEM),),
        input_output_aliases={i: 2 + i for i in range(2 * n)},
        compiler_params=pltpu.CompilerParams(has_side_effects=pltpu.SideEffectType.DATAFLOW_SIDE_EFFECTING),
    )(*[pltpu.with_memory_space_constraint(a, pltpu.HBM) for a in arrays],
      *[pltpu.with_memory_space_constraint(lax.empty(s, a.dtype), pltpu.HBM) for s, a in zip(land_shapes, arrays)])
    return dict(send=res[0], recv=res[1], src=res[2:2 + n], land=res[2 + n:2 + 2 * n], token=res[-1][0, 0],
                gather=gather)


def _push_wait(handle, after, name):
    n = len(handle["src"])
    gather = handle["gather"]

    def body(*refs):
        ins, lands = refs[:n], refs[n:2 * n]
        send_sems, recv_sems = refs[2 * n], refs[2 * n + 1]
        out, inc = _push_copies(ins, lands, send_sems, recv_sems, gather)
        for cp in out:
            cp.wait_send()
        for cp in inc:
            cp.wait_recv()

    hbm = pl.BlockSpec(memory_space=pltpu.HBM)
    sem = pl.BlockSpec(memory_space=pltpu.SEMAPHORE)
    res = pl.pallas_call(
        body, name=name,
        out_shape=tuple(pltpu.HBM(a.shape, a.dtype) for a in (*handle["src"], *handle["land"])),
        in_specs=(hbm,) * (2 * n) + (sem, sem, pl.BlockSpec(memory_space=pl.ANY)),
        out_specs=(hbm,) * (2 * n),
        input_output_aliases={i: i for i in range(2 * n)},
        compiler_params=pltpu.CompilerParams(has_side_effects=pltpu.SideEffectType.DATAFLOW_SIDE_EFFECTING),
    )(*handle["src"], *handle["land"], handle["send"], handle["recv"], after)
    return res[n:]


def _small_all_reduce(v):
    R = v.shape[0]

    def body(v_ref, o_ref, buf, send_sems, recv_sems):
        me = _my_coords()
        my = _lin(me)
        sends = []
        for k in range(1, N_DEV):
            cp = pltpu.make_async_remote_copy(
                src_ref=v_ref, dst_ref=buf.at[my], send_sem=send_sems.at[k - 1], recv_sem=recv_sems.at[k - 1],
                device_id=_peer(me, k), device_id_type=pl.DeviceIdType.MESH)
            cp.start()
            sends.append(cp)
        buf[my] = v_ref[...]
        for k in range(1, N_DEV):
            pltpu.make_async_remote_copy(
                src_ref=v_ref, dst_ref=buf.at[_lin(_peer(me, k))], send_sem=send_sems.at[k - 1],
                recv_sem=recv_sems.at[k - 1], device_id=_peer(me, k),
                device_id_type=pl.DeviceIdType.MESH).wait_recv()
        for cp in sends:
            cp.wait_send()
        tot = buf[0]
        for d in range(1, N_DEV):
            tot = tot + buf[d]
        o_ref[...] = tot

    vm = pl.BlockSpec(memory_space=pltpu.VMEM)
    return pl.pallas_call(
        body, name="small_all_reduce", in_specs=[vm], out_specs=vm, out_shape=_sds(v.shape, F32),
        scratch_shapes=[pltpu.VMEM((N_DEV, R, LANES), F32), pltpu.SemaphoreType.DMA((7,)),
                        pltpu.SemaphoreType.DMA((7,))],
    )(v)


def _adamw_math(w, g, m, v):
    m = ADAM_B1 * m + (1.0 - ADAM_B1) * g
    v = ADAM_B2 * v + (1.0 - ADAM_B2) * (g * g)
    m_hat = m / (1.0 - ADAM_B1 ** ADAM_STEP)
    v_hat = v / (1.0 - ADAM_B2 ** ADAM_STEP)
    delta = -ADAM_LR * (m_hat / (jnp.sqrt(v_hat) + ADAM_EPS) + ADAM_WD * w)
    return delta, m, v


def _adamw(parts, w, m, v, name, tr):
    P, R, C = parts.shape
    tr = min(tr, R)

    def body(p_ref, w_ref, m_ref, v_ref, g_out, d_out, m_out, v_out):
        g = p_ref[0].astype(F32)
        for d in range(1, P):
            g = g + p_ref[d].astype(F32)
        delta, m_new, v_new = _adamw_math(w_ref[...], g, m_ref[...], v_ref[...])
        g_out[...] = g
        d_out[...] = delta
        m_out[...] = m_new
        v_out[...] = v_new

    tile = pl.BlockSpec((tr, C), lambda i: (i, 0))
    return pl.pallas_call(
        body, name=name, grid=(R // tr,),
        in_specs=[pl.BlockSpec((P, tr, C), lambda i: (0, i, 0)), tile, tile, tile],
        out_specs=[tile] * 4, out_shape=[_sds((R, C), F32)] * 4,
        compiler_params=_cp(("parallel",), 32))(parts, w, m, v)


SMALL = (("norm_g", D_MODEL), ("mla_q_a_norm", MLA_Q_LORA), ("mla_kv_a_norm", MLA_KV_LORA), ("mla_q_norm", MLA_QK),
         ("mla_k_norm", MLA_QK), ("swa_q_norm", SWA_HEAD_DIM), ("swa_k_norm", SWA_HEAD_DIM), ("swa_sinks", HEADS))
SMALL_GRAD_KEY = dict(norm_g="norm_g", mla_q_a_norm="qa", mla_kv_a_norm="kva", mla_q_norm="qn", mla_k_norm="kn",
                      swa_q_norm="sqn", swa_k_norm="skn", swa_sinks="sinks")
SMALL_ROWS = 32
CONV_ROWS = 24


def _pack_small(get):
    parts = []
    for l in range(DEPTH):
        for name, n in SMALL:
            v = get(name, l).reshape(-1)
            parts.append(jnp.pad(v, (0, (-n) % LANES)))
    return jnp.concatenate(parts).reshape(SMALL_ROWS, LANES)


def _unpack_small(packed):
    flat = packed.reshape(-1)
    out = {name: [] for name, _ in SMALL}
    off = 0
    for l in range(DEPTH):
        for name, n in SMALL:
            out[name].append(flat[off:off + n])
            off += n + (-n) % LANES
    return {name: jnp.stack(v) for name, v in out.items()}


def kernel(x, norm_g, w_in, mla_q_a_norm, mla_w_qb, mla_kv_a_norm, mla_w_kvb, mla_q_norm, mla_k_norm, conv_w, swa_q_norm, swa_k_norm, swa_sinks, w_out, loss_target, m_norm_g, m_w_in, m_mla_q_a_norm, m_mla_w_qb, m_mla_kv_a_norm, m_mla_w_kvb, m_mla_q_norm, m_mla_k_norm, m_conv_w, m_swa_q_norm, m_swa_k_norm, m_swa_sinks, m_w_out, v_norm_g, v_w_in, v_mla_q_a_norm, v_mla_w_qb, v_mla_kv_a_norm, v_mla_w_kvb, v_mla_q_norm, v_mla_k_norm, v_conv_w, v_swa_q_norm, v_swa_k_norm, v_swa_sinks, v_w_out):
    T = x.shape[1]
    weights = dict(norm_g=norm_g, w_in=w_in, mla_q_a_norm=mla_q_a_norm, mla_w_qb=mla_w_qb,
                   mla_kv_a_norm=mla_kv_a_norm, mla_w_kvb=mla_w_kvb, mla_q_norm=mla_q_norm, mla_k_norm=mla_k_norm,
                   conv_w=conv_w, swa_q_norm=swa_q_norm, swa_k_norm=swa_k_norm, swa_sinks=swa_sinks, w_out=w_out)
    mom_m = dict(norm_g=m_norm_g, w_in=m_w_in, mla_q_a_norm=m_mla_q_a_norm, mla_w_qb=m_mla_w_qb,
                 mla_kv_a_norm=m_mla_kv_a_norm, mla_w_kvb=m_mla_w_kvb, mla_q_norm=m_mla_q_norm,
                 mla_k_norm=m_mla_k_norm, conv_w=m_conv_w, swa_q_norm=m_swa_q_norm, swa_k_norm=m_swa_k_norm,
                 swa_sinks=m_swa_sinks, w_out=m_w_out)
    mom_v = dict(norm_g=v_norm_g, w_in=v_w_in, mla_q_a_norm=v_mla_q_a_norm, mla_w_qb=v_mla_w_qb,
                 mla_kv_a_norm=v_mla_kv_a_norm, mla_w_kvb=v_mla_w_kvb, mla_q_norm=v_mla_q_norm,
                 mla_k_norm=v_mla_k_norm, conv_w=v_conv_w, swa_q_norm=v_swa_q_norm, swa_k_norm=v_swa_k_norm,
                 swa_sinks=v_swa_sinks, w_out=v_w_out)

    my = _lin(_my_coords())

    def shards(l):
        return [w_in[l].astype(MXU_DTYPE).T, mla_w_qb[l].astype(MXU_DTYPE), mla_w_kvb[l].astype(MXU_DTYPE),
                w_out[l].astype(MXU_DTYPE), conv_w[l]]

    def inproj_weights(l, g_win_t):
        return _inproj_weights(l, norm_g, g_win_t.reshape(IN_COLS, D_MODEL))

    def mixer_weights(l, gathered):
        g_wqb, g_wkvb, g_wout, g_conv = gathered
        return _mixer_weights(
            l, mla_q_a_norm, g_wqb, mla_kv_a_norm, g_wkvb, mla_q_norm, mla_k_norm,
            jnp.transpose(g_conv, (1, 0, 2)).reshape(3, GROUP_WIDTH), swa_q_norm, swa_k_norm, swa_sinks,
            g_wout.reshape(D_MIX, D_MODEL))

    slot_of = dict(
        w_in=lambda g: jnp.transpose(g["w_in"].reshape(D_MODEL, N_DEV, IN_COLS // N_DEV), (1, 0, 2)),
        w_out=lambda g: g["w_out"].reshape(N_DEV, D_MIX // N_DEV, D_MODEL),
        w_qb=lambda g: g["w_qb"], w_kvb=lambda g: g["w_kvb"])

    def own_slot(landed, mine):
        return [lax.dynamic_update_index_in_dim(a, m, my, 0) for a, m in zip(landed, mine)]

    def landed(handle, after, name, mine):
        return own_slot(_push_wait(handle, after, name), mine)

    gather_in0 = _push_start(shards(0)[:1], "weight_gather_in0_start", gather=True)
    gather0 = _push_start(shards(0)[1:], "weight_gather0_start", gather=True)
    rope = _rope_tables(T, gather_in0["token"] + gather0["token"])
    w_in0_t = landed(gather_in0, rope[0], "weight_gather_in0_wait", shards(0)[:1])[0]
    w_in0_t, conv_late = lax.optimization_barrier((w_in0_t, conv_w[1]))
    gather1 = _push_start(shards(1)[:4] + [conv_late], "weight_gather1_start", gather=True)
    lw0 = inproj_weights(0, w_in0_t)
    lw0 = dict(lw0, ng=lw0["ng"] + gather1["token"])
    x1, sv0 = _layer_fwd(
        x[0], lw0, rope,
        late_weights=lambda proj: mixer_weights(0, landed(gather0, proj, "weight_gather0_wait", shards(0)[1:])))
    g1_all = landed(gather1, x1, "weight_gather1_wait", shards(1))
    (g2, loss_tile), sv1 = _layer_fwd(x1, dict(inproj_weights(1, g1_all[0]), **mixer_weights(1, g1_all[1:])), rope,
                                      target=loss_target[0])

    parts = {(1, "w_in"): ("w_in", "w_out", "w_qb", "w_kvb"), (0, "mixer"): ("w_out", "w_qb", "w_kvb"),
             (0, "w_in"): ("w_in",)}
    started = []

    def start_exchange(l, part, g):
        if (l, part) not in parts:
            return 0.0
        sl = [slot_of[n](g) for n in parts[(l, part)]]
        handle = _push_start(sl, "grad_exchange%d_%s_start" % (l, part), gather=False)
        started.append((l, part, sl, handle))
        return handle["token"]

    g1, grads1 = _layer_bwd(g2, sv1, sv1["lw"], rope, on_big_grads=functools.partial(start_exchange, 1))
    lw0b = dict(sv0["lw"], conv=sv0["lw"]["conv"] + started[0][3]["token"])
    grad_x, grads0 = _layer_bwd(g1, sv0, lw0b, rope, on_big_grads=functools.partial(start_exchange, 0))
    recv = {}

    def receive(l, part, sl, handle, after):
        got = landed(handle, after, "grad_exchange%d_%s_wait" % (l, part), [s[my] for s in sl])
        recv.update({(l, n): a for n, a in zip(parts[(l, part)], got)})

    for entry in started[:-1]:
        receive(*entry, after=grad_x)
    grads = [grads0, grads1]
    stacked = lambda n: jnp.stack([recv[(0, n)], recv[(1, n)]], axis=1)

    small = jnp.concatenate([
        _pack_small(lambda name, l: grads[l][SMALL_GRAD_KEY[name]]),
        jnp.stack([g["conv"] for g in grads]).reshape(CONV_ROWS, LANES),
        loss_tile], axis=0)
    small = _small_all_reduce(small)
    loss = small[SMALL_ROWS + CONV_ROWS, 0]
    my = _lin(_my_coords())
    conv_g = lax.dynamic_slice_in_dim(small[SMALL_ROWS:SMALL_ROWS + CONV_ROWS].reshape(DEPTH, 3, GROUP_WIDTH),
                                      my * 64, 64, axis=2)

    out = {}

    def big(name, recv, rows, cols, tr):
        res = _adamw(recv.reshape(N_DEV, rows, cols), weights[name].reshape(rows, cols),
                     mom_m[name].reshape(rows, cols), mom_v[name].reshape(rows, cols), "adamw_" + name, tr)
        out[name] = [r.reshape(weights[name].shape) for r in res]

    big("w_out", stacked("w_out"), DEPTH * D_MIX // N_DEV, D_MODEL, 192)
    big("mla_w_qb", stacked("w_qb"), DEPTH * MLA_Q_LORA, MLA_QK, 512)
    big("mla_w_kvb", stacked("w_kvb"), DEPTH * MLA_KV_LORA, 128, 256)
    receive(*started[-1], after=out["w_out"][1])
    big("w_in", stacked("w_in"), DEPTH * D_MODEL, IN_COLS // N_DEV, 256)

    pad_conv = lambda a: jnp.pad(a.reshape(-1), (0, 8 * LANES - 6 * 64)).reshape(8, LANES)
    cat = lambda src: jnp.concatenate([_pack_small(lambda name, l: src[name][l]), pad_conv(src["conv_w"])], axis=0)
    g_small = jnp.concatenate([small[:SMALL_ROWS], pad_conv(conv_g)], axis=0)
    res = _adamw(g_small[None], cat(weights), cat(mom_m), cat(mom_v), "adamw_small", SMALL_ROWS + 8)
    smalls = [_unpack_small(r[:SMALL_ROWS]) for r in res]
    for name, _ in SMALL:
        out[name] = [s[name] for s in smalls]
    out["conv_w"] = [r[SMALL_ROWS:].reshape(-1)[:6 * 64].reshape(DEPTH, 3, 64) for r in res]

    order = ["norm_g", "w_in", "mla_q_a_norm", "mla_w_qb", "mla_kv_a_norm", "mla_w_kvb", "mla_q_norm", "mla_k_norm",
             "conv_w", "swa_q_norm", "swa_k_norm", "swa_sinks", "w_out"]
    result = [loss, grad_x[None]]
    for idx in range(4):
        result += [out[name][idx] for name in order]
    return tuple(result)
```

```python
import functools

import jax
import jax.numpy as jnp
import numpy as np
from jax import lax
from jax.experimental import pallas as pl
from jax.experimental.pallas import tpu as pltpu

F32 = jnp.float32
MXU_DTYPE = jnp.bfloat16
WIRE_DTYPE = jnp.bfloat16

N_DEV = 8
DEPTH = 2
D_MODEL = 1024
GROUP_WIDTH = 512
D_MIX = 3 * GROUP_WIDTH
BLOCK = 128
RMS_EPS = 1e-6
NEG_INF = -1e30
HEADS = 8
MLA_QK = 96
MLA_NOPE = 64
MLA_ROPE = 32
MLA_Q_LORA = 256
MLA_KV_LORA = 128
ROPE_THETA = 10000.0
SWA_HEAD_DIM = 64
LANES = 128
IN_COLS = 4256

ADAM_LR = 0.001
ADAM_B1 = 0.9
ADAM_B2 = 0.999
ADAM_EPS = 1e-08
ADAM_WD = 0.01
ADAM_STEP = 10

NP = 4352
CB_GMLA, CB_CH, CB_CB, CB_CC, CB_GCONV, CB_GSWA, CB_SQ = 0, 1, 2, 3, 4, 5, 7
CB_QLAT = 12
CB_KVLAT, CB_KROPE = 26, 27
CB_SK, CB_SV = 32, 33
DPB_MIX, DPB_MLA, DPB_SQ, DPB_SKV = 0, 6, 7, 16

TM_PROJ = 512
TM_ROW = 256
TK = 256
TQ = 2 * TK
MLA_SCALE = MLA_QK ** -0.5
MLA_ONES_ROW = (64, 0)
LOG2E = 1.4426950408889634
LN2 = 0.6931471805599453
TM_SWA = 512
VMEM_MB = 2 ** 20


def _cp(sem, vmem_mb):
    return pltpu.CompilerParams(dimension_semantics=sem, vmem_limit_bytes=vmem_mb * VMEM_MB)


def _sds(shape, dtype):
    return jax.ShapeDtypeStruct(shape, dtype)


def _dot(a, b):
    return jnp.dot(a, b, preferred_element_type=F32)


def _dot_nt(a, b):
    return lax.dot_general(a, b, (((1,), (1,)), ((), ())), preferred_element_type=F32)


def _dot_tn(a, b):
    return lax.dot_general(a, b, (((0,), (0,)), ((), ())), preferred_element_type=F32)


def _rms(x, n):
    r = lax.rsqrt(jnp.sum(x * x, axis=-1, keepdims=True) * (1.0 / n) + RMS_EPS)
    return x * r, r


def _rms_bwd(dy, xhat, r, w, n):
    g = dy * w
    return r * (g - xhat * (jnp.sum(g * xhat, axis=-1, keepdims=True) * (1.0 / n)))


def _rms_halves(x, half1):
    x2 = x * x
    s0 = jnp.sum(jnp.where(half1, 0.0, x2), axis=-1, keepdims=True)
    s1 = jnp.sum(jnp.where(half1, x2, 0.0), axis=-1, keepdims=True)
    r = jnp.where(half1, lax.rsqrt(s1 * (1.0 / 64) + RMS_EPS), lax.rsqrt(s0 * (1.0 / 64) + RMS_EPS))
    return x * r, r


def _rms_halves_bwd(dy, xhat, r, w, half1):
    g = dy * w
    t = g * xhat
    m0 = jnp.sum(jnp.where(half1, 0.0, t), axis=-1, keepdims=True) * (1.0 / 64)
    m1 = jnp.sum(jnp.where(half1, t, 0.0), axis=-1, keepdims=True) * (1.0 / 64)
    return r * (g - xhat * jnp.where(half1, m1, m0))


def _sigmoid(x):
    return 1.0 / (1.0 + jnp.exp(-x))


def _rope(x, c, s1, s2):
    ax = x.ndim - 1
    return x * c + pltpu.roll(x, 112, ax) * s1 + pltpu.roll(x, 16, ax) * s2


def _rope_bwd(dy, c, s1, s2):
    ax = dy.ndim - 1
    return dy * c + pltpu.roll(dy * s1, 16, ax) + pltpu.roll(dy * s2, 112, ax)


def _fold_rows8(x):
    return jnp.sum(x.reshape(x.shape[0] // 8, 8, x.shape[1]), axis=0)


def _row0(v, rows=8):
    row = lax.broadcasted_iota(jnp.int32, (rows, v.shape[1]), 0)
    return jnp.where(row == 0, jnp.broadcast_to(v, (rows, v.shape[1])), 0.0)


def _mm_nn(a, b, name, out_dtype=F32, residual=None, tm=TM_PROJ):
    M, K = a.shape
    N = b.shape[1]
    tm = min(tm, M)

    def body(*refs):
        if residual is None:
            a_ref, b_ref, o_ref = refs
            acc = _dot(a_ref[...].astype(MXU_DTYPE), b_ref[...])
        else:
            a_ref, b_ref, r_ref, o_ref = refs
            acc = _dot(a_ref[...].astype(MXU_DTYPE), b_ref[...]) + r_ref[...]
        o_ref[...] = acc.astype(out_dtype)

    in_specs = [pl.BlockSpec((tm, K), lambda i: (i, 0)), pl.BlockSpec((K, N), lambda i: (0, 0))]
    args = [a, b]
    if residual is not None:
        in_specs.append(pl.BlockSpec((tm, N), lambda i: (i, 0)))
        args.append(residual)
    return pl.pallas_call(
        body, name=name, grid=(M // tm,), in_specs=in_specs,
        out_specs=pl.BlockSpec((tm, N), lambda i: (i, 0)), out_shape=_sds((M, N), out_dtype),
        compiler_params=_cp(("parallel",), 48))(*args)


def _mm_tn(a, b, name, out_dtype, tn, tk=512):
    T, M = a.shape
    N = b.shape[1]
    tk = min(tk, T)
    nk = T // tk

    def body(a_ref, b_ref, o_ref, acc_ref):
        k = pl.program_id(1)

        @pl.when(k == 0)
        def _():
            acc_ref[...] = jnp.zeros_like(acc_ref)

        acc_ref[...] += _dot_tn(a_ref[...].astype(MXU_DTYPE), b_ref[...].astype(MXU_DTYPE))

        @pl.when(k == nk - 1)
        def _():
            o_ref[...] = acc_ref[...].astype(out_dtype)

    return pl.pallas_call(
        body, name=name, grid=(N // tn, nk),
        in_specs=[pl.BlockSpec((tk, M), lambda n, k: (k, 0)), pl.BlockSpec((tk, tn), lambda n, k: (k, n))],
        out_specs=pl.BlockSpec((M, tn), lambda n, k: (0, n)), out_shape=_sds((M, N), out_dtype),
        scratch_shapes=[pltpu.VMEM((M, tn), F32)],
        compiler_params=_cp(("parallel", "arbitrary"), 48))(a, b)


def _inproj_fwd(x, ng, wp):
    T, D = x.shape
    tm = min(TM_PROJ, T)

    def body(x_ref, g_ref, w_ref, proj_ref, h_ref):
        xhat, _ = _rms(x_ref[...], D)
        h = (xhat * g_ref[...]).astype(MXU_DTYPE)
        h_ref[...] = h
        proj_ref[...] = _dot(h, w_ref[...])

    return pl.pallas_call(
        body, name="inproj_fwd", grid=(T // tm,),
        in_specs=[pl.BlockSpec((tm, D), lambda i: (i, 0)), pl.BlockSpec((1, D), lambda i: (0, 0)),
                  pl.BlockSpec((D, NP), lambda i: (0, 0))],
        out_specs=[pl.BlockSpec((tm, NP), lambda i: (i, 0)), pl.BlockSpec((tm, D), lambda i: (i, 0))],
        out_shape=[_sds((T, NP), F32), _sds((T, D), MXU_DTYPE)],
        compiler_params=_cp(("parallel",), 48))(x, ng, wp)


def _mla_prep_fwd(proj, lw, rope):
    T = proj.shape[0]
    tk = min(TK, T // 2)
    nsub = 2
    tm = nsub * tk

    def body(ql_ref, kvl_ref, kr_ref, qa_ref, kva_ref, wq_ref, wk_ref, wv_ref, qn_ref, kn_ref,
             c_ref, s1_ref, s2_ref, q_out, k_out, kt_out, vt_out):
        c, s1, s2 = c_ref[...], s1_ref[...], s2_ref[...]
        qhat, _ = _rms(ql_ref[...], MLA_Q_LORA)
        qn = (qhat * qa_ref[...]).astype(MXU_DTYPE)
        khat, _ = _rms(kvl_ref[...], MLA_KV_LORA)
        kvn = (khat * kva_ref[...]).astype(MXU_DTYPE)
        kr = kr_ref[...]
        half1 = lax.broadcasted_iota(jnp.int32, (tm, LANES), 1) >= 64
        ones_row = lax.broadcasted_iota(jnp.int32, (LANES, 1), 0)
        q3, _ = _rms(jnp.stack([_dot(qn, wq_ref[h]) for h in range(HEADS)]), MLA_QK)
        q_out[...] = (_rope(q3 * qn_ref[...], c, s1, s2) * (MLA_SCALE * LOG2E)).astype(MXU_DTYPE)
        k3, _ = _rms(jnp.stack([_dot(kvn, wk_ref[h]) for h in range(HEADS)]) + kr, MLA_QK)
        k3 = _rope(k3 * kn_ref[...], c, s1, s2)
        k_out[...] = k3.astype(MXU_DTYPE)
        for h in range(HEADS):
            for t in range(nsub):
                kt_out[h, t] = k3[h, tk * t:tk * (t + 1)].T.astype(MXU_DTYPE)
        v = _dot(kvn, wv_ref[...])
        for h in range(HEADS):
            vp = v[:, LANES * (h // 2):LANES * (h // 2 + 1)]
            own = half1 if h % 2 else jnp.logical_not(half1)
            vp = jnp.where(own, vp, 0.0)
            for t in range(nsub):
                vpt = vp[tk * t:tk * (t + 1)].T
                vt_out[h, t] = jnp.where(ones_row == MLA_ONES_ROW[h % 2], 1.0, vpt).astype(MXU_DTYPE)

    full = lambda shape: pl.BlockSpec(shape, lambda i: (0,) * len(shape))
    hd = pl.BlockSpec((HEADS, tm, LANES), lambda i: (0, i, 0))
    hdt = pl.BlockSpec((HEADS, nsub, LANES, tk), lambda i: (0, i, 0, 0))
    nat = _sds((HEADS, T, LANES), MXU_DTYPE)
    tr = _sds((HEADS, T // tk, LANES, tk), MXU_DTYPE)
    return pl.pallas_call(
        body, name="mla_prep_fwd", grid=(T // tm,),
        in_specs=[pl.BlockSpec((tm, 256), lambda i: (i, CB_QLAT)), pl.BlockSpec((tm, LANES), lambda i: (i, CB_KVLAT)),
                  pl.BlockSpec((tm, LANES), lambda i: (i, CB_KROPE)),
                  full((1, 256)), full((1, LANES)), full((HEADS, 256, LANES)), full((HEADS, LANES, LANES)),
                  full((LANES, 512)), full((1, LANES)), full((1, LANES)),
                  pl.BlockSpec((tm, LANES), lambda i: (i, 0)), pl.BlockSpec((tm, LANES), lambda i: (i, 0)),
                  pl.BlockSpec((tm, LANES), lambda i: (i, 0))],
        out_specs=[hd, hd, hdt, hdt],
        out_shape=[nat, nat, tr, tr],
        compiler_params=_cp(("parallel",), 32))(
            proj, proj, proj, lw["qa"], lw["kva"], lw["wq"], lw["wk"], lw["wv"], lw["qn"], lw["kn"],
            rope[0], rope[1], rope[2])


def _mla_attn_fwd(q, k, vt):
    T = q.shape[1]
    tk = min(TK, T // 2)
    tq = 2 * tk

    def body(q_ref, k_ref, vt_ref, o_ref, lse_ref, acc_s, m_s, s_a, s_b):
        i = pl.program_id(1)
        key = lax.broadcasted_iota(jnp.int32, (tk, tq), 0)
        qry = lax.broadcasted_iota(jnp.int32, (tk, tq), 1)
        qs = [q_ref[0], q_ref[1]]
        acc_s[...] = jnp.zeros_like(acc_s)
        m_s[...] = jnp.full(m_s.shape, NEG_INF, F32)

        def scores(kj, buf):
            rows = pl.ds(pl.multiple_of(kj * tk, tk), tk)
            for r in range(2):
                buf[r] = _dot_nt(k_ref[r, rows, :], qs[r])

        def consume(kj, buf, diag):
            for r in range(2):
                s = buf[r]
                if diag is not None:
                    s = jnp.where(key + diag * tk <= qry, s, NEG_INF)
                m_old = m_s[r]
                m_new = jnp.maximum(m_old, jnp.max(s, axis=0, keepdims=True))
                alpha = jnp.exp2(m_old - m_new)
                p = jnp.exp2(s - m_new)
                m_s[r] = m_new
                acc_s[r] = alpha * acc_s[r] + _dot(vt_ref[r, kj], p.astype(MXU_DTYPE))

        scores(0, s_a)

        def pair(kj):
            scores(kj + 1, s_b)
            consume(kj, s_a, None)
            scores(kj + 2, s_a)
            consume(kj + 1, s_b, None)

        def octet(ko, carry):
            for t in range(4):
                pair(8 * ko + 2 * t)
            return carry

        lax.fori_loop(0, i // 4, octet, 0)

        @pl.when(i % 4 >= 2)
        def _():
            pair(8 * (i // 4))
            pair(8 * (i // 4) + 2)

        @pl.when(i % 2 == 1)
        def _():
            pair(2 * i - 2)

        scores(2 * i + 1, s_b)
        consume(2 * i, s_a, 0)
        consume(2 * i + 1, s_b, 1)
        l = [acc_s[r, pl.ds(MLA_ONES_ROW[r], 1), :] for r in range(2)]
        head0 = lax.broadcasted_iota(jnp.int32, (LANES, 1), 0) < 64
        o_ref[...] = jnp.where(head0, acc_s[0] / l[0], acc_s[1] / l[1]).T
        for r in range(2):
            lse_ref[r] = m_s[r] + jnp.log2(l[r])

    return pl.pallas_call(
        body, name="mla_attn_fwd", grid=(HEADS // 2, T // tq),
        in_specs=[pl.BlockSpec((2, tq, LANES), lambda j, i: (j, i, 0)),
                  pl.BlockSpec((2, T, LANES), lambda j, i: (j, 0, 0)),
                  pl.BlockSpec((2, T // tk, LANES, tk), lambda j, i: (j, 0, 0, 0))],
        out_specs=[pl.BlockSpec((tq, LANES), lambda j, i: (i, j)),
                   pl.BlockSpec((2, 1, tq), lambda j, i: (j, 0, i))],
        out_shape=[_sds((T, GROUP_WIDTH), F32), _sds((HEADS, 1, T), F32)],
        scratch_shapes=[pltpu.VMEM((2, LANES, tq), F32), pltpu.VMEM((2, 1, tq), F32),
                        pltpu.VMEM((2, tk, tq), F32), pltpu.VMEM((2, tk, tq), F32)],
        compiler_params=_cp(("parallel", "arbitrary"), 40))(q, k, vt)


def _swa_kv_variants(x, half1):
    xs = pltpu.roll(x, 64, 1)
    out = {}
    for g in range(2):
        for r in range(2):
            own = half1 if r else jnp.logical_not(half1)
            out[(g, r)] = jnp.where(own, x if g == r else xs, 0.0).astype(MXU_DTYPE)
    return out


def _swa_alibi():
    ki = np.arange(2 * BLOCK)[:, None]
    qi = np.arange(BLOCK)[None, :]
    dist = BLOCK + qi - ki
    slopes = 2.0 ** -(np.arange(HEADS) + 1.0)
    tab = np.where(((dist >= 0) & (dist < BLOCK))[None], slopes[:, None, None] * dist[None], 1e30)
    return jnp.asarray(tab, F32)


def _swa_kv_variants_t(xt, rows1):
    xs = pltpu.roll(xt, 64, 0)
    out = {}
    for g in range(2):
        for r in range(2):
            own = rows1 if r else jnp.logical_not(rows1)
            out[(g, r)] = jnp.where(own, xt if g == r else xs, 0.0).astype(MXU_DTYPE)
    return out


def _swa_probs(i, nb, q_ref, k_ref, v_ref, pk_ref, pv_ref, qw_ref, kw_ref, alibi_ref, sink_ref):
    scale = SWA_HEAD_DIM ** -0.5
    half1 = lax.broadcasted_iota(jnp.int32, (1, LANES), 1) >= 64
    k_all = jnp.concatenate([pk_ref[...], k_ref[...]], axis=0)
    v_all = jnp.concatenate([pv_ref[...], v_ref[...]], axis=0)
    khat, _ = _rms_halves(k_all, half1)
    kn = khat * kw_ref[...]
    kp = _swa_kv_variants(kn, half1)
    qhat, qr, qn, qt = [], [], [], []
    for j in range(4):
        xh, r = _rms_halves(q_ref[:, LANES * j:LANES * (j + 1)], half1)
        qf = xh * qw_ref[...]
        qhat.append(xh)
        qr.append(r)
        qn.append(qf.astype(MXU_DTYPE))
        qt.append(qf.T.astype(MXU_DTYPE))
    key = lax.broadcasted_iota(jnp.int32, (2 * BLOCK, BLOCK), 0)
    first = jnp.where((i == 0) & (key < BLOCK), NEG_INF, 0.0)
    s = jnp.stack([_dot(kp[(h // 4, h % 2)][BLOCK * b:BLOCK * (b + 2)], qt[h // 2][:, BLOCK * b:BLOCK * (b + 1)])
                   for b in range(nb) for h in range(HEADS)]) * scale - alibi_ref[...]
    s = jnp.concatenate([s[:HEADS] + first, s[HEADS:]], axis=0) if nb > 1 else s + first
    sink = jnp.stack([jnp.full((1, 1), sink_ref[h], F32) for _ in range(nb) for h in range(HEADS)])
    m = jnp.maximum(jnp.max(s, axis=1, keepdims=True), sink)
    e = jnp.exp(s - m)
    es = jnp.exp(sink - m)
    inv = 1.0 / (jnp.sum(e, axis=1, keepdims=True) + es)
    return e * inv, es * inv, dict(half1=half1, kn=kn, kp=kp, v_all=v_all, qhat=qhat, qr=qr, qn=qn)


def _swa_fwd(proj, lw):
    T = proj.shape[0]
    tm = min(TM_SWA, T)
    nb = tm // BLOCK

    def body(q_ref, k_ref, v_ref, pk_ref, pv_ref, qw_ref, kw_ref, alibi_ref, sink_ref, o_ref):
        p, _, c = _swa_probs(pl.program_id(0), nb, q_ref, k_ref, v_ref, pk_ref, pv_ref, qw_ref, kw_ref, alibi_ref,
                             sink_ref)
        p = p.astype(MXU_DTYPE)
        rows1 = lax.broadcasted_iota(jnp.int32, (LANES, 1), 0) >= 64
        vpt = _swa_kv_variants_t(c["v_all"].T, rows1)
        for j in range(4):
            g = j // 2
            o_t = [_dot(vpt[(g, 0)][:, BLOCK * b:BLOCK * (b + 2)], p[HEADS * b + 2 * j])
                   + _dot(vpt[(g, 1)][:, BLOCK * b:BLOCK * (b + 2)], p[HEADS * b + 2 * j + 1]) for b in range(nb)]
            o_t = jnp.concatenate(o_t, axis=1) if nb > 1 else o_t[0]
            o_ref[:, LANES * j:LANES * (j + 1)] = o_t.T

    prev = lambda cb: pl.BlockSpec((BLOCK, LANES), lambda i: (jnp.maximum(i * nb - 1, 0), cb))
    return pl.pallas_call(
        body, name="swa_fwd", grid=(T // tm,),
        in_specs=[pl.BlockSpec((tm, 512), lambda i: (i, CB_SQ)), pl.BlockSpec((tm, LANES), lambda i: (i, CB_SK)),
                  pl.BlockSpec((tm, LANES), lambda i: (i, CB_SV)), prev(CB_SK), prev(CB_SV),
                  pl.BlockSpec((1, LANES), lambda i: (0, 0)), pl.BlockSpec((1, LANES), lambda i: (0, 0)),
                  pl.BlockSpec((nb * HEADS, 2 * BLOCK, BLOCK), lambda i: (0, 0, 0)),
                  pl.BlockSpec(memory_space=pltpu.SMEM)],
        out_specs=pl.BlockSpec((tm, 512), lambda i: (i, 0)),
        out_shape=_sds((T, GROUP_WIDTH), F32),
        compiler_params=_cp(("parallel",), 40))(
            proj, proj, proj, proj, proj, lw["sqn"], lw["skn"], jnp.tile(_swa_alibi(), (nb, 1, 1)), lw["sinks"])


def _shift_down(u, prev, n, row):
    tm = u.shape[0]
    out = pltpu.roll(u, n, 0)
    row8 = lax.broadcasted_iota(jnp.int32, prev.shape, 0)
    for t in range(n):
        src = jnp.sum(jnp.where(row8 == 8 - n + t, prev, 0.0), axis=0, keepdims=True)
        out = jnp.where(row == t, src, out)
    return out


def _shift_up(u, nxt, n, row):
    tm = u.shape[0]
    out = pltpu.roll(u, tm - n, 0)
    row8 = lax.broadcasted_iota(jnp.int32, nxt.shape, 0)
    for t in range(n):
        src = jnp.sum(jnp.where(row8 == t, nxt, 0.0), axis=0, keepdims=True)
        out = jnp.where(row == tm - n + t, src, out)
    return out


def _mix_fwd(proj, o_mla, o_swa, conv_w):
    T = proj.shape[0]
    tm = min(TM_ROW, T)

    def body(gm_ref, ch_ref, cb_ref, cc_ref, gc_ref, gs_ref, pch_ref, pcc_ref, om_ref, os_ref, w_ref, y_ref):
        i = pl.program_id(0)
        row = lax.broadcasted_iota(jnp.int32, (tm, GROUP_WIDTH), 0)
        u = cc_ref[...] * ch_ref[...]
        u_prev = jnp.where(i > 0, pcc_ref[...] * pch_ref[...], 0.0)
        z = (w_ref[0:1, :] * _shift_down(u, u_prev, 2, row) + w_ref[1:2, :] * _shift_down(u, u_prev, 1, row)
             + w_ref[2:3, :] * u)
        gm, gc, gs = gm_ref[...], gc_ref[...], gs_ref[...]
        y_ref[:, 0:512] = (om_ref[...] * (gm * _sigmoid(gm))).astype(MXU_DTYPE)
        y_ref[:, 512:1024] = (cb_ref[...] * z * (gc * _sigmoid(gc))).astype(MXU_DTYPE)
        y_ref[:, 1024:1536] = (os_ref[...] * (gs * _sigmoid(gs))).astype(MXU_DTYPE)

    blk = lambda cb: pl.BlockSpec((tm, 512), lambda i: (i, cb))
    prev = lambda cb: pl.BlockSpec((8, 512), lambda i: (jnp.maximum(i * (tm // 8) - 1, 0), cb))
    tile = pl.BlockSpec((tm, 512), lambda i: (i, 0))
    return pl.pallas_call(
        body, name="mix_fwd", grid=(T // tm,),
        in_specs=[blk(CB_GMLA), blk(CB_CH), blk(CB_CB), blk(CB_CC), blk(CB_GCONV), blk(CB_GSWA),
                  prev(CB_CH), prev(CB_CC), tile, tile, pl.BlockSpec((8, 512), lambda i: (0, 0))],
        out_specs=pl.BlockSpec((tm, D_MIX), lambda i: (i, 0)),
        out_shape=_sds((T, D_MIX), MXU_DTYPE),
        compiler_params=_cp(("parallel",), 32))(
            proj, proj, proj, proj, proj, proj, proj, proj, o_mla, o_swa, conv_w)


def _outproj_loss(ycat, wo, x, target):
    T, D = x.shape
    K = ycat.shape[1]
    tm = min(TM_PROJ, T)
    nt = T // tm

    def body(y_ref, w_ref, x_ref, t_ref, g_ref, loss_ref, acc_ref):
        i = pl.program_id(0)

        @pl.when(i == 0)
        def _():
            acc_ref[...] = jnp.zeros_like(acc_ref)

        err = _dot(y_ref[...], w_ref[...]) + x_ref[...] - t_ref[...]
        g_ref[...] = err * (1.0 / D)
        acc_ref[...] += _fold_rows8(err * err)

        @pl.when(i == nt - 1)
        def _():
            tot = jnp.sum(jnp.sum(acc_ref[...], axis=1, keepdims=True), axis=0, keepdims=True)
            loss_ref[...] = jnp.broadcast_to(tot * (0.5 / D), (8, LANES))

    tile = pl.BlockSpec((tm, D), lambda i: (i, 0))
    return pl.pallas_call(
        body, name="outproj_loss", grid=(nt,),
        in_specs=[pl.BlockSpec((tm, K), lambda i: (i, 0)), pl.BlockSpec((K, D), lambda i: (0, 0)), tile, tile],
        out_specs=[tile, pl.BlockSpec((8, LANES), lambda i: (0, 0))],
        out_shape=[_sds((T, D), F32), _sds((8, LANES), F32)],
        scratch_shapes=[pltpu.VMEM((8, D), F32)],
        compiler_params=_cp(("arbitrary",), 48))(ycat, wo, x, target)


def _outproj_bwd(g, ycat, wot):
    T, D = g.shape
    K = ycat.shape[1]
    tm = min(512, T)
    nt = T // tm

    def body(g_ref, y_ref, wt_ref, dy_ref, dw_ref, acc_ref):
        i = pl.program_id(0)

        @pl.when(i == 0)
        def _():
            acc_ref[...] = jnp.zeros_like(acc_ref)

        gb = g_ref[...].astype(MXU_DTYPE)
        dy_ref[...] = _dot(gb, wt_ref[...])
        acc_ref[...] += _dot_tn(y_ref[...], gb)

        @pl.when(i == nt - 1)
        def _():
            dw_ref[...] = acc_ref[...].astype(WIRE_DTYPE)

    return pl.pallas_call(
        body, name="outproj_bwd", grid=(nt,),
        in_specs=[pl.BlockSpec((tm, D), lambda i: (i, 0)), pl.BlockSpec((tm, K), lambda i: (i, 0)),
                  pl.BlockSpec((D, K), lambda i: (0, 0))],
        out_specs=[pl.BlockSpec((tm, K), lambda i: (i, 0)), pl.BlockSpec((K, D), lambda i: (0, 0))],
        out_shape=[_sds((T, K), F32), _sds((K, D), WIRE_DTYPE)],
        scratch_shapes=[pltpu.VMEM((K, D), F32)],
        compiler_params=_cp(("arbitrary",), 48))(g, ycat, wot)


def _mix_bwd(dycat, proj, o_mla, o_swa, conv_w):
    T = proj.shape[0]
    tm = min(TM_ROW, T)
    nt = T // tm

    def body(dym_ref, dyc_ref, dys_ref, gm_ref, ch_ref, cb_ref, cc_ref, gc_ref, gs_ref, pch_ref, pcc_ref,
             ndy_ref, ncb_ref, ngc_ref, om_ref, os_ref, w_ref,
             d1_ref, dom_ref, dos_ref, dw_ref):
        i = pl.program_id(0)

        @pl.when(i == 0)
        def _():
            dw_ref[...] = jnp.zeros_like(dw_ref)

        row = lax.broadcasted_iota(jnp.int32, (tm, GROUP_WIDTH), 0)

        def gate(g):
            sg = _sigmoid(g)
            return g * sg, sg * (1.0 + g * (1.0 - sg))

        gm = gm_ref[...]
        silu, dsilu = gate(gm)
        dym = dym_ref[...]
        dom_ref[...] = dym * silu
        d1_ref[:, 0:512] = (dym * om_ref[...] * dsilu).astype(MXU_DTYPE)

        gs = gs_ref[...]
        silu, dsilu = gate(gs)
        dys = dys_ref[...]
        dos_ref[...] = dys * silu
        d1_ref[:, 2560:3072] = (dys * os_ref[...] * dsilu).astype(MXU_DTYPE)

        ch, cb, cc, gc, dyc = ch_ref[...], cb_ref[...], cc_ref[...], gc_ref[...], dyc_ref[...]
        w0, w1, w2 = w_ref[0:1, :], w_ref[1:2, :], w_ref[2:3, :]
        u = cc * ch
        u_prev = jnp.where(i > 0, pcc_ref[...] * pch_ref[...], 0.0)
        u1 = _shift_down(u, u_prev, 1, row)
        u2 = _shift_down(u, u_prev, 2, row)
        z = w0 * u2 + w1 * u1 + w2 * u
        silu, dsilu = gate(gc)
        dz = dyc * cb * silu
        ngc = ngc_ref[...]
        dz_next = jnp.where(i < nt - 1, ndy_ref[...] * ncb_ref[...] * (ngc * _sigmoid(ngc)), 0.0)
        du = w2 * dz + w1 * _shift_up(dz, dz_next, 1, row) + w0 * _shift_up(dz, dz_next, 2, row)
        d1_ref[:, 512:1024] = (du * cc).astype(MXU_DTYPE)
        d1_ref[:, 1024:1536] = (dyc * z * silu).astype(MXU_DTYPE)
        d1_ref[:, 1536:2048] = (du * ch).astype(MXU_DTYPE)
        d1_ref[:, 2048:2560] = (dyc * cb * z * dsilu).astype(MXU_DTYPE)
        row8 = lax.broadcasted_iota(jnp.int32, (8, GROUP_WIDTH), 0)
        dw = jnp.zeros((8, GROUP_WIDTH), F32)
        for t, shifted in enumerate((u2, u1, u)):
            dw = dw + jnp.where(row8 == t, jnp.sum(dz * shifted, axis=0, keepdims=True), 0.0)
        dw_ref[...] += dw

    blk = lambda cb: pl.BlockSpec((tm, 512), lambda i: (i, cb))
    prev = lambda cb: pl.BlockSpec((8, 512), lambda i: (jnp.maximum(i * (tm // 8) - 1, 0), cb))
    nxt = lambda cb: pl.BlockSpec((8, 512), lambda i: (jnp.minimum((i + 1) * (tm // 8), T // 8 - 1), cb))
    tile = pl.BlockSpec((tm, 512), lambda i: (i, 0))
    return pl.pallas_call(
        body, name="mix_bwd", grid=(nt,),
        in_specs=[blk(0), blk(1), blk(2), blk(CB_GMLA), blk(CB_CH), blk(CB_CB), blk(CB_CC), blk(CB_GCONV),
                  blk(CB_GSWA), prev(CB_CH), prev(CB_CC), nxt(1), nxt(CB_CB), nxt(CB_GCONV), tile, tile,
                  pl.BlockSpec((8, 512), lambda i: (0, 0))],
        out_specs=[pl.BlockSpec((tm, 3072), lambda i: (i, DPB_MIX)), tile, tile,
                   pl.BlockSpec((8, 512), lambda i: (0, 0))],
        out_shape=[_sds((T, NP), MXU_DTYPE), _sds((T, 512), F32), _sds((T, 512), F32), _sds((8, 512), F32)],
        compiler_params=_cp(("arbitrary",), 48))(
            dycat, dycat, dycat, proj, proj, proj, proj, proj, proj, proj, proj, dycat, proj, proj,
            o_mla, o_swa, conv_w)


def _swa_bwd(proj, o_swa, do_swa, lw, dproj):
    T = proj.shape[0]
    tm = min(TM_SWA, T)
    nb = tm // BLOCK
    scale = SWA_HEAD_DIM ** -0.5

    def body(q_ref, k_ref, v_ref, pk_ref, pv_ref, o_ref, do_ref, qw_ref, kw_ref, alibi_ref, sink_ref, dproj_in,
             dq_ref, dk_ref, dv_ref, dqw_ref, dsink_ref):
        i = pl.program_id(0)

        @pl.when(i == 0)
        def _():
            dk_ref[...] = jnp.zeros_like(dk_ref)
            dv_ref[...] = jnp.zeros_like(dv_ref)
            dqw_ref[...] = jnp.zeros_like(dqw_ref)
            dsink_ref[...] = jnp.zeros_like(dsink_ref)

        p, p_sink, c = _swa_probs(i, nb, q_ref, k_ref, v_ref, pk_ref, pv_ref, qw_ref, kw_ref, alibi_ref, sink_ref)
        half1, kp, qn, qhat, qr = c["half1"], c["kp"], c["qn"], c["qhat"], c["qr"]
        rows1 = lax.broadcasted_iota(jnp.int32, (LANES, 1), 0) >= 64
        kpt = _swa_kv_variants_t(c["kn"].T, rows1)
        vp = _swa_kv_variants(c["v_all"], half1)
        qw = qw_ref[...]
        rows = [slice(BLOCK * b, BLOCK * (b + 1)) for b in range(nb)]
        keys = [slice(BLOCK * b, BLOCK * (b + 2)) for b in range(nb)]
        dob, dot_b, dd0, dd1 = [], [], [], []
        for j in range(4):
            cols = slice(LANES * j, LANES * (j + 1))
            do = do_ref[:, cols]
            do_t = do.T
            prod_t = do_t * o_ref[:, cols].T
            dob.append(do.astype(MXU_DTYPE))
            dot_b.append(do_t.astype(MXU_DTYPE))
            dd0.append(jnp.sum(jnp.where(rows1, 0.0, prod_t), axis=0, keepdims=True))
            dd1.append(jnp.sum(jnp.where(rows1, prod_t, 0.0), axis=0, keepdims=True))
        dd = jnp.stack([(dd1 if h % 2 else dd0)[h // 2][:, rows[b]] for b in range(nb) for h in range(HEADS)])
        dp = jnp.stack([_dot(vp[(h // 4, h % 2)][keys[b]], dot_b[h // 2][:, rows[b]])
                        for b in range(nb) for h in range(HEADS)])
        ds = (p * (dp - dd) * scale).astype(MXU_DTYPE)
        dsink = -jnp.sum(p_sink * dd, axis=2, keepdims=True)
        pb = p.astype(MXU_DTYPE)

        dqw = jnp.zeros((1, LANES), F32)
        for j in range(4):
            g = j // 2
            dqn_t = [_dot(kpt[(g, 0)][:, keys[b]], ds[HEADS * b + 2 * j])
                     + _dot(kpt[(g, 1)][:, keys[b]], ds[HEADS * b + 2 * j + 1]) for b in range(nb)]
            dqn = (jnp.concatenate(dqn_t, axis=1) if nb > 1 else dqn_t[0]).T
            dqw = dqw + jnp.sum(dqn * qhat[j], axis=0, keepdims=True)
            dq_ref[:, LANES * j:LANES * (j + 1)] = _rms_halves_bwd(dqn, qhat[j], qr[j], qw, half1).astype(MXU_DTYPE)
        dqw_ref[...] += _row0(dqw + pltpu.roll(dqw, 64, 1))

        dk_tot = jnp.zeros((tm + BLOCK, LANES), F32)
        dv_tot = jnp.zeros((tm + BLOCK, LANES), F32)
        for b in range(nb):
            dk_b = jnp.zeros((2 * BLOCK, LANES), F32)
            dv_b = jnp.zeros((2 * BLOCK, LANES), F32)
            for g in range(2):
                for r in range(2):
                    own = half1 if r else jnp.logical_not(half1)
                    ha, hb = HEADS * b + 4 * g + r, HEADS * b + 4 * g + 2 + r
                    qa, qb = qn[2 * g][rows[b]], qn[2 * g + 1][rows[b]]
                    da, db = dob[2 * g][rows[b]], dob[2 * g + 1][rows[b]]
                    dkp = jnp.where(own, _dot(ds[ha], qa) + _dot(ds[hb], qb), 0.0)
                    dvp = jnp.where(own, _dot(pb[ha], da) + _dot(pb[hb], db), 0.0)
                    if g != r:
                        dkp = pltpu.roll(dkp, 64, 1)
                        dvp = pltpu.roll(dvp, 64, 1)
                    dk_b = dk_b + dkp
                    dv_b = dv_b + dvp
            pad = lambda x: jnp.concatenate(
                [z for z in (jnp.zeros((BLOCK * b, LANES), F32), x, jnp.zeros((BLOCK * (nb - 1 - b), LANES), F32))
                 if z.shape[0]], axis=0)
            dk_tot = dk_tot + pad(dk_b)
            dv_tot = dv_tot + pad(dv_b)
        dst = pl.ds(pl.multiple_of(i * tm, BLOCK), tm + BLOCK)
        dk_ref[dst, :] += dk_tot
        dv_ref[dst, :] += dv_tot

        row8 = lax.broadcasted_iota(jnp.int32, (8, LANES), 0)
        dsink_tile = jnp.zeros((8, LANES), F32)
        for b in range(nb):
            for h in range(HEADS):
                dsink_tile = dsink_tile + jnp.where(row8 == h, jnp.broadcast_to(dsink[HEADS * b + h], (8, LANES)), 0.0)
        dsink_ref[...] += dsink_tile

    prev = lambda cb: pl.BlockSpec((BLOCK, LANES), lambda i: (jnp.maximum(i * nb - 1, 0), cb))
    tile = pl.BlockSpec((tm, 512), lambda i: (i, 0))
    small = pl.BlockSpec((8, LANES), lambda i: (0, 0))
    acc = pl.BlockSpec((T + BLOCK, LANES), lambda i: (0, 0))
    return pl.pallas_call(
        body, name="swa_bwd", grid=(T // tm,),
        in_specs=[pl.BlockSpec((tm, 512), lambda i: (i, CB_SQ)), pl.BlockSpec((tm, LANES), lambda i: (i, CB_SK)),
                  pl.BlockSpec((tm, LANES), lambda i: (i, CB_SV)), prev(CB_SK), prev(CB_SV), tile, tile,
                  pl.BlockSpec((1, LANES), lambda i: (0, 0)), pl.BlockSpec((1, LANES), lambda i: (0, 0)),
                  pl.BlockSpec((nb * HEADS, 2 * BLOCK, BLOCK), lambda i: (0, 0, 0)),
                  pl.BlockSpec(memory_space=pltpu.SMEM), pl.BlockSpec(memory_space=pl.ANY)],
        out_specs=[pl.BlockSpec((tm, 512), lambda i: (i, DPB_SQ)), acc, acc, small, small],
        out_shape=[_sds((T, NP), MXU_DTYPE), _sds((T + BLOCK, LANES), F32), _sds((T + BLOCK, LANES), F32),
                   _sds((8, LANES), F32), _sds((8, LANES), F32)],
        input_output_aliases={11: 0},
        compiler_params=_cp(("arbitrary",), 48))(
            proj, proj, proj, proj, proj, o_swa, do_swa, lw["sqn"], lw["skn"], jnp.tile(_swa_alibi(), (nb, 1, 1)),
            lw["sinks"], dproj)


def _swa_kv_bwd(proj, dkn, dv, lw, dproj):
    T = proj.shape[0]
    tm = min(TM_SWA, T)
    dkn, dv = dkn[BLOCK:], dv[BLOCK:]

    def body(k_ref, dkn_ref, dv_ref, kw_ref, dproj_in, d_ref, dkw_ref):
        i = pl.program_id(0)

        @pl.when(i == 0)
        def _():
            dkw_ref[...] = jnp.zeros_like(dkw_ref)

        half1 = lax.broadcasted_iota(jnp.int32, (1, LANES), 1) >= 64
        khat, kr = _rms_halves(k_ref[...], half1)
        dkn_t = dkn_ref[...]
        dkw = jnp.sum(dkn_t * khat, axis=0, keepdims=True)
        dkw_ref[...] += _row0(dkw + pltpu.roll(dkw, 64, 1))
        d_ref[:, 0:LANES] = _rms_halves_bwd(dkn_t, khat, kr, kw_ref[...], half1).astype(MXU_DTYPE)
        d_ref[:, LANES:2 * LANES] = dv_ref[...].astype(MXU_DTYPE)

    return pl.pallas_call(
        body, name="swa_kv_bwd", grid=(T // tm,),
        in_specs=[pl.BlockSpec((tm, LANES), lambda i: (i, CB_SK)), pl.BlockSpec((tm, LANES), lambda i: (i, 0)),
                  pl.BlockSpec((tm, LANES), lambda i: (i, 0)), pl.BlockSpec((1, LANES), lambda i: (0, 0)),
                  pl.BlockSpec(memory_space=pl.ANY)],
        out_specs=[pl.BlockSpec((tm, 2 * LANES), lambda i: (i, DPB_SKV)), pl.BlockSpec((8, LANES), lambda i: (0, 0))],
        out_shape=[_sds((T, NP), MXU_DTYPE), _sds((8, LANES), F32)],
        input_output_aliases={4: 0},
        compiler_params=_cp(("arbitrary",), 32))(proj, dkn, dv, lw["skn"], dproj)


def _mla_attn_bwd(q, k, kt, vt, o, do, lse):
    T = q.shape[1]
    tk = min(TK, T // 2)
    tq = 2 * tk

    def body(q_ref, k_ref, kt_ref, vt_ref, o_ref, do_ref, lse_ref, dq_ref, dk_ref, dv_ref, dq_s, lse_s, dd_s,
             s_a, s_b, p_a, p_b):
        h = pl.program_id(0)
        i = pl.program_id(1)

        @pl.when(i == 0)
        def _():
            dk_ref[...] = jnp.zeros_like(dk_ref)
            dv_ref[...] = jnp.zeros_like(dv_ref)

        qry = lax.broadcasted_iota(jnp.int32, (tq, tk), 0)
        key = lax.broadcasted_iota(jnp.int32, (tq, tk), 1)
        own = (lax.broadcasted_iota(jnp.int32, (1, LANES), 1) // 64) == (h % 2)
        do_own = jnp.where(own, do_ref[...], 0.0)
        dob = do_own.astype(MXU_DTYPE)
        dob_t = do_own.T.astype(MXU_DTYPE)
        qh = q_ref[0]
        qh_t = qh.astype(F32).T.astype(MXU_DTYPE)
        dd_col = jnp.sum(do_own * o_ref[...], axis=-1, keepdims=True)
        lse_col = jnp.broadcast_to(lse_ref[0], (LANES, tq)).T
        for c in range(tk // LANES):
            lse_s[:, LANES * c:LANES * (c + 1)] = lse_col
            dd_s[:, LANES * c:LANES * (c + 1)] = jnp.broadcast_to(dd_col, (tq, LANES))
        dq_s[...] = jnp.zeros_like(dq_s)

        def scores(kj, s_buf, p_buf):
            s_buf[...] = _dot(qh, kt_ref[0, kj])
            p_buf[...] = _dot(dob, vt_ref[0, kj])

        def consume(kj, s_buf, p_buf, diag):
            rows = pl.ds(pl.multiple_of(kj * tk, tk), tk)
            s = s_buf[...]
            if diag is not None:
                s = jnp.where(key + diag * tk <= qry, s, NEG_INF)
            p = jnp.exp2(s - lse_s[...])
            ds = (p * (p_buf[...] - dd_s[...])).astype(MXU_DTYPE)
            dq_s[...] += _dot(ds, k_ref[0, rows, :])
            dk_ref[0, kj] += _dot(qh_t, ds)
            dv_ref[0, kj] += _dot(dob_t, p.astype(MXU_DTYPE))

        scores(0, s_a, p_a)

        def pair(kj):
            scores(kj + 1, s_b, p_b)
            consume(kj, s_a, p_a, None)
            scores(kj + 2, s_a, p_a)
            consume(kj + 1, s_b, p_b, None)

        def octet(ko, carry):
            for t in range(4):
                pair(8 * ko + 2 * t)
            return carry

        lax.fori_loop(0, i // 4, octet, 0)

        @pl.when(i % 4 >= 2)
        def _():
            pair(8 * (i // 4))
            pair(8 * (i // 4) + 2)

        @pl.when(i % 2 == 1)
        def _():
            pair(2 * i - 2)

        scores(2 * i + 1, s_b, p_b)
        consume(2 * i, s_a, p_a, 0)
        consume(2 * i + 1, s_b, p_b, 1)
        dq_ref[0] = dq_s[...]

    res = pl.BlockSpec((1, T, LANES), lambda h, i: (h, 0, 0))
    res_t = pl.BlockSpec((1, T // tk, LANES, tk), lambda h, i: (h, 0, 0, 0))
    buf = pltpu.VMEM((tq, tk), F32)
    acc_t = _sds((HEADS, T // tk, LANES, tk), F32)
    return pl.pallas_call(
        body, name="mla_attn_bwd", grid=(HEADS, T // tq),
        in_specs=[pl.BlockSpec((1, tq, LANES), lambda h, i: (h, i, 0)), res, res_t, res_t,
                  pl.BlockSpec((tq, LANES), lambda h, i: (i, h // 2)),
                  pl.BlockSpec((tq, LANES), lambda h, i: (i, h // 2)),
                  pl.BlockSpec((1, 1, tq), lambda h, i: (h, 0, i))],
        out_specs=[pl.BlockSpec((1, tq, LANES), lambda h, i: (h, i, 0)), res_t, res_t],
        out_shape=[_sds((HEADS, T, LANES), F32), acc_t, acc_t],
        scratch_shapes=[pltpu.VMEM((tq, LANES), F32), buf, buf, buf, buf, buf, buf],
        compiler_params=_cp(("parallel", "arbitrary"), 48))(q, k, kt, vt, o, do, lse)


def _mla_prep_bwd(proj, dq, dk, dv, lw, rope, dproj):
    T = proj.shape[0]
    tm = min(TK, T // 2)

    def body(ql_ref, kvl_ref, kr_ref, dq_ref, dk_ref, dv_ref, qa_ref, kva_ref, wq_ref, wk_ref, wv_ref,
             wqt_ref, wkt_ref, wvt_ref, qn_ref, kn_ref, c_ref, s1_ref, s2_ref, dproj_in,
             d_ref, dwq_ref, dwk_ref, dwv_ref, dqa_ref, dkva_ref, dqn_ref, dkn_ref):
        i = pl.program_id(0)

        @pl.when(i == 0)
        def _():
            for ref in (dwq_ref, dwk_ref, dwv_ref, dqa_ref, dkva_ref, dqn_ref, dkn_ref):
                ref[...] = jnp.zeros_like(ref)

        c, s1, s2 = c_ref[...], s1_ref[...], s2_ref[...]
        lane = lax.broadcasted_iota(jnp.int32, (1, LANES), 1)
        qlhat, qlr = _rms(ql_ref[...], MLA_Q_LORA)
        qn = (qlhat * qa_ref[...]).astype(MXU_DTYPE)
        kvhat, kvr = _rms(kvl_ref[...], MLA_KV_LORA)
        kvn = (kvhat * kva_ref[...]).astype(MXU_DTYPE)
        kr = kr_ref[...]
        x3, r3 = _rms(jnp.stack([_dot(qn, wq_ref[h]) for h in range(HEADS)]), MLA_QK)
        dy3 = _rope_bwd(dq_ref[...] * MLA_SCALE, c, s1, s2)
        dqw = jnp.sum(jnp.sum(dy3 * x3, axis=0), axis=0, keepdims=True)
        dx3 = _rms_bwd(dy3, x3, r3, qn_ref[...], MLA_QK).astype(MXU_DTYPE)
        dqnl = jnp.zeros((tm, MLA_Q_LORA), F32)
        for h in range(HEADS):
            dwq_ref[h] += _dot_tn(qn, dx3[h])
            dqnl = dqnl + _dot(dx3[h], wqt_ref[h])

        x3, r3 = _rms(jnp.stack([_dot(kvn, wk_ref[h]) for h in range(HEADS)]) + kr, MLA_QK)
        dy3 = _rope_bwd(jnp.stack([dk_ref[h, 0].T for h in range(HEADS)]) * LN2, c, s1, s2)
        dkw = jnp.sum(jnp.sum(dy3 * x3, axis=0), axis=0, keepdims=True)
        dxf3 = _rms_bwd(dy3, x3, r3, kn_ref[...], MLA_QK)
        dkr = jnp.sum(dxf3, axis=0)
        dx3 = dxf3.astype(MXU_DTYPE)
        dkvn = jnp.zeros((tm, MLA_KV_LORA), F32)
        for h in range(HEADS):
            dwk_ref[h] += _dot_tn(kvn, dx3[h])
            dkvn = dkvn + _dot(dx3[h], wkt_ref[h])
        dvc = jnp.concatenate([(dv_ref[2 * j, 0] + dv_ref[2 * j + 1, 0]).T for j in range(4)],
                              axis=1).astype(MXU_DTYPE)
        dwv_ref[...] += _dot_tn(kvn, dvc)
        dkvn = dkvn + _dot(dvc, wvt_ref[...])
        dqa_ref[...] += _row0(jnp.sum(dqnl * qlhat, axis=0, keepdims=True))
        dkva_ref[...] += _row0(jnp.sum(dkvn * kvhat, axis=0, keepdims=True))
        dqn_ref[...] += _row0(dqw)
        dkn_ref[...] += _row0(dkw)
        d_ref[:, 0:256] = _rms_bwd(dqnl, qlhat, qlr, qa_ref[...], MLA_Q_LORA).astype(MXU_DTYPE)
        d_ref[:, 256:384] = _rms_bwd(dkvn, kvhat, kvr, kva_ref[...], MLA_KV_LORA).astype(MXU_DTYPE)
        d_ref[:, 384:512] = jnp.where((lane >= 64) & (lane < 96), dkr, 0.0).astype(MXU_DTYPE)

    full = lambda shape: pl.BlockSpec(shape, lambda i: (0,) * len(shape))
    hd = pl.BlockSpec((HEADS, tm, LANES), lambda i: (0, i, 0))
    hdt = pl.BlockSpec((HEADS, 1, LANES, tm), lambda i: (0, i, 0, 0))
    tab = pl.BlockSpec((tm, LANES), lambda i: (i, 0))
    return pl.pallas_call(
        body, name="mla_prep_bwd", grid=(T // tm,),
        in_specs=[pl.BlockSpec((tm, 256), lambda i: (i, CB_QLAT)), pl.BlockSpec((tm, LANES), lambda i: (i, CB_KVLAT)),
                  pl.BlockSpec((tm, LANES), lambda i: (i, CB_KROPE)), hd, hdt, hdt,
                  full((1, 256)), full((1, LANES)), full((HEADS, 256, LANES)), full((HEADS, LANES, LANES)),
                  full((LANES, 512)), full((HEADS, LANES, 256)), full((HEADS, LANES, LANES)), full((512, LANES)),
                  full((1, LANES)), full((1, LANES)), tab, tab, tab, pl.BlockSpec(memory_space=pl.ANY)],
        out_specs=[pl.BlockSpec((tm, 512), lambda i: (i, DPB_MLA)), full((HEADS, 256, LANES)),
                   full((HEADS, LANES, LANES)), full((LANES, 512)), full((8, 256)), full((8, LANES)),
                   full((8, LANES)), full((8, LANES))],
        out_shape=[_sds((T, NP), MXU_DTYPE), _sds((HEADS, 256, LANES), F32), _sds((HEADS, LANES, LANES), F32),
                   _sds((LANES, 512), F32), _sds((8, 256), F32), _sds((8, LANES), F32), _sds((8, LANES), F32),
                   _sds((8, LANES), F32)],
        input_output_aliases={19: 0},
        compiler_params=_cp(("arbitrary",), 48))(
            proj, proj, proj, dq, dk, dv, lw["qa"], lw["kva"], lw["wq"], lw["wk"], lw["wv"],
            lw["wqt"], lw["wkt"], lw["wvt"], lw["qn"], lw["kn"], rope[0], rope[1], rope[2], dproj)


def _inproj_bwd_dx(dproj, wpt, x, g_in, ng):
    T, D = x.shape
    tm = min(TM_PROJ, T)

    def body(dp_ref, wt_ref, x_ref, g_ref, w_ref, dx_ref, dw_ref):
        i = pl.program_id(0)

        @pl.when(i == 0)
        def _():
            dw_ref[...] = jnp.zeros_like(dw_ref)

        dh = _dot(dp_ref[...], wt_ref[...])
        xhat, r = _rms(x_ref[...], D)
        dw_ref[...] += _row0(jnp.sum(dh * xhat, axis=0, keepdims=True))
        dx_ref[...] = g_ref[...] + _rms_bwd(dh, xhat, r, w_ref[...], D)

    tile = pl.BlockSpec((tm, D), lambda i: (i, 0))
    return pl.pallas_call(
        body, name="inproj_bwd_dx", grid=(T // tm,),
        in_specs=[pl.BlockSpec((tm, NP), lambda i: (i, 0)), pl.BlockSpec((NP, D), lambda i: (0, 0)), tile, tile,
                  pl.BlockSpec((1, D), lambda i: (0, 0))],
        out_specs=[tile, pl.BlockSpec((8, D), lambda i: (0, 0))],
        out_shape=[_sds((T, D), F32), _sds((8, D), F32)],
        compiler_params=_cp(("arbitrary",), 48))(dproj, wpt, x, g_in, ng)


def _rope_tables(T, token=0.0):
    half = MLA_ROPE // 2
    inv_freq = jnp.power(jnp.float32(ROPE_THETA), -jnp.arange(half, dtype=F32) / half)
    z = lambda n: jnp.zeros((n,), F32)
    freq = jnp.concatenate([z(MLA_NOPE), inv_freq, inv_freq, z(32)])
    first = jnp.concatenate([z(64), jnp.ones((16,), F32), z(48)])
    second = jnp.concatenate([z(80), jnp.ones((16,), F32), z(32)])
    ang = (jnp.arange(T, dtype=F32) + token)[:, None] * freq[None, :]
    sin = jnp.sin(ang)
    return jnp.cos(ang), -sin * first[None, :], sin * second[None, :]


def _pad_lanes(v, n=LANES):
    v = v.reshape(1, -1)
    return jnp.pad(v, ((0, 0), (0, n - v.shape[1])))


def _pack_win_t(wt):
    z = lambda n: jnp.zeros((n, wt.shape[1]), wt.dtype)
    return jnp.concatenate([wt[416:2976], wt[3744:4256], wt[0:384], z(64), wt[384:416], z(32), wt[2976:3488],
                            wt[3488:3616], wt[3616:3744]], axis=0)


def _unpack_dwin(d):
    return jnp.concatenate([d[:, 3072:3456], d[:, 3520:3552], d[:, 0:2560], d[:, 3584:4096], d[:, 4096:4224],
                            d[:, 4224:4352], d[:, 2560:3072]], axis=1)


def _inproj_weights(l, norm_g, w_in_t):
    wpt = _pack_win_t(w_in_t)
    return dict(ng=norm_g[l].reshape(1, -1), wp=wpt.T, wpt=wpt)


def _mixer_weights(l, qa, wqb_full, kva, wkvb_full, qn, kn, conv_full, sqn, skn, sinks, w_out_full):
    wq = jnp.pad(wqb_full, ((0, 0), (0, 0), (0, LANES - MLA_QK)))
    wk = jnp.pad(wkvb_full[:, :, :MLA_NOPE], ((0, 0), (0, 0), (0, LANES - MLA_NOPE)))
    wv = jnp.transpose(wkvb_full[:, :, MLA_NOPE:], (1, 0, 2)).reshape(MLA_KV_LORA, GROUP_WIDTH)
    return dict(
        qa=qa[l].reshape(1, -1), kva=kva[l].reshape(1, -1),
        wq=wq, wk=wk, wv=wv, wqt=jnp.transpose(wq, (0, 2, 1)), wkt=jnp.transpose(wk, (0, 2, 1)), wvt=wv.T,
        qn=_pad_lanes(qn[l]), kn=_pad_lanes(kn[l]),
        conv=jnp.pad(conv_full, ((0, 5), (0, 0))),
        sqn=jnp.tile(sqn[l].reshape(1, -1), (1, 2)), skn=jnp.tile(skn[l].reshape(1, -1), (1, 2)),
        sinks=sinks[l], wo=w_out_full, wot=w_out_full.T)


def _layer_weights(l, norm_g, w_in_full, qa, wqb_full, kva, wkvb_full, qn, kn, conv_full, sqn, skn, sinks,
                   w_out_full):
    return dict(_inproj_weights(l, norm_g, w_in_full.T),
                **_mixer_weights(l, qa, wqb_full, kva, wkvb_full, qn, kn, conv_full, sqn, skn, sinks, w_out_full))


def _layer_fwd(x, lw, rope, late_weights=None, target=None):
    proj, h = _inproj_fwd(x, lw["ng"], lw["wp"])
    if late_weights is not None:
        lw = dict(lw, **late_weights(proj))
    q, k, kt, vt = _mla_prep_fwd(proj, lw, rope)
    o_mla, lse = _mla_attn_fwd(q, k, vt)
    o_swa = _swa_fwd(proj, lw)
    ycat = _mix_fwd(proj, o_mla, o_swa, lw["conv"])
    if target is None:
        out = _mm_nn(ycat, lw["wo"], "outproj_fwd", residual=x)
    else:
        out = _outproj_loss(ycat, lw["wo"], x, target)
    return out, dict(x=x, proj=proj, h=h, q=q, k=k, kt=kt, vt=vt, o_mla=o_mla, lse=lse, o_swa=o_swa, ycat=ycat,
                     lw=lw)


def _layer_bwd(g, sv, lw, rope, on_big_grads=None):
    proj = sv["proj"]
    dycat, d_wo = _outproj_bwd(g, sv["ycat"], lw["wot"])
    dproj, do_mla, do_swa, d_conv = _mix_bwd(dycat, proj, sv["o_mla"], sv["o_swa"], lw["conv"])
    dproj, dkn_acc, dv_acc, d_sqn, d_sinks = _swa_bwd(proj, sv["o_swa"], do_swa, lw, dproj)
    dproj, d_skn = _swa_kv_bwd(proj, dkn_acc, dv_acc, lw, dproj)
    dq, dk, dv = _mla_attn_bwd(sv["q"], sv["k"], sv["kt"], sv["vt"], sv["o_mla"], do_mla, sv["lse"])
    dproj, d_wq, d_wk, d_wv, d_qa, d_kva, d_qn, d_kn = _mla_prep_bwd(proj, dq, dk, dv, lw, rope, dproj)
    grads = dict(
        w_out=d_wo, w_qb=d_wq[:, :, :MLA_QK],
        w_kvb=jnp.concatenate([d_wk[:, :, :MLA_NOPE],
                               jnp.transpose(d_wv.reshape(MLA_KV_LORA, HEADS, MLA_NOPE), (1, 0, 2))], axis=2))
    token = 0.0 if on_big_grads is None else on_big_grads("mixer", grads)
    d_wp = _mm_tn(sv["h"], dproj, "inproj_bwd_dw", WIRE_DTYPE, tn=NP // 2)
    grads["w_in"] = _unpack_dwin(d_wp)
    token = token if on_big_grads is None else token + on_big_grads("w_in", grads)
    dx, d_ng = _inproj_bwd_dx(dproj, lw["wpt"], sv["x"], g, lw["ng"] + token)
    grads.update(
        conv=d_conv[0:3], norm_g=d_ng[0], qa=d_qa[0], kva=d_kva[0], qn=d_qn[0, :MLA_QK], kn=d_kn[0, :MLA_QK],
        sqn=d_sqn[0, :SWA_HEAD_DIM], skn=d_skn[0, :SWA_HEAD_DIM], sinks=d_sinks[:, 0])
    return dx, grads


def _local_step(x, target, lws, rope):
    saved = []
    for l, lw in enumerate(lws):
        x, sv = _layer_fwd(x, lw, rope, target=target if l == len(lws) - 1 else None)
        saved.append(sv)
    g, loss_tile = x
    grads = [None] * len(lws)
    for l in reversed(range(len(lws))):
        g, grads[l] = _layer_bwd(g, saved[l], lws[l], rope)
    return loss_tile, g, grads


def _my_coords():
    return lax.axis_index("x"), lax.axis_index("y"), lax.axis_index("c")


def _peer(me, k):
    x, y, c = me
    return (1 - x if k & 4 else x, 1 - y if k & 2 else y, 1 - c if k & 1 else c)


def _lin(d):
    return 4 * d[0] + 2 * d[1] + d[2]


def _push_copies(ins, lands, send_sems, recv_sems, gather):
    me = _my_coords()
    my = _lin(me)
    out, inc = [], []
    for a in range(len(ins)):
        for k in range(1, N_DEV):
            peer = _peer(me, k)
            sems = dict(send_sem=send_sems.at[a * 7 + k - 1], recv_sem=recv_sems.at[a * 7 + k - 1],
                        device_id=peer, device_id_type=pl.DeviceIdType.MESH)
            src = ins[a] if gather else ins[a].at[_lin(peer)]
            out.append(pltpu.make_async_remote_copy(src_ref=src, dst_ref=lands[a].at[my], **sems))
            inc.append(pltpu.make_async_remote_copy(src_ref=src, dst_ref=lands[a].at[_lin(peer)], **sems))
    return out, inc


def _push_start(arrays, name, gather):
    n = len(arrays)
    land_shapes = [((N_DEV,) + a.shape) if gather else a.shape for a in arrays]

    def body(*refs):
        ins, lands = refs[:n], refs[n:2 * n]
        send_sems, recv_sems = refs[2 * n], refs[2 * n + 1]
        token = refs[-1]
        out, _ = _push_copies(ins, lands, send_sems, recv_sems, gather)
        for cp in out:
            cp.start()
        token[...] = jnp.zeros_like(token)

    hbm = pl.BlockSpec(memory_space=pltpu.HBM)
    sem = pl.BlockSpec(memory_space=pltpu.SEMAPHORE)
    res = pl.pallas_call(
        body, name=name,
        out_shape=(pltpu.SemaphoreType.DMA((7 * n,)), pltpu.SemaphoreType.DMA((7 * n,)),
                   *[pltpu.HBM(a.shape, a.dtype) for a in arrays],
                   *[pltpu.HBM(s, a.dtype) for s, a in zip(land_shapes, arrays)],
                   _sds((8, LANES), F32)),
        in_specs=(hbm,) * (2 * n),
        out_specs=(sem, sem) + (hbm,) * (2 * n) + (pl.BlockSpec(memory_space=pltpu.VMEM),),
        input_output_aliases={i: 2 + i for i in range(2 * n)},
        compiler_params=pltpu.CompilerParams(has_side_effects=pltpu.SideEffectType.DATAFLOW_SIDE_EFFECTING),
    )(*[pltpu.with_memory_space_constraint(a, pltpu.HBM) for a in arrays],
      *[pltpu.with_memory_space_constraint(lax.empty(s, a.dtype), pltpu.HBM) for s, a in zip(land_shapes, arrays)])
    return dict(send=res[0], recv=res[1], src=res[2:2 + n], land=res[2 + n:2 + 2 * n], token=res[-1][0, 0],
                gather=gather)


def _push_wait(handle, after, name):
    n = len(handle["src"])
    gather = handle["gather"]

    def body(*refs):
        ins, lands = refs[:n], refs[n:2 * n]
        send_sems, recv_sems = refs[2 * n], refs[2 * n + 1]
        out, inc = _push_copies(ins, lands, send_sems, recv_sems, gather)
        for cp in out:
            cp.wait_send()
        for cp in inc:
            cp.wait_recv()

    hbm = pl.BlockSpec(memory_space=pltpu.HBM)
    sem = pl.BlockSpec(memory_space=pltpu.SEMAPHORE)
    res = pl.pallas_call(
        body, name=name,
        out_shape=tuple(pltpu.HBM(a.shape, a.dtype) for a in (*handle["src"], *handle["land"])),
        in_specs=(hbm,) * (2 * n) + (sem, sem, pl.BlockSpec(memory_space=pl.ANY)),
        out_specs=(hbm,) * (2 * n),
        input_output_aliases={i: i for i in range(2 * n)},
        compiler_params=pltpu.CompilerParams(has_side_effects=pltpu.SideEffectType.DATAFLOW_SIDE_EFFECTING),
    )(*handle["src"], *handle["land"], handle["send"], handle["recv"], after)
    return res[n:]


def _small_all_reduce(v):
    R = v.shape[0]

    def body(v_ref, o_ref, buf, send_sems, recv_sems):
        me = _my_coords()
        my = _lin(me)
        sends = []
        for k in range(1, N_DEV):
            cp = pltpu.make_async_remote_copy(
                src_ref=v_ref, dst_ref=buf.at[my], send_sem=send_sems.at[k - 1], recv_sem=recv_sems.at[k - 1],
                device_id=_peer(me, k), device_id_type=pl.DeviceIdType.MESH)
            cp.start()
            sends.append(cp)
        buf[my] = v_ref[...]
        for k in range(1, N_DEV):
            pltpu.make_async_remote_copy(
                src_ref=v_ref, dst_ref=buf.at[_lin(_peer(me, k))], send_sem=send_sems.at[k - 1],
                recv_sem=recv_sems.at[k - 1], device_id=_peer(me, k),
                device_id_type=pl.DeviceIdType.MESH).wait_recv()
        for cp in sends:
            cp.wait_send()
        tot = buf[0]
        for d in range(1, N_DEV):
            tot = tot + buf[d]
        o_ref[...] = tot

    vm = pl.BlockSpec(memory_space=pltpu.VMEM)
    return pl.pallas_call(
        body, name="small_all_reduce", in_specs=[vm], out_specs=vm, out_shape=_sds(v.shape, F32),
        scratch_shapes=[pltpu.VMEM((N_DEV, R, LANES), F32), pltpu.SemaphoreType.DMA((7,)),
                        pltpu.SemaphoreType.DMA((7,))],
    )(v)


def _adamw_math(w, g, m, v):
    m = ADAM_B1 * m + (1.0 - ADAM_B1) * g
    v = ADAM_B2 * v + (1.0 - ADAM_B2) * (g * g)
    m_hat = m / (1.0 - ADAM_B1 ** ADAM_STEP)
    v_hat = v / (1.0 - ADAM_B2 ** ADAM_STEP)
    delta = -ADAM_LR * (m_hat / (jnp.sqrt(v_hat) + ADAM_EPS) + ADAM_WD * w)
    return delta, m, v


def _adamw(parts, w, m, v, name, tr):
    P, R, C = parts.shape
    tr = min(tr, R)

    def body(p_ref, w_ref, m_ref, v_ref, g_out, d_out, m_out, v_out):
        g = p_ref[0].astype(F32)
        for d in range(1, P):
            g = g + p_ref[d].astype(F32)
        delta, m_new, v_new = _adamw_math(w_ref[...], g, m_ref[...], v_ref[...])
        g_out[...] = g
        d_out[...] = delta
        m_out[...] = m_new
        v_out[...] = v_new

    tile = pl.BlockSpec((tr, C), lambda i: (i, 0))
    return pl.pallas_call(
        body, name=name, grid=(R // tr,),
        in_specs=[pl.BlockSpec((P, tr, C), lambda i: (0, i, 0)), tile, tile, tile],
        out_specs=[tile] * 4, out_shape=[_sds((R, C), F32)] * 4,
        compiler_params=_cp(("parallel",), 32))(parts, w, m, v)


SMALL = (("norm_g", D_MODEL), ("mla_q_a_norm", MLA_Q_LORA), ("mla_kv_a_norm", MLA_KV_LORA), ("mla_q_norm", MLA_QK),
         ("mla_k_norm", MLA_QK), ("swa_q_norm", SWA_HEAD_DIM), ("swa_k_norm", SWA_HEAD_DIM), ("swa_sinks", HEADS))
SMALL_GRAD_KEY = dict(norm_g="norm_g", mla_q_a_norm="qa", mla_kv_a_norm="kva", mla_q_norm="qn", mla_k_norm="kn",
                      swa_q_norm="sqn", swa_k_norm="skn", swa_sinks="sinks")
SMALL_ROWS = 32
CONV_ROWS = 24


def _pack_small(get):
    parts = []
    for l in range(DEPTH):
        for name, n in SMALL:
            v = get(name, l).reshape(-1)
            parts.append(jnp.pad(v, (0, (-n) % LANES)))
    return jnp.concatenate(parts).reshape(SMALL_ROWS, LANES)


def _unpack_small(packed):
    flat = packed.reshape(-1)
    out = {name: [] for name, _ in SMALL}
    off = 0
    for l in range(DEPTH):
        for name, n in SMALL:
            out[name].append(flat[off:off + n])
            off += n + (-n) % LANES
    return {name: jnp.stack(v) for name, v in out.items()}


def kernel(x, norm_g, w_in, mla_q_a_norm, mla_w_qb, mla_kv_a_norm, mla_w_kvb, mla_q_norm, mla_k_norm, conv_w, swa_q_norm, swa_k_norm, swa_sinks, w_out, loss_target, m_norm_g, m_w_in, m_mla_q_a_norm, m_mla_w_qb, m_mla_kv_a_norm, m_mla_w_kvb, m_mla_q_norm, m_mla_k_norm, m_conv_w, m_swa_q_norm, m_swa_k_norm, m_swa_sinks, m_w_out, v_norm_g, v_w_in, v_mla_q_a_norm, v_mla_w_qb, v_mla_kv_a_norm, v_mla_w_kvb, v_mla_q_norm, v_mla_k_norm, v_conv_w, v_swa_q_norm, v_swa_k_norm, v_swa_sinks, v_w_out):
    T = x.shape[1]
    weights = dict(norm_g=norm_g, w_in=w_in, mla_q_a_norm=mla_q_a_norm, mla_w_qb=mla_w_qb,
                   mla_kv_a_norm=mla_kv_a_norm, mla_w_kvb=mla_w_kvb, mla_q_norm=mla_q_norm, mla_k_norm=mla_k_norm,
                   conv_w=conv_w, swa_q_norm=swa_q_norm, swa_k_norm=swa_k_norm, swa_sinks=swa_sinks, w_out=w_out)
    mom_m = dict(norm_g=m_norm_g, w_in=m_w_in, mla_q_a_norm=m_mla_q_a_norm, mla_w_qb=m_mla_w_qb,
                 mla_kv_a_norm=m_mla_kv_a_norm, mla_w_kvb=m_mla_w_kvb, mla_q_norm=m_mla_q_norm,
                 mla_k_norm=m_mla_k_norm, conv_w=m_conv_w, swa_q_norm=m_swa_q_norm, swa_k_norm=m_swa_k_norm,
                 swa_sinks=m_swa_sinks, w_out=m_w_out)
    mom_v = dict(norm_g=v_norm_g, w_in=v_w_in, mla_q_a_norm=v_mla_q_a_norm, mla_w_qb=v_mla_w_qb,
                 mla_kv_a_norm=v_mla_kv_a_norm, mla_w_kvb=v_mla_w_kvb, mla_q_norm=v_mla_q_norm,
                 mla_k_norm=v_mla_k_norm, conv_w=v_conv_w, swa_q_norm=v_swa_q_norm, swa_k_norm=v_swa_k_norm,
                 swa_sinks=v_swa_sinks, w_out=v_w_out)

    my = _lin(_my_coords())

    def shards(l):
        return [w_in[l].astype(MXU_DTYPE).T, mla_w_qb[l].astype(MXU_DTYPE), mla_w_kvb[l].astype(MXU_DTYPE),
                w_out[l].astype(MXU_DTYPE), conv_w[l]]

    def inproj_weights(l, g_win_t):
        return _inproj_weights(l, norm_g, g_win_t.reshape(IN_COLS, D_MODEL))

    def mixer_weights(l, gathered):
        g_wqb, g_wkvb, g_wout, g_conv = gathered
        return _mixer_weights(
            l, mla_q_a_norm, g_wqb, mla_kv_a_norm, g_wkvb, mla_q_norm, mla_k_norm,
            jnp.transpose(g_conv, (1, 0, 2)).reshape(3, GROUP_WIDTH), swa_q_norm, swa_k_norm, swa_sinks,
            g_wout.reshape(D_MIX, D_MODEL))

    slot_of = dict(
        w_in=lambda g: jnp.transpose(g["w_in"].reshape(D_MODEL, N_DEV, IN_COLS // N_DEV), (1, 0, 2)),
        w_out=lambda g: g["w_out"].reshape(N_DEV, D_MIX // N_DEV, D_MODEL),
        w_qb=lambda g: g["w_qb"], w_kvb=lambda g: g["w_kvb"])

    def own_slot(landed, mine):
        return [lax.dynamic_update_index_in_dim(a, m, my, 0) for a, m in zip(landed, mine)]

    def landed(handle, after, name, mine):
        return own_slot(_push_wait(handle, after, name), mine)

    gather_in0 = _push_start(shards(0)[:1], "weight_gather_in0_start", gather=True)
    rope = _rope_tables(T, gather_in0["token"])
    w_in0_t = landed(gather_in0, rope[0], "weight_gather_in0_wait", shards(0)[:1])[0]
    w_in0_t, conv0 = lax.optimization_barrier((w_in0_t, conv_w[0]))
    gather0 = _push_start(shards(0)[1:4] + [conv0], "weight_gather0_start", gather=True)
    lw0 = inproj_weights(0, w_in0_t)
    lw0 = dict(lw0, ng=lw0["ng"] + gather0["token"])
    layer1 = {}

    def mixer0(proj):
        got = landed(gather0, proj, "weight_gather0_wait", shards(0)[1:])
        got[0], conv1 = lax.optimization_barrier((got[0], conv_w[1]))
        layer1["gather"] = _push_start(shards(1)[:4] + [conv1], "weight_gather1_start", gather=True)
        mw = mixer_weights(0, got)
        return dict(mw, qa=mw["qa"] + layer1["gather"]["token"])

    x1, sv0 = _layer_fwd(x[0], lw0, rope, late_weights=mixer0)
    g1_all = landed(layer1["gather"], x1, "weight_gather1_wait", shards(1))
    (g2, loss_tile), sv1 = _layer_fwd(x1, dict(inproj_weights(1, g1_all[0]), **mixer_weights(1, g1_all[1:])), rope,
                                      target=loss_target[0])

    parts = {(1, "w_in"): ("w_in", "w_out", "w_qb", "w_kvb"), (0, "mixer"): ("w_out", "w_qb", "w_kvb"),
             (0, "w_in"): ("w_in",)}
    started = []

    def start_exchange(l, part, g):
        if (l, part) not in parts:
            return 0.0
        sl = [slot_of[n](g) for n in parts[(l, part)]]
        handle = _push_start(sl, "grad_exchange%d_%s_start" % (l, part), gather=False)
        started.append((l, part, sl, handle))
        return handle["token"]

    g1, grads1 = _layer_bwd(g2, sv1, sv1["lw"], rope, on_big_grads=functools.partial(start_exchange, 1))
    lw0b = dict(sv0["lw"], conv=sv0["lw"]["conv"] + started[0][3]["token"])
    grad_x, grads0 = _layer_bwd(g1, sv0, lw0b, rope, on_big_grads=functools.partial(start_exchange, 0))
    recv = {}

    def receive(l, part, sl, handle, after):
        got = landed(handle, after, "grad_exchange%d_%s_wait" % (l, part), [s[my] for s in sl])
        recv.update({(l, n): a for n, a in zip(parts[(l, part)], got)})

    for entry in started[:-1]:
        receive(*entry, after=grad_x)
    grads = [grads0, grads1]
    stacked = lambda n: jnp.stack([recv[(0, n)], recv[(1, n)]], axis=1)

    small = jnp.concatenate([
        _pack_small(lambda name, l: grads[l][SMALL_GRAD_KEY[name]]),
        jnp.stack([g["conv"] for g in grads]).reshape(CONV_ROWS, LANES),
        loss_tile], axis=0)
    small = _small_all_reduce(small)
    loss = small[SMALL_ROWS + CONV_ROWS, 0]
    my = _lin(_my_coords())
    conv_g = lax.dynamic_slice_in_dim(small[SMALL_ROWS:SMALL_ROWS + CONV_ROWS].reshape(DEPTH, 3, GROUP_WIDTH),
                                      my * 64, 64, axis=2)

    out = {}

    def big(name, recv, rows, cols, tr):
        res = _adamw(recv.reshape(N_DEV, rows, cols), weights[name].reshape(rows, cols),
                     mom_m[name].reshape(rows, cols), mom_v[name].reshape(rows, cols), "adamw_" + name, tr)
        out[name] = [r.reshape(weights[name].shape) for r in res]

    big("w_out", stacked("w_out"), DEPTH * D_MIX // N_DEV, D_MODEL, 192)
    big("mla_w_qb", stacked("w_qb"), DEPTH * MLA_Q_LORA, MLA_QK, 512)
    big("mla_w_kvb", stacked("w_kvb"), DEPTH * MLA_KV_LORA, 128, 256)
    receive(*started[-1], after=out["w_out"][1])
    big("w_in", stacked("w_in"), DEPTH * D_MODEL, IN_COLS // N_DEV, 256)

    pad_conv = lambda a: jnp.pad(a.reshape(-1), (0, 8 * LANES - 6 * 64)).reshape(8, LANES)
    cat = lambda src: jnp.concatenate([_pack_small(lambda name, l: src[name][l]), pad_conv(src["conv_w"])], axis=0)
    g_small = jnp.concatenate([small[:SMALL_ROWS], pad_conv(conv_g)], axis=0)
    res = _adamw(g_small[None], cat(weights), cat(mom_m), cat(mom_v), "adamw_small", SMALL_ROWS + 8)
    smalls = [_unpack_small(r[:SMALL_ROWS]) for r in res]
    for name, _ in SMALL:
        out[name] = [s[name] for s in smalls]
    out["conv_w"] = [r[SMALL_ROWS:].reshape(-1)[:6 * 64].reshape(DEPTH, 3, 64) for r in res]

    order = ["norm_g", "w_in", "mla_q_a_norm", "mla_w_qb", "mla_kv_a_norm", "mla_w_kvb", "mla_q_norm", "mla_k_norm",
             "conv_w", "swa_q_norm", "swa_k_norm", "swa_sinks", "w_out"]
    result = [loss, grad_x[None]]
    for idx in range(4):
        result += [out[name][idx] for name in order]
    return tuple(result)
```

```python
import functools

import jax
import jax.numpy as jnp
import numpy as np
from jax import lax
from jax.experimental import pallas as pl
from jax.experimental.pallas import tpu as pltpu

F32 = jnp.float32
MXU_DTYPE = jnp.bfloat16
WIRE_DTYPE = jnp.bfloat16

N_DEV = 8
DEPTH = 2
D_MODEL = 1024
GROUP_WIDTH = 512
D_MIX = 3 * GROUP_WIDTH
BLOCK = 128
RMS_EPS = 1e-6
NEG_INF = -1e30
HEADS = 8
MLA_QK = 96
MLA_NOPE = 64
MLA_ROPE = 32
MLA_Q_LORA = 256
MLA_KV_LORA = 128
ROPE_THETA = 10000.0
SWA_HEAD_DIM = 64
LANES = 128
IN_COLS = 4256

ADAM_LR = 0.001
ADAM_B1 = 0.9
ADAM_B2 = 0.999
ADAM_EPS = 1e-08
ADAM_WD = 0.01
ADAM_STEP = 10

NP = 4352
CB_GMLA, CB_CH, CB_CB, CB_CC, CB_GCONV, CB_GSWA, CB_SQ = 0, 1, 2, 3, 4, 5, 7
CB_QLAT = 12
CB_KVLAT, CB_KROPE = 26, 27
CB_SK, CB_SV = 32, 33
DPB_MIX, DPB_MLA, DPB_SQ, DPB_SKV = 0, 6, 7, 16

TM_PROJ = 512
TM_ROW = 256
TK = 256
TQ = 2 * TK
MLA_SCALE = MLA_QK ** -0.5
MLA_ONES_ROW = (64, 0)
LOG2E = 1.4426950408889634
LN2 = 0.6931471805599453
TM_SWA = 512
VMEM_MB = 2 ** 20


def _cp(sem, vmem_mb):
    return pltpu.CompilerParams(dimension_semantics=sem, vmem_limit_bytes=vmem_mb * VMEM_MB)


def _sds(shape, dtype):
    return jax.ShapeDtypeStruct(shape, dtype)


def _dot(a, b):
    return jnp.dot(a, b, preferred_element_type=F32)


def _dot_nt(a, b):
    return lax.dot_general(a, b, (((1,), (1,)), ((), ())), preferred_element_type=F32)


def _dot_tn(a, b):
    return lax.dot_general(a, b, (((0,), (0,)), ((), ())), preferred_element_type=F32)


def _rms(x, n):
    r = lax.rsqrt(jnp.sum(x * x, axis=-1, keepdims=True) * (1.0 / n) + RMS_EPS)
    return x * r, r


def _rms_bwd(dy, xhat, r, w, n):
    g = dy * w
    return r * (g - xhat * (jnp.sum(g * xhat, axis=-1, keepdims=True) * (1.0 / n)))


def _rms_halves(x, half1):
    x2 = x * x
    s0 = jnp.sum(jnp.where(half1, 0.0, x2), axis=-1, keepdims=True)
    s1 = jnp.sum(jnp.where(half1, x2, 0.0), axis=-1, keepdims=True)
    r = jnp.where(half1, lax.rsqrt(s1 * (1.0 / 64) + RMS_EPS), lax.rsqrt(s0 * (1.0 / 64) + RMS_EPS))
    return x * r, r


def _rms_halves_bwd(dy, xhat, r, w, half1):
    g = dy * w
    t = g * xhat
    m0 = jnp.sum(jnp.where(half1, 0.0, t), axis=-1, keepdims=True) * (1.0 / 64)
    m1 = jnp.sum(jnp.where(half1, t, 0.0), axis=-1, keepdims=True) * (1.0 / 64)
    return r * (g - xhat * jnp.where(half1, m1, m0))


def _sigmoid(x):
    return 1.0 / (1.0 + jnp.exp(-x))


def _rope(x, c, s1, s2):
    ax = x.ndim - 1
    return x * c + pltpu.roll(x, 112, ax) * s1 + pltpu.roll(x, 16, ax) * s2


def _rope_bwd(dy, c, s1, s2):
    ax = dy.ndim - 1
    return dy * c + pltpu.roll(dy * s1, 16, ax) + pltpu.roll(dy * s2, 112, ax)


def _fold_rows8(x):
    return jnp.sum(x.reshape(x.shape[0] // 8, 8, x.shape[1]), axis=0)


def _row0(v, rows=8):
    row = lax.broadcasted_iota(jnp.int32, (rows, v.shape[1]), 0)
    return jnp.where(row == 0, jnp.broadcast_to(v, (rows, v.shape[1])), 0.0)


def _mm_nn(a, b, name, out_dtype=F32, residual=None, tm=TM_PROJ):
    M, K = a.shape
    N = b.shape[1]
    tm = min(tm, M)

    def body(*refs):
        if residual is None:
            a_ref, b_ref, o_ref = refs
            acc = _dot(a_ref[...].astype(MXU_DTYPE), b_ref[...])
        else:
            a_ref, b_ref, r_ref, o_ref = refs
            acc = _dot(a_ref[...].astype(MXU_DTYPE), b_ref[...]) + r_ref[...]
        o_ref[...] = acc.astype(out_dtype)

    in_specs = [pl.BlockSpec((tm, K), lambda i: (i, 0)), pl.BlockSpec((K, N), lambda i: (0, 0))]
    args = [a, b]
    if residual is not None:
        in_specs.append(pl.BlockSpec((tm, N), lambda i: (i, 0)))
        args.append(residual)
    return pl.pallas_call(
        body, name=name, grid=(M // tm,), in_specs=in_specs,
        out_specs=pl.BlockSpec((tm, N), lambda i: (i, 0)), out_shape=_sds((M, N), out_dtype),
        compiler_params=_cp(("parallel",), 48))(*args)


def _mm_tn(a, b, name, out_dtype, tn, tk=512):
    T, M = a.shape
    N = b.shape[1]
    tk = min(tk, T)
    nk = T // tk

    def body(a_ref, b_ref, o_ref, acc_ref):
        k = pl.program_id(1)

        @pl.when(k == 0)
        def _():
            acc_ref[...] = jnp.zeros_like(acc_ref)

        acc_ref[...] += _dot_tn(a_ref[...].astype(MXU_DTYPE), b_ref[...].astype(MXU_DTYPE))

        @pl.when(k == nk - 1)
        def _():
            o_ref[...] = acc_ref[...].astype(out_dtype)

    return pl.pallas_call(
        body, name=name, grid=(N // tn, nk),
        in_specs=[pl.BlockSpec((tk, M), lambda n, k: (k, 0)), pl.BlockSpec((tk, tn), lambda n, k: (k, n))],
        out_specs=pl.BlockSpec((M, tn), lambda n, k: (0, n)), out_shape=_sds((M, N), out_dtype),
        scratch_shapes=[pltpu.VMEM((M, tn), F32)],
        compiler_params=_cp(("parallel", "arbitrary"), 48))(a, b)


def _inproj_fwd(x, ng, wp):
    T, D = x.shape
    tm = min(TM_PROJ, T)

    def body(x_ref, g_ref, w_ref, proj_ref, h_ref):
        xhat, _ = _rms(x_ref[...], D)
        h = (xhat * g_ref[...]).astype(MXU_DTYPE)
        h_ref[...] = h
        proj_ref[...] = _dot(h, w_ref[...])

    return pl.pallas_call(
        body, name="inproj_fwd", grid=(T // tm,),
        in_specs=[pl.BlockSpec((tm, D), lambda i: (i, 0)), pl.BlockSpec((1, D), lambda i: (0, 0)),
                  pl.BlockSpec((D, NP), lambda i: (0, 0))],
        out_specs=[pl.BlockSpec((tm, NP), lambda i: (i, 0)), pl.BlockSpec((tm, D), lambda i: (i, 0))],
        out_shape=[_sds((T, NP), F32), _sds((T, D), MXU_DTYPE)],
        compiler_params=_cp(("parallel",), 48))(x, ng, wp)


def _mla_prep_fwd(proj, lw, rope):
    T = proj.shape[0]
    tk = min(TK, T // 2)
    nsub = 2
    tm = nsub * tk

    def body(ql_ref, kvl_ref, kr_ref, qa_ref, kva_ref, wq_ref, wk_ref, wv_ref, qn_ref, kn_ref,
             c_ref, s1_ref, s2_ref, q_out, k_out, kt_out, vt_out):
        c, s1, s2 = c_ref[...], s1_ref[...], s2_ref[...]
        qhat, _ = _rms(ql_ref[...], MLA_Q_LORA)
        qn = (qhat * qa_ref[...]).astype(MXU_DTYPE)
        khat, _ = _rms(kvl_ref[...], MLA_KV_LORA)
        kvn = (khat * kva_ref[...]).astype(MXU_DTYPE)
        kr = kr_ref[...]
        half1 = lax.broadcasted_iota(jnp.int32, (tm, LANES), 1) >= 64
        ones_row = lax.broadcasted_iota(jnp.int32, (LANES, 1), 0)
        q3, _ = _rms(jnp.stack([_dot(qn, wq_ref[h]) for h in range(HEADS)]), MLA_QK)
        q_out[...] = (_rope(q3 * qn_ref[...], c, s1, s2) * (MLA_SCALE * LOG2E)).astype(MXU_DTYPE)
        k3, _ = _rms(jnp.stack([_dot(kvn, wk_ref[h]) for h in range(HEADS)]) + kr, MLA_QK)
        k3 = _rope(k3 * kn_ref[...], c, s1, s2)
        k_out[...] = k3.astype(MXU_DTYPE)
        for h in range(HEADS):
            for t in range(nsub):
                kt_out[h, t] = k3[h, tk * t:tk * (t + 1)].T.astype(MXU_DTYPE)
        v = _dot(kvn, wv_ref[...])
        for h in range(HEADS):
            vp = v[:, LANES * (h // 2):LANES * (h // 2 + 1)]
            own = half1 if h % 2 else jnp.logical_not(half1)
            vp = jnp.where(own, vp, 0.0)
            for t in range(nsub):
                vpt = vp[tk * t:tk * (t + 1)].T
                vt_out[h, t] = jnp.where(ones_row == MLA_ONES_ROW[h % 2], 1.0, vpt).astype(MXU_DTYPE)

    full = lambda shape: pl.BlockSpec(shape, lambda i: (0,) * len(shape))
    hd = pl.BlockSpec((HEADS, tm, LANES), lambda i: (0, i, 0))
    hdt = pl.BlockSpec((HEADS, nsub, LANES, tk), lambda i: (0, i, 0, 0))
    nat = _sds((HEADS, T, LANES), MXU_DTYPE)
    tr = _sds((HEADS, T // tk, LANES, tk), MXU_DTYPE)
    return pl.pallas_call(
        body, name="mla_prep_fwd", grid=(T // tm,),
        in_specs=[pl.BlockSpec((tm, 256), lambda i: (i, CB_QLAT)), pl.BlockSpec((tm, LANES), lambda i: (i, CB_KVLAT)),
                  pl.BlockSpec((tm, LANES), lambda i: (i, CB_KROPE)),
                  full((1, 256)), full((1, LANES)), full((HEADS, 256, LANES)), full((HEADS, LANES, LANES)),
                  full((LANES, 512)), full((1, LANES)), full((1, LANES)),
                  pl.BlockSpec((tm, LANES), lambda i: (i, 0)), pl.BlockSpec((tm, LANES), lambda i: (i, 0)),
                  pl.BlockSpec((tm, LANES), lambda i: (i, 0))],
        out_specs=[hd, hd, hdt, hdt],
        out_shape=[nat, nat, tr, tr],
        compiler_params=_cp(("parallel",), 32))(
            proj, proj, proj, lw["qa"], lw["kva"], lw["wq"], lw["wk"], lw["wv"], lw["qn"], lw["kn"],
            rope[0], rope[1], rope[2])


def _mla_attn_fwd(q, k, vt):
    T = q.shape[1]
    tk = min(TK, T // 2)
    tq = 2 * tk

    def body(q_ref, k_ref, vt_ref, o_ref, lse_ref, acc_s, m_s, s_a, s_b):
        i = pl.program_id(1)
        key = lax.broadcasted_iota(jnp.int32, (tk, tq), 0)
        qry = lax.broadcasted_iota(jnp.int32, (tk, tq), 1)
        qs = [q_ref[0], q_ref[1]]
        acc_s[...] = jnp.zeros_like(acc_s)
        m_s[...] = jnp.full(m_s.shape, NEG_INF, F32)

        def scores(kj, buf):
            rows = pl.ds(pl.multiple_of(kj * tk, tk), tk)
            for r in range(2):
                buf[r] = _dot_nt(k_ref[r, rows, :], qs[r])

        def consume(kj, buf, diag):
            for r in range(2):
                s = buf[r]
                if diag is not None:
                    s = jnp.where(key + diag * tk <= qry, s, NEG_INF)
                m_old = m_s[r]
                m_new = jnp.maximum(m_old, jnp.max(s, axis=0, keepdims=True))
                alpha = jnp.exp2(m_old - m_new)
                p = jnp.exp2(s - m_new)
                m_s[r] = m_new
                acc_s[r] = alpha * acc_s[r] + _dot(vt_ref[r, kj], p.astype(MXU_DTYPE))

        scores(0, s_a)

        def pair(kj):
            scores(kj + 1, s_b)
            consume(kj, s_a, None)
            scores(kj + 2, s_a)
            consume(kj + 1, s_b, None)

        def octet(ko, carry):
            for t in range(4):
                pair(8 * ko + 2 * t)
            return carry

        lax.fori_loop(0, i // 4, octet, 0)

        @pl.when(i % 4 >= 2)
        def _():
            pair(8 * (i // 4))
            pair(8 * (i // 4) + 2)

        @pl.when(i % 2 == 1)
        def _():
            pair(2 * i - 2)

        scores(2 * i + 1, s_b)
        consume(2 * i, s_a, 0)
        consume(2 * i + 1, s_b, 1)
        l = [acc_s[r, pl.ds(MLA_ONES_ROW[r], 1), :] for r in range(2)]
        head0 = lax.broadcasted_iota(jnp.int32, (LANES, 1), 0) < 64
        o_ref[...] = jnp.where(head0, acc_s[0] / l[0], acc_s[1] / l[1]).T
        for r in range(2):
            lse_ref[r] = m_s[r] + jnp.log2(l[r])

    return pl.pallas_call(
        body, name="mla_attn_fwd", grid=(HEADS // 2, T // tq),
        in_specs=[pl.BlockSpec((2, tq, LANES), lambda j, i: (j, i, 0)),
                  pl.BlockSpec((2, T, LANES), lambda j, i: (j, 0, 0)),
                  pl.BlockSpec((2, T // tk, LANES, tk), lambda j, i: (j, 0, 0, 0))],
        out_specs=[pl.BlockSpec((tq, LANES), lambda j, i: (i, j)),
                   pl.BlockSpec((2, 1, tq), lambda j, i: (j, 0, i))],
        out_shape=[_sds((T, GROUP_WIDTH), F32), _sds((HEADS, 1, T), F32)],
        scratch_shapes=[pltpu.VMEM((2, LANES, tq), F32), pltpu.VMEM((2, 1, tq), F32),
                        pltpu.VMEM((2, tk, tq), F32), pltpu.VMEM((2, tk, tq), F32)],
        compiler_params=_cp(("parallel", "arbitrary"), 40))(q, k, vt)


def _swa_kv_variants(x, half1):
    xs = pltpu.roll(x, 64, 1)
    out = {}
    for g in range(2):
        for r in range(2):
            own = half1 if r else jnp.logical_not(half1)
            out[(g, r)] = jnp.where(own, x if g == r else xs, 0.0).astype(MXU_DTYPE)
    return out


def _swa_alibi():
    ki = np.arange(2 * BLOCK)[:, None]
    qi = np.arange(BLOCK)[None, :]
    dist = BLOCK + qi - ki
    slopes = 2.0 ** -(np.arange(HEADS) + 1.0)
    tab = np.where(((dist >= 0) & (dist < BLOCK))[None], slopes[:, None, None] * dist[None], 1e30)
    return jnp.asarray(tab, F32)


def _swa_kv_variants_t(xt, rows1):
    xs = pltpu.roll(xt, 64, 0)
    out = {}
    for g in range(2):
        for r in range(2):
            own = rows1 if r else jnp.logical_not(rows1)
            out[(g, r)] = jnp.where(own, xt if g == r else xs, 0.0).astype(MXU_DTYPE)
    return out


def _swa_probs(i, nb, q_ref, k_ref, v_ref, pk_ref, pv_ref, qw_ref, kw_ref, alibi_ref, sink_ref):
    scale = SWA_HEAD_DIM ** -0.5
    half1 = lax.broadcasted_iota(jnp.int32, (1, LANES), 1) >= 64
    k_all = jnp.concatenate([pk_ref[...], k_ref[...]], axis=0)
    v_all = jnp.concatenate([pv_ref[...], v_ref[...]], axis=0)
    khat, _ = _rms_halves(k_all, half1)
    kn = khat * kw_ref[...]
    kp = _swa_kv_variants(kn, half1)
    qhat, qr, qn, qt = [], [], [], []
    for j in range(4):
        xh, r = _rms_halves(q_ref[:, LANES * j:LANES * (j + 1)], half1)
        qf = xh * qw_ref[...]
        qhat.append(xh)
        qr.append(r)
        qn.append(qf.astype(MXU_DTYPE))
        qt.append(qf.T.astype(MXU_DTYPE))
    key = lax.broadcasted_iota(jnp.int32, (2 * BLOCK, BLOCK), 0)
    first = jnp.where((i == 0) & (key < BLOCK), NEG_INF, 0.0)
    s = jnp.stack([_dot(kp[(h // 4, h % 2)][BLOCK * b:BLOCK * (b + 2)], qt[h // 2][:, BLOCK * b:BLOCK * (b + 1)])
                   for b in range(nb) for h in range(HEADS)]) * scale - alibi_ref[...]
    s = jnp.concatenate([s[:HEADS] + first, s[HEADS:]], axis=0) if nb > 1 else s + first
    sink = jnp.stack([jnp.full((1, 1), sink_ref[h], F32) for _ in range(nb) for h in range(HEADS)])
    m = jnp.maximum(jnp.max(s, axis=1, keepdims=True), sink)
    e = jnp.exp(s - m)
    es = jnp.exp(sink - m)
    inv = 1.0 / (jnp.sum(e, axis=1, keepdims=True) + es)
    return e * inv, es * inv, dict(half1=half1, kn=kn, kp=kp, v_all=v_all, qhat=qhat, qr=qr, qn=qn)


def _swa_fwd(proj, lw):
    T = proj.shape[0]
    tm = min(TM_SWA, T)
    nb = tm // BLOCK

    def body(q_ref, k_ref, v_ref, pk_ref, pv_ref, qw_ref, kw_ref, alibi_ref, sink_ref, o_ref):
        p, _, c = _swa_probs(pl.program_id(0), nb, q_ref, k_ref, v_ref, pk_ref, pv_ref, qw_ref, kw_ref, alibi_ref,
                             sink_ref)
        p = p.astype(MXU_DTYPE)
        rows1 = lax.broadcasted_iota(jnp.int32, (LANES, 1), 0) >= 64
        vpt = _swa_kv_variants_t(c["v_all"].T, rows1)
        for j in range(4):
            g = j // 2
            o_t = [_dot(vpt[(g, 0)][:, BLOCK * b:BLOCK * (b + 2)], p[HEADS * b + 2 * j])
                   + _dot(vpt[(g, 1)][:, BLOCK * b:BLOCK * (b + 2)], p[HEADS * b + 2 * j + 1]) for b in range(nb)]
            o_t = jnp.concatenate(o_t, axis=1) if nb > 1 else o_t[0]
            o_ref[:, LANES * j:LANES * (j + 1)] = o_t.T

    prev = lambda cb: pl.BlockSpec((BLOCK, LANES), lambda i: (jnp.maximum(i * nb - 1, 0), cb))
    return pl.pallas_call(
        body, name="swa_fwd", grid=(T // tm,),
        in_specs=[pl.BlockSpec((tm, 512), lambda i: (i, CB_SQ)), pl.BlockSpec((tm, LANES), lambda i: (i, CB_SK)),
                  pl.BlockSpec((tm, LANES), lambda i: (i, CB_SV)), prev(CB_SK), prev(CB_SV),
                  pl.BlockSpec((1, LANES), lambda i: (0, 0)), pl.BlockSpec((1, LANES), lambda i: (0, 0)),
                  pl.BlockSpec((nb * HEADS, 2 * BLOCK, BLOCK), lambda i: (0, 0, 0)),
                  pl.BlockSpec(memory_space=pltpu.SMEM)],
        out_specs=pl.BlockSpec((tm, 512), lambda i: (i, 0)),
        out_shape=_sds((T, GROUP_WIDTH), F32),
        compiler_params=_cp(("parallel",), 40))(
            proj, proj, proj, proj, proj, lw["sqn"], lw["skn"], jnp.tile(_swa_alibi(), (nb, 1, 1)), lw["sinks"])


def _shift_down(u, prev, n, row):
    tm = u.shape[0]
    out = pltpu.roll(u, n, 0)
    row8 = lax.broadcasted_iota(jnp.int32, prev.shape, 0)
    for t in range(n):
        src = jnp.sum(jnp.where(row8 == 8 - n + t, prev, 0.0), axis=0, keepdims=True)
        out = jnp.where(row == t, src, out)
    return out


def _shift_up(u, nxt, n, row):
    tm = u.shape[0]
    out = pltpu.roll(u, tm - n, 0)
    row8 = lax.broadcasted_iota(jnp.int32, nxt.shape, 0)
    for t in range(n):
        src = jnp.sum(jnp.where(row8 == t, nxt, 0.0), axis=0, keepdims=True)
        out = jnp.where(row == tm - n + t, src, out)
    return out


def _mix_fwd(proj, o_mla, o_swa, conv_w):
    T = proj.shape[0]
    tm = min(TM_ROW, T)

    def body(gm_ref, ch_ref, cb_ref, cc_ref, gc_ref, gs_ref, pch_ref, pcc_ref, om_ref, os_ref, w_ref, y_ref):
        i = pl.program_id(0)
        row = lax.broadcasted_iota(jnp.int32, (tm, GROUP_WIDTH), 0)
        u = cc_ref[...] * ch_ref[...]
        u_prev = jnp.where(i > 0, pcc_ref[...] * pch_ref[...], 0.0)
        z = (w_ref[0:1, :] * _shift_down(u, u_prev, 2, row) + w_ref[1:2, :] * _shift_down(u, u_prev, 1, row)
             + w_ref[2:3, :] * u)
        gm, gc, gs = gm_ref[...], gc_ref[...], gs_ref[...]
        y_ref[:, 0:512] = (om_ref[...] * (gm * _sigmoid(gm))).astype(MXU_DTYPE)
        y_ref[:, 512:1024] = (cb_ref[...] * z * (gc * _sigmoid(gc))).astype(MXU_DTYPE)
        y_ref[:, 1024:1536] = (os_ref[...] * (gs * _sigmoid(gs))).astype(MXU_DTYPE)

    blk = lambda cb: pl.BlockSpec((tm, 512), lambda i: (i, cb))
    prev = lambda cb: pl.BlockSpec((8, 512), lambda i: (jnp.maximum(i * (tm // 8) - 1, 0), cb))
    tile = pl.BlockSpec((tm, 512), lambda i: (i, 0))
    return pl.pallas_call(
        body, name="mix_fwd", grid=(T // tm,),
        in_specs=[blk(CB_GMLA), blk(CB_CH), blk(CB_CB), blk(CB_CC), blk(CB_GCONV), blk(CB_GSWA),
                  prev(CB_CH), prev(CB_CC), tile, tile, pl.BlockSpec((8, 512), lambda i: (0, 0))],
        out_specs=pl.BlockSpec((tm, D_MIX), lambda i: (i, 0)),
        out_shape=_sds((T, D_MIX), MXU_DTYPE),
        compiler_params=_cp(("parallel",), 32))(
            proj, proj, proj, proj, proj, proj, proj, proj, o_mla, o_swa, conv_w)


def _outproj_loss(ycat, wo, x, target):
    T, D = x.shape
    K = ycat.shape[1]
    tm = min(TM_PROJ, T)
    nt = T // tm

    def body(y_ref, w_ref, x_ref, t_ref, g_ref, loss_ref, acc_ref):
        i = pl.program_id(0)

        @pl.when(i == 0)
        def _():
            acc_ref[...] = jnp.zeros_like(acc_ref)

        err = _dot(y_ref[...], w_ref[...]) + x_ref[...] - t_ref[...]
        g_ref[...] = err * (1.0 / D)
        acc_ref[...] += _fold_rows8(err * err)

        @pl.when(i == nt - 1)
        def _():
            tot = jnp.sum(jnp.sum(acc_ref[...], axis=1, keepdims=True), axis=0, keepdims=True)
            loss_ref[...] = jnp.broadcast_to(tot * (0.5 / D), (8, LANES))

    tile = pl.BlockSpec((tm, D), lambda i: (i, 0))
    return pl.pallas_call(
        body, name="outproj_loss", grid=(nt,),
        in_specs=[pl.BlockSpec((tm, K), lambda i: (i, 0)), pl.BlockSpec((K, D), lambda i: (0, 0)), tile, tile],
        out_specs=[tile, pl.BlockSpec((8, LANES), lambda i: (0, 0))],
        out_shape=[_sds((T, D), F32), _sds((8, LANES), F32)],
        scratch_shapes=[pltpu.VMEM((8, D), F32)],
        compiler_params=_cp(("arbitrary",), 48))(ycat, wo, x, target)


def _outproj_bwd(g, ycat, wot):
    T, D = g.shape
    K = ycat.shape[1]
    tm = min(512, T)
    nt = T // tm

    def body(g_ref, y_ref, wt_ref, dy_ref, dw_ref, acc_ref):
        i = pl.program_id(0)

        @pl.when(i == 0)
        def _():
            acc_ref[...] = jnp.zeros_like(acc_ref)

        gb = g_ref[...].astype(MXU_DTYPE)
        dy_ref[...] = _dot(gb, wt_ref[...])
        acc_ref[...] += _dot_tn(y_ref[...], gb)

        @pl.when(i == nt - 1)
        def _():
            dw_ref[...] = acc_ref[...].astype(WIRE_DTYPE)

    return pl.pallas_call(
        body, name="outproj_bwd", grid=(nt,),
        in_specs=[pl.BlockSpec((tm, D), lambda i: (i, 0)), pl.BlockSpec((tm, K), lambda i: (i, 0)),
                  pl.BlockSpec((D, K), lambda i: (0, 0))],
        out_specs=[pl.BlockSpec((tm, K), lambda i: (i, 0)), pl.BlockSpec((K, D), lambda i: (0, 0))],
        out_shape=[_sds((T, K), F32), _sds((K, D), WIRE_DTYPE)],
        scratch_shapes=[pltpu.VMEM((K, D), F32)],
        compiler_params=_cp(("arbitrary",), 48))(g, ycat, wot)


def _mix_bwd(dycat, proj, o_mla, o_swa, conv_w):
    T = proj.shape[0]
    tm = min(TM_ROW, T)
    nt = T // tm

    def body(dym_ref, dyc_ref, dys_ref, gm_ref, ch_ref, cb_ref, cc_ref, gc_ref, gs_ref, pch_ref, pcc_ref,
             ndy_ref, ncb_ref, ngc_ref, om_ref, os_ref, w_ref,
             d1_ref, dom_ref, dos_ref, dw_ref):
        i = pl.program_id(0)

        @pl.when(i == 0)
        def _():
            dw_ref[...] = jnp.zeros_like(dw_ref)

        row = lax.broadcasted_iota(jnp.int32, (tm, GROUP_WIDTH), 0)

        def gate(g):
            sg = _sigmoid(g)
            return g * sg, sg * (1.0 + g * (1.0 - sg))

        gm = gm_ref[...]
        silu, dsilu = gate(gm)
        dym = dym_ref[...]
        dom_ref[...] = dym * silu
        d1_ref[:, 0:512] = (dym * om_ref[...] * dsilu).astype(MXU_DTYPE)

        gs = gs_ref[...]
        silu, dsilu = gate(gs)
        dys = dys_ref[...]
        dos_ref[...] = dys * silu
        d1_ref[:, 2560:3072] = (dys * os_ref[...] * dsilu).astype(MXU_DTYPE)

        ch, cb, cc, gc, dyc = ch_ref[...], cb_ref[...], cc_ref[...], gc_ref[...], dyc_ref[...]
        w0, w1, w2 = w_ref[0:1, :], w_ref[1:2, :], w_ref[2:3, :]
        u = cc * ch
        u_prev = jnp.where(i > 0, pcc_ref[...] * pch_ref[...], 0.0)
        u1 = _shift_down(u, u_prev, 1, row)
        u2 = _shift_down(u, u_prev, 2, row)
        z = w0 * u2 + w1 * u1 + w2 * u
        silu, dsilu = gate(gc)
        dz = dyc * cb * silu
        ngc = ngc_ref[...]
        dz_next = jnp.where(i < nt - 1, ndy_ref[...] * ncb_ref[...] * (ngc * _sigmoid(ngc)), 0.0)
        du = w2 * dz + w1 * _shift_up(dz, dz_next, 1, row) + w0 * _shift_up(dz, dz_next, 2, row)
        d1_ref[:, 512:1024] = (du * cc).astype(MXU_DTYPE)
        d1_ref[:, 1024:1536] = (dyc * z * silu).astype(MXU_DTYPE)
        d1_ref[:, 1536:2048] = (du * ch).astype(MXU_DTYPE)
        d1_ref[:, 2048:2560] = (dyc * cb * z * dsilu).astype(MXU_DTYPE)
        row8 = lax.broadcasted_iota(jnp.int32, (8, GROUP_WIDTH), 0)
        dw = jnp.zeros((8, GROUP_WIDTH), F32)
        for t, shifted in enumerate((u2, u1, u)):
            dw = dw + jnp.where(row8 == t, jnp.sum(dz * shifted, axis=0, keepdims=True), 0.0)
        dw_ref[...] += dw

    blk = lambda cb: pl.BlockSpec((tm, 512), lambda i: (i, cb))
    prev = lambda cb: pl.BlockSpec((8, 512), lambda i: (jnp.maximum(i * (tm // 8) - 1, 0), cb))
    nxt = lambda cb: pl.BlockSpec((8, 512), lambda i: (jnp.minimum((i + 1) * (tm // 8), T // 8 - 1), cb))
    tile = pl.BlockSpec((tm, 512), lambda i: (i, 0))
    return pl.pallas_call(
        body, name="mix_bwd", grid=(nt,),
        in_specs=[blk(0), blk(1), blk(2), blk(CB_GMLA), blk(CB_CH), blk(CB_CB), blk(CB_CC), blk(CB_GCONV),
                  blk(CB_GSWA), prev(CB_CH), prev(CB_CC), nxt(1), nxt(CB_CB), nxt(CB_GCONV), tile, tile,
                  pl.BlockSpec((8, 512), lambda i: (0, 0))],
        out_specs=[pl.BlockSpec((tm, 3072), lambda i: (i, DPB_MIX)), tile, tile,
                   pl.BlockSpec((8, 512), lambda i: (0, 0))],
        out_shape=[_sds((T, NP), MXU_DTYPE), _sds((T, 512), F32), _sds((T, 512), F32), _sds((8, 512), F32)],
        compiler_params=_cp(("arbitrary",), 48))(
            dycat, dycat, dycat, proj, proj, proj, proj, proj, proj, proj, proj, dycat, proj, proj,
            o_mla, o_swa, conv_w)


def _swa_bwd(proj, o_swa, do_swa, lw, dproj):
    T = proj.shape[0]
    tm = min(TM_SWA, T)
    nb = tm // BLOCK
    scale = SWA_HEAD_DIM ** -0.5

    def body(q_ref, k_ref, v_ref, pk_ref, pv_ref, o_ref, do_ref, qw_ref, kw_ref, alibi_ref, sink_ref, dproj_in,
             dq_ref, dk_ref, dv_ref, dqw_ref, dsink_ref):
        i = pl.program_id(0)

        @pl.when(i == 0)
        def _():
            dk_ref[...] = jnp.zeros_like(dk_ref)
            dv_ref[...] = jnp.zeros_like(dv_ref)
            dqw_ref[...] = jnp.zeros_like(dqw_ref)
            dsink_ref[...] = jnp.zeros_like(dsink_ref)

        p, p_sink, c = _swa_probs(i, nb, q_ref, k_ref, v_ref, pk_ref, pv_ref, qw_ref, kw_ref, alibi_ref, sink_ref)
        half1, kp, qn, qhat, qr = c["half1"], c["kp"], c["qn"], c["qhat"], c["qr"]
        rows1 = lax.broadcasted_iota(jnp.int32, (LANES, 1), 0) >= 64
        kpt = _swa_kv_variants_t(c["kn"].T, rows1)
        vp = _swa_kv_variants(c["v_all"], half1)
        qw = qw_ref[...]
        rows = [slice(BLOCK * b, BLOCK * (b + 1)) for b in range(nb)]
        keys = [slice(BLOCK * b, BLOCK * (b + 2)) for b in range(nb)]
        dob, dot_b, dd0, dd1 = [], [], [], []
        for j in range(4):
            cols = slice(LANES * j, LANES * (j + 1))
            do = do_ref[:, cols]
            do_t = do.T
            prod_t = do_t * o_ref[:, cols].T
            dob.append(do.astype(MXU_DTYPE))
            dot_b.append(do_t.astype(MXU_DTYPE))
            dd0.append(jnp.sum(jnp.where(rows1, 0.0, prod_t), axis=0, keepdims=True))
            dd1.append(jnp.sum(jnp.where(rows1, prod_t, 0.0), axis=0, keepdims=True))
        dd = jnp.stack([(dd1 if h % 2 else dd0)[h // 2][:, rows[b]] for b in range(nb) for h in range(HEADS)])
        dp = jnp.stack([_dot(vp[(h // 4, h % 2)][keys[b]], dot_b[h // 2][:, rows[b]])
                        for b in range(nb) for h in range(HEADS)])
        ds = (p * (dp - dd) * scale).astype(MXU_DTYPE)
        dsink = -jnp.sum(p_sink * dd, axis=2, keepdims=True)
        pb = p.astype(MXU_DTYPE)

        dqw = jnp.zeros((1, LANES), F32)
        for j in range(4):
            g = j // 2
            dqn_t = [_dot(kpt[(g, 0)][:, keys[b]], ds[HEADS * b + 2 * j])
                     + _dot(kpt[(g, 1)][:, keys[b]], ds[HEADS * b + 2 * j + 1]) for b in range(nb)]
            dqn = (jnp.concatenate(dqn_t, axis=1) if nb > 1 else dqn_t[0]).T
            dqw = dqw + jnp.sum(dqn * qhat[j], axis=0, keepdims=True)
            dq_ref[:, LANES * j:LANES * (j + 1)] = _rms_halves_bwd(dqn, qhat[j], qr[j], qw, half1).astype(MXU_DTYPE)
        dqw_ref[...] += _row0(dqw + pltpu.roll(dqw, 64, 1))

        dk_tot = jnp.zeros((tm + BLOCK, LANES), F32)
        dv_tot = jnp.zeros((tm + BLOCK, LANES), F32)
        for b in range(nb):
            dk_b = jnp.zeros((2 * BLOCK, LANES), F32)
            dv_b = jnp.zeros((2 * BLOCK, LANES), F32)
            for g in range(2):
                for r in range(2):
                    own = half1 if r else jnp.logical_not(half1)
                    ha, hb = HEADS * b + 4 * g + r, HEADS * b + 4 * g + 2 + r
                    qa, qb = qn[2 * g][rows[b]], qn[2 * g + 1][rows[b]]
                    da, db = dob[2 * g][rows[b]], dob[2 * g + 1][rows[b]]
                    dkp = jnp.where(own, _dot(ds[ha], qa) + _dot(ds[hb], qb), 0.0)
                    dvp = jnp.where(own, _dot(pb[ha], da) + _dot(pb[hb], db), 0.0)
                    if g != r:
                        dkp = pltpu.roll(dkp, 64, 1)
                        dvp = pltpu.roll(dvp, 64, 1)
                    dk_b = dk_b + dkp
                    dv_b = dv_b + dvp
            pad = lambda x: jnp.concatenate(
                [z for z in (jnp.zeros((BLOCK * b, LANES), F32), x, jnp.zeros((BLOCK * (nb - 1 - b), LANES), F32))
                 if z.shape[0]], axis=0)
            dk_tot = dk_tot + pad(dk_b)
            dv_tot = dv_tot + pad(dv_b)
        dst = pl.ds(pl.multiple_of(i * tm, BLOCK), tm + BLOCK)
        dk_ref[dst, :] += dk_tot
        dv_ref[dst, :] += dv_tot

        row8 = lax.broadcasted_iota(jnp.int32, (8, LANES), 0)
        dsink_tile = jnp.zeros((8, LANES), F32)
        for b in range(nb):
            for h in range(HEADS):
                dsink_tile = dsink_tile + jnp.where(row8 == h, jnp.broadcast_to(dsink[HEADS * b + h], (8, LANES)), 0.0)
        dsink_ref[...] += dsink_tile

    prev = lambda cb: pl.BlockSpec((BLOCK, LANES), lambda i: (jnp.maximum(i * nb - 1, 0), cb))
    tile = pl.BlockSpec((tm, 512), lambda i: (i, 0))
    small = pl.BlockSpec((8, LANES), lambda i: (0, 0))
    acc = pl.BlockSpec((T + BLOCK, LANES), lambda i: (0, 0))
    return pl.pallas_call(
        body, name="swa_bwd", grid=(T // tm,),
        in_specs=[pl.BlockSpec((tm, 512), lambda i: (i, CB_SQ)), pl.BlockSpec((tm, LANES), lambda i: (i, CB_SK)),
                  pl.BlockSpec((tm, LANES), lambda i: (i, CB_SV)), prev(CB_SK), prev(CB_SV), tile, tile,
                  pl.BlockSpec((1, LANES), lambda i: (0, 0)), pl.BlockSpec((1, LANES), lambda i: (0, 0)),
                  pl.BlockSpec((nb * HEADS, 2 * BLOCK, BLOCK), lambda i: (0, 0, 0)),
                  pl.BlockSpec(memory_space=pltpu.SMEM), pl.BlockSpec(memory_space=pl.ANY)],
        out_specs=[pl.BlockSpec((tm, 512), lambda i: (i, DPB_SQ)), acc, acc, small, small],
        out_shape=[_sds((T, NP), MXU_DTYPE), _sds((T + BLOCK, LANES), F32), _sds((T + BLOCK, LANES), F32),
                   _sds((8, LANES), F32), _sds((8, LANES), F32)],
        input_output_aliases={11: 0},
        compiler_params=_cp(("arbitrary",), 48))(
            proj, proj, proj, proj, proj, o_swa, do_swa, lw["sqn"], lw["skn"], jnp.tile(_swa_alibi(), (nb, 1, 1)),
            lw["sinks"], dproj)


def _swa_kv_bwd(proj, dkn, dv, lw, dproj):
    T = proj.shape[0]
    tm = min(TM_SWA, T)
    dkn, dv = dkn[BLOCK:], dv[BLOCK:]

    def body(k_ref, dkn_ref, dv_ref, kw_ref, dproj_in, d_ref, dkw_ref):
        i = pl.program_id(0)

        @pl.when(i == 0)
        def _():
            dkw_ref[...] = jnp.zeros_like(dkw_ref)

        half1 = lax.broadcasted_iota(jnp.int32, (1, LANES), 1) >= 64
        khat, kr = _rms_halves(k_ref[...], half1)
        dkn_t = dkn_ref[...]
        dkw = jnp.sum(dkn_t * khat, axis=0, keepdims=True)
        dkw_ref[...] += _row0(dkw + pltpu.roll(dkw, 64, 1))
        d_ref[:, 0:LANES] = _rms_halves_bwd(dkn_t, khat, kr, kw_ref[...], half1).astype(MXU_DTYPE)
        d_ref[:, LANES:2 * LANES] = dv_ref[...].astype(MXU_DTYPE)

    return pl.pallas_call(
        body, name="swa_kv_bwd", grid=(T // tm,),
        in_specs=[pl.BlockSpec((tm, LANES), lambda i: (i, CB_SK)), pl.BlockSpec((tm, LANES), lambda i: (i, 0)),
                  pl.BlockSpec((tm, LANES), lambda i: (i, 0)), pl.BlockSpec((1, LANES), lambda i: (0, 0)),
                  pl.BlockSpec(memory_space=pl.ANY)],
        out_specs=[pl.BlockSpec((tm, 2 * LANES), lambda i: (i, DPB_SKV)), pl.BlockSpec((8, LANES), lambda i: (0, 0))],
        out_shape=[_sds((T, NP), MXU_DTYPE), _sds((8, LANES), F32)],
        input_output_aliases={4: 0},
        compiler_params=_cp(("arbitrary",), 32))(proj, dkn, dv, lw["skn"], dproj)


def _mla_attn_bwd(q, k, kt, vt, o, do, lse):
    T = q.shape[1]
    tk = min(TK, T // 2)
    tq = 2 * tk

    def body(q_ref, k_ref, kt_ref, vt_ref, o_ref, do_ref, lse_ref, dq_ref, dk_ref, dv_ref, dq_s, lse_s, dd_s,
             s_a, s_b, p_a, p_b):
        h = pl.program_id(0)
        i = pl.program_id(1)

        @pl.when(i == 0)
        def _():
            dk_ref[...] = jnp.zeros_like(dk_ref)
            dv_ref[...] = jnp.zeros_like(dv_ref)

        qry = lax.broadcasted_iota(jnp.int32, (tq, tk), 0)
        key = lax.broadcasted_iota(jnp.int32, (tq, tk), 1)
        own = (lax.broadcasted_iota(jnp.int32, (1, LANES), 1) // 64) == (h % 2)
        do_own = jnp.where(own, do_ref[...], 0.0)
        dob = do_own.astype(MXU_DTYPE)
        dob_t = do_own.T.astype(MXU_DTYPE)
        qh = q_ref[0]
        qh_t = qh.astype(F32).T.astype(MXU_DTYPE)
        dd_col = jnp.sum(do_own * o_ref[...], axis=-1, keepdims=True)
        lse_col = jnp.broadcast_to(lse_ref[0], (LANES, tq)).T
        for c in range(tk // LANES):
            lse_s[:, LANES * c:LANES * (c + 1)] = lse_col
            dd_s[:, LANES * c:LANES * (c + 1)] = jnp.broadcast_to(dd_col, (tq, LANES))
        dq_s[...] = jnp.zeros_like(dq_s)

        def scores(kj, s_buf, p_buf):
            s_buf[...] = _dot(qh, kt_ref[0, kj])
            p_buf[...] = _dot(dob, vt_ref[0, kj])

        def consume(kj, s_buf, p_buf, diag):
            rows = pl.ds(pl.multiple_of(kj * tk, tk), tk)
            s = s_buf[...]
            if diag is not None:
                s = jnp.where(key + diag * tk <= qry, s, NEG_INF)
            p = jnp.exp2(s - lse_s[...])
            ds = (p * (p_buf[...] - dd_s[...])).astype(MXU_DTYPE)
            dq_s[...] += _dot(ds, k_ref[0, rows, :])
            dk_ref[0, kj] += _dot(qh_t, ds)
            dv_ref[0, kj] += _dot(dob_t, p.astype(MXU_DTYPE))

        scores(0, s_a, p_a)

        def pair(kj):
            scores(kj + 1, s_b, p_b)
            consume(kj, s_a, p_a, None)
            scores(kj + 2, s_a, p_a)
            consume(kj + 1, s_b, p_b, None)

        def octet(ko, carry):
            for t in range(4):
                pair(8 * ko + 2 * t)
            return carry

        lax.fori_loop(0, i // 4, octet, 0)

        @pl.when(i % 4 >= 2)
        def _():
            pair(8 * (i // 4))
            pair(8 * (i // 4) + 2)

        @pl.when(i % 2 == 1)
        def _():
            pair(2 * i - 2)

        scores(2 * i + 1, s_b, p_b)
        consume(2 * i, s_a, p_a, 0)
        consume(2 * i + 1, s_b, p_b, 1)
        dq_ref[0] = dq_s[...]

    res = pl.BlockSpec((1, T, LANES), lambda h, i: (h, 0, 0))
    res_t = pl.BlockSpec((1, T // tk, LANES, tk), lambda h, i: (h, 0, 0, 0))
    buf = pltpu.VMEM((tq, tk), F32)
    acc_t = _sds((HEADS, T // tk, LANES, tk), F32)
    return pl.pallas_call(
        body, name="mla_attn_bwd", grid=(HEADS, T // tq),
        in_specs=[pl.BlockSpec((1, tq, LANES), lambda h, i: (h, i, 0)), res, res_t, res_t,
                  pl.BlockSpec((tq, LANES), lambda h, i: (i, h // 2)),
                  pl.BlockSpec((tq, LANES), lambda h, i: (i, h // 2)),
                  pl.BlockSpec((1, 1, tq), lambda h, i: (h, 0, i))],
        out_specs=[pl.BlockSpec((1, tq, LANES), lambda h, i: (h, i, 0)), res_t, res_t],
        out_shape=[_sds((HEADS, T, LANES), F32), acc_t, acc_t],
        scratch_shapes=[pltpu.VMEM((tq, LANES), F32), buf, buf, buf, buf, buf, buf],
        compiler_params=_cp(("parallel", "arbitrary"), 48))(q, k, kt, vt, o, do, lse)


def _mla_prep_bwd(proj, dq, dk, dv, lw, rope, dproj):
    T = proj.shape[0]
    tm = min(TK, T // 2)

    def body(ql_ref, kvl_ref, kr_ref, dq_ref, dk_ref, dv_ref, qa_ref, kva_ref, wq_ref, wk_ref, wv_ref,
             wqt_ref, wkt_ref, wvt_ref, qn_ref, kn_ref, c_ref, s1_ref, s2_ref, dproj_in,
             d_ref, dwq_ref, dwk_ref, dwv_ref, dqa_ref, dkva_ref, dqn_ref, dkn_ref):
        i = pl.program_id(0)

        @pl.when(i == 0)
        def _():
            for ref in (dwq_ref, dwk_ref, dwv_ref, dqa_ref, dkva_ref, dqn_ref, dkn_ref):
                ref[...] = jnp.zeros_like(ref)

        c, s1, s2 = c_ref[...], s1_ref[...], s2_ref[...]
        lane = lax.broadcasted_iota(jnp.int32, (1, LANES), 1)
        qlhat, qlr = _rms(ql_ref[...], MLA_Q_LORA)
        qn = (qlhat * qa_ref[...]).astype(MXU_DTYPE)
        kvhat, kvr = _rms(kvl_ref[...], MLA_KV_LORA)
        kvn = (kvhat * kva_ref[...]).astype(MXU_DTYPE)
        kr = kr_ref[...]
        x3, r3 = _rms(jnp.stack([_dot(qn, wq_ref[h]) for h in range(HEADS)]), MLA_QK)
        dy3 = _rope_bwd(dq_ref[...] * MLA_SCALE, c, s1, s2)
        dqw = jnp.sum(jnp.sum(dy3 * x3, axis=0), axis=0, keepdims=True)
        dx3 = _rms_bwd(dy3, x3, r3, qn_ref[...], MLA_QK).astype(MXU_DTYPE)
        dqnl = jnp.zeros((tm, MLA_Q_LORA), F32)
        for h in range(HEADS):
            dwq_ref[h] += _dot_tn(qn, dx3[h])
            dqnl = dqnl + _dot(dx3[h], wqt_ref[h])

        x3, r3 = _rms(jnp.stack([_dot(kvn, wk_ref[h]) for h in range(HEADS)]) + kr, MLA_QK)
        dy3 = _rope_bwd(jnp.stack([dk_ref[h, 0].T for h in range(HEADS)]) * LN2, c, s1, s2)
        dkw = jnp.sum(jnp.sum(dy3 * x3, axis=0), axis=0, keepdims=True)
        dxf3 = _rms_bwd(dy3, x3, r3, kn_ref[...], MLA_QK)
        dkr = jnp.sum(dxf3, axis=0)
        dx3 = dxf3.astype(MXU_DTYPE)
        dkvn = jnp.zeros((tm, MLA_KV_LORA), F32)
        for h in range(HEADS):
            dwk_ref[h] += _dot_tn(kvn, dx3[h])
            dkvn = dkvn + _dot(dx3[h], wkt_ref[h])
        dvc = jnp.concatenate([(dv_ref[2 * j, 0] + dv_ref[2 * j + 1, 0]).T for j in range(4)],
                              axis=1).astype(MXU_DTYPE)
        dwv_ref[...] += _dot_tn(kvn, dvc)
        dkvn = dkvn + _dot(dvc, wvt_ref[...])
        dqa_ref[...] += _row0(jnp.sum(dqnl * qlhat, axis=0, keepdims=True))
        dkva_ref[...] += _row0(jnp.sum(dkvn * kvhat, axis=0, keepdims=True))
        dqn_ref[...] += _row0(dqw)
        dkn_ref[...] += _row0(dkw)
        d_ref[:, 0:256] = _rms_bwd(dqnl, qlhat, qlr, qa_ref[...], MLA_Q_LORA).astype(MXU_DTYPE)
        d_ref[:, 256:384] = _rms_bwd(dkvn, kvhat, kvr, kva_ref[...], MLA_KV_LORA).astype(MXU_DTYPE)
        d_ref[:, 384:512] = jnp.where((lane >= 64) & (lane < 96), dkr, 0.0).astype(MXU_DTYPE)

    full = lambda shape: pl.BlockSpec(shape, lambda i: (0,) * len(shape))
    hd = pl.BlockSpec((HEADS, tm, LANES), lambda i: (0, i, 0))
    hdt = pl.BlockSpec((HEADS, 1, LANES, tm), lambda i: (0, i, 0, 0))
    tab = pl.BlockSpec((tm, LANES), lambda i: (i, 0))
    return pl.pallas_call(
        body, name="mla_prep_bwd", grid=(T // tm,),
        in_specs=[pl.BlockSpec((tm, 256), lambda i: (i, CB_QLAT)), pl.BlockSpec((tm, LANES), lambda i: (i, CB_KVLAT)),
                  pl.BlockSpec((tm, LANES), lambda i: (i, CB_KROPE)), hd, hdt, hdt,
                  full((1, 256)), full((1, LANES)), full((HEADS, 256, LANES)), full((HEADS, LANES, LANES)),
                  full((LANES, 512)), full((HEADS, LANES, 256)), full((HEADS, LANES, LANES)), full((512, LANES)),
                  full((1, LANES)), full((1, LANES)), tab, tab, tab, pl.BlockSpec(memory_space=pl.ANY)],
        out_specs=[pl.BlockSpec((tm, 512), lambda i: (i, DPB_MLA)), full((HEADS, 256, LANES)),
                   full((HEADS, LANES, LANES)), full((LANES, 512)), full((8, 256)), full((8, LANES)),
                   full((8, LANES)), full((8, LANES))],
        out_shape=[_sds((T, NP), MXU_DTYPE), _sds((HEADS, 256, LANES), F32), _sds((HEADS, LANES, LANES), F32),
                   _sds((LANES, 512), F32), _sds((8, 256), F32), _sds((8, LANES), F32), _sds((8, LANES), F32),
                   _sds((8, LANES), F32)],
        input_output_aliases={19: 0},
        compiler_params=_cp(("arbitrary",), 48))(
            proj, proj, proj, dq, dk, dv, lw["qa"], lw["kva"], lw["wq"], lw["wk"], lw["wv"],
            lw["wqt"], lw["wkt"], lw["wvt"], lw["qn"], lw["kn"], rope[0], rope[1], rope[2], dproj)


def _inproj_bwd_dx(dproj, wpt, x, g_in, ng):
    T, D = x.shape
    tm = min(TM_PROJ, T)

    def body(dp_ref, wt_ref, x_ref, g_ref, w_ref, dx_ref, dw_ref):
        i = pl.program_id(0)

        @pl.when(i == 0)
        def _():
            dw_ref[...] = jnp.zeros_like(dw_ref)

        dh = _dot(dp_ref[...], wt_ref[...])
        xhat, r = _rms(x_ref[...], D)
        dw_ref[...] += _row0(jnp.sum(dh * xhat, axis=0, keepdims=True))
        dx_ref[...] = g_ref[...] + _rms_bwd(dh, xhat, r, w_ref[...], D)

    tile = pl.BlockSpec((tm, D), lambda i: (i, 0))
    return pl.pallas_call(
        body, name="inproj_bwd_dx", grid=(T // tm,),
        in_specs=[pl.BlockSpec((tm, NP), lambda i: (i, 0)), pl.BlockSpec((NP, D), lambda i: (0, 0)), tile, tile,
                  pl.BlockSpec((1, D), lambda i: (0, 0))],
        out_specs=[tile, pl.BlockSpec((8, D), lambda i: (0, 0))],
        out_shape=[_sds((T, D), F32), _sds((8, D), F32)],
        compiler_params=_cp(("arbitrary",), 48))(dproj, wpt, x, g_in, ng)


def _rope_tables(T, token=0.0):
    half = MLA_ROPE // 2
    inv_freq = jnp.power(jnp.float32(ROPE_THETA), -jnp.arange(half, dtype=F32) / half)
    z = lambda n: jnp.zeros((n,), F32)
    freq = jnp.concatenate([z(MLA_NOPE), inv_freq, inv_freq, z(32)])
    first = jnp.concatenate([z(64), jnp.ones((16,), F32), z(48)])
    second = jnp.concatenate([z(80), jnp.ones((16,), F32), z(32)])
    ang = (jnp.arange(T, dtype=F32) + token)[:, None] * freq[None, :]
    sin = jnp.sin(ang)
    return jnp.cos(ang), -sin * first[None, :], sin * second[None, :]


def _pad_lanes(v, n=LANES):
    v = v.reshape(1, -1)
    return jnp.pad(v, ((0, 0), (0, n - v.shape[1])))


def _pack_win_t(wt):
    z = lambda n: jnp.zeros((n, wt.shape[1]), wt.dtype)
    return jnp.concatenate([wt[416:2976], wt[3744:4256], wt[0:384], z(64), wt[384:416], z(32), wt[2976:3488],
                            wt[3488:3616], wt[3616:3744]], axis=0)


def _unpack_dwin(d):
    return jnp.concatenate([d[:, 3072:3456], d[:, 3520:3552], d[:, 0:2560], d[:, 3584:4096], d[:, 4096:4224],
                            d[:, 4224:4352], d[:, 2560:3072]], axis=1)


def _inproj_weights(l, norm_g, w_in_t):
    wpt = _pack_win_t(w_in_t)
    return dict(ng=norm_g[l].reshape(1, -1), wp=wpt.T, wpt=wpt)


def _mixer_weights(l, qa, wqb_full, kva, wkvb_full, qn, kn, conv_full, sqn, skn, sinks, w_out_full):
    wq = jnp.pad(wqb_full, ((0, 0), (0, 0), (0, LANES - MLA_QK)))
    wk = jnp.pad(wkvb_full[:, :, :MLA_NOPE], ((0, 0), (0, 0), (0, LANES - MLA_NOPE)))
    wv = jnp.transpose(wkvb_full[:, :, MLA_NOPE:], (1, 0, 2)).reshape(MLA_KV_LORA, GROUP_WIDTH)
    return dict(
        qa=qa[l].reshape(1, -1), kva=kva[l].reshape(1, -1),
        wq=wq, wk=wk, wv=wv, wqt=jnp.transpose(wq, (0, 2, 1)), wkt=jnp.transpose(wk, (0, 2, 1)), wvt=wv.T,
        qn=_pad_lanes(qn[l]), kn=_pad_lanes(kn[l]),
        conv=jnp.pad(conv_full, ((0, 5), (0, 0))),
        sqn=jnp.tile(sqn[l].reshape(1, -1), (1, 2)), skn=jnp.tile(skn[l].reshape(1, -1), (1, 2)),
        sinks=sinks[l], wo=w_out_full, wot=w_out_full.T)


def _layer_weights(l, norm_g, w_in_full, qa, wqb_full, kva, wkvb_full, qn, kn, conv_full, sqn, skn, sinks,
                   w_out_full):
    return dict(_inproj_weights(l, norm_g, w_in_full.T),
                **_mixer_weights(l, qa, wqb_full, kva, wkvb_full, qn, kn, conv_full, sqn, skn, sinks, w_out_full))


def _layer_fwd(x, lw, rope, late_weights=None, target=None):
    proj, h = _inproj_fwd(x, lw["ng"], lw["wp"])
    if late_weights is not None:
        lw = dict(lw, **late_weights(proj))
    q, k, kt, vt = _mla_prep_fwd(proj, lw, rope)
    o_mla, lse = _mla_attn_fwd(q, k, vt)
    o_swa = _swa_fwd(proj, lw)
    ycat = _mix_fwd(proj, o_mla, o_swa, lw["conv"])
    if target is None:
        out = _mm_nn(ycat, lw["wo"], "outproj_fwd", residual=x)
    else:
        out = _outproj_loss(ycat, lw["wo"], x, target)
    return out, dict(x=x, proj=proj, h=h, q=q, k=k, kt=kt, vt=vt, o_mla=o_mla, lse=lse, o_swa=o_swa, ycat=ycat,
                     lw=lw)


def _layer_bwd(g, sv, lw, rope, on_big_grads=None):
    proj = sv["proj"]
    dycat, d_wo = _outproj_bwd(g, sv["ycat"], lw["wot"])
    dproj, do_mla, do_swa, d_conv = _mix_bwd(dycat, proj, sv["o_mla"], sv["o_swa"], lw["conv"])
    dproj, dkn_acc, dv_acc, d_sqn, d_sinks = _swa_bwd(proj, sv["o_swa"], do_swa, lw, dproj)
    dproj, d_skn = _swa_kv_bwd(proj, dkn_acc, dv_acc, lw, dproj)
    dq, dk, dv = _mla_attn_bwd(sv["q"], sv["k"], sv["kt"], sv["vt"], sv["o_mla"], do_mla, sv["lse"])
    dproj, d_wq, d_wk, d_wv, d_qa, d_kva, d_qn, d_kn = _mla_prep_bwd(proj, dq, dk, dv, lw, rope, dproj)
    grads = dict(
        w_out=d_wo, w_qb=d_wq[:, :, :MLA_QK],
        w_kvb=jnp.concatenate([d_wk[:, :, :MLA_NOPE],
                               jnp.transpose(d_wv.reshape(MLA_KV_LORA, HEADS, MLA_NOPE), (1, 0, 2))], axis=2))
    token = 0.0 if on_big_grads is None else on_big_grads("mixer", grads)
    d_wp = _mm_tn(sv["h"], dproj, "inproj_bwd_dw", WIRE_DTYPE, tn=NP // 2)
    grads["w_in"] = _unpack_dwin(d_wp)
    token = token if on_big_grads is None else token + on_big_grads("w_in", grads)
    dx, d_ng = _inproj_bwd_dx(dproj, lw["wpt"], sv["x"], g, lw["ng"] + token)
    grads.update(
        conv=d_conv[0:3], norm_g=d_ng[0], qa=d_qa[0], kva=d_kva[0], qn=d_qn[0, :MLA_QK], kn=d_kn[0, :MLA_QK],
        sqn=d_sqn[0, :SWA_HEAD_DIM], skn=d_skn[0, :SWA_HEAD_DIM], sinks=d_sinks[:, 0])
    return dx, grads


def _local_step(x, target, lws, rope):
    saved = []
    for l, lw in enumerate(lws):
        x, sv = _layer_fwd(x, lw, rope, target=target if l == len(lws) - 1 else None)
        saved.append(sv)
    g, loss_tile = x
    grads = [None] * len(lws)
    for l in reversed(range(len(lws))):
        g, grads[l] = _layer_bwd(g, saved[l], lws[l], rope)
    return loss_tile, g, grads


def _my_coords():
    return lax.axis_index("x"), lax.axis_index("y"), lax.axis_index("c")


def _peer(me, k):
    x, y, c = me
    return (1 - x if k & 4 else x, 1 - y if k & 2 else y, 1 - c if k & 1 else c)


def _lin(d):
    return 4 * d[0] + 2 * d[1] + d[2]


def _push_copies(ins, lands, send_sems, recv_sems, gather, incoming=False):
    me = _my_coords()
    my = _lin(me)
    copies = []
    for a in range(len(ins)):
        for k in range(1, N_DEV):
            peer = _peer(me, k)
            src = ins[a] if gather else ins[a].at[_lin(peer)]
            copies.append(pltpu.make_async_remote_copy(
                src_ref=src, dst_ref=lands[a].at[_lin(peer) if incoming else my],
                send_sem=send_sems.at[a * 7 + k - 1], recv_sem=recv_sems.at[a * 7 + k - 1],
                device_id=peer, device_id_type=pl.DeviceIdType.MESH))
    return copies


def _push_start(arrays, name, gather):
    n = len(arrays)
    land_shapes = [((N_DEV,) + a.shape) if gather else a.shape for a in arrays]

    def body(*refs):
        ins, lands = refs[:n], refs[n:2 * n]
        send_sems, recv_sems = refs[2 * n], refs[2 * n + 1]
        token = refs[-1]
        for cp in _push_copies(ins, lands, send_sems, recv_sems, gather):
            cp.start()
        token[...] = jnp.zeros_like(token)

    hbm = pl.BlockSpec(memory_space=pltpu.HBM)
    sem = pl.BlockSpec(memory_space=pltpu.SEMAPHORE)
    res = pl.pallas_call(
        body, name=name,
        out_shape=(pltpu.SemaphoreType.DMA((7 * n,)), pltpu.SemaphoreType.DMA((7 * n,)),
                   *[pltpu.HBM(a.shape, a.dtype) for a in arrays],
                   *[pltpu.HBM(s, a.dtype) for s, a in zip(land_shapes, arrays)],
                   _sds((8, LANES), F32)),
        in_specs=(hbm,) * (2 * n),
        out_specs=(sem, sem) + (hbm,) * (2 * n) + (pl.BlockSpec(memory_space=pltpu.VMEM),),
        input_output_aliases={i: 2 + i for i in range(2 * n)},
        compiler_params=pltpu.CompilerParams(has_side_effects=pltpu.SideEffectType.DATAFLOW_SIDE_EFFECTING),
    )(*[pltpu.with_memory_space_constraint(a, pltpu.HBM) for a in arrays],
      *[pltpu.with_memory_space_constraint(lax.empty(s, a.dtype), pltpu.HBM) for s, a in zip(land_shapes, arrays)])
    return dict(send=res[0], recv=res[1], src=res[2:2 + n], land=res[2 + n:2 + 2 * n], token=res[-1][0, 0],
                gather=gather)


def _push_wait(handle, after, name):
    n = len(handle["src"])
    gather = handle["gather"]

    def body(*refs):
        ins, lands = refs[:n], refs[n:2 * n]
        send_sems, recv_sems = refs[2 * n], refs[2 * n + 1]
        for cp in _push_copies(ins, lands, send_sems, recv_sems, gather):
            cp.wait_send()
        for cp in _push_copies(ins, lands, send_sems, recv_sems, gather, incoming=True):
            cp.wait_recv()

    hbm = pl.BlockSpec(memory_space=pltpu.HBM)
    sem = pl.BlockSpec(memory_space=pltpu.SEMAPHORE)
    res = pl.pallas_call(
        body, name=name,
        out_shape=tuple(pltpu.HBM(a.shape, a.dtype) for a in (*handle["src"], *handle["land"])),
        in_specs=(hbm,) * (2 * n) + (sem, sem, pl.BlockSpec(memory_space=pl.ANY)),
        out_specs=(hbm,) * (2 * n),
        input_output_aliases={i: i for i in range(2 * n)},
        compiler_params=pltpu.CompilerParams(has_side_effects=pltpu.SideEffectType.DATAFLOW_SIDE_EFFECTING),
    )(*handle["src"], *handle["land"], handle["send"], handle["recv"], after)
    return res[n:]


def _small_all_reduce(v):
    R = v.shape[0]

    def body(v_ref, o_ref, buf, send_sems, recv_sems):
        me = _my_coords()
        my = _lin(me)
        sends = []
        for k in range(1, N_DEV):
            cp = pltpu.make_async_remote_copy(
                src_ref=v_ref, dst_ref=buf.at[my], send_sem=send_sems.at[k - 1], recv_sem=recv_sems.at[k - 1],
                device_id=_peer(me, k), device_id_type=pl.DeviceIdType.MESH)
            cp.start()
            sends.append(cp)
        buf[my] = v_ref[...]
        for k in range(1, N_DEV):
            pltpu.make_async_remote_copy(
                src_ref=v_ref, dst_ref=buf.at[_lin(_peer(me, k))], send_sem=send_sems.at[k - 1],
                recv_sem=recv_sems.at[k - 1], device_id=_peer(me, k),
                device_id_type=pl.DeviceIdType.MESH).wait_recv()
        for cp in sends:
            cp.wait_send()
        tot = buf[0]
        for d in range(1, N_DEV):
            tot = tot + buf[d]
        o_ref[...] = tot

    vm = pl.BlockSpec(memory_space=pltpu.VMEM)
    return pl.pallas_call(
        body, name="small_all_reduce", in_specs=[vm], out_specs=vm, out_shape=_sds(v.shape, F32),
        scratch_shapes=[pltpu.VMEM((N_DEV, R, LANES), F32), pltpu.SemaphoreType.DMA((7,)),
                        pltpu.SemaphoreType.DMA((7,))],
    )(v)


def _adamw_math(w, g, m, v):
    m = ADAM_B1 * m + (1.0 - ADAM_B1) * g
    v = ADAM_B2 * v + (1.0 - ADAM_B2) * (g * g)
    m_hat = m / (1.0 - ADAM_B1 ** ADAM_STEP)
    v_hat = v / (1.0 - ADAM_B2 ** ADAM_STEP)
    delta = -ADAM_LR * (m_hat / (jnp.sqrt(v_hat) + ADAM_EPS) + ADAM_WD * w)
    return delta, m, v


def _adamw(parts, w, m, v, name, tr):
    P, R, C = parts.shape
    tr = min(tr, R)

    def body(p_ref, w_ref, m_ref, v_ref, g_out, d_out, m_out, v_out):
        g = p_ref[0].astype(F32)
        for d in range(1, P):
            g = g + p_ref[d].astype(F32)
        delta, m_new, v_new = _adamw_math(w_ref[...], g, m_ref[...], v_ref[...])
        g_out[...] = g
        d_out[...] = delta
        m_out[...] = m_new
        v_out[...] = v_new

    tile = pl.BlockSpec((tr, C), lambda i: (i, 0))
    return pl.pallas_call(
        body, name=name, grid=(R // tr,),
        in_specs=[pl.BlockSpec((P, tr, C), lambda i: (0, i, 0)), tile, tile, tile],
        out_specs=[tile] * 4, out_shape=[_sds((R, C), F32)] * 4,
        compiler_params=_cp(("parallel",), 32))(parts, w, m, v)


SMALL = (("norm_g", D_MODEL), ("mla_q_a_norm", MLA_Q_LORA), ("mla_kv_a_norm", MLA_KV_LORA), ("mla_q_norm", MLA_QK),
         ("mla_k_norm", MLA_QK), ("swa_q_norm", SWA_HEAD_DIM), ("swa_k_norm", SWA_HEAD_DIM), ("swa_sinks", HEADS))
SMALL_GRAD_KEY = dict(norm_g="norm_g", mla_q_a_norm="qa", mla_kv_a_norm="kva", mla_q_norm="qn", mla_k_norm="kn",
                      swa_q_norm="sqn", swa_k_norm="skn", swa_sinks="sinks")
SMALL_ROWS = 32
CONV_ROWS = 24


def _pack_small(get):
    parts = []
    for l in range(DEPTH):
        for name, n in SMALL:
            v = get(name, l).reshape(-1)
            parts.append(jnp.pad(v, (0, (-n) % LANES)))
    return jnp.concatenate(parts).reshape(SMALL_ROWS, LANES)


def _unpack_small(packed):
    flat = packed.reshape(-1)
    out = {name: [] for name, _ in SMALL}
    off = 0
    for l in range(DEPTH):
        for name, n in SMALL:
            out[name].append(flat[off:off + n])
            off += n + (-n) % LANES
    return {name: jnp.stack(v) for name, v in out.items()}


def kernel(x, norm_g, w_in, mla_q_a_norm, mla_w_qb, mla_kv_a_norm, mla_w_kvb, mla_q_norm, mla_k_norm, conv_w, swa_q_norm, swa_k_norm, swa_sinks, w_out, loss_target, m_norm_g, m_w_in, m_mla_q_a_norm, m_mla_w_qb, m_mla_kv_a_norm, m_mla_w_kvb, m_mla_q_norm, m_mla_k_norm, m_conv_w, m_swa_q_norm, m_swa_k_norm, m_swa_sinks, m_w_out, v_norm_g, v_w_in, v_mla_q_a_norm, v_mla_w_qb, v_mla_kv_a_norm, v_mla_w_kvb, v_mla_q_norm, v_mla_k_norm, v_conv_w, v_swa_q_norm, v_swa_k_norm, v_swa_sinks, v_w_out):
    T = x.shape[1]
    weights = dict(norm_g=norm_g, w_in=w_in, mla_q_a_norm=mla_q_a_norm, mla_w_qb=mla_w_qb,
                   mla_kv_a_norm=mla_kv_a_norm, mla_w_kvb=mla_w_kvb, mla_q_norm=mla_q_norm, mla_k_norm=mla_k_norm,
                   conv_w=conv_w, swa_q_norm=swa_q_norm, swa_k_norm=swa_k_norm, swa_sinks=swa_sinks, w_out=w_out)
    mom_m = dict(norm_g=m_norm_g, w_in=m_w_in, mla_q_a_norm=m_mla_q_a_norm, mla_w_qb=m_mla_w_qb,
                 mla_kv_a_norm=m_mla_kv_a_norm, mla_w_kvb=m_mla_w_kvb, mla_q_norm=m_mla_q_norm,
                 mla_k_norm=m_mla_k_norm, conv_w=m_conv_w, swa_q_norm=m_swa_q_norm, swa_k_norm=m_swa_k_norm,
                 swa_sinks=m_swa_sinks, w_out=m_w_out)
    mom_v = dict(norm_g=v_norm_g, w_in=v_w_in, mla_q_a_norm=v_mla_q_a_norm, mla_w_qb=v_mla_w_qb,
                 mla_kv_a_norm=v_mla_kv_a_norm, mla_w_kvb=v_mla_w_kvb, mla_q_norm=v_mla_q_norm,
                 mla_k_norm=v_mla_k_norm, conv_w=v_conv_w, swa_q_norm=v_swa_q_norm, swa_k_norm=v_swa_k_norm,
                 swa_sinks=v_swa_sinks, w_out=v_w_out)

    my = _lin(_my_coords())

    def shards(l):
        return [w_in[l].astype(MXU_DTYPE).T, mla_w_qb[l].astype(MXU_DTYPE), mla_w_kvb[l].astype(MXU_DTYPE),
                w_out[l].astype(MXU_DTYPE), conv_w[l]]

    def inproj_weights(l, g_win_t):
        return _inproj_weights(l, norm_g, g_win_t.reshape(IN_COLS, D_MODEL))

    def mixer_weights(l, gathered):
        g_wqb, g_wkvb, g_wout, g_conv = gathered
        return _mixer_weights(
            l, mla_q_a_norm, g_wqb, mla_kv_a_norm, g_wkvb, mla_q_norm, mla_k_norm,
            jnp.transpose(g_conv, (1, 0, 2)).reshape(3, GROUP_WIDTH), swa_q_norm, swa_k_norm, swa_sinks,
            g_wout.reshape(D_MIX, D_MODEL))

    slot_of = dict(
        w_in=lambda g: jnp.transpose(g["w_in"].reshape(D_MODEL, N_DEV, IN_COLS // N_DEV), (1, 0, 2)),
        w_out=lambda g: g["w_out"].reshape(N_DEV, D_MIX // N_DEV, D_MODEL),
        w_qb=lambda g: g["w_qb"], w_kvb=lambda g: g["w_kvb"])

    def own_slot(landed, mine):
        return [lax.dynamic_update_index_in_dim(a, m, my, 0) for a, m in zip(landed, mine)]

    def landed(handle, after, name, mine):
        return own_slot(_push_wait(handle, after, name), mine)

    gather_in0 = _push_start(shards(0)[:1], "weight_gather_in0_start", gather=True)
    rope = _rope_tables(T, gather_in0["token"])
    big_shapes = dict(w_in=(DEPTH * D_MODEL, IN_COLS // N_DEV), w_out=(DEPTH * D_MIX // N_DEV, D_MODEL),
                      mla_w_qb=(DEPTH * MLA_Q_LORA, MLA_QK), mla_w_kvb=(DEPTH * MLA_KV_LORA, 128))
    pad_conv = lambda a: jnp.pad(a.reshape(-1), (0, 8 * LANES - 6 * 64)).reshape(8, LANES)
    cat = lambda src: jnp.concatenate([_pack_small(lambda name, l: src[name][l]), pad_conv(src["conv_w"])], axis=0)
    adam_in = {name: [src[name].reshape(shape) for src in (weights, mom_m, mom_v)]
               for name, shape in big_shapes.items()}
    adam_in["small"] = [cat(weights), cat(mom_m), cat(mom_v)]
    rope0, adam_in = lax.optimization_barrier((rope[0], adam_in))
    w_in0_t = landed(gather_in0, rope0, "weight_gather_in0_wait", shards(0)[:1])[0]
    w_in0_t, conv0 = lax.optimization_barrier((w_in0_t, conv_w[0]))
    gather0 = _push_start(shards(0)[1:4] + [conv0], "weight_gather0_start", gather=True)
    lw0 = inproj_weights(0, w_in0_t)
    lw0 = dict(lw0, ng=lw0["ng"] + gather0["token"])
    layer1 = {}

    def mixer0(proj):
        got = landed(gather0, proj, "weight_gather0_wait", shards(0)[1:])
        got[0], conv1 = lax.optimization_barrier((got[0], conv_w[1]))
        layer1["gather"] = _push_start(shards(1)[:4] + [conv1], "weight_gather1_start", gather=True)
        mw = mixer_weights(0, got)
        return dict(mw, qa=mw["qa"] + layer1["gather"]["token"])

    x1, sv0 = _layer_fwd(x[0], lw0, rope, late_weights=mixer0)
    g1_all = landed(layer1["gather"], x1, "weight_gather1_wait", shards(1))
    (g2, loss_tile), sv1 = _layer_fwd(x1, dict(inproj_weights(1, g1_all[0]), **mixer_weights(1, g1_all[1:])), rope,
                                      target=loss_target[0])

    parts = {(1, "w_in"): ("w_in", "w_out", "w_qb", "w_kvb"), (0, "mixer"): ("w_out", "w_qb", "w_kvb"),
             (0, "w_in"): ("w_in",)}
    started = []

    def start_exchange(l, part, g):
        if (l, part) not in parts:
            return 0.0
        sl = [slot_of[n](g) for n in parts[(l, part)]]
        handle = _push_start(sl, "grad_exchange%d_%s_start" % (l, part), gather=False)
        started.append((l, part, sl, handle))
        return handle["token"]

    g1, grads1 = _layer_bwd(g2, sv1, sv1["lw"], rope, on_big_grads=functools.partial(start_exchange, 1))
    lw0b = dict(sv0["lw"], conv=sv0["lw"]["conv"] + started[0][3]["token"])
    grad_x, grads0 = _layer_bwd(g1, sv0, lw0b, rope, on_big_grads=functools.partial(start_exchange, 0))
    recv = {}

    def receive(l, part, sl, handle, after):
        got = landed(handle, after, "grad_exchange%d_%s_wait" % (l, part), [s[my] for s in sl])
        recv.update({(l, n): a for n, a in zip(parts[(l, part)], got)})

    for entry in started[:-1]:
        receive(*entry, after=grad_x)
    grads = [grads0, grads1]
    stacked = lambda n: jnp.stack([recv[(0, n)], recv[(1, n)]], axis=1)

    small = jnp.concatenate([
        _pack_small(lambda name, l: grads[l][SMALL_GRAD_KEY[name]]),
        jnp.stack([g["conv"] for g in grads]).reshape(CONV_ROWS, LANES),
        loss_tile], axis=0)
    small = _small_all_reduce(small)
    loss = small[SMALL_ROWS + CONV_ROWS, 0]
    my = _lin(_my_coords())
    conv_g = lax.dynamic_slice_in_dim(small[SMALL_ROWS:SMALL_ROWS + CONV_ROWS].reshape(DEPTH, 3, GROUP_WIDTH),
                                      my * 64, 64, axis=2)

    out = {}

    def big(name, recv, tr):
        res = _adamw(recv.reshape((N_DEV,) + big_shapes[name]), *adam_in[name], "adamw_" + name, tr)
        out[name] = [r.reshape(weights[name].shape) for r in res]

    big("w_out", stacked("w_out"), 192)
    big("mla_w_qb", stacked("w_qb"), 512)
    big("mla_w_kvb", stacked("w_kvb"), 256)
    receive(*started[-1], after=out["w_out"][1])
    big("w_in", stacked("w_in"), 256)

    g_small = jnp.concatenate([small[:SMALL_ROWS], pad_conv(conv_g)], axis=0)
    res = _adamw(g_small[None], *adam_in["small"], "adamw_small", SMALL_ROWS + 8)
    smalls = [_unpack_small(r[:SMALL_ROWS]) for r in res]
    for name, _ in SMALL:
        out[name] = [s[name] for s in smalls]
    out["conv_w"] = [r[SMALL_ROWS:].reshape(-1)[:6 * 64].reshape(DEPTH, 3, 64) for r in res]

    order = ["norm_g", "w_in", "mla_q_a_norm", "mla_w_qb", "mla_kv_a_norm", "mla_w_kvb", "mla_q_norm", "mla_k_norm",
             "conv_w", "swa_q_norm", "swa_k_norm", "swa_sinks", "w_out"]
    result = [loss, grad_x[None]]
    for idx in range(4):
        result += [out[name][idx] for name in order]
    return tuple(result)
```

```python
import functools

import jax
import jax.numpy as jnp
import numpy as np
from jax import lax
from jax.experimental import pallas as pl
from jax.experimental.pallas import tpu as pltpu

F32 = jnp.float32
MXU_DTYPE = jnp.bfloat16
WIRE_DTYPE = jnp.bfloat16

N_DEV = 8
DEPTH = 2
D_MODEL = 1024
GROUP_WIDTH = 512
D_MIX = 3 * GROUP_WIDTH
BLOCK = 128
RMS_EPS = 1e-6
NEG_INF = -1e30
HEADS = 8
MLA_QK = 96
MLA_NOPE = 64
MLA_ROPE = 32
MLA_Q_LORA = 256
MLA_KV_LORA = 128
ROPE_THETA = 10000.0
SWA_HEAD_DIM = 64
LANES = 128
IN_COLS = 4256

ADAM_LR = 0.001
ADAM_B1 = 0.9
ADAM_B2 = 0.999
ADAM_EPS = 1e-08
ADAM_WD = 0.01
ADAM_STEP = 10

NP = 4352
CB_GMLA, CB_CH, CB_CB, CB_CC, CB_GCONV, CB_GSWA, CB_SQ = 0, 1, 2, 3, 4, 5, 7
CB_QLAT = 12
CB_KVLAT, CB_KROPE = 26, 27
CB_SK, CB_SV = 32, 33
DPB_MIX, DPB_MLA, DPB_SQ, DPB_SKV = 0, 6, 7, 16

TM_PROJ = 512
TM_ROW = 256
TK = 256
TQ = 2 * TK
MLA_SCALE = MLA_QK ** -0.5
MLA_ONES_ROW = (64, 0)
LOG2E = 1.4426950408889634
LN2 = 0.6931471805599453
TM_SWA = 512
VMEM_MB = 2 ** 20


def _cp(sem, vmem_mb):
    return pltpu.CompilerParams(dimension_semantics=sem, vmem_limit_bytes=vmem_mb * VMEM_MB)


def _sds(shape, dtype):
    return jax.ShapeDtypeStruct(shape, dtype)


def _dot(a, b):
    return jnp.dot(a, b, preferred_element_type=F32)


def _dot_nt(a, b):
    return lax.dot_general(a, b, (((1,), (1,)), ((), ())), preferred_element_type=F32)


def _dot_tn(a, b):
    return lax.dot_general(a, b, (((0,), (0,)), ((), ())), preferred_element_type=F32)


def _rms(x, n):
    r = lax.rsqrt(jnp.sum(x * x, axis=-1, keepdims=True) * (1.0 / n) + RMS_EPS)
    return x * r, r


def _rms_bwd(dy, xhat, r, w, n):
    g = dy * w
    return r * (g - xhat * (jnp.sum(g * xhat, axis=-1, keepdims=True) * (1.0 / n)))


def _rms_halves(x, half1):
    x2 = x * x
    s0 = jnp.sum(jnp.where(half1, 0.0, x2), axis=-1, keepdims=True)
    s1 = jnp.sum(jnp.where(half1, x2, 0.0), axis=-1, keepdims=True)
    r = jnp.where(half1, lax.rsqrt(s1 * (1.0 / 64) + RMS_EPS), lax.rsqrt(s0 * (1.0 / 64) + RMS_EPS))
    return x * r, r


def _rms_halves_bwd(dy, xhat, r, w, half1):
    g = dy * w
    t = g * xhat
    m0 = jnp.sum(jnp.where(half1, 0.0, t), axis=-1, keepdims=True) * (1.0 / 64)
    m1 = jnp.sum(jnp.where(half1, t, 0.0), axis=-1, keepdims=True) * (1.0 / 64)
    return r * (g - xhat * jnp.where(half1, m1, m0))


def _sigmoid(x):
    return 1.0 / (1.0 + jnp.exp(-x))


def _rope(x, c, s1, s2):
    ax = x.ndim - 1
    return x * c + pltpu.roll(x, 112, ax) * s1 + pltpu.roll(x, 16, ax) * s2


def _rope_bwd(dy, c, s1, s2):
    ax = dy.ndim - 1
    return dy * c + pltpu.roll(dy * s1, 16, ax) + pltpu.roll(dy * s2, 112, ax)


def _fold_rows8(x):
    return jnp.sum(x.reshape(x.shape[0] // 8, 8, x.shape[1]), axis=0)


def _row0(v, rows=8):
    row = lax.broadcasted_iota(jnp.int32, (rows, v.shape[1]), 0)
    return jnp.where(row == 0, jnp.broadcast_to(v, (rows, v.shape[1])), 0.0)


def _mm_nn(a, b, name, out_dtype=F32, residual=None, tm=TM_PROJ):
    M, K = a.shape
    N = b.shape[1]
    tm = min(tm, M)

    def body(*refs):
        if residual is None:
            a_ref, b_ref, o_ref = refs
            acc = _dot(a_ref[...].astype(MXU_DTYPE), b_ref[...])
        else:
            a_ref, b_ref, r_ref, o_ref = refs
            acc = _dot(a_ref[...].astype(MXU_DTYPE), b_ref[...]) + r_ref[...]
        o_ref[...] = acc.astype(out_dtype)

    in_specs = [pl.BlockSpec((tm, K), lambda i: (i, 0)), pl.BlockSpec((K, N), lambda i: (0, 0))]
    args = [a, b]
    if residual is not None:
        in_specs.append(pl.BlockSpec((tm, N), lambda i: (i, 0)))
        args.append(residual)
    return pl.pallas_call(
        body, name=name, grid=(M // tm,), in_specs=in_specs,
        out_specs=pl.BlockSpec((tm, N), lambda i: (i, 0)), out_shape=_sds((M, N), out_dtype),
        compiler_params=_cp(("parallel",), 48))(*args)


def _mm_tn(a, b, name, out_dtype, tn, tk=512):
    T, M = a.shape
    N = b.shape[1]
    tk = min(tk, T)
    nk = T // tk

    def body(a_ref, b_ref, o_ref, acc_ref):
        k = pl.program_id(1)

        @pl.when(k == 0)
        def _():
            acc_ref[...] = jnp.zeros_like(acc_ref)

        acc_ref[...] += _dot_tn(a_ref[...].astype(MXU_DTYPE), b_ref[...].astype(MXU_DTYPE))

        @pl.when(k == nk - 1)
        def _():
            o_ref[...] = acc_ref[...].astype(out_dtype)

    return pl.pallas_call(
        body, name=name, grid=(N // tn, nk),
        in_specs=[pl.BlockSpec((tk, M), lambda n, k: (k, 0)), pl.BlockSpec((tk, tn), lambda n, k: (k, n))],
        out_specs=pl.BlockSpec((M, tn), lambda n, k: (0, n)), out_shape=_sds((M, N), out_dtype),
        scratch_shapes=[pltpu.VMEM((M, tn), F32)],
        compiler_params=_cp(("parallel", "arbitrary"), 48))(a, b)


def _inproj_fwd(x, ng, wp):
    T, D = x.shape
    tm = min(TM_PROJ, T)

    def body(x_ref, g_ref, w_ref, proj_ref, h_ref):
        xhat, _ = _rms(x_ref[...], D)
        h = (xhat * g_ref[...]).astype(MXU_DTYPE)
        h_ref[...] = h
        proj_ref[...] = _dot(h, w_ref[...])

    return pl.pallas_call(
        body, name="inproj_fwd", grid=(T // tm,),
        in_specs=[pl.BlockSpec((tm, D), lambda i: (i, 0)), pl.BlockSpec((1, D), lambda i: (0, 0)),
                  pl.BlockSpec((D, NP), lambda i: (0, 0))],
        out_specs=[pl.BlockSpec((tm, NP), lambda i: (i, 0)), pl.BlockSpec((tm, D), lambda i: (i, 0))],
        out_shape=[_sds((T, NP), F32), _sds((T, D), MXU_DTYPE)],
        compiler_params=_cp(("parallel",), 48))(x, ng, wp)


def _mla_prep_fwd(proj, lw, rope):
    T = proj.shape[0]
    tk = min(TK, T // 2)
    nsub = 2
    tm = nsub * tk

    def body(ql_ref, kvl_ref, kr_ref, qa_ref, kva_ref, wq_ref, wk_ref, wv_ref, qn_ref, kn_ref,
             c_ref, s1_ref, s2_ref, q_out, k_out, kt_out, vt_out):
        c, s1, s2 = c_ref[...], s1_ref[...], s2_ref[...]
        qhat, _ = _rms(ql_ref[...], MLA_Q_LORA)
        qn = (qhat * qa_ref[...]).astype(MXU_DTYPE)
        khat, _ = _rms(kvl_ref[...], MLA_KV_LORA)
        kvn = (khat * kva_ref[...]).astype(MXU_DTYPE)
        kr = kr_ref[...]
        half1 = lax.broadcasted_iota(jnp.int32, (tm, LANES), 1) >= 64
        ones_row = lax.broadcasted_iota(jnp.int32, (LANES, 1), 0)
        q3, _ = _rms(jnp.stack([_dot(qn, wq_ref[h]) for h in range(HEADS)]), MLA_QK)
        q_out[...] = (_rope(q3 * qn_ref[...], c, s1, s2) * (MLA_SCALE * LOG2E)).astype(MXU_DTYPE)
        k3, _ = _rms(jnp.stack([_dot(kvn, wk_ref[h]) for h in range(HEADS)]) + kr, MLA_QK)
        k3 = _rope(k3 * kn_ref[...], c, s1, s2)
        k_out[...] = k3.astype(MXU_DTYPE)
        for h in range(HEADS):
            for t in range(nsub):
                kt_out[h, t] = k3[h, tk * t:tk * (t + 1)].T.astype(MXU_DTYPE)
        v = _dot(kvn, wv_ref[...])
        for h in range(HEADS):
            vp = v[:, LANES * (h // 2):LANES * (h // 2 + 1)]
            own = half1 if h % 2 else jnp.logical_not(half1)
            vp = jnp.where(own, vp, 0.0)
            for t in range(nsub):
                vpt = vp[tk * t:tk * (t + 1)].T
                vt_out[h, t] = jnp.where(ones_row == MLA_ONES_ROW[h % 2], 1.0, vpt).astype(MXU_DTYPE)

    full = lambda shape: pl.BlockSpec(shape, lambda i: (0,) * len(shape))
    hd = pl.BlockSpec((HEADS, tm, LANES), lambda i: (0, i, 0))
    hdt = pl.BlockSpec((HEADS, nsub, LANES, tk), lambda i: (0, i, 0, 0))
    nat = _sds((HEADS, T, LANES), MXU_DTYPE)
    tr = _sds((HEADS, T // tk, LANES, tk), MXU_DTYPE)
    return pl.pallas_call(
        body, name="mla_prep_fwd", grid=(T // tm,),
        in_specs=[pl.BlockSpec((tm, 256), lambda i: (i, CB_QLAT)), pl.BlockSpec((tm, LANES), lambda i: (i, CB_KVLAT)),
                  pl.BlockSpec((tm, LANES), lambda i: (i, CB_KROPE)),
                  full((1, 256)), full((1, LANES)), full((HEADS, 256, LANES)), full((HEADS, LANES, LANES)),
                  full((LANES, 512)), full((1, LANES)), full((1, LANES)),
                  pl.BlockSpec((tm, LANES), lambda i: (i, 0)), pl.BlockSpec((tm, LANES), lambda i: (i, 0)),
                  pl.BlockSpec((tm, LANES), lambda i: (i, 0))],
        out_specs=[hd, hd, hdt, hdt],
        out_shape=[nat, nat, tr, tr],
        compiler_params=_cp(("parallel",), 32))(
            proj, proj, proj, lw["qa"], lw["kva"], lw["wq"], lw["wk"], lw["wv"], lw["qn"], lw["kn"],
            rope[0], rope[1], rope[2])


def _mla_attn_fwd(q, k, vt):
    T = q.shape[1]
    tk = min(TK, T // 2)
    tq = 2 * tk

    def body(q_ref, k_ref, vt_ref, o_ref, lse_ref, acc_s, m_s, s_a, s_b):
        i = pl.program_id(1)
        key = lax.broadcasted_iota(jnp.int32, (tk, tq), 0)
        qry = lax.broadcasted_iota(jnp.int32, (tk, tq), 1)
        qs = [q_ref[0], q_ref[1]]
        acc_s[...] = jnp.zeros_like(acc_s)
        m_s[...] = jnp.full(m_s.shape, NEG_INF, F32)

        def scores(kj, buf):
            rows = pl.ds(pl.multiple_of(kj * tk, tk), tk)
            for r in range(2):
                buf[r] = _dot_nt(k_ref[r, rows, :], qs[r])

        def consume(kj, buf, diag):
            for r in range(2):
                s = buf[r]
                if diag is not None:
                    s = jnp.where(key + diag * tk <= qry, s, NEG_INF)
                m_old = m_s[r]
                m_new = jnp.maximum(m_old, jnp.max(s, axis=0, keepdims=True))
                alpha = jnp.exp2(m_old - m_new)
                p = jnp.exp2(s - m_new)
                m_s[r] = m_new
                acc_s[r] = alpha * acc_s[r] + _dot(vt_ref[r, kj], p.astype(MXU_DTYPE))

        scores(0, s_a)

        def pair(kj):
            scores(kj + 1, s_b)
            consume(kj, s_a, None)
            scores(kj + 2, s_a)
            consume(kj + 1, s_b, None)

        def octet(ko, carry):
            for t in range(4):
                pair(8 * ko + 2 * t)
            return carry

        lax.fori_loop(0, i // 4, octet, 0)

        @pl.when(i % 4 >= 2)
        def _():
            pair(8 * (i // 4))
            pair(8 * (i // 4) + 2)

        @pl.when(i % 2 == 1)
        def _():
            pair(2 * i - 2)

        scores(2 * i + 1, s_b)
        consume(2 * i, s_a, 0)
        consume(2 * i + 1, s_b, 1)
        l = [acc_s[r, pl.ds(MLA_ONES_ROW[r], 1), :] for r in range(2)]
        head0 = lax.broadcasted_iota(jnp.int32, (LANES, 1), 0) < 64
        o_ref[...] = jnp.where(head0, acc_s[0] / l[0], acc_s[1] / l[1]).T
        for r in range(2):
            lse_ref[r] = m_s[r] + jnp.log2(l[r])

    return pl.pallas_call(
        body, name="mla_attn_fwd", grid=(HEADS // 2, T // tq),
        in_specs=[pl.BlockSpec((2, tq, LANES), lambda j, i: (j, i, 0)),
                  pl.BlockSpec((2, T, LANES), lambda j, i: (j, 0, 0)),
                  pl.BlockSpec((2, T // tk, LANES, tk), lambda j, i: (j, 0, 0, 0))],
        out_specs=[pl.BlockSpec((tq, LANES), lambda j, i: (i, j)),
                   pl.BlockSpec((2, 1, tq), lambda j, i: (j, 0, i))],
        out_shape=[_sds((T, GROUP_WIDTH), F32), _sds((HEADS, 1, T), F32)],
        scratch_shapes=[pltpu.VMEM((2, LANES, tq), F32), pltpu.VMEM((2, 1, tq), F32),
                        pltpu.VMEM((2, tk, tq), F32), pltpu.VMEM((2, tk, tq), F32)],
        compiler_params=_cp(("parallel", "arbitrary"), 40))(q, k, vt)


def _swa_kv_variants(x, half1):
    xs = pltpu.roll(x, 64, 1)
    out = {}
    for g in range(2):
        for r in range(2):
            own = half1 if r else jnp.logical_not(half1)
            out[(g, r)] = jnp.where(own, x if g == r else xs, 0.0).astype(MXU_DTYPE)
    return out


def _swa_alibi():
    ki = np.arange(2 * BLOCK)[:, None]
    qi = np.arange(BLOCK)[None, :]
    dist = BLOCK + qi - ki
    slopes = 2.0 ** -(np.arange(HEADS) + 1.0)
    tab = np.where(((dist >= 0) & (dist < BLOCK))[None], slopes[:, None, None] * dist[None], 1e30)
    return jnp.asarray(tab, F32)


def _swa_kv_variants_t(xt, rows1):
    xs = pltpu.roll(xt, 64, 0)
    out = {}
    for g in range(2):
        for r in range(2):
            own = rows1 if r else jnp.logical_not(rows1)
            out[(g, r)] = jnp.where(own, xt if g == r else xs, 0.0).astype(MXU_DTYPE)
    return out


def _swa_probs(i, nb, q_ref, k_ref, v_ref, pk_ref, pv_ref, qw_ref, kw_ref, alibi_ref, sink_ref):
    scale = SWA_HEAD_DIM ** -0.5
    half1 = lax.broadcasted_iota(jnp.int32, (1, LANES), 1) >= 64
    k_all = jnp.concatenate([pk_ref[...], k_ref[...]], axis=0)
    v_all = jnp.concatenate([pv_ref[...], v_ref[...]], axis=0)
    khat, _ = _rms_halves(k_all, half1)
    kn = khat * kw_ref[...]
    kp = _swa_kv_variants(kn, half1)
    qhat, qr, qn, qt = [], [], [], []
    for j in range(4):
        xh, r = _rms_halves(q_ref[:, LANES * j:LANES * (j + 1)], half1)
        qf = xh * qw_ref[...]
        qhat.append(xh)
        qr.append(r)
        qn.append(qf.astype(MXU_DTYPE))
        qt.append(qf.T.astype(MXU_DTYPE))
    key = lax.broadcasted_iota(jnp.int32, (2 * BLOCK, BLOCK), 0)
    first = jnp.where((i == 0) & (key < BLOCK), NEG_INF, 0.0)
    s = jnp.stack([_dot(kp[(h // 4, h % 2)][BLOCK * b:BLOCK * (b + 2)], qt[h // 2][:, BLOCK * b:BLOCK * (b + 1)])
                   for b in range(nb) for h in range(HEADS)]) * scale - alibi_ref[...]
    s = jnp.concatenate([s[:HEADS] + first, s[HEADS:]], axis=0) if nb > 1 else s + first
    sink = jnp.stack([jnp.full((1, 1), sink_ref[h], F32) for _ in range(nb) for h in range(HEADS)])
    m = jnp.maximum(jnp.max(s, axis=1, keepdims=True), sink)
    e = jnp.exp(s - m)
    es = jnp.exp(sink - m)
    inv = 1.0 / (jnp.sum(e, axis=1, keepdims=True) + es)
    return e * inv, es * inv, dict(half1=half1, kn=kn, kp=kp, v_all=v_all, qhat=qhat, qr=qr, qn=qn)


def _swa_fwd(proj, lw):
    T = proj.shape[0]
    tm = min(TM_SWA, T)
    nb = tm // BLOCK

    def body(q_ref, k_ref, v_ref, pk_ref, pv_ref, qw_ref, kw_ref, alibi_ref, sink_ref, o_ref):
        p, _, c = _swa_probs(pl.program_id(0), nb, q_ref, k_ref, v_ref, pk_ref, pv_ref, qw_ref, kw_ref, alibi_ref,
                             sink_ref)
        p = p.astype(MXU_DTYPE)
        rows1 = lax.broadcasted_iota(jnp.int32, (LANES, 1), 0) >= 64
        vpt = _swa_kv_variants_t(c["v_all"].T, rows1)
        for j in range(4):
            g = j // 2
            o_t = [_dot(vpt[(g, 0)][:, BLOCK * b:BLOCK * (b + 2)], p[HEADS * b + 2 * j])
                   + _dot(vpt[(g, 1)][:, BLOCK * b:BLOCK * (b + 2)], p[HEADS * b + 2 * j + 1]) for b in range(nb)]
            o_t = jnp.concatenate(o_t, axis=1) if nb > 1 else o_t[0]
            o_ref[:, LANES * j:LANES * (j + 1)] = o_t.T

    prev = lambda cb: pl.BlockSpec((BLOCK, LANES), lambda i: (jnp.maximum(i * nb - 1, 0), cb))
    return pl.pallas_call(
        body, name="swa_fwd", grid=(T // tm,),
        in_specs=[pl.BlockSpec((tm, 512), lambda i: (i, CB_SQ)), pl.BlockSpec((tm, LANES), lambda i: (i, CB_SK)),
                  pl.BlockSpec((tm, LANES), lambda i: (i, CB_SV)), prev(CB_SK), prev(CB_SV),
                  pl.BlockSpec((1, LANES), lambda i: (0, 0)), pl.BlockSpec((1, LANES), lambda i: (0, 0)),
                  pl.BlockSpec((nb * HEADS, 2 * BLOCK, BLOCK), lambda i: (0, 0, 0)),
                  pl.BlockSpec(memory_space=pltpu.SMEM)],
        out_specs=pl.BlockSpec((tm, 512), lambda i: (i, 0)),
        out_shape=_sds((T, GROUP_WIDTH), F32),
        compiler_params=_cp(("parallel",), 40))(
            proj, proj, proj, proj, proj, lw["sqn"], lw["skn"], jnp.tile(_swa_alibi(), (nb, 1, 1)), lw["sinks"])


def _shift_down(u, prev, n, row):
    tm = u.shape[0]
    out = pltpu.roll(u, n, 0)
    row8 = lax.broadcasted_iota(jnp.int32, prev.shape, 0)
    for t in range(n):
        src = jnp.sum(jnp.where(row8 == 8 - n + t, prev, 0.0), axis=0, keepdims=True)
        out = jnp.where(row == t, src, out)
    return out


def _shift_up(u, nxt, n, row):
    tm = u.shape[0]
    out = pltpu.roll(u, tm - n, 0)
    row8 = lax.broadcasted_iota(jnp.int32, nxt.shape, 0)
    for t in range(n):
        src = jnp.sum(jnp.where(row8 == t, nxt, 0.0), axis=0, keepdims=True)
        out = jnp.where(row == tm - n + t, src, out)
    return out


def _mix_fwd(proj, o_mla, o_swa, conv_w):
    T = proj.shape[0]
    tm = min(TM_ROW, T)

    def body(gm_ref, ch_ref, cb_ref, cc_ref, gc_ref, gs_ref, pch_ref, pcc_ref, om_ref, os_ref, w_ref, y_ref):
        i = pl.program_id(0)
        row = lax.broadcasted_iota(jnp.int32, (tm, GROUP_WIDTH), 0)
        u = cc_ref[...] * ch_ref[...]
        u_prev = jnp.where(i > 0, pcc_ref[...] * pch_ref[...], 0.0)
        z = (w_ref[0:1, :] * _shift_down(u, u_prev, 2, row) + w_ref[1:2, :] * _shift_down(u, u_prev, 1, row)
             + w_ref[2:3, :] * u)
        gm, gc, gs = gm_ref[...], gc_ref[...], gs_ref[...]
        y_ref[:, 0:512] = (om_ref[...] * (gm * _sigmoid(gm))).astype(MXU_DTYPE)
        y_ref[:, 512:1024] = (cb_ref[...] * z * (gc * _sigmoid(gc))).astype(MXU_DTYPE)
        y_ref[:, 1024:1536] = (os_ref[...] * (gs * _sigmoid(gs))).astype(MXU_DTYPE)

    blk = lambda cb: pl.BlockSpec((tm, 512), lambda i: (i, cb))
    prev = lambda cb: pl.BlockSpec((8, 512), lambda i: (jnp.maximum(i * (tm // 8) - 1, 0), cb))
    tile = pl.BlockSpec((tm, 512), lambda i: (i, 0))
    return pl.pallas_call(
        body, name="mix_fwd", grid=(T // tm,),
        in_specs=[blk(CB_GMLA), blk(CB_CH), blk(CB_CB), blk(CB_CC), blk(CB_GCONV), blk(CB_GSWA),
                  prev(CB_CH), prev(CB_CC), tile, tile, pl.BlockSpec((8, 512), lambda i: (0, 0))],
        out_specs=pl.BlockSpec((tm, D_MIX), lambda i: (i, 0)),
        out_shape=_sds((T, D_MIX), MXU_DTYPE),
        compiler_params=_cp(("parallel",), 32))(
            proj, proj, proj, proj, proj, proj, proj, proj, o_mla, o_swa, conv_w)


def _outproj_loss(ycat, wo, x, target):
    T, D = x.shape
    K = ycat.shape[1]
    tm = min(TM_PROJ, T)
    nt = T // tm

    def body(y_ref, w_ref, x_ref, t_ref, g_ref, loss_ref, acc_ref):
        i = pl.program_id(0)

        @pl.when(i == 0)
        def _():
            acc_ref[...] = jnp.zeros_like(acc_ref)

        err = _dot(y_ref[...], w_ref[...]) + x_ref[...] - t_ref[...]
        g_ref[...] = err * (1.0 / D)
        acc_ref[...] += _fold_rows8(err * err)

        @pl.when(i == nt - 1)
        def _():
            tot = jnp.sum(jnp.sum(acc_ref[...], axis=1, keepdims=True), axis=0, keepdims=True)
            loss_ref[...] = jnp.broadcast_to(tot * (0.5 / D), (8, LANES))

    tile = pl.BlockSpec((tm, D), lambda i: (i, 0))
    return pl.pallas_call(
        body, name="outproj_loss", grid=(nt,),
        in_specs=[pl.BlockSpec((tm, K), lambda i: (i, 0)), pl.BlockSpec((K, D), lambda i: (0, 0)), tile, tile],
        out_specs=[tile, pl.BlockSpec((8, LANES), lambda i: (0, 0))],
        out_shape=[_sds((T, D), F32), _sds((8, LANES), F32)],
        scratch_shapes=[pltpu.VMEM((8, D), F32)],
        compiler_params=_cp(("arbitrary",), 48))(ycat, wo, x, target)


def _outproj_bwd(g, ycat, wot):
    T, D = g.shape
    K = ycat.shape[1]
    tm = min(512, T)
    nt = T // tm

    def body(g_ref, y_ref, wt_ref, dy_ref, dw_ref, acc_ref):
        i = pl.program_id(0)

        @pl.when(i == 0)
        def _():
            acc_ref[...] = jnp.zeros_like(acc_ref)

        gb = g_ref[...].astype(MXU_DTYPE)
        dy_ref[...] = _dot(gb, wt_ref[...])
        acc_ref[...] += _dot_tn(y_ref[...], gb)

        @pl.when(i == nt - 1)
        def _():
            dw_ref[...] = acc_ref[...].astype(WIRE_DTYPE)

    return pl.pallas_call(
        body, name="outproj_bwd", grid=(nt,),
        in_specs=[pl.BlockSpec((tm, D), lambda i: (i, 0)), pl.BlockSpec((tm, K), lambda i: (i, 0)),
                  pl.BlockSpec((D, K), lambda i: (0, 0))],
        out_specs=[pl.BlockSpec((tm, K), lambda i: (i, 0)), pl.BlockSpec((K, D), lambda i: (0, 0))],
        out_shape=[_sds((T, K), F32), _sds((K, D), WIRE_DTYPE)],
        scratch_shapes=[pltpu.VMEM((K, D), F32)],
        compiler_params=_cp(("arbitrary",), 48))(g, ycat, wot)


def _mix_bwd(dycat, proj, o_mla, o_swa, conv_w):
    T = proj.shape[0]
    tm = min(TM_ROW, T)
    nt = T // tm

    def body(dym_ref, dyc_ref, dys_ref, gm_ref, ch_ref, cb_ref, cc_ref, gc_ref, gs_ref, pch_ref, pcc_ref,
             ndy_ref, ncb_ref, ngc_ref, om_ref, os_ref, w_ref,
             d1_ref, dom_ref, dos_ref, dw_ref):
        i = pl.program_id(0)

        @pl.when(i == 0)
        def _():
            dw_ref[...] = jnp.zeros_like(dw_ref)

        row = lax.broadcasted_iota(jnp.int32, (tm, GROUP_WIDTH), 0)

        def gate(g):
            sg = _sigmoid(g)
            return g * sg, sg * (1.0 + g * (1.0 - sg))

        gm = gm_ref[...]
        silu, dsilu = gate(gm)
        dym = dym_ref[...]
        dom_ref[...] = dym * silu
        d1_ref[:, 0:512] = (dym * om_ref[...] * dsilu).astype(MXU_DTYPE)

        gs = gs_ref[...]
        silu, dsilu = gate(gs)
        dys = dys_ref[...]
        dos_ref[...] = dys * silu
        d1_ref[:, 2560:3072] = (dys * os_ref[...] * dsilu).astype(MXU_DTYPE)

        ch, cb, cc, gc, dyc = ch_ref[...], cb_ref[...], cc_ref[...], gc_ref[...], dyc_ref[...]
        w0, w1, w2 = w_ref[0:1, :], w_ref[1:2, :], w_ref[2:3, :]
        u = cc * ch
        u_prev = jnp.where(i > 0, pcc_ref[...] * pch_ref[...], 0.0)
        u1 = _shift_down(u, u_prev, 1, row)
        u2 = _shift_down(u, u_prev, 2, row)
        z = w0 * u2 + w1 * u1 + w2 * u
        silu, dsilu = gate(gc)
        dz = dyc * cb * silu
        ngc = ngc_ref[...]
        dz_next = jnp.where(i < nt - 1, ndy_ref[...] * ncb_ref[...] * (ngc * _sigmoid(ngc)), 0.0)
        du = w2 * dz + w1 * _shift_up(dz, dz_next, 1, row) + w0 * _shift_up(dz, dz_next, 2, row)
        d1_ref[:, 512:1024] = (du * cc).astype(MXU_DTYPE)
        d1_ref[:, 1024:1536] = (dyc * z * silu).astype(MXU_DTYPE)
        d1_ref[:, 1536:2048] = (du * ch).astype(MXU_DTYPE)
        d1_ref[:, 2048:2560] = (dyc * cb * z * dsilu).astype(MXU_DTYPE)
        row8 = lax.broadcasted_iota(jnp.int32, (8, GROUP_WIDTH), 0)
        dw = jnp.zeros((8, GROUP_WIDTH), F32)
        for t, shifted in enumerate((u2, u1, u)):
            dw = dw + jnp.where(row8 == t, jnp.sum(dz * shifted, axis=0, keepdims=True), 0.0)
        dw_ref[...] += dw

    blk = lambda cb: pl.BlockSpec((tm, 512), lambda i: (i, cb))
    prev = lambda cb: pl.BlockSpec((8, 512), lambda i: (jnp.maximum(i * (tm // 8) - 1, 0), cb))
    nxt = lambda cb: pl.BlockSpec((8, 512), lambda i: (jnp.minimum((i + 1) * (tm // 8), T // 8 - 1), cb))
    tile = pl.BlockSpec((tm, 512), lambda i: (i, 0))
    return pl.pallas_call(
        body, name="mix_bwd", grid=(nt,),
        in_specs=[blk(0), blk(1), blk(2), blk(CB_GMLA), blk(CB_CH), blk(CB_CB), blk(CB_CC), blk(CB_GCONV),
                  blk(CB_GSWA), prev(CB_CH), prev(CB_CC), nxt(1), nxt(CB_CB), nxt(CB_GCONV), tile, tile,
                  pl.BlockSpec((8, 512), lambda i: (0, 0))],
        out_specs=[pl.BlockSpec((tm, 3072), lambda i: (i, DPB_MIX)), tile, tile,
                   pl.BlockSpec((8, 512), lambda i: (0, 0))],
        out_shape=[_sds((T, NP), MXU_DTYPE), _sds((T, 512), F32), _sds((T, 512), F32), _sds((8, 512), F32)],
        compiler_params=_cp(("arbitrary",), 48))(
            dycat, dycat, dycat, proj, proj, proj, proj, proj, proj, proj, proj, dycat, proj, proj,
            o_mla, o_swa, conv_w)


def _swa_bwd(proj, o_swa, do_swa, lw, dproj):
    T = proj.shape[0]
    tm = min(TM_SWA, T)
    nb = tm // BLOCK
    scale = SWA_HEAD_DIM ** -0.5

    def body(q_ref, k_ref, v_ref, pk_ref, pv_ref, o_ref, do_ref, qw_ref, kw_ref, alibi_ref, sink_ref, dproj_in,
             dq_ref, dk_ref, dv_ref, dqw_ref, dsink_ref):
        i = pl.program_id(0)

        @pl.when(i == 0)
        def _():
            dk_ref[...] = jnp.zeros_like(dk_ref)
            dv_ref[...] = jnp.zeros_like(dv_ref)
            dqw_ref[...] = jnp.zeros_like(dqw_ref)
            dsink_ref[...] = jnp.zeros_like(dsink_ref)

        p, p_sink, c = _swa_probs(i, nb, q_ref, k_ref, v_ref, pk_ref, pv_ref, qw_ref, kw_ref, alibi_ref, sink_ref)
        half1, kp, qn, qhat, qr = c["half1"], c["kp"], c["qn"], c["qhat"], c["qr"]
        rows1 = lax.broadcasted_iota(jnp.int32, (LANES, 1), 0) >= 64
        kpt = _swa_kv_variants_t(c["kn"].T, rows1)
        vp = _swa_kv_variants(c["v_all"], half1)
        qw = qw_ref[...]
        rows = [slice(BLOCK * b, BLOCK * (b + 1)) for b in range(nb)]
        keys = [slice(BLOCK * b, BLOCK * (b + 2)) for b in range(nb)]
        dob, dot_b, dd0, dd1 = [], [], [], []
        for j in range(4):
            cols = slice(LANES * j, LANES * (j + 1))
            do = do_ref[:, cols]
            do_t = do.T
            prod_t = do_t * o_ref[:, cols].T
            dob.append(do.astype(MXU_DTYPE))
            dot_b.append(do_t.astype(MXU_DTYPE))
            dd0.append(jnp.sum(jnp.where(rows1, 0.0, prod_t), axis=0, keepdims=True))
            dd1.append(jnp.sum(jnp.where(rows1, prod_t, 0.0), axis=0, keepdims=True))
        dd = jnp.stack([(dd1 if h % 2 else dd0)[h // 2][:, rows[b]] for b in range(nb) for h in range(HEADS)])
        dp = jnp.stack([_dot(vp[(h // 4, h % 2)][keys[b]], dot_b[h // 2][:, rows[b]])
                        for b in range(nb) for h in range(HEADS)])
        ds = (p * (dp - dd) * scale).astype(MXU_DTYPE)
        dsink = -jnp.sum(p_sink * dd, axis=2, keepdims=True)
        pb = p.astype(MXU_DTYPE)

        dqw = jnp.zeros((1, LANES), F32)
        for j in range(4):
            g = j // 2
            dqn_t = [_dot(kpt[(g, 0)][:, keys[b]], ds[HEADS * b + 2 * j])
                     + _dot(kpt[(g, 1)][:, keys[b]], ds[HEADS * b + 2 * j + 1]) for b in range(nb)]
            dqn = (jnp.concatenate(dqn_t, axis=1) if nb > 1 else dqn_t[0]).T
            dqw = dqw + jnp.sum(dqn * qhat[j], axis=0, keepdims=True)
            dq_ref[:, LANES * j:LANES * (j + 1)] = _rms_halves_bwd(dqn, qhat[j], qr[j], qw, half1).astype(MXU_DTYPE)
        dqw_ref[...] += _row0(dqw + pltpu.roll(dqw, 64, 1))

        dk_tot = jnp.zeros((tm + BLOCK, LANES), F32)
        dv_tot = jnp.zeros((tm + BLOCK, LANES), F32)
        for b in range(nb):
            dk_b = jnp.zeros((2 * BLOCK, LANES), F32)
            dv_b = jnp.zeros((2 * BLOCK, LANES), F32)
            for g in range(2):
                for r in range(2):
                    own = half1 if r else jnp.logical_not(half1)
                    ha, hb = HEADS * b + 4 * g + r, HEADS * b + 4 * g + 2 + r
                    qa, qb = qn[2 * g][rows[b]], qn[2 * g + 1][rows[b]]
                    da, db = dob[2 * g][rows[b]], dob[2 * g + 1][rows[b]]
                    dkp = jnp.where(own, _dot(ds[ha], qa) + _dot(ds[hb], qb), 0.0)
                    dvp = jnp.where(own, _dot(pb[ha], da) + _dot(pb[hb], db), 0.0)
                    if g != r:
                        dkp = pltpu.roll(dkp, 64, 1)
                        dvp = pltpu.roll(dvp, 64, 1)
                    dk_b = dk_b + dkp
                    dv_b = dv_b + dvp
            pad = lambda x: jnp.concatenate(
                [z for z in (jnp.zeros((BLOCK * b, LANES), F32), x, jnp.zeros((BLOCK * (nb - 1 - b), LANES), F32))
                 if z.shape[0]], axis=0)
            dk_tot = dk_tot + pad(dk_b)
            dv_tot = dv_tot + pad(dv_b)
        dst = pl.ds(pl.multiple_of(i * tm, BLOCK), tm + BLOCK)
        dk_ref[dst, :] += dk_tot
        dv_ref[dst, :] += dv_tot

        row8 = lax.broadcasted_iota(jnp.int32, (8, LANES), 0)
        dsink_tile = jnp.zeros((8, LANES), F32)
        for b in range(nb):
            for h in range(HEADS):
                dsink_tile = dsink_tile + jnp.where(row8 == h, jnp.broadcast_to(dsink[HEADS * b + h], (8, LANES)), 0.0)
        dsink_ref[...] += dsink_tile

    prev = lambda cb: pl.BlockSpec((BLOCK, LANES), lambda i: (jnp.maximum(i * nb - 1, 0), cb))
    tile = pl.BlockSpec((tm, 512), lambda i: (i, 0))
    small = pl.BlockSpec((8, LANES), lambda i: (0, 0))
    acc = pl.BlockSpec((T + BLOCK, LANES), lambda i: (0, 0))
    return pl.pallas_call(
        body, name="swa_bwd", grid=(T // tm,),
        in_specs=[pl.BlockSpec((tm, 512), lambda i: (i, CB_SQ)), pl.BlockSpec((tm, LANES), lambda i: (i, CB_SK)),
                  pl.BlockSpec((tm, LANES), lambda i: (i, CB_SV)), prev(CB_SK), prev(CB_SV), tile, tile,
                  pl.BlockSpec((1, LANES), lambda i: (0, 0)), pl.BlockSpec((1, LANES), lambda i: (0, 0)),
                  pl.BlockSpec((nb * HEADS, 2 * BLOCK, BLOCK), lambda i: (0, 0, 0)),
                  pl.BlockSpec(memory_space=pltpu.SMEM), pl.BlockSpec(memory_space=pl.ANY)],
        out_specs=[pl.BlockSpec((tm, 512), lambda i: (i, DPB_SQ)), acc, acc, small, small],
        out_shape=[_sds((T, NP), MXU_DTYPE), _sds((T + BLOCK, LANES), F32), _sds((T + BLOCK, LANES), F32),
                   _sds((8, LANES), F32), _sds((8, LANES), F32)],
        input_output_aliases={11: 0},
        compiler_params=_cp(("arbitrary",), 48))(
            proj, proj, proj, proj, proj, o_swa, do_swa, lw["sqn"], lw["skn"], jnp.tile(_swa_alibi(), (nb, 1, 1)),
            lw["sinks"], dproj)


def _swa_kv_bwd(proj, dkn, dv, lw, dproj):
    T = proj.shape[0]
    tm = min(TM_SWA, T)
    dkn, dv = dkn[BLOCK:], dv[BLOCK:]

    def body(k_ref, dkn_ref, dv_ref, kw_ref, dproj_in, d_ref, dkw_ref):
        i = pl.program_id(0)

        @pl.when(i == 0)
        def _():
            dkw_ref[...] = jnp.zeros_like(dkw_ref)

        half1 = lax.broadcasted_iota(jnp.int32, (1, LANES), 1) >= 64
        khat, kr = _rms_halves(k_ref[...], half1)
        dkn_t = dkn_ref[...]
        dkw = jnp.sum(dkn_t * khat, axis=0, keepdims=True)
        dkw_ref[...] += _row0(dkw + pltpu.roll(dkw, 64, 1))
        d_ref[:, 0:LANES] = _rms_halves_bwd(dkn_t, khat, kr, kw_ref[...], half1).astype(MXU_DTYPE)
        d_ref[:, LANES:2 * LANES] = dv_ref[...].astype(MXU_DTYPE)

    return pl.pallas_call(
        body, name="swa_kv_bwd", grid=(T // tm,),
        in_specs=[pl.BlockSpec((tm, LANES), lambda i: (i, CB_SK)), pl.BlockSpec((tm, LANES), lambda i: (i, 0)),
                  pl.BlockSpec((tm, LANES), lambda i: (i, 0)), pl.BlockSpec((1, LANES), lambda i: (0, 0)),
                  pl.BlockSpec(memory_space=pl.ANY)],
        out_specs=[pl.BlockSpec((tm, 2 * LANES), lambda i: (i, DPB_SKV)), pl.BlockSpec((8, LANES), lambda i: (0, 0))],
        out_shape=[_sds((T, NP), MXU_DTYPE), _sds((8, LANES), F32)],
        input_output_aliases={4: 0},
        compiler_params=_cp(("arbitrary",), 32))(proj, dkn, dv, lw["skn"], dproj)


def _mla_attn_bwd(q, k, kt, vt, o, do, lse):
    T = q.shape[1]
    tk = min(TK, T // 2)
    tq = 2 * tk

    def body(q_ref, k_ref, kt_ref, vt_ref, o_ref, do_ref, lse_ref, dq_ref, dk_ref, dv_ref, dq_s, lse_s, dd_s,
             s_a, s_b, p_a, p_b):
        h = pl.program_id(0)
        i = pl.program_id(1)

        @pl.when(i == 0)
        def _():
            dk_ref[...] = jnp.zeros_like(dk_ref)
            dv_ref[...] = jnp.zeros_like(dv_ref)

        qry = lax.broadcasted_iota(jnp.int32, (tq, tk), 0)
        key = lax.broadcasted_iota(jnp.int32, (tq, tk), 1)
        own = (lax.broadcasted_iota(jnp.int32, (1, LANES), 1) // 64) == (h % 2)
        do_own = jnp.where(own, do_ref[...], 0.0)
        dob = do_own.astype(MXU_DTYPE)
        dob_t = do_own.T.astype(MXU_DTYPE)
        qh = q_ref[0]
        qh_t = qh.astype(F32).T.astype(MXU_DTYPE)
        dd_col = jnp.sum(do_own * o_ref[...], axis=-1, keepdims=True)
        lse_col = jnp.broadcast_to(lse_ref[0], (LANES, tq)).T
        for c in range(tk // LANES):
            lse_s[:, LANES * c:LANES * (c + 1)] = lse_col
            dd_s[:, LANES * c:LANES * (c + 1)] = jnp.broadcast_to(dd_col, (tq, LANES))
        dq_s[...] = jnp.zeros_like(dq_s)

        def scores(kj, s_buf, p_buf):
            s_buf[...] = _dot(qh, kt_ref[0, kj])
            p_buf[...] = _dot(dob, vt_ref[0, kj])

        def consume(kj, s_buf, p_buf, diag):
            rows = pl.ds(pl.multiple_of(kj * tk, tk), tk)
            s = s_buf[...]
            if diag is not None:
                s = jnp.where(key + diag * tk <= qry, s, NEG_INF)
            p = jnp.exp2(s - lse_s[...])
            ds = (p * (p_buf[...] - dd_s[...])).astype(MXU_DTYPE)
            dq_s[...] += _dot(ds, k_ref[0, rows, :])
            dk_ref[0, kj] += _dot(qh_t, ds)
            dv_ref[0, kj] += _dot(dob_t, p.astype(MXU_DTYPE))

        scores(0, s_a, p_a)

        def pair(kj):
            scores(kj + 1, s_b, p_b)
            consume(kj, s_a, p_a, None)
            scores(kj + 2, s_a, p_a)
            consume(kj + 1, s_b, p_b, None)

        def octet(ko, carry):
            for t in range(4):
                pair(8 * ko + 2 * t)
            return carry

        lax.fori_loop(0, i // 4, octet, 0)

        @pl.when(i % 4 >= 2)
        def _():
            pair(8 * (i // 4))
            pair(8 * (i // 4) + 2)

        @pl.when(i % 2 == 1)
        def _():
            pair(2 * i - 2)

        scores(2 * i + 1, s_b, p_b)
        consume(2 * i, s_a, p_a, 0)
        consume(2 * i + 1, s_b, p_b, 1)
        dq_ref[0] = dq_s[...]

    res = pl.BlockSpec((1, T, LANES), lambda h, i: (h, 0, 0))
    res_t = pl.BlockSpec((1, T // tk, LANES, tk), lambda h, i: (h, 0, 0, 0))
    buf = pltpu.VMEM((tq, tk), F32)
    acc_t = _sds((HEADS, T // tk, LANES, tk), F32)
    return pl.pallas_call(
        body, name="mla_attn_bwd", grid=(HEADS, T // tq),
        in_specs=[pl.BlockSpec((1, tq, LANES), lambda h, i: (h, i, 0)), res, res_t, res_t,
                  pl.BlockSpec((tq, LANES), lambda h, i: (i, h // 2)),
                  pl.BlockSpec((tq, LANES), lambda h, i: (i, h // 2)),
                  pl.BlockSpec((1, 1, tq), lambda h, i: (h, 0, i))],
        out_specs=[pl.BlockSpec((1, tq, LANES), lambda h, i: (h, i, 0)), res_t, res_t],
        out_shape=[_sds((HEADS, T, LANES), F32), acc_t, acc_t],
        scratch_shapes=[pltpu.VMEM((tq, LANES), F32), buf, buf, buf, buf, buf, buf],
        compiler_params=_cp(("parallel", "arbitrary"), 48))(q, k, kt, vt, o, do, lse)


def _mla_prep_bwd(proj, dq, dk, dv, lw, rope, dproj):
    T = proj.shape[0]
    tm = min(TK, T // 2)

    def body(ql_ref, kvl_ref, kr_ref, dq_ref, dk_ref, dv_ref, qa_ref, kva_ref, wq_ref, wk_ref, wv_ref,
             wqt_ref, wkt_ref, wvt_ref, qn_ref, kn_ref, c_ref, s1_ref, s2_ref, dproj_in,
             d_ref, dwq_ref, dwk_ref, dwv_ref, dqa_ref, dkva_ref, dqn_ref, dkn_ref):
        i = pl.program_id(0)

        @pl.when(i == 0)
        def _():
            for ref in (dwq_ref, dwk_ref, dwv_ref, dqa_ref, dkva_ref, dqn_ref, dkn_ref):
                ref[...] = jnp.zeros_like(ref)

        c, s1, s2 = c_ref[...], s1_ref[...], s2_ref[...]
        lane = lax.broadcasted_iota(jnp.int32, (1, LANES), 1)
        qlhat, qlr = _rms(ql_ref[...], MLA_Q_LORA)
        qn = (qlhat * qa_ref[...]).astype(MXU_DTYPE)
        kvhat, kvr = _rms(kvl_ref[...], MLA_KV_LORA)
        kvn = (kvhat * kva_ref[...]).astype(MXU_DTYPE)
        kr = kr_ref[...]
        x3, r3 = _rms(jnp.stack([_dot(qn, wq_ref[h]) for h in range(HEADS)]), MLA_QK)
        dy3 = _rope_bwd(dq_ref[...] * MLA_SCALE, c, s1, s2)
        dqw = jnp.sum(jnp.sum(dy3 * x3, axis=0), axis=0, keepdims=True)
        dx3 = _rms_bwd(dy3, x3, r3, qn_ref[...], MLA_QK).astype(MXU_DTYPE)
        dqnl = jnp.zeros((tm, MLA_Q_LORA), F32)
        for h in range(HEADS):
            dwq_ref[h] += _dot_tn(qn, dx3[h])
            dqnl = dqnl + _dot(dx3[h], wqt_ref[h])

        x3, r3 = _rms(jnp.stack([_dot(kvn, wk_ref[h]) for h in range(HEADS)]) + kr, MLA_QK)
        dy3 = _rope_bwd(jnp.stack([dk_ref[h, 0].T for h in range(HEADS)]) * LN2, c, s1, s2)
        dkw = jnp.sum(jnp.sum(dy3 * x3, axis=0), axis=0, keepdims=True)
        dxf3 = _rms_bwd(dy3, x3, r3, kn_ref[...], MLA_QK)
        dkr = jnp.sum(dxf3, axis=0)
        dx3 = dxf3.astype(MXU_DTYPE)
        dkvn = jnp.zeros((tm, MLA_KV_LORA), F32)
        for h in range(HEADS):
            dwk_ref[h] += _dot_tn(kvn, dx3[h])
            dkvn = dkvn + _dot(dx3[h], wkt_ref[h])
        dvc = jnp.concatenate([(dv_ref[2 * j, 0] + dv_ref[2 * j + 1, 0]).T for j in range(4)],
                              axis=1).astype(MXU_DTYPE)
        dwv_ref[...] += _dot_tn(kvn, dvc)
        dkvn = dkvn + _dot(dvc, wvt_ref[...])
        dqa_ref[...] += _row0(jnp.sum(dqnl * qlhat, axis=0, keepdims=True))
        dkva_ref[...] += _row0(jnp.sum(dkvn * kvhat, axis=0, keepdims=True))
        dqn_ref[...] += _row0(dqw)
        dkn_ref[...] += _row0(dkw)
        d_ref[:, 0:256] = _rms_bwd(dqnl, qlhat, qlr, qa_ref[...], MLA_Q_LORA).astype(MXU_DTYPE)
        d_ref[:, 256:384] = _rms_bwd(dkvn, kvhat, kvr, kva_ref[...], MLA_KV_LORA).astype(MXU_DTYPE)
        d_ref[:, 384:512] = jnp.where((lane >= 64) & (lane < 96), dkr, 0.0).astype(MXU_DTYPE)

    full = lambda shape: pl.BlockSpec(shape, lambda i: (0,) * len(shape))
    hd = pl.BlockSpec((HEADS, tm, LANES), lambda i: (0, i, 0))
    hdt = pl.BlockSpec((HEADS, 1, LANES, tm), lambda i: (0, i, 0, 0))
    tab = pl.BlockSpec((tm, LANES), lambda i: (i, 0))
    return pl.pallas_call(
        body, name="mla_prep_bwd", grid=(T // tm,),
        in_specs=[pl.BlockSpec((tm, 256), lambda i: (i, CB_QLAT)), pl.BlockSpec((tm, LANES), lambda i: (i, CB_KVLAT)),
                  pl.BlockSpec((tm, LANES), lambda i: (i, CB_KROPE)), hd, hdt, hdt,
                  full((1, 256)), full((1, LANES)), full((HEADS, 256, LANES)), full((HEADS, LANES, LANES)),
                  full((LANES, 512)), full((HEADS, LANES, 256)), full((HEADS, LANES, LANES)), full((512, LANES)),
                  full((1, LANES)), full((1, LANES)), tab, tab, tab, pl.BlockSpec(memory_space=pl.ANY)],
        out_specs=[pl.BlockSpec((tm, 512), lambda i: (i, DPB_MLA)), full((HEADS, 256, LANES)),
                   full((HEADS, LANES, LANES)), full((LANES, 512)), full((8, 256)), full((8, LANES)),
                   full((8, LANES)), full((8, LANES))],
        out_shape=[_sds((T, NP), MXU_DTYPE), _sds((HEADS, 256, LANES), F32), _sds((HEADS, LANES, LANES), F32),
                   _sds((LANES, 512), F32), _sds((8, 256), F32), _sds((8, LANES), F32), _sds((8, LANES), F32),
                   _sds((8, LANES), F32)],
        input_output_aliases={19: 0},
        compiler_params=_cp(("arbitrary",), 48))(
            proj, proj, proj, dq, dk, dv, lw["qa"], lw["kva"], lw["wq"], lw["wk"], lw["wv"],
            lw["wqt"], lw["wkt"], lw["wvt"], lw["qn"], lw["kn"], rope[0], rope[1], rope[2], dproj)


def _inproj_bwd_dx(dproj, wpt, x, g_in, ng):
    T, D = x.shape
    tm = min(TM_PROJ, T)

    def body(dp_ref, wt_ref, x_ref, g_ref, w_ref, dx_ref, dw_ref):
        i = pl.program_id(0)

        @pl.when(i == 0)
        def _():
            dw_ref[...] = jnp.zeros_like(dw_ref)

        dh = _dot(dp_ref[...], wt_ref[...])
        xhat, r = _rms(x_ref[...], D)
        dw_ref[...] += _row0(jnp.sum(dh * xhat, axis=0, keepdims=True))
        dx_ref[...] = g_ref[...] + _rms_bwd(dh, xhat, r, w_ref[...], D)

    tile = pl.BlockSpec((tm, D), lambda i: (i, 0))
    return pl.pallas_call(
        body, name="inproj_bwd_dx", grid=(T // tm,),
        in_specs=[pl.BlockSpec((tm, NP), lambda i: (i, 0)), pl.BlockSpec((NP, D), lambda i: (0, 0)), tile, tile,
                  pl.BlockSpec((1, D), lambda i: (0, 0))],
        out_specs=[tile, pl.BlockSpec((8, D), lambda i: (0, 0))],
        out_shape=[_sds((T, D), F32), _sds((8, D), F32)],
        compiler_params=_cp(("arbitrary",), 48))(dproj, wpt, x, g_in, ng)


def _rope_tables(T, token=0.0):
    half = MLA_ROPE // 2
    inv_freq = jnp.power(jnp.float32(ROPE_THETA), -jnp.arange(half, dtype=F32) / half)
    z = lambda n: jnp.zeros((n,), F32)
    freq = jnp.concatenate([z(MLA_NOPE), inv_freq, inv_freq, z(32)])
    first = jnp.concatenate([z(64), jnp.ones((16,), F32), z(48)])
    second = jnp.concatenate([z(80), jnp.ones((16,), F32), z(32)])
    ang = (jnp.arange(T, dtype=F32) + token)[:, None] * freq[None, :]
    sin = jnp.sin(ang)
    return jnp.cos(ang), -sin * first[None, :], sin * second[None, :]


def _pad_lanes(v, n=LANES):
    v = v.reshape(1, -1)
    return jnp.pad(v, ((0, 0), (0, n - v.shape[1])))


def _pack_win_t(wt):
    z = lambda n: jnp.zeros((n, wt.shape[1]), wt.dtype)
    return jnp.concatenate([wt[416:2976], wt[3744:4256], wt[0:384], z(64), wt[384:416], z(32), wt[2976:3488],
                            wt[3488:3616], wt[3616:3744]], axis=0)


def _unpack_dwin(d):
    return jnp.concatenate([d[:, 3072:3456], d[:, 3520:3552], d[:, 0:2560], d[:, 3584:4096], d[:, 4096:4224],
                            d[:, 4224:4352], d[:, 2560:3072]], axis=1)


def _inproj_weights(l, norm_g, w_in_t):
    wpt = _pack_win_t(w_in_t)
    return dict(ng=norm_g[l].reshape(1, -1), wp=wpt.T, wpt=wpt)


def _mixer_weights(l, qa, wqb_full, kva, wkvb_full, qn, kn, conv_full, sqn, skn, sinks, w_out_full):
    wq = jnp.pad(wqb_full, ((0, 0), (0, 0), (0, LANES - MLA_QK)))
    wk = jnp.pad(wkvb_full[:, :, :MLA_NOPE], ((0, 0), (0, 0), (0, LANES - MLA_NOPE)))
    wv = jnp.transpose(wkvb_full[:, :, MLA_NOPE:], (1, 0, 2)).reshape(MLA_KV_LORA, GROUP_WIDTH)
    return dict(
        qa=qa[l].reshape(1, -1), kva=kva[l].reshape(1, -1),
        wq=wq, wk=wk, wv=wv, wqt=jnp.transpose(wq, (0, 2, 1)), wkt=jnp.transpose(wk, (0, 2, 1)), wvt=wv.T,
        qn=_pad_lanes(qn[l]), kn=_pad_lanes(kn[l]),
        conv=jnp.pad(conv_full, ((0, 5), (0, 0))),
        sqn=jnp.tile(sqn[l].reshape(1, -1), (1, 2)), skn=jnp.tile(skn[l].reshape(1, -1), (1, 2)),
        sinks=sinks[l], wo=w_out_full, wot=w_out_full.T)


def _layer_weights(l, norm_g, w_in_full, qa, wqb_full, kva, wkvb_full, qn, kn, conv_full, sqn, skn, sinks,
                   w_out_full):
    return dict(_inproj_weights(l, norm_g, w_in_full.T),
                **_mixer_weights(l, qa, wqb_full, kva, wkvb_full, qn, kn, conv_full, sqn, skn, sinks, w_out_full))


def _layer_fwd(x, lw, rope, late_weights=None, target=None):
    proj, h = _inproj_fwd(x, lw["ng"], lw["wp"])
    if late_weights is not None:
        lw = dict(lw, **late_weights(proj))
    q, k, kt, vt = _mla_prep_fwd(proj, lw, rope)
    o_mla, lse = _mla_attn_fwd(q, k, vt)
    o_swa = _swa_fwd(proj, lw)
    ycat = _mix_fwd(proj, o_mla, o_swa, lw["conv"])
    if target is None:
        out = _mm_nn(ycat, lw["wo"], "outproj_fwd", residual=x)
    else:
        out = _outproj_loss(ycat, lw["wo"], x, target)
    return out, dict(x=x, proj=proj, h=h, q=q, k=k, kt=kt, vt=vt, o_mla=o_mla, lse=lse, o_swa=o_swa, ycat=ycat,
                     lw=lw)


def _layer_bwd(g, sv, lw, rope, on_big_grads=None):
    proj = sv["proj"]
    dycat, d_wo = _outproj_bwd(g, sv["ycat"], lw["wot"])
    dproj, do_mla, do_swa, d_conv = _mix_bwd(dycat, proj, sv["o_mla"], sv["o_swa"], lw["conv"])
    dproj, dkn_acc, dv_acc, d_sqn, d_sinks = _swa_bwd(proj, sv["o_swa"], do_swa, lw, dproj)
    dproj, d_skn = _swa_kv_bwd(proj, dkn_acc, dv_acc, lw, dproj)
    dq, dk, dv = _mla_attn_bwd(sv["q"], sv["k"], sv["kt"], sv["vt"], sv["o_mla"], do_mla, sv["lse"])
    dproj, d_wq, d_wk, d_wv, d_qa, d_kva, d_qn, d_kn = _mla_prep_bwd(proj, dq, dk, dv, lw, rope, dproj)
    grads = dict(
        w_out=d_wo, w_qb=d_wq[:, :, :MLA_QK],
        w_kvb=jnp.concatenate([d_wk[:, :, :MLA_NOPE],
                               jnp.transpose(d_wv.reshape(MLA_KV_LORA, HEADS, MLA_NOPE), (1, 0, 2))], axis=2))
    token = 0.0 if on_big_grads is None else on_big_grads("mixer", grads)
    d_wp = _mm_tn(sv["h"], dproj, "inproj_bwd_dw", WIRE_DTYPE, tn=NP // 2)
    grads["w_in"] = _unpack_dwin(d_wp)
    token = token if on_big_grads is None else token + on_big_grads("w_in", grads)
    dx, d_ng = _inproj_bwd_dx(dproj, lw["wpt"], sv["x"], g, lw["ng"] + token)
    grads.update(
        conv=d_conv[0:3], norm_g=d_ng[0], qa=d_qa[0], kva=d_kva[0], qn=d_qn[0, :MLA_QK], kn=d_kn[0, :MLA_QK],
        sqn=d_sqn[0, :SWA_HEAD_DIM], skn=d_skn[0, :SWA_HEAD_DIM], sinks=d_sinks[:, 0])
    return dx, grads


def _local_step(x, target, lws, rope):
    saved = []
    for l, lw in enumerate(lws):
        x, sv = _layer_fwd(x, lw, rope, target=target if l == len(lws) - 1 else None)
        saved.append(sv)
    g, loss_tile = x
    grads = [None] * len(lws)
    for l in reversed(range(len(lws))):
        g, grads[l] = _layer_bwd(g, saved[l], lws[l], rope)
    return loss_tile, g, grads


def _my_coords():
    return lax.axis_index("x"), lax.axis_index("y"), lax.axis_index("c")


def _peer(me, k):
    x, y, c = me
    return (1 - x if k & 4 else x, 1 - y if k & 2 else y, 1 - c if k & 1 else c)


def _lin(d):
    return 4 * d[0] + 2 * d[1] + d[2]


def _push_copies(ins, lands, send_sems, recv_sems, gather, incoming=False):
    me = _my_coords()
    my = _lin(me)
    copies = []
    for a in range(len(ins)):
        for k in range(1, N_DEV):
            peer = _peer(me, k)
            src = ins[a] if gather else ins[a].at[_lin(peer)]
            copies.append(pltpu.make_async_remote_copy(
                src_ref=src, dst_ref=lands[a].at[_lin(peer) if incoming else my],
                send_sem=send_sems.at[a * 7 + k - 1], recv_sem=recv_sems.at[a * 7 + k - 1],
                device_id=peer, device_id_type=pl.DeviceIdType.MESH))
    return copies


def _push_start(arrays, name, gather):
    n = len(arrays)
    land_shapes = [((N_DEV,) + a.shape) if gather else a.shape for a in arrays]

    def body(*refs):
        ins, lands = refs[:n], refs[n:2 * n]
        send_sems, recv_sems = refs[2 * n], refs[2 * n + 1]
        token = refs[-1]
        for cp in _push_copies(ins, lands, send_sems, recv_sems, gather):
            cp.start()
        token[...] = jnp.zeros_like(token)

    hbm = pl.BlockSpec(memory_space=pltpu.HBM)
    sem = pl.BlockSpec(memory_space=pltpu.SEMAPHORE)
    res = pl.pallas_call(
        body, name=name,
        out_shape=(pltpu.SemaphoreType.DMA((7 * n,)), pltpu.SemaphoreType.DMA((7 * n,)),
                   *[pltpu.HBM(a.shape, a.dtype) for a in arrays],
                   *[pltpu.HBM(s, a.dtype) for s, a in zip(land_shapes, arrays)],
                   _sds((8, LANES), F32)),
        in_specs=(hbm,) * (2 * n),
        out_specs=(sem, sem) + (hbm,) * (2 * n) + (pl.BlockSpec(memory_space=pltpu.VMEM),),
        input_output_aliases={i: 2 + i for i in range(2 * n)},
        compiler_params=pltpu.CompilerParams(has_side_effects=pltpu.SideEffectType.DATAFLOW_SIDE_EFFECTING),
    )(*[pltpu.with_memory_space_constraint(a, pltpu.HBM) for a in arrays],
      *[pltpu.with_memory_space_constraint(lax.empty(s, a.dtype), pltpu.HBM) for s, a in zip(land_shapes, arrays)])
    return dict(send=res[0], recv=res[1], src=res[2:2 + n], land=res[2 + n:2 + 2 * n], token=res[-1][0, 0],
                gather=gather)


def _push_wait(handle, after, name):
    n = len(handle["src"])
    gather = handle["gather"]

    def body(*refs):
        ins, lands = refs[:n], refs[n:2 * n]
        send_sems, recv_sems = refs[2 * n], refs[2 * n + 1]
        for cp in _push_copies(ins, lands, send_sems, recv_sems, gather):
            cp.wait_send()
        for cp in _push_copies(ins, lands, send_sems, recv_sems, gather, incoming=True):
            cp.wait_recv()

    hbm = pl.BlockSpec(memory_space=pltpu.HBM)
    sem = pl.BlockSpec(memory_space=pltpu.SEMAPHORE)
    res = pl.pallas_call(
        body, name=name,
        out_shape=tuple(pltpu.HBM(a.shape, a.dtype) for a in (*handle["src"], *handle["land"])),
        in_specs=(hbm,) * (2 * n) + (sem, sem, pl.BlockSpec(memory_space=pl.ANY)),
        out_specs=(hbm,) * (2 * n),
        input_output_aliases={i: i for i in range(2 * n)},
        compiler_params=pltpu.CompilerParams(has_side_effects=pltpu.SideEffectType.DATAFLOW_SIDE_EFFECTING),
    )(*handle["src"], *handle["land"], handle["send"], handle["recv"], after)
    return res[n:]


def _small_all_reduce(v):
    R = v.shape[0]

    def body(v_ref, o_ref, buf, send_sems, recv_sems):
        me = _my_coords()
        my = _lin(me)
        sends = []
        for k in range(1, N_DEV):
            cp = pltpu.make_async_remote_copy(
                src_ref=v_ref, dst_ref=buf.at[my], send_sem=send_sems.at[k - 1], recv_sem=recv_sems.at[k - 1],
                device_id=_peer(me, k), device_id_type=pl.DeviceIdType.MESH)
            cp.start()
            sends.append(cp)
        buf[my] = v_ref[...]
        for k in range(1, N_DEV):
            pltpu.make_async_remote_copy(
                src_ref=v_ref, dst_ref=buf.at[_lin(_peer(me, k))], send_sem=send_sems.at[k - 1],
                recv_sem=recv_sems.at[k - 1], device_id=_peer(me, k),
                device_id_type=pl.DeviceIdType.MESH).wait_recv()
        for cp in sends:
            cp.wait_send()
        tot = buf[0]
        for d in range(1, N_DEV):
            tot = tot + buf[d]
        o_ref[...] = tot

    vm = pl.BlockSpec(memory_space=pltpu.VMEM)
    return pl.pallas_call(
        body, name="small_all_reduce", in_specs=[vm], out_specs=vm, out_shape=_sds(v.shape, F32),
        scratch_shapes=[pltpu.VMEM((N_DEV, R, LANES), F32), pltpu.SemaphoreType.DMA((7,)),
                        pltpu.SemaphoreType.DMA((7,))],
    )(v)


def _adamw_math(w, g, m, v):
    m = ADAM_B1 * m + (1.0 - ADAM_B1) * g
    v = ADAM_B2 * v + (1.0 - ADAM_B2) * (g * g)
    m_hat = m / (1.0 - ADAM_B1 ** ADAM_STEP)
    v_hat = v / (1.0 - ADAM_B2 ** ADAM_STEP)
    delta = -ADAM_LR * (m_hat / (jnp.sqrt(v_hat) + ADAM_EPS) + ADAM_WD * w)
    return delta, m, v


def _adamw(parts, w, m, v, name, tr):
    P, R, C = parts.shape
    tr = min(tr, R)

    def body(p_ref, w_ref, m_ref, v_ref, g_out, d_out, m_out, v_out):
        g = p_ref[0].astype(F32)
        for d in range(1, P):
            g = g + p_ref[d].astype(F32)
        delta, m_new, v_new = _adamw_math(w_ref[...], g, m_ref[...], v_ref[...])
        g_out[...] = g
        d_out[...] = delta
        m_out[...] = m_new
        v_out[...] = v_new

    tile = pl.BlockSpec((tr, C), lambda i: (i, 0))
    return pl.pallas_call(
        body, name=name, grid=(R // tr,),
        in_specs=[pl.BlockSpec((P, tr, C), lambda i: (0, i, 0)), tile, tile, tile],
        out_specs=[tile] * 4, out_shape=[_sds((R, C), F32)] * 4,
        compiler_params=_cp(("parallel",), 32))(parts, w, m, v)


SMALL = (("norm_g", D_MODEL), ("mla_q_a_norm", MLA_Q_LORA), ("mla_kv_a_norm", MLA_KV_LORA), ("mla_q_norm", MLA_QK),
         ("mla_k_norm", MLA_QK), ("swa_q_norm", SWA_HEAD_DIM), ("swa_k_norm", SWA_HEAD_DIM), ("swa_sinks", HEADS))
SMALL_GRAD_KEY = dict(norm_g="norm_g", mla_q_a_norm="qa", mla_kv_a_norm="kva", mla_q_norm="qn", mla_k_norm="kn",
                      swa_q_norm="sqn", swa_k_norm="skn", swa_sinks="sinks")
SMALL_ROWS = 32
CONV_ROWS = 24


def _pack_small(get):
    parts = []
    for l in range(DEPTH):
        for name, n in SMALL:
            v = get(name, l).reshape(-1)
            parts.append(jnp.pad(v, (0, (-n) % LANES)))
    return jnp.concatenate(parts).reshape(SMALL_ROWS, LANES)


def _unpack_small(packed):
    flat = packed.reshape(-1)
    out = {name: [] for name, _ in SMALL}
    off = 0
    for l in range(DEPTH):
        for name, n in SMALL:
            out[name].append(flat[off:off + n])
            off += n + (-n) % LANES
    return {name: jnp.stack(v) for name, v in out.items()}


def kernel(x, norm_g, w_in, mla_q_a_norm, mla_w_qb, mla_kv_a_norm, mla_w_kvb, mla_q_norm, mla_k_norm, conv_w, swa_q_norm, swa_k_norm, swa_sinks, w_out, loss_target, m_norm_g, m_w_in, m_mla_q_a_norm, m_mla_w_qb, m_mla_kv_a_norm, m_mla_w_kvb, m_mla_q_norm, m_mla_k_norm, m_conv_w, m_swa_q_norm, m_swa_k_norm, m_swa_sinks, m_w_out, v_norm_g, v_w_in, v_mla_q_a_norm, v_mla_w_qb, v_mla_kv_a_norm, v_mla_w_kvb, v_mla_q_norm, v_mla_k_norm, v_conv_w, v_swa_q_norm, v_swa_k_norm, v_swa_sinks, v_w_out):
    T = x.shape[1]
    weights = dict(norm_g=norm_g, w_in=w_in, mla_q_a_norm=mla_q_a_norm, mla_w_qb=mla_w_qb,
                   mla_kv_a_norm=mla_kv_a_norm, mla_w_kvb=mla_w_kvb, mla_q_norm=mla_q_norm, mla_k_norm=mla_k_norm,
                   conv_w=conv_w, swa_q_norm=swa_q_norm, swa_k_norm=swa_k_norm, swa_sinks=swa_sinks, w_out=w_out)
    mom_m = dict(norm_g=m_norm_g, w_in=m_w_in, mla_q_a_norm=m_mla_q_a_norm, mla_w_qb=m_mla_w_qb,
                 mla_kv_a_norm=m_mla_kv_a_norm, mla_w_kvb=m_mla_w_kvb, mla_q_norm=m_mla_q_norm,
                 mla_k_norm=m_mla_k_norm, conv_w=m_conv_w, swa_q_norm=m_swa_q_norm, swa_k_norm=m_swa_k_norm,
                 swa_sinks=m_swa_sinks, w_out=m_w_out)
    mom_v = dict(norm_g=v_norm_g, w_in=v_w_in, mla_q_a_norm=v_mla_q_a_norm, mla_w_qb=v_mla_w_qb,
                 mla_kv_a_norm=v_mla_kv_a_norm, mla_w_kvb=v_mla_w_kvb, mla_q_norm=v_mla_q_norm,
                 mla_k_norm=v_mla_k_norm, conv_w=v_conv_w, swa_q_norm=v_swa_q_norm, swa_k_norm=v_swa_k_norm,
                 swa_sinks=v_swa_sinks, w_out=v_w_out)

    my = _lin(_my_coords())

    def shards(l):
        return [w_in[l].astype(MXU_DTYPE).T, mla_w_qb[l].astype(MXU_DTYPE), mla_w_kvb[l].astype(MXU_DTYPE),
                w_out[l].astype(MXU_DTYPE), conv_w[l]]

    def inproj_weights(l, g_win_t):
        return _inproj_weights(l, norm_g, g_win_t.reshape(IN_COLS, D_MODEL))

    def mixer_weights(l, gathered):
        g_wqb, g_wkvb, g_wout, g_conv = gathered
        return _mixer_weights(
            l, mla_q_a_norm, g_wqb, mla_kv_a_norm, g_wkvb, mla_q_norm, mla_k_norm,
            jnp.transpose(g_conv, (1, 0, 2)).reshape(3, GROUP_WIDTH), swa_q_norm, swa_k_norm, swa_sinks,
            g_wout.reshape(D_MIX, D_MODEL))

    slot_of = dict(
        w_in=lambda g: jnp.transpose(g["w_in"].reshape(D_MODEL, N_DEV, IN_COLS // N_DEV), (1, 0, 2)),
        w_out=lambda g: g["w_out"].reshape(N_DEV, D_MIX // N_DEV, D_MODEL),
        w_qb=lambda g: g["w_qb"], w_kvb=lambda g: g["w_kvb"])

    def own_slot(landed, mine):
        return [lax.dynamic_update_index_in_dim(a, m, my, 0) for a, m in zip(landed, mine)]

    def landed(handle, after, name, mine):
        return own_slot(_push_wait(handle, after, name), mine)

    sh = [shards(0), shards(1)]
    gather_in0 = _push_start(sh[0][:1], "weight_gather_in0_start", gather=True)
    rope = _rope_tables(T, gather_in0["token"])
    big_shapes = dict(w_in=(DEPTH * D_MODEL, IN_COLS // N_DEV), w_out=(DEPTH * D_MIX // N_DEV, D_MODEL),
                      mla_w_qb=(DEPTH * MLA_Q_LORA, MLA_QK), mla_w_kvb=(DEPTH * MLA_KV_LORA, 128))
    pad_conv = lambda a: jnp.pad(a.reshape(-1), (0, 8 * LANES - 6 * 64)).reshape(8, LANES)
    cat = lambda src: jnp.concatenate([_pack_small(lambda name, l: src[name][l]), pad_conv(src["conv_w"])], axis=0)
    adam_in = {name: [src[name].reshape(shape) for src in (weights, mom_m, mom_v)]
               for name, shape in big_shapes.items()}
    adam_in["small"] = [cat(weights), cat(mom_m), cat(mom_v)]
    rope0, adam_in, casts = lax.optimization_barrier((rope[0], adam_in, [sh[0][1:4], sh[1][:4]]))
    sh = [sh[0][:1] + casts[0] + [conv_w[0]], casts[1] + [conv_w[1]]]
    w_in0_t = landed(gather_in0, rope0, "weight_gather_in0_wait", sh[0][:1])[0]
    w_in0_t, conv0 = lax.optimization_barrier((w_in0_t, conv_w[0]))
    gather0 = _push_start(sh[0][1:4] + [conv0], "weight_gather0_start", gather=True)
    lw0 = inproj_weights(0, w_in0_t)
    lw0 = dict(lw0, ng=lw0["ng"] + gather0["token"])
    layer1 = {}

    def mixer0(proj):
        got = landed(gather0, proj, "weight_gather0_wait", sh[0][1:])
        got[0], conv1 = lax.optimization_barrier((got[0], conv_w[1]))
        layer1["gather"] = _push_start(sh[1][:4] + [conv1], "weight_gather1_start", gather=True)
        mw = mixer_weights(0, got)
        return dict(mw, qa=mw["qa"] + layer1["gather"]["token"])

    x1, sv0 = _layer_fwd(x[0], lw0, rope, late_weights=mixer0)
    g1_all = landed(layer1["gather"], x1, "weight_gather1_wait", sh[1])
    (g2, loss_tile), sv1 = _layer_fwd(x1, dict(inproj_weights(1, g1_all[0]), **mixer_weights(1, g1_all[1:])), rope,
                                      target=loss_target[0])

    parts = {(1, "w_in"): ("w_in", "w_out", "w_qb", "w_kvb"), (0, "mixer"): ("w_out", "w_qb", "w_kvb"),
             (0, "w_in"): ("w_in",)}
    started = []

    def start_exchange(l, part, g):
        if (l, part) not in parts:
            return 0.0
        sl = [slot_of[n](g) for n in parts[(l, part)]]
        handle = _push_start(sl, "grad_exchange%d_%s_start" % (l, part), gather=False)
        started.append((l, part, sl, handle))
        return handle["token"]

    g1, grads1 = _layer_bwd(g2, sv1, sv1["lw"], rope, on_big_grads=functools.partial(start_exchange, 1))
    lw0b = dict(sv0["lw"], conv=sv0["lw"]["conv"] + started[0][3]["token"])
    grad_x, grads0 = _layer_bwd(g1, sv0, lw0b, rope, on_big_grads=functools.partial(start_exchange, 0))
    recv = {}

    def receive(l, part, sl, handle, after):
        got = landed(handle, after, "grad_exchange%d_%s_wait" % (l, part), [s[my] for s in sl])
        recv.update({(l, n): a for n, a in zip(parts[(l, part)], got)})

    for entry in started[:-1]:
        receive(*entry, after=grad_x)
    grads = [grads0, grads1]
    stacked = lambda n: jnp.stack([recv[(0, n)], recv[(1, n)]], axis=1)

    small = jnp.concatenate([
        _pack_small(lambda name, l: grads[l][SMALL_GRAD_KEY[name]]),
        jnp.stack([g["conv"] for g in grads]).reshape(CONV_ROWS, LANES),
        loss_tile], axis=0)
    small = _small_all_reduce(small)
    loss = small[SMALL_ROWS + CONV_ROWS, 0]
    my = _lin(_my_coords())
    conv_g = lax.dynamic_slice_in_dim(small[SMALL_ROWS:SMALL_ROWS + CONV_ROWS].reshape(DEPTH, 3, GROUP_WIDTH),
                                      my * 64, 64, axis=2)

    out = {}

    def big(name, recv, tr):
        res = _adamw(recv.reshape((N_DEV,) + big_shapes[name]), *adam_in[name], "adamw_" + name, tr)
        out[name] = [r.reshape(weights[name].shape) for r in res]

    big("w_out", stacked("w_out"), 192)
    big("mla_w_qb", stacked("w_qb"), 512)
    big("mla_w_kvb", stacked("w_kvb"), 256)
    receive(*started[-1], after=out["w_out"][1])
    big("w_in", stacked("w_in"), 256)

    g_small = jnp.concatenate([small[:SMALL_ROWS], pad_conv(conv_g)], axis=0)
    res = _adamw(g_small[None], *adam_in["small"], "adamw_small", SMALL_ROWS + 8)
    smalls = [_unpack_small(r[:SMALL_ROWS]) for r in res]
    for name, _ in SMALL:
        out[name] = [s[name] for s in smalls]
    out["conv_w"] = [r[SMALL_ROWS:].reshape(-1)[:6 * 64].reshape(DEPTH, 3, 64) for r in res]

    order = ["norm_g", "w_in", "mla_q_a_norm", "mla_w_qb", "mla_kv_a_norm", "mla_w_kvb", "mla_q_norm", "mla_k_norm",
             "conv_w", "swa_q_norm", "swa_k_norm", "swa_sinks", "w_out"]
    result = [loss, grad_x[None]]
    for idx in range(4):
        result += [out[name][idx] for name in order]
    return tuple(result)
```

```python
import functools

import jax
import jax.numpy as jnp
import numpy as np
from jax import lax
from jax.experimental import pallas as pl
from jax.experimental.pallas import tpu as pltpu

F32 = jnp.float32
MXU_DTYPE = jnp.bfloat16
WIRE_DTYPE = jnp.bfloat16

N_DEV = 8
DEPTH = 2
D_MODEL = 1024
GROUP_WIDTH = 512
D_MIX = 3 * GROUP_WIDTH
BLOCK = 128
RMS_EPS = 1e-6
NEG_INF = -1e30
HEADS = 8
MLA_QK = 96
MLA_NOPE = 64
MLA_ROPE = 32
MLA_Q_LORA = 256
MLA_KV_LORA = 128
ROPE_THETA = 10000.0
SWA_HEAD_DIM = 64
LANES = 128
IN_COLS = 4256

ADAM_LR = 0.001
ADAM_B1 = 0.9
ADAM_B2 = 0.999
ADAM_EPS = 1e-08
ADAM_WD = 0.01
ADAM_STEP = 10

NP = 4352
CB_GMLA, CB_CH, CB_CB, CB_CC, CB_GCONV, CB_GSWA, CB_SQ = 0, 1, 2, 3, 4, 5, 7
CB_QLAT = 12
CB_KVLAT, CB_KROPE = 26, 27
CB_SK, CB_SV = 32, 33
DPB_MIX, DPB_MLA, DPB_SQ, DPB_SKV = 0, 6, 7, 16

TM_PROJ = 512
TM_ROW = 256
TK = 256
TQ = 2 * TK
MLA_SCALE = MLA_QK ** -0.5
MLA_ONES_ROW = (64, 0)
LOG2E = 1.4426950408889634
LN2 = 0.6931471805599453
TM_SWA = 512
VMEM_MB = 2 ** 20


def _cp(sem, vmem_mb):
    return pltpu.CompilerParams(dimension_semantics=sem, vmem_limit_bytes=vmem_mb * VMEM_MB)


def _sds(shape, dtype):
    return jax.ShapeDtypeStruct(shape, dtype)


def _dot(a, b):
    return jnp.dot(a, b, preferred_element_type=F32)


def _dot_nt(a, b):
    return lax.dot_general(a, b, (((1,), (1,)), ((), ())), preferred_element_type=F32)


def _dot_tn(a, b):
    return lax.dot_general(a, b, (((0,), (0,)), ((), ())), preferred_element_type=F32)


def _rms(x, n):
    r = lax.rsqrt(jnp.sum(x * x, axis=-1, keepdims=True) * (1.0 / n) + RMS_EPS)
    return x * r, r


def _rms_bwd(dy, xhat, r, w, n):
    g = dy * w
    return r * (g - xhat * (jnp.sum(g * xhat, axis=-1, keepdims=True) * (1.0 / n)))


def _rms_halves(x, half1):
    x2 = x * x
    s0 = jnp.sum(jnp.where(half1, 0.0, x2), axis=-1, keepdims=True)
    s1 = jnp.sum(jnp.where(half1, x2, 0.0), axis=-1, keepdims=True)
    r = jnp.where(half1, lax.rsqrt(s1 * (1.0 / 64) + RMS_EPS), lax.rsqrt(s0 * (1.0 / 64) + RMS_EPS))
    return x * r, r


def _rms_halves_bwd(dy, xhat, r, w, half1):
    g = dy * w
    t = g * xhat
    m0 = jnp.sum(jnp.where(half1, 0.0, t), axis=-1, keepdims=True) * (1.0 / 64)
    m1 = jnp.sum(jnp.where(half1, t, 0.0), axis=-1, keepdims=True) * (1.0 / 64)
    return r * (g - xhat * jnp.where(half1, m1, m0))


def _sigmoid(x):
    return 1.0 / (1.0 + jnp.exp(-x))


def _rope(x, c, s1, s2):
    ax = x.ndim - 1
    return x * c + pltpu.roll(x, 112, ax) * s1 + pltpu.roll(x, 16, ax) * s2


def _rope_bwd(dy, c, s1, s2):
    ax = dy.ndim - 1
    return dy * c + pltpu.roll(dy * s1, 16, ax) + pltpu.roll(dy * s2, 112, ax)


def _fold_rows8(x):
    return jnp.sum(x.reshape(x.shape[0] // 8, 8, x.shape[1]), axis=0)


def _row0(v, rows=8):
    row = lax.broadcasted_iota(jnp.int32, (rows, v.shape[1]), 0)
    return jnp.where(row == 0, jnp.broadcast_to(v, (rows, v.shape[1])), 0.0)


def _mm_nn(a, b, name, out_dtype=F32, residual=None, tm=TM_PROJ):
    M, K = a.shape
    N = b.shape[1]
    tm = min(tm, M)

    def body(*refs):
        if residual is None:
            a_ref, b_ref, o_ref = refs
            acc = _dot(a_ref[...].astype(MXU_DTYPE), b_ref[...])
        else:
            a_ref, b_ref, r_ref, o_ref = refs
            acc = _dot(a_ref[...].astype(MXU_DTYPE), b_ref[...]) + r_ref[...]
        o_ref[...] = acc.astype(out_dtype)

    in_specs = [pl.BlockSpec((tm, K), lambda i: (i, 0)), pl.BlockSpec((K, N), lambda i: (0, 0))]
    args = [a, b]
    if residual is not None:
        in_specs.append(pl.BlockSpec((tm, N), lambda i: (i, 0)))
        args.append(residual)
    return pl.pallas_call(
        body, name=name, grid=(M // tm,), in_specs=in_specs,
        out_specs=pl.BlockSpec((tm, N), lambda i: (i, 0)), out_shape=_sds((M, N), out_dtype),
        compiler_params=_cp(("parallel",), 48))(*args)


def _mm_tn(a, b, name, out_dtype, tn, tk=512):
    T, M = a.shape
    N = b.shape[1]
    tk = min(tk, T)
    nk = T // tk

    def body(a_ref, b_ref, o_ref, acc_ref):
        k = pl.program_id(1)

        @pl.when(k == 0)
        def _():
            acc_ref[...] = jnp.zeros_like(acc_ref)

        acc_ref[...] += _dot_tn(a_ref[...].astype(MXU_DTYPE), b_ref[...].astype(MXU_DTYPE))

        @pl.when(k == nk - 1)
        def _():
            o_ref[...] = acc_ref[...].astype(out_dtype)

    return pl.pallas_call(
        body, name=name, grid=(N // tn, nk),
        in_specs=[pl.BlockSpec((tk, M), lambda n, k: (k, 0)), pl.BlockSpec((tk, tn), lambda n, k: (k, n))],
        out_specs=pl.BlockSpec((M, tn), lambda n, k: (0, n)), out_shape=_sds((M, N), out_dtype),
        scratch_shapes=[pltpu.VMEM((M, tn), F32)],
        compiler_params=_cp(("parallel", "arbitrary"), 48))(a, b)


def _inproj_fwd(x, ng, wp):
    T, D = x.shape
    tm = min(TM_PROJ, T)

    def body(x_ref, g_ref, w_ref, proj_ref, h_ref):
        xhat, _ = _rms(x_ref[...], D)
        h = (xhat * g_ref[...]).astype(MXU_DTYPE)
        h_ref[...] = h
        proj_ref[...] = _dot(h, w_ref[...])

    return pl.pallas_call(
        body, name="inproj_fwd", grid=(T // tm,),
        in_specs=[pl.BlockSpec((tm, D), lambda i: (i, 0)), pl.BlockSpec((1, D), lambda i: (0, 0)),
                  pl.BlockSpec((D, NP), lambda i: (0, 0))],
        out_specs=[pl.BlockSpec((tm, NP), lambda i: (i, 0)), pl.BlockSpec((tm, D), lambda i: (i, 0))],
        out_shape=[_sds((T, NP), F32), _sds((T, D), MXU_DTYPE)],
        compiler_params=_cp(("parallel",), 48))(x, ng, wp)


def _mla_prep_fwd(proj, lw, rope):
    T = proj.shape[0]
    tk = min(TK, T // 2)
    nsub = 2
    tm = nsub * tk

    def body(ql_ref, kvl_ref, kr_ref, qa_ref, kva_ref, wq_ref, wk_ref, wv_ref, qn_ref, kn_ref,
             c_ref, s1_ref, s2_ref, q_out, k_out, kt_out, vt_out):
        c, s1, s2 = c_ref[...], s1_ref[...], s2_ref[...]
        qhat, _ = _rms(ql_ref[...], MLA_Q_LORA)
        qn = (qhat * qa_ref[...]).astype(MXU_DTYPE)
        khat, _ = _rms(kvl_ref[...], MLA_KV_LORA)
        kvn = (khat * kva_ref[...]).astype(MXU_DTYPE)
        kr = kr_ref[...]
        half1 = lax.broadcasted_iota(jnp.int32, (tm, LANES), 1) >= 64
        ones_row = lax.broadcasted_iota(jnp.int32, (LANES, 1), 0)
        q3, _ = _rms(jnp.stack([_dot(qn, wq_ref[h]) for h in range(HEADS)]), MLA_QK)
        q_out[...] = (_rope(q3 * qn_ref[...], c, s1, s2) * (MLA_SCALE * LOG2E)).astype(MXU_DTYPE)
        k3, _ = _rms(jnp.stack([_dot(kvn, wk_ref[h]) for h in range(HEADS)]) + kr, MLA_QK)
        k3 = _rope(k3 * kn_ref[...], c, s1, s2)
        k_out[...] = k3.astype(MXU_DTYPE)
        for h in range(HEADS):
            for t in range(nsub):
                kt_out[h, t] = k3[h, tk * t:tk * (t + 1)].T.astype(MXU_DTYPE)
        v = _dot(kvn, wv_ref[...])
        for h in range(HEADS):
            vp = v[:, LANES * (h // 2):LANES * (h // 2 + 1)]
            own = half1 if h % 2 else jnp.logical_not(half1)
            vp = jnp.where(own, vp, 0.0)
            for t in range(nsub):
                vpt = vp[tk * t:tk * (t + 1)].T
                vt_out[h, t] = jnp.where(ones_row == MLA_ONES_ROW[h % 2], 1.0, vpt).astype(MXU_DTYPE)

    full = lambda shape: pl.BlockSpec(shape, lambda i: (0,) * len(shape))
    hd = pl.BlockSpec((HEADS, tm, LANES), lambda i: (0, i, 0))
    hdt = pl.BlockSpec((HEADS, nsub, LANES, tk), lambda i: (0, i, 0, 0))
    nat = _sds((HEADS, T, LANES), MXU_DTYPE)
    tr = _sds((HEADS, T // tk, LANES, tk), MXU_DTYPE)
    return pl.pallas_call(
        body, name="mla_prep_fwd", grid=(T // tm,),
        in_specs=[pl.BlockSpec((tm, 256), lambda i: (i, CB_QLAT)), pl.BlockSpec((tm, LANES), lambda i: (i, CB_KVLAT)),
                  pl.BlockSpec((tm, LANES), lambda i: (i, CB_KROPE)),
                  full((1, 256)), full((1, LANES)), full((HEADS, 256, LANES)), full((HEADS, LANES, LANES)),
                  full((LANES, 512)), full((1, LANES)), full((1, LANES)),
                  pl.BlockSpec((tm, LANES), lambda i: (i, 0)), pl.BlockSpec((tm, LANES), lambda i: (i, 0)),
                  pl.BlockSpec((tm, LANES), lambda i: (i, 0))],
        out_specs=[hd, hd, hdt, hdt],
        out_shape=[nat, nat, tr, tr],
        compiler_params=_cp(("parallel",), 32))(
            proj, proj, proj, lw["qa"], lw["kva"], lw["wq"], lw["wk"], lw["wv"], lw["qn"], lw["kn"],
            rope[0], rope[1], rope[2])


def _mla_attn_fwd(q, k, vt):
    T = q.shape[1]
    tk = min(TK, T // 2)
    tq = 2 * tk

    def body(q_ref, k_ref, vt_ref, o_ref, lse_ref, acc_s, m_s, s_a, s_b):
        i = pl.program_id(1)
        key = lax.broadcasted_iota(jnp.int32, (tk, tq), 0)
        qry = lax.broadcasted_iota(jnp.int32, (tk, tq), 1)
        qs = [q_ref[0], q_ref[1]]
        acc_s[...] = jnp.zeros_like(acc_s)
        m_s[...] = jnp.full(m_s.shape, NEG_INF, F32)

        def scores(kj, buf):
            rows = pl.ds(pl.multiple_of(kj * tk, tk), tk)
            for r in range(2):
                buf[r] = _dot_nt(k_ref[r, rows, :], qs[r])

        def consume(kj, buf, diag):
            for r in range(2):
                s = buf[r]
                if diag is not None:
                    s = jnp.where(key + diag * tk <= qry, s, NEG_INF)
                m_old = m_s[r]
                m_new = jnp.maximum(m_old, jnp.max(s, axis=0, keepdims=True))
                alpha = jnp.exp2(m_old - m_new)
                p = jnp.exp2(s - m_new)
                m_s[r] = m_new
                acc_s[r] = alpha * acc_s[r] + _dot(vt_ref[r, kj], p.astype(MXU_DTYPE))

        scores(0, s_a)

        def pair(kj):
            scores(kj + 1, s_b)
            consume(kj, s_a, None)
            scores(kj + 2, s_a)
            consume(kj + 1, s_b, None)

        def octet(ko, carry):
            for t in range(4):
                pair(8 * ko + 2 * t)
            return carry

        lax.fori_loop(0, i // 4, octet, 0)

        @pl.when(i % 4 >= 2)
        def _():
            pair(8 * (i // 4))
            pair(8 * (i // 4) + 2)

        @pl.when(i % 2 == 1)
        def _():
            pair(2 * i - 2)

        last = pl.ds(pl.multiple_of((2 * i + 1) * tk, tk), tk)
        for r in range(2):
            s_b[r, :, tk:] = _dot_nt(k_ref[r, last, :], qs[r][tk:])
        consume(2 * i, s_a, 0)
        for r in range(2):
            s = jnp.where(key[:, tk:] + tk <= qry[:, tk:], s_b[r, :, tk:], NEG_INF)
            m_old = m_s[r, :, tk:]
            m_new = jnp.maximum(m_old, jnp.max(s, axis=0, keepdims=True))
            p = jnp.exp2(s - m_new)
            m_s[r, :, tk:] = m_new
            acc_s[r, :, tk:] = (jnp.exp2(m_old - m_new) * acc_s[r, :, tk:]
                                + _dot(vt_ref[r, 2 * i + 1], p.astype(MXU_DTYPE)))
        l = [acc_s[r, pl.ds(MLA_ONES_ROW[r], 1), :] for r in range(2)]
        head0 = lax.broadcasted_iota(jnp.int32, (LANES, 1), 0) < 64
        o_ref[...] = jnp.where(head0, acc_s[0] / l[0], acc_s[1] / l[1]).T
        for r in range(2):
            lse_ref[r] = m_s[r] + jnp.log2(l[r])

    return pl.pallas_call(
        body, name="mla_attn_fwd", grid=(HEADS // 2, T // tq),
        in_specs=[pl.BlockSpec((2, tq, LANES), lambda j, i: (j, i, 0)),
                  pl.BlockSpec((2, T, LANES), lambda j, i: (j, 0, 0)),
                  pl.BlockSpec((2, T // tk, LANES, tk), lambda j, i: (j, 0, 0, 0))],
        out_specs=[pl.BlockSpec((tq, LANES), lambda j, i: (i, j)),
                   pl.BlockSpec((2, 1, tq), lambda j, i: (j, 0, i))],
        out_shape=[_sds((T, GROUP_WIDTH), F32), _sds((HEADS, 1, T), F32)],
        scratch_shapes=[pltpu.VMEM((2, LANES, tq), F32), pltpu.VMEM((2, 1, tq), F32),
                        pltpu.VMEM((2, tk, tq), F32), pltpu.VMEM((2, tk, tq), F32)],
        compiler_params=_cp(("parallel", "arbitrary"), 40))(q, k, vt)


def _swa_kv_variants(x, half1):
    xs = pltpu.roll(x, 64, 1)
    out = {}
    for g in range(2):
        for r in range(2):
            own = half1 if r else jnp.logical_not(half1)
            out[(g, r)] = jnp.where(own, x if g == r else xs, 0.0).astype(MXU_DTYPE)
    return out


def _swa_alibi():
    ki = np.arange(2 * BLOCK)[:, None]
    qi = np.arange(BLOCK)[None, :]
    dist = BLOCK + qi - ki
    slopes = 2.0 ** -(np.arange(HEADS) + 1.0)
    tab = np.where(((dist >= 0) & (dist < BLOCK))[None], slopes[:, None, None] * dist[None], 1e30)
    return jnp.asarray(tab, F32)


def _swa_kv_variants_t(xt, rows1):
    xs = pltpu.roll(xt, 64, 0)
    out = {}
    for g in range(2):
        for r in range(2):
            own = rows1 if r else jnp.logical_not(rows1)
            out[(g, r)] = jnp.where(own, xt if g == r else xs, 0.0).astype(MXU_DTYPE)
    return out


def _swa_probs(i, nb, q_ref, k_ref, v_ref, pk_ref, pv_ref, qw_ref, kw_ref, alibi_ref, sink_ref):
    scale = SWA_HEAD_DIM ** -0.5
    half1 = lax.broadcasted_iota(jnp.int32, (1, LANES), 1) >= 64
    k_all = jnp.concatenate([pk_ref[...], k_ref[...]], axis=0)
    v_all = jnp.concatenate([pv_ref[...], v_ref[...]], axis=0)
    khat, _ = _rms_halves(k_all, half1)
    kn = khat * kw_ref[...]
    kp = _swa_kv_variants(kn, half1)
    qhat, qr, qn, qt = [], [], [], []
    for j in range(4):
        xh, r = _rms_halves(q_ref[:, LANES * j:LANES * (j + 1)], half1)
        qf = xh * qw_ref[...]
        qhat.append(xh)
        qr.append(r)
        qn.append(qf.astype(MXU_DTYPE))
        qt.append(qf.T.astype(MXU_DTYPE))
    key = lax.broadcasted_iota(jnp.int32, (2 * BLOCK, BLOCK), 0)
    first = jnp.where((i == 0) & (key < BLOCK), NEG_INF, 0.0)
    s = jnp.stack([_dot(kp[(h // 4, h % 2)][BLOCK * b:BLOCK * (b + 2)], qt[h // 2][:, BLOCK * b:BLOCK * (b + 1)])
                   for b in range(nb) for h in range(HEADS)]) * scale - alibi_ref[...]
    s = jnp.concatenate([s[:HEADS] + first, s[HEADS:]], axis=0) if nb > 1 else s + first
    sink = jnp.stack([jnp.full((1, 1), sink_ref[h], F32) for _ in range(nb) for h in range(HEADS)])
    m = jnp.maximum(jnp.max(s, axis=1, keepdims=True), sink)
    e = jnp.exp(s - m)
    es = jnp.exp(sink - m)
    inv = 1.0 / (jnp.sum(e, axis=1, keepdims=True) + es)
    return e * inv, es * inv, dict(half1=half1, kn=kn, kp=kp, v_all=v_all, qhat=qhat, qr=qr, qn=qn)


def _swa_fwd(proj, lw):
    T = proj.shape[0]
    tm = min(TM_SWA, T)
    nb = tm // BLOCK

    def body(q_ref, k_ref, v_ref, pk_ref, pv_ref, qw_ref, kw_ref, alibi_ref, sink_ref, o_ref):
        p, _, c = _swa_probs(pl.program_id(0), nb, q_ref, k_ref, v_ref, pk_ref, pv_ref, qw_ref, kw_ref, alibi_ref,
                             sink_ref)
        p = p.astype(MXU_DTYPE)
        rows1 = lax.broadcasted_iota(jnp.int32, (LANES, 1), 0) >= 64
        vpt = _swa_kv_variants_t(c["v_all"].T, rows1)
        for j in range(4):
            g = j // 2
            o_t = [_dot(vpt[(g, 0)][:, BLOCK * b:BLOCK * (b + 2)], p[HEADS * b + 2 * j])
                   + _dot(vpt[(g, 1)][:, BLOCK * b:BLOCK * (b + 2)], p[HEADS * b + 2 * j + 1]) for b in range(nb)]
            o_t = jnp.concatenate(o_t, axis=1) if nb > 1 else o_t[0]
            o_ref[:, LANES * j:LANES * (j + 1)] = o_t.T

    prev = lambda cb: pl.BlockSpec((BLOCK, LANES), lambda i: (jnp.maximum(i * nb - 1, 0), cb))
    return pl.pallas_call(
        body, name="swa_fwd", grid=(T // tm,),
        in_specs=[pl.BlockSpec((tm, 512), lambda i: (i, CB_SQ)), pl.BlockSpec((tm, LANES), lambda i: (i, CB_SK)),
                  pl.BlockSpec((tm, LANES), lambda i: (i, CB_SV)), prev(CB_SK), prev(CB_SV),
                  pl.BlockSpec((1, LANES), lambda i: (0, 0)), pl.BlockSpec((1, LANES), lambda i: (0, 0)),
                  pl.BlockSpec((nb * HEADS, 2 * BLOCK, BLOCK), lambda i: (0, 0, 0)),
                  pl.BlockSpec(memory_space=pltpu.SMEM)],
        out_specs=pl.BlockSpec((tm, 512), lambda i: (i, 0)),
        out_shape=_sds((T, GROUP_WIDTH), F32),
        compiler_params=_cp(("parallel",), 40))(
            proj, proj, proj, proj, proj, lw["sqn"], lw["skn"], jnp.tile(_swa_alibi(), (nb, 1, 1)), lw["sinks"])


def _shift_down(u, prev, n, row):
    tm = u.shape[0]
    out = pltpu.roll(u, n, 0)
    row8 = lax.broadcasted_iota(jnp.int32, prev.shape, 0)
    for t in range(n):
        src = jnp.sum(jnp.where(row8 == 8 - n + t, prev, 0.0), axis=0, keepdims=True)
        out = jnp.where(row == t, src, out)
    return out


def _shift_up(u, nxt, n, row):
    tm = u.shape[0]
    out = pltpu.roll(u, tm - n, 0)
    row8 = lax.broadcasted_iota(jnp.int32, nxt.shape, 0)
    for t in range(n):
        src = jnp.sum(jnp.where(row8 == t, nxt, 0.0), axis=0, keepdims=True)
        out = jnp.where(row == tm - n + t, src, out)
    return out


def _mix_fwd(proj, o_mla, o_swa, conv_w):
    T = proj.shape[0]
    tm = min(TM_ROW, T)

    def body(gm_ref, ch_ref, cb_ref, cc_ref, gc_ref, gs_ref, pch_ref, pcc_ref, om_ref, os_ref, w_ref, y_ref):
        i = pl.program_id(0)
        row = lax.broadcasted_iota(jnp.int32, (tm, GROUP_WIDTH), 0)
        u = cc_ref[...] * ch_ref[...]
        u_prev = jnp.where(i > 0, pcc_ref[...] * pch_ref[...], 0.0)
        z = (w_ref[0:1, :] * _shift_down(u, u_prev, 2, row) + w_ref[1:2, :] * _shift_down(u, u_prev, 1, row)
             + w_ref[2:3, :] * u)
        gm, gc, gs = gm_ref[...], gc_ref[...], gs_ref[...]
        y_ref[:, 0:512] = (om_ref[...] * (gm * _sigmoid(gm))).astype(MXU_DTYPE)
        y_ref[:, 512:1024] = (cb_ref[...] * z * (gc * _sigmoid(gc))).astype(MXU_DTYPE)
        y_ref[:, 1024:1536] = (os_ref[...] * (gs * _sigmoid(gs))).astype(MXU_DTYPE)

    blk = lambda cb: pl.BlockSpec((tm, 512), lambda i: (i, cb))
    prev = lambda cb: pl.BlockSpec((8, 512), lambda i: (jnp.maximum(i * (tm // 8) - 1, 0), cb))
    tile = pl.BlockSpec((tm, 512), lambda i: (i, 0))
    return pl.pallas_call(
        body, name="mix_fwd", grid=(T // tm,),
        in_specs=[blk(CB_GMLA), blk(CB_CH), blk(CB_CB), blk(CB_CC), blk(CB_GCONV), blk(CB_GSWA),
                  prev(CB_CH), prev(CB_CC), tile, tile, pl.BlockSpec((8, 512), lambda i: (0, 0))],
        out_specs=pl.BlockSpec((tm, D_MIX), lambda i: (i, 0)),
        out_shape=_sds((T, D_MIX), MXU_DTYPE),
        compiler_params=_cp(("parallel",), 32))(
            proj, proj, proj, proj, proj, proj, proj, proj, o_mla, o_swa, conv_w)


def _outproj_loss(ycat, wo, x, target):
    T, D = x.shape
    K = ycat.shape[1]
    tm = min(TM_PROJ, T)
    nt = T // tm

    def body(y_ref, w_ref, x_ref, t_ref, g_ref, loss_ref, acc_ref):
        i = pl.program_id(0)

        @pl.when(i == 0)
        def _():
            acc_ref[...] = jnp.zeros_like(acc_ref)

        err = _dot(y_ref[...], w_ref[...]) + x_ref[...] - t_ref[...]
        g_ref[...] = err * (1.0 / D)
        acc_ref[...] += _fold_rows8(err * err)

        @pl.when(i == nt - 1)
        def _():
            tot = jnp.sum(jnp.sum(acc_ref[...], axis=1, keepdims=True), axis=0, keepdims=True)
            loss_ref[...] = jnp.broadcast_to(tot * (0.5 / D), (8, LANES))

    tile = pl.BlockSpec((tm, D), lambda i: (i, 0))
    return pl.pallas_call(
        body, name="outproj_loss", grid=(nt,),
        in_specs=[pl.BlockSpec((tm, K), lambda i: (i, 0)), pl.BlockSpec((K, D), lambda i: (0, 0)), tile, tile],
        out_specs=[tile, pl.BlockSpec((8, LANES), lambda i: (0, 0))],
        out_shape=[_sds((T, D), F32), _sds((8, LANES), F32)],
        scratch_shapes=[pltpu.VMEM((8, D), F32)],
        compiler_params=_cp(("arbitrary",), 48))(ycat, wo, x, target)


def _outproj_bwd(g, ycat, wot):
    T, D = g.shape
    K = ycat.shape[1]
    tm = min(512, T)
    nt = T // tm

    def body(g_ref, y_ref, wt_ref, dy_ref, dw_ref, acc_ref):
        i = pl.program_id(0)

        @pl.when(i == 0)
        def _():
            acc_ref[...] = jnp.zeros_like(acc_ref)

        gb = g_ref[...].astype(MXU_DTYPE)
        dy_ref[...] = _dot(gb, wt_ref[...])
        acc_ref[...] += _dot_tn(y_ref[...], gb)

        @pl.when(i == nt - 1)
        def _():
            dw_ref[...] = acc_ref[...].astype(WIRE_DTYPE)

    return pl.pallas_call(
        body, name="outproj_bwd", grid=(nt,),
        in_specs=[pl.BlockSpec((tm, D), lambda i: (i, 0)), pl.BlockSpec((tm, K), lambda i: (i, 0)),
                  pl.BlockSpec((D, K), lambda i: (0, 0))],
        out_specs=[pl.BlockSpec((tm, K), lambda i: (i, 0)), pl.BlockSpec((K, D), lambda i: (0, 0))],
        out_shape=[_sds((T, K), F32), _sds((K, D), WIRE_DTYPE)],
        scratch_shapes=[pltpu.VMEM((K, D), F32)],
        compiler_params=_cp(("arbitrary",), 48))(g, ycat, wot)


def _mix_bwd(dycat, proj, o_mla, o_swa, conv_w):
    T = proj.shape[0]
    tm = min(TM_ROW, T)
    nt = T // tm

    def body(dym_ref, dyc_ref, dys_ref, gm_ref, ch_ref, cb_ref, cc_ref, gc_ref, gs_ref, pch_ref, pcc_ref,
             ndy_ref, ncb_ref, ngc_ref, om_ref, os_ref, w_ref,
             d1_ref, dom_ref, dos_ref, dw_ref):
        i = pl.program_id(0)

        @pl.when(i == 0)
        def _():
            dw_ref[...] = jnp.zeros_like(dw_ref)

        row = lax.broadcasted_iota(jnp.int32, (tm, GROUP_WIDTH), 0)

        def gate(g):
            sg = _sigmoid(g)
            return g * sg, sg * (1.0 + g * (1.0 - sg))

        gm = gm_ref[...]
        silu, dsilu = gate(gm)
        dym = dym_ref[...]
        dom_ref[...] = dym * silu
        d1_ref[:, 0:512] = (dym * om_ref[...] * dsilu).astype(MXU_DTYPE)

        gs = gs_ref[...]
        silu, dsilu = gate(gs)
        dys = dys_ref[...]
        dos_ref[...] = dys * silu
        d1_ref[:, 2560:3072] = (dys * os_ref[...] * dsilu).astype(MXU_DTYPE)

        ch, cb, cc, gc, dyc = ch_ref[...], cb_ref[...], cc_ref[...], gc_ref[...], dyc_ref[...]
        w0, w1, w2 = w_ref[0:1, :], w_ref[1:2, :], w_ref[2:3, :]
        u = cc * ch
        u_prev = jnp.where(i > 0, pcc_ref[...] * pch_ref[...], 0.0)
        u1 = _shift_down(u, u_prev, 1, row)
        u2 = _shift_down(u, u_prev, 2, row)
        z = w0 * u2 + w1 * u1 + w2 * u
        silu, dsilu = gate(gc)
        dz = dyc * cb * silu
        ngc = ngc_ref[...]
        dz_next = jnp.where(i < nt - 1, ndy_ref[...] * ncb_ref[...] * (ngc * _sigmoid(ngc)), 0.0)
        du = w2 * dz + w1 * _shift_up(dz, dz_next, 1, row) + w0 * _shift_up(dz, dz_next, 2, row)
        d1_ref[:, 512:1024] = (du * cc).astype(MXU_DTYPE)
        d1_ref[:, 1024:1536] = (dyc * z * silu).astype(MXU_DTYPE)
        d1_ref[:, 1536:2048] = (du * ch).astype(MXU_DTYPE)
        d1_ref[:, 2048:2560] = (dyc * cb * z * dsilu).astype(MXU_DTYPE)
        row8 = lax.broadcasted_iota(jnp.int32, (8, GROUP_WIDTH), 0)
        dw = jnp.zeros((8, GROUP_WIDTH), F32)
        for t, shifted in enumerate((u2, u1, u)):
            dw = dw + jnp.where(row8 == t, jnp.sum(dz * shifted, axis=0, keepdims=True), 0.0)
        dw_ref[...] += dw

    blk = lambda cb: pl.BlockSpec((tm, 512), lambda i: (i, cb))
    prev = lambda cb: pl.BlockSpec((8, 512), lambda i: (jnp.maximum(i * (tm // 8) - 1, 0), cb))
    nxt = lambda cb: pl.BlockSpec((8, 512), lambda i: (jnp.minimum((i + 1) * (tm // 8), T // 8 - 1), cb))
    tile = pl.BlockSpec((tm, 512), lambda i: (i, 0))
    return pl.pallas_call(
        body, name="mix_bwd", grid=(nt,),
        in_specs=[blk(0), blk(1), blk(2), blk(CB_GMLA), blk(CB_CH), blk(CB_CB), blk(CB_CC), blk(CB_GCONV),
                  blk(CB_GSWA), prev(CB_CH), prev(CB_CC), nxt(1), nxt(CB_CB), nxt(CB_GCONV), tile, tile,
                  pl.BlockSpec((8, 512), lambda i: (0, 0))],
        out_specs=[pl.BlockSpec((tm, 3072), lambda i: (i, DPB_MIX)), tile, tile,
                   pl.BlockSpec((8, 512), lambda i: (0, 0))],
        out_shape=[_sds((T, NP), MXU_DTYPE), _sds((T, 512), F32), _sds((T, 512), F32), _sds((8, 512), F32)],
        compiler_params=_cp(("arbitrary",), 48))(
            dycat, dycat, dycat, proj, proj, proj, proj, proj, proj, proj, proj, dycat, proj, proj,
            o_mla, o_swa, conv_w)


def _swa_bwd(proj, o_swa, do_swa, lw, dproj):
    T = proj.shape[0]
    tm = min(TM_SWA, T)
    nb = tm // BLOCK
    scale = SWA_HEAD_DIM ** -0.5

    def body(q_ref, k_ref, v_ref, pk_ref, pv_ref, o_ref, do_ref, qw_ref, kw_ref, alibi_ref, sink_ref, dproj_in,
             dq_ref, dk_ref, dv_ref, dqw_ref, dsink_ref):
        i = pl.program_id(0)

        @pl.when(i == 0)
        def _():
            dk_ref[...] = jnp.zeros_like(dk_ref)
            dv_ref[...] = jnp.zeros_like(dv_ref)
            dqw_ref[...] = jnp.zeros_like(dqw_ref)
            dsink_ref[...] = jnp.zeros_like(dsink_ref)

        p, p_sink, c = _swa_probs(i, nb, q_ref, k_ref, v_ref, pk_ref, pv_ref, qw_ref, kw_ref, alibi_ref, sink_ref)
        half1, kp, qn, qhat, qr = c["half1"], c["kp"], c["qn"], c["qhat"], c["qr"]
        rows1 = lax.broadcasted_iota(jnp.int32, (LANES, 1), 0) >= 64
        kpt = _swa_kv_variants_t(c["kn"].T, rows1)
        vp = _swa_kv_variants(c["v_all"], half1)
        qw = qw_ref[...]
        rows = [slice(BLOCK * b, BLOCK * (b + 1)) for b in range(nb)]
        keys = [slice(BLOCK * b, BLOCK * (b + 2)) for b in range(nb)]
        dob, dot_b, dd0, dd1 = [], [], [], []
        for j in range(4):
            cols = slice(LANES * j, LANES * (j + 1))
            do = do_ref[:, cols]
            do_t = do.T
            prod_t = do_t * o_ref[:, cols].T
            dob.append(do.astype(MXU_DTYPE))
            dot_b.append(do_t.astype(MXU_DTYPE))
            dd0.append(jnp.sum(jnp.where(rows1, 0.0, prod_t), axis=0, keepdims=True))
            dd1.append(jnp.sum(jnp.where(rows1, prod_t, 0.0), axis=0, keepdims=True))
        dd = jnp.stack([(dd1 if h % 2 else dd0)[h // 2][:, rows[b]] for b in range(nb) for h in range(HEADS)])
        dp = jnp.stack([_dot(vp[(h // 4, h % 2)][keys[b]], dot_b[h // 2][:, rows[b]])
                        for b in range(nb) for h in range(HEADS)])
        ds = (p * (dp - dd) * scale).astype(MXU_DTYPE)
        dsink = -jnp.sum(p_sink * dd, axis=2, keepdims=True)
        pb = p.astype(MXU_DTYPE)

        dqw = jnp.zeros((1, LANES), F32)
        for j in range(4):
            g = j // 2
            dqn_t = [_dot(kpt[(g, 0)][:, keys[b]], ds[HEADS * b + 2 * j])
                     + _dot(kpt[(g, 1)][:, keys[b]], ds[HEADS * b + 2 * j + 1]) for b in range(nb)]
            dqn = (jnp.concatenate(dqn_t, axis=1) if nb > 1 else dqn_t[0]).T
            dqw = dqw + jnp.sum(dqn * qhat[j], axis=0, keepdims=True)
            dq_ref[:, LANES * j:LANES * (j + 1)] = _rms_halves_bwd(dqn, qhat[j], qr[j], qw, half1).astype(MXU_DTYPE)
        dqw_ref[...] += _row0(dqw + pltpu.roll(dqw, 64, 1))

        dk_tot = jnp.zeros((tm + BLOCK, LANES), F32)
        dv_tot = jnp.zeros((tm + BLOCK, LANES), F32)
        for b in range(nb):
            dk_b = jnp.zeros((2 * BLOCK, LANES), F32)
            dv_b = jnp.zeros((2 * BLOCK, LANES), F32)
            for g in range(2):
                for r in range(2):
                    own = half1 if r else jnp.logical_not(half1)
                    ha, hb = HEADS * b + 4 * g + r, HEADS * b + 4 * g + 2 + r
                    qa, qb = qn[2 * g][rows[b]], qn[2 * g + 1][rows[b]]
                    da, db = dob[2 * g][rows[b]], dob[2 * g + 1][rows[b]]
                    dkp = jnp.where(own, _dot(ds[ha], qa) + _dot(ds[hb], qb), 0.0)
                    dvp = jnp.where(own, _dot(pb[ha], da) + _dot(pb[hb], db), 0.0)
                    if g != r:
                        dkp = pltpu.roll(dkp, 64, 1)
                        dvp = pltpu.roll(dvp, 64, 1)
                    dk_b = dk_b + dkp
                    dv_b = dv_b + dvp
            pad = lambda x: jnp.concatenate(
                [z for z in (jnp.zeros((BLOCK * b, LANES), F32), x, jnp.zeros((BLOCK * (nb - 1 - b), LANES), F32))
                 if z.shape[0]], axis=0)
            dk_tot = dk_tot + pad(dk_b)
            dv_tot = dv_tot + pad(dv_b)
        dst = pl.ds(pl.multiple_of(i * tm, BLOCK), tm + BLOCK)
        dk_ref[dst, :] += dk_tot
        dv_ref[dst, :] += dv_tot

        row8 = lax.broadcasted_iota(jnp.int32, (8, LANES), 0)
        dsink_tile = jnp.zeros((8, LANES), F32)
        for b in range(nb):
            for h in range(HEADS):
                dsink_tile = dsink_tile + jnp.where(row8 == h, jnp.broadcast_to(dsink[HEADS * b + h], (8, LANES)), 0.0)
        dsink_ref[...] += dsink_tile

    prev = lambda cb: pl.BlockSpec((BLOCK, LANES), lambda i: (jnp.maximum(i * nb - 1, 0), cb))
    tile = pl.BlockSpec((tm, 512), lambda i: (i, 0))
    small = pl.BlockSpec((8, LANES), lambda i: (0, 0))
    acc = pl.BlockSpec((T + BLOCK, LANES), lambda i: (0, 0))
    return pl.pallas_call(
        body, name="swa_bwd", grid=(T // tm,),
        in_specs=[pl.BlockSpec((tm, 512), lambda i: (i, CB_SQ)), pl.BlockSpec((tm, LANES), lambda i: (i, CB_SK)),
                  pl.BlockSpec((tm, LANES), lambda i: (i, CB_SV)), prev(CB_SK), prev(CB_SV), tile, tile,
                  pl.BlockSpec((1, LANES), lambda i: (0, 0)), pl.BlockSpec((1, LANES), lambda i: (0, 0)),
                  pl.BlockSpec((nb * HEADS, 2 * BLOCK, BLOCK), lambda i: (0, 0, 0)),
                  pl.BlockSpec(memory_space=pltpu.SMEM), pl.BlockSpec(memory_space=pl.ANY)],
        out_specs=[pl.BlockSpec((tm, 512), lambda i: (i, DPB_SQ)), acc, acc, small, small],
        out_shape=[_sds((T, NP), MXU_DTYPE), _sds((T + BLOCK, LANES), F32), _sds((T + BLOCK, LANES), F32),
                   _sds((8, LANES), F32), _sds((8, LANES), F32)],
        input_output_aliases={11: 0},
        compiler_params=_cp(("arbitrary",), 48))(
            proj, proj, proj, proj, proj, o_swa, do_swa, lw["sqn"], lw["skn"], jnp.tile(_swa_alibi(), (nb, 1, 1)),
            lw["sinks"], dproj)


def _swa_kv_bwd(proj, dkn, dv, lw, dproj):
    T = proj.shape[0]
    tm = min(TM_SWA, T)
    dkn, dv = dkn[BLOCK:], dv[BLOCK:]

    def body(k_ref, dkn_ref, dv_ref, kw_ref, dproj_in, d_ref, dkw_ref):
        i = pl.program_id(0)

        @pl.when(i == 0)
        def _():
            dkw_ref[...] = jnp.zeros_like(dkw_ref)

        half1 = lax.broadcasted_iota(jnp.int32, (1, LANES), 1) >= 64
        khat, kr = _rms_halves(k_ref[...], half1)
        dkn_t = dkn_ref[...]
        dkw = jnp.sum(dkn_t * khat, axis=0, keepdims=True)
        dkw_ref[...] += _row0(dkw + pltpu.roll(dkw, 64, 1))
        d_ref[:, 0:LANES] = _rms_halves_bwd(dkn_t, khat, kr, kw_ref[...], half1).astype(MXU_DTYPE)
        d_ref[:, LANES:2 * LANES] = dv_ref[...].astype(MXU_DTYPE)

    return pl.pallas_call(
        body, name="swa_kv_bwd", grid=(T // tm,),
        in_specs=[pl.BlockSpec((tm, LANES), lambda i: (i, CB_SK)), pl.BlockSpec((tm, LANES), lambda i: (i, 0)),
                  pl.BlockSpec((tm, LANES), lambda i: (i, 0)), pl.BlockSpec((1, LANES), lambda i: (0, 0)),
                  pl.BlockSpec(memory_space=pl.ANY)],
        out_specs=[pl.BlockSpec((tm, 2 * LANES), lambda i: (i, DPB_SKV)), pl.BlockSpec((8, LANES), lambda i: (0, 0))],
        out_shape=[_sds((T, NP), MXU_DTYPE), _sds((8, LANES), F32)],
        input_output_aliases={4: 0},
        compiler_params=_cp(("arbitrary",), 32))(proj, dkn, dv, lw["skn"], dproj)


def _mla_attn_bwd(q, k, kt, vt, o, do, lse):
    T = q.shape[1]
    tk = min(TK, T // 2)
    tq = 2 * tk

    def body(q_ref, k_ref, kt_ref, vt_ref, o_ref, do_ref, lse_ref, dq_ref, dk_ref, dv_ref, dq_s, lse_s, dd_s,
             s_a, s_b, p_a, p_b):
        h = pl.program_id(0)
        i = pl.program_id(1)

        @pl.when(i == 0)
        def _():
            dk_ref[...] = jnp.zeros_like(dk_ref)
            dv_ref[...] = jnp.zeros_like(dv_ref)

        qry = lax.broadcasted_iota(jnp.int32, (tq, tk), 0)
        key = lax.broadcasted_iota(jnp.int32, (tq, tk), 1)
        own = (lax.broadcasted_iota(jnp.int32, (1, LANES), 1) // 64) == (h % 2)
        do_own = jnp.where(own, do_ref[...], 0.0)
        dob = do_own.astype(MXU_DTYPE)
        dob_t = do_own.T.astype(MXU_DTYPE)
        qh = q_ref[0]
        qh_t = qh.astype(F32).T.astype(MXU_DTYPE)
        dd_col = jnp.sum(do_own * o_ref[...], axis=-1, keepdims=True)
        lse_col = jnp.broadcast_to(lse_ref[0], (LANES, tq)).T
        for c in range(tk // LANES):
            lse_s[:, LANES * c:LANES * (c + 1)] = lse_col
            dd_s[:, LANES * c:LANES * (c + 1)] = jnp.broadcast_to(dd_col, (tq, LANES))
        dq_s[...] = jnp.zeros_like(dq_s)

        def scores(kj, s_buf, p_buf):
            s_buf[...] = _dot(qh, kt_ref[0, kj])
            p_buf[...] = _dot(dob, vt_ref[0, kj])

        def consume(kj, s_buf, p_buf, diag):
            rows = pl.ds(pl.multiple_of(kj * tk, tk), tk)
            s = s_buf[...]
            if diag is not None:
                s = jnp.where(key + diag * tk <= qry, s, NEG_INF)
            p = jnp.exp2(s - lse_s[...])
            ds = (p * (p_buf[...] - dd_s[...])).astype(MXU_DTYPE)
            dq_s[...] += _dot(ds, k_ref[0, rows, :])
            dk_ref[0, kj] += _dot(qh_t, ds)
            dv_ref[0, kj] += _dot(dob_t, p.astype(MXU_DTYPE))

        scores(0, s_a, p_a)

        def pair(kj):
            scores(kj + 1, s_b, p_b)
            consume(kj, s_a, p_a, None)
            scores(kj + 2, s_a, p_a)
            consume(kj + 1, s_b, p_b, None)

        def octet(ko, carry):
            for t in range(4):
                pair(8 * ko + 2 * t)
            return carry

        lax.fori_loop(0, i // 4, octet, 0)

        @pl.when(i % 4 >= 2)
        def _():
            pair(8 * (i // 4))
            pair(8 * (i // 4) + 2)

        @pl.when(i % 2 == 1)
        def _():
            pair(2 * i - 2)

        kl = 2 * i + 1
        s_b[tk:, :] = _dot(qh[tk:], kt_ref[0, kl])
        p_b[tk:, :] = _dot(dob[tk:], vt_ref[0, kl])
        consume(2 * i, s_a, p_a, 0)
        s = jnp.where(key[tk:] + tk <= qry[tk:], s_b[tk:, :], NEG_INF)
        p = jnp.exp2(s - lse_s[tk:, :])
        ds = (p * (p_b[tk:, :] - dd_s[tk:, :])).astype(MXU_DTYPE)
        dq_s[tk:, :] += _dot(ds, k_ref[0, pl.ds(pl.multiple_of(kl * tk, tk), tk), :])
        dk_ref[0, kl] += _dot(qh_t[:, tk:], ds)
        dv_ref[0, kl] += _dot(dob_t[:, tk:], p.astype(MXU_DTYPE))
        dq_ref[0] = dq_s[...]

    res = pl.BlockSpec((1, T, LANES), lambda h, i: (h, 0, 0))
    res_t = pl.BlockSpec((1, T // tk, LANES, tk), lambda h, i: (h, 0, 0, 0))
    buf = pltpu.VMEM((tq, tk), F32)
    acc_t = _sds((HEADS, T // tk, LANES, tk), F32)
    return pl.pallas_call(
        body, name="mla_attn_bwd", grid=(HEADS, T // tq),
        in_specs=[pl.BlockSpec((1, tq, LANES), lambda h, i: (h, i, 0)), res, res_t, res_t,
                  pl.BlockSpec((tq, LANES), lambda h, i: (i, h // 2)),
                  pl.BlockSpec((tq, LANES), lambda h, i: (i, h // 2)),
                  pl.BlockSpec((1, 1, tq), lambda h, i: (h, 0, i))],
        out_specs=[pl.BlockSpec((1, tq, LANES), lambda h, i: (h, i, 0)), res_t, res_t],
        out_shape=[_sds((HEADS, T, LANES), F32), acc_t, acc_t],
        scratch_shapes=[pltpu.VMEM((tq, LANES), F32), buf, buf, buf, buf, buf, buf],
        compiler_params=_cp(("parallel", "arbitrary"), 48))(q, k, kt, vt, o, do, lse)


def _mla_prep_bwd(proj, dq, dk, dv, lw, rope, dproj):
    T = proj.shape[0]
    tm = min(TK, T // 2)

    def body(ql_ref, kvl_ref, kr_ref, dq_ref, dk_ref, dv_ref, qa_ref, kva_ref, wq_ref, wk_ref, wv_ref,
             wqt_ref, wkt_ref, wvt_ref, qn_ref, kn_ref, c_ref, s1_ref, s2_ref, dproj_in,
             d_ref, dwq_ref, dwk_ref, dwv_ref, dqa_ref, dkva_ref, dqn_ref, dkn_ref):
        i = pl.program_id(0)

        @pl.when(i == 0)
        def _():
            for ref in (dwq_ref, dwk_ref, dwv_ref, dqa_ref, dkva_ref, dqn_ref, dkn_ref):
                ref[...] = jnp.zeros_like(ref)

        c, s1, s2 = c_ref[...], s1_ref[...], s2_ref[...]
        lane = lax.broadcasted_iota(jnp.int32, (1, LANES), 1)
        qlhat, qlr = _rms(ql_ref[...], MLA_Q_LORA)
        qn = (qlhat * qa_ref[...]).astype(MXU_DTYPE)
        kvhat, kvr = _rms(kvl_ref[...], MLA_KV_LORA)
        kvn = (kvhat * kva_ref[...]).astype(MXU_DTYPE)
        kr = kr_ref[...]
        x3, r3 = _rms(jnp.stack([_dot(qn, wq_ref[h]) for h in range(HEADS)]), MLA_QK)
        dy3 = _rope_bwd(dq_ref[...] * MLA_SCALE, c, s1, s2)
        dqw = jnp.sum(jnp.sum(dy3 * x3, axis=0), axis=0, keepdims=True)
        dx3 = _rms_bwd(dy3, x3, r3, qn_ref[...], MLA_QK).astype(MXU_DTYPE)
        dqnl = jnp.zeros((tm, MLA_Q_LORA), F32)
        for h in range(HEADS):
            dwq_ref[h] += _dot_tn(qn, dx3[h])
            dqnl = dqnl + _dot(dx3[h], wqt_ref[h])

        x3, r3 = _rms(jnp.stack([_dot(kvn, wk_ref[h]) for h in range(HEADS)]) + kr, MLA_QK)
        dy3 = _rope_bwd(jnp.stack([dk_ref[h, 0].T for h in range(HEADS)]) * LN2, c, s1, s2)
        dkw = jnp.sum(jnp.sum(dy3 * x3, axis=0), axis=0, keepdims=True)
        dxf3 = _rms_bwd(dy3, x3, r3, kn_ref[...], MLA_QK)
        dkr = jnp.sum(dxf3, axis=0)
        dx3 = dxf3.astype(MXU_DTYPE)
        dkvn = jnp.zeros((tm, MLA_KV_LORA), F32)
        for h in range(HEADS):
            dwk_ref[h] += _dot_tn(kvn, dx3[h])
            dkvn = dkvn + _dot(dx3[h], wkt_ref[h])
        dvc = jnp.concatenate([(dv_ref[2 * j, 0] + dv_ref[2 * j + 1, 0]).T for j in range(4)],
                              axis=1).astype(MXU_DTYPE)
        dwv_ref[...] += _dot_tn(kvn, dvc)
        dkvn = dkvn + _dot(dvc, wvt_ref[...])
        dqa_ref[...] += _row0(jnp.sum(dqnl * qlhat, axis=0, keepdims=True))
        dkva_ref[...] += _row0(jnp.sum(dkvn * kvhat, axis=0, keepdims=True))
        dqn_ref[...] += _row0(dqw)
        dkn_ref[...] += _row0(dkw)
        d_ref[:, 0:256] = _rms_bwd(dqnl, qlhat, qlr, qa_ref[...], MLA_Q_LORA).astype(MXU_DTYPE)
        d_ref[:, 256:384] = _rms_bwd(dkvn, kvhat, kvr, kva_ref[...], MLA_KV_LORA).astype(MXU_DTYPE)
        d_ref[:, 384:512] = jnp.where((lane >= 64) & (lane < 96), dkr, 0.0).astype(MXU_DTYPE)

    full = lambda shape: pl.BlockSpec(shape, lambda i: (0,) * len(shape))
    hd = pl.BlockSpec((HEADS, tm, LANES), lambda i: (0, i, 0))
    hdt = pl.BlockSpec((HEADS, 1, LANES, tm), lambda i: (0, i, 0, 0))
    tab = pl.BlockSpec((tm, LANES), lambda i: (i, 0))
    return pl.pallas_call(
        body, name="mla_prep_bwd", grid=(T // tm,),
        in_specs=[pl.BlockSpec((tm, 256), lambda i: (i, CB_QLAT)), pl.BlockSpec((tm, LANES), lambda i: (i, CB_KVLAT)),
                  pl.BlockSpec((tm, LANES), lambda i: (i, CB_KROPE)), hd, hdt, hdt,
                  full((1, 256)), full((1, LANES)), full((HEADS, 256, LANES)), full((HEADS, LANES, LANES)),
                  full((LANES, 512)), full((HEADS, LANES, 256)), full((HEADS, LANES, LANES)), full((512, LANES)),
                  full((1, LANES)), full((1, LANES)), tab, tab, tab, pl.BlockSpec(memory_space=pl.ANY)],
        out_specs=[pl.BlockSpec((tm, 512), lambda i: (i, DPB_MLA)), full((HEADS, 256, LANES)),
                   full((HEADS, LANES, LANES)), full((LANES, 512)), full((8, 256)), full((8, LANES)),
                   full((8, LANES)), full((8, LANES))],
        out_shape=[_sds((T, NP), MXU_DTYPE), _sds((HEADS, 256, LANES), F32), _sds((HEADS, LANES, LANES), F32),
                   _sds((LANES, 512), F32), _sds((8, 256), F32), _sds((8, LANES), F32), _sds((8, LANES), F32),
                   _sds((8, LANES), F32)],
        input_output_aliases={19: 0},
        compiler_params=_cp(("arbitrary",), 48))(
            proj, proj, proj, dq, dk, dv, lw["qa"], lw["kva"], lw["wq"], lw["wk"], lw["wv"],
            lw["wqt"], lw["wkt"], lw["wvt"], lw["qn"], lw["kn"], rope[0], rope[1], rope[2], dproj)


def _inproj_bwd_dx(dproj, wpt, x, g_in, ng):
    T, D = x.shape
    tm = min(TM_PROJ, T)

    def body(dp_ref, wt_ref, x_ref, g_ref, w_ref, dx_ref, dw_ref):
        i = pl.program_id(0)

        @pl.when(i == 0)
        def _():
            dw_ref[...] = jnp.zeros_like(dw_ref)

        dh = _dot(dp_ref[...], wt_ref[...])
        xhat, r = _rms(x_ref[...], D)
        dw_ref[...] += _row0(jnp.sum(dh * xhat, axis=0, keepdims=True))
        dx_ref[...] = g_ref[...] + _rms_bwd(dh, xhat, r, w_ref[...], D)

    tile = pl.BlockSpec((tm, D), lambda i: (i, 0))
    return pl.pallas_call(
        body, name="inproj_bwd_dx", grid=(T // tm,),
        in_specs=[pl.BlockSpec((tm, NP), lambda i: (i, 0)), pl.BlockSpec((NP, D), lambda i: (0, 0)), tile, tile,
                  pl.BlockSpec((1, D), lambda i: (0, 0))],
        out_specs=[tile, pl.BlockSpec((8, D), lambda i: (0, 0))],
        out_shape=[_sds((T, D), F32), _sds((8, D), F32)],
        compiler_params=_cp(("arbitrary",), 48))(dproj, wpt, x, g_in, ng)


def _rope_tables(T, token=0.0):
    half = MLA_ROPE // 2
    inv_freq = jnp.power(jnp.float32(ROPE_THETA), -jnp.arange(half, dtype=F32) / half)
    z = lambda n: jnp.zeros((n,), F32)
    freq = jnp.concatenate([z(MLA_NOPE), inv_freq, inv_freq, z(32)])
    first = jnp.concatenate([z(64), jnp.ones((16,), F32), z(48)])
    second = jnp.concatenate([z(80), jnp.ones((16,), F32), z(32)])
    ang = (jnp.arange(T, dtype=F32) + token)[:, None] * freq[None, :]
    sin = jnp.sin(ang)
    return jnp.cos(ang), -sin * first[None, :], sin * second[None, :]


def _pad_lanes(v, n=LANES):
    v = v.reshape(1, -1)
    return jnp.pad(v, ((0, 0), (0, n - v.shape[1])))


def _pack_win_t(wt):
    z = lambda n: jnp.zeros((n, wt.shape[1]), wt.dtype)
    return jnp.concatenate([wt[416:2976], wt[3744:4256], wt[0:384], z(64), wt[384:416], z(32), wt[2976:3488],
                            wt[3488:3616], wt[3616:3744]], axis=0)


def _unpack_dwin(d):
    return jnp.concatenate([d[:, 3072:3456], d[:, 3520:3552], d[:, 0:2560], d[:, 3584:4096], d[:, 4096:4224],
                            d[:, 4224:4352], d[:, 2560:3072]], axis=1)


def _inproj_weights(l, norm_g, w_in_t):
    wpt = _pack_win_t(w_in_t)
    return dict(ng=norm_g[l].reshape(1, -1), wp=wpt.T, wpt=wpt)


def _mixer_weights(l, qa, wqb_full, kva, wkvb_full, qn, kn, conv_full, sqn, skn, sinks, w_out_full):
    wq = jnp.pad(wqb_full, ((0, 0), (0, 0), (0, LANES - MLA_QK)))
    wk = jnp.pad(wkvb_full[:, :, :MLA_NOPE], ((0, 0), (0, 0), (0, LANES - MLA_NOPE)))
    wv = jnp.transpose(wkvb_full[:, :, MLA_NOPE:], (1, 0, 2)).reshape(MLA_KV_LORA, GROUP_WIDTH)
    return dict(
        qa=qa[l].reshape(1, -1), kva=kva[l].reshape(1, -1),
        wq=wq, wk=wk, wv=wv, wqt=jnp.transpose(wq, (0, 2, 1)), wkt=jnp.transpose(wk, (0, 2, 1)), wvt=wv.T,
        qn=_pad_lanes(qn[l]), kn=_pad_lanes(kn[l]),
        conv=jnp.pad(conv_full, ((0, 5), (0, 0))),
        sqn=jnp.tile(sqn[l].reshape(1, -1), (1, 2)), skn=jnp.tile(skn[l].reshape(1, -1), (1, 2)),
        sinks=sinks[l], wo=w_out_full, wot=w_out_full.T)


def _layer_fwd(x, lw, rope, late_weights=None, target=None):
    proj, h = _inproj_fwd(x, lw["ng"], lw["wp"])
    if late_weights is not None:
        lw = dict(lw, **late_weights(proj))
    q, k, kt, vt = _mla_prep_fwd(proj, lw, rope)
    o_mla, lse = _mla_attn_fwd(q, k, vt)
    o_swa = _swa_fwd(proj, lw)
    ycat = _mix_fwd(proj, o_mla, o_swa, lw["conv"])
    if target is None:
        out = _mm_nn(ycat, lw["wo"], "outproj_fwd", residual=x)
    else:
        out = _outproj_loss(ycat, lw["wo"], x, target)
    return out, dict(x=x, proj=proj, h=h, q=q, k=k, kt=kt, vt=vt, o_mla=o_mla, lse=lse, o_swa=o_swa, ycat=ycat,
                     lw=lw)


def _layer_bwd(g, sv, lw, rope, on_big_grads=None):
    proj = sv["proj"]
    dycat, d_wo = _outproj_bwd(g, sv["ycat"], lw["wot"])
    dproj, do_mla, do_swa, d_conv = _mix_bwd(dycat, proj, sv["o_mla"], sv["o_swa"], lw["conv"])
    dproj, dkn_acc, dv_acc, d_sqn, d_sinks = _swa_bwd(proj, sv["o_swa"], do_swa, lw, dproj)
    dproj, d_skn = _swa_kv_bwd(proj, dkn_acc, dv_acc, lw, dproj)
    dq, dk, dv = _mla_attn_bwd(sv["q"], sv["k"], sv["kt"], sv["vt"], sv["o_mla"], do_mla, sv["lse"])
    dproj, d_wq, d_wk, d_wv, d_qa, d_kva, d_qn, d_kn = _mla_prep_bwd(proj, dq, dk, dv, lw, rope, dproj)
    grads = dict(
        w_out=d_wo, w_qb=d_wq[:, :, :MLA_QK],
        w_kvb=jnp.concatenate([d_wk[:, :, :MLA_NOPE],
                               jnp.transpose(d_wv.reshape(MLA_KV_LORA, HEADS, MLA_NOPE), (1, 0, 2))], axis=2))
    token = 0.0 if on_big_grads is None else on_big_grads("mixer", grads)
    d_wp = _mm_tn(sv["h"], dproj, "inproj_bwd_dw", WIRE_DTYPE, tn=NP // 2)
    grads["w_in"] = _unpack_dwin(d_wp)
    token = token if on_big_grads is None else token + on_big_grads("w_in", grads)
    dx, d_ng = _inproj_bwd_dx(dproj, lw["wpt"], sv["x"], g, lw["ng"] + token)
    grads.update(
        conv=d_conv[0:3], norm_g=d_ng[0], qa=d_qa[0], kva=d_kva[0], qn=d_qn[0, :MLA_QK], kn=d_kn[0, :MLA_QK],
        sqn=d_sqn[0, :SWA_HEAD_DIM], skn=d_skn[0, :SWA_HEAD_DIM], sinks=d_sinks[:, 0])
    return dx, grads


def _my_coords():
    return lax.axis_index("x"), lax.axis_index("y"), lax.axis_index("c")


def _peer(me, k):
    x, y, c = me
    return (1 - x if k & 4 else x, 1 - y if k & 2 else y, 1 - c if k & 1 else c)


def _lin(d):
    return 4 * d[0] + 2 * d[1] + d[2]


def _push_copies(ins, lands, send_sems, recv_sems, gather, incoming=False):
    me = _my_coords()
    my = _lin(me)
    copies = []
    for a in range(len(ins)):
        for k in range(1, N_DEV):
            peer = _peer(me, k)
            src = ins[a] if gather else ins[a].at[_lin(peer)]
            copies.append(pltpu.make_async_remote_copy(
                src_ref=src, dst_ref=lands[a].at[_lin(peer) if incoming else my],
                send_sem=send_sems.at[a * 7 + k - 1], recv_sem=recv_sems.at[a * 7 + k - 1],
                device_id=peer, device_id_type=pl.DeviceIdType.MESH))
    return copies


def _push_start(arrays, name, gather):
    n = len(arrays)
    land_shapes = [((N_DEV,) + a.shape) if gather else a.shape for a in arrays]

    def body(*refs):
        ins, lands = refs[:n], refs[n:2 * n]
        send_sems, recv_sems = refs[2 * n], refs[2 * n + 1]
        token = refs[-1]
        for cp in _push_copies(ins, lands, send_sems, recv_sems, gather):
            cp.start()
        token[...] = jnp.zeros_like(token)

    hbm = pl.BlockSpec(memory_space=pltpu.HBM)
    sem = pl.BlockSpec(memory_space=pltpu.SEMAPHORE)
    res = pl.pallas_call(
        body, name=name,
        out_shape=(pltpu.SemaphoreType.DMA((7 * n,)), pltpu.SemaphoreType.DMA((7 * n,)),
                   *[pltpu.HBM(a.shape, a.dtype) for a in arrays],
                   *[pltpu.HBM(s, a.dtype) for s, a in zip(land_shapes, arrays)],
                   _sds((8, LANES), F32)),
        in_specs=(hbm,) * (2 * n),
        out_specs=(sem, sem) + (hbm,) * (2 * n) + (pl.BlockSpec(memory_space=pltpu.VMEM),),
        input_output_aliases={i: 2 + i for i in range(2 * n)},
        compiler_params=pltpu.CompilerParams(has_side_effects=pltpu.SideEffectType.DATAFLOW_SIDE_EFFECTING),
    )(*[pltpu.with_memory_space_constraint(a, pltpu.HBM) for a in arrays],
      *[pltpu.with_memory_space_constraint(lax.empty(s, a.dtype), pltpu.HBM) for s, a in zip(land_shapes, arrays)])
    return dict(send=res[0], recv=res[1], src=res[2:2 + n], land=res[2 + n:2 + 2 * n], token=res[-1][0, 0],
                gather=gather)


def _push_wait(handle, after, name):
    n = len(handle["src"])
    gather = handle["gather"]

    def body(*refs):
        ins, lands = refs[:n], refs[n:2 * n]
        send_sems, recv_sems = refs[2 * n], refs[2 * n + 1]
        for cp in _push_copies(ins, lands, send_sems, recv_sems, gather):
            cp.wait_send()
        for cp in _push_copies(ins, lands, send_sems, recv_sems, gather, incoming=True):
            cp.wait_recv()

    hbm = pl.BlockSpec(memory_space=pltpu.HBM)
    sem = pl.BlockSpec(memory_space=pltpu.SEMAPHORE)
    res = pl.pallas_call(
        body, name=name,
        out_shape=tuple(pltpu.HBM(a.shape, a.dtype) for a in (*handle["src"], *handle["land"])),
        in_specs=(hbm,) * (2 * n) + (sem, sem, pl.BlockSpec(memory_space=pl.ANY)),
        out_specs=(hbm,) * (2 * n),
        input_output_aliases={i: i for i in range(2 * n)},
        compiler_params=pltpu.CompilerParams(has_side_effects=pltpu.SideEffectType.DATAFLOW_SIDE_EFFECTING),
    )(*handle["src"], *handle["land"], handle["send"], handle["recv"], after)
    return res[n:]


def _small_all_reduce(v):
    R = v.shape[0]

    def body(v_ref, o_ref, buf, send_sems, recv_sems):
        me = _my_coords()
        my = _lin(me)
        sends = []
        for k in range(1, N_DEV):
            cp = pltpu.make_async_remote_copy(
                src_ref=v_ref, dst_ref=buf.at[my], send_sem=send_sems.at[k - 1], recv_sem=recv_sems.at[k - 1],
                device_id=_peer(me, k), device_id_type=pl.DeviceIdType.MESH)
            cp.start()
            sends.append(cp)
        buf[my] = v_ref[...]
        for k in range(1, N_DEV):
            pltpu.make_async_remote_copy(
                src_ref=v_ref, dst_ref=buf.at[_lin(_peer(me, k))], send_sem=send_sems.at[k - 1],
                recv_sem=recv_sems.at[k - 1], device_id=_peer(me, k),
                device_id_type=pl.DeviceIdType.MESH).wait_recv()
        for cp in sends:
            cp.wait_send()
        tot = buf[0]
        for d in range(1, N_DEV):
            tot = tot + buf[d]
        o_ref[...] = tot

    vm = pl.BlockSpec(memory_space=pltpu.VMEM)
    return pl.pallas_call(
        body, name="small_all_reduce", in_specs=[vm], out_specs=vm, out_shape=_sds(v.shape, F32),
        scratch_shapes=[pltpu.VMEM((N_DEV, R, LANES), F32), pltpu.SemaphoreType.DMA((7,)),
                        pltpu.SemaphoreType.DMA((7,))],
    )(v)


def _adamw_math(w, g, m, v):
    m = ADAM_B1 * m + (1.0 - ADAM_B1) * g
    v = ADAM_B2 * v + (1.0 - ADAM_B2) * (g * g)
    m_hat = m / (1.0 - ADAM_B1 ** ADAM_STEP)
    v_hat = v / (1.0 - ADAM_B2 ** ADAM_STEP)
    delta = -ADAM_LR * (m_hat / (jnp.sqrt(v_hat) + ADAM_EPS) + ADAM_WD * w)
    return delta, m, v


def _adamw(parts, w, m, v, name, tr):
    P, R, C = parts.shape
    tr = min(tr, R)

    def body(p_ref, w_ref, m_ref, v_ref, g_out, d_out, m_out, v_out):
        g = p_ref[0].astype(F32)
        for d in range(1, P):
            g = g + p_ref[d].astype(F32)
        delta, m_new, v_new = _adamw_math(w_ref[...], g, m_ref[...], v_ref[...])
        g_out[...] = g
        d_out[...] = delta
        m_out[...] = m_new
        v_out[...] = v_new

    tile = pl.BlockSpec((tr, C), lambda i: (i, 0))
    return pl.pallas_call(
        body, name=name, grid=(R // tr,),
        in_specs=[pl.BlockSpec((P, tr, C), lambda i: (0, i, 0)), tile, tile, tile],
        out_specs=[tile] * 4, out_shape=[_sds((R, C), F32)] * 4,
        compiler_params=_cp(("parallel",), 32))(parts, w, m, v)


SMALL = (("norm_g", D_MODEL), ("mla_q_a_norm", MLA_Q_LORA), ("mla_kv_a_norm", MLA_KV_LORA), ("mla_q_norm", MLA_QK),
         ("mla_k_norm", MLA_QK), ("swa_q_norm", SWA_HEAD_DIM), ("swa_k_norm", SWA_HEAD_DIM), ("swa_sinks", HEADS))
SMALL_GRAD_KEY = dict(norm_g="norm_g", mla_q_a_norm="qa", mla_kv_a_norm="kva", mla_q_norm="qn", mla_k_norm="kn",
                      swa_q_norm="sqn", swa_k_norm="skn", swa_sinks="sinks")
SMALL_ROWS = 32
CONV_ROWS = 24


def _pack_small(get):
    parts = []
    for l in range(DEPTH):
        for name, n in SMALL:
            v = get(name, l).reshape(-1)
            parts.append(jnp.pad(v, (0, (-n) % LANES)))
    return jnp.concatenate(parts).reshape(SMALL_ROWS, LANES)


def _unpack_small(packed):
    flat = packed.reshape(-1)
    out = {name: [] for name, _ in SMALL}
    off = 0
    for l in range(DEPTH):
        for name, n in SMALL:
            out[name].append(flat[off:off + n])
            off += n + (-n) % LANES
    return {name: jnp.stack(v) for name, v in out.items()}


def kernel(x, norm_g, w_in, mla_q_a_norm, mla_w_qb, mla_kv_a_norm, mla_w_kvb, mla_q_norm, mla_k_norm, conv_w, swa_q_norm, swa_k_norm, swa_sinks, w_out, loss_target, m_norm_g, m_w_in, m_mla_q_a_norm, m_mla_w_qb, m_mla_kv_a_norm, m_mla_w_kvb, m_mla_q_norm, m_mla_k_norm, m_conv_w, m_swa_q_norm, m_swa_k_norm, m_swa_sinks, m_w_out, v_norm_g, v_w_in, v_mla_q_a_norm, v_mla_w_qb, v_mla_kv_a_norm, v_mla_w_kvb, v_mla_q_norm, v_mla_k_norm, v_conv_w, v_swa_q_norm, v_swa_k_norm, v_swa_sinks, v_w_out):
    T = x.shape[1]
    weights = dict(norm_g=norm_g, w_in=w_in, mla_q_a_norm=mla_q_a_norm, mla_w_qb=mla_w_qb,
                   mla_kv_a_norm=mla_kv_a_norm, mla_w_kvb=mla_w_kvb, mla_q_norm=mla_q_norm, mla_k_norm=mla_k_norm,
                   conv_w=conv_w, swa_q_norm=swa_q_norm, swa_k_norm=swa_k_norm, swa_sinks=swa_sinks, w_out=w_out)
    mom_m = dict(norm_g=m_norm_g, w_in=m_w_in, mla_q_a_norm=m_mla_q_a_norm, mla_w_qb=m_mla_w_qb,
                 mla_kv_a_norm=m_mla_kv_a_norm, mla_w_kvb=m_mla_w_kvb, mla_q_norm=m_mla_q_norm,
                 mla_k_norm=m_mla_k_norm, conv_w=m_conv_w, swa_q_norm=m_swa_q_norm, swa_k_norm=m_swa_k_norm,
                 swa_sinks=m_swa_sinks, w_out=m_w_out)
    mom_v = dict(norm_g=v_norm_g, w_in=v_w_in, mla_q_a_norm=v_mla_q_a_norm, mla_w_qb=v_mla_w_qb,
                 mla_kv_a_norm=v_mla_kv_a_norm, mla_w_kvb=v_mla_w_kvb, mla_q_norm=v_mla_q_norm,
                 mla_k_norm=v_mla_k_norm, conv_w=v_conv_w, swa_q_norm=v_swa_q_norm, swa_k_norm=v_swa_k_norm,
                 swa_sinks=v_swa_sinks, w_out=v_w_out)

    my = _lin(_my_coords())

    def shards(l):
        return [w_in[l].astype(MXU_DTYPE).T, mla_w_qb[l].astype(MXU_DTYPE), mla_w_kvb[l].astype(MXU_DTYPE),
                w_out[l].astype(MXU_DTYPE), conv_w[l]]

    def inproj_weights(l, g_win_t):
        return _inproj_weights(l, norm_g, g_win_t.reshape(IN_COLS, D_MODEL))

    def mixer_weights(l, gathered):
        g_wqb, g_wkvb, g_wout, g_conv = gathered
        return _mixer_weights(
            l, mla_q_a_norm, g_wqb, mla_kv_a_norm, g_wkvb, mla_q_norm, mla_k_norm,
            jnp.transpose(g_conv, (1, 0, 2)).reshape(3, GROUP_WIDTH), swa_q_norm, swa_k_norm, swa_sinks,
            g_wout.reshape(D_MIX, D_MODEL))

    slot_of = dict(
        w_in=lambda g: jnp.transpose(g["w_in"].reshape(D_MODEL, N_DEV, IN_COLS // N_DEV), (1, 0, 2)),
        w_out=lambda g: g["w_out"].reshape(N_DEV, D_MIX // N_DEV, D_MODEL),
        w_qb=lambda g: g["w_qb"], w_kvb=lambda g: g["w_kvb"])

    def own_slot(landed, mine):
        return [lax.dynamic_update_index_in_dim(a, m, my, 0) for a, m in zip(landed, mine)]

    def landed(handle, after, name, mine):
        return own_slot(_push_wait(handle, after, name), mine)

    sh = [shards(0), shards(1)]
    gather_in0 = _push_start(sh[0][:1], "weight_gather_in0_start", gather=True)
    rope = _rope_tables(T, gather_in0["token"])
    big_shapes = dict(w_in=(DEPTH * D_MODEL, IN_COLS // N_DEV), w_out=(DEPTH * D_MIX // N_DEV, D_MODEL),
                      mla_w_qb=(DEPTH * MLA_Q_LORA, MLA_QK), mla_w_kvb=(DEPTH * MLA_KV_LORA, 128))
    pad_conv = lambda a: jnp.pad(a.reshape(-1), (0, 8 * LANES - 6 * 64)).reshape(8, LANES)
    cat = lambda src: jnp.concatenate([_pack_small(lambda name, l: src[name][l]), pad_conv(src["conv_w"])], axis=0)
    adam_in = {name: [src[name].reshape(shape) for src in (weights, mom_m, mom_v)]
               for name, shape in big_shapes.items()}
    adam_in["small"] = [cat(weights), cat(mom_m), cat(mom_v)]
    rope0, adam_in, casts = lax.optimization_barrier((rope[0], adam_in, [sh[0][1:4], sh[1][:4]]))
    sh = [sh[0][:1] + casts[0] + [conv_w[0]], casts[1] + [conv_w[1]]]
    w_in0_t = landed(gather_in0, rope0, "weight_gather_in0_wait", sh[0][:1])[0]
    w_in0_t, conv0 = lax.optimization_barrier((w_in0_t, conv_w[0]))
    gather0 = _push_start(sh[0][1:4] + [conv0], "weight_gather0_start", gather=True)
    lw0 = inproj_weights(0, w_in0_t)
    lw0 = dict(lw0, ng=lw0["ng"] + gather0["token"])
    layer1 = {}

    def mixer0(proj):
        got = landed(gather0, proj, "weight_gather0_wait", sh[0][1:])
        got[0], conv1 = lax.optimization_barrier((got[0], conv_w[1]))
        layer1["gather"] = _push_start(sh[1][:4] + [conv1], "weight_gather1_start", gather=True)
        mw = mixer_weights(0, got)
        return dict(mw, qa=mw["qa"] + layer1["gather"]["token"])

    x1, sv0 = _layer_fwd(x[0], lw0, rope, late_weights=mixer0)
    g1_all = landed(layer1["gather"], x1, "weight_gather1_wait", sh[1])
    (g2, loss_tile), sv1 = _layer_fwd(x1, dict(inproj_weights(1, g1_all[0]), **mixer_weights(1, g1_all[1:])), rope,
                                      target=loss_target[0])

    parts = {(1, "w_in"): ("w_in", "w_out", "w_qb", "w_kvb"), (0, "mixer"): ("w_out", "w_qb", "w_kvb"),
             (0, "w_in"): ("w_in",)}
    started = []

    def start_exchange(l, part, g):
        if (l, part) not in parts:
            return 0.0
        sl = [slot_of[n](g) for n in parts[(l, part)]]
        handle = _push_start(sl, "grad_exchange%d_%s_start" % (l, part), gather=False)
        started.append((l, part, sl, handle))
        return handle["token"]

    g1, grads1 = _layer_bwd(g2, sv1, sv1["lw"], rope, on_big_grads=functools.partial(start_exchange, 1))
    lw0b = dict(sv0["lw"], conv=sv0["lw"]["conv"] + started[0][3]["token"])
    grad_x, grads0 = _layer_bwd(g1, sv0, lw0b, rope, on_big_grads=functools.partial(start_exchange, 0))
    recv = {}

    def receive(l, part, sl, handle, after):
        got = landed(handle, after, "grad_exchange%d_%s_wait" % (l, part), [s[my] for s in sl])
        recv.update({(l, n): a for n, a in zip(parts[(l, part)], got)})

    for entry in started[:-1]:
        receive(*entry, after=grad_x)
    grads = [grads0, grads1]
    stacked = lambda n: jnp.stack([recv[(0, n)], recv[(1, n)]], axis=1)

    small = jnp.concatenate([
        _pack_small(lambda name, l: grads[l][SMALL_GRAD_KEY[name]]),
        jnp.stack([g["conv"] for g in grads]).reshape(CONV_ROWS, LANES),
        loss_tile], axis=0)
    small = _small_all_reduce(small)
    loss = small[SMALL_ROWS + CONV_ROWS, 0]
    my = _lin(_my_coords())
    conv_g = lax.dynamic_slice_in_dim(small[SMALL_ROWS:SMALL_ROWS + CONV_ROWS].reshape(DEPTH, 3, GROUP_WIDTH),
                                      my * 64, 64, axis=2)

    out = {}

    def big(name, recv, tr):
        res = _adamw(recv.reshape((N_DEV,) + big_shapes[name]), *adam_in[name], "adamw_" + name, tr)
        out[name] = [r.reshape(weights[name].shape) for r in res]

    big("w_out", stacked("w_out"), 192)
    big("mla_w_qb", stacked("w_qb"), 512)
    big("mla_w_kvb", stacked("w_kvb"), 256)
    receive(*started[-1], after=out["w_out"][1])
    big("w_in", stacked("w_in"), 256)

    g_small = jnp.concatenate([small[:SMALL_ROWS], pad_conv(conv_g)], axis=0)
    res = _adamw(g_small[None], *adam_in["small"], "adamw_small", SMALL_ROWS + 8)
    smalls = [_unpack_small(r[:SMALL_ROWS]) for r in res]
    for name, _ in SMALL:
        out[name] = [s[name] for s in smalls]
    out["conv_w"] = [r[SMALL_ROWS:].reshape(-1)[:6 * 64].reshape(DEPTH, 3, 64) for r in res]

    order = ["norm_g", "w_in", "mla_q_a_norm", "mla_w_qb", "mla_kv_a_norm", "mla_w_kvb", "mla_q_norm", "mla_k_norm",
             "conv_w", "swa_q_norm", "swa_k_norm", "swa_sinks", "w_out"]
    result = [loss, grad_x[None]]
    for idx in range(4):
        result += [out[name][idx] for name in order]
    return tuple(result)
```

```python
import functools

import jax
import jax.numpy as jnp
import numpy as np
from jax import lax
from jax.experimental import pallas as pl
from jax.experimental.pallas import tpu as pltpu

F32 = jnp.float32
MXU_DTYPE = jnp.bfloat16
WIRE_DTYPE = jnp.bfloat16

N_DEV = 8
DEPTH = 2
D_MODEL = 1024
GROUP_WIDTH = 512
D_MIX = 3 * GROUP_WIDTH
BLOCK = 128
RMS_EPS = 1e-6
NEG_INF = -1e30
HEADS = 8
MLA_QK = 96
MLA_NOPE = 64
MLA_ROPE = 32
MLA_Q_LORA = 256
MLA_KV_LORA = 128
ROPE_THETA = 10000.0
SWA_HEAD_DIM = 64
LANES = 128
IN_COLS = 4256

ADAM_LR = 0.001
ADAM_B1 = 0.9
ADAM_B2 = 0.999
ADAM_EPS = 1e-08
ADAM_WD = 0.01
ADAM_STEP = 10

NP = 4352
CB_GMLA, CB_CH, CB_CB, CB_CC, CB_GCONV, CB_GSWA, CB_SQ = 0, 1, 2, 3, 4, 5, 7
CB_QLAT = 12
CB_KVLAT, CB_KROPE = 26, 27
CB_SK, CB_SV = 32, 33
DPB_MIX, DPB_MLA, DPB_SQ, DPB_SKV = 0, 6, 7, 16

TM_PROJ = 512
TM_ROW = 256
TK = 256
TQ = 2 * TK
MLA_SCALE = MLA_QK ** -0.5
MLA_ONES_ROW = (64, 0)
LOG2E = 1.4426950408889634
LN2 = 0.6931471805599453
TM_SWA = 512
VMEM_MB = 2 ** 20


def _cp(sem, vmem_mb):
    return pltpu.CompilerParams(dimension_semantics=sem, vmem_limit_bytes=vmem_mb * VMEM_MB)


def _sds(shape, dtype):
    return jax.ShapeDtypeStruct(shape, dtype)


def _dot(a, b):
    return jnp.dot(a, b, preferred_element_type=F32)


def _dot_nt(a, b):
    return lax.dot_general(a, b, (((1,), (1,)), ((), ())), preferred_element_type=F32)


def _dot_tn(a, b):
    return lax.dot_general(a, b, (((0,), (0,)), ((), ())), preferred_element_type=F32)


def _rms(x, n):
    r = lax.rsqrt(jnp.sum(x * x, axis=-1, keepdims=True) * (1.0 / n) + RMS_EPS)
    return x * r, r


def _rms_bwd(dy, xhat, r, w, n):
    g = dy * w
    return r * (g - xhat * (jnp.sum(g * xhat, axis=-1, keepdims=True) * (1.0 / n)))


def _rms_halves(x, half1):
    x2 = x * x
    s0 = jnp.sum(jnp.where(half1, 0.0, x2), axis=-1, keepdims=True)
    s1 = jnp.sum(jnp.where(half1, x2, 0.0), axis=-1, keepdims=True)
    r = jnp.where(half1, lax.rsqrt(s1 * (1.0 / 64) + RMS_EPS), lax.rsqrt(s0 * (1.0 / 64) + RMS_EPS))
    return x * r, r


def _rms_halves_bwd(dy, xhat, r, w, half1):
    g = dy * w
    t = g * xhat
    m0 = jnp.sum(jnp.where(half1, 0.0, t), axis=-1, keepdims=True) * (1.0 / 64)
    m1 = jnp.sum(jnp.where(half1, t, 0.0), axis=-1, keepdims=True) * (1.0 / 64)
    return r * (g - xhat * jnp.where(half1, m1, m0))


def _sigmoid(x):
    return 1.0 / (1.0 + jnp.exp(-x))


def _rope(x, c, s1, s2):
    ax = x.ndim - 1
    return x * c + pltpu.roll(x, 112, ax) * s1 + pltpu.roll(x, 16, ax) * s2


def _rope_bwd(dy, c, s1, s2):
    ax = dy.ndim - 1
    return dy * c + pltpu.roll(dy * s1, 16, ax) + pltpu.roll(dy * s2, 112, ax)


def _fold_rows8(x):
    return jnp.sum(x.reshape(x.shape[0] // 8, 8, x.shape[1]), axis=0)


def _row0(v, rows=8):
    row = lax.broadcasted_iota(jnp.int32, (rows, v.shape[1]), 0)
    return jnp.where(row == 0, jnp.broadcast_to(v, (rows, v.shape[1])), 0.0)


def _mm_nn(a, b, name, out_dtype=F32, residual=None, tm=TM_PROJ):
    M, K = a.shape
    N = b.shape[1]
    tm = min(tm, M)

    def body(*refs):
        if residual is None:
            a_ref, b_ref, o_ref = refs
            acc = _dot(a_ref[...].astype(MXU_DTYPE), b_ref[...])
        else:
            a_ref, b_ref, r_ref, o_ref = refs
            acc = _dot(a_ref[...].astype(MXU_DTYPE), b_ref[...]) + r_ref[...]
        o_ref[...] = acc.astype(out_dtype)

    in_specs = [pl.BlockSpec((tm, K), lambda i: (i, 0)), pl.BlockSpec((K, N), lambda i: (0, 0))]
    args = [a, b]
    if residual is not None:
        in_specs.append(pl.BlockSpec((tm, N), lambda i: (i, 0)))
        args.append(residual)
    return pl.pallas_call(
        body, name=name, grid=(M // tm,), in_specs=in_specs,
        out_specs=pl.BlockSpec((tm, N), lambda i: (i, 0)), out_shape=_sds((M, N), out_dtype),
        compiler_params=_cp(("parallel",), 48))(*args)


def _mm_tn(a, b, name, out_dtype, tn, tk=1024):
    T, M = a.shape
    N = b.shape[1]
    tk = min(tk, T)
    nk = T // tk

    def body(a_ref, b_ref, o_ref, acc_ref):
        k = pl.program_id(1)

        @pl.when(k == 0)
        def _():
            acc_ref[...] = jnp.zeros_like(acc_ref)

        acc_ref[...] += _dot_tn(a_ref[...].astype(MXU_DTYPE), b_ref[...].astype(MXU_DTYPE))

        @pl.when(k == nk - 1)
        def _():
            o_ref[...] = acc_ref[...].astype(out_dtype)

    return pl.pallas_call(
        body, name=name, grid=(N // tn, nk),
        in_specs=[pl.BlockSpec((tk, M), lambda n, k: (k, 0)), pl.BlockSpec((tk, tn), lambda n, k: (k, n))],
        out_specs=pl.BlockSpec((M, tn), lambda n, k: (0, n)), out_shape=_sds((M, N), out_dtype),
        scratch_shapes=[pltpu.VMEM((M, tn), F32)],
        compiler_params=_cp(("parallel", "arbitrary"), 48))(a, b)


def _inproj_fwd(x, ng, wp):
    T, D = x.shape
    tm = min(TM_PROJ, T)

    def body(x_ref, g_ref, w_ref, proj_ref, h_ref):
        xhat, _ = _rms(x_ref[...], D)
        h = (xhat * g_ref[...]).astype(MXU_DTYPE)
        h_ref[...] = h
        proj_ref[...] = _dot(h, w_ref[...])

    return pl.pallas_call(
        body, name="inproj_fwd", grid=(T // tm,),
        in_specs=[pl.BlockSpec((tm, D), lambda i: (i, 0)), pl.BlockSpec((1, D), lambda i: (0, 0)),
                  pl.BlockSpec((D, NP), lambda i: (0, 0))],
        out_specs=[pl.BlockSpec((tm, NP), lambda i: (i, 0)), pl.BlockSpec((tm, D), lambda i: (i, 0))],
        out_shape=[_sds((T, NP), F32), _sds((T, D), MXU_DTYPE)],
        compiler_params=_cp(("parallel",), 48))(x, ng, wp)


def _mla_prep_fwd(proj, lw, rope):
    T = proj.shape[0]
    tk = min(TK, T // 2)
    nsub = 2
    tm = nsub * tk

    def body(ql_ref, kvl_ref, kr_ref, qa_ref, kva_ref, wq_ref, wk_ref, wv_ref, qn_ref, kn_ref,
             c_ref, s1_ref, s2_ref, q_out, k_out, kt_out, vt_out):
        c, s1, s2 = c_ref[...], s1_ref[...], s2_ref[...]
        qhat, _ = _rms(ql_ref[...], MLA_Q_LORA)
        qn = (qhat * qa_ref[...]).astype(MXU_DTYPE)
        khat, _ = _rms(kvl_ref[...], MLA_KV_LORA)
        kvn = (khat * kva_ref[...]).astype(MXU_DTYPE)
        kr = kr_ref[...]
        half1 = lax.broadcasted_iota(jnp.int32, (tm, LANES), 1) >= 64
        ones_row = lax.broadcasted_iota(jnp.int32, (LANES, 1), 0)
        q3, _ = _rms(jnp.stack([_dot(qn, wq_ref[h]) for h in range(HEADS)]), MLA_QK)
        q_out[...] = (_rope(q3 * qn_ref[...], c, s1, s2) * (MLA_SCALE * LOG2E)).astype(MXU_DTYPE)
        k3, _ = _rms(jnp.stack([_dot(kvn, wk_ref[h]) for h in range(HEADS)]) + kr, MLA_QK)
        k3 = _rope(k3 * kn_ref[...], c, s1, s2)
        k_out[...] = k3.astype(MXU_DTYPE)
        for h in range(HEADS):
            for t in range(nsub):
                kt_out[h, t] = k3[h, tk * t:tk * (t + 1)].T.astype(MXU_DTYPE)
        v = _dot(kvn, wv_ref[...])
        for h in range(HEADS):
            vp = v[:, LANES * (h // 2):LANES * (h // 2 + 1)]
            own = half1 if h % 2 else jnp.logical_not(half1)
            vp = jnp.where(own, vp, 0.0)
            for t in range(nsub):
                vpt = vp[tk * t:tk * (t + 1)].T
                vt_out[h, t] = jnp.where(ones_row == MLA_ONES_ROW[h % 2], 1.0, vpt).astype(MXU_DTYPE)

    full = lambda shape: pl.BlockSpec(shape, lambda i: (0,) * len(shape))
    hd = pl.BlockSpec((HEADS, tm, LANES), lambda i: (0, i, 0))
    hdt = pl.BlockSpec((HEADS, nsub, LANES, tk), lambda i: (0, i, 0, 0))
    nat = _sds((HEADS, T, LANES), MXU_DTYPE)
    tr = _sds((HEADS, T // tk, LANES, tk), MXU_DTYPE)
    return pl.pallas_call(
        body, name="mla_prep_fwd", grid=(T // tm,),
        in_specs=[pl.BlockSpec((tm, 256), lambda i: (i, CB_QLAT)), pl.BlockSpec((tm, LANES), lambda i: (i, CB_KVLAT)),
                  pl.BlockSpec((tm, LANES), lambda i: (i, CB_KROPE)),
                  full((1, 256)), full((1, LANES)), full((HEADS, 256, LANES)), full((HEADS, LANES, LANES)),
                  full((LANES, 512)), full((1, LANES)), full((1, LANES)),
                  pl.BlockSpec((tm, LANES), lambda i: (i, 0)), pl.BlockSpec((tm, LANES), lambda i: (i, 0)),
                  pl.BlockSpec((tm, LANES), lambda i: (i, 0))],
        out_specs=[hd, hd, hdt, hdt],
        out_shape=[nat, nat, tr, tr],
        compiler_params=_cp(("parallel",), 32))(
            proj, proj, proj, lw["qa"], lw["kva"], lw["wq"], lw["wk"], lw["wv"], lw["qn"], lw["kn"],
            rope[0], rope[1], rope[2])


def _mla_attn_fwd(q, k, vt):
    T = q.shape[1]
    tk = min(TK, T // 2)
    tq = 2 * tk

    def body(q_ref, k_ref, vt_ref, o_ref, lse_ref, acc_s, m_s, s_a, s_b):
        i = pl.program_id(1)
        key = lax.broadcasted_iota(jnp.int32, (tk, tq), 0)
        qry = lax.broadcasted_iota(jnp.int32, (tk, tq), 1)
        qs = [q_ref[0], q_ref[1]]
        acc_s[...] = jnp.zeros_like(acc_s)
        m_s[...] = jnp.full(m_s.shape, NEG_INF, F32)

        def scores(kj, buf):
            rows = pl.ds(pl.multiple_of(kj * tk, tk), tk)
            for r in range(2):
                buf[r] = _dot_nt(k_ref[r, rows, :], qs[r])

        def consume(kj, buf, diag):
            for r in range(2):
                s = buf[r]
                if diag is not None:
                    s = jnp.where(key + diag * tk <= qry, s, NEG_INF)
                m_old = m_s[r]
                m_new = jnp.maximum(m_old, jnp.max(s, axis=0, keepdims=True))
                alpha = jnp.exp2(m_old - m_new)
                p = jnp.exp2(s - m_new)
                m_s[r] = m_new
                acc_s[r] = alpha * acc_s[r] + _dot(vt_ref[r, kj], p.astype(MXU_DTYPE))

        scores(0, s_a)

        def pair(kj):
            scores(kj + 1, s_b)
            consume(kj, s_a, None)
            scores(kj + 2, s_a)
            consume(kj + 1, s_b, None)

        def octet(ko, carry):
            for t in range(4):
                pair(8 * ko + 2 * t)
            return carry

        lax.fori_loop(0, i // 4, octet, 0)

        @pl.when(i % 4 >= 2)
        def _():
            pair(8 * (i // 4))
            pair(8 * (i // 4) + 2)

        @pl.when(i % 2 == 1)
        def _():
            pair(2 * i - 2)

        last = pl.ds(pl.multiple_of((2 * i + 1) * tk, tk), tk)
        for r in range(2):
            s_b[r, :, tk:] = _dot_nt(k_ref[r, last, :], qs[r][tk:])
        consume(2 * i, s_a, 0)
        for r in range(2):
            s = jnp.where(key[:, tk:] + tk <= qry[:, tk:], s_b[r, :, tk:], NEG_INF)
            m_old = m_s[r, :, tk:]
            m_new = jnp.maximum(m_old, jnp.max(s, axis=0, keepdims=True))
            p = jnp.exp2(s - m_new)
            m_s[r, :, tk:] = m_new
            acc_s[r, :, tk:] = (jnp.exp2(m_old - m_new) * acc_s[r, :, tk:]
                                + _dot(vt_ref[r, 2 * i + 1], p.astype(MXU_DTYPE)))
        l = [acc_s[r, pl.ds(MLA_ONES_ROW[r], 1), :] for r in range(2)]
        head0 = lax.broadcasted_iota(jnp.int32, (LANES, 1), 0) < 64
        o_ref[...] = jnp.where(head0, acc_s[0] / l[0], acc_s[1] / l[1]).T
        for r in range(2):
            lse_ref[r] = m_s[r] + jnp.log2(l[r])

    return pl.pallas_call(
        body, name="mla_attn_fwd", grid=(HEADS // 2, T // tq),
        in_specs=[pl.BlockSpec((2, tq, LANES), lambda j, i: (j, i, 0)),
                  pl.BlockSpec((2, T, LANES), lambda j, i: (j, 0, 0)),
                  pl.BlockSpec((2, T // tk, LANES, tk), lambda j, i: (j, 0, 0, 0))],
        out_specs=[pl.BlockSpec((tq, LANES), lambda j, i: (i, j)),
                   pl.BlockSpec((2, 1, tq), lambda j, i: (j, 0, i))],
        out_shape=[_sds((T, GROUP_WIDTH), F32), _sds((HEADS, 1, T), F32)],
        scratch_shapes=[pltpu.VMEM((2, LANES, tq), F32), pltpu.VMEM((2, 1, tq), F32),
                        pltpu.VMEM((2, tk, tq), F32), pltpu.VMEM((2, tk, tq), F32)],
        compiler_params=_cp(("parallel", "arbitrary"), 40))(q, k, vt)


def _swa_kv_variants(x, half1):
    xs = pltpu.roll(x, 64, 1)
    out = {}
    for g in range(2):
        for r in range(2):
            own = half1 if r else jnp.logical_not(half1)
            out[(g, r)] = jnp.where(own, x if g == r else xs, 0.0).astype(MXU_DTYPE)
    return out


def _swa_alibi():
    ki = np.arange(2 * BLOCK)[:, None]
    qi = np.arange(BLOCK)[None, :]
    dist = BLOCK + qi - ki
    slopes = 2.0 ** -(np.arange(HEADS) + 1.0)
    tab = np.where(((dist >= 0) & (dist < BLOCK))[None], slopes[:, None, None] * dist[None], 1e30)
    return jnp.asarray(tab, F32)


def _swa_kv_variants_t(xt, rows1):
    xs = pltpu.roll(xt, 64, 0)
    out = {}
    for g in range(2):
        for r in range(2):
            own = rows1 if r else jnp.logical_not(rows1)
            out[(g, r)] = jnp.where(own, xt if g == r else xs, 0.0).astype(MXU_DTYPE)
    return out


def _swa_probs(i, nb, q_ref, k_ref, v_ref, pk_ref, pv_ref, qw_ref, kw_ref, alibi_ref, sink_ref):
    scale = SWA_HEAD_DIM ** -0.5
    half1 = lax.broadcasted_iota(jnp.int32, (1, LANES), 1) >= 64
    k_all = jnp.concatenate([pk_ref[...], k_ref[...]], axis=0)
    v_all = jnp.concatenate([pv_ref[...], v_ref[...]], axis=0)
    khat, _ = _rms_halves(k_all, half1)
    kn = khat * kw_ref[...]
    kp = _swa_kv_variants(kn, half1)
    qhat, qr, qn, qt = [], [], [], []
    for j in range(4):
        xh, r = _rms_halves(q_ref[:, LANES * j:LANES * (j + 1)], half1)
        qf = xh * qw_ref[...]
        qhat.append(xh)
        qr.append(r)
        qn.append(qf.astype(MXU_DTYPE))
        qt.append(qf.T.astype(MXU_DTYPE))
    key = lax.broadcasted_iota(jnp.int32, (2 * BLOCK, BLOCK), 0)
    first = jnp.where((i == 0) & (key < BLOCK), NEG_INF, 0.0)
    s = jnp.stack([_dot(kp[(h // 4, h % 2)][BLOCK * b:BLOCK * (b + 2)], qt[h // 2][:, BLOCK * b:BLOCK * (b + 1)])
                   for b in range(nb) for h in range(HEADS)]) * scale - alibi_ref[...]
    s = jnp.concatenate([s[:HEADS] + first, s[HEADS:]], axis=0) if nb > 1 else s + first
    sink = jnp.stack([jnp.full((1, 1), sink_ref[h], F32) for _ in range(nb) for h in range(HEADS)])
    m = jnp.maximum(jnp.max(s, axis=1, keepdims=True), sink)
    e = jnp.exp(s - m)
    es = jnp.exp(sink - m)
    inv = 1.0 / (jnp.sum(e, axis=1, keepdims=True) + es)
    return e * inv, es * inv, dict(half1=half1, kn=kn, kp=kp, v_all=v_all, qhat=qhat, qr=qr, qn=qn)


def _swa_fwd(proj, lw):
    T = proj.shape[0]
    tm = min(TM_SWA, T)
    nb = tm // BLOCK

    def body(q_ref, k_ref, v_ref, pk_ref, pv_ref, qw_ref, kw_ref, alibi_ref, sink_ref, o_ref):
        p, _, c = _swa_probs(pl.program_id(0), nb, q_ref, k_ref, v_ref, pk_ref, pv_ref, qw_ref, kw_ref, alibi_ref,
                             sink_ref)
        p = p.astype(MXU_DTYPE)
        rows1 = lax.broadcasted_iota(jnp.int32, (LANES, 1), 0) >= 64
        vpt = _swa_kv_variants_t(c["v_all"].T, rows1)
        for j in range(4):
            g = j // 2
            o_t = [_dot(vpt[(g, 0)][:, BLOCK * b:BLOCK * (b + 2)], p[HEADS * b + 2 * j])
                   + _dot(vpt[(g, 1)][:, BLOCK * b:BLOCK * (b + 2)], p[HEADS * b + 2 * j + 1]) for b in range(nb)]
            o_t = jnp.concatenate(o_t, axis=1) if nb > 1 else o_t[0]
            o_ref[:, LANES * j:LANES * (j + 1)] = o_t.T

    prev = lambda cb: pl.BlockSpec((BLOCK, LANES), lambda i: (jnp.maximum(i * nb - 1, 0), cb))
    return pl.pallas_call(
        body, name="swa_fwd", grid=(T // tm,),
        in_specs=[pl.BlockSpec((tm, 512), lambda i: (i, CB_SQ)), pl.BlockSpec((tm, LANES), lambda i: (i, CB_SK)),
                  pl.BlockSpec((tm, LANES), lambda i: (i, CB_SV)), prev(CB_SK), prev(CB_SV),
                  pl.BlockSpec((1, LANES), lambda i: (0, 0)), pl.BlockSpec((1, LANES), lambda i: (0, 0)),
                  pl.BlockSpec((nb * HEADS, 2 * BLOCK, BLOCK), lambda i: (0, 0, 0)),
                  pl.BlockSpec(memory_space=pltpu.SMEM)],
        out_specs=pl.BlockSpec((tm, 512), lambda i: (i, 0)),
        out_shape=_sds((T, GROUP_WIDTH), F32),
        compiler_params=_cp(("parallel",), 40))(
            proj, proj, proj, proj, proj, lw["sqn"], lw["skn"], jnp.tile(_swa_alibi(), (nb, 1, 1)), lw["sinks"])


def _shift_down(u, prev, n, row):
    tm = u.shape[0]
    out = pltpu.roll(u, n, 0)
    row8 = lax.broadcasted_iota(jnp.int32, prev.shape, 0)
    for t in range(n):
        src = jnp.sum(jnp.where(row8 == 8 - n + t, prev, 0.0), axis=0, keepdims=True)
        out = jnp.where(row == t, src, out)
    return out


def _shift_up(u, nxt, n, row):
    tm = u.shape[0]
    out = pltpu.roll(u, tm - n, 0)
    row8 = lax.broadcasted_iota(jnp.int32, nxt.shape, 0)
    for t in range(n):
        src = jnp.sum(jnp.where(row8 == t, nxt, 0.0), axis=0, keepdims=True)
        out = jnp.where(row == tm - n + t, src, out)
    return out


def _mix_fwd(proj, o_mla, o_swa, conv_w):
    T = proj.shape[0]
    tm = min(TM_ROW, T)

    def body(gm_ref, ch_ref, cb_ref, cc_ref, gc_ref, gs_ref, pch_ref, pcc_ref, om_ref, os_ref, w_ref, y_ref):
        i = pl.program_id(0)
        row = lax.broadcasted_iota(jnp.int32, (tm, GROUP_WIDTH), 0)
        u = cc_ref[...] * ch_ref[...]
        u_prev = jnp.where(i > 0, pcc_ref[...] * pch_ref[...], 0.0)
        z = (w_ref[0:1, :] * _shift_down(u, u_prev, 2, row) + w_ref[1:2, :] * _shift_down(u, u_prev, 1, row)
             + w_ref[2:3, :] * u)
        gm, gc, gs = gm_ref[...], gc_ref[...], gs_ref[...]
        y_ref[:, 0:512] = (om_ref[...] * (gm * _sigmoid(gm))).astype(MXU_DTYPE)
        y_ref[:, 512:1024] = (cb_ref[...] * z * (gc * _sigmoid(gc))).astype(MXU_DTYPE)
        y_ref[:, 1024:1536] = (os_ref[...] * (gs * _sigmoid(gs))).astype(MXU_DTYPE)

    blk = lambda cb: pl.BlockSpec((tm, 512), lambda i: (i, cb))
    prev = lambda cb: pl.BlockSpec((8, 512), lambda i: (jnp.maximum(i * (tm // 8) - 1, 0), cb))
    tile = pl.BlockSpec((tm, 512), lambda i: (i, 0))
    return pl.pallas_call(
        body, name="mix_fwd", grid=(T // tm,),
        in_specs=[blk(CB_GMLA), blk(CB_CH), blk(CB_CB), blk(CB_CC), blk(CB_GCONV), blk(CB_GSWA),
                  prev(CB_CH), prev(CB_CC), tile, tile, pl.BlockSpec((8, 512), lambda i: (0, 0))],
        out_specs=pl.BlockSpec((tm, D_MIX), lambda i: (i, 0)),
        out_shape=_sds((T, D_MIX), MXU_DTYPE),
        compiler_params=_cp(("parallel",), 32))(
            proj, proj, proj, proj, proj, proj, proj, proj, o_mla, o_swa, conv_w)


def _outproj_loss(ycat, wo, x, target):
    T, D = x.shape
    K = ycat.shape[1]
    tm = min(TM_PROJ, T)
    nt = T // tm

    def body(y_ref, w_ref, x_ref, t_ref, g_ref, loss_ref, acc_ref):
        i = pl.program_id(0)

        @pl.when(i == 0)
        def _():
            acc_ref[...] = jnp.zeros_like(acc_ref)

        err = _dot(y_ref[...], w_ref[...]) + x_ref[...] - t_ref[...]
        g_ref[...] = err * (1.0 / D)
        acc_ref[...] += _fold_rows8(err * err)

        @pl.when(i == nt - 1)
        def _():
            tot = jnp.sum(jnp.sum(acc_ref[...], axis=1, keepdims=True), axis=0, keepdims=True)
            loss_ref[...] = jnp.broadcast_to(tot * (0.5 / D), (8, LANES))

    tile = pl.BlockSpec((tm, D), lambda i: (i, 0))
    return pl.pallas_call(
        body, name="outproj_loss", grid=(nt,),
        in_specs=[pl.BlockSpec((tm, K), lambda i: (i, 0)), pl.BlockSpec((K, D), lambda i: (0, 0)), tile, tile],
        out_specs=[tile, pl.BlockSpec((8, LANES), lambda i: (0, 0))],
        out_shape=[_sds((T, D), F32), _sds((8, LANES), F32)],
        scratch_shapes=[pltpu.VMEM((8, D), F32)],
        compiler_params=_cp(("arbitrary",), 48))(ycat, wo, x, target)


def _outproj_bwd(g, ycat, wot):
    T, D = g.shape
    K = ycat.shape[1]
    tm = min(512, T)
    nt = T // tm

    def body(g_ref, y_ref, wt_ref, dy_ref, dw_ref, acc_ref):
        i = pl.program_id(0)

        @pl.when(i == 0)
        def _():
            acc_ref[...] = jnp.zeros_like(acc_ref)

        gb = g_ref[...].astype(MXU_DTYPE)
        dy_ref[...] = _dot(gb, wt_ref[...])
        acc_ref[...] += _dot_tn(y_ref[...], gb)

        @pl.when(i == nt - 1)
        def _():
            dw_ref[...] = acc_ref[...].astype(WIRE_DTYPE)

    return pl.pallas_call(
        body, name="outproj_bwd", grid=(nt,),
        in_specs=[pl.BlockSpec((tm, D), lambda i: (i, 0)), pl.BlockSpec((tm, K), lambda i: (i, 0)),
                  pl.BlockSpec((D, K), lambda i: (0, 0))],
        out_specs=[pl.BlockSpec((tm, K), lambda i: (i, 0)), pl.BlockSpec((K, D), lambda i: (0, 0))],
        out_shape=[_sds((T, K), F32), _sds((K, D), WIRE_DTYPE)],
        scratch_shapes=[pltpu.VMEM((K, D), F32)],
        compiler_params=_cp(("arbitrary",), 48))(g, ycat, wot)


def _mix_bwd(dycat, proj, o_mla, o_swa, conv_w):
    T = proj.shape[0]
    tm = min(TM_ROW, T)
    nt = T // tm

    def body(dym_ref, dyc_ref, dys_ref, gm_ref, ch_ref, cb_ref, cc_ref, gc_ref, gs_ref, pch_ref, pcc_ref,
             ndy_ref, ncb_ref, ngc_ref, om_ref, os_ref, w_ref,
             d1_ref, dom_ref, dos_ref, dw_ref):
        i = pl.program_id(0)

        @pl.when(i == 0)
        def _():
            dw_ref[...] = jnp.zeros_like(dw_ref)

        row = lax.broadcasted_iota(jnp.int32, (tm, GROUP_WIDTH), 0)

        def gate(g):
            sg = _sigmoid(g)
            return g * sg, sg * (1.0 + g * (1.0 - sg))

        gm = gm_ref[...]
        silu, dsilu = gate(gm)
        dym = dym_ref[...]
        dom_ref[...] = dym * silu
        d1_ref[:, 0:512] = (dym * om_ref[...] * dsilu).astype(MXU_DTYPE)

        gs = gs_ref[...]
        silu, dsilu = gate(gs)
        dys = dys_ref[...]
        dos_ref[...] = dys * silu
        d1_ref[:, 2560:3072] = (dys * os_ref[...] * dsilu).astype(MXU_DTYPE)

        ch, cb, cc, gc, dyc = ch_ref[...], cb_ref[...], cc_ref[...], gc_ref[...], dyc_ref[...]
        w0, w1, w2 = w_ref[0:1, :], w_ref[1:2, :], w_ref[2:3, :]
        u = cc * ch
        u_prev = jnp.where(i > 0, pcc_ref[...] * pch_ref[...], 0.0)
        u1 = _shift_down(u, u_prev, 1, row)
        u2 = _shift_down(u, u_prev, 2, row)
        z = w0 * u2 + w1 * u1 + w2 * u
        silu, dsilu = gate(gc)
        dz = dyc * cb * silu
        ngc = ngc_ref[...]
        dz_next = jnp.where(i < nt - 1, ndy_ref[...] * ncb_ref[...] * (ngc * _sigmoid(ngc)), 0.0)
        du = w2 * dz + w1 * _shift_up(dz, dz_next, 1, row) + w0 * _shift_up(dz, dz_next, 2, row)
        d1_ref[:, 512:1024] = (du * cc).astype(MXU_DTYPE)
        d1_ref[:, 1024:1536] = (dyc * z * silu).astype(MXU_DTYPE)
        d1_ref[:, 1536:2048] = (du * ch).astype(MXU_DTYPE)
        d1_ref[:, 2048:2560] = (dyc * cb * z * dsilu).astype(MXU_DTYPE)
        row8 = lax.broadcasted_iota(jnp.int32, (8, GROUP_WIDTH), 0)
        dw = jnp.zeros((8, GROUP_WIDTH), F32)
        for t, shifted in enumerate((u2, u1, u)):
            dw = dw + jnp.where(row8 == t, jnp.sum(dz * shifted, axis=0, keepdims=True), 0.0)
        dw_ref[...] += dw

    blk = lambda cb: pl.BlockSpec((tm, 512), lambda i: (i, cb))
    prev = lambda cb: pl.BlockSpec((8, 512), lambda i: (jnp.maximum(i * (tm // 8) - 1, 0), cb))
    nxt = lambda cb: pl.BlockSpec((8, 512), lambda i: (jnp.minimum((i + 1) * (tm // 8), T // 8 - 1), cb))
    tile = pl.BlockSpec((tm, 512), lambda i: (i, 0))
    return pl.pallas_call(
        body, name="mix_bwd", grid=(nt,),
        in_specs=[blk(0), blk(1), blk(2), blk(CB_GMLA), blk(CB_CH), blk(CB_CB), blk(CB_CC), blk(CB_GCONV),
                  blk(CB_GSWA), prev(CB_CH), prev(CB_CC), nxt(1), nxt(CB_CB), nxt(CB_GCONV), tile, tile,
                  pl.BlockSpec((8, 512), lambda i: (0, 0))],
        out_specs=[pl.BlockSpec((tm, 3072), lambda i: (i, DPB_MIX)), tile, tile,
                   pl.BlockSpec((8, 512), lambda i: (0, 0))],
        out_shape=[_sds((T, NP), MXU_DTYPE), _sds((T, 512), F32), _sds((T, 512), F32), _sds((8, 512), F32)],
        compiler_params=_cp(("arbitrary",), 48))(
            dycat, dycat, dycat, proj, proj, proj, proj, proj, proj, proj, proj, dycat, proj, proj,
            o_mla, o_swa, conv_w)


def _swa_bwd(proj, o_swa, do_swa, lw, dproj):
    T = proj.shape[0]
    tm = min(TM_SWA, T)
    nb = tm // BLOCK
    scale = SWA_HEAD_DIM ** -0.5

    def body(q_ref, k_ref, v_ref, pk_ref, pv_ref, o_ref, do_ref, qw_ref, kw_ref, alibi_ref, sink_ref, dproj_in,
             dq_ref, dk_ref, dv_ref, dqw_ref, dsink_ref):
        i = pl.program_id(0)

        @pl.when(i == 0)
        def _():
            dk_ref[...] = jnp.zeros_like(dk_ref)
            dv_ref[...] = jnp.zeros_like(dv_ref)
            dqw_ref[...] = jnp.zeros_like(dqw_ref)
            dsink_ref[...] = jnp.zeros_like(dsink_ref)

        p, p_sink, c = _swa_probs(i, nb, q_ref, k_ref, v_ref, pk_ref, pv_ref, qw_ref, kw_ref, alibi_ref, sink_ref)
        half1, kp, qn, qhat, qr = c["half1"], c["kp"], c["qn"], c["qhat"], c["qr"]
        rows1 = lax.broadcasted_iota(jnp.int32, (LANES, 1), 0) >= 64
        kpt = _swa_kv_variants_t(c["kn"].T, rows1)
        vp = _swa_kv_variants(c["v_all"], half1)
        qw = qw_ref[...]
        rows = [slice(BLOCK * b, BLOCK * (b + 1)) for b in range(nb)]
        keys = [slice(BLOCK * b, BLOCK * (b + 2)) for b in range(nb)]
        dob, dot_b, dd0, dd1 = [], [], [], []
        for j in range(4):
            cols = slice(LANES * j, LANES * (j + 1))
            do = do_ref[:, cols]
            do_t = do.T
            prod_t = do_t * o_ref[:, cols].T
            dob.append(do.astype(MXU_DTYPE))
            dot_b.append(do_t.astype(MXU_DTYPE))
            dd0.append(jnp.sum(jnp.where(rows1, 0.0, prod_t), axis=0, keepdims=True))
            dd1.append(jnp.sum(jnp.where(rows1, prod_t, 0.0), axis=0, keepdims=True))
        dd = jnp.stack([(dd1 if h % 2 else dd0)[h // 2][:, rows[b]] for b in range(nb) for h in range(HEADS)])
        dp = jnp.stack([_dot(vp[(h // 4, h % 2)][keys[b]], dot_b[h // 2][:, rows[b]])
                        for b in range(nb) for h in range(HEADS)])
        ds = (p * (dp - dd) * scale).astype(MXU_DTYPE)
        dsink = -jnp.sum(p_sink * dd, axis=2, keepdims=True)
        pb = p.astype(MXU_DTYPE)

        dqw = jnp.zeros((1, LANES), F32)
        for j in range(4):
            g = j // 2
            dqn_t = [_dot(kpt[(g, 0)][:, keys[b]], ds[HEADS * b + 2 * j])
                     + _dot(kpt[(g, 1)][:, keys[b]], ds[HEADS * b + 2 * j + 1]) for b in range(nb)]
            dqn = (jnp.concatenate(dqn_t, axis=1) if nb > 1 else dqn_t[0]).T
            dqw = dqw + jnp.sum(dqn * qhat[j], axis=0, keepdims=True)
            dq_ref[:, LANES * j:LANES * (j + 1)] = _rms_halves_bwd(dqn, qhat[j], qr[j], qw, half1).astype(MXU_DTYPE)
        dqw_ref[...] += _row0(dqw + pltpu.roll(dqw, 64, 1))

        dk_tot = jnp.zeros((tm + BLOCK, LANES), F32)
        dv_tot = jnp.zeros((tm + BLOCK, LANES), F32)
        for b in range(nb):
            dk_b = jnp.zeros((2 * BLOCK, LANES), F32)
            dv_b = jnp.zeros((2 * BLOCK, LANES), F32)
            for g in range(2):
                for r in range(2):
                    own = half1 if r else jnp.logical_not(half1)
                    ha, hb = HEADS * b + 4 * g + r, HEADS * b + 4 * g + 2 + r
                    qa, qb = qn[2 * g][rows[b]], qn[2 * g + 1][rows[b]]
                    da, db = dob[2 * g][rows[b]], dob[2 * g + 1][rows[b]]
                    dkp = jnp.where(own, _dot(ds[ha], qa) + _dot(ds[hb], qb), 0.0)
                    dvp = jnp.where(own, _dot(pb[ha], da) + _dot(pb[hb], db), 0.0)
                    if g != r:
                        dkp = pltpu.roll(dkp, 64, 1)
                        dvp = pltpu.roll(dvp, 64, 1)
                    dk_b = dk_b + dkp
                    dv_b = dv_b + dvp
            pad = lambda x: jnp.concatenate(
                [z for z in (jnp.zeros((BLOCK * b, LANES), F32), x, jnp.zeros((BLOCK * (nb - 1 - b), LANES), F32))
                 if z.shape[0]], axis=0)
            dk_tot = dk_tot + pad(dk_b)
            dv_tot = dv_tot + pad(dv_b)
        dst = pl.ds(pl.multiple_of(i * tm, BLOCK), tm + BLOCK)
        dk_ref[dst, :] += dk_tot
        dv_ref[dst, :] += dv_tot

        row8 = lax.broadcasted_iota(jnp.int32, (8, LANES), 0)
        dsink_tile = jnp.zeros((8, LANES), F32)
        for b in range(nb):
            for h in range(HEADS):
                dsink_tile = dsink_tile + jnp.where(row8 == h, jnp.broadcast_to(dsink[HEADS * b + h], (8, LANES)), 0.0)
        dsink_ref[...] += dsink_tile

    prev = lambda cb: pl.BlockSpec((BLOCK, LANES), lambda i: (jnp.maximum(i * nb - 1, 0), cb))
    tile = pl.BlockSpec((tm, 512), lambda i: (i, 0))
    small = pl.BlockSpec((8, LANES), lambda i: (0, 0))
    acc = pl.BlockSpec((T + BLOCK, LANES), lambda i: (0, 0))
    return pl.pallas_call(
        body, name="swa_bwd", grid=(T // tm,),
        in_specs=[pl.BlockSpec((tm, 512), lambda i: (i, CB_SQ)), pl.BlockSpec((tm, LANES), lambda i: (i, CB_SK)),
                  pl.BlockSpec((tm, LANES), lambda i: (i, CB_SV)), prev(CB_SK), prev(CB_SV), tile, tile,
                  pl.BlockSpec((1, LANES), lambda i: (0, 0)), pl.BlockSpec((1, LANES), lambda i: (0, 0)),
                  pl.BlockSpec((nb * HEADS, 2 * BLOCK, BLOCK), lambda i: (0, 0, 0)),
                  pl.BlockSpec(memory_space=pltpu.SMEM), pl.BlockSpec(memory_space=pl.ANY)],
        out_specs=[pl.BlockSpec((tm, 512), lambda i: (i, DPB_SQ)), acc, acc, small, small],
        out_shape=[_sds((T, NP), MXU_DTYPE), _sds((T + BLOCK, LANES), F32), _sds((T + BLOCK, LANES), F32),
                   _sds((8, LANES), F32), _sds((8, LANES), F32)],
        input_output_aliases={11: 0},
        compiler_params=_cp(("arbitrary",), 48))(
            proj, proj, proj, proj, proj, o_swa, do_swa, lw["sqn"], lw["skn"], jnp.tile(_swa_alibi(), (nb, 1, 1)),
            lw["sinks"], dproj)


def _swa_kv_bwd(proj, dkn, dv, lw, dproj):
    T = proj.shape[0]
    tm = min(TM_SWA, T)
    dkn, dv = dkn[BLOCK:], dv[BLOCK:]

    def body(k_ref, dkn_ref, dv_ref, kw_ref, dproj_in, d_ref, dkw_ref):
        i = pl.program_id(0)

        @pl.when(i == 0)
        def _():
            dkw_ref[...] = jnp.zeros_like(dkw_ref)

        half1 = lax.broadcasted_iota(jnp.int32, (1, LANES), 1) >= 64
        khat, kr = _rms_halves(k_ref[...], half1)
        dkn_t = dkn_ref[...]
        dkw = jnp.sum(dkn_t * khat, axis=0, keepdims=True)
        dkw_ref[...] += _row0(dkw + pltpu.roll(dkw, 64, 1))
        d_ref[:, 0:LANES] = _rms_halves_bwd(dkn_t, khat, kr, kw_ref[...], half1).astype(MXU_DTYPE)
        d_ref[:, LANES:2 * LANES] = dv_ref[...].astype(MXU_DTYPE)

    return pl.pallas_call(
        body, name="swa_kv_bwd", grid=(T // tm,),
        in_specs=[pl.BlockSpec((tm, LANES), lambda i: (i, CB_SK)), pl.BlockSpec((tm, LANES), lambda i: (i, 0)),
                  pl.BlockSpec((tm, LANES), lambda i: (i, 0)), pl.BlockSpec((1, LANES), lambda i: (0, 0)),
                  pl.BlockSpec(memory_space=pl.ANY)],
        out_specs=[pl.BlockSpec((tm, 2 * LANES), lambda i: (i, DPB_SKV)), pl.BlockSpec((8, LANES), lambda i: (0, 0))],
        out_shape=[_sds((T, NP), MXU_DTYPE), _sds((8, LANES), F32)],
        input_output_aliases={4: 0},
        compiler_params=_cp(("arbitrary",), 32))(proj, dkn, dv, lw["skn"], dproj)


def _mla_attn_bwd(q, k, kt, vt, o, do, lse):
    T = q.shape[1]
    tk = min(TK, T // 2)
    tq = 2 * tk

    def body(q_ref, k_ref, kt_ref, vt_ref, o_ref, do_ref, lse_ref, dq_ref, dk_ref, dv_ref, dq_s, lse_s, dd_s,
             s_a, s_b, p_a, p_b):
        h = pl.program_id(0)
        i = pl.program_id(1)

        @pl.when(i == 0)
        def _():
            dk_ref[...] = jnp.zeros_like(dk_ref)
            dv_ref[...] = jnp.zeros_like(dv_ref)

        qry = lax.broadcasted_iota(jnp.int32, (tq, tk), 0)
        key = lax.broadcasted_iota(jnp.int32, (tq, tk), 1)
        own = (lax.broadcasted_iota(jnp.int32, (1, LANES), 1) // 64) == (h % 2)
        do_own = jnp.where(own, do_ref[...], 0.0)
        dob = do_own.astype(MXU_DTYPE)
        dob_t = do_own.T.astype(MXU_DTYPE)
        qh = q_ref[0]
        qh_t = qh.astype(F32).T.astype(MXU_DTYPE)
        dd_col = jnp.sum(do_own * o_ref[...], axis=-1, keepdims=True)
        lse_col = jnp.broadcast_to(lse_ref[0], (LANES, tq)).T
        for c in range(tk // LANES):
            lse_s[:, LANES * c:LANES * (c + 1)] = lse_col
            dd_s[:, LANES * c:LANES * (c + 1)] = jnp.broadcast_to(dd_col, (tq, LANES))
        dq_s[...] = jnp.zeros_like(dq_s)

        def scores(kj, s_buf, p_buf):
            s_buf[...] = _dot(qh, kt_ref[0, kj])
            p_buf[...] = _dot(dob, vt_ref[0, kj])

        def consume(kj, s_buf, p_buf, diag):
            rows = pl.ds(pl.multiple_of(kj * tk, tk), tk)
            s = s_buf[...]
            if diag is not None:
                s = jnp.where(key + diag * tk <= qry, s, NEG_INF)
            p = jnp.exp2(s - lse_s[...])
            ds = (p * (p_buf[...] - dd_s[...])).astype(MXU_DTYPE)
            dq_s[...] += _dot(ds, k_ref[0, rows, :])
            dk_ref[0, kj] += _dot(qh_t, ds)
            dv_ref[0, kj] += _dot(dob_t, p.astype(MXU_DTYPE))

        scores(0, s_a, p_a)

        def pair(kj):
            scores(kj + 1, s_b, p_b)
            consume(kj, s_a, p_a, None)
            scores(kj + 2, s_a, p_a)
            consume(kj + 1, s_b, p_b, None)

        def octet(ko, carry):
            for t in range(4):
                pair(8 * ko + 2 * t)
            return carry

        lax.fori_loop(0, i // 4, octet, 0)

        @pl.when(i % 4 >= 2)
        def _():
            pair(8 * (i // 4))
            pair(8 * (i // 4) + 2)

        @pl.when(i % 2 == 1)
        def _():
            pair(2 * i - 2)

        kl = 2 * i + 1
        s_b[tk:, :] = _dot(qh[tk:], kt_ref[0, kl])
        p_b[tk:, :] = _dot(dob[tk:], vt_ref[0, kl])
        consume(2 * i, s_a, p_a, 0)
        s = jnp.where(key[tk:] + tk <= qry[tk:], s_b[tk:, :], NEG_INF)
        p = jnp.exp2(s - lse_s[tk:, :])
        ds = (p * (p_b[tk:, :] - dd_s[tk:, :])).astype(MXU_DTYPE)
        dq_s[tk:, :] += _dot(ds, k_ref[0, pl.ds(pl.multiple_of(kl * tk, tk), tk), :])
        dk_ref[0, kl] += _dot(qh_t[:, tk:], ds)
        dv_ref[0, kl] += _dot(dob_t[:, tk:], p.astype(MXU_DTYPE))
        dq_ref[0] = dq_s[...]

    res = pl.BlockSpec((1, T, LANES), lambda h, i: (h, 0, 0))
    res_t = pl.BlockSpec((1, T // tk, LANES, tk), lambda h, i: (h, 0, 0, 0))
    buf = pltpu.VMEM((tq, tk), F32)
    acc_t = _sds((HEADS, T // tk, LANES, tk), F32)
    return pl.pallas_call(
        body, name="mla_attn_bwd", grid=(HEADS, T // tq),
        in_specs=[pl.BlockSpec((1, tq, LANES), lambda h, i: (h, i, 0)), res, res_t, res_t,
                  pl.BlockSpec((tq, LANES), lambda h, i: (i, h // 2)),
                  pl.BlockSpec((tq, LANES), lambda h, i: (i, h // 2)),
                  pl.BlockSpec((1, 1, tq), lambda h, i: (h, 0, i))],
        out_specs=[pl.BlockSpec((1, tq, LANES), lambda h, i: (h, i, 0)), res_t, res_t],
        out_shape=[_sds((HEADS, T, LANES), F32), acc_t, acc_t],
        scratch_shapes=[pltpu.VMEM((tq, LANES), F32), buf, buf, buf, buf, buf, buf],
        compiler_params=_cp(("parallel", "arbitrary"), 48))(q, k, kt, vt, o, do, lse)


def _mla_prep_bwd(proj, dq, dk, dv, lw, rope, dproj):
    T = proj.shape[0]
    tm = min(TK, T // 2)

    def body(ql_ref, kvl_ref, kr_ref, dq_ref, dk_ref, dv_ref, qa_ref, kva_ref, wq_ref, wk_ref, wv_ref,
             wqt_ref, wkt_ref, wvt_ref, qn_ref, kn_ref, c_ref, s1_ref, s2_ref, dproj_in,
             d_ref, dwq_ref, dwk_ref, dwv_ref, dqa_ref, dkva_ref, dqn_ref, dkn_ref):
        i = pl.program_id(0)

        @pl.when(i == 0)
        def _():
            for ref in (dwq_ref, dwk_ref, dwv_ref, dqa_ref, dkva_ref, dqn_ref, dkn_ref):
                ref[...] = jnp.zeros_like(ref)

        c, s1, s2 = c_ref[...], s1_ref[...], s2_ref[...]
        lane = lax.broadcasted_iota(jnp.int32, (1, LANES), 1)
        qlhat, qlr = _rms(ql_ref[...], MLA_Q_LORA)
        qn = (qlhat * qa_ref[...]).astype(MXU_DTYPE)
        kvhat, kvr = _rms(kvl_ref[...], MLA_KV_LORA)
        kvn = (kvhat * kva_ref[...]).astype(MXU_DTYPE)
        kr = kr_ref[...]
        x3, r3 = _rms(jnp.stack([_dot(qn, wq_ref[h]) for h in range(HEADS)]), MLA_QK)
        dy3 = _rope_bwd(dq_ref[...] * MLA_SCALE, c, s1, s2)
        dqw = jnp.sum(jnp.sum(dy3 * x3, axis=0), axis=0, keepdims=True)
        dx3 = _rms_bwd(dy3, x3, r3, qn_ref[...], MLA_QK).astype(MXU_DTYPE)
        dqnl = jnp.zeros((tm, MLA_Q_LORA), F32)
        for h in range(HEADS):
            dwq_ref[h] += _dot_tn(qn, dx3[h])
            dqnl = dqnl + _dot(dx3[h], wqt_ref[h])

        x3, r3 = _rms(jnp.stack([_dot(kvn, wk_ref[h]) for h in range(HEADS)]) + kr, MLA_QK)
        dy3 = _rope_bwd(jnp.stack([dk_ref[h, 0].T for h in range(HEADS)]) * LN2, c, s1, s2)
        dkw = jnp.sum(jnp.sum(dy3 * x3, axis=0), axis=0, keepdims=True)
        dxf3 = _rms_bwd(dy3, x3, r3, kn_ref[...], MLA_QK)
        dkr = jnp.sum(dxf3, axis=0)
        dx3 = dxf3.astype(MXU_DTYPE)
        dkvn = jnp.zeros((tm, MLA_KV_LORA), F32)
        for h in range(HEADS):
            dwk_ref[h] += _dot_tn(kvn, dx3[h])
            dkvn = dkvn + _dot(dx3[h], wkt_ref[h])
        dvc = jnp.concatenate([(dv_ref[2 * j, 0] + dv_ref[2 * j + 1, 0]).T for j in range(4)],
                              axis=1).astype(MXU_DTYPE)
        dwv_ref[...] += _dot_tn(kvn, dvc)
        dkvn = dkvn + _dot(dvc, wvt_ref[...])
        dqa_ref[...] += _row0(jnp.sum(dqnl * qlhat, axis=0, keepdims=True))
        dkva_ref[...] += _row0(jnp.sum(dkvn * kvhat, axis=0, keepdims=True))
        dqn_ref[...] += _row0(dqw)
        dkn_ref[...] += _row0(dkw)
        d_ref[:, 0:256] = _rms_bwd(dqnl, qlhat, qlr, qa_ref[...], MLA_Q_LORA).astype(MXU_DTYPE)
        d_ref[:, 256:384] = _rms_bwd(dkvn, kvhat, kvr, kva_ref[...], MLA_KV_LORA).astype(MXU_DTYPE)
        d_ref[:, 384:512] = jnp.where((lane >= 64) & (lane < 96), dkr, 0.0).astype(MXU_DTYPE)

    full = lambda shape: pl.BlockSpec(shape, lambda i: (0,) * len(shape))
    hd = pl.BlockSpec((HEADS, tm, LANES), lambda i: (0, i, 0))
    hdt = pl.BlockSpec((HEADS, 1, LANES, tm), lambda i: (0, i, 0, 0))
    tab = pl.BlockSpec((tm, LANES), lambda i: (i, 0))
    return pl.pallas_call(
        body, name="mla_prep_bwd", grid=(T // tm,),
        in_specs=[pl.BlockSpec((tm, 256), lambda i: (i, CB_QLAT)), pl.BlockSpec((tm, LANES), lambda i: (i, CB_KVLAT)),
                  pl.BlockSpec((tm, LANES), lambda i: (i, CB_KROPE)), hd, hdt, hdt,
                  full((1, 256)), full((1, LANES)), full((HEADS, 256, LANES)), full((HEADS, LANES, LANES)),
                  full((LANES, 512)), full((HEADS, LANES, 256)), full((HEADS, LANES, LANES)), full((512, LANES)),
                  full((1, LANES)), full((1, LANES)), tab, tab, tab, pl.BlockSpec(memory_space=pl.ANY)],
        out_specs=[pl.BlockSpec((tm, 512), lambda i: (i, DPB_MLA)), full((HEADS, 256, LANES)),
                   full((HEADS, LANES, LANES)), full((LANES, 512)), full((8, 256)), full((8, LANES)),
                   full((8, LANES)), full((8, LANES))],
        out_shape=[_sds((T, NP), MXU_DTYPE), _sds((HEADS, 256, LANES), F32), _sds((HEADS, LANES, LANES), F32),
                   _sds((LANES, 512), F32), _sds((8, 256), F32), _sds((8, LANES), F32), _sds((8, LANES), F32),
                   _sds((8, LANES), F32)],
        input_output_aliases={19: 0},
        compiler_params=_cp(("arbitrary",), 48))(
            proj, proj, proj, dq, dk, dv, lw["qa"], lw["kva"], lw["wq"], lw["wk"], lw["wv"],
            lw["wqt"], lw["wkt"], lw["wvt"], lw["qn"], lw["kn"], rope[0], rope[1], rope[2], dproj)


def _inproj_bwd_dx(dproj, wpt, x, g_in, ng):
    T, D = x.shape
    tm = min(TM_PROJ, T)

    def body(dp_ref, wt_ref, x_ref, g_ref, w_ref, dx_ref, dw_ref):
        i = pl.program_id(0)

        @pl.when(i == 0)
        def _():
            dw_ref[...] = jnp.zeros_like(dw_ref)

        dh = _dot(dp_ref[...], wt_ref[...])
        xhat, r = _rms(x_ref[...], D)
        dw_ref[...] += _row0(jnp.sum(dh * xhat, axis=0, keepdims=True))
        dx_ref[...] = g_ref[...] + _rms_bwd(dh, xhat, r, w_ref[...], D)

    tile = pl.BlockSpec((tm, D), lambda i: (i, 0))
    return pl.pallas_call(
        body, name="inproj_bwd_dx", grid=(T // tm,),
        in_specs=[pl.BlockSpec((tm, NP), lambda i: (i, 0)), pl.BlockSpec((NP, D), lambda i: (0, 0)), tile, tile,
                  pl.BlockSpec((1, D), lambda i: (0, 0))],
        out_specs=[tile, pl.BlockSpec((8, D), lambda i: (0, 0))],
        out_shape=[_sds((T, D), F32), _sds((8, D), F32)],
        compiler_params=_cp(("arbitrary",), 48))(dproj, wpt, x, g_in, ng)


def _rope_tables(T, token=0.0):
    half = MLA_ROPE // 2
    inv_freq = jnp.power(jnp.float32(ROPE_THETA), -jnp.arange(half, dtype=F32) / half)
    z = lambda n: jnp.zeros((n,), F32)
    freq = jnp.concatenate([z(MLA_NOPE), inv_freq, inv_freq, z(32)])
    first = jnp.concatenate([z(64), jnp.ones((16,), F32), z(48)])
    second = jnp.concatenate([z(80), jnp.ones((16,), F32), z(32)])
    ang = (jnp.arange(T, dtype=F32) + token)[:, None] * freq[None, :]
    sin = jnp.sin(ang)
    return jnp.cos(ang), -sin * first[None, :], sin * second[None, :]


def _pad_lanes(v, n=LANES):
    v = v.reshape(1, -1)
    return jnp.pad(v, ((0, 0), (0, n - v.shape[1])))


def _pack_win_t(wt):
    z = lambda n: jnp.zeros((n, wt.shape[1]), wt.dtype)
    return jnp.concatenate([wt[416:2976], wt[3744:4256], wt[0:384], z(64), wt[384:416], z(32), wt[2976:3488],
                            wt[3488:3616], wt[3616:3744]], axis=0)


def _unpack_dwin(d):
    return jnp.concatenate([d[:, 3072:3456], d[:, 3520:3552], d[:, 0:2560], d[:, 3584:4096], d[:, 4096:4224],
                            d[:, 4224:4352], d[:, 2560:3072]], axis=1)


def _inproj_weights(l, norm_g, w_in_t):
    wpt = _pack_win_t(w_in_t)
    return dict(ng=norm_g[l].reshape(1, -1), wp=wpt.T, wpt=wpt)


def _mixer_weights(l, qa, wqb_full, kva, wkvb_full, qn, kn, conv_full, sqn, skn, sinks, w_out_full):
    wq = jnp.pad(wqb_full, ((0, 0), (0, 0), (0, LANES - MLA_QK)))
    wk = jnp.pad(wkvb_full[:, :, :MLA_NOPE], ((0, 0), (0, 0), (0, LANES - MLA_NOPE)))
    wv = jnp.transpose(wkvb_full[:, :, MLA_NOPE:], (1, 0, 2)).reshape(MLA_KV_LORA, GROUP_WIDTH)
    return dict(
        qa=qa[l].reshape(1, -1), kva=kva[l].reshape(1, -1),
        wq=wq, wk=wk, wv=wv, wqt=jnp.transpose(wq, (0, 2, 1)), wkt=jnp.transpose(wk, (0, 2, 1)), wvt=wv.T,
        qn=_pad_lanes(qn[l]), kn=_pad_lanes(kn[l]),
        conv=jnp.pad(conv_full, ((0, 5), (0, 0))),
        sqn=jnp.tile(sqn[l].reshape(1, -1), (1, 2)), skn=jnp.tile(skn[l].reshape(1, -1), (1, 2)),
        sinks=sinks[l], wo=w_out_full, wot=w_out_full.T)


def _layer_fwd(x, lw, rope, late_weights=None, target=None):
    proj, h = _inproj_fwd(x, lw["ng"], lw["wp"])
    if late_weights is not None:
        lw = dict(lw, **late_weights(proj))
    q, k, kt, vt = _mla_prep_fwd(proj, lw, rope)
    o_mla, lse = _mla_attn_fwd(q, k, vt)
    o_swa = _swa_fwd(proj, lw)
    ycat = _mix_fwd(proj, o_mla, o_swa, lw["conv"])
    if target is None:
        out = _mm_nn(ycat, lw["wo"], "outproj_fwd", residual=x)
    else:
        out = _outproj_loss(ycat, lw["wo"], x, target)
    return out, dict(x=x, proj=proj, h=h, q=q, k=k, kt=kt, vt=vt, o_mla=o_mla, lse=lse, o_swa=o_swa, ycat=ycat,
                     lw=lw)


def _layer_bwd(g, sv, lw, rope, on_big_grads=None):
    proj = sv["proj"]
    dycat, d_wo = _outproj_bwd(g, sv["ycat"], lw["wot"])
    dproj, do_mla, do_swa, d_conv = _mix_bwd(dycat, proj, sv["o_mla"], sv["o_swa"], lw["conv"])
    dproj, dkn_acc, dv_acc, d_sqn, d_sinks = _swa_bwd(proj, sv["o_swa"], do_swa, lw, dproj)
    dproj, d_skn = _swa_kv_bwd(proj, dkn_acc, dv_acc, lw, dproj)
    dq, dk, dv = _mla_attn_bwd(sv["q"], sv["k"], sv["kt"], sv["vt"], sv["o_mla"], do_mla, sv["lse"])
    dproj, d_wq, d_wk, d_wv, d_qa, d_kva, d_qn, d_kn = _mla_prep_bwd(proj, dq, dk, dv, lw, rope, dproj)
    grads = dict(
        w_out=d_wo, w_qb=d_wq[:, :, :MLA_QK],
        w_kvb=jnp.concatenate([d_wk[:, :, :MLA_NOPE],
                               jnp.transpose(d_wv.reshape(MLA_KV_LORA, HEADS, MLA_NOPE), (1, 0, 2))], axis=2))
    token = 0.0 if on_big_grads is None else on_big_grads("mixer", grads)
    d_wp = _mm_tn(sv["h"], dproj, "inproj_bwd_dw", WIRE_DTYPE, tn=NP // 2)
    grads["w_in"] = _unpack_dwin(d_wp)
    token = token if on_big_grads is None else token + on_big_grads("w_in", grads)
    dx, d_ng = _inproj_bwd_dx(dproj, lw["wpt"], sv["x"], g, lw["ng"] + token)
    grads.update(
        conv=d_conv[0:3], norm_g=d_ng[0], qa=d_qa[0], kva=d_kva[0], qn=d_qn[0, :MLA_QK], kn=d_kn[0, :MLA_QK],
        sqn=d_sqn[0, :SWA_HEAD_DIM], skn=d_skn[0, :SWA_HEAD_DIM], sinks=d_sinks[:, 0])
    return dx, grads


def _my_coords():
    return lax.axis_index("x"), lax.axis_index("y"), lax.axis_index("c")


def _peer(me, k):
    x, y, c = me
    return (1 - x if k & 4 else x, 1 - y if k & 2 else y, 1 - c if k & 1 else c)


def _lin(d):
    return 4 * d[0] + 2 * d[1] + d[2]


def _push_copies(ins, lands, send_sems, recv_sems, gather, incoming=False):
    me = _my_coords()
    my = _lin(me)
    copies = []
    for a in range(len(ins)):
        for k in range(1, N_DEV):
            peer = _peer(me, k)
            src = ins[a] if gather else ins[a].at[_lin(peer)]
            copies.append(pltpu.make_async_remote_copy(
                src_ref=src, dst_ref=lands[a].at[_lin(peer) if incoming else my],
                send_sem=send_sems.at[a * 7 + k - 1], recv_sem=recv_sems.at[a * 7 + k - 1],
                device_id=peer, device_id_type=pl.DeviceIdType.MESH))
    return copies


def _push_start(arrays, name, gather):
    n = len(arrays)
    land_shapes = [((N_DEV,) + a.shape) if gather else a.shape for a in arrays]

    def body(*refs):
        ins, lands = refs[:n], refs[n:2 * n]
        send_sems, recv_sems = refs[2 * n], refs[2 * n + 1]
        token = refs[-1]
        for cp in _push_copies(ins, lands, send_sems, recv_sems, gather):
            cp.start()
        token[...] = jnp.zeros_like(token)

    hbm = pl.BlockSpec(memory_space=pltpu.HBM)
    sem = pl.BlockSpec(memory_space=pltpu.SEMAPHORE)
    res = pl.pallas_call(
        body, name=name,
        out_shape=(pltpu.SemaphoreType.DMA((7 * n,)), pltpu.SemaphoreType.DMA((7 * n,)),
                   *[pltpu.HBM(a.shape, a.dtype) for a in arrays],
                   *[pltpu.HBM(s, a.dtype) for s, a in zip(land_shapes, arrays)],
                   _sds((8, LANES), F32)),
        in_specs=(hbm,) * (2 * n),
        out_specs=(sem, sem) + (hbm,) * (2 * n) + (pl.BlockSpec(memory_space=pltpu.VMEM),),
        input_output_aliases={i: 2 + i for i in range(2 * n)},
        compiler_params=pltpu.CompilerParams(has_side_effects=pltpu.SideEffectType.DATAFLOW_SIDE_EFFECTING),
    )(*[pltpu.with_memory_space_constraint(a, pltpu.HBM) for a in arrays],
      *[pltpu.with_memory_space_constraint(lax.empty(s, a.dtype), pltpu.HBM) for s, a in zip(land_shapes, arrays)])
    return dict(send=res[0], recv=res[1], src=res[2:2 + n], land=res[2 + n:2 + 2 * n], token=res[-1][0, 0],
                gather=gather)


def _push_wait(handle, after, name):
    n = len(handle["src"])
    gather = handle["gather"]

    def body(*refs):
        ins, lands = refs[:n], refs[n:2 * n]
        send_sems, recv_sems = refs[2 * n], refs[2 * n + 1]
        for cp in _push_copies(ins, lands, send_sems, recv_sems, gather):
            cp.wait_send()
        for cp in _push_copies(ins, lands, send_sems, recv_sems, gather, incoming=True):
            cp.wait_recv()

    hbm = pl.BlockSpec(memory_space=pltpu.HBM)
    sem = pl.BlockSpec(memory_space=pltpu.SEMAPHORE)
    res = pl.pallas_call(
        body, name=name,
        out_shape=tuple(pltpu.HBM(a.shape, a.dtype) for a in (*handle["src"], *handle["land"])),
        in_specs=(hbm,) * (2 * n) + (sem, sem, pl.BlockSpec(memory_space=pl.ANY)),
        out_specs=(hbm,) * (2 * n),
        input_output_aliases={i: i for i in range(2 * n)},
        compiler_params=pltpu.CompilerParams(has_side_effects=pltpu.SideEffectType.DATAFLOW_SIDE_EFFECTING),
    )(*handle["src"], *handle["land"], handle["send"], handle["recv"], after)
    return res[n:]


def _small_all_reduce(v):
    R = v.shape[0]

    def body(v_ref, o_ref, buf, send_sems, recv_sems):
        me = _my_coords()
        my = _lin(me)
        sends = []
        for k in range(1, N_DEV):
            cp = pltpu.make_async_remote_copy(
                src_ref=v_ref, dst_ref=buf.at[my], send_sem=send_sems.at[k - 1], recv_sem=recv_sems.at[k - 1],
                device_id=_peer(me, k), device_id_type=pl.DeviceIdType.MESH)
            cp.start()
            sends.append(cp)
        buf[my] = v_ref[...]
        for k in range(1, N_DEV):
            pltpu.make_async_remote_copy(
                src_ref=v_ref, dst_ref=buf.at[_lin(_peer(me, k))], send_sem=send_sems.at[k - 1],
                recv_sem=recv_sems.at[k - 1], device_id=_peer(me, k),
                device_id_type=pl.DeviceIdType.MESH).wait_recv()
        for cp in sends:
            cp.wait_send()
        tot = buf[0]
        for d in range(1, N_DEV):
            tot = tot + buf[d]
        o_ref[...] = tot

    vm = pl.BlockSpec(memory_space=pltpu.VMEM)
    return pl.pallas_call(
        body, name="small_all_reduce", in_specs=[vm], out_specs=vm, out_shape=_sds(v.shape, F32),
        scratch_shapes=[pltpu.VMEM((N_DEV, R, LANES), F32), pltpu.SemaphoreType.DMA((7,)),
                        pltpu.SemaphoreType.DMA((7,))],
    )(v)


def _adamw_math(w, g, m, v):
    m = ADAM_B1 * m + (1.0 - ADAM_B1) * g
    v = ADAM_B2 * v + (1.0 - ADAM_B2) * (g * g)
    m_hat = m / (1.0 - ADAM_B1 ** ADAM_STEP)
    v_hat = v / (1.0 - ADAM_B2 ** ADAM_STEP)
    delta = -ADAM_LR * (m_hat / (jnp.sqrt(v_hat) + ADAM_EPS) + ADAM_WD * w)
    return delta, m, v


def _adamw(parts, w, m, v, name, tr):
    P, R, C = parts.shape
    tr = min(tr, R)

    def body(p_ref, w_ref, m_ref, v_ref, g_out, d_out, m_out, v_out):
        g = p_ref[0].astype(F32)
        for d in range(1, P):
            g = g + p_ref[d].astype(F32)
        delta, m_new, v_new = _adamw_math(w_ref[...], g, m_ref[...], v_ref[...])
        g_out[...] = g
        d_out[...] = delta
        m_out[...] = m_new
        v_out[...] = v_new

    tile = pl.BlockSpec((tr, C), lambda i: (i, 0))
    return pl.pallas_call(
        body, name=name, grid=(R // tr,),
        in_specs=[pl.BlockSpec((P, tr, C), lambda i: (0, i, 0)), tile, tile, tile],
        out_specs=[tile] * 4, out_shape=[_sds((R, C), F32)] * 4,
        compiler_params=_cp(("parallel",), 32))(parts, w, m, v)


SMALL = (("norm_g", D_MODEL), ("mla_q_a_norm", MLA_Q_LORA), ("mla_kv_a_norm", MLA_KV_LORA), ("mla_q_norm", MLA_QK),
         ("mla_k_norm", MLA_QK), ("swa_q_norm", SWA_HEAD_DIM), ("swa_k_norm", SWA_HEAD_DIM), ("swa_sinks", HEADS))
SMALL_GRAD_KEY = dict(norm_g="norm_g", mla_q_a_norm="qa", mla_kv_a_norm="kva", mla_q_norm="qn", mla_k_norm="kn",
                      swa_q_norm="sqn", swa_k_norm="skn", swa_sinks="sinks")
SMALL_ROWS = 32
CONV_ROWS = 24


def _pack_small(get):
    parts = []
    for l in range(DEPTH):
        for name, n in SMALL:
            v = get(name, l).reshape(-1)
            parts.append(jnp.pad(v, (0, (-n) % LANES)))
    return jnp.concatenate(parts).reshape(SMALL_ROWS, LANES)


def _unpack_small(packed):
    flat = packed.reshape(-1)
    out = {name: [] for name, _ in SMALL}
    off = 0
    for l in range(DEPTH):
        for name, n in SMALL:
            out[name].append(flat[off:off + n])
            off += n + (-n) % LANES
    return {name: jnp.stack(v) for name, v in out.items()}


def kernel(x, norm_g, w_in, mla_q_a_norm, mla_w_qb, mla_kv_a_norm, mla_w_kvb, mla_q_norm, mla_k_norm, conv_w, swa_q_norm, swa_k_norm, swa_sinks, w_out, loss_target, m_norm_g, m_w_in, m_mla_q_a_norm, m_mla_w_qb, m_mla_kv_a_norm, m_mla_w_kvb, m_mla_q_norm, m_mla_k_norm, m_conv_w, m_swa_q_norm, m_swa_k_norm, m_swa_sinks, m_w_out, v_norm_g, v_w_in, v_mla_q_a_norm, v_mla_w_qb, v_mla_kv_a_norm, v_mla_w_kvb, v_mla_q_norm, v_mla_k_norm, v_conv_w, v_swa_q_norm, v_swa_k_norm, v_swa_sinks, v_w_out):
    T = x.shape[1]
    weights = dict(norm_g=norm_g, w_in=w_in, mla_q_a_norm=mla_q_a_norm, mla_w_qb=mla_w_qb,
                   mla_kv_a_norm=mla_kv_a_norm, mla_w_kvb=mla_w_kvb, mla_q_norm=mla_q_norm, mla_k_norm=mla_k_norm,
                   conv_w=conv_w, swa_q_norm=swa_q_norm, swa_k_norm=swa_k_norm, swa_sinks=swa_sinks, w_out=w_out)
    mom_m = dict(norm_g=m_norm_g, w_in=m_w_in, mla_q_a_norm=m_mla_q_a_norm, mla_w_qb=m_mla_w_qb,
                 mla_kv_a_norm=m_mla_kv_a_norm, mla_w_kvb=m_mla_w_kvb, mla_q_norm=m_mla_q_norm,
                 mla_k_norm=m_mla_k_norm, conv_w=m_conv_w, swa_q_norm=m_swa_q_norm, swa_k_norm=m_swa_k_norm,
                 swa_sinks=m_swa_sinks, w_out=m_w_out)
    mom_v = dict(norm_g=v_norm_g, w_in=v_w_in, mla_q_a_norm=v_mla_q_a_norm, mla_w_qb=v_mla_w_qb,
                 mla_kv_a_norm=v_mla_kv_a_norm, mla_w_kvb=v_mla_w_kvb, mla_q_norm=v_mla_q_norm,
                 mla_k_norm=v_mla_k_norm, conv_w=v_conv_w, swa_q_norm=v_swa_q_norm, swa_k_norm=v_swa_k_norm,
                 swa_sinks=v_swa_sinks, w_out=v_w_out)

    my = _lin(_my_coords())

    def shards(l):
        return [w_in[l].astype(MXU_DTYPE).T, mla_w_qb[l].astype(MXU_DTYPE), mla_w_kvb[l].astype(MXU_DTYPE),
                w_out[l].astype(MXU_DTYPE), conv_w[l]]

    def inproj_weights(l, g_win_t):
        return _inproj_weights(l, norm_g, g_win_t.reshape(IN_COLS, D_MODEL))

    def mixer_weights(l, gathered):
        g_wqb, g_wkvb, g_wout, g_conv = gathered
        return _mixer_weights(
            l, mla_q_a_norm, g_wqb, mla_kv_a_norm, g_wkvb, mla_q_norm, mla_k_norm,
            jnp.transpose(g_conv, (1, 0, 2)).reshape(3, GROUP_WIDTH), swa_q_norm, swa_k_norm, swa_sinks,
            g_wout.reshape(D_MIX, D_MODEL))

    slot_of = dict(
        w_in=lambda g: jnp.transpose(g["w_in"].reshape(D_MODEL, N_DEV, IN_COLS // N_DEV), (1, 0, 2)),
        w_out=lambda g: g["w_out"].reshape(N_DEV, D_MIX // N_DEV, D_MODEL),
        w_qb=lambda g: g["w_qb"], w_kvb=lambda g: g["w_kvb"])

    def own_slot(landed, mine):
        return [lax.dynamic_update_index_in_dim(a, m, my, 0) for a, m in zip(landed, mine)]

    def landed(handle, after, name, mine):
        return own_slot(_push_wait(handle, after, name), mine)

    sh = [shards(0), shards(1)]
    gather_in0 = _push_start(sh[0][:1], "weight_gather_in0_start", gather=True)
    rope = _rope_tables(T, gather_in0["token"])
    big_shapes = dict(w_in=(DEPTH * D_MODEL, IN_COLS // N_DEV), w_out=(DEPTH * D_MIX // N_DEV, D_MODEL),
                      mla_w_qb=(DEPTH * MLA_Q_LORA, MLA_QK), mla_w_kvb=(DEPTH * MLA_KV_LORA, 128))
    pad_conv = lambda a: jnp.pad(a.reshape(-1), (0, 8 * LANES - 6 * 64)).reshape(8, LANES)
    cat = lambda src: jnp.concatenate([_pack_small(lambda name, l: src[name][l]), pad_conv(src["conv_w"])], axis=0)
    adam_in = {name: [src[name].reshape(shape) for src in (weights, mom_m, mom_v)]
               for name, shape in big_shapes.items()}
    adam_in["small"] = [cat(weights), cat(mom_m), cat(mom_v)]
    rope0, adam_in, casts = lax.optimization_barrier((rope[0], adam_in, [sh[0][1:4], sh[1][:4]]))
    sh = [sh[0][:1] + casts[0] + [conv_w[0]], casts[1] + [conv_w[1]]]
    w_in0_t = landed(gather_in0, rope0, "weight_gather_in0_wait", sh[0][:1])[0]
    w_in0_t, conv0 = lax.optimization_barrier((w_in0_t, conv_w[0]))
    gather0 = _push_start(sh[0][1:4] + [conv0], "weight_gather0_start", gather=True)
    lw0 = inproj_weights(0, w_in0_t)
    lw0 = dict(lw0, ng=lw0["ng"] + gather0["token"])
    layer1 = {}

    def mixer0(proj):
        got = landed(gather0, proj, "weight_gather0_wait", sh[0][1:])
        got[0], conv1 = lax.optimization_barrier((got[0], conv_w[1]))
        layer1["gather"] = _push_start(sh[1][:4] + [conv1], "weight_gather1_start", gather=True)
        mw = mixer_weights(0, got)
        return dict(mw, qa=mw["qa"] + layer1["gather"]["token"])

    x1, sv0 = _layer_fwd(x[0], lw0, rope, late_weights=mixer0)
    g1_all = landed(layer1["gather"], x1, "weight_gather1_wait", sh[1])
    (g2, loss_tile), sv1 = _layer_fwd(x1, dict(inproj_weights(1, g1_all[0]), **mixer_weights(1, g1_all[1:])), rope,
                                      target=loss_target[0])

    parts = {(1, "w_in"): ("w_in", "w_out", "w_qb", "w_kvb"), (0, "mixer"): ("w_out", "w_qb", "w_kvb"),
             (0, "w_in"): ("w_in",)}
    started = []

    def start_exchange(l, part, g):
        if (l, part) not in parts:
            return 0.0
        sl = [slot_of[n](g) for n in parts[(l, part)]]
        handle = _push_start(sl, "grad_exchange%d_%s_start" % (l, part), gather=False)
        started.append((l, part, sl, handle))
        return handle["token"]

    g1, grads1 = _layer_bwd(g2, sv1, sv1["lw"], rope, on_big_grads=functools.partial(start_exchange, 1))
    lw0b = dict(sv0["lw"], conv=sv0["lw"]["conv"] + started[0][3]["token"])
    grad_x, grads0 = _layer_bwd(g1, sv0, lw0b, rope, on_big_grads=functools.partial(start_exchange, 0))
    recv = {}

    def receive(l, part, sl, handle, after):
        got = landed(handle, after, "grad_exchange%d_%s_wait" % (l, part), [s[my] for s in sl])
        recv.update({(l, n): a for n, a in zip(parts[(l, part)], got)})

    for entry in started[:-1]:
        receive(*entry, after=grad_x)
    grads = [grads0, grads1]
    stacked = lambda n: jnp.stack([recv[(0, n)], recv[(1, n)]], axis=1)

    small = jnp.concatenate([
        _pack_small(lambda name, l: grads[l][SMALL_GRAD_KEY[name]]),
        jnp.stack([g["conv"] for g in grads]).reshape(CONV_ROWS, LANES),
        loss_tile], axis=0)
    small = _small_all_reduce(small)
    loss = small[SMALL_ROWS + CONV_ROWS, 0]
    my = _lin(_my_coords())
    conv_g = lax.dynamic_slice_in_dim(small[SMALL_ROWS:SMALL_ROWS + CONV_ROWS].reshape(DEPTH, 3, GROUP_WIDTH),
                                      my * 64, 64, axis=2)

    out = {}

    def big(name, recv, tr):
        res = _adamw(recv.reshape((N_DEV,) + big_shapes[name]), *adam_in[name], "adamw_" + name, tr)
        out[name] = [r.reshape(weights[name].shape) for r in res]

    big("w_out", stacked("w_out"), 192)
    big("mla_w_qb", stacked("w_qb"), 512)
    big("mla_w_kvb", stacked("w_kvb"), 256)
    receive(*started[-1], after=out["w_out"][1])
    big("w_in", stacked("w_in"), 256)

    g_small = jnp.concatenate([small[:SMALL_ROWS], pad_conv(conv_g)], axis=0)
    res = _adamw(g_small[None], *adam_in["small"], "adamw_small", SMALL_ROWS + 8)
    smalls = [_unpack_small(r[:SMALL_ROWS]) for r in res]
    for name, _ in SMALL:
        out[name] = [s[name] for s in smalls]
    out["conv_w"] = [r[SMALL_ROWS:].reshape(-1)[:6 * 64].reshape(DEPTH, 3, 64) for r in res]

    order = ["norm_g", "w_in", "mla_q_a_norm", "mla_w_qb", "mla_kv_a_norm", "mla_w_kvb", "mla_q_norm", "mla_k_norm",
             "conv_w", "swa_q_norm", "swa_k_norm", "swa_sinks", "w_out"]
    result = [loss, grad_x[None]]
    for idx in range(4):
        result += [out[name][idx] for name in order]
    return tuple(result)
```

```python
import functools

import jax
import jax.numpy as jnp
import numpy as np
from jax import lax
from jax.experimental import pallas as pl
from jax.experimental.pallas import tpu as pltpu

F32 = jnp.float32
MXU_DTYPE = jnp.bfloat16
WIRE_DTYPE = jnp.bfloat16

N_DEV = 8
DEPTH = 2
D_MODEL = 1024
GROUP_WIDTH = 512
D_MIX = 3 * GROUP_WIDTH
BLOCK = 128
RMS_EPS = 1e-6
NEG_INF = -1e30
HEADS = 8
MLA_QK = 96
MLA_NOPE = 64
MLA_ROPE = 32
MLA_Q_LORA = 256
MLA_KV_LORA = 128
ROPE_THETA = 10000.0
SWA_HEAD_DIM = 64
LANES = 128
IN_COLS = 4256

ADAM_LR = 0.001
ADAM_B1 = 0.9
ADAM_B2 = 0.999
ADAM_EPS = 1e-08
ADAM_WD = 0.01
ADAM_STEP = 10

NP = 4352
CB_GMLA, CB_CH, CB_CB, CB_CC, CB_GCONV, CB_GSWA, CB_SQ = 0, 1, 2, 3, 4, 5, 7
CB_QLAT = 12
CB_KVLAT, CB_KROPE = 26, 27
CB_SK, CB_SV = 32, 33
DPB_MIX, DPB_MLA, DPB_SQ, DPB_SKV = 0, 6, 7, 16

TM_PROJ = 512
TM_ROW = 256
TK = 256
TQ = 2 * TK
MLA_SCALE = MLA_QK ** -0.5
MLA_ONES_ROW = (64, 0)
LOG2E = 1.4426950408889634
LN2 = 0.6931471805599453
TM_SWA = 512
VMEM_MB = 2 ** 20


def _cp(sem, vmem_mb):
    return pltpu.CompilerParams(dimension_semantics=sem, vmem_limit_bytes=vmem_mb * VMEM_MB)


def _sds(shape, dtype):
    return jax.ShapeDtypeStruct(shape, dtype)


def _dot(a, b):
    return jnp.dot(a, b, preferred_element_type=F32)


def _dot_nt(a, b):
    return lax.dot_general(a, b, (((1,), (1,)), ((), ())), preferred_element_type=F32)


def _dot_tn(a, b):
    return lax.dot_general(a, b, (((0,), (0,)), ((), ())), preferred_element_type=F32)


def _rms(x, n):
    r = lax.rsqrt(jnp.sum(x * x, axis=-1, keepdims=True) * (1.0 / n) + RMS_EPS)
    return x * r, r


def _rms_bwd(dy, xhat, r, w, n):
    g = dy * w
    return r * (g - xhat * (jnp.sum(g * xhat, axis=-1, keepdims=True) * (1.0 / n)))


def _rms_halves(x, half1):
    x2 = x * x
    s0 = jnp.sum(jnp.where(half1, 0.0, x2), axis=-1, keepdims=True)
    s1 = jnp.sum(jnp.where(half1, x2, 0.0), axis=-1, keepdims=True)
    r = jnp.where(half1, lax.rsqrt(s1 * (1.0 / 64) + RMS_EPS), lax.rsqrt(s0 * (1.0 / 64) + RMS_EPS))
    return x * r, r


def _rms_halves_bwd(dy, xhat, r, w, half1):
    g = dy * w
    t = g * xhat
    m0 = jnp.sum(jnp.where(half1, 0.0, t), axis=-1, keepdims=True) * (1.0 / 64)
    m1 = jnp.sum(jnp.where(half1, t, 0.0), axis=-1, keepdims=True) * (1.0 / 64)
    return r * (g - xhat * jnp.where(half1, m1, m0))


def _sigmoid(x):
    return 1.0 / (1.0 + jnp.exp(-x))


def _rope(x, c, s1, s2):
    ax = x.ndim - 1
    return x * c + pltpu.roll(x, 112, ax) * s1 + pltpu.roll(x, 16, ax) * s2


def _rope_bwd(dy, c, s1, s2):
    ax = dy.ndim - 1
    return dy * c + pltpu.roll(dy * s1, 16, ax) + pltpu.roll(dy * s2, 112, ax)


def _fold_rows8(x):
    return jnp.sum(x.reshape(x.shape[0] // 8, 8, x.shape[1]), axis=0)


def _row0(v, rows=8):
    row = lax.broadcasted_iota(jnp.int32, (rows, v.shape[1]), 0)
    return jnp.where(row == 0, jnp.broadcast_to(v, (rows, v.shape[1])), 0.0)


def _mm_nn(a, b, name, out_dtype=F32, residual=None, tm=TM_PROJ):
    M, K = a.shape
    N = b.shape[1]
    tm = min(tm, M)

    def body(*refs):
        if residual is None:
            a_ref, b_ref, o_ref = refs
            acc = _dot(a_ref[...].astype(MXU_DTYPE), b_ref[...])
        else:
            a_ref, b_ref, r_ref, o_ref = refs
            acc = _dot(a_ref[...].astype(MXU_DTYPE), b_ref[...]) + r_ref[...]
        o_ref[...] = acc.astype(out_dtype)

    in_specs = [pl.BlockSpec((tm, K), lambda i: (i, 0)), pl.BlockSpec((K, N), lambda i: (0, 0))]
    args = [a, b]
    if residual is not None:
        in_specs.append(pl.BlockSpec((tm, N), lambda i: (i, 0)))
        args.append(residual)
    return pl.pallas_call(
        body, name=name, grid=(M // tm,), in_specs=in_specs,
        out_specs=pl.BlockSpec((tm, N), lambda i: (i, 0)), out_shape=_sds((M, N), out_dtype),
        compiler_params=_cp(("parallel",), 48))(*args)


def _mm_tn(a, b, name, out_dtype, tn, tk=1024):
    T, M = a.shape
    N = b.shape[1]
    tk = min(tk, T)
    nk = T // tk

    def body(a_ref, b_ref, o_ref, acc_ref):
        k = pl.program_id(1)

        @pl.when(k == 0)
        def _():
            acc_ref[...] = jnp.zeros_like(acc_ref)

        acc_ref[...] += _dot_tn(a_ref[...].astype(MXU_DTYPE), b_ref[...].astype(MXU_DTYPE))

        @pl.when(k == nk - 1)
        def _():
            o_ref[...] = acc_ref[...].astype(out_dtype)

    return pl.pallas_call(
        body, name=name, grid=(N // tn, nk),
        in_specs=[pl.BlockSpec((tk, M), lambda n, k: (k, 0)), pl.BlockSpec((tk, tn), lambda n, k: (k, n))],
        out_specs=pl.BlockSpec((M, tn), lambda n, k: (0, n)), out_shape=_sds((M, N), out_dtype),
        scratch_shapes=[pltpu.VMEM((M, tn), F32)],
        compiler_params=_cp(("parallel", "arbitrary"), 48))(a, b)


def _inproj_fwd(x, ng, wp):
    T, D = x.shape
    tm = min(TM_PROJ, T)

    def body(x_ref, g_ref, w_ref, proj_ref, h_ref):
        xhat, _ = _rms(x_ref[...], D)
        h = (xhat * g_ref[...]).astype(MXU_DTYPE)
        h_ref[...] = h
        proj_ref[...] = _dot(h, w_ref[...])

    return pl.pallas_call(
        body, name="inproj_fwd", grid=(T // tm,),
        in_specs=[pl.BlockSpec((tm, D), lambda i: (i, 0)), pl.BlockSpec((1, D), lambda i: (0, 0)),
                  pl.BlockSpec((D, NP), lambda i: (0, 0))],
        out_specs=[pl.BlockSpec((tm, NP), lambda i: (i, 0)), pl.BlockSpec((tm, D), lambda i: (i, 0))],
        out_shape=[_sds((T, NP), F32), _sds((T, D), MXU_DTYPE)],
        compiler_params=_cp(("parallel",), 48))(x, ng, wp)


def _mla_prep_fwd(proj, lw, rope):
    T = proj.shape[0]
    tk = min(TK, T // 2)
    nsub = 2
    tm = nsub * tk

    def body(ql_ref, kvl_ref, kr_ref, qa_ref, kva_ref, wq_ref, wk_ref, wv_ref, qn_ref, kn_ref,
             c_ref, s1_ref, s2_ref, q_out, k_out, kt_out, vt_out):
        c, s1, s2 = c_ref[...], s1_ref[...], s2_ref[...]
        qhat, _ = _rms(ql_ref[...], MLA_Q_LORA)
        qn = (qhat * qa_ref[...]).astype(MXU_DTYPE)
        khat, _ = _rms(kvl_ref[...], MLA_KV_LORA)
        kvn = (khat * kva_ref[...]).astype(MXU_DTYPE)
        kr = kr_ref[...]
        half1 = lax.broadcasted_iota(jnp.int32, (tm, LANES), 1) >= 64
        ones_row = lax.broadcasted_iota(jnp.int32, (LANES, 1), 0)
        q3, _ = _rms(jnp.stack([_dot(qn, wq_ref[h]) for h in range(HEADS)]), MLA_QK)
        q_out[...] = (_rope(q3 * qn_ref[...], c, s1, s2) * (MLA_SCALE * LOG2E)).astype(MXU_DTYPE)
        k3, _ = _rms(jnp.stack([_dot(kvn, wk_ref[h]) for h in range(HEADS)]) + kr, MLA_QK)
        k3 = _rope(k3 * kn_ref[...], c, s1, s2)
        k_out[...] = k3.astype(MXU_DTYPE)
        for h in range(HEADS):
            for t in range(nsub):
                kt_out[h, t] = k3[h, tk * t:tk * (t + 1)].T.astype(MXU_DTYPE)
        v = _dot(kvn, wv_ref[...])
        for h in range(HEADS):
            vp = v[:, LANES * (h // 2):LANES * (h // 2 + 1)]
            own = half1 if h % 2 else jnp.logical_not(half1)
            vp = jnp.where(own, vp, 0.0)
            for t in range(nsub):
                vpt = vp[tk * t:tk * (t + 1)].T
                vt_out[h, t] = jnp.where(ones_row == MLA_ONES_ROW[h % 2], 1.0, vpt).astype(MXU_DTYPE)

    full = lambda shape: pl.BlockSpec(shape, lambda i: (0,) * len(shape))
    hd = pl.BlockSpec((HEADS, tm, LANES), lambda i: (0, i, 0))
    hdt = pl.BlockSpec((HEADS, nsub, LANES, tk), lambda i: (0, i, 0, 0))
    nat = _sds((HEADS, T, LANES), MXU_DTYPE)
    tr = _sds((HEADS, T // tk, LANES, tk), MXU_DTYPE)
    return pl.pallas_call(
        body, name="mla_prep_fwd", grid=(T // tm,),
        in_specs=[pl.BlockSpec((tm, 256), lambda i: (i, CB_QLAT)), pl.BlockSpec((tm, LANES), lambda i: (i, CB_KVLAT)),
                  pl.BlockSpec((tm, LANES), lambda i: (i, CB_KROPE)),
                  full((1, 256)), full((1, LANES)), full((HEADS, 256, LANES)), full((HEADS, LANES, LANES)),
                  full((LANES, 512)), full((1, LANES)), full((1, LANES)),
                  pl.BlockSpec((tm, LANES), lambda i: (i, 0)), pl.BlockSpec((tm, LANES), lambda i: (i, 0)),
                  pl.BlockSpec((tm, LANES), lambda i: (i, 0))],
        out_specs=[hd, hd, hdt, hdt],
        out_shape=[nat, nat, tr, tr],
        compiler_params=_cp(("parallel",), 32))(
            proj, proj, proj, lw["qa"], lw["kva"], lw["wq"], lw["wk"], lw["wv"], lw["qn"], lw["kn"],
            rope[0], rope[1], rope[2])


def _mla_attn_fwd(q, k, vt):
    T = q.shape[1]
    tk = min(TK, T // 2)
    tq = 2 * tk

    def body(q_ref, k_ref, vt_ref, o_ref, lse_ref, acc_s, m_s, s_a, s_b):
        i = pl.program_id(1)
        key = lax.broadcasted_iota(jnp.int32, (tk, tq), 0)
        qry = lax.broadcasted_iota(jnp.int32, (tk, tq), 1)
        qs = [q_ref[0], q_ref[1]]
        acc_s[...] = jnp.zeros_like(acc_s)
        m_s[...] = jnp.full(m_s.shape, NEG_INF, F32)

        def scores(kj, buf):
            rows = pl.ds(pl.multiple_of(kj * tk, tk), tk)
            for r in range(2):
                buf[r] = _dot_nt(k_ref[r, rows, :], qs[r])

        def consume(kj, buf, diag):
            for r in range(2):
                s = buf[r]
                if diag is not None:
                    s = jnp.where(key + diag * tk <= qry, s, NEG_INF)
                m_old = m_s[r]
                m_new = jnp.maximum(m_old, jnp.max(s, axis=0, keepdims=True))
                alpha = jnp.exp2(m_old - m_new)
                p = jnp.exp2(s - m_new)
                m_s[r] = m_new
                acc_s[r] = alpha * acc_s[r] + _dot(vt_ref[r, kj], p.astype(MXU_DTYPE))

        scores(0, s_a)

        def pair(kj):
            scores(kj + 1, s_b)
            consume(kj, s_a, None)
            scores(kj + 2, s_a)
            consume(kj + 1, s_b, None)

        def octet(ko, carry):
            for t in range(4):
                pair(8 * ko + 2 * t)
            return carry

        lax.fori_loop(0, i // 4, octet, 0)

        @pl.when(i % 4 >= 2)
        def _():
            pair(8 * (i // 4))
            pair(8 * (i // 4) + 2)

        @pl.when(i % 2 == 1)
        def _():
            pair(2 * i - 2)

        last = pl.ds(pl.multiple_of((2 * i + 1) * tk, tk), tk)
        for r in range(2):
            s_b[r, :, tk:] = _dot_nt(k_ref[r, last, :], qs[r][tk:])
        consume(2 * i, s_a, 0)
        for r in range(2):
            s = jnp.where(key[:, tk:] + tk <= qry[:, tk:], s_b[r, :, tk:], NEG_INF)
            m_old = m_s[r, :, tk:]
            m_new = jnp.maximum(m_old, jnp.max(s, axis=0, keepdims=True))
            p = jnp.exp2(s - m_new)
            m_s[r, :, tk:] = m_new
            acc_s[r, :, tk:] = (jnp.exp2(m_old - m_new) * acc_s[r, :, tk:]
                                + _dot(vt_ref[r, 2 * i + 1], p.astype(MXU_DTYPE)))
        l = [acc_s[r, pl.ds(MLA_ONES_ROW[r], 1), :] for r in range(2)]
        head0 = lax.broadcasted_iota(jnp.int32, (LANES, 1), 0) < 64
        o_ref[...] = jnp.where(head0, acc_s[0] / l[0], acc_s[1] / l[1]).T
        for r in range(2):
            lse_ref[r] = m_s[r] + jnp.log2(l[r])

    return pl.pallas_call(
        body, name="mla_attn_fwd", grid=(HEADS // 2, T // tq),
        in_specs=[pl.BlockSpec((2, tq, LANES), lambda j, i: (j, i, 0)),
                  pl.BlockSpec((2, T, LANES), lambda j, i: (j, 0, 0)),
                  pl.BlockSpec((2, T // tk, LANES, tk), lambda j, i: (j, 0, 0, 0))],
        out_specs=[pl.BlockSpec((tq, LANES), lambda j, i: (i, j)),
                   pl.BlockSpec((2, 1, tq), lambda j, i: (j, 0, i))],
        out_shape=[_sds((T, GROUP_WIDTH), F32), _sds((HEADS, 1, T), F32)],
        scratch_shapes=[pltpu.VMEM((2, LANES, tq), F32), pltpu.VMEM((2, 1, tq), F32),
                        pltpu.VMEM((2, tk, tq), F32), pltpu.VMEM((2, tk, tq), F32)],
        compiler_params=_cp(("parallel", "arbitrary"), 40))(q, k, vt)


def _swa_kv_variants(x, half1):
    xs = pltpu.roll(x, 64, 1)
    out = {}
    for g in range(2):
        for r in range(2):
            own = half1 if r else jnp.logical_not(half1)
            out[(g, r)] = jnp.where(own, x if g == r else xs, 0.0).astype(MXU_DTYPE)
    return out


def _swa_alibi():
    ki = np.arange(2 * BLOCK)[:, None]
    qi = np.arange(BLOCK)[None, :]
    dist = BLOCK + qi - ki
    slopes = 2.0 ** -(np.arange(HEADS) + 1.0)
    tab = np.where(((dist >= 0) & (dist < BLOCK))[None], slopes[:, None, None] * dist[None], 1e30)
    return jnp.asarray(tab, F32)


def _swa_kv_variants_t(xt, rows1):
    xs = pltpu.roll(xt, 64, 0)
    out = {}
    for g in range(2):
        for r in range(2):
            own = rows1 if r else jnp.logical_not(rows1)
            out[(g, r)] = jnp.where(own, xt if g == r else xs, 0.0).astype(MXU_DTYPE)
    return out


def _swa_probs(i, nb, q_ref, k_ref, v_ref, pk_ref, pv_ref, qw_ref, kw_ref, alibi_ref, sink_ref):
    scale = SWA_HEAD_DIM ** -0.5
    half1 = lax.broadcasted_iota(jnp.int32, (1, LANES), 1) >= 64
    k_all = jnp.concatenate([pk_ref[...], k_ref[...]], axis=0)
    v_all = jnp.concatenate([pv_ref[...], v_ref[...]], axis=0)
    khat, _ = _rms_halves(k_all, half1)
    kn = khat * kw_ref[...]
    kp = _swa_kv_variants(kn, half1)
    qhat, qr, qn, qt = [], [], [], []
    for j in range(4):
        xh, r = _rms_halves(q_ref[:, LANES * j:LANES * (j + 1)], half1)
        qf = xh * qw_ref[...]
        qhat.append(xh)
        qr.append(r)
        qn.append(qf.astype(MXU_DTYPE))
        qt.append(qf.T.astype(MXU_DTYPE))
    key = lax.broadcasted_iota(jnp.int32, (2 * BLOCK, BLOCK), 0)
    first = jnp.where((i == 0) & (key < BLOCK), NEG_INF, 0.0)
    s = jnp.stack([_dot(kp[(h // 4, h % 2)][BLOCK * b:BLOCK * (b + 2)], qt[h // 2][:, BLOCK * b:BLOCK * (b + 1)])
                   for b in range(nb) for h in range(HEADS)]) * scale - alibi_ref[...]
    s = jnp.concatenate([s[:HEADS] + first, s[HEADS:]], axis=0) if nb > 1 else s + first
    sink = jnp.stack([jnp.full((1, 1), sink_ref[h], F32) for _ in range(nb) for h in range(HEADS)])
    m = jnp.maximum(jnp.max(s, axis=1, keepdims=True), sink)
    e = jnp.exp(s - m)
    es = jnp.exp(sink - m)
    inv = 1.0 / (jnp.sum(e, axis=1, keepdims=True) + es)
    return e * inv, es * inv, dict(half1=half1, kn=kn, kp=kp, v_all=v_all, qhat=qhat, qr=qr, qn=qn)


def _swa_fwd(proj, lw):
    T = proj.shape[0]
    tm = min(TM_SWA, T)
    nb = tm // BLOCK

    def body(q_ref, k_ref, v_ref, pk_ref, pv_ref, qw_ref, kw_ref, alibi_ref, sink_ref, o_ref):
        p, _, c = _swa_probs(pl.program_id(0), nb, q_ref, k_ref, v_ref, pk_ref, pv_ref, qw_ref, kw_ref, alibi_ref,
                             sink_ref)
        p = p.astype(MXU_DTYPE)
        rows1 = lax.broadcasted_iota(jnp.int32, (LANES, 1), 0) >= 64
        vpt = _swa_kv_variants_t(c["v_all"].T, rows1)
        for j in range(4):
            g = j // 2
            o_t = [_dot(vpt[(g, 0)][:, BLOCK * b:BLOCK * (b + 2)], p[HEADS * b + 2 * j])
                   + _dot(vpt[(g, 1)][:, BLOCK * b:BLOCK * (b + 2)], p[HEADS * b + 2 * j + 1]) for b in range(nb)]
            o_t = jnp.concatenate(o_t, axis=1) if nb > 1 else o_t[0]
            o_ref[:, LANES * j:LANES * (j + 1)] = o_t.T

    prev = lambda cb: pl.BlockSpec((BLOCK, LANES), lambda i: (jnp.maximum(i * nb - 1, 0), cb))
    return pl.pallas_call(
        body, name="swa_fwd", grid=(T // tm,),
        in_specs=[pl.BlockSpec((tm, 512), lambda i: (i, CB_SQ)), pl.BlockSpec((tm, LANES), lambda i: (i, CB_SK)),
                  pl.BlockSpec((tm, LANES), lambda i: (i, CB_SV)), prev(CB_SK), prev(CB_SV),
                  pl.BlockSpec((1, LANES), lambda i: (0, 0)), pl.BlockSpec((1, LANES), lambda i: (0, 0)),
                  pl.BlockSpec((nb * HEADS, 2 * BLOCK, BLOCK), lambda i: (0, 0, 0)),
                  pl.BlockSpec(memory_space=pltpu.SMEM)],
        out_specs=pl.BlockSpec((tm, 512), lambda i: (i, 0)),
        out_shape=_sds((T, GROUP_WIDTH), F32),
        compiler_params=_cp(("parallel",), 40))(
            proj, proj, proj, proj, proj, lw["sqn"], lw["skn"], jnp.tile(_swa_alibi(), (nb, 1, 1)), lw["sinks"])


def _shift_down(u, prev, n, row):
    tm = u.shape[0]
    out = pltpu.roll(u, n, 0)
    row8 = lax.broadcasted_iota(jnp.int32, prev.shape, 0)
    for t in range(n):
        src = jnp.sum(jnp.where(row8 == 8 - n + t, prev, 0.0), axis=0, keepdims=True)
        out = jnp.where(row == t, src, out)
    return out


def _shift_up(u, nxt, n, row):
    tm = u.shape[0]
    out = pltpu.roll(u, tm - n, 0)
    row8 = lax.broadcasted_iota(jnp.int32, nxt.shape, 0)
    for t in range(n):
        src = jnp.sum(jnp.where(row8 == t, nxt, 0.0), axis=0, keepdims=True)
        out = jnp.where(row == tm - n + t, src, out)
    return out


def _mix_fwd(proj, o_mla, o_swa, conv_w):
    T = proj.shape[0]
    tm = min(TM_ROW, T)

    def body(gm_ref, ch_ref, cb_ref, cc_ref, gc_ref, gs_ref, pch_ref, pcc_ref, om_ref, os_ref, w_ref, y_ref):
        i = pl.program_id(0)
        row = lax.broadcasted_iota(jnp.int32, (tm, GROUP_WIDTH), 0)
        u = cc_ref[...] * ch_ref[...]
        u_prev = jnp.where(i > 0, pcc_ref[...] * pch_ref[...], 0.0)
        z = (w_ref[0:1, :] * _shift_down(u, u_prev, 2, row) + w_ref[1:2, :] * _shift_down(u, u_prev, 1, row)
             + w_ref[2:3, :] * u)
        gm, gc, gs = gm_ref[...], gc_ref[...], gs_ref[...]
        y_ref[:, 0:512] = (om_ref[...] * (gm * _sigmoid(gm))).astype(MXU_DTYPE)
        y_ref[:, 512:1024] = (cb_ref[...] * z * (gc * _sigmoid(gc))).astype(MXU_DTYPE)
        y_ref[:, 1024:1536] = (os_ref[...] * (gs * _sigmoid(gs))).astype(MXU_DTYPE)

    blk = lambda cb: pl.BlockSpec((tm, 512), lambda i: (i, cb))
    prev = lambda cb: pl.BlockSpec((8, 512), lambda i: (jnp.maximum(i * (tm // 8) - 1, 0), cb))
    tile = pl.BlockSpec((tm, 512), lambda i: (i, 0))
    return pl.pallas_call(
        body, name="mix_fwd", grid=(T // tm,),
        in_specs=[blk(CB_GMLA), blk(CB_CH), blk(CB_CB), blk(CB_CC), blk(CB_GCONV), blk(CB_GSWA),
                  prev(CB_CH), prev(CB_CC), tile, tile, pl.BlockSpec((8, 512), lambda i: (0, 0))],
        out_specs=pl.BlockSpec((tm, D_MIX), lambda i: (i, 0)),
        out_shape=_sds((T, D_MIX), MXU_DTYPE),
        compiler_params=_cp(("parallel",), 32))(
            proj, proj, proj, proj, proj, proj, proj, proj, o_mla, o_swa, conv_w)


def _outproj_loss(ycat, wo, x, target):
    T, D = x.shape
    K = ycat.shape[1]
    tm = min(TM_PROJ, T)
    nt = T // tm

    def body(y_ref, w_ref, x_ref, t_ref, g_ref, loss_ref, acc_ref):
        i = pl.program_id(0)

        @pl.when(i == 0)
        def _():
            acc_ref[...] = jnp.zeros_like(acc_ref)

        err = _dot(y_ref[...], w_ref[...]) + x_ref[...] - t_ref[...]
        g_ref[...] = err * (1.0 / D)
        acc_ref[...] += _fold_rows8(err * err)

        @pl.when(i == nt - 1)
        def _():
            tot = jnp.sum(jnp.sum(acc_ref[...], axis=1, keepdims=True), axis=0, keepdims=True)
            loss_ref[...] = jnp.broadcast_to(tot * (0.5 / D), (8, LANES))

    tile = pl.BlockSpec((tm, D), lambda i: (i, 0))
    return pl.pallas_call(
        body, name="outproj_loss", grid=(nt,),
        in_specs=[pl.BlockSpec((tm, K), lambda i: (i, 0)), pl.BlockSpec((K, D), lambda i: (0, 0)), tile, tile],
        out_specs=[tile, pl.BlockSpec((8, LANES), lambda i: (0, 0))],
        out_shape=[_sds((T, D), F32), _sds((8, LANES), F32)],
        scratch_shapes=[pltpu.VMEM((8, D), F32)],
        compiler_params=_cp(("arbitrary",), 48))(ycat, wo, x, target)


def _outproj_bwd(g, ycat, wot):
    T, D = g.shape
    K = ycat.shape[1]
    tm = min(1024, T)
    nt = T // tm

    def body(g_ref, y_ref, wt_ref, dy_ref, dw_ref, acc_ref):
        i = pl.program_id(0)

        @pl.when(i == 0)
        def _():
            acc_ref[...] = jnp.zeros_like(acc_ref)

        gb = g_ref[...].astype(MXU_DTYPE)
        dy_ref[...] = _dot(gb, wt_ref[...])
        acc_ref[...] += _dot_tn(y_ref[...], gb)

        @pl.when(i == nt - 1)
        def _():
            dw_ref[...] = acc_ref[...].astype(WIRE_DTYPE)

    return pl.pallas_call(
        body, name="outproj_bwd", grid=(nt,),
        in_specs=[pl.BlockSpec((tm, D), lambda i: (i, 0)), pl.BlockSpec((tm, K), lambda i: (i, 0)),
                  pl.BlockSpec((D, K), lambda i: (0, 0))],
        out_specs=[pl.BlockSpec((tm, K), lambda i: (i, 0)), pl.BlockSpec((K, D), lambda i: (0, 0))],
        out_shape=[_sds((T, K), F32), _sds((K, D), WIRE_DTYPE)],
        scratch_shapes=[pltpu.VMEM((K, D), F32)],
        compiler_params=_cp(("arbitrary",), 48))(g, ycat, wot)


def _mix_bwd(dycat, proj, o_mla, o_swa, conv_w):
    T = proj.shape[0]
    tm = min(TM_ROW, T)
    nt = T // tm

    def body(dym_ref, dyc_ref, dys_ref, gm_ref, ch_ref, cb_ref, cc_ref, gc_ref, gs_ref, pch_ref, pcc_ref,
             ndy_ref, ncb_ref, ngc_ref, om_ref, os_ref, w_ref,
             d1_ref, dom_ref, dos_ref, dw_ref):
        i = pl.program_id(0)

        @pl.when(i == 0)
        def _():
            dw_ref[...] = jnp.zeros_like(dw_ref)

        row = lax.broadcasted_iota(jnp.int32, (tm, GROUP_WIDTH), 0)

        def gate(g):
            sg = _sigmoid(g)
            return g * sg, sg * (1.0 + g * (1.0 - sg))

        gm = gm_ref[...]
        silu, dsilu = gate(gm)
        dym = dym_ref[...]
        dom_ref[...] = dym * silu
        d1_ref[:, 0:512] = (dym * om_ref[...] * dsilu).astype(MXU_DTYPE)

        gs = gs_ref[...]
        silu, dsilu = gate(gs)
        dys = dys_ref[...]
        dos_ref[...] = dys * silu
        d1_ref[:, 2560:3072] = (dys * os_ref[...] * dsilu).astype(MXU_DTYPE)

        ch, cb, cc, gc, dyc = ch_ref[...], cb_ref[...], cc_ref[...], gc_ref[...], dyc_ref[...]
        w0, w1, w2 = w_ref[0:1, :], w_ref[1:2, :], w_ref[2:3, :]
        u = cc * ch
        u_prev = jnp.where(i > 0, pcc_ref[...] * pch_ref[...], 0.0)
        u1 = _shift_down(u, u_prev, 1, row)
        u2 = _shift_down(u, u_prev, 2, row)
        z = w0 * u2 + w1 * u1 + w2 * u
        silu, dsilu = gate(gc)
        dz = dyc * cb * silu
        ngc = ngc_ref[...]
        dz_next = jnp.where(i < nt - 1, ndy_ref[...] * ncb_ref[...] * (ngc * _sigmoid(ngc)), 0.0)
        du = w2 * dz + w1 * _shift_up(dz, dz_next, 1, row) + w0 * _shift_up(dz, dz_next, 2, row)
        d1_ref[:, 512:1024] = (du * cc).astype(MXU_DTYPE)
        d1_ref[:, 1024:1536] = (dyc * z * silu).astype(MXU_DTYPE)
        d1_ref[:, 1536:2048] = (du * ch).astype(MXU_DTYPE)
        d1_ref[:, 2048:2560] = (dyc * cb * z * dsilu).astype(MXU_DTYPE)
        row8 = lax.broadcasted_iota(jnp.int32, (8, GROUP_WIDTH), 0)
        dw = jnp.zeros((8, GROUP_WIDTH), F32)
        for t, shifted in enumerate((u2, u1, u)):
            dw = dw + jnp.where(row8 == t, jnp.sum(dz * shifted, axis=0, keepdims=True), 0.0)
        dw_ref[...] += dw

    blk = lambda cb: pl.BlockSpec((tm, 512), lambda i: (i, cb))
    prev = lambda cb: pl.BlockSpec((8, 512), lambda i: (jnp.maximum(i * (tm // 8) - 1, 0), cb))
    nxt = lambda cb: pl.BlockSpec((8, 512), lambda i: (jnp.minimum((i + 1) * (tm // 8), T // 8 - 1), cb))
    tile = pl.BlockSpec((tm, 512), lambda i: (i, 0))
    return pl.pallas_call(
        body, name="mix_bwd", grid=(nt,),
        in_specs=[blk(0), blk(1), blk(2), blk(CB_GMLA), blk(CB_CH), blk(CB_CB), blk(CB_CC), blk(CB_GCONV),
                  blk(CB_GSWA), prev(CB_CH), prev(CB_CC), nxt(1), nxt(CB_CB), nxt(CB_GCONV), tile, tile,
                  pl.BlockSpec((8, 512), lambda i: (0, 0))],
        out_specs=[pl.BlockSpec((tm, 3072), lambda i: (i, DPB_MIX)), tile, tile,
                   pl.BlockSpec((8, 512), lambda i: (0, 0))],
        out_shape=[_sds((T, NP), MXU_DTYPE), _sds((T, 512), F32), _sds((T, 512), F32), _sds((8, 512), F32)],
        compiler_params=_cp(("arbitrary",), 48))(
            dycat, dycat, dycat, proj, proj, proj, proj, proj, proj, proj, proj, dycat, proj, proj,
            o_mla, o_swa, conv_w)


def _swa_bwd(proj, o_swa, do_swa, lw, dproj):
    T = proj.shape[0]
    tm = min(TM_SWA, T)
    nb = tm // BLOCK
    scale = SWA_HEAD_DIM ** -0.5

    def body(q_ref, k_ref, v_ref, pk_ref, pv_ref, o_ref, do_ref, qw_ref, kw_ref, alibi_ref, sink_ref, dproj_in,
             dq_ref, dk_ref, dv_ref, dqw_ref, dsink_ref):
        i = pl.program_id(0)

        @pl.when(i == 0)
        def _():
            dk_ref[...] = jnp.zeros_like(dk_ref)
            dv_ref[...] = jnp.zeros_like(dv_ref)
            dqw_ref[...] = jnp.zeros_like(dqw_ref)
            dsink_ref[...] = jnp.zeros_like(dsink_ref)

        p, p_sink, c = _swa_probs(i, nb, q_ref, k_ref, v_ref, pk_ref, pv_ref, qw_ref, kw_ref, alibi_ref, sink_ref)
        half1, kp, qn, qhat, qr = c["half1"], c["kp"], c["qn"], c["qhat"], c["qr"]
        rows1 = lax.broadcasted_iota(jnp.int32, (LANES, 1), 0) >= 64
        kpt = _swa_kv_variants_t(c["kn"].T, rows1)
        vp = _swa_kv_variants(c["v_all"], half1)
        qw = qw_ref[...]
        rows = [slice(BLOCK * b, BLOCK * (b + 1)) for b in range(nb)]
        keys = [slice(BLOCK * b, BLOCK * (b + 2)) for b in range(nb)]
        dob, dot_b, dd0, dd1 = [], [], [], []
        for j in range(4):
            cols = slice(LANES * j, LANES * (j + 1))
            do = do_ref[:, cols]
            do_t = do.T
            prod_t = do_t * o_ref[:, cols].T
            dob.append(do.astype(MXU_DTYPE))
            dot_b.append(do_t.astype(MXU_DTYPE))
            dd0.append(jnp.sum(jnp.where(rows1, 0.0, prod_t), axis=0, keepdims=True))
            dd1.append(jnp.sum(jnp.where(rows1, prod_t, 0.0), axis=0, keepdims=True))
        dd = jnp.stack([(dd1 if h % 2 else dd0)[h // 2][:, rows[b]] for b in range(nb) for h in range(HEADS)])
        dp = jnp.stack([_dot(vp[(h // 4, h % 2)][keys[b]], dot_b[h // 2][:, rows[b]])
                        for b in range(nb) for h in range(HEADS)])
        ds = (p * (dp - dd) * scale).astype(MXU_DTYPE)
        dsink = -jnp.sum(p_sink * dd, axis=2, keepdims=True)
        pb = p.astype(MXU_DTYPE)

        dqw = jnp.zeros((1, LANES), F32)
        for j in range(4):
            g = j // 2
            dqn_t = [_dot(kpt[(g, 0)][:, keys[b]], ds[HEADS * b + 2 * j])
                     + _dot(kpt[(g, 1)][:, keys[b]], ds[HEADS * b + 2 * j + 1]) for b in range(nb)]
            dqn = (jnp.concatenate(dqn_t, axis=1) if nb > 1 else dqn_t[0]).T
            dqw = dqw + jnp.sum(dqn * qhat[j], axis=0, keepdims=True)
            dq_ref[:, LANES * j:LANES * (j + 1)] = _rms_halves_bwd(dqn, qhat[j], qr[j], qw, half1).astype(MXU_DTYPE)
        dqw_ref[...] += _row0(dqw + pltpu.roll(dqw, 64, 1))

        dk_tot = jnp.zeros((tm + BLOCK, LANES), F32)
        dv_tot = jnp.zeros((tm + BLOCK, LANES), F32)
        for b in range(nb):
            dk_b = jnp.zeros((2 * BLOCK, LANES), F32)
            dv_b = jnp.zeros((2 * BLOCK, LANES), F32)
            for g in range(2):
                for r in range(2):
                    own = half1 if r else jnp.logical_not(half1)
                    ha, hb = HEADS * b + 4 * g + r, HEADS * b + 4 * g + 2 + r
                    qa, qb = qn[2 * g][rows[b]], qn[2 * g + 1][rows[b]]
                    da, db = dob[2 * g][rows[b]], dob[2 * g + 1][rows[b]]
                    dkp = jnp.where(own, _dot(ds[ha], qa) + _dot(ds[hb], qb), 0.0)
                    dvp = jnp.where(own, _dot(pb[ha], da) + _dot(pb[hb], db), 0.0)
                    if g != r:
                        dkp = pltpu.roll(dkp, 64, 1)
                        dvp = pltpu.roll(dvp, 64, 1)
                    dk_b = dk_b + dkp
                    dv_b = dv_b + dvp
            pad = lambda x: jnp.concatenate(
                [z for z in (jnp.zeros((BLOCK * b, LANES), F32), x, jnp.zeros((BLOCK * (nb - 1 - b), LANES), F32))
                 if z.shape[0]], axis=0)
            dk_tot = dk_tot + pad(dk_b)
            dv_tot = dv_tot + pad(dv_b)
        dst = pl.ds(pl.multiple_of(i * tm, BLOCK), tm + BLOCK)
        dk_ref[dst, :] += dk_tot
        dv_ref[dst, :] += dv_tot

        row8 = lax.broadcasted_iota(jnp.int32, (8, LANES), 0)
        dsink_tile = jnp.zeros((8, LANES), F32)
        for b in range(nb):
            for h in range(HEADS):
                dsink_tile = dsink_tile + jnp.where(row8 == h, jnp.broadcast_to(dsink[HEADS * b + h], (8, LANES)), 0.0)
        dsink_ref[...] += dsink_tile

    prev = lambda cb: pl.BlockSpec((BLOCK, LANES), lambda i: (jnp.maximum(i * nb - 1, 0), cb))
    tile = pl.BlockSpec((tm, 512), lambda i: (i, 0))
    small = pl.BlockSpec((8, LANES), lambda i: (0, 0))
    acc = pl.BlockSpec((T + BLOCK, LANES), lambda i: (0, 0))
    return pl.pallas_call(
        body, name="swa_bwd", grid=(T // tm,),
        in_specs=[pl.BlockSpec((tm, 512), lambda i: (i, CB_SQ)), pl.BlockSpec((tm, LANES), lambda i: (i, CB_SK)),
                  pl.BlockSpec((tm, LANES), lambda i: (i, CB_SV)), prev(CB_SK), prev(CB_SV), tile, tile,
                  pl.BlockSpec((1, LANES), lambda i: (0, 0)), pl.BlockSpec((1, LANES), lambda i: (0, 0)),
                  pl.BlockSpec((nb * HEADS, 2 * BLOCK, BLOCK), lambda i: (0, 0, 0)),
                  pl.BlockSpec(memory_space=pltpu.SMEM), pl.BlockSpec(memory_space=pl.ANY)],
        out_specs=[pl.BlockSpec((tm, 512), lambda i: (i, DPB_SQ)), acc, acc, small, small],
        out_shape=[_sds((T, NP), MXU_DTYPE), _sds((T + BLOCK, LANES), F32), _sds((T + BLOCK, LANES), F32),
                   _sds((8, LANES), F32), _sds((8, LANES), F32)],
        input_output_aliases={11: 0},
        compiler_params=_cp(("arbitrary",), 48))(
            proj, proj, proj, proj, proj, o_swa, do_swa, lw["sqn"], lw["skn"], jnp.tile(_swa_alibi(), (nb, 1, 1)),
            lw["sinks"], dproj)


def _swa_kv_bwd(proj, dkn, dv, lw, dproj):
    T = proj.shape[0]
    tm = min(TM_SWA, T)
    dkn, dv = dkn[BLOCK:], dv[BLOCK:]

    def body(k_ref, dkn_ref, dv_ref, kw_ref, dproj_in, d_ref, dkw_ref):
        i = pl.program_id(0)

        @pl.when(i == 0)
        def _():
            dkw_ref[...] = jnp.zeros_like(dkw_ref)

        half1 = lax.broadcasted_iota(jnp.int32, (1, LANES), 1) >= 64
        khat, kr = _rms_halves(k_ref[...], half1)
        dkn_t = dkn_ref[...]
        dkw = jnp.sum(dkn_t * khat, axis=0, keepdims=True)
        dkw_ref[...] += _row0(dkw + pltpu.roll(dkw, 64, 1))
        d_ref[:, 0:LANES] = _rms_halves_bwd(dkn_t, khat, kr, kw_ref[...], half1).astype(MXU_DTYPE)
        d_ref[:, LANES:2 * LANES] = dv_ref[...].astype(MXU_DTYPE)

    return pl.pallas_call(
        body, name="swa_kv_bwd", grid=(T // tm,),
        in_specs=[pl.BlockSpec((tm, LANES), lambda i: (i, CB_SK)), pl.BlockSpec((tm, LANES), lambda i: (i, 0)),
                  pl.BlockSpec((tm, LANES), lambda i: (i, 0)), pl.BlockSpec((1, LANES), lambda i: (0, 0)),
                  pl.BlockSpec(memory_space=pl.ANY)],
        out_specs=[pl.BlockSpec((tm, 2 * LANES), lambda i: (i, DPB_SKV)), pl.BlockSpec((8, LANES), lambda i: (0, 0))],
        out_shape=[_sds((T, NP), MXU_DTYPE), _sds((8, LANES), F32)],
        input_output_aliases={4: 0},
        compiler_params=_cp(("arbitrary",), 32))(proj, dkn, dv, lw["skn"], dproj)


def _mla_attn_bwd(q, k, kt, vt, o, do, lse):
    T = q.shape[1]
    tk = min(TK, T // 2)
    tq = 2 * tk

    def body(q_ref, k_ref, kt_ref, vt_ref, o_ref, do_ref, lse_ref, dq_ref, dk_ref, dv_ref, dq_s, lse_s, dd_s,
             s_a, s_b, p_a, p_b):
        h = pl.program_id(0)
        i = pl.program_id(1)

        @pl.when(i == 0)
        def _():
            dk_ref[...] = jnp.zeros_like(dk_ref)
            dv_ref[...] = jnp.zeros_like(dv_ref)

        qry = lax.broadcasted_iota(jnp.int32, (tq, tk), 0)
        key = lax.broadcasted_iota(jnp.int32, (tq, tk), 1)
        own = (lax.broadcasted_iota(jnp.int32, (1, LANES), 1) // 64) == (h % 2)
        do_own = jnp.where(own, do_ref[...], 0.0)
        dob = do_own.astype(MXU_DTYPE)
        dob_t = do_own.T.astype(MXU_DTYPE)
        qh = q_ref[0]
        qh_t = qh.astype(F32).T.astype(MXU_DTYPE)
        dd_col = jnp.sum(do_own * o_ref[...], axis=-1, keepdims=True)
        lse_col = jnp.broadcast_to(lse_ref[0], (LANES, tq)).T
        for c in range(tk // LANES):
            lse_s[:, LANES * c:LANES * (c + 1)] = lse_col
            dd_s[:, LANES * c:LANES * (c + 1)] = jnp.broadcast_to(dd_col, (tq, LANES))
        dq_s[...] = jnp.zeros_like(dq_s)

        def scores(kj, s_buf, p_buf):
            s_buf[...] = _dot(qh, kt_ref[0, kj])
            p_buf[...] = _dot(dob, vt_ref[0, kj])

        def consume(kj, s_buf, p_buf, diag):
            rows = pl.ds(pl.multiple_of(kj * tk, tk), tk)
            s = s_buf[...]
            if diag is not None:
                s = jnp.where(key + diag * tk <= qry, s, NEG_INF)
            p = jnp.exp2(s - lse_s[...])
            ds = (p * (p_buf[...] - dd_s[...])).astype(MXU_DTYPE)
            dq_s[...] += _dot(ds, k_ref[0, rows, :])
            dk_ref[0, kj] += _dot(qh_t, ds)
            dv_ref[0, kj] += _dot(dob_t, p.astype(MXU_DTYPE))

        scores(0, s_a, p_a)

        def pair(kj):
            scores(kj + 1, s_b, p_b)
            consume(kj, s_a, p_a, None)
            scores(kj + 2, s_a, p_a)
            consume(kj + 1, s_b, p_b, None)

        def octet(ko, carry):
            for t in range(4):
                pair(8 * ko + 2 * t)
            return carry

        lax.fori_loop(0, i // 4, octet, 0)

        @pl.when(i % 4 >= 2)
        def _():
            pair(8 * (i // 4))
            pair(8 * (i // 4) + 2)

        @pl.when(i % 2 == 1)
        def _():
            pair(2 * i - 2)

        kl = 2 * i + 1
        s_b[tk:, :] = _dot(qh[tk:], kt_ref[0, kl])
        p_b[tk:, :] = _dot(dob[tk:], vt_ref[0, kl])
        consume(2 * i, s_a, p_a, 0)
        s = jnp.where(key[tk:] + tk <= qry[tk:], s_b[tk:, :], NEG_INF)
        p = jnp.exp2(s - lse_s[tk:, :])
        ds = (p * (p_b[tk:, :] - dd_s[tk:, :])).astype(MXU_DTYPE)
        dq_s[tk:, :] += _dot(ds, k_ref[0, pl.ds(pl.multiple_of(kl * tk, tk), tk), :])
        dk_ref[0, kl] += _dot(qh_t[:, tk:], ds)
        dv_ref[0, kl] += _dot(dob_t[:, tk:], p.astype(MXU_DTYPE))
        dq_ref[0] = dq_s[...]

    res = pl.BlockSpec((1, T, LANES), lambda h, i: (h, 0, 0))
    res_t = pl.BlockSpec((1, T // tk, LANES, tk), lambda h, i: (h, 0, 0, 0))
    buf = pltpu.VMEM((tq, tk), F32)
    acc_t = _sds((HEADS, T // tk, LANES, tk), F32)
    return pl.pallas_call(
        body, name="mla_attn_bwd", grid=(HEADS, T // tq),
        in_specs=[pl.BlockSpec((1, tq, LANES), lambda h, i: (h, i, 0)), res, res_t, res_t,
                  pl.BlockSpec((tq, LANES), lambda h, i: (i, h // 2)),
                  pl.BlockSpec((tq, LANES), lambda h, i: (i, h // 2)),
                  pl.BlockSpec((1, 1, tq), lambda h, i: (h, 0, i))],
        out_specs=[pl.BlockSpec((1, tq, LANES), lambda h, i: (h, i, 0)), res_t, res_t],
        out_shape=[_sds((HEADS, T, LANES), F32), acc_t, acc_t],
        scratch_shapes=[pltpu.VMEM((tq, LANES), F32), buf, buf, buf, buf, buf, buf],
        compiler_params=_cp(("parallel", "arbitrary"), 48))(q, k, kt, vt, o, do, lse)


def _mla_prep_bwd(proj, dq, dk, dv, lw, rope, dproj):
    T = proj.shape[0]
    tm = min(TK, T // 2)

    def body(ql_ref, kvl_ref, kr_ref, dq_ref, dk_ref, dv_ref, qa_ref, kva_ref, wq_ref, wk_ref, wv_ref,
             wqt_ref, wkt_ref, wvt_ref, qn_ref, kn_ref, c_ref, s1_ref, s2_ref, dproj_in,
             d_ref, dwq_ref, dwk_ref, dwv_ref, dqa_ref, dkva_ref, dqn_ref, dkn_ref):
        i = pl.program_id(0)

        @pl.when(i == 0)
        def _():
            for ref in (dwq_ref, dwk_ref, dwv_ref, dqa_ref, dkva_ref, dqn_ref, dkn_ref):
                ref[...] = jnp.zeros_like(ref)

        c, s1, s2 = c_ref[...], s1_ref[...], s2_ref[...]
        lane = lax.broadcasted_iota(jnp.int32, (1, LANES), 1)
        qlhat, qlr = _rms(ql_ref[...], MLA_Q_LORA)
        qn = (qlhat * qa_ref[...]).astype(MXU_DTYPE)
        kvhat, kvr = _rms(kvl_ref[...], MLA_KV_LORA)
        kvn = (kvhat * kva_ref[...]).astype(MXU_DTYPE)
        kr = kr_ref[...]
        x3, r3 = _rms(jnp.stack([_dot(qn, wq_ref[h]) for h in range(HEADS)]), MLA_QK)
        dy3 = _rope_bwd(dq_ref[...] * MLA_SCALE, c, s1, s2)
        dqw = jnp.sum(jnp.sum(dy3 * x3, axis=0), axis=0, keepdims=True)
        dx3 = _rms_bwd(dy3, x3, r3, qn_ref[...], MLA_QK).astype(MXU_DTYPE)
        dqnl = jnp.zeros((tm, MLA_Q_LORA), F32)
        for h in range(HEADS):
            dwq_ref[h] += _dot_tn(qn, dx3[h])
            dqnl = dqnl + _dot(dx3[h], wqt_ref[h])

        x3, r3 = _rms(jnp.stack([_dot(kvn, wk_ref[h]) for h in range(HEADS)]) + kr, MLA_QK)
        dy3 = _rope_bwd(jnp.stack([dk_ref[h, 0].T for h in range(HEADS)]) * LN2, c, s1, s2)
        dkw = jnp.sum(jnp.sum(dy3 * x3, axis=0), axis=0, keepdims=True)
        dxf3 = _rms_bwd(dy3, x3, r3, kn_ref[...], MLA_QK)
        dkr = jnp.sum(dxf3, axis=0)
        dx3 = dxf3.astype(MXU_DTYPE)
        dkvn = jnp.zeros((tm, MLA_KV_LORA), F32)
        for h in range(HEADS):
            dwk_ref[h] += _dot_tn(kvn, dx3[h])
            dkvn = dkvn + _dot(dx3[h], wkt_ref[h])
        dvc = jnp.concatenate([(dv_ref[2 * j, 0] + dv_ref[2 * j + 1, 0]).T for j in range(4)],
                              axis=1).astype(MXU_DTYPE)
        dwv_ref[...] += _dot_tn(kvn, dvc)
        dkvn = dkvn + _dot(dvc, wvt_ref[...])
        dqa_ref[...] += _row0(jnp.sum(dqnl * qlhat, axis=0, keepdims=True))
        dkva_ref[...] += _row0(jnp.sum(dkvn * kvhat, axis=0, keepdims=True))
        dqn_ref[...] += _row0(dqw)
        dkn_ref[...] += _row0(dkw)
        d_ref[:, 0:256] = _rms_bwd(dqnl, qlhat, qlr, qa_ref[...], MLA_Q_LORA).astype(MXU_DTYPE)
        d_ref[:, 256:384] = _rms_bwd(dkvn, kvhat, kvr, kva_ref[...], MLA_KV_LORA).astype(MXU_DTYPE)
        d_ref[:, 384:512] = jnp.where((lane >= 64) & (lane < 96), dkr, 0.0).astype(MXU_DTYPE)

    full = lambda shape: pl.BlockSpec(shape, lambda i: (0,) * len(shape))
    hd = pl.BlockSpec((HEADS, tm, LANES), lambda i: (0, i, 0))
    hdt = pl.BlockSpec((HEADS, 1, LANES, tm), lambda i: (0, i, 0, 0))
    tab = pl.BlockSpec((tm, LANES), lambda i: (i, 0))
    return pl.pallas_call(
        body, name="mla_prep_bwd", grid=(T // tm,),
        in_specs=[pl.BlockSpec((tm, 256), lambda i: (i, CB_QLAT)), pl.BlockSpec((tm, LANES), lambda i: (i, CB_KVLAT)),
                  pl.BlockSpec((tm, LANES), lambda i: (i, CB_KROPE)), hd, hdt, hdt,
                  full((1, 256)), full((1, LANES)), full((HEADS, 256, LANES)), full((HEADS, LANES, LANES)),
                  full((LANES, 512)), full((HEADS, LANES, 256)), full((HEADS, LANES, LANES)), full((512, LANES)),
                  full((1, LANES)), full((1, LANES)), tab, tab, tab, pl.BlockSpec(memory_space=pl.ANY)],
        out_specs=[pl.BlockSpec((tm, 512), lambda i: (i, DPB_MLA)), full((HEADS, 256, LANES)),
                   full((HEADS, LANES, LANES)), full((LANES, 512)), full((8, 256)), full((8, LANES)),
                   full((8, LANES)), full((8, LANES))],
        out_shape=[_sds((T, NP), MXU_DTYPE), _sds((HEADS, 256, LANES), F32), _sds((HEADS, LANES, LANES), F32),
                   _sds((LANES, 512), F32), _sds((8, 256), F32), _sds((8, LANES), F32), _sds((8, LANES), F32),
                   _sds((8, LANES), F32)],
        input_output_aliases={19: 0},
        compiler_params=_cp(("arbitrary",), 48))(
            proj, proj, proj, dq, dk, dv, lw["qa"], lw["kva"], lw["wq"], lw["wk"], lw["wv"],
            lw["wqt"], lw["wkt"], lw["wvt"], lw["qn"], lw["kn"], rope[0], rope[1], rope[2], dproj)


def _inproj_bwd_dx(dproj, wpt, x, g_in, ng):
    T, D = x.shape
    tm = min(TM_PROJ, T)

    def body(dp_ref, wt_ref, x_ref, g_ref, w_ref, dx_ref, dw_ref):
        i = pl.program_id(0)

        @pl.when(i == 0)
        def _():
            dw_ref[...] = jnp.zeros_like(dw_ref)

        dh = _dot(dp_ref[...], wt_ref[...])
        xhat, r = _rms(x_ref[...], D)
        dw_ref[...] += _row0(jnp.sum(dh * xhat, axis=0, keepdims=True))
        dx_ref[...] = g_ref[...] + _rms_bwd(dh, xhat, r, w_ref[...], D)

    tile = pl.BlockSpec((tm, D), lambda i: (i, 0))
    return pl.pallas_call(
        body, name="inproj_bwd_dx", grid=(T // tm,),
        in_specs=[pl.BlockSpec((tm, NP), lambda i: (i, 0)), pl.BlockSpec((NP, D), lambda i: (0, 0)), tile, tile,
                  pl.BlockSpec((1, D), lambda i: (0, 0))],
        out_specs=[tile, pl.BlockSpec((8, D), lambda i: (0, 0))],
        out_shape=[_sds((T, D), F32), _sds((8, D), F32)],
        compiler_params=_cp(("arbitrary",), 48))(dproj, wpt, x, g_in, ng)


def _rope_tables(T, token=0.0):
    half = MLA_ROPE // 2
    inv_freq = jnp.power(jnp.float32(ROPE_THETA), -jnp.arange(half, dtype=F32) / half)
    z = lambda n: jnp.zeros((n,), F32)
    freq = jnp.concatenate([z(MLA_NOPE), inv_freq, inv_freq, z(32)])
    first = jnp.concatenate([z(64), jnp.ones((16,), F32), z(48)])
    second = jnp.concatenate([z(80), jnp.ones((16,), F32), z(32)])
    ang = (jnp.arange(T, dtype=F32) + token)[:, None] * freq[None, :]
    sin = jnp.sin(ang)
    return jnp.cos(ang), -sin * first[None, :], sin * second[None, :]


def _pad_lanes(v, n=LANES):
    v = v.reshape(1, -1)
    return jnp.pad(v, ((0, 0), (0, n - v.shape[1])))


def _pack_win_t(wt):
    z = lambda n: jnp.zeros((n, wt.shape[1]), wt.dtype)
    return jnp.concatenate([wt[416:2976], wt[3744:4256], wt[0:384], z(64), wt[384:416], z(32), wt[2976:3488],
                            wt[3488:3616], wt[3616:3744]], axis=0)


def _unpack_dwin(d):
    return jnp.concatenate([d[:, 3072:3456], d[:, 3520:3552], d[:, 0:2560], d[:, 3584:4096], d[:, 4096:4224],
                            d[:, 4224:4352], d[:, 2560:3072]], axis=1)


def _inproj_weights(l, norm_g, w_in_t):
    wpt = _pack_win_t(w_in_t)
    return dict(ng=norm_g[l].reshape(1, -1), wp=wpt.T, wpt=wpt)


def _mixer_weights(l, qa, wqb_full, kva, wkvb_full, qn, kn, conv_full, sqn, skn, sinks, w_out_full):
    wq = jnp.pad(wqb_full, ((0, 0), (0, 0), (0, LANES - MLA_QK)))
    wk = jnp.pad(wkvb_full[:, :, :MLA_NOPE], ((0, 0), (0, 0), (0, LANES - MLA_NOPE)))
    wv = jnp.transpose(wkvb_full[:, :, MLA_NOPE:], (1, 0, 2)).reshape(MLA_KV_LORA, GROUP_WIDTH)
    return dict(
        qa=qa[l].reshape(1, -1), kva=kva[l].reshape(1, -1),
        wq=wq, wk=wk, wv=wv, wqt=jnp.transpose(wq, (0, 2, 1)), wkt=jnp.transpose(wk, (0, 2, 1)), wvt=wv.T,
        qn=_pad_lanes(qn[l]), kn=_pad_lanes(kn[l]),
        conv=jnp.pad(conv_full, ((0, 5), (0, 0))),
        sqn=jnp.tile(sqn[l].reshape(1, -1), (1, 2)), skn=jnp.tile(skn[l].reshape(1, -1), (1, 2)),
        sinks=sinks[l], wo=w_out_full, wot=w_out_full.T)


def _layer_fwd(x, lw, rope, late_weights=None, target=None):
    proj, h = _inproj_fwd(x, lw["ng"], lw["wp"])
    if late_weights is not None:
        lw = dict(lw, **late_weights(proj))
    q, k, kt, vt = _mla_prep_fwd(proj, lw, rope)
    o_mla, lse = _mla_attn_fwd(q, k, vt)
    o_swa = _swa_fwd(proj, lw)
    ycat = _mix_fwd(proj, o_mla, o_swa, lw["conv"])
    if target is None:
        out = _mm_nn(ycat, lw["wo"], "outproj_fwd", residual=x)
    else:
        out = _outproj_loss(ycat, lw["wo"], x, target)
    return out, dict(x=x, proj=proj, h=h, q=q, k=k, kt=kt, vt=vt, o_mla=o_mla, lse=lse, o_swa=o_swa, ycat=ycat,
                     lw=lw)


def _layer_bwd(g, sv, lw, rope, on_big_grads=None):
    proj = sv["proj"]
    dycat, d_wo = _outproj_bwd(g, sv["ycat"], lw["wot"])
    dproj, do_mla, do_swa, d_conv = _mix_bwd(dycat, proj, sv["o_mla"], sv["o_swa"], lw["conv"])
    dproj, dkn_acc, dv_acc, d_sqn, d_sinks = _swa_bwd(proj, sv["o_swa"], do_swa, lw, dproj)
    dproj, d_skn = _swa_kv_bwd(proj, dkn_acc, dv_acc, lw, dproj)
    dq, dk, dv = _mla_attn_bwd(sv["q"], sv["k"], sv["kt"], sv["vt"], sv["o_mla"], do_mla, sv["lse"])
    dproj, d_wq, d_wk, d_wv, d_qa, d_kva, d_qn, d_kn = _mla_prep_bwd(proj, dq, dk, dv, lw, rope, dproj)
    grads = dict(
        w_out=d_wo, w_qb=d_wq[:, :, :MLA_QK],
        w_kvb=jnp.concatenate([d_wk[:, :, :MLA_NOPE],
                               jnp.transpose(d_wv.reshape(MLA_KV_LORA, HEADS, MLA_NOPE), (1, 0, 2))], axis=2))
    token = 0.0 if on_big_grads is None else on_big_grads("mixer", grads)
    d_wp = _mm_tn(sv["h"], dproj, "inproj_bwd_dw", WIRE_DTYPE, tn=NP // 2)
    grads["w_in"] = _unpack_dwin(d_wp)
    token = token if on_big_grads is None else token + on_big_grads("w_in", grads)
    dx, d_ng = _inproj_bwd_dx(dproj, lw["wpt"], sv["x"], g, lw["ng"] + token)
    grads.update(
        conv=d_conv[0:3], norm_g=d_ng[0], qa=d_qa[0], kva=d_kva[0], qn=d_qn[0, :MLA_QK], kn=d_kn[0, :MLA_QK],
        sqn=d_sqn[0, :SWA_HEAD_DIM], skn=d_skn[0, :SWA_HEAD_DIM], sinks=d_sinks[:, 0])
    return dx, grads


def _my_coords():
    return lax.axis_index("x"), lax.axis_index("y"), lax.axis_index("c")


def _peer(me, k):
    x, y, c = me
    return (1 - x if k & 4 else x, 1 - y if k & 2 else y, 1 - c if k & 1 else c)


def _lin(d):
    return 4 * d[0] + 2 * d[1] + d[2]


def _push_copies(ins, lands, send_sems, recv_sems, gather, incoming=False):
    me = _my_coords()
    my = _lin(me)
    copies = []
    for a in range(len(ins)):
        for k in range(1, N_DEV):
            peer = _peer(me, k)
            src = ins[a] if gather else ins[a].at[_lin(peer)]
            copies.append(pltpu.make_async_remote_copy(
                src_ref=src, dst_ref=lands[a].at[_lin(peer) if incoming else my],
                send_sem=send_sems.at[a * 7 + k - 1], recv_sem=recv_sems.at[a * 7 + k - 1],
                device_id=peer, device_id_type=pl.DeviceIdType.MESH))
    return copies


def _push_start(arrays, name, gather):
    n = len(arrays)
    land_shapes = [((N_DEV,) + a.shape) if gather else a.shape for a in arrays]

    def body(*refs):
        ins, lands = refs[:n], refs[n:2 * n]
        send_sems, recv_sems = refs[2 * n], refs[2 * n + 1]
        token = refs[-1]
        for cp in _push_copies(ins, lands, send_sems, recv_sems, gather):
            cp.start()
        token[...] = jnp.zeros_like(token)

    hbm = pl.BlockSpec(memory_space=pltpu.HBM)
    sem = pl.BlockSpec(memory_space=pltpu.SEMAPHORE)
    res = pl.pallas_call(
        body, name=name,
        out_shape=(pltpu.SemaphoreType.DMA((7 * n,)), pltpu.SemaphoreType.DMA((7 * n,)),
                   *[pltpu.HBM(a.shape, a.dtype) for a in arrays],
                   *[pltpu.HBM(s, a.dtype) for s, a in zip(land_shapes, arrays)],
                   _sds((8, LANES), F32)),
        in_specs=(hbm,) * (2 * n),
        out_specs=(sem, sem) + (hbm,) * (2 * n) + (pl.BlockSpec(memory_space=pltpu.VMEM),),
        input_output_aliases={i: 2 + i for i in range(2 * n)},
        compiler_params=pltpu.CompilerParams(has_side_effects=pltpu.SideEffectType.DATAFLOW_SIDE_EFFECTING),
    )(*[pltpu.with_memory_space_constraint(a, pltpu.HBM) for a in arrays],
      *[pltpu.with_memory_space_constraint(lax.empty(s, a.dtype), pltpu.HBM) for s, a in zip(land_shapes, arrays)])
    return dict(send=res[0], recv=res[1], src=res[2:2 + n], land=res[2 + n:2 + 2 * n], token=res[-1][0, 0],
                gather=gather)


def _push_wait(handle, after, name):
    n = len(handle["src"])
    gather = handle["gather"]

    def body(*refs):
        ins, lands = refs[:n], refs[n:2 * n]
        send_sems, recv_sems = refs[2 * n], refs[2 * n + 1]
        for cp in _push_copies(ins, lands, send_sems, recv_sems, gather):
            cp.wait_send()
        for cp in _push_copies(ins, lands, send_sems, recv_sems, gather, incoming=True):
            cp.wait_recv()

    hbm = pl.BlockSpec(memory_space=pltpu.HBM)
    sem = pl.BlockSpec(memory_space=pltpu.SEMAPHORE)
    res = pl.pallas_call(
        body, name=name,
        out_shape=tuple(pltpu.HBM(a.shape, a.dtype) for a in (*handle["src"], *handle["land"])),
        in_specs=(hbm,) * (2 * n) + (sem, sem, pl.BlockSpec(memory_space=pl.ANY)),
        out_specs=(hbm,) * (2 * n),
        input_output_aliases={i: i for i in range(2 * n)},
        compiler_params=pltpu.CompilerParams(has_side_effects=pltpu.SideEffectType.DATAFLOW_SIDE_EFFECTING),
    )(*handle["src"], *handle["land"], handle["send"], handle["recv"], after)
    return res[n:]


def _small_all_reduce(v):
    R = v.shape[0]

    def body(v_ref, o_ref, buf, send_sems, recv_sems):
        me = _my_coords()
        my = _lin(me)
        sends = []
        for k in range(1, N_DEV):
            cp = pltpu.make_async_remote_copy(
                src_ref=v_ref, dst_ref=buf.at[my], send_sem=send_sems.at[k - 1], recv_sem=recv_sems.at[k - 1],
                device_id=_peer(me, k), device_id_type=pl.DeviceIdType.MESH)
            cp.start()
            sends.append(cp)
        buf[my] = v_ref[...]
        for k in range(1, N_DEV):
            pltpu.make_async_remote_copy(
                src_ref=v_ref, dst_ref=buf.at[_lin(_peer(me, k))], send_sem=send_sems.at[k - 1],
                recv_sem=recv_sems.at[k - 1], device_id=_peer(me, k),
                device_id_type=pl.DeviceIdType.MESH).wait_recv()
        for cp in sends:
            cp.wait_send()
        tot = buf[0]
        for d in range(1, N_DEV):
            tot = tot + buf[d]
        o_ref[...] = tot

    vm = pl.BlockSpec(memory_space=pltpu.VMEM)
    return pl.pallas_call(
        body, name="small_all_reduce", in_specs=[vm], out_specs=vm, out_shape=_sds(v.shape, F32),
        scratch_shapes=[pltpu.VMEM((N_DEV, R, LANES), F32), pltpu.SemaphoreType.DMA((7,)),
                        pltpu.SemaphoreType.DMA((7,))],
    )(v)


def _adamw_math(w, g, m, v):
    m = ADAM_B1 * m + (1.0 - ADAM_B1) * g
    v = ADAM_B2 * v + (1.0 - ADAM_B2) * (g * g)
    m_hat = m / (1.0 - ADAM_B1 ** ADAM_STEP)
    v_hat = v / (1.0 - ADAM_B2 ** ADAM_STEP)
    delta = -ADAM_LR * (m_hat / (jnp.sqrt(v_hat) + ADAM_EPS) + ADAM_WD * w)
    return delta, m, v


def _adamw(parts, w, m, v, name, tr):
    P, R, C = parts.shape
    tr = min(tr, R)

    def body(p_ref, w_ref, m_ref, v_ref, g_out, d_out, m_out, v_out):
        g = p_ref[0].astype(F32)
        for d in range(1, P):
            g = g + p_ref[d].astype(F32)
        delta, m_new, v_new = _adamw_math(w_ref[...], g, m_ref[...], v_ref[...])
        g_out[...] = g
        d_out[...] = delta
        m_out[...] = m_new
        v_out[...] = v_new

    tile = pl.BlockSpec((tr, C), lambda i: (i, 0))
    return pl.pallas_call(
        body, name=name, grid=(R // tr,),
        in_specs=[pl.BlockSpec((P, tr, C), lambda i: (0, i, 0)), tile, tile, tile],
        out_specs=[tile] * 4, out_shape=[_sds((R, C), F32)] * 4,
        compiler_params=_cp(("parallel",), 32))(parts, w, m, v)


SMALL = (("norm_g", D_MODEL), ("mla_q_a_norm", MLA_Q_LORA), ("mla_kv_a_norm", MLA_KV_LORA), ("mla_q_norm", MLA_QK),
         ("mla_k_norm", MLA_QK), ("swa_q_norm", SWA_HEAD_DIM), ("swa_k_norm", SWA_HEAD_DIM), ("swa_sinks", HEADS))
SMALL_GRAD_KEY = dict(norm_g="norm_g", mla_q_a_norm="qa", mla_kv_a_norm="kva", mla_q_norm="qn", mla_k_norm="kn",
                      swa_q_norm="sqn", swa_k_norm="skn", swa_sinks="sinks")
SMALL_ROWS = 32
CONV_ROWS = 24


def _pack_small(get):
    parts = []
    for l in range(DEPTH):
        for name, n in SMALL:
            v = get(name, l).reshape(-1)
            parts.append(jnp.pad(v, (0, (-n) % LANES)))
    return jnp.concatenate(parts).reshape(SMALL_ROWS, LANES)


def _unpack_small(packed):
    flat = packed.reshape(-1)
    out = {name: [] for name, _ in SMALL}
    off = 0
    for l in range(DEPTH):
        for name, n in SMALL:
            out[name].append(flat[off:off + n])
            off += n + (-n) % LANES
    return {name: jnp.stack(v) for name, v in out.items()}


def kernel(x, norm_g, w_in, mla_q_a_norm, mla_w_qb, mla_kv_a_norm, mla_w_kvb, mla_q_norm, mla_k_norm, conv_w, swa_q_norm, swa_k_norm, swa_sinks, w_out, loss_target, m_norm_g, m_w_in, m_mla_q_a_norm, m_mla_w_qb, m_mla_kv_a_norm, m_mla_w_kvb, m_mla_q_norm, m_mla_k_norm, m_conv_w, m_swa_q_norm, m_swa_k_norm, m_swa_sinks, m_w_out, v_norm_g, v_w_in, v_mla_q_a_norm, v_mla_w_qb, v_mla_kv_a_norm, v_mla_w_kvb, v_mla_q_norm, v_mla_k_norm, v_conv_w, v_swa_q_norm, v_swa_k_norm, v_swa_sinks, v_w_out):
    T = x.shape[1]
    weights = dict(norm_g=norm_g, w_in=w_in, mla_q_a_norm=mla_q_a_norm, mla_w_qb=mla_w_qb,
                   mla_kv_a_norm=mla_kv_a_norm, mla_w_kvb=mla_w_kvb, mla_q_norm=mla_q_norm, mla_k_norm=mla_k_norm,
                   conv_w=conv_w, swa_q_norm=swa_q_norm, swa_k_norm=swa_k_norm, swa_sinks=swa_sinks, w_out=w_out)
    mom_m = dict(norm_g=m_norm_g, w_in=m_w_in, mla_q_a_norm=m_mla_q_a_norm, mla_w_qb=m_mla_w_qb,
                 mla_kv_a_norm=m_mla_kv_a_norm, mla_w_kvb=m_mla_w_kvb, mla_q_norm=m_mla_q_norm,
                 mla_k_norm=m_mla_k_norm, conv_w=m_conv_w, swa_q_norm=m_swa_q_norm, swa_k_norm=m_swa_k_norm,
                 swa_sinks=m_swa_sinks, w_out=m_w_out)
    mom_v = dict(norm_g=v_norm_g, w_in=v_w_in, mla_q_a_norm=v_mla_q_a_norm, mla_w_qb=v_mla_w_qb,
                 mla_kv_a_norm=v_mla_kv_a_norm, mla_w_kvb=v_mla_w_kvb, mla_q_norm=v_mla_q_norm,
                 mla_k_norm=v_mla_k_norm, conv_w=v_conv_w, swa_q_norm=v_swa_q_norm, swa_k_norm=v_swa_k_norm,
                 swa_sinks=v_swa_sinks, w_out=v_w_out)

    my = _lin(_my_coords())

    def shards(l):
        return [w_in[l].astype(MXU_DTYPE).T, mla_w_qb[l].astype(MXU_DTYPE), mla_w_kvb[l].astype(MXU_DTYPE),
                w_out[l].astype(MXU_DTYPE), conv_w[l]]

    def inproj_weights(l, g_win_t):
        return _inproj_weights(l, norm_g, g_win_t.reshape(IN_COLS, D_MODEL))

    def mixer_weights(l, gathered):
        g_wqb, g_wkvb, g_wout, g_conv = gathered
        return _mixer_weights(
            l, mla_q_a_norm, g_wqb, mla_kv_a_norm, g_wkvb, mla_q_norm, mla_k_norm,
            jnp.transpose(g_conv, (1, 0, 2)).reshape(3, GROUP_WIDTH), swa_q_norm, swa_k_norm, swa_sinks,
            g_wout.reshape(D_MIX, D_MODEL))

    slot_of = dict(
        w_in=lambda g: jnp.transpose(g["w_in"].reshape(D_MODEL, N_DEV, IN_COLS // N_DEV), (1, 0, 2)),
        w_out=lambda g: g["w_out"].reshape(N_DEV, D_MIX // N_DEV, D_MODEL),
        w_qb=lambda g: g["w_qb"], w_kvb=lambda g: g["w_kvb"])

    def own_slot(landed, mine):
        return [lax.dynamic_update_index_in_dim(a, m, my, 0) for a, m in zip(landed, mine)]

    def landed(handle, after, name, mine):
        return own_slot(_push_wait(handle, after, name), mine)

    sh = [shards(0), shards(1)]
    gather_in0 = _push_start(sh[0][:1], "weight_gather_in0_start", gather=True)
    rope = _rope_tables(T, gather_in0["token"])
    big_shapes = dict(w_in=(DEPTH * D_MODEL, IN_COLS // N_DEV), w_out=(DEPTH * D_MIX // N_DEV, D_MODEL),
                      mla_w_qb=(DEPTH * MLA_Q_LORA, MLA_QK), mla_w_kvb=(DEPTH * MLA_KV_LORA, 128))
    pad_conv = lambda a: jnp.pad(a.reshape(-1), (0, 8 * LANES - 6 * 64)).reshape(8, LANES)
    cat = lambda src: jnp.concatenate([_pack_small(lambda name, l: src[name][l]), pad_conv(src["conv_w"])], axis=0)
    adam_in = {name: [src[name].reshape(shape) for src in (weights, mom_m, mom_v)]
               for name, shape in big_shapes.items()}
    adam_in["small"] = [cat(weights), cat(mom_m), cat(mom_v)]
    rope0, adam_in, casts = lax.optimization_barrier((rope[0], adam_in, [sh[0][1:4], sh[1][:4]]))
    sh = [sh[0][:1] + casts[0] + [conv_w[0]], casts[1] + [conv_w[1]]]
    w_in0_t = landed(gather_in0, rope0, "weight_gather_in0_wait", sh[0][:1])[0]
    w_in0_t, conv0 = lax.optimization_barrier((w_in0_t, conv_w[0]))
    gather0 = _push_start(sh[0][1:4] + [conv0], "weight_gather0_start", gather=True)
    lw0 = inproj_weights(0, w_in0_t)
    lw0 = dict(lw0, ng=lw0["ng"] + gather0["token"])
    layer1 = {}

    def mixer0(proj):
        got = landed(gather0, proj, "weight_gather0_wait", sh[0][1:])
        got[0], conv1 = lax.optimization_barrier((got[0], conv_w[1]))
        layer1["gather"] = _push_start(sh[1][:4] + [conv1], "weight_gather1_start", gather=True)
        mw = mixer_weights(0, got)
        return dict(mw, qa=mw["qa"] + layer1["gather"]["token"])

    x1, sv0 = _layer_fwd(x[0], lw0, rope, late_weights=mixer0)
    g1_all = landed(layer1["gather"], x1, "weight_gather1_wait", sh[1])
    (g2, loss_tile), sv1 = _layer_fwd(x1, dict(inproj_weights(1, g1_all[0]), **mixer_weights(1, g1_all[1:])), rope,
                                      target=loss_target[0])

    parts = {(1, "w_in"): ("w_in", "w_out", "w_qb", "w_kvb"), (0, "mixer"): ("w_out", "w_qb", "w_kvb"),
             (0, "w_in"): ("w_in",)}
    started = []

    def start_exchange(l, part, g):
        if (l, part) not in parts:
            return 0.0
        sl = [slot_of[n](g) for n in parts[(l, part)]]
        handle = _push_start(sl, "grad_exchange%d_%s_start" % (l, part), gather=False)
        started.append((l, part, sl, handle))
        return handle["token"]

    g1, grads1 = _layer_bwd(g2, sv1, sv1["lw"], rope, on_big_grads=functools.partial(start_exchange, 1))
    lw0b = dict(sv0["lw"], conv=sv0["lw"]["conv"] + started[0][3]["token"])
    grad_x, grads0 = _layer_bwd(g1, sv0, lw0b, rope, on_big_grads=functools.partial(start_exchange, 0))
    recv = {}

    def receive(l, part, sl, handle, after):
        got = landed(handle, after, "grad_exchange%d_%s_wait" % (l, part), [s[my] for s in sl])
        recv.update({(l, n): a for n, a in zip(parts[(l, part)], got)})

    for entry in started[:-1]:
        receive(*entry, after=grad_x)
    grads = [grads0, grads1]
    stacked = lambda n: jnp.stack([recv[(0, n)], recv[(1, n)]], axis=1)

    small = jnp.concatenate([
        _pack_small(lambda name, l: grads[l][SMALL_GRAD_KEY[name]]),
        jnp.stack([g["conv"] for g in grads]).reshape(CONV_ROWS, LANES),
        loss_tile], axis=0)
    small = _small_all_reduce(small)
    loss = small[SMALL_ROWS + CONV_ROWS, 0]
    my = _lin(_my_coords())
    conv_g = lax.dynamic_slice_in_dim(small[SMALL_ROWS:SMALL_ROWS + CONV_ROWS].reshape(DEPTH, 3, GROUP_WIDTH),
                                      my * 64, 64, axis=2)

    out = {}

    def big(name, recv, tr):
        res = _adamw(recv.reshape((N_DEV,) + big_shapes[name]), *adam_in[name], "adamw_" + name, tr)
        out[name] = [r.reshape(weights[name].shape) for r in res]

    big("w_out", stacked("w_out"), 192)
    big("mla_w_qb", stacked("w_qb"), 512)
    big("mla_w_kvb", stacked("w_kvb"), 256)
    receive(*started[-1], after=out["w_out"][1])
    big("w_in", stacked("w_in"), 256)

    g_small = jnp.concatenate([small[:SMALL_ROWS], pad_conv(conv_g)], axis=0)
    res = _adamw(g_small[None], *adam_in["small"], "adamw_small", SMALL_ROWS + 8)
    smalls = [_unpack_small(r[:SMALL_ROWS]) for r in res]
    for name, _ in SMALL:
        out[name] = [s[name] for s in smalls]
    out["conv_w"] = [r[SMALL_ROWS:].reshape(-1)[:6 * 64].reshape(DEPTH, 3, 64) for r in res]

    order = ["norm_g", "w_in", "mla_q_a_norm", "mla_w_qb", "mla_kv_a_norm", "mla_w_kvb", "mla_q_norm", "mla_k_norm",
             "conv_w", "swa_q_norm", "swa_k_norm", "swa_sinks", "w_out"]
    result = [loss, grad_x[None]]
    for idx in range(4):
        result += [out[name][idx] for name in order]
    return tuple(result)
```
